```python
import math
import jax, jax.numpy as jnp
from jax import lax
import numpy as np

D_MODEL = 1024
BATCH = 8
SEQ = 4096
DEPTH = 1

D_MIX = D_MODEL
D_SSM = D_MIX // 2
D_CONV = D_MIX - D_SSM
SSM_GROUP = 16
N_SSM_GROUPS = D_SSM // SSM_GROUP
SSM_STATE = 64
CONV_HEADS = 8
CONV_WIDTH = 3
D_FF = ((8 * D_MODEL // 3 + 127) // 128) * 128
FFN_CONV_WIDTH = 3
N_MOD = 6
D_IN_PROJ = D_SSM + 3 * D_CONV
EPS = 1e-6
DT_MIN = 1e-3
DT_MAX = 1e-1
LAMBDA_RE_MAX = -1e-4

kernel_name = 'hymba_s5_shortconv_convffn_adaln'


def rms_norm(x, g):
    xf = x.astype(jnp.float32)
    y = xf * lax.rsqrt(jnp.mean(xf * xf, axis=-1, keepdims=True) + EPS)
    return (y * g.astype(jnp.float32)).astype(x.dtype)


def head_rms_norm(y, g, n_heads):
    shp = y.shape
    yf = y.astype(jnp.float32).reshape(shp[:-1] + (n_heads, shp[-1] // n_heads))
    yf = yf * lax.rsqrt(jnp.mean(yf * yf, axis=-1, keepdims=True) + EPS)
    return (yf.reshape(shp) * g.astype(jnp.float32)).astype(y.dtype)


def causal_dwconv(x, w):
    k_w = w.shape[0]
    seq = x.shape[1]
    xp = jnp.pad(x, ((0, 0), (k_w - 1, 0), (0, 0)))
    y = xp[:, 0:seq, :] * w[0]
    for k in range(1, k_w):
        y = y + xp[:, k:k + seq, :] * w[k]
    return y


def _s5_binop(e1, e2):
    a1, b1 = e1
    a2, b2 = e2
    return a2 * a1, a2 * b1 + b2


def s5_mixer(u, lam_re, lam_im, log_step, b_re, b_im, c_re, c_im, d_skip, glu_w, glu_b):
    bsz, seq, _ = u.shape
    uf = u.astype(jnp.float32)
    ug = uf.reshape(bsz, seq, N_SSM_GROUPS, SSM_GROUP)
    lam = lax.complex(jnp.minimum(lam_re.astype(jnp.float32), LAMBDA_RE_MAX),
                      lam_im.astype(jnp.float32))
    step = jnp.exp(log_step.astype(jnp.float32))[:, None]
    lam_bar = jnp.exp(lam * step)
    b_c = lax.complex(b_re.astype(jnp.float32), b_im.astype(jnp.float32))
    b_bar = ((lam_bar - 1.0) / lam)[..., None] * b_c
    bu = jnp.einsum('blgh,gph->blgp', ug.astype(jnp.complex64), b_bar)
    a = jnp.broadcast_to(lam_bar, (1, seq) + lam_bar.shape)
    _, states = lax.associative_scan(_s5_binop, (a, bu), axis=1)
    c_c = lax.complex(c_re.astype(jnp.float32), c_im.astype(jnp.float32))
    y = jnp.einsum('blgp,ghp->blgh', states, c_c).real.reshape(bsz, seq, D_SSM)
    y = y + d_skip.astype(jnp.float32) * uf
    z = jax.nn.gelu(y)
    z = z * jax.nn.sigmoid(z @ glu_w.astype(jnp.float32) + glu_b.astype(jnp.float32))
    return z.astype(u.dtype)


def short_conv_mixer(bg, cg, v, conv_w):
    return bg * causal_dwconv(cg * v, conv_w)


def conv_ffn(h, w_up, ffn_conv_w, w_down):
    hid = causal_dwconv(h @ w_up, ffn_conv_w)
    a, v = jnp.split(hid, 2, axis=-1)
    return (jax.nn.silu(a) * v) @ w_down


def _fwd_setup_inputs(seed: int = 0) -> dict:
    key = jax.random.key(seed)
    ks = jax.random.split(key, 26)
    f32 = jnp.float32

    def nrm(k, shape, s):
        return jax.random.normal(k, shape, f32) * s

    nl = DEPTH
    g_, p_, h_ = N_SSM_GROUPS, SSM_STATE, SSM_GROUP
    n_idx = jnp.arange(SSM_STATE, dtype=f32)
    return {
        'x': nrm(ks[0], (BATCH, SEQ, D_MODEL), 1.0),
        'c': nrm(ks[1], (BATCH, D_MODEL), 1.0),
        'w_ada': nrm(ks[2], (nl, D_MODEL, N_MOD * D_MODEL), 0.5 * D_MODEL ** -0.5),
        'b_ada': nrm(ks[3], (nl, N_MOD * D_MODEL), 0.02),
        'g_pre_mix': 1.0 + nrm(ks[4], (nl, D_MODEL), 0.02),
        'g_post_mix': 1.0 + nrm(ks[5], (nl, D_MODEL), 0.02),
        'w_in': nrm(ks[6], (nl, D_MODEL, D_IN_PROJ), D_MODEL ** -0.5),
        'ssm_lam_re': -0.5 + nrm(ks[7], (nl, g_, p_), 0.01),
        'ssm_lam_im': math.pi * n_idx + nrm(ks[8], (nl, g_, p_), 0.01),
        'ssm_log_step': jax.random.uniform(ks[9], (nl, g_), f32, math.log(DT_MIN), math.log(DT_MAX)),
        'ssm_b_re': nrm(ks[10], (nl, g_, p_, h_), (2 * h_) ** -0.5),
        'ssm_b_im': nrm(ks[11], (nl, g_, p_, h_), (2 * h_) ** -0.5),
        'ssm_c_re': nrm(ks[12], (nl, g_, h_, p_), p_ ** -0.5),
        'ssm_c_im': nrm(ks[13], (nl, g_, h_, p_), p_ ** -0.5),
        'ssm_d': nrm(ks[14], (nl, D_SSM), 1.0),
        'glu_w': nrm(ks[15], (nl, D_SSM, D_SSM), D_SSM ** -0.5),
        'glu_b': nrm(ks[16], (nl, D_SSM), 0.02),
        'g_out_ssm': 1.0 + nrm(ks[17], (nl, D_SSM), 0.02),
        'conv_w': nrm(ks[18], (nl, CONV_WIDTH, D_CONV), CONV_WIDTH ** -0.5),
        'g_out_conv': 1.0 + nrm(ks[19], (nl, D_CONV), 0.02),
        'w_out': nrm(ks[20], (nl, D_MIX, D_MODEL), D_MIX ** -0.5),
        'g_pre_ffn': 1.0 + nrm(ks[21], (nl, D_MODEL), 0.02),
        'g_post_ffn': 1.0 + nrm(ks[22], (nl, D_MODEL), 0.02),
        'w_up': nrm(ks[23], (nl, D_MODEL, 2 * D_FF), D_MODEL ** -0.5),
        'ffn_conv_w': nrm(ks[24], (nl, FFN_CONV_WIDTH, 2 * D_FF), FFN_CONV_WIDTH ** -0.5),
        'w_down': nrm(ks[25], (nl, D_FF, D_MODEL), D_FF ** -0.5),
    }


def _fwd_reference(x, c, w_ada, b_ada, g_pre_mix, g_post_mix, w_in, ssm_lam_re, ssm_lam_im, ssm_log_step,
              ssm_b_re, ssm_b_im, ssm_c_re, ssm_c_im, ssm_d, glu_w, glu_b, g_out_ssm, conv_w, g_out_conv,
              w_out, g_pre_ffn, g_post_ffn, w_up, ffn_conv_w, w_down):
    c_act = jax.nn.silu(c)
    for i in range(DEPTH):
        mod = (c_act @ w_ada[i] + b_ada[i])[:, None, :]
        sh1, sc1, gt1, sh2, sc2, gt2 = jnp.split(mod, N_MOD, axis=-1)

        h = rms_norm(x, g_pre_mix[i]) * (1.0 + sc1) + sh1
        proj = h @ w_in[i]
        u = proj[..., :D_SSM]
        bg, cg, v = jnp.split(proj[..., D_SSM:], 3, axis=-1)
        y_a = s5_mixer(u, ssm_lam_re[i], ssm_lam_im[i], ssm_log_step[i], ssm_b_re[i], ssm_b_im[i],
                       ssm_c_re[i], ssm_c_im[i], ssm_d[i], glu_w[i], glu_b[i])
        y_b = short_conv_mixer(bg, cg, v, conv_w[i])
        y = jnp.concatenate([head_rms_norm(y_a, g_out_ssm[i], N_SSM_GROUPS),
                             head_rms_norm(y_b, g_out_conv[i], CONV_HEADS)], axis=-1)
        x = x + gt1 * rms_norm(y @ w_out[i], g_post_mix[i])

        h = rms_norm(x, g_pre_ffn[i]) * (1.0 + sc2) + sh2
        x = x + gt2 * rms_norm(conv_ffn(h, w_up[i], ffn_conv_w[i], w_down[i]), g_post_ffn[i])
    return x


import jax as _jax
import jax.numpy as _jnp

TWIN_FORMAT = 'train_step'
FWD_PARAMS = ['x', 'c', 'w_ada', 'b_ada', 'g_pre_mix', 'g_post_mix', 'w_in', 'ssm_lam_re', 'ssm_lam_im', 'ssm_log_step', 'ssm_b_re', 'ssm_b_im', 'ssm_c_re', 'ssm_c_im', 'ssm_d', 'glu_w', 'glu_b', 'g_out_ssm', 'conv_w', 'g_out_conv', 'w_out', 'g_pre_ffn', 'g_post_ffn', 'w_up', 'ffn_conv_w', 'w_down']
TWIN_WEIGHTS = ['w_ada', 'b_ada', 'g_pre_mix', 'g_post_mix', 'w_in', 'ssm_lam_re', 'ssm_lam_im', 'ssm_log_step', 'ssm_b_re', 'ssm_b_im', 'ssm_c_re', 'ssm_c_im', 'ssm_d', 'glu_w', 'glu_b', 'g_out_ssm', 'conv_w', 'g_out_conv', 'w_out', 'g_pre_ffn', 'g_post_ffn', 'w_up', 'ffn_conv_w', 'w_down']
TWIN_DIFF_INPUT = 'x'
TWIN_INPUTS = ['x', 'c', 'w_ada', 'b_ada', 'g_pre_mix', 'g_post_mix', 'w_in', 'ssm_lam_re', 'ssm_lam_im', 'ssm_log_step', 'ssm_b_re', 'ssm_b_im', 'ssm_c_re', 'ssm_c_im', 'ssm_d', 'glu_w', 'glu_b', 'g_out_ssm', 'conv_w', 'g_out_conv', 'w_out', 'g_pre_ffn', 'g_post_ffn', 'w_up', 'ffn_conv_w', 'w_down', 'loss_target', 'm_w_ada', 'm_b_ada', 'm_g_pre_mix', 'm_g_post_mix', 'm_w_in', 'm_ssm_lam_re', 'm_ssm_lam_im', 'm_ssm_log_step', 'm_ssm_b_re', 'm_ssm_b_im', 'm_ssm_c_re', 'm_ssm_c_im', 'm_ssm_d', 'm_glu_w', 'm_glu_b', 'm_g_out_ssm', 'm_conv_w', 'm_g_out_conv', 'm_w_out', 'm_g_pre_ffn', 'm_g_post_ffn', 'm_w_up', 'm_ffn_conv_w', 'm_w_down', 'v_w_ada', 'v_b_ada', 'v_g_pre_mix', 'v_g_post_mix', 'v_w_in', 'v_ssm_lam_re', 'v_ssm_lam_im', 'v_ssm_log_step', 'v_ssm_b_re', 'v_ssm_b_im', 'v_ssm_c_re', 'v_ssm_c_im', 'v_ssm_d', 'v_glu_w', 'v_glu_b', 'v_g_out_ssm', 'v_conv_w', 'v_g_out_conv', 'v_w_out', 'v_g_pre_ffn', 'v_g_post_ffn', 'v_w_up', 'v_ffn_conv_w', 'v_w_down']
TWIN_OUTPUTS = ['loss', 'grad_x', 'grad_w_ada', 'grad_b_ada', 'grad_g_pre_mix', 'grad_g_post_mix', 'grad_w_in', 'grad_ssm_lam_re', 'grad_ssm_lam_im', 'grad_ssm_log_step', 'grad_ssm_b_re', 'grad_ssm_b_im', 'grad_ssm_c_re', 'grad_ssm_c_im', 'grad_ssm_d', 'grad_glu_w', 'grad_glu_b', 'grad_g_out_ssm', 'grad_conv_w', 'grad_g_out_conv', 'grad_w_out', 'grad_g_pre_ffn', 'grad_g_post_ffn', 'grad_w_up', 'grad_ffn_conv_w', 'grad_w_down', 'delta_w_ada', 'delta_b_ada', 'delta_g_pre_mix', 'delta_g_post_mix', 'delta_w_in', 'delta_ssm_lam_re', 'delta_ssm_lam_im', 'delta_ssm_log_step', 'delta_ssm_b_re', 'delta_ssm_b_im', 'delta_ssm_c_re', 'delta_ssm_c_im', 'delta_ssm_d', 'delta_glu_w', 'delta_glu_b', 'delta_g_out_ssm', 'delta_conv_w', 'delta_g_out_conv', 'delta_w_out', 'delta_g_pre_ffn', 'delta_g_post_ffn', 'delta_w_up', 'delta_ffn_conv_w', 'delta_w_down', 'new_m_w_ada', 'new_m_b_ada', 'new_m_g_pre_mix', 'new_m_g_post_mix', 'new_m_w_in', 'new_m_ssm_lam_re', 'new_m_ssm_lam_im', 'new_m_ssm_log_step', 'new_m_ssm_b_re', 'new_m_ssm_b_im', 'new_m_ssm_c_re', 'new_m_ssm_c_im', 'new_m_ssm_d', 'new_m_glu_w', 'new_m_glu_b', 'new_m_g_out_ssm', 'new_m_conv_w', 'new_m_g_out_conv', 'new_m_w_out', 'new_m_g_pre_ffn', 'new_m_g_post_ffn', 'new_m_w_up', 'new_m_ffn_conv_w', 'new_m_w_down', 'new_v_w_ada', 'new_v_b_ada', 'new_v_g_pre_mix', 'new_v_g_post_mix', 'new_v_w_in', 'new_v_ssm_lam_re', 'new_v_ssm_lam_im', 'new_v_ssm_log_step', 'new_v_ssm_b_re', 'new_v_ssm_b_im', 'new_v_ssm_c_re', 'new_v_ssm_c_im', 'new_v_ssm_d', 'new_v_glu_w', 'new_v_glu_b', 'new_v_g_out_ssm', 'new_v_conv_w', 'new_v_g_out_conv', 'new_v_w_out', 'new_v_g_pre_ffn', 'new_v_g_post_ffn', 'new_v_w_up', 'new_v_ffn_conv_w', 'new_v_w_down']
TWIN_LEAF_KINDS = {'loss': 'loss', 'grad_x': 'grad_x', 'grad_w_ada': 'grad_w', 'grad_b_ada': 'grad_w', 'grad_g_pre_mix': 'grad_w', 'grad_g_post_mix': 'grad_w', 'grad_w_in': 'grad_w', 'grad_ssm_lam_re': 'grad_w', 'grad_ssm_lam_im': 'grad_w', 'grad_ssm_log_step': 'grad_w', 'grad_ssm_b_re': 'grad_w', 'grad_ssm_b_im': 'grad_w', 'grad_ssm_c_re': 'grad_w', 'grad_ssm_c_im': 'grad_w', 'grad_ssm_d': 'grad_w', 'grad_glu_w': 'grad_w', 'grad_glu_b': 'grad_w', 'grad_g_out_ssm': 'grad_w', 'grad_conv_w': 'grad_w', 'grad_g_out_conv': 'grad_w', 'grad_w_out': 'grad_w', 'grad_g_pre_ffn': 'grad_w', 'grad_g_post_ffn': 'grad_w', 'grad_w_up': 'grad_w', 'grad_ffn_conv_w': 'grad_w', 'grad_w_down': 'grad_w', 'delta_w_ada': 'delta_w', 'delta_b_ada': 'delta_w', 'delta_g_pre_mix': 'delta_w', 'delta_g_post_mix': 'delta_w', 'delta_w_in': 'delta_w', 'delta_ssm_lam_re': 'delta_w', 'delta_ssm_lam_im': 'delta_w', 'delta_ssm_log_step': 'delta_w', 'delta_ssm_b_re': 'delta_w', 'delta_ssm_b_im': 'delta_w', 'delta_ssm_c_re': 'delta_w', 'delta_ssm_c_im': 'delta_w', 'delta_ssm_d': 'delta_w', 'delta_glu_w': 'delta_w', 'delta_glu_b': 'delta_w', 'delta_g_out_ssm': 'delta_w', 'delta_conv_w': 'delta_w', 'delta_g_out_conv': 'delta_w', 'delta_w_out': 'delta_w', 'delta_g_pre_ffn': 'delta_w', 'delta_g_post_ffn': 'delta_w', 'delta_w_up': 'delta_w', 'delta_ffn_conv_w': 'delta_w', 'delta_w_down': 'delta_w', 'new_m_w_ada': 'new_m', 'new_m_b_ada': 'new_m', 'new_m_g_pre_mix': 'new_m', 'new_m_g_post_mix': 'new_m', 'new_m_w_in': 'new_m', 'new_m_ssm_lam_re': 'new_m', 'new_m_ssm_lam_im': 'new_m', 'new_m_ssm_log_step': 'new_m', 'new_m_ssm_b_re': 'new_m', 'new_m_ssm_b_im': 'new_m', 'new_m_ssm_c_re': 'new_m', 'new_m_ssm_c_im': 'new_m', 'new_m_ssm_d': 'new_m', 'new_m_glu_w': 'new_m', 'new_m_glu_b': 'new_m', 'new_m_g_out_ssm': 'new_m', 'new_m_conv_w': 'new_m', 'new_m_g_out_conv': 'new_m', 'new_m_w_out': 'new_m', 'new_m_g_pre_ffn': 'new_m', 'new_m_g_post_ffn': 'new_m', 'new_m_w_up': 'new_m', 'new_m_ffn_conv_w': 'new_m', 'new_m_w_down': 'new_m', 'new_v_w_ada': 'new_v', 'new_v_b_ada': 'new_v', 'new_v_g_pre_mix': 'new_v', 'new_v_g_post_mix': 'new_v', 'new_v_w_in': 'new_v', 'new_v_ssm_lam_re': 'new_v', 'new_v_ssm_lam_im': 'new_v', 'new_v_ssm_log_step': 'new_v', 'new_v_ssm_b_re': 'new_v', 'new_v_ssm_b_im': 'new_v', 'new_v_ssm_c_re': 'new_v', 'new_v_ssm_c_im': 'new_v', 'new_v_ssm_d': 'new_v', 'new_v_glu_w': 'new_v', 'new_v_glu_b': 'new_v', 'new_v_g_out_ssm': 'new_v', 'new_v_conv_w': 'new_v', 'new_v_g_out_conv': 'new_v', 'new_v_w_out': 'new_v', 'new_v_g_pre_ffn': 'new_v', 'new_v_g_post_ffn': 'new_v', 'new_v_w_up': 'new_v', 'new_v_ffn_conv_w': 'new_v', 'new_v_w_down': 'new_v'}


def _forward(args):
    return _fwd_reference(*[args[k] for k in FWD_PARAMS])


def _output_shape():
    out = _jax.eval_shape(lambda: _forward(_fwd_setup_inputs(0)))
    return out.shape, out.dtype

N_MICROBATCH = 1
ADAM_LR = 0.001
ADAM_B1 = 0.9
ADAM_B2 = 0.999
ADAM_EPS = 1e-08
ADAM_WD = 0.01
ADAM_STEP = 10
PER_EXAMPLE_BATCH_AXIS = {'x': 0, 'c': 0, 'loss_target': 0}
SHARED_INPUTS = []
_WEIGHT_DTYPES = {'w_ada': _jnp.float32, 'b_ada': _jnp.float32, 'g_pre_mix': _jnp.float32, 'g_post_mix': _jnp.float32, 'w_in': _jnp.float32, 'ssm_lam_re': _jnp.float32, 'ssm_lam_im': _jnp.float32, 'ssm_log_step': _jnp.float32, 'ssm_b_re': _jnp.float32, 'ssm_b_im': _jnp.float32, 'ssm_c_re': _jnp.float32, 'ssm_c_im': _jnp.float32, 'ssm_d': _jnp.float32, 'glu_w': _jnp.float32, 'glu_b': _jnp.float32, 'g_out_ssm': _jnp.float32, 'conv_w': _jnp.float32, 'g_out_conv': _jnp.float32, 'w_out': _jnp.float32, 'g_pre_ffn': _jnp.float32, 'g_post_ffn': _jnp.float32, 'w_up': _jnp.float32, 'ffn_conv_w': _jnp.float32, 'w_down': _jnp.float32}
MOMENT_SCALE = {'w_ada': 1.744233e+00, 'b_ada': 3.348643e+00, 'g_pre_mix': 1.374504e-01, 'g_post_mix': 3.794543e+00, 'w_in': 1.162076e-01, 'ssm_lam_re': 1.792419e-02, 'ssm_lam_im': 5.359919e-02, 'ssm_log_step': 1.340711e+01, 'ssm_b_re': 1.776972e-02, 'ssm_b_im': 1.935799e-02, 'ssm_c_re': 2.509263e-02, 'ssm_c_im': 2.421238e-02, 'ssm_d': 3.024270e-01, 'glu_w': 3.558996e-02, 'glu_b': 7.843168e-02, 'g_out_ssm': 2.927810e-01, 'conv_w': 1.038153e-01, 'g_out_conv': 1.261992e-01, 'w_out': 2.130608e-01, 'g_pre_ffn': 1.201274e-01, 'g_post_ffn': 3.800720e+00, 'w_up': 5.877939e-02, 'ffn_conv_w': 6.611685e-02, 'w_down': 1.113059e-01}


def _to_microbatches(a, axis):
    t = _jnp.moveaxis(a, axis, 0)
    t = t.reshape((N_MICROBATCH, t.shape[0] // N_MICROBATCH) + t.shape[1:])
    return _jnp.moveaxis(t, 1, axis + 1)


def setup_inputs(seed: int = 0) -> dict:
    inp = _fwd_setup_inputs(seed)
    key = _jax.random.fold_in(_jax.random.key(seed), 7919)
    shape, _ = _output_shape()
    out = dict(inp)
    out["loss_target"] = _jax.random.normal(_jax.random.fold_in(key, 0), shape, _jnp.float32)
    for i, name in enumerate(TWIN_WEIGHTS):
        w = inp[name].astype(_jnp.float32)
        if MOMENT_SCALE is None:
            s = _jnp.sqrt(_jnp.mean(_jnp.square(w)) + 1e-30)
        else:
            s = MOMENT_SCALE[name]
        km, kv = _jax.random.split(_jax.random.fold_in(key, i + 1))
        out[name] = w
        out["m_" + name] = s * _jax.random.normal(km, w.shape, _jnp.float32)
        out["v_" + name] = (s * s) * _jax.random.uniform(kv, w.shape, _jnp.float32, 0.5, 1.5)
    if N_MICROBATCH > 1:
        for name, axis in PER_EXAMPLE_BATCH_AXIS.items():
            out[name] = _to_microbatches(out[name], axis)
    return {'x': out['x'], 'c': out['c'], 'w_ada': out['w_ada'], 'b_ada': out['b_ada'], 'g_pre_mix': out['g_pre_mix'], 'g_post_mix': out['g_post_mix'], 'w_in': out['w_in'], 'ssm_lam_re': out['ssm_lam_re'], 'ssm_lam_im': out['ssm_lam_im'], 'ssm_log_step': out['ssm_log_step'], 'ssm_b_re': out['ssm_b_re'], 'ssm_b_im': out['ssm_b_im'], 'ssm_c_re': out['ssm_c_re'], 'ssm_c_im': out['ssm_c_im'], 'ssm_d': out['ssm_d'], 'glu_w': out['glu_w'], 'glu_b': out['glu_b'], 'g_out_ssm': out['g_out_ssm'], 'conv_w': out['conv_w'], 'g_out_conv': out['g_out_conv'], 'w_out': out['w_out'], 'g_pre_ffn': out['g_pre_ffn'], 'g_post_ffn': out['g_post_ffn'], 'w_up': out['w_up'], 'ffn_conv_w': out['ffn_conv_w'], 'w_down': out['w_down'], 'loss_target': out['loss_target'], 'm_w_ada': out['m_w_ada'], 'm_b_ada': out['m_b_ada'], 'm_g_pre_mix': out['m_g_pre_mix'], 'm_g_post_mix': out['m_g_post_mix'], 'm_w_in': out['m_w_in'], 'm_ssm_lam_re': out['m_ssm_lam_re'], 'm_ssm_lam_im': out['m_ssm_lam_im'], 'm_ssm_log_step': out['m_ssm_log_step'], 'm_ssm_b_re': out['m_ssm_b_re'], 'm_ssm_b_im': out['m_ssm_b_im'], 'm_ssm_c_re': out['m_ssm_c_re'], 'm_ssm_c_im': out['m_ssm_c_im'], 'm_ssm_d': out['m_ssm_d'], 'm_glu_w': out['m_glu_w'], 'm_glu_b': out['m_glu_b'], 'm_g_out_ssm': out['m_g_out_ssm'], 'm_conv_w': out['m_conv_w'], 'm_g_out_conv': out['m_g_out_conv'], 'm_w_out': out['m_w_out'], 'm_g_pre_ffn': out['m_g_pre_ffn'], 'm_g_post_ffn': out['m_g_post_ffn'], 'm_w_up': out['m_w_up'], 'm_ffn_conv_w': out['m_ffn_conv_w'], 'm_w_down': out['m_w_down'], 'v_w_ada': out['v_w_ada'], 'v_b_ada': out['v_b_ada'], 'v_g_pre_mix': out['v_g_pre_mix'], 'v_g_post_mix': out['v_g_post_mix'], 'v_w_in': out['v_w_in'], 'v_ssm_lam_re': out['v_ssm_lam_re'], 'v_ssm_lam_im': out['v_ssm_lam_im'], 'v_ssm_log_step': out['v_ssm_log_step'], 'v_ssm_b_re': out['v_ssm_b_re'], 'v_ssm_b_im': out['v_ssm_b_im'], 'v_ssm_c_re': out['v_ssm_c_re'], 'v_ssm_c_im': out['v_ssm_c_im'], 'v_ssm_d': out['v_ssm_d'], 'v_glu_w': out['v_glu_w'], 'v_glu_b': out['v_glu_b'], 'v_g_out_ssm': out['v_g_out_ssm'], 'v_conv_w': out['v_conv_w'], 'v_g_out_conv': out['v_g_out_conv'], 'v_w_out': out['v_w_out'], 'v_g_pre_ffn': out['v_g_pre_ffn'], 'v_g_post_ffn': out['v_g_post_ffn'], 'v_w_up': out['v_w_up'], 'v_ffn_conv_w': out['v_ffn_conv_w'], 'v_w_down': out['v_w_down']}


def _loss(weights, diff, rest, loss_target):
    with _jax.named_scope("forward"):
        args = {**rest, TWIN_DIFF_INPUT: diff, **{k: w.astype(_WEIGHT_DTYPES[k]) for k, w in weights.items()}}
        y = _forward(args)
    with _jax.named_scope("loss_head"):
        err = _jnp.square(y.astype(_jnp.float32) - loss_target)
        return 0.5 * _jnp.sum(_jnp.mean(err, axis=-1)) if err.ndim else 0.5 * err


def _adamw(w, g, m, v):
    m = ADAM_B1 * m + (1.0 - ADAM_B1) * g
    v = ADAM_B2 * v + (1.0 - ADAM_B2) * _jnp.square(g)
    m_hat = m / (1.0 - ADAM_B1 ** ADAM_STEP)
    v_hat = v / (1.0 - ADAM_B2 ** ADAM_STEP)
    delta = -ADAM_LR * (m_hat / (_jnp.sqrt(v_hat) + ADAM_EPS) + ADAM_WD * w)
    return delta, m, v


def reference(x, c, w_ada, b_ada, g_pre_mix, g_post_mix, w_in, ssm_lam_re, ssm_lam_im, ssm_log_step, ssm_b_re, ssm_b_im, ssm_c_re, ssm_c_im, ssm_d, glu_w, glu_b, g_out_ssm, conv_w, g_out_conv, w_out, g_pre_ffn, g_post_ffn, w_up, ffn_conv_w, w_down, loss_target, m_w_ada, m_b_ada, m_g_pre_mix, m_g_post_mix, m_w_in, m_ssm_lam_re, m_ssm_lam_im, m_ssm_log_step, m_ssm_b_re, m_ssm_b_im, m_ssm_c_re, m_ssm_c_im, m_ssm_d, m_glu_w, m_glu_b, m_g_out_ssm, m_conv_w, m_g_out_conv, m_w_out, m_g_pre_ffn, m_g_post_ffn, m_w_up, m_ffn_conv_w, m_w_down, v_w_ada, v_b_ada, v_g_pre_mix, v_g_post_mix, v_w_in, v_ssm_lam_re, v_ssm_lam_im, v_ssm_log_step, v_ssm_b_re, v_ssm_b_im, v_ssm_c_re, v_ssm_c_im, v_ssm_d, v_glu_w, v_glu_b, v_g_out_ssm, v_conv_w, v_g_out_conv, v_w_out, v_g_pre_ffn, v_g_post_ffn, v_w_up, v_ffn_conv_w, v_w_down):
    given = dict(x=x, c=c, w_ada=w_ada, b_ada=b_ada, g_pre_mix=g_pre_mix, g_post_mix=g_post_mix, w_in=w_in, ssm_lam_re=ssm_lam_re, ssm_lam_im=ssm_lam_im, ssm_log_step=ssm_log_step, ssm_b_re=ssm_b_re, ssm_b_im=ssm_b_im, ssm_c_re=ssm_c_re, ssm_c_im=ssm_c_im, ssm_d=ssm_d, glu_w=glu_w, glu_b=glu_b, g_out_ssm=g_out_ssm, conv_w=conv_w, g_out_conv=g_out_conv, w_out=w_out, g_pre_ffn=g_pre_ffn, g_post_ffn=g_post_ffn, w_up=w_up, ffn_conv_w=ffn_conv_w, w_down=w_down, loss_target=loss_target, m_w_ada=m_w_ada, m_b_ada=m_b_ada, m_g_pre_mix=m_g_pre_mix, m_g_post_mix=m_g_post_mix, m_w_in=m_w_in, m_ssm_lam_re=m_ssm_lam_re, m_ssm_lam_im=m_ssm_lam_im, m_ssm_log_step=m_ssm_log_step, m_ssm_b_re=m_ssm_b_re, m_ssm_b_im=m_ssm_b_im, m_ssm_c_re=m_ssm_c_re, m_ssm_c_im=m_ssm_c_im, m_ssm_d=m_ssm_d, m_glu_w=m_glu_w, m_glu_b=m_glu_b, m_g_out_ssm=m_g_out_ssm, m_conv_w=m_conv_w, m_g_out_conv=m_g_out_conv, m_w_out=m_w_out, m_g_pre_ffn=m_g_pre_ffn, m_g_post_ffn=m_g_post_ffn, m_w_up=m_w_up, m_ffn_conv_w=m_ffn_conv_w, m_w_down=m_w_down, v_w_ada=v_w_ada, v_b_ada=v_b_ada, v_g_pre_mix=v_g_pre_mix, v_g_post_mix=v_g_post_mix, v_w_in=v_w_in, v_ssm_lam_re=v_ssm_lam_re, v_ssm_lam_im=v_ssm_lam_im, v_ssm_log_step=v_ssm_log_step, v_ssm_b_re=v_ssm_b_re, v_ssm_b_im=v_ssm_b_im, v_ssm_c_re=v_ssm_c_re, v_ssm_c_im=v_ssm_c_im, v_ssm_d=v_ssm_d, v_glu_w=v_glu_w, v_glu_b=v_glu_b, v_g_out_ssm=v_g_out_ssm, v_conv_w=v_conv_w, v_g_out_conv=v_g_out_conv, v_w_out=v_w_out, v_g_pre_ffn=v_g_pre_ffn, v_g_post_ffn=v_g_post_ffn, v_w_up=v_w_up, v_ffn_conv_w=v_ffn_conv_w, v_w_down=v_w_down)
    weights = {n: given[n] for n in TWIN_WEIGHTS}
    shared = {n: given[n] for n in SHARED_INPUTS}
    per_example = {n: given[n] for n in ['x', 'c']}
    grad_fn = _jax.value_and_grad(_loss, argnums=(0, 1))

    def one_microbatch(ex, loss_target):
        ex = dict(ex)
        diff = ex.pop(TWIN_DIFF_INPUT)
        return grad_fn(weights, diff, {**shared, **ex}, loss_target)

    if N_MICROBATCH == 1:
        loss, (grad_w, grad_x) = one_microbatch(per_example, given["loss_target"])
    else:
        def body(carry, xs):
            loss_sum, grad_sum = carry
            l_k, (gw_k, gx_k) = one_microbatch(xs[0], xs[1])
            with _jax.named_scope("update"):
                return (loss_sum + l_k, _jax.tree.map(_jnp.add, grad_sum, gw_k)), gx_k

        init = (_jnp.zeros((), _jnp.float32), _jax.tree.map(_jnp.zeros_like, weights))
        (loss, grad_w), grad_x = _jax.lax.scan(body, init, (per_example, given["loss_target"]))
    with _jax.named_scope("update"):
        delta_w, new_m, new_v = {}, {}, {}
        for n in TWIN_WEIGHTS:
            delta_w[n], new_m[n], new_v[n] = _adamw(weights[n], grad_w[n], given["m_" + n], given["v_" + n])
    return (loss, grad_x, *[grad_w[n] for n in TWIN_WEIGHTS], *[delta_w[n] for n in TWIN_WEIGHTS],
            *[new_m[n] for n in TWIN_WEIGHTS], *[new_v[n] for n in TWIN_WEIGHTS])
```

```python
import math

import jax
import jax.numpy as jnp
from jax import lax
from jax.experimental import pallas as pl
from jax.experimental.pallas import tpu as pltpu

F32, BF16 = jnp.float32, jnp.bfloat16

D_MODEL = 1024
D_SSM = 512
D_CONV = 512
SSM_GROUP = 16
N_GROUPS = 32
SSM_STATE = 64
N_STATE = N_GROUPS * SSM_STATE
CONV_HEADS = 8
D_FF = 2816
N_MOD = 6
D_IN_PROJ = D_SSM + 3 * D_CONV
N_DEV = 8
FF_SHARD = 2 * D_FF // N_DEV
IN_SHARD = D_IN_PROJ // N_DEV
ADA_SHARD = N_MOD * D_MODEL // N_DEV
EPS = 1e-6
LAMBDA_RE_MAX = -1e-4
ADAM_LR, ADAM_B1, ADAM_B2, ADAM_EPS, ADAM_WD, ADAM_STEP = 0.001, 0.9, 0.999, 1e-08, 0.01, 10
GELU_C = math.sqrt(2.0 / math.pi)
GELU_A = 0.044715

SUBLANES = 8
HALO = 8
STATE_BLOCK = 256
CHAN_BLOCK = 128
VMEM_BIG = 48 << 20

WEIGHTS = ['w_ada', 'b_ada', 'g_pre_mix', 'g_post_mix', 'w_in', 'ssm_lam_re', 'ssm_lam_im', 'ssm_log_step',
           'ssm_b_re', 'ssm_b_im', 'ssm_c_re', 'ssm_c_im', 'ssm_d', 'glu_w', 'glu_b', 'g_out_ssm', 'conv_w',
           'g_out_conv', 'w_out', 'g_pre_ffn', 'g_post_ffn', 'w_up', 'ffn_conv_w', 'w_down']
SHARDED = ('w_ada', 'w_in', 'glu_w', 'conv_w', 'w_out', 'w_up', 'ffn_conv_w', 'w_down')
SMALL = tuple(n for n in WEIGHTS if n not in SHARDED)
PACK_COLS = 1024


def _call(body, *, name, grid, in_specs, out_specs, out_shape, scratch=(), sem=None, vmem=None):
    params = {}
    if sem is not None:
        params['dimension_semantics'] = sem
    if vmem is not None:
        params['vmem_limit_bytes'] = vmem
    return pl.pallas_call(body, name=name, grid=grid, in_specs=in_specs, out_specs=out_specs, out_shape=out_shape,
                          scratch_shapes=list(scratch), compiler_params=pltpu.CompilerParams(**params))


def _const(shape):
    nd = len(shape)
    return pl.BlockSpec(shape, lambda *_: (0,) * nd)


def _sds(shape, dtype=F32):
    return jax.ShapeDtypeStruct(shape, dtype)


def _dot(a, b):
    return jnp.dot(a, b, preferred_element_type=F32)


def _dot_nt(a, b):
    return lax.dot_general(a, b, (((1,), (1,)), ((), ())), preferred_element_type=F32)


def _dot_tn(a, b):
    return lax.dot_general(a, b, (((0,), (0,)), ((), ())), preferred_element_type=F32)


def _dot_split(x, mat, parts):
    acc = None
    rem = x
    for _ in range(parts):
        piece = rem.astype(BF16)
        rem = rem - piece.astype(F32)
        term = _dot(piece, mat)
        acc = term if acc is None else acc + term
    return acc


def _sigmoid(x):
    return 1.0 / (1.0 + jnp.exp(-x))


def _gelu(x):
    t = jnp.tanh(GELU_C * (x + GELU_A * x * x * x))
    return 0.5 * x * (1.0 + t), t


def _gelu_grad(x, t):
    return 0.5 * (1.0 + t) + 0.5 * x * (1.0 - t * t) * GELU_C * (1.0 + 3.0 * GELU_A * x * x)


def _rsqrt_mean(x):
    return lax.rsqrt(jnp.mean(x * x, axis=-1, keepdims=True) + EPS)


def _colsum(x):
    return jnp.sum(x, axis=0, keepdims=True)


def _shift_down(x, k, halo):
    r = pltpu.roll(x, k, 0)
    row = lax.broadcasted_iota(jnp.int32, x.shape, 0)
    for q in range(k):
        r = jnp.where(row == q, halo[HALO - k + q:HALO - k + q + 1, :], r)
    return r


def _shift_up(x, k, halo):
    n = x.shape[0]
    r = pltpu.roll(x, n - k, 0)
    row = lax.broadcasted_iota(jnp.int32, x.shape, 0)
    for q in range(k):
        r = jnp.where(row == n - k + q, halo[q:q + 1, :], r)
    return r


def _conv3(x, halo, w_ref):
    x1 = _shift_down(x, 1, halo)
    x2 = _shift_down(x, 2, halo)
    return w_ref[0:1, :] * x2 + w_ref[1:2, :] * x1 + w_ref[2:3, :] * x, x1, x2


def _conv3_t(g, halo, w_ref):
    return w_ref[2:3, :] * g + w_ref[1:2, :] * _shift_up(g, 1, halo) + w_ref[0:1, :] * _shift_up(g, 2, halo)


def _silu_parts(x):
    s = _sigmoid(x)
    return x * s, s * (1.0 + x * (1.0 - s))


def _norm_bwd(dn, x, r, g):
    gd = g * dn
    return r * gd - x * (r * r * r) * jnp.mean(gd * x, axis=-1, keepdims=True)


def _head_norm_bwd(dn, y, rs, g, avg):
    gd = g * dn
    return rs * gd - y * (rs * rs * rs) * _dot_split(gd * y, avg, 2)


def _me():
    x, y, c = lax.axis_index('x'), lax.axis_index('y'), lax.axis_index('c')
    return x, y, c, 4 * x + 2 * y + c


def _peer(k):
    x, y, c, _ = _me()
    px = 1 - x if k & 4 else x
    py = 1 - y if k & 2 else y
    pc = 1 - c if k & 1 else c
    return (px, py, pc), 4 * px + 2 * py + pc


def _exchange(arrs, *, name, scatter):
    n = len(arrs)

    def body(*refs):
        ins, outs = refs[:n], refs[n:2 * n]
        send_sems, recv_sems, local_sems = refs[2 * n:]
        me = _me()[3]
        local = []
        for a in range(n):
            src = ins[a].at[me] if scatter else ins[a]
            cp = pltpu.make_async_copy(src, outs[a].at[me], local_sems.at[a])
            cp.start()
            local.append(cp)
        sends, recvs = [], []
        for a in range(n):
            for k in range(1, N_DEV):
                dev, idx = _peer(k)
                src = ins[a].at[idx] if scatter else ins[a]
                cp = pltpu.make_async_remote_copy(src_ref=src, dst_ref=outs[a].at[me], send_sem=send_sems.at[a, k - 1],
                                                  recv_sem=recv_sems.at[a, k - 1], device_id=dev,
                                                  device_id_type=pl.DeviceIdType.MESH)
                cp.start()
                sends.append(cp)
                recvs.append(pltpu.make_async_remote_copy(src_ref=src, dst_ref=outs[a].at[idx],
                                                          send_sem=send_sems.at[a, k - 1], recv_sem=recv_sems.at[a, k - 1],
                                                          device_id=dev, device_id_type=pl.DeviceIdType.MESH))
        for cp in recvs:
            cp.wait_recv()
        for cp in sends:
            cp.wait_send()
        for cp in local:
            cp.wait()

    out_shape = [_sds(a.shape if scatter else (N_DEV,) + a.shape, a.dtype) for a in arrs]
    any_spec = pl.BlockSpec(memory_space=pl.ANY)
    outs = pl.pallas_call(
        body, name=name, out_shape=out_shape, in_specs=[any_spec] * n, out_specs=[any_spec] * n,
        scratch_shapes=[pltpu.SemaphoreType.DMA((n, N_DEV - 1)), pltpu.SemaphoreType.DMA((n, N_DEV - 1)),
                        pltpu.SemaphoreType.DMA((n,))],
    )(*arrs)
    return list(outs)


def _mod_cols(c_all, w_ada, b_cols):
    def body(c_ref, w_ref, b_ref, mod_ref, act_ref):
        c = c_ref[...]
        act = c * _sigmoid(c)
        act_ref[...] = act
        mod_ref[...] = _dot(act.astype(BF16), w_ref[...].astype(BF16)) + b_ref[...]

    return _call(body, name='mod_cols', grid=(1,),
                 in_specs=[_const(c_all.shape), _const(w_ada.shape), _const(b_cols.shape)],
                 out_specs=[_const((N_DEV, ADA_SHARD)), _const(c_all.shape)],
                 out_shape=[_sds((N_DEV, ADA_SHARD)), _sds(c_all.shape)], vmem=VMEM_BIG)(c_all, w_ada, b_cols)


def _grad_w_ada(act_t, dmod_cols):
    def body(a_ref, d_ref, o_ref):
        o_ref[...] = _dot(a_ref[...], d_ref[...])

    return _call(body, name='grad_w_ada', grid=(1,), in_specs=[_const(act_t.shape), _const(dmod_cols.shape)],
                 out_specs=_const((D_MODEL, ADA_SHARD)), out_shape=_sds((D_MODEL, ADA_SHARD)),
                 vmem=VMEM_BIG)(act_t, dmod_cols)


def _pre_mix(x, sc, sh, g, w_s, tm):
    T = x.shape[0]

    def body(x_ref, sc_ref, sh_ref, g_ref, w_ref, proj_ref, h_ref):
        @pl.when(pl.program_id(1) == 0)
        def _():
            xv = x_ref[...]
            h_ref[...] = ((xv * _rsqrt_mean(xv) * g_ref[...]) * (1.0 + sc_ref[...]) + sh_ref[...]).astype(BF16)

        proj_ref[...] = _dot(h_ref[...], w_ref[...])

    row = pl.BlockSpec((tm, D_MODEL), lambda i, j: (i, 0))
    vec = _const((1, D_MODEL))
    return _call(body, name='pre_mix', grid=(T // tm, N_DEV),
                 in_specs=[row, vec, vec, vec, pl.BlockSpec((None, D_MODEL, IN_SHARD), lambda i, j: (j, 0, 0))],
                 out_specs=[pl.BlockSpec((tm, IN_SHARD), lambda i, j: (i, j)), row],
                 out_shape=[_sds((T, D_IN_PROJ)), _sds((T, D_MODEL), BF16)],
                 sem=('parallel', 'arbitrary'))(x, sc, sh, g, w_s)


def _halo_before(tm):
    return lambda i: jnp.maximum(i * (tm // HALO) - 1, 0)


def _halo_after(tm, T):
    return lambda i: jnp.minimum((i + 1) * (tm // HALO), T // HALO - 1)


def _mix_fwd(yssm, proj, d, glu_w, glu_b, g_ssm, cw, g_conv, avg16, avg64, tm):
    T = yssm.shape[0]
    hb = _halo_before(tm)

    def body(y_ref, p_ref, ph_ref, d_ref, gw_ref, gb_ref, gs_ref, cw_ref, gc_ref, a16_ref, a64_ref, o_ref):
        i = pl.program_id(0)
        u = p_ref[:, 0:D_SSM]
        y = y_ref[...] + d_ref[...] * u
        z, _ = _gelu(y)
        gate = _sigmoid(_dot(z.astype(BF16), gw_ref[...]) + gb_ref[...])
        ya = z * gate
        rs = lax.rsqrt(_dot_split(ya * ya, a16_ref[...], 2) + EPS)
        o_ref[:, 0:D_SSM] = (ya * rs * gs_ref[...]).astype(BF16)
        bg = p_ref[:, D_SSM:D_SSM + D_CONV]
        cv = p_ref[:, D_SSM + D_CONV:D_SSM + 2 * D_CONV] * p_ref[:, D_SSM + 2 * D_CONV:D_IN_PROJ]
        hv = ph_ref[:, D_SSM + D_CONV:D_SSM + 2 * D_CONV] * ph_ref[:, D_SSM + 2 * D_CONV:D_IN_PROJ]
        hv = jnp.where(i > 0, hv, 0.0)
        conv, _, _ = _conv3(cv, hv, cw_ref)
        yb = bg * conv
        rsb = lax.rsqrt(_dot_split(yb * yb, a64_ref[...], 2) + EPS)
        o_ref[:, D_SSM:D_MODEL] = (yb * rsb * gc_ref[...]).astype(BF16)

    vec = _const((1, D_SSM))
    sq = _const((D_SSM, D_SSM))
    return _call(body, name='mix_fwd', grid=(T // tm,),
                 in_specs=[pl.BlockSpec((tm, D_SSM), lambda i: (i, 0)), pl.BlockSpec((tm, D_IN_PROJ), lambda i: (i, 0)),
                           pl.BlockSpec((HALO, D_IN_PROJ), lambda i: (hb(i), 0)), vec, sq, vec, vec,
                           _const((3, D_CONV)), vec, sq, sq],
                 out_specs=pl.BlockSpec((tm, D_MODEL), lambda i: (i, 0)), out_shape=_sds((T, D_MODEL), BF16),
                 sem=('parallel',), vmem=VMEM_BIG)(yssm, proj, proj, d, glu_w, glu_b, g_ssm, cw, g_conv, avg16, avg64)


def _out_proj(ycat, w_out, x, gt, g_post, g_pre, sc, sh, tm):
    T = x.shape[0]

    def body(y_ref, w_ref, x_ref, gt_ref, gp_ref, g2_ref, sc_ref, sh_ref, o_ref, x1_ref, h_ref):
        o = _dot(y_ref[...], w_ref[...])
        o_ref[...] = o
        x1 = x_ref[...] + gt_ref[...] * (o * _rsqrt_mean(o) * gp_ref[...])
        x1_ref[...] = x1
        h_ref[...] = ((x1 * _rsqrt_mean(x1) * g2_ref[...]) * (1.0 + sc_ref[...]) + sh_ref[...]).astype(BF16)

    row = pl.BlockSpec((tm, D_MODEL), lambda i: (i, 0))
    vec = _const((1, D_MODEL))
    return _call(body, name='out_proj', grid=(T // tm,),
                 in_specs=[row, _const((D_MODEL, D_MODEL)), row, vec, vec, vec, vec, vec],
                 out_specs=[row, row, row],
                 out_shape=[_sds((T, D_MODEL)), _sds((T, D_MODEL)), _sds((T, D_MODEL), BF16)],
                 sem=('parallel',), vmem=VMEM_BIG)(ycat, w_out, x, gt, g_post, g_pre, sc, sh)


def _ffn_up(h2, w_s, tm):
    T = h2.shape[0]

    def body(h_ref, w_ref, o_ref):
        o_ref[...] = _dot(h_ref[...], w_ref[...])

    return _call(body, name='ffn_up', grid=(T // tm, N_DEV),
                 in_specs=[pl.BlockSpec((tm, D_MODEL), lambda i, j: (i, 0)),
                           pl.BlockSpec((None, D_MODEL, FF_SHARD), lambda i, j: (j, 0, 0))],
                 out_specs=pl.BlockSpec((None, tm, FF_SHARD), lambda i, j: (j, i, 0)),
                 out_shape=_sds((N_DEV, T, FF_SHARD)), sem=('parallel', 'parallel'))(h2, w_s)


def _ffn_hidden(up_ref, halo_ref, cw_ref, i):
    hid = []
    for part in range(2):
        halo = jnp.where(i > 0, halo_ref[part], 0.0)
        hid.append(_conv3(up_ref[part], halo, cw_ref.at[part])[0])
    return hid


def _ffn_act(up4, cw4, tm):
    T = up4.shape[2]
    hb = _halo_before(tm)

    def body(up_ref, halo_ref, cw_ref, o_ref):
        hid_a, hid_v = _ffn_hidden(up_ref, halo_ref, cw_ref, pl.program_id(0))
        o_ref[...] = (_silu_parts(hid_a)[0] * hid_v).astype(BF16)

    return _call(body, name='ffn_act', grid=(T // tm, 4),
                 in_specs=[pl.BlockSpec((2, None, tm, FF_SHARD), lambda i, j: (0, j, i, 0)),
                           pl.BlockSpec((2, None, HALO, FF_SHARD), lambda i, j: (0, j, hb(i), 0)),
                           pl.BlockSpec((2, None, 3, FF_SHARD), lambda i, j: (0, j, 0, 0))],
                 out_specs=pl.BlockSpec((None, tm, FF_SHARD), lambda i, j: (j, i, 0)),
                 out_shape=_sds((4, T, FF_SHARD), BF16), sem=('parallel', 'parallel'))(up4, up4, cw4)


def _ffn_down(act, wd4, x1, tgt, gt, g_post, tm):
    T = x1.shape[0]
    nb = T // tm

    def body(a_ref, w_ref, x1_ref, t_ref, gt_ref, g_ref, dn_ref, dx_ref, loss_ref):
        j = pl.program_id(1)
        part = _dot(a_ref[...], w_ref[...])

        @pl.when(j == 0)
        def _():
            dn_ref[...] = part

        @pl.when(j > 0)
        def _():
            dn_ref[...] += part

        @pl.when(j == 3)
        def _():
            dn = dn_ref[...]
            x2 = x1_ref[...] + gt_ref[...] * (dn * _rsqrt_mean(dn) * g_ref[...])
            err = x2 - t_ref[...]
            dx_ref[...] = err * (1.0 / D_MODEL)
            tot = jnp.sum(jnp.sum(err * err, axis=1, keepdims=True), axis=0, keepdims=True) * (0.5 / D_MODEL)
            loss_ref[...] = jnp.broadcast_to(tot, (8, 128))

    row = pl.BlockSpec((tm, D_MODEL), lambda i, j: (i, 0))
    vec = _const((1, D_MODEL))
    return _call(body, name='ffn_down', grid=(nb, 4),
                 in_specs=[pl.BlockSpec((None, tm, FF_SHARD), lambda i, j: (j, i, 0)),
                           pl.BlockSpec((None, FF_SHARD, D_MODEL), lambda i, j: (j, 0, 0)), row, row, vec, vec],
                 out_specs=[row, row, pl.BlockSpec((None, 8, 128), lambda i, j: (i, 0, 0))],
                 out_shape=[_sds((T, D_MODEL)), _sds((T, D_MODEL)), _sds((nb, 8, 128))],
                 sem=('parallel', 'arbitrary'))(act, wd4, x1, tgt, gt, g_post)


def _ssm_prep(lre, lim, lst, b_re, b_im):
    def body(lre_ref, lim_ref, lst_ref, br_ref, bi_ref, ar_ref, ai_ref, bbr_ref, bbi_ref):
        ar, ai, qr, qi = _zoh(lre_ref[...], lim_ref[...], lst_ref[...])[:4]
        ar_ref[...] = ar
        ai_ref[...] = ai
        bbr_ref[...] = qr * br_ref[...] - qi * bi_ref[...]
        bbi_ref[...] = qr * bi_ref[...] + qi * br_ref[...]

    shp = lre.shape
    return _call(body, name='ssm_prep', grid=(1,), in_specs=[_const(shp)] * 5, out_specs=[_const(shp)] * 4,
                 out_shape=[_sds(shp)] * 4)(lre, lim, lst, b_re, b_im)


def _zoh(lre, lim, lst):
    lr = jnp.minimum(lre, LAMBDA_RE_MAX)
    st = jnp.exp(lst)
    mag = jnp.exp(lr * st)
    ar = mag * jnp.cos(lim * st)
    ai = mag * jnp.sin(lim * st)
    den = lr * lr + lim * lim
    qr = ((ar - 1.0) * lr + ai * lim) / den
    qi = (ai * lr - (ar - 1.0) * lim) / den
    return ar, ai, qr, qi, lr, st, den


def _ssm_prep_bwd(lre, lim, lst, b_re, b_im, dbbr, dbbi, dar, dai, seg):
    def body(lre_ref, lim_ref, lst_ref, br_ref, bi_ref, dbbr_ref, dbbi_ref, dar_ref, dai_ref, seg_ref,
             dbr_ref, dbi_ref, dlre_ref, dlim_ref, dlst_ref):
        lre_v = lre_ref[...]
        li = lim_ref[...]
        ar, ai, qr, qi, lr, st, den = _zoh(lre_v, li, lst_ref[...])
        br, bi, gbr, gbi = br_ref[...], bi_ref[...], dbbr_ref[...], dbbi_ref[...]
        dbr_ref[...] = qr * gbr + qi * gbi
        dbi_ref[...] = qr * gbi - qi * gbr
        gqr = _dot_split(br * gbr + bi * gbi, seg_ref[...], 3)
        gqi = _dot_split(br * gbi - bi * gbr, seg_ref[...], 3)
        ir, ii = lr / den, -li / den
        gar = dar_ref[...] + ir * gqr + ii * gqi
        gai = dai_ref[...] + ir * gqi - ii * gqr
        tr, ti = qr * ir - qi * ii, qr * ii + qi * ir
        glr = -(tr * gqr + ti * gqi)
        gli = -(tr * gqi - ti * gqr)
        gzr = ar * gar + ai * gai
        gzi = ar * gai - ai * gar
        glr = glr + st * gzr
        gli = gli + st * gzi
        gst = (lr * gzr + li * gzi) * st
        dlre_ref[...] = jnp.where(lre_v < LAMBDA_RE_MAX, glr, 0.0)
        dlim_ref[...] = gli
        dlst_ref[...] = jnp.sum(gst, axis=1, keepdims=True) * (1.0 / SSM_GROUP)

    shp = lre.shape
    return _call(body, name='ssm_prep_bwd', grid=(1,), in_specs=[_const(shp)] * 9 + [_const(seg.shape)],
                 out_specs=[_const(shp)] * 4 + [_const((N_GROUPS, 1))],
                 out_shape=[_sds(shp)] * 4 + [_sds((N_GROUPS, 1))], vmem=VMEM_BIG)(
                     lre, lim, lst, b_re, b_im, dbbr, dbbi, dar, dai, seg)


def _scan_specs(T):
    half = lambda cb: cb // 2
    return dict(
        chan=pl.BlockSpec((T, CHAN_BLOCK), lambda cb: (0, half(cb))),
        state=pl.BlockSpec((T, STATE_BLOCK), lambda cb: (0, cb)),
        b=pl.BlockSpec((CHAN_BLOCK, STATE_BLOCK), lambda cb: (half(cb), cb)),
        c=pl.BlockSpec((STATE_BLOCK, CHAN_BLOCK), lambda cb: (cb, half(cb))),
        lam=pl.BlockSpec((1, STATE_BLOCK), lambda cb: (0, cb)),
    )


def _rows8(i):
    return pl.ds(pl.multiple_of(i * SUBLANES, SUBLANES), SUBLANES)


def _ssm_fwd(u_perm, b_re, b_im, c_re, c_im, lam_r, lam_i):
    T = u_perm.shape[0]
    ls = T // SUBLANES
    rc = min(512, T)
    sp = _scan_specs(T)

    def body(u_ref, bre_ref, bim_ref, cre_ref, cim_ref, lr_ref, li_ref, sre_ref, sim_ref, y_ref):
        cb = pl.program_id(0)
        for c in range(T // rc):
            rows = pl.ds(c * rc, rc)
            sre_ref[rows, :] = _dot(u_ref[rows, :], bre_ref[...])
            sim_ref[rows, :] = _dot(u_ref[rows, :], bim_ref[...])
        shp = (SUBLANES, STATE_BLOCK)
        lr = jnp.broadcast_to(lr_ref[...], shp)
        li = jnp.broadcast_to(li_ref[...], shp)
        zero = jnp.zeros(shp, F32)

        def step(i, carry):
            sr, si, wr, wi = carry
            rows = _rows8(i)
            nr = lr * sr - li * si + sre_ref[rows, :]
            ni = lr * si + li * sr + sim_ref[rows, :]
            sre_ref[rows, :] = nr
            sim_ref[rows, :] = ni
            return nr, ni, lr * wr - li * wi, lr * wi + li * wr

        fr, fi, pr, pi_ = lax.fori_loop(0, ls, step, (zero, zero, jnp.ones(shp, F32), zero))
        row = lax.broadcasted_iota(jnp.int32, shp, 0)
        ir, ii = zero, zero
        for _ in range(SUBLANES - 1):
            er = fr + pr * ir - pi_ * ii
            ei = fi + pr * ii + pi_ * ir
            ir = jnp.where(row == 0, 0.0, pltpu.roll(er, 1, 0))
            ii = jnp.where(row == 0, 0.0, pltpu.roll(ei, 1, 0))

        def fix(i, carry):
            wr, wi = carry
            rows = _rows8(i)
            sre_ref[rows, :] += wr * ir - wi * ii
            sim_ref[rows, :] += wr * ii + wi * ir
            return lr * wr - li * wi, lr * wi + li * wr

        lax.fori_loop(0, ls, fix, (lr, li))
        for c in range(T // rc):
            rows = pl.ds(c * rc, rc)
            yc = _dot(sre_ref[rows, :].astype(BF16), cre_ref[...]) - _dot(sim_ref[rows, :].astype(BF16), cim_ref[...])

            @pl.when(cb % 2 == 0)
            def _():
                y_ref[rows, :] = yc

            @pl.when(cb % 2 == 1)
            def _():
                y_ref[rows, :] += yc

    return _call(body, name='ssm_fwd', grid=(N_STATE // STATE_BLOCK,),
                 in_specs=[sp['chan'], sp['b'], sp['b'], sp['c'], sp['c'], sp['lam'], sp['lam']],
                 out_specs=[sp['state'], sp['state'], sp['chan']],
                 out_shape=[_sds((T, N_STATE)), _sds((T, N_STATE)), _sds((T, D_SSM))],
                 sem=('arbitrary',), vmem=VMEM_BIG)(u_perm, b_re, b_im, c_re, c_im, lam_r, lam_i)


def _ssm_bwd(dy_perm, u_perm, s_re, s_im, b_re, b_im, c_re, c_im, lam_r, lam_i):
    T = u_perm.shape[0]
    ls = T // SUBLANES
    rc = min(512, T)
    sp = _scan_specs(T)
    ncb = N_STATE // STATE_BLOCK

    def body(dy_ref, u_ref, sre_ref, sim_ref, bre_ref, bim_ref, cre_ref, cim_ref, lr_ref, li_ref,
             du_ref, dbr_ref, dbi_ref, dcr_ref, dci_ref, dar_ref, dai_ref, gre_ref, gim_ref):
        cb = pl.program_id(0)
        for c in range(T // rc):
            rows = pl.ds(c * rc, rc)
            gre_ref[rows, :] = _dot_nt(dy_ref[rows, :], cre_ref[...])
            gim_ref[rows, :] = -_dot_nt(dy_ref[rows, :], cim_ref[...])
        shp = (SUBLANES, STATE_BLOCK)
        lr = jnp.broadcast_to(lr_ref[...], shp)
        li = jnp.broadcast_to(li_ref[...], shp)
        zero = jnp.zeros(shp, F32)

        def step(k, carry):
            gr, gi, wr, wi = carry
            rows = _rows8(ls - 1 - k)
            nr = lr * gr + li * gi + gre_ref[rows, :]
            ni = lr * gi - li * gr + gim_ref[rows, :]
            gre_ref[rows, :] = nr
            gim_ref[rows, :] = ni
            return nr, ni, lr * wr + li * wi, lr * wi - li * wr

        fr, fi, pr, pi_ = lax.fori_loop(0, ls, step, (zero, zero, jnp.ones(shp, F32), zero))
        row = lax.broadcasted_iota(jnp.int32, shp, 0)
        cr, ci = zero, zero
        for _ in range(SUBLANES - 1):
            er = fr + pr * cr - pi_ * ci
            ei = fi + pr * ci + pi_ * cr
            cr = jnp.where(row == SUBLANES - 1, 0.0, pltpu.roll(er, SUBLANES - 1, 0))
            ci = jnp.where(row == SUBLANES - 1, 0.0, pltpu.roll(ei, SUBLANES - 1, 0))

        def fix(k, carry):
            wr, wi, ar, ai = carry
            rows = _rows8(ls - 1 - k)
            gr = gre_ref[rows, :] + wr * cr - wi * ci
            gi = gim_ref[rows, :] + wr * ci + wi * cr
            gre_ref[rows, :] = gr
            gim_ref[rows, :] = gi
            prev = _rows8(ls - 2 - k)
            spr, spi = sre_ref[prev, :], sim_ref[prev, :]
            return (lr * wr + li * wi, lr * wi - li * wr, ar + gr * spr + gi * spi, ai + gi * spr - gr * spi)

        wr, wi, ar, ai = lax.fori_loop(0, ls - 1, fix, (lr, -li, zero, zero))
        first = pl.ds(0, SUBLANES)
        last = pl.ds((ls - 1) * SUBLANES, SUBLANES)
        gr = gre_ref[first, :] + wr * cr - wi * ci
        gi = gim_ref[first, :] + wr * ci + wi * cr
        gre_ref[first, :] = gr
        gim_ref[first, :] = gi
        spr = jnp.where(row == 0, 0.0, pltpu.roll(sre_ref[last, :], 1, 0))
        spi = jnp.where(row == 0, 0.0, pltpu.roll(sim_ref[last, :], 1, 0))
        dar_ref[...] = _colsum(ar + gr * spr + gi * spi)
        dai_ref[...] = _colsum(ai + gi * spr - gr * spi)

        for c in range(T // rc):
            rows = pl.ds(c * rc, rc)
            g_r, g_i = gre_ref[rows, :].astype(BF16), gim_ref[rows, :].astype(BF16)
            s_r, s_i = sre_ref[rows, :].astype(BF16), sim_ref[rows, :].astype(BF16)
            ub, dyb = u_ref[rows, :], dy_ref[rows, :]
            duc = _dot_nt(g_r, bre_ref[...]) + _dot_nt(g_i, bim_ref[...])
            parts = (_dot_tn(ub, g_r), _dot_tn(ub, g_i), _dot_tn(s_r, dyb), -_dot_tn(s_i, dyb))
            outs = (dbr_ref, dbi_ref, dcr_ref, dci_ref)
            for o_ref, part in zip(outs, parts):
                if c == 0:
                    o_ref[...] = part
                else:
                    o_ref[...] += part

            @pl.when(cb % 2 == 0)
            def _():
                du_ref[rows, :] = duc

            @pl.when(cb % 2 == 1)
            def _():
                du_ref[rows, :] += duc

    blk = lambda r, c: pl.BlockSpec((None, r, c), lambda cb: (cb, 0, 0))
    return _call(body, name='ssm_bwd', grid=(ncb,),
                 in_specs=[sp['chan'], sp['chan'], sp['state'], sp['state'], sp['b'], sp['b'], sp['c'], sp['c'],
                           sp['lam'], sp['lam']],
                 out_specs=[sp['chan'], blk(CHAN_BLOCK, STATE_BLOCK), blk(CHAN_BLOCK, STATE_BLOCK),
                            blk(STATE_BLOCK, CHAN_BLOCK), blk(STATE_BLOCK, CHAN_BLOCK), blk(1, STATE_BLOCK),
                            blk(1, STATE_BLOCK)],
                 out_shape=[_sds((T, D_SSM)), _sds((ncb, CHAN_BLOCK, STATE_BLOCK)), _sds((ncb, CHAN_BLOCK, STATE_BLOCK)),
                            _sds((ncb, STATE_BLOCK, CHAN_BLOCK)), _sds((ncb, STATE_BLOCK, CHAN_BLOCK)),
                            _sds((ncb, 1, STATE_BLOCK)), _sds((ncb, 1, STATE_BLOCK))],
                 scratch=[pltpu.VMEM((T, STATE_BLOCK), F32), pltpu.VMEM((T, STATE_BLOCK), F32)],
                 sem=('arbitrary',), vmem=VMEM_BIG)(dy_perm, u_perm, s_re, s_im, b_re, b_im, c_re, c_im, lam_r, lam_i)


def _post_norm_bwd(dx, val, gate, g, tm, name):
    T = dx.shape[0]

    def body(dx_ref, v_ref, gt_ref, g_ref, dv_ref, dgt_ref, dg_ref):
        @pl.when(pl.program_id(0) == 0)
        def _():
            dgt_ref[...] = jnp.zeros_like(dgt_ref)
            dg_ref[...] = jnp.zeros_like(dg_ref)

        dxv, v, gv = dx_ref[...], v_ref[...], g_ref[...]
        r = _rsqrt_mean(v)
        dgt_ref[...] += _colsum(dxv * (v * r * gv))
        dn = dxv * gt_ref[...]
        dg_ref[...] += _colsum(dn * v * r)
        dv_ref[...] = _norm_bwd(dn, v, r, gv).astype(BF16)

    row = pl.BlockSpec((tm, D_MODEL), lambda i: (i, 0))
    vec = _const((1, D_MODEL))
    return _call(body, name=name, grid=(T // tm,), in_specs=[row, row, vec, vec], out_specs=[row, vec, vec],
                 out_shape=[_sds((T, D_MODEL), BF16), _sds((1, D_MODEL)), _sds((1, D_MODEL))],
                 sem=('arbitrary',))(dx, val, gate, g)


def _ffn_dact(ddn, wd4, up4, cw4, tm):
    T = ddn.shape[0]
    hb = _halo_before(tm)

    def body(d_ref, w_ref, up_ref, halo_ref, cw_ref, o_ref):
        dact = _dot_nt(d_ref[...], w_ref[...])
        hid_a, hid_v = _ffn_hidden(up_ref, halo_ref, cw_ref, pl.program_id(0))
        silu, dsilu = _silu_parts(hid_a)
        o_ref[0] = dact * hid_v * dsilu
        o_ref[1] = dact * silu

    return _call(body, name='ffn_dact', grid=(T // tm, 4),
                 in_specs=[pl.BlockSpec((tm, D_MODEL), lambda i, j: (i, 0)),
                           pl.BlockSpec((None, FF_SHARD, D_MODEL), lambda i, j: (j, 0, 0)),
                           pl.BlockSpec((2, None, tm, FF_SHARD), lambda i, j: (0, j, i, 0)),
                           pl.BlockSpec((2, None, HALO, FF_SHARD), lambda i, j: (0, j, hb(i), 0)),
                           pl.BlockSpec((2, None, 3, FF_SHARD), lambda i, j: (0, j, 0, 0))],
                 out_specs=pl.BlockSpec((2, None, tm, FF_SHARD), lambda i, j: (0, j, i, 0)),
                 out_shape=_sds((2, 4, T, FF_SHARD)), sem=('parallel', 'parallel'))(ddn, wd4, up4, up4, cw4)


def _ffn_dup(dhid8, up8, cw8, tm):
    T = up8.shape[1]
    nb = T // tm
    hb, ha = _halo_before(tm), _halo_after(tm, T)

    def body(dh_ref, dha_ref, up_ref, uph_ref, cw_ref, dup_ref, dcw_ref):
        i = pl.program_id(1)

        @pl.when(i == 0)
        def _():
            dcw_ref[...] = jnp.zeros_like(dcw_ref)

        dh = dh_ref[...]
        dup_ref[...] = _conv3_t(dh, jnp.where(i < nb - 1, dha_ref[...], 0.0), cw_ref).astype(BF16)
        up = up_ref[...]
        halo = jnp.where(i > 0, uph_ref[...], 0.0)
        dcw_ref[0:1, :] += _colsum(dh * _shift_down(up, 2, halo))
        dcw_ref[1:2, :] += _colsum(dh * _shift_down(up, 1, halo))
        dcw_ref[2:3, :] += _colsum(dh * up)

    main = pl.BlockSpec((None, tm, FF_SHARD), lambda j, i: (j, i, 0))
    return _call(body, name='ffn_dup', grid=(N_DEV, nb),
                 in_specs=[main, pl.BlockSpec((None, HALO, FF_SHARD), lambda j, i: (j, ha(i), 0)), main,
                           pl.BlockSpec((None, HALO, FF_SHARD), lambda j, i: (j, hb(i), 0)),
                           pl.BlockSpec((None, 3, FF_SHARD), lambda j, i: (j, 0, 0))],
                 out_specs=[main, pl.BlockSpec((None, 8, FF_SHARD), lambda j, i: (j, 0, 0))],
                 out_shape=[_sds((N_DEV, T, FF_SHARD), BF16), _sds((N_DEV, 8, FF_SHARD))],
                 sem=('parallel', 'arbitrary'))(dhid8, dhid8, up8, up8, cw8)


def _grad_tn(a, b, a_spec, b_spec, groups, m, n, tk, name):
    T = a.shape[-2]

    def body(a_ref, b_ref, o_ref):
        part = _dot_tn(a_ref[...], b_ref[...])

        @pl.when(pl.program_id(1) == 0)
        def _():
            o_ref[...] = part

        @pl.when(pl.program_id(1) > 0)
        def _():
            o_ref[...] += part

    return _call(body, name=name, grid=(groups, T // tk), in_specs=[a_spec, b_spec],
                 out_specs=pl.BlockSpec((None, m, n), lambda g, k: (g, 0, 0)), out_shape=_sds((groups, m, n)),
                 sem=('parallel', 'arbitrary'), vmem=VMEM_BIG)(a, b)


def _pre_norm_bwd(dz, dz_spec, w_s, xin, dres, sc, g, tm, name):
    T = xin.shape[0]
    n = w_s.shape[2]

    def body(dz_ref, w_ref, x_ref, dr_ref, sc_ref, g_ref, dx_ref, dsh_ref, dsc_ref, dg_ref):
        i, j = pl.program_id(0), pl.program_id(1)
        part = _dot_nt(dz_ref[...], w_ref[...])

        @pl.when(jnp.logical_and(i == 0, j == 0))
        def _():
            dsh_ref[...] = jnp.zeros_like(dsh_ref)
            dsc_ref[...] = jnp.zeros_like(dsc_ref)
            dg_ref[...] = jnp.zeros_like(dg_ref)

        @pl.when(j == 0)
        def _():
            dx_ref[...] = part

        @pl.when(j > 0)
        def _():
            dx_ref[...] += part

        @pl.when(j == N_DEV - 1)
        def _():
            dh, xv, gv = dx_ref[...], x_ref[...], g_ref[...]
            r = _rsqrt_mean(xv)
            dsh_ref[...] += _colsum(dh)
            dsc_ref[...] += _colsum(dh * (xv * r * gv))
            dxn = dh * (1.0 + sc_ref[...])
            dg_ref[...] += _colsum(dxn * xv * r)
            dx_ref[...] = dr_ref[...] + _norm_bwd(dxn, xv, r, gv)

    row = pl.BlockSpec((tm, D_MODEL), lambda i, j: (i, 0))
    vec = _const((1, D_MODEL))
    return _call(body, name=name, grid=(T // tm, N_DEV),
                 in_specs=[dz_spec, pl.BlockSpec((None, D_MODEL, n), lambda i, j: (j, 0, 0)), row, row, vec, vec],
                 out_specs=[row, vec, vec, vec],
                 out_shape=[_sds((T, D_MODEL)), _sds((1, D_MODEL)), _sds((1, D_MODEL)), _sds((1, D_MODEL))],
                 sem=('arbitrary', 'arbitrary'))(dz, w_s, xin, dres, sc, g)


def _d_ycat(d_o, w_out, tm):
    T = d_o.shape[0]

    def body(d_ref, w_ref, o_ref):
        o_ref[...] = _dot_nt(d_ref[...], w_ref[...])

    row = pl.BlockSpec((tm, D_MODEL), lambda i: (i, 0))
    return _call(body, name='d_ycat', grid=(T // tm,), in_specs=[row, _const((D_MODEL, D_MODEL))], out_specs=row,
                 out_shape=_sds((T, D_MODEL)), sem=('parallel',))(d_o, w_out)


def _mix_bwd(dycat, yssm, proj, d, glu_w, glu_b, g_ssm, cw, g_conv, avg16, avg64, tm):
    T = yssm.shape[0]
    hb = _halo_before(tm)

    def body(dyc_ref, y_ref, p_ref, ph_ref, d_ref, gw_ref, gb_ref, gs_ref, cw_ref, gc_ref, a16_ref, a64_ref,
             dy_ref, dconv_ref, dbg_ref, z_ref, dlin_ref, acc_ref):
        i = pl.program_id(0)

        @pl.when(i == 0)
        def _():
            acc_ref[...] = jnp.zeros_like(acc_ref)

        u = p_ref[:, 0:D_SSM]
        y = y_ref[...] + d_ref[...] * u
        z, t = _gelu(y)
        gate = _sigmoid(_dot(z.astype(BF16), gw_ref[...]) + gb_ref[...])
        ya = z * gate
        rs = lax.rsqrt(_dot_split(ya * ya, a16_ref[...], 2) + EPS)
        dna = dyc_ref[:, 0:D_SSM]
        acc_ref[1:2, :] += _colsum(dna * ya * rs)
        dya = _head_norm_bwd(dna, ya, rs, gs_ref[...], a16_ref[...])
        dlin = dya * z * gate * (1.0 - gate)
        acc_ref[0:1, :] += _colsum(dlin)
        dlin_b = dlin.astype(BF16)
        dz = dya * gate + _dot_nt(dlin_b, gw_ref[...])
        dy = dz * _gelu_grad(y, t)
        acc_ref[3:4, :] += _colsum(dy * u)
        dy_ref[...] = dy
        z_ref[...] = z.astype(BF16)
        dlin_ref[...] = dlin_b

        bg = p_ref[:, D_SSM:D_SSM + D_CONV]
        cv = p_ref[:, D_SSM + D_CONV:D_SSM + 2 * D_CONV] * p_ref[:, D_SSM + 2 * D_CONV:D_IN_PROJ]
        hv = ph_ref[:, D_SSM + D_CONV:D_SSM + 2 * D_CONV] * ph_ref[:, D_SSM + 2 * D_CONV:D_IN_PROJ]
        hv = jnp.where(i > 0, hv, 0.0)
        conv, cv1, cv2 = _conv3(cv, hv, cw_ref)
        yb = bg * conv
        rsb = lax.rsqrt(_dot_split(yb * yb, a64_ref[...], 2) + EPS)
        dnb = dyc_ref[:, D_SSM:D_MODEL]
        acc_ref[2:3, :] += _colsum(dnb * yb * rsb)
        dyb = _head_norm_bwd(dnb, yb, rsb, gc_ref[...], a64_ref[...])
        dbg_ref[...] = dyb * conv
        dconv = dyb * bg
        dconv_ref[...] = dconv
        acc_ref[4:5, :] += _colsum(dconv * cv2)
        acc_ref[5:6, :] += _colsum(dconv * cv1)
        acc_ref[6:7, :] += _colsum(dconv * cv)

    vec = _const((1, D_SSM))
    sq = _const((D_SSM, D_SSM))
    half = pl.BlockSpec((tm, D_SSM), lambda i: (i, 0))
    return _call(body, name='mix_bwd', grid=(T // tm,),
                 in_specs=[pl.BlockSpec((tm, D_MODEL), lambda i: (i, 0)), half,
                           pl.BlockSpec((tm, D_IN_PROJ), lambda i: (i, 0)),
                           pl.BlockSpec((HALO, D_IN_PROJ), lambda i: (hb(i), 0)), vec, sq, vec, vec,
                           _const((3, D_CONV)), vec, sq, sq],
                 out_specs=[half, half, half, half, half, _const((8, D_SSM))],
                 out_shape=[_sds((T, D_SSM)), _sds((T, D_SSM)), _sds((T, D_SSM)), _sds((T, D_SSM), BF16),
                            _sds((T, D_SSM), BF16), _sds((8, D_SSM))],
                 sem=('arbitrary',), vmem=VMEM_BIG)(dycat, yssm, proj, proj, d, glu_w, glu_b, g_ssm, cw, g_conv,
                                                   avg16, avg64)


def _mix_bwd_proj(dconv, proj, du_ssm, dy, d, dbg, cw, tm):
    T = dy.shape[0]
    nb = T // tm
    ha = _halo_after(tm, T)

    def body(dc_ref, dch_ref, cg_ref, v_ref, du_ref, dy_ref, d_ref, dbg_ref, cw_ref, o_ref):
        i = pl.program_id(0)
        dcv = _conv3_t(dc_ref[...], jnp.where(i < nb - 1, dch_ref[...], 0.0), cw_ref)
        o_ref[:, 0:D_SSM] = (du_ref[...] + dy_ref[...] * d_ref[...]).astype(BF16)
        o_ref[:, D_SSM:D_SSM + D_CONV] = dbg_ref[...].astype(BF16)
        o_ref[:, D_SSM + D_CONV:D_SSM + 2 * D_CONV] = (dcv * v_ref[...]).astype(BF16)
        o_ref[:, D_SSM + 2 * D_CONV:D_IN_PROJ] = (dcv * cg_ref[...]).astype(BF16)

    half = pl.BlockSpec((tm, D_SSM), lambda i: (i, 0))
    return _call(body, name='mix_bwd_proj', grid=(nb,),
                 in_specs=[half, pl.BlockSpec((HALO, D_CONV), lambda i: (ha(i), 0)),
                           pl.BlockSpec((tm, D_CONV), lambda i: (i, 2)), pl.BlockSpec((tm, D_CONV), lambda i: (i, 3)),
                           half, half, _const((1, D_SSM)), half, _const((3, D_CONV))],
                 out_specs=pl.BlockSpec((tm, D_IN_PROJ), lambda i: (i, 0)), out_shape=_sds((T, D_IN_PROJ), BF16),
                 sem=('parallel',))(dconv, dconv, proj, proj, du_ssm, dy, d, dbg, cw)


def _row_tile(rows, cols, slots):
    for cand in (512, 256, 128, 64, 32, 16, 8):
        if rows % cand == 0 and slots * cand * cols * 4 <= (2 << 20):
            return cand
    return rows


def _adamw(gslots, w, m, v, name):
    slots, rows, cols = gslots.shape
    tr = _row_tile(rows, cols, slots)

    def body(g_ref, w_ref, m_ref, v_ref, go_ref, d_ref, mo_ref, vo_ref):
        g = g_ref[0]
        for s in range(1, slots):
            g = g + g_ref[s]
        m2 = ADAM_B1 * m_ref[...] + (1.0 - ADAM_B1) * g
        v2 = ADAM_B2 * v_ref[...] + (1.0 - ADAM_B2) * (g * g)
        m_hat = m2 / (1.0 - ADAM_B1 ** ADAM_STEP)
        v_hat = v2 / (1.0 - ADAM_B2 ** ADAM_STEP)
        go_ref[...] = g
        d_ref[...] = -ADAM_LR * (m_hat / (jnp.sqrt(v_hat) + ADAM_EPS) + ADAM_WD * w_ref[...])
        mo_ref[...] = m2
        vo_ref[...] = v2

    blk = pl.BlockSpec((tr, cols), lambda i: (i, 0))
    return _call(body, name=name, grid=(rows // tr,),
                 in_specs=[pl.BlockSpec((slots, tr, cols), lambda i: (0, i, 0)), blk, blk, blk],
                 out_specs=[blk] * 4, out_shape=[_sds((rows, cols))] * 4, sem=('parallel',))(gslots, w, m, v)


def _to_scan_rows(a):
    T, n = a.shape
    return a.reshape(SUBLANES, T // SUBLANES, n).transpose(1, 0, 2).reshape(T, n)


def _from_scan_rows(a):
    T, n = a.shape
    return a.reshape(T // SUBLANES, SUBLANES, n).transpose(1, 0, 2).reshape(T, n)


def _expand(a):
    return jnp.repeat(a, SSM_GROUP, axis=1)


def _block_diag_b(bb):
    eye = jnp.eye(N_GROUPS, dtype=bb.dtype)
    return (bb.transpose(0, 2, 1)[:, :, None, :] * eye[:, None, :, None]).reshape(D_SSM, N_STATE)


def _block_diag_c(cc):
    eye = jnp.eye(N_GROUPS, dtype=cc.dtype)
    return (cc.transpose(0, 2, 1)[:, :, None, :] * eye[:, None, :, None]).reshape(N_STATE, D_SSM)


def _diag_blocks(x, chan_major):
    e2 = jnp.eye(2, dtype=x.dtype)
    e4 = jnp.eye(4, dtype=x.dtype)
    if chan_major:
        x = x.reshape(4, 2, 2, 4, SSM_GROUP, 4, SSM_STATE)
        x = x * e2[None, :, :, None, None, None, None] * e4[None, None, None, :, None, :, None]
        return x.sum(axis=(2, 3)).transpose(0, 1, 3, 4, 2).reshape(N_GROUPS, SSM_STATE, SSM_GROUP)
    x = x.reshape(4, 2, 4, SSM_STATE, 2, 4, SSM_GROUP)
    x = x * e2[None, :, None, None, :, None, None] * e4[None, None, :, None, None, :, None]
    return x.sum(axis=(4, 5)).reshape(N_GROUPS, SSM_STATE, SSM_GROUP)


def _pack(parts):
    flat = jnp.concatenate([p.reshape(-1) for p in parts])
    rows = -(-flat.shape[0] // (8 * PACK_COLS)) * 8
    return jnp.pad(flat, (0, rows * PACK_COLS - flat.shape[0])).reshape(rows, PACK_COLS)


def _unpack(packed, shapes):
    flat = packed.reshape(-1)
    out, pos = [], 0
    for shp in shapes:
        size = math.prod(shp)
        out.append(flat[pos:pos + size].reshape(shp))
        pos += size
    return out


def kernel(x, c, w_ada, b_ada, g_pre_mix, g_post_mix, w_in, ssm_lam_re, ssm_lam_im, ssm_log_step, ssm_b_re, ssm_b_im, ssm_c_re, ssm_c_im, ssm_d, glu_w, glu_b, g_out_ssm, conv_w, g_out_conv, w_out, g_pre_ffn, g_post_ffn, w_up, ffn_conv_w, w_down, loss_target, m_w_ada, m_b_ada, m_g_pre_mix, m_g_post_mix, m_w_in, m_ssm_lam_re, m_ssm_lam_im, m_ssm_log_step, m_ssm_b_re, m_ssm_b_im, m_ssm_c_re, m_ssm_c_im, m_ssm_d, m_glu_w, m_glu_b, m_g_out_ssm, m_conv_w, m_g_out_conv, m_w_out, m_g_pre_ffn, m_g_post_ffn, m_w_up, m_ffn_conv_w, m_w_down, v_w_ada, v_b_ada, v_g_pre_mix, v_g_post_mix, v_w_in, v_ssm_lam_re, v_ssm_lam_im, v_ssm_log_step, v_ssm_b_re, v_ssm_b_im, v_ssm_c_re, v_ssm_c_im, v_ssm_d, v_glu_w, v_glu_b, v_g_out_ssm, v_conv_w, v_g_out_conv, v_w_out, v_g_pre_ffn, v_g_post_ffn, v_w_up, v_ffn_conv_w, v_w_down):
    args = dict(locals())
    wts = {n: args[n] for n in WEIGHTS}
    mom_m = {n: args['m_' + n] for n in WEIGHTS}
    mom_v = {n: args['v_' + n] for n in WEIGHTS}
    T = x.shape[1]
    tm = min(512, T)
    me = _me()[3]
    xt, tgt = x[0], loss_target[0]

    (c_all,) = _exchange([c], name='gather_c', scatter=False)
    c_all = c_all.reshape(N_DEV, D_MODEL)
    b_cols = lax.dynamic_slice(b_ada, (0, me * ADA_SHARD), (1, ADA_SHARD))
    mod_cols, c_act = _mod_cols(c_all, w_ada[0], b_cols)
    (mod_all,) = _exchange([mod_cols], name='gather_mod', scatter=False)
    mod = lax.dynamic_slice(mod_all, (0, me, 0), (N_DEV, 1, ADA_SHARD)).reshape(N_MOD, 1, D_MODEL)
    sh1, sc1, gt1, sh2, sc2, gt2 = [mod[k] for k in range(N_MOD)]

    gathered = _exchange([w_in[0].astype(BF16), glu_w[0].astype(BF16), w_out[0].astype(BF16), w_up[0].astype(BF16),
                          w_down[0].astype(BF16), conv_w[0], ffn_conv_w[0]], name='gather_weights', scatter=False)
    w_in_s, glu_s, w_out_s, w_up_s, w_down_s, conv_s, ffn_conv_s = gathered
    glu_full = glu_s.reshape(D_SSM, D_SSM)
    w_out_full = w_out_s.reshape(D_MODEL, D_MODEL)
    wd4 = w_down_s.reshape(4, FF_SHARD, D_MODEL)
    cw_full = conv_s.transpose(1, 0, 2).reshape(3, D_CONV)
    cw4 = ffn_conv_s.reshape(2, 4, 3, FF_SHARD)

    lre_x, lim_x = _expand(ssm_lam_re[0]), _expand(ssm_lam_im[0])
    lst_x = jnp.broadcast_to(ssm_log_step[0][:, None], (N_GROUPS, SSM_STATE * SSM_GROUP))
    b_re_x = ssm_b_re[0].reshape(N_GROUPS, -1)
    b_im_x = ssm_b_im[0].reshape(N_GROUPS, -1)
    ar_x, ai_x, bbr_x, bbi_x = _ssm_prep(lre_x, lim_x, lst_x, b_re_x, b_im_x)
    lam_r = ar_x[:, ::SSM_GROUP].reshape(1, N_STATE)
    lam_i = ai_x[:, ::SSM_GROUP].reshape(1, N_STATE)
    big_b_re = _block_diag_b(bbr_x.reshape(N_GROUPS, SSM_STATE, SSM_GROUP)).astype(BF16)
    big_b_im = _block_diag_b(bbi_x.reshape(N_GROUPS, SSM_STATE, SSM_GROUP)).astype(BF16)
    big_c_re = _block_diag_c(ssm_c_re[0]).astype(BF16)
    big_c_im = _block_diag_c(ssm_c_im[0]).astype(BF16)
    head = jnp.arange(D_SSM)
    avg16 = jnp.where(head[:, None] // SSM_GROUP == head[None, :] // SSM_GROUP, 1.0 / SSM_GROUP, 0.0).astype(BF16)
    hd = D_CONV // CONV_HEADS
    avg64 = jnp.where(head[:, None] // hd == head[None, :] // hd, 1.0 / hd, 0.0).astype(BF16)

    proj, h1 = _pre_mix(xt, sc1, sh1, g_pre_mix, w_in_s, tm)
    u_perm = _to_scan_rows(proj[:, :D_SSM]).astype(BF16)
    s_re, s_im, y_perm = _ssm_fwd(u_perm, big_b_re, big_b_im, big_c_re, big_c_im, lam_r, lam_i)
    yssm = _from_scan_rows(y_perm)
    mix_args = (ssm_d, glu_full, glu_b, g_out_ssm, cw_full, g_out_conv, avg16, avg64)
    ycat = _mix_fwd(yssm, proj, *mix_args, tm)
    o, x1, h2 = _out_proj(ycat, w_out_full, xt, gt1, g_post_mix, g_pre_ffn, sc2, sh2, tm)
    up8 = _ffn_up(h2, w_up_s, tm)
    up4 = up8.reshape(2, 4, T, FF_SHARD)
    act = _ffn_act(up4, cw4, tm)
    dn, dx2, loss_parts = _ffn_down(act, wd4, x1, tgt, gt2, g_post_ffn, tm)
    loss = lax.psum(jnp.sum(loss_parts[:, 0, 0]), ('x', 'y', 'c'))

    ddn, d_gt2, d_g_post_ffn = _post_norm_bwd(dx2, dn, gt2, g_post_ffn, tm, 'ffn_norm_bwd')
    dhid = _ffn_dact(ddn, wd4, up4, cw4, tm)
    g_w_down = _grad_tn(act, ddn, pl.BlockSpec((None, tm, FF_SHARD), lambda g, k: (g, k, 0)),
                        pl.BlockSpec((tm, D_MODEL), lambda g, k: (k, 0)), 4, FF_SHARD, D_MODEL, tm, 'grad_w_down')
    dup8, dcw_ffn = _ffn_dup(dhid.reshape(N_DEV, T, FF_SHARD), up8, ffn_conv_s, tm)
    g_w_up = _grad_tn(h2, dup8, pl.BlockSpec((tm, D_MODEL), lambda g, k: (k, 0)),
                      pl.BlockSpec((None, tm, FF_SHARD), lambda g, k: (g, k, 0)), N_DEV, D_MODEL, FF_SHARD, tm,
                      'grad_w_up')
    dx1, d_sh2, d_sc2, d_g_pre_ffn = _pre_norm_bwd(
        dup8, pl.BlockSpec((None, tm, FF_SHARD), lambda i, j: (j, i, 0)), w_up_s, x1, dx2, sc2, g_pre_ffn, tm,
        'ffn_in_bwd')

    d_o, d_gt1, d_g_post_mix = _post_norm_bwd(dx1, o, gt1, g_post_mix, tm, 'mix_norm_bwd')
    g_w_out = _grad_tn(ycat, d_o, pl.BlockSpec((tm, D_MODEL), lambda g, k: (k, 0)),
                       pl.BlockSpec((tm, D_MODEL), lambda g, k: (k, 0)), 1, D_MODEL, D_MODEL, tm, 'grad_w_out')
    dycat = _d_ycat(d_o, w_out_full, tm)
    dy, dconv, dbg, z_b, dlin_b, sums = _mix_bwd(dycat, yssm, proj, *mix_args, tm)
    g_glu_w = _grad_tn(z_b, dlin_b, pl.BlockSpec((tm, D_SSM), lambda g, k: (k, 0)),
                       pl.BlockSpec((tm, D_SSM), lambda g, k: (k, 0)), 1, D_SSM, D_SSM, tm, 'grad_glu_w')
    dy_perm = _to_scan_rows(dy).astype(BF16)
    du_perm, dbr_blk, dbi_blk, dcr_blk, dci_blk, dar_blk, dai_blk = _ssm_bwd(
        dy_perm, u_perm, s_re, s_im, big_b_re, big_b_im, big_c_re, big_c_im, lam_r, lam_i)
    du_ssm = _from_scan_rows(du_perm)
    dproj = _mix_bwd_proj(dconv, proj, du_ssm, dy, ssm_d, dbg, cw_full, tm)
    g_w_in = _grad_tn(h1, dproj, pl.BlockSpec((tm, D_MODEL), lambda g, k: (k, 0)),
                      pl.BlockSpec((tm, IN_SHARD), lambda g, k: (k, g)), N_DEV, D_MODEL, IN_SHARD, tm, 'grad_w_in')
    grad_x, d_sh1, d_sc1, d_g_pre_mix = _pre_norm_bwd(
        dproj, pl.BlockSpec((tm, IN_SHARD), lambda i, j: (i, j)), w_in_s, xt, dx1, sc1, g_pre_mix, tm, 'mix_in_bwd')

    dbb_re = _diag_blocks(dbr_blk, True).reshape(N_GROUPS, -1)
    dbb_im = _diag_blocks(dbi_blk, True).reshape(N_GROUPS, -1)
    d_c_re = _diag_blocks(dcr_blk, False).transpose(0, 2, 1)
    d_c_im = _diag_blocks(dci_blk, False).transpose(0, 2, 1)
    lane = jnp.arange(SSM_STATE * SSM_GROUP)
    seg = jnp.where(lane[:, None] // SSM_GROUP == lane[None, :] // SSM_GROUP, 1.0, 0.0).astype(BF16)
    d_b_re_x, d_b_im_x, d_lre_x, d_lim_x, d_lst = _ssm_prep_bwd(
        lre_x, lim_x, lst_x, b_re_x, b_im_x, dbb_re, dbb_im, _expand(dar_blk.reshape(N_GROUPS, SSM_STATE)),
        _expand(dai_blk.reshape(N_GROUPS, SSM_STATE)), seg)

    dmod = jnp.concatenate([d_sh1, d_sc1, d_gt1, d_sh2, d_sc2, d_gt2], axis=1)
    small_grads = {
        'b_ada': dmod, 'g_pre_mix': d_g_pre_mix, 'g_post_mix': d_g_post_mix,
        'ssm_lam_re': d_lre_x[:, ::SSM_GROUP], 'ssm_lam_im': d_lim_x[:, ::SSM_GROUP], 'ssm_log_step': d_lst,
        'ssm_b_re': d_b_re_x, 'ssm_b_im': d_b_im_x, 'ssm_c_re': d_c_re, 'ssm_c_im': d_c_im,
        'ssm_d': sums[3:4], 'glu_b': sums[0:1], 'g_out_ssm': sums[1:2], 'g_out_conv': sums[2:3],
        'g_pre_ffn': d_g_pre_ffn, 'g_post_ffn': d_g_post_ffn,
    }
    (small_all,) = _exchange([_pack([small_grads[n] for n in SMALL])], name='gather_small_grads', scatter=False)
    small_out = _adamw(small_all, _pack([wts[n] for n in SMALL]), _pack([mom_m[n] for n in SMALL]),
                       _pack([mom_v[n] for n in SMALL]), 'adamw_small')
    small_shapes = [wts[n].shape for n in SMALL]
    res = {}
    for kind, packed in zip(('g', 'd', 'm', 'v'), small_out):
        for n, val in zip(SMALL, _unpack(packed, small_shapes)):
            res[kind, n] = val

    dmod_all = small_all[:, :N_MOD, :].reshape(N_DEV, N_MOD * D_MODEL)
    dmod_cols = lax.dynamic_slice(dmod_all, (0, me * ADA_SHARD), (N_DEV, ADA_SHARD))
    g_w_ada = _grad_w_ada(c_act.T, dmod_cols)

    big_local = {
        'w_in': g_w_in, 'glu_w': g_glu_w.reshape(N_DEV, D_SSM // N_DEV, D_SSM),
        'w_out': g_w_out.reshape(N_DEV, D_MODEL // N_DEV, D_MODEL), 'w_up': g_w_up,
        'w_down': g_w_down.reshape(N_DEV, D_FF // N_DEV, D_MODEL),
        'conv_w': jnp.concatenate([sums[4:7], jnp.zeros((5, D_CONV), F32)]).reshape(8, N_DEV, D_CONV // N_DEV)
        .transpose(1, 0, 2),
        'ffn_conv_w': dcw_ffn,
    }
    big_names = list(big_local)
    received = _exchange([big_local[n] for n in big_names], name='scatter_grads', scatter=True)
    for n, slots in zip(big_names, received):
        if n in ('conv_w', 'ffn_conv_w'):
            slots = slots[:, :3, :]
        outs = _adamw(slots, wts[n][0], mom_m[n][0], mom_v[n][0], 'adamw_' + n)
        for kind, val in zip(('g', 'd', 'm', 'v'), outs):
            res[kind, n] = val[None]
    outs = _adamw(g_w_ada[None], w_ada[0], m_w_ada[0], v_w_ada[0], 'adamw_w_ada')
    for kind, val in zip(('g', 'd', 'm', 'v'), outs):
        res[kind, 'w_ada'] = val[None]

    return (loss, grad_x[None], *[res['g', n] for n in WEIGHTS], *[res['d', n] for n in WEIGHTS],
            *[res['m', n] for n in WEIGHTS], *[res['v', n] for n in WEIGHTS])
```

```python
import math

import jax
import jax.numpy as jnp
from jax import lax
from jax.experimental import pallas as pl
from jax.experimental.pallas import tpu as pltpu

F32, BF16 = jnp.float32, jnp.bfloat16

D_MODEL = 1024
D_SSM = 512
D_CONV = 512
SSM_GROUP = 16
N_GROUPS = 32
SSM_STATE = 64
N_STATE = N_GROUPS * SSM_STATE
CONV_HEADS = 8
D_FF = 2816
N_MOD = 6
D_IN_PROJ = D_SSM + 3 * D_CONV
N_DEV = 8
FF_SHARD = 2 * D_FF // N_DEV
IN_SHARD = D_IN_PROJ // N_DEV
ADA_SHARD = N_MOD * D_MODEL // N_DEV
EPS = 1e-6
LAMBDA_RE_MAX = -1e-4
ADAM_LR, ADAM_B1, ADAM_B2, ADAM_EPS, ADAM_WD, ADAM_STEP = 0.001, 0.9, 0.999, 1e-08, 0.01, 10
GELU_C = math.sqrt(2.0 / math.pi)
GELU_A = 0.044715

SUBLANES = 8
HALO = 8
STATE_BLOCK = 256
CHAN_BLOCK = 128
VMEM_BIG = 48 << 20

WEIGHTS = ['w_ada', 'b_ada', 'g_pre_mix', 'g_post_mix', 'w_in', 'ssm_lam_re', 'ssm_lam_im', 'ssm_log_step',
           'ssm_b_re', 'ssm_b_im', 'ssm_c_re', 'ssm_c_im', 'ssm_d', 'glu_w', 'glu_b', 'g_out_ssm', 'conv_w',
           'g_out_conv', 'w_out', 'g_pre_ffn', 'g_post_ffn', 'w_up', 'ffn_conv_w', 'w_down']
SHARDED = ('w_ada', 'w_in', 'glu_w', 'conv_w', 'w_out', 'w_up', 'ffn_conv_w', 'w_down')
SMALL = tuple(n for n in WEIGHTS if n not in SHARDED)
PACK_COLS = 1024


def _call(body, *, name, grid, in_specs, out_specs, out_shape, scratch=(), sem=None, vmem=None, ride=None):
    params = {}
    if vmem is not None:
        params['vmem_limit_bytes'] = vmem
    if ride is None:
        if sem is not None:
            params['dimension_semantics'] = sem
        return pl.pallas_call(body, name=name, grid=grid, in_specs=in_specs, out_specs=out_specs,
                              out_shape=out_shape, scratch_shapes=list(scratch),
                              compiler_params=pltpu.CompilerParams(**params))
    arrs, scatter = ride
    single = not isinstance(out_shape, (list, tuple))
    out_shape_l = [out_shape] if single else list(out_shape)
    out_specs_l = [out_specs] if single else list(out_specs)
    n, n_in, n_out, n_scr = len(arrs), len(in_specs), len(out_shape_l), len(scratch)
    any_spec = pl.BlockSpec(memory_space=pl.ANY)
    params['dimension_semantics'] = ('arbitrary',) * len(grid)

    def carried(*refs):
        ins, rin = refs[:n_in], refs[n_in:n_in + n]
        outs, rout = refs[n_in + n:n_in + n + n_out], refs[n_in + n + n_out:n_in + 2 * n + n_out]
        scr, sems = refs[n_in + 2 * n + n_out:n_in + 2 * n + n_out + n_scr], refs[n_in + 2 * n + n_out + n_scr:]
        first = pl.program_id(0) == 0
        last = pl.program_id(0) == grid[0] - 1
        for ax in range(1, len(grid)):
            first = jnp.logical_and(first, pl.program_id(ax) == 0)
            last = jnp.logical_and(last, pl.program_id(ax) == grid[ax] - 1)

        @pl.when(first)
        def _():
            _exchange_start(rin, rout, sems, scatter)

        body(*ins, *outs, *scr)

        @pl.when(last)
        def _():
            _exchange_wait(rin, rout, sems, scatter)

    call = pl.pallas_call(carried, name=name, grid=grid, in_specs=list(in_specs) + [any_spec] * n,
                          out_specs=out_specs_l + [any_spec] * n,
                          out_shape=out_shape_l + _exchange_shapes(arrs, scatter),
                          scratch_shapes=list(scratch) + _exchange_sems(n),
                          compiler_params=pltpu.CompilerParams(**params))

    def run(*args):
        res = call(*args, *arrs)
        own = res[0] if single else list(res[:n_out])
        return own, list(res[n_out:])

    return run


def _const(shape):
    nd = len(shape)
    return pl.BlockSpec(shape, lambda *_: (0,) * nd)


def _sds(shape, dtype=F32):
    return jax.ShapeDtypeStruct(shape, dtype)


def _dot(a, b):
    return jnp.dot(a, b, preferred_element_type=F32)


def _dot_nt(a, b):
    return lax.dot_general(a, b, (((1,), (1,)), ((), ())), preferred_element_type=F32)


def _dot_tn(a, b):
    return lax.dot_general(a, b, (((0,), (0,)), ((), ())), preferred_element_type=F32)


def _dot_split(x, mat, parts):
    acc = None
    rem = x
    for _ in range(parts):
        piece = rem.astype(BF16)
        rem = rem - piece.astype(F32)
        term = _dot(piece, mat)
        acc = term if acc is None else acc + term
    return acc


def _sigmoid(x):
    return 1.0 / (1.0 + jnp.exp(-x))


def _gelu(x):
    t = jnp.tanh(GELU_C * (x + GELU_A * x * x * x))
    return 0.5 * x * (1.0 + t), t


def _gelu_grad(x, t):
    return 0.5 * (1.0 + t) + 0.5 * x * (1.0 - t * t) * GELU_C * (1.0 + 3.0 * GELU_A * x * x)


def _rsqrt_mean(x):
    return lax.rsqrt(jnp.mean(x * x, axis=-1, keepdims=True) + EPS)


def _colsum(x):
    return jnp.sum(x, axis=0, keepdims=True)


def _shift_down(x, k, halo):
    r = pltpu.roll(x, k, 0)
    row = lax.broadcasted_iota(jnp.int32, x.shape, 0)
    for q in range(k):
        r = jnp.where(row == q, halo[HALO - k + q:HALO - k + q + 1, :], r)
    return r


def _shift_up(x, k, halo):
    n = x.shape[0]
    r = pltpu.roll(x, n - k, 0)
    row = lax.broadcasted_iota(jnp.int32, x.shape, 0)
    for q in range(k):
        r = jnp.where(row == n - k + q, halo[q:q + 1, :], r)
    return r


def _conv3(x, halo, w_ref):
    x1 = _shift_down(x, 1, halo)
    x2 = _shift_down(x, 2, halo)
    return w_ref[0:1, :] * x2 + w_ref[1:2, :] * x1 + w_ref[2:3, :] * x, x1, x2


def _conv3_t(g, halo, w_ref):
    return w_ref[2:3, :] * g + w_ref[1:2, :] * _shift_up(g, 1, halo) + w_ref[0:1, :] * _shift_up(g, 2, halo)


def _silu_parts(x):
    s = _sigmoid(x)
    return x * s, s * (1.0 + x * (1.0 - s))


def _norm_bwd(dn, x, r, g):
    gd = g * dn
    return r * gd - x * (r * r * r) * jnp.mean(gd * x, axis=-1, keepdims=True)


def _head_norm_bwd(dn, y, rs, g, avg):
    gd = g * dn
    return rs * gd - y * (rs * rs * rs) * _dot_split(gd * y, avg, 2)


def _me():
    x, y, c = lax.axis_index('x'), lax.axis_index('y'), lax.axis_index('c')
    return x, y, c, 4 * x + 2 * y + c


def _peer(k):
    x, y, c, _ = _me()
    px = 1 - x if k & 4 else x
    py = 1 - y if k & 2 else y
    pc = 1 - c if k & 1 else c
    return (px, py, pc), 4 * px + 2 * py + pc


def _exchange_copies(ins, outs, sems, scatter):
    send_sems, recv_sems, local_sems = sems
    me = _me()[3]
    local, sends, recvs = [], [], []
    for a in range(len(ins)):
        src = ins[a].at[me] if scatter else ins[a]
        local.append(pltpu.make_async_copy(src, outs[a].at[me], local_sems.at[a]))
        for k in range(1, N_DEV):
            dev, idx = _peer(k)
            src = ins[a].at[idx] if scatter else ins[a]
            for dst, group in ((outs[a].at[me], sends), (outs[a].at[idx], recvs)):
                group.append(pltpu.make_async_remote_copy(
                    src_ref=src, dst_ref=dst, send_sem=send_sems.at[a, k - 1], recv_sem=recv_sems.at[a, k - 1],
                    device_id=dev, device_id_type=pl.DeviceIdType.MESH))
    return local, sends, recvs


def _exchange_start(ins, outs, sems, scatter):
    local, sends, _ = _exchange_copies(ins, outs, sems, scatter)
    for cp in local + sends:
        cp.start()


def _exchange_wait(ins, outs, sems, scatter):
    local, sends, recvs = _exchange_copies(ins, outs, sems, scatter)
    for cp in recvs:
        cp.wait_recv()
    for cp in sends:
        cp.wait_send()
    for cp in local:
        cp.wait()


def _exchange_shapes(arrs, scatter):
    return [_sds(a.shape if scatter else (N_DEV,) + a.shape, a.dtype) for a in arrs]


def _exchange_sems(n):
    return [pltpu.SemaphoreType.DMA((n, N_DEV - 1)), pltpu.SemaphoreType.DMA((n, N_DEV - 1)),
            pltpu.SemaphoreType.DMA((n,))]


def _exchange(arrs, *, name, scatter):
    n = len(arrs)

    def body(*refs):
        _exchange_start(refs[:n], refs[n:2 * n], refs[2 * n:], scatter)
        _exchange_wait(refs[:n], refs[n:2 * n], refs[2 * n:], scatter)

    any_spec = pl.BlockSpec(memory_space=pl.ANY)
    outs = pl.pallas_call(body, name=name, out_shape=_exchange_shapes(arrs, scatter), in_specs=[any_spec] * n,
                          out_specs=[any_spec] * n, scratch_shapes=_exchange_sems(n))(*arrs)
    return list(outs)


def _mod_cols(c_all, w_ada, b_cols):
    def body(c_ref, w_ref, b_ref, mod_ref, act_ref):
        c = c_ref[...]
        act = c * _sigmoid(c)
        act_ref[...] = act
        mod_ref[...] = _dot(act.astype(BF16), w_ref[...].astype(BF16)) + b_ref[...]

    return _call(body, name='mod_cols', grid=(1,),
                 in_specs=[_const(c_all.shape), _const(w_ada.shape), _const(b_cols.shape)],
                 out_specs=[_const((N_DEV, ADA_SHARD)), _const(c_all.shape)],
                 out_shape=[_sds((N_DEV, ADA_SHARD)), _sds(c_all.shape)], vmem=VMEM_BIG)(c_all, w_ada, b_cols)


def _grad_w_ada(act_t, dmod_cols):
    def body(a_ref, d_ref, o_ref):
        o_ref[...] = _dot(a_ref[...], d_ref[...])

    return _call(body, name='grad_w_ada', grid=(1,), in_specs=[_const(act_t.shape), _const(dmod_cols.shape)],
                 out_specs=_const((D_MODEL, ADA_SHARD)), out_shape=_sds((D_MODEL, ADA_SHARD)),
                 vmem=VMEM_BIG)(act_t, dmod_cols)


def _pre_mix(x, sc, sh, g, w_s, tm, ride):
    T = x.shape[0]

    def body(x_ref, sc_ref, sh_ref, g_ref, w_ref, proj_ref, h_ref):
        @pl.when(pl.program_id(1) == 0)
        def _():
            xv = x_ref[...]
            h_ref[...] = ((xv * _rsqrt_mean(xv) * g_ref[...]) * (1.0 + sc_ref[...]) + sh_ref[...]).astype(BF16)

        proj_ref[...] = _dot(h_ref[...], w_ref[...])

    row = pl.BlockSpec((tm, D_MODEL), lambda i, j: (i, 0))
    vec = _const((1, D_MODEL))
    return _call(body, name='pre_mix', grid=(T // tm, N_DEV),
                 in_specs=[row, vec, vec, vec, pl.BlockSpec((None, D_MODEL, IN_SHARD), lambda i, j: (j, 0, 0))],
                 out_specs=[pl.BlockSpec((tm, IN_SHARD), lambda i, j: (i, j)), row],
                 out_shape=[_sds((T, D_IN_PROJ)), _sds((T, D_MODEL), BF16)],
                 sem=('parallel', 'arbitrary'), ride=ride)(x, sc, sh, g, w_s)


def _halo_before(tm):
    return lambda i: jnp.maximum(i * (tm // HALO) - 1, 0)


def _halo_after(tm, T):
    return lambda i: jnp.minimum((i + 1) * (tm // HALO), T // HALO - 1)


def _mix_fwd(yssm, proj, d, glu_w, glu_b, g_ssm, cw, g_conv, avg16, avg64, tm):
    T = yssm.shape[0]
    hb = _halo_before(tm)

    def body(y_ref, p_ref, ph_ref, d_ref, gw_ref, gb_ref, gs_ref, cw_ref, gc_ref, a16_ref, a64_ref, o_ref):
        i = pl.program_id(0)
        u = p_ref[:, 0:D_SSM]
        y = y_ref[...] + d_ref[...] * u
        z, _ = _gelu(y)
        gate = _sigmoid(_dot(z.astype(BF16), gw_ref[...]) + gb_ref[...])
        ya = z * gate
        rs = lax.rsqrt(_dot_split(ya * ya, a16_ref[...], 2) + EPS)
        o_ref[:, 0:D_SSM] = (ya * rs * gs_ref[...]).astype(BF16)
        bg = p_ref[:, D_SSM:D_SSM + D_CONV]
        cv = p_ref[:, D_SSM + D_CONV:D_SSM + 2 * D_CONV] * p_ref[:, D_SSM + 2 * D_CONV:D_IN_PROJ]
        hv = ph_ref[:, D_SSM + D_CONV:D_SSM + 2 * D_CONV] * ph_ref[:, D_SSM + 2 * D_CONV:D_IN_PROJ]
        hv = jnp.where(i > 0, hv, 0.0)
        conv, _, _ = _conv3(cv, hv, cw_ref)
        yb = bg * conv
        rsb = lax.rsqrt(_dot_split(yb * yb, a64_ref[...], 2) + EPS)
        o_ref[:, D_SSM:D_MODEL] = (yb * rsb * gc_ref[...]).astype(BF16)

    vec = _const((1, D_SSM))
    sq = _const((D_SSM, D_SSM))
    return _call(body, name='mix_fwd', grid=(T // tm,),
                 in_specs=[pl.BlockSpec((tm, D_SSM), lambda i: (i, 0)), pl.BlockSpec((tm, D_IN_PROJ), lambda i: (i, 0)),
                           pl.BlockSpec((HALO, D_IN_PROJ), lambda i: (hb(i), 0)), vec, sq, vec, vec,
                           _const((3, D_CONV)), vec, sq, sq],
                 out_specs=pl.BlockSpec((tm, D_MODEL), lambda i: (i, 0)), out_shape=_sds((T, D_MODEL), BF16),
                 sem=('parallel',), vmem=VMEM_BIG)(yssm, proj, proj, d, glu_w, glu_b, g_ssm, cw, g_conv, avg16, avg64)


def _out_proj(ycat, w_out, x, gt, g_post, g_pre, sc, sh, tm):
    T = x.shape[0]

    def body(y_ref, w_ref, x_ref, gt_ref, gp_ref, g2_ref, sc_ref, sh_ref, o_ref, x1_ref, h_ref):
        o = _dot(y_ref[...], w_ref[...])
        o_ref[...] = o
        x1 = x_ref[...] + gt_ref[...] * (o * _rsqrt_mean(o) * gp_ref[...])
        x1_ref[...] = x1
        h_ref[...] = ((x1 * _rsqrt_mean(x1) * g2_ref[...]) * (1.0 + sc_ref[...]) + sh_ref[...]).astype(BF16)

    row = pl.BlockSpec((tm, D_MODEL), lambda i: (i, 0))
    vec = _const((1, D_MODEL))
    return _call(body, name='out_proj', grid=(T // tm,),
                 in_specs=[row, _const((D_MODEL, D_MODEL)), row, vec, vec, vec, vec, vec],
                 out_specs=[row, row, row],
                 out_shape=[_sds((T, D_MODEL)), _sds((T, D_MODEL)), _sds((T, D_MODEL), BF16)],
                 sem=('parallel',), vmem=VMEM_BIG)(ycat, w_out, x, gt, g_post, g_pre, sc, sh)


def _ffn_up(h2, w_s, tm):
    T = h2.shape[0]

    def body(h_ref, w_ref, o_ref):
        o_ref[...] = _dot(h_ref[...], w_ref[...])

    return _call(body, name='ffn_up', grid=(T // tm, N_DEV),
                 in_specs=[pl.BlockSpec((tm, D_MODEL), lambda i, j: (i, 0)),
                           pl.BlockSpec((None, D_MODEL, FF_SHARD), lambda i, j: (j, 0, 0))],
                 out_specs=pl.BlockSpec((None, tm, FF_SHARD), lambda i, j: (j, i, 0)),
                 out_shape=_sds((N_DEV, T, FF_SHARD)), sem=('parallel', 'parallel'))(h2, w_s)


def _ffn_hidden(up_ref, halo_ref, cw_ref, i):
    hid = []
    for part in range(2):
        halo = jnp.where(i > 0, halo_ref[part], 0.0)
        hid.append(_conv3(up_ref[part], halo, cw_ref.at[part])[0])
    return hid


def _ffn_act(up4, cw4, tm):
    T = up4.shape[2]
    hb = _halo_before(tm)

    def body(up_ref, halo_ref, cw_ref, o_ref):
        hid_a, hid_v = _ffn_hidden(up_ref, halo_ref, cw_ref, pl.program_id(0))
        o_ref[...] = (_silu_parts(hid_a)[0] * hid_v).astype(BF16)

    return _call(body, name='ffn_act', grid=(T // tm, 4),
                 in_specs=[pl.BlockSpec((2, None, tm, FF_SHARD), lambda i, j: (0, j, i, 0)),
                           pl.BlockSpec((2, None, HALO, FF_SHARD), lambda i, j: (0, j, hb(i), 0)),
                           pl.BlockSpec((2, None, 3, FF_SHARD), lambda i, j: (0, j, 0, 0))],
                 out_specs=pl.BlockSpec((None, tm, FF_SHARD), lambda i, j: (j, i, 0)),
                 out_shape=_sds((4, T, FF_SHARD), BF16), sem=('parallel', 'parallel'))(up4, up4, cw4)


def _ffn_down(act, wd4, x1, tgt, gt, g_post, tm):
    T = x1.shape[0]
    nb = T // tm

    def body(a_ref, w_ref, x1_ref, t_ref, gt_ref, g_ref, dn_ref, dx_ref, loss_ref):
        j = pl.program_id(1)
        part = _dot(a_ref[...], w_ref[...])

        @pl.when(j == 0)
        def _():
            dn_ref[...] = part

        @pl.when(j > 0)
        def _():
            dn_ref[...] += part

        @pl.when(j == 3)
        def _():
            dn = dn_ref[...]
            x2 = x1_ref[...] + gt_ref[...] * (dn * _rsqrt_mean(dn) * g_ref[...])
            err = x2 - t_ref[...]
            dx_ref[...] = err * (1.0 / D_MODEL)
            tot = jnp.sum(jnp.sum(err * err, axis=1, keepdims=True), axis=0, keepdims=True) * (0.5 / D_MODEL)
            loss_ref[...] = jnp.broadcast_to(tot, (8, 128))

    row = pl.BlockSpec((tm, D_MODEL), lambda i, j: (i, 0))
    vec = _const((1, D_MODEL))
    return _call(body, name='ffn_down', grid=(nb, 4),
                 in_specs=[pl.BlockSpec((None, tm, FF_SHARD), lambda i, j: (j, i, 0)),
                           pl.BlockSpec((None, FF_SHARD, D_MODEL), lambda i, j: (j, 0, 0)), row, row, vec, vec],
                 out_specs=[row, row, pl.BlockSpec((None, 8, 128), lambda i, j: (i, 0, 0))],
                 out_shape=[_sds((T, D_MODEL)), _sds((T, D_MODEL)), _sds((nb, 8, 128))],
                 sem=('parallel', 'arbitrary'))(act, wd4, x1, tgt, gt, g_post)


def _ssm_prep(lre, lim, lst, b_re, b_im):
    def body(lre_ref, lim_ref, lst_ref, br_ref, bi_ref, ar_ref, ai_ref, bbr_ref, bbi_ref):
        ar, ai, qr, qi = _zoh(lre_ref[...], lim_ref[...], lst_ref[...])[:4]
        ar_ref[...] = ar
        ai_ref[...] = ai
        bbr_ref[...] = qr * br_ref[...] - qi * bi_ref[...]
        bbi_ref[...] = qr * bi_ref[...] + qi * br_ref[...]

    shp = lre.shape
    return _call(body, name='ssm_prep', grid=(1,), in_specs=[_const(shp)] * 5, out_specs=[_const(shp)] * 4,
                 out_shape=[_sds(shp)] * 4)(lre, lim, lst, b_re, b_im)


def _zoh(lre, lim, lst):
    lr = jnp.minimum(lre, LAMBDA_RE_MAX)
    st = jnp.exp(lst)
    mag = jnp.exp(lr * st)
    ar = mag * jnp.cos(lim * st)
    ai = mag * jnp.sin(lim * st)
    den = lr * lr + lim * lim
    qr = ((ar - 1.0) * lr + ai * lim) / den
    qi = (ai * lr - (ar - 1.0) * lim) / den
    return ar, ai, qr, qi, lr, st, den


def _ssm_prep_bwd(lre, lim, lst, b_re, b_im, dbbr, dbbi, dar, dai, seg):
    def body(lre_ref, lim_ref, lst_ref, br_ref, bi_ref, dbbr_ref, dbbi_ref, dar_ref, dai_ref, seg_ref,
             dbr_ref, dbi_ref, dlre_ref, dlim_ref, dlst_ref):
        lre_v = lre_ref[...]
        li = lim_ref[...]
        ar, ai, qr, qi, lr, st, den = _zoh(lre_v, li, lst_ref[...])
        br, bi, gbr, gbi = br_ref[...], bi_ref[...], dbbr_ref[...], dbbi_ref[...]
        dbr_ref[...] = qr * gbr + qi * gbi
        dbi_ref[...] = qr * gbi - qi * gbr
        gqr = _dot_split(br * gbr + bi * gbi, seg_ref[...], 3)
        gqi = _dot_split(br * gbi - bi * gbr, seg_ref[...], 3)
        ir, ii = lr / den, -li / den
        gar = dar_ref[...] + ir * gqr + ii * gqi
        gai = dai_ref[...] + ir * gqi - ii * gqr
        tr, ti = qr * ir - qi * ii, qr * ii + qi * ir
        glr = -(tr * gqr + ti * gqi)
        gli = -(tr * gqi - ti * gqr)
        gzr = ar * gar + ai * gai
        gzi = ar * gai - ai * gar
        glr = glr + st * gzr
        gli = gli + st * gzi
        gst = (lr * gzr + li * gzi) * st
        dlre_ref[...] = jnp.where(lre_v < LAMBDA_RE_MAX, glr, 0.0)
        dlim_ref[...] = gli
        dlst_ref[...] = jnp.sum(gst, axis=1, keepdims=True) * (1.0 / SSM_GROUP)

    shp = lre.shape
    return _call(body, name='ssm_prep_bwd', grid=(1,), in_specs=[_const(shp)] * 9 + [_const(seg.shape)],
                 out_specs=[_const(shp)] * 4 + [_const((N_GROUPS, 1))],
                 out_shape=[_sds(shp)] * 4 + [_sds((N_GROUPS, 1))], vmem=VMEM_BIG)(
                     lre, lim, lst, b_re, b_im, dbbr, dbbi, dar, dai, seg)


def _scan_specs(T):
    half = lambda cb: cb // 2
    return dict(
        chan=pl.BlockSpec((T, CHAN_BLOCK), lambda cb: (0, half(cb))),
        state=pl.BlockSpec((T, STATE_BLOCK), lambda cb: (0, cb)),
        b=pl.BlockSpec((CHAN_BLOCK, STATE_BLOCK), lambda cb: (half(cb), cb)),
        c=pl.BlockSpec((STATE_BLOCK, CHAN_BLOCK), lambda cb: (cb, half(cb))),
        lam=pl.BlockSpec((1, STATE_BLOCK), lambda cb: (0, cb)),
    )


def _rows8(i):
    return pl.ds(pl.multiple_of(i * SUBLANES, SUBLANES), SUBLANES)


def _ssm_fwd(u_perm, b_re, b_im, c_re, c_im, lam_r, lam_i, ride):
    T = u_perm.shape[0]
    ls = T // SUBLANES
    rc = min(512, T)
    sp = _scan_specs(T)

    def body(u_ref, bre_ref, bim_ref, cre_ref, cim_ref, lr_ref, li_ref, sre_ref, sim_ref, y_ref):
        cb = pl.program_id(0)
        for c in range(T // rc):
            rows = pl.ds(c * rc, rc)
            sre_ref[rows, :] = _dot(u_ref[rows, :], bre_ref[...])
            sim_ref[rows, :] = _dot(u_ref[rows, :], bim_ref[...])
        shp = (SUBLANES, STATE_BLOCK)
        lr = jnp.broadcast_to(lr_ref[...], shp)
        li = jnp.broadcast_to(li_ref[...], shp)
        zero = jnp.zeros(shp, F32)

        def step(i, carry):
            sr, si, wr, wi = carry
            rows = _rows8(i)
            nr = lr * sr - li * si + sre_ref[rows, :]
            ni = lr * si + li * sr + sim_ref[rows, :]
            sre_ref[rows, :] = nr
            sim_ref[rows, :] = ni
            return nr, ni, lr * wr - li * wi, lr * wi + li * wr

        fr, fi, pr, pi_ = lax.fori_loop(0, ls, step, (zero, zero, jnp.ones(shp, F32), zero))
        row = lax.broadcasted_iota(jnp.int32, shp, 0)
        ir, ii = zero, zero
        for _ in range(SUBLANES - 1):
            er = fr + pr * ir - pi_ * ii
            ei = fi + pr * ii + pi_ * ir
            ir = jnp.where(row == 0, 0.0, pltpu.roll(er, 1, 0))
            ii = jnp.where(row == 0, 0.0, pltpu.roll(ei, 1, 0))

        def fix(i, carry):
            wr, wi = carry
            rows = _rows8(i)
            sre_ref[rows, :] += wr * ir - wi * ii
            sim_ref[rows, :] += wr * ii + wi * ir
            return lr * wr - li * wi, lr * wi + li * wr

        lax.fori_loop(0, ls, fix, (lr, li))
        for c in range(T // rc):
            rows = pl.ds(c * rc, rc)
            yc = _dot(sre_ref[rows, :].astype(BF16), cre_ref[...]) - _dot(sim_ref[rows, :].astype(BF16), cim_ref[...])

            @pl.when(cb % 2 == 0)
            def _():
                y_ref[rows, :] = yc

            @pl.when(cb % 2 == 1)
            def _():
                y_ref[rows, :] += yc

    return _call(body, name='ssm_fwd', grid=(N_STATE // STATE_BLOCK,),
                 in_specs=[sp['chan'], sp['b'], sp['b'], sp['c'], sp['c'], sp['lam'], sp['lam']],
                 out_specs=[sp['state'], sp['state'], sp['chan']],
                 out_shape=[_sds((T, N_STATE)), _sds((T, N_STATE)), _sds((T, D_SSM))],
                 sem=('arbitrary',), vmem=VMEM_BIG, ride=ride)(u_perm, b_re, b_im, c_re, c_im, lam_r, lam_i)


def _ssm_bwd(dy_perm, u_perm, s_re, s_im, b_re, b_im, c_re, c_im, lam_r, lam_i, ride):
    T = u_perm.shape[0]
    ls = T // SUBLANES
    rc = min(512, T)
    sp = _scan_specs(T)
    ncb = N_STATE // STATE_BLOCK

    def body(dy_ref, u_ref, sre_ref, sim_ref, bre_ref, bim_ref, cre_ref, cim_ref, lr_ref, li_ref,
             du_ref, dbr_ref, dbi_ref, dcr_ref, dci_ref, dar_ref, dai_ref, gre_ref, gim_ref):
        cb = pl.program_id(0)
        for c in range(T // rc):
            rows = pl.ds(c * rc, rc)
            gre_ref[rows, :] = _dot_nt(dy_ref[rows, :], cre_ref[...])
            gim_ref[rows, :] = -_dot_nt(dy_ref[rows, :], cim_ref[...])
        shp = (SUBLANES, STATE_BLOCK)
        lr = jnp.broadcast_to(lr_ref[...], shp)
        li = jnp.broadcast_to(li_ref[...], shp)
        zero = jnp.zeros(shp, F32)

        def step(k, carry):
            gr, gi, wr, wi = carry
            rows = _rows8(ls - 1 - k)
            nr = lr * gr + li * gi + gre_ref[rows, :]
            ni = lr * gi - li * gr + gim_ref[rows, :]
            gre_ref[rows, :] = nr
            gim_ref[rows, :] = ni
            return nr, ni, lr * wr + li * wi, lr * wi - li * wr

        fr, fi, pr, pi_ = lax.fori_loop(0, ls, step, (zero, zero, jnp.ones(shp, F32), zero))
        row = lax.broadcasted_iota(jnp.int32, shp, 0)
        cr, ci = zero, zero
        for _ in range(SUBLANES - 1):
            er = fr + pr * cr - pi_ * ci
            ei = fi + pr * ci + pi_ * cr
            cr = jnp.where(row == SUBLANES - 1, 0.0, pltpu.roll(er, SUBLANES - 1, 0))
            ci = jnp.where(row == SUBLANES - 1, 0.0, pltpu.roll(ei, SUBLANES - 1, 0))

        def fix(k, carry):
            wr, wi, ar, ai = carry
            rows = _rows8(ls - 1 - k)
            gr = gre_ref[rows, :] + wr * cr - wi * ci
            gi = gim_ref[rows, :] + wr * ci + wi * cr
            gre_ref[rows, :] = gr
            gim_ref[rows, :] = gi
            prev = _rows8(ls - 2 - k)
            spr, spi = sre_ref[prev, :], sim_ref[prev, :]
            return (lr * wr + li * wi, lr * wi - li * wr, ar + gr * spr + gi * spi, ai + gi * spr - gr * spi)

        wr, wi, ar, ai = lax.fori_loop(0, ls - 1, fix, (lr, -li, zero, zero))
        first = pl.ds(0, SUBLANES)
        last = pl.ds((ls - 1) * SUBLANES, SUBLANES)
        gr = gre_ref[first, :] + wr * cr - wi * ci
        gi = gim_ref[first, :] + wr * ci + wi * cr
        gre_ref[first, :] = gr
        gim_ref[first, :] = gi
        spr = jnp.where(row == 0, 0.0, pltpu.roll(sre_ref[last, :], 1, 0))
        spi = jnp.where(row == 0, 0.0, pltpu.roll(sim_ref[last, :], 1, 0))
        dar_ref[...] = _colsum(ar + gr * spr + gi * spi)
        dai_ref[...] = _colsum(ai + gi * spr - gr * spi)

        for c in range(T // rc):
            rows = pl.ds(c * rc, rc)
            g_r, g_i = gre_ref[rows, :].astype(BF16), gim_ref[rows, :].astype(BF16)
            s_r, s_i = sre_ref[rows, :].astype(BF16), sim_ref[rows, :].astype(BF16)
            ub, dyb = u_ref[rows, :], dy_ref[rows, :]
            duc = _dot_nt(g_r, bre_ref[...]) + _dot_nt(g_i, bim_ref[...])
            parts = (_dot_tn(ub, g_r), _dot_tn(ub, g_i), _dot_tn(s_r, dyb), -_dot_tn(s_i, dyb))
            outs = (dbr_ref, dbi_ref, dcr_ref, dci_ref)
            for o_ref, part in zip(outs, parts):
                if c == 0:
                    o_ref[...] = part
                else:
                    o_ref[...] += part

            @pl.when(cb % 2 == 0)
            def _():
                du_ref[rows, :] = duc

            @pl.when(cb % 2 == 1)
            def _():
                du_ref[rows, :] += duc

    blk = lambda r, c: pl.BlockSpec((None, r, c), lambda cb: (cb, 0, 0))
    return _call(body, name='ssm_bwd', grid=(ncb,),
                 in_specs=[sp['chan'], sp['chan'], sp['state'], sp['state'], sp['b'], sp['b'], sp['c'], sp['c'],
                           sp['lam'], sp['lam']],
                 out_specs=[sp['chan'], blk(CHAN_BLOCK, STATE_BLOCK), blk(CHAN_BLOCK, STATE_BLOCK),
                            blk(STATE_BLOCK, CHAN_BLOCK), blk(STATE_BLOCK, CHAN_BLOCK), blk(1, STATE_BLOCK),
                            blk(1, STATE_BLOCK)],
                 out_shape=[_sds((T, D_SSM)), _sds((ncb, CHAN_BLOCK, STATE_BLOCK)), _sds((ncb, CHAN_BLOCK, STATE_BLOCK)),
                            _sds((ncb, STATE_BLOCK, CHAN_BLOCK)), _sds((ncb, STATE_BLOCK, CHAN_BLOCK)),
                            _sds((ncb, 1, STATE_BLOCK)), _sds((ncb, 1, STATE_BLOCK))],
                 scratch=[pltpu.VMEM((T, STATE_BLOCK), F32), pltpu.VMEM((T, STATE_BLOCK), F32)],
                 sem=('arbitrary',), vmem=VMEM_BIG, ride=ride)(dy_perm, u_perm, s_re, s_im, b_re, b_im, c_re, c_im,
                                                               lam_r, lam_i)


def _post_norm_bwd(dx, val, gate, g, tm, name):
    T = dx.shape[0]

    def body(dx_ref, v_ref, gt_ref, g_ref, dv_ref, dgt_ref, dg_ref):
        @pl.when(pl.program_id(0) == 0)
        def _():
            dgt_ref[...] = jnp.zeros_like(dgt_ref)
            dg_ref[...] = jnp.zeros_like(dg_ref)

        dxv, v, gv = dx_ref[...], v_ref[...], g_ref[...]
        r = _rsqrt_mean(v)
        dgt_ref[...] += _colsum(dxv * (v * r * gv))
        dn = dxv * gt_ref[...]
        dg_ref[...] += _colsum(dn * v * r)
        dv_ref[...] = _norm_bwd(dn, v, r, gv).astype(BF16)

    row = pl.BlockSpec((tm, D_MODEL), lambda i: (i, 0))
    vec = _const((1, D_MODEL))
    return _call(body, name=name, grid=(T // tm,), in_specs=[row, row, vec, vec], out_specs=[row, vec, vec],
                 out_shape=[_sds((T, D_MODEL), BF16), _sds((1, D_MODEL)), _sds((1, D_MODEL))],
                 sem=('arbitrary',))(dx, val, gate, g)


def _ffn_dact(ddn, wd4, up4, cw4, tm):
    T = ddn.shape[0]
    hb = _halo_before(tm)

    def body(d_ref, w_ref, up_ref, halo_ref, cw_ref, o_ref):
        dact = _dot_nt(d_ref[...], w_ref[...])
        hid_a, hid_v = _ffn_hidden(up_ref, halo_ref, cw_ref, pl.program_id(0))
        silu, dsilu = _silu_parts(hid_a)
        o_ref[0] = dact * hid_v * dsilu
        o_ref[1] = dact * silu

    return _call(body, name='ffn_dact', grid=(T // tm, 4),
                 in_specs=[pl.BlockSpec((tm, D_MODEL), lambda i, j: (i, 0)),
                           pl.BlockSpec((None, FF_SHARD, D_MODEL), lambda i, j: (j, 0, 0)),
                           pl.BlockSpec((2, None, tm, FF_SHARD), lambda i, j: (0, j, i, 0)),
                           pl.BlockSpec((2, None, HALO, FF_SHARD), lambda i, j: (0, j, hb(i), 0)),
                           pl.BlockSpec((2, None, 3, FF_SHARD), lambda i, j: (0, j, 0, 0))],
                 out_specs=pl.BlockSpec((2, None, tm, FF_SHARD), lambda i, j: (0, j, i, 0)),
                 out_shape=_sds((2, 4, T, FF_SHARD)), sem=('parallel', 'parallel'))(ddn, wd4, up4, up4, cw4)


def _ffn_dup(dhid8, up8, cw8, tm):
    T = up8.shape[1]
    nb = T // tm
    hb, ha = _halo_before(tm), _halo_after(tm, T)

    def body(dh_ref, dha_ref, up_ref, uph_ref, cw_ref, dup_ref, dcw_ref):
        i = pl.program_id(1)

        @pl.when(i == 0)
        def _():
            dcw_ref[...] = jnp.zeros_like(dcw_ref)

        dh = dh_ref[...]
        dup_ref[...] = _conv3_t(dh, jnp.where(i < nb - 1, dha_ref[...], 0.0), cw_ref).astype(BF16)
        up = up_ref[...]
        halo = jnp.where(i > 0, uph_ref[...], 0.0)
        dcw_ref[0:1, :] += _colsum(dh * _shift_down(up, 2, halo))
        dcw_ref[1:2, :] += _colsum(dh * _shift_down(up, 1, halo))
        dcw_ref[2:3, :] += _colsum(dh * up)

    main = pl.BlockSpec((None, tm, FF_SHARD), lambda j, i: (j, i, 0))
    return _call(body, name='ffn_dup', grid=(N_DEV, nb),
                 in_specs=[main, pl.BlockSpec((None, HALO, FF_SHARD), lambda j, i: (j, ha(i), 0)), main,
                           pl.BlockSpec((None, HALO, FF_SHARD), lambda j, i: (j, hb(i), 0)),
                           pl.BlockSpec((None, 3, FF_SHARD), lambda j, i: (j, 0, 0))],
                 out_specs=[main, pl.BlockSpec((None, 8, FF_SHARD), lambda j, i: (j, 0, 0))],
                 out_shape=[_sds((N_DEV, T, FF_SHARD), BF16), _sds((N_DEV, 8, FF_SHARD))],
                 sem=('parallel', 'arbitrary'))(dhid8, dhid8, up8, up8, cw8)


def _grad_tn(a, b, a_spec, b_spec, groups, m, n, tk, name, ride=None):
    T = a.shape[-2]
    nk = T // tk

    def body(a_ref, b_ref, o_ref, acc_ref):
        k = pl.program_id(1)
        part = _dot_tn(a_ref[...], b_ref[...])

        @pl.when(k == 0)
        def _():
            acc_ref[...] = part

        @pl.when(k > 0)
        def _():
            acc_ref[...] += part

        @pl.when(k == nk - 1)
        def _():
            o_ref[...] = acc_ref[...].astype(BF16)

    return _call(body, name=name, grid=(groups, nk), in_specs=[a_spec, b_spec],
                 out_specs=pl.BlockSpec((None, m, n), lambda g, k: (g, 0, 0)), out_shape=_sds((groups, m, n), BF16),
                 scratch=[pltpu.VMEM((m, n), F32)], sem=('parallel', 'arbitrary'), vmem=VMEM_BIG, ride=ride)(a, b)


def _pre_norm_bwd(dz, dz_spec, w_s, xin, dres, sc, g, tm, name, ride):
    T = xin.shape[0]
    n = w_s.shape[2]

    def body(dz_ref, w_ref, x_ref, dr_ref, sc_ref, g_ref, dx_ref, dsh_ref, dsc_ref, dg_ref):
        i, j = pl.program_id(0), pl.program_id(1)
        part = _dot_nt(dz_ref[...], w_ref[...])

        @pl.when(jnp.logical_and(i == 0, j == 0))
        def _():
            dsh_ref[...] = jnp.zeros_like(dsh_ref)
            dsc_ref[...] = jnp.zeros_like(dsc_ref)
            dg_ref[...] = jnp.zeros_like(dg_ref)

        @pl.when(j == 0)
        def _():
            dx_ref[...] = part

        @pl.when(j > 0)
        def _():
            dx_ref[...] += part

        @pl.when(j == N_DEV - 1)
        def _():
            dh, xv, gv = dx_ref[...], x_ref[...], g_ref[...]
            r = _rsqrt_mean(xv)
            dsh_ref[...] += _colsum(dh)
            dsc_ref[...] += _colsum(dh * (xv * r * gv))
            dxn = dh * (1.0 + sc_ref[...])
            dg_ref[...] += _colsum(dxn * xv * r)
            dx_ref[...] = dr_ref[...] + _norm_bwd(dxn, xv, r, gv)

    row = pl.BlockSpec((tm, D_MODEL), lambda i, j: (i, 0))
    vec = _const((1, D_MODEL))
    return _call(body, name=name, grid=(T // tm, N_DEV),
                 in_specs=[dz_spec, pl.BlockSpec((None, D_MODEL, n), lambda i, j: (j, 0, 0)), row, row, vec, vec],
                 out_specs=[row, vec, vec, vec],
                 out_shape=[_sds((T, D_MODEL)), _sds((1, D_MODEL)), _sds((1, D_MODEL)), _sds((1, D_MODEL))],
                 sem=('arbitrary', 'arbitrary'), ride=ride)(dz, w_s, xin, dres, sc, g)


def _d_ycat(d_o, w_out, tm):
    T = d_o.shape[0]

    def body(d_ref, w_ref, o_ref):
        o_ref[...] = _dot_nt(d_ref[...], w_ref[...])

    row = pl.BlockSpec((tm, D_MODEL), lambda i: (i, 0))
    return _call(body, name='d_ycat', grid=(T // tm,), in_specs=[row, _const((D_MODEL, D_MODEL))], out_specs=row,
                 out_shape=_sds((T, D_MODEL)), sem=('parallel',))(d_o, w_out)


def _mix_bwd(dycat, yssm, proj, d, glu_w, glu_b, g_ssm, cw, g_conv, avg16, avg64, tm):
    T = yssm.shape[0]
    hb = _halo_before(tm)

    def body(dyc_ref, y_ref, p_ref, ph_ref, d_ref, gw_ref, gb_ref, gs_ref, cw_ref, gc_ref, a16_ref, a64_ref,
             dy_ref, dconv_ref, dbg_ref, z_ref, dlin_ref, acc_ref):
        i = pl.program_id(0)

        @pl.when(i == 0)
        def _():
            acc_ref[...] = jnp.zeros_like(acc_ref)

        u = p_ref[:, 0:D_SSM]
        y = y_ref[...] + d_ref[...] * u
        z, t = _gelu(y)
        gate = _sigmoid(_dot(z.astype(BF16), gw_ref[...]) + gb_ref[...])
        ya = z * gate
        rs = lax.rsqrt(_dot_split(ya * ya, a16_ref[...], 2) + EPS)
        dna = dyc_ref[:, 0:D_SSM]
        acc_ref[1:2, :] += _colsum(dna * ya * rs)
        dya = _head_norm_bwd(dna, ya, rs, gs_ref[...], a16_ref[...])
        dlin = dya * z * gate * (1.0 - gate)
        acc_ref[0:1, :] += _colsum(dlin)
        dlin_b = dlin.astype(BF16)
        dz = dya * gate + _dot_nt(dlin_b, gw_ref[...])
        dy = dz * _gelu_grad(y, t)
        acc_ref[3:4, :] += _colsum(dy * u)
        dy_ref[...] = dy
        z_ref[...] = z.astype(BF16)
        dlin_ref[...] = dlin_b

        bg = p_ref[:, D_SSM:D_SSM + D_CONV]
        cv = p_ref[:, D_SSM + D_CONV:D_SSM + 2 * D_CONV] * p_ref[:, D_SSM + 2 * D_CONV:D_IN_PROJ]
        hv = ph_ref[:, D_SSM + D_CONV:D_SSM + 2 * D_CONV] * ph_ref[:, D_SSM + 2 * D_CONV:D_IN_PROJ]
        hv = jnp.where(i > 0, hv, 0.0)
        conv, cv1, cv2 = _conv3(cv, hv, cw_ref)
        yb = bg * conv
        rsb = lax.rsqrt(_dot_split(yb * yb, a64_ref[...], 2) + EPS)
        dnb = dyc_ref[:, D_SSM:D_MODEL]
        acc_ref[2:3, :] += _colsum(dnb * yb * rsb)
        dyb = _head_norm_bwd(dnb, yb, rsb, gc_ref[...], a64_ref[...])
        dbg_ref[...] = dyb * conv
        dconv = dyb * bg
        dconv_ref[...] = dconv
        acc_ref[4:5, :] += _colsum(dconv * cv2)
        acc_ref[5:6, :] += _colsum(dconv * cv1)
        acc_ref[6:7, :] += _colsum(dconv * cv)

    vec = _const((1, D_SSM))
    sq = _const((D_SSM, D_SSM))
    half = pl.BlockSpec((tm, D_SSM), lambda i: (i, 0))
    return _call(body, name='mix_bwd', grid=(T // tm,),
                 in_specs=[pl.BlockSpec((tm, D_MODEL), lambda i: (i, 0)), half,
                           pl.BlockSpec((tm, D_IN_PROJ), lambda i: (i, 0)),
                           pl.BlockSpec((HALO, D_IN_PROJ), lambda i: (hb(i), 0)), vec, sq, vec, vec,
                           _const((3, D_CONV)), vec, sq, sq],
                 out_specs=[half, half, half, half, half, _const((8, D_SSM))],
                 out_shape=[_sds((T, D_SSM)), _sds((T, D_SSM)), _sds((T, D_SSM)), _sds((T, D_SSM), BF16),
                            _sds((T, D_SSM), BF16), _sds((8, D_SSM))],
                 sem=('arbitrary',), vmem=VMEM_BIG)(dycat, yssm, proj, proj, d, glu_w, glu_b, g_ssm, cw, g_conv,
                                                   avg16, avg64)


def _mix_bwd_proj(dconv, proj, du_ssm, dy, d, dbg, cw, tm):
    T = dy.shape[0]
    nb = T // tm
    ha = _halo_after(tm, T)

    def body(dc_ref, dch_ref, cg_ref, v_ref, du_ref, dy_ref, d_ref, dbg_ref, cw_ref, o_ref):
        i = pl.program_id(0)
        dcv = _conv3_t(dc_ref[...], jnp.where(i < nb - 1, dch_ref[...], 0.0), cw_ref)
        o_ref[:, 0:D_SSM] = (du_ref[...] + dy_ref[...] * d_ref[...]).astype(BF16)
        o_ref[:, D_SSM:D_SSM + D_CONV] = dbg_ref[...].astype(BF16)
        o_ref[:, D_SSM + D_CONV:D_SSM + 2 * D_CONV] = (dcv * v_ref[...]).astype(BF16)
        o_ref[:, D_SSM + 2 * D_CONV:D_IN_PROJ] = (dcv * cg_ref[...]).astype(BF16)

    half = pl.BlockSpec((tm, D_SSM), lambda i: (i, 0))
    return _call(body, name='mix_bwd_proj', grid=(nb,),
                 in_specs=[half, pl.BlockSpec((HALO, D_CONV), lambda i: (ha(i), 0)),
                           pl.BlockSpec((tm, D_CONV), lambda i: (i, 2)), pl.BlockSpec((tm, D_CONV), lambda i: (i, 3)),
                           half, half, _const((1, D_SSM)), half, _const((3, D_CONV))],
                 out_specs=pl.BlockSpec((tm, D_IN_PROJ), lambda i: (i, 0)), out_shape=_sds((T, D_IN_PROJ), BF16),
                 sem=('parallel',))(dconv, dconv, proj, proj, du_ssm, dy, d, dbg, cw)


def _row_tile(rows, cols, slots):
    for cand in (512, 256, 128, 64, 32, 16, 8):
        if rows % cand == 0 and slots * cand * cols * 4 <= (2 << 20):
            return cand
    return rows


def _adamw(gslots, w, m, v, name):
    slots, rows, cols = gslots.shape
    tr = _row_tile(rows, cols, slots)

    def body(g_ref, w_ref, m_ref, v_ref, go_ref, d_ref, mo_ref, vo_ref):
        g = g_ref[0].astype(F32)
        for s in range(1, slots):
            g = g + g_ref[s].astype(F32)
        m2 = ADAM_B1 * m_ref[...] + (1.0 - ADAM_B1) * g
        v2 = ADAM_B2 * v_ref[...] + (1.0 - ADAM_B2) * (g * g)
        m_hat = m2 / (1.0 - ADAM_B1 ** ADAM_STEP)
        v_hat = v2 / (1.0 - ADAM_B2 ** ADAM_STEP)
        go_ref[...] = g
        d_ref[...] = -ADAM_LR * (m_hat / (jnp.sqrt(v_hat) + ADAM_EPS) + ADAM_WD * w_ref[...])
        mo_ref[...] = m2
        vo_ref[...] = v2

    blk = pl.BlockSpec((tr, cols), lambda i: (i, 0))
    return _call(body, name=name, grid=(rows // tr,),
                 in_specs=[pl.BlockSpec((slots, tr, cols), lambda i: (0, i, 0)), blk, blk, blk],
                 out_specs=[blk] * 4, out_shape=[_sds((rows, cols))] * 4, sem=('parallel',))(gslots, w, m, v)


def _to_scan_rows(a):
    T, n = a.shape
    return a.reshape(SUBLANES, T // SUBLANES, n).transpose(1, 0, 2).reshape(T, n)


def _from_scan_rows(a):
    T, n = a.shape
    return a.reshape(T // SUBLANES, SUBLANES, n).transpose(1, 0, 2).reshape(T, n)


def _expand(a):
    return jnp.repeat(a, SSM_GROUP, axis=1)


def _block_diag_b(bb):
    eye = jnp.eye(N_GROUPS, dtype=bb.dtype)
    return (bb.transpose(0, 2, 1)[:, :, None, :] * eye[:, None, :, None]).reshape(D_SSM, N_STATE)


def _block_diag_c(cc):
    eye = jnp.eye(N_GROUPS, dtype=cc.dtype)
    return (cc.transpose(0, 2, 1)[:, :, None, :] * eye[:, None, :, None]).reshape(N_STATE, D_SSM)


def _diag_blocks(x, chan_major):
    e2 = jnp.eye(2, dtype=x.dtype)
    e4 = jnp.eye(4, dtype=x.dtype)
    if chan_major:
        x = x.reshape(4, 2, 2, 4, SSM_GROUP, 4, SSM_STATE)
        x = x * e2[None, :, :, None, None, None, None] * e4[None, None, None, :, None, :, None]
        return x.sum(axis=(2, 3)).transpose(0, 1, 3, 4, 2).reshape(N_GROUPS, SSM_STATE, SSM_GROUP)
    x = x.reshape(4, 2, 4, SSM_STATE, 2, 4, SSM_GROUP)
    x = x * e2[None, :, None, None, :, None, None] * e4[None, None, :, None, None, :, None]
    return x.sum(axis=(4, 5)).reshape(N_GROUPS, SSM_STATE, SSM_GROUP)


def _pack(parts):
    flat = jnp.concatenate([p.reshape(-1) for p in parts])
    rows = -(-flat.shape[0] // (8 * PACK_COLS)) * 8
    return jnp.pad(flat, (0, rows * PACK_COLS - flat.shape[0])).reshape(rows, PACK_COLS)


def _unpack(packed, shapes):
    flat = packed.reshape(-1)
    out, pos = [], 0
    for shp in shapes:
        size = math.prod(shp)
        out.append(flat[pos:pos + size].reshape(shp))
        pos += size
    return out


def kernel(x, c, w_ada, b_ada, g_pre_mix, g_post_mix, w_in, ssm_lam_re, ssm_lam_im, ssm_log_step, ssm_b_re, ssm_b_im, ssm_c_re, ssm_c_im, ssm_d, glu_w, glu_b, g_out_ssm, conv_w, g_out_conv, w_out, g_pre_ffn, g_post_ffn, w_up, ffn_conv_w, w_down, loss_target, m_w_ada, m_b_ada, m_g_pre_mix, m_g_post_mix, m_w_in, m_ssm_lam_re, m_ssm_lam_im, m_ssm_log_step, m_ssm_b_re, m_ssm_b_im, m_ssm_c_re, m_ssm_c_im, m_ssm_d, m_glu_w, m_glu_b, m_g_out_ssm, m_conv_w, m_g_out_conv, m_w_out, m_g_pre_ffn, m_g_post_ffn, m_w_up, m_ffn_conv_w, m_w_down, v_w_ada, v_b_ada, v_g_pre_mix, v_g_post_mix, v_w_in, v_ssm_lam_re, v_ssm_lam_im, v_ssm_log_step, v_ssm_b_re, v_ssm_b_im, v_ssm_c_re, v_ssm_c_im, v_ssm_d, v_glu_w, v_glu_b, v_g_out_ssm, v_conv_w, v_g_out_conv, v_w_out, v_g_pre_ffn, v_g_post_ffn, v_w_up, v_ffn_conv_w, v_w_down):
    args = dict(locals())
    wts = {n: args[n] for n in WEIGHTS}
    mom_m = {n: args['m_' + n] for n in WEIGHTS}
    mom_v = {n: args['v_' + n] for n in WEIGHTS}
    T = x.shape[1]
    tm = min(512, T)
    me = _me()[3]
    xt, tgt = x[0], loss_target[0]

    (c_all,) = _exchange([c], name='gather_c', scatter=False)
    c_all = c_all.reshape(N_DEV, D_MODEL)
    b_cols = lax.dynamic_slice(b_ada, (0, me * ADA_SHARD), (1, ADA_SHARD))
    mod_cols, c_act = _mod_cols(c_all, w_ada[0], b_cols)
    (mod_all,) = _exchange([mod_cols], name='gather_mod', scatter=False)
    mod = lax.dynamic_slice(mod_all, (0, me, 0), (N_DEV, 1, ADA_SHARD)).reshape(N_MOD, 1, D_MODEL)
    sh1, sc1, gt1, sh2, sc2, gt2 = [mod[k] for k in range(N_MOD)]

    w_in_s, glu_s, w_out_s, conv_s = _exchange(
        [w_in[0].astype(BF16), glu_w[0].astype(BF16), w_out[0].astype(BF16), conv_w[0]], name='gather_weights',
        scatter=False)
    glu_full = glu_s.reshape(D_SSM, D_SSM)
    w_out_full = w_out_s.reshape(D_MODEL, D_MODEL)
    cw_full = conv_s.transpose(1, 0, 2).reshape(3, D_CONV)

    lre_x, lim_x = _expand(ssm_lam_re[0]), _expand(ssm_lam_im[0])
    lst_x = jnp.broadcast_to(ssm_log_step[0][:, None], (N_GROUPS, SSM_STATE * SSM_GROUP))
    b_re_x = ssm_b_re[0].reshape(N_GROUPS, -1)
    b_im_x = ssm_b_im[0].reshape(N_GROUPS, -1)
    ar_x, ai_x, bbr_x, bbi_x = _ssm_prep(lre_x, lim_x, lst_x, b_re_x, b_im_x)
    lam_r = ar_x[:, ::SSM_GROUP].reshape(1, N_STATE)
    lam_i = ai_x[:, ::SSM_GROUP].reshape(1, N_STATE)
    big_b_re = _block_diag_b(bbr_x.reshape(N_GROUPS, SSM_STATE, SSM_GROUP)).astype(BF16)
    big_b_im = _block_diag_b(bbi_x.reshape(N_GROUPS, SSM_STATE, SSM_GROUP)).astype(BF16)
    big_c_re = _block_diag_c(ssm_c_re[0]).astype(BF16)
    big_c_im = _block_diag_c(ssm_c_im[0]).astype(BF16)
    head = jnp.arange(D_SSM)
    avg16 = jnp.where(head[:, None] // SSM_GROUP == head[None, :] // SSM_GROUP, 1.0 / SSM_GROUP, 0.0).astype(BF16)
    hd = D_CONV // CONV_HEADS
    avg64 = jnp.where(head[:, None] // hd == head[None, :] // hd, 1.0 / hd, 0.0).astype(BF16)

    (proj, h1), (w_down_s, ffn_conv_s) = _pre_mix(xt, sc1, sh1, g_pre_mix, w_in_s, tm,
                                                  ([w_down[0].astype(BF16), ffn_conv_w[0]], False))
    wd4 = w_down_s.reshape(4, FF_SHARD, D_MODEL)
    cw4 = ffn_conv_s.reshape(2, 4, 3, FF_SHARD)
    u_perm = _to_scan_rows(proj[:, :D_SSM]).astype(BF16)
    (s_re, s_im, y_perm), (w_up_s,) = _ssm_fwd(u_perm, big_b_re, big_b_im, big_c_re, big_c_im, lam_r, lam_i,
                                               ([w_up[0].astype(BF16)], False))
    yssm = _from_scan_rows(y_perm)
    mix_args = (ssm_d, glu_full, glu_b, g_out_ssm, cw_full, g_out_conv, avg16, avg64)
    ycat = _mix_fwd(yssm, proj, *mix_args, tm)
    o, x1, h2 = _out_proj(ycat, w_out_full, xt, gt1, g_post_mix, g_pre_ffn, sc2, sh2, tm)
    up8 = _ffn_up(h2, w_up_s, tm)
    up4 = up8.reshape(2, 4, T, FF_SHARD)
    act = _ffn_act(up4, cw4, tm)
    dn, dx2, loss_parts = _ffn_down(act, wd4, x1, tgt, gt2, g_post_ffn, tm)
    loss = lax.psum(jnp.sum(loss_parts[:, 0, 0]), ('x', 'y', 'c'))

    got = {}
    ddn, d_gt2, d_g_post_ffn = _post_norm_bwd(dx2, dn, gt2, g_post_ffn, tm, 'ffn_norm_bwd')
    dhid = _ffn_dact(ddn, wd4, up4, cw4, tm)
    g_w_down = _grad_tn(act, ddn, pl.BlockSpec((None, tm, FF_SHARD), lambda g, k: (g, k, 0)),
                        pl.BlockSpec((tm, D_MODEL), lambda g, k: (k, 0)), 4, FF_SHARD, D_MODEL, tm, 'grad_w_down')
    dup8, dcw_ffn = _ffn_dup(dhid.reshape(N_DEV, T, FF_SHARD), up8, ffn_conv_s, tm)
    g_w_up, (got['w_down'],) = _grad_tn(
        h2, dup8, pl.BlockSpec((tm, D_MODEL), lambda g, k: (k, 0)),
        pl.BlockSpec((None, tm, FF_SHARD), lambda g, k: (g, k, 0)), N_DEV, D_MODEL, FF_SHARD, tm, 'grad_w_up',
        ride=([g_w_down.reshape(N_DEV, D_FF // N_DEV, D_MODEL)], True))
    (dx1, d_sh2, d_sc2, d_g_pre_ffn), (got['w_up'], got['ffn_conv_w']) = _pre_norm_bwd(
        dup8, pl.BlockSpec((None, tm, FF_SHARD), lambda i, j: (j, i, 0)), w_up_s, x1, dx2, sc2, g_pre_ffn, tm,
        'ffn_in_bwd', ([g_w_up, dcw_ffn], True))

    d_o, d_gt1, d_g_post_mix = _post_norm_bwd(dx1, o, gt1, g_post_mix, tm, 'mix_norm_bwd')
    g_w_out = _grad_tn(ycat, d_o, pl.BlockSpec((tm, D_MODEL), lambda g, k: (k, 0)),
                       pl.BlockSpec((tm, D_MODEL), lambda g, k: (k, 0)), 1, D_MODEL, D_MODEL, tm, 'grad_w_out')
    dycat = _d_ycat(d_o, w_out_full, tm)
    dy, dconv, dbg, z_b, dlin_b, sums = _mix_bwd(dycat, yssm, proj, *mix_args, tm)
    g_glu_w = _grad_tn(z_b, dlin_b, pl.BlockSpec((tm, D_SSM), lambda g, k: (k, 0)),
                       pl.BlockSpec((tm, D_SSM), lambda g, k: (k, 0)), 1, D_SSM, D_SSM, tm, 'grad_glu_w')
    dy_perm = _to_scan_rows(dy).astype(BF16)
    (du_perm, dbr_blk, dbi_blk, dcr_blk, dci_blk, dar_blk, dai_blk), (got['w_out'], got['glu_w']) = _ssm_bwd(
        dy_perm, u_perm, s_re, s_im, big_b_re, big_b_im, big_c_re, big_c_im, lam_r, lam_i,
        ([g_w_out.reshape(N_DEV, D_MODEL // N_DEV, D_MODEL), g_glu_w.reshape(N_DEV, D_SSM // N_DEV, D_SSM)], True))
    du_ssm = _from_scan_rows(du_perm)
    dproj = _mix_bwd_proj(dconv, proj, du_ssm, dy, ssm_d, dbg, cw_full, tm)
    g_w_in = _grad_tn(h1, dproj, pl.BlockSpec((tm, D_MODEL), lambda g, k: (k, 0)),
                      pl.BlockSpec((tm, IN_SHARD), lambda g, k: (k, g)), N_DEV, D_MODEL, IN_SHARD, tm, 'grad_w_in')
    g_conv_slots = jnp.concatenate([sums[4:7], jnp.zeros((5, D_CONV), F32)]).reshape(
        8, N_DEV, D_CONV // N_DEV).transpose(1, 0, 2)
    (grad_x, d_sh1, d_sc1, d_g_pre_mix), (got['w_in'], got['conv_w']) = _pre_norm_bwd(
        dproj, pl.BlockSpec((tm, IN_SHARD), lambda i, j: (i, j)), w_in_s, xt, dx1, sc1, g_pre_mix, tm, 'mix_in_bwd',
        ([g_w_in, g_conv_slots], True))

    dbb_re = _diag_blocks(dbr_blk, True).reshape(N_GROUPS, -1)
    dbb_im = _diag_blocks(dbi_blk, True).reshape(N_GROUPS, -1)
    d_c_re = _diag_blocks(dcr_blk, False).transpose(0, 2, 1)
    d_c_im = _diag_blocks(dci_blk, False).transpose(0, 2, 1)
    lane = jnp.arange(SSM_STATE * SSM_GROUP)
    seg = jnp.where(lane[:, None] // SSM_GROUP == lane[None, :] // SSM_GROUP, 1.0, 0.0).astype(BF16)
    d_b_re_x, d_b_im_x, d_lre_x, d_lim_x, d_lst = _ssm_prep_bwd(
        lre_x, lim_x, lst_x, b_re_x, b_im_x, dbb_re, dbb_im, _expand(dar_blk.reshape(N_GROUPS, SSM_STATE)),
        _expand(dai_blk.reshape(N_GROUPS, SSM_STATE)), seg)

    dmod = jnp.concatenate([d_sh1, d_sc1, d_gt1, d_sh2, d_sc2, d_gt2], axis=1)
    small_grads = {
        'b_ada': dmod, 'g_pre_mix': d_g_pre_mix, 'g_post_mix': d_g_post_mix,
        'ssm_lam_re': d_lre_x[:, ::SSM_GROUP], 'ssm_lam_im': d_lim_x[:, ::SSM_GROUP], 'ssm_log_step': d_lst,
        'ssm_b_re': d_b_re_x, 'ssm_b_im': d_b_im_x, 'ssm_c_re': d_c_re, 'ssm_c_im': d_c_im,
        'ssm_d': sums[3:4], 'glu_b': sums[0:1], 'g_out_ssm': sums[1:2], 'g_out_conv': sums[2:3],
        'g_pre_ffn': d_g_pre_ffn, 'g_post_ffn': d_g_post_ffn,
    }
    (small_all,) = _exchange([_pack([small_grads[n] for n in SMALL])], name='gather_small_grads', scatter=False)
    small_out = _adamw(small_all, _pack([wts[n] for n in SMALL]), _pack([mom_m[n] for n in SMALL]),
                       _pack([mom_v[n] for n in SMALL]), 'adamw_small')
    small_shapes = [wts[n].shape for n in SMALL]
    res = {}
    for kind, packed in zip(('g', 'd', 'm', 'v'), small_out):
        for n, val in zip(SMALL, _unpack(packed, small_shapes)):
            res[kind, n] = val

    dmod_all = small_all[:, :N_MOD, :].reshape(N_DEV, N_MOD * D_MODEL)
    dmod_cols = lax.dynamic_slice(dmod_all, (0, me * ADA_SHARD), (N_DEV, ADA_SHARD))
    g_w_ada = _grad_w_ada(c_act.T, dmod_cols)

    for n, slots in got.items():
        if n in ('conv_w', 'ffn_conv_w'):
            slots = slots[:, :3, :]
        outs = _adamw(slots, wts[n][0], mom_m[n][0], mom_v[n][0], 'adamw_' + n)
        for kind, val in zip(('g', 'd', 'm', 'v'), outs):
            res[kind, n] = val[None]
    outs = _adamw(g_w_ada[None], w_ada[0], m_w_ada[0], v_w_ada[0], 'adamw_w_ada')
    for kind, val in zip(('g', 'd', 'm', 'v'), outs):
        res[kind, 'w_ada'] = val[None]

    return (loss, grad_x[None], *[res['g', n] for n in WEIGHTS], *[res['d', n] for n in WEIGHTS],
            *[res['m', n] for n in WEIGHTS], *[res['v', n] for n in WEIGHTS])
```

```python
import math

import jax
import jax.numpy as jnp
from jax import lax
from jax.experimental import pallas as pl
from jax.experimental.pallas import tpu as pltpu

F32, BF16 = jnp.float32, jnp.bfloat16

D_MODEL = 1024
D_SSM = 512
D_CONV = 512
SSM_GROUP = 16
N_GROUPS = 32
SSM_STATE = 64
N_STATE = N_GROUPS * SSM_STATE
CONV_HEADS = 8
D_FF = 2816
N_MOD = 6
D_IN_PROJ = D_SSM + 3 * D_CONV
N_DEV = 8
FF_SHARD = 2 * D_FF // N_DEV
IN_SHARD = D_IN_PROJ // N_DEV
ADA_SHARD = N_MOD * D_MODEL // N_DEV
EPS = 1e-6
LAMBDA_RE_MAX = -1e-4
ADAM_LR, ADAM_B1, ADAM_B2, ADAM_EPS, ADAM_WD, ADAM_STEP = 0.001, 0.9, 0.999, 1e-08, 0.01, 10
GELU_C = math.sqrt(2.0 / math.pi)
GELU_A = 0.044715

SUBLANES = 8
HALO = 8
HALO16 = 16
STATE_BLOCK = 256
CHAN_BLOCK = 128
VMEM_BIG = 48 << 20

WEIGHTS = ['w_ada', 'b_ada', 'g_pre_mix', 'g_post_mix', 'w_in', 'ssm_lam_re', 'ssm_lam_im', 'ssm_log_step',
           'ssm_b_re', 'ssm_b_im', 'ssm_c_re', 'ssm_c_im', 'ssm_d', 'glu_w', 'glu_b', 'g_out_ssm', 'conv_w',
           'g_out_conv', 'w_out', 'g_pre_ffn', 'g_post_ffn', 'w_up', 'ffn_conv_w', 'w_down']
SHARDED = ('w_ada', 'w_in', 'glu_w', 'conv_w', 'w_out', 'w_up', 'ffn_conv_w', 'w_down')
SMALL = tuple(n for n in WEIGHTS if n not in SHARDED)
PACK_COLS = 1024


def _call(body, *, name, grid, in_specs, out_specs, out_shape, scratch=(), sem=None, vmem=None, ride=None):
    params = {}
    if vmem is not None:
        params['vmem_limit_bytes'] = vmem
    if ride is None:
        if sem is not None:
            params['dimension_semantics'] = sem
        return pl.pallas_call(body, name=name, grid=grid, in_specs=in_specs, out_specs=out_specs,
                              out_shape=out_shape, scratch_shapes=list(scratch),
                              compiler_params=pltpu.CompilerParams(**params))
    arrs, scatter = ride
    single = not isinstance(out_shape, (list, tuple))
    out_shape_l = [out_shape] if single else list(out_shape)
    out_specs_l = [out_specs] if single else list(out_specs)
    n, n_in, n_out, n_scr = len(arrs), len(in_specs), len(out_shape_l), len(scratch)
    any_spec = pl.BlockSpec(memory_space=pl.ANY)
    params['dimension_semantics'] = ('arbitrary',) * len(grid)

    def carried(*refs):
        ins, rin = refs[:n_in], refs[n_in:n_in + n]
        outs, rout = refs[n_in + n:n_in + n + n_out], refs[n_in + n + n_out:n_in + 2 * n + n_out]
        scr, sems = refs[n_in + 2 * n + n_out:n_in + 2 * n + n_out + n_scr], refs[n_in + 2 * n + n_out + n_scr:]
        first = pl.program_id(0) == 0
        last = pl.program_id(0) == grid[0] - 1
        for ax in range(1, len(grid)):
            first = jnp.logical_and(first, pl.program_id(ax) == 0)
            last = jnp.logical_and(last, pl.program_id(ax) == grid[ax] - 1)

        @pl.when(first)
        def _():
            _exchange_start(rin, rout, sems, scatter)

        body(*ins, *outs, *scr)

        @pl.when(last)
        def _():
            _exchange_wait(rin, rout, sems, scatter)

    call = pl.pallas_call(carried, name=name, grid=grid, in_specs=list(in_specs) + [any_spec] * n,
                          out_specs=out_specs_l + [any_spec] * n,
                          out_shape=out_shape_l + _exchange_shapes(arrs, scatter),
                          scratch_shapes=list(scratch) + _exchange_sems(n),
                          compiler_params=pltpu.CompilerParams(**params))

    def run(*args):
        res = call(*args, *arrs)
        own = res[0] if single else list(res[:n_out])
        return own, list(res[n_out:])

    return run


def _const(shape):
    nd = len(shape)
    return pl.BlockSpec(shape, lambda *_: (0,) * nd)


def _sds(shape, dtype=F32):
    return jax.ShapeDtypeStruct(shape, dtype)


def _dot(a, b):
    return jnp.dot(a, b, preferred_element_type=F32)


def _dot_nt(a, b):
    return lax.dot_general(a, b, (((1,), (1,)), ((), ())), preferred_element_type=F32)


def _dot_tn(a, b):
    return lax.dot_general(a, b, (((0,), (0,)), ((), ())), preferred_element_type=F32)


def _dot_split(x, mat, parts):
    acc = None
    rem = x
    for _ in range(parts):
        piece = rem.astype(BF16)
        rem = rem - piece.astype(F32)
        term = _dot(piece, mat)
        acc = term if acc is None else acc + term
    return acc


def _sigmoid(x):
    return 1.0 / (1.0 + jnp.exp(-x))


def _gelu(x):
    t = jnp.tanh(GELU_C * (x + GELU_A * x * x * x))
    return 0.5 * x * (1.0 + t), t


def _gelu_grad(x, t):
    return 0.5 * (1.0 + t) + 0.5 * x * (1.0 - t * t) * GELU_C * (1.0 + 3.0 * GELU_A * x * x)


def _rsqrt_mean(x):
    return lax.rsqrt(jnp.mean(x * x, axis=-1, keepdims=True) + EPS)


def _colsum(x):
    return jnp.sum(x, axis=0, keepdims=True)


def _shifts_down(x, halo):
    ext = jnp.concatenate([halo, x], axis=0)
    return pltpu.roll(ext, 1, 0)[halo.shape[0]:], pltpu.roll(ext, 2, 0)[halo.shape[0]:]


def _shifts_up(x, halo):
    n = x.shape[0]
    ext = jnp.concatenate([x, halo], axis=0)
    total = ext.shape[0]
    return pltpu.roll(ext, total - 1, 0)[:n], pltpu.roll(ext, total - 2, 0)[:n]


def _conv3(x, halo, w_ref):
    x1, x2 = _shifts_down(x, halo)
    return w_ref[0:1, :] * x2 + w_ref[1:2, :] * x1 + w_ref[2:3, :] * x, x1, x2


def _conv3_t(g, halo, w_ref):
    g1, g2 = _shifts_up(g, halo)
    return w_ref[2:3, :] * g + w_ref[1:2, :] * g1 + w_ref[0:1, :] * g2, g1, g2


def _silu_parts(x):
    s = _sigmoid(x)
    return x * s, s * (1.0 + x * (1.0 - s))


def _norm_bwd(dn, x, r, g):
    gd = g * dn
    return r * gd - x * (r * r * r) * jnp.mean(gd * x, axis=-1, keepdims=True)


def _head_norm_bwd(dn, y, rs, g, avg):
    gd = g * dn
    return rs * gd - y * (rs * rs * rs) * _dot_split(gd * y, avg, 2)


def _me():
    x, y, c = lax.axis_index('x'), lax.axis_index('y'), lax.axis_index('c')
    return x, y, c, 4 * x + 2 * y + c


def _peer(k):
    x, y, c, _ = _me()
    px = 1 - x if k & 4 else x
    py = 1 - y if k & 2 else y
    pc = 1 - c if k & 1 else c
    return (px, py, pc), 4 * px + 2 * py + pc


def _exchange_copies(ins, outs, sems, scatter):
    send_sems, recv_sems, local_sems = sems
    me = _me()[3]
    local, sends, recvs = [], [], []
    for a in range(len(ins)):
        src = ins[a].at[me] if scatter else ins[a]
        local.append(pltpu.make_async_copy(src, outs[a].at[me], local_sems.at[a]))
        for k in range(1, N_DEV):
            dev, idx = _peer(k)
            src = ins[a].at[idx] if scatter else ins[a]
            for dst, group in ((outs[a].at[me], sends), (outs[a].at[idx], recvs)):
                group.append(pltpu.make_async_remote_copy(
                    src_ref=src, dst_ref=dst, send_sem=send_sems.at[a, k - 1], recv_sem=recv_sems.at[a, k - 1],
                    device_id=dev, device_id_type=pl.DeviceIdType.MESH))
    return local, sends, recvs


def _exchange_start(ins, outs, sems, scatter):
    local, sends, _ = _exchange_copies(ins, outs, sems, scatter)
    for cp in local + sends:
        cp.start()


def _exchange_wait(ins, outs, sems, scatter):
    local, sends, recvs = _exchange_copies(ins, outs, sems, scatter)
    for cp in recvs:
        cp.wait_recv()
    for cp in sends:
        cp.wait_send()
    for cp in local:
        cp.wait()


def _exchange_shapes(arrs, scatter):
    return [_sds(a.shape if scatter else (N_DEV,) + a.shape, a.dtype) for a in arrs]


def _exchange_sems(n):
    return [pltpu.SemaphoreType.DMA((n, N_DEV - 1)), pltpu.SemaphoreType.DMA((n, N_DEV - 1)),
            pltpu.SemaphoreType.DMA((n,))]


def _exchange(arrs, *, name, scatter):
    n = len(arrs)

    def body(*refs):
        _exchange_start(refs[:n], refs[n:2 * n], refs[2 * n:], scatter)
        _exchange_wait(refs[:n], refs[n:2 * n], refs[2 * n:], scatter)

    any_spec = pl.BlockSpec(memory_space=pl.ANY)
    outs = pl.pallas_call(body, name=name, out_shape=_exchange_shapes(arrs, scatter), in_specs=[any_spec] * n,
                          out_specs=[any_spec] * n, scratch_shapes=_exchange_sems(n))(*arrs)
    return list(outs)


def _mod_cols(c_all, w_ada, b_cols):
    def body(c_ref, w_ref, b_ref, mod_ref, act_ref):
        c = c_ref[...]
        act = c * _sigmoid(c)
        act_ref[...] = act
        mod_ref[...] = _dot(act.astype(BF16), w_ref[...].astype(BF16)) + b_ref[...]

    return _call(body, name='mod_cols', grid=(1,),
                 in_specs=[_const(c_all.shape), _const(w_ada.shape), _const(b_cols.shape)],
                 out_specs=[_const((N_DEV, ADA_SHARD)), _const(c_all.shape)],
                 out_shape=[_sds((N_DEV, ADA_SHARD)), _sds(c_all.shape)], vmem=VMEM_BIG)(c_all, w_ada, b_cols)


def _grad_w_ada(act_t, dmod_cols):
    def body(a_ref, d_ref, o_ref):
        o_ref[...] = _dot(a_ref[...], d_ref[...])

    return _call(body, name='grad_w_ada', grid=(1,), in_specs=[_const(act_t.shape), _const(dmod_cols.shape)],
                 out_specs=_const((D_MODEL, ADA_SHARD)), out_shape=_sds((D_MODEL, ADA_SHARD)),
                 vmem=VMEM_BIG)(act_t, dmod_cols)


def _pre_mix(x, sc, sh, g, w_s, tm, ride):
    T = x.shape[0]

    def body(x_ref, sc_ref, sh_ref, g_ref, w_ref, proj_ref, h_ref):
        @pl.when(pl.program_id(1) == 0)
        def _():
            xv = x_ref[...]
            h_ref[...] = ((xv * _rsqrt_mean(xv) * g_ref[...]) * (1.0 + sc_ref[...]) + sh_ref[...]).astype(BF16)

        proj_ref[...] = _dot(h_ref[...], w_ref[...])

    row = pl.BlockSpec((tm, D_MODEL), lambda i, j: (i, 0))
    vec = _const((1, D_MODEL))
    return _call(body, name='pre_mix', grid=(T // tm, N_DEV),
                 in_specs=[row, vec, vec, vec, pl.BlockSpec((None, D_MODEL, IN_SHARD), lambda i, j: (j, 0, 0))],
                 out_specs=[pl.BlockSpec((tm, IN_SHARD), lambda i, j: (i, j)), row],
                 out_shape=[_sds((T, D_IN_PROJ)), _sds((T, D_MODEL), BF16)],
                 sem=('parallel', 'arbitrary'), ride=ride)(x, sc, sh, g, w_s)


def _halo_before(tm, rows=HALO):
    return lambda i: jnp.maximum(i * (tm // rows) - 1, 0)


def _halo_after(tm, T, rows=HALO):
    return lambda i: jnp.minimum((i + 1) * (tm // rows), T // rows - 1)


def _mix_fwd(yssm, proj, d, glu_w, glu_b, g_ssm, cw, g_conv, avg16, avg64, tm):
    T = yssm.shape[0]
    hb = _halo_before(tm)

    def body(y_ref, p_ref, ph_ref, d_ref, gw_ref, gb_ref, gs_ref, cw_ref, gc_ref, a16_ref, a64_ref, o_ref):
        i = pl.program_id(0)
        u = p_ref[:, 0:D_SSM]
        y = y_ref[...] + d_ref[...] * u
        z, _ = _gelu(y)
        gate = _sigmoid(_dot(z.astype(BF16), gw_ref[...]) + gb_ref[...])
        ya = z * gate
        rs = lax.rsqrt(_dot_split(ya * ya, a16_ref[...], 2) + EPS)
        o_ref[:, 0:D_SSM] = (ya * rs * gs_ref[...]).astype(BF16)
        bg = p_ref[:, D_SSM:D_SSM + D_CONV]
        cv = p_ref[:, D_SSM + D_CONV:D_SSM + 2 * D_CONV] * p_ref[:, D_SSM + 2 * D_CONV:D_IN_PROJ]
        hv = ph_ref[:, D_SSM + D_CONV:D_SSM + 2 * D_CONV] * ph_ref[:, D_SSM + 2 * D_CONV:D_IN_PROJ]
        hv = jnp.where(i > 0, hv, 0.0)
        conv, _, _ = _conv3(cv, hv, cw_ref)
        yb = bg * conv
        rsb = lax.rsqrt(_dot_split(yb * yb, a64_ref[...], 2) + EPS)
        o_ref[:, D_SSM:D_MODEL] = (yb * rsb * gc_ref[...]).astype(BF16)

    vec = _const((1, D_SSM))
    sq = _const((D_SSM, D_SSM))
    return _call(body, name='mix_fwd', grid=(T // tm,),
                 in_specs=[pl.BlockSpec((tm, D_SSM), lambda i: (i, 0)), pl.BlockSpec((tm, D_IN_PROJ), lambda i: (i, 0)),
                           pl.BlockSpec((HALO, D_IN_PROJ), lambda i: (hb(i), 0)), vec, sq, vec, vec,
                           _const((3, D_CONV)), vec, sq, sq],
                 out_specs=pl.BlockSpec((tm, D_MODEL), lambda i: (i, 0)), out_shape=_sds((T, D_MODEL), BF16),
                 sem=('parallel',), vmem=VMEM_BIG)(yssm, proj, proj, d, glu_w, glu_b, g_ssm, cw, g_conv, avg16, avg64)


def _out_proj(ycat, w_out, x, gt, g_post, g_pre, sc, sh, tm):
    T = x.shape[0]

    def body(y_ref, w_ref, x_ref, gt_ref, gp_ref, g2_ref, sc_ref, sh_ref, o_ref, x1_ref, h_ref):
        o = _dot(y_ref[...], w_ref[...])
        o_ref[...] = o
        x1 = x_ref[...] + gt_ref[...] * (o * _rsqrt_mean(o) * gp_ref[...])
        x1_ref[...] = x1
        h_ref[...] = ((x1 * _rsqrt_mean(x1) * g2_ref[...]) * (1.0 + sc_ref[...]) + sh_ref[...]).astype(BF16)

    row = pl.BlockSpec((tm, D_MODEL), lambda i: (i, 0))
    vec = _const((1, D_MODEL))
    return _call(body, name='out_proj', grid=(T // tm,),
                 in_specs=[row, _const((D_MODEL, D_MODEL)), row, vec, vec, vec, vec, vec],
                 out_specs=[row, row, row],
                 out_shape=[_sds((T, D_MODEL)), _sds((T, D_MODEL)), _sds((T, D_MODEL), BF16)],
                 sem=('parallel',), vmem=VMEM_BIG)(ycat, w_out, x, gt, g_post, g_pre, sc, sh)


def _ffn_up(h2, w_s, tm):
    T = h2.shape[0]

    def body(h_ref, w_ref, o_ref):
        o_ref[...] = _dot(h_ref[...], w_ref[...]).astype(BF16)

    return _call(body, name='ffn_up', grid=(T // tm, N_DEV),
                 in_specs=[pl.BlockSpec((tm, D_MODEL), lambda i, j: (i, 0)),
                           pl.BlockSpec((None, D_MODEL, FF_SHARD), lambda i, j: (j, 0, 0))],
                 out_specs=pl.BlockSpec((None, tm, FF_SHARD), lambda i, j: (j, i, 0)),
                 out_shape=_sds((N_DEV, T, FF_SHARD), BF16), sem=('parallel', 'parallel'))(h2, w_s)


def _ffn_hidden(up_ref, halo_ref, cw_ref, i):
    hid = []
    for part in range(2):
        halo = jnp.where(i > 0, halo_ref[part].astype(F32), 0.0)
        hid.append(_conv3(up_ref[part].astype(F32), halo, cw_ref.at[part])[0])
    return hid


def _ffn_act(up4, cw4, tm):
    T = up4.shape[2]
    hb = _halo_before(tm, HALO16)

    def body(up_ref, halo_ref, cw_ref, o_ref):
        hid_a, hid_v = _ffn_hidden(up_ref, halo_ref, cw_ref, pl.program_id(0))
        o_ref[...] = (_silu_parts(hid_a)[0] * hid_v).astype(BF16)

    return _call(body, name='ffn_act', grid=(T // tm, 4),
                 in_specs=[pl.BlockSpec((2, None, tm, FF_SHARD), lambda i, j: (0, j, i, 0)),
                           pl.BlockSpec((2, None, HALO16, FF_SHARD), lambda i, j: (0, j, hb(i), 0)),
                           pl.BlockSpec((2, None, 3, FF_SHARD), lambda i, j: (0, j, 0, 0))],
                 out_specs=pl.BlockSpec((None, tm, FF_SHARD), lambda i, j: (j, i, 0)),
                 out_shape=_sds((4, T, FF_SHARD), BF16), sem=('parallel', 'parallel'))(up4, up4, cw4)


def _ffn_down(act, wd4, x1, tgt, gt, g_post, tm):
    T = x1.shape[0]
    nb = T // tm

    def body(a_ref, w_ref, x1_ref, t_ref, gt_ref, g_ref, dn_ref, dx_ref, loss_ref):
        j = pl.program_id(1)
        part = _dot(a_ref[...], w_ref[...])

        @pl.when(j == 0)
        def _():
            dn_ref[...] = part

        @pl.when(j > 0)
        def _():
            dn_ref[...] += part

        @pl.when(j == 3)
        def _():
            dn = dn_ref[...]
            x2 = x1_ref[...] + gt_ref[...] * (dn * _rsqrt_mean(dn) * g_ref[...])
            err = x2 - t_ref[...]
            dx_ref[...] = err * (1.0 / D_MODEL)
            tot = jnp.sum(jnp.sum(err * err, axis=1, keepdims=True), axis=0, keepdims=True) * (0.5 / D_MODEL)
            loss_ref[...] = jnp.broadcast_to(tot, (8, 128))

    row = pl.BlockSpec((tm, D_MODEL), lambda i, j: (i, 0))
    vec = _const((1, D_MODEL))
    return _call(body, name='ffn_down', grid=(nb, 4),
                 in_specs=[pl.BlockSpec((None, tm, FF_SHARD), lambda i, j: (j, i, 0)),
                           pl.BlockSpec((None, FF_SHARD, D_MODEL), lambda i, j: (j, 0, 0)), row, row, vec, vec],
                 out_specs=[row, row, pl.BlockSpec((None, 8, 128), lambda i, j: (i, 0, 0))],
                 out_shape=[_sds((T, D_MODEL)), _sds((T, D_MODEL)), _sds((nb, 8, 128))],
                 sem=('parallel', 'arbitrary'), vmem=VMEM_BIG)(act, wd4, x1, tgt, gt, g_post)


def _ssm_prep(lre, lim, lst, b_re, b_im):
    def body(lre_ref, lim_ref, lst_ref, br_ref, bi_ref, ar_ref, ai_ref, bbr_ref, bbi_ref):
        ar, ai, qr, qi = _zoh(lre_ref[...], lim_ref[...], lst_ref[...])[:4]
        ar_ref[...] = ar
        ai_ref[...] = ai
        bbr_ref[...] = qr * br_ref[...] - qi * bi_ref[...]
        bbi_ref[...] = qr * bi_ref[...] + qi * br_ref[...]

    shp = lre.shape
    return _call(body, name='ssm_prep', grid=(1,), in_specs=[_const(shp)] * 5, out_specs=[_const(shp)] * 4,
                 out_shape=[_sds(shp)] * 4)(lre, lim, lst, b_re, b_im)


def _zoh(lre, lim, lst):
    lr = jnp.minimum(lre, LAMBDA_RE_MAX)
    st = jnp.exp(lst)
    mag = jnp.exp(lr * st)
    ar = mag * jnp.cos(lim * st)
    ai = mag * jnp.sin(lim * st)
    den = lr * lr + lim * lim
    qr = ((ar - 1.0) * lr + ai * lim) / den
    qi = (ai * lr - (ar - 1.0) * lim) / den
    return ar, ai, qr, qi, lr, st, den


def _ssm_prep_bwd(lre, lim, lst, b_re, b_im, dbbr, dbbi, dar, dai, seg):
    def body(lre_ref, lim_ref, lst_ref, br_ref, bi_ref, dbbr_ref, dbbi_ref, dar_ref, dai_ref, seg_ref,
             dbr_ref, dbi_ref, dlre_ref, dlim_ref, dlst_ref):
        lre_v = lre_ref[...]
        li = lim_ref[...]
        ar, ai, qr, qi, lr, st, den = _zoh(lre_v, li, lst_ref[...])
        br, bi, gbr, gbi = br_ref[...], bi_ref[...], dbbr_ref[...], dbbi_ref[...]
        dbr_ref[...] = qr * gbr + qi * gbi
        dbi_ref[...] = qr * gbi - qi * gbr
        gqr = _dot_split(br * gbr + bi * gbi, seg_ref[...], 3)
        gqi = _dot_split(br * gbi - bi * gbr, seg_ref[...], 3)
        ir, ii = lr / den, -li / den
        gar = dar_ref[...] + ir * gqr + ii * gqi
        gai = dai_ref[...] + ir * gqi - ii * gqr
        tr, ti = qr * ir - qi * ii, qr * ii + qi * ir
        glr = -(tr * gqr + ti * gqi)
        gli = -(tr * gqi - ti * gqr)
        gzr = ar * gar + ai * gai
        gzi = ar * gai - ai * gar
        glr = glr + st * gzr
        gli = gli + st * gzi
        gst = (lr * gzr + li * gzi) * st
        dlre_ref[...] = jnp.where(lre_v < LAMBDA_RE_MAX, glr, 0.0)
        dlim_ref[...] = gli
        dlst_ref[...] = jnp.sum(gst, axis=1, keepdims=True) * (1.0 / SSM_GROUP)

    shp = lre.shape
    return _call(body, name='ssm_prep_bwd', grid=(1,), in_specs=[_const(shp)] * 9 + [_const(seg.shape)],
                 out_specs=[_const(shp)] * 4 + [_const((N_GROUPS, 1))],
                 out_shape=[_sds(shp)] * 4 + [_sds((N_GROUPS, 1))], vmem=VMEM_BIG)(
                     lre, lim, lst, b_re, b_im, dbbr, dbbi, dar, dai, seg)


def _scan_specs(T):
    half = lambda cb: cb // 2
    return dict(
        chan=pl.BlockSpec((T, CHAN_BLOCK), lambda cb: (0, half(cb))),
        state=pl.BlockSpec((T, STATE_BLOCK), lambda cb: (0, cb)),
        b=pl.BlockSpec((CHAN_BLOCK, STATE_BLOCK), lambda cb: (half(cb), cb)),
        c=pl.BlockSpec((STATE_BLOCK, CHAN_BLOCK), lambda cb: (cb, half(cb))),
        lam=pl.BlockSpec((1, STATE_BLOCK), lambda cb: (0, cb)),
    )


def _rows8(i):
    return pl.ds(pl.multiple_of(i * SUBLANES, SUBLANES), SUBLANES)


def _ssm_fwd(u_perm, b_re, b_im, c_re, c_im, lam_r, lam_i, ride):
    T = u_perm.shape[0]
    ls = T // SUBLANES
    rc = min(512, T)
    sp = _scan_specs(T)

    def body(u_ref, bre_ref, bim_ref, cre_ref, cim_ref, lr_ref, li_ref, sre_ref, sim_ref, y_ref):
        cb = pl.program_id(0)
        for c in range(T // rc):
            rows = pl.ds(c * rc, rc)
            sre_ref[rows, :] = _dot(u_ref[rows, :], bre_ref[...])
            sim_ref[rows, :] = _dot(u_ref[rows, :], bim_ref[...])
        shp = (SUBLANES, STATE_BLOCK)
        lr = jnp.broadcast_to(lr_ref[...], shp)
        li = jnp.broadcast_to(li_ref[...], shp)
        zero = jnp.zeros(shp, F32)

        def step(i, carry):
            sr, si, wr, wi = carry
            rows = _rows8(i)
            nr = lr * sr - li * si + sre_ref[rows, :]
            ni = lr * si + li * sr + sim_ref[rows, :]
            sre_ref[rows, :] = nr
            sim_ref[rows, :] = ni
            return nr, ni, lr * wr - li * wi, lr * wi + li * wr

        fr, fi, pr, pi_ = lax.fori_loop(0, ls, step, (zero, zero, jnp.ones(shp, F32), zero))
        row = lax.broadcasted_iota(jnp.int32, shp, 0)
        ir, ii = zero, zero
        for _ in range(SUBLANES - 1):
            er = fr + pr * ir - pi_ * ii
            ei = fi + pr * ii + pi_ * ir
            ir = jnp.where(row == 0, 0.0, pltpu.roll(er, 1, 0))
            ii = jnp.where(row == 0, 0.0, pltpu.roll(ei, 1, 0))

        def fix(i, carry):
            wr, wi = carry
            rows = _rows8(i)
            sre_ref[rows, :] += wr * ir - wi * ii
            sim_ref[rows, :] += wr * ii + wi * ir
            return lr * wr - li * wi, lr * wi + li * wr

        lax.fori_loop(0, ls, fix, (lr, li))
        for c in range(T // rc):
            rows = pl.ds(c * rc, rc)
            yc = _dot(sre_ref[rows, :].astype(BF16), cre_ref[...]) - _dot(sim_ref[rows, :].astype(BF16), cim_ref[...])

            @pl.when(cb % 2 == 0)
            def _():
                y_ref[rows, :] = yc

            @pl.when(cb % 2 == 1)
            def _():
                y_ref[rows, :] += yc

    return _call(body, name='ssm_fwd', grid=(N_STATE // STATE_BLOCK,),
                 in_specs=[sp['chan'], sp['b'], sp['b'], sp['c'], sp['c'], sp['lam'], sp['lam']],
                 out_specs=[sp['state'], sp['state'], sp['chan']],
                 out_shape=[_sds((T, N_STATE)), _sds((T, N_STATE)), _sds((T, D_SSM))],
                 sem=('arbitrary',), vmem=VMEM_BIG, ride=ride)(u_perm, b_re, b_im, c_re, c_im, lam_r, lam_i)


def _ssm_bwd(dy_perm, u_perm, s_re, s_im, b_re, b_im, c_re, c_im, lam_r, lam_i, ride):
    T = u_perm.shape[0]
    ls = T // SUBLANES
    rc = min(512, T)
    sp = _scan_specs(T)
    ncb = N_STATE // STATE_BLOCK

    def body(dy_ref, u_ref, sre_ref, sim_ref, bre_ref, bim_ref, cre_ref, cim_ref, lr_ref, li_ref,
             du_ref, dbr_ref, dbi_ref, dcr_ref, dci_ref, dar_ref, dai_ref, gre_ref, gim_ref):
        cb = pl.program_id(0)
        for c in range(T // rc):
            rows = pl.ds(c * rc, rc)
            gre_ref[rows, :] = _dot_nt(dy_ref[rows, :], cre_ref[...])
            gim_ref[rows, :] = -_dot_nt(dy_ref[rows, :], cim_ref[...])
        shp = (SUBLANES, STATE_BLOCK)
        lr = jnp.broadcast_to(lr_ref[...], shp)
        li = jnp.broadcast_to(li_ref[...], shp)
        zero = jnp.zeros(shp, F32)

        def step(k, carry):
            gr, gi, wr, wi = carry
            rows = _rows8(ls - 1 - k)
            nr = lr * gr + li * gi + gre_ref[rows, :]
            ni = lr * gi - li * gr + gim_ref[rows, :]
            gre_ref[rows, :] = nr
            gim_ref[rows, :] = ni
            return nr, ni, lr * wr + li * wi, lr * wi - li * wr

        fr, fi, pr, pi_ = lax.fori_loop(0, ls, step, (zero, zero, jnp.ones(shp, F32), zero))
        row = lax.broadcasted_iota(jnp.int32, shp, 0)
        cr, ci = zero, zero
        for _ in range(SUBLANES - 1):
            er = fr + pr * cr - pi_ * ci
            ei = fi + pr * ci + pi_ * cr
            cr = jnp.where(row == SUBLANES - 1, 0.0, pltpu.roll(er, SUBLANES - 1, 0))
            ci = jnp.where(row == SUBLANES - 1, 0.0, pltpu.roll(ei, SUBLANES - 1, 0))

        def fix(k, carry):
            wr, wi, ar, ai = carry
            rows = _rows8(ls - 1 - k)
            gr = gre_ref[rows, :] + wr * cr - wi * ci
            gi = gim_ref[rows, :] + wr * ci + wi * cr
            gre_ref[rows, :] = gr
            gim_ref[rows, :] = gi
            prev = _rows8(ls - 2 - k)
            spr, spi = sre_ref[prev, :], sim_ref[prev, :]
            return (lr * wr + li * wi, lr * wi - li * wr, ar + gr * spr + gi * spi, ai + gi * spr - gr * spi)

        wr, wi, ar, ai = lax.fori_loop(0, ls - 1, fix, (lr, -li, zero, zero))
        first = pl.ds(0, SUBLANES)
        last = pl.ds((ls - 1) * SUBLANES, SUBLANES)
        gr = gre_ref[first, :] + wr * cr - wi * ci
        gi = gim_ref[first, :] + wr * ci + wi * cr
        gre_ref[first, :] = gr
        gim_ref[first, :] = gi
        spr = jnp.where(row == 0, 0.0, pltpu.roll(sre_ref[last, :], 1, 0))
        spi = jnp.where(row == 0, 0.0, pltpu.roll(sim_ref[last, :], 1, 0))
        dar_ref[...] = _colsum(ar + gr * spr + gi * spi)
        dai_ref[...] = _colsum(ai + gi * spr - gr * spi)

        for c in range(T // rc):
            rows = pl.ds(c * rc, rc)
            g_r, g_i = gre_ref[rows, :].astype(BF16), gim_ref[rows, :].astype(BF16)
            s_r, s_i = sre_ref[rows, :].astype(BF16), sim_ref[rows, :].astype(BF16)
            ub, dyb = u_ref[rows, :], dy_ref[rows, :]
            duc = _dot_nt(g_r, bre_ref[...]) + _dot_nt(g_i, bim_ref[...])
            parts = (_dot_tn(ub, g_r), _dot_tn(ub, g_i), _dot_tn(s_r, dyb), -_dot_tn(s_i, dyb))
            outs = (dbr_ref, dbi_ref, dcr_ref, dci_ref)
            for o_ref, part in zip(outs, parts):
                if c == 0:
                    o_ref[...] = part
                else:
                    o_ref[...] += part

            @pl.when(cb % 2 == 0)
            def _():
                du_ref[rows, :] = duc

            @pl.when(cb % 2 == 1)
            def _():
                du_ref[rows, :] += duc

    blk = lambda r, c: pl.BlockSpec((None, r, c), lambda cb: (cb, 0, 0))
    return _call(body, name='ssm_bwd', grid=(ncb,),
                 in_specs=[sp['chan'], sp['chan'], sp['state'], sp['state'], sp['b'], sp['b'], sp['c'], sp['c'],
                           sp['lam'], sp['lam']],
                 out_specs=[sp['chan'], blk(CHAN_BLOCK, STATE_BLOCK), blk(CHAN_BLOCK, STATE_BLOCK),
                            blk(STATE_BLOCK, CHAN_BLOCK), blk(STATE_BLOCK, CHAN_BLOCK), blk(1, STATE_BLOCK),
                            blk(1, STATE_BLOCK)],
                 out_shape=[_sds((T, D_SSM)), _sds((ncb, CHAN_BLOCK, STATE_BLOCK)), _sds((ncb, CHAN_BLOCK, STATE_BLOCK)),
                            _sds((ncb, STATE_BLOCK, CHAN_BLOCK)), _sds((ncb, STATE_BLOCK, CHAN_BLOCK)),
                            _sds((ncb, 1, STATE_BLOCK)), _sds((ncb, 1, STATE_BLOCK))],
                 scratch=[pltpu.VMEM((T, STATE_BLOCK), F32), pltpu.VMEM((T, STATE_BLOCK), F32)],
                 sem=('arbitrary',), vmem=VMEM_BIG, ride=ride)(dy_perm, u_perm, s_re, s_im, b_re, b_im, c_re, c_im,
                                                               lam_r, lam_i)


def _post_norm_bwd(dx, val, gate, g, tm, name):
    T = dx.shape[0]

    def body(dx_ref, v_ref, gt_ref, g_ref, dv_ref, dgt_ref, dg_ref):
        @pl.when(pl.program_id(0) == 0)
        def _():
            dgt_ref[...] = jnp.zeros_like(dgt_ref)
            dg_ref[...] = jnp.zeros_like(dg_ref)

        dxv, v, gv = dx_ref[...], v_ref[...], g_ref[...]
        r = _rsqrt_mean(v)
        dgt_ref[...] += _colsum(dxv * (v * r * gv))
        dn = dxv * gt_ref[...]
        dg_ref[...] += _colsum(dn * v * r)
        dv_ref[...] = _norm_bwd(dn, v, r, gv).astype(BF16)

    row = pl.BlockSpec((tm, D_MODEL), lambda i: (i, 0))
    vec = _const((1, D_MODEL))
    return _call(body, name=name, grid=(T // tm,), in_specs=[row, row, vec, vec], out_specs=[row, vec, vec],
                 out_shape=[_sds((T, D_MODEL), BF16), _sds((1, D_MODEL)), _sds((1, D_MODEL))],
                 sem=('arbitrary',))(dx, val, gate, g)


def _ffn_dact(ddn, wd4, up4, cw4, tm):
    T = ddn.shape[0]
    hb = _halo_before(tm, HALO16)

    def body(d_ref, w_ref, up_ref, halo_ref, cw_ref, o_ref):
        dact = _dot_nt(d_ref[...], w_ref[...])
        hid_a, hid_v = _ffn_hidden(up_ref, halo_ref, cw_ref, pl.program_id(0))
        silu, dsilu = _silu_parts(hid_a)
        o_ref[0] = (dact * hid_v * dsilu).astype(BF16)
        o_ref[1] = (dact * silu).astype(BF16)

    return _call(body, name='ffn_dact', grid=(T // tm, 4),
                 in_specs=[pl.BlockSpec((tm, D_MODEL), lambda i, j: (i, 0)),
                           pl.BlockSpec((None, FF_SHARD, D_MODEL), lambda i, j: (j, 0, 0)),
                           pl.BlockSpec((2, None, tm, FF_SHARD), lambda i, j: (0, j, i, 0)),
                           pl.BlockSpec((2, None, HALO16, FF_SHARD), lambda i, j: (0, j, hb(i), 0)),
                           pl.BlockSpec((2, None, 3, FF_SHARD), lambda i, j: (0, j, 0, 0))],
                 out_specs=pl.BlockSpec((2, None, tm, FF_SHARD), lambda i, j: (0, j, i, 0)),
                 out_shape=_sds((2, 4, T, FF_SHARD), BF16), sem=('parallel', 'parallel'))(ddn, wd4, up4, up4, cw4)


def _ffn_dup(dhid8, up8, cw8, tm):
    T = up8.shape[1]
    nb = T // tm
    ha = _halo_after(tm, T, HALO16)

    def body(dh_ref, dha_ref, up_ref, cw_ref, dup_ref, dcw_ref):
        i = pl.program_id(1)

        @pl.when(i == 0)
        def _():
            dcw_ref[...] = jnp.zeros_like(dcw_ref)

        dh = dh_ref[...].astype(F32)
        dup, dh1, dh2 = _conv3_t(dh, jnp.where(i < nb - 1, dha_ref[...].astype(F32), 0.0), cw_ref)
        dup_ref[...] = dup.astype(BF16)
        up = up_ref[...].astype(F32)
        dcw_ref[0:1, :] += _colsum(dh2 * up)
        dcw_ref[1:2, :] += _colsum(dh1 * up)
        dcw_ref[2:3, :] += _colsum(dh * up)

    main = pl.BlockSpec((None, tm, FF_SHARD), lambda j, i: (j, i, 0))
    return _call(body, name='ffn_dup', grid=(N_DEV, nb),
                 in_specs=[main, pl.BlockSpec((None, HALO16, FF_SHARD), lambda j, i: (j, ha(i), 0)), main,
                           pl.BlockSpec((None, 3, FF_SHARD), lambda j, i: (j, 0, 0))],
                 out_specs=[main, pl.BlockSpec((None, 8, FF_SHARD), lambda j, i: (j, 0, 0))],
                 out_shape=[_sds((N_DEV, T, FF_SHARD), BF16), _sds((N_DEV, 8, FF_SHARD))],
                 sem=('parallel', 'arbitrary'))(dhid8, dhid8, up8, cw8)


def _grad_tn(a, b, a_spec, b_spec, groups, m, n, tk, name, ride=None):
    T = a.shape[-2]
    nk = T // tk

    def body(a_ref, b_ref, o_ref, acc_ref):
        k = pl.program_id(1)
        part = _dot_tn(a_ref[...], b_ref[...])

        @pl.when(k == 0)
        def _():
            acc_ref[...] = part

        @pl.when(k > 0)
        def _():
            acc_ref[...] += part

        @pl.when(k == nk - 1)
        def _():
            o_ref[...] = acc_ref[...].astype(BF16)

    return _call(body, name=name, grid=(groups, nk), in_specs=[a_spec, b_spec],
                 out_specs=pl.BlockSpec((None, m, n), lambda g, k: (g, 0, 0)), out_shape=_sds((groups, m, n), BF16),
                 scratch=[pltpu.VMEM((m, n), F32)], sem=('parallel', 'arbitrary'), vmem=VMEM_BIG, ride=ride)(a, b)


def _pre_norm_bwd(dz, dz_spec, w_s, xin, dres, sc, g, tm, name, ride):
    T = xin.shape[0]
    n = w_s.shape[2]

    def body(dz_ref, w_ref, x_ref, dr_ref, sc_ref, g_ref, dx_ref, dsh_ref, dsc_ref, dg_ref):
        i, j = pl.program_id(0), pl.program_id(1)
        part = _dot_nt(dz_ref[...], w_ref[...])

        @pl.when(jnp.logical_and(i == 0, j == 0))
        def _():
            dsh_ref[...] = jnp.zeros_like(dsh_ref)
            dsc_ref[...] = jnp.zeros_like(dsc_ref)
            dg_ref[...] = jnp.zeros_like(dg_ref)

        @pl.when(j == 0)
        def _():
            dx_ref[...] = part

        @pl.when(j > 0)
        def _():
            dx_ref[...] += part

        @pl.when(j == N_DEV - 1)
        def _():
            dh, xv, gv = dx_ref[...], x_ref[...], g_ref[...]
            r = _rsqrt_mean(xv)
            dsh_ref[...] += _colsum(dh)
            dsc_ref[...] += _colsum(dh * (xv * r * gv))
            dxn = dh * (1.0 + sc_ref[...])
            dg_ref[...] += _colsum(dxn * xv * r)
            dx_ref[...] = dr_ref[...] + _norm_bwd(dxn, xv, r, gv)

    row = pl.BlockSpec((tm, D_MODEL), lambda i, j: (i, 0))
    vec = _const((1, D_MODEL))
    return _call(body, name=name, grid=(T // tm, N_DEV),
                 in_specs=[dz_spec, pl.BlockSpec((None, D_MODEL, n), lambda i, j: (j, 0, 0)), row, row, vec, vec],
                 out_specs=[row, vec, vec, vec],
                 out_shape=[_sds((T, D_MODEL)), _sds((1, D_MODEL)), _sds((1, D_MODEL)), _sds((1, D_MODEL))],
                 sem=('arbitrary', 'arbitrary'), vmem=VMEM_BIG, ride=ride)(dz, w_s, xin, dres, sc, g)


def _d_ycat(d_o, w_out, tm):
    T = d_o.shape[0]

    def body(d_ref, w_ref, o_ref):
        o_ref[...] = _dot_nt(d_ref[...], w_ref[...])

    row = pl.BlockSpec((tm, D_MODEL), lambda i: (i, 0))
    return _call(body, name='d_ycat', grid=(T // tm,), in_specs=[row, _const((D_MODEL, D_MODEL))], out_specs=row,
                 out_shape=_sds((T, D_MODEL)), sem=('parallel',))(d_o, w_out)


def _mix_bwd(dycat, yssm, proj, d, glu_w, glu_b, g_ssm, cw, g_conv, avg16, avg64, tm):
    T = yssm.shape[0]
    hb = _halo_before(tm)

    def body(dyc_ref, y_ref, p_ref, ph_ref, d_ref, gw_ref, gb_ref, gs_ref, cw_ref, gc_ref, a16_ref, a64_ref,
             dy_ref, dconv_ref, dbg_ref, z_ref, dlin_ref, acc_ref):
        i = pl.program_id(0)

        @pl.when(i == 0)
        def _():
            acc_ref[...] = jnp.zeros_like(acc_ref)

        u = p_ref[:, 0:D_SSM]
        y = y_ref[...] + d_ref[...] * u
        z, t = _gelu(y)
        gate = _sigmoid(_dot(z.astype(BF16), gw_ref[...]) + gb_ref[...])
        ya = z * gate
        rs = lax.rsqrt(_dot_split(ya * ya, a16_ref[...], 2) + EPS)
        dna = dyc_ref[:, 0:D_SSM]
        acc_ref[1:2, :] += _colsum(dna * ya * rs)
        dya = _head_norm_bwd(dna, ya, rs, gs_ref[...], a16_ref[...])
        dlin = dya * z * gate * (1.0 - gate)
        acc_ref[0:1, :] += _colsum(dlin)
        dlin_b = dlin.astype(BF16)
        dz = dya * gate + _dot_nt(dlin_b, gw_ref[...])
        dy = dz * _gelu_grad(y, t)
        acc_ref[3:4, :] += _colsum(dy * u)
        dy_ref[...] = dy
        z_ref[...] = z.astype(BF16)
        dlin_ref[...] = dlin_b

        bg = p_ref[:, D_SSM:D_SSM + D_CONV]
        cv = p_ref[:, D_SSM + D_CONV:D_SSM + 2 * D_CONV] * p_ref[:, D_SSM + 2 * D_CONV:D_IN_PROJ]
        hv = ph_ref[:, D_SSM + D_CONV:D_SSM + 2 * D_CONV] * ph_ref[:, D_SSM + 2 * D_CONV:D_IN_PROJ]
        hv = jnp.where(i > 0, hv, 0.0)
        conv, cv1, cv2 = _conv3(cv, hv, cw_ref)
        yb = bg * conv
        rsb = lax.rsqrt(_dot_split(yb * yb, a64_ref[...], 2) + EPS)
        dnb = dyc_ref[:, D_SSM:D_MODEL]
        acc_ref[2:3, :] += _colsum(dnb * yb * rsb)
        dyb = _head_norm_bwd(dnb, yb, rsb, gc_ref[...], a64_ref[...])
        dbg_ref[...] = dyb * conv
        dconv = dyb * bg
        dconv_ref[...] = dconv
        acc_ref[4:5, :] += _colsum(dconv * cv2)
        acc_ref[5:6, :] += _colsum(dconv * cv1)
        acc_ref[6:7, :] += _colsum(dconv * cv)

    vec = _const((1, D_SSM))
    sq = _const((D_SSM, D_SSM))
    half = pl.BlockSpec((tm, D_SSM), lambda i: (i, 0))
    return _call(body, name='mix_bwd', grid=(T // tm,),
                 in_specs=[pl.BlockSpec((tm, D_MODEL), lambda i: (i, 0)), half,
                           pl.BlockSpec((tm, D_IN_PROJ), lambda i: (i, 0)),
                           pl.BlockSpec((HALO, D_IN_PROJ), lambda i: (hb(i), 0)), vec, sq, vec, vec,
                           _const((3, D_CONV)), vec, sq, sq],
                 out_specs=[half, half, half, half, half, _const((8, D_SSM))],
                 out_shape=[_sds((T, D_SSM)), _sds((T, D_SSM)), _sds((T, D_SSM)), _sds((T, D_SSM), BF16),
                            _sds((T, D_SSM), BF16), _sds((8, D_SSM))],
                 sem=('arbitrary',), vmem=VMEM_BIG)(dycat, yssm, proj, proj, d, glu_w, glu_b, g_ssm, cw, g_conv,
                                                   avg16, avg64)


def _mix_bwd_proj(dconv, proj, du_ssm, dy, d, dbg, cw, tm):
    T = dy.shape[0]
    nb = T // tm
    ha = _halo_after(tm, T)

    def body(dc_ref, dch_ref, cg_ref, v_ref, du_ref, dy_ref, d_ref, dbg_ref, cw_ref, o_ref):
        i = pl.program_id(0)
        dcv = _conv3_t(dc_ref[...], jnp.where(i < nb - 1, dch_ref[...], 0.0), cw_ref)[0]
        o_ref[:, 0:D_SSM] = (du_ref[...] + dy_ref[...] * d_ref[...]).astype(BF16)
        o_ref[:, D_SSM:D_SSM + D_CONV] = dbg_ref[...].astype(BF16)
        o_ref[:, D_SSM + D_CONV:D_SSM + 2 * D_CONV] = (dcv * v_ref[...]).astype(BF16)
        o_ref[:, D_SSM + 2 * D_CONV:D_IN_PROJ] = (dcv * cg_ref[...]).astype(BF16)

    half = pl.BlockSpec((tm, D_SSM), lambda i: (i, 0))
    return _call(body, name='mix_bwd_proj', grid=(nb,),
                 in_specs=[half, pl.BlockSpec((HALO, D_CONV), lambda i: (ha(i), 0)),
                           pl.BlockSpec((tm, D_CONV), lambda i: (i, 2)), pl.BlockSpec((tm, D_CONV), lambda i: (i, 3)),
                           half, half, _const((1, D_SSM)), half, _const((3, D_CONV))],
                 out_specs=pl.BlockSpec((tm, D_IN_PROJ), lambda i: (i, 0)), out_shape=_sds((T, D_IN_PROJ), BF16),
                 sem=('parallel',))(dconv, dconv, proj, proj, du_ssm, dy, d, dbg, cw)


def _row_tile(rows, cols, slots):
    for cand in (512, 256, 128, 64, 32, 16, 8):
        if rows % cand == 0 and slots * cand * cols * 4 <= (2 << 20):
            return cand
    return rows


def _adamw(gslots, w, m, v, name):
    slots, rows, cols = gslots.shape
    tr = _row_tile(rows, cols, slots)

    def body(g_ref, w_ref, m_ref, v_ref, go_ref, d_ref, mo_ref, vo_ref):
        g = g_ref[0].astype(F32)
        for s in range(1, slots):
            g = g + g_ref[s].astype(F32)
        m2 = ADAM_B1 * m_ref[...] + (1.0 - ADAM_B1) * g
        v2 = ADAM_B2 * v_ref[...] + (1.0 - ADAM_B2) * (g * g)
        m_hat = m2 / (1.0 - ADAM_B1 ** ADAM_STEP)
        v_hat = v2 / (1.0 - ADAM_B2 ** ADAM_STEP)
        go_ref[...] = g
        d_ref[...] = -ADAM_LR * (m_hat / (jnp.sqrt(v_hat) + ADAM_EPS) + ADAM_WD * w_ref[...])
        mo_ref[...] = m2
        vo_ref[...] = v2

    blk = pl.BlockSpec((tr, cols), lambda i: (i, 0))
    return _call(body, name=name, grid=(rows // tr,),
                 in_specs=[pl.BlockSpec((slots, tr, cols), lambda i: (0, i, 0)), blk, blk, blk],
                 out_specs=[blk] * 4, out_shape=[_sds((rows, cols))] * 4, sem=('parallel',))(gslots, w, m, v)


def _to_scan_rows(a):
    T, n = a.shape
    return a.reshape(SUBLANES, T // SUBLANES, n).transpose(1, 0, 2).reshape(T, n)


def _from_scan_rows(a):
    T, n = a.shape
    return a.reshape(T // SUBLANES, SUBLANES, n).transpose(1, 0, 2).reshape(T, n)


def _expand(a):
    return jnp.repeat(a, SSM_GROUP, axis=1)


def _block_diag_b(bb):
    eye = jnp.eye(N_GROUPS, dtype=bb.dtype)
    return (bb.transpose(0, 2, 1)[:, :, None, :] * eye[:, None, :, None]).reshape(D_SSM, N_STATE)


def _block_diag_c(cc):
    eye = jnp.eye(N_GROUPS, dtype=cc.dtype)
    return (cc.transpose(0, 2, 1)[:, :, None, :] * eye[:, None, :, None]).reshape(N_STATE, D_SSM)


def _diag_blocks(x, chan_major):
    e2 = jnp.eye(2, dtype=x.dtype)
    e4 = jnp.eye(4, dtype=x.dtype)
    if chan_major:
        x = x.reshape(4, 2, 2, 4, SSM_GROUP, 4, SSM_STATE)
        x = x * e2[None, :, :, None, None, None, None] * e4[None, None, None, :, None, :, None]
        return x.sum(axis=(2, 3)).transpose(0, 1, 3, 4, 2).reshape(N_GROUPS, SSM_STATE, SSM_GROUP)
    x = x.reshape(4, 2, 4, SSM_STATE, 2, 4, SSM_GROUP)
    x = x * e2[None, :, None, None, :, None, None] * e4[None, None, :, None, None, :, None]
    return x.sum(axis=(4, 5)).reshape(N_GROUPS, SSM_STATE, SSM_GROUP)


def _pack(parts):
    flat = jnp.concatenate([p.reshape(-1) for p in parts])
    rows = -(-flat.shape[0] // (8 * PACK_COLS)) * 8
    return jnp.pad(flat, (0, rows * PACK_COLS - flat.shape[0])).reshape(rows, PACK_COLS)


def _unpack(packed, shapes):
    flat = packed.reshape(-1)
    out, pos = [], 0
    for shp in shapes:
        size = math.prod(shp)
        out.append(flat[pos:pos + size].reshape(shp))
        pos += size
    return out


def kernel(x, c, w_ada, b_ada, g_pre_mix, g_post_mix, w_in, ssm_lam_re, ssm_lam_im, ssm_log_step, ssm_b_re, ssm_b_im, ssm_c_re, ssm_c_im, ssm_d, glu_w, glu_b, g_out_ssm, conv_w, g_out_conv, w_out, g_pre_ffn, g_post_ffn, w_up, ffn_conv_w, w_down, loss_target, m_w_ada, m_b_ada, m_g_pre_mix, m_g_post_mix, m_w_in, m_ssm_lam_re, m_ssm_lam_im, m_ssm_log_step, m_ssm_b_re, m_ssm_b_im, m_ssm_c_re, m_ssm_c_im, m_ssm_d, m_glu_w, m_glu_b, m_g_out_ssm, m_conv_w, m_g_out_conv, m_w_out, m_g_pre_ffn, m_g_post_ffn, m_w_up, m_ffn_conv_w, m_w_down, v_w_ada, v_b_ada, v_g_pre_mix, v_g_post_mix, v_w_in, v_ssm_lam_re, v_ssm_lam_im, v_ssm_log_step, v_ssm_b_re, v_ssm_b_im, v_ssm_c_re, v_ssm_c_im, v_ssm_d, v_glu_w, v_glu_b, v_g_out_ssm, v_conv_w, v_g_out_conv, v_w_out, v_g_pre_ffn, v_g_post_ffn, v_w_up, v_ffn_conv_w, v_w_down):
    args = dict(locals())
    wts = {n: args[n] for n in WEIGHTS}
    mom_m = {n: args['m_' + n] for n in WEIGHTS}
    mom_v = {n: args['v_' + n] for n in WEIGHTS}
    T = x.shape[1]
    tm = min(512, T)
    tw = min(1024, T)
    me = _me()[3]
    xt, tgt = x[0], loss_target[0]

    (c_all,) = _exchange([c], name='gather_c', scatter=False)
    c_all = c_all.reshape(N_DEV, D_MODEL)
    b_cols = lax.dynamic_slice(b_ada, (0, me * ADA_SHARD), (1, ADA_SHARD))
    mod_cols, c_act = _mod_cols(c_all, w_ada[0], b_cols)
    (mod_all,) = _exchange([mod_cols], name='gather_mod', scatter=False)
    mod = lax.dynamic_slice(mod_all, (0, me, 0), (N_DEV, 1, ADA_SHARD)).reshape(N_MOD, 1, D_MODEL)
    sh1, sc1, gt1, sh2, sc2, gt2 = [mod[k] for k in range(N_MOD)]

    w_in_s, glu_s, w_out_s, conv_s = _exchange(
        [w_in[0].astype(BF16), glu_w[0].astype(BF16), w_out[0].astype(BF16), conv_w[0]], name='gather_weights',
        scatter=False)
    glu_full = glu_s.reshape(D_SSM, D_SSM)
    w_out_full = w_out_s.reshape(D_MODEL, D_MODEL)
    cw_full = conv_s.transpose(1, 0, 2).reshape(3, D_CONV)

    lre_x, lim_x = _expand(ssm_lam_re[0]), _expand(ssm_lam_im[0])
    lst_x = jnp.broadcast_to(ssm_log_step[0][:, None], (N_GROUPS, SSM_STATE * SSM_GROUP))
    b_re_x = ssm_b_re[0].reshape(N_GROUPS, -1)
    b_im_x = ssm_b_im[0].reshape(N_GROUPS, -1)
    ar_x, ai_x, bbr_x, bbi_x = _ssm_prep(lre_x, lim_x, lst_x, b_re_x, b_im_x)
    lam_r = ar_x[:, ::SSM_GROUP].reshape(1, N_STATE)
    lam_i = ai_x[:, ::SSM_GROUP].reshape(1, N_STATE)
    big_b_re = _block_diag_b(bbr_x.reshape(N_GROUPS, SSM_STATE, SSM_GROUP)).astype(BF16)
    big_b_im = _block_diag_b(bbi_x.reshape(N_GROUPS, SSM_STATE, SSM_GROUP)).astype(BF16)
    big_c_re = _block_diag_c(ssm_c_re[0]).astype(BF16)
    big_c_im = _block_diag_c(ssm_c_im[0]).astype(BF16)
    head = jnp.arange(D_SSM)
    avg16 = jnp.where(head[:, None] // SSM_GROUP == head[None, :] // SSM_GROUP, 1.0 / SSM_GROUP, 0.0).astype(BF16)
    hd = D_CONV // CONV_HEADS
    avg64 = jnp.where(head[:, None] // hd == head[None, :] // hd, 1.0 / hd, 0.0).astype(BF16)

    (proj, h1), (w_down_s, ffn_conv_s) = _pre_mix(xt, sc1, sh1, g_pre_mix, w_in_s, tw,
                                                  ([w_down[0].astype(BF16), ffn_conv_w[0]], False))
    wd4 = w_down_s.reshape(4, FF_SHARD, D_MODEL)
    cw4 = ffn_conv_s.reshape(2, 4, 3, FF_SHARD)
    u_perm = _to_scan_rows(proj[:, :D_SSM]).astype(BF16)
    (s_re, s_im, y_perm), (w_up_s,) = _ssm_fwd(u_perm, big_b_re, big_b_im, big_c_re, big_c_im, lam_r, lam_i,
                                               ([w_up[0].astype(BF16)], False))
    yssm = _from_scan_rows(y_perm)
    mix_args = (ssm_d, glu_full, glu_b, g_out_ssm, cw_full, g_out_conv, avg16, avg64)
    ycat = _mix_fwd(yssm, proj, *mix_args, tm)
    o, x1, h2 = _out_proj(ycat, w_out_full, xt, gt1, g_post_mix, g_pre_ffn, sc2, sh2, tm)
    up8 = _ffn_up(h2, w_up_s, tw)
    up4 = up8.reshape(2, 4, T, FF_SHARD)
    act = _ffn_act(up4, cw4, tm)
    dn, dx2, loss_parts = _ffn_down(act, wd4, x1, tgt, gt2, g_post_ffn, tw)
    loss = lax.psum(jnp.sum(loss_parts[:, 0, 0]), ('x', 'y', 'c'))

    got = {}
    ddn, d_gt2, d_g_post_ffn = _post_norm_bwd(dx2, dn, gt2, g_post_ffn, tm, 'ffn_norm_bwd')
    dhid = _ffn_dact(ddn, wd4, up4, cw4, tm)
    g_w_down = _grad_tn(act, ddn, pl.BlockSpec((None, tm, FF_SHARD), lambda g, k: (g, k, 0)),
                        pl.BlockSpec((tm, D_MODEL), lambda g, k: (k, 0)), 4, FF_SHARD, D_MODEL, tm, 'grad_w_down')
    dup8, dcw_ffn = _ffn_dup(dhid.reshape(N_DEV, T, FF_SHARD), up8, ffn_conv_s, tm)
    g_w_up, (got['w_down'],) = _grad_tn(
        h2, dup8, pl.BlockSpec((tm, D_MODEL), lambda g, k: (k, 0)),
        pl.BlockSpec((None, tm, FF_SHARD), lambda g, k: (g, k, 0)), N_DEV, D_MODEL, FF_SHARD, tm, 'grad_w_up',
        ride=([g_w_down.reshape(N_DEV, D_FF // N_DEV, D_MODEL)], True))
    (dx1, d_sh2, d_sc2, d_g_pre_ffn), (got['w_up'], got['ffn_conv_w']) = _pre_norm_bwd(
        dup8, pl.BlockSpec((None, tw, FF_SHARD), lambda i, j: (j, i, 0)), w_up_s, x1, dx2, sc2, g_pre_ffn, tw,
        'ffn_in_bwd', ([g_w_up, dcw_ffn], True))

    d_o, d_gt1, d_g_post_mix = _post_norm_bwd(dx1, o, gt1, g_post_mix, tm, 'mix_norm_bwd')
    g_w_out = _grad_tn(ycat, d_o, pl.BlockSpec((tm, D_MODEL), lambda g, k: (k, 0)),
                       pl.BlockSpec((tm, D_MODEL), lambda g, k: (k, 0)), 1, D_MODEL, D_MODEL, tm, 'grad_w_out')
    dycat = _d_ycat(d_o, w_out_full, tm)
    dy, dconv, dbg, z_b, dlin_b, sums = _mix_bwd(dycat, yssm, proj, *mix_args, tm)
    g_glu_w = _grad_tn(z_b, dlin_b, pl.BlockSpec((tm, D_SSM), lambda g, k: (k, 0)),
                       pl.BlockSpec((tm, D_SSM), lambda g, k: (k, 0)), 1, D_SSM, D_SSM, tm, 'grad_glu_w')
    dy_perm = _to_scan_rows(dy).astype(BF16)
    (du_perm, dbr_blk, dbi_blk, dcr_blk, dci_blk, dar_blk, dai_blk), (got['w_out'], got['glu_w']) = _ssm_bwd(
        dy_perm, u_perm, s_re, s_im, big_b_re, big_b_im, big_c_re, big_c_im, lam_r, lam_i,
        ([g_w_out.reshape(N_DEV, D_MODEL // N_DEV, D_MODEL), g_glu_w.reshape(N_DEV, D_SSM // N_DEV, D_SSM)], True))
    du_ssm = _from_scan_rows(du_perm)
    dproj = _mix_bwd_proj(dconv, proj, du_ssm, dy, ssm_d, dbg, cw_full, tm)
    g_w_in = _grad_tn(h1, dproj, pl.BlockSpec((tm, D_MODEL), lambda g, k: (k, 0)),
                      pl.BlockSpec((tm, IN_SHARD), lambda g, k: (k, g)), N_DEV, D_MODEL, IN_SHARD, tm, 'grad_w_in')
    g_conv_slots = jnp.concatenate([sums[4:7], jnp.zeros((5, D_CONV), F32)]).reshape(
        8, N_DEV, D_CONV // N_DEV).transpose(1, 0, 2)
    (grad_x, d_sh1, d_sc1, d_g_pre_mix), (got['w_in'], got['conv_w']) = _pre_norm_bwd(
        dproj, pl.BlockSpec((tw, IN_SHARD), lambda i, j: (i, j)), w_in_s, xt, dx1, sc1, g_pre_mix, tw, 'mix_in_bwd',
        ([g_w_in, g_conv_slots], True))

    dbb_re = _diag_blocks(dbr_blk, True).reshape(N_GROUPS, -1)
    dbb_im = _diag_blocks(dbi_blk, True).reshape(N_GROUPS, -1)
    d_c_re = _diag_blocks(dcr_blk, False).transpose(0, 2, 1)
    d_c_im = _diag_blocks(dci_blk, False).transpose(0, 2, 1)
    lane = jnp.arange(SSM_STATE * SSM_GROUP)
    seg = jnp.where(lane[:, None] // SSM_GROUP == lane[None, :] // SSM_GROUP, 1.0, 0.0).astype(BF16)
    d_b_re_x, d_b_im_x, d_lre_x, d_lim_x, d_lst = _ssm_prep_bwd(
        lre_x, lim_x, lst_x, b_re_x, b_im_x, dbb_re, dbb_im, _expand(dar_blk.reshape(N_GROUPS, SSM_STATE)),
        _expand(dai_blk.reshape(N_GROUPS, SSM_STATE)), seg)

    dmod = jnp.concatenate([d_sh1, d_sc1, d_gt1, d_sh2, d_sc2, d_gt2], axis=1)
    small_grads = {
        'b_ada': dmod, 'g_pre_mix': d_g_pre_mix, 'g_post_mix': d_g_post_mix,
        'ssm_lam_re': d_lre_x[:, ::SSM_GROUP], 'ssm_lam_im': d_lim_x[:, ::SSM_GROUP], 'ssm_log_step': d_lst,
        'ssm_b_re': d_b_re_x, 'ssm_b_im': d_b_im_x, 'ssm_c_re': d_c_re, 'ssm_c_im': d_c_im,
        'ssm_d': sums[3:4], 'glu_b': sums[0:1], 'g_out_ssm': sums[1:2], 'g_out_conv': sums[2:3],
        'g_pre_ffn': d_g_pre_ffn, 'g_post_ffn': d_g_post_ffn,
    }
    (small_all,) = _exchange([_pack([small_grads[n] for n in SMALL])], name='gather_small_grads', scatter=False)
    small_out = _adamw(small_all, _pack([wts[n] for n in SMALL]), _pack([mom_m[n] for n in SMALL]),
                       _pack([mom_v[n] for n in SMALL]), 'adamw_small')
    small_shapes = [wts[n].shape for n in SMALL]
    res = {}
    for kind, packed in zip(('g', 'd', 'm', 'v'), small_out):
        for n, val in zip(SMALL, _unpack(packed, small_shapes)):
            res[kind, n] = val

    dmod_all = small_all[:, :N_MOD, :].reshape(N_DEV, N_MOD * D_MODEL)
    dmod_cols = lax.dynamic_slice(dmod_all, (0, me * ADA_SHARD), (N_DEV, ADA_SHARD))
    g_w_ada = _grad_w_ada(c_act.T, dmod_cols)

    for n, slots in got.items():
        if n in ('conv_w', 'ffn_conv_w'):
            slots = slots[:, :3, :]
        outs = _adamw(slots, wts[n][0], mom_m[n][0], mom_v[n][0], 'adamw_' + n)
        for kind, val in zip(('g', 'd', 'm', 'v'), outs):
            res[kind, n] = val[None]
    outs = _adamw(g_w_ada[None], w_ada[0], m_w_ada[0], v_w_ada[0], 'adamw_w_ada')
    for kind, val in zip(('g', 'd', 'm', 'v'), outs):
        res[kind, 'w_ada'] = val[None]

    return (loss, grad_x[None], *[res['g', n] for n in WEIGHTS], *[res['d', n] for n in WEIGHTS],
            *[res['m', n] for n in WEIGHTS], *[res['v', n] for n in WEIGHTS])
```

```python
import math

import jax
import jax.numpy as jnp
from jax import lax
from jax.experimental import pallas as pl
from jax.experimental.pallas import tpu as pltpu

F32, BF16 = jnp.float32, jnp.bfloat16

D_MODEL = 1024
D_SSM = 512
D_CONV = 512
SSM_GROUP = 16
N_GROUPS = 32
SSM_STATE = 64
N_STATE = N_GROUPS * SSM_STATE
CONV_HEADS = 8
D_FF = 2816
N_MOD = 6
D_IN_PROJ = D_SSM + 3 * D_CONV
N_DEV = 8
FF_SHARD = 2 * D_FF // N_DEV
IN_SHARD = D_IN_PROJ // N_DEV
ADA_SHARD = N_MOD * D_MODEL // N_DEV
EPS = 1e-6
LAMBDA_RE_MAX = -1e-4
ADAM_LR, ADAM_B1, ADAM_B2, ADAM_EPS, ADAM_WD, ADAM_STEP = 0.001, 0.9, 0.999, 1e-08, 0.01, 10
GELU_C = math.sqrt(2.0 / math.pi)
GELU_A = 0.044715

SUBLANES = 8
HALO = 8
HALO16 = 16
STATE_BLOCK = 256
CHAN_BLOCK = 128
VMEM_BIG = 48 << 20

WEIGHTS = ['w_ada', 'b_ada', 'g_pre_mix', 'g_post_mix', 'w_in', 'ssm_lam_re', 'ssm_lam_im', 'ssm_log_step',
           'ssm_b_re', 'ssm_b_im', 'ssm_c_re', 'ssm_c_im', 'ssm_d', 'glu_w', 'glu_b', 'g_out_ssm', 'conv_w',
           'g_out_conv', 'w_out', 'g_pre_ffn', 'g_post_ffn', 'w_up', 'ffn_conv_w', 'w_down']
SHARDED = ('w_ada', 'w_in', 'glu_w', 'conv_w', 'w_out', 'w_up', 'ffn_conv_w', 'w_down')
PACK_COLS = 1024


def _call(body, *, name, grid, in_specs, out_specs, out_shape, scratch=(), sem=None, vmem=None, ride=None):
    params = {}
    if vmem is not None:
        params['vmem_limit_bytes'] = vmem
    if ride is None:
        if sem is not None:
            params['dimension_semantics'] = sem
        return pl.pallas_call(body, name=name, grid=grid, in_specs=in_specs, out_specs=out_specs,
                              out_shape=out_shape, scratch_shapes=list(scratch),
                              compiler_params=pltpu.CompilerParams(**params))
    arrs, scatter = ride
    single = not isinstance(out_shape, (list, tuple))
    out_shape_l = [out_shape] if single else list(out_shape)
    out_specs_l = [out_specs] if single else list(out_specs)
    n, n_in, n_out, n_scr = len(arrs), len(in_specs), len(out_shape_l), len(scratch)
    any_spec = pl.BlockSpec(memory_space=pl.ANY)
    params['dimension_semantics'] = ('arbitrary',) * len(grid)

    def carried(*refs):
        ins, rin = refs[:n_in], refs[n_in:n_in + n]
        outs, rout = refs[n_in + n:n_in + n + n_out], refs[n_in + n + n_out:n_in + 2 * n + n_out]
        scr, sems = refs[n_in + 2 * n + n_out:n_in + 2 * n + n_out + n_scr], refs[n_in + 2 * n + n_out + n_scr:]
        first = pl.program_id(0) == 0
        last = pl.program_id(0) == grid[0] - 1
        for ax in range(1, len(grid)):
            first = jnp.logical_and(first, pl.program_id(ax) == 0)
            last = jnp.logical_and(last, pl.program_id(ax) == grid[ax] - 1)

        @pl.when(first)
        def _():
            _exchange_start(rin, rout, sems, scatter)

        body(*ins, *outs, *scr)

        @pl.when(last)
        def _():
            _exchange_wait(rin, rout, sems, scatter)

    call = pl.pallas_call(carried, name=name, grid=grid, in_specs=list(in_specs) + [any_spec] * n,
                          out_specs=out_specs_l + [any_spec] * n,
                          out_shape=out_shape_l + _exchange_shapes(arrs, scatter),
                          scratch_shapes=list(scratch) + _exchange_sems(n),
                          compiler_params=pltpu.CompilerParams(**params))

    def run(*args):
        res = call(*args, *arrs)
        own = res[0] if single else list(res[:n_out])
        return own, list(res[n_out:])

    return run


def _const(shape):
    nd = len(shape)
    return pl.BlockSpec(shape, lambda *_: (0,) * nd)


def _sds(shape, dtype=F32):
    return jax.ShapeDtypeStruct(shape, dtype)


def _dot(a, b):
    return jnp.dot(a, b, preferred_element_type=F32)


def _dot_nt(a, b):
    return lax.dot_general(a, b, (((1,), (1,)), ((), ())), preferred_element_type=F32)


def _dot_tn(a, b):
    return lax.dot_general(a, b, (((0,), (0,)), ((), ())), preferred_element_type=F32)


def _dot_split(x, mat, parts):
    acc = None
    rem = x
    for _ in range(parts):
        piece = rem.astype(BF16)
        rem = rem - piece.astype(F32)
        term = _dot(piece, mat)
        acc = term if acc is None else acc + term
    return acc


def _sigmoid(x):
    return 1.0 / (1.0 + jnp.exp(-x))


def _gelu(x):
    t = jnp.tanh(GELU_C * (x + GELU_A * x * x * x))
    return 0.5 * x * (1.0 + t), t


def _gelu_grad(x, t):
    return 0.5 * (1.0 + t) + 0.5 * x * (1.0 - t * t) * GELU_C * (1.0 + 3.0 * GELU_A * x * x)


def _rsqrt_mean(x):
    return lax.rsqrt(jnp.mean(x * x, axis=-1, keepdims=True) + EPS)


def _colsum(x):
    return jnp.sum(x, axis=0, keepdims=True)


def _shifts_down(x, halo):
    ext = jnp.concatenate([halo, x], axis=0)
    return pltpu.roll(ext, 1, 0)[halo.shape[0]:], pltpu.roll(ext, 2, 0)[halo.shape[0]:]


def _shifts_up(x, halo):
    n = x.shape[0]
    ext = jnp.concatenate([x, halo], axis=0)
    total = ext.shape[0]
    return pltpu.roll(ext, total - 1, 0)[:n], pltpu.roll(ext, total - 2, 0)[:n]


def _conv3(x, halo, w_ref):
    x1, x2 = _shifts_down(x, halo)
    return w_ref[0:1, :] * x2 + w_ref[1:2, :] * x1 + w_ref[2:3, :] * x, x1, x2


def _conv3_t(g, halo, w_ref):
    g1, g2 = _shifts_up(g, halo)
    return w_ref[2:3, :] * g + w_ref[1:2, :] * g1 + w_ref[0:1, :] * g2, g1, g2


def _silu_parts(x):
    s = _sigmoid(x)
    return x * s, s * (1.0 + x * (1.0 - s))


def _norm_bwd(dn, x, r, g):
    gd = g * dn
    return r * gd - x * (r * r * r) * jnp.mean(gd * x, axis=-1, keepdims=True)


def _head_norm_bwd(dn, y, rs, g, avg):
    gd = g * dn
    return rs * gd - y * (rs * rs * rs) * _dot_split(gd * y, avg, 2)


def _me():
    x, y, c = lax.axis_index('x'), lax.axis_index('y'), lax.axis_index('c')
    return x, y, c, 4 * x + 2 * y + c


def _peer(k):
    x, y, c, _ = _me()
    px = 1 - x if k & 4 else x
    py = 1 - y if k & 2 else y
    pc = 1 - c if k & 1 else c
    return (px, py, pc), 4 * px + 2 * py + pc


def _exchange_copies(ins, outs, sems, scatter):
    send_sems, recv_sems, local_sems = sems
    me = _me()[3]
    local, sends, recvs = [], [], []
    for a in range(len(ins)):
        src = ins[a].at[me] if scatter else ins[a]
        local.append(pltpu.make_async_copy(src, outs[a].at[me], local_sems.at[a]))
        for k in range(1, N_DEV):
            dev, idx = _peer(k)
            src = ins[a].at[idx] if scatter else ins[a]
            for dst, group in ((outs[a].at[me], sends), (outs[a].at[idx], recvs)):
                group.append(pltpu.make_async_remote_copy(
                    src_ref=src, dst_ref=dst, send_sem=send_sems.at[a, k - 1], recv_sem=recv_sems.at[a, k - 1],
                    device_id=dev, device_id_type=pl.DeviceIdType.MESH))
    return local, sends, recvs


def _exchange_start(ins, outs, sems, scatter):
    local, sends, _ = _exchange_copies(ins, outs, sems, scatter)
    for cp in local + sends:
        cp.start()


def _exchange_wait(ins, outs, sems, scatter):
    local, sends, recvs = _exchange_copies(ins, outs, sems, scatter)
    for cp in recvs:
        cp.wait_recv()
    for cp in sends:
        cp.wait_send()
    for cp in local:
        cp.wait()


def _exchange_shapes(arrs, scatter):
    return [_sds(a.shape if scatter else (N_DEV,) + a.shape, a.dtype) for a in arrs]


def _exchange_sems(n):
    return [pltpu.SemaphoreType.DMA((n, N_DEV - 1)), pltpu.SemaphoreType.DMA((n, N_DEV - 1)),
            pltpu.SemaphoreType.DMA((n,))]


def _exchange(arrs, *, name, scatter):
    n = len(arrs)

    def body(*refs):
        _exchange_start(refs[:n], refs[n:2 * n], refs[2 * n:], scatter)
        _exchange_wait(refs[:n], refs[n:2 * n], refs[2 * n:], scatter)

    any_spec = pl.BlockSpec(memory_space=pl.ANY)
    outs = pl.pallas_call(body, name=name, out_shape=_exchange_shapes(arrs, scatter), in_specs=[any_spec] * n,
                          out_specs=[any_spec] * n, scratch_shapes=_exchange_sems(n))(*arrs)
    return list(outs)


def _mod_cols(c_all, w_ada, b_cols):
    def body(c_ref, w_ref, b_ref, mod_ref, act_ref):
        c = c_ref[...]
        act = c * _sigmoid(c)
        act_ref[...] = act
        mod_ref[...] = _dot(act.astype(BF16), w_ref[...].astype(BF16)) + b_ref[...]

    return _call(body, name='mod_cols', grid=(1,),
                 in_specs=[_const(c_all.shape), _const(w_ada.shape), _const(b_cols.shape)],
                 out_specs=[_const((N_DEV, ADA_SHARD)), _const(c_all.shape)],
                 out_shape=[_sds((N_DEV, ADA_SHARD)), _sds(c_all.shape)], vmem=VMEM_BIG)(c_all, w_ada, b_cols)


def _grad_w_ada(act_t, dmod_cols):
    def body(a_ref, d_ref, o_ref):
        o_ref[...] = _dot(a_ref[...], d_ref[...])

    return _call(body, name='grad_w_ada', grid=(1,), in_specs=[_const(act_t.shape), _const(dmod_cols.shape)],
                 out_specs=_const((D_MODEL, ADA_SHARD)), out_shape=_sds((D_MODEL, ADA_SHARD)),
                 vmem=VMEM_BIG)(act_t, dmod_cols)


def _pre_mix(x, sc, sh, g, w_s, tm, ride):
    T = x.shape[0]

    def body(x_ref, sc_ref, sh_ref, g_ref, w_ref, proj_ref, h_ref):
        @pl.when(pl.program_id(1) == 0)
        def _():
            xv = x_ref[...]
            h_ref[...] = ((xv * _rsqrt_mean(xv) * g_ref[...]) * (1.0 + sc_ref[...]) + sh_ref[...]).astype(BF16)

        proj_ref[...] = _dot(h_ref[...], w_ref[...])

    row = pl.BlockSpec((tm, D_MODEL), lambda i, j: (i, 0))
    vec = _const((1, D_MODEL))
    return _call(body, name='pre_mix', grid=(T // tm, N_DEV),
                 in_specs=[row, vec, vec, vec, pl.BlockSpec((None, D_MODEL, IN_SHARD), lambda i, j: (j, 0, 0))],
                 out_specs=[pl.BlockSpec((tm, IN_SHARD), lambda i, j: (i, j)), row],
                 out_shape=[_sds((T, D_IN_PROJ)), _sds((T, D_MODEL), BF16)],
                 sem=('parallel', 'arbitrary'), ride=ride)(x, sc, sh, g, w_s)


def _halo_before(tm, rows=HALO):
    return lambda i: jnp.maximum(i * (tm // rows) - 1, 0)


def _halo_after(tm, T, rows=HALO):
    return lambda i: jnp.minimum((i + 1) * (tm // rows), T // rows - 1)


def _mix_fwd(yssm, proj, d, glu_w, glu_b, g_ssm, cw, g_conv, avg16, avg64, tm):
    T = yssm.shape[0]
    hb = _halo_before(tm)

    def body(y_ref, p_ref, ph_ref, d_ref, gw_ref, gb_ref, gs_ref, cw_ref, gc_ref, a16_ref, a64_ref, o_ref):
        i = pl.program_id(0)
        u = p_ref[:, 0:D_SSM]
        y = y_ref[...] + d_ref[...] * u
        z, _ = _gelu(y)
        gate = _sigmoid(_dot(z.astype(BF16), gw_ref[...]) + gb_ref[...])
        ya = z * gate
        rs = lax.rsqrt(_dot_split(ya * ya, a16_ref[...], 2) + EPS)
        o_ref[:, 0:D_SSM] = (ya * rs * gs_ref[...]).astype(BF16)
        bg = p_ref[:, D_SSM:D_SSM + D_CONV]
        cv = p_ref[:, D_SSM + D_CONV:D_SSM + 2 * D_CONV] * p_ref[:, D_SSM + 2 * D_CONV:D_IN_PROJ]
        hv = ph_ref[:, D_SSM + D_CONV:D_SSM + 2 * D_CONV] * ph_ref[:, D_SSM + 2 * D_CONV:D_IN_PROJ]
        hv = jnp.where(i > 0, hv, 0.0)
        conv, _, _ = _conv3(cv, hv, cw_ref)
        yb = bg * conv
        rsb = lax.rsqrt(_dot_split(yb * yb, a64_ref[...], 2) + EPS)
        o_ref[:, D_SSM:D_MODEL] = (yb * rsb * gc_ref[...]).astype(BF16)

    vec = _const((1, D_SSM))
    sq = _const((D_SSM, D_SSM))
    return _call(body, name='mix_fwd', grid=(T // tm,),
                 in_specs=[pl.BlockSpec((tm, D_SSM), lambda i: (i, 0)), pl.BlockSpec((tm, D_IN_PROJ), lambda i: (i, 0)),
                           pl.BlockSpec((HALO, D_IN_PROJ), lambda i: (hb(i), 0)), vec, sq, vec, vec,
                           _const((3, D_CONV)), vec, sq, sq],
                 out_specs=pl.BlockSpec((tm, D_MODEL), lambda i: (i, 0)), out_shape=_sds((T, D_MODEL), BF16),
                 sem=('parallel',), vmem=VMEM_BIG)(yssm, proj, proj, d, glu_w, glu_b, g_ssm, cw, g_conv, avg16, avg64)


def _out_proj(ycat, w_out, x, gt, g_post, g_pre, sc, sh, tm):
    T = x.shape[0]

    def body(y_ref, w_ref, x_ref, gt_ref, gp_ref, g2_ref, sc_ref, sh_ref, o_ref, x1_ref, h_ref):
        o = _dot(y_ref[...], w_ref[...])
        o_ref[...] = o
        x1 = x_ref[...] + gt_ref[...] * (o * _rsqrt_mean(o) * gp_ref[...])
        x1_ref[...] = x1
        h_ref[...] = ((x1 * _rsqrt_mean(x1) * g2_ref[...]) * (1.0 + sc_ref[...]) + sh_ref[...]).astype(BF16)

    row = pl.BlockSpec((tm, D_MODEL), lambda i: (i, 0))
    vec = _const((1, D_MODEL))
    return _call(body, name='out_proj', grid=(T // tm,),
                 in_specs=[row, _const((D_MODEL, D_MODEL)), row, vec, vec, vec, vec, vec],
                 out_specs=[row, row, row],
                 out_shape=[_sds((T, D_MODEL)), _sds((T, D_MODEL)), _sds((T, D_MODEL), BF16)],
                 sem=('parallel',), vmem=VMEM_BIG)(ycat, w_out, x, gt, g_post, g_pre, sc, sh)


def _ffn_up(h2, w_s, tm):
    T = h2.shape[0]

    def body(h_ref, w_ref, o_ref):
        o_ref[...] = _dot(h_ref[...], w_ref[...]).astype(BF16)

    return _call(body, name='ffn_up', grid=(T // tm, N_DEV),
                 in_specs=[pl.BlockSpec((tm, D_MODEL), lambda i, j: (i, 0)),
                           pl.BlockSpec((None, D_MODEL, FF_SHARD), lambda i, j: (j, 0, 0))],
                 out_specs=pl.BlockSpec((None, tm, FF_SHARD), lambda i, j: (j, i, 0)),
                 out_shape=_sds((N_DEV, T, FF_SHARD), BF16), sem=('parallel', 'parallel'))(h2, w_s)


def _ffn_hidden(up_ref, halo_ref, cw_ref, i):
    hid = []
    for part in range(2):
        halo = jnp.where(i > 0, halo_ref[part].astype(F32), 0.0)
        hid.append(_conv3(up_ref[part].astype(F32), halo, cw_ref.at[part])[0])
    return hid


def _ffn_act(up4, cw4, tm):
    T = up4.shape[2]
    hb = _halo_before(tm, HALO16)

    def body(up_ref, halo_ref, cw_ref, o_ref):
        hid_a, hid_v = _ffn_hidden(up_ref, halo_ref, cw_ref, pl.program_id(0))
        o_ref[...] = (_silu_parts(hid_a)[0] * hid_v).astype(BF16)

    return _call(body, name='ffn_act', grid=(T // tm, 4),
                 in_specs=[pl.BlockSpec((2, None, tm, FF_SHARD), lambda i, j: (0, j, i, 0)),
                           pl.BlockSpec((2, None, HALO16, FF_SHARD), lambda i, j: (0, j, hb(i), 0)),
                           pl.BlockSpec((2, None, 3, FF_SHARD), lambda i, j: (0, j, 0, 0))],
                 out_specs=pl.BlockSpec((None, tm, FF_SHARD), lambda i, j: (j, i, 0)),
                 out_shape=_sds((4, T, FF_SHARD), BF16), sem=('parallel', 'parallel'))(up4, up4, cw4)


def _ffn_down(act, wd4, x1, tgt, gt, g_post, tm):
    T = x1.shape[0]
    nb = T // tm

    def body(a_ref, w_ref, x1_ref, t_ref, gt_ref, g_ref, dn_ref, dx_ref, loss_ref):
        j = pl.program_id(1)
        part = _dot(a_ref[...], w_ref[...])

        @pl.when(j == 0)
        def _():
            dn_ref[...] = part

        @pl.when(j > 0)
        def _():
            dn_ref[...] += part

        @pl.when(j == 3)
        def _():
            dn = dn_ref[...]
            x2 = x1_ref[...] + gt_ref[...] * (dn * _rsqrt_mean(dn) * g_ref[...])
            err = x2 - t_ref[...]
            dx_ref[...] = err * (1.0 / D_MODEL)
            tot = jnp.sum(jnp.sum(err * err, axis=1, keepdims=True), axis=0, keepdims=True) * (0.5 / D_MODEL)
            loss_ref[...] = jnp.broadcast_to(tot, (8, 128))

    row = pl.BlockSpec((tm, D_MODEL), lambda i, j: (i, 0))
    vec = _const((1, D_MODEL))
    return _call(body, name='ffn_down', grid=(nb, 4),
                 in_specs=[pl.BlockSpec((None, tm, FF_SHARD), lambda i, j: (j, i, 0)),
                           pl.BlockSpec((None, FF_SHARD, D_MODEL), lambda i, j: (j, 0, 0)), row, row, vec, vec],
                 out_specs=[row, row, pl.BlockSpec((None, 8, 128), lambda i, j: (i, 0, 0))],
                 out_shape=[_sds((T, D_MODEL)), _sds((T, D_MODEL)), _sds((nb, 8, 128))],
                 sem=('parallel', 'arbitrary'), vmem=VMEM_BIG)(act, wd4, x1, tgt, gt, g_post)


def _ssm_prep(lre, lim, lst, b_re, b_im):
    def body(lre_ref, lim_ref, lst_ref, br_ref, bi_ref, ar_ref, ai_ref, bbr_ref, bbi_ref):
        ar, ai, qr, qi = _zoh(lre_ref[...], lim_ref[...], lst_ref[...])[:4]
        ar_ref[...] = ar
        ai_ref[...] = ai
        bbr_ref[...] = qr * br_ref[...] - qi * bi_ref[...]
        bbi_ref[...] = qr * bi_ref[...] + qi * br_ref[...]

    shp = lre.shape
    return _call(body, name='ssm_prep', grid=(1,), in_specs=[_const(shp)] * 5, out_specs=[_const(shp)] * 4,
                 out_shape=[_sds(shp)] * 4)(lre, lim, lst, b_re, b_im)


def _zoh(lre, lim, lst):
    lr = jnp.minimum(lre, LAMBDA_RE_MAX)
    st = jnp.exp(lst)
    mag = jnp.exp(lr * st)
    ar = mag * jnp.cos(lim * st)
    ai = mag * jnp.sin(lim * st)
    den = lr * lr + lim * lim
    qr = ((ar - 1.0) * lr + ai * lim) / den
    qi = (ai * lr - (ar - 1.0) * lim) / den
    return ar, ai, qr, qi, lr, st, den


def _ssm_prep_bwd(lre, lim, lst, b_re, b_im, dbbr, dbbi, dar, dai, seg):
    def body(lre_ref, lim_ref, lst_ref, br_ref, bi_ref, dbbr_ref, dbbi_ref, dar_ref, dai_ref, seg_ref,
             dbr_ref, dbi_ref, dlre_ref, dlim_ref, dlst_ref):
        lre_v = lre_ref[...]
        li = lim_ref[...]
        ar, ai, qr, qi, lr, st, den = _zoh(lre_v, li, lst_ref[...])
        br, bi, gbr, gbi = br_ref[...], bi_ref[...], dbbr_ref[...], dbbi_ref[...]
        dbr_ref[...] = qr * gbr + qi * gbi
        dbi_ref[...] = qr * gbi - qi * gbr
        gqr = _dot_split(br * gbr + bi * gbi, seg_ref[...], 3)
        gqi = _dot_split(br * gbi - bi * gbr, seg_ref[...], 3)
        ir, ii = lr / den, -li / den
        gar = dar_ref[...] + ir * gqr + ii * gqi
        gai = dai_ref[...] + ir * gqi - ii * gqr
        tr, ti = qr * ir - qi * ii, qr * ii + qi * ir
        glr = -(tr * gqr + ti * gqi)
        gli = -(tr * gqi - ti * gqr)
        gzr = ar * gar + ai * gai
        gzi = ar * gai - ai * gar
        glr = glr + st * gzr
        gli = gli + st * gzi
        gst = (lr * gzr + li * gzi) * st
        dlre_ref[...] = jnp.where(lre_v < LAMBDA_RE_MAX, glr, 0.0)
        dlim_ref[...] = gli
        dlst_ref[...] = jnp.sum(gst, axis=1, keepdims=True) * (1.0 / SSM_GROUP)

    shp = lre.shape
    return _call(body, name='ssm_prep_bwd', grid=(1,), in_specs=[_const(shp)] * 9 + [_const(seg.shape)],
                 out_specs=[_const(shp)] * 4 + [_const((N_GROUPS, 1))],
                 out_shape=[_sds(shp)] * 4 + [_sds((N_GROUPS, 1))], vmem=VMEM_BIG)(
                     lre, lim, lst, b_re, b_im, dbbr, dbbi, dar, dai, seg)


def _scan_specs(T):
    half = lambda cb: cb // 2
    return dict(
        chan=pl.BlockSpec((T, CHAN_BLOCK), lambda cb: (0, half(cb))),
        state=pl.BlockSpec((T, STATE_BLOCK), lambda cb: (0, cb)),
        b=pl.BlockSpec((CHAN_BLOCK, STATE_BLOCK), lambda cb: (half(cb), cb)),
        c=pl.BlockSpec((STATE_BLOCK, CHAN_BLOCK), lambda cb: (cb, half(cb))),
        lam=pl.BlockSpec((1, STATE_BLOCK), lambda cb: (0, cb)),
    )


def _complex_power(re, im, n):
    out = None
    while True:
        if n & 1:
            out = (re, im) if out is None else (out[0] * re - out[1] * im, out[0] * im + out[1] * re)
        n >>= 1
        if n == 0:
            return out
        re, im = re * re - im * im, 2.0 * re * im


def _rows8(i):
    return pl.ds(pl.multiple_of(i * SUBLANES, SUBLANES), SUBLANES)


def _ssm_fwd(u_perm, b_re, b_im, c_re, c_im, lam_r, lam_i, ride):
    T = u_perm.shape[0]
    ls = T // SUBLANES
    rc = min(512, T)
    sp = _scan_specs(T)

    def body(u_ref, bre_ref, bim_ref, cre_ref, cim_ref, lr_ref, li_ref, sre_ref, sim_ref, y_ref):
        cb = pl.program_id(0)
        for c in range(T // rc):
            rows = pl.ds(c * rc, rc)
            sre_ref[rows, :] = _dot(u_ref[rows, :], bre_ref[...])
            sim_ref[rows, :] = _dot(u_ref[rows, :], bim_ref[...])
        shp = (SUBLANES, STATE_BLOCK)
        lr = jnp.broadcast_to(lr_ref[...], shp)
        li = jnp.broadcast_to(li_ref[...], shp)
        zero = jnp.zeros(shp, F32)

        def step(i, carry):
            sr, si = carry
            rows = _rows8(i)
            nr = lr * sr - li * si + sre_ref[rows, :]
            ni = lr * si + li * sr + sim_ref[rows, :]
            sre_ref[rows, :] = nr
            sim_ref[rows, :] = ni
            return nr, ni

        fr, fi = lax.fori_loop(0, ls, step, (zero, zero))
        pr, pi_ = _complex_power(lr, li, ls)
        row = lax.broadcasted_iota(jnp.int32, shp, 0)
        ir, ii = zero, zero
        for _ in range(SUBLANES - 1):
            er = fr + pr * ir - pi_ * ii
            ei = fi + pr * ii + pi_ * ir
            ir = jnp.where(row == 0, 0.0, pltpu.roll(er, 1, 0))
            ii = jnp.where(row == 0, 0.0, pltpu.roll(ei, 1, 0))

        def fix(i, carry):
            cr, ci = carry
            rows = _rows8(i)
            nr = lr * cr - li * ci
            ni = lr * ci + li * cr
            sre_ref[rows, :] += nr
            sim_ref[rows, :] += ni
            return nr, ni

        lax.fori_loop(0, ls, fix, (ir, ii))
        for c in range(T // rc):
            rows = pl.ds(c * rc, rc)
            yc = _dot(sre_ref[rows, :].astype(BF16), cre_ref[...]) - _dot(sim_ref[rows, :].astype(BF16), cim_ref[...])

            @pl.when(cb % 2 == 0)
            def _():
                y_ref[rows, :] = yc

            @pl.when(cb % 2 == 1)
            def _():
                y_ref[rows, :] += yc

    return _call(body, name='ssm_fwd', grid=(N_STATE // STATE_BLOCK,),
                 in_specs=[sp['chan'], sp['b'], sp['b'], sp['c'], sp['c'], sp['lam'], sp['lam']],
                 out_specs=[sp['state'], sp['state'], sp['chan']],
                 out_shape=[_sds((T, N_STATE)), _sds((T, N_STATE)), _sds((T, D_SSM))],
                 sem=('arbitrary',), vmem=VMEM_BIG, ride=ride)(u_perm, b_re, b_im, c_re, c_im, lam_r, lam_i)


def _ssm_bwd(dy_perm, u_perm, s_re, s_im, b_re, b_im, c_re, c_im, lam_r, lam_i, ride):
    T = u_perm.shape[0]
    ls = T // SUBLANES
    rc = min(512, T)
    sp = _scan_specs(T)
    ncb = N_STATE // STATE_BLOCK

    def body(dy_ref, u_ref, sre_ref, sim_ref, bre_ref, bim_ref, cre_ref, cim_ref, lr_ref, li_ref,
             du_ref, dbr_ref, dbi_ref, dcr_ref, dci_ref, dar_ref, dai_ref, gre_ref, gim_ref):
        cb = pl.program_id(0)
        for c in range(T // rc):
            rows = pl.ds(c * rc, rc)
            gre_ref[rows, :] = _dot_nt(dy_ref[rows, :], cre_ref[...])
            gim_ref[rows, :] = -_dot_nt(dy_ref[rows, :], cim_ref[...])
        shp = (SUBLANES, STATE_BLOCK)
        lr = jnp.broadcast_to(lr_ref[...], shp)
        li = jnp.broadcast_to(li_ref[...], shp)
        zero = jnp.zeros(shp, F32)

        def step(k, carry):
            gr, gi = carry
            rows = _rows8(ls - 1 - k)
            nr = lr * gr + li * gi + gre_ref[rows, :]
            ni = lr * gi - li * gr + gim_ref[rows, :]
            gre_ref[rows, :] = nr
            gim_ref[rows, :] = ni
            return nr, ni

        fr, fi = lax.fori_loop(0, ls, step, (zero, zero))
        pr, pi_ = _complex_power(lr, -li, ls)
        row = lax.broadcasted_iota(jnp.int32, shp, 0)
        cr, ci = zero, zero
        for _ in range(SUBLANES - 1):
            er = fr + pr * cr - pi_ * ci
            ei = fi + pr * ci + pi_ * cr
            cr = jnp.where(row == SUBLANES - 1, 0.0, pltpu.roll(er, SUBLANES - 1, 0))
            ci = jnp.where(row == SUBLANES - 1, 0.0, pltpu.roll(ei, SUBLANES - 1, 0))

        def fix(k, carry):
            dr, di, ar, ai = carry
            rows = _rows8(ls - 1 - k)
            dr, di = lr * dr + li * di, lr * di - li * dr
            gr = gre_ref[rows, :] + dr
            gi = gim_ref[rows, :] + di
            gre_ref[rows, :] = gr
            gim_ref[rows, :] = gi
            prev = _rows8(ls - 2 - k)
            spr, spi = sre_ref[prev, :], sim_ref[prev, :]
            return dr, di, ar + gr * spr + gi * spi, ai + gi * spr - gr * spi

        dr, di, ar, ai = lax.fori_loop(0, ls - 1, fix, (cr, ci, zero, zero))
        first = pl.ds(0, SUBLANES)
        last = pl.ds((ls - 1) * SUBLANES, SUBLANES)
        gr = gre_ref[first, :] + (lr * dr + li * di)
        gi = gim_ref[first, :] + (lr * di - li * dr)
        gre_ref[first, :] = gr
        gim_ref[first, :] = gi
        spr = jnp.where(row == 0, 0.0, pltpu.roll(sre_ref[last, :], 1, 0))
        spi = jnp.where(row == 0, 0.0, pltpu.roll(sim_ref[last, :], 1, 0))
        dar_ref[...] = _colsum(ar + gr * spr + gi * spi)
        dai_ref[...] = _colsum(ai + gi * spr - gr * spi)

        for c in range(T // rc):
            rows = pl.ds(c * rc, rc)
            g_r, g_i = gre_ref[rows, :].astype(BF16), gim_ref[rows, :].astype(BF16)
            s_r, s_i = sre_ref[rows, :].astype(BF16), sim_ref[rows, :].astype(BF16)
            ub, dyb = u_ref[rows, :], dy_ref[rows, :]
            duc = _dot_nt(g_r, bre_ref[...]) + _dot_nt(g_i, bim_ref[...])
            parts = (_dot_tn(ub, g_r), _dot_tn(ub, g_i), _dot_tn(s_r, dyb), -_dot_tn(s_i, dyb))
            outs = (dbr_ref, dbi_ref, dcr_ref, dci_ref)
            for o_ref, part in zip(outs, parts):
                if c == 0:
                    o_ref[...] = part
                else:
                    o_ref[...] += part

            @pl.when(cb % 2 == 0)
            def _():
                du_ref[rows, :] = duc

            @pl.when(cb % 2 == 1)
            def _():
                du_ref[rows, :] += duc

    blk = lambda r, c: pl.BlockSpec((None, r, c), lambda cb: (cb, 0, 0))
    return _call(body, name='ssm_bwd', grid=(ncb,),
                 in_specs=[sp['chan'], sp['chan'], sp['state'], sp['state'], sp['b'], sp['b'], sp['c'], sp['c'],
                           sp['lam'], sp['lam']],
                 out_specs=[sp['chan'], blk(CHAN_BLOCK, STATE_BLOCK), blk(CHAN_BLOCK, STATE_BLOCK),
                            blk(STATE_BLOCK, CHAN_BLOCK), blk(STATE_BLOCK, CHAN_BLOCK), blk(1, STATE_BLOCK),
                            blk(1, STATE_BLOCK)],
                 out_shape=[_sds((T, D_SSM)), _sds((ncb, CHAN_BLOCK, STATE_BLOCK)), _sds((ncb, CHAN_BLOCK, STATE_BLOCK)),
                            _sds((ncb, STATE_BLOCK, CHAN_BLOCK)), _sds((ncb, STATE_BLOCK, CHAN_BLOCK)),
                            _sds((ncb, 1, STATE_BLOCK)), _sds((ncb, 1, STATE_BLOCK))],
                 scratch=[pltpu.VMEM((T, STATE_BLOCK), F32), pltpu.VMEM((T, STATE_BLOCK), F32)],
                 sem=('arbitrary',), vmem=VMEM_BIG, ride=ride)(dy_perm, u_perm, s_re, s_im, b_re, b_im, c_re, c_im,
                                                               lam_r, lam_i)


def _post_norm_bwd(dx, val, gate, g, tm, name):
    T = dx.shape[0]

    def body(dx_ref, v_ref, gt_ref, g_ref, dv_ref, dgt_ref, dg_ref):
        @pl.when(pl.program_id(0) == 0)
        def _():
            dgt_ref[...] = jnp.zeros_like(dgt_ref)
            dg_ref[...] = jnp.zeros_like(dg_ref)

        dxv, v, gv = dx_ref[...], v_ref[...], g_ref[...]
        r = _rsqrt_mean(v)
        dgt_ref[...] += _colsum(dxv * (v * r * gv))
        dn = dxv * gt_ref[...]
        dg_ref[...] += _colsum(dn * v * r)
        dv_ref[...] = _norm_bwd(dn, v, r, gv).astype(BF16)

    row = pl.BlockSpec((tm, D_MODEL), lambda i: (i, 0))
    vec = _const((1, D_MODEL))
    return _call(body, name=name, grid=(T // tm,), in_specs=[row, row, vec, vec], out_specs=[row, vec, vec],
                 out_shape=[_sds((T, D_MODEL), BF16), _sds((1, D_MODEL)), _sds((1, D_MODEL))],
                 sem=('arbitrary',))(dx, val, gate, g)


def _ffn_dact(ddn, wd4, up4, cw4, tm):
    T = ddn.shape[0]
    hb = _halo_before(tm, HALO16)

    def body(d_ref, w_ref, up_ref, halo_ref, cw_ref, o_ref):
        dact = _dot_nt(d_ref[...], w_ref[...])
        hid_a, hid_v = _ffn_hidden(up_ref, halo_ref, cw_ref, pl.program_id(0))
        silu, dsilu = _silu_parts(hid_a)
        o_ref[0] = (dact * hid_v * dsilu).astype(BF16)
        o_ref[1] = (dact * silu).astype(BF16)

    return _call(body, name='ffn_dact', grid=(T // tm, 4),
                 in_specs=[pl.BlockSpec((tm, D_MODEL), lambda i, j: (i, 0)),
                           pl.BlockSpec((None, FF_SHARD, D_MODEL), lambda i, j: (j, 0, 0)),
                           pl.BlockSpec((2, None, tm, FF_SHARD), lambda i, j: (0, j, i, 0)),
                           pl.BlockSpec((2, None, HALO16, FF_SHARD), lambda i, j: (0, j, hb(i), 0)),
                           pl.BlockSpec((2, None, 3, FF_SHARD), lambda i, j: (0, j, 0, 0))],
                 out_specs=pl.BlockSpec((2, None, tm, FF_SHARD), lambda i, j: (0, j, i, 0)),
                 out_shape=_sds((2, 4, T, FF_SHARD), BF16), sem=('parallel', 'parallel'))(ddn, wd4, up4, up4, cw4)


def _ffn_dup(dhid8, up8, cw8, tm):
    T = up8.shape[1]
    nb = T // tm
    ha = _halo_after(tm, T, HALO16)

    def body(dh_ref, dha_ref, up_ref, cw_ref, dup_ref, dcw_ref):
        i = pl.program_id(1)

        @pl.when(i == 0)
        def _():
            dcw_ref[...] = jnp.zeros_like(dcw_ref)

        dh = dh_ref[...].astype(F32)
        dup, dh1, dh2 = _conv3_t(dh, jnp.where(i < nb - 1, dha_ref[...].astype(F32), 0.0), cw_ref)
        dup_ref[...] = dup.astype(BF16)
        up = up_ref[...].astype(F32)
        dcw_ref[0:1, :] += _colsum(dh2 * up)
        dcw_ref[1:2, :] += _colsum(dh1 * up)
        dcw_ref[2:3, :] += _colsum(dh * up)

    main = pl.BlockSpec((None, tm, FF_SHARD), lambda j, i: (j, i, 0))
    return _call(body, name='ffn_dup', grid=(N_DEV, nb),
                 in_specs=[main, pl.BlockSpec((None, HALO16, FF_SHARD), lambda j, i: (j, ha(i), 0)), main,
                           pl.BlockSpec((None, 3, FF_SHARD), lambda j, i: (j, 0, 0))],
                 out_specs=[main, pl.BlockSpec((None, 8, FF_SHARD), lambda j, i: (j, 0, 0))],
                 out_shape=[_sds((N_DEV, T, FF_SHARD), BF16), _sds((N_DEV, 8, FF_SHARD))],
                 sem=('parallel', 'arbitrary'))(dhid8, dhid8, up8, cw8)


def _grad_tn(a, b, a_spec, b_spec, groups, m, n, tk, name, ride=None):
    T = a.shape[-2]
    nk = T // tk

    def body(a_ref, b_ref, o_ref, acc_ref):
        k = pl.program_id(1)
        part = _dot_tn(a_ref[...], b_ref[...])

        @pl.when(k == 0)
        def _():
            acc_ref[...] = part

        @pl.when(k > 0)
        def _():
            acc_ref[...] += part

        @pl.when(k == nk - 1)
        def _():
            o_ref[...] = acc_ref[...].astype(BF16)

    return _call(body, name=name, grid=(groups, nk), in_specs=[a_spec, b_spec],
                 out_specs=pl.BlockSpec((None, m, n), lambda g, k: (g, 0, 0)), out_shape=_sds((groups, m, n), BF16),
                 scratch=[pltpu.VMEM((m, n), F32)], sem=('parallel', 'arbitrary'), vmem=VMEM_BIG, ride=ride)(a, b)


def _pre_norm_bwd(dz, dz_spec, w_s, xin, dres, sc, g, tm, name, ride):
    T = xin.shape[0]
    n = w_s.shape[2]

    def body(dz_ref, w_ref, x_ref, dr_ref, sc_ref, g_ref, dx_ref, dsh_ref, dsc_ref, dg_ref):
        i, j = pl.program_id(0), pl.program_id(1)
        part = _dot_nt(dz_ref[...], w_ref[...])

        @pl.when(jnp.logical_and(i == 0, j == 0))
        def _():
            dsh_ref[...] = jnp.zeros_like(dsh_ref)
            dsc_ref[...] = jnp.zeros_like(dsc_ref)
            dg_ref[...] = jnp.zeros_like(dg_ref)

        @pl.when(j == 0)
        def _():
            dx_ref[...] = part

        @pl.when(j > 0)
        def _():
            dx_ref[...] += part

        @pl.when(j == N_DEV - 1)
        def _():
            dh, xv, gv = dx_ref[...], x_ref[...], g_ref[...]
            r = _rsqrt_mean(xv)
            dsh_ref[...] += _colsum(dh)
            dsc_ref[...] += _colsum(dh * (xv * r * gv))
            dxn = dh * (1.0 + sc_ref[...])
            dg_ref[...] += _colsum(dxn * xv * r)
            dx_ref[...] = dr_ref[...] + _norm_bwd(dxn, xv, r, gv)

    row = pl.BlockSpec((tm, D_MODEL), lambda i, j: (i, 0))
    vec = _const((1, D_MODEL))
    return _call(body, name=name, grid=(T // tm, N_DEV),
                 in_specs=[dz_spec, pl.BlockSpec((None, D_MODEL, n), lambda i, j: (j, 0, 0)), row, row, vec, vec],
                 out_specs=[row, vec, vec, vec],
                 out_shape=[_sds((T, D_MODEL)), _sds((1, D_MODEL)), _sds((1, D_MODEL)), _sds((1, D_MODEL))],
                 sem=('arbitrary', 'arbitrary'), vmem=VMEM_BIG, ride=ride)(dz, w_s, xin, dres, sc, g)


def _d_ycat(d_o, w_out, tm):
    T = d_o.shape[0]

    def body(d_ref, w_ref, o_ref):
        o_ref[...] = _dot_nt(d_ref[...], w_ref[...])

    row = pl.BlockSpec((tm, D_MODEL), lambda i: (i, 0))
    return _call(body, name='d_ycat', grid=(T // tm,), in_specs=[row, _const((D_MODEL, D_MODEL))], out_specs=row,
                 out_shape=_sds((T, D_MODEL)), sem=('parallel',))(d_o, w_out)


def _mix_bwd(dycat, yssm, proj, d, glu_w, glu_b, g_ssm, cw, g_conv, avg16, avg64, tm):
    T = yssm.shape[0]
    hb = _halo_before(tm)

    def body(dyc_ref, y_ref, p_ref, ph_ref, d_ref, gw_ref, gb_ref, gs_ref, cw_ref, gc_ref, a16_ref, a64_ref,
             dy_ref, dconv_ref, dbg_ref, z_ref, dlin_ref, acc_ref):
        i = pl.program_id(0)

        @pl.when(i == 0)
        def _():
            acc_ref[...] = jnp.zeros_like(acc_ref)

        u = p_ref[:, 0:D_SSM]
        y = y_ref[...] + d_ref[...] * u
        z, t = _gelu(y)
        gate = _sigmoid(_dot(z.astype(BF16), gw_ref[...]) + gb_ref[...])
        ya = z * gate
        rs = lax.rsqrt(_dot_split(ya * ya, a16_ref[...], 2) + EPS)
        dna = dyc_ref[:, 0:D_SSM]
        acc_ref[1:2, :] += _colsum(dna * ya * rs)
        dya = _head_norm_bwd(dna, ya, rs, gs_ref[...], a16_ref[...])
        dlin = dya * z * gate * (1.0 - gate)
        acc_ref[0:1, :] += _colsum(dlin)
        dlin_b = dlin.astype(BF16)
        dz = dya * gate + _dot_nt(dlin_b, gw_ref[...])
        dy = dz * _gelu_grad(y, t)
        acc_ref[3:4, :] += _colsum(dy * u)
        dy_ref[...] = dy
        z_ref[...] = z.astype(BF16)
        dlin_ref[...] = dlin_b

        bg = p_ref[:, D_SSM:D_SSM + D_CONV]
        cv = p_ref[:, D_SSM + D_CONV:D_SSM + 2 * D_CONV] * p_ref[:, D_SSM + 2 * D_CONV:D_IN_PROJ]
        hv = ph_ref[:, D_SSM + D_CONV:D_SSM + 2 * D_CONV] * ph_ref[:, D_SSM + 2 * D_CONV:D_IN_PROJ]
        hv = jnp.where(i > 0, hv, 0.0)
        conv, cv1, cv2 = _conv3(cv, hv, cw_ref)
        yb = bg * conv
        rsb = lax.rsqrt(_dot_split(yb * yb, a64_ref[...], 2) + EPS)
        dnb = dyc_ref[:, D_SSM:D_MODEL]
        acc_ref[2:3, :] += _colsum(dnb * yb * rsb)
        dyb = _head_norm_bwd(dnb, yb, rsb, gc_ref[...], a64_ref[...])
        dbg_ref[...] = dyb * conv
        dconv = dyb * bg
        dconv_ref[...] = dconv
        acc_ref[4:5, :] += _colsum(dconv * cv2)
        acc_ref[5:6, :] += _colsum(dconv * cv1)
        acc_ref[6:7, :] += _colsum(dconv * cv)

    vec = _const((1, D_SSM))
    sq = _const((D_SSM, D_SSM))
    half = pl.BlockSpec((tm, D_SSM), lambda i: (i, 0))
    return _call(body, name='mix_bwd', grid=(T // tm,),
                 in_specs=[pl.BlockSpec((tm, D_MODEL), lambda i: (i, 0)), half,
                           pl.BlockSpec((tm, D_IN_PROJ), lambda i: (i, 0)),
                           pl.BlockSpec((HALO, D_IN_PROJ), lambda i: (hb(i), 0)), vec, sq, vec, vec,
                           _const((3, D_CONV)), vec, sq, sq],
                 out_specs=[half, half, half, half, half, _const((8, D_SSM))],
                 out_shape=[_sds((T, D_SSM)), _sds((T, D_SSM)), _sds((T, D_SSM)), _sds((T, D_SSM), BF16),
                            _sds((T, D_SSM), BF16), _sds((8, D_SSM))],
                 sem=('arbitrary',), vmem=VMEM_BIG)(dycat, yssm, proj, proj, d, glu_w, glu_b, g_ssm, cw, g_conv,
                                                   avg16, avg64)


def _mix_bwd_proj(dconv, proj, du_ssm, dy, d, dbg, cw, tm):
    T = dy.shape[0]
    nb = T // tm
    ha = _halo_after(tm, T)

    def body(dc_ref, dch_ref, cg_ref, v_ref, du_ref, dy_ref, d_ref, dbg_ref, cw_ref, o_ref):
        i = pl.program_id(0)
        dcv = _conv3_t(dc_ref[...], jnp.where(i < nb - 1, dch_ref[...], 0.0), cw_ref)[0]
        o_ref[:, 0:D_SSM] = (du_ref[...] + dy_ref[...] * d_ref[...]).astype(BF16)
        o_ref[:, D_SSM:D_SSM + D_CONV] = dbg_ref[...].astype(BF16)
        o_ref[:, D_SSM + D_CONV:D_SSM + 2 * D_CONV] = (dcv * v_ref[...]).astype(BF16)
        o_ref[:, D_SSM + 2 * D_CONV:D_IN_PROJ] = (dcv * cg_ref[...]).astype(BF16)

    half = pl.BlockSpec((tm, D_SSM), lambda i: (i, 0))
    return _call(body, name='mix_bwd_proj', grid=(nb,),
                 in_specs=[half, pl.BlockSpec((HALO, D_CONV), lambda i: (ha(i), 0)),
                           pl.BlockSpec((tm, D_CONV), lambda i: (i, 2)), pl.BlockSpec((tm, D_CONV), lambda i: (i, 3)),
                           half, half, _const((1, D_SSM)), half, _const((3, D_CONV))],
                 out_specs=pl.BlockSpec((tm, D_IN_PROJ), lambda i: (i, 0)), out_shape=_sds((T, D_IN_PROJ), BF16),
                 sem=('parallel',))(dconv, dconv, proj, proj, du_ssm, dy, d, dbg, cw)


def _row_tile(rows, cols, slots):
    for cand in (512, 256, 128, 64, 32, 16, 8):
        if rows % cand == 0 and slots * cand * cols * 4 <= (2 << 20):
            return cand
    return rows


def _adamw_math(g, w, m, v):
    m2 = ADAM_B1 * m + (1.0 - ADAM_B1) * g
    v2 = ADAM_B2 * v + (1.0 - ADAM_B2) * (g * g)
    m_hat = m2 / (1.0 - ADAM_B1 ** ADAM_STEP)
    v_hat = v2 / (1.0 - ADAM_B2 ** ADAM_STEP)
    return -ADAM_LR * (m_hat / (jnp.sqrt(v_hat) + ADAM_EPS) + ADAM_WD * w), m2, v2


def _adamw(gslots, w, m, v, name):
    slots, rows, cols = gslots.shape
    tr = _row_tile(rows, cols, slots)

    def body(g_ref, w_ref, m_ref, v_ref, go_ref, d_ref, mo_ref, vo_ref):
        g = g_ref[0].astype(F32)
        for s in range(1, slots):
            g = g + g_ref[s].astype(F32)
        go_ref[...] = g
        d_ref[...], mo_ref[...], vo_ref[...] = _adamw_math(g, w_ref[...], m_ref[...], v_ref[...])

    blk = pl.BlockSpec((tr, cols), lambda i: (i, 0))
    return _call(body, name=name, grid=(rows // tr,),
                 in_specs=[pl.BlockSpec((slots, tr, cols), lambda i: (0, i, 0)), blk, blk, blk],
                 out_specs=[blk] * 4, out_shape=[_sds((rows, cols))] * 4, sem=('parallel',))(gslots, w, m, v)


def _to_scan_rows(a):
    T, n = a.shape
    return a.reshape(SUBLANES, T // SUBLANES, n).transpose(1, 0, 2).reshape(T, n)


def _from_scan_rows(a):
    T, n = a.shape
    return a.reshape(T // SUBLANES, SUBLANES, n).transpose(1, 0, 2).reshape(T, n)


def _expand(a):
    return jnp.repeat(a, SSM_GROUP, axis=1)


def _block_diag_b(bb):
    eye = jnp.eye(N_GROUPS, dtype=bb.dtype)
    return (bb.transpose(0, 2, 1)[:, :, None, :] * eye[:, None, :, None]).reshape(D_SSM, N_STATE)


def _block_diag_c(cc):
    eye = jnp.eye(N_GROUPS, dtype=cc.dtype)
    return (cc.transpose(0, 2, 1)[:, :, None, :] * eye[:, None, :, None]).reshape(N_STATE, D_SSM)


def _diag_blocks(x, chan_major):
    e2 = jnp.eye(2, dtype=x.dtype)
    e4 = jnp.eye(4, dtype=x.dtype)
    if chan_major:
        x = x.reshape(4, 2, 2, 4, SSM_GROUP, 4, SSM_STATE)
        x = x * e2[None, :, :, None, None, None, None] * e4[None, None, None, :, None, :, None]
        return x.sum(axis=(2, 3)).transpose(0, 1, 3, 4, 2).reshape(N_GROUPS, SSM_STATE, SSM_GROUP)
    x = x.reshape(4, 2, 4, SSM_STATE, 2, 4, SSM_GROUP)
    x = x * e2[None, :, None, None, :, None, None] * e4[None, None, :, None, None, :, None]
    return x.sum(axis=(4, 5)).reshape(N_GROUPS, SSM_STATE, SSM_GROUP)


SMALL_LAYOUT = {
    'ssm_b_re': (0, 0, 32, 1024), 'ssm_b_im': (32, 0, 32, 1024), 'ssm_c_re': (64, 0, 32, 1024),
    'ssm_c_im': (96, 0, 32, 1024), 'b_ada': (128, 0, 6, 1024), 'g_pre_mix': (134, 0, 1, 1024),
    'g_post_mix': (135, 0, 1, 1024), 'ssm_lam_re': (136, 0, 2, 1024), 'ssm_lam_im': (138, 0, 2, 1024),
    'ssm_log_step': (140, 0, 1, 32), 'glu_b': (141, 0, 1, 512), 'g_out_ssm': (141, 512, 1, 512),
    'g_out_conv': (142, 0, 1, 512), 'ssm_d': (142, 512, 1, 512), 'g_pre_ffn': (143, 0, 1, 1024),
    'g_post_ffn': (144, 0, 1, 1024)}
SMALL_ROWS = 152
B_ADA_ROW = SMALL_LAYOUT['b_ada'][0]


def _adamw_small(gathered, wts, mom_m, mom_v):
    names = list(SMALL_LAYOUT)
    n = len(names)

    def body(*refs):
        g_ref, ins, outs = refs[0], refs[1:1 + 3 * n], refs[1 + 3 * n:]
        for p, name in enumerate(names):
            r0, c0, rows, cols = SMALL_LAYOUT[name]
            pieces = [(0, rows)] if rows % 8 == 0 else [(r, 1) for r in range(rows)]
            for r, cnt in pieces:
                g = g_ref[0, r0 + r:r0 + r + cnt, c0:c0 + cols]
                for s in range(1, N_DEV):
                    g = g + g_ref[s, r0 + r:r0 + r + cnt, c0:c0 + cols]
                w, m, v = (ins[3 * p + q][r:r + cnt, :] for q in range(3))
                res = (g,) + _adamw_math(g, w, m, v)
                for q in range(4):
                    outs[4 * p + q][r:r + cnt, :] = res[q]

    shapes = [SMALL_LAYOUT[name][2:] for name in names]
    args = [gathered]
    for name, shp in zip(names, shapes):
        args += [wts[name].reshape(shp), mom_m[name].reshape(shp), mom_v[name].reshape(shp)]
    outs = _call(body, name='adamw_small', grid=(1,),
                 in_specs=[_const(gathered.shape)] + [_const(shp) for shp in shapes for _ in range(3)],
                 out_specs=[_const(shp) for shp in shapes for _ in range(4)],
                 out_shape=[_sds(shp) for shp in shapes for _ in range(4)], vmem=VMEM_BIG)(*args)
    res = {}
    for p, name in enumerate(names):
        for q, kind in enumerate(('g', 'd', 'm', 'v')):
            res[kind, name] = outs[4 * p + q].reshape(wts[name].shape)
    return res


def kernel(x, c, w_ada, b_ada, g_pre_mix, g_post_mix, w_in, ssm_lam_re, ssm_lam_im, ssm_log_step, ssm_b_re, ssm_b_im, ssm_c_re, ssm_c_im, ssm_d, glu_w, glu_b, g_out_ssm, conv_w, g_out_conv, w_out, g_pre_ffn, g_post_ffn, w_up, ffn_conv_w, w_down, loss_target, m_w_ada, m_b_ada, m_g_pre_mix, m_g_post_mix, m_w_in, m_ssm_lam_re, m_ssm_lam_im, m_ssm_log_step, m_ssm_b_re, m_ssm_b_im, m_ssm_c_re, m_ssm_c_im, m_ssm_d, m_glu_w, m_glu_b, m_g_out_ssm, m_conv_w, m_g_out_conv, m_w_out, m_g_pre_ffn, m_g_post_ffn, m_w_up, m_ffn_conv_w, m_w_down, v_w_ada, v_b_ada, v_g_pre_mix, v_g_post_mix, v_w_in, v_ssm_lam_re, v_ssm_lam_im, v_ssm_log_step, v_ssm_b_re, v_ssm_b_im, v_ssm_c_re, v_ssm_c_im, v_ssm_d, v_glu_w, v_glu_b, v_g_out_ssm, v_conv_w, v_g_out_conv, v_w_out, v_g_pre_ffn, v_g_post_ffn, v_w_up, v_ffn_conv_w, v_w_down):
    args = dict(locals())
    wts = {n: args[n] for n in WEIGHTS}
    mom_m = {n: args['m_' + n] for n in WEIGHTS}
    mom_v = {n: args['v_' + n] for n in WEIGHTS}
    T = x.shape[1]
    tm = min(512, T)
    tw = min(1024, T)
    me = _me()[3]
    xt, tgt = x[0], loss_target[0]

    (c_all,) = _exchange([c], name='gather_c', scatter=False)
    c_all = c_all.reshape(N_DEV, D_MODEL)
    b_cols = lax.dynamic_slice(b_ada, (0, me * ADA_SHARD), (1, ADA_SHARD))
    mod_cols, c_act = _mod_cols(c_all, w_ada[0], b_cols)
    (mod_all,) = _exchange([mod_cols], name='gather_mod', scatter=False)
    mod = lax.dynamic_slice(mod_all, (0, me, 0), (N_DEV, 1, ADA_SHARD)).reshape(N_MOD, 1, D_MODEL)
    sh1, sc1, gt1, sh2, sc2, gt2 = [mod[k] for k in range(N_MOD)]

    w_in_s, glu_s, w_out_s, conv_s = _exchange(
        [w_in[0].astype(BF16), glu_w[0].astype(BF16), w_out[0].astype(BF16), conv_w[0]], name='gather_weights',
        scatter=False)
    glu_full = glu_s.reshape(D_SSM, D_SSM)
    w_out_full = w_out_s.reshape(D_MODEL, D_MODEL)
    cw_full = conv_s.transpose(1, 0, 2).reshape(3, D_CONV)

    lre_x, lim_x = _expand(ssm_lam_re[0]), _expand(ssm_lam_im[0])
    lst_x = jnp.broadcast_to(ssm_log_step[0][:, None], (N_GROUPS, SSM_STATE * SSM_GROUP))
    b_re_x = ssm_b_re[0].reshape(N_GROUPS, -1)
    b_im_x = ssm_b_im[0].reshape(N_GROUPS, -1)
    ar_x, ai_x, bbr_x, bbi_x = _ssm_prep(lre_x, lim_x, lst_x, b_re_x, b_im_x)
    lam_r = ar_x[:, ::SSM_GROUP].reshape(1, N_STATE)
    lam_i = ai_x[:, ::SSM_GROUP].reshape(1, N_STATE)
    big_b_re = _block_diag_b(bbr_x.reshape(N_GROUPS, SSM_STATE, SSM_GROUP)).astype(BF16)
    big_b_im = _block_diag_b(bbi_x.reshape(N_GROUPS, SSM_STATE, SSM_GROUP)).astype(BF16)
    big_c_re = _block_diag_c(ssm_c_re[0]).astype(BF16)
    big_c_im = _block_diag_c(ssm_c_im[0]).astype(BF16)
    head = jnp.arange(D_SSM)
    avg16 = jnp.where(head[:, None] // SSM_GROUP == head[None, :] // SSM_GROUP, 1.0 / SSM_GROUP, 0.0).astype(BF16)
    hd = D_CONV // CONV_HEADS
    avg64 = jnp.where(head[:, None] // hd == head[None, :] // hd, 1.0 / hd, 0.0).astype(BF16)

    (proj, h1), (w_down_s, ffn_conv_s) = _pre_mix(xt, sc1, sh1, g_pre_mix, w_in_s, tw,
                                                  ([w_down[0].astype(BF16), ffn_conv_w[0]], False))
    wd4 = w_down_s.reshape(4, FF_SHARD, D_MODEL)
    cw4 = ffn_conv_s.reshape(2, 4, 3, FF_SHARD)
    u_perm = _to_scan_rows(proj[:, :D_SSM]).astype(BF16)
    (s_re, s_im, y_perm), (w_up_s,) = _ssm_fwd(u_perm, big_b_re, big_b_im, big_c_re, big_c_im, lam_r, lam_i,
                                               ([w_up[0].astype(BF16)], False))
    yssm = _from_scan_rows(y_perm)
    mix_args = (ssm_d, glu_full, glu_b, g_out_ssm, cw_full, g_out_conv, avg16, avg64)
    ycat = _mix_fwd(yssm, proj, *mix_args, tm)
    o, x1, h2 = _out_proj(ycat, w_out_full, xt, gt1, g_post_mix, g_pre_ffn, sc2, sh2, tm)
    up8 = _ffn_up(h2, w_up_s, tw)
    up4 = up8.reshape(2, 4, T, FF_SHARD)
    act = _ffn_act(up4, cw4, tm)
    dn, dx2, loss_parts = _ffn_down(act, wd4, x1, tgt, gt2, g_post_ffn, tw)
    loss = lax.psum(jnp.sum(loss_parts[:, 0, 0]), ('x', 'y', 'c'))

    got = {}
    ddn, d_gt2, d_g_post_ffn = _post_norm_bwd(dx2, dn, gt2, g_post_ffn, tm, 'ffn_norm_bwd')
    dhid = _ffn_dact(ddn, wd4, up4, cw4, tm)
    g_w_down = _grad_tn(act, ddn, pl.BlockSpec((None, tw, FF_SHARD), lambda g, k: (g, k, 0)),
                        pl.BlockSpec((tw, D_MODEL), lambda g, k: (k, 0)), 4, FF_SHARD, D_MODEL, tw, 'grad_w_down')
    dup8, dcw_ffn = _ffn_dup(dhid.reshape(N_DEV, T, FF_SHARD), up8, ffn_conv_s, tm)
    g_w_up, (got['w_down'],) = _grad_tn(
        h2, dup8, pl.BlockSpec((tw, D_MODEL), lambda g, k: (k, 0)),
        pl.BlockSpec((None, tw, FF_SHARD), lambda g, k: (g, k, 0)), N_DEV, D_MODEL, FF_SHARD, tw, 'grad_w_up',
        ride=([g_w_down.reshape(N_DEV, D_FF // N_DEV, D_MODEL)], True))
    (dx1, d_sh2, d_sc2, d_g_pre_ffn), (got['w_up'], got['ffn_conv_w']) = _pre_norm_bwd(
        dup8, pl.BlockSpec((None, tw, FF_SHARD), lambda i, j: (j, i, 0)), w_up_s, x1, dx2, sc2, g_pre_ffn, tw,
        'ffn_in_bwd', ([g_w_up, dcw_ffn], True))

    d_o, d_gt1, d_g_post_mix = _post_norm_bwd(dx1, o, gt1, g_post_mix, tm, 'mix_norm_bwd')
    g_w_out = _grad_tn(ycat, d_o, pl.BlockSpec((tw, D_MODEL), lambda g, k: (k, 0)),
                       pl.BlockSpec((tw, D_MODEL), lambda g, k: (k, 0)), 1, D_MODEL, D_MODEL, tw, 'grad_w_out')
    dycat = _d_ycat(d_o, w_out_full, tm)
    dy, dconv, dbg, z_b, dlin_b, sums = _mix_bwd(dycat, yssm, proj, *mix_args, tm)
    g_glu_w = _grad_tn(z_b, dlin_b, pl.BlockSpec((tw, D_SSM), lambda g, k: (k, 0)),
                       pl.BlockSpec((tw, D_SSM), lambda g, k: (k, 0)), 1, D_SSM, D_SSM, tw, 'grad_glu_w')
    dy_perm = _to_scan_rows(dy).astype(BF16)
    (du_perm, dbr_blk, dbi_blk, dcr_blk, dci_blk, dar_blk, dai_blk), (got['w_out'], got['glu_w']) = _ssm_bwd(
        dy_perm, u_perm, s_re, s_im, big_b_re, big_b_im, big_c_re, big_c_im, lam_r, lam_i,
        ([g_w_out.reshape(N_DEV, D_MODEL // N_DEV, D_MODEL), g_glu_w.reshape(N_DEV, D_SSM // N_DEV, D_SSM)], True))
    du_ssm = _from_scan_rows(du_perm)
    dproj = _mix_bwd_proj(dconv, proj, du_ssm, dy, ssm_d, dbg, cw_full, tm)
    g_w_in = _grad_tn(h1, dproj, pl.BlockSpec((tw, D_MODEL), lambda g, k: (k, 0)),
                      pl.BlockSpec((tw, IN_SHARD), lambda g, k: (k, g)), N_DEV, D_MODEL, IN_SHARD, tw, 'grad_w_in')
    g_conv_slots = jnp.concatenate([sums[4:7], jnp.zeros((5, D_CONV), F32)]).reshape(
        8, N_DEV, D_CONV // N_DEV).transpose(1, 0, 2)
    (grad_x, d_sh1, d_sc1, d_g_pre_mix), (got['w_in'], got['conv_w']) = _pre_norm_bwd(
        dproj, pl.BlockSpec((tw, IN_SHARD), lambda i, j: (i, j)), w_in_s, xt, dx1, sc1, g_pre_mix, tw, 'mix_in_bwd',
        ([g_w_in, g_conv_slots], True))

    dbb_re = _diag_blocks(dbr_blk, True).reshape(N_GROUPS, -1)
    dbb_im = _diag_blocks(dbi_blk, True).reshape(N_GROUPS, -1)
    d_c_re = _diag_blocks(dcr_blk, False).transpose(0, 2, 1)
    d_c_im = _diag_blocks(dci_blk, False).transpose(0, 2, 1)
    lane = jnp.arange(SSM_STATE * SSM_GROUP)
    seg = jnp.where(lane[:, None] // SSM_GROUP == lane[None, :] // SSM_GROUP, 1.0, 0.0).astype(BF16)
    d_b_re_x, d_b_im_x, d_lre_x, d_lim_x, d_lst = _ssm_prep_bwd(
        lre_x, lim_x, lst_x, b_re_x, b_im_x, dbb_re, dbb_im, _expand(dar_blk.reshape(N_GROUPS, SSM_STATE)),
        _expand(dai_blk.reshape(N_GROUPS, SSM_STATE)), seg)

    row = lambda a: a.reshape(-1, PACK_COLS)
    small_pack = jnp.concatenate([
        d_b_re_x, d_b_im_x, row(d_c_re), row(d_c_im), d_sh1, d_sc1, d_gt1, d_sh2, d_sc2, d_gt2, d_g_pre_mix,
        d_g_post_mix, row(d_lre_x[:, ::SSM_GROUP]), row(d_lim_x[:, ::SSM_GROUP]),
        jnp.pad(d_lst.reshape(1, N_GROUPS), ((0, 0), (0, PACK_COLS - N_GROUPS))), row(sums[0:4]), d_g_pre_ffn,
        d_g_post_ffn, jnp.zeros((SMALL_ROWS - 145, PACK_COLS), F32)])
    (small_all,) = _exchange([small_pack], name='gather_small_grads', scatter=False)
    res = _adamw_small(small_all, wts, mom_m, mom_v)

    dmod_all = small_all[:, B_ADA_ROW:B_ADA_ROW + N_MOD, :].reshape(N_DEV, N_MOD * D_MODEL)
    dmod_cols = lax.dynamic_slice(dmod_all, (0, me * ADA_SHARD), (N_DEV, ADA_SHARD))
    g_w_ada = _grad_w_ada(c_act.T, dmod_cols)

    for n, slots in got.items():
        if n in ('conv_w', 'ffn_conv_w'):
            slots = slots[:, :3, :]
        outs = _adamw(slots, wts[n][0], mom_m[n][0], mom_v[n][0], 'adamw_' + n)
        for kind, val in zip(('g', 'd', 'm', 'v'), outs):
            res[kind, n] = val[None]
    outs = _adamw(g_w_ada[None], w_ada[0], m_w_ada[0], v_w_ada[0], 'adamw_w_ada')
    for kind, val in zip(('g', 'd', 'm', 'v'), outs):
        res[kind, 'w_ada'] = val[None]

    return (loss, grad_x[None], *[res['g', n] for n in WEIGHTS], *[res['d', n] for n in WEIGHTS],
            *[res['m', n] for n in WEIGHTS], *[res['v', n] for n in WEIGHTS])
```

```python
import math

import jax
import jax.numpy as jnp
from jax import lax
from jax.experimental import pallas as pl
from jax.experimental.pallas import tpu as pltpu

F32, BF16 = jnp.float32, jnp.bfloat16

D_MODEL = 1024
D_SSM = 512
D_CONV = 512
SSM_GROUP = 16
N_GROUPS = 32
SSM_STATE = 64
N_STATE = N_GROUPS * SSM_STATE
CONV_HEADS = 8
D_FF = 2816
N_MOD = 6
D_IN_PROJ = D_SSM + 3 * D_CONV
N_DEV = 8
FF_SHARD = 2 * D_FF // N_DEV
IN_SHARD = D_IN_PROJ // N_DEV
ADA_SHARD = N_MOD * D_MODEL // N_DEV
EPS = 1e-6
LAMBDA_RE_MAX = -1e-4
ADAM_LR, ADAM_B1, ADAM_B2, ADAM_EPS, ADAM_WD, ADAM_STEP = 0.001, 0.9, 0.999, 1e-08, 0.01, 10
GELU_C = math.sqrt(2.0 / math.pi)
GELU_A = 0.044715

SUBLANES = 8
HALO = 8
HALO16 = 16
SCAN_UNROLL = 8
STATE_BLOCK = 256
CHAN_BLOCK = 128
VMEM_BIG = 48 << 20

WEIGHTS = ['w_ada', 'b_ada', 'g_pre_mix', 'g_post_mix', 'w_in', 'ssm_lam_re', 'ssm_lam_im', 'ssm_log_step',
           'ssm_b_re', 'ssm_b_im', 'ssm_c_re', 'ssm_c_im', 'ssm_d', 'glu_w', 'glu_b', 'g_out_ssm', 'conv_w',
           'g_out_conv', 'w_out', 'g_pre_ffn', 'g_post_ffn', 'w_up', 'ffn_conv_w', 'w_down']
SHARDED = ('w_ada', 'w_in', 'glu_w', 'conv_w', 'w_out', 'w_up', 'ffn_conv_w', 'w_down')
PACK_COLS = 1024


def _call(body, *, name, grid, in_specs, out_specs, out_shape, scratch=(), sem=None, vmem=None, ride=None):
    params = {}
    if vmem is not None:
        params['vmem_limit_bytes'] = vmem
    if ride is None:
        if sem is not None:
            params['dimension_semantics'] = sem
        return pl.pallas_call(body, name=name, grid=grid, in_specs=in_specs, out_specs=out_specs,
                              out_shape=out_shape, scratch_shapes=list(scratch),
                              compiler_params=pltpu.CompilerParams(**params))
    arrs, scatter = ride
    single = not isinstance(out_shape, (list, tuple))
    out_shape_l = [out_shape] if single else list(out_shape)
    out_specs_l = [out_specs] if single else list(out_specs)
    n, n_in, n_out, n_scr = len(arrs), len(in_specs), len(out_shape_l), len(scratch)
    any_spec = pl.BlockSpec(memory_space=pl.ANY)
    params['dimension_semantics'] = ('arbitrary',) * len(grid)

    def carried(*refs):
        ins, rin = refs[:n_in], refs[n_in:n_in + n]
        outs, rout = refs[n_in + n:n_in + n + n_out], refs[n_in + n + n_out:n_in + 2 * n + n_out]
        scr, sems = refs[n_in + 2 * n + n_out:n_in + 2 * n + n_out + n_scr], refs[n_in + 2 * n + n_out + n_scr:]
        first = pl.program_id(0) == 0
        last = pl.program_id(0) == grid[0] - 1
        for ax in range(1, len(grid)):
            first = jnp.logical_and(first, pl.program_id(ax) == 0)
            last = jnp.logical_and(last, pl.program_id(ax) == grid[ax] - 1)

        @pl.when(first)
        def _():
            _exchange_start(rin, rout, sems, scatter)

        body(*ins, *outs, *scr)

        @pl.when(last)
        def _():
            _exchange_wait(rin, rout, sems, scatter)

    call = pl.pallas_call(carried, name=name, grid=grid, in_specs=list(in_specs) + [any_spec] * n,
                          out_specs=out_specs_l + [any_spec] * n,
                          out_shape=out_shape_l + _exchange_shapes(arrs, scatter),
                          scratch_shapes=list(scratch) + _exchange_sems(n),
                          compiler_params=pltpu.CompilerParams(**params))

    def run(*args):
        res = call(*args, *arrs)
        own = res[0] if single else list(res[:n_out])
        return own, list(res[n_out:])

    return run


def _const(shape):
    nd = len(shape)
    return pl.BlockSpec(shape, lambda *_: (0,) * nd)


def _sds(shape, dtype=F32):
    return jax.ShapeDtypeStruct(shape, dtype)


def _dot(a, b):
    return jnp.dot(a, b, preferred_element_type=F32)


def _dot_nt(a, b):
    return lax.dot_general(a, b, (((1,), (1,)), ((), ())), preferred_element_type=F32)


def _dot_tn(a, b):
    return lax.dot_general(a, b, (((0,), (0,)), ((), ())), preferred_element_type=F32)


def _dot_split(x, mat, parts):
    acc = None
    rem = x
    for _ in range(parts):
        piece = rem.astype(BF16)
        rem = rem - piece.astype(F32)
        term = _dot(piece, mat)
        acc = term if acc is None else acc + term
    return acc


def _sigmoid(x):
    return 1.0 / (1.0 + jnp.exp(-x))


def _gelu(x):
    t = jnp.tanh(GELU_C * (x + GELU_A * x * x * x))
    return 0.5 * x * (1.0 + t), t


def _gelu_grad(x, t):
    return 0.5 * (1.0 + t) + 0.5 * x * (1.0 - t * t) * GELU_C * (1.0 + 3.0 * GELU_A * x * x)


def _rsqrt_mean(x):
    return lax.rsqrt(jnp.mean(x * x, axis=-1, keepdims=True) + EPS)


def _colsum(x):
    return jnp.sum(x, axis=0, keepdims=True)


def _shifts_down(x, halo):
    ext = jnp.concatenate([halo, x], axis=0)
    return pltpu.roll(ext, 1, 0)[halo.shape[0]:], pltpu.roll(ext, 2, 0)[halo.shape[0]:]


def _shifts_up(x, halo):
    n = x.shape[0]
    ext = jnp.concatenate([x, halo], axis=0)
    total = ext.shape[0]
    return pltpu.roll(ext, total - 1, 0)[:n], pltpu.roll(ext, total - 2, 0)[:n]


def _conv3(x, halo, w_ref):
    x1, x2 = _shifts_down(x, halo)
    return w_ref[0:1, :] * x2 + w_ref[1:2, :] * x1 + w_ref[2:3, :] * x, x1, x2


def _conv3_t(g, halo, w_ref):
    g1, g2 = _shifts_up(g, halo)
    return w_ref[2:3, :] * g + w_ref[1:2, :] * g1 + w_ref[0:1, :] * g2, g1, g2


def _silu_parts(x):
    s = _sigmoid(x)
    return x * s, s * (1.0 + x * (1.0 - s))


def _norm_bwd(dn, x, r, g):
    gd = g * dn
    return r * gd - x * (r * r * r) * jnp.mean(gd * x, axis=-1, keepdims=True)


def _head_norm_bwd(dn, y, rs, g, avg):
    gd = g * dn
    return rs * gd - y * (rs * rs * rs) * _dot_split(gd * y, avg, 2)


def _me():
    x, y, c = lax.axis_index('x'), lax.axis_index('y'), lax.axis_index('c')
    return x, y, c, 4 * x + 2 * y + c


def _peer(k):
    x, y, c, _ = _me()
    px = 1 - x if k & 4 else x
    py = 1 - y if k & 2 else y
    pc = 1 - c if k & 1 else c
    return (px, py, pc), 4 * px + 2 * py + pc


SIBLING = 1
OTHER_CHIPS = (2, 4, 6)


def _remote(src, dst, sems, a, k, dev):
    return pltpu.make_async_remote_copy(src_ref=src, dst_ref=dst, send_sem=sems[0].at[a, k - 1],
                                        recv_sem=sems[1].at[a, k - 1], device_id=dev,
                                        device_id_type=pl.DeviceIdType.MESH)


def _exchange_copies(ins, outs, sems, scatter):
    me = _me()[3]
    local, first, relay, arrivals = [], [], [], []
    for a in range(len(ins)):
        src = ins[a].at[me] if scatter else ins[a]
        local.append(pltpu.make_async_copy(src, outs[a].at[me], sems[2].at[a]))
        for k in range(1, N_DEV):
            dev, idx = _peer(k)
            landed = _remote(src, outs[a].at[idx], sems, a, k, dev)
            if scatter:
                first.append(_remote(ins[a].at[idx], outs[a].at[me], sems, a, k, dev))
                arrivals.append(landed)
            elif k == SIBLING:
                first.append(_remote(src, outs[a].at[me], sems, a, k, dev))
                arrivals.append(landed)
            elif k in OTHER_CHIPS:
                first.append(_remote(src, outs[a].at[me], sems, a, k, dev))
                sib, _ = _peer(SIBLING)
                relay.append((landed, _remote(outs[a].at[idx], outs[a].at[idx], sems, a, k | SIBLING, sib)))
            else:
                arrivals.append(landed)
    return local, first, relay, arrivals


def _exchange_start(ins, outs, sems, scatter):
    local, first, _, _ = _exchange_copies(ins, outs, sems, scatter)
    for cp in local + first:
        cp.start()


def _exchange_wait(ins, outs, sems, scatter):
    local, first, relay, arrivals = _exchange_copies(ins, outs, sems, scatter)
    for landed, forward in relay:
        landed.wait_recv()
        forward.start()
    for cp in arrivals:
        cp.wait_recv()
    for cp in first + [forward for _, forward in relay]:
        cp.wait_send()
    for cp in local:
        cp.wait()


def _exchange_shapes(arrs, scatter):
    return [_sds(a.shape if scatter else (N_DEV,) + a.shape, a.dtype) for a in arrs]


def _exchange_sems(n):
    return [pltpu.SemaphoreType.DMA((n, N_DEV - 1)), pltpu.SemaphoreType.DMA((n, N_DEV - 1)),
            pltpu.SemaphoreType.DMA((n,))]


def _exchange(arrs, *, name, scatter):
    n = len(arrs)

    def body(*refs):
        _exchange_start(refs[:n], refs[n:2 * n], refs[2 * n:], scatter)
        _exchange_wait(refs[:n], refs[n:2 * n], refs[2 * n:], scatter)

    any_spec = pl.BlockSpec(memory_space=pl.ANY)
    outs = pl.pallas_call(body, name=name, out_shape=_exchange_shapes(arrs, scatter), in_specs=[any_spec] * n,
                          out_specs=[any_spec] * n, scratch_shapes=_exchange_sems(n))(*arrs)
    return list(outs)


def _mod_cols(c_all, w_ada, b_cols):
    def body(c_ref, w_ref, b_ref, mod_ref, act_ref):
        c = c_ref[...]
        act = c * _sigmoid(c)
        act_ref[...] = act
        mod_ref[...] = _dot(act.astype(BF16), w_ref[...].astype(BF16)) + b_ref[...]

    return _call(body, name='mod_cols', grid=(1,),
                 in_specs=[_const(c_all.shape), _const(w_ada.shape), _const(b_cols.shape)],
                 out_specs=[_const((N_DEV, ADA_SHARD)), _const(c_all.shape)],
                 out_shape=[_sds((N_DEV, ADA_SHARD)), _sds(c_all.shape)], vmem=VMEM_BIG)(c_all, w_ada, b_cols)


def _grad_w_ada(act_t, dmod_cols):
    def body(a_ref, d_ref, o_ref):
        o_ref[...] = _dot(a_ref[...], d_ref[...])

    return _call(body, name='grad_w_ada', grid=(1,), in_specs=[_const(act_t.shape), _const(dmod_cols.shape)],
                 out_specs=_const((D_MODEL, ADA_SHARD)), out_shape=_sds((D_MODEL, ADA_SHARD)),
                 vmem=VMEM_BIG)(act_t, dmod_cols)


def _pre_mix(x, sc, sh, g, w_s, tm, ride):
    T = x.shape[0]

    def body(x_ref, sc_ref, sh_ref, g_ref, w_ref, proj_ref, h_ref):
        @pl.when(pl.program_id(1) == 0)
        def _():
            xv = x_ref[...]
            h_ref[...] = ((xv * _rsqrt_mean(xv) * g_ref[...]) * (1.0 + sc_ref[...]) + sh_ref[...]).astype(BF16)

        proj_ref[...] = _dot(h_ref[...], w_ref[...])

    row = pl.BlockSpec((tm, D_MODEL), lambda i, j: (i, 0))
    vec = _const((1, D_MODEL))
    return _call(body, name='pre_mix', grid=(T // tm, N_DEV),
                 in_specs=[row, vec, vec, vec, pl.BlockSpec((None, D_MODEL, IN_SHARD), lambda i, j: (j, 0, 0))],
                 out_specs=[pl.BlockSpec((tm, IN_SHARD), lambda i, j: (i, j)), row],
                 out_shape=[_sds((T, D_IN_PROJ)), _sds((T, D_MODEL), BF16)],
                 sem=('parallel', 'arbitrary'), ride=ride)(x, sc, sh, g, w_s)


def _halo_before(tm, rows=HALO):
    return lambda i: jnp.maximum(i * (tm // rows) - 1, 0)


def _halo_after(tm, T, rows=HALO):
    return lambda i: jnp.minimum((i + 1) * (tm // rows), T // rows - 1)


def _mix_fwd(yssm, proj, d, glu_w, glu_b, g_ssm, cw, g_conv, avg16, avg64, tm):
    T = yssm.shape[0]
    hb = _halo_before(tm)

    def body(y_ref, p_ref, ph_ref, d_ref, gw_ref, gb_ref, gs_ref, cw_ref, gc_ref, a16_ref, a64_ref, o_ref):
        i = pl.program_id(0)
        u = p_ref[:, 0:D_SSM]
        y = y_ref[...] + d_ref[...] * u
        z, _ = _gelu(y)
        gate = _sigmoid(_dot(z.astype(BF16), gw_ref[...]) + gb_ref[...])
        ya = z * gate
        rs = lax.rsqrt(_dot_split(ya * ya, a16_ref[...], 2) + EPS)
        o_ref[:, 0:D_SSM] = (ya * rs * gs_ref[...]).astype(BF16)
        bg = p_ref[:, D_SSM:D_SSM + D_CONV]
        cv = p_ref[:, D_SSM + D_CONV:D_SSM + 2 * D_CONV] * p_ref[:, D_SSM + 2 * D_CONV:D_IN_PROJ]
        hv = ph_ref[:, D_SSM + D_CONV:D_SSM + 2 * D_CONV] * ph_ref[:, D_SSM + 2 * D_CONV:D_IN_PROJ]
        hv = jnp.where(i > 0, hv, 0.0)
        conv, _, _ = _conv3(cv, hv, cw_ref)
        yb = bg * conv
        rsb = lax.rsqrt(_dot_split(yb * yb, a64_ref[...], 2) + EPS)
        o_ref[:, D_SSM:D_MODEL] = (yb * rsb * gc_ref[...]).astype(BF16)

    vec = _const((1, D_SSM))
    sq = _const((D_SSM, D_SSM))
    return _call(body, name='mix_fwd', grid=(T // tm,),
                 in_specs=[pl.BlockSpec((tm, D_SSM), lambda i: (i, 0)), pl.BlockSpec((tm, D_IN_PROJ), lambda i: (i, 0)),
                           pl.BlockSpec((HALO, D_IN_PROJ), lambda i: (hb(i), 0)), vec, sq, vec, vec,
                           _const((3, D_CONV)), vec, sq, sq],
                 out_specs=pl.BlockSpec((tm, D_MODEL), lambda i: (i, 0)), out_shape=_sds((T, D_MODEL), BF16),
                 sem=('parallel',), vmem=VMEM_BIG)(yssm, proj, proj, d, glu_w, glu_b, g_ssm, cw, g_conv, avg16, avg64)


def _out_proj(ycat, w_out, x, gt, g_post, g_pre, sc, sh, tm):
    T = x.shape[0]

    def body(y_ref, w_ref, x_ref, gt_ref, gp_ref, g2_ref, sc_ref, sh_ref, o_ref, x1_ref, h_ref):
        o = _dot(y_ref[...], w_ref[...])
        o_ref[...] = o
        x1 = x_ref[...] + gt_ref[...] * (o * _rsqrt_mean(o) * gp_ref[...])
        x1_ref[...] = x1
        h_ref[...] = ((x1 * _rsqrt_mean(x1) * g2_ref[...]) * (1.0 + sc_ref[...]) + sh_ref[...]).astype(BF16)

    row = pl.BlockSpec((tm, D_MODEL), lambda i: (i, 0))
    vec = _const((1, D_MODEL))
    return _call(body, name='out_proj', grid=(T // tm,),
                 in_specs=[row, _const((D_MODEL, D_MODEL)), row, vec, vec, vec, vec, vec],
                 out_specs=[row, row, row],
                 out_shape=[_sds((T, D_MODEL)), _sds((T, D_MODEL)), _sds((T, D_MODEL), BF16)],
                 sem=('parallel',), vmem=VMEM_BIG)(ycat, w_out, x, gt, g_post, g_pre, sc, sh)


def _ffn_up(h2, w_s, tm):
    T = h2.shape[0]

    def body(h_ref, w_ref, o_ref):
        o_ref[...] = _dot(h_ref[...], w_ref[...]).astype(BF16)

    return _call(body, name='ffn_up', grid=(T // tm, N_DEV),
                 in_specs=[pl.BlockSpec((tm, D_MODEL), lambda i, j: (i, 0)),
                           pl.BlockSpec((None, D_MODEL, FF_SHARD), lambda i, j: (j, 0, 0))],
                 out_specs=pl.BlockSpec((None, tm, FF_SHARD), lambda i, j: (j, i, 0)),
                 out_shape=_sds((N_DEV, T, FF_SHARD), BF16), sem=('parallel', 'parallel'))(h2, w_s)


def _ffn_hidden(up_ref, halo_ref, cw_ref, i):
    hid = []
    for part in range(2):
        halo = jnp.where(i > 0, halo_ref[part].astype(F32), 0.0)
        hid.append(_conv3(up_ref[part].astype(F32), halo, cw_ref.at[part])[0])
    return hid


def _ffn_act(up4, cw4, tm):
    T = up4.shape[2]
    hb = _halo_before(tm, HALO16)

    def body(up_ref, halo_ref, cw_ref, o_ref):
        hid_a, hid_v = _ffn_hidden(up_ref, halo_ref, cw_ref, pl.program_id(0))
        o_ref[...] = (_silu_parts(hid_a)[0] * hid_v).astype(BF16)

    return _call(body, name='ffn_act', grid=(T // tm, 4),
                 in_specs=[pl.BlockSpec((2, None, tm, FF_SHARD), lambda i, j: (0, j, i, 0)),
                           pl.BlockSpec((2, None, HALO16, FF_SHARD), lambda i, j: (0, j, hb(i), 0)),
                           pl.BlockSpec((2, None, 3, FF_SHARD), lambda i, j: (0, j, 0, 0))],
                 out_specs=pl.BlockSpec((None, tm, FF_SHARD), lambda i, j: (j, i, 0)),
                 out_shape=_sds((4, T, FF_SHARD), BF16), sem=('parallel', 'parallel'))(up4, up4, cw4)


def _ffn_down(act, wd4, x1, tgt, gt, g_post, tm):
    T = x1.shape[0]
    nb = T // tm

    def body(a_ref, w_ref, x1_ref, t_ref, gt_ref, g_ref, dn_ref, dx_ref, loss_ref):
        j = pl.program_id(1)
        part = _dot(a_ref[...], w_ref[...])

        @pl.when(j == 0)
        def _():
            dn_ref[...] = part

        @pl.when(j > 0)
        def _():
            dn_ref[...] += part

        @pl.when(j == 3)
        def _():
            dn = dn_ref[...]
            x2 = x1_ref[...] + gt_ref[...] * (dn * _rsqrt_mean(dn) * g_ref[...])
            err = x2 - t_ref[...]
            dx_ref[...] = err * (1.0 / D_MODEL)
            tot = jnp.sum(jnp.sum(err * err, axis=1, keepdims=True), axis=0, keepdims=True) * (0.5 / D_MODEL)
            loss_ref[...] = jnp.broadcast_to(tot, (8, 128))

    row = pl.BlockSpec((tm, D_MODEL), lambda i, j: (i, 0))
    vec = _const((1, D_MODEL))
    return _call(body, name='ffn_down', grid=(nb, 4),
                 in_specs=[pl.BlockSpec((None, tm, FF_SHARD), lambda i, j: (j, i, 0)),
                           pl.BlockSpec((None, FF_SHARD, D_MODEL), lambda i, j: (j, 0, 0)), row, row, vec, vec],
                 out_specs=[row, row, pl.BlockSpec((None, 8, 128), lambda i, j: (i, 0, 0))],
                 out_shape=[_sds((T, D_MODEL)), _sds((T, D_MODEL)), _sds((nb, 8, 128))],
                 sem=('parallel', 'arbitrary'), vmem=VMEM_BIG)(act, wd4, x1, tgt, gt, g_post)


def _ssm_prep(lre, lim, lst, b_re, b_im):
    def body(lre_ref, lim_ref, lst_ref, br_ref, bi_ref, ar_ref, ai_ref, bbr_ref, bbi_ref):
        ar, ai, qr, qi = _zoh(lre_ref[...], lim_ref[...], lst_ref[...])[:4]
        ar_ref[...] = ar
        ai_ref[...] = ai
        bbr_ref[...] = qr * br_ref[...] - qi * bi_ref[...]
        bbi_ref[...] = qr * bi_ref[...] + qi * br_ref[...]

    shp = lre.shape
    return _call(body, name='ssm_prep', grid=(1,), in_specs=[_const(shp)] * 5, out_specs=[_const(shp)] * 4,
                 out_shape=[_sds(shp)] * 4)(lre, lim, lst, b_re, b_im)


def _zoh(lre, lim, lst):
    lr = jnp.minimum(lre, LAMBDA_RE_MAX)
    st = jnp.exp(lst)
    mag = jnp.exp(lr * st)
    ar = mag * jnp.cos(lim * st)
    ai = mag * jnp.sin(lim * st)
    den = lr * lr + lim * lim
    qr = ((ar - 1.0) * lr + ai * lim) / den
    qi = (ai * lr - (ar - 1.0) * lim) / den
    return ar, ai, qr, qi, lr, st, den


def _ssm_prep_bwd(lre, lim, lst, b_re, b_im, dbbr, dbbi, dar, dai, seg):
    def body(lre_ref, lim_ref, lst_ref, br_ref, bi_ref, dbbr_ref, dbbi_ref, dar_ref, dai_ref, seg_ref,
             dbr_ref, dbi_ref, dlre_ref, dlim_ref, dlst_ref):
        lre_v = lre_ref[...]
        li = lim_ref[...]
        ar, ai, qr, qi, lr, st, den = _zoh(lre_v, li, lst_ref[...])
        br, bi, gbr, gbi = br_ref[...], bi_ref[...], dbbr_ref[...], dbbi_ref[...]
        dbr_ref[...] = qr * gbr + qi * gbi
        dbi_ref[...] = qr * gbi - qi * gbr
        gqr = _dot_split(br * gbr + bi * gbi, seg_ref[...], 3)
        gqi = _dot_split(br * gbi - bi * gbr, seg_ref[...], 3)
        ir, ii = lr / den, -li / den
        gar = dar_ref[...] + ir * gqr + ii * gqi
        gai = dai_ref[...] + ir * gqi - ii * gqr
        tr, ti = qr * ir - qi * ii, qr * ii + qi * ir
        glr = -(tr * gqr + ti * gqi)
        gli = -(tr * gqi - ti * gqr)
        gzr = ar * gar + ai * gai
        gzi = ar * gai - ai * gar
        glr = glr + st * gzr
        gli = gli + st * gzi
        gst = (lr * gzr + li * gzi) * st
        dlre_ref[...] = jnp.where(lre_v < LAMBDA_RE_MAX, glr, 0.0)
        dlim_ref[...] = gli
        dlst_ref[...] = jnp.sum(gst, axis=1, keepdims=True) * (1.0 / SSM_GROUP)

    shp = lre.shape
    return _call(body, name='ssm_prep_bwd', grid=(1,), in_specs=[_const(shp)] * 9 + [_const(seg.shape)],
                 out_specs=[_const(shp)] * 4 + [_const((N_GROUPS, 1))],
                 out_shape=[_sds(shp)] * 4 + [_sds((N_GROUPS, 1))], vmem=VMEM_BIG)(
                     lre, lim, lst, b_re, b_im, dbbr, dbbi, dar, dai, seg)


def _scan_specs(T):
    half = lambda cb: cb // 2
    return dict(
        chan=pl.BlockSpec((T, CHAN_BLOCK), lambda cb: (0, half(cb))),
        state=pl.BlockSpec((T, STATE_BLOCK), lambda cb: (0, cb)),
        b=pl.BlockSpec((CHAN_BLOCK, STATE_BLOCK), lambda cb: (half(cb), cb)),
        c=pl.BlockSpec((STATE_BLOCK, CHAN_BLOCK), lambda cb: (cb, half(cb))),
        lam=pl.BlockSpec((1, STATE_BLOCK), lambda cb: (0, cb)),
    )


def _complex_power(re, im, n):
    out = None
    while True:
        if n & 1:
            out = (re, im) if out is None else (out[0] * re - out[1] * im, out[0] * im + out[1] * re)
        n >>= 1
        if n == 0:
            return out
        re, im = re * re - im * im, 2.0 * re * im


def _rows8(i):
    return pl.ds(pl.multiple_of(i * SUBLANES, SUBLANES), SUBLANES)


def _ssm_fwd(u_perm, b_re, b_im, c_re, c_im, lam_r, lam_i, ride):
    T = u_perm.shape[0]
    ls = T // SUBLANES
    rc = min(512, T)
    sp = _scan_specs(T)

    def body(u_ref, bre_ref, bim_ref, cre_ref, cim_ref, lr_ref, li_ref, sre_ref, sim_ref, y_ref):
        cb = pl.program_id(0)
        for c in range(T // rc):
            rows = pl.ds(c * rc, rc)
            sre_ref[rows, :] = _dot(u_ref[rows, :], bre_ref[...])
            sim_ref[rows, :] = _dot(u_ref[rows, :], bim_ref[...])
        shp = (SUBLANES, STATE_BLOCK)
        lr = jnp.broadcast_to(lr_ref[...], shp)
        li = jnp.broadcast_to(li_ref[...], shp)
        zero = jnp.zeros(shp, F32)

        def step(i, carry):
            sr, si = carry
            rows = _rows8(i)
            nr = lr * sr - li * si + sre_ref[rows, :]
            ni = lr * si + li * sr + sim_ref[rows, :]
            sre_ref[rows, :] = nr
            sim_ref[rows, :] = ni
            return nr, ni

        fr, fi = lax.fori_loop(0, ls, step, (zero, zero), unroll=SCAN_UNROLL)
        pr, pi_ = _complex_power(lr, li, ls)
        row = lax.broadcasted_iota(jnp.int32, shp, 0)
        ir, ii = zero, zero
        for _ in range(SUBLANES - 1):
            er = fr + pr * ir - pi_ * ii
            ei = fi + pr * ii + pi_ * ir
            ir = jnp.where(row == 0, 0.0, pltpu.roll(er, 1, 0))
            ii = jnp.where(row == 0, 0.0, pltpu.roll(ei, 1, 0))

        def fix(i, carry):
            cr, ci = carry
            rows = _rows8(i)
            nr = lr * cr - li * ci
            ni = lr * ci + li * cr
            sre_ref[rows, :] += nr
            sim_ref[rows, :] += ni
            return nr, ni

        lax.fori_loop(0, ls, fix, (ir, ii), unroll=SCAN_UNROLL)
        for c in range(T // rc):
            rows = pl.ds(c * rc, rc)
            yc = _dot(sre_ref[rows, :].astype(BF16), cre_ref[...]) - _dot(sim_ref[rows, :].astype(BF16), cim_ref[...])

            @pl.when(cb % 2 == 0)
            def _():
                y_ref[rows, :] = yc

            @pl.when(cb % 2 == 1)
            def _():
                y_ref[rows, :] += yc

    return _call(body, name='ssm_fwd', grid=(N_STATE // STATE_BLOCK,),
                 in_specs=[sp['chan'], sp['b'], sp['b'], sp['c'], sp['c'], sp['lam'], sp['lam']],
                 out_specs=[sp['state'], sp['state'], sp['chan']],
                 out_shape=[_sds((T, N_STATE)), _sds((T, N_STATE)), _sds((T, D_SSM))],
                 sem=('arbitrary',), vmem=VMEM_BIG, ride=ride)(u_perm, b_re, b_im, c_re, c_im, lam_r, lam_i)


def _ssm_bwd(dy_perm, u_perm, s_re, s_im, b_re, b_im, c_re, c_im, lam_r, lam_i, ride):
    T = u_perm.shape[0]
    ls = T // SUBLANES
    rc = min(512, T)
    sp = _scan_specs(T)
    ncb = N_STATE // STATE_BLOCK

    def body(dy_ref, u_ref, sre_ref, sim_ref, bre_ref, bim_ref, cre_ref, cim_ref, lr_ref, li_ref,
             du_ref, dbr_ref, dbi_ref, dcr_ref, dci_ref, dar_ref, dai_ref, gre_ref, gim_ref):
        cb = pl.program_id(0)
        for c in range(T // rc):
            rows = pl.ds(c * rc, rc)
            gre_ref[rows, :] = _dot_nt(dy_ref[rows, :], cre_ref[...])
            gim_ref[rows, :] = -_dot_nt(dy_ref[rows, :], cim_ref[...])
        shp = (SUBLANES, STATE_BLOCK)
        lr = jnp.broadcast_to(lr_ref[...], shp)
        li = jnp.broadcast_to(li_ref[...], shp)
        zero = jnp.zeros(shp, F32)

        def step(k, carry):
            gr, gi = carry
            rows = _rows8(ls - 1 - k)
            nr = lr * gr + li * gi + gre_ref[rows, :]
            ni = lr * gi - li * gr + gim_ref[rows, :]
            gre_ref[rows, :] = nr
            gim_ref[rows, :] = ni
            return nr, ni

        fr, fi = lax.fori_loop(0, ls, step, (zero, zero), unroll=SCAN_UNROLL)
        pr, pi_ = _complex_power(lr, -li, ls)
        row = lax.broadcasted_iota(jnp.int32, shp, 0)
        cr, ci = zero, zero
        for _ in range(SUBLANES - 1):
            er = fr + pr * cr - pi_ * ci
            ei = fi + pr * ci + pi_ * cr
            cr = jnp.where(row == SUBLANES - 1, 0.0, pltpu.roll(er, SUBLANES - 1, 0))
            ci = jnp.where(row == SUBLANES - 1, 0.0, pltpu.roll(ei, SUBLANES - 1, 0))

        def fix(k, carry):
            dr, di, ar, ai = carry
            rows = _rows8(ls - 1 - k)
            dr, di = lr * dr + li * di, lr * di - li * dr
            gr = gre_ref[rows, :] + dr
            gi = gim_ref[rows, :] + di
            gre_ref[rows, :] = gr
            gim_ref[rows, :] = gi
            prev = _rows8(ls - 2 - k)
            spr, spi = sre_ref[prev, :], sim_ref[prev, :]
            return dr, di, ar + gr * spr + gi * spi, ai + gi * spr - gr * spi

        dr, di, ar, ai = lax.fori_loop(0, ls - 1, fix, (cr, ci, zero, zero), unroll=SCAN_UNROLL)
        first = pl.ds(0, SUBLANES)
        last = pl.ds((ls - 1) * SUBLANES, SUBLANES)
        gr = gre_ref[first, :] + (lr * dr + li * di)
        gi = gim_ref[first, :] + (lr * di - li * dr)
        gre_ref[first, :] = gr
        gim_ref[first, :] = gi
        spr = jnp.where(row == 0, 0.0, pltpu.roll(sre_ref[last, :], 1, 0))
        spi = jnp.where(row == 0, 0.0, pltpu.roll(sim_ref[last, :], 1, 0))
        dar_ref[...] = _colsum(ar + gr * spr + gi * spi)
        dai_ref[...] = _colsum(ai + gi * spr - gr * spi)

        for c in range(T // rc):
            rows = pl.ds(c * rc, rc)
            g_r, g_i = gre_ref[rows, :].astype(BF16), gim_ref[rows, :].astype(BF16)
            s_r, s_i = sre_ref[rows, :].astype(BF16), sim_ref[rows, :].astype(BF16)
            ub, dyb = u_ref[rows, :], dy_ref[rows, :]
            duc = _dot_nt(g_r, bre_ref[...]) + _dot_nt(g_i, bim_ref[...])
            parts = (_dot_tn(ub, g_r), _dot_tn(ub, g_i), _dot_tn(s_r, dyb), -_dot_tn(s_i, dyb))
            outs = (dbr_ref, dbi_ref, dcr_ref, dci_ref)
            for o_ref, part in zip(outs, parts):
                if c == 0:
                    o_ref[...] = part
                else:
                    o_ref[...] += part

            @pl.when(cb % 2 == 0)
            def _():
                du_ref[rows, :] = duc

            @pl.when(cb % 2 == 1)
            def _():
                du_ref[rows, :] += duc

    blk = lambda r, c: pl.BlockSpec((None, r, c), lambda cb: (cb, 0, 0))
    return _call(body, name='ssm_bwd', grid=(ncb,),
                 in_specs=[sp['chan'], sp['chan'], sp['state'], sp['state'], sp['b'], sp['b'], sp['c'], sp['c'],
                           sp['lam'], sp['lam']],
                 out_specs=[sp['chan'], blk(CHAN_BLOCK, STATE_BLOCK), blk(CHAN_BLOCK, STATE_BLOCK),
                            blk(STATE_BLOCK, CHAN_BLOCK), blk(STATE_BLOCK, CHAN_BLOCK), blk(1, STATE_BLOCK),
                            blk(1, STATE_BLOCK)],
                 out_shape=[_sds((T, D_SSM)), _sds((ncb, CHAN_BLOCK, STATE_BLOCK)), _sds((ncb, CHAN_BLOCK, STATE_BLOCK)),
                            _sds((ncb, STATE_BLOCK, CHAN_BLOCK)), _sds((ncb, STATE_BLOCK, CHAN_BLOCK)),
                            _sds((ncb, 1, STATE_BLOCK)), _sds((ncb, 1, STATE_BLOCK))],
                 scratch=[pltpu.VMEM((T, STATE_BLOCK), F32), pltpu.VMEM((T, STATE_BLOCK), F32)],
                 sem=('arbitrary',), vmem=VMEM_BIG, ride=ride)(dy_perm, u_perm, s_re, s_im, b_re, b_im, c_re, c_im,
                                                               lam_r, lam_i)


def _post_norm_bwd(dx, val, gate, g, tm, name):
    T = dx.shape[0]

    def body(dx_ref, v_ref, gt_ref, g_ref, dv_ref, dgt_ref, dg_ref):
        @pl.when(pl.program_id(0) == 0)
        def _():
            dgt_ref[...] = jnp.zeros_like(dgt_ref)
            dg_ref[...] = jnp.zeros_like(dg_ref)

        dxv, v, gv = dx_ref[...], v_ref[...], g_ref[...]
        r = _rsqrt_mean(v)
        dgt_ref[...] += _colsum(dxv * (v * r * gv))
        dn = dxv * gt_ref[...]
        dg_ref[...] += _colsum(dn * v * r)
        dv_ref[...] = _norm_bwd(dn, v, r, gv).astype(BF16)

    row = pl.BlockSpec((tm, D_MODEL), lambda i: (i, 0))
    vec = _const((1, D_MODEL))
    return _call(body, name=name, grid=(T // tm,), in_specs=[row, row, vec, vec], out_specs=[row, vec, vec],
                 out_shape=[_sds((T, D_MODEL), BF16), _sds((1, D_MODEL)), _sds((1, D_MODEL))],
                 sem=('arbitrary',))(dx, val, gate, g)


def _ffn_dact(ddn, wd4, up4, cw4, tm):
    T = ddn.shape[0]
    hb = _halo_before(tm, HALO16)

    def body(d_ref, w_ref, up_ref, halo_ref, cw_ref, o_ref):
        dact = _dot_nt(d_ref[...], w_ref[...])
        hid_a, hid_v = _ffn_hidden(up_ref, halo_ref, cw_ref, pl.program_id(0))
        silu, dsilu = _silu_parts(hid_a)
        o_ref[0] = (dact * hid_v * dsilu).astype(BF16)
        o_ref[1] = (dact * silu).astype(BF16)

    return _call(body, name='ffn_dact', grid=(T // tm, 4),
                 in_specs=[pl.BlockSpec((tm, D_MODEL), lambda i, j: (i, 0)),
                           pl.BlockSpec((None, FF_SHARD, D_MODEL), lambda i, j: (j, 0, 0)),
                           pl.BlockSpec((2, None, tm, FF_SHARD), lambda i, j: (0, j, i, 0)),
                           pl.BlockSpec((2, None, HALO16, FF_SHARD), lambda i, j: (0, j, hb(i), 0)),
                           pl.BlockSpec((2, None, 3, FF_SHARD), lambda i, j: (0, j, 0, 0))],
                 out_specs=pl.BlockSpec((2, None, tm, FF_SHARD), lambda i, j: (0, j, i, 0)),
                 out_shape=_sds((2, 4, T, FF_SHARD), BF16), sem=('parallel', 'parallel'))(ddn, wd4, up4, up4, cw4)


def _ffn_dup(dhid8, up8, cw8, tm, ride):
    T = up8.shape[1]
    nb = T // tm
    ha = _halo_after(tm, T, HALO16)

    def body(dh_ref, dha_ref, up_ref, cw_ref, dup_ref, dcw_ref):
        i = pl.program_id(1)

        @pl.when(i == 0)
        def _():
            dcw_ref[...] = jnp.zeros_like(dcw_ref)

        dh = dh_ref[...].astype(F32)
        dup, dh1, dh2 = _conv3_t(dh, jnp.where(i < nb - 1, dha_ref[...].astype(F32), 0.0), cw_ref)
        dup_ref[...] = dup.astype(BF16)
        up = up_ref[...].astype(F32)
        dcw_ref[0:1, :] += _colsum(dh2 * up)
        dcw_ref[1:2, :] += _colsum(dh1 * up)
        dcw_ref[2:3, :] += _colsum(dh * up)

    main = pl.BlockSpec((None, tm, FF_SHARD), lambda j, i: (j, i, 0))
    return _call(body, name='ffn_dup', grid=(N_DEV, nb),
                 in_specs=[main, pl.BlockSpec((None, HALO16, FF_SHARD), lambda j, i: (j, ha(i), 0)), main,
                           pl.BlockSpec((None, 3, FF_SHARD), lambda j, i: (j, 0, 0))],
                 out_specs=[main, pl.BlockSpec((None, 8, FF_SHARD), lambda j, i: (j, 0, 0))],
                 out_shape=[_sds((N_DEV, T, FF_SHARD), BF16), _sds((N_DEV, 8, FF_SHARD))],
                 sem=('parallel', 'arbitrary'), ride=ride)(dhid8, dhid8, up8, cw8)


def _grad_tn(a, b, a_spec, b_spec, groups, m, n, tk, name, ride=None):
    T = a.shape[-2]
    nk = T // tk

    def body(a_ref, b_ref, o_ref, acc_ref):
        k = pl.program_id(1)
        part = _dot_tn(a_ref[...], b_ref[...])

        @pl.when(k == 0)
        def _():
            acc_ref[...] = part

        @pl.when(k > 0)
        def _():
            acc_ref[...] += part

        @pl.when(k == nk - 1)
        def _():
            o_ref[...] = acc_ref[...].astype(BF16)

    return _call(body, name=name, grid=(groups, nk), in_specs=[a_spec, b_spec],
                 out_specs=pl.BlockSpec((None, m, n), lambda g, k: (g, 0, 0)), out_shape=_sds((groups, m, n), BF16),
                 scratch=[pltpu.VMEM((m, n), F32)], sem=('parallel', 'arbitrary'), vmem=VMEM_BIG, ride=ride)(a, b)


def _pre_norm_bwd(dz, dz_spec, w_s, xin, dres, sc, g, tm, name, ride):
    T = xin.shape[0]
    n = w_s.shape[2]

    def body(dz_ref, w_ref, x_ref, dr_ref, sc_ref, g_ref, dx_ref, dsh_ref, dsc_ref, dg_ref):
        i, j = pl.program_id(0), pl.program_id(1)
        part = _dot_nt(dz_ref[...], w_ref[...])

        @pl.when(jnp.logical_and(i == 0, j == 0))
        def _():
            dsh_ref[...] = jnp.zeros_like(dsh_ref)
            dsc_ref[...] = jnp.zeros_like(dsc_ref)
            dg_ref[...] = jnp.zeros_like(dg_ref)

        @pl.when(j == 0)
        def _():
            dx_ref[...] = part

        @pl.when(j > 0)
        def _():
            dx_ref[...] += part

        @pl.when(j == N_DEV - 1)
        def _():
            dh, xv, gv = dx_ref[...], x_ref[...], g_ref[...]
            r = _rsqrt_mean(xv)
            dsh_ref[...] += _colsum(dh)
            dsc_ref[...] += _colsum(dh * (xv * r * gv))
            dxn = dh * (1.0 + sc_ref[...])
            dg_ref[...] += _colsum(dxn * xv * r)
            dx_ref[...] = dr_ref[...] + _norm_bwd(dxn, xv, r, gv)

    row = pl.BlockSpec((tm, D_MODEL), lambda i, j: (i, 0))
    vec = _const((1, D_MODEL))
    return _call(body, name=name, grid=(T // tm, N_DEV),
                 in_specs=[dz_spec, pl.BlockSpec((None, D_MODEL, n), lambda i, j: (j, 0, 0)), row, row, vec, vec],
                 out_specs=[row, vec, vec, vec],
                 out_shape=[_sds((T, D_MODEL)), _sds((1, D_MODEL)), _sds((1, D_MODEL)), _sds((1, D_MODEL))],
                 sem=('arbitrary', 'arbitrary'), vmem=VMEM_BIG, ride=ride)(dz, w_s, xin, dres, sc, g)


def _d_ycat(d_o, w_out, tm):
    T = d_o.shape[0]

    def body(d_ref, w_ref, o_ref):
        o_ref[...] = _dot_nt(d_ref[...], w_ref[...])

    row = pl.BlockSpec((tm, D_MODEL), lambda i: (i, 0))
    return _call(body, name='d_ycat', grid=(T // tm,), in_specs=[row, _const((D_MODEL, D_MODEL))], out_specs=row,
                 out_shape=_sds((T, D_MODEL)), sem=('parallel',))(d_o, w_out)


def _mix_bwd(dycat, yssm, proj, d, glu_w, glu_b, g_ssm, cw, g_conv, avg16, avg64, tm):
    T = yssm.shape[0]
    hb = _halo_before(tm)

    def body(dyc_ref, y_ref, p_ref, ph_ref, d_ref, gw_ref, gb_ref, gs_ref, cw_ref, gc_ref, a16_ref, a64_ref,
             dy_ref, dconv_ref, dbg_ref, z_ref, dlin_ref, acc_ref):
        i = pl.program_id(0)

        @pl.when(i == 0)
        def _():
            acc_ref[...] = jnp.zeros_like(acc_ref)

        u = p_ref[:, 0:D_SSM]
        y = y_ref[...] + d_ref[...] * u
        z, t = _gelu(y)
        gate = _sigmoid(_dot(z.astype(BF16), gw_ref[...]) + gb_ref[...])
        ya = z * gate
        rs = lax.rsqrt(_dot_split(ya * ya, a16_ref[...], 2) + EPS)
        dna = dyc_ref[:, 0:D_SSM]
        acc_ref[1:2, :] += _colsum(dna * ya * rs)
        dya = _head_norm_bwd(dna, ya, rs, gs_ref[...], a16_ref[...])
        dlin = dya * z * gate * (1.0 - gate)
        acc_ref[0:1, :] += _colsum(dlin)
        dlin_b = dlin.astype(BF16)
        dz = dya * gate + _dot_nt(dlin_b, gw_ref[...])
        dy = dz * _gelu_grad(y, t)
        acc_ref[3:4, :] += _colsum(dy * u)
        dy_ref[...] = dy
        z_ref[...] = z.astype(BF16)
        dlin_ref[...] = dlin_b

        bg = p_ref[:, D_SSM:D_SSM + D_CONV]
        cv = p_ref[:, D_SSM + D_CONV:D_SSM + 2 * D_CONV] * p_ref[:, D_SSM + 2 * D_CONV:D_IN_PROJ]
        hv = ph_ref[:, D_SSM + D_CONV:D_SSM + 2 * D_CONV] * ph_ref[:, D_SSM + 2 * D_CONV:D_IN_PROJ]
        hv = jnp.where(i > 0, hv, 0.0)
        conv, cv1, cv2 = _conv3(cv, hv, cw_ref)
        yb = bg * conv
        rsb = lax.rsqrt(_dot_split(yb * yb, a64_ref[...], 2) + EPS)
        dnb = dyc_ref[:, D_SSM:D_MODEL]
        acc_ref[2:3, :] += _colsum(dnb * yb * rsb)
        dyb = _head_norm_bwd(dnb, yb, rsb, gc_ref[...], a64_ref[...])
        dbg_ref[...] = dyb * conv
        dconv = dyb * bg
        dconv_ref[...] = dconv
        acc_ref[4:5, :] += _colsum(dconv * cv2)
        acc_ref[5:6, :] += _colsum(dconv * cv1)
        acc_ref[6:7, :] += _colsum(dconv * cv)

    vec = _const((1, D_SSM))
    sq = _const((D_SSM, D_SSM))
    half = pl.BlockSpec((tm, D_SSM), lambda i: (i, 0))
    return _call(body, name='mix_bwd', grid=(T // tm,),
                 in_specs=[pl.BlockSpec((tm, D_MODEL), lambda i: (i, 0)), half,
                           pl.BlockSpec((tm, D_IN_PROJ), lambda i: (i, 0)),
                           pl.BlockSpec((HALO, D_IN_PROJ), lambda i: (hb(i), 0)), vec, sq, vec, vec,
                           _const((3, D_CONV)), vec, sq, sq],
                 out_specs=[half, half, half, half, half, _const((8, D_SSM))],
                 out_shape=[_sds((T, D_SSM)), _sds((T, D_SSM)), _sds((T, D_SSM)), _sds((T, D_SSM), BF16),
                            _sds((T, D_SSM), BF16), _sds((8, D_SSM))],
                 sem=('arbitrary',), vmem=VMEM_BIG)(dycat, yssm, proj, proj, d, glu_w, glu_b, g_ssm, cw, g_conv,
                                                   avg16, avg64)


def _mix_bwd_proj(dconv, proj, du_ssm, dy, d, dbg, cw, tm):
    T = dy.shape[0]
    nb = T // tm
    ha = _halo_after(tm, T)

    def body(dc_ref, dch_ref, cg_ref, v_ref, du_ref, dy_ref, d_ref, dbg_ref, cw_ref, o_ref):
        i = pl.program_id(0)
        dcv = _conv3_t(dc_ref[...], jnp.where(i < nb - 1, dch_ref[...], 0.0), cw_ref)[0]
        o_ref[:, 0:D_SSM] = (du_ref[...] + dy_ref[...] * d_ref[...]).astype(BF16)
        o_ref[:, D_SSM:D_SSM + D_CONV] = dbg_ref[...].astype(BF16)
        o_ref[:, D_SSM + D_CONV:D_SSM + 2 * D_CONV] = (dcv * v_ref[...]).astype(BF16)
        o_ref[:, D_SSM + 2 * D_CONV:D_IN_PROJ] = (dcv * cg_ref[...]).astype(BF16)

    half = pl.BlockSpec((tm, D_SSM), lambda i: (i, 0))
    return _call(body, name='mix_bwd_proj', grid=(nb,),
                 in_specs=[half, pl.BlockSpec((HALO, D_CONV), lambda i: (ha(i), 0)),
                           pl.BlockSpec((tm, D_CONV), lambda i: (i, 2)), pl.BlockSpec((tm, D_CONV), lambda i: (i, 3)),
                           half, half, _const((1, D_SSM)), half, _const((3, D_CONV))],
                 out_specs=pl.BlockSpec((tm, D_IN_PROJ), lambda i: (i, 0)), out_shape=_sds((T, D_IN_PROJ), BF16),
                 sem=('parallel',))(dconv, dconv, proj, proj, du_ssm, dy, d, dbg, cw)


def _row_tile(rows, cols, slots):
    for cand in (512, 256, 128, 64, 32, 16, 8):
        if rows % cand == 0 and slots * cand * cols * 4 <= (2 << 20):
            return cand
    return rows


def _adamw_math(g, w, m, v):
    m2 = ADAM_B1 * m + (1.0 - ADAM_B1) * g
    v2 = ADAM_B2 * v + (1.0 - ADAM_B2) * (g * g)
    m_hat = m2 / (1.0 - ADAM_B1 ** ADAM_STEP)
    v_hat = v2 / (1.0 - ADAM_B2 ** ADAM_STEP)
    return -ADAM_LR * (m_hat / (jnp.sqrt(v_hat) + ADAM_EPS) + ADAM_WD * w), m2, v2


def _adamw(pieces, w, m, v, name):
    slots, _, cols = pieces[0].shape
    rows = sum(p.shape[1] for p in pieces)
    tr = _row_tile(pieces[0].shape[1], cols, slots)
    starts, pos = [], 0
    for p in pieces:
        assert p.shape[1] % tr == 0
        starts.append(pos)
        pos += p.shape[1] // tr

    def body(*refs):
        g_refs = refs[:len(pieces)]
        w_ref, m_ref, v_ref, go_ref, d_ref, mo_ref, vo_ref = refs[len(pieces):]
        i = pl.program_id(0)
        g = None
        for g_ref, start in zip(g_refs, starts):
            part = g_ref[0].astype(F32)
            for s in range(1, slots):
                part = part + g_ref[s].astype(F32)
            g = part if g is None else jnp.where(i >= start, part, g)
        go_ref[...] = g
        d_ref[...], mo_ref[...], vo_ref[...] = _adamw_math(g, w_ref[...], m_ref[...], v_ref[...])

    def piece_spec(start, count):
        return pl.BlockSpec((slots, tr, cols), lambda i: (0, jnp.clip(i - start, 0, count - 1), 0))

    blk = pl.BlockSpec((tr, cols), lambda i: (i, 0))
    return _call(body, name=name, grid=(rows // tr,),
                 in_specs=[piece_spec(s, p.shape[1] // tr) for s, p in zip(starts, pieces)] + [blk, blk, blk],
                 out_specs=[blk] * 4, out_shape=[_sds((rows, cols))] * 4, sem=('parallel',))(*pieces, w, m, v)


def _to_scan_rows(a):
    T, n = a.shape
    return a.reshape(SUBLANES, T // SUBLANES, n).transpose(1, 0, 2).reshape(T, n)


def _from_scan_rows(a):
    T, n = a.shape
    return a.reshape(T // SUBLANES, SUBLANES, n).transpose(1, 0, 2).reshape(T, n)


def _expand(a):
    return jnp.repeat(a, SSM_GROUP, axis=1)


def _block_diag_b(bb):
    eye = jnp.eye(N_GROUPS, dtype=bb.dtype)
    return (bb.transpose(0, 2, 1)[:, :, None, :] * eye[:, None, :, None]).reshape(D_SSM, N_STATE)


def _block_diag_c(cc):
    eye = jnp.eye(N_GROUPS, dtype=cc.dtype)
    return (cc.transpose(0, 2, 1)[:, :, None, :] * eye[:, None, :, None]).reshape(N_STATE, D_SSM)


def _diag_blocks(x, chan_major):
    e2 = jnp.eye(2, dtype=x.dtype)
    e4 = jnp.eye(4, dtype=x.dtype)
    if chan_major:
        x = x.reshape(4, 2, 2, 4, SSM_GROUP, 4, SSM_STATE)
        x = x * e2[None, :, :, None, None, None, None] * e4[None, None, None, :, None, :, None]
        return x.sum(axis=(2, 3)).transpose(0, 1, 3, 4, 2).reshape(N_GROUPS, SSM_STATE, SSM_GROUP)
    x = x.reshape(4, 2, 4, SSM_STATE, 2, 4, SSM_GROUP)
    x = x * e2[None, :, None, None, :, None, None] * e4[None, None, :, None, None, :, None]
    return x.sum(axis=(4, 5)).reshape(N_GROUPS, SSM_STATE, SSM_GROUP)


SMALL_LAYOUT = {
    'ssm_b_re': (0, 0, 32, 1024), 'ssm_b_im': (32, 0, 32, 1024), 'ssm_c_re': (64, 0, 32, 1024),
    'ssm_c_im': (96, 0, 32, 1024), 'b_ada': (128, 0, 6, 1024), 'g_pre_mix': (134, 0, 1, 1024),
    'g_post_mix': (135, 0, 1, 1024), 'ssm_lam_re': (136, 0, 2, 1024), 'ssm_lam_im': (138, 0, 2, 1024),
    'ssm_log_step': (140, 0, 1, 32), 'glu_b': (141, 0, 1, 512), 'g_out_ssm': (141, 512, 1, 512),
    'g_out_conv': (142, 0, 1, 512), 'ssm_d': (142, 512, 1, 512), 'g_pre_ffn': (143, 0, 1, 1024),
    'g_post_ffn': (144, 0, 1, 1024)}
SMALL_ROWS = 152
B_ADA_ROW = SMALL_LAYOUT['b_ada'][0]


def _adamw_small(gathered, wts, mom_m, mom_v):
    names = list(SMALL_LAYOUT)
    n = len(names)

    def body(*refs):
        g_ref, ins, outs = refs[0], refs[1:1 + 3 * n], refs[1 + 3 * n:]
        for p, name in enumerate(names):
            r0, c0, rows, cols = SMALL_LAYOUT[name]
            pieces = [(0, rows)] if rows % 8 == 0 else [(r, 1) for r in range(rows)]
            for r, cnt in pieces:
                g = g_ref[0, r0 + r:r0 + r + cnt, c0:c0 + cols]
                for s in range(1, N_DEV):
                    g = g + g_ref[s, r0 + r:r0 + r + cnt, c0:c0 + cols]
                w, m, v = (ins[3 * p + q][r:r + cnt, :] for q in range(3))
                res = (g,) + _adamw_math(g, w, m, v)
                for q in range(4):
                    outs[4 * p + q][r:r + cnt, :] = res[q]

    shapes = [SMALL_LAYOUT[name][2:] for name in names]
    args = [gathered]
    for name, shp in zip(names, shapes):
        args += [wts[name].reshape(shp), mom_m[name].reshape(shp), mom_v[name].reshape(shp)]
    outs = _call(body, name='adamw_small', grid=(1,),
                 in_specs=[_const(gathered.shape)] + [_const(shp) for shp in shapes for _ in range(3)],
                 out_specs=[_const(shp) for shp in shapes for _ in range(4)],
                 out_shape=[_sds(shp) for shp in shapes for _ in range(4)], vmem=VMEM_BIG)(*args)
    res = {}
    for p, name in enumerate(names):
        for q, kind in enumerate(('g', 'd', 'm', 'v')):
            res[kind, name] = outs[4 * p + q].reshape(wts[name].shape)
    return res


def kernel(x, c, w_ada, b_ada, g_pre_mix, g_post_mix, w_in, ssm_lam_re, ssm_lam_im, ssm_log_step, ssm_b_re, ssm_b_im, ssm_c_re, ssm_c_im, ssm_d, glu_w, glu_b, g_out_ssm, conv_w, g_out_conv, w_out, g_pre_ffn, g_post_ffn, w_up, ffn_conv_w, w_down, loss_target, m_w_ada, m_b_ada, m_g_pre_mix, m_g_post_mix, m_w_in, m_ssm_lam_re, m_ssm_lam_im, m_ssm_log_step, m_ssm_b_re, m_ssm_b_im, m_ssm_c_re, m_ssm_c_im, m_ssm_d, m_glu_w, m_glu_b, m_g_out_ssm, m_conv_w, m_g_out_conv, m_w_out, m_g_pre_ffn, m_g_post_ffn, m_w_up, m_ffn_conv_w, m_w_down, v_w_ada, v_b_ada, v_g_pre_mix, v_g_post_mix, v_w_in, v_ssm_lam_re, v_ssm_lam_im, v_ssm_log_step, v_ssm_b_re, v_ssm_b_im, v_ssm_c_re, v_ssm_c_im, v_ssm_d, v_glu_w, v_glu_b, v_g_out_ssm, v_conv_w, v_g_out_conv, v_w_out, v_g_pre_ffn, v_g_post_ffn, v_w_up, v_ffn_conv_w, v_w_down):
    args = dict(locals())
    wts = {n: args[n] for n in WEIGHTS}
    mom_m = {n: args['m_' + n] for n in WEIGHTS}
    mom_v = {n: args['v_' + n] for n in WEIGHTS}
    T = x.shape[1]
    tm = min(512, T)
    tw = min(1024, T)
    me = _me()[3]
    xt, tgt = x[0], loss_target[0]

    (c_all,) = _exchange([c], name='gather_c', scatter=False)
    c_all = c_all.reshape(N_DEV, D_MODEL)
    b_cols = lax.dynamic_slice(b_ada, (0, me * ADA_SHARD), (1, ADA_SHARD))
    mod_cols, c_act = _mod_cols(c_all, w_ada[0], b_cols)
    (mod_all,) = _exchange([mod_cols], name='gather_mod', scatter=False)
    mod = lax.dynamic_slice(mod_all, (0, me, 0), (N_DEV, 1, ADA_SHARD)).reshape(N_MOD, 1, D_MODEL)
    sh1, sc1, gt1, sh2, sc2, gt2 = [mod[k] for k in range(N_MOD)]

    w_in_s, glu_s, w_out_s, conv_s = _exchange(
        [w_in[0].astype(BF16), glu_w[0].astype(BF16), w_out[0].astype(BF16), conv_w[0]], name='gather_weights',
        scatter=False)
    glu_full = glu_s.reshape(D_SSM, D_SSM)
    w_out_full = w_out_s.reshape(D_MODEL, D_MODEL)
    cw_full = conv_s.transpose(1, 0, 2).reshape(3, D_CONV)

    lre_x, lim_x = _expand(ssm_lam_re[0]), _expand(ssm_lam_im[0])
    lst_x = jnp.broadcast_to(ssm_log_step[0][:, None], (N_GROUPS, SSM_STATE * SSM_GROUP))
    b_re_x = ssm_b_re[0].reshape(N_GROUPS, -1)
    b_im_x = ssm_b_im[0].reshape(N_GROUPS, -1)
    ar_x, ai_x, bbr_x, bbi_x = _ssm_prep(lre_x, lim_x, lst_x, b_re_x, b_im_x)
    lam_r = ar_x[:, ::SSM_GROUP].reshape(1, N_STATE)
    lam_i = ai_x[:, ::SSM_GROUP].reshape(1, N_STATE)
    big_b_re = _block_diag_b(bbr_x.reshape(N_GROUPS, SSM_STATE, SSM_GROUP)).astype(BF16)
    big_b_im = _block_diag_b(bbi_x.reshape(N_GROUPS, SSM_STATE, SSM_GROUP)).astype(BF16)
    big_c_re = _block_diag_c(ssm_c_re[0]).astype(BF16)
    big_c_im = _block_diag_c(ssm_c_im[0]).astype(BF16)
    head = jnp.arange(D_SSM)
    avg16 = jnp.where(head[:, None] // SSM_GROUP == head[None, :] // SSM_GROUP, 1.0 / SSM_GROUP, 0.0).astype(BF16)
    hd = D_CONV // CONV_HEADS
    avg64 = jnp.where(head[:, None] // hd == head[None, :] // hd, 1.0 / hd, 0.0).astype(BF16)

    (proj, h1), (w_down_s, ffn_conv_s) = _pre_mix(xt, sc1, sh1, g_pre_mix, w_in_s, tw,
                                                  ([w_down[0].astype(BF16), ffn_conv_w[0]], False))
    wd4 = w_down_s.reshape(4, FF_SHARD, D_MODEL)
    cw4 = ffn_conv_s.reshape(2, 4, 3, FF_SHARD)
    u_perm = _to_scan_rows(proj[:, :D_SSM]).astype(BF16)
    (s_re, s_im, y_perm), (w_up_s,) = _ssm_fwd(u_perm, big_b_re, big_b_im, big_c_re, big_c_im, lam_r, lam_i,
                                               ([w_up[0].astype(BF16)], False))
    yssm = _from_scan_rows(y_perm)
    mix_args = (ssm_d, glu_full, glu_b, g_out_ssm, cw_full, g_out_conv, avg16, avg64)
    ycat = _mix_fwd(yssm, proj, *mix_args, tm)
    o, x1, h2 = _out_proj(ycat, w_out_full, xt, gt1, g_post_mix, g_pre_ffn, sc2, sh2, tm)
    up8 = _ffn_up(h2, w_up_s, tw)
    up4 = up8.reshape(2, 4, T, FF_SHARD)
    act = _ffn_act(up4, cw4, tm)
    dn, dx2, loss_parts = _ffn_down(act, wd4, x1, tgt, gt2, g_post_ffn, tw)
    loss = lax.psum(jnp.sum(loss_parts[:, 0, 0]), ('x', 'y', 'c'))

    got = {}
    ddn, d_gt2, d_g_post_ffn = _post_norm_bwd(dx2, dn, gt2, g_post_ffn, tm, 'ffn_norm_bwd')
    dhid = _ffn_dact(ddn, wd4, up4, cw4, tm)
    g_w_down = _grad_tn(act, ddn, pl.BlockSpec((None, tw, FF_SHARD), lambda g, k: (g, k, 0)),
                        pl.BlockSpec((tw, D_MODEL), lambda g, k: (k, 0)), 4, FF_SHARD, D_MODEL, tw, 'grad_w_down')
    (dup8, dcw_ffn), (got['w_down'],) = _ffn_dup(dhid.reshape(N_DEV, T, FF_SHARD), up8, ffn_conv_s, tm,
                                                 ([g_w_down.reshape(N_DEV, D_FF // N_DEV, D_MODEL)], True))
    half = D_MODEL // 2
    g_w_up_halves = [
        _grad_tn(h2, dup8, pl.BlockSpec((tw, half), lambda g, k, c=c: (k, c)),
                 pl.BlockSpec((None, tw, FF_SHARD), lambda g, k: (g, k, 0)), N_DEV, half, FF_SHARD, tw,
                 'grad_w_up_%d' % c) for c in range(2)]
    (dx1, d_sh2, d_sc2, d_g_pre_ffn), (got_up_0, got['ffn_conv_w']) = _pre_norm_bwd(
        dup8, pl.BlockSpec((None, tw, FF_SHARD), lambda i, j: (j, i, 0)), w_up_s, x1, dx2, sc2, g_pre_ffn, tw,
        'ffn_in_bwd', ([g_w_up_halves[0], dcw_ffn], True))

    d_o, d_gt1, d_g_post_mix = _post_norm_bwd(dx1, o, gt1, g_post_mix, tm, 'mix_norm_bwd')
    g_w_out = _grad_tn(ycat, d_o, pl.BlockSpec((tw, D_MODEL), lambda g, k: (k, 0)),
                       pl.BlockSpec((tw, D_MODEL), lambda g, k: (k, 0)), 1, D_MODEL, D_MODEL, tw, 'grad_w_out')
    dycat = _d_ycat(d_o, w_out_full, tm)
    dy, dconv, dbg, z_b, dlin_b, sums = _mix_bwd(dycat, yssm, proj, *mix_args, tm)
    g_glu_w = _grad_tn(z_b, dlin_b, pl.BlockSpec((tw, D_SSM), lambda g, k: (k, 0)),
                       pl.BlockSpec((tw, D_SSM), lambda g, k: (k, 0)), 1, D_SSM, D_SSM, tw, 'grad_glu_w')
    dy_perm = _to_scan_rows(dy).astype(BF16)
    (du_perm, dbr_blk, dbi_blk, dcr_blk, dci_blk, dar_blk, dai_blk), (got_up_1, got['w_out'], got['glu_w']) = _ssm_bwd(
        dy_perm, u_perm, s_re, s_im, big_b_re, big_b_im, big_c_re, big_c_im, lam_r, lam_i,
        ([g_w_up_halves[1], g_w_out.reshape(N_DEV, D_MODEL // N_DEV, D_MODEL),
          g_glu_w.reshape(N_DEV, D_SSM // N_DEV, D_SSM)], True))
    du_ssm = _from_scan_rows(du_perm)
    dproj = _mix_bwd_proj(dconv, proj, du_ssm, dy, ssm_d, dbg, cw_full, tm)
    g_w_in = _grad_tn(h1, dproj, pl.BlockSpec((tw, D_MODEL), lambda g, k: (k, 0)),
                      pl.BlockSpec((tw, IN_SHARD), lambda g, k: (k, g)), N_DEV, D_MODEL, IN_SHARD, tw, 'grad_w_in')
    g_conv_slots = jnp.concatenate([sums[4:7], jnp.zeros((5, D_CONV), F32)]).reshape(
        8, N_DEV, D_CONV // N_DEV).transpose(1, 0, 2)
    (grad_x, d_sh1, d_sc1, d_g_pre_mix), (got['w_in'], got['conv_w']) = _pre_norm_bwd(
        dproj, pl.BlockSpec((tw, IN_SHARD), lambda i, j: (i, j)), w_in_s, xt, dx1, sc1, g_pre_mix, tw, 'mix_in_bwd',
        ([g_w_in, g_conv_slots], True))

    dbb_re = _diag_blocks(dbr_blk, True).reshape(N_GROUPS, -1)
    dbb_im = _diag_blocks(dbi_blk, True).reshape(N_GROUPS, -1)
    d_c_re = _diag_blocks(dcr_blk, False).transpose(0, 2, 1)
    d_c_im = _diag_blocks(dci_blk, False).transpose(0, 2, 1)
    lane = jnp.arange(SSM_STATE * SSM_GROUP)
    seg = jnp.where(lane[:, None] // SSM_GROUP == lane[None, :] // SSM_GROUP, 1.0, 0.0).astype(BF16)
    d_b_re_x, d_b_im_x, d_lre_x, d_lim_x, d_lst = _ssm_prep_bwd(
        lre_x, lim_x, lst_x, b_re_x, b_im_x, dbb_re, dbb_im, _expand(dar_blk.reshape(N_GROUPS, SSM_STATE)),
        _expand(dai_blk.reshape(N_GROUPS, SSM_STATE)), seg)

    row = lambda a: a.reshape(-1, PACK_COLS)
    small_pack = jnp.concatenate([
        d_b_re_x, d_b_im_x, row(d_c_re), row(d_c_im), d_sh1, d_sc1, d_gt1, d_sh2, d_sc2, d_gt2, d_g_pre_mix,
        d_g_post_mix, row(d_lre_x[:, ::SSM_GROUP]), row(d_lim_x[:, ::SSM_GROUP]),
        jnp.pad(d_lst.reshape(1, N_GROUPS), ((0, 0), (0, PACK_COLS - N_GROUPS))), row(sums[0:4]), d_g_pre_ffn,
        d_g_post_ffn, jnp.zeros((SMALL_ROWS - 145, PACK_COLS), F32)])
    (small_all,) = _exchange([small_pack], name='gather_small_grads', scatter=False)
    res = _adamw_small(small_all, wts, mom_m, mom_v)

    dmod_all = small_all[:, B_ADA_ROW:B_ADA_ROW + N_MOD, :].reshape(N_DEV, N_MOD * D_MODEL)
    dmod_cols = lax.dynamic_slice(dmod_all, (0, me * ADA_SHARD), (N_DEV, ADA_SHARD))
    g_w_ada = _grad_w_ada(c_act.T, dmod_cols)

    pieces = {n: [slots[:, :3, :] if n in ('conv_w', 'ffn_conv_w') else slots] for n, slots in got.items()}
    pieces['w_up'] = [got_up_0, got_up_1]
    for n, parts in pieces.items():
        outs = _adamw(parts, wts[n][0], mom_m[n][0], mom_v[n][0], 'adamw_' + n)
        for kind, val in zip(('g', 'd', 'm', 'v'), outs):
            res[kind, n] = val[None]
    outs = _adamw([g_w_ada[None]], w_ada[0], m_w_ada[0], v_w_ada[0], 'adamw_w_ada')
    for kind, val in zip(('g', 'd', 'm', 'v'), outs):
        res[kind, 'w_ada'] = val[None]

    return (loss, grad_x[None], *[res['g', n] for n in WEIGHTS], *[res['d', n] for n in WEIGHTS],
            *[res['m', n] for n in WEIGHTS], *[res['v', n] for n in WEIGHTS])
```

```python
import math

import jax
import jax.numpy as jnp
from jax import lax
from jax.experimental import pallas as pl
from jax.experimental.pallas import tpu as pltpu

F32, BF16 = jnp.float32, jnp.bfloat16

D_MODEL = 1024
D_SSM = 512
D_CONV = 512
SSM_GROUP = 16
N_GROUPS = 32
SSM_STATE = 64
N_STATE = N_GROUPS * SSM_STATE
CONV_HEADS = 8
D_FF = 2816
N_MOD = 6
D_IN_PROJ = D_SSM + 3 * D_CONV
N_DEV = 8
FF_SHARD = 2 * D_FF // N_DEV
IN_SHARD = D_IN_PROJ // N_DEV
ADA_SHARD = N_MOD * D_MODEL // N_DEV
EPS = 1e-6
LAMBDA_RE_MAX = -1e-4
ADAM_LR, ADAM_B1, ADAM_B2, ADAM_EPS, ADAM_WD, ADAM_STEP = 0.001, 0.9, 0.999, 1e-08, 0.01, 10
GELU_C = math.sqrt(2.0 / math.pi)
GELU_A = 0.044715

SUBLANES = 8
HALO = 8
HALO16 = 16
SCAN_UNROLL = 8
STATE_BLOCK = 256
CHAN_BLOCK = 128
VMEM_BIG = 48 << 20
VMEM_MOST = 58 << 20

WEIGHTS = ['w_ada', 'b_ada', 'g_pre_mix', 'g_post_mix', 'w_in', 'ssm_lam_re', 'ssm_lam_im', 'ssm_log_step',
           'ssm_b_re', 'ssm_b_im', 'ssm_c_re', 'ssm_c_im', 'ssm_d', 'glu_w', 'glu_b', 'g_out_ssm', 'conv_w',
           'g_out_conv', 'w_out', 'g_pre_ffn', 'g_post_ffn', 'w_up', 'ffn_conv_w', 'w_down']
SHARDED = ('w_ada', 'w_in', 'glu_w', 'conv_w', 'w_out', 'w_up', 'ffn_conv_w', 'w_down')
PACK_COLS = 1024


def _call(body, *, name, grid, in_specs, out_specs, out_shape, scratch=(), sem=None, vmem=None, ride=None):
    params = {}
    if vmem is not None:
        params['vmem_limit_bytes'] = vmem
    if ride is None:
        if sem is not None:
            params['dimension_semantics'] = sem
        return pl.pallas_call(body, name=name, grid=grid, in_specs=in_specs, out_specs=out_specs,
                              out_shape=out_shape, scratch_shapes=list(scratch),
                              compiler_params=pltpu.CompilerParams(**params))
    arrs, scatter = ride
    single = not isinstance(out_shape, (list, tuple))
    out_shape_l = [out_shape] if single else list(out_shape)
    out_specs_l = [out_specs] if single else list(out_specs)
    n, n_in, n_out, n_scr = len(arrs), len(in_specs), len(out_shape_l), len(scratch)
    any_spec = pl.BlockSpec(memory_space=pl.ANY)
    params['dimension_semantics'] = ('arbitrary',) * len(grid)

    def carried(*refs):
        ins, rin = refs[:n_in], refs[n_in:n_in + n]
        outs, rout = refs[n_in + n:n_in + n + n_out], refs[n_in + n + n_out:n_in + 2 * n + n_out]
        scr, sems = refs[n_in + 2 * n + n_out:n_in + 2 * n + n_out + n_scr], refs[n_in + 2 * n + n_out + n_scr:]
        first = pl.program_id(0) == 0
        last = pl.program_id(0) == grid[0] - 1
        for ax in range(1, len(grid)):
            first = jnp.logical_and(first, pl.program_id(ax) == 0)
            last = jnp.logical_and(last, pl.program_id(ax) == grid[ax] - 1)

        @pl.when(first)
        def _():
            _exchange_start(rin, rout, sems, scatter)

        body(*ins, *outs, *scr)

        @pl.when(last)
        def _():
            _exchange_wait(rin, rout, sems, scatter)

    call = pl.pallas_call(carried, name=name, grid=grid, in_specs=list(in_specs) + [any_spec] * n,
                          out_specs=out_specs_l + [any_spec] * n,
                          out_shape=out_shape_l + _exchange_shapes(arrs, scatter),
                          scratch_shapes=list(scratch) + _exchange_sems(n),
                          compiler_params=pltpu.CompilerParams(**params))

    def run(*args):
        res = call(*args, *arrs)
        own = res[0] if single else list(res[:n_out])
        return own, list(res[n_out:])

    return run


def _const(shape):
    nd = len(shape)
    return pl.BlockSpec(shape, lambda *_: (0,) * nd)


def _sds(shape, dtype=F32):
    return jax.ShapeDtypeStruct(shape, dtype)


def _dot(a, b):
    return jnp.dot(a, b, preferred_element_type=F32)


def _dot_nt(a, b):
    return lax.dot_general(a, b, (((1,), (1,)), ((), ())), preferred_element_type=F32)


def _dot_tn(a, b):
    return lax.dot_general(a, b, (((0,), (0,)), ((), ())), preferred_element_type=F32)


def _dot_split(x, mat, parts):
    acc = None
    rem = x
    for _ in range(parts):
        piece = rem.astype(BF16)
        rem = rem - piece.astype(F32)
        term = _dot(piece, mat)
        acc = term if acc is None else acc + term
    return acc


def _sigmoid(x):
    return 1.0 / (1.0 + jnp.exp(-x))


def _gelu(x):
    t = jnp.tanh(GELU_C * (x + GELU_A * x * x * x))
    return 0.5 * x * (1.0 + t), t


def _gelu_grad(x, t):
    return 0.5 * (1.0 + t) + 0.5 * x * (1.0 - t * t) * GELU_C * (1.0 + 3.0 * GELU_A * x * x)


def _rsqrt_mean(x):
    return lax.rsqrt(jnp.mean(x * x, axis=-1, keepdims=True) + EPS)


def _colsum(x):
    return jnp.sum(x, axis=0, keepdims=True)


def _shifts_down(x, halo):
    ext = jnp.concatenate([halo, x], axis=0)
    return pltpu.roll(ext, 1, 0)[halo.shape[0]:], pltpu.roll(ext, 2, 0)[halo.shape[0]:]


def _shifts_up(x, halo):
    n = x.shape[0]
    ext = jnp.concatenate([x, halo], axis=0)
    total = ext.shape[0]
    return pltpu.roll(ext, total - 1, 0)[:n], pltpu.roll(ext, total - 2, 0)[:n]


def _conv3(x, halo, w_ref):
    x1, x2 = _shifts_down(x, halo)
    return w_ref[0:1, :] * x2 + w_ref[1:2, :] * x1 + w_ref[2:3, :] * x, x1, x2


def _conv3_t(g, halo, w_ref):
    g1, g2 = _shifts_up(g, halo)
    return w_ref[2:3, :] * g + w_ref[1:2, :] * g1 + w_ref[0:1, :] * g2, g1, g2


def _silu_parts(x):
    s = _sigmoid(x)
    return x * s, s * (1.0 + x * (1.0 - s))


def _norm_bwd(dn, x, r, g):
    gd = g * dn
    return r * gd - x * (r * r * r) * jnp.mean(gd * x, axis=-1, keepdims=True)


def _head_norm_bwd(dn, y, rs, g, avg):
    gd = g * dn
    return rs * gd - y * (rs * rs * rs) * _dot_split(gd * y, avg, 2)


def _me():
    x, y, c = lax.axis_index('x'), lax.axis_index('y'), lax.axis_index('c')
    return x, y, c, 4 * x + 2 * y + c


def _peer(k):
    x, y, c, _ = _me()
    px = 1 - x if k & 4 else x
    py = 1 - y if k & 2 else y
    pc = 1 - c if k & 1 else c
    return (px, py, pc), 4 * px + 2 * py + pc


SIBLING = 1
OTHER_CHIPS = (2, 4, 6)


def _remote(src, dst, sems, a, k, dev):
    return pltpu.make_async_remote_copy(src_ref=src, dst_ref=dst, send_sem=sems[0].at[a, k - 1],
                                        recv_sem=sems[1].at[a, k - 1], device_id=dev,
                                        device_id_type=pl.DeviceIdType.MESH)


def _exchange_copies(ins, outs, sems, scatter):
    me = _me()[3]
    local, first, relay, arrivals = [], [], [], []
    for a in range(len(ins)):
        src = ins[a].at[me] if scatter else ins[a]
        local.append(pltpu.make_async_copy(src, outs[a].at[me], sems[2].at[a]))
        for k in range(1, N_DEV):
            dev, idx = _peer(k)
            landed = _remote(src, outs[a].at[idx], sems, a, k, dev)
            if scatter:
                first.append(_remote(ins[a].at[idx], outs[a].at[me], sems, a, k, dev))
                arrivals.append(landed)
            elif k == SIBLING:
                first.append(_remote(src, outs[a].at[me], sems, a, k, dev))
                arrivals.append(landed)
            elif k in OTHER_CHIPS:
                first.append(_remote(src, outs[a].at[me], sems, a, k, dev))
                sib, _ = _peer(SIBLING)
                relay.append((landed, _remote(outs[a].at[idx], outs[a].at[idx], sems, a, k | SIBLING, sib)))
            else:
                arrivals.append(landed)
    return local, first, relay, arrivals


def _exchange_start(ins, outs, sems, scatter):
    local, first, _, _ = _exchange_copies(ins, outs, sems, scatter)
    for cp in local + first:
        cp.start()


def _exchange_wait(ins, outs, sems, scatter):
    local, first, relay, arrivals = _exchange_copies(ins, outs, sems, scatter)
    for landed, forward in relay:
        landed.wait_recv()
        forward.start()
    for cp in arrivals:
        cp.wait_recv()
    for cp in first + [forward for _, forward in relay]:
        cp.wait_send()
    for cp in local:
        cp.wait()


def _exchange_shapes(arrs, scatter):
    return [_sds(a.shape if scatter else (N_DEV,) + a.shape, a.dtype) for a in arrs]


def _exchange_sems(n):
    return [pltpu.SemaphoreType.DMA((n, N_DEV - 1)), pltpu.SemaphoreType.DMA((n, N_DEV - 1)),
            pltpu.SemaphoreType.DMA((n,))]


def _exchange(arrs, *, name, scatter):
    n = len(arrs)

    def body(*refs):
        _exchange_start(refs[:n], refs[n:2 * n], refs[2 * n:], scatter)
        _exchange_wait(refs[:n], refs[n:2 * n], refs[2 * n:], scatter)

    any_spec = pl.BlockSpec(memory_space=pl.ANY)
    outs = pl.pallas_call(body, name=name, out_shape=_exchange_shapes(arrs, scatter), in_specs=[any_spec] * n,
                          out_specs=[any_spec] * n, scratch_shapes=_exchange_sems(n))(*arrs)
    return list(outs)


def _mod_cols(c_all, w_ada, b_cols):
    def body(c_ref, w_ref, b_ref, mod_ref, act_ref):
        c = c_ref[...]
        act = c * _sigmoid(c)
        act_ref[...] = act
        mod_ref[...] = _dot(act.astype(BF16), w_ref[...].astype(BF16)) + b_ref[...]

    return _call(body, name='mod_cols', grid=(1,),
                 in_specs=[_const(c_all.shape), _const(w_ada.shape), _const(b_cols.shape)],
                 out_specs=[_const((N_DEV, ADA_SHARD)), _const(c_all.shape)],
                 out_shape=[_sds((N_DEV, ADA_SHARD)), _sds(c_all.shape)], vmem=VMEM_BIG)(c_all, w_ada, b_cols)


def _grad_w_ada(act_t, dmod_cols):
    def body(a_ref, d_ref, o_ref):
        o_ref[...] = _dot(a_ref[...], d_ref[...])

    return _call(body, name='grad_w_ada', grid=(1,), in_specs=[_const(act_t.shape), _const(dmod_cols.shape)],
                 out_specs=_const((D_MODEL, ADA_SHARD)), out_shape=_sds((D_MODEL, ADA_SHARD)),
                 vmem=VMEM_BIG)(act_t, dmod_cols)


def _pre_mix(x, sc, sh, g, w_s, tm, ride):
    T = x.shape[0]

    def body(x_ref, sc_ref, sh_ref, g_ref, w_ref, proj_ref, h_ref):
        @pl.when(pl.program_id(1) == 0)
        def _():
            xv = x_ref[...]
            h_ref[...] = ((xv * _rsqrt_mean(xv) * g_ref[...]) * (1.0 + sc_ref[...]) + sh_ref[...]).astype(BF16)

        proj_ref[...] = _dot(h_ref[...], w_ref[...])

    row = pl.BlockSpec((tm, D_MODEL), lambda i, j: (i, 0))
    vec = _const((1, D_MODEL))
    return _call(body, name='pre_mix', grid=(T // tm, N_DEV),
                 in_specs=[row, vec, vec, vec, pl.BlockSpec((None, D_MODEL, IN_SHARD), lambda i, j: (j, 0, 0))],
                 out_specs=[pl.BlockSpec((tm, IN_SHARD), lambda i, j: (i, j)), row],
                 out_shape=[_sds((T, D_IN_PROJ)), _sds((T, D_MODEL), BF16)],
                 sem=('parallel', 'arbitrary'), ride=ride)(x, sc, sh, g, w_s)


def _halo_before(tm, rows=HALO):
    return lambda i: jnp.maximum(i * (tm // rows) - 1, 0)


def _halo_after(tm, T, rows=HALO):
    return lambda i: jnp.minimum((i + 1) * (tm // rows), T // rows - 1)


def _mix_fwd(yssm, proj, d, glu_w, glu_b, g_ssm, cw, g_conv, avg16, avg64, tm):
    T = yssm.shape[0]
    hb = _halo_before(tm)

    def body(y_ref, p_ref, ph_ref, d_ref, gw_ref, gb_ref, gs_ref, cw_ref, gc_ref, a16_ref, a64_ref, o_ref):
        i = pl.program_id(0)
        u = p_ref[:, 0:D_SSM]
        y = y_ref[...] + d_ref[...] * u
        z, _ = _gelu(y)
        gate = _sigmoid(_dot(z.astype(BF16), gw_ref[...]) + gb_ref[...])
        ya = z * gate
        rs = lax.rsqrt(_dot_split(ya * ya, a16_ref[...], 2) + EPS)
        o_ref[:, 0:D_SSM] = (ya * rs * gs_ref[...]).astype(BF16)
        bg = p_ref[:, D_SSM:D_SSM + D_CONV]
        cv = p_ref[:, D_SSM + D_CONV:D_SSM + 2 * D_CONV] * p_ref[:, D_SSM + 2 * D_CONV:D_IN_PROJ]
        hv = ph_ref[:, D_SSM + D_CONV:D_SSM + 2 * D_CONV] * ph_ref[:, D_SSM + 2 * D_CONV:D_IN_PROJ]
        hv = jnp.where(i > 0, hv, 0.0)
        conv, _, _ = _conv3(cv, hv, cw_ref)
        yb = bg * conv
        rsb = lax.rsqrt(_dot_split(yb * yb, a64_ref[...], 2) + EPS)
        o_ref[:, D_SSM:D_MODEL] = (yb * rsb * gc_ref[...]).astype(BF16)

    vec = _const((1, D_SSM))
    sq = _const((D_SSM, D_SSM))
    return _call(body, name='mix_fwd', grid=(T // tm,),
                 in_specs=[pl.BlockSpec((tm, D_SSM), lambda i: (i, 0)), pl.BlockSpec((tm, D_IN_PROJ), lambda i: (i, 0)),
                           pl.BlockSpec((HALO, D_IN_PROJ), lambda i: (hb(i), 0)), vec, sq, vec, vec,
                           _const((3, D_CONV)), vec, sq, sq],
                 out_specs=pl.BlockSpec((tm, D_MODEL), lambda i: (i, 0)), out_shape=_sds((T, D_MODEL), BF16),
                 sem=('parallel',), vmem=VMEM_BIG)(yssm, proj, proj, d, glu_w, glu_b, g_ssm, cw, g_conv, avg16, avg64)


def _out_proj(ycat, w_out, x, gt, g_post, g_pre, sc, sh, tm):
    T = x.shape[0]

    def body(y_ref, w_ref, x_ref, gt_ref, gp_ref, g2_ref, sc_ref, sh_ref, o_ref, x1_ref, h_ref):
        o = _dot(y_ref[...], w_ref[...])
        o_ref[...] = o
        x1 = x_ref[...] + gt_ref[...] * (o * _rsqrt_mean(o) * gp_ref[...])
        x1_ref[...] = x1
        h_ref[...] = ((x1 * _rsqrt_mean(x1) * g2_ref[...]) * (1.0 + sc_ref[...]) + sh_ref[...]).astype(BF16)

    row = pl.BlockSpec((tm, D_MODEL), lambda i: (i, 0))
    vec = _const((1, D_MODEL))
    return _call(body, name='out_proj', grid=(T // tm,),
                 in_specs=[row, _const((D_MODEL, D_MODEL)), row, vec, vec, vec, vec, vec],
                 out_specs=[row, row, row],
                 out_shape=[_sds((T, D_MODEL)), _sds((T, D_MODEL)), _sds((T, D_MODEL), BF16)],
                 sem=('parallel',), vmem=VMEM_BIG)(ycat, w_out, x, gt, g_post, g_pre, sc, sh)


def _ffn_up(h2, w_s, tm):
    T = h2.shape[0]

    def body(h_ref, w_ref, o_ref):
        o_ref[...] = _dot(h_ref[...], w_ref[...]).astype(BF16)

    return _call(body, name='ffn_up', grid=(T // tm, N_DEV),
                 in_specs=[pl.BlockSpec((tm, D_MODEL), lambda i, j: (i, 0)),
                           pl.BlockSpec((None, D_MODEL, FF_SHARD), lambda i, j: (j, 0, 0))],
                 out_specs=pl.BlockSpec((None, tm, FF_SHARD), lambda i, j: (j, i, 0)),
                 out_shape=_sds((N_DEV, T, FF_SHARD), BF16), sem=('parallel', 'parallel'))(h2, w_s)


def _ffn_hidden(up_ref, halo_ref, cw_ref, i):
    hid = []
    for part in range(2):
        halo = jnp.where(i > 0, halo_ref[part].astype(F32), 0.0)
        hid.append(_conv3(up_ref[part].astype(F32), halo, cw_ref.at[part])[0])
    return hid


def _ffn_act(up4, cw4, tm):
    T = up4.shape[2]
    hb = _halo_before(tm, HALO16)

    def body(up_ref, halo_ref, cw_ref, o_ref):
        hid_a, hid_v = _ffn_hidden(up_ref, halo_ref, cw_ref, pl.program_id(0))
        o_ref[...] = (_silu_parts(hid_a)[0] * hid_v).astype(BF16)

    return _call(body, name='ffn_act', grid=(T // tm, 4),
                 in_specs=[pl.BlockSpec((2, None, tm, FF_SHARD), lambda i, j: (0, j, i, 0)),
                           pl.BlockSpec((2, None, HALO16, FF_SHARD), lambda i, j: (0, j, hb(i), 0)),
                           pl.BlockSpec((2, None, 3, FF_SHARD), lambda i, j: (0, j, 0, 0))],
                 out_specs=pl.BlockSpec((None, tm, FF_SHARD), lambda i, j: (j, i, 0)),
                 out_shape=_sds((4, T, FF_SHARD), BF16), sem=('parallel', 'parallel'))(up4, up4, cw4)


def _ffn_down(act, wd4, x1, tgt, gt, g_post, tm):
    T = x1.shape[0]
    nb = T // tm

    def body(a_ref, w_ref, x1_ref, t_ref, gt_ref, g_ref, ddn_ref, dx_ref, loss_ref, dgt_ref, dg_ref, dn_ref):
        i, j = pl.program_id(0), pl.program_id(1)
        part = _dot(a_ref[...], w_ref[...])

        @pl.when(jnp.logical_and(i == 0, j == 0))
        def _():
            dgt_ref[...] = jnp.zeros_like(dgt_ref)
            dg_ref[...] = jnp.zeros_like(dg_ref)

        @pl.when(j == 0)
        def _():
            dn_ref[...] = part

        @pl.when(j > 0)
        def _():
            dn_ref[...] += part

        @pl.when(j == 3)
        def _():
            dn, gv, gate = dn_ref[...], g_ref[...], gt_ref[...]
            r = _rsqrt_mean(dn)
            normed = dn * r * gv
            err = x1_ref[...] + gate * normed - t_ref[...]
            dx = err * (1.0 / D_MODEL)
            dx_ref[...] = dx
            tot = jnp.sum(jnp.sum(err * err, axis=1, keepdims=True), axis=0, keepdims=True) * (0.5 / D_MODEL)
            loss_ref[...] = jnp.broadcast_to(tot, (8, 128))
            dgt_ref[...] += _colsum(dx * normed)
            dnn = dx * gate
            dg_ref[...] += _colsum(dnn * dn * r)
            ddn_ref[...] = _norm_bwd(dnn, dn, r, gv).astype(BF16)

    row = pl.BlockSpec((tm, D_MODEL), lambda i, j: (i, 0))
    vec = _const((1, D_MODEL))
    return _call(body, name='ffn_down', grid=(nb, 4),
                 in_specs=[pl.BlockSpec((None, tm, FF_SHARD), lambda i, j: (j, i, 0)),
                           pl.BlockSpec((None, FF_SHARD, D_MODEL), lambda i, j: (j, 0, 0)), row, row, vec, vec],
                 out_specs=[row, row, pl.BlockSpec((None, 8, 128), lambda i, j: (i, 0, 0)), vec, vec],
                 out_shape=[_sds((T, D_MODEL), BF16), _sds((T, D_MODEL)), _sds((nb, 8, 128)), _sds((1, D_MODEL)),
                            _sds((1, D_MODEL))],
                 scratch=[pltpu.VMEM((tm, D_MODEL), F32)], sem=('arbitrary', 'arbitrary'),
                 vmem=VMEM_BIG)(act, wd4, x1, tgt, gt, g_post)


def _ssm_prep(lre, lim, lst, b_re, b_im):
    def body(lre_ref, lim_ref, lst_ref, br_ref, bi_ref, ar_ref, ai_ref, bbr_ref, bbi_ref):
        ar, ai, qr, qi = _zoh(lre_ref[...], lim_ref[...], lst_ref[...])[:4]
        ar_ref[...] = ar
        ai_ref[...] = ai
        bbr_ref[...] = qr * br_ref[...] - qi * bi_ref[...]
        bbi_ref[...] = qr * bi_ref[...] + qi * br_ref[...]

    shp = lre.shape
    return _call(body, name='ssm_prep', grid=(1,), in_specs=[_const(shp)] * 5, out_specs=[_const(shp)] * 4,
                 out_shape=[_sds(shp)] * 4)(lre, lim, lst, b_re, b_im)


def _zoh(lre, lim, lst):
    lr = jnp.minimum(lre, LAMBDA_RE_MAX)
    st = jnp.exp(lst)
    mag = jnp.exp(lr * st)
    ar = mag * jnp.cos(lim * st)
    ai = mag * jnp.sin(lim * st)
    den = lr * lr + lim * lim
    qr = ((ar - 1.0) * lr + ai * lim) / den
    qi = (ai * lr - (ar - 1.0) * lim) / den
    return ar, ai, qr, qi, lr, st, den


def _ssm_prep_bwd(lre, lim, lst, b_re, b_im, dbbr, dbbi, dar, dai, seg):
    def body(lre_ref, lim_ref, lst_ref, br_ref, bi_ref, dbbr_ref, dbbi_ref, dar_ref, dai_ref, seg_ref,
             dbr_ref, dbi_ref, dlre_ref, dlim_ref, dlst_ref):
        lre_v = lre_ref[...]
        li = lim_ref[...]
        ar, ai, qr, qi, lr, st, den = _zoh(lre_v, li, lst_ref[...])
        br, bi, gbr, gbi = br_ref[...], bi_ref[...], dbbr_ref[...], dbbi_ref[...]
        dbr_ref[...] = qr * gbr + qi * gbi
        dbi_ref[...] = qr * gbi - qi * gbr
        gqr = _dot_split(br * gbr + bi * gbi, seg_ref[...], 3)
        gqi = _dot_split(br * gbi - bi * gbr, seg_ref[...], 3)
        ir, ii = lr / den, -li / den
        gar = dar_ref[...] + ir * gqr + ii * gqi
        gai = dai_ref[...] + ir * gqi - ii * gqr
        tr, ti = qr * ir - qi * ii, qr * ii + qi * ir
        glr = -(tr * gqr + ti * gqi)
        gli = -(tr * gqi - ti * gqr)
        gzr = ar * gar + ai * gai
        gzi = ar * gai - ai * gar
        glr = glr + st * gzr
        gli = gli + st * gzi
        gst = (lr * gzr + li * gzi) * st
        dlre_ref[...] = jnp.where(lre_v < LAMBDA_RE_MAX, glr, 0.0)
        dlim_ref[...] = gli
        dlst_ref[...] = jnp.sum(gst, axis=1, keepdims=True) * (1.0 / SSM_GROUP)

    shp = lre.shape
    return _call(body, name='ssm_prep_bwd', grid=(1,), in_specs=[_const(shp)] * 9 + [_const(seg.shape)],
                 out_specs=[_const(shp)] * 4 + [_const((N_GROUPS, 1))],
                 out_shape=[_sds(shp)] * 4 + [_sds((N_GROUPS, 1))], vmem=VMEM_BIG)(
                     lre, lim, lst, b_re, b_im, dbbr, dbbi, dar, dai, seg)


def _scan_specs(T):
    half = lambda cb: cb // 2
    return dict(
        chan=pl.BlockSpec((T, CHAN_BLOCK), lambda cb: (0, half(cb))),
        state=pl.BlockSpec((T, STATE_BLOCK), lambda cb: (0, cb)),
        b=pl.BlockSpec((CHAN_BLOCK, STATE_BLOCK), lambda cb: (half(cb), cb)),
        c=pl.BlockSpec((STATE_BLOCK, CHAN_BLOCK), lambda cb: (cb, half(cb))),
        lam=pl.BlockSpec((1, STATE_BLOCK), lambda cb: (0, cb)),
    )


def _complex_power(re, im, n):
    out = None
    while True:
        if n & 1:
            out = (re, im) if out is None else (out[0] * re - out[1] * im, out[0] * im + out[1] * re)
        n >>= 1
        if n == 0:
            return out
        re, im = re * re - im * im, 2.0 * re * im


def _rows8(i):
    return pl.ds(pl.multiple_of(i * SUBLANES, SUBLANES), SUBLANES)


def _ssm_fwd(u_perm, b_re, b_im, c_re, c_im, lam_r, lam_i, ride):
    T = u_perm.shape[0]
    ls = T // SUBLANES
    rc = min(512, T)
    sp = _scan_specs(T)

    def body(u_ref, bre_ref, bim_ref, cre_ref, cim_ref, lr_ref, li_ref, sre_ref, sim_ref, y_ref):
        cb = pl.program_id(0)
        for c in range(T // rc):
            rows = pl.ds(c * rc, rc)
            sre_ref[rows, :] = _dot(u_ref[rows, :], bre_ref[...])
            sim_ref[rows, :] = _dot(u_ref[rows, :], bim_ref[...])
        shp = (SUBLANES, STATE_BLOCK)
        lr = jnp.broadcast_to(lr_ref[...], shp)
        li = jnp.broadcast_to(li_ref[...], shp)
        zero = jnp.zeros(shp, F32)

        def step(i, carry):
            sr, si = carry
            rows = _rows8(i)
            nr = lr * sr - li * si + sre_ref[rows, :]
            ni = lr * si + li * sr + sim_ref[rows, :]
            sre_ref[rows, :] = nr
            sim_ref[rows, :] = ni
            return nr, ni

        fr, fi = lax.fori_loop(0, ls, step, (zero, zero), unroll=SCAN_UNROLL)
        pr, pi_ = _complex_power(lr, li, ls)
        row = lax.broadcasted_iota(jnp.int32, shp, 0)
        ir, ii = zero, zero
        for _ in range(SUBLANES - 1):
            er = fr + pr * ir - pi_ * ii
            ei = fi + pr * ii + pi_ * ir
            ir = jnp.where(row == 0, 0.0, pltpu.roll(er, 1, 0))
            ii = jnp.where(row == 0, 0.0, pltpu.roll(ei, 1, 0))

        def fix(i, carry):
            cr, ci = carry
            rows = _rows8(i)
            nr = lr * cr - li * ci
            ni = lr * ci + li * cr
            sre_ref[rows, :] += nr
            sim_ref[rows, :] += ni
            return nr, ni

        lax.fori_loop(0, ls, fix, (ir, ii), unroll=SCAN_UNROLL)
        for c in range(T // rc):
            rows = pl.ds(c * rc, rc)
            yc = _dot(sre_ref[rows, :].astype(BF16), cre_ref[...]) - _dot(sim_ref[rows, :].astype(BF16), cim_ref[...])

            @pl.when(cb % 2 == 0)
            def _():
                y_ref[rows, :] = yc

            @pl.when(cb % 2 == 1)
            def _():
                y_ref[rows, :] += yc

    return _call(body, name='ssm_fwd', grid=(N_STATE // STATE_BLOCK,),
                 in_specs=[sp['chan'], sp['b'], sp['b'], sp['c'], sp['c'], sp['lam'], sp['lam']],
                 out_specs=[sp['state'], sp['state'], sp['chan']],
                 out_shape=[_sds((T, N_STATE)), _sds((T, N_STATE)), _sds((T, D_SSM))],
                 sem=('arbitrary',), vmem=VMEM_BIG, ride=ride)(u_perm, b_re, b_im, c_re, c_im, lam_r, lam_i)


def _ssm_bwd(dy_perm, u_perm, s_re, s_im, b_re, b_im, c_re, c_im, lam_r, lam_i, ride):
    T = u_perm.shape[0]
    ls = T // SUBLANES
    rc = min(512, T)
    sp = _scan_specs(T)
    ncb = N_STATE // STATE_BLOCK

    def body(dy_ref, u_ref, sre_ref, sim_ref, bre_ref, bim_ref, cre_ref, cim_ref, lr_ref, li_ref,
             du_ref, dbr_ref, dbi_ref, dcr_ref, dci_ref, dar_ref, dai_ref, gre_ref, gim_ref):
        cb = pl.program_id(0)
        for c in range(T // rc):
            rows = pl.ds(c * rc, rc)
            gre_ref[rows, :] = _dot_nt(dy_ref[rows, :], cre_ref[...])
            gim_ref[rows, :] = -_dot_nt(dy_ref[rows, :], cim_ref[...])
        shp = (SUBLANES, STATE_BLOCK)
        lr = jnp.broadcast_to(lr_ref[...], shp)
        li = jnp.broadcast_to(li_ref[...], shp)
        zero = jnp.zeros(shp, F32)

        def step(k, carry):
            gr, gi = carry
            rows = _rows8(ls - 1 - k)
            nr = lr * gr + li * gi + gre_ref[rows, :]
            ni = lr * gi - li * gr + gim_ref[rows, :]
            gre_ref[rows, :] = nr
            gim_ref[rows, :] = ni
            return nr, ni

        fr, fi = lax.fori_loop(0, ls, step, (zero, zero), unroll=SCAN_UNROLL)
        pr, pi_ = _complex_power(lr, -li, ls)
        row = lax.broadcasted_iota(jnp.int32, shp, 0)
        cr, ci = zero, zero
        for _ in range(SUBLANES - 1):
            er = fr + pr * cr - pi_ * ci
            ei = fi + pr * ci + pi_ * cr
            cr = jnp.where(row == SUBLANES - 1, 0.0, pltpu.roll(er, SUBLANES - 1, 0))
            ci = jnp.where(row == SUBLANES - 1, 0.0, pltpu.roll(ei, SUBLANES - 1, 0))

        def fix(k, carry):
            dr, di, ar, ai = carry
            rows = _rows8(ls - 1 - k)
            dr, di = lr * dr + li * di, lr * di - li * dr
            gr = gre_ref[rows, :] + dr
            gi = gim_ref[rows, :] + di
            gre_ref[rows, :] = gr
            gim_ref[rows, :] = gi
            prev = _rows8(ls - 2 - k)
            spr, spi = sre_ref[prev, :], sim_ref[prev, :]
            return dr, di, ar + gr * spr + gi * spi, ai + gi * spr - gr * spi

        dr, di, ar, ai = lax.fori_loop(0, ls - 1, fix, (cr, ci, zero, zero), unroll=SCAN_UNROLL)
        first = pl.ds(0, SUBLANES)
        last = pl.ds((ls - 1) * SUBLANES, SUBLANES)
        gr = gre_ref[first, :] + (lr * dr + li * di)
        gi = gim_ref[first, :] + (lr * di - li * dr)
        gre_ref[first, :] = gr
        gim_ref[first, :] = gi
        spr = jnp.where(row == 0, 0.0, pltpu.roll(sre_ref[last, :], 1, 0))
        spi = jnp.where(row == 0, 0.0, pltpu.roll(sim_ref[last, :], 1, 0))
        dar_ref[...] = _colsum(ar + gr * spr + gi * spi)
        dai_ref[...] = _colsum(ai + gi * spr - gr * spi)

        for c in range(T // rc):
            rows = pl.ds(c * rc, rc)
            g_r, g_i = gre_ref[rows, :].astype(BF16), gim_ref[rows, :].astype(BF16)
            s_r, s_i = sre_ref[rows, :].astype(BF16), sim_ref[rows, :].astype(BF16)
            ub, dyb = u_ref[rows, :], dy_ref[rows, :]
            duc = _dot_nt(g_r, bre_ref[...]) + _dot_nt(g_i, bim_ref[...])
            parts = (_dot_tn(ub, g_r), _dot_tn(ub, g_i), _dot_tn(s_r, dyb), -_dot_tn(s_i, dyb))
            outs = (dbr_ref, dbi_ref, dcr_ref, dci_ref)
            for o_ref, part in zip(outs, parts):
                if c == 0:
                    o_ref[...] = part
                else:
                    o_ref[...] += part

            @pl.when(cb % 2 == 0)
            def _():
                du_ref[rows, :] = duc

            @pl.when(cb % 2 == 1)
            def _():
                du_ref[rows, :] += duc

    blk = lambda r, c: pl.BlockSpec((None, r, c), lambda cb: (cb, 0, 0))
    return _call(body, name='ssm_bwd', grid=(ncb,),
                 in_specs=[sp['chan'], sp['chan'], sp['state'], sp['state'], sp['b'], sp['b'], sp['c'], sp['c'],
                           sp['lam'], sp['lam']],
                 out_specs=[sp['chan'], blk(CHAN_BLOCK, STATE_BLOCK), blk(CHAN_BLOCK, STATE_BLOCK),
                            blk(STATE_BLOCK, CHAN_BLOCK), blk(STATE_BLOCK, CHAN_BLOCK), blk(1, STATE_BLOCK),
                            blk(1, STATE_BLOCK)],
                 out_shape=[_sds((T, D_SSM)), _sds((ncb, CHAN_BLOCK, STATE_BLOCK)), _sds((ncb, CHAN_BLOCK, STATE_BLOCK)),
                            _sds((ncb, STATE_BLOCK, CHAN_BLOCK)), _sds((ncb, STATE_BLOCK, CHAN_BLOCK)),
                            _sds((ncb, 1, STATE_BLOCK)), _sds((ncb, 1, STATE_BLOCK))],
                 scratch=[pltpu.VMEM((T, STATE_BLOCK), F32), pltpu.VMEM((T, STATE_BLOCK), F32)],
                 sem=('arbitrary',), vmem=VMEM_BIG, ride=ride)(dy_perm, u_perm, s_re, s_im, b_re, b_im, c_re, c_im,
                                                               lam_r, lam_i)


def _ffn_dact(ddn, wd4, up4, cw4, tm):
    T = ddn.shape[0]
    hb = _halo_before(tm, HALO16)

    def body(d_ref, w_ref, up_ref, halo_ref, cw_ref, o_ref):
        dact = _dot_nt(d_ref[...], w_ref[...])
        hid_a, hid_v = _ffn_hidden(up_ref, halo_ref, cw_ref, pl.program_id(0))
        silu, dsilu = _silu_parts(hid_a)
        o_ref[0] = (dact * hid_v * dsilu).astype(BF16)
        o_ref[1] = (dact * silu).astype(BF16)

    return _call(body, name='ffn_dact', grid=(T // tm, 4),
                 in_specs=[pl.BlockSpec((tm, D_MODEL), lambda i, j: (i, 0)),
                           pl.BlockSpec((None, FF_SHARD, D_MODEL), lambda i, j: (j, 0, 0)),
                           pl.BlockSpec((2, None, tm, FF_SHARD), lambda i, j: (0, j, i, 0)),
                           pl.BlockSpec((2, None, HALO16, FF_SHARD), lambda i, j: (0, j, hb(i), 0)),
                           pl.BlockSpec((2, None, 3, FF_SHARD), lambda i, j: (0, j, 0, 0))],
                 out_specs=pl.BlockSpec((2, None, tm, FF_SHARD), lambda i, j: (0, j, i, 0)),
                 out_shape=_sds((2, 4, T, FF_SHARD), BF16), sem=('parallel', 'parallel'))(ddn, wd4, up4, up4, cw4)


def _ffn_dup(dhid8, up8, cw8, tm, ride):
    T = up8.shape[1]
    nb = T // tm
    ha = _halo_after(tm, T, HALO16)

    def body(dh_ref, dha_ref, up_ref, cw_ref, dup_ref, dcw_ref):
        i = pl.program_id(1)

        @pl.when(i == 0)
        def _():
            dcw_ref[...] = jnp.zeros_like(dcw_ref)

        dh = dh_ref[...].astype(F32)
        dup, dh1, dh2 = _conv3_t(dh, jnp.where(i < nb - 1, dha_ref[...].astype(F32), 0.0), cw_ref)
        dup_ref[...] = dup.astype(BF16)
        up = up_ref[...].astype(F32)
        dcw_ref[0:1, :] += _colsum(dh2 * up)
        dcw_ref[1:2, :] += _colsum(dh1 * up)
        dcw_ref[2:3, :] += _colsum(dh * up)

    main = pl.BlockSpec((None, tm, FF_SHARD), lambda j, i: (j, i, 0))
    return _call(body, name='ffn_dup', grid=(N_DEV, nb),
                 in_specs=[main, pl.BlockSpec((None, HALO16, FF_SHARD), lambda j, i: (j, ha(i), 0)), main,
                           pl.BlockSpec((None, 3, FF_SHARD), lambda j, i: (j, 0, 0))],
                 out_specs=[main, pl.BlockSpec((None, 8, FF_SHARD), lambda j, i: (j, 0, 0))],
                 out_shape=[_sds((N_DEV, T, FF_SHARD), BF16), _sds((N_DEV, 8, FF_SHARD))],
                 sem=('parallel', 'arbitrary'), ride=ride)(dhid8, dhid8, up8, cw8)


def _grad_tn(a, b, a_spec, b_spec, groups, m, n, tk, name, ride=None, parts=1):
    T = a.shape[-2]
    nk = T // tk
    mp = m // parts

    def body(a_ref, b_ref, *refs):
        o_refs, acc_ref = refs[:parts], refs[parts]
        k = pl.program_id(1)
        part = _dot_tn(a_ref[...], b_ref[...])

        @pl.when(k == 0)
        def _():
            acc_ref[...] = part

        @pl.when(k > 0)
        def _():
            acc_ref[...] += part

        @pl.when(k == nk - 1)
        def _():
            for p, o_ref in enumerate(o_refs):
                o_ref[...] = acc_ref[p * mp:(p + 1) * mp, :].astype(BF16)

    out_spec = pl.BlockSpec((None, mp, n), lambda g, k: (g, 0, 0))
    res = _call(body, name=name, grid=(groups, nk), in_specs=[a_spec, b_spec], out_specs=[out_spec] * parts,
                out_shape=[_sds((groups, mp, n), BF16)] * parts, scratch=[pltpu.VMEM((m, n), F32)],
                sem=('parallel', 'arbitrary'), vmem=VMEM_BIG, ride=ride)(a, b)
    if parts > 1:
        return res
    return res[0] if ride is None else (res[0][0], res[1])


def _pre_norm_bwd(dz, dz_spec, w_s, xin, dres, sc, g, tm, name, ride, below=None):
    T = xin.shape[0]
    n = w_s.shape[2]

    def body(dz_ref, w_ref, x_ref, dr_ref, sc_ref, g_ref, *refs):
        if below is None:
            dx_ref, dsh_ref, dsc_ref, dg_ref = refs
            sums = (dsh_ref, dsc_ref, dg_ref)
        else:
            v_ref, gate_ref, g2_ref, dx_ref, dsh_ref, dsc_ref, dg_ref, dv_ref, dgate_ref, dg2_ref = refs
            sums = (dsh_ref, dsc_ref, dg_ref, dgate_ref, dg2_ref)
        i, j = pl.program_id(0), pl.program_id(1)
        part = _dot_nt(dz_ref[...], w_ref[...])

        @pl.when(jnp.logical_and(i == 0, j == 0))
        def _():
            for s_ref in sums:
                s_ref[...] = jnp.zeros_like(s_ref)

        @pl.when(j == 0)
        def _():
            dx_ref[...] = part

        @pl.when(j > 0)
        def _():
            dx_ref[...] += part

        @pl.when(j == N_DEV - 1)
        def _():
            dh, xv, gv = dx_ref[...], x_ref[...], g_ref[...]
            r = _rsqrt_mean(xv)
            dsh_ref[...] += _colsum(dh)
            dsc_ref[...] += _colsum(dh * (xv * r * gv))
            dxn = dh * (1.0 + sc_ref[...])
            dg_ref[...] += _colsum(dxn * xv * r)
            dx = dr_ref[...] + _norm_bwd(dxn, xv, r, gv)
            dx_ref[...] = dx
            if below is not None:
                v, g2 = v_ref[...], g2_ref[...]
                rv = _rsqrt_mean(v)
                dgate_ref[...] += _colsum(dx * (v * rv * g2))
                dn = dx * gate_ref[...]
                dg2_ref[...] += _colsum(dn * v * rv)
                dv_ref[...] = _norm_bwd(dn, v, rv, g2).astype(BF16)

    row = pl.BlockSpec((tm, D_MODEL), lambda i, j: (i, 0))
    vec = _const((1, D_MODEL))
    in_specs = [dz_spec, pl.BlockSpec((None, D_MODEL, n), lambda i, j: (j, 0, 0)), row, row, vec, vec]
    out_specs = [row, vec, vec, vec]
    out_shape = [_sds((T, D_MODEL)), _sds((1, D_MODEL)), _sds((1, D_MODEL)), _sds((1, D_MODEL))]
    args = [dz, w_s, xin, dres, sc, g]
    if below is not None:
        in_specs += [row, vec, vec]
        out_specs += [row, vec, vec]
        out_shape += [_sds((T, D_MODEL), BF16), _sds((1, D_MODEL)), _sds((1, D_MODEL))]
        args += list(below)
    return _call(body, name=name, grid=(T // tm, N_DEV), in_specs=in_specs, out_specs=out_specs,
                 out_shape=out_shape, sem=('arbitrary', 'arbitrary'), vmem=VMEM_MOST, ride=ride)(*args)


def _mix_bwd(d_o, w_out, yssm, proj, d, glu_w, glu_b, g_ssm, cw, g_conv, avg16, avg64, tm):
    T = yssm.shape[0]
    hb = _halo_before(tm)

    def body(do_ref, wo_ref, y_ref, p_ref, ph_ref, d_ref, gw_ref, gb_ref, gs_ref, cw_ref, gc_ref, a16_ref, a64_ref,
             dy_ref, dconv_ref, dbg_ref, z_ref, dlin_ref, acc_ref):
        i = pl.program_id(0)
        dyc = _dot_nt(do_ref[...], wo_ref[...])

        @pl.when(i == 0)
        def _():
            acc_ref[...] = jnp.zeros_like(acc_ref)

        u = p_ref[:, 0:D_SSM]
        y = y_ref[...] + d_ref[...] * u
        z, t = _gelu(y)
        gate = _sigmoid(_dot(z.astype(BF16), gw_ref[...]) + gb_ref[...])
        ya = z * gate
        rs = lax.rsqrt(_dot_split(ya * ya, a16_ref[...], 2) + EPS)
        dna = dyc[:, 0:D_SSM]
        acc_ref[1:2, :] += _colsum(dna * ya * rs)
        dya = _head_norm_bwd(dna, ya, rs, gs_ref[...], a16_ref[...])
        dlin = dya * z * gate * (1.0 - gate)
        acc_ref[0:1, :] += _colsum(dlin)
        dlin_b = dlin.astype(BF16)
        dz = dya * gate + _dot_nt(dlin_b, gw_ref[...])
        dy = dz * _gelu_grad(y, t)
        acc_ref[3:4, :] += _colsum(dy * u)
        dy_ref[...] = dy
        z_ref[...] = z.astype(BF16)
        dlin_ref[...] = dlin_b

        bg = p_ref[:, D_SSM:D_SSM + D_CONV]
        cv = p_ref[:, D_SSM + D_CONV:D_SSM + 2 * D_CONV] * p_ref[:, D_SSM + 2 * D_CONV:D_IN_PROJ]
        hv = ph_ref[:, D_SSM + D_CONV:D_SSM + 2 * D_CONV] * ph_ref[:, D_SSM + 2 * D_CONV:D_IN_PROJ]
        hv = jnp.where(i > 0, hv, 0.0)
        conv, cv1, cv2 = _conv3(cv, hv, cw_ref)
        yb = bg * conv
        rsb = lax.rsqrt(_dot_split(yb * yb, a64_ref[...], 2) + EPS)
        dnb = dyc[:, D_SSM:D_MODEL]
        acc_ref[2:3, :] += _colsum(dnb * yb * rsb)
        dyb = _head_norm_bwd(dnb, yb, rsb, gc_ref[...], a64_ref[...])
        dbg_ref[...] = dyb * conv
        dconv = dyb * bg
        dconv_ref[...] = dconv
        acc_ref[4:5, :] += _colsum(dconv * cv2)
        acc_ref[5:6, :] += _colsum(dconv * cv1)
        acc_ref[6:7, :] += _colsum(dconv * cv)

    vec = _const((1, D_SSM))
    sq = _const((D_SSM, D_SSM))
    half = pl.BlockSpec((tm, D_SSM), lambda i: (i, 0))
    return _call(body, name='mix_bwd', grid=(T // tm,),
                 in_specs=[pl.BlockSpec((tm, D_MODEL), lambda i: (i, 0)), _const((D_MODEL, D_MODEL)), half,
                           pl.BlockSpec((tm, D_IN_PROJ), lambda i: (i, 0)),
                           pl.BlockSpec((HALO, D_IN_PROJ), lambda i: (hb(i), 0)), vec, sq, vec, vec,
                           _const((3, D_CONV)), vec, sq, sq],
                 out_specs=[half, half, half, half, half, _const((8, D_SSM))],
                 out_shape=[_sds((T, D_SSM)), _sds((T, D_SSM)), _sds((T, D_SSM)), _sds((T, D_SSM), BF16),
                            _sds((T, D_SSM), BF16), _sds((8, D_SSM))],
                 sem=('arbitrary',), vmem=VMEM_BIG)(d_o, w_out, yssm, proj, proj, d, glu_w, glu_b, g_ssm, cw, g_conv,
                                                   avg16, avg64)


def _mix_bwd_proj(dconv, proj, du_ssm, dy, d, dbg, cw, tm):
    T = dy.shape[0]
    nb = T // tm
    ha = _halo_after(tm, T)

    def body(dc_ref, dch_ref, cg_ref, v_ref, du_ref, dy_ref, d_ref, dbg_ref, cw_ref, o_ref):
        i = pl.program_id(0)
        dcv = _conv3_t(dc_ref[...], jnp.where(i < nb - 1, dch_ref[...], 0.0), cw_ref)[0]
        o_ref[:, 0:D_SSM] = (du_ref[...] + dy_ref[...] * d_ref[...]).astype(BF16)
        o_ref[:, D_SSM:D_SSM + D_CONV] = dbg_ref[...].astype(BF16)
        o_ref[:, D_SSM + D_CONV:D_SSM + 2 * D_CONV] = (dcv * v_ref[...]).astype(BF16)
        o_ref[:, D_SSM + 2 * D_CONV:D_IN_PROJ] = (dcv * cg_ref[...]).astype(BF16)

    half = pl.BlockSpec((tm, D_SSM), lambda i: (i, 0))
    return _call(body, name='mix_bwd_proj', grid=(nb,),
                 in_specs=[half, pl.BlockSpec((HALO, D_CONV), lambda i: (ha(i), 0)),
                           pl.BlockSpec((tm, D_CONV), lambda i: (i, 2)), pl.BlockSpec((tm, D_CONV), lambda i: (i, 3)),
                           half, half, _const((1, D_SSM)), half, _const((3, D_CONV))],
                 out_specs=pl.BlockSpec((tm, D_IN_PROJ), lambda i: (i, 0)), out_shape=_sds((T, D_IN_PROJ), BF16),
                 sem=('parallel',))(dconv, dconv, proj, proj, du_ssm, dy, d, dbg, cw)


def _row_tile(rows, cols, slots):
    for cand in (512, 256, 128, 64, 32, 16, 8):
        if rows % cand == 0 and slots * cand * cols * 4 <= (2 << 20):
            return cand
    return rows


def _adamw_math(g, w, m, v):
    m2 = ADAM_B1 * m + (1.0 - ADAM_B1) * g
    v2 = ADAM_B2 * v + (1.0 - ADAM_B2) * (g * g)
    m_hat = m2 / (1.0 - ADAM_B1 ** ADAM_STEP)
    v_hat = v2 / (1.0 - ADAM_B2 ** ADAM_STEP)
    return -ADAM_LR * (m_hat / (jnp.sqrt(v_hat) + ADAM_EPS) + ADAM_WD * w), m2, v2


def _adamw(pieces, w, m, v, name):
    slots, _, cols = pieces[0].shape
    rows = sum(p.shape[1] for p in pieces)
    tr = _row_tile(pieces[0].shape[1], cols, slots)
    starts, pos = [], 0
    for p in pieces:
        assert p.shape[1] % tr == 0
        starts.append(pos)
        pos += p.shape[1] // tr

    def body(*refs):
        g_refs = refs[:len(pieces)]
        w_ref, m_ref, v_ref, go_ref, d_ref, mo_ref, vo_ref = refs[len(pieces):]
        i = pl.program_id(0)
        g = None
        for g_ref, start in zip(g_refs, starts):
            part = g_ref[0].astype(F32)
            for s in range(1, slots):
                part = part + g_ref[s].astype(F32)
            g = part if g is None else jnp.where(i >= start, part, g)
        go_ref[...] = g
        d_ref[...], mo_ref[...], vo_ref[...] = _adamw_math(g, w_ref[...], m_ref[...], v_ref[...])

    def piece_spec(start, count):
        return pl.BlockSpec((slots, tr, cols), lambda i: (0, jnp.clip(i - start, 0, count - 1), 0))

    blk = pl.BlockSpec((tr, cols), lambda i: (i, 0))
    return _call(body, name=name, grid=(rows // tr,),
                 in_specs=[piece_spec(s, p.shape[1] // tr) for s, p in zip(starts, pieces)] + [blk, blk, blk],
                 out_specs=[blk] * 4, out_shape=[_sds((rows, cols))] * 4, sem=('parallel',))(*pieces, w, m, v)


def _to_scan_rows(a):
    T, n = a.shape
    return a.reshape(SUBLANES, T // SUBLANES, n).transpose(1, 0, 2).reshape(T, n)


def _from_scan_rows(a):
    T, n = a.shape
    return a.reshape(T // SUBLANES, SUBLANES, n).transpose(1, 0, 2).reshape(T, n)


def _expand(a):
    return jnp.repeat(a, SSM_GROUP, axis=1)


def _block_diag_b(bb):
    eye = jnp.eye(N_GROUPS, dtype=bb.dtype)
    return (bb.transpose(0, 2, 1)[:, :, None, :] * eye[:, None, :, None]).reshape(D_SSM, N_STATE)


def _block_diag_c(cc):
    eye = jnp.eye(N_GROUPS, dtype=cc.dtype)
    return (cc.transpose(0, 2, 1)[:, :, None, :] * eye[:, None, :, None]).reshape(N_STATE, D_SSM)


def _diag_blocks(x, chan_major):
    e2 = jnp.eye(2, dtype=x.dtype)
    e4 = jnp.eye(4, dtype=x.dtype)
    if chan_major:
        x = x.reshape(4, 2, 2, 4, SSM_GROUP, 4, SSM_STATE)
        x = x * e2[None, :, :, None, None, None, None] * e4[None, None, None, :, None, :, None]
        return x.sum(axis=(2, 3)).transpose(0, 1, 3, 4, 2).reshape(N_GROUPS, SSM_STATE, SSM_GROUP)
    x = x.reshape(4, 2, 4, SSM_STATE, 2, 4, SSM_GROUP)
    x = x * e2[None, :, None, None, :, None, None] * e4[None, None, :, None, None, :, None]
    return x.sum(axis=(4, 5)).reshape(N_GROUPS, SSM_STATE, SSM_GROUP)


SMALL_LAYOUT = {
    'ssm_b_re': (0, 0, 32, 1024), 'ssm_b_im': (32, 0, 32, 1024), 'ssm_c_re': (64, 0, 32, 1024),
    'ssm_c_im': (96, 0, 32, 1024), 'b_ada': (128, 0, 6, 1024), 'g_pre_mix': (134, 0, 1, 1024),
    'g_post_mix': (135, 0, 1, 1024), 'ssm_lam_re': (136, 0, 2, 1024), 'ssm_lam_im': (138, 0, 2, 1024),
    'ssm_log_step': (140, 0, 1, 32), 'glu_b': (141, 0, 1, 512), 'g_out_ssm': (141, 512, 1, 512),
    'g_out_conv': (142, 0, 1, 512), 'ssm_d': (142, 512, 1, 512), 'g_pre_ffn': (143, 0, 1, 1024),
    'g_post_ffn': (144, 0, 1, 1024)}
SMALL_ROWS = 152
B_ADA_ROW = SMALL_LAYOUT['b_ada'][0]


def _adamw_small(gathered, wts, mom_m, mom_v):
    names = list(SMALL_LAYOUT)
    n = len(names)

    def body(*refs):
        g_ref, ins, outs = refs[0], refs[1:1 + 3 * n], refs[1 + 3 * n:]
        for p, name in enumerate(names):
            r0, c0, rows, cols = SMALL_LAYOUT[name]
            pieces = [(0, rows)] if rows % 8 == 0 else [(r, 1) for r in range(rows)]
            for r, cnt in pieces:
                g = g_ref[0, r0 + r:r0 + r + cnt, c0:c0 + cols]
                for s in range(1, N_DEV):
                    g = g + g_ref[s, r0 + r:r0 + r + cnt, c0:c0 + cols]
                w, m, v = (ins[3 * p + q][r:r + cnt, :] for q in range(3))
                res = (g,) + _adamw_math(g, w, m, v)
                for q in range(4):
                    outs[4 * p + q][r:r + cnt, :] = res[q]

    shapes = [SMALL_LAYOUT[name][2:] for name in names]
    args = [gathered]
    for name, shp in zip(names, shapes):
        args += [wts[name].reshape(shp), mom_m[name].reshape(shp), mom_v[name].reshape(shp)]
    outs = _call(body, name='adamw_small', grid=(1,),
                 in_specs=[_const(gathered.shape)] + [_const(shp) for shp in shapes for _ in range(3)],
                 out_specs=[_const(shp) for shp in shapes for _ in range(4)],
                 out_shape=[_sds(shp) for shp in shapes for _ in range(4)], vmem=VMEM_BIG)(*args)
    res = {}
    for p, name in enumerate(names):
        for q, kind in enumerate(('g', 'd', 'm', 'v')):
            res[kind, name] = outs[4 * p + q].reshape(wts[name].shape)
    return res


def kernel(x, c, w_ada, b_ada, g_pre_mix, g_post_mix, w_in, ssm_lam_re, ssm_lam_im, ssm_log_step, ssm_b_re, ssm_b_im, ssm_c_re, ssm_c_im, ssm_d, glu_w, glu_b, g_out_ssm, conv_w, g_out_conv, w_out, g_pre_ffn, g_post_ffn, w_up, ffn_conv_w, w_down, loss_target, m_w_ada, m_b_ada, m_g_pre_mix, m_g_post_mix, m_w_in, m_ssm_lam_re, m_ssm_lam_im, m_ssm_log_step, m_ssm_b_re, m_ssm_b_im, m_ssm_c_re, m_ssm_c_im, m_ssm_d, m_glu_w, m_glu_b, m_g_out_ssm, m_conv_w, m_g_out_conv, m_w_out, m_g_pre_ffn, m_g_post_ffn, m_w_up, m_ffn_conv_w, m_w_down, v_w_ada, v_b_ada, v_g_pre_mix, v_g_post_mix, v_w_in, v_ssm_lam_re, v_ssm_lam_im, v_ssm_log_step, v_ssm_b_re, v_ssm_b_im, v_ssm_c_re, v_ssm_c_im, v_ssm_d, v_glu_w, v_glu_b, v_g_out_ssm, v_conv_w, v_g_out_conv, v_w_out, v_g_pre_ffn, v_g_post_ffn, v_w_up, v_ffn_conv_w, v_w_down):
    args = dict(locals())
    wts = {n: args[n] for n in WEIGHTS}
    mom_m = {n: args['m_' + n] for n in WEIGHTS}
    mom_v = {n: args['v_' + n] for n in WEIGHTS}
    T = x.shape[1]
    tm = min(512, T)
    tw = min(1024, T)
    me = _me()[3]
    xt, tgt = x[0], loss_target[0]

    (c_all,) = _exchange([c], name='gather_c', scatter=False)
    c_all = c_all.reshape(N_DEV, D_MODEL)
    b_cols = lax.dynamic_slice(b_ada, (0, me * ADA_SHARD), (1, ADA_SHARD))
    mod_cols, c_act = _mod_cols(c_all, w_ada[0], b_cols)
    (mod_all,) = _exchange([mod_cols], name='gather_mod', scatter=False)
    mod = lax.dynamic_slice(mod_all, (0, me, 0), (N_DEV, 1, ADA_SHARD)).reshape(N_MOD, 1, D_MODEL)
    sh1, sc1, gt1, sh2, sc2, gt2 = [mod[k] for k in range(N_MOD)]

    w_in_s, glu_s, w_out_s, conv_s = _exchange(
        [w_in[0].astype(BF16), glu_w[0].astype(BF16), w_out[0].astype(BF16), conv_w[0]], name='gather_weights',
        scatter=False)
    glu_full = glu_s.reshape(D_SSM, D_SSM)
    w_out_full = w_out_s.reshape(D_MODEL, D_MODEL)
    cw_full = conv_s.transpose(1, 0, 2).reshape(3, D_CONV)

    lre_x, lim_x = _expand(ssm_lam_re[0]), _expand(ssm_lam_im[0])
    lst_x = jnp.broadcast_to(ssm_log_step[0][:, None], (N_GROUPS, SSM_STATE * SSM_GROUP))
    b_re_x = ssm_b_re[0].reshape(N_GROUPS, -1)
    b_im_x = ssm_b_im[0].reshape(N_GROUPS, -1)
    ar_x, ai_x, bbr_x, bbi_x = _ssm_prep(lre_x, lim_x, lst_x, b_re_x, b_im_x)
    lam_r = ar_x[:, ::SSM_GROUP].reshape(1, N_STATE)
    lam_i = ai_x[:, ::SSM_GROUP].reshape(1, N_STATE)
    big_b_re = _block_diag_b(bbr_x.reshape(N_GROUPS, SSM_STATE, SSM_GROUP)).astype(BF16)
    big_b_im = _block_diag_b(bbi_x.reshape(N_GROUPS, SSM_STATE, SSM_GROUP)).astype(BF16)
    big_c_re = _block_diag_c(ssm_c_re[0]).astype(BF16)
    big_c_im = _block_diag_c(ssm_c_im[0]).astype(BF16)
    head = jnp.arange(D_SSM)
    avg16 = jnp.where(head[:, None] // SSM_GROUP == head[None, :] // SSM_GROUP, 1.0 / SSM_GROUP, 0.0).astype(BF16)
    hd = D_CONV // CONV_HEADS
    avg64 = jnp.where(head[:, None] // hd == head[None, :] // hd, 1.0 / hd, 0.0).astype(BF16)

    (proj, h1), (w_down_s, ffn_conv_s) = _pre_mix(xt, sc1, sh1, g_pre_mix, w_in_s, tw,
                                                  ([w_down[0].astype(BF16), ffn_conv_w[0]], False))
    wd4 = w_down_s.reshape(4, FF_SHARD, D_MODEL)
    cw4 = ffn_conv_s.reshape(2, 4, 3, FF_SHARD)
    u_perm = _to_scan_rows(proj[:, :D_SSM]).astype(BF16)
    (s_re, s_im, y_perm), (w_up_s,) = _ssm_fwd(u_perm, big_b_re, big_b_im, big_c_re, big_c_im, lam_r, lam_i,
                                               ([w_up[0].astype(BF16)], False))
    yssm = _from_scan_rows(y_perm)
    mix_args = (ssm_d, glu_full, glu_b, g_out_ssm, cw_full, g_out_conv, avg16, avg64)
    ycat = _mix_fwd(yssm, proj, *mix_args, tm)
    o, x1, h2 = _out_proj(ycat, w_out_full, xt, gt1, g_post_mix, g_pre_ffn, sc2, sh2, tm)
    up8 = _ffn_up(h2, w_up_s, tw)
    up4 = up8.reshape(2, 4, T, FF_SHARD)
    act = _ffn_act(up4, cw4, tm)
    ddn, dx2, loss_parts, d_gt2, d_g_post_ffn = _ffn_down(act, wd4, x1, tgt, gt2, g_post_ffn, tw)
    loss = lax.psum(jnp.sum(loss_parts[:, 0, 0]), ('x', 'y', 'c'))

    got = {}
    dhid = _ffn_dact(ddn, wd4, up4, cw4, tm)
    g_w_down = _grad_tn(act, ddn, pl.BlockSpec((None, tw, FF_SHARD), lambda g, k: (g, k, 0)),
                        pl.BlockSpec((tw, D_MODEL), lambda g, k: (k, 0)), 4, FF_SHARD, D_MODEL, tw, 'grad_w_down')
    (dup8, dcw_ffn), (got['w_down'],) = _ffn_dup(dhid.reshape(N_DEV, T, FF_SHARD), up8, ffn_conv_s, tm,
                                                 ([g_w_down.reshape(N_DEV, D_FF // N_DEV, D_MODEL)], True))
    g_w_up_halves = _grad_tn(h2, dup8, pl.BlockSpec((tw, D_MODEL), lambda g, k: (k, 0)),
                             pl.BlockSpec((None, tw, FF_SHARD), lambda g, k: (g, k, 0)), N_DEV, D_MODEL, FF_SHARD, tw,
                             'grad_w_up', parts=2)
    (dx1, d_sh2, d_sc2, d_g_pre_ffn, d_o, d_gt1, d_g_post_mix), (got_up_0, got['ffn_conv_w']) = _pre_norm_bwd(
        dup8, pl.BlockSpec((None, tw, FF_SHARD), lambda i, j: (j, i, 0)), w_up_s, x1, dx2, sc2, g_pre_ffn, tw,
        'ffn_in_bwd', ([g_w_up_halves[0], dcw_ffn], True), below=(o, gt1, g_post_mix))

    g_w_out = _grad_tn(ycat, d_o, pl.BlockSpec((tw, D_MODEL), lambda g, k: (k, 0)),
                       pl.BlockSpec((tw, D_MODEL), lambda g, k: (k, 0)), 1, D_MODEL, D_MODEL, tw, 'grad_w_out')
    dy, dconv, dbg, z_b, dlin_b, sums = _mix_bwd(d_o, w_out_full, yssm, proj, *mix_args, tm)
    g_glu_w = _grad_tn(z_b, dlin_b, pl.BlockSpec((tw, D_SSM), lambda g, k: (k, 0)),
                       pl.BlockSpec((tw, D_SSM), lambda g, k: (k, 0)), 1, D_SSM, D_SSM, tw, 'grad_glu_w')
    dy_perm = _to_scan_rows(dy).astype(BF16)
    (du_perm, dbr_blk, dbi_blk, dcr_blk, dci_blk, dar_blk, dai_blk), (got_up_1, got['w_out'], got['glu_w']) = _ssm_bwd(
        dy_perm, u_perm, s_re, s_im, big_b_re, big_b_im, big_c_re, big_c_im, lam_r, lam_i,
        ([g_w_up_halves[1], g_w_out.reshape(N_DEV, D_MODEL // N_DEV, D_MODEL),
          g_glu_w.reshape(N_DEV, D_SSM // N_DEV, D_SSM)], True))
    du_ssm = _from_scan_rows(du_perm)
    dproj = _mix_bwd_proj(dconv, proj, du_ssm, dy, ssm_d, dbg, cw_full, tm)
    g_w_in = _grad_tn(h1, dproj, pl.BlockSpec((tw, D_MODEL), lambda g, k: (k, 0)),
                      pl.BlockSpec((tw, IN_SHARD), lambda g, k: (k, g)), N_DEV, D_MODEL, IN_SHARD, tw, 'grad_w_in')
    g_conv_slots = jnp.concatenate([sums[4:7], jnp.zeros((5, D_CONV), F32)]).reshape(
        8, N_DEV, D_CONV // N_DEV).transpose(1, 0, 2)
    (grad_x, d_sh1, d_sc1, d_g_pre_mix), (got['w_in'], got['conv_w']) = _pre_norm_bwd(
        dproj, pl.BlockSpec((tw, IN_SHARD), lambda i, j: (i, j)), w_in_s, xt, dx1, sc1, g_pre_mix, tw, 'mix_in_bwd',
        ([g_w_in, g_conv_slots], True))

    dbb_re = _diag_blocks(dbr_blk, True).reshape(N_GROUPS, -1)
    dbb_im = _diag_blocks(dbi_blk, True).reshape(N_GROUPS, -1)
    d_c_re = _diag_blocks(dcr_blk, False).transpose(0, 2, 1)
    d_c_im = _diag_blocks(dci_blk, False).transpose(0, 2, 1)
    lane = jnp.arange(SSM_STATE * SSM_GROUP)
    seg = jnp.where(lane[:, None] // SSM_GROUP == lane[None, :] // SSM_GROUP, 1.0, 0.0).astype(BF16)
    d_b_re_x, d_b_im_x, d_lre_x, d_lim_x, d_lst = _ssm_prep_bwd(
        lre_x, lim_x, lst_x, b_re_x, b_im_x, dbb_re, dbb_im, _expand(dar_blk.reshape(N_GROUPS, SSM_STATE)),
        _expand(dai_blk.reshape(N_GROUPS, SSM_STATE)), seg)

    row = lambda a: a.reshape(-1, PACK_COLS)
    small_pack = jnp.concatenate([
        d_b_re_x, d_b_im_x, row(d_c_re), row(d_c_im), d_sh1, d_sc1, d_gt1, d_sh2, d_sc2, d_gt2, d_g_pre_mix,
        d_g_post_mix, row(d_lre_x[:, ::SSM_GROUP]), row(d_lim_x[:, ::SSM_GROUP]),
        jnp.pad(d_lst.reshape(1, N_GROUPS), ((0, 0), (0, PACK_COLS - N_GROUPS))), row(sums[0:4]), d_g_pre_ffn,
        d_g_post_ffn, jnp.zeros((SMALL_ROWS - 145, PACK_COLS), F32)])
    (small_all,) = _exchange([small_pack], name='gather_small_grads', scatter=False)
    res = _adamw_small(small_all, wts, mom_m, mom_v)

    dmod_all = small_all[:, B_ADA_ROW:B_ADA_ROW + N_MOD, :].reshape(N_DEV, N_MOD * D_MODEL)
    dmod_cols = lax.dynamic_slice(dmod_all, (0, me * ADA_SHARD), (N_DEV, ADA_SHARD))
    g_w_ada = _grad_w_ada(c_act.T, dmod_cols)

    pieces = {n: [slots[:, :3, :] if n in ('conv_w', 'ffn_conv_w') else slots] for n, slots in got.items()}
    pieces['w_up'] = [got_up_0, got_up_1]
    for n, parts in pieces.items():
        outs = _adamw(parts, wts[n][0], mom_m[n][0], mom_v[n][0], 'adamw_' + n)
        for kind, val in zip(('g', 'd', 'm', 'v'), outs):
            res[kind, n] = val[None]
    outs = _adamw([g_w_ada[None]], w_ada[0], m_w_ada[0], v_w_ada[0], 'adamw_w_ada')
    for kind, val in zip(('g', 'd', 'm', 'v'), outs):
        res[kind, 'w_ada'] = val[None]

    return (loss, grad_x[None], *[res['g', n] for n in WEIGHTS], *[res['d', n] for n in WEIGHTS],
            *[res['m', n] for n in WEIGHTS], *[res['v', n] for n in WEIGHTS])
```

```python
import math

import jax
import jax.numpy as jnp
from jax import lax
from jax.experimental import pallas as pl
from jax.experimental.pallas import tpu as pltpu

F32, BF16 = jnp.float32, jnp.bfloat16

D_MODEL = 1024
D_SSM = 512
D_CONV = 512
SSM_GROUP = 16
N_GROUPS = 32
SSM_STATE = 64
N_STATE = N_GROUPS * SSM_STATE
CONV_HEADS = 8
D_FF = 2816
N_MOD = 6
D_IN_PROJ = D_SSM + 3 * D_CONV
N_DEV = 8
FF_SHARD = 2 * D_FF // N_DEV
IN_SHARD = D_IN_PROJ // N_DEV
ADA_SHARD = N_MOD * D_MODEL // N_DEV
EPS = 1e-6
LAMBDA_RE_MAX = -1e-4
ADAM_LR, ADAM_B1, ADAM_B2, ADAM_EPS, ADAM_WD, ADAM_STEP = 0.001, 0.9, 0.999, 1e-08, 0.01, 10
GELU_C = math.sqrt(2.0 / math.pi)
GELU_A = 0.044715

SUBLANES = 8
HALO = 8
HALO16 = 16
SCAN_UNROLL = 8
STATE_BLOCK = 256
CHAN_BLOCK = 128
VMEM_BIG = 48 << 20
VMEM_MOST = 58 << 20

WEIGHTS = ['w_ada', 'b_ada', 'g_pre_mix', 'g_post_mix', 'w_in', 'ssm_lam_re', 'ssm_lam_im', 'ssm_log_step',
           'ssm_b_re', 'ssm_b_im', 'ssm_c_re', 'ssm_c_im', 'ssm_d', 'glu_w', 'glu_b', 'g_out_ssm', 'conv_w',
           'g_out_conv', 'w_out', 'g_pre_ffn', 'g_post_ffn', 'w_up', 'ffn_conv_w', 'w_down']
SHARDED = ('w_ada', 'w_in', 'glu_w', 'conv_w', 'w_out', 'w_up', 'ffn_conv_w', 'w_down')
PACK_COLS = 1024


def _call(body, *, name, grid, in_specs, out_specs, out_shape, scratch=(), sem=None, vmem=None, ride=None):
    params = {}
    if vmem is not None:
        params['vmem_limit_bytes'] = vmem
    if ride is None:
        if sem is not None:
            params['dimension_semantics'] = sem
        return pl.pallas_call(body, name=name, grid=grid, in_specs=in_specs, out_specs=out_specs,
                              out_shape=out_shape, scratch_shapes=list(scratch),
                              compiler_params=pltpu.CompilerParams(**params))
    arrs, scatter = ride
    single = not isinstance(out_shape, (list, tuple))
    out_shape_l = [out_shape] if single else list(out_shape)
    out_specs_l = [out_specs] if single else list(out_specs)
    n, n_in, n_out, n_scr = len(arrs), len(in_specs), len(out_shape_l), len(scratch)
    any_spec = pl.BlockSpec(memory_space=pl.ANY)
    params['dimension_semantics'] = ('arbitrary',) * len(grid)

    def carried(*refs):
        ins, rin = refs[:n_in], refs[n_in:n_in + n]
        outs, rout = refs[n_in + n:n_in + n + n_out], refs[n_in + n + n_out:n_in + 2 * n + n_out]
        scr, sems = refs[n_in + 2 * n + n_out:n_in + 2 * n + n_out + n_scr], refs[n_in + 2 * n + n_out + n_scr:]
        first = pl.program_id(0) == 0
        last = pl.program_id(0) == grid[0] - 1
        for ax in range(1, len(grid)):
            first = jnp.logical_and(first, pl.program_id(ax) == 0)
            last = jnp.logical_and(last, pl.program_id(ax) == grid[ax] - 1)

        @pl.when(first)
        def _():
            _exchange_start(rin, rout, sems, scatter)

        body(*ins, *outs, *scr)

        @pl.when(last)
        def _():
            _exchange_wait(rin, rout, sems, scatter)

    call = pl.pallas_call(carried, name=name, grid=grid, in_specs=list(in_specs) + [any_spec] * n,
                          out_specs=out_specs_l + [any_spec] * n,
                          out_shape=out_shape_l + _exchange_shapes(arrs, scatter),
                          scratch_shapes=list(scratch) + _exchange_sems(n),
                          compiler_params=pltpu.CompilerParams(**params))

    def run(*args):
        res = call(*args, *arrs)
        own = res[0] if single else list(res[:n_out])
        return own, list(res[n_out:])

    return run


def _const(shape):
    nd = len(shape)
    return pl.BlockSpec(shape, lambda *_: (0,) * nd)


def _sds(shape, dtype=F32):
    return jax.ShapeDtypeStruct(shape, dtype)


def _dot(a, b):
    return jnp.dot(a, b, preferred_element_type=F32)


def _dot_nt(a, b):
    return lax.dot_general(a, b, (((1,), (1,)), ((), ())), preferred_element_type=F32)


def _dot_tn(a, b):
    return lax.dot_general(a, b, (((0,), (0,)), ((), ())), preferred_element_type=F32)


def _dot_split(x, mat, parts):
    acc = None
    rem = x
    for _ in range(parts):
        piece = rem.astype(BF16)
        rem = rem - piece.astype(F32)
        term = _dot(piece, mat)
        acc = term if acc is None else acc + term
    return acc


def _sigmoid(x):
    return 1.0 / (1.0 + jnp.exp(-x))


def _gelu(x):
    t = jnp.tanh(GELU_C * (x + GELU_A * x * x * x))
    return 0.5 * x * (1.0 + t), t


def _gelu_grad(x, t):
    return 0.5 * (1.0 + t) + 0.5 * x * (1.0 - t * t) * GELU_C * (1.0 + 3.0 * GELU_A * x * x)


def _rsqrt_mean(x):
    return lax.rsqrt(jnp.mean(x * x, axis=-1, keepdims=True) + EPS)


def _colsum(x):
    return jnp.sum(x, axis=0, keepdims=True)


def _shifts_down(x, halo):
    ext = jnp.concatenate([halo, x], axis=0)
    return pltpu.roll(ext, 1, 0)[halo.shape[0]:], pltpu.roll(ext, 2, 0)[halo.shape[0]:]


def _shifts_up(x, halo):
    n = x.shape[0]
    ext = jnp.concatenate([x, halo], axis=0)
    total = ext.shape[0]
    return pltpu.roll(ext, total - 1, 0)[:n], pltpu.roll(ext, total - 2, 0)[:n]


def _conv3(x, halo, w_ref):
    x1, x2 = _shifts_down(x, halo)
    return w_ref[0:1, :] * x2 + w_ref[1:2, :] * x1 + w_ref[2:3, :] * x, x1, x2


def _conv3_t(g, halo, w_ref):
    g1, g2 = _shifts_up(g, halo)
    return w_ref[2:3, :] * g + w_ref[1:2, :] * g1 + w_ref[0:1, :] * g2, g1, g2


def _silu_parts(x):
    s = _sigmoid(x)
    return x * s, s * (1.0 + x * (1.0 - s))


def _norm_bwd(dn, x, r, g):
    gd = g * dn
    return r * gd - x * (r * r * r) * jnp.mean(gd * x, axis=-1, keepdims=True)


def _head_norm_bwd(dn, y, rs, g, avg):
    gd = g * dn
    return rs * gd - y * (rs * rs * rs) * _dot_split(gd * y, avg, 2)


def _me():
    x, y, c = lax.axis_index('x'), lax.axis_index('y'), lax.axis_index('c')
    return x, y, c, 4 * x + 2 * y + c


def _peer(k):
    x, y, c, _ = _me()
    px = 1 - x if k & 4 else x
    py = 1 - y if k & 2 else y
    pc = 1 - c if k & 1 else c
    return (px, py, pc), 4 * px + 2 * py + pc


SIBLING = 1
OTHER_CHIPS = (2, 4, 6)


def _remote(src, dst, sems, a, k, dev):
    return pltpu.make_async_remote_copy(src_ref=src, dst_ref=dst, send_sem=sems[0].at[a, k - 1],
                                        recv_sem=sems[1].at[a, k - 1], device_id=dev,
                                        device_id_type=pl.DeviceIdType.MESH)


def _exchange_copies(ins, outs, sems, scatter):
    me = _me()[3]
    local, first, relay, arrivals = [], [], [], []
    for a in range(len(ins)):
        src = ins[a].at[me] if scatter else ins[a]
        local.append(pltpu.make_async_copy(src, outs[a].at[me], sems[2].at[a]))
        for k in range(1, N_DEV):
            dev, idx = _peer(k)
            landed = _remote(src, outs[a].at[idx], sems, a, k, dev)
            if scatter:
                first.append(_remote(ins[a].at[idx], outs[a].at[me], sems, a, k, dev))
                arrivals.append(landed)
            elif k == SIBLING:
                first.append(_remote(src, outs[a].at[me], sems, a, k, dev))
                arrivals.append(landed)
            elif k in OTHER_CHIPS:
                first.append(_remote(src, outs[a].at[me], sems, a, k, dev))
                sib, _ = _peer(SIBLING)
                relay.append((landed, _remote(outs[a].at[idx], outs[a].at[idx], sems, a, k | SIBLING, sib)))
            else:
                arrivals.append(landed)
    return local, first, relay, arrivals


def _exchange_start(ins, outs, sems, scatter):
    local, first, _, _ = _exchange_copies(ins, outs, sems, scatter)
    for cp in local + first:
        cp.start()


def _exchange_wait(ins, outs, sems, scatter):
    local, first, relay, arrivals = _exchange_copies(ins, outs, sems, scatter)
    for landed, forward in relay:
        landed.wait_recv()
        forward.start()
    for cp in arrivals:
        cp.wait_recv()
    for cp in first + [forward for _, forward in relay]:
        cp.wait_send()
    for cp in local:
        cp.wait()


def _exchange_shapes(arrs, scatter):
    return [_sds(a.shape if scatter else (N_DEV,) + a.shape, a.dtype) for a in arrs]


def _exchange_sems(n):
    return [pltpu.SemaphoreType.DMA((n, N_DEV - 1)), pltpu.SemaphoreType.DMA((n, N_DEV - 1)),
            pltpu.SemaphoreType.DMA((n,))]


def _exchange(arrs, *, name, scatter):
    n = len(arrs)

    def body(*refs):
        _exchange_start(refs[:n], refs[n:2 * n], refs[2 * n:], scatter)
        _exchange_wait(refs[:n], refs[n:2 * n], refs[2 * n:], scatter)

    any_spec = pl.BlockSpec(memory_space=pl.ANY)
    outs = pl.pallas_call(body, name=name, out_shape=_exchange_shapes(arrs, scatter), in_specs=[any_spec] * n,
                          out_specs=[any_spec] * n, scratch_shapes=_exchange_sems(n))(*arrs)
    return list(outs)


def _mod_cols(c_all, w_ada, b_cols):
    def body(c_ref, w_ref, b_ref, mod_ref, act_ref):
        c = c_ref[...]
        act = c * _sigmoid(c)
        act_ref[...] = act
        mod_ref[...] = _dot(act.astype(BF16), w_ref[...].astype(BF16)) + b_ref[...]

    return _call(body, name='mod_cols', grid=(1,),
                 in_specs=[_const(c_all.shape), _const(w_ada.shape), _const(b_cols.shape)],
                 out_specs=[_const((N_DEV, ADA_SHARD)), _const(c_all.shape)],
                 out_shape=[_sds((N_DEV, ADA_SHARD)), _sds(c_all.shape)], vmem=VMEM_BIG)(c_all, w_ada, b_cols)


def _grad_w_ada(act_t, dmod_cols):
    def body(a_ref, d_ref, o_ref):
        o_ref[...] = _dot(a_ref[...], d_ref[...])

    return _call(body, name='grad_w_ada', grid=(1,), in_specs=[_const(act_t.shape), _const(dmod_cols.shape)],
                 out_specs=_const((D_MODEL, ADA_SHARD)), out_shape=_sds((D_MODEL, ADA_SHARD)),
                 vmem=VMEM_BIG)(act_t, dmod_cols)


def _pre_mix(x, sc, sh, g, w_s, tm, ride):
    T = x.shape[0]

    def body(x_ref, sc_ref, sh_ref, g_ref, w_ref, proj_ref, h_ref):
        @pl.when(pl.program_id(1) == 0)
        def _():
            xv = x_ref[...]
            h_ref[...] = ((xv * _rsqrt_mean(xv) * g_ref[...]) * (1.0 + sc_ref[...]) + sh_ref[...]).astype(BF16)

        proj_ref[...] = _dot(h_ref[...], w_ref[...])

    row = pl.BlockSpec((tm, D_MODEL), lambda i, j: (i, 0))
    vec = _const((1, D_MODEL))
    return _call(body, name='pre_mix', grid=(T // tm, N_DEV),
                 in_specs=[row, vec, vec, vec, pl.BlockSpec((None, D_MODEL, IN_SHARD), lambda i, j: (j, 0, 0))],
                 out_specs=[pl.BlockSpec((tm, IN_SHARD), lambda i, j: (i, j)), row],
                 out_shape=[_sds((T, D_IN_PROJ)), _sds((T, D_MODEL), BF16)],
                 sem=('parallel', 'arbitrary'), ride=ride)(x, sc, sh, g, w_s)


def _halo_before(tm, rows=HALO):
    return lambda i: jnp.maximum(i * (tm // rows) - 1, 0)


def _halo_after(tm, T, rows=HALO):
    return lambda i: jnp.minimum((i + 1) * (tm // rows), T // rows - 1)


def _mix_fwd(yssm, proj, d, glu_w, glu_b, g_ssm, cw, g_conv, avg16, avg64, tm):
    T = yssm.shape[0]
    hb = _halo_before(tm)

    def body(y_ref, p_ref, ph_ref, d_ref, gw_ref, gb_ref, gs_ref, cw_ref, gc_ref, a16_ref, a64_ref, o_ref):
        i = pl.program_id(0)
        u = p_ref[:, 0:D_SSM]
        y = y_ref[...] + d_ref[...] * u
        z, _ = _gelu(y)
        gate = _sigmoid(_dot(z.astype(BF16), gw_ref[...]) + gb_ref[...])
        ya = z * gate
        rs = lax.rsqrt(_dot_split(ya * ya, a16_ref[...], 2) + EPS)
        o_ref[:, 0:D_SSM] = (ya * rs * gs_ref[...]).astype(BF16)
        bg = p_ref[:, D_SSM:D_SSM + D_CONV]
        cv = p_ref[:, D_SSM + D_CONV:D_SSM + 2 * D_CONV] * p_ref[:, D_SSM + 2 * D_CONV:D_IN_PROJ]
        hv = ph_ref[:, D_SSM + D_CONV:D_SSM + 2 * D_CONV] * ph_ref[:, D_SSM + 2 * D_CONV:D_IN_PROJ]
        hv = jnp.where(i > 0, hv, 0.0)
        conv, _, _ = _conv3(cv, hv, cw_ref)
        yb = bg * conv
        rsb = lax.rsqrt(_dot_split(yb * yb, a64_ref[...], 2) + EPS)
        o_ref[:, D_SSM:D_MODEL] = (yb * rsb * gc_ref[...]).astype(BF16)

    vec = _const((1, D_SSM))
    sq = _const((D_SSM, D_SSM))
    return _call(body, name='mix_fwd', grid=(T // tm,),
                 in_specs=[pl.BlockSpec((tm, D_SSM), lambda i: (i, 0)), pl.BlockSpec((tm, D_IN_PROJ), lambda i: (i, 0)),
                           pl.BlockSpec((HALO, D_IN_PROJ), lambda i: (hb(i), 0)), vec, sq, vec, vec,
                           _const((3, D_CONV)), vec, sq, sq],
                 out_specs=pl.BlockSpec((tm, D_MODEL), lambda i: (i, 0)), out_shape=_sds((T, D_MODEL), BF16),
                 sem=('parallel',), vmem=VMEM_BIG)(yssm, proj, proj, d, glu_w, glu_b, g_ssm, cw, g_conv, avg16, avg64)


def _out_proj(ycat, w_out, x, gt, g_post, g_pre, sc, sh, tm):
    T = x.shape[0]

    def body(y_ref, w_ref, x_ref, gt_ref, gp_ref, g2_ref, sc_ref, sh_ref, o_ref, x1_ref, h_ref):
        o = _dot(y_ref[...], w_ref[...])
        o_ref[...] = o
        x1 = x_ref[...] + gt_ref[...] * (o * _rsqrt_mean(o) * gp_ref[...])
        x1_ref[...] = x1
        h_ref[...] = ((x1 * _rsqrt_mean(x1) * g2_ref[...]) * (1.0 + sc_ref[...]) + sh_ref[...]).astype(BF16)

    row = pl.BlockSpec((tm, D_MODEL), lambda i: (i, 0))
    vec = _const((1, D_MODEL))
    return _call(body, name='out_proj', grid=(T // tm,),
                 in_specs=[row, _const((D_MODEL, D_MODEL)), row, vec, vec, vec, vec, vec],
                 out_specs=[row, row, row],
                 out_shape=[_sds((T, D_MODEL)), _sds((T, D_MODEL)), _sds((T, D_MODEL), BF16)],
                 sem=('parallel',), vmem=VMEM_BIG)(ycat, w_out, x, gt, g_post, g_pre, sc, sh)


def _ffn_up(h2, w_s, tm):
    T = h2.shape[0]

    def body(h_ref, w_ref, o_ref):
        o_ref[...] = _dot(h_ref[...], w_ref[...]).astype(BF16)

    return _call(body, name='ffn_up', grid=(T // tm, N_DEV),
                 in_specs=[pl.BlockSpec((tm, D_MODEL), lambda i, j: (i, 0)),
                           pl.BlockSpec((None, D_MODEL, FF_SHARD), lambda i, j: (j, 0, 0))],
                 out_specs=pl.BlockSpec((None, tm, FF_SHARD), lambda i, j: (j, i, 0)),
                 out_shape=_sds((N_DEV, T, FF_SHARD), BF16), sem=('parallel', 'parallel'))(h2, w_s)


def _ffn_hidden(up_ref, halo_ref, cw_ref, i):
    hid = []
    for part in range(2):
        halo = jnp.where(i > 0, halo_ref[part].astype(F32), 0.0)
        hid.append(_conv3(up_ref[part].astype(F32), halo, cw_ref.at[part])[0])
    return hid


def _ffn_act(up4, cw4, tm):
    T = up4.shape[2]
    hb = _halo_before(tm, HALO16)

    def body(up_ref, halo_ref, cw_ref, o_ref):
        hid_a, hid_v = _ffn_hidden(up_ref, halo_ref, cw_ref, pl.program_id(0))
        o_ref[...] = (_silu_parts(hid_a)[0] * hid_v).astype(BF16)

    return _call(body, name='ffn_act', grid=(T // tm, 4),
                 in_specs=[pl.BlockSpec((2, None, tm, FF_SHARD), lambda i, j: (0, j, i, 0)),
                           pl.BlockSpec((2, None, HALO16, FF_SHARD), lambda i, j: (0, j, hb(i), 0)),
                           pl.BlockSpec((2, None, 3, FF_SHARD), lambda i, j: (0, j, 0, 0))],
                 out_specs=pl.BlockSpec((None, tm, FF_SHARD), lambda i, j: (j, i, 0)),
                 out_shape=_sds((4, T, FF_SHARD), BF16), sem=('parallel', 'parallel'))(up4, up4, cw4)


def _ffn_down(act, wd4, x1, tgt, gt, g_post, tm):
    T = x1.shape[0]
    nb = T // tm

    def body(a_ref, w_ref, x1_ref, t_ref, gt_ref, g_ref, ddn_ref, dx_ref, loss_ref, dgt_ref, dg_ref, dn_ref):
        i, j = pl.program_id(0), pl.program_id(1)
        part = _dot(a_ref[...], w_ref[...])

        @pl.when(jnp.logical_and(i == 0, j == 0))
        def _():
            dgt_ref[...] = jnp.zeros_like(dgt_ref)
            dg_ref[...] = jnp.zeros_like(dg_ref)

        @pl.when(j == 0)
        def _():
            dn_ref[...] = part

        @pl.when(j > 0)
        def _():
            dn_ref[...] += part

        @pl.when(j == 3)
        def _():
            dn, gv, gate = dn_ref[...], g_ref[...], gt_ref[...]
            r = _rsqrt_mean(dn)
            normed = dn * r * gv
            err = x1_ref[...] + gate * normed - t_ref[...]
            dx = err * (1.0 / D_MODEL)
            dx_ref[...] = dx
            tot = jnp.sum(jnp.sum(err * err, axis=1, keepdims=True), axis=0, keepdims=True) * (0.5 / D_MODEL)
            loss_ref[...] = jnp.broadcast_to(tot, (8, 128))
            dgt_ref[...] += _colsum(dx * normed)
            dnn = dx * gate
            dg_ref[...] += _colsum(dnn * dn * r)
            ddn_ref[...] = _norm_bwd(dnn, dn, r, gv).astype(BF16)

    row = pl.BlockSpec((tm, D_MODEL), lambda i, j: (i, 0))
    vec = _const((1, D_MODEL))
    return _call(body, name='ffn_down', grid=(nb, 4),
                 in_specs=[pl.BlockSpec((None, tm, FF_SHARD), lambda i, j: (j, i, 0)),
                           pl.BlockSpec((None, FF_SHARD, D_MODEL), lambda i, j: (j, 0, 0)), row, row, vec, vec],
                 out_specs=[row, row, pl.BlockSpec((None, 8, 128), lambda i, j: (i, 0, 0)), vec, vec],
                 out_shape=[_sds((T, D_MODEL), BF16), _sds((T, D_MODEL)), _sds((nb, 8, 128)), _sds((1, D_MODEL)),
                            _sds((1, D_MODEL))],
                 scratch=[pltpu.VMEM((tm, D_MODEL), F32)], sem=('arbitrary', 'arbitrary'),
                 vmem=VMEM_BIG)(act, wd4, x1, tgt, gt, g_post)


def _ssm_prep(lre, lim, lst, b_re, b_im):
    def body(lre_ref, lim_ref, lst_ref, br_ref, bi_ref, ar_ref, ai_ref, bbr_ref, bbi_ref):
        ar, ai, qr, qi = _zoh(lre_ref[...], lim_ref[...], lst_ref[...])[:4]
        ar_ref[...] = ar
        ai_ref[...] = ai
        bbr_ref[...] = qr * br_ref[...] - qi * bi_ref[...]
        bbi_ref[...] = qr * bi_ref[...] + qi * br_ref[...]

    shp = lre.shape
    return _call(body, name='ssm_prep', grid=(1,), in_specs=[_const(shp)] * 5, out_specs=[_const(shp)] * 4,
                 out_shape=[_sds(shp)] * 4)(lre, lim, lst, b_re, b_im)


def _zoh(lre, lim, lst):
    lr = jnp.minimum(lre, LAMBDA_RE_MAX)
    st = jnp.exp(lst)
    mag = jnp.exp(lr * st)
    ar = mag * jnp.cos(lim * st)
    ai = mag * jnp.sin(lim * st)
    den = lr * lr + lim * lim
    qr = ((ar - 1.0) * lr + ai * lim) / den
    qi = (ai * lr - (ar - 1.0) * lim) / den
    return ar, ai, qr, qi, lr, st, den


def _ssm_prep_bwd(lre, lim, lst, b_re, b_im, dbbr, dbbi, dar, dai, seg):
    def body(lre_ref, lim_ref, lst_ref, br_ref, bi_ref, dbbr_ref, dbbi_ref, dar_ref, dai_ref, seg_ref,
             dbr_ref, dbi_ref, dlre_ref, dlim_ref, dlst_ref):
        lre_v = lre_ref[...]
        li = lim_ref[...]
        ar, ai, qr, qi, lr, st, den = _zoh(lre_v, li, lst_ref[...])
        br, bi, gbr, gbi = br_ref[...], bi_ref[...], dbbr_ref[...], dbbi_ref[...]
        dbr_ref[...] = qr * gbr + qi * gbi
        dbi_ref[...] = qr * gbi - qi * gbr
        gqr = _dot_split(br * gbr + bi * gbi, seg_ref[...], 3)
        gqi = _dot_split(br * gbi - bi * gbr, seg_ref[...], 3)
        ir, ii = lr / den, -li / den
        gar = dar_ref[...] + ir * gqr + ii * gqi
        gai = dai_ref[...] + ir * gqi - ii * gqr
        tr, ti = qr * ir - qi * ii, qr * ii + qi * ir
        glr = -(tr * gqr + ti * gqi)
        gli = -(tr * gqi - ti * gqr)
        gzr = ar * gar + ai * gai
        gzi = ar * gai - ai * gar
        glr = glr + st * gzr
        gli = gli + st * gzi
        gst = (lr * gzr + li * gzi) * st
        dlre_ref[...] = jnp.where(lre_v < LAMBDA_RE_MAX, glr, 0.0)
        dlim_ref[...] = gli
        dlst_ref[...] = jnp.sum(gst, axis=1, keepdims=True) * (1.0 / SSM_GROUP)

    shp = lre.shape
    return _call(body, name='ssm_prep_bwd', grid=(1,), in_specs=[_const(shp)] * 9 + [_const(seg.shape)],
                 out_specs=[_const(shp)] * 4 + [_const((N_GROUPS, 1))],
                 out_shape=[_sds(shp)] * 4 + [_sds((N_GROUPS, 1))], vmem=VMEM_BIG)(
                     lre, lim, lst, b_re, b_im, dbbr, dbbi, dar, dai, seg)


def _scan_specs(T):
    half = lambda cb: cb // 2
    return dict(
        chan=pl.BlockSpec((T, CHAN_BLOCK), lambda cb: (0, half(cb))),
        state=pl.BlockSpec((T, STATE_BLOCK), lambda cb: (0, cb)),
        b=pl.BlockSpec((CHAN_BLOCK, STATE_BLOCK), lambda cb: (half(cb), cb)),
        c=pl.BlockSpec((STATE_BLOCK, CHAN_BLOCK), lambda cb: (cb, half(cb))),
        lam=pl.BlockSpec((1, STATE_BLOCK), lambda cb: (0, cb)),
    )


def _complex_power(re, im, n):
    out = None
    while True:
        if n & 1:
            out = (re, im) if out is None else (out[0] * re - out[1] * im, out[0] * im + out[1] * re)
        n >>= 1
        if n == 0:
            return out
        re, im = re * re - im * im, 2.0 * re * im


def _rows8(i):
    if isinstance(i, int):
        return pl.ds(i * SUBLANES, SUBLANES)
    return pl.ds(pl.multiple_of(i * SUBLANES, SUBLANES), SUBLANES)


def _scan_loop(n_steps, body, init):
    trips = n_steps // SCAN_UNROLL

    def trip(t, carry):
        for u in range(SCAN_UNROLL):
            carry = body(t * SCAN_UNROLL + u, carry)
        return carry

    carry = lax.fori_loop(0, trips, trip, init)
    for step in range(trips * SCAN_UNROLL, n_steps):
        carry = body(step, carry)
    return carry


def _ssm_fwd(u_perm, b_re, b_im, c_re, c_im, lam_r, lam_i, ride):
    T = u_perm.shape[0]
    ls = T // SUBLANES
    rc = min(512, T)
    sp = _scan_specs(T)

    def body(u_ref, bre_ref, bim_ref, cre_ref, cim_ref, lr_ref, li_ref, sre_ref, sim_ref, y_ref):
        cb = pl.program_id(0)
        for c in range(T // rc):
            rows = pl.ds(c * rc, rc)
            sre_ref[rows, :] = _dot(u_ref[rows, :], bre_ref[...])
            sim_ref[rows, :] = _dot(u_ref[rows, :], bim_ref[...])
        shp = (SUBLANES, STATE_BLOCK)
        lr = jnp.broadcast_to(lr_ref[...], shp)
        li = jnp.broadcast_to(li_ref[...], shp)
        zero = jnp.zeros(shp, F32)

        def step(i, carry):
            sr, si = carry
            rows = _rows8(i)
            nr = lr * sr - li * si + sre_ref[rows, :]
            ni = lr * si + li * sr + sim_ref[rows, :]
            sre_ref[rows, :] = nr
            sim_ref[rows, :] = ni
            return nr, ni

        fr, fi = _scan_loop(ls, step, (zero, zero))
        pr, pi_ = _complex_power(lr, li, ls)
        row = lax.broadcasted_iota(jnp.int32, shp, 0)
        ir, ii = zero, zero
        for _ in range(SUBLANES - 1):
            er = fr + pr * ir - pi_ * ii
            ei = fi + pr * ii + pi_ * ir
            ir = jnp.where(row == 0, 0.0, pltpu.roll(er, 1, 0))
            ii = jnp.where(row == 0, 0.0, pltpu.roll(ei, 1, 0))

        def fix(i, carry):
            cr, ci = carry
            rows = _rows8(i)
            nr = lr * cr - li * ci
            ni = lr * ci + li * cr
            sre_ref[rows, :] += nr
            sim_ref[rows, :] += ni
            return nr, ni

        _scan_loop(ls, fix, (ir, ii))
        for c in range(T // rc):
            rows = pl.ds(c * rc, rc)
            yc = _dot(sre_ref[rows, :].astype(BF16), cre_ref[...]) - _dot(sim_ref[rows, :].astype(BF16), cim_ref[...])

            @pl.when(cb % 2 == 0)
            def _():
                y_ref[rows, :] = yc

            @pl.when(cb % 2 == 1)
            def _():
                y_ref[rows, :] += yc

    return _call(body, name='ssm_fwd', grid=(N_STATE // STATE_BLOCK,),
                 in_specs=[sp['chan'], sp['b'], sp['b'], sp['c'], sp['c'], sp['lam'], sp['lam']],
                 out_specs=[sp['state'], sp['state'], sp['chan']],
                 out_shape=[_sds((T, N_STATE)), _sds((T, N_STATE)), _sds((T, D_SSM))],
                 sem=('arbitrary',), vmem=VMEM_BIG, ride=ride)(u_perm, b_re, b_im, c_re, c_im, lam_r, lam_i)


def _ssm_bwd(dy_perm, u_perm, s_re, s_im, b_re, b_im, c_re, c_im, lam_r, lam_i, ride):
    T = u_perm.shape[0]
    ls = T // SUBLANES
    rc = min(512, T)
    sp = _scan_specs(T)
    ncb = N_STATE // STATE_BLOCK

    def body(dy_ref, u_ref, sre_ref, sim_ref, bre_ref, bim_ref, cre_ref, cim_ref, lr_ref, li_ref,
             du_ref, dbr_ref, dbi_ref, dcr_ref, dci_ref, dar_ref, dai_ref, gre_ref, gim_ref):
        cb = pl.program_id(0)
        for c in range(T // rc):
            rows = pl.ds(c * rc, rc)
            gre_ref[rows, :] = _dot_nt(dy_ref[rows, :], cre_ref[...])
            gim_ref[rows, :] = -_dot_nt(dy_ref[rows, :], cim_ref[...])
        shp = (SUBLANES, STATE_BLOCK)
        lr = jnp.broadcast_to(lr_ref[...], shp)
        li = jnp.broadcast_to(li_ref[...], shp)
        zero = jnp.zeros(shp, F32)

        def step(k, carry):
            gr, gi = carry
            rows = _rows8(ls - 1 - k)
            nr = lr * gr + li * gi + gre_ref[rows, :]
            ni = lr * gi - li * gr + gim_ref[rows, :]
            gre_ref[rows, :] = nr
            gim_ref[rows, :] = ni
            return nr, ni

        fr, fi = _scan_loop(ls, step, (zero, zero))
        pr, pi_ = _complex_power(lr, -li, ls)
        row = lax.broadcasted_iota(jnp.int32, shp, 0)
        cr, ci = zero, zero
        for _ in range(SUBLANES - 1):
            er = fr + pr * cr - pi_ * ci
            ei = fi + pr * ci + pi_ * cr
            cr = jnp.where(row == SUBLANES - 1, 0.0, pltpu.roll(er, SUBLANES - 1, 0))
            ci = jnp.where(row == SUBLANES - 1, 0.0, pltpu.roll(ei, SUBLANES - 1, 0))

        def fix(k, carry):
            dr, di, ar, ai = carry
            rows = _rows8(ls - 1 - k)
            dr, di = lr * dr + li * di, lr * di - li * dr
            gr = gre_ref[rows, :] + dr
            gi = gim_ref[rows, :] + di
            gre_ref[rows, :] = gr
            gim_ref[rows, :] = gi
            prev = _rows8(ls - 2 - k)
            spr, spi = sre_ref[prev, :], sim_ref[prev, :]
            return dr, di, ar + gr * spr + gi * spi, ai + gi * spr - gr * spi

        dr, di, ar, ai = _scan_loop(ls - 1, fix, (cr, ci, zero, zero))
        first = pl.ds(0, SUBLANES)
        last = pl.ds((ls - 1) * SUBLANES, SUBLANES)
        gr = gre_ref[first, :] + (lr * dr + li * di)
        gi = gim_ref[first, :] + (lr * di - li * dr)
        gre_ref[first, :] = gr
        gim_ref[first, :] = gi
        spr = jnp.where(row == 0, 0.0, pltpu.roll(sre_ref[last, :], 1, 0))
        spi = jnp.where(row == 0, 0.0, pltpu.roll(sim_ref[last, :], 1, 0))
        dar_ref[...] = _colsum(ar + gr * spr + gi * spi)
        dai_ref[...] = _colsum(ai + gi * spr - gr * spi)

        for c in range(T // rc):
            rows = pl.ds(c * rc, rc)
            g_r, g_i = gre_ref[rows, :].astype(BF16), gim_ref[rows, :].astype(BF16)
            s_r, s_i = sre_ref[rows, :].astype(BF16), sim_ref[rows, :].astype(BF16)
            ub, dyb = u_ref[rows, :], dy_ref[rows, :]
            duc = _dot_nt(g_r, bre_ref[...]) + _dot_nt(g_i, bim_ref[...])
            parts = (_dot_tn(ub, g_r), _dot_tn(ub, g_i), _dot_tn(s_r, dyb), -_dot_tn(s_i, dyb))
            outs = (dbr_ref, dbi_ref, dcr_ref, dci_ref)
            for o_ref, part in zip(outs, parts):
                if c == 0:
                    o_ref[...] = part
                else:
                    o_ref[...] += part

            @pl.when(cb % 2 == 0)
            def _():
                du_ref[rows, :] = duc

            @pl.when(cb % 2 == 1)
            def _():
                du_ref[rows, :] += duc

    blk = lambda r, c: pl.BlockSpec((None, r, c), lambda cb: (cb, 0, 0))
    return _call(body, name='ssm_bwd', grid=(ncb,),
                 in_specs=[sp['chan'], sp['chan'], sp['state'], sp['state'], sp['b'], sp['b'], sp['c'], sp['c'],
                           sp['lam'], sp['lam']],
                 out_specs=[sp['chan'], blk(CHAN_BLOCK, STATE_BLOCK), blk(CHAN_BLOCK, STATE_BLOCK),
                            blk(STATE_BLOCK, CHAN_BLOCK), blk(STATE_BLOCK, CHAN_BLOCK), blk(1, STATE_BLOCK),
                            blk(1, STATE_BLOCK)],
                 out_shape=[_sds((T, D_SSM)), _sds((ncb, CHAN_BLOCK, STATE_BLOCK)), _sds((ncb, CHAN_BLOCK, STATE_BLOCK)),
                            _sds((ncb, STATE_BLOCK, CHAN_BLOCK)), _sds((ncb, STATE_BLOCK, CHAN_BLOCK)),
                            _sds((ncb, 1, STATE_BLOCK)), _sds((ncb, 1, STATE_BLOCK))],
                 scratch=[pltpu.VMEM((T, STATE_BLOCK), F32), pltpu.VMEM((T, STATE_BLOCK), F32)],
                 sem=('arbitrary',), vmem=VMEM_BIG, ride=ride)(dy_perm, u_perm, s_re, s_im, b_re, b_im, c_re, c_im,
                                                               lam_r, lam_i)


def _ffn_dact(ddn, wd4, up4, cw4, tm):
    T = ddn.shape[0]
    hb = _halo_before(tm, HALO16)

    def body(d_ref, w_ref, up_ref, halo_ref, cw_ref, o_ref):
        dact = _dot_nt(d_ref[...], w_ref[...])
        hid_a, hid_v = _ffn_hidden(up_ref, halo_ref, cw_ref, pl.program_id(0))
        silu, dsilu = _silu_parts(hid_a)
        o_ref[0] = (dact * hid_v * dsilu).astype(BF16)
        o_ref[1] = (dact * silu).astype(BF16)

    return _call(body, name='ffn_dact', grid=(T // tm, 4),
                 in_specs=[pl.BlockSpec((tm, D_MODEL), lambda i, j: (i, 0)),
                           pl.BlockSpec((None, FF_SHARD, D_MODEL), lambda i, j: (j, 0, 0)),
                           pl.BlockSpec((2, None, tm, FF_SHARD), lambda i, j: (0, j, i, 0)),
                           pl.BlockSpec((2, None, HALO16, FF_SHARD), lambda i, j: (0, j, hb(i), 0)),
                           pl.BlockSpec((2, None, 3, FF_SHARD), lambda i, j: (0, j, 0, 0))],
                 out_specs=pl.BlockSpec((2, None, tm, FF_SHARD), lambda i, j: (0, j, i, 0)),
                 out_shape=_sds((2, 4, T, FF_SHARD), BF16), sem=('parallel', 'parallel'))(ddn, wd4, up4, up4, cw4)


def _ffn_dup(dhid8, up8, cw8, tm, ride):
    T = up8.shape[1]
    nb = T // tm
    ha = _halo_after(tm, T, HALO16)

    def body(dh_ref, dha_ref, up_ref, cw_ref, dup_ref, dcw_ref):
        i = pl.program_id(1)

        @pl.when(i == 0)
        def _():
            dcw_ref[...] = jnp.zeros_like(dcw_ref)

        dh = dh_ref[...].astype(F32)
        dup, dh1, dh2 = _conv3_t(dh, jnp.where(i < nb - 1, dha_ref[...].astype(F32), 0.0), cw_ref)
        dup_ref[...] = dup.astype(BF16)
        up = up_ref[...].astype(F32)
        dcw_ref[0:1, :] += _colsum(dh2 * up)
        dcw_ref[1:2, :] += _colsum(dh1 * up)
        dcw_ref[2:3, :] += _colsum(dh * up)

    main = pl.BlockSpec((None, tm, FF_SHARD), lambda j, i: (j, i, 0))
    return _call(body, name='ffn_dup', grid=(N_DEV, nb),
                 in_specs=[main, pl.BlockSpec((None, HALO16, FF_SHARD), lambda j, i: (j, ha(i), 0)), main,
                           pl.BlockSpec((None, 3, FF_SHARD), lambda j, i: (j, 0, 0))],
                 out_specs=[main, pl.BlockSpec((None, 8, FF_SHARD), lambda j, i: (j, 0, 0))],
                 out_shape=[_sds((N_DEV, T, FF_SHARD), BF16), _sds((N_DEV, 8, FF_SHARD))],
                 sem=('parallel', 'arbitrary'), ride=ride)(dhid8, dhid8, up8, cw8)


def _grad_tn(a, b, a_spec, b_spec, groups, m, n, tk, name, ride=None, parts=1):
    T = a.shape[-2]
    nk = T // tk
    mp = m // parts

    def body(a_ref, b_ref, *refs):
        o_refs, acc_ref = refs[:parts], refs[parts]
        k = pl.program_id(1)
        part = _dot_tn(a_ref[...], b_ref[...])

        @pl.when(k == 0)
        def _():
            acc_ref[...] = part

        @pl.when(k > 0)
        def _():
            acc_ref[...] += part

        @pl.when(k == nk - 1)
        def _():
            for p, o_ref in enumerate(o_refs):
                o_ref[...] = acc_ref[p * mp:(p + 1) * mp, :].astype(BF16)

    out_spec = pl.BlockSpec((None, mp, n), lambda g, k: (g, 0, 0))
    res = _call(body, name=name, grid=(groups, nk), in_specs=[a_spec, b_spec], out_specs=[out_spec] * parts,
                out_shape=[_sds((groups, mp, n), BF16)] * parts, scratch=[pltpu.VMEM((m, n), F32)],
                sem=('parallel', 'arbitrary'), vmem=VMEM_BIG, ride=ride)(a, b)
    if parts > 1:
        return res
    return res[0] if ride is None else (res[0][0], res[1])


def _pre_norm_bwd(dz, dz_spec, w_s, xin, dres, sc, g, tm, name, ride, below=None, group=1):
    T = xin.shape[0]
    n = w_s.shape[2]
    steps = N_DEV // group

    def body(dz_ref, w_ref, x_ref, dr_ref, sc_ref, g_ref, *refs):
        if below is None:
            dx_ref, dsh_ref, dsc_ref, dg_ref = refs
            sums = (dsh_ref, dsc_ref, dg_ref)
        else:
            v_ref, gate_ref, g2_ref, dx_ref, dsh_ref, dsc_ref, dg_ref, dv_ref, dgate_ref, dg2_ref = refs
            sums = (dsh_ref, dsc_ref, dg_ref, dgate_ref, dg2_ref)
        i, j = pl.program_id(0), pl.program_id(1)
        part = _dot_nt(dz_ref[:, 0:n], w_ref[0])
        for s in range(1, group):
            part = part + _dot_nt(dz_ref[:, s * n:(s + 1) * n], w_ref[s])

        @pl.when(jnp.logical_and(i == 0, j == 0))
        def _():
            for s_ref in sums:
                s_ref[...] = jnp.zeros_like(s_ref)

        @pl.when(j == 0)
        def _():
            dx_ref[...] = part

        @pl.when(j > 0)
        def _():
            dx_ref[...] += part

        @pl.when(j == steps - 1)
        def _():
            dh, xv, gv = dx_ref[...], x_ref[...], g_ref[...]
            r = _rsqrt_mean(xv)
            dsh_ref[...] += _colsum(dh)
            dsc_ref[...] += _colsum(dh * (xv * r * gv))
            dxn = dh * (1.0 + sc_ref[...])
            dg_ref[...] += _colsum(dxn * xv * r)
            dx = dr_ref[...] + _norm_bwd(dxn, xv, r, gv)
            dx_ref[...] = dx
            if below is not None:
                v, g2 = v_ref[...], g2_ref[...]
                rv = _rsqrt_mean(v)
                dgate_ref[...] += _colsum(dx * (v * rv * g2))
                dn = dx * gate_ref[...]
                dg2_ref[...] += _colsum(dn * v * rv)
                dv_ref[...] = _norm_bwd(dn, v, rv, g2).astype(BF16)

    row = pl.BlockSpec((tm, D_MODEL), lambda i, j: (i, 0))
    vec = _const((1, D_MODEL))
    in_specs = [dz_spec, pl.BlockSpec((group, D_MODEL, n), lambda i, j: (j, 0, 0)), row, row, vec, vec]
    out_specs = [row, vec, vec, vec]
    out_shape = [_sds((T, D_MODEL)), _sds((1, D_MODEL)), _sds((1, D_MODEL)), _sds((1, D_MODEL))]
    args = [dz, w_s, xin, dres, sc, g]
    if below is not None:
        in_specs += [row, vec, vec]
        out_specs += [row, vec, vec]
        out_shape += [_sds((T, D_MODEL), BF16), _sds((1, D_MODEL)), _sds((1, D_MODEL))]
        args += list(below)
    return _call(body, name=name, grid=(T // tm, steps), in_specs=in_specs, out_specs=out_specs,
                 out_shape=out_shape, sem=('arbitrary', 'arbitrary'), vmem=VMEM_MOST, ride=ride)(*args)


def _mix_bwd(d_o, w_out, yssm, proj, d, glu_w, glu_b, g_ssm, cw, g_conv, avg16, avg64, tm):
    T = yssm.shape[0]
    hb = _halo_before(tm)

    def body(do_ref, wo_ref, y_ref, p_ref, ph_ref, d_ref, gw_ref, gb_ref, gs_ref, cw_ref, gc_ref, a16_ref, a64_ref,
             dy_ref, dconv_ref, dbg_ref, z_ref, dlin_ref, acc_ref):
        i = pl.program_id(0)
        dyc = _dot_nt(do_ref[...], wo_ref[...])

        @pl.when(i == 0)
        def _():
            acc_ref[...] = jnp.zeros_like(acc_ref)

        u = p_ref[:, 0:D_SSM]
        y = y_ref[...] + d_ref[...] * u
        z, t = _gelu(y)
        gate = _sigmoid(_dot(z.astype(BF16), gw_ref[...]) + gb_ref[...])
        ya = z * gate
        rs = lax.rsqrt(_dot_split(ya * ya, a16_ref[...], 2) + EPS)
        dna = dyc[:, 0:D_SSM]
        acc_ref[1:2, :] += _colsum(dna * ya * rs)
        dya = _head_norm_bwd(dna, ya, rs, gs_ref[...], a16_ref[...])
        dlin = dya * z * gate * (1.0 - gate)
        acc_ref[0:1, :] += _colsum(dlin)
        dlin_b = dlin.astype(BF16)
        dz = dya * gate + _dot_nt(dlin_b, gw_ref[...])
        dy = dz * _gelu_grad(y, t)
        acc_ref[3:4, :] += _colsum(dy * u)
        dy_ref[...] = dy
        z_ref[...] = z.astype(BF16)
        dlin_ref[...] = dlin_b

        bg = p_ref[:, D_SSM:D_SSM + D_CONV]
        cv = p_ref[:, D_SSM + D_CONV:D_SSM + 2 * D_CONV] * p_ref[:, D_SSM + 2 * D_CONV:D_IN_PROJ]
        hv = ph_ref[:, D_SSM + D_CONV:D_SSM + 2 * D_CONV] * ph_ref[:, D_SSM + 2 * D_CONV:D_IN_PROJ]
        hv = jnp.where(i > 0, hv, 0.0)
        conv, cv1, cv2 = _conv3(cv, hv, cw_ref)
        yb = bg * conv
        rsb = lax.rsqrt(_dot_split(yb * yb, a64_ref[...], 2) + EPS)
        dnb = dyc[:, D_SSM:D_MODEL]
        acc_ref[2:3, :] += _colsum(dnb * yb * rsb)
        dyb = _head_norm_bwd(dnb, yb, rsb, gc_ref[...], a64_ref[...])
        dbg_ref[...] = dyb * conv
        dconv = dyb * bg
        dconv_ref[...] = dconv
        acc_ref[4:5, :] += _colsum(dconv * cv2)
        acc_ref[5:6, :] += _colsum(dconv * cv1)
        acc_ref[6:7, :] += _colsum(dconv * cv)

    vec = _const((1, D_SSM))
    sq = _const((D_SSM, D_SSM))
    half = pl.BlockSpec((tm, D_SSM), lambda i: (i, 0))
    return _call(body, name='mix_bwd', grid=(T // tm,),
                 in_specs=[pl.BlockSpec((tm, D_MODEL), lambda i: (i, 0)), _const((D_MODEL, D_MODEL)), half,
                           pl.BlockSpec((tm, D_IN_PROJ), lambda i: (i, 0)),
                           pl.BlockSpec((HALO, D_IN_PROJ), lambda i: (hb(i), 0)), vec, sq, vec, vec,
                           _const((3, D_CONV)), vec, sq, sq],
                 out_specs=[half, half, half, half, half, _const((8, D_SSM))],
                 out_shape=[_sds((T, D_SSM)), _sds((T, D_SSM)), _sds((T, D_SSM)), _sds((T, D_SSM), BF16),
                            _sds((T, D_SSM), BF16), _sds((8, D_SSM))],
                 sem=('arbitrary',), vmem=VMEM_BIG)(d_o, w_out, yssm, proj, proj, d, glu_w, glu_b, g_ssm, cw, g_conv,
                                                   avg16, avg64)


def _mix_bwd_proj(dconv, proj, du_ssm, dy, d, dbg, cw, tm):
    T = dy.shape[0]
    nb = T // tm
    ha = _halo_after(tm, T)

    def body(dc_ref, dch_ref, cg_ref, v_ref, du_ref, dy_ref, d_ref, dbg_ref, cw_ref, o_ref):
        i = pl.program_id(0)
        dcv = _conv3_t(dc_ref[...], jnp.where(i < nb - 1, dch_ref[...], 0.0), cw_ref)[0]
        o_ref[:, 0:D_SSM] = (du_ref[...] + dy_ref[...] * d_ref[...]).astype(BF16)
        o_ref[:, D_SSM:D_SSM + D_CONV] = dbg_ref[...].astype(BF16)
        o_ref[:, D_SSM + D_CONV:D_SSM + 2 * D_CONV] = (dcv * v_ref[...]).astype(BF16)
        o_ref[:, D_SSM + 2 * D_CONV:D_IN_PROJ] = (dcv * cg_ref[...]).astype(BF16)

    half = pl.BlockSpec((tm, D_SSM), lambda i: (i, 0))
    return _call(body, name='mix_bwd_proj', grid=(nb,),
                 in_specs=[half, pl.BlockSpec((HALO, D_CONV), lambda i: (ha(i), 0)),
                           pl.BlockSpec((tm, D_CONV), lambda i: (i, 2)), pl.BlockSpec((tm, D_CONV), lambda i: (i, 3)),
                           half, half, _const((1, D_SSM)), half, _const((3, D_CONV))],
                 out_specs=pl.BlockSpec((tm, D_IN_PROJ), lambda i: (i, 0)), out_shape=_sds((T, D_IN_PROJ), BF16),
                 sem=('parallel',))(dconv, dconv, proj, proj, du_ssm, dy, d, dbg, cw)


def _row_tile(rows, cols, slots):
    for cand in (512, 256, 128, 64, 32, 16, 8):
        if rows % cand == 0 and slots * cand * cols * 4 <= (2 << 20):
            return cand
    return rows


def _adamw_math(g, w, m, v):
    m2 = ADAM_B1 * m + (1.0 - ADAM_B1) * g
    v2 = ADAM_B2 * v + (1.0 - ADAM_B2) * (g * g)
    m_hat = m2 / (1.0 - ADAM_B1 ** ADAM_STEP)
    v_hat = v2 / (1.0 - ADAM_B2 ** ADAM_STEP)
    return -ADAM_LR * (m_hat / (jnp.sqrt(v_hat) + ADAM_EPS) + ADAM_WD * w), m2, v2


def _adamw(pieces, w, m, v, name):
    slots, _, cols = pieces[0].shape
    rows = sum(p.shape[1] for p in pieces)
    tr = _row_tile(pieces[0].shape[1], cols, slots)
    starts, pos = [], 0
    for p in pieces:
        assert p.shape[1] % tr == 0
        starts.append(pos)
        pos += p.shape[1] // tr

    def body(*refs):
        g_refs = refs[:len(pieces)]
        w_ref, m_ref, v_ref, go_ref, d_ref, mo_ref, vo_ref = refs[len(pieces):]
        i = pl.program_id(0)
        g = None
        for g_ref, start in zip(g_refs, starts):
            part = g_ref[0].astype(F32)
            for s in range(1, slots):
                part = part + g_ref[s].astype(F32)
            g = part if g is None else jnp.where(i >= start, part, g)
        go_ref[...] = g
        d_ref[...], mo_ref[...], vo_ref[...] = _adamw_math(g, w_ref[...], m_ref[...], v_ref[...])

    def piece_spec(start, count):
        return pl.BlockSpec((slots, tr, cols), lambda i: (0, jnp.clip(i - start, 0, count - 1), 0))

    blk = pl.BlockSpec((tr, cols), lambda i: (i, 0))
    return _call(body, name=name, grid=(rows // tr,),
                 in_specs=[piece_spec(s, p.shape[1] // tr) for s, p in zip(starts, pieces)] + [blk, blk, blk],
                 out_specs=[blk] * 4, out_shape=[_sds((rows, cols))] * 4, sem=('parallel',))(*pieces, w, m, v)


def _to_scan_rows(a):
    T, n = a.shape
    return a.reshape(SUBLANES, T // SUBLANES, n).transpose(1, 0, 2).reshape(T, n)


def _from_scan_rows(a):
    T, n = a.shape
    return a.reshape(T // SUBLANES, SUBLANES, n).transpose(1, 0, 2).reshape(T, n)


def _expand(a):
    return jnp.repeat(a, SSM_GROUP, axis=1)


def _block_diag_b(bb):
    eye = jnp.eye(N_GROUPS, dtype=bb.dtype)
    return (bb.transpose(0, 2, 1)[:, :, None, :] * eye[:, None, :, None]).reshape(D_SSM, N_STATE)


def _block_diag_c(cc):
    eye = jnp.eye(N_GROUPS, dtype=cc.dtype)
    return (cc.transpose(0, 2, 1)[:, :, None, :] * eye[:, None, :, None]).reshape(N_STATE, D_SSM)


def _diag_blocks(x, chan_major):
    e2 = jnp.eye(2, dtype=x.dtype)
    e4 = jnp.eye(4, dtype=x.dtype)
    if chan_major:
        x = x.reshape(4, 2, 2, 4, SSM_GROUP, 4, SSM_STATE)
        x = x * e2[None, :, :, None, None, None, None] * e4[None, None, None, :, None, :, None]
        return x.sum(axis=(2, 3)).transpose(0, 1, 3, 4, 2).reshape(N_GROUPS, SSM_STATE, SSM_GROUP)
    x = x.reshape(4, 2, 4, SSM_STATE, 2, 4, SSM_GROUP)
    x = x * e2[None, :, None, None, :, None, None] * e4[None, None, :, None, None, :, None]
    return x.sum(axis=(4, 5)).reshape(N_GROUPS, SSM_STATE, SSM_GROUP)


SMALL_LAYOUT = {
    'ssm_b_re': (0, 0, 32, 1024), 'ssm_b_im': (32, 0, 32, 1024), 'ssm_c_re': (64, 0, 32, 1024),
    'ssm_c_im': (96, 0, 32, 1024), 'b_ada': (128, 0, 6, 1024), 'g_pre_mix': (134, 0, 1, 1024),
    'g_post_mix': (135, 0, 1, 1024), 'ssm_lam_re': (136, 0, 2, 1024), 'ssm_lam_im': (138, 0, 2, 1024),
    'ssm_log_step': (140, 0, 1, 32), 'glu_b': (141, 0, 1, 512), 'g_out_ssm': (141, 512, 1, 512),
    'g_out_conv': (142, 0, 1, 512), 'ssm_d': (142, 512, 1, 512), 'g_pre_ffn': (143, 0, 1, 1024),
    'g_post_ffn': (144, 0, 1, 1024)}
SMALL_ROWS = 152
B_ADA_ROW = SMALL_LAYOUT['b_ada'][0]
LATE_ROWS = {('b_ada', 0): 0, ('b_ada', 1): 1, ('g_pre_mix', 0): 2}


def _adamw_small(gathered, late, wts, mom_m, mom_v):
    names = list(SMALL_LAYOUT)
    n = len(names)

    def body(*refs):
        g_ref, late_ref, ins, outs = refs[0], refs[1], refs[2:2 + 3 * n], refs[2 + 3 * n:]
        for p, name in enumerate(names):
            r0, c0, rows, cols = SMALL_LAYOUT[name]
            pieces = [(0, rows)] if rows % 8 == 0 else [(r, 1) for r in range(rows)]
            for r, cnt in pieces:
                src_ref, first = (late_ref, LATE_ROWS[name, r]) if (name, r) in LATE_ROWS else (g_ref, r0 + r)
                g = src_ref[0, first:first + cnt, c0:c0 + cols]
                for s in range(1, N_DEV):
                    g = g + src_ref[s, first:first + cnt, c0:c0 + cols]
                w, m, v = (ins[3 * p + q][r:r + cnt, :] for q in range(3))
                res = (g,) + _adamw_math(g, w, m, v)
                for q in range(4):
                    outs[4 * p + q][r:r + cnt, :] = res[q]

    shapes = [SMALL_LAYOUT[name][2:] for name in names]
    args = [gathered, late]
    for name, shp in zip(names, shapes):
        args += [wts[name].reshape(shp), mom_m[name].reshape(shp), mom_v[name].reshape(shp)]
    outs = _call(body, name='adamw_small', grid=(1,),
                 in_specs=[_const(gathered.shape), _const(late.shape)]
                 + [_const(shp) for shp in shapes for _ in range(3)],
                 out_specs=[_const(shp) for shp in shapes for _ in range(4)],
                 out_shape=[_sds(shp) for shp in shapes for _ in range(4)], vmem=VMEM_BIG)(*args)
    res = {}
    for p, name in enumerate(names):
        for q, kind in enumerate(('g', 'd', 'm', 'v')):
            res[kind, name] = outs[4 * p + q].reshape(wts[name].shape)
    return res


def kernel(x, c, w_ada, b_ada, g_pre_mix, g_post_mix, w_in, ssm_lam_re, ssm_lam_im, ssm_log_step, ssm_b_re, ssm_b_im, ssm_c_re, ssm_c_im, ssm_d, glu_w, glu_b, g_out_ssm, conv_w, g_out_conv, w_out, g_pre_ffn, g_post_ffn, w_up, ffn_conv_w, w_down, loss_target, m_w_ada, m_b_ada, m_g_pre_mix, m_g_post_mix, m_w_in, m_ssm_lam_re, m_ssm_lam_im, m_ssm_log_step, m_ssm_b_re, m_ssm_b_im, m_ssm_c_re, m_ssm_c_im, m_ssm_d, m_glu_w, m_glu_b, m_g_out_ssm, m_conv_w, m_g_out_conv, m_w_out, m_g_pre_ffn, m_g_post_ffn, m_w_up, m_ffn_conv_w, m_w_down, v_w_ada, v_b_ada, v_g_pre_mix, v_g_post_mix, v_w_in, v_ssm_lam_re, v_ssm_lam_im, v_ssm_log_step, v_ssm_b_re, v_ssm_b_im, v_ssm_c_re, v_ssm_c_im, v_ssm_d, v_glu_w, v_glu_b, v_g_out_ssm, v_conv_w, v_g_out_conv, v_w_out, v_g_pre_ffn, v_g_post_ffn, v_w_up, v_ffn_conv_w, v_w_down):
    args = dict(locals())
    wts = {n: args[n] for n in WEIGHTS}
    mom_m = {n: args['m_' + n] for n in WEIGHTS}
    mom_v = {n: args['v_' + n] for n in WEIGHTS}
    T = x.shape[1]
    tm = min(512, T)
    tw = min(1024, T)
    me = _me()[3]
    xt, tgt = x[0], loss_target[0]

    (c_all,) = _exchange([c], name='gather_c', scatter=False)
    c_all = c_all.reshape(N_DEV, D_MODEL)
    b_cols = lax.dynamic_slice(b_ada, (0, me * ADA_SHARD), (1, ADA_SHARD))
    mod_cols, c_act = _mod_cols(c_all, w_ada[0], b_cols)
    (mod_all,) = _exchange([mod_cols], name='gather_mod', scatter=False)
    mod = lax.dynamic_slice(mod_all, (0, me, 0), (N_DEV, 1, ADA_SHARD)).reshape(N_MOD, 1, D_MODEL)
    sh1, sc1, gt1, sh2, sc2, gt2 = [mod[k] for k in range(N_MOD)]

    w_in_s, glu_s, w_out_s, conv_s = _exchange(
        [w_in[0].astype(BF16), glu_w[0].astype(BF16), w_out[0].astype(BF16), conv_w[0]], name='gather_weights',
        scatter=False)
    glu_full = glu_s.reshape(D_SSM, D_SSM)
    w_out_full = w_out_s.reshape(D_MODEL, D_MODEL)
    cw_full = conv_s.transpose(1, 0, 2).reshape(3, D_CONV)

    lre_x, lim_x = _expand(ssm_lam_re[0]), _expand(ssm_lam_im[0])
    lst_x = jnp.broadcast_to(ssm_log_step[0][:, None], (N_GROUPS, SSM_STATE * SSM_GROUP))
    b_re_x = ssm_b_re[0].reshape(N_GROUPS, -1)
    b_im_x = ssm_b_im[0].reshape(N_GROUPS, -1)
    ar_x, ai_x, bbr_x, bbi_x = _ssm_prep(lre_x, lim_x, lst_x, b_re_x, b_im_x)
    lam_r = ar_x[:, ::SSM_GROUP].reshape(1, N_STATE)
    lam_i = ai_x[:, ::SSM_GROUP].reshape(1, N_STATE)
    big_b_re = _block_diag_b(bbr_x.reshape(N_GROUPS, SSM_STATE, SSM_GROUP)).astype(BF16)
    big_b_im = _block_diag_b(bbi_x.reshape(N_GROUPS, SSM_STATE, SSM_GROUP)).astype(BF16)
    big_c_re = _block_diag_c(ssm_c_re[0]).astype(BF16)
    big_c_im = _block_diag_c(ssm_c_im[0]).astype(BF16)
    head = jnp.arange(D_SSM)
    avg16 = jnp.where(head[:, None] // SSM_GROUP == head[None, :] // SSM_GROUP, 1.0 / SSM_GROUP, 0.0).astype(BF16)
    hd = D_CONV // CONV_HEADS
    avg64 = jnp.where(head[:, None] // hd == head[None, :] // hd, 1.0 / hd, 0.0).astype(BF16)

    (proj, h1), (w_down_s, ffn_conv_s) = _pre_mix(xt, sc1, sh1, g_pre_mix, w_in_s, tw,
                                                  ([w_down[0].astype(BF16), ffn_conv_w[0]], False))
    wd4 = w_down_s.reshape(4, FF_SHARD, D_MODEL)
    cw4 = ffn_conv_s.reshape(2, 4, 3, FF_SHARD)
    u_perm = _to_scan_rows(proj[:, :D_SSM]).astype(BF16)
    (s_re, s_im, y_perm), (w_up_s,) = _ssm_fwd(u_perm, big_b_re, big_b_im, big_c_re, big_c_im, lam_r, lam_i,
                                               ([w_up[0].astype(BF16)], False))
    yssm = _from_scan_rows(y_perm)
    mix_args = (ssm_d, glu_full, glu_b, g_out_ssm, cw_full, g_out_conv, avg16, avg64)
    ycat = _mix_fwd(yssm, proj, *mix_args, tm)
    o, x1, h2 = _out_proj(ycat, w_out_full, xt, gt1, g_post_mix, g_pre_ffn, sc2, sh2, tm)
    up8 = _ffn_up(h2, w_up_s, tw)
    up4 = up8.reshape(2, 4, T, FF_SHARD)
    act = _ffn_act(up4, cw4, tm)
    ddn, dx2, loss_parts, d_gt2, d_g_post_ffn = _ffn_down(act, wd4, x1, tgt, gt2, g_post_ffn, tw)
    loss = lax.psum(jnp.sum(loss_parts[:, 0, 0]), ('x', 'y', 'c'))

    got = {}
    dhid = _ffn_dact(ddn, wd4, up4, cw4, tm)
    g_w_down = _grad_tn(act, ddn, pl.BlockSpec((None, tw, FF_SHARD), lambda g, k: (g, k, 0)),
                        pl.BlockSpec((tw, D_MODEL), lambda g, k: (k, 0)), 4, FF_SHARD, D_MODEL, tw, 'grad_w_down')
    (dup8, dcw_ffn), (got['w_down'],) = _ffn_dup(dhid.reshape(N_DEV, T, FF_SHARD), up8, ffn_conv_s, tm,
                                                 ([g_w_down.reshape(N_DEV, D_FF // N_DEV, D_MODEL)], True))
    g_w_up_halves = _grad_tn(h2, dup8, pl.BlockSpec((tw, D_MODEL), lambda g, k: (k, 0)),
                             pl.BlockSpec((None, tw, FF_SHARD), lambda g, k: (g, k, 0)), N_DEV, D_MODEL, FF_SHARD, tw,
                             'grad_w_up', parts=2)
    (dx1, d_sh2, d_sc2, d_g_pre_ffn, d_o, d_gt1, d_g_post_mix), (got_up_0, got['ffn_conv_w']) = _pre_norm_bwd(
        dup8, pl.BlockSpec((None, tw, FF_SHARD), lambda i, j: (j, i, 0)), w_up_s, x1, dx2, sc2, g_pre_ffn, tw,
        'ffn_in_bwd', ([g_w_up_halves[0], dcw_ffn], True), below=(o, gt1, g_post_mix))

    g_w_out = _grad_tn(ycat, d_o, pl.BlockSpec((tw, D_MODEL), lambda g, k: (k, 0)),
                       pl.BlockSpec((tw, D_MODEL), lambda g, k: (k, 0)), 1, D_MODEL, D_MODEL, tw, 'grad_w_out')
    dy, dconv, dbg, z_b, dlin_b, sums = _mix_bwd(d_o, w_out_full, yssm, proj, *mix_args, tm)
    g_glu_w = _grad_tn(z_b, dlin_b, pl.BlockSpec((tw, D_SSM), lambda g, k: (k, 0)),
                       pl.BlockSpec((tw, D_SSM), lambda g, k: (k, 0)), 1, D_SSM, D_SSM, tw, 'grad_glu_w')
    dy_perm = _to_scan_rows(dy).astype(BF16)
    (du_perm, dbr_blk, dbi_blk, dcr_blk, dci_blk, dar_blk, dai_blk), (got_up_1, got['w_out'], got['glu_w']) = _ssm_bwd(
        dy_perm, u_perm, s_re, s_im, big_b_re, big_b_im, big_c_re, big_c_im, lam_r, lam_i,
        ([g_w_up_halves[1], g_w_out.reshape(N_DEV, D_MODEL // N_DEV, D_MODEL),
          g_glu_w.reshape(N_DEV, D_SSM // N_DEV, D_SSM)], True))
    du_ssm = _from_scan_rows(du_perm)
    dproj = _mix_bwd_proj(dconv, proj, du_ssm, dy, ssm_d, dbg, cw_full, tm)
    dbb_re = _diag_blocks(dbr_blk, True).reshape(N_GROUPS, -1)
    dbb_im = _diag_blocks(dbi_blk, True).reshape(N_GROUPS, -1)
    d_c_re = _diag_blocks(dcr_blk, False).transpose(0, 2, 1)
    d_c_im = _diag_blocks(dci_blk, False).transpose(0, 2, 1)
    lane = jnp.arange(SSM_STATE * SSM_GROUP)
    seg = jnp.where(lane[:, None] // SSM_GROUP == lane[None, :] // SSM_GROUP, 1.0, 0.0).astype(BF16)
    d_b_re_x, d_b_im_x, d_lre_x, d_lim_x, d_lst = _ssm_prep_bwd(
        lre_x, lim_x, lst_x, b_re_x, b_im_x, dbb_re, dbb_im, _expand(dar_blk.reshape(N_GROUPS, SSM_STATE)),
        _expand(dai_blk.reshape(N_GROUPS, SSM_STATE)), seg)

    row = lambda a: a.reshape(-1, PACK_COLS)
    blank = jnp.zeros((1, PACK_COLS), F32)
    small_pack = jnp.concatenate([
        d_b_re_x, d_b_im_x, row(d_c_re), row(d_c_im), blank, blank, d_gt1, d_sh2, d_sc2, d_gt2, blank,
        d_g_post_mix, row(d_lre_x[:, ::SSM_GROUP]), row(d_lim_x[:, ::SSM_GROUP]),
        jnp.pad(d_lst.reshape(1, N_GROUPS), ((0, 0), (0, PACK_COLS - N_GROUPS))), row(sums[0:4]), d_g_pre_ffn,
        d_g_post_ffn, jnp.zeros((SMALL_ROWS - 145, PACK_COLS), F32)])
    g_w_in, (small_all,) = _grad_tn(
        h1, dproj, pl.BlockSpec((tw, D_MODEL), lambda g, k: (k, 0)), pl.BlockSpec((tw, IN_SHARD), lambda g, k: (k, g)),
        N_DEV, D_MODEL, IN_SHARD, tw, 'grad_w_in', ride=([small_pack], False))
    g_conv_slots = jnp.concatenate([sums[4:7], jnp.zeros((5, D_CONV), F32)]).reshape(
        8, N_DEV, D_CONV // N_DEV).transpose(1, 0, 2)
    (grad_x, d_sh1, d_sc1, d_g_pre_mix), (got['w_in'], got['conv_w']) = _pre_norm_bwd(
        dproj, pl.BlockSpec((tw, 4 * IN_SHARD), lambda i, j: (i, j)), w_in_s, xt, dx1, sc1, g_pre_mix, tw,
        'mix_in_bwd', ([g_w_in, g_conv_slots], True), group=4)
    late_pack = jnp.concatenate([d_sh1, d_sc1, d_g_pre_mix, jnp.zeros((5, PACK_COLS), F32)])
    (late_all,) = _exchange([late_pack], name='gather_late_grads', scatter=False)
    res = _adamw_small(small_all, late_all, wts, mom_m, mom_v)

    dmod_all = jnp.concatenate([late_all[:, 0:2, :], small_all[:, B_ADA_ROW + 2:B_ADA_ROW + N_MOD, :]],
                               axis=1).reshape(N_DEV, N_MOD * D_MODEL)
    dmod_cols = lax.dynamic_slice(dmod_all, (0, me * ADA_SHARD), (N_DEV, ADA_SHARD))
    g_w_ada = _grad_w_ada(c_act.T, dmod_cols)

    pieces = {n: [slots[:, :3, :] if n in ('conv_w', 'ffn_conv_w') else slots] for n, slots in got.items()}
    pieces['w_up'] = [got_up_0, got_up_1]
    for n, parts in pieces.items():
        outs = _adamw(parts, wts[n][0], mom_m[n][0], mom_v[n][0], 'adamw_' + n)
        for kind, val in zip(('g', 'd', 'm', 'v'), outs):
            res[kind, n] = val[None]
    outs = _adamw([g_w_ada[None]], w_ada[0], m_w_ada[0], v_w_ada[0], 'adamw_w_ada')
    for kind, val in zip(('g', 'd', 'm', 'v'), outs):
        res[kind, 'w_ada'] = val[None]

    return (loss, grad_x[None], *[res['g', n] for n in WEIGHTS], *[res['d', n] for n in WEIGHTS],
            *[res['m', n] for n in WEIGHTS], *[res['v', n] for n in WEIGHTS])
```

```python
import math

import jax
import jax.numpy as jnp
from jax import lax
from jax.experimental import pallas as pl
from jax.experimental.pallas import tpu as pltpu

F32, BF16 = jnp.float32, jnp.bfloat16

D_MODEL = 1024
D_SSM = 512
D_CONV = 512
SSM_GROUP = 16
N_GROUPS = 32
SSM_STATE = 64
N_STATE = N_GROUPS * SSM_STATE
CONV_HEADS = 8
D_FF = 2816
N_MOD = 6
D_IN_PROJ = D_SSM + 3 * D_CONV
N_DEV = 8
FF_SHARD = 2 * D_FF // N_DEV
IN_SHARD = D_IN_PROJ // N_DEV
ADA_SHARD = N_MOD * D_MODEL // N_DEV
EPS = 1e-6
LAMBDA_RE_MAX = -1e-4
ADAM_LR, ADAM_B1, ADAM_B2, ADAM_EPS, ADAM_WD, ADAM_STEP = 0.001, 0.9, 0.999, 1e-08, 0.01, 10
GELU_C = math.sqrt(2.0 / math.pi)
GELU_A = 0.044715

SUBLANES = 8
HALO = 8
HALO16 = 16
SCAN_UNROLL = 8
STATE_BLOCK = 256
CHAN_BLOCK = 128
VMEM_BIG = 48 << 20
VMEM_MOST = 58 << 20

WEIGHTS = ['w_ada', 'b_ada', 'g_pre_mix', 'g_post_mix', 'w_in', 'ssm_lam_re', 'ssm_lam_im', 'ssm_log_step',
           'ssm_b_re', 'ssm_b_im', 'ssm_c_re', 'ssm_c_im', 'ssm_d', 'glu_w', 'glu_b', 'g_out_ssm', 'conv_w',
           'g_out_conv', 'w_out', 'g_pre_ffn', 'g_post_ffn', 'w_up', 'ffn_conv_w', 'w_down']
SHARDED = ('w_ada', 'w_in', 'glu_w', 'conv_w', 'w_out', 'w_up', 'ffn_conv_w', 'w_down')
PACK_COLS = 1024


def _call(body, *, name, grid, in_specs, out_specs, out_shape, scratch=(), sem=None, vmem=None, ride=None):
    params = {}
    if vmem is not None:
        params['vmem_limit_bytes'] = vmem
    if ride is None:
        if sem is not None:
            params['dimension_semantics'] = sem
        return pl.pallas_call(body, name=name, grid=grid, in_specs=in_specs, out_specs=out_specs,
                              out_shape=out_shape, scratch_shapes=list(scratch),
                              compiler_params=pltpu.CompilerParams(**params))
    arrs, scatter = ride
    single = not isinstance(out_shape, (list, tuple))
    out_shape_l = [out_shape] if single else list(out_shape)
    out_specs_l = [out_specs] if single else list(out_specs)
    n, n_in, n_out, n_scr = len(arrs), len(in_specs), len(out_shape_l), len(scratch)
    any_spec = pl.BlockSpec(memory_space=pl.ANY)
    params['dimension_semantics'] = ('arbitrary',) * len(grid)

    def carried(*refs):
        ins, rin = refs[:n_in], refs[n_in:n_in + n]
        outs, rout = refs[n_in + n:n_in + n + n_out], refs[n_in + n + n_out:n_in + 2 * n + n_out]
        scr, sems = refs[n_in + 2 * n + n_out:n_in + 2 * n + n_out + n_scr], refs[n_in + 2 * n + n_out + n_scr:]
        first = pl.program_id(0) == 0
        last = pl.program_id(0) == grid[0] - 1
        for ax in range(1, len(grid)):
            first = jnp.logical_and(first, pl.program_id(ax) == 0)
            last = jnp.logical_and(last, pl.program_id(ax) == grid[ax] - 1)

        @pl.when(first)
        def _():
            _exchange_start(rin, rout, sems, scatter)

        body(*ins, *outs, *scr)

        @pl.when(last)
        def _():
            _exchange_wait(rin, rout, sems, scatter)

    call = pl.pallas_call(carried, name=name, grid=grid, in_specs=list(in_specs) + [any_spec] * n,
                          out_specs=out_specs_l + [any_spec] * n,
                          out_shape=out_shape_l + _exchange_shapes(arrs, scatter),
                          scratch_shapes=list(scratch) + _exchange_sems(n),
                          compiler_params=pltpu.CompilerParams(**params))

    def run(*args):
        res = call(*args, *arrs)
        own = res[0] if single else list(res[:n_out])
        return own, list(res[n_out:])

    return run


def _const(shape):
    nd = len(shape)
    return pl.BlockSpec(shape, lambda *_: (0,) * nd)


def _sds(shape, dtype=F32):
    return jax.ShapeDtypeStruct(shape, dtype)


def _dot(a, b):
    return jnp.dot(a, b, preferred_element_type=F32)


def _dot_nt(a, b):
    return lax.dot_general(a, b, (((1,), (1,)), ((), ())), preferred_element_type=F32)


def _dot_tn(a, b):
    return lax.dot_general(a, b, (((0,), (0,)), ((), ())), preferred_element_type=F32)


def _dot_split(x, mat, parts):
    acc = None
    rem = x
    for _ in range(parts):
        piece = rem.astype(BF16)
        rem = rem - piece.astype(F32)
        term = _dot(piece, mat)
        acc = term if acc is None else acc + term
    return acc


def _sigmoid(x):
    return 1.0 / (1.0 + jnp.exp(-x))


def _gelu(x):
    t = jnp.tanh(GELU_C * (x + GELU_A * x * x * x))
    return 0.5 * x * (1.0 + t), t


def _gelu_grad(x, t):
    return 0.5 * (1.0 + t) + 0.5 * x * (1.0 - t * t) * GELU_C * (1.0 + 3.0 * GELU_A * x * x)


def _rsqrt_mean(x):
    return lax.rsqrt(jnp.mean(x * x, axis=-1, keepdims=True) + EPS)


def _colsum(x):
    return jnp.sum(x, axis=0, keepdims=True)


def _shifts_down(x, halo):
    ext = jnp.concatenate([halo, x], axis=0)
    return pltpu.roll(ext, 1, 0)[halo.shape[0]:], pltpu.roll(ext, 2, 0)[halo.shape[0]:]


def _shifts_up(x, halo):
    n = x.shape[0]
    ext = jnp.concatenate([x, halo], axis=0)
    total = ext.shape[0]
    return pltpu.roll(ext, total - 1, 0)[:n], pltpu.roll(ext, total - 2, 0)[:n]


def _conv3(x, halo, w_ref):
    x1, x2 = _shifts_down(x, halo)
    return w_ref[0:1, :] * x2 + w_ref[1:2, :] * x1 + w_ref[2:3, :] * x, x1, x2


def _conv3_t(g, halo, w_ref):
    g1, g2 = _shifts_up(g, halo)
    return w_ref[2:3, :] * g + w_ref[1:2, :] * g1 + w_ref[0:1, :] * g2, g1, g2


def _silu_parts(x):
    s = _sigmoid(x)
    return x * s, s * (1.0 + x * (1.0 - s))


def _norm_bwd(dn, x, r, g):
    gd = g * dn
    return r * gd - x * (r * r * r) * jnp.mean(gd * x, axis=-1, keepdims=True)


def _head_norm_bwd(dn, y, rs, g, avg):
    gd = g * dn
    return rs * gd - y * (rs * rs * rs) * _dot_split(gd * y, avg, 2)


def _me():
    x, y, c = lax.axis_index('x'), lax.axis_index('y'), lax.axis_index('c')
    return x, y, c, 4 * x + 2 * y + c


def _peer(k):
    x, y, c, _ = _me()
    px = 1 - x if k & 4 else x
    py = 1 - y if k & 2 else y
    pc = 1 - c if k & 1 else c
    return (px, py, pc), 4 * px + 2 * py + pc


SIBLING = 1
OTHER_CHIPS = (2, 4, 6)


def _remote(src, dst, sems, a, k, dev):
    return pltpu.make_async_remote_copy(src_ref=src, dst_ref=dst, send_sem=sems[0].at[a, k - 1],
                                        recv_sem=sems[1].at[a, k - 1], device_id=dev,
                                        device_id_type=pl.DeviceIdType.MESH)


def _exchange_copies(ins, outs, sems, scatter):
    me = _me()[3]
    local, first, relay, arrivals = [], [], [], []
    for a in range(len(ins)):
        src = ins[a].at[me] if scatter else ins[a]
        local.append(pltpu.make_async_copy(src, outs[a].at[me], sems[2].at[a]))
        for k in range(1, N_DEV):
            dev, idx = _peer(k)
            landed = _remote(src, outs[a].at[idx], sems, a, k, dev)
            if scatter:
                first.append(_remote(ins[a].at[idx], outs[a].at[me], sems, a, k, dev))
                arrivals.append(landed)
            elif k == SIBLING:
                first.append(_remote(src, outs[a].at[me], sems, a, k, dev))
                arrivals.append(landed)
            elif k in OTHER_CHIPS:
                first.append(_remote(src, outs[a].at[me], sems, a, k, dev))
                sib, _ = _peer(SIBLING)
                relay.append((landed, _remote(outs[a].at[idx], outs[a].at[idx], sems, a, k | SIBLING, sib)))
            else:
                arrivals.append(landed)
    return local, first, relay, arrivals


def _exchange_start(ins, outs, sems, scatter):
    local, first, _, _ = _exchange_copies(ins, outs, sems, scatter)
    for cp in local + first:
        cp.start()


def _exchange_wait(ins, outs, sems, scatter):
    local, first, relay, arrivals = _exchange_copies(ins, outs, sems, scatter)
    for landed, forward in relay:
        landed.wait_recv()
        forward.start()
    for cp in arrivals:
        cp.wait_recv()
    for cp in first + [forward for _, forward in relay]:
        cp.wait_send()
    for cp in local:
        cp.wait()


def _exchange_shapes(arrs, scatter):
    return [_sds(a.shape if scatter else (N_DEV,) + a.shape, a.dtype) for a in arrs]


def _exchange_sems(n):
    return [pltpu.SemaphoreType.DMA((n, N_DEV - 1)), pltpu.SemaphoreType.DMA((n, N_DEV - 1)),
            pltpu.SemaphoreType.DMA((n,))]


def _exchange(arrs, *, name, scatter):
    n = len(arrs)

    def body(*refs):
        _exchange_start(refs[:n], refs[n:2 * n], refs[2 * n:], scatter)
        _exchange_wait(refs[:n], refs[n:2 * n], refs[2 * n:], scatter)

    any_spec = pl.BlockSpec(memory_space=pl.ANY)
    outs = pl.pallas_call(body, name=name, out_shape=_exchange_shapes(arrs, scatter), in_specs=[any_spec] * n,
                          out_specs=[any_spec] * n, scratch_shapes=_exchange_sems(n))(*arrs)
    return list(outs)


def _mod_cols(c_all, w_ada, b_cols):
    def body(c_ref, w_ref, b_ref, mod_ref, act_ref):
        c = c_ref[...]
        act = c * _sigmoid(c)
        act_ref[...] = act
        mod_ref[...] = _dot(act.astype(BF16), w_ref[...].astype(BF16)) + b_ref[...]

    return _call(body, name='mod_cols', grid=(1,),
                 in_specs=[_const(c_all.shape), _const(w_ada.shape), _const(b_cols.shape)],
                 out_specs=[_const((N_DEV, ADA_SHARD)), _const(c_all.shape)],
                 out_shape=[_sds((N_DEV, ADA_SHARD)), _sds(c_all.shape)], vmem=VMEM_BIG)(c_all, w_ada, b_cols)


def _grad_w_ada(act_t, dmod_cols):
    def body(a_ref, d_ref, o_ref):
        o_ref[...] = _dot(a_ref[...], d_ref[...])

    return _call(body, name='grad_w_ada', grid=(1,), in_specs=[_const(act_t.shape), _const(dmod_cols.shape)],
                 out_specs=_const((D_MODEL, ADA_SHARD)), out_shape=_sds((D_MODEL, ADA_SHARD)),
                 vmem=VMEM_BIG)(act_t, dmod_cols)


def _pre_mix(x, sc, sh, g, w_s, tm, ride):
    T = x.shape[0]

    def body(x_ref, sc_ref, sh_ref, g_ref, w_ref, proj_ref, h_ref):
        @pl.when(pl.program_id(1) == 0)
        def _():
            xv = x_ref[...]
            h_ref[...] = ((xv * _rsqrt_mean(xv) * g_ref[...]) * (1.0 + sc_ref[...]) + sh_ref[...]).astype(BF16)

        for s in range(2):
            proj_ref[:, s * IN_SHARD:(s + 1) * IN_SHARD] = _dot(h_ref[...], w_ref[s])

    row = pl.BlockSpec((tm, D_MODEL), lambda i, j: (i, 0))
    vec = _const((1, D_MODEL))
    return _call(body, name='pre_mix', grid=(T // tm, N_DEV // 2),
                 in_specs=[row, vec, vec, vec, pl.BlockSpec((2, D_MODEL, IN_SHARD), lambda i, j: (j, 0, 0))],
                 out_specs=[pl.BlockSpec((tm, 2 * IN_SHARD), lambda i, j: (i, j)), row],
                 out_shape=[_sds((T, D_IN_PROJ)), _sds((T, D_MODEL), BF16)],
                 sem=('parallel', 'arbitrary'), ride=ride)(x, sc, sh, g, w_s)


def _halo_before(tm, rows=HALO):
    return lambda i: jnp.maximum(i * (tm // rows) - 1, 0)


def _halo_after(tm, T, rows=HALO):
    return lambda i: jnp.minimum((i + 1) * (tm // rows), T // rows - 1)


def _mix_fwd(yssm, proj, d, glu_w, glu_b, g_ssm, cw, g_conv, avg16, avg64, tm):
    T = yssm.shape[0]
    hb = _halo_before(tm)

    def body(y_ref, p_ref, ph_ref, d_ref, gw_ref, gb_ref, gs_ref, cw_ref, gc_ref, a16_ref, a64_ref, o_ref):
        i = pl.program_id(0)
        u = p_ref[:, 0:D_SSM]
        y = y_ref[...] + d_ref[...] * u
        z, _ = _gelu(y)
        gate = _sigmoid(_dot(z.astype(BF16), gw_ref[...]) + gb_ref[...])
        ya = z * gate
        rs = lax.rsqrt(_dot_split(ya * ya, a16_ref[...], 2) + EPS)
        o_ref[:, 0:D_SSM] = (ya * rs * gs_ref[...]).astype(BF16)
        bg = p_ref[:, D_SSM:D_SSM + D_CONV]
        cv = p_ref[:, D_SSM + D_CONV:D_SSM + 2 * D_CONV] * p_ref[:, D_SSM + 2 * D_CONV:D_IN_PROJ]
        hv = ph_ref[:, D_SSM + D_CONV:D_SSM + 2 * D_CONV] * ph_ref[:, D_SSM + 2 * D_CONV:D_IN_PROJ]
        hv = jnp.where(i > 0, hv, 0.0)
        conv, _, _ = _conv3(cv, hv, cw_ref)
        yb = bg * conv
        rsb = lax.rsqrt(_dot_split(yb * yb, a64_ref[...], 2) + EPS)
        o_ref[:, D_SSM:D_MODEL] = (yb * rsb * gc_ref[...]).astype(BF16)

    vec = _const((1, D_SSM))
    sq = _const((D_SSM, D_SSM))
    return _call(body, name='mix_fwd', grid=(T // tm,),
                 in_specs=[pl.BlockSpec((tm, D_SSM), lambda i: (i, 0)), pl.BlockSpec((tm, D_IN_PROJ), lambda i: (i, 0)),
                           pl.BlockSpec((HALO, D_IN_PROJ), lambda i: (hb(i), 0)), vec, sq, vec, vec,
                           _const((3, D_CONV)), vec, sq, sq],
                 out_specs=pl.BlockSpec((tm, D_MODEL), lambda i: (i, 0)), out_shape=_sds((T, D_MODEL), BF16),
                 sem=('parallel',), vmem=VMEM_BIG)(yssm, proj, proj, d, glu_w, glu_b, g_ssm, cw, g_conv, avg16, avg64)


def _out_proj(ycat, w_out, x, gt, g_post, g_pre, sc, sh, tm):
    T = x.shape[0]

    def body(y_ref, w_ref, x_ref, gt_ref, gp_ref, g2_ref, sc_ref, sh_ref, o_ref, x1_ref, h_ref):
        o = _dot(y_ref[...], w_ref[...])
        o_ref[...] = o
        x1 = x_ref[...] + gt_ref[...] * (o * _rsqrt_mean(o) * gp_ref[...])
        x1_ref[...] = x1
        h_ref[...] = ((x1 * _rsqrt_mean(x1) * g2_ref[...]) * (1.0 + sc_ref[...]) + sh_ref[...]).astype(BF16)

    row = pl.BlockSpec((tm, D_MODEL), lambda i: (i, 0))
    vec = _const((1, D_MODEL))
    return _call(body, name='out_proj', grid=(T // tm,),
                 in_specs=[row, _const((D_MODEL, D_MODEL)), row, vec, vec, vec, vec, vec],
                 out_specs=[row, row, row],
                 out_shape=[_sds((T, D_MODEL)), _sds((T, D_MODEL)), _sds((T, D_MODEL), BF16)],
                 sem=('parallel',), vmem=VMEM_BIG)(ycat, w_out, x, gt, g_post, g_pre, sc, sh)


def _ffn_up(h2, w_s, tm):
    T = h2.shape[0]

    def body(h_ref, w_ref, o_ref):
        o_ref[...] = _dot(h_ref[...], w_ref[...]).astype(BF16)

    return _call(body, name='ffn_up', grid=(T // tm, N_DEV),
                 in_specs=[pl.BlockSpec((tm, D_MODEL), lambda i, j: (i, 0)),
                           pl.BlockSpec((None, D_MODEL, FF_SHARD), lambda i, j: (j, 0, 0))],
                 out_specs=pl.BlockSpec((None, tm, FF_SHARD), lambda i, j: (j, i, 0)),
                 out_shape=_sds((N_DEV, T, FF_SHARD), BF16), sem=('parallel', 'parallel'))(h2, w_s)


def _ffn_hidden(up_ref, halo_ref, cw_ref, i):
    hid = []
    for part in range(2):
        halo = jnp.where(i > 0, halo_ref[part].astype(F32), 0.0)
        hid.append(_conv3(up_ref[part].astype(F32), halo, cw_ref.at[part])[0])
    return hid


def _ffn_act(up4, cw4, tm):
    T = up4.shape[2]
    hb = _halo_before(tm, HALO16)

    def body(up_ref, halo_ref, cw_ref, o_ref, hid_ref):
        hid_a, hid_v = _ffn_hidden(up_ref, halo_ref, cw_ref, pl.program_id(0))
        o_ref[...] = (_silu_parts(hid_a)[0] * hid_v).astype(BF16)
        hid_ref[0] = hid_a.astype(BF16)
        hid_ref[1] = hid_v.astype(BF16)

    return _call(body, name='ffn_act', grid=(T // tm, 4),
                 in_specs=[pl.BlockSpec((2, None, tm, FF_SHARD), lambda i, j: (0, j, i, 0)),
                           pl.BlockSpec((2, None, HALO16, FF_SHARD), lambda i, j: (0, j, hb(i), 0)),
                           pl.BlockSpec((2, None, 3, FF_SHARD), lambda i, j: (0, j, 0, 0))],
                 out_specs=[pl.BlockSpec((None, tm, FF_SHARD), lambda i, j: (j, i, 0)),
                            pl.BlockSpec((2, None, tm, FF_SHARD), lambda i, j: (0, j, i, 0))],
                 out_shape=[_sds((4, T, FF_SHARD), BF16), _sds((2, 4, T, FF_SHARD), BF16)],
                 sem=('parallel', 'parallel'))(up4, up4, cw4)


def _ffn_down(act, wd4, x1, tgt, gt, g_post, tm):
    T = x1.shape[0]
    nb = T // tm

    def body(a_ref, w_ref, x1_ref, t_ref, gt_ref, g_ref, ddn_ref, dx_ref, loss_ref, dgt_ref, dg_ref, dn_ref):
        i, j = pl.program_id(0), pl.program_id(1)
        part = _dot(a_ref[0], w_ref[0]) + _dot(a_ref[1], w_ref[1])

        @pl.when(jnp.logical_and(i == 0, j == 0))
        def _():
            dgt_ref[...] = jnp.zeros_like(dgt_ref)
            dg_ref[...] = jnp.zeros_like(dg_ref)

        @pl.when(j == 0)
        def _():
            dn_ref[...] = part

        @pl.when(j > 0)
        def _():
            dn_ref[...] += part

        @pl.when(j == 1)
        def _():
            dn, gv, gate = dn_ref[...], g_ref[...], gt_ref[...]
            r = _rsqrt_mean(dn)
            normed = dn * r * gv
            err = x1_ref[...] + gate * normed - t_ref[...]
            dx = err * (1.0 / D_MODEL)
            dx_ref[...] = dx
            tot = jnp.sum(jnp.sum(err * err, axis=1, keepdims=True), axis=0, keepdims=True) * (0.5 / D_MODEL)
            loss_ref[...] = jnp.broadcast_to(tot, (8, 128))
            dgt_ref[...] += _colsum(dx * normed)
            dnn = dx * gate
            dg_ref[...] += _colsum(dnn * dn * r)
            ddn_ref[...] = _norm_bwd(dnn, dn, r, gv).astype(BF16)

    row = pl.BlockSpec((tm, D_MODEL), lambda i, j: (i, 0))
    vec = _const((1, D_MODEL))
    return _call(body, name='ffn_down', grid=(nb, 2),
                 in_specs=[pl.BlockSpec((2, tm, FF_SHARD), lambda i, j: (j, i, 0)),
                           pl.BlockSpec((2, FF_SHARD, D_MODEL), lambda i, j: (j, 0, 0)), row, row, vec, vec],
                 out_specs=[row, row, pl.BlockSpec((None, 8, 128), lambda i, j: (i, 0, 0)), vec, vec],
                 out_shape=[_sds((T, D_MODEL), BF16), _sds((T, D_MODEL)), _sds((nb, 8, 128)), _sds((1, D_MODEL)),
                            _sds((1, D_MODEL))],
                 scratch=[pltpu.VMEM((tm, D_MODEL), F32)], sem=('arbitrary', 'arbitrary'),
                 vmem=VMEM_BIG)(act, wd4, x1, tgt, gt, g_post)


def _ssm_prep(lre, lim, lst, b_re, b_im):
    def body(lre_ref, lim_ref, lst_ref, br_ref, bi_ref, ar_ref, ai_ref, bbr_ref, bbi_ref):
        ar, ai, qr, qi = _zoh(lre_ref[...], lim_ref[...], lst_ref[...])[:4]
        ar_ref[...] = ar
        ai_ref[...] = ai
        bbr_ref[...] = qr * br_ref[...] - qi * bi_ref[...]
        bbi_ref[...] = qr * bi_ref[...] + qi * br_ref[...]

    shp = lre.shape
    return _call(body, name='ssm_prep', grid=(1,), in_specs=[_const(shp)] * 5, out_specs=[_const(shp)] * 4,
                 out_shape=[_sds(shp)] * 4)(lre, lim, lst, b_re, b_im)


def _zoh(lre, lim, lst):
    lr = jnp.minimum(lre, LAMBDA_RE_MAX)
    st = jnp.exp(lst)
    mag = jnp.exp(lr * st)
    ar = mag * jnp.cos(lim * st)
    ai = mag * jnp.sin(lim * st)
    den = lr * lr + lim * lim
    qr = ((ar - 1.0) * lr + ai * lim) / den
    qi = (ai * lr - (ar - 1.0) * lim) / den
    return ar, ai, qr, qi, lr, st, den


def _ssm_prep_bwd(lre, lim, lst, b_re, b_im, dbbr, dbbi, dar, dai, seg):
    def body(lre_ref, lim_ref, lst_ref, br_ref, bi_ref, dbbr_ref, dbbi_ref, dar_ref, dai_ref, seg_ref,
             dbr_ref, dbi_ref, dlre_ref, dlim_ref, dlst_ref):
        lre_v = lre_ref[...]
        li = lim_ref[...]
        ar, ai, qr, qi, lr, st, den = _zoh(lre_v, li, lst_ref[...])
        br, bi, gbr, gbi = br_ref[...], bi_ref[...], dbbr_ref[...], dbbi_ref[...]
        dbr_ref[...] = qr * gbr + qi * gbi
        dbi_ref[...] = qr * gbi - qi * gbr
        gqr = _dot_split(br * gbr + bi * gbi, seg_ref[...], 3)
        gqi = _dot_split(br * gbi - bi * gbr, seg_ref[...], 3)
        ir, ii = lr / den, -li / den
        gar = dar_ref[...] + ir * gqr + ii * gqi
        gai = dai_ref[...] + ir * gqi - ii * gqr
        tr, ti = qr * ir - qi * ii, qr * ii + qi * ir
        glr = -(tr * gqr + ti * gqi)
        gli = -(tr * gqi - ti * gqr)
        gzr = ar * gar + ai * gai
        gzi = ar * gai - ai * gar
        glr = glr + st * gzr
        gli = gli + st * gzi
        gst = (lr * gzr + li * gzi) * st
        dlre_ref[...] = jnp.where(lre_v < LAMBDA_RE_MAX, glr, 0.0)
        dlim_ref[...] = gli
        dlst_ref[...] = jnp.sum(gst, axis=1, keepdims=True) * (1.0 / SSM_GROUP)

    shp = lre.shape
    return _call(body, name='ssm_prep_bwd', grid=(1,), in_specs=[_const(shp)] * 9 + [_const(seg.shape)],
                 out_specs=[_const(shp)] * 4 + [_const((N_GROUPS, 1))],
                 out_shape=[_sds(shp)] * 4 + [_sds((N_GROUPS, 1))], vmem=VMEM_BIG)(
                     lre, lim, lst, b_re, b_im, dbbr, dbbi, dar, dai, seg)


def _scan_specs(T):
    half = lambda cb: cb // 2
    return dict(
        chan=pl.BlockSpec((T, CHAN_BLOCK), lambda cb: (0, half(cb))),
        state=pl.BlockSpec((T, STATE_BLOCK), lambda cb: (0, cb)),
        b=pl.BlockSpec((CHAN_BLOCK, STATE_BLOCK), lambda cb: (half(cb), cb)),
        c=pl.BlockSpec((STATE_BLOCK, CHAN_BLOCK), lambda cb: (cb, half(cb))),
        lam=pl.BlockSpec((1, STATE_BLOCK), lambda cb: (0, cb)),
    )


def _complex_power(re, im, n):
    out = None
    while True:
        if n & 1:
            out = (re, im) if out is None else (out[0] * re - out[1] * im, out[0] * im + out[1] * re)
        n >>= 1
        if n == 0:
            return out
        re, im = re * re - im * im, 2.0 * re * im


def _rows8(i):
    if isinstance(i, int):
        return pl.ds(i * SUBLANES, SUBLANES)
    return pl.ds(pl.multiple_of(i * SUBLANES, SUBLANES), SUBLANES)


def _scan_loop(n_steps, body, init):
    trips = n_steps // SCAN_UNROLL

    def trip(t, carry):
        for u in range(SCAN_UNROLL):
            carry = body(t * SCAN_UNROLL + u, carry)
        return carry

    carry = lax.fori_loop(0, trips, trip, init)
    for step in range(trips * SCAN_UNROLL, n_steps):
        carry = body(step, carry)
    return carry


def _ssm_fwd(u_perm, b_re, b_im, c_re, c_im, lam_r, lam_i, ride):
    T = u_perm.shape[0]
    ls = T // SUBLANES
    rc = min(512, T)
    sp = _scan_specs(T)

    def body(u_ref, bre_ref, bim_ref, cre_ref, cim_ref, lr_ref, li_ref, sre_ref, sim_ref, y_ref):
        cb = pl.program_id(0)
        for c in range(T // rc):
            rows = pl.ds(c * rc, rc)
            sre_ref[rows, :] = _dot(u_ref[rows, :], bre_ref[...])
            sim_ref[rows, :] = _dot(u_ref[rows, :], bim_ref[...])
        shp = (SUBLANES, STATE_BLOCK)
        lr = jnp.broadcast_to(lr_ref[...], shp)
        li = jnp.broadcast_to(li_ref[...], shp)
        zero = jnp.zeros(shp, F32)

        def step(i, carry):
            sr, si = carry
            rows = _rows8(i)
            nr = lr * sr - li * si + sre_ref[rows, :]
            ni = lr * si + li * sr + sim_ref[rows, :]
            sre_ref[rows, :] = nr
            sim_ref[rows, :] = ni
            return nr, ni

        fr, fi = _scan_loop(ls, step, (zero, zero))
        pr, pi_ = _complex_power(lr, li, ls)
        row = lax.broadcasted_iota(jnp.int32, shp, 0)
        ir, ii = zero, zero
        for _ in range(SUBLANES - 1):
            er = fr + pr * ir - pi_ * ii
            ei = fi + pr * ii + pi_ * ir
            ir = jnp.where(row == 0, 0.0, pltpu.roll(er, 1, 0))
            ii = jnp.where(row == 0, 0.0, pltpu.roll(ei, 1, 0))

        def fix(i, carry):
            cr, ci = carry
            rows = _rows8(i)
            nr = lr * cr - li * ci
            ni = lr * ci + li * cr
            sre_ref[rows, :] += nr
            sim_ref[rows, :] += ni
            return nr, ni

        _scan_loop(ls, fix, (ir, ii))
        for c in range(T // rc):
            rows = pl.ds(c * rc, rc)
            yc = _dot(sre_ref[rows, :].astype(BF16), cre_ref[...]) - _dot(sim_ref[rows, :].astype(BF16), cim_ref[...])

            @pl.when(cb % 2 == 0)
            def _():
                y_ref[rows, :] = yc

            @pl.when(cb % 2 == 1)
            def _():
                y_ref[rows, :] += yc

    return _call(body, name='ssm_fwd', grid=(N_STATE // STATE_BLOCK,),
                 in_specs=[sp['chan'], sp['b'], sp['b'], sp['c'], sp['c'], sp['lam'], sp['lam']],
                 out_specs=[sp['state'], sp['state'], sp['chan']],
                 out_shape=[_sds((T, N_STATE)), _sds((T, N_STATE)), _sds((T, D_SSM))],
                 sem=('arbitrary',), vmem=VMEM_BIG, ride=ride)(u_perm, b_re, b_im, c_re, c_im, lam_r, lam_i)


def _ssm_bwd(dy_perm, u_perm, s_re, s_im, b_re, b_im, c_re, c_im, lam_r, lam_i, ride):
    T = u_perm.shape[0]
    ls = T // SUBLANES
    rc = min(512, T)
    sp = _scan_specs(T)
    ncb = N_STATE // STATE_BLOCK

    def body(dy_ref, u_ref, sre_ref, sim_ref, bre_ref, bim_ref, cre_ref, cim_ref, lr_ref, li_ref,
             du_ref, dbr_ref, dbi_ref, dcr_ref, dci_ref, dar_ref, dai_ref, gre_ref, gim_ref):
        cb = pl.program_id(0)
        for c in range(T // rc):
            rows = pl.ds(c * rc, rc)
            gre_ref[rows, :] = _dot_nt(dy_ref[rows, :], cre_ref[...])
            gim_ref[rows, :] = -_dot_nt(dy_ref[rows, :], cim_ref[...])
        shp = (SUBLANES, STATE_BLOCK)
        lr = jnp.broadcast_to(lr_ref[...], shp)
        li = jnp.broadcast_to(li_ref[...], shp)
        zero = jnp.zeros(shp, F32)

        def step(k, carry):
            gr, gi = carry
            rows = _rows8(ls - 1 - k)
            nr = lr * gr + li * gi + gre_ref[rows, :]
            ni = lr * gi - li * gr + gim_ref[rows, :]
            gre_ref[rows, :] = nr
            gim_ref[rows, :] = ni
            return nr, ni

        fr, fi = _scan_loop(ls, step, (zero, zero))
        pr, pi_ = _complex_power(lr, -li, ls)
        row = lax.broadcasted_iota(jnp.int32, shp, 0)
        cr, ci = zero, zero
        for _ in range(SUBLANES - 1):
            er = fr + pr * cr - pi_ * ci
            ei = fi + pr * ci + pi_ * cr
            cr = jnp.where(row == SUBLANES - 1, 0.0, pltpu.roll(er, SUBLANES - 1, 0))
            ci = jnp.where(row == SUBLANES - 1, 0.0, pltpu.roll(ei, SUBLANES - 1, 0))

        def fix(k, carry):
            dr, di, ar, ai = carry
            rows = _rows8(ls - 1 - k)
            dr, di = lr * dr + li * di, lr * di - li * dr
            gr = gre_ref[rows, :] + dr
            gi = gim_ref[rows, :] + di
            gre_ref[rows, :] = gr
            gim_ref[rows, :] = gi
            prev = _rows8(ls - 2 - k)
            spr, spi = sre_ref[prev, :], sim_ref[prev, :]
            return dr, di, ar + gr * spr + gi * spi, ai + gi * spr - gr * spi

        dr, di, ar, ai = _scan_loop(ls - 1, fix, (cr, ci, zero, zero))
        first = pl.ds(0, SUBLANES)
        last = pl.ds((ls - 1) * SUBLANES, SUBLANES)
        gr = gre_ref[first, :] + (lr * dr + li * di)
        gi = gim_ref[first, :] + (lr * di - li * dr)
        gre_ref[first, :] = gr
        gim_ref[first, :] = gi
        spr = jnp.where(row == 0, 0.0, pltpu.roll(sre_ref[last, :], 1, 0))
        spi = jnp.where(row == 0, 0.0, pltpu.roll(sim_ref[last, :], 1, 0))
        dar_ref[...] = _colsum(ar + gr * spr + gi * spi)
        dai_ref[...] = _colsum(ai + gi * spr - gr * spi)

        for c in range(T // rc):
            rows = pl.ds(c * rc, rc)
            g_r, g_i = gre_ref[rows, :].astype(BF16), gim_ref[rows, :].astype(BF16)
            s_r, s_i = sre_ref[rows, :].astype(BF16), sim_ref[rows, :].astype(BF16)
            ub, dyb = u_ref[rows, :], dy_ref[rows, :]
            duc = _dot_nt(g_r, bre_ref[...]) + _dot_nt(g_i, bim_ref[...])
            parts = (_dot_tn(ub, g_r), _dot_tn(ub, g_i), _dot_tn(s_r, dyb), -_dot_tn(s_i, dyb))
            outs = (dbr_ref, dbi_ref, dcr_ref, dci_ref)
            for o_ref, part in zip(outs, parts):
                if c == 0:
                    o_ref[...] = part
                else:
                    o_ref[...] += part

            @pl.when(cb % 2 == 0)
            def _():
                du_ref[rows, :] = duc

            @pl.when(cb % 2 == 1)
            def _():
                du_ref[rows, :] += duc

    blk = lambda r, c: pl.BlockSpec((None, r, c), lambda cb: (cb, 0, 0))
    return _call(body, name='ssm_bwd', grid=(ncb,),
                 in_specs=[sp['chan'], sp['chan'], sp['state'], sp['state'], sp['b'], sp['b'], sp['c'], sp['c'],
                           sp['lam'], sp['lam']],
                 out_specs=[sp['chan'], blk(CHAN_BLOCK, STATE_BLOCK), blk(CHAN_BLOCK, STATE_BLOCK),
                            blk(STATE_BLOCK, CHAN_BLOCK), blk(STATE_BLOCK, CHAN_BLOCK), blk(1, STATE_BLOCK),
                            blk(1, STATE_BLOCK)],
                 out_shape=[_sds((T, D_SSM)), _sds((ncb, CHAN_BLOCK, STATE_BLOCK)), _sds((ncb, CHAN_BLOCK, STATE_BLOCK)),
                            _sds((ncb, STATE_BLOCK, CHAN_BLOCK)), _sds((ncb, STATE_BLOCK, CHAN_BLOCK)),
                            _sds((ncb, 1, STATE_BLOCK)), _sds((ncb, 1, STATE_BLOCK))],
                 scratch=[pltpu.VMEM((T, STATE_BLOCK), F32), pltpu.VMEM((T, STATE_BLOCK), F32)],
                 sem=('arbitrary',), vmem=VMEM_BIG, ride=ride)(dy_perm, u_perm, s_re, s_im, b_re, b_im, c_re, c_im,
                                                               lam_r, lam_i)


def _ffn_dact(ddn, wd4, hid4, tm):
    T = ddn.shape[0]

    def body(d_ref, w_ref, hid_ref, o_ref):
        dact = _dot_nt(d_ref[...], w_ref[...])
        silu, dsilu = _silu_parts(hid_ref[0].astype(F32))
        o_ref[0] = (dact * hid_ref[1].astype(F32) * dsilu).astype(BF16)
        o_ref[1] = (dact * silu).astype(BF16)

    blk = pl.BlockSpec((2, None, tm, FF_SHARD), lambda i, j: (0, j, i, 0))
    return _call(body, name='ffn_dact', grid=(T // tm, 4),
                 in_specs=[pl.BlockSpec((tm, D_MODEL), lambda i, j: (i, 0)),
                           pl.BlockSpec((None, FF_SHARD, D_MODEL), lambda i, j: (j, 0, 0)), blk],
                 out_specs=blk, out_shape=_sds((2, 4, T, FF_SHARD), BF16),
                 sem=('parallel', 'parallel'))(ddn, wd4, hid4)


def _ffn_dup(dhid8, up8, cw8, tm, ride):
    T = up8.shape[1]
    nb = T // tm
    ha = _halo_after(tm, T, HALO16)

    def body(dh_ref, dha_ref, up_ref, cw_ref, dup_ref, dcw_ref):
        i = pl.program_id(1)

        @pl.when(i == 0)
        def _():
            dcw_ref[...] = jnp.zeros_like(dcw_ref)

        dh = dh_ref[...].astype(F32)
        dup, dh1, dh2 = _conv3_t(dh, jnp.where(i < nb - 1, dha_ref[...].astype(F32), 0.0), cw_ref)
        dup_ref[...] = dup.astype(BF16)
        up = up_ref[...].astype(F32)
        dcw_ref[0:1, :] += _colsum(dh2 * up)
        dcw_ref[1:2, :] += _colsum(dh1 * up)
        dcw_ref[2:3, :] += _colsum(dh * up)

    main = pl.BlockSpec((None, tm, FF_SHARD), lambda j, i: (j, i, 0))
    return _call(body, name='ffn_dup', grid=(N_DEV, nb),
                 in_specs=[main, pl.BlockSpec((None, HALO16, FF_SHARD), lambda j, i: (j, ha(i), 0)), main,
                           pl.BlockSpec((None, 3, FF_SHARD), lambda j, i: (j, 0, 0))],
                 out_specs=[main, pl.BlockSpec((None, 8, FF_SHARD), lambda j, i: (j, 0, 0))],
                 out_shape=[_sds((N_DEV, T, FF_SHARD), BF16), _sds((N_DEV, 8, FF_SHARD))],
                 sem=('parallel', 'arbitrary'), ride=ride)(dhid8, dhid8, up8, cw8)


def _grad_tn(a, b, a_spec, b_spec, groups, m, n, tk, name, ride=None, parts=1):
    T = a.shape[-2]
    nk = T // tk
    mp = m // parts

    def body(a_ref, b_ref, *refs):
        o_refs, acc_ref = refs[:parts], refs[parts]
        k = pl.program_id(1)
        part = _dot_tn(a_ref[...], b_ref[...])

        @pl.when(k == 0)
        def _():
            acc_ref[...] = part

        @pl.when(k > 0)
        def _():
            acc_ref[...] += part

        @pl.when(k == nk - 1)
        def _():
            for p, o_ref in enumerate(o_refs):
                o_ref[...] = acc_ref[p * mp:(p + 1) * mp, :].astype(BF16)

    out_spec = pl.BlockSpec((None, mp, n), lambda g, k: (g, 0, 0))
    res = _call(body, name=name, grid=(groups, nk), in_specs=[a_spec, b_spec], out_specs=[out_spec] * parts,
                out_shape=[_sds((groups, mp, n), BF16)] * parts, scratch=[pltpu.VMEM((m, n), F32)],
                sem=('parallel', 'arbitrary'), vmem=VMEM_BIG, ride=ride)(a, b)
    if parts > 1:
        return res
    return res[0] if ride is None else (res[0][0], res[1])


def _pre_norm_bwd(dz, dz_spec, w_s, xin, dres, sc, g, tm, name, ride, below=None, group=1):
    T = xin.shape[0]
    n = w_s.shape[2]
    steps = N_DEV // group

    def body(dz_ref, w_ref, x_ref, dr_ref, sc_ref, g_ref, *refs):
        if below is None:
            dx_ref, dsh_ref, dsc_ref, dg_ref = refs
            sums = (dsh_ref, dsc_ref, dg_ref)
        else:
            v_ref, gate_ref, g2_ref, dx_ref, dsh_ref, dsc_ref, dg_ref, dv_ref, dgate_ref, dg2_ref = refs
            sums = (dsh_ref, dsc_ref, dg_ref, dgate_ref, dg2_ref)
        i, j = pl.program_id(0), pl.program_id(1)
        piece = (lambda s: dz_ref[s]) if dz.ndim == 3 else (lambda s: dz_ref[:, s * n:(s + 1) * n])
        part = _dot_nt(piece(0), w_ref[0])
        for s in range(1, group):
            part = part + _dot_nt(piece(s), w_ref[s])

        @pl.when(jnp.logical_and(i == 0, j == 0))
        def _():
            for s_ref in sums:
                s_ref[...] = jnp.zeros_like(s_ref)

        @pl.when(j == 0)
        def _():
            dx_ref[...] = part

        @pl.when(j > 0)
        def _():
            dx_ref[...] += part

        @pl.when(j == steps - 1)
        def _():
            dh, xv, gv = dx_ref[...], x_ref[...], g_ref[...]
            r = _rsqrt_mean(xv)
            dsh_ref[...] += _colsum(dh)
            dsc_ref[...] += _colsum(dh * (xv * r * gv))
            dxn = dh * (1.0 + sc_ref[...])
            dg_ref[...] += _colsum(dxn * xv * r)
            dx = dr_ref[...] + _norm_bwd(dxn, xv, r, gv)
            dx_ref[...] = dx
            if below is not None:
                v, g2 = v_ref[...], g2_ref[...]
                rv = _rsqrt_mean(v)
                dgate_ref[...] += _colsum(dx * (v * rv * g2))
                dn = dx * gate_ref[...]
                dg2_ref[...] += _colsum(dn * v * rv)
                dv_ref[...] = _norm_bwd(dn, v, rv, g2).astype(BF16)

    row = pl.BlockSpec((tm, D_MODEL), lambda i, j: (i, 0))
    vec = _const((1, D_MODEL))
    in_specs = [dz_spec, pl.BlockSpec((group, D_MODEL, n), lambda i, j: (j, 0, 0)), row, row, vec, vec]
    out_specs = [row, vec, vec, vec]
    out_shape = [_sds((T, D_MODEL)), _sds((1, D_MODEL)), _sds((1, D_MODEL)), _sds((1, D_MODEL))]
    args = [dz, w_s, xin, dres, sc, g]
    if below is not None:
        in_specs += [row, vec, vec]
        out_specs += [row, vec, vec]
        out_shape += [_sds((T, D_MODEL), BF16), _sds((1, D_MODEL)), _sds((1, D_MODEL))]
        args += list(below)
    return _call(body, name=name, grid=(T // tm, steps), in_specs=in_specs, out_specs=out_specs,
                 out_shape=out_shape, sem=('arbitrary', 'arbitrary'), vmem=VMEM_MOST, ride=ride)(*args)


def _mix_bwd(d_o, w_out, yssm, proj, d, glu_w, glu_b, g_ssm, cw, g_conv, avg16, avg64, tm):
    T = yssm.shape[0]
    hb = _halo_before(tm)

    def body(do_ref, wo_ref, y_ref, p_ref, ph_ref, d_ref, gw_ref, gb_ref, gs_ref, cw_ref, gc_ref, a16_ref, a64_ref,
             dy_ref, dconv_ref, dbg_ref, z_ref, dlin_ref, acc_ref):
        i = pl.program_id(0)
        dyc = _dot_nt(do_ref[...], wo_ref[...])

        @pl.when(i == 0)
        def _():
            acc_ref[...] = jnp.zeros_like(acc_ref)

        u = p_ref[:, 0:D_SSM]
        y = y_ref[...] + d_ref[...] * u
        z, t = _gelu(y)
        gate = _sigmoid(_dot(z.astype(BF16), gw_ref[...]) + gb_ref[...])
        ya = z * gate
        rs = lax.rsqrt(_dot_split(ya * ya, a16_ref[...], 2) + EPS)
        dna = dyc[:, 0:D_SSM]
        acc_ref[1:2, :] += _colsum(dna * ya * rs)
        dya = _head_norm_bwd(dna, ya, rs, gs_ref[...], a16_ref[...])
        dlin = dya * z * gate * (1.0 - gate)
        acc_ref[0:1, :] += _colsum(dlin)
        dlin_b = dlin.astype(BF16)
        dz = dya * gate + _dot_nt(dlin_b, gw_ref[...])
        dy = dz * _gelu_grad(y, t)
        acc_ref[3:4, :] += _colsum(dy * u)
        dy_ref[...] = dy
        z_ref[...] = z.astype(BF16)
        dlin_ref[...] = dlin_b

        bg = p_ref[:, D_SSM:D_SSM + D_CONV]
        cv = p_ref[:, D_SSM + D_CONV:D_SSM + 2 * D_CONV] * p_ref[:, D_SSM + 2 * D_CONV:D_IN_PROJ]
        hv = ph_ref[:, D_SSM + D_CONV:D_SSM + 2 * D_CONV] * ph_ref[:, D_SSM + 2 * D_CONV:D_IN_PROJ]
        hv = jnp.where(i > 0, hv, 0.0)
        conv, cv1, cv2 = _conv3(cv, hv, cw_ref)
        yb = bg * conv
        rsb = lax.rsqrt(_dot_split(yb * yb, a64_ref[...], 2) + EPS)
        dnb = dyc[:, D_SSM:D_MODEL]
        acc_ref[2:3, :] += _colsum(dnb * yb * rsb)
        dyb = _head_norm_bwd(dnb, yb, rsb, gc_ref[...], a64_ref[...])
        dbg_ref[...] = dyb * conv
        dconv = dyb * bg
        dconv_ref[...] = dconv
        acc_ref[4:5, :] += _colsum(dconv * cv2)
        acc_ref[5:6, :] += _colsum(dconv * cv1)
        acc_ref[6:7, :] += _colsum(dconv * cv)

    vec = _const((1, D_SSM))
    sq = _const((D_SSM, D_SSM))
    half = pl.BlockSpec((tm, D_SSM), lambda i: (i, 0))
    return _call(body, name='mix_bwd', grid=(T // tm,),
                 in_specs=[pl.BlockSpec((tm, D_MODEL), lambda i: (i, 0)), _const((D_MODEL, D_MODEL)), half,
                           pl.BlockSpec((tm, D_IN_PROJ), lambda i: (i, 0)),
                           pl.BlockSpec((HALO, D_IN_PROJ), lambda i: (hb(i), 0)), vec, sq, vec, vec,
                           _const((3, D_CONV)), vec, sq, sq],
                 out_specs=[half, half, half, half, half, _const((8, D_SSM))],
                 out_shape=[_sds((T, D_SSM)), _sds((T, D_SSM)), _sds((T, D_SSM)), _sds((T, D_SSM), BF16),
                            _sds((T, D_SSM), BF16), _sds((8, D_SSM))],
                 sem=('arbitrary',), vmem=VMEM_BIG)(d_o, w_out, yssm, proj, proj, d, glu_w, glu_b, g_ssm, cw, g_conv,
                                                   avg16, avg64)


def _mix_bwd_proj(dconv, proj, du_ssm, dy, d, dbg, cw, tm):
    T = dy.shape[0]
    nb = T // tm
    ha = _halo_after(tm, T)

    def body(dc_ref, dch_ref, cg_ref, v_ref, du_ref, dy_ref, d_ref, dbg_ref, cw_ref, o_ref):
        i = pl.program_id(0)
        dcv = _conv3_t(dc_ref[...], jnp.where(i < nb - 1, dch_ref[...], 0.0), cw_ref)[0]
        o_ref[:, 0:D_SSM] = (du_ref[...] + dy_ref[...] * d_ref[...]).astype(BF16)
        o_ref[:, D_SSM:D_SSM + D_CONV] = dbg_ref[...].astype(BF16)
        o_ref[:, D_SSM + D_CONV:D_SSM + 2 * D_CONV] = (dcv * v_ref[...]).astype(BF16)
        o_ref[:, D_SSM + 2 * D_CONV:D_IN_PROJ] = (dcv * cg_ref[...]).astype(BF16)

    half = pl.BlockSpec((tm, D_SSM), lambda i: (i, 0))
    return _call(body, name='mix_bwd_proj', grid=(nb,),
                 in_specs=[half, pl.BlockSpec((HALO, D_CONV), lambda i: (ha(i), 0)),
                           pl.BlockSpec((tm, D_CONV), lambda i: (i, 2)), pl.BlockSpec((tm, D_CONV), lambda i: (i, 3)),
                           half, half, _const((1, D_SSM)), half, _const((3, D_CONV))],
                 out_specs=pl.BlockSpec((tm, D_IN_PROJ), lambda i: (i, 0)), out_shape=_sds((T, D_IN_PROJ), BF16),
                 sem=('parallel',))(dconv, dconv, proj, proj, du_ssm, dy, d, dbg, cw)


def _row_tile(rows, cols, slots):
    for cand in (512, 256, 128, 64, 32, 16, 8):
        if rows % cand == 0 and slots * cand * cols * 4 <= (2 << 20):
            return cand
    return rows


def _adamw_math(g, w, m, v):
    m2 = ADAM_B1 * m + (1.0 - ADAM_B1) * g
    v2 = ADAM_B2 * v + (1.0 - ADAM_B2) * (g * g)
    m_hat = m2 / (1.0 - ADAM_B1 ** ADAM_STEP)
    v_hat = v2 / (1.0 - ADAM_B2 ** ADAM_STEP)
    return -ADAM_LR * (m_hat / (jnp.sqrt(v_hat) + ADAM_EPS) + ADAM_WD * w), m2, v2


def _adamw(pieces, w, m, v, name):
    slots, _, cols = pieces[0].shape
    rows = sum(p.shape[1] for p in pieces)
    tr = _row_tile(pieces[0].shape[1], cols, slots)
    starts, pos = [], 0
    for p in pieces:
        assert p.shape[1] % tr == 0
        starts.append(pos)
        pos += p.shape[1] // tr

    def body(*refs):
        g_refs = refs[:len(pieces)]
        w_ref, m_ref, v_ref, go_ref, d_ref, mo_ref, vo_ref = refs[len(pieces):]
        i = pl.program_id(0)
        g = None
        for g_ref, start in zip(g_refs, starts):
            part = g_ref[0].astype(F32)
            for s in range(1, slots):
                part = part + g_ref[s].astype(F32)
            g = part if g is None else jnp.where(i >= start, part, g)
        go_ref[...] = g
        d_ref[...], mo_ref[...], vo_ref[...] = _adamw_math(g, w_ref[...], m_ref[...], v_ref[...])

    def piece_spec(start, count):
        return pl.BlockSpec((slots, tr, cols), lambda i: (0, jnp.clip(i - start, 0, count - 1), 0))

    blk = pl.BlockSpec((tr, cols), lambda i: (i, 0))
    return _call(body, name=name, grid=(rows // tr,),
                 in_specs=[piece_spec(s, p.shape[1] // tr) for s, p in zip(starts, pieces)] + [blk, blk, blk],
                 out_specs=[blk] * 4, out_shape=[_sds((rows, cols))] * 4, sem=('parallel',))(*pieces, w, m, v)


def _to_scan_rows(a):
    T, n = a.shape
    return a.reshape(SUBLANES, T // SUBLANES, n).transpose(1, 0, 2).reshape(T, n)


def _from_scan_rows(a):
    T, n = a.shape
    return a.reshape(T // SUBLANES, SUBLANES, n).transpose(1, 0, 2).reshape(T, n)


def _expand(a):
    return jnp.repeat(a, SSM_GROUP, axis=1)


def _block_diag_b(bb):
    eye = jnp.eye(N_GROUPS, dtype=bb.dtype)
    return (bb.transpose(0, 2, 1)[:, :, None, :] * eye[:, None, :, None]).reshape(D_SSM, N_STATE)


def _block_diag_c(cc):
    eye = jnp.eye(N_GROUPS, dtype=cc.dtype)
    return (cc.transpose(0, 2, 1)[:, :, None, :] * eye[:, None, :, None]).reshape(N_STATE, D_SSM)


def _diag_blocks(x, chan_major):
    e2 = jnp.eye(2, dtype=x.dtype)
    e4 = jnp.eye(4, dtype=x.dtype)
    if chan_major:
        x = x.reshape(4, 2, 2, 4, SSM_GROUP, 4, SSM_STATE)
        x = x * e2[None, :, :, None, None, None, None] * e4[None, None, None, :, None, :, None]
        return x.sum(axis=(2, 3)).transpose(0, 1, 3, 4, 2).reshape(N_GROUPS, SSM_STATE, SSM_GROUP)
    x = x.reshape(4, 2, 4, SSM_STATE, 2, 4, SSM_GROUP)
    x = x * e2[None, :, None, None, :, None, None] * e4[None, None, :, None, None, :, None]
    return x.sum(axis=(4, 5)).reshape(N_GROUPS, SSM_STATE, SSM_GROUP)


SMALL_LAYOUT = {
    'ssm_b_re': (0, 0, 32, 1024), 'ssm_b_im': (32, 0, 32, 1024), 'ssm_c_re': (64, 0, 32, 1024),
    'ssm_c_im': (96, 0, 32, 1024), 'b_ada': (128, 0, 6, 1024), 'g_pre_mix': (134, 0, 1, 1024),
    'g_post_mix': (135, 0, 1, 1024), 'ssm_lam_re': (136, 0, 2, 1024), 'ssm_lam_im': (138, 0, 2, 1024),
    'ssm_log_step': (140, 0, 1, 32), 'glu_b': (141, 0, 1, 512), 'g_out_ssm': (141, 512, 1, 512),
    'g_out_conv': (142, 0, 1, 512), 'ssm_d': (142, 512, 1, 512), 'g_pre_ffn': (143, 0, 1, 1024),
    'g_post_ffn': (144, 0, 1, 1024)}
SMALL_ROWS = 152
B_ADA_ROW = SMALL_LAYOUT['b_ada'][0]
LATE_ROWS = {('b_ada', 0): 0, ('b_ada', 1): 1, ('g_pre_mix', 0): 2}


def _adamw_small(gathered, late, wts, mom_m, mom_v):
    names = list(SMALL_LAYOUT)
    n = len(names)

    def body(*refs):
        g_ref, late_ref, ins, outs = refs[0], refs[1], refs[2:2 + 3 * n], refs[2 + 3 * n:]
        for p, name in enumerate(names):
            r0, c0, rows, cols = SMALL_LAYOUT[name]
            pieces = [(0, rows)] if rows % 8 == 0 else [(r, 1) for r in range(rows)]
            for r, cnt in pieces:
                src_ref, first = (late_ref, LATE_ROWS[name, r]) if (name, r) in LATE_ROWS else (g_ref, r0 + r)
                g = src_ref[0, first:first + cnt, c0:c0 + cols]
                for s in range(1, N_DEV):
                    g = g + src_ref[s, first:first + cnt, c0:c0 + cols]
                w, m, v = (ins[3 * p + q][r:r + cnt, :] for q in range(3))
                res = (g,) + _adamw_math(g, w, m, v)
                for q in range(4):
                    outs[4 * p + q][r:r + cnt, :] = res[q]

    shapes = [SMALL_LAYOUT[name][2:] for name in names]
    args = [gathered, late]
    for name, shp in zip(names, shapes):
        args += [wts[name].reshape(shp), mom_m[name].reshape(shp), mom_v[name].reshape(shp)]
    outs = _call(body, name='adamw_small', grid=(1,),
                 in_specs=[_const(gathered.shape), _const(late.shape)]
                 + [_const(shp) for shp in shapes for _ in range(3)],
                 out_specs=[_const(shp) for shp in shapes for _ in range(4)],
                 out_shape=[_sds(shp) for shp in shapes for _ in range(4)], vmem=VMEM_BIG)(*args)
    res = {}
    for p, name in enumerate(names):
        for q, kind in enumerate(('g', 'd', 'm', 'v')):
            res[kind, name] = outs[4 * p + q].reshape(wts[name].shape)
    return res


def kernel(x, c, w_ada, b_ada, g_pre_mix, g_post_mix, w_in, ssm_lam_re, ssm_lam_im, ssm_log_step, ssm_b_re, ssm_b_im, ssm_c_re, ssm_c_im, ssm_d, glu_w, glu_b, g_out_ssm, conv_w, g_out_conv, w_out, g_pre_ffn, g_post_ffn, w_up, ffn_conv_w, w_down, loss_target, m_w_ada, m_b_ada, m_g_pre_mix, m_g_post_mix, m_w_in, m_ssm_lam_re, m_ssm_lam_im, m_ssm_log_step, m_ssm_b_re, m_ssm_b_im, m_ssm_c_re, m_ssm_c_im, m_ssm_d, m_glu_w, m_glu_b, m_g_out_ssm, m_conv_w, m_g_out_conv, m_w_out, m_g_pre_ffn, m_g_post_ffn, m_w_up, m_ffn_conv_w, m_w_down, v_w_ada, v_b_ada, v_g_pre_mix, v_g_post_mix, v_w_in, v_ssm_lam_re, v_ssm_lam_im, v_ssm_log_step, v_ssm_b_re, v_ssm_b_im, v_ssm_c_re, v_ssm_c_im, v_ssm_d, v_glu_w, v_glu_b, v_g_out_ssm, v_conv_w, v_g_out_conv, v_w_out, v_g_pre_ffn, v_g_post_ffn, v_w_up, v_ffn_conv_w, v_w_down):
    args = dict(locals())
    wts = {n: args[n] for n in WEIGHTS}
    mom_m = {n: args['m_' + n] for n in WEIGHTS}
    mom_v = {n: args['v_' + n] for n in WEIGHTS}
    T = x.shape[1]
    tm = min(512, T)
    tw = min(1024, T)
    me = _me()[3]
    xt, tgt = x[0], loss_target[0]

    (c_all,) = _exchange([c], name='gather_c', scatter=False)
    c_all = c_all.reshape(N_DEV, D_MODEL)
    b_cols = lax.dynamic_slice(b_ada, (0, me * ADA_SHARD), (1, ADA_SHARD))
    mod_cols, c_act = _mod_cols(c_all, w_ada[0], b_cols)
    (mod_all,) = _exchange([mod_cols], name='gather_mod', scatter=False)
    mod = lax.dynamic_slice(mod_all, (0, me, 0), (N_DEV, 1, ADA_SHARD)).reshape(N_MOD, 1, D_MODEL)
    sh1, sc1, gt1, sh2, sc2, gt2 = [mod[k] for k in range(N_MOD)]

    w_in_s, glu_s, w_out_s, conv_s = _exchange(
        [w_in[0].astype(BF16), glu_w[0].astype(BF16), w_out[0].astype(BF16), conv_w[0]], name='gather_weights',
        scatter=False)
    glu_full = glu_s.reshape(D_SSM, D_SSM)
    w_out_full = w_out_s.reshape(D_MODEL, D_MODEL)
    cw_full = conv_s.transpose(1, 0, 2).reshape(3, D_CONV)

    lre_x, lim_x = _expand(ssm_lam_re[0]), _expand(ssm_lam_im[0])
    lst_x = jnp.broadcast_to(ssm_log_step[0][:, None], (N_GROUPS, SSM_STATE * SSM_GROUP))
    b_re_x = ssm_b_re[0].reshape(N_GROUPS, -1)
    b_im_x = ssm_b_im[0].reshape(N_GROUPS, -1)
    ar_x, ai_x, bbr_x, bbi_x = _ssm_prep(lre_x, lim_x, lst_x, b_re_x, b_im_x)
    lam_r = ar_x[:, ::SSM_GROUP].reshape(1, N_STATE)
    lam_i = ai_x[:, ::SSM_GROUP].reshape(1, N_STATE)
    big_b_re = _block_diag_b(bbr_x.reshape(N_GROUPS, SSM_STATE, SSM_GROUP)).astype(BF16)
    big_b_im = _block_diag_b(bbi_x.reshape(N_GROUPS, SSM_STATE, SSM_GROUP)).astype(BF16)
    big_c_re = _block_diag_c(ssm_c_re[0]).astype(BF16)
    big_c_im = _block_diag_c(ssm_c_im[0]).astype(BF16)
    head = jnp.arange(D_SSM)
    avg16 = jnp.where(head[:, None] // SSM_GROUP == head[None, :] // SSM_GROUP, 1.0 / SSM_GROUP, 0.0).astype(BF16)
    hd = D_CONV // CONV_HEADS
    avg64 = jnp.where(head[:, None] // hd == head[None, :] // hd, 1.0 / hd, 0.0).astype(BF16)

    (proj, h1), (w_down_s, ffn_conv_s) = _pre_mix(xt, sc1, sh1, g_pre_mix, w_in_s, tw,
                                                  ([w_down[0].astype(BF16), ffn_conv_w[0]], False))
    wd4 = w_down_s.reshape(4, FF_SHARD, D_MODEL)
    cw4 = ffn_conv_s.reshape(2, 4, 3, FF_SHARD)
    u_perm = _to_scan_rows(proj[:, :D_SSM]).astype(BF16)
    (s_re, s_im, y_perm), (w_up_s,) = _ssm_fwd(u_perm, big_b_re, big_b_im, big_c_re, big_c_im, lam_r, lam_i,
                                               ([w_up[0].astype(BF16)], False))
    yssm = _from_scan_rows(y_perm)
    mix_args = (ssm_d, glu_full, glu_b, g_out_ssm, cw_full, g_out_conv, avg16, avg64)
    ycat = _mix_fwd(yssm, proj, *mix_args, tm)
    o, x1, h2 = _out_proj(ycat, w_out_full, xt, gt1, g_post_mix, g_pre_ffn, sc2, sh2, tm)
    up8 = _ffn_up(h2, w_up_s, tw)
    up4 = up8.reshape(2, 4, T, FF_SHARD)
    act, hid4 = _ffn_act(up4, cw4, tm)
    ddn, dx2, loss_parts, d_gt2, d_g_post_ffn = _ffn_down(act, wd4, x1, tgt, gt2, g_post_ffn, tw)
    loss = lax.psum(jnp.sum(loss_parts[:, 0, 0]), ('x', 'y', 'c'))

    got = {}
    dhid = _ffn_dact(ddn, wd4, hid4, tm)
    g_w_down = _grad_tn(act, ddn, pl.BlockSpec((None, tw, FF_SHARD), lambda g, k: (g, k, 0)),
                        pl.BlockSpec((tw, D_MODEL), lambda g, k: (k, 0)), 4, FF_SHARD, D_MODEL, tw, 'grad_w_down')
    (dup8, dcw_ffn), (got['w_down'],) = _ffn_dup(dhid.reshape(N_DEV, T, FF_SHARD), up8, ffn_conv_s, tm,
                                                 ([g_w_down.reshape(N_DEV, D_FF // N_DEV, D_MODEL)], True))
    g_w_up_halves = _grad_tn(h2, dup8, pl.BlockSpec((tw, D_MODEL), lambda g, k: (k, 0)),
                             pl.BlockSpec((None, tw, FF_SHARD), lambda g, k: (g, k, 0)), N_DEV, D_MODEL, FF_SHARD, tw,
                             'grad_w_up', parts=2)
    (dx1, d_sh2, d_sc2, d_g_pre_ffn, d_o, d_gt1, d_g_post_mix), (got_up_0, got['ffn_conv_w']) = _pre_norm_bwd(
        dup8, pl.BlockSpec((2, tw, FF_SHARD), lambda i, j: (j, i, 0)), w_up_s, x1, dx2, sc2, g_pre_ffn, tw,
        'ffn_in_bwd', ([g_w_up_halves[0], dcw_ffn], True), below=(o, gt1, g_post_mix), group=2)

    g_w_out = _grad_tn(ycat, d_o, pl.BlockSpec((tw, D_MODEL), lambda g, k: (k, 0)),
                       pl.BlockSpec((tw, D_MODEL), lambda g, k: (k, 0)), 1, D_MODEL, D_MODEL, tw, 'grad_w_out')
    dy, dconv, dbg, z_b, dlin_b, sums = _mix_bwd(d_o, w_out_full, yssm, proj, *mix_args, tm)
    g_glu_w = _grad_tn(z_b, dlin_b, pl.BlockSpec((tw, D_SSM), lambda g, k: (k, 0)),
                       pl.BlockSpec((tw, D_SSM), lambda g, k: (k, 0)), 1, D_SSM, D_SSM, tw, 'grad_glu_w')
    dy_perm = _to_scan_rows(dy).astype(BF16)
    (du_perm, dbr_blk, dbi_blk, dcr_blk, dci_blk, dar_blk, dai_blk), (got_up_1, got['w_out'], got['glu_w']) = _ssm_bwd(
        dy_perm, u_perm, s_re, s_im, big_b_re, big_b_im, big_c_re, big_c_im, lam_r, lam_i,
        ([g_w_up_halves[1], g_w_out.reshape(N_DEV, D_MODEL // N_DEV, D_MODEL),
          g_glu_w.reshape(N_DEV, D_SSM // N_DEV, D_SSM)], True))
    du_ssm = _from_scan_rows(du_perm)
    dproj = _mix_bwd_proj(dconv, proj, du_ssm, dy, ssm_d, dbg, cw_full, tm)
    dbb_re = _diag_blocks(dbr_blk, True).reshape(N_GROUPS, -1)
    dbb_im = _diag_blocks(dbi_blk, True).reshape(N_GROUPS, -1)
    d_c_re = _diag_blocks(dcr_blk, False).transpose(0, 2, 1)
    d_c_im = _diag_blocks(dci_blk, False).transpose(0, 2, 1)
    lane = jnp.arange(SSM_STATE * SSM_GROUP)
    seg = jnp.where(lane[:, None] // SSM_GROUP == lane[None, :] // SSM_GROUP, 1.0, 0.0).astype(BF16)
    d_b_re_x, d_b_im_x, d_lre_x, d_lim_x, d_lst = _ssm_prep_bwd(
        lre_x, lim_x, lst_x, b_re_x, b_im_x, dbb_re, dbb_im, _expand(dar_blk.reshape(N_GROUPS, SSM_STATE)),
        _expand(dai_blk.reshape(N_GROUPS, SSM_STATE)), seg)

    row = lambda a: a.reshape(-1, PACK_COLS)
    blank = jnp.zeros((1, PACK_COLS), F32)
    small_pack = jnp.concatenate([
        d_b_re_x, d_b_im_x, row(d_c_re), row(d_c_im), blank, blank, d_gt1, d_sh2, d_sc2, d_gt2, blank,
        d_g_post_mix, row(d_lre_x[:, ::SSM_GROUP]), row(d_lim_x[:, ::SSM_GROUP]),
        jnp.pad(d_lst.reshape(1, N_GROUPS), ((0, 0), (0, PACK_COLS - N_GROUPS))), row(sums[0:4]), d_g_pre_ffn,
        d_g_post_ffn, jnp.zeros((SMALL_ROWS - 145, PACK_COLS), F32)])
    g_w_in, (small_all,) = _grad_tn(
        h1, dproj, pl.BlockSpec((tw, D_MODEL), lambda g, k: (k, 0)), pl.BlockSpec((tw, IN_SHARD), lambda g, k: (k, g)),
        N_DEV, D_MODEL, IN_SHARD, tw, 'grad_w_in', ride=([small_pack], False))
    g_conv_slots = jnp.concatenate([sums[4:7], jnp.zeros((5, D_CONV), F32)]).reshape(
        8, N_DEV, D_CONV // N_DEV).transpose(1, 0, 2)
    (grad_x, d_sh1, d_sc1, d_g_pre_mix), (got['w_in'], got['conv_w']) = _pre_norm_bwd(
        dproj, pl.BlockSpec((tw, 4 * IN_SHARD), lambda i, j: (i, j)), w_in_s, xt, dx1, sc1, g_pre_mix, tw,
        'mix_in_bwd', ([g_w_in, g_conv_slots], True), group=4)
    late_pack = jnp.concatenate([d_sh1, d_sc1, d_g_pre_mix, jnp.zeros((5, PACK_COLS), F32)])
    (late_all,) = _exchange([late_pack], name='gather_late_grads', scatter=False)
    res = _adamw_small(small_all, late_all, wts, mom_m, mom_v)

    dmod_all = jnp.concatenate([late_all[:, 0:2, :], small_all[:, B_ADA_ROW + 2:B_ADA_ROW + N_MOD, :]],
                               axis=1).reshape(N_DEV, N_MOD * D_MODEL)
    dmod_cols = lax.dynamic_slice(dmod_all, (0, me * ADA_SHARD), (N_DEV, ADA_SHARD))
    g_w_ada = _grad_w_ada(c_act.T, dmod_cols)

    pieces = {n: [slots[:, :3, :] if n in ('conv_w', 'ffn_conv_w') else slots] for n, slots in got.items()}
    pieces['w_up'] = [got_up_0, got_up_1]
    for n, parts in pieces.items():
        outs = _adamw(parts, wts[n][0], mom_m[n][0], mom_v[n][0], 'adamw_' + n)
        for kind, val in zip(('g', 'd', 'm', 'v'), outs):
            res[kind, n] = val[None]
    outs = _adamw([g_w_ada[None]], w_ada[0], m_w_ada[0], v_w_ada[0], 'adamw_w_ada')
    for kind, val in zip(('g', 'd', 'm', 'v'), outs):
        res[kind, 'w_ada'] = val[None]

    return (loss, grad_x[None], *[res['g', n] for n in WEIGHTS], *[res['d', n] for n in WEIGHTS],
            *[res['m', n] for n in WEIGHTS], *[res['v', n] for n in WEIGHTS])
```

```python
import math

import jax
import jax.numpy as jnp
from jax import lax
from jax.experimental import pallas as pl
from jax.experimental.pallas import tpu as pltpu

F32, BF16 = jnp.float32, jnp.bfloat16

D_MODEL = 1024
D_SSM = 512
D_CONV = 512
SSM_GROUP = 16
N_GROUPS = 32
SSM_STATE = 64
N_STATE = N_GROUPS * SSM_STATE
CONV_HEADS = 8
D_FF = 2816
N_MOD = 6
D_IN_PROJ = D_SSM + 3 * D_CONV
N_DEV = 8
FF_SHARD = 2 * D_FF // N_DEV
IN_SHARD = D_IN_PROJ // N_DEV
ADA_SHARD = N_MOD * D_MODEL // N_DEV
EPS = 1e-6
LAMBDA_RE_MAX = -1e-4
ADAM_LR, ADAM_B1, ADAM_B2, ADAM_EPS, ADAM_WD, ADAM_STEP = 0.001, 0.9, 0.999, 1e-08, 0.01, 10
GELU_C = math.sqrt(2.0 / math.pi)
GELU_A = 0.044715

SUBLANES = 8
HALO = 8
HALO16 = 16
SCAN_UNROLL = 8
STATE_BLOCK = 256
CHAN_BLOCK = 128
VMEM_BIG = 48 << 20
VMEM_MOST = 58 << 20

WEIGHTS = ['w_ada', 'b_ada', 'g_pre_mix', 'g_post_mix', 'w_in', 'ssm_lam_re', 'ssm_lam_im', 'ssm_log_step',
           'ssm_b_re', 'ssm_b_im', 'ssm_c_re', 'ssm_c_im', 'ssm_d', 'glu_w', 'glu_b', 'g_out_ssm', 'conv_w',
           'g_out_conv', 'w_out', 'g_pre_ffn', 'g_post_ffn', 'w_up', 'ffn_conv_w', 'w_down']
SHARDED = ('w_ada', 'w_in', 'glu_w', 'conv_w', 'w_out', 'w_up', 'ffn_conv_w', 'w_down')
PACK_COLS = 1024


def _call(body, *, name, grid, in_specs, out_specs, out_shape, scratch=(), sem=None, vmem=None, ride=None):
    params = {}
    if vmem is not None:
        params['vmem_limit_bytes'] = vmem
    if ride is None:
        if sem is not None:
            params['dimension_semantics'] = sem
        return pl.pallas_call(body, name=name, grid=grid, in_specs=in_specs, out_specs=out_specs,
                              out_shape=out_shape, scratch_shapes=list(scratch),
                              compiler_params=pltpu.CompilerParams(**params))
    arrs, scatter = ride
    single = not isinstance(out_shape, (list, tuple))
    out_shape_l = [out_shape] if single else list(out_shape)
    out_specs_l = [out_specs] if single else list(out_specs)
    n, n_in, n_out, n_scr = len(arrs), len(in_specs), len(out_shape_l), len(scratch)
    any_spec = pl.BlockSpec(memory_space=pl.ANY)
    params['dimension_semantics'] = ('arbitrary',) * len(grid)

    def carried(*refs):
        ins, rin = refs[:n_in], refs[n_in:n_in + n]
        outs, rout = refs[n_in + n:n_in + n + n_out], refs[n_in + n + n_out:n_in + 2 * n + n_out]
        scr, sems = refs[n_in + 2 * n + n_out:n_in + 2 * n + n_out + n_scr], refs[n_in + 2 * n + n_out + n_scr:]
        first = pl.program_id(0) == 0
        last = pl.program_id(0) == grid[0] - 1
        for ax in range(1, len(grid)):
            first = jnp.logical_and(first, pl.program_id(ax) == 0)
            last = jnp.logical_and(last, pl.program_id(ax) == grid[ax] - 1)

        @pl.when(first)
        def _():
            _exchange_start(rin, rout, sems, scatter)

        body(*ins, *outs, *scr)

        @pl.when(last)
        def _():
            _exchange_wait(rin, rout, sems, scatter)

    call = pl.pallas_call(carried, name=name, grid=grid, in_specs=list(in_specs) + [any_spec] * n,
                          out_specs=out_specs_l + [any_spec] * n,
                          out_shape=out_shape_l + _exchange_shapes(arrs, scatter),
                          scratch_shapes=list(scratch) + _exchange_sems(n),
                          compiler_params=pltpu.CompilerParams(**params))

    def run(*args):
        res = call(*args, *arrs)
        own = res[0] if single else list(res[:n_out])
        return own, list(res[n_out:])

    return run


def _const(shape):
    nd = len(shape)
    return pl.BlockSpec(shape, lambda *_: (0,) * nd)


def _sds(shape, dtype=F32):
    return jax.ShapeDtypeStruct(shape, dtype)


def _dot(a, b):
    return jnp.dot(a, b, preferred_element_type=F32)


def _dot_nt(a, b):
    return lax.dot_general(a, b, (((1,), (1,)), ((), ())), preferred_element_type=F32)


def _dot_tn(a, b):
    return lax.dot_general(a, b, (((0,), (0,)), ((), ())), preferred_element_type=F32)


def _dot_split(x, mat, parts):
    acc = None
    rem = x
    for _ in range(parts):
        piece = rem.astype(BF16)
        rem = rem - piece.astype(F32)
        term = _dot(piece, mat)
        acc = term if acc is None else acc + term
    return acc


def _sigmoid(x):
    return 1.0 / (1.0 + jnp.exp(-x))


def _gelu(x):
    t = jnp.tanh(GELU_C * (x + GELU_A * x * x * x))
    return 0.5 * x * (1.0 + t), t


def _gelu_grad(x, t):
    return 0.5 * (1.0 + t) + 0.5 * x * (1.0 - t * t) * GELU_C * (1.0 + 3.0 * GELU_A * x * x)


def _rsqrt_mean(x):
    return lax.rsqrt(jnp.mean(x * x, axis=-1, keepdims=True) + EPS)


def _colsum(x):
    return jnp.sum(x, axis=0, keepdims=True)


def _shifts_down(x, halo):
    ext = jnp.concatenate([halo, x], axis=0)
    return pltpu.roll(ext, 1, 0)[halo.shape[0]:], pltpu.roll(ext, 2, 0)[halo.shape[0]:]


def _shifts_up(x, halo):
    n = x.shape[0]
    ext = jnp.concatenate([x, halo], axis=0)
    total = ext.shape[0]
    return pltpu.roll(ext, total - 1, 0)[:n], pltpu.roll(ext, total - 2, 0)[:n]


def _conv3(x, halo, w_ref):
    x1, x2 = _shifts_down(x, halo)
    return w_ref[0:1, :] * x2 + w_ref[1:2, :] * x1 + w_ref[2:3, :] * x, x1, x2


def _conv3_t(g, halo, w_ref):
    g1, g2 = _shifts_up(g, halo)
    return w_ref[2:3, :] * g + w_ref[1:2, :] * g1 + w_ref[0:1, :] * g2, g1, g2


def _silu_parts(x):
    s = _sigmoid(x)
    return x * s, s * (1.0 + x * (1.0 - s))


def _norm_bwd(dn, x, r, g):
    gd = g * dn
    return r * gd - x * (r * r * r) * jnp.mean(gd * x, axis=-1, keepdims=True)


def _head_norm_bwd(dn, y, rs, g, avg):
    gd = g * dn
    return rs * gd - y * (rs * rs * rs) * _dot_split(gd * y, avg, 2)


def _me():
    x, y, c = lax.axis_index('x'), lax.axis_index('y'), lax.axis_index('c')
    return x, y, c, 4 * x + 2 * y + c


def _peer(k):
    x, y, c, _ = _me()
    px = 1 - x if k & 4 else x
    py = 1 - y if k & 2 else y
    pc = 1 - c if k & 1 else c
    return (px, py, pc), 4 * px + 2 * py + pc


SIBLING = 1
OTHER_CHIPS = (2, 4, 6)


def _remote(src, dst, sems, a, k, dev):
    return pltpu.make_async_remote_copy(src_ref=src, dst_ref=dst, send_sem=sems[0].at[a, k - 1],
                                        recv_sem=sems[1].at[a, k - 1], device_id=dev,
                                        device_id_type=pl.DeviceIdType.MESH)


def _exchange_copies(ins, outs, sems, scatter):
    me = _me()[3]
    local, first, relay, arrivals = [], [], [], []
    for a in range(len(ins)):
        src = ins[a].at[me] if scatter else ins[a]
        local.append(pltpu.make_async_copy(src, outs[a].at[me], sems[2].at[a]))
        for k in range(1, N_DEV):
            dev, idx = _peer(k)
            landed = _remote(src, outs[a].at[idx], sems, a, k, dev)
            if scatter:
                first.append(_remote(ins[a].at[idx], outs[a].at[me], sems, a, k, dev))
                arrivals.append(landed)
            elif k == SIBLING:
                first.append(_remote(src, outs[a].at[me], sems, a, k, dev))
                arrivals.append(landed)
            elif k in OTHER_CHIPS:
                first.append(_remote(src, outs[a].at[me], sems, a, k, dev))
                sib, _ = _peer(SIBLING)
                relay.append((landed, _remote(outs[a].at[idx], outs[a].at[idx], sems, a, k | SIBLING, sib)))
            else:
                arrivals.append(landed)
    return local, first, relay, arrivals


def _exchange_start(ins, outs, sems, scatter):
    local, first, _, _ = _exchange_copies(ins, outs, sems, scatter)
    for cp in local + first:
        cp.start()


def _exchange_wait(ins, outs, sems, scatter):
    local, first, relay, arrivals = _exchange_copies(ins, outs, sems, scatter)
    for landed, forward in relay:
        landed.wait_recv()
        forward.start()
    for cp in arrivals:
        cp.wait_recv()
    for cp in first + [forward for _, forward in relay]:
        cp.wait_send()
    for cp in local:
        cp.wait()


def _exchange_shapes(arrs, scatter):
    return [_sds(a.shape if scatter else (N_DEV,) + a.shape, a.dtype) for a in arrs]


def _exchange_sems(n):
    return [pltpu.SemaphoreType.DMA((n, N_DEV - 1)), pltpu.SemaphoreType.DMA((n, N_DEV - 1)),
            pltpu.SemaphoreType.DMA((n,))]


def _exchange(arrs, *, name, scatter):
    n = len(arrs)

    def body(*refs):
        _exchange_start(refs[:n], refs[n:2 * n], refs[2 * n:], scatter)
        _exchange_wait(refs[:n], refs[n:2 * n], refs[2 * n:], scatter)

    any_spec = pl.BlockSpec(memory_space=pl.ANY)
    outs = pl.pallas_call(body, name=name, out_shape=_exchange_shapes(arrs, scatter), in_specs=[any_spec] * n,
                          out_specs=[any_spec] * n, scratch_shapes=_exchange_sems(n))(*arrs)
    return list(outs)


def _mod_cols(c_all, w_ada, b_cols):
    def body(c_ref, w_ref, b_ref, mod_ref, act_ref):
        c = c_ref[...]
        act = c * _sigmoid(c)
        act_ref[...] = act
        mod_ref[...] = _dot(act.astype(BF16), w_ref[...].astype(BF16)) + b_ref[...]

    return _call(body, name='mod_cols', grid=(1,),
                 in_specs=[_const(c_all.shape), _const(w_ada.shape), _const(b_cols.shape)],
                 out_specs=[_const((N_DEV, ADA_SHARD)), _const(c_all.shape)],
                 out_shape=[_sds((N_DEV, ADA_SHARD)), _sds(c_all.shape)], vmem=VMEM_BIG)(c_all, w_ada, b_cols)


def _grad_w_ada(act_t, dmod_cols):
    def body(a_ref, d_ref, o_ref):
        o_ref[...] = _dot(a_ref[...], d_ref[...])

    return _call(body, name='grad_w_ada', grid=(1,), in_specs=[_const(act_t.shape), _const(dmod_cols.shape)],
                 out_specs=_const((D_MODEL, ADA_SHARD)), out_shape=_sds((D_MODEL, ADA_SHARD)),
                 vmem=VMEM_BIG)(act_t, dmod_cols)


def _pre_mix(x, sc, sh, g, w_s, tm, ride):
    T = x.shape[0]

    def body(x_ref, sc_ref, sh_ref, g_ref, w_ref, proj_ref, h_ref):
        @pl.when(pl.program_id(1) == 0)
        def _():
            xv = x_ref[...]
            h_ref[...] = ((xv * _rsqrt_mean(xv) * g_ref[...]) * (1.0 + sc_ref[...]) + sh_ref[...]).astype(BF16)

        for s in range(2):
            proj_ref[:, s * IN_SHARD:(s + 1) * IN_SHARD] = _dot(h_ref[...], w_ref[s])

    row = pl.BlockSpec((tm, D_MODEL), lambda i, j: (i, 0))
    vec = _const((1, D_MODEL))
    return _call(body, name='pre_mix', grid=(T // tm, N_DEV // 2),
                 in_specs=[row, vec, vec, vec, pl.BlockSpec((2, D_MODEL, IN_SHARD), lambda i, j: (j, 0, 0))],
                 out_specs=[pl.BlockSpec((tm, 2 * IN_SHARD), lambda i, j: (i, j)), row],
                 out_shape=[_sds((T, D_IN_PROJ)), _sds((T, D_MODEL), BF16)],
                 sem=('parallel', 'arbitrary'), ride=ride)(x, sc, sh, g, w_s)


def _halo_before(tm, rows=HALO):
    return lambda i: jnp.maximum(i * (tm // rows) - 1, 0)


def _halo_after(tm, T, rows=HALO):
    return lambda i: jnp.minimum((i + 1) * (tm // rows), T // rows - 1)


def _mix_fwd(yssm, proj, d, glu_w, glu_b, g_ssm, cw, g_conv, avg16, avg64, tm):
    T = yssm.shape[0]
    hb = _halo_before(tm)

    def body(y_ref, p_ref, ph_ref, d_ref, gw_ref, gb_ref, gs_ref, cw_ref, gc_ref, a16_ref, a64_ref, o_ref):
        i = pl.program_id(0)
        u = p_ref[:, 0:D_SSM]
        y = y_ref[...] + d_ref[...] * u
        z, _ = _gelu(y)
        gate = _sigmoid(_dot(z.astype(BF16), gw_ref[...]) + gb_ref[...])
        ya = z * gate
        rs = lax.rsqrt(_dot_split(ya * ya, a16_ref[...], 2) + EPS)
        o_ref[:, 0:D_SSM] = (ya * rs * gs_ref[...]).astype(BF16)
        bg = p_ref[:, D_SSM:D_SSM + D_CONV]
        cv = p_ref[:, D_SSM + D_CONV:D_SSM + 2 * D_CONV] * p_ref[:, D_SSM + 2 * D_CONV:D_IN_PROJ]
        hv = ph_ref[:, D_SSM + D_CONV:D_SSM + 2 * D_CONV] * ph_ref[:, D_SSM + 2 * D_CONV:D_IN_PROJ]
        hv = jnp.where(i > 0, hv, 0.0)
        conv, _, _ = _conv3(cv, hv, cw_ref)
        yb = bg * conv
        rsb = lax.rsqrt(_dot_split(yb * yb, a64_ref[...], 2) + EPS)
        o_ref[:, D_SSM:D_MODEL] = (yb * rsb * gc_ref[...]).astype(BF16)

    vec = _const((1, D_SSM))
    sq = _const((D_SSM, D_SSM))
    return _call(body, name='mix_fwd', grid=(T // tm,),
                 in_specs=[pl.BlockSpec((tm, D_SSM), lambda i: (i, 0)), pl.BlockSpec((tm, D_IN_PROJ), lambda i: (i, 0)),
                           pl.BlockSpec((HALO, D_IN_PROJ), lambda i: (hb(i), 0)), vec, sq, vec, vec,
                           _const((3, D_CONV)), vec, sq, sq],
                 out_specs=pl.BlockSpec((tm, D_MODEL), lambda i: (i, 0)), out_shape=_sds((T, D_MODEL), BF16),
                 sem=('parallel',), vmem=VMEM_BIG)(yssm, proj, proj, d, glu_w, glu_b, g_ssm, cw, g_conv, avg16, avg64)


def _out_proj(ycat, w_out, x, gt, g_post, g_pre, sc, sh, tm):
    T = x.shape[0]

    def body(y_ref, w_ref, x_ref, gt_ref, gp_ref, g2_ref, sc_ref, sh_ref, o_ref, x1_ref, h_ref):
        o = _dot(y_ref[...], w_ref[...])
        o_ref[...] = o
        x1 = x_ref[...] + gt_ref[...] * (o * _rsqrt_mean(o) * gp_ref[...])
        x1_ref[...] = x1
        h_ref[...] = ((x1 * _rsqrt_mean(x1) * g2_ref[...]) * (1.0 + sc_ref[...]) + sh_ref[...]).astype(BF16)

    row = pl.BlockSpec((tm, D_MODEL), lambda i: (i, 0))
    vec = _const((1, D_MODEL))
    return _call(body, name='out_proj', grid=(T // tm,),
                 in_specs=[row, _const((D_MODEL, D_MODEL)), row, vec, vec, vec, vec, vec],
                 out_specs=[row, row, row],
                 out_shape=[_sds((T, D_MODEL)), _sds((T, D_MODEL)), _sds((T, D_MODEL), BF16)],
                 sem=('parallel',), vmem=VMEM_BIG)(ycat, w_out, x, gt, g_post, g_pre, sc, sh)


def _ffn_up(h2, w_s, tm):
    T = h2.shape[0]

    def body(h_ref, w_ref, o_ref):
        o_ref[...] = _dot(h_ref[...], w_ref[...]).astype(BF16)

    return _call(body, name='ffn_up', grid=(T // tm, N_DEV),
                 in_specs=[pl.BlockSpec((tm, D_MODEL), lambda i, j: (i, 0)),
                           pl.BlockSpec((None, D_MODEL, FF_SHARD), lambda i, j: (j, 0, 0))],
                 out_specs=pl.BlockSpec((None, tm, FF_SHARD), lambda i, j: (j, i, 0)),
                 out_shape=_sds((N_DEV, T, FF_SHARD), BF16), sem=('parallel', 'parallel'))(h2, w_s)


def _ffn_hidden(up_ref, halo_ref, cw_ref, i):
    hid = []
    for part in range(2):
        halo = jnp.where(i > 0, halo_ref[part].astype(F32), 0.0)
        hid.append(_conv3(up_ref[part].astype(F32), halo, cw_ref.at[part])[0])
    return hid


def _ffn_act(up4, cw4, tm):
    T = up4.shape[2]
    hb = _halo_before(tm, HALO16)

    def body(up_ref, halo_ref, cw_ref, o_ref, hid_ref):
        hid_a, hid_v = _ffn_hidden(up_ref, halo_ref, cw_ref, pl.program_id(0))
        o_ref[...] = (_silu_parts(hid_a)[0] * hid_v).astype(BF16)
        hid_ref[0] = hid_a.astype(BF16)
        hid_ref[1] = hid_v.astype(BF16)

    return _call(body, name='ffn_act', grid=(T // tm, 4),
                 in_specs=[pl.BlockSpec((2, None, tm, FF_SHARD), lambda i, j: (0, j, i, 0)),
                           pl.BlockSpec((2, None, HALO16, FF_SHARD), lambda i, j: (0, j, hb(i), 0)),
                           pl.BlockSpec((2, None, 3, FF_SHARD), lambda i, j: (0, j, 0, 0))],
                 out_specs=[pl.BlockSpec((None, tm, FF_SHARD), lambda i, j: (j, i, 0)),
                            pl.BlockSpec((2, None, tm, FF_SHARD), lambda i, j: (0, j, i, 0))],
                 out_shape=[_sds((4, T, FF_SHARD), BF16), _sds((2, 4, T, FF_SHARD), BF16)],
                 sem=('parallel', 'parallel'))(up4, up4, cw4)


def _ffn_down(act, wd4, x1, tgt, gt, g_post, tm):
    T = x1.shape[0]
    nb = T // tm

    def body(a_ref, w_ref, x1_ref, t_ref, gt_ref, g_ref, ddn_ref, dx_ref, loss_ref, dgt_ref, dg_ref, dn_ref):
        i, j = pl.program_id(0), pl.program_id(1)
        part = _dot(a_ref[0], w_ref[0]) + _dot(a_ref[1], w_ref[1])

        @pl.when(jnp.logical_and(i == 0, j == 0))
        def _():
            dgt_ref[...] = jnp.zeros_like(dgt_ref)
            dg_ref[...] = jnp.zeros_like(dg_ref)

        @pl.when(j == 0)
        def _():
            dn_ref[...] = part

        @pl.when(j > 0)
        def _():
            dn_ref[...] += part

        @pl.when(j == 1)
        def _():
            dn, gv, gate = dn_ref[...], g_ref[...], gt_ref[...]
            r = _rsqrt_mean(dn)
            normed = dn * r * gv
            err = x1_ref[...] + gate * normed - t_ref[...]
            dx = err * (1.0 / D_MODEL)
            dx_ref[...] = dx
            tot = jnp.sum(jnp.sum(err * err, axis=1, keepdims=True), axis=0, keepdims=True) * (0.5 / D_MODEL)
            loss_ref[...] = jnp.broadcast_to(tot, (8, 128))
            dgt_ref[...] += _colsum(dx * normed)
            dnn = dx * gate
            dg_ref[...] += _colsum(dnn * dn * r)
            ddn_ref[...] = _norm_bwd(dnn, dn, r, gv).astype(BF16)

    row = pl.BlockSpec((tm, D_MODEL), lambda i, j: (i, 0))
    vec = _const((1, D_MODEL))
    return _call(body, name='ffn_down', grid=(nb, 2),
                 in_specs=[pl.BlockSpec((2, tm, FF_SHARD), lambda i, j: (j, i, 0)),
                           pl.BlockSpec((2, FF_SHARD, D_MODEL), lambda i, j: (j, 0, 0)), row, row, vec, vec],
                 out_specs=[row, row, pl.BlockSpec((None, 8, 128), lambda i, j: (i, 0, 0)), vec, vec],
                 out_shape=[_sds((T, D_MODEL), BF16), _sds((T, D_MODEL)), _sds((nb, 8, 128)), _sds((1, D_MODEL)),
                            _sds((1, D_MODEL))],
                 scratch=[pltpu.VMEM((tm, D_MODEL), F32)], sem=('arbitrary', 'arbitrary'),
                 vmem=VMEM_BIG)(act, wd4, x1, tgt, gt, g_post)


def _ssm_prep(lre, lim, lst, b_re, b_im):
    def body(lre_ref, lim_ref, lst_ref, br_ref, bi_ref, ar_ref, ai_ref, bbr_ref, bbi_ref):
        ar, ai, qr, qi = _zoh(lre_ref[...], lim_ref[...], lst_ref[...])[:4]
        ar_ref[...] = ar
        ai_ref[...] = ai
        bbr_ref[...] = qr * br_ref[...] - qi * bi_ref[...]
        bbi_ref[...] = qr * bi_ref[...] + qi * br_ref[...]

    shp = lre.shape
    return _call(body, name='ssm_prep', grid=(1,), in_specs=[_const(shp)] * 5, out_specs=[_const(shp)] * 4,
                 out_shape=[_sds(shp)] * 4)(lre, lim, lst, b_re, b_im)


def _zoh(lre, lim, lst):
    lr = jnp.minimum(lre, LAMBDA_RE_MAX)
    st = jnp.exp(lst)
    mag = jnp.exp(lr * st)
    ar = mag * jnp.cos(lim * st)
    ai = mag * jnp.sin(lim * st)
    den = lr * lr + lim * lim
    qr = ((ar - 1.0) * lr + ai * lim) / den
    qi = (ai * lr - (ar - 1.0) * lim) / den
    return ar, ai, qr, qi, lr, st, den


def _ssm_prep_bwd(lre, lim, lst, b_re, b_im, dbbr, dbbi, dar, dai, seg):
    def body(lre_ref, lim_ref, lst_ref, br_ref, bi_ref, dbbr_ref, dbbi_ref, dar_ref, dai_ref, seg_ref,
             dbr_ref, dbi_ref, dlre_ref, dlim_ref, dlst_ref):
        lre_v = lre_ref[...]
        li = lim_ref[...]
        ar, ai, qr, qi, lr, st, den = _zoh(lre_v, li, lst_ref[...])
        br, bi, gbr, gbi = br_ref[...], bi_ref[...], dbbr_ref[...], dbbi_ref[...]
        dbr_ref[...] = qr * gbr + qi * gbi
        dbi_ref[...] = qr * gbi - qi * gbr
        gqr = _dot_split(br * gbr + bi * gbi, seg_ref[...], 3)
        gqi = _dot_split(br * gbi - bi * gbr, seg_ref[...], 3)
        ir, ii = lr / den, -li / den
        gar = dar_ref[...] + ir * gqr + ii * gqi
        gai = dai_ref[...] + ir * gqi - ii * gqr
        tr, ti = qr * ir - qi * ii, qr * ii + qi * ir
        glr = -(tr * gqr + ti * gqi)
        gli = -(tr * gqi - ti * gqr)
        gzr = ar * gar + ai * gai
        gzi = ar * gai - ai * gar
        glr = glr + st * gzr
        gli = gli + st * gzi
        gst = (lr * gzr + li * gzi) * st
        dlre_ref[...] = jnp.where(lre_v < LAMBDA_RE_MAX, glr, 0.0)
        dlim_ref[...] = gli
        dlst_ref[...] = jnp.sum(gst, axis=1, keepdims=True) * (1.0 / SSM_GROUP)

    shp = lre.shape
    return _call(body, name='ssm_prep_bwd', grid=(1,), in_specs=[_const(shp)] * 9 + [_const(seg.shape)],
                 out_specs=[_const(shp)] * 4 + [_const((N_GROUPS, 1))],
                 out_shape=[_sds(shp)] * 4 + [_sds((N_GROUPS, 1))], vmem=VMEM_BIG)(
                     lre, lim, lst, b_re, b_im, dbbr, dbbi, dar, dai, seg)


def _scan_specs(T):
    half = lambda cb: cb // 2
    return dict(
        chan=pl.BlockSpec((T, CHAN_BLOCK), lambda cb: (0, half(cb))),
        state=pl.BlockSpec((T, STATE_BLOCK), lambda cb: (0, cb)),
        b=pl.BlockSpec((CHAN_BLOCK, STATE_BLOCK), lambda cb: (half(cb), cb)),
        c=pl.BlockSpec((STATE_BLOCK, CHAN_BLOCK), lambda cb: (cb, half(cb))),
        lam=pl.BlockSpec((1, STATE_BLOCK), lambda cb: (0, cb)),
    )


def _complex_power(re, im, n):
    out = None
    while True:
        if n & 1:
            out = (re, im) if out is None else (out[0] * re - out[1] * im, out[0] * im + out[1] * re)
        n >>= 1
        if n == 0:
            return out
        re, im = re * re - im * im, 2.0 * re * im


def _rows8(i):
    if isinstance(i, int):
        return pl.ds(i * SUBLANES, SUBLANES)
    return pl.ds(pl.multiple_of(i * SUBLANES, SUBLANES), SUBLANES)


def _scan_loop(n_steps, body, init):
    trips = n_steps // SCAN_UNROLL

    def trip(t, carry):
        for u in range(SCAN_UNROLL):
            carry = body(t * SCAN_UNROLL + u, carry)
        return carry

    carry = lax.fori_loop(0, trips, trip, init)
    for step in range(trips * SCAN_UNROLL, n_steps):
        carry = body(step, carry)
    return carry


def _ssm_fwd(u_perm, b_re, b_im, c_re, c_im, lam_r, lam_i, ride):
    T = u_perm.shape[0]
    ls = T // SUBLANES
    rc = min(512, T)
    sp = _scan_specs(T)

    def body(u_ref, bre_ref, bim_ref, cre_ref, cim_ref, lr_ref, li_ref, sre_ref, sim_ref, y_ref):
        cb = pl.program_id(0)
        for c in range(T // rc):
            rows = pl.ds(c * rc, rc)
            sre_ref[rows, :] = _dot(u_ref[rows, :], bre_ref[...])
            sim_ref[rows, :] = _dot(u_ref[rows, :], bim_ref[...])
        shp = (SUBLANES, STATE_BLOCK)
        lr = jnp.broadcast_to(lr_ref[...], shp)
        li = jnp.broadcast_to(li_ref[...], shp)
        zero = jnp.zeros(shp, F32)

        def step(i, carry):
            sr, si = carry
            rows = _rows8(i)
            nr = lr * sr - li * si + sre_ref[rows, :]
            ni = lr * si + li * sr + sim_ref[rows, :]
            sre_ref[rows, :] = nr
            sim_ref[rows, :] = ni
            return nr, ni

        fr, fi = _scan_loop(ls, step, (zero, zero))
        pr, pi_ = _complex_power(lr, li, ls)
        row = lax.broadcasted_iota(jnp.int32, shp, 0)
        ir, ii = zero, zero
        for _ in range(SUBLANES - 1):
            er = fr + pr * ir - pi_ * ii
            ei = fi + pr * ii + pi_ * ir
            ir = jnp.where(row == 0, 0.0, pltpu.roll(er, 1, 0))
            ii = jnp.where(row == 0, 0.0, pltpu.roll(ei, 1, 0))

        def fix(i, carry):
            cr, ci = carry
            rows = _rows8(i)
            nr = lr * cr - li * ci
            ni = lr * ci + li * cr
            sre_ref[rows, :] += nr
            sim_ref[rows, :] += ni
            return nr, ni

        _scan_loop(ls, fix, (ir, ii))
        for c in range(T // rc):
            rows = pl.ds(c * rc, rc)
            yc = _dot(sre_ref[rows, :].astype(BF16), cre_ref[...]) - _dot(sim_ref[rows, :].astype(BF16), cim_ref[...])

            @pl.when(cb % 2 == 0)
            def _():
                y_ref[rows, :] = yc

            @pl.when(cb % 2 == 1)
            def _():
                y_ref[rows, :] += yc

    return _call(body, name='ssm_fwd', grid=(N_STATE // STATE_BLOCK,),
                 in_specs=[sp['chan'], sp['b'], sp['b'], sp['c'], sp['c'], sp['lam'], sp['lam']],
                 out_specs=[sp['state'], sp['state'], sp['chan']],
                 out_shape=[_sds((T, N_STATE)), _sds((T, N_STATE)), _sds((T, D_SSM))],
                 sem=('arbitrary',), vmem=VMEM_BIG, ride=ride)(u_perm, b_re, b_im, c_re, c_im, lam_r, lam_i)


def _ssm_bwd(dy_perm, u_perm, s_re, s_im, b_re, b_im, c_re, c_im, lam_r, lam_i, ride):
    T = u_perm.shape[0]
    ls = T // SUBLANES
    rc = min(512, T)
    sp = _scan_specs(T)
    ncb = N_STATE // STATE_BLOCK

    def body(dy_ref, u_ref, sre_ref, sim_ref, bre_ref, bim_ref, cre_ref, cim_ref, lr_ref, li_ref,
             du_ref, dbr_ref, dbi_ref, dcr_ref, dci_ref, dar_ref, dai_ref, gre_ref, gim_ref):
        cb = pl.program_id(0)
        for c in range(T // rc):
            rows = pl.ds(c * rc, rc)
            gre_ref[rows, :] = _dot_nt(dy_ref[rows, :], cre_ref[...])
            gim_ref[rows, :] = -_dot_nt(dy_ref[rows, :], cim_ref[...])
        shp = (SUBLANES, STATE_BLOCK)
        lr = jnp.broadcast_to(lr_ref[...], shp)
        li = jnp.broadcast_to(li_ref[...], shp)
        zero = jnp.zeros(shp, F32)

        def step(k, carry):
            gr, gi = carry
            rows = _rows8(ls - 1 - k)
            nr = lr * gr + li * gi + gre_ref[rows, :]
            ni = lr * gi - li * gr + gim_ref[rows, :]
            gre_ref[rows, :] = nr
            gim_ref[rows, :] = ni
            return nr, ni

        fr, fi = _scan_loop(ls, step, (zero, zero))
        pr, pi_ = _complex_power(lr, -li, ls)
        row = lax.broadcasted_iota(jnp.int32, shp, 0)
        cr, ci = zero, zero
        for _ in range(SUBLANES - 1):
            er = fr + pr * cr - pi_ * ci
            ei = fi + pr * ci + pi_ * cr
            cr = jnp.where(row == SUBLANES - 1, 0.0, pltpu.roll(er, SUBLANES - 1, 0))
            ci = jnp.where(row == SUBLANES - 1, 0.0, pltpu.roll(ei, SUBLANES - 1, 0))

        def fix(k, carry):
            dr, di, ar, ai = carry
            rows = _rows8(ls - 1 - k)
            dr, di = lr * dr + li * di, lr * di - li * dr
            gr = gre_ref[rows, :] + dr
            gi = gim_ref[rows, :] + di
            gre_ref[rows, :] = gr
            gim_ref[rows, :] = gi
            prev = _rows8(ls - 2 - k)
            spr, spi = sre_ref[prev, :], sim_ref[prev, :]
            return dr, di, ar + gr * spr + gi * spi, ai + gi * spr - gr * spi

        dr, di, ar, ai = _scan_loop(ls - 1, fix, (cr, ci, zero, zero))
        first = pl.ds(0, SUBLANES)
        last = pl.ds((ls - 1) * SUBLANES, SUBLANES)
        gr = gre_ref[first, :] + (lr * dr + li * di)
        gi = gim_ref[first, :] + (lr * di - li * dr)
        gre_ref[first, :] = gr
        gim_ref[first, :] = gi
        spr = jnp.where(row == 0, 0.0, pltpu.roll(sre_ref[last, :], 1, 0))
        spi = jnp.where(row == 0, 0.0, pltpu.roll(sim_ref[last, :], 1, 0))
        dar_ref[...] = _colsum(ar + gr * spr + gi * spi)
        dai_ref[...] = _colsum(ai + gi * spr - gr * spi)

        for c in range(T // rc):
            rows = pl.ds(c * rc, rc)
            g_r, g_i = gre_ref[rows, :].astype(BF16), gim_ref[rows, :].astype(BF16)
            s_r, s_i = sre_ref[rows, :].astype(BF16), sim_ref[rows, :].astype(BF16)
            ub, dyb = u_ref[rows, :], dy_ref[rows, :]
            duc = _dot_nt(g_r, bre_ref[...]) + _dot_nt(g_i, bim_ref[...])
            parts = (_dot_tn(ub, g_r), _dot_tn(ub, g_i), _dot_tn(s_r, dyb), -_dot_tn(s_i, dyb))
            outs = (dbr_ref, dbi_ref, dcr_ref, dci_ref)
            for o_ref, part in zip(outs, parts):
                if c == 0:
                    o_ref[...] = part
                else:
                    o_ref[...] += part

            @pl.when(cb % 2 == 0)
            def _():
                du_ref[rows, :] = duc

            @pl.when(cb % 2 == 1)
            def _():
                du_ref[rows, :] += duc

    blk = lambda r, c: pl.BlockSpec((None, r, c), lambda cb: (cb, 0, 0))
    return _call(body, name='ssm_bwd', grid=(ncb,),
                 in_specs=[sp['chan'], sp['chan'], sp['state'], sp['state'], sp['b'], sp['b'], sp['c'], sp['c'],
                           sp['lam'], sp['lam']],
                 out_specs=[sp['chan'], blk(CHAN_BLOCK, STATE_BLOCK), blk(CHAN_BLOCK, STATE_BLOCK),
                            blk(STATE_BLOCK, CHAN_BLOCK), blk(STATE_BLOCK, CHAN_BLOCK), blk(1, STATE_BLOCK),
                            blk(1, STATE_BLOCK)],
                 out_shape=[_sds((T, D_SSM)), _sds((ncb, CHAN_BLOCK, STATE_BLOCK)), _sds((ncb, CHAN_BLOCK, STATE_BLOCK)),
                            _sds((ncb, STATE_BLOCK, CHAN_BLOCK)), _sds((ncb, STATE_BLOCK, CHAN_BLOCK)),
                            _sds((ncb, 1, STATE_BLOCK)), _sds((ncb, 1, STATE_BLOCK))],
                 scratch=[pltpu.VMEM((T, STATE_BLOCK), F32), pltpu.VMEM((T, STATE_BLOCK), F32)],
                 sem=('arbitrary',), vmem=VMEM_BIG, ride=ride)(dy_perm, u_perm, s_re, s_im, b_re, b_im, c_re, c_im,
                                                               lam_r, lam_i)


def _ffn_dact(ddn, wd4, hid4, tm):
    T = ddn.shape[0]

    def body(d_ref, w_ref, hid_ref, o_ref):
        dact = _dot_nt(d_ref[...], w_ref[...])
        silu, dsilu = _silu_parts(hid_ref[0].astype(F32))
        o_ref[0] = (dact * hid_ref[1].astype(F32) * dsilu).astype(BF16)
        o_ref[1] = (dact * silu).astype(BF16)

    blk = pl.BlockSpec((2, None, tm, FF_SHARD), lambda i, j: (0, j, i, 0))
    return _call(body, name='ffn_dact', grid=(T // tm, 4),
                 in_specs=[pl.BlockSpec((tm, D_MODEL), lambda i, j: (i, 0)),
                           pl.BlockSpec((None, FF_SHARD, D_MODEL), lambda i, j: (j, 0, 0)), blk],
                 out_specs=blk, out_shape=_sds((2, 4, T, FF_SHARD), BF16),
                 sem=('parallel', 'parallel'))(ddn, wd4, hid4)


def _ffn_dup(dhid8, up8, cw8, tm, ride):
    T = up8.shape[1]
    nb = T // tm
    ha = _halo_after(tm, T, HALO16)

    def body(dh_ref, dha_ref, up_ref, cw_ref, dup_ref, dcw_ref):
        i = pl.program_id(1)

        @pl.when(i == 0)
        def _():
            dcw_ref[...] = jnp.zeros_like(dcw_ref)

        dh = dh_ref[...].astype(F32)
        dup, dh1, dh2 = _conv3_t(dh, jnp.where(i < nb - 1, dha_ref[...].astype(F32), 0.0), cw_ref)
        dup_ref[...] = dup.astype(BF16)
        up = up_ref[...].astype(F32)
        dcw_ref[0:1, :] += _colsum(dh2 * up)
        dcw_ref[1:2, :] += _colsum(dh1 * up)
        dcw_ref[2:3, :] += _colsum(dh * up)

    main = pl.BlockSpec((None, tm, FF_SHARD), lambda j, i: (j, i, 0))
    return _call(body, name='ffn_dup', grid=(N_DEV, nb),
                 in_specs=[main, pl.BlockSpec((None, HALO16, FF_SHARD), lambda j, i: (j, ha(i), 0)), main,
                           pl.BlockSpec((None, 3, FF_SHARD), lambda j, i: (j, 0, 0))],
                 out_specs=[main, pl.BlockSpec((None, 8, FF_SHARD), lambda j, i: (j, 0, 0))],
                 out_shape=[_sds((N_DEV, T, FF_SHARD), BF16), _sds((N_DEV, 8, FF_SHARD))],
                 sem=('parallel', 'arbitrary'), ride=ride)(dhid8, dhid8, up8, cw8)


def _grad_tn(a, b, a_spec, b_spec, groups, m, n, tk, name, ride=None, parts=1):
    T = a.shape[-2]
    nk = T // tk
    mp = m // parts

    def body(a_ref, b_ref, *refs):
        o_refs, acc_ref = refs[:parts], refs[parts]
        k = pl.program_id(1)
        part = _dot_tn(a_ref[...], b_ref[...])

        @pl.when(k == 0)
        def _():
            acc_ref[...] = part

        @pl.when(k > 0)
        def _():
            acc_ref[...] += part

        @pl.when(k == nk - 1)
        def _():
            for p, o_ref in enumerate(o_refs):
                o_ref[...] = acc_ref[p * mp:(p + 1) * mp, :].astype(BF16)

    out_spec = pl.BlockSpec((None, mp, n), lambda g, k: (g, 0, 0))
    res = _call(body, name=name, grid=(groups, nk), in_specs=[a_spec, b_spec], out_specs=[out_spec] * parts,
                out_shape=[_sds((groups, mp, n), BF16)] * parts, scratch=[pltpu.VMEM((m, n), F32)],
                sem=('parallel', 'arbitrary'), vmem=VMEM_BIG, ride=ride)(a, b)
    if parts > 1:
        return res
    return res[0] if ride is None else (res[0][0], res[1])


def _grad_w_in(h1, dproj, tk, ride):
    T = h1.shape[0]
    nk = T // tk
    half = D_IN_PROJ // 2

    def body(a_ref, b_ref, o_ref, acc_ref):
        k = pl.program_id(0)
        for h in range(2):
            cols = slice(h * half, (h + 1) * half)
            part = _dot_tn(a_ref[...], b_ref[:, cols])

            @pl.when(k == 0)
            def _():
                acc_ref[:, cols] = part

            @pl.when(k > 0)
            def _():
                acc_ref[:, cols] += part

        @pl.when(k == nk - 1)
        def _():
            for g in range(N_DEV):
                o_ref[g] = acc_ref[:, g * IN_SHARD:(g + 1) * IN_SHARD].astype(BF16)

    return _call(body, name='grad_w_in', grid=(nk,),
                 in_specs=[pl.BlockSpec((tk, D_MODEL), lambda k: (k, 0)), pl.BlockSpec((tk, D_IN_PROJ), lambda k: (k, 0))],
                 out_specs=_const((N_DEV, D_MODEL, IN_SHARD)), out_shape=_sds((N_DEV, D_MODEL, IN_SHARD), BF16),
                 scratch=[pltpu.VMEM((D_MODEL, D_IN_PROJ), F32)], sem=('arbitrary',), vmem=VMEM_BIG, ride=ride)(h1, dproj)


def _pre_norm_bwd(dz, dz_spec, w_s, xin, dres, sc, g, tm, name, ride, below=None, group=1):
    T = xin.shape[0]
    n = w_s.shape[2]
    steps = N_DEV // group

    def body(dz_ref, w_ref, x_ref, dr_ref, sc_ref, g_ref, *refs):
        if below is None:
            dx_ref, dsh_ref, dsc_ref, dg_ref = refs
            sums = (dsh_ref, dsc_ref, dg_ref)
        else:
            v_ref, gate_ref, g2_ref, dx_ref, dsh_ref, dsc_ref, dg_ref, dv_ref, dgate_ref, dg2_ref = refs
            sums = (dsh_ref, dsc_ref, dg_ref, dgate_ref, dg2_ref)
        i, j = pl.program_id(0), pl.program_id(1)
        piece = (lambda s: dz_ref[s]) if dz.ndim == 3 else (lambda s: dz_ref[:, s * n:(s + 1) * n])
        part = _dot_nt(piece(0), w_ref[0])
        for s in range(1, group):
            part = part + _dot_nt(piece(s), w_ref[s])

        @pl.when(jnp.logical_and(i == 0, j == 0))
        def _():
            for s_ref in sums:
                s_ref[...] = jnp.zeros_like(s_ref)

        @pl.when(j == 0)
        def _():
            dx_ref[...] = part

        @pl.when(j > 0)
        def _():
            dx_ref[...] += part

        @pl.when(j == steps - 1)
        def _():
            dh, xv, gv = dx_ref[...], x_ref[...], g_ref[...]
            r = _rsqrt_mean(xv)
            dsh_ref[...] += _colsum(dh)
            dsc_ref[...] += _colsum(dh * (xv * r * gv))
            dxn = dh * (1.0 + sc_ref[...])
            dg_ref[...] += _colsum(dxn * xv * r)
            dx = dr_ref[...] + _norm_bwd(dxn, xv, r, gv)
            dx_ref[...] = dx
            if below is not None:
                v, g2 = v_ref[...], g2_ref[...]
                rv = _rsqrt_mean(v)
                dgate_ref[...] += _colsum(dx * (v * rv * g2))
                dn = dx * gate_ref[...]
                dg2_ref[...] += _colsum(dn * v * rv)
                dv_ref[...] = _norm_bwd(dn, v, rv, g2).astype(BF16)

    row = pl.BlockSpec((tm, D_MODEL), lambda i, j: (i, 0))
    vec = _const((1, D_MODEL))
    in_specs = [dz_spec, pl.BlockSpec((group, D_MODEL, n), lambda i, j: (j, 0, 0)), row, row, vec, vec]
    out_specs = [row, vec, vec, vec]
    out_shape = [_sds((T, D_MODEL)), _sds((1, D_MODEL)), _sds((1, D_MODEL)), _sds((1, D_MODEL))]
    args = [dz, w_s, xin, dres, sc, g]
    if below is not None:
        in_specs += [row, vec, vec]
        out_specs += [row, vec, vec]
        out_shape += [_sds((T, D_MODEL), BF16), _sds((1, D_MODEL)), _sds((1, D_MODEL))]
        args += list(below)
    return _call(body, name=name, grid=(T // tm, steps), in_specs=in_specs, out_specs=out_specs,
                 out_shape=out_shape, sem=('arbitrary', 'arbitrary'), vmem=VMEM_MOST, ride=ride)(*args)


def _mix_bwd(d_o, w_out, yssm, proj, d, glu_w, glu_b, g_ssm, cw, g_conv, avg16, avg64, tm):
    T = yssm.shape[0]
    hb = _halo_before(tm)

    def body(do_ref, wo_ref, y_ref, p_ref, ph_ref, d_ref, gw_ref, gb_ref, gs_ref, cw_ref, gc_ref, a16_ref, a64_ref,
             dy_ref, dconv_ref, dbg_ref, z_ref, dlin_ref, acc_ref):
        i = pl.program_id(0)
        dyc = _dot_nt(do_ref[...], wo_ref[...])

        @pl.when(i == 0)
        def _():
            acc_ref[...] = jnp.zeros_like(acc_ref)

        u = p_ref[:, 0:D_SSM]
        y = y_ref[...] + d_ref[...] * u
        z, t = _gelu(y)
        gate = _sigmoid(_dot(z.astype(BF16), gw_ref[...]) + gb_ref[...])
        ya = z * gate
        rs = lax.rsqrt(_dot_split(ya * ya, a16_ref[...], 2) + EPS)
        dna = dyc[:, 0:D_SSM]
        acc_ref[1:2, :] += _colsum(dna * ya * rs)
        dya = _head_norm_bwd(dna, ya, rs, gs_ref[...], a16_ref[...])
        dlin = dya * z * gate * (1.0 - gate)
        acc_ref[0:1, :] += _colsum(dlin)
        dlin_b = dlin.astype(BF16)
        dz = dya * gate + _dot_nt(dlin_b, gw_ref[...])
        dy = dz * _gelu_grad(y, t)
        acc_ref[3:4, :] += _colsum(dy * u)
        dy_ref[...] = dy
        z_ref[...] = z.astype(BF16)
        dlin_ref[...] = dlin_b

        bg = p_ref[:, D_SSM:D_SSM + D_CONV]
        cv = p_ref[:, D_SSM + D_CONV:D_SSM + 2 * D_CONV] * p_ref[:, D_SSM + 2 * D_CONV:D_IN_PROJ]
        hv = ph_ref[:, D_SSM + D_CONV:D_SSM + 2 * D_CONV] * ph_ref[:, D_SSM + 2 * D_CONV:D_IN_PROJ]
        hv = jnp.where(i > 0, hv, 0.0)
        conv, cv1, cv2 = _conv3(cv, hv, cw_ref)
        yb = bg * conv
        rsb = lax.rsqrt(_dot_split(yb * yb, a64_ref[...], 2) + EPS)
        dnb = dyc[:, D_SSM:D_MODEL]
        acc_ref[2:3, :] += _colsum(dnb * yb * rsb)
        dyb = _head_norm_bwd(dnb, yb, rsb, gc_ref[...], a64_ref[...])
        dbg_ref[...] = dyb * conv
        dconv = dyb * bg
        dconv_ref[...] = dconv
        acc_ref[4:5, :] += _colsum(dconv * cv2)
        acc_ref[5:6, :] += _colsum(dconv * cv1)
        acc_ref[6:7, :] += _colsum(dconv * cv)

    vec = _const((1, D_SSM))
    sq = _const((D_SSM, D_SSM))
    half = pl.BlockSpec((tm, D_SSM), lambda i: (i, 0))
    return _call(body, name='mix_bwd', grid=(T // tm,),
                 in_specs=[pl.BlockSpec((tm, D_MODEL), lambda i: (i, 0)), _const((D_MODEL, D_MODEL)), half,
                           pl.BlockSpec((tm, D_IN_PROJ), lambda i: (i, 0)),
                           pl.BlockSpec((HALO, D_IN_PROJ), lambda i: (hb(i), 0)), vec, sq, vec, vec,
                           _const((3, D_CONV)), vec, sq, sq],
                 out_specs=[half, half, half, half, half, _const((8, D_SSM))],
                 out_shape=[_sds((T, D_SSM)), _sds((T, D_SSM)), _sds((T, D_SSM)), _sds((T, D_SSM), BF16),
                            _sds((T, D_SSM), BF16), _sds((8, D_SSM))],
                 sem=('arbitrary',), vmem=VMEM_BIG)(d_o, w_out, yssm, proj, proj, d, glu_w, glu_b, g_ssm, cw, g_conv,
                                                   avg16, avg64)


def _mix_bwd_proj(dconv, proj, du_ssm, dy, d, dbg, cw, tm):
    T = dy.shape[0]
    nb = T // tm
    ha = _halo_after(tm, T)

    def body(dc_ref, dch_ref, cg_ref, v_ref, du_ref, dy_ref, d_ref, dbg_ref, cw_ref, o_ref):
        i = pl.program_id(0)
        dcv = _conv3_t(dc_ref[...], jnp.where(i < nb - 1, dch_ref[...], 0.0), cw_ref)[0]
        o_ref[:, 0:D_SSM] = (du_ref[...] + dy_ref[...] * d_ref[...]).astype(BF16)
        o_ref[:, D_SSM:D_SSM + D_CONV] = dbg_ref[...].astype(BF16)
        o_ref[:, D_SSM + D_CONV:D_SSM + 2 * D_CONV] = (dcv * v_ref[...]).astype(BF16)
        o_ref[:, D_SSM + 2 * D_CONV:D_IN_PROJ] = (dcv * cg_ref[...]).astype(BF16)

    half = pl.BlockSpec((tm, D_SSM), lambda i: (i, 0))
    return _call(body, name='mix_bwd_proj', grid=(nb,),
                 in_specs=[half, pl.BlockSpec((HALO, D_CONV), lambda i: (ha(i), 0)),
                           pl.BlockSpec((tm, D_CONV), lambda i: (i, 2)), pl.BlockSpec((tm, D_CONV), lambda i: (i, 3)),
                           half, half, _const((1, D_SSM)), half, _const((3, D_CONV))],
                 out_specs=pl.BlockSpec((tm, D_IN_PROJ), lambda i: (i, 0)), out_shape=_sds((T, D_IN_PROJ), BF16),
                 sem=('parallel',))(dconv, dconv, proj, proj, du_ssm, dy, d, dbg, cw)


def _row_tile(rows, cols, slots):
    for cand in (512, 256, 128, 64, 32, 16, 8):
        if rows % cand == 0 and slots * cand * cols * 4 <= (2 << 20):
            return cand
    return rows


def _adamw_math(g, w, m, v):
    m2 = ADAM_B1 * m + (1.0 - ADAM_B1) * g
    v2 = ADAM_B2 * v + (1.0 - ADAM_B2) * (g * g)
    m_hat = m2 / (1.0 - ADAM_B1 ** ADAM_STEP)
    v_hat = v2 / (1.0 - ADAM_B2 ** ADAM_STEP)
    return -ADAM_LR * (m_hat / (jnp.sqrt(v_hat) + ADAM_EPS) + ADAM_WD * w), m2, v2


def _adamw(pieces, w, m, v, name):
    slots, _, cols = pieces[0].shape
    rows = sum(p.shape[1] for p in pieces)
    tr = _row_tile(pieces[0].shape[1], cols, slots)
    starts, pos = [], 0
    for p in pieces:
        assert p.shape[1] % tr == 0
        starts.append(pos)
        pos += p.shape[1] // tr

    def body(*refs):
        g_refs = refs[:len(pieces)]
        w_ref, m_ref, v_ref, go_ref, d_ref, mo_ref, vo_ref = refs[len(pieces):]
        i = pl.program_id(0)
        g = None
        for g_ref, start in zip(g_refs, starts):
            part = g_ref[0].astype(F32)
            for s in range(1, slots):
                part = part + g_ref[s].astype(F32)
            g = part if g is None else jnp.where(i >= start, part, g)
        go_ref[...] = g
        d_ref[...], mo_ref[...], vo_ref[...] = _adamw_math(g, w_ref[...], m_ref[...], v_ref[...])

    def piece_spec(start, count):
        return pl.BlockSpec((slots, tr, cols), lambda i: (0, jnp.clip(i - start, 0, count - 1), 0))

    blk = pl.BlockSpec((tr, cols), lambda i: (i, 0))
    return _call(body, name=name, grid=(rows // tr,),
                 in_specs=[piece_spec(s, p.shape[1] // tr) for s, p in zip(starts, pieces)] + [blk, blk, blk],
                 out_specs=[blk] * 4, out_shape=[_sds((rows, cols))] * 4, sem=('parallel',))(*pieces, w, m, v)


def _to_scan_rows(a):
    T, n = a.shape
    return a.reshape(SUBLANES, T // SUBLANES, n).transpose(1, 0, 2).reshape(T, n)


def _from_scan_rows(a):
    T, n = a.shape
    return a.reshape(T // SUBLANES, SUBLANES, n).transpose(1, 0, 2).reshape(T, n)


def _expand(a):
    return jnp.repeat(a, SSM_GROUP, axis=1)


def _block_diag_b(bb):
    eye = jnp.eye(N_GROUPS, dtype=bb.dtype)
    return (bb.transpose(0, 2, 1)[:, :, None, :] * eye[:, None, :, None]).reshape(D_SSM, N_STATE)


def _block_diag_c(cc):
    eye = jnp.eye(N_GROUPS, dtype=cc.dtype)
    return (cc.transpose(0, 2, 1)[:, :, None, :] * eye[:, None, :, None]).reshape(N_STATE, D_SSM)


def _diag_blocks(x, chan_major):
    e2 = jnp.eye(2, dtype=x.dtype)
    e4 = jnp.eye(4, dtype=x.dtype)
    if chan_major:
        x = x.reshape(4, 2, 2, 4, SSM_GROUP, 4, SSM_STATE)
        x = x * e2[None, :, :, None, None, None, None] * e4[None, None, None, :, None, :, None]
        return x.sum(axis=(2, 3)).transpose(0, 1, 3, 4, 2).reshape(N_GROUPS, SSM_STATE, SSM_GROUP)
    x = x.reshape(4, 2, 4, SSM_STATE, 2, 4, SSM_GROUP)
    x = x * e2[None, :, None, None, :, None, None] * e4[None, None, :, None, None, :, None]
    return x.sum(axis=(4, 5)).reshape(N_GROUPS, SSM_STATE, SSM_GROUP)


SMALL_LAYOUT = {
    'ssm_b_re': (0, 0, 32, 1024), 'ssm_b_im': (32, 0, 32, 1024), 'ssm_c_re': (64, 0, 32, 1024),
    'ssm_c_im': (96, 0, 32, 1024), 'b_ada': (128, 0, 6, 1024), 'g_pre_mix': (134, 0, 1, 1024),
    'g_post_mix': (135, 0, 1, 1024), 'ssm_lam_re': (136, 0, 2, 1024), 'ssm_lam_im': (138, 0, 2, 1024),
    'ssm_log_step': (140, 0, 1, 32), 'glu_b': (141, 0, 1, 512), 'g_out_ssm': (141, 512, 1, 512),
    'g_out_conv': (142, 0, 1, 512), 'ssm_d': (142, 512, 1, 512), 'g_pre_ffn': (143, 0, 1, 1024),
    'g_post_ffn': (144, 0, 1, 1024)}
SMALL_ROWS = 152
B_ADA_ROW = SMALL_LAYOUT['b_ada'][0]
LATE_ROWS = {('b_ada', 0): 0, ('b_ada', 1): 1, ('g_pre_mix', 0): 2}


def _adamw_small(gathered, late, wts, mom_m, mom_v):
    names = list(SMALL_LAYOUT)
    n = len(names)

    def body(*refs):
        g_ref, late_ref, ins, outs = refs[0], refs[1], refs[2:2 + 3 * n], refs[2 + 3 * n:]
        for p, name in enumerate(names):
            r0, c0, rows, cols = SMALL_LAYOUT[name]
            pieces = [(0, rows)] if rows % 8 == 0 else [(r, 1) for r in range(rows)]
            for r, cnt in pieces:
                src_ref, first = (late_ref, LATE_ROWS[name, r]) if (name, r) in LATE_ROWS else (g_ref, r0 + r)
                g = src_ref[0, first:first + cnt, c0:c0 + cols]
                for s in range(1, N_DEV):
                    g = g + src_ref[s, first:first + cnt, c0:c0 + cols]
                w, m, v = (ins[3 * p + q][r:r + cnt, :] for q in range(3))
                res = (g,) + _adamw_math(g, w, m, v)
                for q in range(4):
                    outs[4 * p + q][r:r + cnt, :] = res[q]

    shapes = [SMALL_LAYOUT[name][2:] for name in names]
    args = [gathered, late]
    for name, shp in zip(names, shapes):
        args += [wts[name].reshape(shp), mom_m[name].reshape(shp), mom_v[name].reshape(shp)]
    outs = _call(body, name='adamw_small', grid=(1,),
                 in_specs=[_const(gathered.shape), _const(late.shape)]
                 + [_const(shp) for shp in shapes for _ in range(3)],
                 out_specs=[_const(shp) for shp in shapes for _ in range(4)],
                 out_shape=[_sds(shp) for shp in shapes for _ in range(4)], vmem=VMEM_BIG)(*args)
    res = {}
    for p, name in enumerate(names):
        for q, kind in enumerate(('g', 'd', 'm', 'v')):
            res[kind, name] = outs[4 * p + q].reshape(wts[name].shape)
    return res


def kernel(x, c, w_ada, b_ada, g_pre_mix, g_post_mix, w_in, ssm_lam_re, ssm_lam_im, ssm_log_step, ssm_b_re, ssm_b_im, ssm_c_re, ssm_c_im, ssm_d, glu_w, glu_b, g_out_ssm, conv_w, g_out_conv, w_out, g_pre_ffn, g_post_ffn, w_up, ffn_conv_w, w_down, loss_target, m_w_ada, m_b_ada, m_g_pre_mix, m_g_post_mix, m_w_in, m_ssm_lam_re, m_ssm_lam_im, m_ssm_log_step, m_ssm_b_re, m_ssm_b_im, m_ssm_c_re, m_ssm_c_im, m_ssm_d, m_glu_w, m_glu_b, m_g_out_ssm, m_conv_w, m_g_out_conv, m_w_out, m_g_pre_ffn, m_g_post_ffn, m_w_up, m_ffn_conv_w, m_w_down, v_w_ada, v_b_ada, v_g_pre_mix, v_g_post_mix, v_w_in, v_ssm_lam_re, v_ssm_lam_im, v_ssm_log_step, v_ssm_b_re, v_ssm_b_im, v_ssm_c_re, v_ssm_c_im, v_ssm_d, v_glu_w, v_glu_b, v_g_out_ssm, v_conv_w, v_g_out_conv, v_w_out, v_g_pre_ffn, v_g_post_ffn, v_w_up, v_ffn_conv_w, v_w_down):
    args = dict(locals())
    wts = {n: args[n] for n in WEIGHTS}
    mom_m = {n: args['m_' + n] for n in WEIGHTS}
    mom_v = {n: args['v_' + n] for n in WEIGHTS}
    T = x.shape[1]
    tm = min(512, T)
    tw = min(1024, T)
    me = _me()[3]
    xt, tgt = x[0], loss_target[0]

    c_all, w_in_s, glu_s, w_out_s, conv_s = _exchange(
        [c, w_in[0].astype(BF16), glu_w[0].astype(BF16), w_out[0].astype(BF16), conv_w[0]], name='gather_first',
        scatter=False)
    c_all = c_all.reshape(N_DEV, D_MODEL)
    b_cols = lax.dynamic_slice(b_ada, (0, me * ADA_SHARD), (1, ADA_SHARD))
    mod_cols, c_act = _mod_cols(c_all, w_ada[0], b_cols)
    (mod_all,) = _exchange([mod_cols], name='gather_mod', scatter=False)
    mod = lax.dynamic_slice(mod_all, (0, me, 0), (N_DEV, 1, ADA_SHARD)).reshape(N_MOD, 1, D_MODEL)
    sh1, sc1, gt1, sh2, sc2, gt2 = [mod[k] for k in range(N_MOD)]

    glu_full = glu_s.reshape(D_SSM, D_SSM)
    w_out_full = w_out_s.reshape(D_MODEL, D_MODEL)
    cw_full = conv_s.transpose(1, 0, 2).reshape(3, D_CONV)

    lre_x, lim_x = _expand(ssm_lam_re[0]), _expand(ssm_lam_im[0])
    lst_x = jnp.broadcast_to(ssm_log_step[0][:, None], (N_GROUPS, SSM_STATE * SSM_GROUP))
    b_re_x = ssm_b_re[0].reshape(N_GROUPS, -1)
    b_im_x = ssm_b_im[0].reshape(N_GROUPS, -1)
    ar_x, ai_x, bbr_x, bbi_x = _ssm_prep(lre_x, lim_x, lst_x, b_re_x, b_im_x)
    lam_r = ar_x[:, ::SSM_GROUP].reshape(1, N_STATE)
    lam_i = ai_x[:, ::SSM_GROUP].reshape(1, N_STATE)
    big_b_re = _block_diag_b(bbr_x.reshape(N_GROUPS, SSM_STATE, SSM_GROUP)).astype(BF16)
    big_b_im = _block_diag_b(bbi_x.reshape(N_GROUPS, SSM_STATE, SSM_GROUP)).astype(BF16)
    big_c_re = _block_diag_c(ssm_c_re[0]).astype(BF16)
    big_c_im = _block_diag_c(ssm_c_im[0]).astype(BF16)
    head = jnp.arange(D_SSM)
    avg16 = jnp.where(head[:, None] // SSM_GROUP == head[None, :] // SSM_GROUP, 1.0 / SSM_GROUP, 0.0).astype(BF16)
    hd = D_CONV // CONV_HEADS
    avg64 = jnp.where(head[:, None] // hd == head[None, :] // hd, 1.0 / hd, 0.0).astype(BF16)

    (proj, h1), (w_down_s, ffn_conv_s) = _pre_mix(xt, sc1, sh1, g_pre_mix, w_in_s, tw,
                                                  ([w_down[0].astype(BF16), ffn_conv_w[0]], False))
    wd4 = w_down_s.reshape(4, FF_SHARD, D_MODEL)
    cw4 = ffn_conv_s.reshape(2, 4, 3, FF_SHARD)
    u_perm = _to_scan_rows(proj[:, :D_SSM]).astype(BF16)
    (s_re, s_im, y_perm), (w_up_s,) = _ssm_fwd(u_perm, big_b_re, big_b_im, big_c_re, big_c_im, lam_r, lam_i,
                                               ([w_up[0].astype(BF16)], False))
    yssm = _from_scan_rows(y_perm)
    mix_args = (ssm_d, glu_full, glu_b, g_out_ssm, cw_full, g_out_conv, avg16, avg64)
    ycat = _mix_fwd(yssm, proj, *mix_args, tm)
    o, x1, h2 = _out_proj(ycat, w_out_full, xt, gt1, g_post_mix, g_pre_ffn, sc2, sh2, tm)
    up8 = _ffn_up(h2, w_up_s, tw)
    up4 = up8.reshape(2, 4, T, FF_SHARD)
    act, hid4 = _ffn_act(up4, cw4, tm)
    ddn, dx2, loss_parts, d_gt2, d_g_post_ffn = _ffn_down(act, wd4, x1, tgt, gt2, g_post_ffn, tw)
    loss_local = jnp.sum(loss_parts[:, 0, 0])

    got = {}
    dhid = _ffn_dact(ddn, wd4, hid4, tm)
    g_w_down = _grad_tn(act, ddn, pl.BlockSpec((None, tw, FF_SHARD), lambda g, k: (g, k, 0)),
                        pl.BlockSpec((tw, D_MODEL), lambda g, k: (k, 0)), 4, FF_SHARD, D_MODEL, tw, 'grad_w_down')
    (dup8, dcw_ffn), (got['w_down'],) = _ffn_dup(dhid.reshape(N_DEV, T, FF_SHARD), up8, ffn_conv_s, tm,
                                                 ([g_w_down.reshape(N_DEV, D_FF // N_DEV, D_MODEL)], True))
    g_w_up_halves = _grad_tn(h2, dup8, pl.BlockSpec((tw, D_MODEL), lambda g, k: (k, 0)),
                             pl.BlockSpec((None, tw, FF_SHARD), lambda g, k: (g, k, 0)), N_DEV, D_MODEL, FF_SHARD, tw,
                             'grad_w_up', parts=2)
    (dx1, d_sh2, d_sc2, d_g_pre_ffn, d_o, d_gt1, d_g_post_mix), (got_up_0, got['ffn_conv_w']) = _pre_norm_bwd(
        dup8, pl.BlockSpec((2, tw, FF_SHARD), lambda i, j: (j, i, 0)), w_up_s, x1, dx2, sc2, g_pre_ffn, tw,
        'ffn_in_bwd', ([g_w_up_halves[0], dcw_ffn], True), below=(o, gt1, g_post_mix), group=2)

    g_w_out = _grad_tn(ycat, d_o, pl.BlockSpec((tw, D_MODEL), lambda g, k: (k, 0)),
                       pl.BlockSpec((tw, D_MODEL), lambda g, k: (k, 0)), 1, D_MODEL, D_MODEL, tw, 'grad_w_out')
    dy, dconv, dbg, z_b, dlin_b, sums = _mix_bwd(d_o, w_out_full, yssm, proj, *mix_args, tm)
    g_glu_w = _grad_tn(z_b, dlin_b, pl.BlockSpec((tw, D_SSM), lambda g, k: (k, 0)),
                       pl.BlockSpec((tw, D_SSM), lambda g, k: (k, 0)), 1, D_SSM, D_SSM, tw, 'grad_glu_w')
    dy_perm = _to_scan_rows(dy).astype(BF16)
    (du_perm, dbr_blk, dbi_blk, dcr_blk, dci_blk, dar_blk, dai_blk), (got_up_1, got['w_out'], got['glu_w']) = _ssm_bwd(
        dy_perm, u_perm, s_re, s_im, big_b_re, big_b_im, big_c_re, big_c_im, lam_r, lam_i,
        ([g_w_up_halves[1], g_w_out.reshape(N_DEV, D_MODEL // N_DEV, D_MODEL),
          g_glu_w.reshape(N_DEV, D_SSM // N_DEV, D_SSM)], True))
    du_ssm = _from_scan_rows(du_perm)
    dproj = _mix_bwd_proj(dconv, proj, du_ssm, dy, ssm_d, dbg, cw_full, tm)
    dbb_re = _diag_blocks(dbr_blk, True).reshape(N_GROUPS, -1)
    dbb_im = _diag_blocks(dbi_blk, True).reshape(N_GROUPS, -1)
    d_c_re = _diag_blocks(dcr_blk, False).transpose(0, 2, 1)
    d_c_im = _diag_blocks(dci_blk, False).transpose(0, 2, 1)
    lane = jnp.arange(SSM_STATE * SSM_GROUP)
    seg = jnp.where(lane[:, None] // SSM_GROUP == lane[None, :] // SSM_GROUP, 1.0, 0.0).astype(BF16)
    d_b_re_x, d_b_im_x, d_lre_x, d_lim_x, d_lst = _ssm_prep_bwd(
        lre_x, lim_x, lst_x, b_re_x, b_im_x, dbb_re, dbb_im, _expand(dar_blk.reshape(N_GROUPS, SSM_STATE)),
        _expand(dai_blk.reshape(N_GROUPS, SSM_STATE)), seg)

    row = lambda a: a.reshape(-1, PACK_COLS)
    blank = jnp.zeros((1, PACK_COLS), F32)
    small_pack = jnp.concatenate([
        d_b_re_x, d_b_im_x, row(d_c_re), row(d_c_im), blank, blank, d_gt1, d_sh2, d_sc2, d_gt2, blank,
        d_g_post_mix, row(d_lre_x[:, ::SSM_GROUP]), row(d_lim_x[:, ::SSM_GROUP]),
        jnp.pad(d_lst.reshape(1, N_GROUPS), ((0, 0), (0, PACK_COLS - N_GROUPS))), row(sums[0:4]), d_g_pre_ffn,
        d_g_post_ffn, jnp.zeros((SMALL_ROWS - 145, PACK_COLS), F32)])
    g_w_in, (small_all,) = _grad_w_in(h1, dproj, tw, ([small_pack], False))
    g_conv_slots = jnp.concatenate([sums[4:7], jnp.zeros((5, D_CONV), F32)]).reshape(
        8, N_DEV, D_CONV // N_DEV).transpose(1, 0, 2)
    (grad_x, d_sh1, d_sc1, d_g_pre_mix), (got['w_in'], got['conv_w']) = _pre_norm_bwd(
        dproj, pl.BlockSpec((tw, 4 * IN_SHARD), lambda i, j: (i, j)), w_in_s, xt, dx1, sc1, g_pre_mix, tw,
        'mix_in_bwd', ([g_w_in, g_conv_slots], True), group=4)
    late_pack = jnp.concatenate([d_sh1, d_sc1, d_g_pre_mix, jnp.full((1, PACK_COLS), loss_local, F32),
                                 jnp.zeros((4, PACK_COLS), F32)])
    (late_all,) = _exchange([late_pack], name='gather_late_grads', scatter=False)
    loss = jnp.sum(late_all[:, 3, 0])
    res = _adamw_small(small_all, late_all, wts, mom_m, mom_v)

    dmod_all = jnp.concatenate([late_all[:, 0:2, :], small_all[:, B_ADA_ROW + 2:B_ADA_ROW + N_MOD, :]],
                               axis=1).reshape(N_DEV, N_MOD * D_MODEL)
    dmod_cols = lax.dynamic_slice(dmod_all, (0, me * ADA_SHARD), (N_DEV, ADA_SHARD))
    g_w_ada = _grad_w_ada(c_act.T, dmod_cols)

    pieces = {n: [slots[:, :3, :] if n in ('conv_w', 'ffn_conv_w') else slots] for n, slots in got.items()}
    pieces['w_up'] = [got_up_0, got_up_1]
    for n, parts in pieces.items():
        outs = _adamw(parts, wts[n][0], mom_m[n][0], mom_v[n][0], 'adamw_' + n)
        for kind, val in zip(('g', 'd', 'm', 'v'), outs):
            res[kind, n] = val[None]
    outs = _adamw([g_w_ada[None]], w_ada[0], m_w_ada[0], v_w_ada[0], 'adamw_w_ada')
    for kind, val in zip(('g', 'd', 'm', 'v'), outs):
        res[kind, 'w_ada'] = val[None]

    return (loss, grad_x[None], *[res['g', n] for n in WEIGHTS], *[res['d', n] for n in WEIGHTS],
            *[res['m', n] for n in WEIGHTS], *[res['v', n] for n in WEIGHTS])
```

```python
import math

import jax
import jax.numpy as jnp
from jax import lax
from jax.experimental import pallas as pl
from jax.experimental.pallas import tpu as pltpu

F32, BF16 = jnp.float32, jnp.bfloat16

D_MODEL = 1024
D_SSM = 512
D_CONV = 512
SSM_GROUP = 16
N_GROUPS = 32
SSM_STATE = 64
N_STATE = N_GROUPS * SSM_STATE
CONV_HEADS = 8
D_FF = 2816
N_MOD = 6
D_IN_PROJ = D_SSM + 3 * D_CONV
N_DEV = 8
FF_SHARD = 2 * D_FF // N_DEV
IN_SHARD = D_IN_PROJ // N_DEV
ADA_SHARD = N_MOD * D_MODEL // N_DEV
EPS = 1e-6
LAMBDA_RE_MAX = -1e-4
ADAM_LR, ADAM_B1, ADAM_B2, ADAM_EPS, ADAM_WD, ADAM_STEP = 0.001, 0.9, 0.999, 1e-08, 0.01, 10
GELU_C = math.sqrt(2.0 / math.pi)
GELU_A = 0.044715

SUBLANES = 8
HALO = 8
HALO16 = 16
SCAN_UNROLL = 8
STATE_BLOCK = 256
CHAN_BLOCK = 128
VMEM_BIG = 48 << 20
VMEM_MOST = 58 << 20

WEIGHTS = ['w_ada', 'b_ada', 'g_pre_mix', 'g_post_mix', 'w_in', 'ssm_lam_re', 'ssm_lam_im', 'ssm_log_step',
           'ssm_b_re', 'ssm_b_im', 'ssm_c_re', 'ssm_c_im', 'ssm_d', 'glu_w', 'glu_b', 'g_out_ssm', 'conv_w',
           'g_out_conv', 'w_out', 'g_pre_ffn', 'g_post_ffn', 'w_up', 'ffn_conv_w', 'w_down']
SHARDED = ('w_ada', 'w_in', 'glu_w', 'conv_w', 'w_out', 'w_up', 'ffn_conv_w', 'w_down')
PACK_COLS = 1024


def _call(body, *, name, grid, in_specs, out_specs, out_shape, scratch=(), sem=None, vmem=None, ride=None):
    params = {}
    if vmem is not None:
        params['vmem_limit_bytes'] = vmem
    if ride is None:
        if sem is not None:
            params['dimension_semantics'] = sem
        return pl.pallas_call(body, name=name, grid=grid, in_specs=in_specs, out_specs=out_specs,
                              out_shape=out_shape, scratch_shapes=list(scratch),
                              compiler_params=pltpu.CompilerParams(**params))
    arrs, scatter = ride
    single = not isinstance(out_shape, (list, tuple))
    out_shape_l = [out_shape] if single else list(out_shape)
    out_specs_l = [out_specs] if single else list(out_specs)
    n, n_in, n_out, n_scr = len(arrs), len(in_specs), len(out_shape_l), len(scratch)
    any_spec = pl.BlockSpec(memory_space=pl.ANY)
    params['dimension_semantics'] = ('arbitrary',) * len(grid)

    def carried(*refs):
        ins, rin = refs[:n_in], refs[n_in:n_in + n]
        outs, rout = refs[n_in + n:n_in + n + n_out], refs[n_in + n + n_out:n_in + 2 * n + n_out]
        scr, sems = refs[n_in + 2 * n + n_out:n_in + 2 * n + n_out + n_scr], refs[n_in + 2 * n + n_out + n_scr:]
        first = pl.program_id(0) == 0
        last = pl.program_id(0) == grid[0] - 1
        for ax in range(1, len(grid)):
            first = jnp.logical_and(first, pl.program_id(ax) == 0)
            last = jnp.logical_and(last, pl.program_id(ax) == grid[ax] - 1)

        @pl.when(first)
        def _():
            _exchange_start(rin, rout, sems, scatter)

        body(*ins, *outs, *scr)

        @pl.when(last)
        def _():
            _exchange_wait(rin, rout, sems, scatter)

    call = pl.pallas_call(carried, name=name, grid=grid, in_specs=list(in_specs) + [any_spec] * n,
                          out_specs=out_specs_l + [any_spec] * n,
                          out_shape=out_shape_l + _exchange_shapes(arrs, scatter),
                          scratch_shapes=list(scratch) + _exchange_sems(n),
                          compiler_params=pltpu.CompilerParams(**params))

    def run(*args):
        res = call(*args, *arrs)
        own = res[0] if single else list(res[:n_out])
        return own, list(res[n_out:])

    return run


def _const(shape):
    nd = len(shape)
    return pl.BlockSpec(shape, lambda *_: (0,) * nd)


def _sds(shape, dtype=F32):
    return jax.ShapeDtypeStruct(shape, dtype)


def _dot(a, b):
    return jnp.dot(a, b, preferred_element_type=F32)


def _dot_nt(a, b):
    return lax.dot_general(a, b, (((1,), (1,)), ((), ())), preferred_element_type=F32)


def _dot_tn(a, b):
    return lax.dot_general(a, b, (((0,), (0,)), ((), ())), preferred_element_type=F32)


def _dot_split(x, mat, parts):
    acc = None
    rem = x
    for _ in range(parts):
        piece = rem.astype(BF16)
        rem = rem - piece.astype(F32)
        term = _dot(piece, mat)
        acc = term if acc is None else acc + term
    return acc


def _sigmoid(x):
    return 1.0 / (1.0 + jnp.exp(-x))


def _gelu(x):
    t = jnp.tanh(GELU_C * (x + GELU_A * x * x * x))
    return 0.5 * x * (1.0 + t), t


def _gelu_grad(x, t):
    return 0.5 * (1.0 + t) + 0.5 * x * (1.0 - t * t) * GELU_C * (1.0 + 3.0 * GELU_A * x * x)


def _rsqrt_mean(x):
    return lax.rsqrt(jnp.mean(x * x, axis=-1, keepdims=True) + EPS)


def _colsum(x):
    return jnp.sum(x, axis=0, keepdims=True)


def _shifts_down(x, halo):
    ext = jnp.concatenate([halo, x], axis=0)
    return pltpu.roll(ext, 1, 0)[halo.shape[0]:], pltpu.roll(ext, 2, 0)[halo.shape[0]:]


def _shifts_up(x, halo):
    n = x.shape[0]
    ext = jnp.concatenate([x, halo], axis=0)
    total = ext.shape[0]
    return pltpu.roll(ext, total - 1, 0)[:n], pltpu.roll(ext, total - 2, 0)[:n]


def _conv3(x, halo, w_ref):
    x1, x2 = _shifts_down(x, halo)
    return w_ref[0:1, :] * x2 + w_ref[1:2, :] * x1 + w_ref[2:3, :] * x, x1, x2


def _conv3_t(g, halo, w_ref):
    g1, g2 = _shifts_up(g, halo)
    return w_ref[2:3, :] * g + w_ref[1:2, :] * g1 + w_ref[0:1, :] * g2, g1, g2


def _silu_parts(x):
    s = _sigmoid(x)
    return x * s, s * (1.0 + x * (1.0 - s))


def _norm_bwd(dn, x, r, g):
    gd = g * dn
    return r * gd - x * (r * r * r) * jnp.mean(gd * x, axis=-1, keepdims=True)


def _head_norm_bwd(dn, y, rs, g, avg):
    gd = g * dn
    return rs * gd - y * (rs * rs * rs) * _dot_split(gd * y, avg, 2)


def _me():
    x, y, c = lax.axis_index('x'), lax.axis_index('y'), lax.axis_index('c')
    return x, y, c, 4 * x + 2 * y + c


def _peer(k):
    x, y, c, _ = _me()
    px = 1 - x if k & 4 else x
    py = 1 - y if k & 2 else y
    pc = 1 - c if k & 1 else c
    return (px, py, pc), 4 * px + 2 * py + pc


SIBLING = 1
OTHER_CHIPS = (2, 4, 6)


def _remote(src, dst, sems, a, k, dev):
    return pltpu.make_async_remote_copy(src_ref=src, dst_ref=dst, send_sem=sems[0].at[a, k - 1],
                                        recv_sem=sems[1].at[a, k - 1], device_id=dev,
                                        device_id_type=pl.DeviceIdType.MESH)


def _exchange_copies(ins, outs, sems, scatter):
    me = _me()[3]
    local, first, relay, arrivals = [], [], [], []
    for a in range(len(ins)):
        src = ins[a].at[me] if scatter else ins[a]
        local.append(pltpu.make_async_copy(src, outs[a].at[me], sems[2].at[a]))
        for k in range(1, N_DEV):
            dev, idx = _peer(k)
            landed = _remote(src, outs[a].at[idx], sems, a, k, dev)
            if scatter:
                first.append(_remote(ins[a].at[idx], outs[a].at[me], sems, a, k, dev))
                arrivals.append(landed)
            elif k == SIBLING:
                first.append(_remote(src, outs[a].at[me], sems, a, k, dev))
                arrivals.append(landed)
            elif k in OTHER_CHIPS:
                first.append(_remote(src, outs[a].at[me], sems, a, k, dev))
                sib, _ = _peer(SIBLING)
                relay.append((landed, _remote(outs[a].at[idx], outs[a].at[idx], sems, a, k | SIBLING, sib)))
            else:
                arrivals.append(landed)
    return local, first, relay, arrivals


def _exchange_start(ins, outs, sems, scatter):
    local, first, _, _ = _exchange_copies(ins, outs, sems, scatter)
    for cp in local + first:
        cp.start()


def _exchange_wait(ins, outs, sems, scatter):
    local, first, relay, arrivals = _exchange_copies(ins, outs, sems, scatter)
    for landed, forward in relay:
        landed.wait_recv()
        forward.start()
    for cp in arrivals:
        cp.wait_recv()
    for cp in first + [forward for _, forward in relay]:
        cp.wait_send()
    for cp in local:
        cp.wait()


def _exchange_shapes(arrs, scatter):
    return [_sds(a.shape if scatter else (N_DEV,) + a.shape, a.dtype) for a in arrs]


def _exchange_sems(n):
    return [pltpu.SemaphoreType.DMA((n, N_DEV - 1)), pltpu.SemaphoreType.DMA((n, N_DEV - 1)),
            pltpu.SemaphoreType.DMA((n,))]


def _exchange(arrs, *, name, scatter):
    n = len(arrs)

    def body(*refs):
        _exchange_start(refs[:n], refs[n:2 * n], refs[2 * n:], scatter)
        _exchange_wait(refs[:n], refs[n:2 * n], refs[2 * n:], scatter)

    any_spec = pl.BlockSpec(memory_space=pl.ANY)
    outs = pl.pallas_call(body, name=name, out_shape=_exchange_shapes(arrs, scatter), in_specs=[any_spec] * n,
                          out_specs=[any_spec] * n, scratch_shapes=_exchange_sems(n))(*arrs)
    return list(outs)


def _mod_cols(c_all, w_ada, b_cols):
    def body(c_ref, w_ref, b_ref, mod_ref, act_ref):
        c = c_ref[...]
        act = c * _sigmoid(c)
        act_ref[...] = act
        mod_ref[...] = _dot(act.astype(BF16), w_ref[...].astype(BF16)) + b_ref[...]

    return _call(body, name='mod_cols', grid=(1,),
                 in_specs=[_const(c_all.shape), _const(w_ada.shape), _const(b_cols.shape)],
                 out_specs=[_const((N_DEV, ADA_SHARD)), _const(c_all.shape)],
                 out_shape=[_sds((N_DEV, ADA_SHARD)), _sds(c_all.shape)], vmem=VMEM_BIG)(c_all, w_ada, b_cols)


def _grad_w_ada(act_t, dmod_cols):
    def body(a_ref, d_ref, o_ref):
        o_ref[...] = _dot(a_ref[...], d_ref[...])

    return _call(body, name='grad_w_ada', grid=(1,), in_specs=[_const(act_t.shape), _const(dmod_cols.shape)],
                 out_specs=_const((D_MODEL, ADA_SHARD)), out_shape=_sds((D_MODEL, ADA_SHARD)),
                 vmem=VMEM_BIG)(act_t, dmod_cols)


def _pre_mix(x, sc, sh, g, w_s, tm, ride):
    T = x.shape[0]

    def body(x_ref, sc_ref, sh_ref, g_ref, w_ref, proj_ref, h_ref):
        @pl.when(pl.program_id(1) == 0)
        def _():
            xv = x_ref[...]
            h_ref[...] = ((xv * _rsqrt_mean(xv) * g_ref[...]) * (1.0 + sc_ref[...]) + sh_ref[...]).astype(BF16)

        for s in range(2):
            proj_ref[:, s * IN_SHARD:(s + 1) * IN_SHARD] = _dot(h_ref[...], w_ref[s])

    row = pl.BlockSpec((tm, D_MODEL), lambda i, j: (i, 0))
    vec = _const((1, D_MODEL))
    return _call(body, name='pre_mix', grid=(T // tm, N_DEV // 2),
                 in_specs=[row, vec, vec, vec, pl.BlockSpec((2, D_MODEL, IN_SHARD), lambda i, j: (j, 0, 0))],
                 out_specs=[pl.BlockSpec((tm, 2 * IN_SHARD), lambda i, j: (i, j)), row],
                 out_shape=[_sds((T, D_IN_PROJ)), _sds((T, D_MODEL), BF16)],
                 sem=('parallel', 'arbitrary'), ride=ride)(x, sc, sh, g, w_s)


def _halo_before(tm, rows=HALO):
    return lambda i: jnp.maximum(i * (tm // rows) - 1, 0)


def _halo_after(tm, T, rows=HALO):
    return lambda i: jnp.minimum((i + 1) * (tm // rows), T // rows - 1)


def _mix_fwd(yssm, proj, d, glu_w, glu_b, g_ssm, cw, g_conv, avg16, avg64, tm):
    T = yssm.shape[0]
    hb = _halo_before(tm)

    def body(y_ref, p_ref, ph_ref, d_ref, gw_ref, gb_ref, gs_ref, cw_ref, gc_ref, a16_ref, a64_ref, o_ref):
        i = pl.program_id(0)
        u = p_ref[:, 0:D_SSM]
        y = y_ref[...] + d_ref[...] * u
        z, _ = _gelu(y)
        gate = _sigmoid(_dot(z.astype(BF16), gw_ref[...]) + gb_ref[...])
        ya = z * gate
        rs = lax.rsqrt(_dot_split(ya * ya, a16_ref[...], 2) + EPS)
        o_ref[:, 0:D_SSM] = (ya * rs * gs_ref[...]).astype(BF16)
        bg = p_ref[:, D_SSM:D_SSM + D_CONV]
        cv = p_ref[:, D_SSM + D_CONV:D_SSM + 2 * D_CONV] * p_ref[:, D_SSM + 2 * D_CONV:D_IN_PROJ]
        hv = ph_ref[:, D_SSM + D_CONV:D_SSM + 2 * D_CONV] * ph_ref[:, D_SSM + 2 * D_CONV:D_IN_PROJ]
        hv = jnp.where(i > 0, hv, 0.0)
        conv, _, _ = _conv3(cv, hv, cw_ref)
        yb = bg * conv
        rsb = lax.rsqrt(_dot_split(yb * yb, a64_ref[...], 2) + EPS)
        o_ref[:, D_SSM:D_MODEL] = (yb * rsb * gc_ref[...]).astype(BF16)

    vec = _const((1, D_SSM))
    sq = _const((D_SSM, D_SSM))
    return _call(body, name='mix_fwd', grid=(T // tm,),
                 in_specs=[pl.BlockSpec((tm, D_SSM), lambda i: (i, 0)), pl.BlockSpec((tm, D_IN_PROJ), lambda i: (i, 0)),
                           pl.BlockSpec((HALO, D_IN_PROJ), lambda i: (hb(i), 0)), vec, sq, vec, vec,
                           _const((3, D_CONV)), vec, sq, sq],
                 out_specs=pl.BlockSpec((tm, D_MODEL), lambda i: (i, 0)), out_shape=_sds((T, D_MODEL), BF16),
                 sem=('parallel',), vmem=VMEM_BIG)(yssm, proj, proj, d, glu_w, glu_b, g_ssm, cw, g_conv, avg16, avg64)


def _out_proj(ycat, w_out, x, gt, g_post, g_pre, sc, sh, tm):
    T = x.shape[0]

    def body(y_ref, w_ref, x_ref, gt_ref, gp_ref, g2_ref, sc_ref, sh_ref, o_ref, x1_ref, h_ref):
        o = _dot(y_ref[...], w_ref[...])
        o_ref[...] = o
        x1 = x_ref[...] + gt_ref[...] * (o * _rsqrt_mean(o) * gp_ref[...])
        x1_ref[...] = x1
        h_ref[...] = ((x1 * _rsqrt_mean(x1) * g2_ref[...]) * (1.0 + sc_ref[...]) + sh_ref[...]).astype(BF16)

    row = pl.BlockSpec((tm, D_MODEL), lambda i: (i, 0))
    vec = _const((1, D_MODEL))
    return _call(body, name='out_proj', grid=(T // tm,),
                 in_specs=[row, _const((D_MODEL, D_MODEL)), row, vec, vec, vec, vec, vec],
                 out_specs=[row, row, row],
                 out_shape=[_sds((T, D_MODEL)), _sds((T, D_MODEL)), _sds((T, D_MODEL), BF16)],
                 sem=('parallel',), vmem=VMEM_BIG)(ycat, w_out, x, gt, g_post, g_pre, sc, sh)


def _ffn_up(h2, w_s, cw8, tm):
    T = h2.shape[0]
    hb = _halo_before(tm, HALO16)

    def body(h_ref, hh_ref, w_ref, cw_ref, up_ref, hid_ref):
        up = _dot(h_ref[...], w_ref[...])
        up_ref[...] = up.astype(BF16)
        before = jnp.where(pl.program_id(0) > 0, _dot(hh_ref[...], w_ref[...]), 0.0)
        hid_ref[...] = _conv3(up, before, cw_ref)[0].astype(BF16)

    out = pl.BlockSpec((None, tm, FF_SHARD), lambda i, j: (j, i, 0))
    return _call(body, name='ffn_up', grid=(T // tm, N_DEV),
                 in_specs=[pl.BlockSpec((tm, D_MODEL), lambda i, j: (i, 0)),
                           pl.BlockSpec((HALO16, D_MODEL), lambda i, j: (hb(i), 0)),
                           pl.BlockSpec((None, D_MODEL, FF_SHARD), lambda i, j: (j, 0, 0)),
                           pl.BlockSpec((None, 3, FF_SHARD), lambda i, j: (j, 0, 0))],
                 out_specs=[out, out], out_shape=[_sds((N_DEV, T, FF_SHARD), BF16)] * 2,
                 sem=('parallel', 'parallel'))(h2, h2, w_s, cw8)


def _ffn_down(hid4, wd4, x1, tgt, gt, g_post, tm):
    T = x1.shape[0]
    nb = T // tm

    def body(a_ref, w_ref, x1_ref, t_ref, gt_ref, g_ref, ddn_ref, dx_ref, loss_ref, dgt_ref, dg_ref, dn_ref):
        i, j = pl.program_id(0), pl.program_id(1)
        part = None
        for s in range(2):
            act = (_silu_parts(a_ref[0, s].astype(F32))[0] * a_ref[1, s].astype(F32)).astype(BF16)
            term = _dot(act, w_ref[s])
            part = term if part is None else part + term

        @pl.when(jnp.logical_and(i == 0, j == 0))
        def _():
            dgt_ref[...] = jnp.zeros_like(dgt_ref)
            dg_ref[...] = jnp.zeros_like(dg_ref)

        @pl.when(j == 0)
        def _():
            dn_ref[...] = part

        @pl.when(j > 0)
        def _():
            dn_ref[...] += part

        @pl.when(j == 1)
        def _():
            dn, gv, gate = dn_ref[...], g_ref[...], gt_ref[...]
            r = _rsqrt_mean(dn)
            normed = dn * r * gv
            err = x1_ref[...] + gate * normed - t_ref[...]
            dx = err * (1.0 / D_MODEL)
            dx_ref[...] = dx
            tot = jnp.sum(jnp.sum(err * err, axis=1, keepdims=True), axis=0, keepdims=True) * (0.5 / D_MODEL)
            loss_ref[...] = jnp.broadcast_to(tot, (8, 128))
            dgt_ref[...] += _colsum(dx * normed)
            dnn = dx * gate
            dg_ref[...] += _colsum(dnn * dn * r)
            ddn_ref[...] = _norm_bwd(dnn, dn, r, gv).astype(BF16)

    row = pl.BlockSpec((tm, D_MODEL), lambda i, j: (i, 0))
    vec = _const((1, D_MODEL))
    return _call(body, name='ffn_down', grid=(nb, 2),
                 in_specs=[pl.BlockSpec((2, 2, tm, FF_SHARD), lambda i, j: (0, j, i, 0)),
                           pl.BlockSpec((2, FF_SHARD, D_MODEL), lambda i, j: (j, 0, 0)), row, row, vec, vec],
                 out_specs=[row, row, pl.BlockSpec((None, 8, 128), lambda i, j: (i, 0, 0)), vec, vec],
                 out_shape=[_sds((T, D_MODEL), BF16), _sds((T, D_MODEL)), _sds((nb, 8, 128)), _sds((1, D_MODEL)),
                            _sds((1, D_MODEL))],
                 scratch=[pltpu.VMEM((tm, D_MODEL), F32)], sem=('arbitrary', 'arbitrary'),
                 vmem=VMEM_BIG)(hid4, wd4, x1, tgt, gt, g_post)


def _ssm_prep(lre, lim, lst, b_re, b_im):
    def body(lre_ref, lim_ref, lst_ref, br_ref, bi_ref, ar_ref, ai_ref, bbr_ref, bbi_ref):
        ar, ai, qr, qi = _zoh(lre_ref[...], lim_ref[...], lst_ref[...])[:4]
        ar_ref[...] = ar
        ai_ref[...] = ai
        bbr_ref[...] = qr * br_ref[...] - qi * bi_ref[...]
        bbi_ref[...] = qr * bi_ref[...] + qi * br_ref[...]

    shp = lre.shape
    return _call(body, name='ssm_prep', grid=(1,), in_specs=[_const(shp)] * 5, out_specs=[_const(shp)] * 4,
                 out_shape=[_sds(shp)] * 4)(lre, lim, lst, b_re, b_im)


def _zoh(lre, lim, lst):
    lr = jnp.minimum(lre, LAMBDA_RE_MAX)
    st = jnp.exp(lst)
    mag = jnp.exp(lr * st)
    ar = mag * jnp.cos(lim * st)
    ai = mag * jnp.sin(lim * st)
    den = lr * lr + lim * lim
    qr = ((ar - 1.0) * lr + ai * lim) / den
    qi = (ai * lr - (ar - 1.0) * lim) / den
    return ar, ai, qr, qi, lr, st, den


def _ssm_prep_bwd(lre, lim, lst, b_re, b_im, dbbr, dbbi, dar, dai, seg):
    def body(lre_ref, lim_ref, lst_ref, br_ref, bi_ref, dbbr_ref, dbbi_ref, dar_ref, dai_ref, seg_ref,
             dbr_ref, dbi_ref, dlre_ref, dlim_ref, dlst_ref):
        lre_v = lre_ref[...]
        li = lim_ref[...]
        ar, ai, qr, qi, lr, st, den = _zoh(lre_v, li, lst_ref[...])
        br, bi, gbr, gbi = br_ref[...], bi_ref[...], dbbr_ref[...], dbbi_ref[...]
        dbr_ref[...] = qr * gbr + qi * gbi
        dbi_ref[...] = qr * gbi - qi * gbr
        gqr = _dot_split(br * gbr + bi * gbi, seg_ref[...], 3)
        gqi = _dot_split(br * gbi - bi * gbr, seg_ref[...], 3)
        ir, ii = lr / den, -li / den
        gar = dar_ref[...] + ir * gqr + ii * gqi
        gai = dai_ref[...] + ir * gqi - ii * gqr
        tr, ti = qr * ir - qi * ii, qr * ii + qi * ir
        glr = -(tr * gqr + ti * gqi)
        gli = -(tr * gqi - ti * gqr)
        gzr = ar * gar + ai * gai
        gzi = ar * gai - ai * gar
        glr = glr + st * gzr
        gli = gli + st * gzi
        gst = (lr * gzr + li * gzi) * st
        dlre_ref[...] = jnp.where(lre_v < LAMBDA_RE_MAX, glr, 0.0)
        dlim_ref[...] = gli
        dlst_ref[...] = jnp.sum(gst, axis=1, keepdims=True) * (1.0 / SSM_GROUP)

    shp = lre.shape
    return _call(body, name='ssm_prep_bwd', grid=(1,), in_specs=[_const(shp)] * 9 + [_const(seg.shape)],
                 out_specs=[_const(shp)] * 4 + [_const((N_GROUPS, 1))],
                 out_shape=[_sds(shp)] * 4 + [_sds((N_GROUPS, 1))], vmem=VMEM_BIG)(
                     lre, lim, lst, b_re, b_im, dbbr, dbbi, dar, dai, seg)


def _scan_specs(T):
    half = lambda cb: cb // 2
    return dict(
        chan=pl.BlockSpec((T, CHAN_BLOCK), lambda cb: (0, half(cb))),
        state=pl.BlockSpec((T, STATE_BLOCK), lambda cb: (0, cb)),
        b=pl.BlockSpec((CHAN_BLOCK, STATE_BLOCK), lambda cb: (half(cb), cb)),
        c=pl.BlockSpec((STATE_BLOCK, CHAN_BLOCK), lambda cb: (cb, half(cb))),
        lam=pl.BlockSpec((1, STATE_BLOCK), lambda cb: (0, cb)),
    )


def _complex_power(re, im, n):
    out = None
    while True:
        if n & 1:
            out = (re, im) if out is None else (out[0] * re - out[1] * im, out[0] * im + out[1] * re)
        n >>= 1
        if n == 0:
            return out
        re, im = re * re - im * im, 2.0 * re * im


def _rows8(i):
    if isinstance(i, int):
        return pl.ds(i * SUBLANES, SUBLANES)
    return pl.ds(pl.multiple_of(i * SUBLANES, SUBLANES), SUBLANES)


def _scan_loop(n_steps, body, init):
    trips = n_steps // SCAN_UNROLL

    def trip(t, carry):
        for u in range(SCAN_UNROLL):
            carry = body(t * SCAN_UNROLL + u, carry)
        return carry

    carry = lax.fori_loop(0, trips, trip, init)
    for step in range(trips * SCAN_UNROLL, n_steps):
        carry = body(step, carry)
    return carry


def _ssm_fwd(u_perm, b_re, b_im, c_re, c_im, lam_r, lam_i, ride):
    T = u_perm.shape[0]
    ls = T // SUBLANES
    rc = min(512, T)
    sp = _scan_specs(T)

    def body(u_ref, bre_ref, bim_ref, cre_ref, cim_ref, lr_ref, li_ref, sre_ref, sim_ref, y_ref):
        cb = pl.program_id(0)
        for c in range(T // rc):
            rows = pl.ds(c * rc, rc)
            sre_ref[rows, :] = _dot(u_ref[rows, :], bre_ref[...])
            sim_ref[rows, :] = _dot(u_ref[rows, :], bim_ref[...])
        shp = (SUBLANES, STATE_BLOCK)
        lr = jnp.broadcast_to(lr_ref[...], shp)
        li = jnp.broadcast_to(li_ref[...], shp)
        zero = jnp.zeros(shp, F32)

        def step(i, carry):
            sr, si = carry
            rows = _rows8(i)
            nr = lr * sr - li * si + sre_ref[rows, :]
            ni = lr * si + li * sr + sim_ref[rows, :]
            sre_ref[rows, :] = nr
            sim_ref[rows, :] = ni
            return nr, ni

        fr, fi = _scan_loop(ls, step, (zero, zero))
        pr, pi_ = _complex_power(lr, li, ls)
        row = lax.broadcasted_iota(jnp.int32, shp, 0)
        ir, ii = zero, zero
        for _ in range(SUBLANES - 1):
            er = fr + pr * ir - pi_ * ii
            ei = fi + pr * ii + pi_ * ir
            ir = jnp.where(row == 0, 0.0, pltpu.roll(er, 1, 0))
            ii = jnp.where(row == 0, 0.0, pltpu.roll(ei, 1, 0))

        def fix(i, carry):
            cr, ci = carry
            rows = _rows8(i)
            nr = lr * cr - li * ci
            ni = lr * ci + li * cr
            sre_ref[rows, :] += nr
            sim_ref[rows, :] += ni
            return nr, ni

        _scan_loop(ls, fix, (ir, ii))
        for c in range(T // rc):
            rows = pl.ds(c * rc, rc)
            yc = _dot(sre_ref[rows, :].astype(BF16), cre_ref[...]) - _dot(sim_ref[rows, :].astype(BF16), cim_ref[...])

            @pl.when(cb % 2 == 0)
            def _():
                y_ref[rows, :] = yc

            @pl.when(cb % 2 == 1)
            def _():
                y_ref[rows, :] += yc

    return _call(body, name='ssm_fwd', grid=(N_STATE // STATE_BLOCK,),
                 in_specs=[sp['chan'], sp['b'], sp['b'], sp['c'], sp['c'], sp['lam'], sp['lam']],
                 out_specs=[sp['state'], sp['state'], sp['chan']],
                 out_shape=[_sds((T, N_STATE)), _sds((T, N_STATE)), _sds((T, D_SSM))],
                 sem=('arbitrary',), vmem=VMEM_BIG, ride=ride)(u_perm, b_re, b_im, c_re, c_im, lam_r, lam_i)


def _ssm_bwd(dy_perm, u_perm, s_re, s_im, b_re, b_im, c_re, c_im, lam_r, lam_i, ride):
    T = u_perm.shape[0]
    ls = T // SUBLANES
    rc = min(512, T)
    sp = _scan_specs(T)
    ncb = N_STATE // STATE_BLOCK

    def body(dy_ref, u_ref, sre_ref, sim_ref, bre_ref, bim_ref, cre_ref, cim_ref, lr_ref, li_ref,
             du_ref, dbr_ref, dbi_ref, dcr_ref, dci_ref, dar_ref, dai_ref, gre_ref, gim_ref):
        cb = pl.program_id(0)
        for c in range(T // rc):
            rows = pl.ds(c * rc, rc)
            gre_ref[rows, :] = _dot_nt(dy_ref[rows, :], cre_ref[...])
            gim_ref[rows, :] = -_dot_nt(dy_ref[rows, :], cim_ref[...])
        shp = (SUBLANES, STATE_BLOCK)
        lr = jnp.broadcast_to(lr_ref[...], shp)
        li = jnp.broadcast_to(li_ref[...], shp)
        zero = jnp.zeros(shp, F32)

        def step(k, carry):
            gr, gi = carry
            rows = _rows8(ls - 1 - k)
            nr = lr * gr + li * gi + gre_ref[rows, :]
            ni = lr * gi - li * gr + gim_ref[rows, :]
            gre_ref[rows, :] = nr
            gim_ref[rows, :] = ni
            return nr, ni

        fr, fi = _scan_loop(ls, step, (zero, zero))
        pr, pi_ = _complex_power(lr, -li, ls)
        row = lax.broadcasted_iota(jnp.int32, shp, 0)
        cr, ci = zero, zero
        for _ in range(SUBLANES - 1):
            er = fr + pr * cr - pi_ * ci
            ei = fi + pr * ci + pi_ * cr
            cr = jnp.where(row == SUBLANES - 1, 0.0, pltpu.roll(er, SUBLANES - 1, 0))
            ci = jnp.where(row == SUBLANES - 1, 0.0, pltpu.roll(ei, SUBLANES - 1, 0))

        def fix(k, carry):
            dr, di, ar, ai = carry
            rows = _rows8(ls - 1 - k)
            dr, di = lr * dr + li * di, lr * di - li * dr
            gr = gre_ref[rows, :] + dr
            gi = gim_ref[rows, :] + di
            gre_ref[rows, :] = gr
            gim_ref[rows, :] = gi
            prev = _rows8(ls - 2 - k)
            spr, spi = sre_ref[prev, :], sim_ref[prev, :]
            return dr, di, ar + gr * spr + gi * spi, ai + gi * spr - gr * spi

        dr, di, ar, ai = _scan_loop(ls - 1, fix, (cr, ci, zero, zero))
        first = pl.ds(0, SUBLANES)
        last = pl.ds((ls - 1) * SUBLANES, SUBLANES)
        gr = gre_ref[first, :] + (lr * dr + li * di)
        gi = gim_ref[first, :] + (lr * di - li * dr)
        gre_ref[first, :] = gr
        gim_ref[first, :] = gi
        spr = jnp.where(row == 0, 0.0, pltpu.roll(sre_ref[last, :], 1, 0))
        spi = jnp.where(row == 0, 0.0, pltpu.roll(sim_ref[last, :], 1, 0))
        dar_ref[...] = _colsum(ar + gr * spr + gi * spi)
        dai_ref[...] = _colsum(ai + gi * spr - gr * spi)

        for c in range(T // rc):
            rows = pl.ds(c * rc, rc)
            g_r, g_i = gre_ref[rows, :].astype(BF16), gim_ref[rows, :].astype(BF16)
            s_r, s_i = sre_ref[rows, :].astype(BF16), sim_ref[rows, :].astype(BF16)
            ub, dyb = u_ref[rows, :], dy_ref[rows, :]
            duc = _dot_nt(g_r, bre_ref[...]) + _dot_nt(g_i, bim_ref[...])
            parts = (_dot_tn(ub, g_r), _dot_tn(ub, g_i), _dot_tn(s_r, dyb), -_dot_tn(s_i, dyb))
            outs = (dbr_ref, dbi_ref, dcr_ref, dci_ref)
            for o_ref, part in zip(outs, parts):
                if c == 0:
                    o_ref[...] = part
                else:
                    o_ref[...] += part

            @pl.when(cb % 2 == 0)
            def _():
                du_ref[rows, :] = duc

            @pl.when(cb % 2 == 1)
            def _():
                du_ref[rows, :] += duc

    blk = lambda r, c: pl.BlockSpec((None, r, c), lambda cb: (cb, 0, 0))
    return _call(body, name='ssm_bwd', grid=(ncb,),
                 in_specs=[sp['chan'], sp['chan'], sp['state'], sp['state'], sp['b'], sp['b'], sp['c'], sp['c'],
                           sp['lam'], sp['lam']],
                 out_specs=[sp['chan'], blk(CHAN_BLOCK, STATE_BLOCK), blk(CHAN_BLOCK, STATE_BLOCK),
                            blk(STATE_BLOCK, CHAN_BLOCK), blk(STATE_BLOCK, CHAN_BLOCK), blk(1, STATE_BLOCK),
                            blk(1, STATE_BLOCK)],
                 out_shape=[_sds((T, D_SSM)), _sds((ncb, CHAN_BLOCK, STATE_BLOCK)), _sds((ncb, CHAN_BLOCK, STATE_BLOCK)),
                            _sds((ncb, STATE_BLOCK, CHAN_BLOCK)), _sds((ncb, STATE_BLOCK, CHAN_BLOCK)),
                            _sds((ncb, 1, STATE_BLOCK)), _sds((ncb, 1, STATE_BLOCK))],
                 scratch=[pltpu.VMEM((T, STATE_BLOCK), F32), pltpu.VMEM((T, STATE_BLOCK), F32)],
                 sem=('arbitrary',), vmem=VMEM_BIG, ride=ride)(dy_perm, u_perm, s_re, s_im, b_re, b_im, c_re, c_im,
                                                               lam_r, lam_i)


def _ffn_dact(ddn, wd4, hid4, tm):
    T = ddn.shape[0]

    def body(d_ref, w_ref, hid_ref, o_ref, act_ref):
        dact = _dot_nt(d_ref[...], w_ref[...])
        silu, dsilu = _silu_parts(hid_ref[0].astype(F32))
        hid_v = hid_ref[1].astype(F32)
        o_ref[0] = (dact * hid_v * dsilu).astype(BF16)
        o_ref[1] = (dact * silu).astype(BF16)
        act_ref[...] = (silu * hid_v).astype(BF16)

    blk = pl.BlockSpec((2, None, tm, FF_SHARD), lambda i, j: (0, j, i, 0))
    return _call(body, name='ffn_dact', grid=(T // tm, 4),
                 in_specs=[pl.BlockSpec((tm, D_MODEL), lambda i, j: (i, 0)),
                           pl.BlockSpec((None, FF_SHARD, D_MODEL), lambda i, j: (j, 0, 0)), blk],
                 out_specs=[blk, pl.BlockSpec((None, tm, FF_SHARD), lambda i, j: (j, i, 0))],
                 out_shape=[_sds((2, 4, T, FF_SHARD), BF16), _sds((4, T, FF_SHARD), BF16)],
                 sem=('parallel', 'parallel'))(ddn, wd4, hid4)


def _ffn_dup(dhid8, up8, cw8, tm, ride):
    T = up8.shape[1]
    nb = T // tm
    ha = _halo_after(tm, T, HALO16)

    def body(dh_ref, dha_ref, up_ref, cw_ref, dup_ref, dcw_ref):
        i = pl.program_id(1)

        @pl.when(i == 0)
        def _():
            dcw_ref[...] = jnp.zeros_like(dcw_ref)

        dh = dh_ref[...].astype(F32)
        dup, dh1, dh2 = _conv3_t(dh, jnp.where(i < nb - 1, dha_ref[...].astype(F32), 0.0), cw_ref)
        dup_ref[...] = dup.astype(BF16)
        up = up_ref[...].astype(F32)
        dcw_ref[0:1, :] += _colsum(dh2 * up)
        dcw_ref[1:2, :] += _colsum(dh1 * up)
        dcw_ref[2:3, :] += _colsum(dh * up)

    main = pl.BlockSpec((None, tm, FF_SHARD), lambda j, i: (j, i, 0))
    return _call(body, name='ffn_dup', grid=(N_DEV, nb),
                 in_specs=[main, pl.BlockSpec((None, HALO16, FF_SHARD), lambda j, i: (j, ha(i), 0)), main,
                           pl.BlockSpec((None, 3, FF_SHARD), lambda j, i: (j, 0, 0))],
                 out_specs=[main, pl.BlockSpec((None, 8, FF_SHARD), lambda j, i: (j, 0, 0))],
                 out_shape=[_sds((N_DEV, T, FF_SHARD), BF16), _sds((N_DEV, 8, FF_SHARD))],
                 sem=('parallel', 'arbitrary'), ride=ride)(dhid8, dhid8, up8, cw8)


def _grad_tn(a, b, a_spec, b_spec, groups, m, n, tk, name, ride=None, parts=1):
    T = a.shape[-2]
    nk = T // tk
    mp = m // parts

    def body(a_ref, b_ref, *refs):
        o_refs, acc_ref = refs[:parts], refs[parts]
        k = pl.program_id(1)
        part = _dot_tn(a_ref[...], b_ref[...])

        @pl.when(k == 0)
        def _():
            acc_ref[...] = part

        @pl.when(k > 0)
        def _():
            acc_ref[...] += part

        @pl.when(k == nk - 1)
        def _():
            for p, o_ref in enumerate(o_refs):
                o_ref[...] = acc_ref[p * mp:(p + 1) * mp, :].astype(BF16)

    out_spec = pl.BlockSpec((None, mp, n), lambda g, k: (g, 0, 0))
    res = _call(body, name=name, grid=(groups, nk), in_specs=[a_spec, b_spec], out_specs=[out_spec] * parts,
                out_shape=[_sds((groups, mp, n), BF16)] * parts, scratch=[pltpu.VMEM((m, n), F32)],
                sem=('parallel', 'arbitrary'), vmem=VMEM_BIG, ride=ride)(a, b)
    if parts > 1:
        return res
    return res[0] if ride is None else (res[0][0], res[1])


def _grad_w_in(h1, dproj, tk, ride):
    T = h1.shape[0]
    nk = T // tk
    half = D_IN_PROJ // 2

    def body(a_ref, b_ref, o_ref, acc_ref):
        k = pl.program_id(0)
        for h in range(2):
            cols = slice(h * half, (h + 1) * half)
            part = _dot_tn(a_ref[...], b_ref[:, cols])

            @pl.when(k == 0)
            def _():
                acc_ref[:, cols] = part

            @pl.when(k > 0)
            def _():
                acc_ref[:, cols] += part

        @pl.when(k == nk - 1)
        def _():
            for g in range(N_DEV):
                o_ref[g] = acc_ref[:, g * IN_SHARD:(g + 1) * IN_SHARD].astype(BF16)

    return _call(body, name='grad_w_in', grid=(nk,),
                 in_specs=[pl.BlockSpec((tk, D_MODEL), lambda k: (k, 0)), pl.BlockSpec((tk, D_IN_PROJ), lambda k: (k, 0))],
                 out_specs=_const((N_DEV, D_MODEL, IN_SHARD)), out_shape=_sds((N_DEV, D_MODEL, IN_SHARD), BF16),
                 scratch=[pltpu.VMEM((D_MODEL, D_IN_PROJ), F32)], sem=('arbitrary',), vmem=VMEM_BIG, ride=ride)(h1, dproj)


def _pre_norm_bwd(dz, dz_spec, w_s, xin, dres, sc, g, tm, name, ride, below=None, group=1):
    T = xin.shape[0]
    n = w_s.shape[2]
    steps = N_DEV // group

    def body(dz_ref, w_ref, x_ref, dr_ref, sc_ref, g_ref, *refs):
        if below is None:
            dx_ref, dsh_ref, dsc_ref, dg_ref = refs
            sums = (dsh_ref, dsc_ref, dg_ref)
        else:
            v_ref, gate_ref, g2_ref, dx_ref, dsh_ref, dsc_ref, dg_ref, dv_ref, dgate_ref, dg2_ref = refs
            sums = (dsh_ref, dsc_ref, dg_ref, dgate_ref, dg2_ref)
        i, j = pl.program_id(0), pl.program_id(1)
        piece = (lambda s: dz_ref[s]) if dz.ndim == 3 else (lambda s: dz_ref[:, s * n:(s + 1) * n])
        part = _dot_nt(piece(0), w_ref[0])
        for s in range(1, group):
            part = part + _dot_nt(piece(s), w_ref[s])

        @pl.when(jnp.logical_and(i == 0, j == 0))
        def _():
            for s_ref in sums:
                s_ref[...] = jnp.zeros_like(s_ref)

        @pl.when(j == 0)
        def _():
            dx_ref[...] = part

        @pl.when(j > 0)
        def _():
            dx_ref[...] += part

        @pl.when(j == steps - 1)
        def _():
            dh, xv, gv = dx_ref[...], x_ref[...], g_ref[...]
            r = _rsqrt_mean(xv)
            dsh_ref[...] += _colsum(dh)
            dsc_ref[...] += _colsum(dh * (xv * r * gv))
            dxn = dh * (1.0 + sc_ref[...])
            dg_ref[...] += _colsum(dxn * xv * r)
            dx = dr_ref[...] + _norm_bwd(dxn, xv, r, gv)
            dx_ref[...] = dx
            if below is not None:
                v, g2 = v_ref[...], g2_ref[...]
                rv = _rsqrt_mean(v)
                dgate_ref[...] += _colsum(dx * (v * rv * g2))
                dn = dx * gate_ref[...]
                dg2_ref[...] += _colsum(dn * v * rv)
                dv_ref[...] = _norm_bwd(dn, v, rv, g2).astype(BF16)

    row = pl.BlockSpec((tm, D_MODEL), lambda i, j: (i, 0))
    vec = _const((1, D_MODEL))
    in_specs = [dz_spec, pl.BlockSpec((group, D_MODEL, n), lambda i, j: (j, 0, 0)), row, row, vec, vec]
    out_specs = [row, vec, vec, vec]
    out_shape = [_sds((T, D_MODEL)), _sds((1, D_MODEL)), _sds((1, D_MODEL)), _sds((1, D_MODEL))]
    args = [dz, w_s, xin, dres, sc, g]
    if below is not None:
        in_specs += [row, vec, vec]
        out_specs += [row, vec, vec]
        out_shape += [_sds((T, D_MODEL), BF16), _sds((1, D_MODEL)), _sds((1, D_MODEL))]
        args += list(below)
    return _call(body, name=name, grid=(T // tm, steps), in_specs=in_specs, out_specs=out_specs,
                 out_shape=out_shape, sem=('arbitrary', 'arbitrary'), vmem=VMEM_MOST, ride=ride)(*args)


def _mix_bwd(d_o, w_out, yssm, proj, d, glu_w, glu_b, g_ssm, cw, g_conv, avg16, avg64, tm):
    T = yssm.shape[0]
    hb = _halo_before(tm)

    def body(do_ref, wo_ref, y_ref, p_ref, ph_ref, d_ref, gw_ref, gb_ref, gs_ref, cw_ref, gc_ref, a16_ref, a64_ref,
             dy_ref, dconv_ref, dbg_ref, z_ref, dlin_ref, acc_ref):
        i = pl.program_id(0)
        dyc = _dot_nt(do_ref[...], wo_ref[...])

        @pl.when(i == 0)
        def _():
            acc_ref[...] = jnp.zeros_like(acc_ref)

        u = p_ref[:, 0:D_SSM]
        y = y_ref[...] + d_ref[...] * u
        z, t = _gelu(y)
        gate = _sigmoid(_dot(z.astype(BF16), gw_ref[...]) + gb_ref[...])
        ya = z * gate
        rs = lax.rsqrt(_dot_split(ya * ya, a16_ref[...], 2) + EPS)
        dna = dyc[:, 0:D_SSM]
        acc_ref[1:2, :] += _colsum(dna * ya * rs)
        dya = _head_norm_bwd(dna, ya, rs, gs_ref[...], a16_ref[...])
        dlin = dya * z * gate * (1.0 - gate)
        acc_ref[0:1, :] += _colsum(dlin)
        dlin_b = dlin.astype(BF16)
        dz = dya * gate + _dot_nt(dlin_b, gw_ref[...])
        dy = dz * _gelu_grad(y, t)
        acc_ref[3:4, :] += _colsum(dy * u)
        dy_ref[...] = dy
        z_ref[...] = z.astype(BF16)
        dlin_ref[...] = dlin_b

        bg = p_ref[:, D_SSM:D_SSM + D_CONV]
        cv = p_ref[:, D_SSM + D_CONV:D_SSM + 2 * D_CONV] * p_ref[:, D_SSM + 2 * D_CONV:D_IN_PROJ]
        hv = ph_ref[:, D_SSM + D_CONV:D_SSM + 2 * D_CONV] * ph_ref[:, D_SSM + 2 * D_CONV:D_IN_PROJ]
        hv = jnp.where(i > 0, hv, 0.0)
        conv, cv1, cv2 = _conv3(cv, hv, cw_ref)
        yb = bg * conv
        rsb = lax.rsqrt(_dot_split(yb * yb, a64_ref[...], 2) + EPS)
        dnb = dyc[:, D_SSM:D_MODEL]
        acc_ref[2:3, :] += _colsum(dnb * yb * rsb)
        dyb = _head_norm_bwd(dnb, yb, rsb, gc_ref[...], a64_ref[...])
        dbg_ref[...] = dyb * conv
        dconv = dyb * bg
        dconv_ref[...] = dconv
        acc_ref[4:5, :] += _colsum(dconv * cv2)
        acc_ref[5:6, :] += _colsum(dconv * cv1)
        acc_ref[6:7, :] += _colsum(dconv * cv)

    vec = _const((1, D_SSM))
    sq = _const((D_SSM, D_SSM))
    half = pl.BlockSpec((tm, D_SSM), lambda i: (i, 0))
    return _call(body, name='mix_bwd', grid=(T // tm,),
                 in_specs=[pl.BlockSpec((tm, D_MODEL), lambda i: (i, 0)), _const((D_MODEL, D_MODEL)), half,
                           pl.BlockSpec((tm, D_IN_PROJ), lambda i: (i, 0)),
                           pl.BlockSpec((HALO, D_IN_PROJ), lambda i: (hb(i), 0)), vec, sq, vec, vec,
                           _const((3, D_CONV)), vec, sq, sq],
                 out_specs=[half, half, half, half, half, _const((8, D_SSM))],
                 out_shape=[_sds((T, D_SSM)), _sds((T, D_SSM)), _sds((T, D_SSM)), _sds((T, D_SSM), BF16),
                            _sds((T, D_SSM), BF16), _sds((8, D_SSM))],
                 sem=('arbitrary',), vmem=VMEM_BIG)(d_o, w_out, yssm, proj, proj, d, glu_w, glu_b, g_ssm, cw, g_conv,
                                                   avg16, avg64)


def _mix_bwd_proj(dconv, proj, du_ssm, dy, d, dbg, cw, tm):
    T = dy.shape[0]
    nb = T // tm
    ha = _halo_after(tm, T)

    def body(dc_ref, dch_ref, cg_ref, v_ref, du_ref, dy_ref, d_ref, dbg_ref, cw_ref, o_ref):
        i = pl.program_id(0)
        dcv = _conv3_t(dc_ref[...], jnp.where(i < nb - 1, dch_ref[...], 0.0), cw_ref)[0]
        o_ref[:, 0:D_SSM] = (du_ref[...] + dy_ref[...] * d_ref[...]).astype(BF16)
        o_ref[:, D_SSM:D_SSM + D_CONV] = dbg_ref[...].astype(BF16)
        o_ref[:, D_SSM + D_CONV:D_SSM + 2 * D_CONV] = (dcv * v_ref[...]).astype(BF16)
        o_ref[:, D_SSM + 2 * D_CONV:D_IN_PROJ] = (dcv * cg_ref[...]).astype(BF16)

    half = pl.BlockSpec((tm, D_SSM), lambda i: (i, 0))
    return _call(body, name='mix_bwd_proj', grid=(nb,),
                 in_specs=[half, pl.BlockSpec((HALO, D_CONV), lambda i: (ha(i), 0)),
                           pl.BlockSpec((tm, D_CONV), lambda i: (i, 2)), pl.BlockSpec((tm, D_CONV), lambda i: (i, 3)),
                           half, half, _const((1, D_SSM)), half, _const((3, D_CONV))],
                 out_specs=pl.BlockSpec((tm, D_IN_PROJ), lambda i: (i, 0)), out_shape=_sds((T, D_IN_PROJ), BF16),
                 sem=('parallel',))(dconv, dconv, proj, proj, du_ssm, dy, d, dbg, cw)


def _row_tile(rows, cols, slots):
    for cand in (512, 256, 128, 64, 32, 16, 8):
        if rows % cand == 0 and slots * cand * cols * 4 <= (2 << 20):
            return cand
    return rows


def _adamw_math(g, w, m, v):
    m2 = ADAM_B1 * m + (1.0 - ADAM_B1) * g
    v2 = ADAM_B2 * v + (1.0 - ADAM_B2) * (g * g)
    m_hat = m2 / (1.0 - ADAM_B1 ** ADAM_STEP)
    v_hat = v2 / (1.0 - ADAM_B2 ** ADAM_STEP)
    return -ADAM_LR * (m_hat / (jnp.sqrt(v_hat) + ADAM_EPS) + ADAM_WD * w), m2, v2


def _adamw(pieces, w, m, v, name):
    slots, _, cols = pieces[0].shape
    rows = sum(p.shape[1] for p in pieces)
    tr = _row_tile(pieces[0].shape[1], cols, slots)
    starts, pos = [], 0
    for p in pieces:
        assert p.shape[1] % tr == 0
        starts.append(pos)
        pos += p.shape[1] // tr

    def body(*refs):
        g_refs = refs[:len(pieces)]
        w_ref, m_ref, v_ref, go_ref, d_ref, mo_ref, vo_ref = refs[len(pieces):]
        i = pl.program_id(0)
        g = None
        for g_ref, start in zip(g_refs, starts):
            part = g_ref[0].astype(F32)
            for s in range(1, slots):
                part = part + g_ref[s].astype(F32)
            g = part if g is None else jnp.where(i >= start, part, g)
        go_ref[...] = g
        d_ref[...], mo_ref[...], vo_ref[...] = _adamw_math(g, w_ref[...], m_ref[...], v_ref[...])

    def piece_spec(start, count):
        return pl.BlockSpec((slots, tr, cols), lambda i: (0, jnp.clip(i - start, 0, count - 1), 0))

    blk = pl.BlockSpec((tr, cols), lambda i: (i, 0))
    return _call(body, name=name, grid=(rows // tr,),
                 in_specs=[piece_spec(s, p.shape[1] // tr) for s, p in zip(starts, pieces)] + [blk, blk, blk],
                 out_specs=[blk] * 4, out_shape=[_sds((rows, cols))] * 4, sem=('parallel',))(*pieces, w, m, v)


def _to_scan_rows(a):
    T, n = a.shape
    return a.reshape(SUBLANES, T // SUBLANES, n).transpose(1, 0, 2).reshape(T, n)


def _from_scan_rows(a):
    T, n = a.shape
    return a.reshape(T // SUBLANES, SUBLANES, n).transpose(1, 0, 2).reshape(T, n)


def _expand(a):
    return jnp.repeat(a, SSM_GROUP, axis=1)


def _block_diag_b(bb):
    eye = jnp.eye(N_GROUPS, dtype=bb.dtype)
    return (bb.transpose(0, 2, 1)[:, :, None, :] * eye[:, None, :, None]).reshape(D_SSM, N_STATE)


def _block_diag_c(cc):
    eye = jnp.eye(N_GROUPS, dtype=cc.dtype)
    return (cc.transpose(0, 2, 1)[:, :, None, :] * eye[:, None, :, None]).reshape(N_STATE, D_SSM)


def _diag_blocks(x, chan_major):
    e2 = jnp.eye(2, dtype=x.dtype)
    e4 = jnp.eye(4, dtype=x.dtype)
    if chan_major:
        x = x.reshape(4, 2, 2, 4, SSM_GROUP, 4, SSM_STATE)
        x = x * e2[None, :, :, None, None, None, None] * e4[None, None, None, :, None, :, None]
        return x.sum(axis=(2, 3)).transpose(0, 1, 3, 4, 2).reshape(N_GROUPS, SSM_STATE, SSM_GROUP)
    x = x.reshape(4, 2, 4, SSM_STATE, 2, 4, SSM_GROUP)
    x = x * e2[None, :, None, None, :, None, None] * e4[None, None, :, None, None, :, None]
    return x.sum(axis=(4, 5)).reshape(N_GROUPS, SSM_STATE, SSM_GROUP)


SMALL_LAYOUT = {
    'ssm_b_re': (0, 0, 32, 1024), 'ssm_b_im': (32, 0, 32, 1024), 'ssm_c_re': (64, 0, 32, 1024),
    'ssm_c_im': (96, 0, 32, 1024), 'b_ada': (128, 0, 6, 1024), 'g_pre_mix': (134, 0, 1, 1024),
    'g_post_mix': (135, 0, 1, 1024), 'ssm_lam_re': (136, 0, 2, 1024), 'ssm_lam_im': (138, 0, 2, 1024),
    'ssm_log_step': (140, 0, 1, 32), 'glu_b': (141, 0, 1, 512), 'g_out_ssm': (141, 512, 1, 512),
    'g_out_conv': (142, 0, 1, 512), 'ssm_d': (142, 512, 1, 512), 'g_pre_ffn': (143, 0, 1, 1024),
    'g_post_ffn': (144, 0, 1, 1024)}
SMALL_ROWS = 152
B_ADA_ROW = SMALL_LAYOUT['b_ada'][0]
LATE_ROWS = {('b_ada', 0): 0, ('b_ada', 1): 1, ('g_pre_mix', 0): 2}


def _adamw_small(gathered, late, wts, mom_m, mom_v):
    names = list(SMALL_LAYOUT)
    n = len(names)

    def body(*refs):
        g_ref, late_ref, ins, outs = refs[0], refs[1], refs[2:2 + 3 * n], refs[2 + 3 * n:]
        for p, name in enumerate(names):
            r0, c0, rows, cols = SMALL_LAYOUT[name]
            pieces = [(0, rows)] if rows % 8 == 0 else [(r, 1) for r in range(rows)]
            for r, cnt in pieces:
                src_ref, first = (late_ref, LATE_ROWS[name, r]) if (name, r) in LATE_ROWS else (g_ref, r0 + r)
                g = src_ref[0, first:first + cnt, c0:c0 + cols]
                for s in range(1, N_DEV):
                    g = g + src_ref[s, first:first + cnt, c0:c0 + cols]
                w, m, v = (ins[3 * p + q][r:r + cnt, :] for q in range(3))
                res = (g,) + _adamw_math(g, w, m, v)
                for q in range(4):
                    outs[4 * p + q][r:r + cnt, :] = res[q]

    shapes = [SMALL_LAYOUT[name][2:] for name in names]
    args = [gathered, late]
    for name, shp in zip(names, shapes):
        args += [wts[name].reshape(shp), mom_m[name].reshape(shp), mom_v[name].reshape(shp)]
    outs = _call(body, name='adamw_small', grid=(1,),
                 in_specs=[_const(gathered.shape), _const(late.shape)]
                 + [_const(shp) for shp in shapes for _ in range(3)],
                 out_specs=[_const(shp) for shp in shapes for _ in range(4)],
                 out_shape=[_sds(shp) for shp in shapes for _ in range(4)], vmem=VMEM_BIG)(*args)
    res = {}
    for p, name in enumerate(names):
        for q, kind in enumerate(('g', 'd', 'm', 'v')):
            res[kind, name] = outs[4 * p + q].reshape(wts[name].shape)
    return res


def kernel(x, c, w_ada, b_ada, g_pre_mix, g_post_mix, w_in, ssm_lam_re, ssm_lam_im, ssm_log_step, ssm_b_re, ssm_b_im, ssm_c_re, ssm_c_im, ssm_d, glu_w, glu_b, g_out_ssm, conv_w, g_out_conv, w_out, g_pre_ffn, g_post_ffn, w_up, ffn_conv_w, w_down, loss_target, m_w_ada, m_b_ada, m_g_pre_mix, m_g_post_mix, m_w_in, m_ssm_lam_re, m_ssm_lam_im, m_ssm_log_step, m_ssm_b_re, m_ssm_b_im, m_ssm_c_re, m_ssm_c_im, m_ssm_d, m_glu_w, m_glu_b, m_g_out_ssm, m_conv_w, m_g_out_conv, m_w_out, m_g_pre_ffn, m_g_post_ffn, m_w_up, m_ffn_conv_w, m_w_down, v_w_ada, v_b_ada, v_g_pre_mix, v_g_post_mix, v_w_in, v_ssm_lam_re, v_ssm_lam_im, v_ssm_log_step, v_ssm_b_re, v_ssm_b_im, v_ssm_c_re, v_ssm_c_im, v_ssm_d, v_glu_w, v_glu_b, v_g_out_ssm, v_conv_w, v_g_out_conv, v_w_out, v_g_pre_ffn, v_g_post_ffn, v_w_up, v_ffn_conv_w, v_w_down):
    args = dict(locals())
    wts = {n: args[n] for n in WEIGHTS}
    mom_m = {n: args['m_' + n] for n in WEIGHTS}
    mom_v = {n: args['v_' + n] for n in WEIGHTS}
    T = x.shape[1]
    tm = min(512, T)
    tw = min(1024, T)
    me = _me()[3]
    xt, tgt = x[0], loss_target[0]

    c_all, w_in_s, glu_s, w_out_s, conv_s = _exchange(
        [c, w_in[0].astype(BF16), glu_w[0].astype(BF16), w_out[0].astype(BF16), conv_w[0]], name='gather_first',
        scatter=False)
    c_all = c_all.reshape(N_DEV, D_MODEL)
    b_cols = lax.dynamic_slice(b_ada, (0, me * ADA_SHARD), (1, ADA_SHARD))
    mod_cols, c_act = _mod_cols(c_all, w_ada[0], b_cols)
    (mod_all,) = _exchange([mod_cols], name='gather_mod', scatter=False)
    mod = lax.dynamic_slice(mod_all, (0, me, 0), (N_DEV, 1, ADA_SHARD)).reshape(N_MOD, 1, D_MODEL)
    sh1, sc1, gt1, sh2, sc2, gt2 = [mod[k] for k in range(N_MOD)]

    glu_full = glu_s.reshape(D_SSM, D_SSM)
    w_out_full = w_out_s.reshape(D_MODEL, D_MODEL)
    cw_full = conv_s.transpose(1, 0, 2).reshape(3, D_CONV)

    lre_x, lim_x = _expand(ssm_lam_re[0]), _expand(ssm_lam_im[0])
    lst_x = jnp.broadcast_to(ssm_log_step[0][:, None], (N_GROUPS, SSM_STATE * SSM_GROUP))
    b_re_x = ssm_b_re[0].reshape(N_GROUPS, -1)
    b_im_x = ssm_b_im[0].reshape(N_GROUPS, -1)
    ar_x, ai_x, bbr_x, bbi_x = _ssm_prep(lre_x, lim_x, lst_x, b_re_x, b_im_x)
    lam_r = ar_x[:, ::SSM_GROUP].reshape(1, N_STATE)
    lam_i = ai_x[:, ::SSM_GROUP].reshape(1, N_STATE)
    big_b_re = _block_diag_b(bbr_x.reshape(N_GROUPS, SSM_STATE, SSM_GROUP)).astype(BF16)
    big_b_im = _block_diag_b(bbi_x.reshape(N_GROUPS, SSM_STATE, SSM_GROUP)).astype(BF16)
    big_c_re = _block_diag_c(ssm_c_re[0]).astype(BF16)
    big_c_im = _block_diag_c(ssm_c_im[0]).astype(BF16)
    head = jnp.arange(D_SSM)
    avg16 = jnp.where(head[:, None] // SSM_GROUP == head[None, :] // SSM_GROUP, 1.0 / SSM_GROUP, 0.0).astype(BF16)
    hd = D_CONV // CONV_HEADS
    avg64 = jnp.where(head[:, None] // hd == head[None, :] // hd, 1.0 / hd, 0.0).astype(BF16)

    (proj, h1), (w_down_s, ffn_conv_s) = _pre_mix(xt, sc1, sh1, g_pre_mix, w_in_s, tw,
                                                  ([w_down[0].astype(BF16), ffn_conv_w[0]], False))
    wd4 = w_down_s.reshape(4, FF_SHARD, D_MODEL)
    u_perm = _to_scan_rows(proj[:, :D_SSM]).astype(BF16)
    (s_re, s_im, y_perm), (w_up_s,) = _ssm_fwd(u_perm, big_b_re, big_b_im, big_c_re, big_c_im, lam_r, lam_i,
                                               ([w_up[0].astype(BF16)], False))
    yssm = _from_scan_rows(y_perm)
    mix_args = (ssm_d, glu_full, glu_b, g_out_ssm, cw_full, g_out_conv, avg16, avg64)
    ycat = _mix_fwd(yssm, proj, *mix_args, tm)
    o, x1, h2 = _out_proj(ycat, w_out_full, xt, gt1, g_post_mix, g_pre_ffn, sc2, sh2, tm)
    up8, hid8 = _ffn_up(h2, w_up_s, ffn_conv_s, tw)
    hid4 = hid8.reshape(2, 4, T, FF_SHARD)
    ddn, dx2, loss_parts, d_gt2, d_g_post_ffn = _ffn_down(hid4, wd4, x1, tgt, gt2, g_post_ffn, tm)
    loss_local = jnp.sum(loss_parts[:, 0, 0])

    got = {}
    dhid, act = _ffn_dact(ddn, wd4, hid4, tm)
    g_w_down = _grad_tn(act, ddn, pl.BlockSpec((None, tw, FF_SHARD), lambda g, k: (g, k, 0)),
                        pl.BlockSpec((tw, D_MODEL), lambda g, k: (k, 0)), 4, FF_SHARD, D_MODEL, tw, 'grad_w_down')
    (dup8, dcw_ffn), (got['w_down'],) = _ffn_dup(dhid.reshape(N_DEV, T, FF_SHARD), up8, ffn_conv_s, tm,
                                                 ([g_w_down.reshape(N_DEV, D_FF // N_DEV, D_MODEL)], True))
    g_w_up_halves = _grad_tn(h2, dup8, pl.BlockSpec((tw, D_MODEL), lambda g, k: (k, 0)),
                             pl.BlockSpec((None, tw, FF_SHARD), lambda g, k: (g, k, 0)), N_DEV, D_MODEL, FF_SHARD, tw,
                             'grad_w_up', parts=2)
    (dx1, d_sh2, d_sc2, d_g_pre_ffn, d_o, d_gt1, d_g_post_mix), (got_up_0, got['ffn_conv_w']) = _pre_norm_bwd(
        dup8, pl.BlockSpec((2, tw, FF_SHARD), lambda i, j: (j, i, 0)), w_up_s, x1, dx2, sc2, g_pre_ffn, tw,
        'ffn_in_bwd', ([g_w_up_halves[0], dcw_ffn], True), below=(o, gt1, g_post_mix), group=2)

    g_w_out = _grad_tn(ycat, d_o, pl.BlockSpec((tw, D_MODEL), lambda g, k: (k, 0)),
                       pl.BlockSpec((tw, D_MODEL), lambda g, k: (k, 0)), 1, D_MODEL, D_MODEL, tw, 'grad_w_out')
    dy, dconv, dbg, z_b, dlin_b, sums = _mix_bwd(d_o, w_out_full, yssm, proj, *mix_args, tm)
    g_glu_w = _grad_tn(z_b, dlin_b, pl.BlockSpec((tw, D_SSM), lambda g, k: (k, 0)),
                       pl.BlockSpec((tw, D_SSM), lambda g, k: (k, 0)), 1, D_SSM, D_SSM, tw, 'grad_glu_w')
    dy_perm = _to_scan_rows(dy).astype(BF16)
    (du_perm, dbr_blk, dbi_blk, dcr_blk, dci_blk, dar_blk, dai_blk), (got_up_1, got['w_out'], got['glu_w']) = _ssm_bwd(
        dy_perm, u_perm, s_re, s_im, big_b_re, big_b_im, big_c_re, big_c_im, lam_r, lam_i,
        ([g_w_up_halves[1], g_w_out.reshape(N_DEV, D_MODEL // N_DEV, D_MODEL),
          g_glu_w.reshape(N_DEV, D_SSM // N_DEV, D_SSM)], True))
    du_ssm = _from_scan_rows(du_perm)
    dproj = _mix_bwd_proj(dconv, proj, du_ssm, dy, ssm_d, dbg, cw_full, tm)
    dbb_re = _diag_blocks(dbr_blk, True).reshape(N_GROUPS, -1)
    dbb_im = _diag_blocks(dbi_blk, True).reshape(N_GROUPS, -1)
    d_c_re = _diag_blocks(dcr_blk, False).transpose(0, 2, 1)
    d_c_im = _diag_blocks(dci_blk, False).transpose(0, 2, 1)
    lane = jnp.arange(SSM_STATE * SSM_GROUP)
    seg = jnp.where(lane[:, None] // SSM_GROUP == lane[None, :] // SSM_GROUP, 1.0, 0.0).astype(BF16)
    d_b_re_x, d_b_im_x, d_lre_x, d_lim_x, d_lst = _ssm_prep_bwd(
        lre_x, lim_x, lst_x, b_re_x, b_im_x, dbb_re, dbb_im, _expand(dar_blk.reshape(N_GROUPS, SSM_STATE)),
        _expand(dai_blk.reshape(N_GROUPS, SSM_STATE)), seg)

    row = lambda a: a.reshape(-1, PACK_COLS)
    blank = jnp.zeros((1, PACK_COLS), F32)
    small_pack = jnp.concatenate([
        d_b_re_x, d_b_im_x, row(d_c_re), row(d_c_im), blank, blank, d_gt1, d_sh2, d_sc2, d_gt2, blank,
        d_g_post_mix, row(d_lre_x[:, ::SSM_GROUP]), row(d_lim_x[:, ::SSM_GROUP]),
        jnp.pad(d_lst.reshape(1, N_GROUPS), ((0, 0), (0, PACK_COLS - N_GROUPS))), row(sums[0:4]), d_g_pre_ffn,
        d_g_post_ffn, jnp.zeros((SMALL_ROWS - 145, PACK_COLS), F32)])
    g_w_in, (small_all,) = _grad_w_in(h1, dproj, tw, ([small_pack], False))
    g_conv_slots = jnp.concatenate([sums[4:7], jnp.zeros((5, D_CONV), F32)]).reshape(
        8, N_DEV, D_CONV // N_DEV).transpose(1, 0, 2)
    (grad_x, d_sh1, d_sc1, d_g_pre_mix), (got['w_in'], got['conv_w']) = _pre_norm_bwd(
        dproj, pl.BlockSpec((tw, 4 * IN_SHARD), lambda i, j: (i, j)), w_in_s, xt, dx1, sc1, g_pre_mix, tw,
        'mix_in_bwd', ([g_w_in, g_conv_slots], True), group=4)
    late_pack = jnp.concatenate([d_sh1, d_sc1, d_g_pre_mix, jnp.full((1, PACK_COLS), loss_local, F32),
                                 jnp.zeros((4, PACK_COLS), F32)])
    (late_all,) = _exchange([late_pack], name='gather_late_grads', scatter=False)
    loss = jnp.sum(late_all[:, 3, 0])
    res = _adamw_small(small_all, late_all, wts, mom_m, mom_v)

    dmod_all = jnp.concatenate([late_all[:, 0:2, :], small_all[:, B_ADA_ROW + 2:B_ADA_ROW + N_MOD, :]],
                               axis=1).reshape(N_DEV, N_MOD * D_MODEL)
    dmod_cols = lax.dynamic_slice(dmod_all, (0, me * ADA_SHARD), (N_DEV, ADA_SHARD))
    g_w_ada = _grad_w_ada(c_act.T, dmod_cols)

    pieces = {n: [slots[:, :3, :] if n in ('conv_w', 'ffn_conv_w') else slots] for n, slots in got.items()}
    pieces['w_up'] = [got_up_0, got_up_1]
    for n, parts in pieces.items():
        outs = _adamw(parts, wts[n][0], mom_m[n][0], mom_v[n][0], 'adamw_' + n)
        for kind, val in zip(('g', 'd', 'm', 'v'), outs):
            res[kind, n] = val[None]
    outs = _adamw([g_w_ada[None]], w_ada[0], m_w_ada[0], v_w_ada[0], 'adamw_w_ada')
    for kind, val in zip(('g', 'd', 'm', 'v'), outs):
        res[kind, 'w_ada'] = val[None]

    return (loss, grad_x[None], *[res['g', n] for n in WEIGHTS], *[res['d', n] for n in WEIGHTS],
            *[res['m', n] for n in WEIGHTS], *[res['v', n] for n in WEIGHTS])
```

```python
import math

import jax
import jax.numpy as jnp
from jax import lax
from jax.experimental import pallas as pl
from jax.experimental.pallas import tpu as pltpu

F32, BF16 = jnp.float32, jnp.bfloat16

D_MODEL = 1024
D_SSM = 512
D_CONV = 512
SSM_GROUP = 16
N_GROUPS = 32
SSM_STATE = 64
N_STATE = N_GROUPS * SSM_STATE
CONV_HEADS = 8
D_FF = 2816
N_MOD = 6
D_IN_PROJ = D_SSM + 3 * D_CONV
N_DEV = 8
FF_SHARD = 2 * D_FF // N_DEV
IN_SHARD = D_IN_PROJ // N_DEV
ADA_SHARD = N_MOD * D_MODEL // N_DEV
EPS = 1e-6
LAMBDA_RE_MAX = -1e-4
ADAM_LR, ADAM_B1, ADAM_B2, ADAM_EPS, ADAM_WD, ADAM_STEP = 0.001, 0.9, 0.999, 1e-08, 0.01, 10
GELU_C = math.sqrt(2.0 / math.pi)
GELU_A = 0.044715

SUBLANES = 8
HALO = 8
HALO16 = 16
SCAN_UNROLL = 8
STATE_BLOCK = 256
CHAN_BLOCK = 128
VMEM_BIG = 48 << 20
VMEM_MOST = 58 << 20

WEIGHTS = ['w_ada', 'b_ada', 'g_pre_mix', 'g_post_mix', 'w_in', 'ssm_lam_re', 'ssm_lam_im', 'ssm_log_step',
           'ssm_b_re', 'ssm_b_im', 'ssm_c_re', 'ssm_c_im', 'ssm_d', 'glu_w', 'glu_b', 'g_out_ssm', 'conv_w',
           'g_out_conv', 'w_out', 'g_pre_ffn', 'g_post_ffn', 'w_up', 'ffn_conv_w', 'w_down']
SHARDED = ('w_ada', 'w_in', 'glu_w', 'conv_w', 'w_out', 'w_up', 'ffn_conv_w', 'w_down')
PACK_COLS = 1024


def _call(body, *, name, grid, in_specs, out_specs, out_shape, scratch=(), sem=None, vmem=None, ride=None):
    params = {}
    if vmem is not None:
        params['vmem_limit_bytes'] = vmem
    if ride is None:
        if sem is not None:
            params['dimension_semantics'] = sem
        return pl.pallas_call(body, name=name, grid=grid, in_specs=in_specs, out_specs=out_specs,
                              out_shape=out_shape, scratch_shapes=list(scratch),
                              compiler_params=pltpu.CompilerParams(**params))
    arrs, scatter = ride
    single = not isinstance(out_shape, (list, tuple))
    out_shape_l = [out_shape] if single else list(out_shape)
    out_specs_l = [out_specs] if single else list(out_specs)
    n, n_in, n_out, n_scr = len(arrs), len(in_specs), len(out_shape_l), len(scratch)
    any_spec = pl.BlockSpec(memory_space=pl.ANY)
    params['dimension_semantics'] = ('arbitrary',) * len(grid)

    def carried(*refs):
        ins, rin = refs[:n_in], refs[n_in:n_in + n]
        outs, rout = refs[n_in + n:n_in + n + n_out], refs[n_in + n + n_out:n_in + 2 * n + n_out]
        scr, sems = refs[n_in + 2 * n + n_out:n_in + 2 * n + n_out + n_scr], refs[n_in + 2 * n + n_out + n_scr:]
        first = pl.program_id(0) == 0
        last = pl.program_id(0) == grid[0] - 1
        for ax in range(1, len(grid)):
            first = jnp.logical_and(first, pl.program_id(ax) == 0)
            last = jnp.logical_and(last, pl.program_id(ax) == grid[ax] - 1)

        @pl.when(first)
        def _():
            _exchange_start(rin, rout, sems, scatter)

        body(*ins, *outs, *scr)

        @pl.when(last)
        def _():
            _exchange_wait(rin, rout, sems, scatter)

    call = pl.pallas_call(carried, name=name, grid=grid, in_specs=list(in_specs) + [any_spec] * n,
                          out_specs=out_specs_l + [any_spec] * n,
                          out_shape=out_shape_l + _exchange_shapes(arrs, scatter),
                          scratch_shapes=list(scratch) + _exchange_sems(n),
                          compiler_params=pltpu.CompilerParams(**params))

    def run(*args):
        res = call(*args, *arrs)
        own = res[0] if single else list(res[:n_out])
        return own, list(res[n_out:])

    return run


def _const(shape):
    nd = len(shape)
    return pl.BlockSpec(shape, lambda *_: (0,) * nd)


def _sds(shape, dtype=F32):
    return jax.ShapeDtypeStruct(shape, dtype)


def _dot(a, b):
    return jnp.dot(a, b, preferred_element_type=F32)


def _dot_nt(a, b):
    return lax.dot_general(a, b, (((1,), (1,)), ((), ())), preferred_element_type=F32)


def _dot_tn(a, b):
    return lax.dot_general(a, b, (((0,), (0,)), ((), ())), preferred_element_type=F32)


def _dot_split(x, mat, parts):
    acc = None
    rem = x
    for _ in range(parts):
        piece = rem.astype(BF16)
        rem = rem - piece.astype(F32)
        term = _dot(piece, mat)
        acc = term if acc is None else acc + term
    return acc


def _sigmoid(x):
    return 1.0 / (1.0 + jnp.exp(-x))


def _gelu(x):
    t = jnp.tanh(GELU_C * (x + GELU_A * x * x * x))
    return 0.5 * x * (1.0 + t), t


def _gelu_grad(x, t):
    return 0.5 * (1.0 + t) + 0.5 * x * (1.0 - t * t) * GELU_C * (1.0 + 3.0 * GELU_A * x * x)


def _rsqrt_mean(x):
    return lax.rsqrt(jnp.mean(x * x, axis=-1, keepdims=True) + EPS)


def _colsum(x):
    return jnp.sum(x, axis=0, keepdims=True)


def _shifts_down(x, halo):
    ext = jnp.concatenate([halo, x], axis=0)
    return pltpu.roll(ext, 1, 0)[halo.shape[0]:], pltpu.roll(ext, 2, 0)[halo.shape[0]:]


def _shifts_up(x, halo):
    n = x.shape[0]
    ext = jnp.concatenate([x, halo], axis=0)
    total = ext.shape[0]
    return pltpu.roll(ext, total - 1, 0)[:n], pltpu.roll(ext, total - 2, 0)[:n]


def _conv3(x, halo, w_ref):
    x1, x2 = _shifts_down(x, halo)
    return w_ref[0:1, :] * x2 + w_ref[1:2, :] * x1 + w_ref[2:3, :] * x, x1, x2


def _conv3_t(g, halo, w_ref):
    g1, g2 = _shifts_up(g, halo)
    return w_ref[2:3, :] * g + w_ref[1:2, :] * g1 + w_ref[0:1, :] * g2, g1, g2


def _silu_parts(x):
    s = _sigmoid(x)
    return x * s, s * (1.0 + x * (1.0 - s))


def _norm_bwd(dn, x, r, g):
    gd = g * dn
    return r * gd - x * (r * r * r) * jnp.mean(gd * x, axis=-1, keepdims=True)


def _head_norm_bwd(dn, y, rs, g, avg):
    gd = g * dn
    return rs * gd - y * (rs * rs * rs) * _dot_split(gd * y, avg, 2)


def _me():
    x, y, c = lax.axis_index('x'), lax.axis_index('y'), lax.axis_index('c')
    return x, y, c, 4 * x + 2 * y + c


def _peer(k):
    x, y, c, _ = _me()
    px = 1 - x if k & 4 else x
    py = 1 - y if k & 2 else y
    pc = 1 - c if k & 1 else c
    return (px, py, pc), 4 * px + 2 * py + pc


SIBLING = 1
OTHER_CHIPS = (2, 4, 6)


def _remote(src, dst, sems, a, k, dev):
    return pltpu.make_async_remote_copy(src_ref=src, dst_ref=dst, send_sem=sems[0].at[a, k - 1],
                                        recv_sem=sems[1].at[a, k - 1], device_id=dev,
                                        device_id_type=pl.DeviceIdType.MESH)


def _exchange_copies(ins, outs, sems, scatter):
    me = _me()[3]
    local, first, relay, arrivals = [], [], [], []
    for a in range(len(ins)):
        src = ins[a].at[me] if scatter else ins[a]
        local.append(pltpu.make_async_copy(src, outs[a].at[me], sems[2].at[a]))
        for k in range(1, N_DEV):
            dev, idx = _peer(k)
            landed = _remote(src, outs[a].at[idx], sems, a, k, dev)
            if scatter:
                first.append(_remote(ins[a].at[idx], outs[a].at[me], sems, a, k, dev))
                arrivals.append(landed)
            elif k == SIBLING:
                first.append(_remote(src, outs[a].at[me], sems, a, k, dev))
                arrivals.append(landed)
            elif k in OTHER_CHIPS:
                first.append(_remote(src, outs[a].at[me], sems, a, k, dev))
                sib, _ = _peer(SIBLING)
                relay.append((landed, _remote(outs[a].at[idx], outs[a].at[idx], sems, a, k | SIBLING, sib)))
            else:
                arrivals.append(landed)
    return local, first, relay, arrivals


def _exchange_start(ins, outs, sems, scatter):
    local, first, _, _ = _exchange_copies(ins, outs, sems, scatter)
    for cp in local + first:
        cp.start()


def _exchange_wait(ins, outs, sems, scatter):
    local, first, relay, arrivals = _exchange_copies(ins, outs, sems, scatter)
    for landed, forward in relay:
        landed.wait_recv()
        forward.start()
    for cp in arrivals:
        cp.wait_recv()
    for cp in first + [forward for _, forward in relay]:
        cp.wait_send()
    for cp in local:
        cp.wait()


def _exchange_shapes(arrs, scatter):
    return [_sds(a.shape if scatter else (N_DEV,) + a.shape, a.dtype) for a in arrs]


def _exchange_sems(n):
    return [pltpu.SemaphoreType.DMA((n, N_DEV - 1)), pltpu.SemaphoreType.DMA((n, N_DEV - 1)),
            pltpu.SemaphoreType.DMA((n,))]


def _exchange(arrs, *, name, scatter):
    n = len(arrs)

    def body(*refs):
        _exchange_start(refs[:n], refs[n:2 * n], refs[2 * n:], scatter)
        _exchange_wait(refs[:n], refs[n:2 * n], refs[2 * n:], scatter)

    any_spec = pl.BlockSpec(memory_space=pl.ANY)
    outs = pl.pallas_call(body, name=name, out_shape=_exchange_shapes(arrs, scatter), in_specs=[any_spec] * n,
                          out_specs=[any_spec] * n, scratch_shapes=_exchange_sems(n))(*arrs)
    return list(outs)


def _mod_cols(c_all, w_ada, b_cols):
    def body(c_ref, w_ref, b_ref, mod_ref, act_ref):
        c = c_ref[...]
        act = c * _sigmoid(c)
        act_ref[...] = act
        mod_ref[...] = _dot(act.astype(BF16), w_ref[...].astype(BF16)) + b_ref[...]

    return _call(body, name='mod_cols', grid=(1,),
                 in_specs=[_const(c_all.shape), _const(w_ada.shape), _const(b_cols.shape)],
                 out_specs=[_const((N_DEV, ADA_SHARD)), _const(c_all.shape)],
                 out_shape=[_sds((N_DEV, ADA_SHARD)), _sds(c_all.shape)], vmem=VMEM_BIG)(c_all, w_ada, b_cols)


def _grad_w_ada(act_t, dmod_cols):
    def body(a_ref, d_ref, o_ref):
        o_ref[...] = _dot(a_ref[...], d_ref[...])

    return _call(body, name='grad_w_ada', grid=(1,), in_specs=[_const(act_t.shape), _const(dmod_cols.shape)],
                 out_specs=_const((D_MODEL, ADA_SHARD)), out_shape=_sds((D_MODEL, ADA_SHARD)),
                 vmem=VMEM_BIG)(act_t, dmod_cols)


def _pre_mix(x, sc, sh, g, w_s, tm, ride):
    T = x.shape[0]

    def body(x_ref, sc_ref, sh_ref, g_ref, w_ref, proj_ref, h_ref):
        @pl.when(pl.program_id(1) == 0)
        def _():
            xv = x_ref[...]
            h_ref[...] = ((xv * _rsqrt_mean(xv) * g_ref[...]) * (1.0 + sc_ref[...]) + sh_ref[...]).astype(BF16)

        for s in range(2):
            proj_ref[:, s * IN_SHARD:(s + 1) * IN_SHARD] = _dot(h_ref[...], w_ref[s])

    row = pl.BlockSpec((tm, D_MODEL), lambda i, j: (i, 0))
    vec = _const((1, D_MODEL))
    return _call(body, name='pre_mix', grid=(T // tm, N_DEV // 2),
                 in_specs=[row, vec, vec, vec, pl.BlockSpec((2, D_MODEL, IN_SHARD), lambda i, j: (j, 0, 0))],
                 out_specs=[pl.BlockSpec((tm, 2 * IN_SHARD), lambda i, j: (i, j)), row],
                 out_shape=[_sds((T, D_IN_PROJ)), _sds((T, D_MODEL), BF16)],
                 sem=('parallel', 'arbitrary'), ride=ride)(x, sc, sh, g, w_s)


def _halo_before(tm, rows=HALO):
    return lambda i: jnp.maximum(i * (tm // rows) - 1, 0)


def _halo_after(tm, T, rows=HALO):
    return lambda i: jnp.minimum((i + 1) * (tm // rows), T // rows - 1)


def _mix_fwd(yssm, proj, d, glu_w, glu_b, g_ssm, cw, g_conv, avg16, avg64, tm):
    T = yssm.shape[0]
    hb = _halo_before(tm)

    def body(y_ref, p_ref, ph_ref, d_ref, gw_ref, gb_ref, gs_ref, cw_ref, gc_ref, a16_ref, a64_ref, o_ref):
        i = pl.program_id(0)
        u = p_ref[:, 0:D_SSM]
        y = y_ref[...] + d_ref[...] * u
        z, _ = _gelu(y)
        gate = _sigmoid(_dot(z.astype(BF16), gw_ref[...]) + gb_ref[...])
        ya = z * gate
        rs = lax.rsqrt(_dot_split(ya * ya, a16_ref[...], 2) + EPS)
        o_ref[:, 0:D_SSM] = (ya * rs * gs_ref[...]).astype(BF16)
        bg = p_ref[:, D_SSM:D_SSM + D_CONV]
        cv = p_ref[:, D_SSM + D_CONV:D_SSM + 2 * D_CONV] * p_ref[:, D_SSM + 2 * D_CONV:D_IN_PROJ]
        hv = ph_ref[:, D_SSM + D_CONV:D_SSM + 2 * D_CONV] * ph_ref[:, D_SSM + 2 * D_CONV:D_IN_PROJ]
        hv = jnp.where(i > 0, hv, 0.0)
        conv, _, _ = _conv3(cv, hv, cw_ref)
        yb = bg * conv
        rsb = lax.rsqrt(_dot_split(yb * yb, a64_ref[...], 2) + EPS)
        o_ref[:, D_SSM:D_MODEL] = (yb * rsb * gc_ref[...]).astype(BF16)

    vec = _const((1, D_SSM))
    sq = _const((D_SSM, D_SSM))
    return _call(body, name='mix_fwd', grid=(T // tm,),
                 in_specs=[pl.BlockSpec((tm, D_SSM), lambda i: (i, 0)), pl.BlockSpec((tm, D_IN_PROJ), lambda i: (i, 0)),
                           pl.BlockSpec((HALO, D_IN_PROJ), lambda i: (hb(i), 0)), vec, sq, vec, vec,
                           _const((3, D_CONV)), vec, sq, sq],
                 out_specs=pl.BlockSpec((tm, D_MODEL), lambda i: (i, 0)), out_shape=_sds((T, D_MODEL), BF16),
                 sem=('parallel',), vmem=VMEM_BIG)(yssm, proj, proj, d, glu_w, glu_b, g_ssm, cw, g_conv, avg16, avg64)


def _out_proj(ycat, w_out, x, gt, g_post, g_pre, sc, sh, tm):
    T = x.shape[0]

    def body(y_ref, w_ref, x_ref, gt_ref, gp_ref, g2_ref, sc_ref, sh_ref, o_ref, x1_ref, h_ref):
        o = _dot(y_ref[...], w_ref[...])
        o_ref[...] = o
        x1 = x_ref[...] + gt_ref[...] * (o * _rsqrt_mean(o) * gp_ref[...])
        x1_ref[...] = x1
        h_ref[...] = ((x1 * _rsqrt_mean(x1) * g2_ref[...]) * (1.0 + sc_ref[...]) + sh_ref[...]).astype(BF16)

    row = pl.BlockSpec((tm, D_MODEL), lambda i: (i, 0))
    vec = _const((1, D_MODEL))
    return _call(body, name='out_proj', grid=(T // tm,),
                 in_specs=[row, _const((D_MODEL, D_MODEL)), row, vec, vec, vec, vec, vec],
                 out_specs=[row, row, row],
                 out_shape=[_sds((T, D_MODEL)), _sds((T, D_MODEL)), _sds((T, D_MODEL), BF16)],
                 sem=('parallel',), vmem=VMEM_BIG)(ycat, w_out, x, gt, g_post, g_pre, sc, sh)


def _ffn_up(h2, w_s, cw8, tm):
    T = h2.shape[0]
    hb = _halo_before(tm, HALO16)

    def body(h_ref, hh_ref, w_ref, cw_ref, up_ref, hid_ref):
        up = _dot_nt(h_ref[...], w_ref[...])
        up_ref[...] = up.astype(BF16)
        before = jnp.where(pl.program_id(0) > 0, _dot_nt(hh_ref[...], w_ref[...]), 0.0)
        hid_ref[...] = _conv3(up, before, cw_ref)[0].astype(BF16)

    out = pl.BlockSpec((None, tm, FF_SHARD), lambda i, j: (j, i, 0))
    return _call(body, name='ffn_up', grid=(T // tm, N_DEV),
                 in_specs=[pl.BlockSpec((tm, D_MODEL), lambda i, j: (i, 0)),
                           pl.BlockSpec((HALO16, D_MODEL), lambda i, j: (hb(i), 0)),
                           pl.BlockSpec((None, FF_SHARD, D_MODEL), lambda i, j: (j, 0, 0)),
                           pl.BlockSpec((None, 3, FF_SHARD), lambda i, j: (j, 0, 0))],
                 out_specs=[out, out], out_shape=[_sds((N_DEV, T, FF_SHARD), BF16)] * 2,
                 sem=('parallel', 'parallel'))(h2, h2, w_s, cw8)


def _ffn_down(hid4, wd4, x1, tgt, gt, g_post, tm):
    T = x1.shape[0]
    nb = T // tm

    def body(a_ref, w_ref, x1_ref, t_ref, gt_ref, g_ref, ddn_ref, dx_ref, loss_ref, dgt_ref, dg_ref, dn_ref):
        i, j = pl.program_id(0), pl.program_id(1)
        part = None
        for s in range(2):
            act = (_silu_parts(a_ref[0, s].astype(F32))[0] * a_ref[1, s].astype(F32)).astype(BF16)
            term = _dot(act, w_ref[s])
            part = term if part is None else part + term

        @pl.when(jnp.logical_and(i == 0, j == 0))
        def _():
            dgt_ref[...] = jnp.zeros_like(dgt_ref)
            dg_ref[...] = jnp.zeros_like(dg_ref)

        @pl.when(j == 0)
        def _():
            dn_ref[...] = part

        @pl.when(j > 0)
        def _():
            dn_ref[...] += part

        @pl.when(j == 1)
        def _():
            dn, gv, gate = dn_ref[...], g_ref[...], gt_ref[...]
            r = _rsqrt_mean(dn)
            normed = dn * r * gv
            err = x1_ref[...] + gate * normed - t_ref[...]
            dx = err * (1.0 / D_MODEL)
            dx_ref[...] = dx
            tot = jnp.sum(jnp.sum(err * err, axis=1, keepdims=True), axis=0, keepdims=True) * (0.5 / D_MODEL)
            loss_ref[...] = jnp.broadcast_to(tot, (8, 128))
            dgt_ref[...] += _colsum(dx * normed)
            dnn = dx * gate
            dg_ref[...] += _colsum(dnn * dn * r)
            ddn_ref[...] = _norm_bwd(dnn, dn, r, gv).astype(BF16)

    row = pl.BlockSpec((tm, D_MODEL), lambda i, j: (i, 0))
    vec = _const((1, D_MODEL))
    return _call(body, name='ffn_down', grid=(nb, 2),
                 in_specs=[pl.BlockSpec((2, 2, tm, FF_SHARD), lambda i, j: (0, j, i, 0)),
                           pl.BlockSpec((2, FF_SHARD, D_MODEL), lambda i, j: (j, 0, 0)), row, row, vec, vec],
                 out_specs=[row, row, pl.BlockSpec((None, 8, 128), lambda i, j: (i, 0, 0)), vec, vec],
                 out_shape=[_sds((T, D_MODEL), BF16), _sds((T, D_MODEL)), _sds((nb, 8, 128)), _sds((1, D_MODEL)),
                            _sds((1, D_MODEL))],
                 scratch=[pltpu.VMEM((tm, D_MODEL), F32)], sem=('arbitrary', 'arbitrary'),
                 vmem=VMEM_BIG)(hid4, wd4, x1, tgt, gt, g_post)


def _ssm_prep(lre, lim, lst, b_re, b_im):
    def body(lre_ref, lim_ref, lst_ref, br_ref, bi_ref, ar_ref, ai_ref, bbr_ref, bbi_ref):
        ar, ai, qr, qi = _zoh(lre_ref[...], lim_ref[...], lst_ref[...])[:4]
        ar_ref[...] = ar
        ai_ref[...] = ai
        bbr_ref[...] = qr * br_ref[...] - qi * bi_ref[...]
        bbi_ref[...] = qr * bi_ref[...] + qi * br_ref[...]

    shp = lre.shape
    return _call(body, name='ssm_prep', grid=(1,), in_specs=[_const(shp)] * 5, out_specs=[_const(shp)] * 4,
                 out_shape=[_sds(shp)] * 4)(lre, lim, lst, b_re, b_im)


def _zoh(lre, lim, lst):
    lr = jnp.minimum(lre, LAMBDA_RE_MAX)
    st = jnp.exp(lst)
    mag = jnp.exp(lr * st)
    ar = mag * jnp.cos(lim * st)
    ai = mag * jnp.sin(lim * st)
    den = lr * lr + lim * lim
    qr = ((ar - 1.0) * lr + ai * lim) / den
    qi = (ai * lr - (ar - 1.0) * lim) / den
    return ar, ai, qr, qi, lr, st, den


def _ssm_prep_bwd(lre, lim, lst, b_re, b_im, dbbr, dbbi, dar, dai, seg):
    def body(lre_ref, lim_ref, lst_ref, br_ref, bi_ref, dbbr_ref, dbbi_ref, dar_ref, dai_ref, seg_ref,
             dbr_ref, dbi_ref, dlre_ref, dlim_ref, dlst_ref):
        lre_v = lre_ref[...]
        li = lim_ref[...]
        ar, ai, qr, qi, lr, st, den = _zoh(lre_v, li, lst_ref[...])
        br, bi, gbr, gbi = br_ref[...], bi_ref[...], dbbr_ref[...], dbbi_ref[...]
        dbr_ref[...] = qr * gbr + qi * gbi
        dbi_ref[...] = qr * gbi - qi * gbr
        gqr = _dot_split(br * gbr + bi * gbi, seg_ref[...], 3)
        gqi = _dot_split(br * gbi - bi * gbr, seg_ref[...], 3)
        ir, ii = lr / den, -li / den
        gar = dar_ref[...] + ir * gqr + ii * gqi
        gai = dai_ref[...] + ir * gqi - ii * gqr
        tr, ti = qr * ir - qi * ii, qr * ii + qi * ir
        glr = -(tr * gqr + ti * gqi)
        gli = -(tr * gqi - ti * gqr)
        gzr = ar * gar + ai * gai
        gzi = ar * gai - ai * gar
        glr = glr + st * gzr
        gli = gli + st * gzi
        gst = (lr * gzr + li * gzi) * st
        dlre_ref[...] = jnp.where(lre_v < LAMBDA_RE_MAX, glr, 0.0)
        dlim_ref[...] = gli
        dlst_ref[...] = jnp.sum(gst, axis=1, keepdims=True) * (1.0 / SSM_GROUP)

    shp = lre.shape
    return _call(body, name='ssm_prep_bwd', grid=(1,), in_specs=[_const(shp)] * 9 + [_const(seg.shape)],
                 out_specs=[_const(shp)] * 4 + [_const((N_GROUPS, 1))],
                 out_shape=[_sds(shp)] * 4 + [_sds((N_GROUPS, 1))], vmem=VMEM_BIG)(
                     lre, lim, lst, b_re, b_im, dbbr, dbbi, dar, dai, seg)


def _scan_specs(T):
    half = lambda cb: cb // 2
    return dict(
        chan=pl.BlockSpec((T, CHAN_BLOCK), lambda cb: (0, half(cb))),
        state=pl.BlockSpec((T, STATE_BLOCK), lambda cb: (0, cb)),
        b=pl.BlockSpec((CHAN_BLOCK, STATE_BLOCK), lambda cb: (half(cb), cb)),
        c=pl.BlockSpec((STATE_BLOCK, CHAN_BLOCK), lambda cb: (cb, half(cb))),
        lam=pl.BlockSpec((1, STATE_BLOCK), lambda cb: (0, cb)),
    )


def _complex_power(re, im, n):
    out = None
    while True:
        if n & 1:
            out = (re, im) if out is None else (out[0] * re - out[1] * im, out[0] * im + out[1] * re)
        n >>= 1
        if n == 0:
            return out
        re, im = re * re - im * im, 2.0 * re * im


def _rows8(i):
    if isinstance(i, int):
        return pl.ds(i * SUBLANES, SUBLANES)
    return pl.ds(pl.multiple_of(i * SUBLANES, SUBLANES), SUBLANES)


def _scan_loop(n_steps, body, init):
    trips = n_steps // SCAN_UNROLL

    def trip(t, carry):
        for u in range(SCAN_UNROLL):
            carry = body(t * SCAN_UNROLL + u, carry)
        return carry

    carry = lax.fori_loop(0, trips, trip, init)
    for step in range(trips * SCAN_UNROLL, n_steps):
        carry = body(step, carry)
    return carry


def _ssm_fwd(u_perm, b_re, b_im, c_re, c_im, lam_r, lam_i, ride):
    T = u_perm.shape[0]
    ls = T // SUBLANES
    rc = min(512, T)
    sp = _scan_specs(T)

    def body(u_ref, bre_ref, bim_ref, cre_ref, cim_ref, lr_ref, li_ref, sre_ref, sim_ref, y_ref):
        cb = pl.program_id(0)
        for c in range(T // rc):
            rows = pl.ds(c * rc, rc)
            sre_ref[rows, :] = _dot(u_ref[rows, :], bre_ref[...])
            sim_ref[rows, :] = _dot(u_ref[rows, :], bim_ref[...])
        shp = (SUBLANES, STATE_BLOCK)
        lr = jnp.broadcast_to(lr_ref[...], shp)
        li = jnp.broadcast_to(li_ref[...], shp)
        zero = jnp.zeros(shp, F32)

        def step(i, carry):
            sr, si = carry
            rows = _rows8(i)
            nr = lr * sr - li * si + sre_ref[rows, :]
            ni = lr * si + li * sr + sim_ref[rows, :]
            sre_ref[rows, :] = nr
            sim_ref[rows, :] = ni
            return nr, ni

        fr, fi = _scan_loop(ls, step, (zero, zero))
        pr, pi_ = _complex_power(lr, li, ls)
        row = lax.broadcasted_iota(jnp.int32, shp, 0)
        ir, ii = zero, zero
        for _ in range(SUBLANES - 1):
            er = fr + pr * ir - pi_ * ii
            ei = fi + pr * ii + pi_ * ir
            ir = jnp.where(row == 0, 0.0, pltpu.roll(er, 1, 0))
            ii = jnp.where(row == 0, 0.0, pltpu.roll(ei, 1, 0))

        def fix(i, carry):
            cr, ci = carry
            rows = _rows8(i)
            nr = lr * cr - li * ci
            ni = lr * ci + li * cr
            sre_ref[rows, :] += nr
            sim_ref[rows, :] += ni
            return nr, ni

        _scan_loop(ls, fix, (ir, ii))
        for c in range(T // rc):
            rows = pl.ds(c * rc, rc)
            yc = _dot(sre_ref[rows, :].astype(BF16), cre_ref[...]) - _dot(sim_ref[rows, :].astype(BF16), cim_ref[...])

            @pl.when(cb % 2 == 0)
            def _():
                y_ref[rows, :] = yc

            @pl.when(cb % 2 == 1)
            def _():
                y_ref[rows, :] += yc

    return _call(body, name='ssm_fwd', grid=(N_STATE // STATE_BLOCK,),
                 in_specs=[sp['chan'], sp['b'], sp['b'], sp['c'], sp['c'], sp['lam'], sp['lam']],
                 out_specs=[sp['state'], sp['state'], sp['chan']],
                 out_shape=[_sds((T, N_STATE)), _sds((T, N_STATE)), _sds((T, D_SSM))],
                 sem=('arbitrary',), vmem=VMEM_BIG, ride=ride)(u_perm, b_re, b_im, c_re, c_im, lam_r, lam_i)


def _ssm_bwd(dy_perm, u_perm, s_re, s_im, b_re, b_im, c_re, c_im, lam_r, lam_i, ride):
    T = u_perm.shape[0]
    ls = T // SUBLANES
    rc = min(512, T)
    sp = _scan_specs(T)
    ncb = N_STATE // STATE_BLOCK

    def body(dy_ref, u_ref, sre_ref, sim_ref, bre_ref, bim_ref, cre_ref, cim_ref, lr_ref, li_ref,
             du_ref, dbr_ref, dbi_ref, dcr_ref, dci_ref, dar_ref, dai_ref, gre_ref, gim_ref):
        cb = pl.program_id(0)
        for c in range(T // rc):
            rows = pl.ds(c * rc, rc)
            gre_ref[rows, :] = _dot_nt(dy_ref[rows, :], cre_ref[...])
            gim_ref[rows, :] = -_dot_nt(dy_ref[rows, :], cim_ref[...])
        shp = (SUBLANES, STATE_BLOCK)
        lr = jnp.broadcast_to(lr_ref[...], shp)
        li = jnp.broadcast_to(li_ref[...], shp)
        zero = jnp.zeros(shp, F32)

        def step(k, carry):
            gr, gi = carry
            rows = _rows8(ls - 1 - k)
            nr = lr * gr + li * gi + gre_ref[rows, :]
            ni = lr * gi - li * gr + gim_ref[rows, :]
            gre_ref[rows, :] = nr
            gim_ref[rows, :] = ni
            return nr, ni

        fr, fi = _scan_loop(ls, step, (zero, zero))
        pr, pi_ = _complex_power(lr, -li, ls)
        row = lax.broadcasted_iota(jnp.int32, shp, 0)
        cr, ci = zero, zero
        for _ in range(SUBLANES - 1):
            er = fr + pr * cr - pi_ * ci
            ei = fi + pr * ci + pi_ * cr
            cr = jnp.where(row == SUBLANES - 1, 0.0, pltpu.roll(er, SUBLANES - 1, 0))
            ci = jnp.where(row == SUBLANES - 1, 0.0, pltpu.roll(ei, SUBLANES - 1, 0))

        def fix(k, carry):
            dr, di, ar, ai = carry
            rows = _rows8(ls - 1 - k)
            dr, di = lr * dr + li * di, lr * di - li * dr
            gr = gre_ref[rows, :] + dr
            gi = gim_ref[rows, :] + di
            gre_ref[rows, :] = gr
            gim_ref[rows, :] = gi
            prev = _rows8(ls - 2 - k)
            spr, spi = sre_ref[prev, :], sim_ref[prev, :]
            return dr, di, ar + gr * spr + gi * spi, ai + gi * spr - gr * spi

        dr, di, ar, ai = _scan_loop(ls - 1, fix, (cr, ci, zero, zero))
        first = pl.ds(0, SUBLANES)
        last = pl.ds((ls - 1) * SUBLANES, SUBLANES)
        gr = gre_ref[first, :] + (lr * dr + li * di)
        gi = gim_ref[first, :] + (lr * di - li * dr)
        gre_ref[first, :] = gr
        gim_ref[first, :] = gi
        spr = jnp.where(row == 0, 0.0, pltpu.roll(sre_ref[last, :], 1, 0))
        spi = jnp.where(row == 0, 0.0, pltpu.roll(sim_ref[last, :], 1, 0))
        dar_ref[...] = _colsum(ar + gr * spr + gi * spi)
        dai_ref[...] = _colsum(ai + gi * spr - gr * spi)

        for c in range(T // rc):
            rows = pl.ds(c * rc, rc)
            g_r, g_i = gre_ref[rows, :].astype(BF16), gim_ref[rows, :].astype(BF16)
            s_r, s_i = sre_ref[rows, :].astype(BF16), sim_ref[rows, :].astype(BF16)
            ub, dyb = u_ref[rows, :], dy_ref[rows, :]
            duc = _dot_nt(g_r, bre_ref[...]) + _dot_nt(g_i, bim_ref[...])
            parts = (_dot_tn(ub, g_r), _dot_tn(ub, g_i), _dot_tn(s_r, dyb), -_dot_tn(s_i, dyb))
            outs = (dbr_ref, dbi_ref, dcr_ref, dci_ref)
            for o_ref, part in zip(outs, parts):
                if c == 0:
                    o_ref[...] = part
                else:
                    o_ref[...] += part

            @pl.when(cb % 2 == 0)
            def _():
                du_ref[rows, :] = duc

            @pl.when(cb % 2 == 1)
            def _():
                du_ref[rows, :] += duc

    blk = lambda r, c: pl.BlockSpec((None, r, c), lambda cb: (cb, 0, 0))
    return _call(body, name='ssm_bwd', grid=(ncb,),
                 in_specs=[sp['chan'], sp['chan'], sp['state'], sp['state'], sp['b'], sp['b'], sp['c'], sp['c'],
                           sp['lam'], sp['lam']],
                 out_specs=[sp['chan'], blk(CHAN_BLOCK, STATE_BLOCK), blk(CHAN_BLOCK, STATE_BLOCK),
                            blk(STATE_BLOCK, CHAN_BLOCK), blk(STATE_BLOCK, CHAN_BLOCK), blk(1, STATE_BLOCK),
                            blk(1, STATE_BLOCK)],
                 out_shape=[_sds((T, D_SSM)), _sds((ncb, CHAN_BLOCK, STATE_BLOCK)), _sds((ncb, CHAN_BLOCK, STATE_BLOCK)),
                            _sds((ncb, STATE_BLOCK, CHAN_BLOCK)), _sds((ncb, STATE_BLOCK, CHAN_BLOCK)),
                            _sds((ncb, 1, STATE_BLOCK)), _sds((ncb, 1, STATE_BLOCK))],
                 scratch=[pltpu.VMEM((T, STATE_BLOCK), F32), pltpu.VMEM((T, STATE_BLOCK), F32)],
                 sem=('arbitrary',), vmem=VMEM_BIG, ride=ride)(dy_perm, u_perm, s_re, s_im, b_re, b_im, c_re, c_im,
                                                               lam_r, lam_i)


def _ffn_dact(ddn, wd4, hid4, tm):
    T = ddn.shape[0]

    def body(d_ref, w_ref, hid_ref, o_ref, act_ref):
        dact = _dot_nt(d_ref[...], w_ref[...])
        silu, dsilu = _silu_parts(hid_ref[0].astype(F32))
        hid_v = hid_ref[1].astype(F32)
        o_ref[0] = (dact * hid_v * dsilu).astype(BF16)
        o_ref[1] = (dact * silu).astype(BF16)
        act_ref[...] = (silu * hid_v).astype(BF16)

    blk = pl.BlockSpec((2, None, tm, FF_SHARD), lambda i, j: (0, j, i, 0))
    return _call(body, name='ffn_dact', grid=(T // tm, 4),
                 in_specs=[pl.BlockSpec((tm, D_MODEL), lambda i, j: (i, 0)),
                           pl.BlockSpec((None, FF_SHARD, D_MODEL), lambda i, j: (j, 0, 0)), blk],
                 out_specs=[blk, pl.BlockSpec((None, tm, FF_SHARD), lambda i, j: (j, i, 0))],
                 out_shape=[_sds((2, 4, T, FF_SHARD), BF16), _sds((4, T, FF_SHARD), BF16)],
                 sem=('parallel', 'parallel'))(ddn, wd4, hid4)


def _ffn_dup(dhid8, up8, cw8, tm, ride):
    T = up8.shape[1]
    nb = T // tm
    ha = _halo_after(tm, T, HALO16)

    def body(dh_ref, dha_ref, up_ref, cw_ref, dup_ref, dcw_ref):
        i = pl.program_id(1)

        @pl.when(i == 0)
        def _():
            dcw_ref[...] = jnp.zeros_like(dcw_ref)

        dh = dh_ref[...].astype(F32)
        dup, dh1, dh2 = _conv3_t(dh, jnp.where(i < nb - 1, dha_ref[...].astype(F32), 0.0), cw_ref)
        dup_ref[...] = dup.astype(BF16)
        up = up_ref[...].astype(F32)
        dcw_ref[0:1, :] += _colsum(dh2 * up)
        dcw_ref[1:2, :] += _colsum(dh1 * up)
        dcw_ref[2:3, :] += _colsum(dh * up)

    main = pl.BlockSpec((None, tm, FF_SHARD), lambda j, i: (j, i, 0))
    return _call(body, name='ffn_dup', grid=(N_DEV, nb),
                 in_specs=[main, pl.BlockSpec((None, HALO16, FF_SHARD), lambda j, i: (j, ha(i), 0)), main,
                           pl.BlockSpec((None, 3, FF_SHARD), lambda j, i: (j, 0, 0))],
                 out_specs=[main, pl.BlockSpec((None, 8, FF_SHARD), lambda j, i: (j, 0, 0))],
                 out_shape=[_sds((N_DEV, T, FF_SHARD), BF16), _sds((N_DEV, 8, FF_SHARD))],
                 sem=('parallel', 'arbitrary'), ride=ride)(dhid8, dhid8, up8, cw8)


def _grad_tn(a, b, a_spec, b_spec, groups, m, n, tk, name, ride=None, parts=1):
    T = a.shape[-2]
    nk = T // tk
    mp = m // parts

    def body(a_ref, b_ref, *refs):
        o_refs, acc_ref = refs[:parts], refs[parts]
        k = pl.program_id(1)
        part = _dot_tn(a_ref[...], b_ref[...])

        @pl.when(k == 0)
        def _():
            acc_ref[...] = part

        @pl.when(k > 0)
        def _():
            acc_ref[...] += part

        @pl.when(k == nk - 1)
        def _():
            for p, o_ref in enumerate(o_refs):
                o_ref[...] = acc_ref[p * mp:(p + 1) * mp, :].astype(BF16)

    out_spec = pl.BlockSpec((None, mp, n), lambda g, k: (g, 0, 0))
    res = _call(body, name=name, grid=(groups, nk), in_specs=[a_spec, b_spec], out_specs=[out_spec] * parts,
                out_shape=[_sds((groups, mp, n), BF16)] * parts, scratch=[pltpu.VMEM((m, n), F32)],
                sem=('parallel', 'arbitrary'), vmem=VMEM_BIG, ride=ride)(a, b)
    if parts > 1:
        return res
    return res[0] if ride is None else (res[0][0], res[1])


def _grad_w_in(h1, dproj, tk, ride):
    T = h1.shape[0]
    nk = T // tk
    half = D_IN_PROJ // 2

    def body(a_ref, b_ref, o_ref, acc_ref):
        k = pl.program_id(0)
        for h in range(2):
            cols = slice(h * half, (h + 1) * half)
            part = _dot_tn(a_ref[...], b_ref[:, cols])

            @pl.when(k == 0)
            def _():
                acc_ref[:, cols] = part

            @pl.when(k > 0)
            def _():
                acc_ref[:, cols] += part

        @pl.when(k == nk - 1)
        def _():
            for g in range(N_DEV):
                o_ref[g] = acc_ref[:, g * IN_SHARD:(g + 1) * IN_SHARD].astype(BF16)

    return _call(body, name='grad_w_in', grid=(nk,),
                 in_specs=[pl.BlockSpec((tk, D_MODEL), lambda k: (k, 0)), pl.BlockSpec((tk, D_IN_PROJ), lambda k: (k, 0))],
                 out_specs=_const((N_DEV, D_MODEL, IN_SHARD)), out_shape=_sds((N_DEV, D_MODEL, IN_SHARD), BF16),
                 scratch=[pltpu.VMEM((D_MODEL, D_IN_PROJ), F32)], sem=('arbitrary',), vmem=VMEM_BIG, ride=ride)(h1, dproj)


def _pre_norm_bwd(dz, dz_spec, w_s, xin, dres, sc, g, tm, name, ride, below=None, group=1, w_t=False):
    T = xin.shape[0]
    n = w_s.shape[1] if w_t else w_s.shape[2]
    mul = _dot if w_t else _dot_nt
    steps = N_DEV // group

    def body(dz_ref, w_ref, x_ref, dr_ref, sc_ref, g_ref, *refs):
        if below is None:
            dx_ref, dsh_ref, dsc_ref, dg_ref = refs
            sums = (dsh_ref, dsc_ref, dg_ref)
        else:
            v_ref, gate_ref, g2_ref, dx_ref, dsh_ref, dsc_ref, dg_ref, dv_ref, dgate_ref, dg2_ref = refs
            sums = (dsh_ref, dsc_ref, dg_ref, dgate_ref, dg2_ref)
        i, j = pl.program_id(0), pl.program_id(1)
        piece = (lambda s: dz_ref[s]) if dz.ndim == 3 else (lambda s: dz_ref[:, s * n:(s + 1) * n])
        part = mul(piece(0), w_ref[0])
        for s in range(1, group):
            part = part + mul(piece(s), w_ref[s])

        @pl.when(jnp.logical_and(i == 0, j == 0))
        def _():
            for s_ref in sums:
                s_ref[...] = jnp.zeros_like(s_ref)

        @pl.when(j == 0)
        def _():
            dx_ref[...] = part

        @pl.when(j > 0)
        def _():
            dx_ref[...] += part

        @pl.when(j == steps - 1)
        def _():
            dh, xv, gv = dx_ref[...], x_ref[...], g_ref[...]
            r = _rsqrt_mean(xv)
            dsh_ref[...] += _colsum(dh)
            dsc_ref[...] += _colsum(dh * (xv * r * gv))
            dxn = dh * (1.0 + sc_ref[...])
            dg_ref[...] += _colsum(dxn * xv * r)
            dx = dr_ref[...] + _norm_bwd(dxn, xv, r, gv)
            dx_ref[...] = dx
            if below is not None:
                v, g2 = v_ref[...], g2_ref[...]
                rv = _rsqrt_mean(v)
                dgate_ref[...] += _colsum(dx * (v * rv * g2))
                dn = dx * gate_ref[...]
                dg2_ref[...] += _colsum(dn * v * rv)
                dv_ref[...] = _norm_bwd(dn, v, rv, g2).astype(BF16)

    row = pl.BlockSpec((tm, D_MODEL), lambda i, j: (i, 0))
    vec = _const((1, D_MODEL))
    in_specs = [dz_spec, pl.BlockSpec((group,) + w_s.shape[1:], lambda i, j: (j, 0, 0)), row, row, vec, vec]
    out_specs = [row, vec, vec, vec]
    out_shape = [_sds((T, D_MODEL)), _sds((1, D_MODEL)), _sds((1, D_MODEL)), _sds((1, D_MODEL))]
    args = [dz, w_s, xin, dres, sc, g]
    if below is not None:
        in_specs += [row, vec, vec]
        out_specs += [row, vec, vec]
        out_shape += [_sds((T, D_MODEL), BF16), _sds((1, D_MODEL)), _sds((1, D_MODEL))]
        args += list(below)
    return _call(body, name=name, grid=(T // tm, steps), in_specs=in_specs, out_specs=out_specs,
                 out_shape=out_shape, sem=('arbitrary', 'arbitrary'), vmem=VMEM_MOST, ride=ride)(*args)


def _mix_bwd(d_o, w_out, yssm, proj, d, glu_w, glu_b, g_ssm, cw, g_conv, avg16, avg64, tm):
    T = yssm.shape[0]
    hb = _halo_before(tm)

    def body(do_ref, wo_ref, y_ref, p_ref, ph_ref, d_ref, gw_ref, gb_ref, gs_ref, cw_ref, gc_ref, a16_ref, a64_ref,
             dy_ref, dconv_ref, dbg_ref, z_ref, dlin_ref, acc_ref):
        i = pl.program_id(0)
        dyc = _dot_nt(do_ref[...], wo_ref[...])

        @pl.when(i == 0)
        def _():
            acc_ref[...] = jnp.zeros_like(acc_ref)

        u = p_ref[:, 0:D_SSM]
        y = y_ref[...] + d_ref[...] * u
        z, t = _gelu(y)
        gate = _sigmoid(_dot(z.astype(BF16), gw_ref[...]) + gb_ref[...])
        ya = z * gate
        rs = lax.rsqrt(_dot_split(ya * ya, a16_ref[...], 2) + EPS)
        dna = dyc[:, 0:D_SSM]
        acc_ref[1:2, :] += _colsum(dna * ya * rs)
        dya = _head_norm_bwd(dna, ya, rs, gs_ref[...], a16_ref[...])
        dlin = dya * z * gate * (1.0 - gate)
        acc_ref[0:1, :] += _colsum(dlin)
        dlin_b = dlin.astype(BF16)
        dz = dya * gate + _dot_nt(dlin_b, gw_ref[...])
        dy = dz * _gelu_grad(y, t)
        acc_ref[3:4, :] += _colsum(dy * u)
        dy_ref[...] = dy
        z_ref[...] = z.astype(BF16)
        dlin_ref[...] = dlin_b

        bg = p_ref[:, D_SSM:D_SSM + D_CONV]
        cv = p_ref[:, D_SSM + D_CONV:D_SSM + 2 * D_CONV] * p_ref[:, D_SSM + 2 * D_CONV:D_IN_PROJ]
        hv = ph_ref[:, D_SSM + D_CONV:D_SSM + 2 * D_CONV] * ph_ref[:, D_SSM + 2 * D_CONV:D_IN_PROJ]
        hv = jnp.where(i > 0, hv, 0.0)
        conv, cv1, cv2 = _conv3(cv, hv, cw_ref)
        yb = bg * conv
        rsb = lax.rsqrt(_dot_split(yb * yb, a64_ref[...], 2) + EPS)
        dnb = dyc[:, D_SSM:D_MODEL]
        acc_ref[2:3, :] += _colsum(dnb * yb * rsb)
        dyb = _head_norm_bwd(dnb, yb, rsb, gc_ref[...], a64_ref[...])
        dbg_ref[...] = dyb * conv
        dconv = dyb * bg
        dconv_ref[...] = dconv
        acc_ref[4:5, :] += _colsum(dconv * cv2)
        acc_ref[5:6, :] += _colsum(dconv * cv1)
        acc_ref[6:7, :] += _colsum(dconv * cv)

    vec = _const((1, D_SSM))
    sq = _const((D_SSM, D_SSM))
    half = pl.BlockSpec((tm, D_SSM), lambda i: (i, 0))
    return _call(body, name='mix_bwd', grid=(T // tm,),
                 in_specs=[pl.BlockSpec((tm, D_MODEL), lambda i: (i, 0)), _const((D_MODEL, D_MODEL)), half,
                           pl.BlockSpec((tm, D_IN_PROJ), lambda i: (i, 0)),
                           pl.BlockSpec((HALO, D_IN_PROJ), lambda i: (hb(i), 0)), vec, sq, vec, vec,
                           _const((3, D_CONV)), vec, sq, sq],
                 out_specs=[half, half, half, half, half, _const((8, D_SSM))],
                 out_shape=[_sds((T, D_SSM)), _sds((T, D_SSM)), _sds((T, D_SSM)), _sds((T, D_SSM), BF16),
                            _sds((T, D_SSM), BF16), _sds((8, D_SSM))],
                 sem=('arbitrary',), vmem=VMEM_BIG)(d_o, w_out, yssm, proj, proj, d, glu_w, glu_b, g_ssm, cw, g_conv,
                                                   avg16, avg64)


def _mix_bwd_proj(dconv, proj, du_ssm, dy, d, dbg, cw, tm):
    T = dy.shape[0]
    nb = T // tm
    ha = _halo_after(tm, T)

    def body(dc_ref, dch_ref, cg_ref, v_ref, du_ref, dy_ref, d_ref, dbg_ref, cw_ref, o_ref):
        i = pl.program_id(0)
        dcv = _conv3_t(dc_ref[...], jnp.where(i < nb - 1, dch_ref[...], 0.0), cw_ref)[0]
        o_ref[:, 0:D_SSM] = (du_ref[...] + dy_ref[...] * d_ref[...]).astype(BF16)
        o_ref[:, D_SSM:D_SSM + D_CONV] = dbg_ref[...].astype(BF16)
        o_ref[:, D_SSM + D_CONV:D_SSM + 2 * D_CONV] = (dcv * v_ref[...]).astype(BF16)
        o_ref[:, D_SSM + 2 * D_CONV:D_IN_PROJ] = (dcv * cg_ref[...]).astype(BF16)

    half = pl.BlockSpec((tm, D_SSM), lambda i: (i, 0))
    return _call(body, name='mix_bwd_proj', grid=(nb,),
                 in_specs=[half, pl.BlockSpec((HALO, D_CONV), lambda i: (ha(i), 0)),
                           pl.BlockSpec((tm, D_CONV), lambda i: (i, 2)), pl.BlockSpec((tm, D_CONV), lambda i: (i, 3)),
                           half, half, _const((1, D_SSM)), half, _const((3, D_CONV))],
                 out_specs=pl.BlockSpec((tm, D_IN_PROJ), lambda i: (i, 0)), out_shape=_sds((T, D_IN_PROJ), BF16),
                 sem=('parallel',))(dconv, dconv, proj, proj, du_ssm, dy, d, dbg, cw)


def _row_tile(rows, cols, slots):
    for cand in (512, 256, 128, 64, 32, 16, 8):
        if rows % cand == 0 and slots * cand * cols * 4 <= (2 << 20):
            return cand
    return rows


def _adamw_math(g, w, m, v):
    m2 = ADAM_B1 * m + (1.0 - ADAM_B1) * g
    v2 = ADAM_B2 * v + (1.0 - ADAM_B2) * (g * g)
    m_hat = m2 / (1.0 - ADAM_B1 ** ADAM_STEP)
    v_hat = v2 / (1.0 - ADAM_B2 ** ADAM_STEP)
    return -ADAM_LR * (m_hat / (jnp.sqrt(v_hat) + ADAM_EPS) + ADAM_WD * w), m2, v2


def _adamw(pieces, w, m, v, name):
    slots, _, cols = pieces[0].shape
    rows = sum(p.shape[1] for p in pieces)
    tr = _row_tile(pieces[0].shape[1], cols, slots)
    starts, pos = [], 0
    for p in pieces:
        assert p.shape[1] % tr == 0
        starts.append(pos)
        pos += p.shape[1] // tr

    def body(*refs):
        g_refs = refs[:len(pieces)]
        w_ref, m_ref, v_ref, go_ref, d_ref, mo_ref, vo_ref = refs[len(pieces):]
        i = pl.program_id(0)
        g = None
        for g_ref, start in zip(g_refs, starts):
            part = g_ref[0].astype(F32)
            for s in range(1, slots):
                part = part + g_ref[s].astype(F32)
            g = part if g is None else jnp.where(i >= start, part, g)
        go_ref[...] = g
        d_ref[...], mo_ref[...], vo_ref[...] = _adamw_math(g, w_ref[...], m_ref[...], v_ref[...])

    def piece_spec(start, count):
        return pl.BlockSpec((slots, tr, cols), lambda i: (0, jnp.clip(i - start, 0, count - 1), 0))

    blk = pl.BlockSpec((tr, cols), lambda i: (i, 0))
    return _call(body, name=name, grid=(rows // tr,),
                 in_specs=[piece_spec(s, p.shape[1] // tr) for s, p in zip(starts, pieces)] + [blk, blk, blk],
                 out_specs=[blk] * 4, out_shape=[_sds((rows, cols))] * 4, sem=('parallel',))(*pieces, w, m, v)


def _to_scan_rows(a):
    T, n = a.shape
    return a.reshape(SUBLANES, T // SUBLANES, n).transpose(1, 0, 2).reshape(T, n)


def _from_scan_rows(a):
    T, n = a.shape
    return a.reshape(T // SUBLANES, SUBLANES, n).transpose(1, 0, 2).reshape(T, n)


def _expand(a):
    return jnp.repeat(a, SSM_GROUP, axis=1)


def _block_diag_b(bb):
    eye = jnp.eye(N_GROUPS, dtype=bb.dtype)
    return (bb.transpose(0, 2, 1)[:, :, None, :] * eye[:, None, :, None]).reshape(D_SSM, N_STATE)


def _block_diag_c(cc):
    eye = jnp.eye(N_GROUPS, dtype=cc.dtype)
    return (cc.transpose(0, 2, 1)[:, :, None, :] * eye[:, None, :, None]).reshape(N_STATE, D_SSM)


def _diag_blocks(x, chan_major):
    e2 = jnp.eye(2, dtype=x.dtype)
    e4 = jnp.eye(4, dtype=x.dtype)
    if chan_major:
        x = x.reshape(4, 2, 2, 4, SSM_GROUP, 4, SSM_STATE)
        x = x * e2[None, :, :, None, None, None, None] * e4[None, None, None, :, None, :, None]
        return x.sum(axis=(2, 3)).transpose(0, 1, 3, 4, 2).reshape(N_GROUPS, SSM_STATE, SSM_GROUP)
    x = x.reshape(4, 2, 4, SSM_STATE, 2, 4, SSM_GROUP)
    x = x * e2[None, :, None, None, :, None, None] * e4[None, None, :, None, None, :, None]
    return x.sum(axis=(4, 5)).reshape(N_GROUPS, SSM_STATE, SSM_GROUP)


SMALL_LAYOUT = {
    'ssm_b_re': (0, 0, 32, 1024), 'ssm_b_im': (32, 0, 32, 1024), 'ssm_c_re': (64, 0, 32, 1024),
    'ssm_c_im': (96, 0, 32, 1024), 'b_ada': (128, 0, 6, 1024), 'g_pre_mix': (134, 0, 1, 1024),
    'g_post_mix': (135, 0, 1, 1024), 'ssm_lam_re': (136, 0, 2, 1024), 'ssm_lam_im': (138, 0, 2, 1024),
    'ssm_log_step': (140, 0, 1, 32), 'glu_b': (141, 0, 1, 512), 'g_out_ssm': (141, 512, 1, 512),
    'g_out_conv': (142, 0, 1, 512), 'ssm_d': (142, 512, 1, 512), 'g_pre_ffn': (143, 0, 1, 1024),
    'g_post_ffn': (144, 0, 1, 1024)}
SMALL_ROWS = 152
B_ADA_ROW = SMALL_LAYOUT['b_ada'][0]
LATE_ROWS = {('b_ada', 0): 0, ('b_ada', 1): 1, ('g_pre_mix', 0): 2}


def _adamw_small(gathered, late, wts, mom_m, mom_v):
    names = list(SMALL_LAYOUT)
    n = len(names)

    def body(*refs):
        g_ref, late_ref, ins, outs = refs[0], refs[1], refs[2:2 + 3 * n], refs[2 + 3 * n:]
        for p, name in enumerate(names):
            r0, c0, rows, cols = SMALL_LAYOUT[name]
            pieces = [(0, rows)] if rows % 8 == 0 else [(r, 1) for r in range(rows)]
            for r, cnt in pieces:
                src_ref, first = (late_ref, LATE_ROWS[name, r]) if (name, r) in LATE_ROWS else (g_ref, r0 + r)
                g = src_ref[0, first:first + cnt, c0:c0 + cols]
                for s in range(1, N_DEV):
                    g = g + src_ref[s, first:first + cnt, c0:c0 + cols]
                w, m, v = (ins[3 * p + q][r:r + cnt, :] for q in range(3))
                res = (g,) + _adamw_math(g, w, m, v)
                for q in range(4):
                    outs[4 * p + q][r:r + cnt, :] = res[q]

    shapes = [SMALL_LAYOUT[name][2:] for name in names]
    args = [gathered, late]
    for name, shp in zip(names, shapes):
        args += [wts[name].reshape(shp), mom_m[name].reshape(shp), mom_v[name].reshape(shp)]
    outs = _call(body, name='adamw_small', grid=(1,),
                 in_specs=[_const(gathered.shape), _const(late.shape)]
                 + [_const(shp) for shp in shapes for _ in range(3)],
                 out_specs=[_const(shp) for shp in shapes for _ in range(4)],
                 out_shape=[_sds(shp) for shp in shapes for _ in range(4)], vmem=VMEM_BIG)(*args)
    res = {}
    for p, name in enumerate(names):
        for q, kind in enumerate(('g', 'd', 'm', 'v')):
            res[kind, name] = outs[4 * p + q].reshape(wts[name].shape)
    return res


def kernel(x, c, w_ada, b_ada, g_pre_mix, g_post_mix, w_in, ssm_lam_re, ssm_lam_im, ssm_log_step, ssm_b_re, ssm_b_im, ssm_c_re, ssm_c_im, ssm_d, glu_w, glu_b, g_out_ssm, conv_w, g_out_conv, w_out, g_pre_ffn, g_post_ffn, w_up, ffn_conv_w, w_down, loss_target, m_w_ada, m_b_ada, m_g_pre_mix, m_g_post_mix, m_w_in, m_ssm_lam_re, m_ssm_lam_im, m_ssm_log_step, m_ssm_b_re, m_ssm_b_im, m_ssm_c_re, m_ssm_c_im, m_ssm_d, m_glu_w, m_glu_b, m_g_out_ssm, m_conv_w, m_g_out_conv, m_w_out, m_g_pre_ffn, m_g_post_ffn, m_w_up, m_ffn_conv_w, m_w_down, v_w_ada, v_b_ada, v_g_pre_mix, v_g_post_mix, v_w_in, v_ssm_lam_re, v_ssm_lam_im, v_ssm_log_step, v_ssm_b_re, v_ssm_b_im, v_ssm_c_re, v_ssm_c_im, v_ssm_d, v_glu_w, v_glu_b, v_g_out_ssm, v_conv_w, v_g_out_conv, v_w_out, v_g_pre_ffn, v_g_post_ffn, v_w_up, v_ffn_conv_w, v_w_down):
    args = dict(locals())
    wts = {n: args[n] for n in WEIGHTS}
    mom_m = {n: args['m_' + n] for n in WEIGHTS}
    mom_v = {n: args['v_' + n] for n in WEIGHTS}
    T = x.shape[1]
    tm = min(512, T)
    tw = min(1024, T)
    me = _me()[3]
    xt, tgt = x[0], loss_target[0]

    c_all, w_in_s, glu_s, w_out_s, conv_s = _exchange(
        [c, w_in[0].astype(BF16), glu_w[0].astype(BF16), w_out[0].astype(BF16), conv_w[0]], name='gather_first',
        scatter=False)
    c_all = c_all.reshape(N_DEV, D_MODEL)
    b_cols = lax.dynamic_slice(b_ada, (0, me * ADA_SHARD), (1, ADA_SHARD))
    mod_cols, c_act = _mod_cols(c_all, w_ada[0], b_cols)
    (mod_all,) = _exchange([mod_cols], name='gather_mod', scatter=False)
    mod = lax.dynamic_slice(mod_all, (0, me, 0), (N_DEV, 1, ADA_SHARD)).reshape(N_MOD, 1, D_MODEL)
    sh1, sc1, gt1, sh2, sc2, gt2 = [mod[k] for k in range(N_MOD)]

    glu_full = glu_s.reshape(D_SSM, D_SSM)
    w_out_full = w_out_s.reshape(D_MODEL, D_MODEL)
    cw_full = conv_s.transpose(1, 0, 2).reshape(3, D_CONV)

    lre_x, lim_x = _expand(ssm_lam_re[0]), _expand(ssm_lam_im[0])
    lst_x = jnp.broadcast_to(ssm_log_step[0][:, None], (N_GROUPS, SSM_STATE * SSM_GROUP))
    b_re_x = ssm_b_re[0].reshape(N_GROUPS, -1)
    b_im_x = ssm_b_im[0].reshape(N_GROUPS, -1)
    ar_x, ai_x, bbr_x, bbi_x = _ssm_prep(lre_x, lim_x, lst_x, b_re_x, b_im_x)
    lam_r = ar_x[:, ::SSM_GROUP].reshape(1, N_STATE)
    lam_i = ai_x[:, ::SSM_GROUP].reshape(1, N_STATE)
    big_b_re = _block_diag_b(bbr_x.reshape(N_GROUPS, SSM_STATE, SSM_GROUP)).astype(BF16)
    big_b_im = _block_diag_b(bbi_x.reshape(N_GROUPS, SSM_STATE, SSM_GROUP)).astype(BF16)
    big_c_re = _block_diag_c(ssm_c_re[0]).astype(BF16)
    big_c_im = _block_diag_c(ssm_c_im[0]).astype(BF16)
    head = jnp.arange(D_SSM)
    avg16 = jnp.where(head[:, None] // SSM_GROUP == head[None, :] // SSM_GROUP, 1.0 / SSM_GROUP, 0.0).astype(BF16)
    hd = D_CONV // CONV_HEADS
    avg64 = jnp.where(head[:, None] // hd == head[None, :] // hd, 1.0 / hd, 0.0).astype(BF16)

    (proj, h1), (w_down_s, ffn_conv_s) = _pre_mix(xt, sc1, sh1, g_pre_mix, w_in_s, tw,
                                                  ([w_down[0].astype(BF16), ffn_conv_w[0]], False))
    wd4 = w_down_s.reshape(4, FF_SHARD, D_MODEL)
    u_perm = _to_scan_rows(proj[:, :D_SSM]).astype(BF16)
    (s_re, s_im, y_perm), (w_up_s,) = _ssm_fwd(u_perm, big_b_re, big_b_im, big_c_re, big_c_im, lam_r, lam_i,
                                               ([w_up[0].T.astype(BF16)], False))
    yssm = _from_scan_rows(y_perm)
    mix_args = (ssm_d, glu_full, glu_b, g_out_ssm, cw_full, g_out_conv, avg16, avg64)
    ycat = _mix_fwd(yssm, proj, *mix_args, tm)
    o, x1, h2 = _out_proj(ycat, w_out_full, xt, gt1, g_post_mix, g_pre_ffn, sc2, sh2, tm)
    up8, hid8 = _ffn_up(h2, w_up_s, ffn_conv_s, tw)
    hid4 = hid8.reshape(2, 4, T, FF_SHARD)
    ddn, dx2, loss_parts, d_gt2, d_g_post_ffn = _ffn_down(hid4, wd4, x1, tgt, gt2, g_post_ffn, tm)
    loss_local = jnp.sum(loss_parts[:, 0, 0])

    got = {}
    dhid, act = _ffn_dact(ddn, wd4, hid4, tm)
    g_w_down = _grad_tn(act, ddn, pl.BlockSpec((None, tw, FF_SHARD), lambda g, k: (g, k, 0)),
                        pl.BlockSpec((tw, D_MODEL), lambda g, k: (k, 0)), 4, FF_SHARD, D_MODEL, tw, 'grad_w_down')
    (dup8, dcw_ffn), (got['w_down'],) = _ffn_dup(dhid.reshape(N_DEV, T, FF_SHARD), up8, ffn_conv_s, tm,
                                                 ([g_w_down.reshape(N_DEV, D_FF // N_DEV, D_MODEL)], True))
    g_w_up_halves = _grad_tn(dup8, h2, pl.BlockSpec((None, tw, FF_SHARD), lambda g, k: (g, k, 0)),
                             pl.BlockSpec((tw, D_MODEL), lambda g, k: (k, 0)), N_DEV, FF_SHARD, D_MODEL, tw,
                             'grad_w_up', parts=2)
    (dx1, d_sh2, d_sc2, d_g_pre_ffn, d_o, d_gt1, d_g_post_mix), (got_up_0, got['ffn_conv_w']) = _pre_norm_bwd(
        dup8, pl.BlockSpec((2, tw, FF_SHARD), lambda i, j: (j, i, 0)), w_up_s, x1, dx2, sc2, g_pre_ffn, tw,
        'ffn_in_bwd', ([g_w_up_halves[0], dcw_ffn], True), below=(o, gt1, g_post_mix), group=2, w_t=True)

    g_w_out = _grad_tn(ycat, d_o, pl.BlockSpec((tw, D_MODEL), lambda g, k: (k, 0)),
                       pl.BlockSpec((tw, D_MODEL), lambda g, k: (k, 0)), 1, D_MODEL, D_MODEL, tw, 'grad_w_out')
    dy, dconv, dbg, z_b, dlin_b, sums = _mix_bwd(d_o, w_out_full, yssm, proj, *mix_args, tm)
    g_glu_w = _grad_tn(z_b, dlin_b, pl.BlockSpec((tw, D_SSM), lambda g, k: (k, 0)),
                       pl.BlockSpec((tw, D_SSM), lambda g, k: (k, 0)), 1, D_SSM, D_SSM, tw, 'grad_glu_w')
    dy_perm = _to_scan_rows(dy).astype(BF16)
    (du_perm, dbr_blk, dbi_blk, dcr_blk, dci_blk, dar_blk, dai_blk), (got_up_1, got['w_out'], got['glu_w']) = _ssm_bwd(
        dy_perm, u_perm, s_re, s_im, big_b_re, big_b_im, big_c_re, big_c_im, lam_r, lam_i,
        ([g_w_up_halves[1], g_w_out.reshape(N_DEV, D_MODEL // N_DEV, D_MODEL),
          g_glu_w.reshape(N_DEV, D_SSM // N_DEV, D_SSM)], True))
    du_ssm = _from_scan_rows(du_perm)
    dproj = _mix_bwd_proj(dconv, proj, du_ssm, dy, ssm_d, dbg, cw_full, tm)
    dbb_re = _diag_blocks(dbr_blk, True).reshape(N_GROUPS, -1)
    dbb_im = _diag_blocks(dbi_blk, True).reshape(N_GROUPS, -1)
    d_c_re = _diag_blocks(dcr_blk, False).transpose(0, 2, 1)
    d_c_im = _diag_blocks(dci_blk, False).transpose(0, 2, 1)
    lane = jnp.arange(SSM_STATE * SSM_GROUP)
    seg = jnp.where(lane[:, None] // SSM_GROUP == lane[None, :] // SSM_GROUP, 1.0, 0.0).astype(BF16)
    d_b_re_x, d_b_im_x, d_lre_x, d_lim_x, d_lst = _ssm_prep_bwd(
        lre_x, lim_x, lst_x, b_re_x, b_im_x, dbb_re, dbb_im, _expand(dar_blk.reshape(N_GROUPS, SSM_STATE)),
        _expand(dai_blk.reshape(N_GROUPS, SSM_STATE)), seg)

    row = lambda a: a.reshape(-1, PACK_COLS)
    blank = jnp.zeros((1, PACK_COLS), F32)
    small_pack = jnp.concatenate([
        d_b_re_x, d_b_im_x, row(d_c_re), row(d_c_im), blank, blank, d_gt1, d_sh2, d_sc2, d_gt2, blank,
        d_g_post_mix, row(d_lre_x[:, ::SSM_GROUP]), row(d_lim_x[:, ::SSM_GROUP]),
        jnp.pad(d_lst.reshape(1, N_GROUPS), ((0, 0), (0, PACK_COLS - N_GROUPS))), row(sums[0:4]), d_g_pre_ffn,
        d_g_post_ffn, jnp.zeros((SMALL_ROWS - 145, PACK_COLS), F32)])
    g_w_in, (small_all,) = _grad_w_in(h1, dproj, tw, ([small_pack], False))
    g_conv_slots = jnp.concatenate([sums[4:7], jnp.zeros((5, D_CONV), F32)]).reshape(
        8, N_DEV, D_CONV // N_DEV).transpose(1, 0, 2)
    (grad_x, d_sh1, d_sc1, d_g_pre_mix), (got['w_in'], got['conv_w']) = _pre_norm_bwd(
        dproj, pl.BlockSpec((tw, 4 * IN_SHARD), lambda i, j: (i, j)), w_in_s, xt, dx1, sc1, g_pre_mix, tw,
        'mix_in_bwd', ([g_w_in, g_conv_slots], True), group=4)
    late_pack = jnp.concatenate([d_sh1, d_sc1, d_g_pre_mix, jnp.full((1, PACK_COLS), loss_local, F32),
                                 jnp.zeros((4, PACK_COLS), F32)])
    (late_all,) = _exchange([late_pack], name='gather_late_grads', scatter=False)
    loss = jnp.sum(late_all[:, 3, 0])
    res = _adamw_small(small_all, late_all, wts, mom_m, mom_v)

    dmod_all = jnp.concatenate([late_all[:, 0:2, :], small_all[:, B_ADA_ROW + 2:B_ADA_ROW + N_MOD, :]],
                               axis=1).reshape(N_DEV, N_MOD * D_MODEL)
    dmod_cols = lax.dynamic_slice(dmod_all, (0, me * ADA_SHARD), (N_DEV, ADA_SHARD))
    g_w_ada = _grad_w_ada(c_act.T, dmod_cols)

    pieces = {n: [slots[:, :3, :] if n in ('conv_w', 'ffn_conv_w') else slots] for n, slots in got.items()}
    for n, parts in pieces.items():
        outs = _adamw(parts, wts[n][0], mom_m[n][0], mom_v[n][0], 'adamw_' + n)
        for kind, val in zip(('g', 'd', 'm', 'v'), outs):
            res[kind, n] = val[None]
    outs = _adamw([got_up_0, got_up_1], w_up[0].T, m_w_up[0].T, v_w_up[0].T, 'adamw_w_up')
    for kind, val in zip(('g', 'd', 'm', 'v'), outs):
        res[kind, 'w_up'] = val.T[None]
    outs = _adamw([g_w_ada[None]], w_ada[0], m_w_ada[0], v_w_ada[0], 'adamw_w_ada')
    for kind, val in zip(('g', 'd', 'm', 'v'), outs):
        res[kind, 'w_ada'] = val[None]

    return (loss, grad_x[None], *[res['g', n] for n in WEIGHTS], *[res['d', n] for n in WEIGHTS],
            *[res['m', n] for n in WEIGHTS], *[res['v', n] for n in WEIGHTS])
```

```python
import math

import jax
import jax.numpy as jnp
from jax import lax
from jax.experimental import pallas as pl
from jax.experimental.pallas import tpu as pltpu

F32, BF16 = jnp.float32, jnp.bfloat16

D_MODEL = 1024
D_SSM = 512
D_CONV = 512
SSM_GROUP = 16
N_GROUPS = 32
SSM_STATE = 64
N_STATE = N_GROUPS * SSM_STATE
CONV_HEADS = 8
D_FF = 2816
N_MOD = 6
D_IN_PROJ = D_SSM + 3 * D_CONV
N_DEV = 8
FF_SHARD = 2 * D_FF // N_DEV
IN_SHARD = D_IN_PROJ // N_DEV
ADA_SHARD = N_MOD * D_MODEL // N_DEV
EPS = 1e-6
LAMBDA_RE_MAX = -1e-4
ADAM_LR, ADAM_B1, ADAM_B2, ADAM_EPS, ADAM_WD, ADAM_STEP = 0.001, 0.9, 0.999, 1e-08, 0.01, 10
GELU_C = math.sqrt(2.0 / math.pi)
GELU_A = 0.044715

SUBLANES = 8
HALO = 8
HALO16 = 16
SCAN_UNROLL = 8
STATE_BLOCK = 256
CHAN_BLOCK = 128
VMEM_BIG = 48 << 20
VMEM_MOST = 58 << 20

WEIGHTS = ['w_ada', 'b_ada', 'g_pre_mix', 'g_post_mix', 'w_in', 'ssm_lam_re', 'ssm_lam_im', 'ssm_log_step',
           'ssm_b_re', 'ssm_b_im', 'ssm_c_re', 'ssm_c_im', 'ssm_d', 'glu_w', 'glu_b', 'g_out_ssm', 'conv_w',
           'g_out_conv', 'w_out', 'g_pre_ffn', 'g_post_ffn', 'w_up', 'ffn_conv_w', 'w_down']
SHARDED = ('w_ada', 'w_in', 'glu_w', 'conv_w', 'w_out', 'w_up', 'ffn_conv_w', 'w_down')
PACK_COLS = 1024


def _call(body, *, name, grid, in_specs, out_specs, out_shape, scratch=(), sem=None, vmem=None, ride=None):
    params = {}
    if vmem is not None:
        params['vmem_limit_bytes'] = vmem
    if ride is None:
        if sem is not None:
            params['dimension_semantics'] = sem
        return pl.pallas_call(body, name=name, grid=grid, in_specs=in_specs, out_specs=out_specs,
                              out_shape=out_shape, scratch_shapes=list(scratch),
                              compiler_params=pltpu.CompilerParams(**params))
    arrs, scatter = ride
    single = not isinstance(out_shape, (list, tuple))
    out_shape_l = [out_shape] if single else list(out_shape)
    out_specs_l = [out_specs] if single else list(out_specs)
    n, n_in, n_out, n_scr = len(arrs), len(in_specs), len(out_shape_l), len(scratch)
    any_spec = pl.BlockSpec(memory_space=pl.ANY)
    params['dimension_semantics'] = ('arbitrary',) * len(grid)

    def carried(*refs):
        ins, rin = refs[:n_in], refs[n_in:n_in + n]
        outs, rout = refs[n_in + n:n_in + n + n_out], refs[n_in + n + n_out:n_in + 2 * n + n_out]
        scr, sems = refs[n_in + 2 * n + n_out:n_in + 2 * n + n_out + n_scr], refs[n_in + 2 * n + n_out + n_scr:]
        first = pl.program_id(0) == 0
        last = pl.program_id(0) == grid[0] - 1
        for ax in range(1, len(grid)):
            first = jnp.logical_and(first, pl.program_id(ax) == 0)
            last = jnp.logical_and(last, pl.program_id(ax) == grid[ax] - 1)

        @pl.when(first)
        def _():
            _exchange_start(rin, rout, sems, scatter)

        body(*ins, *outs, *scr)

        @pl.when(last)
        def _():
            _exchange_wait(rin, rout, sems, scatter)

    call = pl.pallas_call(carried, name=name, grid=grid, in_specs=list(in_specs) + [any_spec] * n,
                          out_specs=out_specs_l + [any_spec] * n,
                          out_shape=out_shape_l + _exchange_shapes(arrs, scatter),
                          scratch_shapes=list(scratch) + _exchange_sems(n),
                          compiler_params=pltpu.CompilerParams(**params))

    def run(*args):
        res = call(*args, *arrs)
        own = res[0] if single else list(res[:n_out])
        return own, list(res[n_out:])

    return run


def _const(shape):
    nd = len(shape)
    return pl.BlockSpec(shape, lambda *_: (0,) * nd)


def _sds(shape, dtype=F32):
    return jax.ShapeDtypeStruct(shape, dtype)


def _dot(a, b):
    return jnp.dot(a, b, preferred_element_type=F32)


def _dot_nt(a, b):
    return lax.dot_general(a, b, (((1,), (1,)), ((), ())), preferred_element_type=F32)


def _dot_tn(a, b):
    return lax.dot_general(a, b, (((0,), (0,)), ((), ())), preferred_element_type=F32)


def _dot_split(x, mat, parts):
    acc = None
    rem = x
    for _ in range(parts):
        piece = rem.astype(BF16)
        rem = rem - piece.astype(F32)
        term = _dot(piece, mat)
        acc = term if acc is None else acc + term
    return acc


def _sigmoid(x):
    return 1.0 / (1.0 + jnp.exp(-x))


def _gelu(x):
    t = jnp.tanh(GELU_C * (x + GELU_A * x * x * x))
    return 0.5 * x * (1.0 + t), t


def _gelu_grad(x, t):
    return 0.5 * (1.0 + t) + 0.5 * x * (1.0 - t * t) * GELU_C * (1.0 + 3.0 * GELU_A * x * x)


def _rsqrt_mean(x):
    return lax.rsqrt(jnp.mean(x * x, axis=-1, keepdims=True) + EPS)


def _colsum(x):
    return jnp.sum(x, axis=0, keepdims=True)


def _shifts_down(x, halo):
    ext = jnp.concatenate([halo, x], axis=0)
    return pltpu.roll(ext, 1, 0)[halo.shape[0]:], pltpu.roll(ext, 2, 0)[halo.shape[0]:]


def _shifts_up(x, halo):
    n = x.shape[0]
    ext = jnp.concatenate([x, halo], axis=0)
    total = ext.shape[0]
    return pltpu.roll(ext, total - 1, 0)[:n], pltpu.roll(ext, total - 2, 0)[:n]


def _conv3(x, halo, w_ref):
    x1, x2 = _shifts_down(x, halo)
    return w_ref[0:1, :] * x2 + w_ref[1:2, :] * x1 + w_ref[2:3, :] * x, x1, x2


def _conv3_t(g, halo, w_ref):
    g1, g2 = _shifts_up(g, halo)
    return w_ref[2:3, :] * g + w_ref[1:2, :] * g1 + w_ref[0:1, :] * g2, g1, g2


def _silu_parts(x):
    s = _sigmoid(x)
    return x * s, s * (1.0 + x * (1.0 - s))


def _norm_bwd(dn, x, r, g):
    gd = g * dn
    return r * gd - x * (r * r * r) * jnp.mean(gd * x, axis=-1, keepdims=True)


def _head_norm_bwd(dn, y, rs, g, avg):
    gd = g * dn
    return rs * gd - y * (rs * rs * rs) * _dot_split(gd * y, avg, 2)


def _me():
    x, y, c = lax.axis_index('x'), lax.axis_index('y'), lax.axis_index('c')
    return x, y, c, 4 * x + 2 * y + c


def _peer(k):
    x, y, c, _ = _me()
    px = 1 - x if k & 4 else x
    py = 1 - y if k & 2 else y
    pc = 1 - c if k & 1 else c
    return (px, py, pc), 4 * px + 2 * py + pc


SIBLING = 1
OTHER_CHIPS = (2, 4, 6)


def _remote(src, dst, sems, a, k, dev):
    return pltpu.make_async_remote_copy(src_ref=src, dst_ref=dst, send_sem=sems[0].at[a, k - 1],
                                        recv_sem=sems[1].at[a, k - 1], device_id=dev,
                                        device_id_type=pl.DeviceIdType.MESH)


def _exchange_copies(ins, outs, sems, scatter):
    me = _me()[3]
    local, first, relay, arrivals = [], [], [], []
    for a in range(len(ins)):
        src = ins[a].at[me] if scatter else ins[a]
        local.append(pltpu.make_async_copy(src, outs[a].at[me], sems[2].at[a]))
        for k in range(1, N_DEV):
            dev, idx = _peer(k)
            landed = _remote(src, outs[a].at[idx], sems, a, k, dev)
            if scatter:
                first.append(_remote(ins[a].at[idx], outs[a].at[me], sems, a, k, dev))
                arrivals.append(landed)
            elif k == SIBLING:
                first.append(_remote(src, outs[a].at[me], sems, a, k, dev))
                arrivals.append(landed)
            elif k in OTHER_CHIPS:
                first.append(_remote(src, outs[a].at[me], sems, a, k, dev))
                sib, _ = _peer(SIBLING)
                relay.append((landed, _remote(outs[a].at[idx], outs[a].at[idx], sems, a, k | SIBLING, sib)))
            else:
                arrivals.append(landed)
    return local, first, relay, arrivals


def _exchange_start(ins, outs, sems, scatter):
    local, first, _, _ = _exchange_copies(ins, outs, sems, scatter)
    for cp in local + first:
        cp.start()


def _exchange_wait(ins, outs, sems, scatter):
    local, first, relay, arrivals = _exchange_copies(ins, outs, sems, scatter)
    for landed, forward in relay:
        landed.wait_recv()
        forward.start()
    for cp in arrivals:
        cp.wait_recv()
    for cp in first + [forward for _, forward in relay]:
        cp.wait_send()
    for cp in local:
        cp.wait()


def _exchange_shapes(arrs, scatter):
    return [_sds(a.shape if scatter else (N_DEV,) + a.shape, a.dtype) for a in arrs]


def _exchange_sems(n):
    return [pltpu.SemaphoreType.DMA((n, N_DEV - 1)), pltpu.SemaphoreType.DMA((n, N_DEV - 1)),
            pltpu.SemaphoreType.DMA((n,))]


def _exchange(arrs, *, name, scatter):
    n = len(arrs)

    def body(*refs):
        _exchange_start(refs[:n], refs[n:2 * n], refs[2 * n:], scatter)
        _exchange_wait(refs[:n], refs[n:2 * n], refs[2 * n:], scatter)

    any_spec = pl.BlockSpec(memory_space=pl.ANY)
    outs = pl.pallas_call(body, name=name, out_shape=_exchange_shapes(arrs, scatter), in_specs=[any_spec] * n,
                          out_specs=[any_spec] * n, scratch_shapes=_exchange_sems(n))(*arrs)
    return list(outs)


def _mod_cols(c_all, w_ada, b_cols):
    def body(c_ref, w_ref, b_ref, mod_ref, act_ref):
        c = c_ref[...]
        act = c * _sigmoid(c)
        act_ref[...] = act
        mod_ref[...] = _dot(act.astype(BF16), w_ref[...].astype(BF16)) + b_ref[...]

    return _call(body, name='mod_cols', grid=(1,),
                 in_specs=[_const(c_all.shape), _const(w_ada.shape), _const(b_cols.shape)],
                 out_specs=[_const((N_DEV, ADA_SHARD)), _const(c_all.shape)],
                 out_shape=[_sds((N_DEV, ADA_SHARD)), _sds(c_all.shape)], vmem=VMEM_BIG)(c_all, w_ada, b_cols)


def _grad_w_ada(act_t, dmod_cols):
    def body(a_ref, d_ref, o_ref):
        o_ref[...] = _dot(a_ref[...], d_ref[...])

    return _call(body, name='grad_w_ada', grid=(1,), in_specs=[_const(act_t.shape), _const(dmod_cols.shape)],
                 out_specs=_const((D_MODEL, ADA_SHARD)), out_shape=_sds((D_MODEL, ADA_SHARD)),
                 vmem=VMEM_BIG)(act_t, dmod_cols)


def _pre_mix(x, sc, sh, g, w_s, tm, ride):
    T = x.shape[0]

    def body(x_ref, sc_ref, sh_ref, g_ref, w_ref, proj_ref, h_ref):
        @pl.when(pl.program_id(1) == 0)
        def _():
            xv = x_ref[...]
            h_ref[...] = ((xv * _rsqrt_mean(xv) * g_ref[...]) * (1.0 + sc_ref[...]) + sh_ref[...]).astype(BF16)

        for s in range(2):
            proj_ref[:, s * IN_SHARD:(s + 1) * IN_SHARD] = _dot(h_ref[...], w_ref[s])

    row = pl.BlockSpec((tm, D_MODEL), lambda i, j: (i, 0))
    vec = _const((1, D_MODEL))
    return _call(body, name='pre_mix', grid=(T // tm, N_DEV // 2),
                 in_specs=[row, vec, vec, vec, pl.BlockSpec((2, D_MODEL, IN_SHARD), lambda i, j: (j, 0, 0))],
                 out_specs=[pl.BlockSpec((tm, 2 * IN_SHARD), lambda i, j: (i, j)), row],
                 out_shape=[_sds((T, D_IN_PROJ)), _sds((T, D_MODEL), BF16)],
                 sem=('parallel', 'arbitrary'), ride=ride)(x, sc, sh, g, w_s)


def _halo_before(tm, rows=HALO):
    return lambda i: jnp.maximum(i * (tm // rows) - 1, 0)


def _halo_after(tm, T, rows=HALO):
    return lambda i: jnp.minimum((i + 1) * (tm // rows), T // rows - 1)


def _mix_fwd(yssm, proj, d, glu_w, glu_b, g_ssm, cw, g_conv, avg16, avg64, tm):
    T = yssm.shape[0]
    hb = _halo_before(tm)

    def body(y_ref, p_ref, ph_ref, d_ref, gw_ref, gb_ref, gs_ref, cw_ref, gc_ref, a16_ref, a64_ref, o_ref):
        i = pl.program_id(0)
        u = p_ref[:, 0:D_SSM]
        y = y_ref[...] + d_ref[...] * u
        z, _ = _gelu(y)
        gate = _sigmoid(_dot(z.astype(BF16), gw_ref[...]) + gb_ref[...])
        ya = z * gate
        rs = lax.rsqrt(_dot_split(ya * ya, a16_ref[...], 2) + EPS)
        o_ref[:, 0:D_SSM] = (ya * rs * gs_ref[...]).astype(BF16)
        bg = p_ref[:, D_SSM:D_SSM + D_CONV]
        cv = p_ref[:, D_SSM + D_CONV:D_SSM + 2 * D_CONV] * p_ref[:, D_SSM + 2 * D_CONV:D_IN_PROJ]
        hv = ph_ref[:, D_SSM + D_CONV:D_SSM + 2 * D_CONV] * ph_ref[:, D_SSM + 2 * D_CONV:D_IN_PROJ]
        hv = jnp.where(i > 0, hv, 0.0)
        conv, _, _ = _conv3(cv, hv, cw_ref)
        yb = bg * conv
        rsb = lax.rsqrt(_dot_split(yb * yb, a64_ref[...], 2) + EPS)
        o_ref[:, D_SSM:D_MODEL] = (yb * rsb * gc_ref[...]).astype(BF16)

    vec = _const((1, D_SSM))
    sq = _const((D_SSM, D_SSM))
    return _call(body, name='mix_fwd', grid=(T // tm,),
                 in_specs=[pl.BlockSpec((tm, D_SSM), lambda i: (i, 0)), pl.BlockSpec((tm, D_IN_PROJ), lambda i: (i, 0)),
                           pl.BlockSpec((HALO, D_IN_PROJ), lambda i: (hb(i), 0)), vec, sq, vec, vec,
                           _const((3, D_CONV)), vec, sq, sq],
                 out_specs=pl.BlockSpec((tm, D_MODEL), lambda i: (i, 0)), out_shape=_sds((T, D_MODEL), BF16),
                 sem=('parallel',), vmem=VMEM_BIG)(yssm, proj, proj, d, glu_w, glu_b, g_ssm, cw, g_conv, avg16, avg64)


def _out_proj(ycat, w_out, x, gt, g_post, g_pre, sc, sh, tm):
    T = x.shape[0]

    def body(y_ref, w_ref, x_ref, gt_ref, gp_ref, g2_ref, sc_ref, sh_ref, o_ref, x1_ref, h_ref):
        o = _dot(y_ref[...], w_ref[...])
        o_ref[...] = o
        x1 = x_ref[...] + gt_ref[...] * (o * _rsqrt_mean(o) * gp_ref[...])
        x1_ref[...] = x1
        h_ref[...] = ((x1 * _rsqrt_mean(x1) * g2_ref[...]) * (1.0 + sc_ref[...]) + sh_ref[...]).astype(BF16)

    row = pl.BlockSpec((tm, D_MODEL), lambda i: (i, 0))
    vec = _const((1, D_MODEL))
    return _call(body, name='out_proj', grid=(T // tm,),
                 in_specs=[row, _const((D_MODEL, D_MODEL)), row, vec, vec, vec, vec, vec],
                 out_specs=[row, row, row],
                 out_shape=[_sds((T, D_MODEL)), _sds((T, D_MODEL)), _sds((T, D_MODEL), BF16)],
                 sem=('parallel',), vmem=VMEM_BIG)(ycat, w_out, x, gt, g_post, g_pre, sc, sh)


def _ffn_up(h2, w_s, cw8, tm):
    T = h2.shape[0]
    hb = _halo_before(tm, HALO16)

    def body(h_ref, hh_ref, w_ref, cw_ref, up_ref, hid_ref):
        up = _dot_nt(h_ref[...], w_ref[...])
        up_ref[...] = up.astype(BF16)
        before = jnp.where(pl.program_id(0) > 0, _dot_nt(hh_ref[...], w_ref[...]), 0.0)
        hid_ref[...] = _conv3(up, before, cw_ref)[0].astype(BF16)

    out = pl.BlockSpec((None, tm, FF_SHARD), lambda i, j: (j, i, 0))
    return _call(body, name='ffn_up', grid=(T // tm, N_DEV),
                 in_specs=[pl.BlockSpec((tm, D_MODEL), lambda i, j: (i, 0)),
                           pl.BlockSpec((HALO16, D_MODEL), lambda i, j: (hb(i), 0)),
                           pl.BlockSpec((None, FF_SHARD, D_MODEL), lambda i, j: (j, 0, 0)),
                           pl.BlockSpec((None, 3, FF_SHARD), lambda i, j: (j, 0, 0))],
                 out_specs=[out, out], out_shape=[_sds((N_DEV, T, FF_SHARD), BF16)] * 2,
                 sem=('parallel', 'parallel'))(h2, h2, w_s, cw8)


def _ffn_down(hid4, wd4, x1, tgt, gt, g_post, tm):
    T = x1.shape[0]
    nb = T // tm

    def body(a_ref, w_ref, x1_ref, t_ref, gt_ref, g_ref, ddn_ref, dx_ref, loss_ref, dgt_ref, dg_ref, dn_ref):
        i, j = pl.program_id(0), pl.program_id(1)
        part = None
        for s in range(2):
            act = (_silu_parts(a_ref[0, s].astype(F32))[0] * a_ref[1, s].astype(F32)).astype(BF16)
            term = _dot(act, w_ref[s])
            part = term if part is None else part + term

        @pl.when(jnp.logical_and(i == 0, j == 0))
        def _():
            dgt_ref[...] = jnp.zeros_like(dgt_ref)
            dg_ref[...] = jnp.zeros_like(dg_ref)

        @pl.when(j == 0)
        def _():
            dn_ref[...] = part

        @pl.when(j > 0)
        def _():
            dn_ref[...] += part

        @pl.when(j == 1)
        def _():
            dn, gv, gate = dn_ref[...], g_ref[...], gt_ref[...]
            r = _rsqrt_mean(dn)
            normed = dn * r * gv
            err = x1_ref[...] + gate * normed - t_ref[...]
            dx = err * (1.0 / D_MODEL)
            dx_ref[...] = dx
            tot = jnp.sum(jnp.sum(err * err, axis=1, keepdims=True), axis=0, keepdims=True) * (0.5 / D_MODEL)
            loss_ref[...] = jnp.broadcast_to(tot, (8, 128))
            dgt_ref[...] += _colsum(dx * normed)
            dnn = dx * gate
            dg_ref[...] += _colsum(dnn * dn * r)
            ddn_ref[...] = _norm_bwd(dnn, dn, r, gv).astype(BF16)

    row = pl.BlockSpec((tm, D_MODEL), lambda i, j: (i, 0))
    vec = _const((1, D_MODEL))
    return _call(body, name='ffn_down', grid=(nb, 2),
                 in_specs=[pl.BlockSpec((2, 2, tm, FF_SHARD), lambda i, j: (0, j, i, 0)),
                           pl.BlockSpec((2, FF_SHARD, D_MODEL), lambda i, j: (j, 0, 0)), row, row, vec, vec],
                 out_specs=[row, row, pl.BlockSpec((None, 8, 128), lambda i, j: (i, 0, 0)), vec, vec],
                 out_shape=[_sds((T, D_MODEL), BF16), _sds((T, D_MODEL)), _sds((nb, 8, 128)), _sds((1, D_MODEL)),
                            _sds((1, D_MODEL))],
                 scratch=[pltpu.VMEM((tm, D_MODEL), F32)], sem=('arbitrary', 'arbitrary'),
                 vmem=VMEM_BIG)(hid4, wd4, x1, tgt, gt, g_post)


def _ssm_prep(lre, lim, lst, b_re, b_im):
    def body(lre_ref, lim_ref, lst_ref, br_ref, bi_ref, ar_ref, ai_ref, bbr_ref, bbi_ref):
        ar, ai, qr, qi = _zoh(lre_ref[...], lim_ref[...], lst_ref[...])[:4]
        ar_ref[...] = ar
        ai_ref[...] = ai
        bbr_ref[...] = qr * br_ref[...] - qi * bi_ref[...]
        bbi_ref[...] = qr * bi_ref[...] + qi * br_ref[...]

    shp = lre.shape
    return _call(body, name='ssm_prep', grid=(1,), in_specs=[_const(shp)] * 5, out_specs=[_const(shp)] * 4,
                 out_shape=[_sds(shp)] * 4)(lre, lim, lst, b_re, b_im)


def _zoh(lre, lim, lst):
    lr = jnp.minimum(lre, LAMBDA_RE_MAX)
    st = jnp.exp(lst)
    mag = jnp.exp(lr * st)
    ar = mag * jnp.cos(lim * st)
    ai = mag * jnp.sin(lim * st)
    den = lr * lr + lim * lim
    qr = ((ar - 1.0) * lr + ai * lim) / den
    qi = (ai * lr - (ar - 1.0) * lim) / den
    return ar, ai, qr, qi, lr, st, den


def _ssm_prep_bwd(lre, lim, lst, b_re, b_im, dbbr, dbbi, dar, dai, seg):
    def body(lre_ref, lim_ref, lst_ref, br_ref, bi_ref, dbbr_ref, dbbi_ref, dar_ref, dai_ref, seg_ref,
             dbr_ref, dbi_ref, dlre_ref, dlim_ref, dlst_ref):
        lre_v = lre_ref[...]
        li = lim_ref[...]
        ar, ai, qr, qi, lr, st, den = _zoh(lre_v, li, lst_ref[...])
        br, bi, gbr, gbi = br_ref[...], bi_ref[...], dbbr_ref[...], dbbi_ref[...]
        dbr_ref[...] = qr * gbr + qi * gbi
        dbi_ref[...] = qr * gbi - qi * gbr
        gqr = _dot_split(br * gbr + bi * gbi, seg_ref[...], 3)
        gqi = _dot_split(br * gbi - bi * gbr, seg_ref[...], 3)
        ir, ii = lr / den, -li / den
        gar = dar_ref[...] + ir * gqr + ii * gqi
        gai = dai_ref[...] + ir * gqi - ii * gqr
        tr, ti = qr * ir - qi * ii, qr * ii + qi * ir
        glr = -(tr * gqr + ti * gqi)
        gli = -(tr * gqi - ti * gqr)
        gzr = ar * gar + ai * gai
        gzi = ar * gai - ai * gar
        glr = glr + st * gzr
        gli = gli + st * gzi
        gst = (lr * gzr + li * gzi) * st
        dlre_ref[...] = jnp.where(lre_v < LAMBDA_RE_MAX, glr, 0.0)
        dlim_ref[...] = gli
        dlst_ref[...] = jnp.sum(gst, axis=1, keepdims=True) * (1.0 / SSM_GROUP)

    shp = lre.shape
    return _call(body, name='ssm_prep_bwd', grid=(1,), in_specs=[_const(shp)] * 9 + [_const(seg.shape)],
                 out_specs=[_const(shp)] * 4 + [_const((N_GROUPS, 1))],
                 out_shape=[_sds(shp)] * 4 + [_sds((N_GROUPS, 1))], vmem=VMEM_BIG)(
                     lre, lim, lst, b_re, b_im, dbbr, dbbi, dar, dai, seg)


def _scan_specs(T):
    half = lambda cb: cb // 2
    return dict(
        chan=pl.BlockSpec((T, CHAN_BLOCK), lambda cb: (0, half(cb))),
        state=pl.BlockSpec((T, STATE_BLOCK), lambda cb: (0, cb)),
        b=pl.BlockSpec((CHAN_BLOCK, STATE_BLOCK), lambda cb: (half(cb), cb)),
        c=pl.BlockSpec((STATE_BLOCK, CHAN_BLOCK), lambda cb: (cb, half(cb))),
        lam=pl.BlockSpec((1, STATE_BLOCK), lambda cb: (0, cb)),
    )


def _complex_power(re, im, n):
    out = None
    while True:
        if n & 1:
            out = (re, im) if out is None else (out[0] * re - out[1] * im, out[0] * im + out[1] * re)
        n >>= 1
        if n == 0:
            return out
        re, im = re * re - im * im, 2.0 * re * im


def _rows8(i):
    if isinstance(i, int):
        return pl.ds(i * SUBLANES, SUBLANES)
    return pl.ds(pl.multiple_of(i * SUBLANES, SUBLANES), SUBLANES)


def _scan_loop(n_steps, body, init):
    trips = n_steps // SCAN_UNROLL

    def trip(t, carry):
        for u in range(SCAN_UNROLL):
            carry = body(t * SCAN_UNROLL + u, carry)
        return carry

    carry = lax.fori_loop(0, trips, trip, init)
    for step in range(trips * SCAN_UNROLL, n_steps):
        carry = body(step, carry)
    return carry


def _ssm_fwd(u_perm, b_re, b_im, c_re, c_im, lam_r, lam_i, ride):
    T = u_perm.shape[0]
    ls = T // SUBLANES
    rc = min(512, T)
    sp = _scan_specs(T)

    def body(u_ref, bre_ref, bim_ref, cre_ref, cim_ref, lr_ref, li_ref, sre_ref, sim_ref, y_ref):
        cb = pl.program_id(0)
        for c in range(T // rc):
            rows = pl.ds(c * rc, rc)
            ub = u_ref[rows, :].astype(BF16)
            sre_ref[rows, :] = _dot(ub, bre_ref[...])
            sim_ref[rows, :] = _dot(ub, bim_ref[...])
        shp = (SUBLANES, STATE_BLOCK)
        lr = jnp.broadcast_to(lr_ref[...], shp)
        li = jnp.broadcast_to(li_ref[...], shp)
        zero = jnp.zeros(shp, F32)

        def step(i, carry):
            sr, si = carry
            rows = _rows8(i)
            nr = lr * sr - li * si + sre_ref[rows, :]
            ni = lr * si + li * sr + sim_ref[rows, :]
            sre_ref[rows, :] = nr
            sim_ref[rows, :] = ni
            return nr, ni

        fr, fi = _scan_loop(ls, step, (zero, zero))
        pr, pi_ = _complex_power(lr, li, ls)
        row = lax.broadcasted_iota(jnp.int32, shp, 0)
        ir, ii = zero, zero
        for _ in range(SUBLANES - 1):
            er = fr + pr * ir - pi_ * ii
            ei = fi + pr * ii + pi_ * ir
            ir = jnp.where(row == 0, 0.0, pltpu.roll(er, 1, 0))
            ii = jnp.where(row == 0, 0.0, pltpu.roll(ei, 1, 0))

        def fix(i, carry):
            cr, ci = carry
            rows = _rows8(i)
            nr = lr * cr - li * ci
            ni = lr * ci + li * cr
            sre_ref[rows, :] += nr
            sim_ref[rows, :] += ni
            return nr, ni

        _scan_loop(ls, fix, (ir, ii))
        for c in range(T // rc):
            rows = pl.ds(c * rc, rc)
            yc = _dot(sre_ref[rows, :].astype(BF16), cre_ref[...]) - _dot(sim_ref[rows, :].astype(BF16), cim_ref[...])

            @pl.when(cb % 2 == 0)
            def _():
                y_ref[rows, :] = yc

            @pl.when(cb % 2 == 1)
            def _():
                y_ref[rows, :] += yc

    return _call(body, name='ssm_fwd', grid=(N_STATE // STATE_BLOCK,),
                 in_specs=[sp['chan'], sp['b'], sp['b'], sp['c'], sp['c'], sp['lam'], sp['lam']],
                 out_specs=[sp['state'], sp['state'], sp['chan']],
                 out_shape=[_sds((T, N_STATE)), _sds((T, N_STATE)), _sds((T, D_SSM))],
                 sem=('arbitrary',), vmem=VMEM_BIG, ride=ride)(u_perm, b_re, b_im, c_re, c_im, lam_r, lam_i)


def _ssm_bwd(dy_perm, u_perm, s_re, s_im, b_re, b_im, c_re, c_im, lam_r, lam_i, ride):
    T = u_perm.shape[0]
    ls = T // SUBLANES
    rc = min(512, T)
    sp = _scan_specs(T)
    ncb = N_STATE // STATE_BLOCK

    def body(dy_ref, u_ref, sre_ref, sim_ref, bre_ref, bim_ref, cre_ref, cim_ref, lr_ref, li_ref,
             du_ref, dbr_ref, dbi_ref, dcr_ref, dci_ref, dar_ref, dai_ref, gre_ref, gim_ref):
        cb = pl.program_id(0)
        for c in range(T // rc):
            rows = pl.ds(c * rc, rc)
            dyb = dy_ref[rows, :].astype(BF16)
            gre_ref[rows, :] = _dot_nt(dyb, cre_ref[...])
            gim_ref[rows, :] = -_dot_nt(dyb, cim_ref[...])
        shp = (SUBLANES, STATE_BLOCK)
        lr = jnp.broadcast_to(lr_ref[...], shp)
        li = jnp.broadcast_to(li_ref[...], shp)
        zero = jnp.zeros(shp, F32)

        def step(k, carry):
            gr, gi = carry
            rows = _rows8(ls - 1 - k)
            nr = lr * gr + li * gi + gre_ref[rows, :]
            ni = lr * gi - li * gr + gim_ref[rows, :]
            gre_ref[rows, :] = nr
            gim_ref[rows, :] = ni
            return nr, ni

        fr, fi = _scan_loop(ls, step, (zero, zero))
        pr, pi_ = _complex_power(lr, -li, ls)
        row = lax.broadcasted_iota(jnp.int32, shp, 0)
        cr, ci = zero, zero
        for _ in range(SUBLANES - 1):
            er = fr + pr * cr - pi_ * ci
            ei = fi + pr * ci + pi_ * cr
            cr = jnp.where(row == SUBLANES - 1, 0.0, pltpu.roll(er, SUBLANES - 1, 0))
            ci = jnp.where(row == SUBLANES - 1, 0.0, pltpu.roll(ei, SUBLANES - 1, 0))

        def fix(k, carry):
            dr, di, ar, ai = carry
            rows = _rows8(ls - 1 - k)
            dr, di = lr * dr + li * di, lr * di - li * dr
            gr = gre_ref[rows, :] + dr
            gi = gim_ref[rows, :] + di
            gre_ref[rows, :] = gr
            gim_ref[rows, :] = gi
            prev = _rows8(ls - 2 - k)
            spr, spi = sre_ref[prev, :], sim_ref[prev, :]
            return dr, di, ar + gr * spr + gi * spi, ai + gi * spr - gr * spi

        dr, di, ar, ai = _scan_loop(ls - 1, fix, (cr, ci, zero, zero))
        first = pl.ds(0, SUBLANES)
        last = pl.ds((ls - 1) * SUBLANES, SUBLANES)
        gr = gre_ref[first, :] + (lr * dr + li * di)
        gi = gim_ref[first, :] + (lr * di - li * dr)
        gre_ref[first, :] = gr
        gim_ref[first, :] = gi
        spr = jnp.where(row == 0, 0.0, pltpu.roll(sre_ref[last, :], 1, 0))
        spi = jnp.where(row == 0, 0.0, pltpu.roll(sim_ref[last, :], 1, 0))
        dar_ref[...] = _colsum(ar + gr * spr + gi * spi)
        dai_ref[...] = _colsum(ai + gi * spr - gr * spi)

        for c in range(T // rc):
            rows = pl.ds(c * rc, rc)
            g_r, g_i = gre_ref[rows, :].astype(BF16), gim_ref[rows, :].astype(BF16)
            s_r, s_i = sre_ref[rows, :].astype(BF16), sim_ref[rows, :].astype(BF16)
            ub, dyb = u_ref[rows, :].astype(BF16), dy_ref[rows, :].astype(BF16)
            duc = _dot_nt(g_r, bre_ref[...]) + _dot_nt(g_i, bim_ref[...])
            parts = (_dot_tn(ub, g_r), _dot_tn(ub, g_i), _dot_tn(s_r, dyb), -_dot_tn(s_i, dyb))
            outs = (dbr_ref, dbi_ref, dcr_ref, dci_ref)
            for o_ref, part in zip(outs, parts):
                if c == 0:
                    o_ref[...] = part
                else:
                    o_ref[...] += part

            @pl.when(cb % 2 == 0)
            def _():
                du_ref[rows, :] = duc

            @pl.when(cb % 2 == 1)
            def _():
                du_ref[rows, :] += duc

    blk = lambda r, c: pl.BlockSpec((None, r, c), lambda cb: (cb, 0, 0))
    return _call(body, name='ssm_bwd', grid=(ncb,),
                 in_specs=[sp['chan'], sp['chan'], sp['state'], sp['state'], sp['b'], sp['b'], sp['c'], sp['c'],
                           sp['lam'], sp['lam']],
                 out_specs=[sp['chan'], blk(CHAN_BLOCK, STATE_BLOCK), blk(CHAN_BLOCK, STATE_BLOCK),
                            blk(STATE_BLOCK, CHAN_BLOCK), blk(STATE_BLOCK, CHAN_BLOCK), blk(1, STATE_BLOCK),
                            blk(1, STATE_BLOCK)],
                 out_shape=[_sds((T, D_SSM)), _sds((ncb, CHAN_BLOCK, STATE_BLOCK)), _sds((ncb, CHAN_BLOCK, STATE_BLOCK)),
                            _sds((ncb, STATE_BLOCK, CHAN_BLOCK)), _sds((ncb, STATE_BLOCK, CHAN_BLOCK)),
                            _sds((ncb, 1, STATE_BLOCK)), _sds((ncb, 1, STATE_BLOCK))],
                 scratch=[pltpu.VMEM((T, STATE_BLOCK), F32), pltpu.VMEM((T, STATE_BLOCK), F32)],
                 sem=('arbitrary',), vmem=VMEM_BIG, ride=ride)(dy_perm, u_perm, s_re, s_im, b_re, b_im, c_re, c_im,
                                                               lam_r, lam_i)


def _ffn_dact(ddn, wd4, hid4, tm):
    T = ddn.shape[0]

    def body(d_ref, w_ref, hid_ref, o_ref, act_ref):
        dact = _dot_nt(d_ref[...], w_ref[...])
        silu, dsilu = _silu_parts(hid_ref[0].astype(F32))
        hid_v = hid_ref[1].astype(F32)
        o_ref[0] = (dact * hid_v * dsilu).astype(BF16)
        o_ref[1] = (dact * silu).astype(BF16)
        act_ref[...] = (silu * hid_v).astype(BF16)

    blk = pl.BlockSpec((2, None, tm, FF_SHARD), lambda i, j: (0, j, i, 0))
    return _call(body, name='ffn_dact', grid=(T // tm, 4),
                 in_specs=[pl.BlockSpec((tm, D_MODEL), lambda i, j: (i, 0)),
                           pl.BlockSpec((None, FF_SHARD, D_MODEL), lambda i, j: (j, 0, 0)), blk],
                 out_specs=[blk, pl.BlockSpec((None, tm, FF_SHARD), lambda i, j: (j, i, 0))],
                 out_shape=[_sds((2, 4, T, FF_SHARD), BF16), _sds((4, T, FF_SHARD), BF16)],
                 sem=('parallel', 'parallel'))(ddn, wd4, hid4)


def _ffn_dup(dhid8, up8, cw8, tm, ride):
    T = up8.shape[1]
    nb = T // tm
    ha = _halo_after(tm, T, HALO16)

    def body(dh_ref, dha_ref, up_ref, cw_ref, dup_ref, dcw_ref):
        i = pl.program_id(1)

        @pl.when(i == 0)
        def _():
            dcw_ref[...] = jnp.zeros_like(dcw_ref)

        dh = dh_ref[...].astype(F32)
        dup, dh1, dh2 = _conv3_t(dh, jnp.where(i < nb - 1, dha_ref[...].astype(F32), 0.0), cw_ref)
        dup_ref[...] = dup.astype(BF16)
        up = up_ref[...].astype(F32)
        dcw_ref[0:1, :] += _colsum(dh2 * up)
        dcw_ref[1:2, :] += _colsum(dh1 * up)
        dcw_ref[2:3, :] += _colsum(dh * up)

    main = pl.BlockSpec((None, tm, FF_SHARD), lambda j, i: (j, i, 0))
    return _call(body, name='ffn_dup', grid=(N_DEV, nb),
                 in_specs=[main, pl.BlockSpec((None, HALO16, FF_SHARD), lambda j, i: (j, ha(i), 0)), main,
                           pl.BlockSpec((None, 3, FF_SHARD), lambda j, i: (j, 0, 0))],
                 out_specs=[main, pl.BlockSpec((None, 8, FF_SHARD), lambda j, i: (j, 0, 0))],
                 out_shape=[_sds((N_DEV, T, FF_SHARD), BF16), _sds((N_DEV, 8, FF_SHARD))],
                 sem=('parallel', 'arbitrary'), ride=ride)(dhid8, dhid8, up8, cw8)


def _grad_tn(a, b, a_spec, b_spec, groups, m, n, tk, name, ride=None, parts=1):
    T = a.shape[-2]
    nk = T // tk
    mp = m // parts

    def body(a_ref, b_ref, *refs):
        o_refs, acc_ref = refs[:parts], refs[parts]
        k = pl.program_id(1)
        part = _dot_tn(a_ref[...], b_ref[...])

        @pl.when(k == 0)
        def _():
            acc_ref[...] = part

        @pl.when(k > 0)
        def _():
            acc_ref[...] += part

        @pl.when(k == nk - 1)
        def _():
            for p, o_ref in enumerate(o_refs):
                o_ref[...] = acc_ref[p * mp:(p + 1) * mp, :].astype(BF16)

    out_spec = pl.BlockSpec((None, mp, n), lambda g, k: (g, 0, 0))
    res = _call(body, name=name, grid=(groups, nk), in_specs=[a_spec, b_spec], out_specs=[out_spec] * parts,
                out_shape=[_sds((groups, mp, n), BF16)] * parts, scratch=[pltpu.VMEM((m, n), F32)],
                sem=('parallel', 'arbitrary'), vmem=VMEM_BIG, ride=ride)(a, b)
    if parts > 1:
        return res
    return res[0] if ride is None else (res[0][0], res[1])


def _grad_w_in(h1, dproj, tk, ride):
    T = h1.shape[0]
    nk = T // tk
    half = D_IN_PROJ // 2

    def body(a_ref, b_ref, o_ref, acc_ref):
        k = pl.program_id(0)
        for h in range(2):
            cols = slice(h * half, (h + 1) * half)
            part = _dot_tn(a_ref[...], b_ref[:, cols])

            @pl.when(k == 0)
            def _():
                acc_ref[:, cols] = part

            @pl.when(k > 0)
            def _():
                acc_ref[:, cols] += part

        @pl.when(k == nk - 1)
        def _():
            for g in range(N_DEV):
                o_ref[g] = acc_ref[:, g * IN_SHARD:(g + 1) * IN_SHARD].astype(BF16)

    return _call(body, name='grad_w_in', grid=(nk,),
                 in_specs=[pl.BlockSpec((tk, D_MODEL), lambda k: (k, 0)), pl.BlockSpec((tk, D_IN_PROJ), lambda k: (k, 0))],
                 out_specs=_const((N_DEV, D_MODEL, IN_SHARD)), out_shape=_sds((N_DEV, D_MODEL, IN_SHARD), BF16),
                 scratch=[pltpu.VMEM((D_MODEL, D_IN_PROJ), F32)], sem=('arbitrary',), vmem=VMEM_BIG, ride=ride)(h1, dproj)


def _pre_norm_bwd(dz, dz_spec, w_s, xin, dres, sc, g, tm, name, ride, below=None, group=1, w_t=False):
    T = xin.shape[0]
    n = w_s.shape[1] if w_t else w_s.shape[2]
    mul = _dot if w_t else _dot_nt
    steps = N_DEV // group

    def body(dz_ref, w_ref, x_ref, dr_ref, sc_ref, g_ref, *refs):
        if below is None:
            dx_ref, dsh_ref, dsc_ref, dg_ref = refs
            sums = (dsh_ref, dsc_ref, dg_ref)
        else:
            v_ref, gate_ref, g2_ref, dx_ref, dsh_ref, dsc_ref, dg_ref, dv_ref, dgate_ref, dg2_ref = refs
            sums = (dsh_ref, dsc_ref, dg_ref, dgate_ref, dg2_ref)
        i, j = pl.program_id(0), pl.program_id(1)
        piece = (lambda s: dz_ref[s]) if dz.ndim == 3 else (lambda s: dz_ref[:, s * n:(s + 1) * n])
        part = mul(piece(0), w_ref[0])
        for s in range(1, group):
            part = part + mul(piece(s), w_ref[s])

        @pl.when(jnp.logical_and(i == 0, j == 0))
        def _():
            for s_ref in sums:
                s_ref[...] = jnp.zeros_like(s_ref)

        @pl.when(j == 0)
        def _():
            dx_ref[...] = part

        @pl.when(j > 0)
        def _():
            dx_ref[...] += part

        @pl.when(j == steps - 1)
        def _():
            dh, xv, gv = dx_ref[...], x_ref[...], g_ref[...]
            r = _rsqrt_mean(xv)
            dsh_ref[...] += _colsum(dh)
            dsc_ref[...] += _colsum(dh * (xv * r * gv))
            dxn = dh * (1.0 + sc_ref[...])
            dg_ref[...] += _colsum(dxn * xv * r)
            dx = dr_ref[...] + _norm_bwd(dxn, xv, r, gv)
            dx_ref[...] = dx
            if below is not None:
                v, g2 = v_ref[...], g2_ref[...]
                rv = _rsqrt_mean(v)
                dgate_ref[...] += _colsum(dx * (v * rv * g2))
                dn = dx * gate_ref[...]
                dg2_ref[...] += _colsum(dn * v * rv)
                dv_ref[...] = _norm_bwd(dn, v, rv, g2).astype(BF16)

    row = pl.BlockSpec((tm, D_MODEL), lambda i, j: (i, 0))
    vec = _const((1, D_MODEL))
    in_specs = [dz_spec, pl.BlockSpec((group,) + w_s.shape[1:], lambda i, j: (j, 0, 0)), row, row, vec, vec]
    out_specs = [row, vec, vec, vec]
    out_shape = [_sds((T, D_MODEL)), _sds((1, D_MODEL)), _sds((1, D_MODEL)), _sds((1, D_MODEL))]
    args = [dz, w_s, xin, dres, sc, g]
    if below is not None:
        in_specs += [row, vec, vec]
        out_specs += [row, vec, vec]
        out_shape += [_sds((T, D_MODEL), BF16), _sds((1, D_MODEL)), _sds((1, D_MODEL))]
        args += list(below)
    return _call(body, name=name, grid=(T // tm, steps), in_specs=in_specs, out_specs=out_specs,
                 out_shape=out_shape, sem=('arbitrary', 'arbitrary'), vmem=VMEM_MOST, ride=ride)(*args)


def _mix_bwd(d_o, w_out, yssm, proj, d, glu_w, glu_b, g_ssm, cw, g_conv, avg16, avg64, tm):
    T = yssm.shape[0]
    hb = _halo_before(tm)

    def body(do_ref, wo_ref, y_ref, p_ref, ph_ref, d_ref, gw_ref, gb_ref, gs_ref, cw_ref, gc_ref, a16_ref, a64_ref,
             dy_ref, dconv_ref, dbg_ref, z_ref, dlin_ref, acc_ref):
        i = pl.program_id(0)
        dyc = _dot_nt(do_ref[...], wo_ref[...])

        @pl.when(i == 0)
        def _():
            acc_ref[...] = jnp.zeros_like(acc_ref)

        u = p_ref[:, 0:D_SSM]
        y = y_ref[...] + d_ref[...] * u
        z, t = _gelu(y)
        gate = _sigmoid(_dot(z.astype(BF16), gw_ref[...]) + gb_ref[...])
        ya = z * gate
        rs = lax.rsqrt(_dot_split(ya * ya, a16_ref[...], 2) + EPS)
        dna = dyc[:, 0:D_SSM]
        acc_ref[1:2, :] += _colsum(dna * ya * rs)
        dya = _head_norm_bwd(dna, ya, rs, gs_ref[...], a16_ref[...])
        dlin = dya * z * gate * (1.0 - gate)
        acc_ref[0:1, :] += _colsum(dlin)
        dlin_b = dlin.astype(BF16)
        dz = dya * gate + _dot_nt(dlin_b, gw_ref[...])
        dy = dz * _gelu_grad(y, t)
        acc_ref[3:4, :] += _colsum(dy * u)
        dy_ref[...] = dy
        z_ref[...] = z.astype(BF16)
        dlin_ref[...] = dlin_b

        bg = p_ref[:, D_SSM:D_SSM + D_CONV]
        cv = p_ref[:, D_SSM + D_CONV:D_SSM + 2 * D_CONV] * p_ref[:, D_SSM + 2 * D_CONV:D_IN_PROJ]
        hv = ph_ref[:, D_SSM + D_CONV:D_SSM + 2 * D_CONV] * ph_ref[:, D_SSM + 2 * D_CONV:D_IN_PROJ]
        hv = jnp.where(i > 0, hv, 0.0)
        conv, cv1, cv2 = _conv3(cv, hv, cw_ref)
        yb = bg * conv
        rsb = lax.rsqrt(_dot_split(yb * yb, a64_ref[...], 2) + EPS)
        dnb = dyc[:, D_SSM:D_MODEL]
        acc_ref[2:3, :] += _colsum(dnb * yb * rsb)
        dyb = _head_norm_bwd(dnb, yb, rsb, gc_ref[...], a64_ref[...])
        dbg_ref[...] = dyb * conv
        dconv = dyb * bg
        dconv_ref[...] = dconv
        acc_ref[4:5, :] += _colsum(dconv * cv2)
        acc_ref[5:6, :] += _colsum(dconv * cv1)
        acc_ref[6:7, :] += _colsum(dconv * cv)

    vec = _const((1, D_SSM))
    sq = _const((D_SSM, D_SSM))
    half = pl.BlockSpec((tm, D_SSM), lambda i: (i, 0))
    return _call(body, name='mix_bwd', grid=(T // tm,),
                 in_specs=[pl.BlockSpec((tm, D_MODEL), lambda i: (i, 0)), _const((D_MODEL, D_MODEL)), half,
                           pl.BlockSpec((tm, D_IN_PROJ), lambda i: (i, 0)),
                           pl.BlockSpec((HALO, D_IN_PROJ), lambda i: (hb(i), 0)), vec, sq, vec, vec,
                           _const((3, D_CONV)), vec, sq, sq],
                 out_specs=[half, half, half, half, half, _const((8, D_SSM))],
                 out_shape=[_sds((T, D_SSM)), _sds((T, D_SSM)), _sds((T, D_SSM)), _sds((T, D_SSM), BF16),
                            _sds((T, D_SSM), BF16), _sds((8, D_SSM))],
                 sem=('arbitrary',), vmem=VMEM_BIG)(d_o, w_out, yssm, proj, proj, d, glu_w, glu_b, g_ssm, cw, g_conv,
                                                   avg16, avg64)


def _mix_bwd_proj(dconv, proj, du_ssm, dy, d, dbg, cw, tm):
    T = dy.shape[0]
    nb = T // tm
    ha = _halo_after(tm, T)

    def body(dc_ref, dch_ref, cg_ref, v_ref, du_ref, dy_ref, d_ref, dbg_ref, cw_ref, o_ref):
        i = pl.program_id(0)
        dcv = _conv3_t(dc_ref[...], jnp.where(i < nb - 1, dch_ref[...], 0.0), cw_ref)[0]
        o_ref[:, 0:D_SSM] = (du_ref[...] + dy_ref[...] * d_ref[...]).astype(BF16)
        o_ref[:, D_SSM:D_SSM + D_CONV] = dbg_ref[...].astype(BF16)
        o_ref[:, D_SSM + D_CONV:D_SSM + 2 * D_CONV] = (dcv * v_ref[...]).astype(BF16)
        o_ref[:, D_SSM + 2 * D_CONV:D_IN_PROJ] = (dcv * cg_ref[...]).astype(BF16)

    half = pl.BlockSpec((tm, D_SSM), lambda i: (i, 0))
    return _call(body, name='mix_bwd_proj', grid=(nb,),
                 in_specs=[half, pl.BlockSpec((HALO, D_CONV), lambda i: (ha(i), 0)),
                           pl.BlockSpec((tm, D_CONV), lambda i: (i, 2)), pl.BlockSpec((tm, D_CONV), lambda i: (i, 3)),
                           half, half, _const((1, D_SSM)), half, _const((3, D_CONV))],
                 out_specs=pl.BlockSpec((tm, D_IN_PROJ), lambda i: (i, 0)), out_shape=_sds((T, D_IN_PROJ), BF16),
                 sem=('parallel',))(dconv, dconv, proj, proj, du_ssm, dy, d, dbg, cw)


def _row_tile(rows, cols, slots):
    for cand in (512, 256, 128, 64, 32, 16, 8):
        if rows % cand == 0 and slots * cand * cols * 4 <= (2 << 20):
            return cand
    return rows


def _adamw_math(g, w, m, v):
    m2 = ADAM_B1 * m + (1.0 - ADAM_B1) * g
    v2 = ADAM_B2 * v + (1.0 - ADAM_B2) * (g * g)
    m_hat = m2 / (1.0 - ADAM_B1 ** ADAM_STEP)
    v_hat = v2 / (1.0 - ADAM_B2 ** ADAM_STEP)
    return -ADAM_LR * (m_hat / (jnp.sqrt(v_hat) + ADAM_EPS) + ADAM_WD * w), m2, v2


def _adamw(pieces, w, m, v, name):
    slots, _, cols = pieces[0].shape
    rows = sum(p.shape[1] for p in pieces)
    tr = _row_tile(pieces[0].shape[1], cols, slots)
    starts, pos = [], 0
    for p in pieces:
        assert p.shape[1] % tr == 0
        starts.append(pos)
        pos += p.shape[1] // tr

    def body(*refs):
        g_refs = refs[:len(pieces)]
        w_ref, m_ref, v_ref, go_ref, d_ref, mo_ref, vo_ref = refs[len(pieces):]
        i = pl.program_id(0)
        g = None
        for g_ref, start in zip(g_refs, starts):
            part = g_ref[0].astype(F32)
            for s in range(1, slots):
                part = part + g_ref[s].astype(F32)
            g = part if g is None else jnp.where(i >= start, part, g)
        go_ref[...] = g
        d_ref[...], mo_ref[...], vo_ref[...] = _adamw_math(g, w_ref[...], m_ref[...], v_ref[...])

    def piece_spec(start, count):
        return pl.BlockSpec((slots, tr, cols), lambda i: (0, jnp.clip(i - start, 0, count - 1), 0))

    blk = pl.BlockSpec((tr, cols), lambda i: (i, 0))
    return _call(body, name=name, grid=(rows // tr,),
                 in_specs=[piece_spec(s, p.shape[1] // tr) for s, p in zip(starts, pieces)] + [blk, blk, blk],
                 out_specs=[blk] * 4, out_shape=[_sds((rows, cols))] * 4, sem=('parallel',))(*pieces, w, m, v)


def _to_scan_rows(a):
    T, n = a.shape
    return a.reshape(SUBLANES, T // SUBLANES, n).transpose(1, 0, 2).reshape(T, n)


def _from_scan_rows(a):
    T, n = a.shape
    return a.reshape(T // SUBLANES, SUBLANES, n).transpose(1, 0, 2).reshape(T, n)


def _expand(a):
    return jnp.repeat(a, SSM_GROUP, axis=1)


def _block_diag(rows, row_group, col_group):
    r, n = rows.shape
    tiled = jnp.tile(rows, (1, N_GROUPS))
    keep = (jnp.arange(r)[:, None] // row_group) == (jnp.arange(n * N_GROUPS)[None, :] // col_group)
    return jnp.where(keep, tiled, 0.0)


def _block_diag_b(bb):
    return _block_diag(bb.transpose(0, 2, 1).reshape(D_SSM, SSM_STATE), SSM_GROUP, SSM_STATE)


def _block_diag_c(cc):
    return _block_diag(cc.transpose(0, 2, 1).reshape(N_STATE, SSM_GROUP), SSM_STATE, SSM_GROUP)


def _diag_blocks(x, chan_major):
    e2 = jnp.eye(2, dtype=x.dtype)
    e4 = jnp.eye(4, dtype=x.dtype)
    if chan_major:
        x = x.reshape(4, 2, 2, 4, SSM_GROUP, 4, SSM_STATE)
        x = x * e2[None, :, :, None, None, None, None] * e4[None, None, None, :, None, :, None]
        return x.sum(axis=(2, 3)).transpose(0, 1, 3, 4, 2).reshape(N_GROUPS, SSM_STATE, SSM_GROUP)
    x = x.reshape(4, 2, 4, SSM_STATE, 2, 4, SSM_GROUP)
    x = x * e2[None, :, None, None, :, None, None] * e4[None, None, :, None, None, :, None]
    return x.sum(axis=(4, 5)).reshape(N_GROUPS, SSM_STATE, SSM_GROUP)


SMALL_LAYOUT = {
    'ssm_b_re': (0, 0, 32, 1024), 'ssm_b_im': (32, 0, 32, 1024), 'ssm_c_re': (64, 0, 32, 1024),
    'ssm_c_im': (96, 0, 32, 1024), 'b_ada': (128, 0, 6, 1024), 'g_pre_mix': (134, 0, 1, 1024),
    'g_post_mix': (135, 0, 1, 1024), 'ssm_lam_re': (136, 0, 2, 1024), 'ssm_lam_im': (138, 0, 2, 1024),
    'ssm_log_step': (140, 0, 1, 32), 'glu_b': (141, 0, 1, 512), 'g_out_ssm': (141, 512, 1, 512),
    'g_out_conv': (142, 0, 1, 512), 'ssm_d': (142, 512, 1, 512), 'g_pre_ffn': (143, 0, 1, 1024),
    'g_post_ffn': (144, 0, 1, 1024)}
SMALL_ROWS = 152
B_ADA_ROW = SMALL_LAYOUT['b_ada'][0]
LATE_ROWS = {('b_ada', 0): 0, ('b_ada', 1): 1, ('g_pre_mix', 0): 2}


def _adamw_small(gathered, late, wts, mom_m, mom_v):
    names = list(SMALL_LAYOUT)
    n = len(names)

    def body(*refs):
        g_ref, late_ref, ins, outs = refs[0], refs[1], refs[2:2 + 3 * n], refs[2 + 3 * n:]
        for p, name in enumerate(names):
            r0, c0, rows, cols = SMALL_LAYOUT[name]
            pieces = [(0, rows)] if rows % 8 == 0 else [(r, 1) for r in range(rows)]
            for r, cnt in pieces:
                src_ref, first = (late_ref, LATE_ROWS[name, r]) if (name, r) in LATE_ROWS else (g_ref, r0 + r)
                g = src_ref[0, first:first + cnt, c0:c0 + cols]
                for s in range(1, N_DEV):
                    g = g + src_ref[s, first:first + cnt, c0:c0 + cols]
                w, m, v = (ins[3 * p + q][r:r + cnt, :] for q in range(3))
                res = (g,) + _adamw_math(g, w, m, v)
                for q in range(4):
                    outs[4 * p + q][r:r + cnt, :] = res[q]

    shapes = [SMALL_LAYOUT[name][2:] for name in names]
    args = [gathered, late]
    for name, shp in zip(names, shapes):
        args += [wts[name].reshape(shp), mom_m[name].reshape(shp), mom_v[name].reshape(shp)]
    outs = _call(body, name='adamw_small', grid=(1,),
                 in_specs=[_const(gathered.shape), _const(late.shape)]
                 + [_const(shp) for shp in shapes for _ in range(3)],
                 out_specs=[_const(shp) for shp in shapes for _ in range(4)],
                 out_shape=[_sds(shp) for shp in shapes for _ in range(4)], vmem=VMEM_BIG)(*args)
    res = {}
    for p, name in enumerate(names):
        for q, kind in enumerate(('g', 'd', 'm', 'v')):
            res[kind, name] = outs[4 * p + q].reshape(wts[name].shape)
    return res


def kernel(x, c, w_ada, b_ada, g_pre_mix, g_post_mix, w_in, ssm_lam_re, ssm_lam_im, ssm_log_step, ssm_b_re, ssm_b_im, ssm_c_re, ssm_c_im, ssm_d, glu_w, glu_b, g_out_ssm, conv_w, g_out_conv, w_out, g_pre_ffn, g_post_ffn, w_up, ffn_conv_w, w_down, loss_target, m_w_ada, m_b_ada, m_g_pre_mix, m_g_post_mix, m_w_in, m_ssm_lam_re, m_ssm_lam_im, m_ssm_log_step, m_ssm_b_re, m_ssm_b_im, m_ssm_c_re, m_ssm_c_im, m_ssm_d, m_glu_w, m_glu_b, m_g_out_ssm, m_conv_w, m_g_out_conv, m_w_out, m_g_pre_ffn, m_g_post_ffn, m_w_up, m_ffn_conv_w, m_w_down, v_w_ada, v_b_ada, v_g_pre_mix, v_g_post_mix, v_w_in, v_ssm_lam_re, v_ssm_lam_im, v_ssm_log_step, v_ssm_b_re, v_ssm_b_im, v_ssm_c_re, v_ssm_c_im, v_ssm_d, v_glu_w, v_glu_b, v_g_out_ssm, v_conv_w, v_g_out_conv, v_w_out, v_g_pre_ffn, v_g_post_ffn, v_w_up, v_ffn_conv_w, v_w_down):
    args = dict(locals())
    wts = {n: args[n] for n in WEIGHTS}
    mom_m = {n: args['m_' + n] for n in WEIGHTS}
    mom_v = {n: args['v_' + n] for n in WEIGHTS}
    T = x.shape[1]
    tm = min(512, T)
    tw = min(1024, T)
    me = _me()[3]
    xt, tgt = x[0], loss_target[0]

    c_all, w_in_s, glu_s, w_out_s, conv_s = _exchange(
        [c, w_in[0].astype(BF16), glu_w[0].astype(BF16), w_out[0].astype(BF16), conv_w[0]], name='gather_first',
        scatter=False)
    c_all = c_all.reshape(N_DEV, D_MODEL)
    b_cols = lax.dynamic_slice(b_ada, (0, me * ADA_SHARD), (1, ADA_SHARD))
    mod_cols, c_act = _mod_cols(c_all, w_ada[0], b_cols)
    (mod_all,) = _exchange([mod_cols], name='gather_mod', scatter=False)
    mod = lax.dynamic_slice(mod_all, (0, me, 0), (N_DEV, 1, ADA_SHARD)).reshape(N_MOD, 1, D_MODEL)
    sh1, sc1, gt1, sh2, sc2, gt2 = [mod[k] for k in range(N_MOD)]

    glu_full = glu_s.reshape(D_SSM, D_SSM)
    w_out_full = w_out_s.reshape(D_MODEL, D_MODEL)
    cw_full = conv_s.transpose(1, 0, 2).reshape(3, D_CONV)

    lre_x, lim_x = _expand(ssm_lam_re[0]), _expand(ssm_lam_im[0])
    lst_x = jnp.broadcast_to(ssm_log_step[0][:, None], (N_GROUPS, SSM_STATE * SSM_GROUP))
    b_re_x = ssm_b_re[0].reshape(N_GROUPS, -1)
    b_im_x = ssm_b_im[0].reshape(N_GROUPS, -1)
    ar_x, ai_x, bbr_x, bbi_x = _ssm_prep(lre_x, lim_x, lst_x, b_re_x, b_im_x)
    lam_r = ar_x[:, ::SSM_GROUP].reshape(1, N_STATE)
    lam_i = ai_x[:, ::SSM_GROUP].reshape(1, N_STATE)
    big_b_re = _block_diag_b(bbr_x.reshape(N_GROUPS, SSM_STATE, SSM_GROUP)).astype(BF16)
    big_b_im = _block_diag_b(bbi_x.reshape(N_GROUPS, SSM_STATE, SSM_GROUP)).astype(BF16)
    big_c_re = _block_diag_c(ssm_c_re[0]).astype(BF16)
    big_c_im = _block_diag_c(ssm_c_im[0]).astype(BF16)
    head = jnp.arange(D_SSM)
    avg16 = jnp.where(head[:, None] // SSM_GROUP == head[None, :] // SSM_GROUP, 1.0 / SSM_GROUP, 0.0).astype(BF16)
    hd = D_CONV // CONV_HEADS
    avg64 = jnp.where(head[:, None] // hd == head[None, :] // hd, 1.0 / hd, 0.0).astype(BF16)

    (proj, h1), (w_down_s, ffn_conv_s) = _pre_mix(xt, sc1, sh1, g_pre_mix, w_in_s, tw,
                                                  ([w_down[0].astype(BF16), ffn_conv_w[0]], False))
    wd4 = w_down_s.reshape(4, FF_SHARD, D_MODEL)
    u_perm = _to_scan_rows(proj[:, :D_SSM])
    (s_re, s_im, y_perm), (w_up_s,) = _ssm_fwd(u_perm, big_b_re, big_b_im, big_c_re, big_c_im, lam_r, lam_i,
                                               ([w_up[0].T.astype(BF16)], False))
    yssm = _from_scan_rows(y_perm)
    mix_args = (ssm_d, glu_full, glu_b, g_out_ssm, cw_full, g_out_conv, avg16, avg64)
    ycat = _mix_fwd(yssm, proj, *mix_args, tm)
    o, x1, h2 = _out_proj(ycat, w_out_full, xt, gt1, g_post_mix, g_pre_ffn, sc2, sh2, tm)
    up8, hid8 = _ffn_up(h2, w_up_s, ffn_conv_s, tw)
    hid4 = hid8.reshape(2, 4, T, FF_SHARD)
    ddn, dx2, loss_parts, d_gt2, d_g_post_ffn = _ffn_down(hid4, wd4, x1, tgt, gt2, g_post_ffn, tm)
    loss_local = jnp.sum(loss_parts[:, 0, 0])

    got = {}
    dhid, act = _ffn_dact(ddn, wd4, hid4, tm)
    g_w_down = _grad_tn(act, ddn, pl.BlockSpec((None, tw, FF_SHARD), lambda g, k: (g, k, 0)),
                        pl.BlockSpec((tw, D_MODEL), lambda g, k: (k, 0)), 4, FF_SHARD, D_MODEL, tw, 'grad_w_down')
    (dup8, dcw_ffn), (got['w_down'],) = _ffn_dup(dhid.reshape(N_DEV, T, FF_SHARD), up8, ffn_conv_s, tm,
                                                 ([g_w_down.reshape(N_DEV, D_FF // N_DEV, D_MODEL)], True))
    g_w_up_halves = _grad_tn(dup8, h2, pl.BlockSpec((None, tw, FF_SHARD), lambda g, k: (g, k, 0)),
                             pl.BlockSpec((tw, D_MODEL), lambda g, k: (k, 0)), N_DEV, FF_SHARD, D_MODEL, tw,
                             'grad_w_up', parts=2)
    (dx1, d_sh2, d_sc2, d_g_pre_ffn, d_o, d_gt1, d_g_post_mix), (got_up_0, got['ffn_conv_w']) = _pre_norm_bwd(
        dup8, pl.BlockSpec((2, tw, FF_SHARD), lambda i, j: (j, i, 0)), w_up_s, x1, dx2, sc2, g_pre_ffn, tw,
        'ffn_in_bwd', ([g_w_up_halves[0], dcw_ffn], True), below=(o, gt1, g_post_mix), group=2, w_t=True)

    g_w_out = _grad_tn(ycat, d_o, pl.BlockSpec((tw, D_MODEL), lambda g, k: (k, 0)),
                       pl.BlockSpec((tw, D_MODEL), lambda g, k: (k, 0)), 1, D_MODEL, D_MODEL, tw, 'grad_w_out')
    dy, dconv, dbg, z_b, dlin_b, sums = _mix_bwd(d_o, w_out_full, yssm, proj, *mix_args, tm)
    g_glu_w = _grad_tn(z_b, dlin_b, pl.BlockSpec((tw, D_SSM), lambda g, k: (k, 0)),
                       pl.BlockSpec((tw, D_SSM), lambda g, k: (k, 0)), 1, D_SSM, D_SSM, tw, 'grad_glu_w')
    dy_perm = _to_scan_rows(dy)
    (du_perm, dbr_blk, dbi_blk, dcr_blk, dci_blk, dar_blk, dai_blk), (got_up_1, got['w_out'], got['glu_w']) = _ssm_bwd(
        dy_perm, u_perm, s_re, s_im, big_b_re, big_b_im, big_c_re, big_c_im, lam_r, lam_i,
        ([g_w_up_halves[1], g_w_out.reshape(N_DEV, D_MODEL // N_DEV, D_MODEL),
          g_glu_w.reshape(N_DEV, D_SSM // N_DEV, D_SSM)], True))
    du_ssm = _from_scan_rows(du_perm)
    dproj = _mix_bwd_proj(dconv, proj, du_ssm, dy, ssm_d, dbg, cw_full, tm)
    dbb_re = _diag_blocks(dbr_blk, True).reshape(N_GROUPS, -1)
    dbb_im = _diag_blocks(dbi_blk, True).reshape(N_GROUPS, -1)
    d_c_re = _diag_blocks(dcr_blk, False).transpose(0, 2, 1)
    d_c_im = _diag_blocks(dci_blk, False).transpose(0, 2, 1)
    lane = jnp.arange(SSM_STATE * SSM_GROUP)
    seg = jnp.where(lane[:, None] // SSM_GROUP == lane[None, :] // SSM_GROUP, 1.0, 0.0).astype(BF16)
    d_b_re_x, d_b_im_x, d_lre_x, d_lim_x, d_lst = _ssm_prep_bwd(
        lre_x, lim_x, lst_x, b_re_x, b_im_x, dbb_re, dbb_im, _expand(dar_blk.reshape(N_GROUPS, SSM_STATE)),
        _expand(dai_blk.reshape(N_GROUPS, SSM_STATE)), seg)

    row = lambda a: a.reshape(-1, PACK_COLS)
    blank = jnp.zeros((1, PACK_COLS), F32)
    small_pack = jnp.concatenate([
        d_b_re_x, d_b_im_x, row(d_c_re), row(d_c_im), blank, blank, d_gt1, d_sh2, d_sc2, d_gt2, blank,
        d_g_post_mix, row(d_lre_x[:, ::SSM_GROUP]), row(d_lim_x[:, ::SSM_GROUP]),
        jnp.pad(d_lst.reshape(1, N_GROUPS), ((0, 0), (0, PACK_COLS - N_GROUPS))), row(sums[0:4]), d_g_pre_ffn,
        d_g_post_ffn, jnp.zeros((SMALL_ROWS - 145, PACK_COLS), F32)])
    g_w_in, (small_all,) = _grad_w_in(h1, dproj, tw, ([small_pack], False))
    g_conv_slots = jnp.concatenate([sums[4:7], jnp.zeros((5, D_CONV), F32)]).reshape(
        8, N_DEV, D_CONV // N_DEV).transpose(1, 0, 2)
    (grad_x, d_sh1, d_sc1, d_g_pre_mix), (got['w_in'], got['conv_w']) = _pre_norm_bwd(
        dproj, pl.BlockSpec((tw, 4 * IN_SHARD), lambda i, j: (i, j)), w_in_s, xt, dx1, sc1, g_pre_mix, tw,
        'mix_in_bwd', ([g_w_in, g_conv_slots], True), group=4)
    late_pack = jnp.concatenate([d_sh1, d_sc1, d_g_pre_mix, jnp.full((1, PACK_COLS), loss_local, F32),
                                 jnp.zeros((4, PACK_COLS), F32)])
    (late_all,) = _exchange([late_pack], name='gather_late_grads', scatter=False)
    loss = jnp.sum(late_all[:, 3, 0])
    res = _adamw_small(small_all, late_all, wts, mom_m, mom_v)

    dmod_all = jnp.concatenate([late_all[:, 0:2, :], small_all[:, B_ADA_ROW + 2:B_ADA_ROW + N_MOD, :]],
                               axis=1).reshape(N_DEV, N_MOD * D_MODEL)
    dmod_cols = lax.dynamic_slice(dmod_all, (0, me * ADA_SHARD), (N_DEV, ADA_SHARD))
    g_w_ada = _grad_w_ada(c_act.T, dmod_cols)

    pieces = {n: [slots[:, :3, :] if n in ('conv_w', 'ffn_conv_w') else slots] for n, slots in got.items()}
    for n, parts in pieces.items():
        outs = _adamw(parts, wts[n][0], mom_m[n][0], mom_v[n][0], 'adamw_' + n)
        for kind, val in zip(('g', 'd', 'm', 'v'), outs):
            res[kind, n] = val[None]
    outs = _adamw([got_up_0, got_up_1], w_up[0].T, m_w_up[0].T, v_w_up[0].T, 'adamw_w_up')
    for kind, val in zip(('g', 'd', 'm', 'v'), outs):
        res[kind, 'w_up'] = val.T[None]
    outs = _adamw([g_w_ada[None]], w_ada[0], m_w_ada[0], v_w_ada[0], 'adamw_w_ada')
    for kind, val in zip(('g', 'd', 'm', 'v'), outs):
        res[kind, 'w_ada'] = val[None]

    return (loss, grad_x[None], *[res['g', n] for n in WEIGHTS], *[res['d', n] for n in WEIGHTS],
            *[res['m', n] for n in WEIGHTS], *[res['v', n] for n in WEIGHTS])
```

```python
import math

import jax
import jax.numpy as jnp
from jax import lax
from jax.experimental import pallas as pl
from jax.experimental.pallas import tpu as pltpu

F32, BF16 = jnp.float32, jnp.bfloat16

D_MODEL = 1024
D_SSM = 512
D_CONV = 512
SSM_GROUP = 16
N_GROUPS = 32
SSM_STATE = 64
N_STATE = N_GROUPS * SSM_STATE
CONV_HEADS = 8
D_FF = 2816
N_MOD = 6
D_IN_PROJ = D_SSM + 3 * D_CONV
N_DEV = 8
FF_SHARD = 2 * D_FF // N_DEV
IN_SHARD = D_IN_PROJ // N_DEV
ADA_SHARD = N_MOD * D_MODEL // N_DEV
EPS = 1e-6
LAMBDA_RE_MAX = -1e-4
ADAM_LR, ADAM_B1, ADAM_B2, ADAM_EPS, ADAM_WD, ADAM_STEP = 0.001, 0.9, 0.999, 1e-08, 0.01, 10
GELU_C = math.sqrt(2.0 / math.pi)
GELU_A = 0.044715

SUBLANES = 8
HALO = 8
HALO16 = 16
SCAN_UNROLL = 8
STATE_BLOCK = 512
CHAN_BLOCK = 128
VMEM_BIG = 48 << 20
VMEM_MOST = 58 << 20

WEIGHTS = ['w_ada', 'b_ada', 'g_pre_mix', 'g_post_mix', 'w_in', 'ssm_lam_re', 'ssm_lam_im', 'ssm_log_step',
           'ssm_b_re', 'ssm_b_im', 'ssm_c_re', 'ssm_c_im', 'ssm_d', 'glu_w', 'glu_b', 'g_out_ssm', 'conv_w',
           'g_out_conv', 'w_out', 'g_pre_ffn', 'g_post_ffn', 'w_up', 'ffn_conv_w', 'w_down']
SHARDED = ('w_ada', 'w_in', 'glu_w', 'conv_w', 'w_out', 'w_up', 'ffn_conv_w', 'w_down')
PACK_COLS = 1024


def _call(body, *, name, grid, in_specs, out_specs, out_shape, scratch=(), sem=None, vmem=None, ride=None):
    params = {}
    if vmem is not None:
        params['vmem_limit_bytes'] = vmem
    if ride is None:
        if sem is not None:
            params['dimension_semantics'] = sem
        return pl.pallas_call(body, name=name, grid=grid, in_specs=in_specs, out_specs=out_specs,
                              out_shape=out_shape, scratch_shapes=list(scratch),
                              compiler_params=pltpu.CompilerParams(**params))
    arrs, scatter = ride
    single = not isinstance(out_shape, (list, tuple))
    out_shape_l = [out_shape] if single else list(out_shape)
    out_specs_l = [out_specs] if single else list(out_specs)
    n, n_in, n_out, n_scr = len(arrs), len(in_specs), len(out_shape_l), len(scratch)
    any_spec = pl.BlockSpec(memory_space=pl.ANY)
    params['dimension_semantics'] = ('arbitrary',) * len(grid)

    def carried(*refs):
        ins, rin = refs[:n_in], refs[n_in:n_in + n]
        outs, rout = refs[n_in + n:n_in + n + n_out], refs[n_in + n + n_out:n_in + 2 * n + n_out]
        scr, sems = refs[n_in + 2 * n + n_out:n_in + 2 * n + n_out + n_scr], refs[n_in + 2 * n + n_out + n_scr:]
        first = pl.program_id(0) == 0
        last = pl.program_id(0) == grid[0] - 1
        for ax in range(1, len(grid)):
            first = jnp.logical_and(first, pl.program_id(ax) == 0)
            last = jnp.logical_and(last, pl.program_id(ax) == grid[ax] - 1)

        @pl.when(first)
        def _():
            _exchange_start(rin, rout, sems, scatter)

        body(*ins, *outs, *scr)

        @pl.when(last)
        def _():
            _exchange_wait(rin, rout, sems, scatter)

    call = pl.pallas_call(carried, name=name, grid=grid, in_specs=list(in_specs) + [any_spec] * n,
                          out_specs=out_specs_l + [any_spec] * n,
                          out_shape=out_shape_l + _exchange_shapes(arrs, scatter),
                          scratch_shapes=list(scratch) + _exchange_sems(n),
                          compiler_params=pltpu.CompilerParams(**params))

    def run(*args):
        res = call(*args, *arrs)
        own = res[0] if single else list(res[:n_out])
        return own, list(res[n_out:])

    return run


def _const(shape):
    nd = len(shape)
    return pl.BlockSpec(shape, lambda *_: (0,) * nd)


def _sds(shape, dtype=F32):
    return jax.ShapeDtypeStruct(shape, dtype)


def _dot(a, b):
    return jnp.dot(a, b, preferred_element_type=F32)


def _dot_nt(a, b):
    return lax.dot_general(a, b, (((1,), (1,)), ((), ())), preferred_element_type=F32)


def _dot_tn(a, b):
    return lax.dot_general(a, b, (((0,), (0,)), ((), ())), preferred_element_type=F32)


def _dot_split(x, mat, parts):
    acc = None
    rem = x
    for _ in range(parts):
        piece = rem.astype(BF16)
        rem = rem - piece.astype(F32)
        term = _dot(piece, mat)
        acc = term if acc is None else acc + term
    return acc


def _sigmoid(x):
    return 1.0 / (1.0 + jnp.exp(-x))


def _gelu(x):
    t = jnp.tanh(GELU_C * (x + GELU_A * x * x * x))
    return 0.5 * x * (1.0 + t), t


def _gelu_grad(x, t):
    return 0.5 * (1.0 + t) + 0.5 * x * (1.0 - t * t) * GELU_C * (1.0 + 3.0 * GELU_A * x * x)


def _rsqrt_mean(x):
    return lax.rsqrt(jnp.mean(x * x, axis=-1, keepdims=True) + EPS)


def _colsum(x):
    return jnp.sum(x, axis=0, keepdims=True)


def _shifts_down(x, halo):
    ext = jnp.concatenate([halo, x], axis=0)
    return pltpu.roll(ext, 1, 0)[halo.shape[0]:], pltpu.roll(ext, 2, 0)[halo.shape[0]:]


def _shifts_up(x, halo):
    n = x.shape[0]
    ext = jnp.concatenate([x, halo], axis=0)
    total = ext.shape[0]
    return pltpu.roll(ext, total - 1, 0)[:n], pltpu.roll(ext, total - 2, 0)[:n]


def _conv3(x, halo, w_ref):
    x1, x2 = _shifts_down(x, halo)
    return w_ref[0:1, :] * x2 + w_ref[1:2, :] * x1 + w_ref[2:3, :] * x, x1, x2


def _conv3_t(g, halo, w_ref):
    g1, g2 = _shifts_up(g, halo)
    return w_ref[2:3, :] * g + w_ref[1:2, :] * g1 + w_ref[0:1, :] * g2, g1, g2


def _silu_parts(x):
    s = _sigmoid(x)
    return x * s, s * (1.0 + x * (1.0 - s))


def _norm_bwd(dn, x, r, g):
    gd = g * dn
    return r * gd - x * (r * r * r) * jnp.mean(gd * x, axis=-1, keepdims=True)


def _head_norm_bwd(dn, y, rs, g, avg):
    gd = g * dn
    return rs * gd - y * (rs * rs * rs) * _dot_split(gd * y, avg, 2)


def _me():
    x, y, c = lax.axis_index('x'), lax.axis_index('y'), lax.axis_index('c')
    return x, y, c, 4 * x + 2 * y + c


def _peer(k):
    x, y, c, _ = _me()
    px = 1 - x if k & 4 else x
    py = 1 - y if k & 2 else y
    pc = 1 - c if k & 1 else c
    return (px, py, pc), 4 * px + 2 * py + pc


SIBLING = 1
OTHER_CHIPS = (2, 4, 6)


def _remote(src, dst, sems, a, k, dev):
    return pltpu.make_async_remote_copy(src_ref=src, dst_ref=dst, send_sem=sems[0].at[a, k - 1],
                                        recv_sem=sems[1].at[a, k - 1], device_id=dev,
                                        device_id_type=pl.DeviceIdType.MESH)


def _exchange_copies(ins, outs, sems, scatter):
    me = _me()[3]
    local, first, relay, arrivals = [], [], [], []
    for a in range(len(ins)):
        src = ins[a].at[me] if scatter else ins[a]
        local.append(pltpu.make_async_copy(src, outs[a].at[me], sems[2].at[a]))
        for k in range(1, N_DEV):
            dev, idx = _peer(k)
            landed = _remote(src, outs[a].at[idx], sems, a, k, dev)
            if scatter:
                first.append(_remote(ins[a].at[idx], outs[a].at[me], sems, a, k, dev))
                arrivals.append(landed)
            elif k == SIBLING:
                first.append(_remote(src, outs[a].at[me], sems, a, k, dev))
                arrivals.append(landed)
            elif k in OTHER_CHIPS:
                first.append(_remote(src, outs[a].at[me], sems, a, k, dev))
                sib, _ = _peer(SIBLING)
                relay.append((landed, _remote(outs[a].at[idx], outs[a].at[idx], sems, a, k | SIBLING, sib)))
            else:
                arrivals.append(landed)
    return local, first, relay, arrivals


def _exchange_start(ins, outs, sems, scatter):
    local, first, _, _ = _exchange_copies(ins, outs, sems, scatter)
    for cp in local + first:
        cp.start()


def _exchange_wait(ins, outs, sems, scatter):
    local, first, relay, arrivals = _exchange_copies(ins, outs, sems, scatter)
    for landed, forward in relay:
        landed.wait_recv()
        forward.start()
    for cp in arrivals:
        cp.wait_recv()
    for cp in first + [forward for _, forward in relay]:
        cp.wait_send()
    for cp in local:
        cp.wait()


def _exchange_shapes(arrs, scatter):
    return [_sds(a.shape if scatter else (N_DEV,) + a.shape, a.dtype) for a in arrs]


def _exchange_sems(n):
    return [pltpu.SemaphoreType.DMA((n, N_DEV - 1)), pltpu.SemaphoreType.DMA((n, N_DEV - 1)),
            pltpu.SemaphoreType.DMA((n,))]


def _exchange(arrs, *, name, scatter):
    n = len(arrs)

    def body(*refs):
        _exchange_start(refs[:n], refs[n:2 * n], refs[2 * n:], scatter)
        _exchange_wait(refs[:n], refs[n:2 * n], refs[2 * n:], scatter)

    any_spec = pl.BlockSpec(memory_space=pl.ANY)
    outs = pl.pallas_call(body, name=name, out_shape=_exchange_shapes(arrs, scatter), in_specs=[any_spec] * n,
                          out_specs=[any_spec] * n, scratch_shapes=_exchange_sems(n))(*arrs)
    return list(outs)


def _mod_cols(c_all, w_ada, b_cols):
    def body(c_ref, w_ref, b_ref, mod_ref, act_ref):
        c = c_ref[...]
        act = c * _sigmoid(c)
        act_ref[...] = act
        mod_ref[...] = _dot(act.astype(BF16), w_ref[...].astype(BF16)) + b_ref[...]

    return _call(body, name='mod_cols', grid=(1,),
                 in_specs=[_const(c_all.shape), _const(w_ada.shape), _const(b_cols.shape)],
                 out_specs=[_const((N_DEV, ADA_SHARD)), _const(c_all.shape)],
                 out_shape=[_sds((N_DEV, ADA_SHARD)), _sds(c_all.shape)], vmem=VMEM_BIG)(c_all, w_ada, b_cols)


def _grad_w_ada(act_t, dmod_cols):
    def body(a_ref, d_ref, o_ref):
        o_ref[...] = _dot(a_ref[...], d_ref[...])

    return _call(body, name='grad_w_ada', grid=(1,), in_specs=[_const(act_t.shape), _const(dmod_cols.shape)],
                 out_specs=_const((D_MODEL, ADA_SHARD)), out_shape=_sds((D_MODEL, ADA_SHARD)),
                 vmem=VMEM_BIG)(act_t, dmod_cols)


def _pre_mix(x, sc, sh, g, w_s, tm, ride):
    T = x.shape[0]

    def body(x_ref, sc_ref, sh_ref, g_ref, w_ref, proj_ref, h_ref):
        @pl.when(pl.program_id(1) == 0)
        def _():
            xv = x_ref[...]
            h_ref[...] = ((xv * _rsqrt_mean(xv) * g_ref[...]) * (1.0 + sc_ref[...]) + sh_ref[...]).astype(BF16)

        for s in range(2):
            proj_ref[:, s * IN_SHARD:(s + 1) * IN_SHARD] = _dot(h_ref[...], w_ref[s])

    row = pl.BlockSpec((tm, D_MODEL), lambda i, j: (i, 0))
    vec = _const((1, D_MODEL))
    return _call(body, name='pre_mix', grid=(T // tm, N_DEV // 2),
                 in_specs=[row, vec, vec, vec, pl.BlockSpec((2, D_MODEL, IN_SHARD), lambda i, j: (j, 0, 0))],
                 out_specs=[pl.BlockSpec((tm, 2 * IN_SHARD), lambda i, j: (i, j)), row],
                 out_shape=[_sds((T, D_IN_PROJ)), _sds((T, D_MODEL), BF16)],
                 sem=('parallel', 'arbitrary'), ride=ride)(x, sc, sh, g, w_s)


def _halo_before(tm, rows=HALO):
    return lambda i: jnp.maximum(i * (tm // rows) - 1, 0)


def _halo_after(tm, T, rows=HALO):
    return lambda i: jnp.minimum((i + 1) * (tm // rows), T // rows - 1)


def _mix_fwd(yssm, proj, d, glu_w, glu_b, g_ssm, cw, g_conv, avg16, avg64, tm):
    T = yssm.shape[0]
    hb = _halo_before(tm)

    def body(y_ref, p_ref, ph_ref, d_ref, gw_ref, gb_ref, gs_ref, cw_ref, gc_ref, a16_ref, a64_ref, o_ref):
        i = pl.program_id(0)
        u = p_ref[:, 0:D_SSM]
        y = y_ref[...] + d_ref[...] * u
        z, _ = _gelu(y)
        gate = _sigmoid(_dot(z.astype(BF16), gw_ref[...]) + gb_ref[...])
        ya = z * gate
        rs = lax.rsqrt(_dot_split(ya * ya, a16_ref[...], 2) + EPS)
        o_ref[:, 0:D_SSM] = (ya * rs * gs_ref[...]).astype(BF16)
        bg = p_ref[:, D_SSM:D_SSM + D_CONV]
        cv = p_ref[:, D_SSM + D_CONV:D_SSM + 2 * D_CONV] * p_ref[:, D_SSM + 2 * D_CONV:D_IN_PROJ]
        hv = ph_ref[:, D_SSM + D_CONV:D_SSM + 2 * D_CONV] * ph_ref[:, D_SSM + 2 * D_CONV:D_IN_PROJ]
        hv = jnp.where(i > 0, hv, 0.0)
        conv, _, _ = _conv3(cv, hv, cw_ref)
        yb = bg * conv
        rsb = lax.rsqrt(_dot_split(yb * yb, a64_ref[...], 2) + EPS)
        o_ref[:, D_SSM:D_MODEL] = (yb * rsb * gc_ref[...]).astype(BF16)

    vec = _const((1, D_SSM))
    sq = _const((D_SSM, D_SSM))
    return _call(body, name='mix_fwd', grid=(T // tm,),
                 in_specs=[pl.BlockSpec((tm, D_SSM), lambda i: (i, 0)), pl.BlockSpec((tm, D_IN_PROJ), lambda i: (i, 0)),
                           pl.BlockSpec((HALO, D_IN_PROJ), lambda i: (hb(i), 0)), vec, sq, vec, vec,
                           _const((3, D_CONV)), vec, sq, sq],
                 out_specs=pl.BlockSpec((tm, D_MODEL), lambda i: (i, 0)), out_shape=_sds((T, D_MODEL), BF16),
                 sem=('parallel',), vmem=VMEM_BIG)(yssm, proj, proj, d, glu_w, glu_b, g_ssm, cw, g_conv, avg16, avg64)


def _out_proj(ycat, w_out, x, gt, g_post, g_pre, sc, sh, tm):
    T = x.shape[0]

    def body(y_ref, w_ref, x_ref, gt_ref, gp_ref, g2_ref, sc_ref, sh_ref, o_ref, x1_ref, h_ref):
        o = _dot(y_ref[...], w_ref[...])
        o_ref[...] = o
        x1 = x_ref[...] + gt_ref[...] * (o * _rsqrt_mean(o) * gp_ref[...])
        x1_ref[...] = x1
        h_ref[...] = ((x1 * _rsqrt_mean(x1) * g2_ref[...]) * (1.0 + sc_ref[...]) + sh_ref[...]).astype(BF16)

    row = pl.BlockSpec((tm, D_MODEL), lambda i: (i, 0))
    vec = _const((1, D_MODEL))
    return _call(body, name='out_proj', grid=(T // tm,),
                 in_specs=[row, _const((D_MODEL, D_MODEL)), row, vec, vec, vec, vec, vec],
                 out_specs=[row, row, row],
                 out_shape=[_sds((T, D_MODEL)), _sds((T, D_MODEL)), _sds((T, D_MODEL), BF16)],
                 sem=('parallel',), vmem=VMEM_BIG)(ycat, w_out, x, gt, g_post, g_pre, sc, sh)


def _ffn_up(h2, w_s, cw8, tm):
    T = h2.shape[0]
    hb = _halo_before(tm, HALO16)

    def body(h_ref, hh_ref, w_ref, cw_ref, up_ref, hid_ref):
        up = _dot_nt(h_ref[...], w_ref[...])
        up_ref[...] = up.astype(BF16)
        before = jnp.where(pl.program_id(0) > 0, _dot_nt(hh_ref[...], w_ref[...]), 0.0)
        hid_ref[...] = _conv3(up, before, cw_ref)[0].astype(BF16)

    out = pl.BlockSpec((None, tm, FF_SHARD), lambda i, j: (j, i, 0))
    return _call(body, name='ffn_up', grid=(T // tm, N_DEV),
                 in_specs=[pl.BlockSpec((tm, D_MODEL), lambda i, j: (i, 0)),
                           pl.BlockSpec((HALO16, D_MODEL), lambda i, j: (hb(i), 0)),
                           pl.BlockSpec((None, FF_SHARD, D_MODEL), lambda i, j: (j, 0, 0)),
                           pl.BlockSpec((None, 3, FF_SHARD), lambda i, j: (j, 0, 0))],
                 out_specs=[out, out], out_shape=[_sds((N_DEV, T, FF_SHARD), BF16)] * 2,
                 sem=('parallel', 'parallel'))(h2, h2, w_s, cw8)


def _ffn_down(hid4, wd4, x1, tgt, gt, g_post, tm):
    T = x1.shape[0]
    nb = T // tm

    def body(a_ref, w_ref, x1_ref, t_ref, gt_ref, g_ref, ddn_ref, dx_ref, loss_ref, dgt_ref, dg_ref, dn_ref):
        i, j = pl.program_id(0), pl.program_id(1)
        part = None
        for s in range(2):
            act = (_silu_parts(a_ref[0, s].astype(F32))[0] * a_ref[1, s].astype(F32)).astype(BF16)
            term = _dot(act, w_ref[s])
            part = term if part is None else part + term

        @pl.when(jnp.logical_and(i == 0, j == 0))
        def _():
            dgt_ref[...] = jnp.zeros_like(dgt_ref)
            dg_ref[...] = jnp.zeros_like(dg_ref)

        @pl.when(j == 0)
        def _():
            dn_ref[...] = part

        @pl.when(j > 0)
        def _():
            dn_ref[...] += part

        @pl.when(j == 1)
        def _():
            dn, gv, gate = dn_ref[...], g_ref[...], gt_ref[...]
            r = _rsqrt_mean(dn)
            normed = dn * r * gv
            err = x1_ref[...] + gate * normed - t_ref[...]
            dx = err * (1.0 / D_MODEL)
            dx_ref[...] = dx
            tot = jnp.sum(jnp.sum(err * err, axis=1, keepdims=True), axis=0, keepdims=True) * (0.5 / D_MODEL)
            loss_ref[...] = jnp.broadcast_to(tot, (8, 128))
            dgt_ref[...] += _colsum(dx * normed)
            dnn = dx * gate
            dg_ref[...] += _colsum(dnn * dn * r)
            ddn_ref[...] = _norm_bwd(dnn, dn, r, gv).astype(BF16)

    row = pl.BlockSpec((tm, D_MODEL), lambda i, j: (i, 0))
    vec = _const((1, D_MODEL))
    return _call(body, name='ffn_down', grid=(nb, 2),
                 in_specs=[pl.BlockSpec((2, 2, tm, FF_SHARD), lambda i, j: (0, j, i, 0)),
                           pl.BlockSpec((2, FF_SHARD, D_MODEL), lambda i, j: (j, 0, 0)), row, row, vec, vec],
                 out_specs=[row, row, pl.BlockSpec((None, 8, 128), lambda i, j: (i, 0, 0)), vec, vec],
                 out_shape=[_sds((T, D_MODEL), BF16), _sds((T, D_MODEL)), _sds((nb, 8, 128)), _sds((1, D_MODEL)),
                            _sds((1, D_MODEL))],
                 scratch=[pltpu.VMEM((tm, D_MODEL), F32)], sem=('arbitrary', 'arbitrary'),
                 vmem=VMEM_BIG)(hid4, wd4, x1, tgt, gt, g_post)


def _ssm_prep(lre, lim, lst, b_re, b_im):
    def body(lre_ref, lim_ref, lst_ref, br_ref, bi_ref, ar_ref, ai_ref, bbr_ref, bbi_ref):
        ar, ai, qr, qi = _zoh(lre_ref[...], lim_ref[...], lst_ref[...])[:4]
        ar_ref[...] = ar
        ai_ref[...] = ai
        bbr_ref[...] = qr * br_ref[...] - qi * bi_ref[...]
        bbi_ref[...] = qr * bi_ref[...] + qi * br_ref[...]

    shp = lre.shape
    return _call(body, name='ssm_prep', grid=(1,), in_specs=[_const(shp)] * 5, out_specs=[_const(shp)] * 4,
                 out_shape=[_sds(shp)] * 4)(lre, lim, lst, b_re, b_im)


def _zoh(lre, lim, lst):
    lr = jnp.minimum(lre, LAMBDA_RE_MAX)
    st = jnp.exp(lst)
    mag = jnp.exp(lr * st)
    ar = mag * jnp.cos(lim * st)
    ai = mag * jnp.sin(lim * st)
    den = lr * lr + lim * lim
    qr = ((ar - 1.0) * lr + ai * lim) / den
    qi = (ai * lr - (ar - 1.0) * lim) / den
    return ar, ai, qr, qi, lr, st, den


def _ssm_prep_bwd(lre, lim, lst, b_re, b_im, dbbr, dbbi, dar, dai, seg):
    def body(lre_ref, lim_ref, lst_ref, br_ref, bi_ref, dbbr_ref, dbbi_ref, dar_ref, dai_ref, seg_ref,
             dbr_ref, dbi_ref, dlre_ref, dlim_ref, dlst_ref):
        lre_v = lre_ref[...]
        li = lim_ref[...]
        ar, ai, qr, qi, lr, st, den = _zoh(lre_v, li, lst_ref[...])
        br, bi, gbr, gbi = br_ref[...], bi_ref[...], dbbr_ref[...], dbbi_ref[...]
        dbr_ref[...] = qr * gbr + qi * gbi
        dbi_ref[...] = qr * gbi - qi * gbr
        gqr = _dot_split(br * gbr + bi * gbi, seg_ref[...], 3)
        gqi = _dot_split(br * gbi - bi * gbr, seg_ref[...], 3)
        ir, ii = lr / den, -li / den
        gar = dar_ref[...] + ir * gqr + ii * gqi
        gai = dai_ref[...] + ir * gqi - ii * gqr
        tr, ti = qr * ir - qi * ii, qr * ii + qi * ir
        glr = -(tr * gqr + ti * gqi)
        gli = -(tr * gqi - ti * gqr)
        gzr = ar * gar + ai * gai
        gzi = ar * gai - ai * gar
        glr = glr + st * gzr
        gli = gli + st * gzi
        gst = (lr * gzr + li * gzi) * st
        dlre_ref[...] = jnp.where(lre_v < LAMBDA_RE_MAX, glr, 0.0)
        dlim_ref[...] = gli
        dlst_ref[...] = jnp.sum(gst, axis=1, keepdims=True) * (1.0 / SSM_GROUP)

    shp = lre.shape
    return _call(body, name='ssm_prep_bwd', grid=(1,), in_specs=[_const(shp)] * 9 + [_const(seg.shape)],
                 out_specs=[_const(shp)] * 4 + [_const((N_GROUPS, 1))],
                 out_shape=[_sds(shp)] * 4 + [_sds((N_GROUPS, 1))], vmem=VMEM_BIG)(
                     lre, lim, lst, b_re, b_im, dbbr, dbbi, dar, dai, seg)


def _scan_specs(T):
    return dict(
        chan=pl.BlockSpec((T, CHAN_BLOCK), lambda cb: (0, cb)),
        state=pl.BlockSpec((T, STATE_BLOCK), lambda cb: (0, cb)),
        b=pl.BlockSpec((CHAN_BLOCK, STATE_BLOCK), lambda cb: (cb, cb)),
        c=pl.BlockSpec((STATE_BLOCK, CHAN_BLOCK), lambda cb: (cb, cb)),
        lam=pl.BlockSpec((1, STATE_BLOCK), lambda cb: (0, cb)),
    )


def _complex_power(re, im, n):
    out = None
    while True:
        if n & 1:
            out = (re, im) if out is None else (out[0] * re - out[1] * im, out[0] * im + out[1] * re)
        n >>= 1
        if n == 0:
            return out
        re, im = re * re - im * im, 2.0 * re * im


def _rows8(i):
    if isinstance(i, int):
        return pl.ds(i * SUBLANES, SUBLANES)
    return pl.ds(pl.multiple_of(i * SUBLANES, SUBLANES), SUBLANES)


def _scan_loop(n_steps, body, init):
    trips = n_steps // SCAN_UNROLL

    def trip(t, carry):
        for u in range(SCAN_UNROLL):
            carry = body(t * SCAN_UNROLL + u, carry)
        return carry

    carry = lax.fori_loop(0, trips, trip, init)
    for step in range(trips * SCAN_UNROLL, n_steps):
        carry = body(step, carry)
    return carry


def _ssm_fwd(u_perm, b_re, b_im, c_re, c_im, lam_r, lam_i, ride):
    T = u_perm.shape[0]
    ls = T // SUBLANES
    rc = min(512, T)
    sp = _scan_specs(T)

    def body(u_ref, bre_ref, bim_ref, cre_ref, cim_ref, lr_ref, li_ref, so_re_ref, so_im_ref, y_ref, sre_ref, sim_ref):
        for c in range(T // rc):
            rows = pl.ds(c * rc, rc)
            ub = u_ref[rows, :].astype(BF16)
            sre_ref[rows, :] = _dot(ub, bre_ref[...])
            sim_ref[rows, :] = _dot(ub, bim_ref[...])
        shp = (SUBLANES, STATE_BLOCK)
        lr = jnp.broadcast_to(lr_ref[...], shp)
        li = jnp.broadcast_to(li_ref[...], shp)
        zero = jnp.zeros(shp, F32)

        def step(i, carry):
            sr, si = carry
            rows = _rows8(i)
            nr = lr * sr - li * si + sre_ref[rows, :]
            ni = lr * si + li * sr + sim_ref[rows, :]
            sre_ref[rows, :] = nr
            sim_ref[rows, :] = ni
            return nr, ni

        fr, fi = _scan_loop(ls, step, (zero, zero))
        pr, pi_ = _complex_power(lr, li, ls)
        row = lax.broadcasted_iota(jnp.int32, shp, 0)
        ir, ii = zero, zero
        for _ in range(SUBLANES - 1):
            er = fr + pr * ir - pi_ * ii
            ei = fi + pr * ii + pi_ * ir
            ir = jnp.where(row == 0, 0.0, pltpu.roll(er, 1, 0))
            ii = jnp.where(row == 0, 0.0, pltpu.roll(ei, 1, 0))

        def fix(i, carry):
            cr, ci = carry
            rows = _rows8(i)
            nr = lr * cr - li * ci
            ni = lr * ci + li * cr
            sre_ref[rows, :] += nr
            sim_ref[rows, :] += ni
            return nr, ni

        _scan_loop(ls, fix, (ir, ii))
        for c in range(T // rc):
            rows = pl.ds(c * rc, rc)
            s_r, s_i = sre_ref[rows, :].astype(BF16), sim_ref[rows, :].astype(BF16)
            so_re_ref[rows, :] = s_r
            so_im_ref[rows, :] = s_i
            y_ref[rows, :] = _dot(s_r, cre_ref[...]) - _dot(s_i, cim_ref[...])

    return _call(body, name='ssm_fwd', grid=(N_STATE // STATE_BLOCK,),
                 in_specs=[sp['chan'], sp['b'], sp['b'], sp['c'], sp['c'], sp['lam'], sp['lam']],
                 out_specs=[sp['state'], sp['state'], sp['chan']],
                 out_shape=[_sds((T, N_STATE), BF16), _sds((T, N_STATE), BF16), _sds((T, D_SSM))],
                 scratch=[pltpu.VMEM((T, STATE_BLOCK), F32), pltpu.VMEM((T, STATE_BLOCK), F32)],
                 sem=('arbitrary',), vmem=VMEM_MOST, ride=ride)(u_perm, b_re, b_im, c_re, c_im, lam_r, lam_i)


def _ssm_bwd(dy_perm, u_perm, s_re, s_im, b_re, b_im, c_re, c_im, lam_r, lam_i, ride):
    T = u_perm.shape[0]
    ls = T // SUBLANES
    rc = min(512, T)
    sp = _scan_specs(T)
    ncb = N_STATE // STATE_BLOCK

    def body(dy_ref, u_ref, sre_ref, sim_ref, bre_ref, bim_ref, cre_ref, cim_ref, lr_ref, li_ref,
             du_ref, dbr_ref, dbi_ref, dcr_ref, dci_ref, dar_ref, dai_ref, gre_ref, gim_ref):
        shp = (SUBLANES, STATE_BLOCK)
        zero = jnp.zeros(shp, F32)
        tail = pl.ds(T, SUBLANES)
        gre_ref[tail, :] = zero
        gim_ref[tail, :] = zero
        for c in range(T // rc):
            rows = pl.ds(c * rc, rc)
            dyb = dy_ref[rows, :].astype(BF16)
            gre_ref[rows, :] = _dot_nt(dyb, cre_ref[...])
            gim_ref[rows, :] = -_dot_nt(dyb, cim_ref[...])
        lr = jnp.broadcast_to(lr_ref[...], shp)
        li = jnp.broadcast_to(li_ref[...], shp)

        def step(k, carry):
            gr, gi = carry
            rows = _rows8(ls - 1 - k)
            nr = lr * gr + li * gi + gre_ref[rows, :]
            ni = lr * gi - li * gr + gim_ref[rows, :]
            gre_ref[rows, :] = nr
            gim_ref[rows, :] = ni
            return nr, ni

        fr, fi = _scan_loop(ls, step, (zero, zero))
        pr, pi_ = _complex_power(lr, -li, ls)
        row = lax.broadcasted_iota(jnp.int32, shp, 0)
        cr, ci = zero, zero
        for _ in range(SUBLANES - 1):
            er = fr + pr * cr - pi_ * ci
            ei = fi + pr * ci + pi_ * cr
            cr = jnp.where(row == SUBLANES - 1, 0.0, pltpu.roll(er, SUBLANES - 1, 0))
            ci = jnp.where(row == SUBLANES - 1, 0.0, pltpu.roll(ei, SUBLANES - 1, 0))

        def fix(k, carry):
            dr, di = carry
            rows = _rows8(ls - 1 - k)
            dr, di = lr * dr + li * di, lr * di - li * dr
            gre_ref[rows, :] += dr
            gim_ref[rows, :] += di
            return dr, di

        _scan_loop(ls, fix, (cr, ci))

        acc_r = jnp.zeros((1, STATE_BLOCK), F32)
        acc_i = jnp.zeros((1, STATE_BLOCK), F32)
        for c in range(T // rc):
            rows, nxt = pl.ds(c * rc, rc), pl.ds(c * rc + SUBLANES, rc)
            s_r, s_i = sre_ref[rows, :].astype(F32), sim_ref[rows, :].astype(F32)
            g_r, g_i = gre_ref[nxt, :], gim_ref[nxt, :]
            acc_r = acc_r + _colsum(g_r * s_r + g_i * s_i)
            acc_i = acc_i + _colsum(g_i * s_r - g_r * s_i)
        last = pl.ds(T - 2 * SUBLANES, 2 * SUBLANES)
        first = pl.ds(0, SUBLANES)
        spr = jnp.where(row == 0, 0.0, pltpu.roll(sre_ref[last, :].astype(F32)[SUBLANES:], 1, 0))
        spi = jnp.where(row == 0, 0.0, pltpu.roll(sim_ref[last, :].astype(F32)[SUBLANES:], 1, 0))
        gr, gi = gre_ref[first, :], gim_ref[first, :]
        dar_ref[...] = acc_r + _colsum(gr * spr + gi * spi)
        dai_ref[...] = acc_i + _colsum(gi * spr - gr * spi)

        for c in range(T // rc):
            rows = pl.ds(c * rc, rc)
            g_r, g_i = gre_ref[rows, :].astype(BF16), gim_ref[rows, :].astype(BF16)
            s_r, s_i = sre_ref[rows, :], sim_ref[rows, :]
            ub, dyb = u_ref[rows, :].astype(BF16), dy_ref[rows, :].astype(BF16)
            du_ref[rows, :] = _dot_nt(g_r, bre_ref[...]) + _dot_nt(g_i, bim_ref[...])
            parts = (_dot_tn(ub, g_r), _dot_tn(ub, g_i), _dot_tn(s_r, dyb), -_dot_tn(s_i, dyb))
            outs = (dbr_ref, dbi_ref, dcr_ref, dci_ref)
            for o_ref, part in zip(outs, parts):
                if c == 0:
                    o_ref[...] = part
                else:
                    o_ref[...] += part

    blk = lambda r, c: pl.BlockSpec((None, r, c), lambda cb: (cb, 0, 0))
    return _call(body, name='ssm_bwd', grid=(ncb,),
                 in_specs=[sp['chan'], sp['chan'], sp['state'], sp['state'], sp['b'], sp['b'], sp['c'], sp['c'],
                           sp['lam'], sp['lam']],
                 out_specs=[sp['chan'], blk(CHAN_BLOCK, STATE_BLOCK), blk(CHAN_BLOCK, STATE_BLOCK),
                            blk(STATE_BLOCK, CHAN_BLOCK), blk(STATE_BLOCK, CHAN_BLOCK), blk(1, STATE_BLOCK),
                            blk(1, STATE_BLOCK)],
                 out_shape=[_sds((T, D_SSM)), _sds((ncb, CHAN_BLOCK, STATE_BLOCK)), _sds((ncb, CHAN_BLOCK, STATE_BLOCK)),
                            _sds((ncb, STATE_BLOCK, CHAN_BLOCK)), _sds((ncb, STATE_BLOCK, CHAN_BLOCK)),
                            _sds((ncb, 1, STATE_BLOCK)), _sds((ncb, 1, STATE_BLOCK))],
                 scratch=[pltpu.VMEM((T + SUBLANES, STATE_BLOCK), F32), pltpu.VMEM((T + SUBLANES, STATE_BLOCK), F32)],
                 sem=('arbitrary',), vmem=VMEM_MOST, ride=ride)(dy_perm, u_perm, s_re, s_im, b_re, b_im, c_re, c_im,
                                                                lam_r, lam_i)


def _ffn_dact(ddn, wd4, hid4, tm):
    T = ddn.shape[0]

    def body(d_ref, w_ref, hid_ref, o_ref, act_ref):
        dact = _dot_nt(d_ref[...], w_ref[...])
        silu, dsilu = _silu_parts(hid_ref[0].astype(F32))
        hid_v = hid_ref[1].astype(F32)
        o_ref[0] = (dact * hid_v * dsilu).astype(BF16)
        o_ref[1] = (dact * silu).astype(BF16)
        act_ref[...] = (silu * hid_v).astype(BF16)

    blk = pl.BlockSpec((2, None, tm, FF_SHARD), lambda i, j: (0, j, i, 0))
    return _call(body, name='ffn_dact', grid=(T // tm, 4),
                 in_specs=[pl.BlockSpec((tm, D_MODEL), lambda i, j: (i, 0)),
                           pl.BlockSpec((None, FF_SHARD, D_MODEL), lambda i, j: (j, 0, 0)), blk],
                 out_specs=[blk, pl.BlockSpec((None, tm, FF_SHARD), lambda i, j: (j, i, 0))],
                 out_shape=[_sds((2, 4, T, FF_SHARD), BF16), _sds((4, T, FF_SHARD), BF16)],
                 sem=('parallel', 'parallel'))(ddn, wd4, hid4)


def _ffn_dup(dhid8, up8, cw8, tm, ride):
    T = up8.shape[1]
    nb = T // tm
    ha = _halo_after(tm, T, HALO16)

    def body(dh_ref, dha_ref, up_ref, cw_ref, dup_ref, dcw_ref):
        i = pl.program_id(1)

        @pl.when(i == 0)
        def _():
            dcw_ref[...] = jnp.zeros_like(dcw_ref)

        dh = dh_ref[...].astype(F32)
        dup, dh1, dh2 = _conv3_t(dh, jnp.where(i < nb - 1, dha_ref[...].astype(F32), 0.0), cw_ref)
        dup_ref[...] = dup.astype(BF16)
        up = up_ref[...].astype(F32)
        dcw_ref[0:1, :] += _colsum(dh2 * up)
        dcw_ref[1:2, :] += _colsum(dh1 * up)
        dcw_ref[2:3, :] += _colsum(dh * up)

    main = pl.BlockSpec((None, tm, FF_SHARD), lambda j, i: (j, i, 0))
    return _call(body, name='ffn_dup', grid=(N_DEV, nb),
                 in_specs=[main, pl.BlockSpec((None, HALO16, FF_SHARD), lambda j, i: (j, ha(i), 0)), main,
                           pl.BlockSpec((None, 3, FF_SHARD), lambda j, i: (j, 0, 0))],
                 out_specs=[main, pl.BlockSpec((None, 8, FF_SHARD), lambda j, i: (j, 0, 0))],
                 out_shape=[_sds((N_DEV, T, FF_SHARD), BF16), _sds((N_DEV, 8, FF_SHARD))],
                 sem=('parallel', 'arbitrary'), ride=ride)(dhid8, dhid8, up8, cw8)


def _grad_tn(a, b, a_spec, b_spec, groups, m, n, tk, name, ride=None, parts=1):
    T = a.shape[-2]
    nk = T // tk
    mp = m // parts

    def body(a_ref, b_ref, *refs):
        o_refs, acc_ref = refs[:parts], refs[parts]
        k = pl.program_id(1)
        part = _dot_tn(a_ref[...], b_ref[...])

        @pl.when(k == 0)
        def _():
            acc_ref[...] = part

        @pl.when(k > 0)
        def _():
            acc_ref[...] += part

        @pl.when(k == nk - 1)
        def _():
            for p, o_ref in enumerate(o_refs):
                o_ref[...] = acc_ref[p * mp:(p + 1) * mp, :].astype(BF16)

    out_spec = pl.BlockSpec((None, mp, n), lambda g, k: (g, 0, 0))
    res = _call(body, name=name, grid=(groups, nk), in_specs=[a_spec, b_spec], out_specs=[out_spec] * parts,
                out_shape=[_sds((groups, mp, n), BF16)] * parts, scratch=[pltpu.VMEM((m, n), F32)],
                sem=('parallel', 'arbitrary'), vmem=VMEM_BIG, ride=ride)(a, b)
    if parts > 1:
        return res
    return res[0] if ride is None else (res[0][0], res[1])


def _grad_w_in(h1, dproj, tk, ride):
    T = h1.shape[0]
    nk = T // tk
    half = D_IN_PROJ // 2

    def body(a_ref, b_ref, o_ref, acc_ref):
        k = pl.program_id(0)
        for h in range(2):
            cols = slice(h * half, (h + 1) * half)
            part = _dot_tn(a_ref[...], b_ref[:, cols])

            @pl.when(k == 0)
            def _():
                acc_ref[:, cols] = part

            @pl.when(k > 0)
            def _():
                acc_ref[:, cols] += part

        @pl.when(k == nk - 1)
        def _():
            for g in range(N_DEV):
                o_ref[g] = acc_ref[:, g * IN_SHARD:(g + 1) * IN_SHARD].astype(BF16)

    return _call(body, name='grad_w_in', grid=(nk,),
                 in_specs=[pl.BlockSpec((tk, D_MODEL), lambda k: (k, 0)), pl.BlockSpec((tk, D_IN_PROJ), lambda k: (k, 0))],
                 out_specs=_const((N_DEV, D_MODEL, IN_SHARD)), out_shape=_sds((N_DEV, D_MODEL, IN_SHARD), BF16),
                 scratch=[pltpu.VMEM((D_MODEL, D_IN_PROJ), F32)], sem=('arbitrary',), vmem=VMEM_BIG, ride=ride)(h1, dproj)


def _pre_norm_bwd(dz, dz_spec, w_s, xin, dres, sc, g, tm, name, ride, below=None, group=1, w_t=False):
    T = xin.shape[0]
    n = w_s.shape[1] if w_t else w_s.shape[2]
    mul = _dot if w_t else _dot_nt
    steps = N_DEV // group

    def body(dz_ref, w_ref, x_ref, dr_ref, sc_ref, g_ref, *refs):
        if below is None:
            dx_ref, dsh_ref, dsc_ref, dg_ref = refs
            sums = (dsh_ref, dsc_ref, dg_ref)
        else:
            v_ref, gate_ref, g2_ref, dx_ref, dsh_ref, dsc_ref, dg_ref, dv_ref, dgate_ref, dg2_ref = refs
            sums = (dsh_ref, dsc_ref, dg_ref, dgate_ref, dg2_ref)
        i, j = pl.program_id(0), pl.program_id(1)
        piece = (lambda s: dz_ref[s]) if dz.ndim == 3 else (lambda s: dz_ref[:, s * n:(s + 1) * n])
        part = mul(piece(0), w_ref[0])
        for s in range(1, group):
            part = part + mul(piece(s), w_ref[s])

        @pl.when(jnp.logical_and(i == 0, j == 0))
        def _():
            for s_ref in sums:
                s_ref[...] = jnp.zeros_like(s_ref)

        @pl.when(j == 0)
        def _():
            dx_ref[...] = part

        @pl.when(j > 0)
        def _():
            dx_ref[...] += part

        @pl.when(j == steps - 1)
        def _():
            dh, xv, gv = dx_ref[...], x_ref[...], g_ref[...]
            r = _rsqrt_mean(xv)
            dsh_ref[...] += _colsum(dh)
            dsc_ref[...] += _colsum(dh * (xv * r * gv))
            dxn = dh * (1.0 + sc_ref[...])
            dg_ref[...] += _colsum(dxn * xv * r)
            dx = dr_ref[...] + _norm_bwd(dxn, xv, r, gv)
            dx_ref[...] = dx
            if below is not None:
                v, g2 = v_ref[...], g2_ref[...]
                rv = _rsqrt_mean(v)
                dgate_ref[...] += _colsum(dx * (v * rv * g2))
                dn = dx * gate_ref[...]
                dg2_ref[...] += _colsum(dn * v * rv)
                dv_ref[...] = _norm_bwd(dn, v, rv, g2).astype(BF16)

    row = pl.BlockSpec((tm, D_MODEL), lambda i, j: (i, 0))
    vec = _const((1, D_MODEL))
    in_specs = [dz_spec, pl.BlockSpec((group,) + w_s.shape[1:], lambda i, j: (j, 0, 0)), row, row, vec, vec]
    out_specs = [row, vec, vec, vec]
    out_shape = [_sds((T, D_MODEL)), _sds((1, D_MODEL)), _sds((1, D_MODEL)), _sds((1, D_MODEL))]
    args = [dz, w_s, xin, dres, sc, g]
    if below is not None:
        in_specs += [row, vec, vec]
        out_specs += [row, vec, vec]
        out_shape += [_sds((T, D_MODEL), BF16), _sds((1, D_MODEL)), _sds((1, D_MODEL))]
        args += list(below)
    return _call(body, name=name, grid=(T // tm, steps), in_specs=in_specs, out_specs=out_specs,
                 out_shape=out_shape, sem=('arbitrary', 'arbitrary'), vmem=VMEM_MOST, ride=ride)(*args)


def _mix_bwd(d_o, w_out, yssm, proj, d, glu_w, glu_b, g_ssm, cw, g_conv, avg16, avg64, tm):
    T = yssm.shape[0]
    hb = _halo_before(tm)

    def body(do_ref, wo_ref, y_ref, p_ref, ph_ref, d_ref, gw_ref, gb_ref, gs_ref, cw_ref, gc_ref, a16_ref, a64_ref,
             dy_ref, dconv_ref, dbg_ref, z_ref, dlin_ref, acc_ref):
        i = pl.program_id(0)
        dyc = _dot_nt(do_ref[...], wo_ref[...])

        @pl.when(i == 0)
        def _():
            acc_ref[...] = jnp.zeros_like(acc_ref)

        u = p_ref[:, 0:D_SSM]
        y = y_ref[...] + d_ref[...] * u
        z, t = _gelu(y)
        gate = _sigmoid(_dot(z.astype(BF16), gw_ref[...]) + gb_ref[...])
        ya = z * gate
        rs = lax.rsqrt(_dot_split(ya * ya, a16_ref[...], 2) + EPS)
        dna = dyc[:, 0:D_SSM]
        acc_ref[1:2, :] += _colsum(dna * ya * rs)
        dya = _head_norm_bwd(dna, ya, rs, gs_ref[...], a16_ref[...])
        dlin = dya * z * gate * (1.0 - gate)
        acc_ref[0:1, :] += _colsum(dlin)
        dlin_b = dlin.astype(BF16)
        dz = dya * gate + _dot_nt(dlin_b, gw_ref[...])
        dy = dz * _gelu_grad(y, t)
        acc_ref[3:4, :] += _colsum(dy * u)
        dy_ref[...] = dy
        z_ref[...] = z.astype(BF16)
        dlin_ref[...] = dlin_b

        bg = p_ref[:, D_SSM:D_SSM + D_CONV]
        cv = p_ref[:, D_SSM + D_CONV:D_SSM + 2 * D_CONV] * p_ref[:, D_SSM + 2 * D_CONV:D_IN_PROJ]
        hv = ph_ref[:, D_SSM + D_CONV:D_SSM + 2 * D_CONV] * ph_ref[:, D_SSM + 2 * D_CONV:D_IN_PROJ]
        hv = jnp.where(i > 0, hv, 0.0)
        conv, cv1, cv2 = _conv3(cv, hv, cw_ref)
        yb = bg * conv
        rsb = lax.rsqrt(_dot_split(yb * yb, a64_ref[...], 2) + EPS)
        dnb = dyc[:, D_SSM:D_MODEL]
        acc_ref[2:3, :] += _colsum(dnb * yb * rsb)
        dyb = _head_norm_bwd(dnb, yb, rsb, gc_ref[...], a64_ref[...])
        dbg_ref[...] = dyb * conv
        dconv = dyb * bg
        dconv_ref[...] = dconv
        acc_ref[4:5, :] += _colsum(dconv * cv2)
        acc_ref[5:6, :] += _colsum(dconv * cv1)
        acc_ref[6:7, :] += _colsum(dconv * cv)

    vec = _const((1, D_SSM))
    sq = _const((D_SSM, D_SSM))
    half = pl.BlockSpec((tm, D_SSM), lambda i: (i, 0))
    return _call(body, name='mix_bwd', grid=(T // tm,),
                 in_specs=[pl.BlockSpec((tm, D_MODEL), lambda i: (i, 0)), _const((D_MODEL, D_MODEL)), half,
                           pl.BlockSpec((tm, D_IN_PROJ), lambda i: (i, 0)),
                           pl.BlockSpec((HALO, D_IN_PROJ), lambda i: (hb(i), 0)), vec, sq, vec, vec,
                           _const((3, D_CONV)), vec, sq, sq],
                 out_specs=[half, half, half, half, half, _const((8, D_SSM))],
                 out_shape=[_sds((T, D_SSM)), _sds((T, D_SSM)), _sds((T, D_SSM)), _sds((T, D_SSM), BF16),
                            _sds((T, D_SSM), BF16), _sds((8, D_SSM))],
                 sem=('arbitrary',), vmem=VMEM_BIG)(d_o, w_out, yssm, proj, proj, d, glu_w, glu_b, g_ssm, cw, g_conv,
                                                   avg16, avg64)


def _mix_bwd_proj(dconv, proj, du_ssm, dy, d, dbg, cw, tm):
    T = dy.shape[0]
    nb = T // tm
    ha = _halo_after(tm, T)

    def body(dc_ref, dch_ref, cg_ref, v_ref, du_ref, dy_ref, d_ref, dbg_ref, cw_ref, o_ref):
        i = pl.program_id(0)
        dcv = _conv3_t(dc_ref[...], jnp.where(i < nb - 1, dch_ref[...], 0.0), cw_ref)[0]
        o_ref[:, 0:D_SSM] = (du_ref[...] + dy_ref[...] * d_ref[...]).astype(BF16)
        o_ref[:, D_SSM:D_SSM + D_CONV] = dbg_ref[...].astype(BF16)
        o_ref[:, D_SSM + D_CONV:D_SSM + 2 * D_CONV] = (dcv * v_ref[...]).astype(BF16)
        o_ref[:, D_SSM + 2 * D_CONV:D_IN_PROJ] = (dcv * cg_ref[...]).astype(BF16)

    half = pl.BlockSpec((tm, D_SSM), lambda i: (i, 0))
    return _call(body, name='mix_bwd_proj', grid=(nb,),
                 in_specs=[half, pl.BlockSpec((HALO, D_CONV), lambda i: (ha(i), 0)),
                           pl.BlockSpec((tm, D_CONV), lambda i: (i, 2)), pl.BlockSpec((tm, D_CONV), lambda i: (i, 3)),
                           half, half, _const((1, D_SSM)), half, _const((3, D_CONV))],
                 out_specs=pl.BlockSpec((tm, D_IN_PROJ), lambda i: (i, 0)), out_shape=_sds((T, D_IN_PROJ), BF16),
                 sem=('parallel',))(dconv, dconv, proj, proj, du_ssm, dy, d, dbg, cw)


def _row_tile(rows, cols, slots):
    for cand in (512, 256, 128, 64, 32, 16, 8):
        if rows % cand == 0 and slots * cand * cols * 4 <= (2 << 20):
            return cand
    return rows


def _adamw_math(g, w, m, v):
    m2 = ADAM_B1 * m + (1.0 - ADAM_B1) * g
    v2 = ADAM_B2 * v + (1.0 - ADAM_B2) * (g * g)
    m_hat = m2 / (1.0 - ADAM_B1 ** ADAM_STEP)
    v_hat = v2 / (1.0 - ADAM_B2 ** ADAM_STEP)
    return -ADAM_LR * (m_hat / (jnp.sqrt(v_hat) + ADAM_EPS) + ADAM_WD * w), m2, v2


def _adamw(pieces, w, m, v, name):
    slots, _, cols = pieces[0].shape
    rows = sum(p.shape[1] for p in pieces)
    tr = _row_tile(pieces[0].shape[1], cols, slots)
    starts, pos = [], 0
    for p in pieces:
        assert p.shape[1] % tr == 0
        starts.append(pos)
        pos += p.shape[1] // tr

    def body(*refs):
        g_refs = refs[:len(pieces)]
        w_ref, m_ref, v_ref, go_ref, d_ref, mo_ref, vo_ref = refs[len(pieces):]
        i = pl.program_id(0)
        g = None
        for g_ref, start in zip(g_refs, starts):
            part = g_ref[0].astype(F32)
            for s in range(1, slots):
                part = part + g_ref[s].astype(F32)
            g = part if g is None else jnp.where(i >= start, part, g)
        go_ref[...] = g
        d_ref[...], mo_ref[...], vo_ref[...] = _adamw_math(g, w_ref[...], m_ref[...], v_ref[...])

    def piece_spec(start, count):
        return pl.BlockSpec((slots, tr, cols), lambda i: (0, jnp.clip(i - start, 0, count - 1), 0))

    blk = pl.BlockSpec((tr, cols), lambda i: (i, 0))
    return _call(body, name=name, grid=(rows // tr,),
                 in_specs=[piece_spec(s, p.shape[1] // tr) for s, p in zip(starts, pieces)] + [blk, blk, blk],
                 out_specs=[blk] * 4, out_shape=[_sds((rows, cols))] * 4, sem=('parallel',))(*pieces, w, m, v)


def _to_scan_rows(a):
    T, n = a.shape
    return a.reshape(SUBLANES, T // SUBLANES, n).transpose(1, 0, 2).reshape(T, n)


def _from_scan_rows(a):
    T, n = a.shape
    return a.reshape(T // SUBLANES, SUBLANES, n).transpose(1, 0, 2).reshape(T, n)


def _expand(a):
    return jnp.repeat(a, SSM_GROUP, axis=1)


def _block_diag(rows, row_group, col_group):
    r, n = rows.shape
    tiled = jnp.tile(rows, (1, N_GROUPS))
    keep = (jnp.arange(r)[:, None] // row_group) == (jnp.arange(n * N_GROUPS)[None, :] // col_group)
    return jnp.where(keep, tiled, 0.0)


def _block_diag_b(bb):
    return _block_diag(bb.transpose(0, 2, 1).reshape(D_SSM, SSM_STATE), SSM_GROUP, SSM_STATE)


def _block_diag_c(cc):
    return _block_diag(cc.transpose(0, 2, 1).reshape(N_STATE, SSM_GROUP), SSM_STATE, SSM_GROUP)


def _diag_blocks(x, chan_major):
    per = CHAN_BLOCK // SSM_GROUP
    eye = jnp.eye(per, dtype=x.dtype)
    if chan_major:
        x = x.reshape(-1, per, SSM_GROUP, per, SSM_STATE) * eye[None, :, None, :, None]
        return x.sum(axis=1).transpose(0, 2, 3, 1).reshape(N_GROUPS, SSM_STATE, SSM_GROUP)
    x = x.reshape(-1, per, SSM_STATE, per, SSM_GROUP) * eye[None, :, None, :, None]
    return x.sum(axis=3).reshape(N_GROUPS, SSM_STATE, SSM_GROUP)


SMALL_LAYOUT = {
    'ssm_b_re': (0, 0, 32, 1024), 'ssm_b_im': (32, 0, 32, 1024), 'ssm_c_re': (64, 0, 32, 1024),
    'ssm_c_im': (96, 0, 32, 1024), 'b_ada': (128, 0, 6, 1024), 'g_pre_mix': (134, 0, 1, 1024),
    'g_post_mix': (135, 0, 1, 1024), 'ssm_lam_re': (136, 0, 2, 1024), 'ssm_lam_im': (138, 0, 2, 1024),
    'ssm_log_step': (140, 0, 1, 32), 'glu_b': (141, 0, 1, 512), 'g_out_ssm': (141, 512, 1, 512),
    'g_out_conv': (142, 0, 1, 512), 'ssm_d': (142, 512, 1, 512), 'g_pre_ffn': (143, 0, 1, 1024),
    'g_post_ffn': (144, 0, 1, 1024)}
SMALL_ROWS = 152
B_ADA_ROW = SMALL_LAYOUT['b_ada'][0]
LATE_ROWS = {('b_ada', 0): 0, ('b_ada', 1): 1, ('g_pre_mix', 0): 2}


def _adamw_small(gathered, late, wts, mom_m, mom_v):
    names = list(SMALL_LAYOUT)
    n = len(names)

    def body(*refs):
        g_ref, late_ref, ins, outs = refs[0], refs[1], refs[2:2 + 3 * n], refs[2 + 3 * n:]
        for p, name in enumerate(names):
            r0, c0, rows, cols = SMALL_LAYOUT[name]
            pieces = [(0, rows)] if rows % 8 == 0 else [(r, 1) for r in range(rows)]
            for r, cnt in pieces:
                src_ref, first = (late_ref, LATE_ROWS[name, r]) if (name, r) in LATE_ROWS else (g_ref, r0 + r)
                g = src_ref[0, first:first + cnt, c0:c0 + cols]
                for s in range(1, N_DEV):
                    g = g + src_ref[s, first:first + cnt, c0:c0 + cols]
                w, m, v = (ins[3 * p + q][r:r + cnt, :] for q in range(3))
                res = (g,) + _adamw_math(g, w, m, v)
                for q in range(4):
                    outs[4 * p + q][r:r + cnt, :] = res[q]

    shapes = [SMALL_LAYOUT[name][2:] for name in names]
    args = [gathered, late]
    for name, shp in zip(names, shapes):
        args += [wts[name].reshape(shp), mom_m[name].reshape(shp), mom_v[name].reshape(shp)]
    outs = _call(body, name='adamw_small', grid=(1,),
                 in_specs=[_const(gathered.shape), _const(late.shape)]
                 + [_const(shp) for shp in shapes for _ in range(3)],
                 out_specs=[_const(shp) for shp in shapes for _ in range(4)],
                 out_shape=[_sds(shp) for shp in shapes for _ in range(4)], vmem=VMEM_BIG)(*args)
    res = {}
    for p, name in enumerate(names):
        for q, kind in enumerate(('g', 'd', 'm', 'v')):
            res[kind, name] = outs[4 * p + q].reshape(wts[name].shape)
    return res


def kernel(x, c, w_ada, b_ada, g_pre_mix, g_post_mix, w_in, ssm_lam_re, ssm_lam_im, ssm_log_step, ssm_b_re, ssm_b_im, ssm_c_re, ssm_c_im, ssm_d, glu_w, glu_b, g_out_ssm, conv_w, g_out_conv, w_out, g_pre_ffn, g_post_ffn, w_up, ffn_conv_w, w_down, loss_target, m_w_ada, m_b_ada, m_g_pre_mix, m_g_post_mix, m_w_in, m_ssm_lam_re, m_ssm_lam_im, m_ssm_log_step, m_ssm_b_re, m_ssm_b_im, m_ssm_c_re, m_ssm_c_im, m_ssm_d, m_glu_w, m_glu_b, m_g_out_ssm, m_conv_w, m_g_out_conv, m_w_out, m_g_pre_ffn, m_g_post_ffn, m_w_up, m_ffn_conv_w, m_w_down, v_w_ada, v_b_ada, v_g_pre_mix, v_g_post_mix, v_w_in, v_ssm_lam_re, v_ssm_lam_im, v_ssm_log_step, v_ssm_b_re, v_ssm_b_im, v_ssm_c_re, v_ssm_c_im, v_ssm_d, v_glu_w, v_glu_b, v_g_out_ssm, v_conv_w, v_g_out_conv, v_w_out, v_g_pre_ffn, v_g_post_ffn, v_w_up, v_ffn_conv_w, v_w_down):
    args = dict(locals())
    wts = {n: args[n] for n in WEIGHTS}
    mom_m = {n: args['m_' + n] for n in WEIGHTS}
    mom_v = {n: args['v_' + n] for n in WEIGHTS}
    T = x.shape[1]
    tm = min(512, T)
    tw = min(1024, T)
    me = _me()[3]
    xt, tgt = x[0], loss_target[0]

    c_all, w_in_s, glu_s, w_out_s, conv_s = _exchange(
        [c, w_in[0].astype(BF16), glu_w[0].astype(BF16), w_out[0].astype(BF16), conv_w[0]], name='gather_first',
        scatter=False)
    c_all = c_all.reshape(N_DEV, D_MODEL)
    b_cols = lax.dynamic_slice(b_ada, (0, me * ADA_SHARD), (1, ADA_SHARD))
    mod_cols, c_act = _mod_cols(c_all, w_ada[0], b_cols)
    (mod_all,) = _exchange([mod_cols], name='gather_mod', scatter=False)
    mod = lax.dynamic_slice(mod_all, (0, me, 0), (N_DEV, 1, ADA_SHARD)).reshape(N_MOD, 1, D_MODEL)
    sh1, sc1, gt1, sh2, sc2, gt2 = [mod[k] for k in range(N_MOD)]

    glu_full = glu_s.reshape(D_SSM, D_SSM)
    w_out_full = w_out_s.reshape(D_MODEL, D_MODEL)
    cw_full = conv_s.transpose(1, 0, 2).reshape(3, D_CONV)

    lre_x, lim_x = _expand(ssm_lam_re[0]), _expand(ssm_lam_im[0])
    lst_x = jnp.broadcast_to(ssm_log_step[0][:, None], (N_GROUPS, SSM_STATE * SSM_GROUP))
    b_re_x = ssm_b_re[0].reshape(N_GROUPS, -1)
    b_im_x = ssm_b_im[0].reshape(N_GROUPS, -1)
    ar_x, ai_x, bbr_x, bbi_x = _ssm_prep(lre_x, lim_x, lst_x, b_re_x, b_im_x)
    lam_r = ar_x[:, ::SSM_GROUP].reshape(1, N_STATE)
    lam_i = ai_x[:, ::SSM_GROUP].reshape(1, N_STATE)
    big_b_re = _block_diag_b(bbr_x.reshape(N_GROUPS, SSM_STATE, SSM_GROUP)).astype(BF16)
    big_b_im = _block_diag_b(bbi_x.reshape(N_GROUPS, SSM_STATE, SSM_GROUP)).astype(BF16)
    big_c_re = _block_diag_c(ssm_c_re[0]).astype(BF16)
    big_c_im = _block_diag_c(ssm_c_im[0]).astype(BF16)
    head = jnp.arange(D_SSM)
    avg16 = jnp.where(head[:, None] // SSM_GROUP == head[None, :] // SSM_GROUP, 1.0 / SSM_GROUP, 0.0).astype(BF16)
    hd = D_CONV // CONV_HEADS
    avg64 = jnp.where(head[:, None] // hd == head[None, :] // hd, 1.0 / hd, 0.0).astype(BF16)

    (proj, h1), (w_down_s, ffn_conv_s) = _pre_mix(xt, sc1, sh1, g_pre_mix, w_in_s, tw,
                                                  ([w_down[0].astype(BF16), ffn_conv_w[0]], False))
    wd4 = w_down_s.reshape(4, FF_SHARD, D_MODEL)
    u_perm = _to_scan_rows(proj[:, :D_SSM])
    (s_re, s_im, y_perm), (w_up_s,) = _ssm_fwd(u_perm, big_b_re, big_b_im, big_c_re, big_c_im, lam_r, lam_i,
                                               ([w_up[0].T.astype(BF16)], False))
    yssm = _from_scan_rows(y_perm)
    mix_args = (ssm_d, glu_full, glu_b, g_out_ssm, cw_full, g_out_conv, avg16, avg64)
    ycat = _mix_fwd(yssm, proj, *mix_args, tm)
    o, x1, h2 = _out_proj(ycat, w_out_full, xt, gt1, g_post_mix, g_pre_ffn, sc2, sh2, tm)
    up8, hid8 = _ffn_up(h2, w_up_s, ffn_conv_s, tw)
    hid4 = hid8.reshape(2, 4, T, FF_SHARD)
    ddn, dx2, loss_parts, d_gt2, d_g_post_ffn = _ffn_down(hid4, wd4, x1, tgt, gt2, g_post_ffn, tm)
    loss_local = jnp.sum(loss_parts[:, 0, 0])

    got = {}
    dhid, act = _ffn_dact(ddn, wd4, hid4, tm)
    g_w_down = _grad_tn(act, ddn, pl.BlockSpec((None, tw, FF_SHARD), lambda g, k: (g, k, 0)),
                        pl.BlockSpec((tw, D_MODEL), lambda g, k: (k, 0)), 4, FF_SHARD, D_MODEL, tw, 'grad_w_down')
    (dup8, dcw_ffn), (got['w_down'],) = _ffn_dup(dhid.reshape(N_DEV, T, FF_SHARD), up8, ffn_conv_s, tm,
                                                 ([g_w_down.reshape(N_DEV, D_FF // N_DEV, D_MODEL)], True))
    g_w_up_halves = _grad_tn(dup8, h2, pl.BlockSpec((None, tw, FF_SHARD), lambda g, k: (g, k, 0)),
                             pl.BlockSpec((tw, D_MODEL), lambda g, k: (k, 0)), N_DEV, FF_SHARD, D_MODEL, tw,
                             'grad_w_up', parts=2)
    (dx1, d_sh2, d_sc2, d_g_pre_ffn, d_o, d_gt1, d_g_post_mix), (got_up_0, got['ffn_conv_w']) = _pre_norm_bwd(
        dup8, pl.BlockSpec((2, tw, FF_SHARD), lambda i, j: (j, i, 0)), w_up_s, x1, dx2, sc2, g_pre_ffn, tw,
        'ffn_in_bwd', ([g_w_up_halves[0], dcw_ffn], True), below=(o, gt1, g_post_mix), group=2, w_t=True)

    g_w_out = _grad_tn(ycat, d_o, pl.BlockSpec((tw, D_MODEL), lambda g, k: (k, 0)),
                       pl.BlockSpec((tw, D_MODEL), lambda g, k: (k, 0)), 1, D_MODEL, D_MODEL, tw, 'grad_w_out')
    dy, dconv, dbg, z_b, dlin_b, sums = _mix_bwd(d_o, w_out_full, yssm, proj, *mix_args, tm)
    g_glu_w = _grad_tn(z_b, dlin_b, pl.BlockSpec((tw, D_SSM), lambda g, k: (k, 0)),
                       pl.BlockSpec((tw, D_SSM), lambda g, k: (k, 0)), 1, D_SSM, D_SSM, tw, 'grad_glu_w')
    dy_perm = _to_scan_rows(dy)
    (du_perm, dbr_blk, dbi_blk, dcr_blk, dci_blk, dar_blk, dai_blk), (got_up_1, got['w_out'], got['glu_w']) = _ssm_bwd(
        dy_perm, u_perm, s_re, s_im, big_b_re, big_b_im, big_c_re, big_c_im, lam_r, lam_i,
        ([g_w_up_halves[1], g_w_out.reshape(N_DEV, D_MODEL // N_DEV, D_MODEL),
          g_glu_w.reshape(N_DEV, D_SSM // N_DEV, D_SSM)], True))
    du_ssm = _from_scan_rows(du_perm)
    dproj = _mix_bwd_proj(dconv, proj, du_ssm, dy, ssm_d, dbg, cw_full, tm)
    dbb_re = _diag_blocks(dbr_blk, True).reshape(N_GROUPS, -1)
    dbb_im = _diag_blocks(dbi_blk, True).reshape(N_GROUPS, -1)
    d_c_re = _diag_blocks(dcr_blk, False).transpose(0, 2, 1)
    d_c_im = _diag_blocks(dci_blk, False).transpose(0, 2, 1)
    lane = jnp.arange(SSM_STATE * SSM_GROUP)
    seg = jnp.where(lane[:, None] // SSM_GROUP == lane[None, :] // SSM_GROUP, 1.0, 0.0).astype(BF16)
    d_b_re_x, d_b_im_x, d_lre_x, d_lim_x, d_lst = _ssm_prep_bwd(
        lre_x, lim_x, lst_x, b_re_x, b_im_x, dbb_re, dbb_im, _expand(dar_blk.reshape(N_GROUPS, SSM_STATE)),
        _expand(dai_blk.reshape(N_GROUPS, SSM_STATE)), seg)

    row = lambda a: a.reshape(-1, PACK_COLS)
    blank = jnp.zeros((1, PACK_COLS), F32)
    small_pack = jnp.concatenate([
        d_b_re_x, d_b_im_x, row(d_c_re), row(d_c_im), blank, blank, d_gt1, d_sh2, d_sc2, d_gt2, blank,
        d_g_post_mix, row(d_lre_x[:, ::SSM_GROUP]), row(d_lim_x[:, ::SSM_GROUP]),
        jnp.pad(d_lst.reshape(1, N_GROUPS), ((0, 0), (0, PACK_COLS - N_GROUPS))), row(sums[0:4]), d_g_pre_ffn,
        d_g_post_ffn, jnp.zeros((SMALL_ROWS - 145, PACK_COLS), F32)])
    g_w_in, (small_all,) = _grad_w_in(h1, dproj, tw, ([small_pack], False))
    g_conv_slots = jnp.concatenate([sums[4:7], jnp.zeros((5, D_CONV), F32)]).reshape(
        8, N_DEV, D_CONV // N_DEV).transpose(1, 0, 2)
    (grad_x, d_sh1, d_sc1, d_g_pre_mix), (got['w_in'], got['conv_w']) = _pre_norm_bwd(
        dproj, pl.BlockSpec((tw, 4 * IN_SHARD), lambda i, j: (i, j)), w_in_s, xt, dx1, sc1, g_pre_mix, tw,
        'mix_in_bwd', ([g_w_in, g_conv_slots], True), group=4)
    late_pack = jnp.concatenate([d_sh1, d_sc1, d_g_pre_mix, jnp.full((1, PACK_COLS), loss_local, F32),
                                 jnp.zeros((4, PACK_COLS), F32)])
    (late_all,) = _exchange([late_pack], name='gather_late_grads', scatter=False)
    loss = jnp.sum(late_all[:, 3, 0])
    res = _adamw_small(small_all, late_all, wts, mom_m, mom_v)

    dmod_all = jnp.concatenate([late_all[:, 0:2, :], small_all[:, B_ADA_ROW + 2:B_ADA_ROW + N_MOD, :]],
                               axis=1).reshape(N_DEV, N_MOD * D_MODEL)
    dmod_cols = lax.dynamic_slice(dmod_all, (0, me * ADA_SHARD), (N_DEV, ADA_SHARD))
    g_w_ada = _grad_w_ada(c_act.T, dmod_cols)

    pieces = {n: [slots[:, :3, :] if n in ('conv_w', 'ffn_conv_w') else slots] for n, slots in got.items()}
    for n, parts in pieces.items():
        outs = _adamw(parts, wts[n][0], mom_m[n][0], mom_v[n][0], 'adamw_' + n)
        for kind, val in zip(('g', 'd', 'm', 'v'), outs):
            res[kind, n] = val[None]
    outs = _adamw([got_up_0, got_up_1], w_up[0].T, m_w_up[0].T, v_w_up[0].T, 'adamw_w_up')
    for kind, val in zip(('g', 'd', 'm', 'v'), outs):
        res[kind, 'w_up'] = val.T[None]
    outs = _adamw([g_w_ada[None]], w_ada[0], m_w_ada[0], v_w_ada[0], 'adamw_w_ada')
    for kind, val in zip(('g', 'd', 'm', 'v'), outs):
        res[kind, 'w_ada'] = val[None]

    return (loss, grad_x[None], *[res['g', n] for n in WEIGHTS], *[res['d', n] for n in WEIGHTS],
            *[res['m', n] for n in WEIGHTS], *[res['v', n] for n in WEIGHTS])
```

```python
import math

import jax
import jax.numpy as jnp
from jax import lax
from jax.experimental import pallas as pl
from jax.experimental.pallas import tpu as pltpu

F32, BF16 = jnp.float32, jnp.bfloat16

D_MODEL = 1024
D_SSM = 512
D_CONV = 512
SSM_GROUP = 16
N_GROUPS = 32
SSM_STATE = 64
N_STATE = N_GROUPS * SSM_STATE
CONV_HEADS = 8
D_FF = 2816
N_MOD = 6
D_IN_PROJ = D_SSM + 3 * D_CONV
N_DEV = 8
FF_SHARD = 2 * D_FF // N_DEV
IN_SHARD = D_IN_PROJ // N_DEV
ADA_SHARD = N_MOD * D_MODEL // N_DEV
EPS = 1e-6
LAMBDA_RE_MAX = -1e-4
ADAM_LR, ADAM_B1, ADAM_B2, ADAM_EPS, ADAM_WD, ADAM_STEP = 0.001, 0.9, 0.999, 1e-08, 0.01, 10
GELU_C = math.sqrt(2.0 / math.pi)
GELU_A = 0.044715

SUBLANES = 8
HALO = 8
HALO16 = 16
SCAN_UNROLL = 8
STATE_BLOCK = 512
CHAN_BLOCK = 128
VMEM_BIG = 48 << 20
VMEM_MOST = 58 << 20

WEIGHTS = ['w_ada', 'b_ada', 'g_pre_mix', 'g_post_mix', 'w_in', 'ssm_lam_re', 'ssm_lam_im', 'ssm_log_step',
           'ssm_b_re', 'ssm_b_im', 'ssm_c_re', 'ssm_c_im', 'ssm_d', 'glu_w', 'glu_b', 'g_out_ssm', 'conv_w',
           'g_out_conv', 'w_out', 'g_pre_ffn', 'g_post_ffn', 'w_up', 'ffn_conv_w', 'w_down']
SHARDED = ('w_ada', 'w_in', 'glu_w', 'conv_w', 'w_out', 'w_up', 'ffn_conv_w', 'w_down')
PACK_COLS = 1024


def _call(body, *, name, grid, in_specs, out_specs, out_shape, scratch=(), sem=None, vmem=None, ride=None):
    params = {}
    if vmem is not None:
        params['vmem_limit_bytes'] = vmem
    if ride is None:
        if sem is not None:
            params['dimension_semantics'] = sem
        return pl.pallas_call(body, name=name, grid=grid, in_specs=in_specs, out_specs=out_specs,
                              out_shape=out_shape, scratch_shapes=list(scratch),
                              compiler_params=pltpu.CompilerParams(**params))
    arrs, scatter = ride
    single = not isinstance(out_shape, (list, tuple))
    out_shape_l = [out_shape] if single else list(out_shape)
    out_specs_l = [out_specs] if single else list(out_specs)
    n, n_in, n_out, n_scr = len(arrs), len(in_specs), len(out_shape_l), len(scratch)
    any_spec = pl.BlockSpec(memory_space=pl.ANY)
    params['dimension_semantics'] = ('arbitrary',) * len(grid)

    def carried(*refs):
        ins, rin = refs[:n_in], refs[n_in:n_in + n]
        outs, rout = refs[n_in + n:n_in + n + n_out], refs[n_in + n + n_out:n_in + 2 * n + n_out]
        scr, sems = refs[n_in + 2 * n + n_out:n_in + 2 * n + n_out + n_scr], refs[n_in + 2 * n + n_out + n_scr:]
        first = pl.program_id(0) == 0
        last = pl.program_id(0) == grid[0] - 1
        for ax in range(1, len(grid)):
            first = jnp.logical_and(first, pl.program_id(ax) == 0)
            last = jnp.logical_and(last, pl.program_id(ax) == grid[ax] - 1)

        @pl.when(first)
        def _():
            _exchange_start(rin, rout, sems, scatter)

        body(*ins, *outs, *scr)

        @pl.when(last)
        def _():
            _exchange_wait(rin, rout, sems, scatter)

    call = pl.pallas_call(carried, name=name, grid=grid, in_specs=list(in_specs) + [any_spec] * n,
                          out_specs=out_specs_l + [any_spec] * n,
                          out_shape=out_shape_l + _exchange_shapes(arrs, scatter),
                          scratch_shapes=list(scratch) + _exchange_sems(n),
                          compiler_params=pltpu.CompilerParams(**params))

    def run(*args):
        res = call(*args, *arrs)
        own = res[0] if single else list(res[:n_out])
        return own, list(res[n_out:])

    return run


def _const(shape):
    nd = len(shape)
    return pl.BlockSpec(shape, lambda *_: (0,) * nd)


def _sds(shape, dtype=F32):
    return jax.ShapeDtypeStruct(shape, dtype)


def _dot(a, b):
    return jnp.dot(a, b, preferred_element_type=F32)


def _dot_nt(a, b):
    return lax.dot_general(a, b, (((1,), (1,)), ((), ())), preferred_element_type=F32)


def _dot_tn(a, b):
    return lax.dot_general(a, b, (((0,), (0,)), ((), ())), preferred_element_type=F32)


def _dot_split(x, mat, parts):
    acc = None
    rem = x
    for _ in range(parts):
        piece = rem.astype(BF16)
        rem = rem - piece.astype(F32)
        term = _dot(piece, mat)
        acc = term if acc is None else acc + term
    return acc


def _sigmoid(x):
    return 1.0 / (1.0 + jnp.exp(-x))


def _gelu(x):
    t = jnp.tanh(GELU_C * (x + GELU_A * x * x * x))
    return 0.5 * x * (1.0 + t), t


def _gelu_grad(x, t):
    return 0.5 * (1.0 + t) + 0.5 * x * (1.0 - t * t) * GELU_C * (1.0 + 3.0 * GELU_A * x * x)


def _rsqrt_mean(x):
    return lax.rsqrt(jnp.mean(x * x, axis=-1, keepdims=True) + EPS)


def _colsum(x):
    return jnp.sum(x, axis=0, keepdims=True)


def _shifts_down(x, halo):
    ext = jnp.concatenate([halo, x], axis=0)
    return pltpu.roll(ext, 1, 0)[halo.shape[0]:], pltpu.roll(ext, 2, 0)[halo.shape[0]:]


def _shifts_up(x, halo):
    n = x.shape[0]
    ext = jnp.concatenate([x, halo], axis=0)
    total = ext.shape[0]
    return pltpu.roll(ext, total - 1, 0)[:n], pltpu.roll(ext, total - 2, 0)[:n]


def _conv3(x, halo, w_ref):
    x1, x2 = _shifts_down(x, halo)
    return w_ref[0:1, :] * x2 + w_ref[1:2, :] * x1 + w_ref[2:3, :] * x, x1, x2


def _conv3_t(g, halo, w_ref):
    g1, g2 = _shifts_up(g, halo)
    return w_ref[2:3, :] * g + w_ref[1:2, :] * g1 + w_ref[0:1, :] * g2, g1, g2


def _silu_parts(x):
    s = _sigmoid(x)
    return x * s, s * (1.0 + x * (1.0 - s))


def _norm_bwd(dn, x, r, g):
    gd = g * dn
    return r * gd - x * (r * r * r) * jnp.mean(gd * x, axis=-1, keepdims=True)


def _head_norm_bwd(dn, y, rs, g, avg):
    gd = g * dn
    return rs * gd - y * (rs * rs * rs) * _dot_split(gd * y, avg, 2)


def _me():
    x, y, c = lax.axis_index('x'), lax.axis_index('y'), lax.axis_index('c')
    return x, y, c, 4 * x + 2 * y + c


def _peer(k):
    x, y, c, _ = _me()
    px = 1 - x if k & 4 else x
    py = 1 - y if k & 2 else y
    pc = 1 - c if k & 1 else c
    return (px, py, pc), 4 * px + 2 * py + pc


SIBLING = 1
OTHER_CHIPS = (2, 4, 6)


def _remote(src, dst, sems, a, k, dev):
    return pltpu.make_async_remote_copy(src_ref=src, dst_ref=dst, send_sem=sems[0].at[a, k - 1],
                                        recv_sem=sems[1].at[a, k - 1], device_id=dev,
                                        device_id_type=pl.DeviceIdType.MESH)


def _exchange_copies(ins, outs, sems, scatter):
    me = _me()[3]
    local, first, relay, arrivals = [], [], [], []
    for a in range(len(ins)):
        src = ins[a].at[me] if scatter else ins[a]
        local.append(pltpu.make_async_copy(src, outs[a].at[me], sems[2].at[a]))
        for k in range(1, N_DEV):
            dev, idx = _peer(k)
            landed = _remote(src, outs[a].at[idx], sems, a, k, dev)
            if scatter:
                first.append(_remote(ins[a].at[idx], outs[a].at[me], sems, a, k, dev))
                arrivals.append(landed)
            elif k == SIBLING:
                first.append(_remote(src, outs[a].at[me], sems, a, k, dev))
                arrivals.append(landed)
            elif k in OTHER_CHIPS:
                first.append(_remote(src, outs[a].at[me], sems, a, k, dev))
                sib, _ = _peer(SIBLING)
                relay.append((landed, _remote(outs[a].at[idx], outs[a].at[idx], sems, a, k | SIBLING, sib)))
            else:
                arrivals.append(landed)
    return local, first, relay, arrivals


def _exchange_start(ins, outs, sems, scatter):
    local, first, _, _ = _exchange_copies(ins, outs, sems, scatter)
    for cp in local + first:
        cp.start()


def _exchange_wait(ins, outs, sems, scatter):
    local, first, relay, arrivals = _exchange_copies(ins, outs, sems, scatter)
    for landed, forward in relay:
        landed.wait_recv()
        forward.start()
    for cp in arrivals:
        cp.wait_recv()
    for cp in first + [forward for _, forward in relay]:
        cp.wait_send()
    for cp in local:
        cp.wait()


def _exchange_shapes(arrs, scatter):
    return [_sds(a.shape if scatter else (N_DEV,) + a.shape, a.dtype) for a in arrs]


def _exchange_sems(n):
    return [pltpu.SemaphoreType.DMA((n, N_DEV - 1)), pltpu.SemaphoreType.DMA((n, N_DEV - 1)),
            pltpu.SemaphoreType.DMA((n,))]


def _exchange(arrs, *, name, scatter):
    n = len(arrs)

    def body(*refs):
        _exchange_start(refs[:n], refs[n:2 * n], refs[2 * n:], scatter)
        _exchange_wait(refs[:n], refs[n:2 * n], refs[2 * n:], scatter)

    any_spec = pl.BlockSpec(memory_space=pl.ANY)
    outs = pl.pallas_call(body, name=name, out_shape=_exchange_shapes(arrs, scatter), in_specs=[any_spec] * n,
                          out_specs=[any_spec] * n, scratch_shapes=_exchange_sems(n))(*arrs)
    return list(outs)


def _mod_cols(c_all, w_ada, b_cols):
    def body(c_ref, w_ref, b_ref, mod_ref, act_ref):
        c = c_ref[...]
        act = c * _sigmoid(c)
        act_ref[...] = act
        mod_ref[...] = _dot(act.astype(BF16), w_ref[...].astype(BF16)) + b_ref[...]

    return _call(body, name='mod_cols', grid=(1,),
                 in_specs=[_const(c_all.shape), _const(w_ada.shape), _const(b_cols.shape)],
                 out_specs=[_const((N_DEV, ADA_SHARD)), _const(c_all.shape)],
                 out_shape=[_sds((N_DEV, ADA_SHARD)), _sds(c_all.shape)], vmem=VMEM_BIG)(c_all, w_ada, b_cols)


def _grad_w_ada(act_t, dmod_cols):
    def body(a_ref, d_ref, o_ref):
        o_ref[...] = _dot(a_ref[...], d_ref[...])

    return _call(body, name='grad_w_ada', grid=(1,), in_specs=[_const(act_t.shape), _const(dmod_cols.shape)],
                 out_specs=_const((D_MODEL, ADA_SHARD)), out_shape=_sds((D_MODEL, ADA_SHARD)),
                 vmem=VMEM_BIG)(act_t, dmod_cols)


def _pre_mix(x, sc, sh, g, w_s, tm, ride):
    T = x.shape[0]

    def body(x_ref, sc_ref, sh_ref, g_ref, w_ref, proj_ref, h_ref):
        @pl.when(pl.program_id(1) == 0)
        def _():
            xv = x_ref[...]
            h_ref[...] = ((xv * _rsqrt_mean(xv) * g_ref[...]) * (1.0 + sc_ref[...]) + sh_ref[...]).astype(BF16)

        for s in range(2):
            proj_ref[:, s * IN_SHARD:(s + 1) * IN_SHARD] = _dot(h_ref[...], w_ref[s])

    row = pl.BlockSpec((tm, D_MODEL), lambda i, j: (i, 0))
    vec = _const((1, D_MODEL))
    return _call(body, name='pre_mix', grid=(T // tm, N_DEV // 2),
                 in_specs=[row, vec, vec, vec, pl.BlockSpec((2, D_MODEL, IN_SHARD), lambda i, j: (j, 0, 0))],
                 out_specs=[pl.BlockSpec((tm, 2 * IN_SHARD), lambda i, j: (i, j)), row],
                 out_shape=[_sds((T, D_IN_PROJ)), _sds((T, D_MODEL), BF16)],
                 sem=('parallel', 'arbitrary'), ride=ride)(x, sc, sh, g, w_s)


def _halo_before(tm, rows=HALO):
    return lambda i: jnp.maximum(i * (tm // rows) - 1, 0)


def _halo_after(tm, T, rows=HALO):
    return lambda i: jnp.minimum((i + 1) * (tm // rows), T // rows - 1)


def _mix_fwd(yssm, proj, d, glu_w, glu_b, g_ssm, cw, g_conv, avg16, avg64, tm):
    T = yssm.shape[0]
    hb = _halo_before(tm)

    def body(y_ref, p_ref, ph_ref, d_ref, gw_ref, gb_ref, gs_ref, cw_ref, gc_ref, a16_ref, a64_ref, o_ref):
        i = pl.program_id(0)
        u = p_ref[:, 0:D_SSM]
        y = y_ref[...] + d_ref[...] * u
        z, _ = _gelu(y)
        gate = _sigmoid(_dot(z.astype(BF16), gw_ref[...]) + gb_ref[...])
        ya = z * gate
        rs = lax.rsqrt(_dot_split(ya * ya, a16_ref[...], 2) + EPS)
        o_ref[:, 0:D_SSM] = (ya * rs * gs_ref[...]).astype(BF16)
        bg = p_ref[:, D_SSM:D_SSM + D_CONV]
        cv = p_ref[:, D_SSM + D_CONV:D_SSM + 2 * D_CONV] * p_ref[:, D_SSM + 2 * D_CONV:D_IN_PROJ]
        hv = ph_ref[:, D_SSM + D_CONV:D_SSM + 2 * D_CONV] * ph_ref[:, D_SSM + 2 * D_CONV:D_IN_PROJ]
        hv = jnp.where(i > 0, hv, 0.0)
        conv, _, _ = _conv3(cv, hv, cw_ref)
        yb = bg * conv
        rsb = lax.rsqrt(_dot_split(yb * yb, a64_ref[...], 2) + EPS)
        o_ref[:, D_SSM:D_MODEL] = (yb * rsb * gc_ref[...]).astype(BF16)

    vec = _const((1, D_SSM))
    sq = _const((D_SSM, D_SSM))
    return _call(body, name='mix_fwd', grid=(T // tm,),
                 in_specs=[pl.BlockSpec((tm, D_SSM), lambda i: (i, 0)), pl.BlockSpec((tm, D_IN_PROJ), lambda i: (i, 0)),
                           pl.BlockSpec((HALO, D_IN_PROJ), lambda i: (hb(i), 0)), vec, sq, vec, vec,
                           _const((3, D_CONV)), vec, sq, sq],
                 out_specs=pl.BlockSpec((tm, D_MODEL), lambda i: (i, 0)), out_shape=_sds((T, D_MODEL), BF16),
                 sem=('parallel',), vmem=VMEM_BIG)(yssm, proj, proj, d, glu_w, glu_b, g_ssm, cw, g_conv, avg16, avg64)


def _out_proj(ycat, w_out, x, gt, g_post, g_pre, sc, sh, tm):
    T = x.shape[0]

    def body(y_ref, w_ref, x_ref, gt_ref, gp_ref, g2_ref, sc_ref, sh_ref, o_ref, x1_ref, h_ref):
        o = _dot(y_ref[...], w_ref[...])
        o_ref[...] = o
        x1 = x_ref[...] + gt_ref[...] * (o * _rsqrt_mean(o) * gp_ref[...])
        x1_ref[...] = x1
        h_ref[...] = ((x1 * _rsqrt_mean(x1) * g2_ref[...]) * (1.0 + sc_ref[...]) + sh_ref[...]).astype(BF16)

    row = pl.BlockSpec((tm, D_MODEL), lambda i: (i, 0))
    vec = _const((1, D_MODEL))
    return _call(body, name='out_proj', grid=(T // tm,),
                 in_specs=[row, _const((D_MODEL, D_MODEL)), row, vec, vec, vec, vec, vec],
                 out_specs=[row, row, row],
                 out_shape=[_sds((T, D_MODEL)), _sds((T, D_MODEL)), _sds((T, D_MODEL), BF16)],
                 sem=('parallel',), vmem=VMEM_BIG)(ycat, w_out, x, gt, g_post, g_pre, sc, sh)


def _ffn_up(h2, w_s, cw8, tm):
    T = h2.shape[0]
    hb = _halo_before(tm, HALO16)

    def body(h_ref, hh_ref, w_ref, cw_ref, up_ref, hid_ref):
        up = _dot_nt(h_ref[...], w_ref[...])
        up_ref[...] = up.astype(BF16)
        before = jnp.where(pl.program_id(0) > 0, _dot_nt(hh_ref[...], w_ref[...]), 0.0)
        hid_ref[...] = _conv3(up, before, cw_ref)[0].astype(BF16)

    out = pl.BlockSpec((None, tm, FF_SHARD), lambda i, j: (j, i, 0))
    return _call(body, name='ffn_up', grid=(T // tm, N_DEV),
                 in_specs=[pl.BlockSpec((tm, D_MODEL), lambda i, j: (i, 0)),
                           pl.BlockSpec((HALO16, D_MODEL), lambda i, j: (hb(i), 0)),
                           pl.BlockSpec((None, FF_SHARD, D_MODEL), lambda i, j: (j, 0, 0)),
                           pl.BlockSpec((None, 3, FF_SHARD), lambda i, j: (j, 0, 0))],
                 out_specs=[out, out], out_shape=[_sds((N_DEV, T, FF_SHARD), BF16)] * 2,
                 sem=('parallel', 'parallel'))(h2, h2, w_s, cw8)


def _ffn_down(hid4, wd4, x1, tgt, gt, g_post, tm):
    T = x1.shape[0]
    nb = T // tm

    def body(a_ref, w_ref, x1_ref, t_ref, gt_ref, g_ref, ddn_ref, dx_ref, loss_ref, dgt_ref, dg_ref, dn_ref):
        i, j = pl.program_id(0), pl.program_id(1)
        part = None
        for s in range(2):
            act = (_silu_parts(a_ref[0, s].astype(F32))[0] * a_ref[1, s].astype(F32)).astype(BF16)
            term = _dot(act, w_ref[s])
            part = term if part is None else part + term

        @pl.when(jnp.logical_and(i == 0, j == 0))
        def _():
            dgt_ref[...] = jnp.zeros_like(dgt_ref)
            dg_ref[...] = jnp.zeros_like(dg_ref)

        @pl.when(j == 0)
        def _():
            dn_ref[...] = part

        @pl.when(j > 0)
        def _():
            dn_ref[...] += part

        @pl.when(j == 1)
        def _():
            dn, gv, gate = dn_ref[...], g_ref[...], gt_ref[...]
            r = _rsqrt_mean(dn)
            normed = dn * r * gv
            err = x1_ref[...] + gate * normed - t_ref[...]
            dx = err * (1.0 / D_MODEL)
            dx_ref[...] = dx
            tot = jnp.sum(jnp.sum(err * err, axis=1, keepdims=True), axis=0, keepdims=True) * (0.5 / D_MODEL)
            loss_ref[...] = jnp.broadcast_to(tot, (8, 128))
            dgt_ref[...] += _colsum(dx * normed)
            dnn = dx * gate
            dg_ref[...] += _colsum(dnn * dn * r)
            ddn_ref[...] = _norm_bwd(dnn, dn, r, gv).astype(BF16)

    row = pl.BlockSpec((tm, D_MODEL), lambda i, j: (i, 0))
    vec = _const((1, D_MODEL))
    return _call(body, name='ffn_down', grid=(nb, 2),
                 in_specs=[pl.BlockSpec((2, 2, tm, FF_SHARD), lambda i, j: (0, j, i, 0)),
                           pl.BlockSpec((2, FF_SHARD, D_MODEL), lambda i, j: (j, 0, 0)), row, row, vec, vec],
                 out_specs=[row, row, pl.BlockSpec((None, 8, 128), lambda i, j: (i, 0, 0)), vec, vec],
                 out_shape=[_sds((T, D_MODEL), BF16), _sds((T, D_MODEL)), _sds((nb, 8, 128)), _sds((1, D_MODEL)),
                            _sds((1, D_MODEL))],
                 scratch=[pltpu.VMEM((tm, D_MODEL), F32)], sem=('arbitrary', 'arbitrary'),
                 vmem=VMEM_BIG)(hid4, wd4, x1, tgt, gt, g_post)


def _ssm_prep(lre, lim, lst, b_re, b_im):
    def body(lre_ref, lim_ref, lst_ref, br_ref, bi_ref, ar_ref, ai_ref, bbr_ref, bbi_ref):
        ar, ai, qr, qi = _zoh(lre_ref[...], lim_ref[...], lst_ref[...])[:4]
        ar_ref[...] = ar
        ai_ref[...] = ai
        bbr_ref[...] = qr * br_ref[...] - qi * bi_ref[...]
        bbi_ref[...] = qr * bi_ref[...] + qi * br_ref[...]

    shp = lre.shape
    return _call(body, name='ssm_prep', grid=(1,), in_specs=[_const(shp)] * 5, out_specs=[_const(shp)] * 4,
                 out_shape=[_sds(shp)] * 4)(lre, lim, lst, b_re, b_im)


def _zoh(lre, lim, lst):
    lr = jnp.minimum(lre, LAMBDA_RE_MAX)
    st = jnp.exp(lst)
    mag = jnp.exp(lr * st)
    ar = mag * jnp.cos(lim * st)
    ai = mag * jnp.sin(lim * st)
    den = lr * lr + lim * lim
    qr = ((ar - 1.0) * lr + ai * lim) / den
    qi = (ai * lr - (ar - 1.0) * lim) / den
    return ar, ai, qr, qi, lr, st, den


def _ssm_prep_bwd(lre, lim, lst, b_re, b_im, dbbr, dbbi, dar, dai, seg):
    def body(lre_ref, lim_ref, lst_ref, br_ref, bi_ref, dbbr_ref, dbbi_ref, dar_ref, dai_ref, seg_ref,
             dbr_ref, dbi_ref, dlre_ref, dlim_ref, dlst_ref):
        lre_v = lre_ref[...]
        li = lim_ref[...]
        ar, ai, qr, qi, lr, st, den = _zoh(lre_v, li, lst_ref[...])
        br, bi, gbr, gbi = br_ref[...], bi_ref[...], dbbr_ref[...], dbbi_ref[...]
        dbr_ref[...] = qr * gbr + qi * gbi
        dbi_ref[...] = qr * gbi - qi * gbr
        gqr = _dot_split(br * gbr + bi * gbi, seg_ref[...], 3)
        gqi = _dot_split(br * gbi - bi * gbr, seg_ref[...], 3)
        ir, ii = lr / den, -li / den
        gar = dar_ref[...] + ir * gqr + ii * gqi
        gai = dai_ref[...] + ir * gqi - ii * gqr
        tr, ti = qr * ir - qi * ii, qr * ii + qi * ir
        glr = -(tr * gqr + ti * gqi)
        gli = -(tr * gqi - ti * gqr)
        gzr = ar * gar + ai * gai
        gzi = ar * gai - ai * gar
        glr = glr + st * gzr
        gli = gli + st * gzi
        gst = (lr * gzr + li * gzi) * st
        dlre_ref[...] = jnp.where(lre_v < LAMBDA_RE_MAX, glr, 0.0)
        dlim_ref[...] = gli
        dlst_ref[...] = jnp.sum(gst, axis=1, keepdims=True) * (1.0 / SSM_GROUP)

    shp = lre.shape
    return _call(body, name='ssm_prep_bwd', grid=(1,), in_specs=[_const(shp)] * 9 + [_const(seg.shape)],
                 out_specs=[_const(shp)] * 4 + [_const((N_GROUPS, 1))],
                 out_shape=[_sds(shp)] * 4 + [_sds((N_GROUPS, 1))], vmem=VMEM_BIG)(
                     lre, lim, lst, b_re, b_im, dbbr, dbbi, dar, dai, seg)


def _scan_specs(T):
    return dict(
        chan=pl.BlockSpec((T, CHAN_BLOCK), lambda cb: (0, cb)),
        state=pl.BlockSpec((T, STATE_BLOCK), lambda cb: (0, cb)),
        b=pl.BlockSpec((CHAN_BLOCK, STATE_BLOCK), lambda cb: (cb, cb)),
        c=pl.BlockSpec((STATE_BLOCK, CHAN_BLOCK), lambda cb: (cb, cb)),
        lam=pl.BlockSpec((1, STATE_BLOCK), lambda cb: (0, cb)),
    )


def _complex_power(re, im, n):
    out = None
    while True:
        if n & 1:
            out = (re, im) if out is None else (out[0] * re - out[1] * im, out[0] * im + out[1] * re)
        n >>= 1
        if n == 0:
            return out
        re, im = re * re - im * im, 2.0 * re * im


def _rows8(i):
    if isinstance(i, int):
        return pl.ds(i * SUBLANES, SUBLANES)
    return pl.ds(pl.multiple_of(i * SUBLANES, SUBLANES), SUBLANES)


def _scan_loop(n_steps, body, init):
    trips = n_steps // SCAN_UNROLL

    def trip(t, carry):
        for u in range(SCAN_UNROLL):
            carry = body(t * SCAN_UNROLL + u, carry)
        return carry

    carry = lax.fori_loop(0, trips, trip, init)
    for step in range(trips * SCAN_UNROLL, n_steps):
        carry = body(step, carry)
    return carry


def _ssm_fwd(u_perm, b_re, b_im, c_re, c_im, lam_r, lam_i, ride):
    T = u_perm.shape[0]
    ls = T // SUBLANES
    rc = min(512, T)
    sp = _scan_specs(T)

    def body(u_ref, bre_ref, bim_ref, cre_ref, cim_ref, lr_ref, li_ref, so_re_ref, so_im_ref, y_ref, sre_ref, sim_ref):
        for c in range(T // rc):
            rows = pl.ds(c * rc, rc)
            ub = u_ref[rows, :].astype(BF16)
            sre_ref[rows, :] = _dot(ub, bre_ref[...])
            sim_ref[rows, :] = _dot(ub, bim_ref[...])
        shp = (SUBLANES, STATE_BLOCK)
        lr = jnp.broadcast_to(lr_ref[...], shp)
        li = jnp.broadcast_to(li_ref[...], shp)
        zero = jnp.zeros(shp, F32)

        def step(i, carry):
            sr, si = carry
            rows = _rows8(i)
            nr = lr * sr - li * si + sre_ref[rows, :]
            ni = lr * si + li * sr + sim_ref[rows, :]
            sre_ref[rows, :] = nr
            sim_ref[rows, :] = ni
            return nr, ni

        fr, fi = _scan_loop(ls, step, (zero, zero))
        pr, pi_ = _complex_power(lr, li, ls)
        row = lax.broadcasted_iota(jnp.int32, shp, 0)
        ir, ii = zero, zero
        for _ in range(SUBLANES - 1):
            er = fr + pr * ir - pi_ * ii
            ei = fi + pr * ii + pi_ * ir
            ir = jnp.where(row == 0, 0.0, pltpu.roll(er, 1, 0))
            ii = jnp.where(row == 0, 0.0, pltpu.roll(ei, 1, 0))

        def fix(i, carry):
            cr, ci = carry
            rows = _rows8(i)
            nr = lr * cr - li * ci
            ni = lr * ci + li * cr
            sre_ref[rows, :] += nr
            sim_ref[rows, :] += ni
            return nr, ni

        _scan_loop(ls, fix, (ir, ii))
        for c in range(T // rc):
            rows = pl.ds(c * rc, rc)
            s_r, s_i = sre_ref[rows, :].astype(BF16), sim_ref[rows, :].astype(BF16)
            so_re_ref[rows, :] = s_r
            so_im_ref[rows, :] = s_i
            y_ref[rows, :] = _dot(s_r, cre_ref[...]) - _dot(s_i, cim_ref[...])

    return _call(body, name='ssm_fwd', grid=(N_STATE // STATE_BLOCK,),
                 in_specs=[sp['chan'], sp['b'], sp['b'], sp['c'], sp['c'], sp['lam'], sp['lam']],
                 out_specs=[sp['state'], sp['state'], sp['chan']],
                 out_shape=[_sds((T, N_STATE), BF16), _sds((T, N_STATE), BF16), _sds((T, D_SSM))],
                 scratch=[pltpu.VMEM((T, STATE_BLOCK), F32), pltpu.VMEM((T, STATE_BLOCK), F32)],
                 sem=('arbitrary',), vmem=VMEM_MOST, ride=ride)(u_perm, b_re, b_im, c_re, c_im, lam_r, lam_i)


def _ssm_bwd(dy_perm, u_perm, s_re, s_im, b_re, b_im, c_re, c_im, lam_r, lam_i, ride):
    T = u_perm.shape[0]
    ls = T // SUBLANES
    rc = min(512, T)
    sp = _scan_specs(T)
    ncb = N_STATE // STATE_BLOCK

    def body(dy_ref, u_ref, sre_ref, sim_ref, bre_ref, bim_ref, cre_ref, cim_ref, lr_ref, li_ref,
             du_ref, dbr_ref, dbi_ref, dcr_ref, dci_ref, dar_ref, dai_ref, gre_ref, gim_ref):
        shp = (SUBLANES, STATE_BLOCK)
        zero = jnp.zeros(shp, F32)
        tail = pl.ds(T, SUBLANES)
        gre_ref[tail, :] = zero
        gim_ref[tail, :] = zero
        for c in range(T // rc):
            rows = pl.ds(c * rc, rc)
            dyb = dy_ref[rows, :].astype(BF16)
            gre_ref[rows, :] = _dot_nt(dyb, cre_ref[...])
            gim_ref[rows, :] = -_dot_nt(dyb, cim_ref[...])
        lr = jnp.broadcast_to(lr_ref[...], shp)
        li = jnp.broadcast_to(li_ref[...], shp)

        def step(k, carry):
            gr, gi = carry
            rows = _rows8(ls - 1 - k)
            nr = lr * gr + li * gi + gre_ref[rows, :]
            ni = lr * gi - li * gr + gim_ref[rows, :]
            gre_ref[rows, :] = nr
            gim_ref[rows, :] = ni
            return nr, ni

        fr, fi = _scan_loop(ls, step, (zero, zero))
        pr, pi_ = _complex_power(lr, -li, ls)
        row = lax.broadcasted_iota(jnp.int32, shp, 0)
        cr, ci = zero, zero
        for _ in range(SUBLANES - 1):
            er = fr + pr * cr - pi_ * ci
            ei = fi + pr * ci + pi_ * cr
            cr = jnp.where(row == SUBLANES - 1, 0.0, pltpu.roll(er, SUBLANES - 1, 0))
            ci = jnp.where(row == SUBLANES - 1, 0.0, pltpu.roll(ei, SUBLANES - 1, 0))

        def fix(k, carry):
            dr, di = carry
            rows = _rows8(ls - 1 - k)
            dr, di = lr * dr + li * di, lr * di - li * dr
            gre_ref[rows, :] += dr
            gim_ref[rows, :] += di
            return dr, di

        _scan_loop(ls, fix, (cr, ci))

        acc_r = jnp.zeros((1, STATE_BLOCK), F32)
        acc_i = jnp.zeros((1, STATE_BLOCK), F32)
        for c in range(T // rc):
            rows, nxt = pl.ds(c * rc, rc), pl.ds(c * rc + SUBLANES, rc)
            s_r, s_i = sre_ref[rows, :].astype(F32), sim_ref[rows, :].astype(F32)
            g_r, g_i = gre_ref[nxt, :], gim_ref[nxt, :]
            acc_r = acc_r + _colsum(g_r * s_r + g_i * s_i)
            acc_i = acc_i + _colsum(g_i * s_r - g_r * s_i)
        last = pl.ds(T - 2 * SUBLANES, 2 * SUBLANES)
        first = pl.ds(0, SUBLANES)
        spr = jnp.where(row == 0, 0.0, pltpu.roll(sre_ref[last, :].astype(F32)[SUBLANES:], 1, 0))
        spi = jnp.where(row == 0, 0.0, pltpu.roll(sim_ref[last, :].astype(F32)[SUBLANES:], 1, 0))
        gr, gi = gre_ref[first, :], gim_ref[first, :]
        dar_ref[...] = acc_r + _colsum(gr * spr + gi * spi)
        dai_ref[...] = acc_i + _colsum(gi * spr - gr * spi)

        for c in range(T // rc):
            rows = pl.ds(c * rc, rc)
            g_r, g_i = gre_ref[rows, :].astype(BF16), gim_ref[rows, :].astype(BF16)
            s_r, s_i = sre_ref[rows, :], sim_ref[rows, :]
            ub, dyb = u_ref[rows, :].astype(BF16), dy_ref[rows, :].astype(BF16)
            du_ref[rows, :] = _dot_nt(g_r, bre_ref[...]) + _dot_nt(g_i, bim_ref[...])
            parts = (_dot_tn(ub, g_r), _dot_tn(ub, g_i), _dot_tn(s_r, dyb), -_dot_tn(s_i, dyb))
            outs = (dbr_ref, dbi_ref, dcr_ref, dci_ref)
            for o_ref, part in zip(outs, parts):
                if c == 0:
                    o_ref[...] = part
                else:
                    o_ref[...] += part

    blk = lambda r, c: pl.BlockSpec((None, r, c), lambda cb: (cb, 0, 0))
    return _call(body, name='ssm_bwd', grid=(ncb,),
                 in_specs=[sp['chan'], sp['chan'], sp['state'], sp['state'], sp['b'], sp['b'], sp['c'], sp['c'],
                           sp['lam'], sp['lam']],
                 out_specs=[sp['chan'], blk(CHAN_BLOCK, STATE_BLOCK), blk(CHAN_BLOCK, STATE_BLOCK),
                            blk(STATE_BLOCK, CHAN_BLOCK), blk(STATE_BLOCK, CHAN_BLOCK), blk(1, STATE_BLOCK),
                            blk(1, STATE_BLOCK)],
                 out_shape=[_sds((T, D_SSM)), _sds((ncb, CHAN_BLOCK, STATE_BLOCK)), _sds((ncb, CHAN_BLOCK, STATE_BLOCK)),
                            _sds((ncb, STATE_BLOCK, CHAN_BLOCK)), _sds((ncb, STATE_BLOCK, CHAN_BLOCK)),
                            _sds((ncb, 1, STATE_BLOCK)), _sds((ncb, 1, STATE_BLOCK))],
                 scratch=[pltpu.VMEM((T + SUBLANES, STATE_BLOCK), F32), pltpu.VMEM((T + SUBLANES, STATE_BLOCK), F32)],
                 sem=('arbitrary',), vmem=VMEM_MOST, ride=ride)(dy_perm, u_perm, s_re, s_im, b_re, b_im, c_re, c_im,
                                                                lam_r, lam_i)


def _ffn_dact(ddn, wd4, hid4, tm):
    T = ddn.shape[0]

    def body(d_ref, w_ref, hid_ref, o_ref, act_ref):
        dact = _dot_nt(d_ref[...], w_ref[...])
        silu, dsilu = _silu_parts(hid_ref[0].astype(F32))
        hid_v = hid_ref[1].astype(F32)
        o_ref[0] = (dact * hid_v * dsilu).astype(BF16)
        o_ref[1] = (dact * silu).astype(BF16)
        act_ref[...] = (silu * hid_v).astype(BF16)

    blk = pl.BlockSpec((2, None, tm, FF_SHARD), lambda i, j: (0, j, i, 0))
    return _call(body, name='ffn_dact', grid=(T // tm, 4),
                 in_specs=[pl.BlockSpec((tm, D_MODEL), lambda i, j: (i, 0)),
                           pl.BlockSpec((None, FF_SHARD, D_MODEL), lambda i, j: (j, 0, 0)), blk],
                 out_specs=[blk, pl.BlockSpec((None, tm, FF_SHARD), lambda i, j: (j, i, 0))],
                 out_shape=[_sds((2, 4, T, FF_SHARD), BF16), _sds((4, T, FF_SHARD), BF16)],
                 sem=('parallel', 'parallel'))(ddn, wd4, hid4)


def _ffn_dup(dhid8, up8, cw8, tm, ride):
    T = up8.shape[1]
    nb = T // tm
    ha = _halo_after(tm, T, HALO16)

    def body(dh_ref, dha_ref, up_ref, cw_ref, dup_ref, dcw_ref):
        i = pl.program_id(1)

        @pl.when(i == 0)
        def _():
            dcw_ref[...] = jnp.zeros_like(dcw_ref)

        dh = dh_ref[...].astype(F32)
        dup, dh1, dh2 = _conv3_t(dh, jnp.where(i < nb - 1, dha_ref[...].astype(F32), 0.0), cw_ref)
        dup_ref[...] = dup.astype(BF16)
        up = up_ref[...].astype(F32)
        dcw_ref[0:1, :] += _colsum(dh2 * up)
        dcw_ref[1:2, :] += _colsum(dh1 * up)
        dcw_ref[2:3, :] += _colsum(dh * up)

    main = pl.BlockSpec((None, tm, FF_SHARD), lambda j, i: (j, i, 0))
    return _call(body, name='ffn_dup', grid=(N_DEV, nb),
                 in_specs=[main, pl.BlockSpec((None, HALO16, FF_SHARD), lambda j, i: (j, ha(i), 0)), main,
                           pl.BlockSpec((None, 3, FF_SHARD), lambda j, i: (j, 0, 0))],
                 out_specs=[main, pl.BlockSpec((None, 8, FF_SHARD), lambda j, i: (j, 0, 0))],
                 out_shape=[_sds((N_DEV, T, FF_SHARD), BF16), _sds((N_DEV, 8, FF_SHARD))],
                 sem=('parallel', 'arbitrary'), ride=ride)(dhid8, dhid8, up8, cw8)


def _grad_tn(a, b, a_spec, b_spec, groups, m, n, tk, name, ride=None, parts=1):
    T = a.shape[-2]
    nk = T // tk
    mp = m // parts

    def body(a_ref, b_ref, *refs):
        o_refs, acc_ref = refs[:parts], refs[parts]
        k = pl.program_id(1)
        part = _dot_tn(a_ref[...], b_ref[...])

        @pl.when(k == 0)
        def _():
            acc_ref[...] = part

        @pl.when(k > 0)
        def _():
            acc_ref[...] += part

        @pl.when(k == nk - 1)
        def _():
            for p, o_ref in enumerate(o_refs):
                o_ref[...] = acc_ref[p * mp:(p + 1) * mp, :].astype(BF16)

    out_spec = pl.BlockSpec((None, mp, n), lambda g, k: (g, 0, 0))
    res = _call(body, name=name, grid=(groups, nk), in_specs=[a_spec, b_spec], out_specs=[out_spec] * parts,
                out_shape=[_sds((groups, mp, n), BF16)] * parts, scratch=[pltpu.VMEM((m, n), F32)],
                sem=('parallel', 'arbitrary'), vmem=VMEM_BIG, ride=ride)(a, b)
    if parts > 1:
        return res
    return res[0] if ride is None else (res[0][0], res[1])


def _grad_w_in(h1, dproj, tk, ride):
    T = h1.shape[0]
    nk = T // tk
    half = D_IN_PROJ // 2

    def body(a_ref, b_ref, o_ref, acc_ref):
        k = pl.program_id(0)
        for h in range(2):
            cols = slice(h * half, (h + 1) * half)
            part = _dot_tn(a_ref[...], b_ref[:, cols])

            @pl.when(k == 0)
            def _():
                acc_ref[:, cols] = part

            @pl.when(k > 0)
            def _():
                acc_ref[:, cols] += part

        @pl.when(k == nk - 1)
        def _():
            for g in range(N_DEV):
                o_ref[g] = acc_ref[:, g * IN_SHARD:(g + 1) * IN_SHARD].astype(BF16)

    return _call(body, name='grad_w_in', grid=(nk,),
                 in_specs=[pl.BlockSpec((tk, D_MODEL), lambda k: (k, 0)), pl.BlockSpec((tk, D_IN_PROJ), lambda k: (k, 0))],
                 out_specs=_const((N_DEV, D_MODEL, IN_SHARD)), out_shape=_sds((N_DEV, D_MODEL, IN_SHARD), BF16),
                 scratch=[pltpu.VMEM((D_MODEL, D_IN_PROJ), F32)], sem=('arbitrary',), vmem=VMEM_BIG, ride=ride)(h1, dproj)


def _pre_norm_bwd(dz, dz_spec, w_s, xin, dres, sc, g, tm, name, ride, below=None, group=1, w_t=False):
    T = xin.shape[0]
    n = w_s.shape[1] if w_t else w_s.shape[2]
    mul = _dot if w_t else _dot_nt
    steps = N_DEV // group

    def body(dz_ref, w_ref, x_ref, dr_ref, sc_ref, g_ref, *refs):
        if below is None:
            dx_ref, dsh_ref, dsc_ref, dg_ref = refs
            sums = (dsh_ref, dsc_ref, dg_ref)
        else:
            v_ref, gate_ref, g2_ref, dx_ref, dsh_ref, dsc_ref, dg_ref, dv_ref, dgate_ref, dg2_ref = refs
            sums = (dsh_ref, dsc_ref, dg_ref, dgate_ref, dg2_ref)
        i, j = pl.program_id(0), pl.program_id(1)
        piece = (lambda s: dz_ref[s]) if dz.ndim == 3 else (lambda s: dz_ref[:, s * n:(s + 1) * n])
        part = mul(piece(0), w_ref[0])
        for s in range(1, group):
            part = part + mul(piece(s), w_ref[s])

        @pl.when(jnp.logical_and(i == 0, j == 0))
        def _():
            for s_ref in sums:
                s_ref[...] = jnp.zeros_like(s_ref)

        @pl.when(j == 0)
        def _():
            dx_ref[...] = part

        @pl.when(j > 0)
        def _():
            dx_ref[...] += part

        @pl.when(j == steps - 1)
        def _():
            dh, xv, gv = dx_ref[...], x_ref[...], g_ref[...]
            r = _rsqrt_mean(xv)
            dsh_ref[...] += _colsum(dh)
            dsc_ref[...] += _colsum(dh * (xv * r * gv))
            dxn = dh * (1.0 + sc_ref[...])
            dg_ref[...] += _colsum(dxn * xv * r)
            dx = dr_ref[...] + _norm_bwd(dxn, xv, r, gv)
            dx_ref[...] = dx
            if below is not None:
                v, g2 = v_ref[...], g2_ref[...]
                rv = _rsqrt_mean(v)
                dgate_ref[...] += _colsum(dx * (v * rv * g2))
                dn = dx * gate_ref[...]
                dg2_ref[...] += _colsum(dn * v * rv)
                dv_ref[...] = _norm_bwd(dn, v, rv, g2).astype(BF16)

    row = pl.BlockSpec((tm, D_MODEL), lambda i, j: (i, 0))
    vec = _const((1, D_MODEL))
    in_specs = [dz_spec, pl.BlockSpec((group,) + w_s.shape[1:], lambda i, j: (j, 0, 0)), row, row, vec, vec]
    out_specs = [row, vec, vec, vec]
    out_shape = [_sds((T, D_MODEL)), _sds((1, D_MODEL)), _sds((1, D_MODEL)), _sds((1, D_MODEL))]
    args = [dz, w_s, xin, dres, sc, g]
    if below is not None:
        in_specs += [row, vec, vec]
        out_specs += [row, vec, vec]
        out_shape += [_sds((T, D_MODEL), BF16), _sds((1, D_MODEL)), _sds((1, D_MODEL))]
        args += list(below)
    return _call(body, name=name, grid=(T // tm, steps), in_specs=in_specs, out_specs=out_specs,
                 out_shape=out_shape, sem=('arbitrary', 'arbitrary'), vmem=VMEM_MOST, ride=ride)(*args)


def _mix_bwd(d_o, w_out, yssm, proj, d, glu_w, glu_b, g_ssm, cw, g_conv, avg16, avg64, tm, ride):
    T = yssm.shape[0]
    hb = _halo_before(tm)

    def body(do_ref, wo_ref, y_ref, p_ref, ph_ref, d_ref, gw_ref, gb_ref, gs_ref, cw_ref, gc_ref, a16_ref, a64_ref,
             dy_ref, dconv_ref, dbg_ref, z_ref, dlin_ref, acc_ref):
        i = pl.program_id(0)
        dyc = _dot_nt(do_ref[...], wo_ref[...])

        @pl.when(i == 0)
        def _():
            acc_ref[...] = jnp.zeros_like(acc_ref)

        u = p_ref[:, 0:D_SSM]
        y = y_ref[...] + d_ref[...] * u
        z, t = _gelu(y)
        gate = _sigmoid(_dot(z.astype(BF16), gw_ref[...]) + gb_ref[...])
        ya = z * gate
        rs = lax.rsqrt(_dot_split(ya * ya, a16_ref[...], 2) + EPS)
        dna = dyc[:, 0:D_SSM]
        acc_ref[1:2, :] += _colsum(dna * ya * rs)
        dya = _head_norm_bwd(dna, ya, rs, gs_ref[...], a16_ref[...])
        dlin = dya * z * gate * (1.0 - gate)
        acc_ref[0:1, :] += _colsum(dlin)
        dlin_b = dlin.astype(BF16)
        dz = dya * gate + _dot_nt(dlin_b, gw_ref[...])
        dy = dz * _gelu_grad(y, t)
        acc_ref[3:4, :] += _colsum(dy * u)
        dy_ref[...] = dy
        z_ref[...] = z.astype(BF16)
        dlin_ref[...] = dlin_b

        bg = p_ref[:, D_SSM:D_SSM + D_CONV]
        cv = p_ref[:, D_SSM + D_CONV:D_SSM + 2 * D_CONV] * p_ref[:, D_SSM + 2 * D_CONV:D_IN_PROJ]
        hv = ph_ref[:, D_SSM + D_CONV:D_SSM + 2 * D_CONV] * ph_ref[:, D_SSM + 2 * D_CONV:D_IN_PROJ]
        hv = jnp.where(i > 0, hv, 0.0)
        conv, cv1, cv2 = _conv3(cv, hv, cw_ref)
        yb = bg * conv
        rsb = lax.rsqrt(_dot_split(yb * yb, a64_ref[...], 2) + EPS)
        dnb = dyc[:, D_SSM:D_MODEL]
        acc_ref[2:3, :] += _colsum(dnb * yb * rsb)
        dyb = _head_norm_bwd(dnb, yb, rsb, gc_ref[...], a64_ref[...])
        dbg_ref[...] = dyb * conv
        dconv = dyb * bg
        dconv_ref[...] = dconv
        acc_ref[4:5, :] += _colsum(dconv * cv2)
        acc_ref[5:6, :] += _colsum(dconv * cv1)
        acc_ref[6:7, :] += _colsum(dconv * cv)

    vec = _const((1, D_SSM))
    sq = _const((D_SSM, D_SSM))
    half = pl.BlockSpec((tm, D_SSM), lambda i: (i, 0))
    return _call(body, name='mix_bwd', grid=(T // tm,),
                 in_specs=[pl.BlockSpec((tm, D_MODEL), lambda i: (i, 0)), _const((D_MODEL, D_MODEL)), half,
                           pl.BlockSpec((tm, D_IN_PROJ), lambda i: (i, 0)),
                           pl.BlockSpec((HALO, D_IN_PROJ), lambda i: (hb(i), 0)), vec, sq, vec, vec,
                           _const((3, D_CONV)), vec, sq, sq],
                 out_specs=[half, half, half, half, half, _const((8, D_SSM))],
                 out_shape=[_sds((T, D_SSM)), _sds((T, D_SSM)), _sds((T, D_SSM)), _sds((T, D_SSM), BF16),
                            _sds((T, D_SSM), BF16), _sds((8, D_SSM))],
                 sem=('arbitrary',), vmem=VMEM_BIG, ride=ride)(d_o, w_out, yssm, proj, proj, d, glu_w, glu_b, g_ssm, cw,
                                                              g_conv, avg16, avg64)


def _mix_bwd_proj(dconv, proj, du_ssm, dy, d, dbg, cw, tm):
    T = dy.shape[0]
    nb = T // tm
    ha = _halo_after(tm, T)

    def body(dc_ref, dch_ref, cg_ref, v_ref, du_ref, dy_ref, d_ref, dbg_ref, cw_ref, o_ref):
        i = pl.program_id(0)
        dcv = _conv3_t(dc_ref[...], jnp.where(i < nb - 1, dch_ref[...], 0.0), cw_ref)[0]
        o_ref[:, 0:D_SSM] = (du_ref[...] + dy_ref[...] * d_ref[...]).astype(BF16)
        o_ref[:, D_SSM:D_SSM + D_CONV] = dbg_ref[...].astype(BF16)
        o_ref[:, D_SSM + D_CONV:D_SSM + 2 * D_CONV] = (dcv * v_ref[...]).astype(BF16)
        o_ref[:, D_SSM + 2 * D_CONV:D_IN_PROJ] = (dcv * cg_ref[...]).astype(BF16)

    half = pl.BlockSpec((tm, D_SSM), lambda i: (i, 0))
    return _call(body, name='mix_bwd_proj', grid=(nb,),
                 in_specs=[half, pl.BlockSpec((HALO, D_CONV), lambda i: (ha(i), 0)),
                           pl.BlockSpec((tm, D_CONV), lambda i: (i, 2)), pl.BlockSpec((tm, D_CONV), lambda i: (i, 3)),
                           half, half, _const((1, D_SSM)), half, _const((3, D_CONV))],
                 out_specs=pl.BlockSpec((tm, D_IN_PROJ), lambda i: (i, 0)), out_shape=_sds((T, D_IN_PROJ), BF16),
                 sem=('parallel',))(dconv, dconv, proj, proj, du_ssm, dy, d, dbg, cw)


def _row_tile(rows, cols, slots):
    for cand in (512, 256, 128, 64, 32, 16, 8):
        if rows % cand == 0 and slots * cand * cols * 4 <= (2 << 20):
            return cand
    return rows


def _adamw_math(g, w, m, v):
    m2 = ADAM_B1 * m + (1.0 - ADAM_B1) * g
    v2 = ADAM_B2 * v + (1.0 - ADAM_B2) * (g * g)
    m_hat = m2 / (1.0 - ADAM_B1 ** ADAM_STEP)
    v_hat = v2 / (1.0 - ADAM_B2 ** ADAM_STEP)
    return -ADAM_LR * (m_hat / (jnp.sqrt(v_hat) + ADAM_EPS) + ADAM_WD * w), m2, v2


def _adamw(pieces, w, m, v, name):
    slots, _, cols = pieces[0].shape
    rows = sum(p.shape[1] for p in pieces)
    tr = _row_tile(pieces[0].shape[1], cols, slots)
    starts, pos = [], 0
    for p in pieces:
        assert p.shape[1] % tr == 0
        starts.append(pos)
        pos += p.shape[1] // tr

    def body(*refs):
        g_refs = refs[:len(pieces)]
        w_ref, m_ref, v_ref, go_ref, d_ref, mo_ref, vo_ref = refs[len(pieces):]
        i = pl.program_id(0)
        g = None
        for g_ref, start in zip(g_refs, starts):
            part = g_ref[0].astype(F32)
            for s in range(1, slots):
                part = part + g_ref[s].astype(F32)
            g = part if g is None else jnp.where(i >= start, part, g)
        go_ref[...] = g
        d_ref[...], mo_ref[...], vo_ref[...] = _adamw_math(g, w_ref[...], m_ref[...], v_ref[...])

    def piece_spec(start, count):
        return pl.BlockSpec((slots, tr, cols), lambda i: (0, jnp.clip(i - start, 0, count - 1), 0))

    blk = pl.BlockSpec((tr, cols), lambda i: (i, 0))
    return _call(body, name=name, grid=(rows // tr,),
                 in_specs=[piece_spec(s, p.shape[1] // tr) for s, p in zip(starts, pieces)] + [blk, blk, blk],
                 out_specs=[blk] * 4, out_shape=[_sds((rows, cols))] * 4, sem=('parallel',))(*pieces, w, m, v)


def _to_scan_rows(a):
    T, n = a.shape
    return a.reshape(SUBLANES, T // SUBLANES, n).transpose(1, 0, 2).reshape(T, n)


def _from_scan_rows(a):
    T, n = a.shape
    return a.reshape(T // SUBLANES, SUBLANES, n).transpose(1, 0, 2).reshape(T, n)


def _expand(a):
    return jnp.repeat(a, SSM_GROUP, axis=1)


def _block_diag(rows, row_group, col_group):
    r, n = rows.shape
    tiled = jnp.tile(rows, (1, N_GROUPS))
    keep = (jnp.arange(r)[:, None] // row_group) == (jnp.arange(n * N_GROUPS)[None, :] // col_group)
    return jnp.where(keep, tiled, 0.0)


def _block_diag_b(bb):
    return _block_diag(bb.transpose(0, 2, 1).reshape(D_SSM, SSM_STATE), SSM_GROUP, SSM_STATE)


def _block_diag_c(cc):
    return _block_diag(cc.transpose(0, 2, 1).reshape(N_STATE, SSM_GROUP), SSM_STATE, SSM_GROUP)


def _diag_blocks(x, chan_major):
    per = CHAN_BLOCK // SSM_GROUP
    eye = jnp.eye(per, dtype=x.dtype)
    if chan_major:
        x = x.reshape(-1, per, SSM_GROUP, per, SSM_STATE) * eye[None, :, None, :, None]
        return x.sum(axis=1).transpose(0, 2, 3, 1).reshape(N_GROUPS, SSM_STATE, SSM_GROUP)
    x = x.reshape(-1, per, SSM_STATE, per, SSM_GROUP) * eye[None, :, None, :, None]
    return x.sum(axis=3).reshape(N_GROUPS, SSM_STATE, SSM_GROUP)


SMALL_LAYOUT = {
    'ssm_b_re': (0, 0, 32, 1024), 'ssm_b_im': (32, 0, 32, 1024), 'ssm_c_re': (64, 0, 32, 1024),
    'ssm_c_im': (96, 0, 32, 1024), 'b_ada': (128, 0, 6, 1024), 'g_pre_mix': (134, 0, 1, 1024),
    'g_post_mix': (135, 0, 1, 1024), 'ssm_lam_re': (136, 0, 2, 1024), 'ssm_lam_im': (138, 0, 2, 1024),
    'ssm_log_step': (140, 0, 1, 32), 'glu_b': (141, 0, 1, 512), 'g_out_ssm': (141, 512, 1, 512),
    'g_out_conv': (142, 0, 1, 512), 'ssm_d': (142, 512, 1, 512), 'g_pre_ffn': (143, 0, 1, 1024),
    'g_post_ffn': (144, 0, 1, 1024)}
SMALL_ROWS = 152
B_ADA_ROW = SMALL_LAYOUT['b_ada'][0]
LATE_ROWS = {('b_ada', 0): 0, ('b_ada', 1): 1, ('g_pre_mix', 0): 2}


def _adamw_small(gathered, late, wts, mom_m, mom_v):
    names = list(SMALL_LAYOUT)
    n = len(names)

    def body(*refs):
        g_ref, late_ref, ins, outs = refs[0], refs[1], refs[2:2 + 3 * n], refs[2 + 3 * n:]
        for p, name in enumerate(names):
            r0, c0, rows, cols = SMALL_LAYOUT[name]
            pieces = [(0, rows)] if rows % 8 == 0 else [(r, 1) for r in range(rows)]
            for r, cnt in pieces:
                src_ref, first = (late_ref, LATE_ROWS[name, r]) if (name, r) in LATE_ROWS else (g_ref, r0 + r)
                g = src_ref[0, first:first + cnt, c0:c0 + cols]
                for s in range(1, N_DEV):
                    g = g + src_ref[s, first:first + cnt, c0:c0 + cols]
                w, m, v = (ins[3 * p + q][r:r + cnt, :] for q in range(3))
                res = (g,) + _adamw_math(g, w, m, v)
                for q in range(4):
                    outs[4 * p + q][r:r + cnt, :] = res[q]

    shapes = [SMALL_LAYOUT[name][2:] for name in names]
    args = [gathered, late]
    for name, shp in zip(names, shapes):
        args += [wts[name].reshape(shp), mom_m[name].reshape(shp), mom_v[name].reshape(shp)]
    outs = _call(body, name='adamw_small', grid=(1,),
                 in_specs=[_const(gathered.shape), _const(late.shape)]
                 + [_const(shp) for shp in shapes for _ in range(3)],
                 out_specs=[_const(shp) for shp in shapes for _ in range(4)],
                 out_shape=[_sds(shp) for shp in shapes for _ in range(4)], vmem=VMEM_BIG)(*args)
    res = {}
    for p, name in enumerate(names):
        for q, kind in enumerate(('g', 'd', 'm', 'v')):
            res[kind, name] = outs[4 * p + q].reshape(wts[name].shape)
    return res


def kernel(x, c, w_ada, b_ada, g_pre_mix, g_post_mix, w_in, ssm_lam_re, ssm_lam_im, ssm_log_step, ssm_b_re, ssm_b_im, ssm_c_re, ssm_c_im, ssm_d, glu_w, glu_b, g_out_ssm, conv_w, g_out_conv, w_out, g_pre_ffn, g_post_ffn, w_up, ffn_conv_w, w_down, loss_target, m_w_ada, m_b_ada, m_g_pre_mix, m_g_post_mix, m_w_in, m_ssm_lam_re, m_ssm_lam_im, m_ssm_log_step, m_ssm_b_re, m_ssm_b_im, m_ssm_c_re, m_ssm_c_im, m_ssm_d, m_glu_w, m_glu_b, m_g_out_ssm, m_conv_w, m_g_out_conv, m_w_out, m_g_pre_ffn, m_g_post_ffn, m_w_up, m_ffn_conv_w, m_w_down, v_w_ada, v_b_ada, v_g_pre_mix, v_g_post_mix, v_w_in, v_ssm_lam_re, v_ssm_lam_im, v_ssm_log_step, v_ssm_b_re, v_ssm_b_im, v_ssm_c_re, v_ssm_c_im, v_ssm_d, v_glu_w, v_glu_b, v_g_out_ssm, v_conv_w, v_g_out_conv, v_w_out, v_g_pre_ffn, v_g_post_ffn, v_w_up, v_ffn_conv_w, v_w_down):
    args = dict(locals())
    wts = {n: args[n] for n in WEIGHTS}
    mom_m = {n: args['m_' + n] for n in WEIGHTS}
    mom_v = {n: args['v_' + n] for n in WEIGHTS}
    T = x.shape[1]
    tm = min(512, T)
    tw = min(1024, T)
    me = _me()[3]
    xt, tgt = x[0], loss_target[0]

    c_all, w_in_s, glu_s, w_out_s, conv_s = _exchange(
        [c, w_in[0].astype(BF16), glu_w[0].astype(BF16), w_out[0].astype(BF16), conv_w[0]], name='gather_first',
        scatter=False)
    c_all = c_all.reshape(N_DEV, D_MODEL)
    b_cols = lax.dynamic_slice(b_ada, (0, me * ADA_SHARD), (1, ADA_SHARD))
    mod_cols, c_act = _mod_cols(c_all, w_ada[0], b_cols)
    (mod_all,) = _exchange([mod_cols], name='gather_mod', scatter=False)
    mod = lax.dynamic_slice(mod_all, (0, me, 0), (N_DEV, 1, ADA_SHARD)).reshape(N_MOD, 1, D_MODEL)
    sh1, sc1, gt1, sh2, sc2, gt2 = [mod[k] for k in range(N_MOD)]

    glu_full = glu_s.reshape(D_SSM, D_SSM)
    w_out_full = w_out_s.reshape(D_MODEL, D_MODEL)
    cw_full = conv_s.transpose(1, 0, 2).reshape(3, D_CONV)

    lre_x, lim_x = _expand(ssm_lam_re[0]), _expand(ssm_lam_im[0])
    lst_x = jnp.broadcast_to(ssm_log_step[0][:, None], (N_GROUPS, SSM_STATE * SSM_GROUP))
    b_re_x = ssm_b_re[0].reshape(N_GROUPS, -1)
    b_im_x = ssm_b_im[0].reshape(N_GROUPS, -1)
    ar_x, ai_x, bbr_x, bbi_x = _ssm_prep(lre_x, lim_x, lst_x, b_re_x, b_im_x)
    lam_r = ar_x[:, ::SSM_GROUP].reshape(1, N_STATE)
    lam_i = ai_x[:, ::SSM_GROUP].reshape(1, N_STATE)
    big_b_re = _block_diag_b(bbr_x.reshape(N_GROUPS, SSM_STATE, SSM_GROUP)).astype(BF16)
    big_b_im = _block_diag_b(bbi_x.reshape(N_GROUPS, SSM_STATE, SSM_GROUP)).astype(BF16)
    big_c_re = _block_diag_c(ssm_c_re[0]).astype(BF16)
    big_c_im = _block_diag_c(ssm_c_im[0]).astype(BF16)
    head = jnp.arange(D_SSM)
    avg16 = jnp.where(head[:, None] // SSM_GROUP == head[None, :] // SSM_GROUP, 1.0 / SSM_GROUP, 0.0).astype(BF16)
    hd = D_CONV // CONV_HEADS
    avg64 = jnp.where(head[:, None] // hd == head[None, :] // hd, 1.0 / hd, 0.0).astype(BF16)

    (proj, h1), (w_down_s, ffn_conv_s) = _pre_mix(xt, sc1, sh1, g_pre_mix, w_in_s, tw,
                                                  ([w_down[0].astype(BF16), ffn_conv_w[0]], False))
    wd4 = w_down_s.reshape(4, FF_SHARD, D_MODEL)
    u_perm = _to_scan_rows(proj[:, :D_SSM])
    (s_re, s_im, y_perm), (w_up_s,) = _ssm_fwd(u_perm, big_b_re, big_b_im, big_c_re, big_c_im, lam_r, lam_i,
                                               ([w_up[0].T.astype(BF16)], False))
    yssm = _from_scan_rows(y_perm)
    mix_args = (ssm_d, glu_full, glu_b, g_out_ssm, cw_full, g_out_conv, avg16, avg64)
    ycat = _mix_fwd(yssm, proj, *mix_args, tm)
    o, x1, h2 = _out_proj(ycat, w_out_full, xt, gt1, g_post_mix, g_pre_ffn, sc2, sh2, tm)
    up8, hid8 = _ffn_up(h2, w_up_s, ffn_conv_s, tw)
    hid4 = hid8.reshape(2, 4, T, FF_SHARD)
    ddn, dx2, loss_parts, d_gt2, d_g_post_ffn = _ffn_down(hid4, wd4, x1, tgt, gt2, g_post_ffn, tm)
    loss_local = jnp.sum(loss_parts[:, 0, 0])

    got = {}
    dhid, act = _ffn_dact(ddn, wd4, hid4, tm)
    g_w_down = _grad_tn(act, ddn, pl.BlockSpec((None, tw, FF_SHARD), lambda g, k: (g, k, 0)),
                        pl.BlockSpec((tw, D_MODEL), lambda g, k: (k, 0)), 4, FF_SHARD, D_MODEL, tw, 'grad_w_down')
    (dup8, dcw_ffn), (got['w_down'],) = _ffn_dup(dhid.reshape(N_DEV, T, FF_SHARD), up8, ffn_conv_s, tm,
                                                 ([g_w_down.reshape(N_DEV, D_FF // N_DEV, D_MODEL)], True))
    g_w_up_halves = _grad_tn(dup8, h2, pl.BlockSpec((None, tw, FF_SHARD), lambda g, k: (g, k, 0)),
                             pl.BlockSpec((tw, D_MODEL), lambda g, k: (k, 0)), N_DEV, FF_SHARD, D_MODEL, tw,
                             'grad_w_up', parts=2)
    (dx1, d_sh2, d_sc2, d_g_pre_ffn, d_o, d_gt1, d_g_post_mix), (got_up_0, got['ffn_conv_w']) = _pre_norm_bwd(
        dup8, pl.BlockSpec((2, tw, FF_SHARD), lambda i, j: (j, i, 0)), w_up_s, x1, dx2, sc2, g_pre_ffn, tw,
        'ffn_in_bwd', ([g_w_up_halves[0], dcw_ffn], True), below=(o, gt1, g_post_mix), group=2, w_t=True)

    g_w_out = _grad_tn(ycat, d_o, pl.BlockSpec((tw, D_MODEL), lambda g, k: (k, 0)),
                       pl.BlockSpec((tw, D_MODEL), lambda g, k: (k, 0)), 1, D_MODEL, D_MODEL, tw, 'grad_w_out')
    (dy, dconv, dbg, z_b, dlin_b, sums), (got['w_out'],) = _mix_bwd(
        d_o, w_out_full, yssm, proj, *mix_args, tm, ([g_w_out.reshape(N_DEV, D_MODEL // N_DEV, D_MODEL)], True))
    g_glu_w = _grad_tn(z_b, dlin_b, pl.BlockSpec((tw, D_SSM), lambda g, k: (k, 0)),
                       pl.BlockSpec((tw, D_SSM), lambda g, k: (k, 0)), 1, D_SSM, D_SSM, tw, 'grad_glu_w')
    dy_perm = _to_scan_rows(dy)
    (du_perm, dbr_blk, dbi_blk, dcr_blk, dci_blk, dar_blk, dai_blk), (got_up_1, got['glu_w']) = _ssm_bwd(
        dy_perm, u_perm, s_re, s_im, big_b_re, big_b_im, big_c_re, big_c_im, lam_r, lam_i,
        ([g_w_up_halves[1], g_glu_w.reshape(N_DEV, D_SSM // N_DEV, D_SSM)], True))
    du_ssm = _from_scan_rows(du_perm)
    dproj = _mix_bwd_proj(dconv, proj, du_ssm, dy, ssm_d, dbg, cw_full, tm)
    dbb_re = _diag_blocks(dbr_blk, True).reshape(N_GROUPS, -1)
    dbb_im = _diag_blocks(dbi_blk, True).reshape(N_GROUPS, -1)
    d_c_re = _diag_blocks(dcr_blk, False).transpose(0, 2, 1)
    d_c_im = _diag_blocks(dci_blk, False).transpose(0, 2, 1)
    lane = jnp.arange(SSM_STATE * SSM_GROUP)
    seg = jnp.where(lane[:, None] // SSM_GROUP == lane[None, :] // SSM_GROUP, 1.0, 0.0).astype(BF16)
    d_b_re_x, d_b_im_x, d_lre_x, d_lim_x, d_lst = _ssm_prep_bwd(
        lre_x, lim_x, lst_x, b_re_x, b_im_x, dbb_re, dbb_im, _expand(dar_blk.reshape(N_GROUPS, SSM_STATE)),
        _expand(dai_blk.reshape(N_GROUPS, SSM_STATE)), seg)

    row = lambda a: a.reshape(-1, PACK_COLS)
    blank = jnp.zeros((1, PACK_COLS), F32)
    small_pack = jnp.concatenate([
        d_b_re_x, d_b_im_x, row(d_c_re), row(d_c_im), blank, blank, d_gt1, d_sh2, d_sc2, d_gt2, blank,
        d_g_post_mix, row(d_lre_x[:, ::SSM_GROUP]), row(d_lim_x[:, ::SSM_GROUP]),
        jnp.pad(d_lst.reshape(1, N_GROUPS), ((0, 0), (0, PACK_COLS - N_GROUPS))), row(sums[0:4]), d_g_pre_ffn,
        d_g_post_ffn, jnp.zeros((SMALL_ROWS - 145, PACK_COLS), F32)])
    g_w_in, (small_all,) = _grad_w_in(h1, dproj, tw, ([small_pack], False))
    g_conv_slots = jnp.concatenate([sums[4:7], jnp.zeros((5, D_CONV), F32)]).reshape(
        8, N_DEV, D_CONV // N_DEV).transpose(1, 0, 2)
    (grad_x, d_sh1, d_sc1, d_g_pre_mix), (got['w_in'], got['conv_w']) = _pre_norm_bwd(
        dproj, pl.BlockSpec((tw, D_IN_PROJ), lambda i, j: (i, j)), w_in_s, xt, dx1, sc1, g_pre_mix, tw,
        'mix_in_bwd', ([g_w_in, g_conv_slots], True), group=N_DEV)
    late_pack = jnp.concatenate([d_sh1, d_sc1, d_g_pre_mix, jnp.full((1, PACK_COLS), loss_local, F32),
                                 jnp.zeros((4, PACK_COLS), F32)])
    (late_all,) = _exchange([late_pack], name='gather_late_grads', scatter=False)
    loss = jnp.sum(late_all[:, 3, 0])
    res = _adamw_small(small_all, late_all, wts, mom_m, mom_v)

    dmod_all = jnp.concatenate([late_all[:, 0:2, :], small_all[:, B_ADA_ROW + 2:B_ADA_ROW + N_MOD, :]],
                               axis=1).reshape(N_DEV, N_MOD * D_MODEL)
    dmod_cols = lax.dynamic_slice(dmod_all, (0, me * ADA_SHARD), (N_DEV, ADA_SHARD))
    g_w_ada = _grad_w_ada(c_act.T, dmod_cols)

    pieces = {n: [slots[:, :3, :] if n in ('conv_w', 'ffn_conv_w') else slots] for n, slots in got.items()}
    for n, parts in pieces.items():
        outs = _adamw(parts, wts[n][0], mom_m[n][0], mom_v[n][0], 'adamw_' + n)
        for kind, val in zip(('g', 'd', 'm', 'v'), outs):
            res[kind, n] = val[None]
    outs = _adamw([got_up_0, got_up_1], w_up[0].T, m_w_up[0].T, v_w_up[0].T, 'adamw_w_up')
    for kind, val in zip(('g', 'd', 'm', 'v'), outs):
        res[kind, 'w_up'] = val.T[None]
    outs = _adamw([g_w_ada[None]], w_ada[0], m_w_ada[0], v_w_ada[0], 'adamw_w_ada')
    for kind, val in zip(('g', 'd', 'm', 'v'), outs):
        res[kind, 'w_ada'] = val[None]

    return (loss, grad_x[None], *[res['g', n] for n in WEIGHTS], *[res['d', n] for n in WEIGHTS],
            *[res['m', n] for n in WEIGHTS], *[res['v', n] for n in WEIGHTS])
```

```python
import math

import jax
import jax.numpy as jnp
from jax import lax
from jax.experimental import pallas as pl
from jax.experimental.pallas import tpu as pltpu

F32, BF16 = jnp.float32, jnp.bfloat16

D_MODEL = 1024
D_SSM = 512
D_CONV = 512
SSM_GROUP = 16
N_GROUPS = 32
SSM_STATE = 64
N_STATE = N_GROUPS * SSM_STATE
CONV_HEADS = 8
D_FF = 2816
N_MOD = 6
D_IN_PROJ = D_SSM + 3 * D_CONV
N_DEV = 8
FF_SHARD = 2 * D_FF // N_DEV
IN_SHARD = D_IN_PROJ // N_DEV
ADA_SHARD = N_MOD * D_MODEL // N_DEV
EPS = 1e-6
LAMBDA_RE_MAX = -1e-4
ADAM_LR, ADAM_B1, ADAM_B2, ADAM_EPS, ADAM_WD, ADAM_STEP = 0.001, 0.9, 0.999, 1e-08, 0.01, 10
GELU_C = math.sqrt(2.0 / math.pi)
GELU_A = 0.044715

SUBLANES = 8
HALO = 8
HALO16 = 16
SCAN_UNROLL = 8
STATE_BLOCK = 512
CHAN_BLOCK = 128
VMEM_BIG = 48 << 20
VMEM_MOST = 58 << 20

WEIGHTS = ['w_ada', 'b_ada', 'g_pre_mix', 'g_post_mix', 'w_in', 'ssm_lam_re', 'ssm_lam_im', 'ssm_log_step',
           'ssm_b_re', 'ssm_b_im', 'ssm_c_re', 'ssm_c_im', 'ssm_d', 'glu_w', 'glu_b', 'g_out_ssm', 'conv_w',
           'g_out_conv', 'w_out', 'g_pre_ffn', 'g_post_ffn', 'w_up', 'ffn_conv_w', 'w_down']
SHARDED = ('w_ada', 'w_in', 'glu_w', 'conv_w', 'w_out', 'w_up', 'ffn_conv_w', 'w_down')
PACK_COLS = 1024


def _call(body, *, name, grid, in_specs, out_specs, out_shape, scratch=(), sem=None, vmem=None, ride=None):
    params = {}
    if vmem is not None:
        params['vmem_limit_bytes'] = vmem
    if ride is None:
        if sem is not None:
            params['dimension_semantics'] = sem
        return pl.pallas_call(body, name=name, grid=grid, in_specs=in_specs, out_specs=out_specs,
                              out_shape=out_shape, scratch_shapes=list(scratch),
                              compiler_params=pltpu.CompilerParams(**params))
    arrs, scatter = ride
    single = not isinstance(out_shape, (list, tuple))
    out_shape_l = [out_shape] if single else list(out_shape)
    out_specs_l = [out_specs] if single else list(out_specs)
    n, n_in, n_out, n_scr = len(arrs), len(in_specs), len(out_shape_l), len(scratch)
    any_spec = pl.BlockSpec(memory_space=pl.ANY)
    params['dimension_semantics'] = ('arbitrary',) * len(grid)

    def carried(*refs):
        ins, rin = refs[:n_in], refs[n_in:n_in + n]
        outs, rout = refs[n_in + n:n_in + n + n_out], refs[n_in + n + n_out:n_in + 2 * n + n_out]
        scr, sems = refs[n_in + 2 * n + n_out:n_in + 2 * n + n_out + n_scr], refs[n_in + 2 * n + n_out + n_scr:]
        first = pl.program_id(0) == 0
        last = pl.program_id(0) == grid[0] - 1
        for ax in range(1, len(grid)):
            first = jnp.logical_and(first, pl.program_id(ax) == 0)
            last = jnp.logical_and(last, pl.program_id(ax) == grid[ax] - 1)

        @pl.when(first)
        def _():
            _exchange_start(rin, rout, sems, scatter)

        body(*ins, *outs, *scr)

        @pl.when(last)
        def _():
            _exchange_wait(rin, rout, sems, scatter)

    call = pl.pallas_call(carried, name=name, grid=grid, in_specs=list(in_specs) + [any_spec] * n,
                          out_specs=out_specs_l + [any_spec] * n,
                          out_shape=out_shape_l + _exchange_shapes(arrs, scatter),
                          scratch_shapes=list(scratch) + _exchange_sems(n),
                          compiler_params=pltpu.CompilerParams(**params))

    def run(*args):
        res = call(*args, *arrs)
        own = res[0] if single else list(res[:n_out])
        return own, list(res[n_out:])

    return run


def _const(shape):
    nd = len(shape)
    return pl.BlockSpec(shape, lambda *_: (0,) * nd)


def _sds(shape, dtype=F32):
    return jax.ShapeDtypeStruct(shape, dtype)


def _dot(a, b):
    return jnp.dot(a, b, preferred_element_type=F32)


def _dot_nt(a, b):
    return lax.dot_general(a, b, (((1,), (1,)), ((), ())), preferred_element_type=F32)


def _dot_tn(a, b):
    return lax.dot_general(a, b, (((0,), (0,)), ((), ())), preferred_element_type=F32)


def _dot_split(x, mat, parts):
    acc = None
    rem = x
    for _ in range(parts):
        piece = rem.astype(BF16)
        rem = rem - piece.astype(F32)
        term = _dot(piece, mat)
        acc = term if acc is None else acc + term
    return acc


def _sigmoid(x):
    return 1.0 / (1.0 + jnp.exp(-x))


def _gelu(x):
    t = jnp.tanh(GELU_C * (x + GELU_A * x * x * x))
    return 0.5 * x * (1.0 + t), t


def _gelu_grad(x, t):
    return 0.5 * (1.0 + t) + 0.5 * x * (1.0 - t * t) * GELU_C * (1.0 + 3.0 * GELU_A * x * x)


def _rsqrt_mean(x):
    return lax.rsqrt(jnp.mean(x * x, axis=-1, keepdims=True) + EPS)


def _colsum(x):
    return jnp.sum(x, axis=0, keepdims=True)


def _shifts_down(x, halo):
    ext = jnp.concatenate([halo, x], axis=0)
    return pltpu.roll(ext, 1, 0)[halo.shape[0]:], pltpu.roll(ext, 2, 0)[halo.shape[0]:]


def _shifts_up(x, halo):
    n = x.shape[0]
    ext = jnp.concatenate([x, halo], axis=0)
    total = ext.shape[0]
    return pltpu.roll(ext, total - 1, 0)[:n], pltpu.roll(ext, total - 2, 0)[:n]


def _conv3(x, halo, w_ref):
    x1, x2 = _shifts_down(x, halo)
    return w_ref[0:1, :] * x2 + w_ref[1:2, :] * x1 + w_ref[2:3, :] * x, x1, x2


def _conv3_t(g, halo, w_ref):
    g1, g2 = _shifts_up(g, halo)
    return w_ref[2:3, :] * g + w_ref[1:2, :] * g1 + w_ref[0:1, :] * g2, g1, g2


def _silu_parts(x):
    s = _sigmoid(x)
    return x * s, s * (1.0 + x * (1.0 - s))


def _norm_bwd(dn, x, r, g):
    gd = g * dn
    return r * gd - x * (r * r * r) * jnp.mean(gd * x, axis=-1, keepdims=True)


def _head_norm_bwd(dn, y, rs, g, avg):
    gd = g * dn
    return rs * gd - y * (rs * rs * rs) * _dot_split(gd * y, avg, 2)


def _me():
    x, y, c = lax.axis_index('x'), lax.axis_index('y'), lax.axis_index('c')
    return x, y, c, 4 * x + 2 * y + c


def _peer(k):
    x, y, c, _ = _me()
    px = 1 - x if k & 4 else x
    py = 1 - y if k & 2 else y
    pc = 1 - c if k & 1 else c
    return (px, py, pc), 4 * px + 2 * py + pc


SIBLING = 1
OTHER_CHIPS = (2, 4, 6)


def _remote(src, dst, sems, a, k, dev):
    return pltpu.make_async_remote_copy(src_ref=src, dst_ref=dst, send_sem=sems[0].at[a, k - 1],
                                        recv_sem=sems[1].at[a, k - 1], device_id=dev,
                                        device_id_type=pl.DeviceIdType.MESH)


def _exchange_copies(ins, outs, sems, scatter):
    me = _me()[3]
    local, first, relay, arrivals = [], [], [], []
    for a in range(len(ins)):
        src = ins[a].at[me] if scatter else ins[a]
        local.append(pltpu.make_async_copy(src, outs[a].at[me], sems[2].at[a]))
        for k in range(1, N_DEV):
            dev, idx = _peer(k)
            landed = _remote(src, outs[a].at[idx], sems, a, k, dev)
            if scatter:
                first.append(_remote(ins[a].at[idx], outs[a].at[me], sems, a, k, dev))
                arrivals.append(landed)
            elif k == SIBLING:
                first.append(_remote(src, outs[a].at[me], sems, a, k, dev))
                arrivals.append(landed)
            elif k in OTHER_CHIPS:
                first.append(_remote(src, outs[a].at[me], sems, a, k, dev))
                sib, _ = _peer(SIBLING)
                relay.append((landed, _remote(outs[a].at[idx], outs[a].at[idx], sems, a, k | SIBLING, sib)))
            else:
                arrivals.append(landed)
    return local, first, relay, arrivals


def _exchange_start(ins, outs, sems, scatter):
    local, first, _, _ = _exchange_copies(ins, outs, sems, scatter)
    for cp in local + first:
        cp.start()


def _exchange_wait(ins, outs, sems, scatter):
    local, first, relay, arrivals = _exchange_copies(ins, outs, sems, scatter)
    for landed, forward in relay:
        landed.wait_recv()
        forward.start()
    for cp in arrivals:
        cp.wait_recv()
    for cp in first + [forward for _, forward in relay]:
        cp.wait_send()
    for cp in local:
        cp.wait()


def _exchange_shapes(arrs, scatter):
    return [_sds(a.shape if scatter else (N_DEV,) + a.shape, a.dtype) for a in arrs]


def _exchange_sems(n):
    return [pltpu.SemaphoreType.DMA((n, N_DEV - 1)), pltpu.SemaphoreType.DMA((n, N_DEV - 1)),
            pltpu.SemaphoreType.DMA((n,))]


def _exchange(arrs, *, name, scatter):
    n = len(arrs)

    def body(*refs):
        _exchange_start(refs[:n], refs[n:2 * n], refs[2 * n:], scatter)
        _exchange_wait(refs[:n], refs[n:2 * n], refs[2 * n:], scatter)

    any_spec = pl.BlockSpec(memory_space=pl.ANY)
    outs = pl.pallas_call(body, name=name, out_shape=_exchange_shapes(arrs, scatter), in_specs=[any_spec] * n,
                          out_specs=[any_spec] * n, scratch_shapes=_exchange_sems(n))(*arrs)
    return list(outs)


def _mod_cols(c_all, w_ada, b_cols):
    def body(c_ref, w_ref, b_ref, mod_ref, act_ref):
        c = c_ref[...]
        act = c * _sigmoid(c)
        act_ref[...] = act
        mod_ref[...] = _dot(act.astype(BF16), w_ref[...].astype(BF16)) + b_ref[...]

    return _call(body, name='mod_cols', grid=(1,),
                 in_specs=[_const(c_all.shape), _const(w_ada.shape), _const(b_cols.shape)],
                 out_specs=[_const((N_DEV, ADA_SHARD)), _const(c_all.shape)],
                 out_shape=[_sds((N_DEV, ADA_SHARD)), _sds(c_all.shape)], vmem=VMEM_BIG)(c_all, w_ada, b_cols)


def _grad_w_ada(act_t, dmod_cols):
    def body(a_ref, d_ref, o_ref):
        o_ref[...] = _dot(a_ref[...], d_ref[...])

    return _call(body, name='grad_w_ada', grid=(1,), in_specs=[_const(act_t.shape), _const(dmod_cols.shape)],
                 out_specs=_const((D_MODEL, ADA_SHARD)), out_shape=_sds((D_MODEL, ADA_SHARD)),
                 vmem=VMEM_BIG)(act_t, dmod_cols)


def _pre_mix(x, sc, sh, g, w_s, tm, ride):
    T = x.shape[0]

    def body(x_ref, sc_ref, sh_ref, g_ref, w_ref, proj_ref, h_ref):
        @pl.when(pl.program_id(1) == 0)
        def _():
            xv = x_ref[...]
            h_ref[...] = ((xv * _rsqrt_mean(xv) * g_ref[...]) * (1.0 + sc_ref[...]) + sh_ref[...]).astype(BF16)

        for s in range(2):
            proj_ref[:, s * IN_SHARD:(s + 1) * IN_SHARD] = _dot(h_ref[...], w_ref[s])

    row = pl.BlockSpec((tm, D_MODEL), lambda i, j: (i, 0))
    vec = _const((1, D_MODEL))
    return _call(body, name='pre_mix', grid=(T // tm, N_DEV // 2),
                 in_specs=[row, vec, vec, vec, pl.BlockSpec((2, D_MODEL, IN_SHARD), lambda i, j: (j, 0, 0))],
                 out_specs=[pl.BlockSpec((tm, 2 * IN_SHARD), lambda i, j: (i, j)), row],
                 out_shape=[_sds((T, D_IN_PROJ)), _sds((T, D_MODEL), BF16)],
                 sem=('parallel', 'arbitrary'), ride=ride)(x, sc, sh, g, w_s)


def _halo_before(tm, rows=HALO):
    return lambda i: jnp.maximum(i * (tm // rows) - 1, 0)


def _halo_after(tm, T, rows=HALO):
    return lambda i: jnp.minimum((i + 1) * (tm // rows), T // rows - 1)


def _mix_fwd(yssm, proj, d, glu_w, glu_b, g_ssm, cw, g_conv, avg16, avg64, tm):
    T = yssm.shape[0]
    hb = _halo_before(tm)

    def body(y_ref, p_ref, ph_ref, d_ref, gw_ref, gb_ref, gs_ref, cw_ref, gc_ref, a16_ref, a64_ref, o_ref):
        i = pl.program_id(0)
        u = p_ref[:, 0:D_SSM]
        y = y_ref[...] + d_ref[...] * u
        z, _ = _gelu(y)
        gate = _sigmoid(_dot(z.astype(BF16), gw_ref[...]) + gb_ref[...])
        ya = z * gate
        rs = lax.rsqrt(_dot_split(ya * ya, a16_ref[...], 2) + EPS)
        o_ref[:, 0:D_SSM] = (ya * rs * gs_ref[...]).astype(BF16)
        bg = p_ref[:, D_SSM:D_SSM + D_CONV]
        cv = p_ref[:, D_SSM + D_CONV:D_SSM + 2 * D_CONV] * p_ref[:, D_SSM + 2 * D_CONV:D_IN_PROJ]
        hv = ph_ref[:, D_SSM + D_CONV:D_SSM + 2 * D_CONV] * ph_ref[:, D_SSM + 2 * D_CONV:D_IN_PROJ]
        hv = jnp.where(i > 0, hv, 0.0)
        conv, _, _ = _conv3(cv, hv, cw_ref)
        yb = bg * conv
        rsb = lax.rsqrt(_dot_split(yb * yb, a64_ref[...], 2) + EPS)
        o_ref[:, D_SSM:D_MODEL] = (yb * rsb * gc_ref[...]).astype(BF16)

    vec = _const((1, D_SSM))
    sq = _const((D_SSM, D_SSM))
    return _call(body, name='mix_fwd', grid=(T // tm,),
                 in_specs=[pl.BlockSpec((tm, D_SSM), lambda i: (i, 0)), pl.BlockSpec((tm, D_IN_PROJ), lambda i: (i, 0)),
                           pl.BlockSpec((HALO, D_IN_PROJ), lambda i: (hb(i), 0)), vec, sq, vec, vec,
                           _const((3, D_CONV)), vec, sq, sq],
                 out_specs=pl.BlockSpec((tm, D_MODEL), lambda i: (i, 0)), out_shape=_sds((T, D_MODEL), BF16),
                 sem=('parallel',), vmem=VMEM_BIG)(yssm, proj, proj, d, glu_w, glu_b, g_ssm, cw, g_conv, avg16, avg64)


def _out_proj(ycat, w_out, x, gt, g_post, g_pre, sc, sh, tm):
    T = x.shape[0]

    def body(y_ref, w_ref, x_ref, gt_ref, gp_ref, g2_ref, sc_ref, sh_ref, o_ref, x1_ref, h_ref):
        o = _dot(y_ref[...], w_ref[...])
        o_ref[...] = o
        x1 = x_ref[...] + gt_ref[...] * (o * _rsqrt_mean(o) * gp_ref[...])
        x1_ref[...] = x1
        h_ref[...] = ((x1 * _rsqrt_mean(x1) * g2_ref[...]) * (1.0 + sc_ref[...]) + sh_ref[...]).astype(BF16)

    row = pl.BlockSpec((tm, D_MODEL), lambda i: (i, 0))
    vec = _const((1, D_MODEL))
    return _call(body, name='out_proj', grid=(T // tm,),
                 in_specs=[row, _const((D_MODEL, D_MODEL)), row, vec, vec, vec, vec, vec],
                 out_specs=[row, row, row],
                 out_shape=[_sds((T, D_MODEL)), _sds((T, D_MODEL)), _sds((T, D_MODEL), BF16)],
                 sem=('parallel',), vmem=VMEM_BIG)(ycat, w_out, x, gt, g_post, g_pre, sc, sh)


def _ffn_up(h2, w_s, cw8, tm):
    T = h2.shape[0]
    hb = _halo_before(tm, HALO16)

    def body(h_ref, hh_ref, w_ref, cw_ref, up_ref, hid_ref):
        up = _dot_nt(h_ref[...], w_ref[...])
        up_ref[...] = up.astype(BF16)
        before = jnp.where(pl.program_id(0) > 0, _dot_nt(hh_ref[...], w_ref[...]), 0.0)
        hid_ref[...] = _conv3(up, before, cw_ref)[0].astype(BF16)

    out = pl.BlockSpec((None, tm, FF_SHARD), lambda i, j: (j, i, 0))
    return _call(body, name='ffn_up', grid=(T // tm, N_DEV),
                 in_specs=[pl.BlockSpec((tm, D_MODEL), lambda i, j: (i, 0)),
                           pl.BlockSpec((HALO16, D_MODEL), lambda i, j: (hb(i), 0)),
                           pl.BlockSpec((None, FF_SHARD, D_MODEL), lambda i, j: (j, 0, 0)),
                           pl.BlockSpec((None, 3, FF_SHARD), lambda i, j: (j, 0, 0))],
                 out_specs=[out, out], out_shape=[_sds((N_DEV, T, FF_SHARD), BF16)] * 2,
                 sem=('parallel', 'parallel'))(h2, h2, w_s, cw8)


def _ffn_down(hid4, wd4, x1, tgt, gt, g_post, tm):
    T = x1.shape[0]
    nb = T // tm

    def body(a_ref, w_ref, x1_ref, t_ref, gt_ref, g_ref, ddn_ref, dx_ref, loss_ref, dgt_ref, dg_ref, dn_ref):
        i, j = pl.program_id(0), pl.program_id(1)
        part = None
        for s in range(2):
            act = (_silu_parts(a_ref[0, s].astype(F32))[0] * a_ref[1, s].astype(F32)).astype(BF16)
            term = _dot(act, w_ref[s])
            part = term if part is None else part + term

        @pl.when(jnp.logical_and(i == 0, j == 0))
        def _():
            dgt_ref[...] = jnp.zeros_like(dgt_ref)
            dg_ref[...] = jnp.zeros_like(dg_ref)

        @pl.when(j == 0)
        def _():
            dn_ref[...] = part

        @pl.when(j > 0)
        def _():
            dn_ref[...] += part

        @pl.when(j == 1)
        def _():
            dn, gv, gate = dn_ref[...], g_ref[...], gt_ref[...]
            r = _rsqrt_mean(dn)
            normed = dn * r * gv
            err = x1_ref[...] + gate * normed - t_ref[...]
            dx = err * (1.0 / D_MODEL)
            dx_ref[...] = dx
            tot = jnp.sum(jnp.sum(err * err, axis=1, keepdims=True), axis=0, keepdims=True) * (0.5 / D_MODEL)
            loss_ref[...] = jnp.broadcast_to(tot, (8, 128))
            dgt_ref[...] += _colsum(dx * normed)
            dnn = dx * gate
            dg_ref[...] += _colsum(dnn * dn * r)
            ddn_ref[...] = _norm_bwd(dnn, dn, r, gv).astype(BF16)

    row = pl.BlockSpec((tm, D_MODEL), lambda i, j: (i, 0))
    vec = _const((1, D_MODEL))
    return _call(body, name='ffn_down', grid=(nb, 2),
                 in_specs=[pl.BlockSpec((2, 2, tm, FF_SHARD), lambda i, j: (0, j, i, 0)),
                           pl.BlockSpec((2, FF_SHARD, D_MODEL), lambda i, j: (j, 0, 0)), row, row, vec, vec],
                 out_specs=[row, row, pl.BlockSpec((None, 8, 128), lambda i, j: (i, 0, 0)), vec, vec],
                 out_shape=[_sds((T, D_MODEL), BF16), _sds((T, D_MODEL)), _sds((nb, 8, 128)), _sds((1, D_MODEL)),
                            _sds((1, D_MODEL))],
                 scratch=[pltpu.VMEM((tm, D_MODEL), F32)], sem=('arbitrary', 'arbitrary'),
                 vmem=VMEM_BIG)(hid4, wd4, x1, tgt, gt, g_post)


def _ssm_prep(lre, lim, lst, b_re, b_im):
    def body(lre_ref, lim_ref, lst_ref, br_ref, bi_ref, ar_ref, ai_ref, bbr_ref, bbi_ref):
        ar, ai, qr, qi = _zoh(lre_ref[...], lim_ref[...], lst_ref[...])[:4]
        ar_ref[...] = ar
        ai_ref[...] = ai
        bbr_ref[...] = qr * br_ref[...] - qi * bi_ref[...]
        bbi_ref[...] = qr * bi_ref[...] + qi * br_ref[...]

    shp = lre.shape
    return _call(body, name='ssm_prep', grid=(1,), in_specs=[_const(shp)] * 5, out_specs=[_const(shp)] * 4,
                 out_shape=[_sds(shp)] * 4)(lre, lim, lst, b_re, b_im)


def _zoh(lre, lim, lst):
    lr = jnp.minimum(lre, LAMBDA_RE_MAX)
    st = jnp.exp(lst)
    mag = jnp.exp(lr * st)
    ar = mag * jnp.cos(lim * st)
    ai = mag * jnp.sin(lim * st)
    den = lr * lr + lim * lim
    qr = ((ar - 1.0) * lr + ai * lim) / den
    qi = (ai * lr - (ar - 1.0) * lim) / den
    return ar, ai, qr, qi, lr, st, den


def _ssm_prep_bwd(lre, lim, lst, b_re, b_im, dbbr, dbbi, dar, dai, seg):
    def body(lre_ref, lim_ref, lst_ref, br_ref, bi_ref, dbbr_ref, dbbi_ref, dar_ref, dai_ref, seg_ref,
             dbr_ref, dbi_ref, dlre_ref, dlim_ref, dlst_ref):
        lre_v = lre_ref[...]
        li = lim_ref[...]
        ar, ai, qr, qi, lr, st, den = _zoh(lre_v, li, lst_ref[...])
        br, bi, gbr, gbi = br_ref[...], bi_ref[...], dbbr_ref[...], dbbi_ref[...]
        dbr_ref[...] = qr * gbr + qi * gbi
        dbi_ref[...] = qr * gbi - qi * gbr
        gqr = _dot_split(br * gbr + bi * gbi, seg_ref[...], 3)
        gqi = _dot_split(br * gbi - bi * gbr, seg_ref[...], 3)
        ir, ii = lr / den, -li / den
        gar = dar_ref[...] + ir * gqr + ii * gqi
        gai = dai_ref[...] + ir * gqi - ii * gqr
        tr, ti = qr * ir - qi * ii, qr * ii + qi * ir
        glr = -(tr * gqr + ti * gqi)
        gli = -(tr * gqi - ti * gqr)
        gzr = ar * gar + ai * gai
        gzi = ar * gai - ai * gar
        glr = glr + st * gzr
        gli = gli + st * gzi
        gst = (lr * gzr + li * gzi) * st
        dlre_ref[...] = jnp.where(lre_v < LAMBDA_RE_MAX, glr, 0.0)
        dlim_ref[...] = gli
        dlst_ref[...] = jnp.sum(gst, axis=1, keepdims=True) * (1.0 / SSM_GROUP)

    shp = lre.shape
    return _call(body, name='ssm_prep_bwd', grid=(1,), in_specs=[_const(shp)] * 9 + [_const(seg.shape)],
                 out_specs=[_const(shp)] * 4 + [_const((N_GROUPS, 1))],
                 out_shape=[_sds(shp)] * 4 + [_sds((N_GROUPS, 1))], vmem=VMEM_BIG)(
                     lre, lim, lst, b_re, b_im, dbbr, dbbi, dar, dai, seg)


def _scan_specs(T):
    return dict(
        chan=pl.BlockSpec((T, CHAN_BLOCK), lambda cb: (0, cb)),
        state=pl.BlockSpec((T, STATE_BLOCK), lambda cb: (0, cb)),
        b=pl.BlockSpec((CHAN_BLOCK, STATE_BLOCK), lambda cb: (cb, cb)),
        c=pl.BlockSpec((STATE_BLOCK, CHAN_BLOCK), lambda cb: (cb, cb)),
        lam=pl.BlockSpec((1, STATE_BLOCK), lambda cb: (0, cb)),
    )


def _complex_power(re, im, n):
    out = None
    while True:
        if n & 1:
            out = (re, im) if out is None else (out[0] * re - out[1] * im, out[0] * im + out[1] * re)
        n >>= 1
        if n == 0:
            return out
        re, im = re * re - im * im, 2.0 * re * im


def _rows8(i):
    if isinstance(i, int):
        return pl.ds(i * SUBLANES, SUBLANES)
    return pl.ds(pl.multiple_of(i * SUBLANES, SUBLANES), SUBLANES)


def _scan_loop(n_steps, body, init):
    trips = n_steps // SCAN_UNROLL

    def trip(t, carry):
        for u in range(SCAN_UNROLL):
            carry = body(t * SCAN_UNROLL + u, carry)
        return carry

    carry = lax.fori_loop(0, trips, trip, init)
    for step in range(trips * SCAN_UNROLL, n_steps):
        carry = body(step, carry)
    return carry


def _ssm_fwd(u_perm, b_re, b_im, c_re, c_im, lam_r, lam_i, ride):
    T = u_perm.shape[0]
    ls = T // SUBLANES
    rc = min(512, T)
    sp = _scan_specs(T)

    def body(u_ref, bre_ref, bim_ref, cre_ref, cim_ref, lr_ref, li_ref, so_re_ref, so_im_ref, y_ref, sre_ref, sim_ref):
        for c in range(T // rc):
            rows = pl.ds(c * rc, rc)
            ub = u_ref[rows, :].astype(BF16)
            sre_ref[rows, :] = _dot(ub, bre_ref[...])
            sim_ref[rows, :] = _dot(ub, bim_ref[...])
        shp = (SUBLANES, STATE_BLOCK)
        lr = jnp.broadcast_to(lr_ref[...], shp)
        li = jnp.broadcast_to(li_ref[...], shp)
        zero = jnp.zeros(shp, F32)

        def step(i, carry):
            sr, si = carry
            rows = _rows8(i)
            nr = lr * sr - li * si + sre_ref[rows, :]
            ni = lr * si + li * sr + sim_ref[rows, :]
            sre_ref[rows, :] = nr
            sim_ref[rows, :] = ni
            return nr, ni

        fr, fi = _scan_loop(ls, step, (zero, zero))
        pr, pi_ = _complex_power(lr, li, ls)
        row = lax.broadcasted_iota(jnp.int32, shp, 0)
        ir, ii = zero, zero
        for _ in range(SUBLANES - 1):
            er = fr + pr * ir - pi_ * ii
            ei = fi + pr * ii + pi_ * ir
            ir = jnp.where(row == 0, 0.0, pltpu.roll(er, 1, 0))
            ii = jnp.where(row == 0, 0.0, pltpu.roll(ei, 1, 0))

        def fix(i, carry):
            cr, ci = carry
            rows = _rows8(i)
            nr = lr * cr - li * ci
            ni = lr * ci + li * cr
            sre_ref[rows, :] += nr
            sim_ref[rows, :] += ni
            return nr, ni

        _scan_loop(ls, fix, (ir, ii))
        for c in range(T // rc):
            rows = pl.ds(c * rc, rc)
            s_r, s_i = sre_ref[rows, :].astype(BF16), sim_ref[rows, :].astype(BF16)
            so_re_ref[rows, :] = s_r
            so_im_ref[rows, :] = s_i
            y_ref[rows, :] = _dot(s_r, cre_ref[...]) - _dot(s_i, cim_ref[...])

    return _call(body, name='ssm_fwd', grid=(N_STATE // STATE_BLOCK,),
                 in_specs=[sp['chan'], sp['b'], sp['b'], sp['c'], sp['c'], sp['lam'], sp['lam']],
                 out_specs=[sp['state'], sp['state'], sp['chan']],
                 out_shape=[_sds((T, N_STATE), BF16), _sds((T, N_STATE), BF16), _sds((T, D_SSM))],
                 scratch=[pltpu.VMEM((T, STATE_BLOCK), F32), pltpu.VMEM((T, STATE_BLOCK), F32)],
                 sem=('arbitrary',), vmem=VMEM_MOST, ride=ride)(u_perm, b_re, b_im, c_re, c_im, lam_r, lam_i)


def _ssm_bwd(dy_perm, u_perm, s_re, s_im, b_re, b_im, c_re, c_im, lam_r, lam_i, ride):
    T = u_perm.shape[0]
    ls = T // SUBLANES
    rc = min(512, T)
    sp = _scan_specs(T)
    ncb = N_STATE // STATE_BLOCK

    def body(dy_ref, u_ref, sre_ref, sim_ref, bre_ref, bim_ref, cre_ref, cim_ref, lr_ref, li_ref,
             du_ref, dbr_ref, dbi_ref, dcr_ref, dci_ref, dar_ref, dai_ref, gre_ref, gim_ref):
        shp = (SUBLANES, STATE_BLOCK)
        zero = jnp.zeros(shp, F32)
        tail = pl.ds(T, SUBLANES)
        gre_ref[tail, :] = zero
        gim_ref[tail, :] = zero
        for c in range(T // rc):
            rows = pl.ds(c * rc, rc)
            dyb = dy_ref[rows, :].astype(BF16)
            gre_ref[rows, :] = _dot_nt(dyb, cre_ref[...])
            gim_ref[rows, :] = -_dot_nt(dyb, cim_ref[...])
        lr = jnp.broadcast_to(lr_ref[...], shp)
        li = jnp.broadcast_to(li_ref[...], shp)

        def step(k, carry):
            gr, gi = carry
            rows = _rows8(ls - 1 - k)
            nr = lr * gr + li * gi + gre_ref[rows, :]
            ni = lr * gi - li * gr + gim_ref[rows, :]
            gre_ref[rows, :] = nr
            gim_ref[rows, :] = ni
            return nr, ni

        fr, fi = _scan_loop(ls, step, (zero, zero))
        pr, pi_ = _complex_power(lr, -li, ls)
        row = lax.broadcasted_iota(jnp.int32, shp, 0)
        cr, ci = zero, zero
        for _ in range(SUBLANES - 1):
            er = fr + pr * cr - pi_ * ci
            ei = fi + pr * ci + pi_ * cr
            cr = jnp.where(row == SUBLANES - 1, 0.0, pltpu.roll(er, SUBLANES - 1, 0))
            ci = jnp.where(row == SUBLANES - 1, 0.0, pltpu.roll(ei, SUBLANES - 1, 0))

        def fix(k, carry):
            dr, di = carry
            rows = _rows8(ls - 1 - k)
            dr, di = lr * dr + li * di, lr * di - li * dr
            gre_ref[rows, :] += dr
            gim_ref[rows, :] += di
            return dr, di

        _scan_loop(ls, fix, (cr, ci))

        acc_r = jnp.zeros((1, STATE_BLOCK), F32)
        acc_i = jnp.zeros((1, STATE_BLOCK), F32)
        for c in range(T // rc):
            rows, nxt = pl.ds(c * rc, rc), pl.ds(c * rc + SUBLANES, rc)
            s_r, s_i = sre_ref[rows, :].astype(F32), sim_ref[rows, :].astype(F32)
            g_r, g_i = gre_ref[nxt, :], gim_ref[nxt, :]
            acc_r = acc_r + _colsum(g_r * s_r + g_i * s_i)
            acc_i = acc_i + _colsum(g_i * s_r - g_r * s_i)
        last = pl.ds(T - 2 * SUBLANES, 2 * SUBLANES)
        first = pl.ds(0, SUBLANES)
        spr = jnp.where(row == 0, 0.0, pltpu.roll(sre_ref[last, :].astype(F32)[SUBLANES:], 1, 0))
        spi = jnp.where(row == 0, 0.0, pltpu.roll(sim_ref[last, :].astype(F32)[SUBLANES:], 1, 0))
        gr, gi = gre_ref[first, :], gim_ref[first, :]
        dar_ref[...] = acc_r + _colsum(gr * spr + gi * spi)
        dai_ref[...] = acc_i + _colsum(gi * spr - gr * spi)

        for c in range(T // rc):
            rows = pl.ds(c * rc, rc)
            g_r, g_i = gre_ref[rows, :].astype(BF16), gim_ref[rows, :].astype(BF16)
            s_r, s_i = sre_ref[rows, :], sim_ref[rows, :]
            ub, dyb = u_ref[rows, :].astype(BF16), dy_ref[rows, :].astype(BF16)
            du_ref[rows, :] = _dot_nt(g_r, bre_ref[...]) + _dot_nt(g_i, bim_ref[...])
            parts = (_dot_tn(ub, g_r), _dot_tn(ub, g_i), _dot_tn(s_r, dyb), -_dot_tn(s_i, dyb))
            outs = (dbr_ref, dbi_ref, dcr_ref, dci_ref)
            for o_ref, part in zip(outs, parts):
                if c == 0:
                    o_ref[...] = part
                else:
                    o_ref[...] += part

    blk = lambda r, c: pl.BlockSpec((None, r, c), lambda cb: (cb, 0, 0))
    return _call(body, name='ssm_bwd', grid=(ncb,),
                 in_specs=[sp['chan'], sp['chan'], sp['state'], sp['state'], sp['b'], sp['b'], sp['c'], sp['c'],
                           sp['lam'], sp['lam']],
                 out_specs=[sp['chan'], blk(CHAN_BLOCK, STATE_BLOCK), blk(CHAN_BLOCK, STATE_BLOCK),
                            blk(STATE_BLOCK, CHAN_BLOCK), blk(STATE_BLOCK, CHAN_BLOCK), blk(1, STATE_BLOCK),
                            blk(1, STATE_BLOCK)],
                 out_shape=[_sds((T, D_SSM)), _sds((ncb, CHAN_BLOCK, STATE_BLOCK)), _sds((ncb, CHAN_BLOCK, STATE_BLOCK)),
                            _sds((ncb, STATE_BLOCK, CHAN_BLOCK)), _sds((ncb, STATE_BLOCK, CHAN_BLOCK)),
                            _sds((ncb, 1, STATE_BLOCK)), _sds((ncb, 1, STATE_BLOCK))],
                 scratch=[pltpu.VMEM((T + SUBLANES, STATE_BLOCK), F32), pltpu.VMEM((T + SUBLANES, STATE_BLOCK), F32)],
                 sem=('arbitrary',), vmem=VMEM_MOST, ride=ride)(dy_perm, u_perm, s_re, s_im, b_re, b_im, c_re, c_im,
                                                                lam_r, lam_i)


def _ffn_dact(ddn, wd4, hid4, tm):
    T = ddn.shape[0]

    def body(d_ref, w_ref, hid_ref, o_ref, act_ref):
        dact = _dot_nt(d_ref[...], w_ref[...])
        silu, dsilu = _silu_parts(hid_ref[0].astype(F32))
        hid_v = hid_ref[1].astype(F32)
        o_ref[0] = (dact * hid_v * dsilu).astype(BF16)
        o_ref[1] = (dact * silu).astype(BF16)
        act_ref[...] = (silu * hid_v).astype(BF16)

    blk = pl.BlockSpec((2, None, tm, FF_SHARD), lambda i, j: (0, j, i, 0))
    return _call(body, name='ffn_dact', grid=(T // tm, 4),
                 in_specs=[pl.BlockSpec((tm, D_MODEL), lambda i, j: (i, 0)),
                           pl.BlockSpec((None, FF_SHARD, D_MODEL), lambda i, j: (j, 0, 0)), blk],
                 out_specs=[blk, pl.BlockSpec((None, tm, FF_SHARD), lambda i, j: (j, i, 0))],
                 out_shape=[_sds((2, 4, T, FF_SHARD), BF16), _sds((4, T, FF_SHARD), BF16)],
                 sem=('parallel', 'parallel'))(ddn, wd4, hid4)


def _ffn_dup(dhid8, up8, cw8, tm, ride):
    T = up8.shape[1]
    nb = T // tm
    ha = _halo_after(tm, T, HALO16)

    def body(dh_ref, dha_ref, up_ref, cw_ref, dup_ref, dcw_ref):
        i = pl.program_id(1)

        @pl.when(i == 0)
        def _():
            dcw_ref[...] = jnp.zeros_like(dcw_ref)

        dh = dh_ref[...].astype(F32)
        dup, dh1, dh2 = _conv3_t(dh, jnp.where(i < nb - 1, dha_ref[...].astype(F32), 0.0), cw_ref)
        dup_ref[...] = dup.astype(BF16)
        up = up_ref[...].astype(F32)
        dcw_ref[0:1, :] += _colsum(dh2 * up)
        dcw_ref[1:2, :] += _colsum(dh1 * up)
        dcw_ref[2:3, :] += _colsum(dh * up)

    main = pl.BlockSpec((None, tm, FF_SHARD), lambda j, i: (j, i, 0))
    return _call(body, name='ffn_dup', grid=(N_DEV, nb),
                 in_specs=[main, pl.BlockSpec((None, HALO16, FF_SHARD), lambda j, i: (j, ha(i), 0)), main,
                           pl.BlockSpec((None, 3, FF_SHARD), lambda j, i: (j, 0, 0))],
                 out_specs=[main, pl.BlockSpec((None, 8, FF_SHARD), lambda j, i: (j, 0, 0))],
                 out_shape=[_sds((N_DEV, T, FF_SHARD), BF16), _sds((N_DEV, 8, FF_SHARD))],
                 sem=('parallel', 'arbitrary'), ride=ride)(dhid8, dhid8, up8, cw8)


def _grad_tn(a, b, a_spec, b_spec, groups, m, n, tk, name, ride=None, parts=1):
    T = a.shape[-2]
    nk = T // tk
    mp = m // parts

    def body(a_ref, b_ref, *refs):
        o_refs, acc_ref = refs[:parts], refs[parts]
        k = pl.program_id(1)
        part = _dot_tn(a_ref[...], b_ref[...])

        @pl.when(k == 0)
        def _():
            acc_ref[...] = part

        @pl.when(k > 0)
        def _():
            acc_ref[...] += part

        @pl.when(k == nk - 1)
        def _():
            for p, o_ref in enumerate(o_refs):
                o_ref[...] = acc_ref[p * mp:(p + 1) * mp, :].astype(BF16)

    out_spec = pl.BlockSpec((None, mp, n), lambda g, k: (g, 0, 0))
    res = _call(body, name=name, grid=(groups, nk), in_specs=[a_spec, b_spec], out_specs=[out_spec] * parts,
                out_shape=[_sds((groups, mp, n), BF16)] * parts, scratch=[pltpu.VMEM((m, n), F32)],
                sem=('parallel', 'arbitrary'), vmem=VMEM_BIG, ride=ride)(a, b)
    if parts > 1:
        return res
    return res[0] if ride is None else (res[0][0], res[1])


def _grad_w_in(h1, dproj, tk, ride):
    T = h1.shape[0]
    nk = T // tk
    half = D_IN_PROJ // 2

    def body(a_ref, b_ref, o_ref, acc_ref):
        k = pl.program_id(0)
        for h in range(2):
            cols = slice(h * half, (h + 1) * half)
            part = _dot_tn(a_ref[...], b_ref[:, cols])

            @pl.when(k == 0)
            def _():
                acc_ref[:, cols] = part

            @pl.when(k > 0)
            def _():
                acc_ref[:, cols] += part

        @pl.when(k == nk - 1)
        def _():
            for g in range(N_DEV):
                o_ref[g] = acc_ref[:, g * IN_SHARD:(g + 1) * IN_SHARD].astype(BF16)

    return _call(body, name='grad_w_in', grid=(nk,),
                 in_specs=[pl.BlockSpec((tk, D_MODEL), lambda k: (k, 0)), pl.BlockSpec((tk, D_IN_PROJ), lambda k: (k, 0))],
                 out_specs=_const((N_DEV, D_MODEL, IN_SHARD)), out_shape=_sds((N_DEV, D_MODEL, IN_SHARD), BF16),
                 scratch=[pltpu.VMEM((D_MODEL, D_IN_PROJ), F32)], sem=('arbitrary',), vmem=VMEM_BIG, ride=ride)(h1, dproj)


def _pre_norm_bwd(dz, dz_spec, w_s, xin, dres, sc, g, tm, name, ride, below=None, group=1, w_t=False):
    T = xin.shape[0]
    n = w_s.shape[1] if w_t else w_s.shape[2]
    mul = _dot if w_t else _dot_nt
    steps = N_DEV // group

    def body(dz_ref, w_ref, x_ref, dr_ref, sc_ref, g_ref, *refs):
        if below is None:
            dx_ref, dsh_ref, dsc_ref, dg_ref = refs
            sums = (dsh_ref, dsc_ref, dg_ref)
        else:
            v_ref, gate_ref, g2_ref, dx_ref, dsh_ref, dsc_ref, dg_ref, dv_ref, dgate_ref, dg2_ref = refs
            sums = (dsh_ref, dsc_ref, dg_ref, dgate_ref, dg2_ref)
        i, j = pl.program_id(0), pl.program_id(1)
        piece = (lambda s: dz_ref[s]) if dz.ndim == 3 else (lambda s: dz_ref[:, s * n:(s + 1) * n])
        part = mul(piece(0), w_ref[0])
        for s in range(1, group):
            part = part + mul(piece(s), w_ref[s])

        @pl.when(jnp.logical_and(i == 0, j == 0))
        def _():
            for s_ref in sums:
                s_ref[...] = jnp.zeros_like(s_ref)

        @pl.when(j == 0)
        def _():
            dx_ref[...] = part

        @pl.when(j > 0)
        def _():
            dx_ref[...] += part

        @pl.when(j == steps - 1)
        def _():
            dh, xv, gv = dx_ref[...], x_ref[...], g_ref[...]
            r = _rsqrt_mean(xv)
            dsh_ref[...] += _colsum(dh)
            dsc_ref[...] += _colsum(dh * (xv * r * gv))
            dxn = dh * (1.0 + sc_ref[...])
            dg_ref[...] += _colsum(dxn * xv * r)
            dx = dr_ref[...] + _norm_bwd(dxn, xv, r, gv)
            dx_ref[...] = dx
            if below is not None:
                v, g2 = v_ref[...], g2_ref[...]
                rv = _rsqrt_mean(v)
                dgate_ref[...] += _colsum(dx * (v * rv * g2))
                dn = dx * gate_ref[...]
                dg2_ref[...] += _colsum(dn * v * rv)
                dv_ref[...] = _norm_bwd(dn, v, rv, g2).astype(BF16)

    row = pl.BlockSpec((tm, D_MODEL), lambda i, j: (i, 0))
    vec = _const((1, D_MODEL))
    in_specs = [dz_spec, pl.BlockSpec((group,) + w_s.shape[1:], lambda i, j: (j, 0, 0)), row, row, vec, vec]
    out_specs = [row, vec, vec, vec]
    out_shape = [_sds((T, D_MODEL)), _sds((1, D_MODEL)), _sds((1, D_MODEL)), _sds((1, D_MODEL))]
    args = [dz, w_s, xin, dres, sc, g]
    if below is not None:
        in_specs += [row, vec, vec]
        out_specs += [row, vec, vec]
        out_shape += [_sds((T, D_MODEL), BF16), _sds((1, D_MODEL)), _sds((1, D_MODEL))]
        args += list(below)
    return _call(body, name=name, grid=(T // tm, steps), in_specs=in_specs, out_specs=out_specs,
                 out_shape=out_shape, sem=('arbitrary', 'arbitrary'), vmem=VMEM_MOST, ride=ride)(*args)


def _mix_bwd(d_o, w_out, yssm, proj, d, glu_w, glu_b, g_ssm, cw, g_conv, avg16, avg64, tm, ride):
    T = yssm.shape[0]
    hb = _halo_before(tm)

    def body(do_ref, wo_ref, y_ref, p_ref, ph_ref, d_ref, gw_ref, gb_ref, gs_ref, cw_ref, gc_ref, a16_ref, a64_ref,
             dy_ref, dconv_ref, dbg_ref, z_ref, dlin_ref, acc_ref):
        i = pl.program_id(0)
        dyc = _dot_nt(do_ref[...], wo_ref[...])

        @pl.when(i == 0)
        def _():
            acc_ref[...] = jnp.zeros_like(acc_ref)

        u = p_ref[:, 0:D_SSM]
        y = y_ref[...] + d_ref[...] * u
        z, t = _gelu(y)
        gate = _sigmoid(_dot(z.astype(BF16), gw_ref[...]) + gb_ref[...])
        ya = z * gate
        rs = lax.rsqrt(_dot_split(ya * ya, a16_ref[...], 2) + EPS)
        dna = dyc[:, 0:D_SSM]
        acc_ref[1:2, :] += _colsum(dna * ya * rs)
        dya = _head_norm_bwd(dna, ya, rs, gs_ref[...], a16_ref[...])
        dlin = dya * z * gate * (1.0 - gate)
        acc_ref[0:1, :] += _colsum(dlin)
        dlin_b = dlin.astype(BF16)
        dz = dya * gate + _dot_nt(dlin_b, gw_ref[...])
        dy = dz * _gelu_grad(y, t)
        acc_ref[3:4, :] += _colsum(dy * u)
        dy_ref[...] = dy
        z_ref[...] = z.astype(BF16)
        dlin_ref[...] = dlin_b

        bg = p_ref[:, D_SSM:D_SSM + D_CONV]
        cv = p_ref[:, D_SSM + D_CONV:D_SSM + 2 * D_CONV] * p_ref[:, D_SSM + 2 * D_CONV:D_IN_PROJ]
        hv = ph_ref[:, D_SSM + D_CONV:D_SSM + 2 * D_CONV] * ph_ref[:, D_SSM + 2 * D_CONV:D_IN_PROJ]
        hv = jnp.where(i > 0, hv, 0.0)
        conv, cv1, cv2 = _conv3(cv, hv, cw_ref)
        yb = bg * conv
        rsb = lax.rsqrt(_dot_split(yb * yb, a64_ref[...], 2) + EPS)
        dnb = dyc[:, D_SSM:D_MODEL]
        acc_ref[2:3, :] += _colsum(dnb * yb * rsb)
        dyb = _head_norm_bwd(dnb, yb, rsb, gc_ref[...], a64_ref[...])
        dbg_ref[...] = dyb * conv
        dconv = dyb * bg
        dconv_ref[...] = dconv
        acc_ref[4:5, :] += _colsum(dconv * cv2)
        acc_ref[5:6, :] += _colsum(dconv * cv1)
        acc_ref[6:7, :] += _colsum(dconv * cv)

    vec = _const((1, D_SSM))
    sq = _const((D_SSM, D_SSM))
    half = pl.BlockSpec((tm, D_SSM), lambda i: (i, 0))
    return _call(body, name='mix_bwd', grid=(T // tm,),
                 in_specs=[pl.BlockSpec((tm, D_MODEL), lambda i: (i, 0)), _const((D_MODEL, D_MODEL)), half,
                           pl.BlockSpec((tm, D_IN_PROJ), lambda i: (i, 0)),
                           pl.BlockSpec((HALO, D_IN_PROJ), lambda i: (hb(i), 0)), vec, sq, vec, vec,
                           _const((3, D_CONV)), vec, sq, sq],
                 out_specs=[half, half, half, half, half, _const((8, D_SSM))],
                 out_shape=[_sds((T, D_SSM)), _sds((T, D_SSM)), _sds((T, D_SSM)), _sds((T, D_SSM), BF16),
                            _sds((T, D_SSM), BF16), _sds((8, D_SSM))],
                 sem=('arbitrary',), vmem=VMEM_BIG, ride=ride)(d_o, w_out, yssm, proj, proj, d, glu_w, glu_b, g_ssm, cw,
                                                              g_conv, avg16, avg64)


def _mix_bwd_proj(dconv, proj, du_ssm, dy, d, dbg, cw, tm):
    T = dy.shape[0]
    nb = T // tm
    ha = _halo_after(tm, T)

    def body(dc_ref, dch_ref, cg_ref, v_ref, du_ref, dy_ref, d_ref, dbg_ref, cw_ref, o_ref):
        i = pl.program_id(0)
        dcv = _conv3_t(dc_ref[...], jnp.where(i < nb - 1, dch_ref[...], 0.0), cw_ref)[0]
        o_ref[:, 0:D_SSM] = (du_ref[...] + dy_ref[...] * d_ref[...]).astype(BF16)
        o_ref[:, D_SSM:D_SSM + D_CONV] = dbg_ref[...].astype(BF16)
        o_ref[:, D_SSM + D_CONV:D_SSM + 2 * D_CONV] = (dcv * v_ref[...]).astype(BF16)
        o_ref[:, D_SSM + 2 * D_CONV:D_IN_PROJ] = (dcv * cg_ref[...]).astype(BF16)

    half = pl.BlockSpec((tm, D_SSM), lambda i: (i, 0))
    return _call(body, name='mix_bwd_proj', grid=(nb,),
                 in_specs=[half, pl.BlockSpec((HALO, D_CONV), lambda i: (ha(i), 0)),
                           pl.BlockSpec((tm, D_CONV), lambda i: (i, 2)), pl.BlockSpec((tm, D_CONV), lambda i: (i, 3)),
                           half, half, _const((1, D_SSM)), half, _const((3, D_CONV))],
                 out_specs=pl.BlockSpec((tm, D_IN_PROJ), lambda i: (i, 0)), out_shape=_sds((T, D_IN_PROJ), BF16),
                 sem=('parallel',))(dconv, dconv, proj, proj, du_ssm, dy, d, dbg, cw)


def _row_tile(rows, cols, slots):
    for cand in (512, 256, 128, 64, 32, 16, 8):
        if rows % cand == 0 and slots * cand * cols * 4 <= (2 << 20):
            return cand
    return rows


def _adamw_math(g, w, m, v):
    m2 = ADAM_B1 * m + (1.0 - ADAM_B1) * g
    v2 = ADAM_B2 * v + (1.0 - ADAM_B2) * (g * g)
    m_hat = m2 / (1.0 - ADAM_B1 ** ADAM_STEP)
    v_hat = v2 / (1.0 - ADAM_B2 ** ADAM_STEP)
    return -ADAM_LR * (m_hat / (jnp.sqrt(v_hat) + ADAM_EPS) + ADAM_WD * w), m2, v2


def _adamw(pieces, w, m, v, name):
    slots, _, cols = pieces[0].shape
    rows = sum(p.shape[1] for p in pieces)
    tr = _row_tile(pieces[0].shape[1], cols, slots)
    starts, pos = [], 0
    for p in pieces:
        assert p.shape[1] % tr == 0
        starts.append(pos)
        pos += p.shape[1] // tr

    def body(*refs):
        g_refs = refs[:len(pieces)]
        w_ref, m_ref, v_ref, go_ref, d_ref, mo_ref, vo_ref = refs[len(pieces):]
        i = pl.program_id(0)
        g = None
        for g_ref, start in zip(g_refs, starts):
            part = g_ref[0].astype(F32)
            for s in range(1, slots):
                part = part + g_ref[s].astype(F32)
            g = part if g is None else jnp.where(i >= start, part, g)
        go_ref[...] = g
        d_ref[...], mo_ref[...], vo_ref[...] = _adamw_math(g, w_ref[...], m_ref[...], v_ref[...])

    def piece_spec(start, count):
        return pl.BlockSpec((slots, tr, cols), lambda i: (0, jnp.clip(i - start, 0, count - 1), 0))

    blk = pl.BlockSpec((tr, cols), lambda i: (i, 0))
    return _call(body, name=name, grid=(rows // tr,),
                 in_specs=[piece_spec(s, p.shape[1] // tr) for s, p in zip(starts, pieces)] + [blk, blk, blk],
                 out_specs=[blk] * 4, out_shape=[_sds((rows, cols))] * 4, sem=('parallel',))(*pieces, w, m, v)


def _to_scan_rows(a):
    T, n = a.shape
    return a.reshape(SUBLANES, T // SUBLANES, n).transpose(1, 0, 2).reshape(T, n)


def _from_scan_rows(a):
    T, n = a.shape
    return a.reshape(T // SUBLANES, SUBLANES, n).transpose(1, 0, 2).reshape(T, n)


def _expand(a):
    return jnp.repeat(a, SSM_GROUP, axis=1)


def _block_diag(rows, row_group, col_group):
    r, n = rows.shape
    tiled = jnp.tile(rows, (1, N_GROUPS))
    keep = (jnp.arange(r)[:, None] // row_group) == (jnp.arange(n * N_GROUPS)[None, :] // col_group)
    return jnp.where(keep, tiled, 0.0)


def _block_diag_b(bb):
    return _block_diag(bb.transpose(0, 2, 1).reshape(D_SSM, SSM_STATE), SSM_GROUP, SSM_STATE)


def _block_diag_c(cc):
    return _block_diag(cc.transpose(0, 2, 1).reshape(N_STATE, SSM_GROUP), SSM_STATE, SSM_GROUP)


def _diag_blocks(x, chan_major):
    per = CHAN_BLOCK // SSM_GROUP
    eye = jnp.eye(per, dtype=x.dtype)
    if chan_major:
        x = x.reshape(-1, per, SSM_GROUP, per, SSM_STATE) * eye[None, :, None, :, None]
        return x.sum(axis=1).transpose(0, 2, 3, 1).reshape(N_GROUPS, SSM_STATE, SSM_GROUP)
    x = x.reshape(-1, per, SSM_STATE, per, SSM_GROUP) * eye[None, :, None, :, None]
    return x.sum(axis=3).reshape(N_GROUPS, SSM_STATE, SSM_GROUP)


SMALL_LAYOUT = {
    'ssm_b_re': (0, 0, 32, 1024), 'ssm_b_im': (32, 0, 32, 1024), 'ssm_c_re': (64, 0, 32, 1024),
    'ssm_c_im': (96, 0, 32, 1024), 'b_ada': (128, 0, 6, 1024), 'g_pre_mix': (134, 0, 1, 1024),
    'g_post_mix': (135, 0, 1, 1024), 'ssm_lam_re': (136, 0, 2, 1024), 'ssm_lam_im': (138, 0, 2, 1024),
    'ssm_log_step': (140, 0, 1, 32), 'glu_b': (141, 0, 1, 512), 'g_out_ssm': (141, 512, 1, 512),
    'g_out_conv': (142, 0, 1, 512), 'ssm_d': (142, 512, 1, 512), 'g_pre_ffn': (143, 0, 1, 1024),
    'g_post_ffn': (144, 0, 1, 1024)}
SMALL_ROWS = 152
B_ADA_ROW = SMALL_LAYOUT['b_ada'][0]
LATE_ROWS = {('b_ada', 0): 0, ('b_ada', 1): 1, ('g_pre_mix', 0): 2}


def _adamw_small(gathered, late, wts, mom_m, mom_v):
    names = list(SMALL_LAYOUT)
    n = len(names)

    def body(*refs):
        g_ref, late_ref, ins, outs = refs[0], refs[1], refs[2:2 + 3 * n], refs[2 + 3 * n:]
        for p, name in enumerate(names):
            r0, c0, rows, cols = SMALL_LAYOUT[name]
            pieces = [(0, rows)] if rows % 8 == 0 else [(r, 1) for r in range(rows)]
            for r, cnt in pieces:
                src_ref, first = (late_ref, LATE_ROWS[name, r]) if (name, r) in LATE_ROWS else (g_ref, r0 + r)
                g = src_ref[0, first:first + cnt, c0:c0 + cols]
                for s in range(1, N_DEV):
                    g = g + src_ref[s, first:first + cnt, c0:c0 + cols]
                w, m, v = (ins[3 * p + q][r:r + cnt, :] for q in range(3))
                res = (g,) + _adamw_math(g, w, m, v)
                for q in range(4):
                    outs[4 * p + q][r:r + cnt, :] = res[q]

    shapes = [SMALL_LAYOUT[name][2:] for name in names]
    args = [gathered, late]
    for name, shp in zip(names, shapes):
        args += [wts[name].reshape(shp), mom_m[name].reshape(shp), mom_v[name].reshape(shp)]
    outs = _call(body, name='adamw_small', grid=(1,),
                 in_specs=[_const(gathered.shape), _const(late.shape)]
                 + [_const(shp) for shp in shapes for _ in range(3)],
                 out_specs=[_const(shp) for shp in shapes for _ in range(4)],
                 out_shape=[_sds(shp) for shp in shapes for _ in range(4)], vmem=VMEM_BIG)(*args)
    res = {}
    for p, name in enumerate(names):
        for q, kind in enumerate(('g', 'd', 'm', 'v')):
            res[kind, name] = outs[4 * p + q].reshape(wts[name].shape)
    return res


def kernel(x, c, w_ada, b_ada, g_pre_mix, g_post_mix, w_in, ssm_lam_re, ssm_lam_im, ssm_log_step, ssm_b_re, ssm_b_im, ssm_c_re, ssm_c_im, ssm_d, glu_w, glu_b, g_out_ssm, conv_w, g_out_conv, w_out, g_pre_ffn, g_post_ffn, w_up, ffn_conv_w, w_down, loss_target, m_w_ada, m_b_ada, m_g_pre_mix, m_g_post_mix, m_w_in, m_ssm_lam_re, m_ssm_lam_im, m_ssm_log_step, m_ssm_b_re, m_ssm_b_im, m_ssm_c_re, m_ssm_c_im, m_ssm_d, m_glu_w, m_glu_b, m_g_out_ssm, m_conv_w, m_g_out_conv, m_w_out, m_g_pre_ffn, m_g_post_ffn, m_w_up, m_ffn_conv_w, m_w_down, v_w_ada, v_b_ada, v_g_pre_mix, v_g_post_mix, v_w_in, v_ssm_lam_re, v_ssm_lam_im, v_ssm_log_step, v_ssm_b_re, v_ssm_b_im, v_ssm_c_re, v_ssm_c_im, v_ssm_d, v_glu_w, v_glu_b, v_g_out_ssm, v_conv_w, v_g_out_conv, v_w_out, v_g_pre_ffn, v_g_post_ffn, v_w_up, v_ffn_conv_w, v_w_down):
    args = dict(locals())
    wts = {n: args[n] for n in WEIGHTS}
    mom_m = {n: args['m_' + n] for n in WEIGHTS}
    mom_v = {n: args['v_' + n] for n in WEIGHTS}
    T = x.shape[1]
    tm = min(512, T)
    tw = min(1024, T)
    me = _me()[3]
    xt, tgt = x[0], loss_target[0]

    c_all, w_in_s = _exchange([c, w_in[0].astype(BF16)], name='gather_first', scatter=False)
    c_all = c_all.reshape(N_DEV, D_MODEL)
    b_cols = lax.dynamic_slice(b_ada, (0, me * ADA_SHARD), (1, ADA_SHARD))
    mod_cols, c_act = _mod_cols(c_all, w_ada[0], b_cols)
    (mod_all,) = _exchange([mod_cols], name='gather_mod', scatter=False)
    mod = lax.dynamic_slice(mod_all, (0, me, 0), (N_DEV, 1, ADA_SHARD)).reshape(N_MOD, 1, D_MODEL)
    sh1, sc1, gt1, sh2, sc2, gt2 = [mod[k] for k in range(N_MOD)]


    lre_x, lim_x = _expand(ssm_lam_re[0]), _expand(ssm_lam_im[0])
    lst_x = jnp.broadcast_to(ssm_log_step[0][:, None], (N_GROUPS, SSM_STATE * SSM_GROUP))
    b_re_x = ssm_b_re[0].reshape(N_GROUPS, -1)
    b_im_x = ssm_b_im[0].reshape(N_GROUPS, -1)
    ar_x, ai_x, bbr_x, bbi_x = _ssm_prep(lre_x, lim_x, lst_x, b_re_x, b_im_x)
    lam_r = ar_x[:, ::SSM_GROUP].reshape(1, N_STATE)
    lam_i = ai_x[:, ::SSM_GROUP].reshape(1, N_STATE)
    big_b_re = _block_diag_b(bbr_x.reshape(N_GROUPS, SSM_STATE, SSM_GROUP)).astype(BF16)
    big_b_im = _block_diag_b(bbi_x.reshape(N_GROUPS, SSM_STATE, SSM_GROUP)).astype(BF16)
    big_c_re = _block_diag_c(ssm_c_re[0]).astype(BF16)
    big_c_im = _block_diag_c(ssm_c_im[0]).astype(BF16)
    head = jnp.arange(D_SSM)
    avg16 = jnp.where(head[:, None] // SSM_GROUP == head[None, :] // SSM_GROUP, 1.0 / SSM_GROUP, 0.0).astype(BF16)
    hd = D_CONV // CONV_HEADS
    avg64 = jnp.where(head[:, None] // hd == head[None, :] // hd, 1.0 / hd, 0.0).astype(BF16)

    (proj, h1), (w_down_s, ffn_conv_s, glu_s, w_out_s, conv_s) = _pre_mix(
        xt, sc1, sh1, g_pre_mix, w_in_s, tw,
        ([w_down[0].astype(BF16), ffn_conv_w[0], glu_w[0].astype(BF16), w_out[0].astype(BF16), conv_w[0]], False))
    glu_full = glu_s.reshape(D_SSM, D_SSM)
    w_out_full = w_out_s.reshape(D_MODEL, D_MODEL)
    cw_full = conv_s.transpose(1, 0, 2).reshape(3, D_CONV)
    wd4 = w_down_s.reshape(4, FF_SHARD, D_MODEL)
    u_perm = _to_scan_rows(proj[:, :D_SSM])
    (s_re, s_im, y_perm), (w_up_s,) = _ssm_fwd(u_perm, big_b_re, big_b_im, big_c_re, big_c_im, lam_r, lam_i,
                                               ([w_up[0].T.astype(BF16)], False))
    yssm = _from_scan_rows(y_perm)
    mix_args = (ssm_d, glu_full, glu_b, g_out_ssm, cw_full, g_out_conv, avg16, avg64)
    ycat = _mix_fwd(yssm, proj, *mix_args, tm)
    o, x1, h2 = _out_proj(ycat, w_out_full, xt, gt1, g_post_mix, g_pre_ffn, sc2, sh2, tm)
    up8, hid8 = _ffn_up(h2, w_up_s, ffn_conv_s, tw)
    hid4 = hid8.reshape(2, 4, T, FF_SHARD)
    ddn, dx2, loss_parts, d_gt2, d_g_post_ffn = _ffn_down(hid4, wd4, x1, tgt, gt2, g_post_ffn, tm)
    loss_local = jnp.sum(loss_parts[:, 0, 0])

    got = {}
    dhid, act = _ffn_dact(ddn, wd4, hid4, tm)
    g_w_down = _grad_tn(act, ddn, pl.BlockSpec((None, tw, FF_SHARD), lambda g, k: (g, k, 0)),
                        pl.BlockSpec((tw, D_MODEL), lambda g, k: (k, 0)), 4, FF_SHARD, D_MODEL, tw, 'grad_w_down')
    (dup8, dcw_ffn), (got['w_down'],) = _ffn_dup(dhid.reshape(N_DEV, T, FF_SHARD), up8, ffn_conv_s, tm,
                                                 ([g_w_down.reshape(N_DEV, D_FF // N_DEV, D_MODEL)], True))
    g_w_up_halves = _grad_tn(dup8, h2, pl.BlockSpec((None, tw, FF_SHARD), lambda g, k: (g, k, 0)),
                             pl.BlockSpec((tw, D_MODEL), lambda g, k: (k, 0)), N_DEV, FF_SHARD, D_MODEL, tw,
                             'grad_w_up', parts=2)
    (dx1, d_sh2, d_sc2, d_g_pre_ffn, d_o, d_gt1, d_g_post_mix), (got_up_0, got['ffn_conv_w']) = _pre_norm_bwd(
        dup8, pl.BlockSpec((2, tw, FF_SHARD), lambda i, j: (j, i, 0)), w_up_s, x1, dx2, sc2, g_pre_ffn, tw,
        'ffn_in_bwd', ([g_w_up_halves[0], dcw_ffn], True), below=(o, gt1, g_post_mix), group=2, w_t=True)

    g_w_out = _grad_tn(ycat, d_o, pl.BlockSpec((tw, D_MODEL), lambda g, k: (k, 0)),
                       pl.BlockSpec((tw, D_MODEL), lambda g, k: (k, 0)), 1, D_MODEL, D_MODEL, tw, 'grad_w_out')
    (dy, dconv, dbg, z_b, dlin_b, sums), (got['w_out'],) = _mix_bwd(
        d_o, w_out_full, yssm, proj, *mix_args, tm, ([g_w_out.reshape(N_DEV, D_MODEL // N_DEV, D_MODEL)], True))
    g_glu_w = _grad_tn(z_b, dlin_b, pl.BlockSpec((tw, D_SSM), lambda g, k: (k, 0)),
                       pl.BlockSpec((tw, D_SSM), lambda g, k: (k, 0)), 1, D_SSM, D_SSM, tw, 'grad_glu_w')
    dy_perm = _to_scan_rows(dy)
    (du_perm, dbr_blk, dbi_blk, dcr_blk, dci_blk, dar_blk, dai_blk), (got_up_1, got['glu_w']) = _ssm_bwd(
        dy_perm, u_perm, s_re, s_im, big_b_re, big_b_im, big_c_re, big_c_im, lam_r, lam_i,
        ([g_w_up_halves[1], g_glu_w.reshape(N_DEV, D_SSM // N_DEV, D_SSM)], True))
    du_ssm = _from_scan_rows(du_perm)
    dproj = _mix_bwd_proj(dconv, proj, du_ssm, dy, ssm_d, dbg, cw_full, tm)
    dbb_re = _diag_blocks(dbr_blk, True).reshape(N_GROUPS, -1)
    dbb_im = _diag_blocks(dbi_blk, True).reshape(N_GROUPS, -1)
    d_c_re = _diag_blocks(dcr_blk, False).transpose(0, 2, 1)
    d_c_im = _diag_blocks(dci_blk, False).transpose(0, 2, 1)
    lane = jnp.arange(SSM_STATE * SSM_GROUP)
    seg = jnp.where(lane[:, None] // SSM_GROUP == lane[None, :] // SSM_GROUP, 1.0, 0.0).astype(BF16)
    d_b_re_x, d_b_im_x, d_lre_x, d_lim_x, d_lst = _ssm_prep_bwd(
        lre_x, lim_x, lst_x, b_re_x, b_im_x, dbb_re, dbb_im, _expand(dar_blk.reshape(N_GROUPS, SSM_STATE)),
        _expand(dai_blk.reshape(N_GROUPS, SSM_STATE)), seg)

    row = lambda a: a.reshape(-1, PACK_COLS)
    blank = jnp.zeros((1, PACK_COLS), F32)
    small_pack = jnp.concatenate([
        d_b_re_x, d_b_im_x, row(d_c_re), row(d_c_im), blank, blank, d_gt1, d_sh2, d_sc2, d_gt2, blank,
        d_g_post_mix, row(d_lre_x[:, ::SSM_GROUP]), row(d_lim_x[:, ::SSM_GROUP]),
        jnp.pad(d_lst.reshape(1, N_GROUPS), ((0, 0), (0, PACK_COLS - N_GROUPS))), row(sums[0:4]), d_g_pre_ffn,
        d_g_post_ffn, jnp.zeros((SMALL_ROWS - 145, PACK_COLS), F32)])
    g_w_in, (small_all,) = _grad_w_in(h1, dproj, tw, ([small_pack], False))
    g_conv_slots = jnp.concatenate([sums[4:7], jnp.zeros((5, D_CONV), F32)]).reshape(
        8, N_DEV, D_CONV // N_DEV).transpose(1, 0, 2)
    (grad_x, d_sh1, d_sc1, d_g_pre_mix), (got['w_in'], got['conv_w']) = _pre_norm_bwd(
        dproj, pl.BlockSpec((tw, D_IN_PROJ), lambda i, j: (i, j)), w_in_s, xt, dx1, sc1, g_pre_mix, tw,
        'mix_in_bwd', ([g_w_in, g_conv_slots], True), group=N_DEV)
    late_pack = jnp.concatenate([d_sh1, d_sc1, d_g_pre_mix, jnp.full((1, PACK_COLS), loss_local, F32),
                                 jnp.zeros((4, PACK_COLS), F32)])
    (late_all,) = _exchange([late_pack], name='gather_late_grads', scatter=False)
    loss = jnp.sum(late_all[:, 3, 0])
    res = _adamw_small(small_all, late_all, wts, mom_m, mom_v)

    dmod_all = jnp.concatenate([late_all[:, 0:2, :], small_all[:, B_ADA_ROW + 2:B_ADA_ROW + N_MOD, :]],
                               axis=1).reshape(N_DEV, N_MOD * D_MODEL)
    dmod_cols = lax.dynamic_slice(dmod_all, (0, me * ADA_SHARD), (N_DEV, ADA_SHARD))
    g_w_ada = _grad_w_ada(c_act.T, dmod_cols)

    pieces = {n: [slots[:, :3, :] if n in ('conv_w', 'ffn_conv_w') else slots] for n, slots in got.items()}
    for n, parts in pieces.items():
        outs = _adamw(parts, wts[n][0], mom_m[n][0], mom_v[n][0], 'adamw_' + n)
        for kind, val in zip(('g', 'd', 'm', 'v'), outs):
            res[kind, n] = val[None]
    outs = _adamw([got_up_0, got_up_1], w_up[0].T, m_w_up[0].T, v_w_up[0].T, 'adamw_w_up')
    for kind, val in zip(('g', 'd', 'm', 'v'), outs):
        res[kind, 'w_up'] = val.T[None]
    outs = _adamw([g_w_ada[None]], w_ada[0], m_w_ada[0], v_w_ada[0], 'adamw_w_ada')
    for kind, val in zip(('g', 'd', 'm', 'v'), outs):
        res[kind, 'w_ada'] = val[None]

    return (loss, grad_x[None], *[res['g', n] for n in WEIGHTS], *[res['d', n] for n in WEIGHTS],
            *[res['m', n] for n in WEIGHTS], *[res['v', n] for n in WEIGHTS])
```

```python
import math

import jax
import jax.numpy as jnp
from jax import lax
from jax.experimental import pallas as pl
from jax.experimental.pallas import tpu as pltpu

F32, BF16 = jnp.float32, jnp.bfloat16

D_MODEL = 1024
D_SSM = 512
D_CONV = 512
SSM_GROUP = 16
N_GROUPS = 32
SSM_STATE = 64
N_STATE = N_GROUPS * SSM_STATE
CONV_HEADS = 8
D_FF = 2816
N_MOD = 6
D_IN_PROJ = D_SSM + 3 * D_CONV
N_DEV = 8
FF_SHARD = 2 * D_FF // N_DEV
IN_SHARD = D_IN_PROJ // N_DEV
ADA_SHARD = N_MOD * D_MODEL // N_DEV
EPS = 1e-6
LAMBDA_RE_MAX = -1e-4
ADAM_LR, ADAM_B1, ADAM_B2, ADAM_EPS, ADAM_WD, ADAM_STEP = 0.001, 0.9, 0.999, 1e-08, 0.01, 10
GELU_C = math.sqrt(2.0 / math.pi)
GELU_A = 0.044715

SUBLANES = 8
HALO = 8
HALO16 = 16
SCAN_UNROLL = 8
STATE_BLOCK = 512
CHAN_BLOCK = 128
VMEM_BIG = 48 << 20
VMEM_MOST = 58 << 20

WEIGHTS = ['w_ada', 'b_ada', 'g_pre_mix', 'g_post_mix', 'w_in', 'ssm_lam_re', 'ssm_lam_im', 'ssm_log_step',
           'ssm_b_re', 'ssm_b_im', 'ssm_c_re', 'ssm_c_im', 'ssm_d', 'glu_w', 'glu_b', 'g_out_ssm', 'conv_w',
           'g_out_conv', 'w_out', 'g_pre_ffn', 'g_post_ffn', 'w_up', 'ffn_conv_w', 'w_down']
SHARDED = ('w_ada', 'w_in', 'glu_w', 'conv_w', 'w_out', 'w_up', 'ffn_conv_w', 'w_down')
PACK_COLS = 1024


def _call(body, *, name, grid, in_specs, out_specs, out_shape, scratch=(), sem=None, vmem=None, ride=None):
    params = {}
    if vmem is not None:
        params['vmem_limit_bytes'] = vmem
    if ride is None:
        if sem is not None:
            params['dimension_semantics'] = sem
        return pl.pallas_call(body, name=name, grid=grid, in_specs=in_specs, out_specs=out_specs,
                              out_shape=out_shape, scratch_shapes=list(scratch),
                              compiler_params=pltpu.CompilerParams(**params))
    arrs, scatter = ride
    single = not isinstance(out_shape, (list, tuple))
    out_shape_l = [out_shape] if single else list(out_shape)
    out_specs_l = [out_specs] if single else list(out_specs)
    n, n_in, n_out, n_scr = len(arrs), len(in_specs), len(out_shape_l), len(scratch)
    any_spec = pl.BlockSpec(memory_space=pl.ANY)
    params['dimension_semantics'] = ('arbitrary',) * len(grid)

    def carried(*refs):
        ins, rin = refs[:n_in], refs[n_in:n_in + n]
        outs, rout = refs[n_in + n:n_in + n + n_out], refs[n_in + n + n_out:n_in + 2 * n + n_out]
        scr, sems = refs[n_in + 2 * n + n_out:n_in + 2 * n + n_out + n_scr], refs[n_in + 2 * n + n_out + n_scr:]
        first = pl.program_id(0) == 0
        last = pl.program_id(0) == grid[0] - 1
        for ax in range(1, len(grid)):
            first = jnp.logical_and(first, pl.program_id(ax) == 0)
            last = jnp.logical_and(last, pl.program_id(ax) == grid[ax] - 1)

        @pl.when(first)
        def _():
            _exchange_start(rin, rout, sems, scatter)

        body(*ins, *outs, *scr)

        @pl.when(last)
        def _():
            _exchange_wait(rin, rout, sems, scatter)

    call = pl.pallas_call(carried, name=name, grid=grid, in_specs=list(in_specs) + [any_spec] * n,
                          out_specs=out_specs_l + [any_spec] * n,
                          out_shape=out_shape_l + _exchange_shapes(arrs, scatter),
                          scratch_shapes=list(scratch) + _exchange_sems(n),
                          compiler_params=pltpu.CompilerParams(**params))

    def run(*args):
        res = call(*args, *arrs)
        own = res[0] if single else list(res[:n_out])
        return own, list(res[n_out:])

    return run


def _const(shape):
    nd = len(shape)
    return pl.BlockSpec(shape, lambda *_: (0,) * nd)


def _sds(shape, dtype=F32):
    return jax.ShapeDtypeStruct(shape, dtype)


def _dot(a, b):
    return jnp.dot(a, b, preferred_element_type=F32)


def _dot_nt(a, b):
    return lax.dot_general(a, b, (((1,), (1,)), ((), ())), preferred_element_type=F32)


def _dot_tn(a, b):
    return lax.dot_general(a, b, (((0,), (0,)), ((), ())), preferred_element_type=F32)


def _dot_split(x, mat, parts):
    acc = None
    rem = x
    for _ in range(parts):
        piece = rem.astype(BF16)
        rem = rem - piece.astype(F32)
        term = _dot(piece, mat)
        acc = term if acc is None else acc + term
    return acc


def _sigmoid(x):
    return 1.0 / (1.0 + jnp.exp(-x))


def _gelu(x):
    t = jnp.tanh(GELU_C * (x + GELU_A * x * x * x))
    return 0.5 * x * (1.0 + t), t


def _gelu_grad(x, t):
    return 0.5 * (1.0 + t) + 0.5 * x * (1.0 - t * t) * GELU_C * (1.0 + 3.0 * GELU_A * x * x)


def _rsqrt_mean(x):
    return lax.rsqrt(jnp.mean(x * x, axis=-1, keepdims=True) + EPS)


def _colsum(x):
    return jnp.sum(x, axis=0, keepdims=True)


def _shifts_down(x, halo):
    ext = jnp.concatenate([halo, x], axis=0)
    return pltpu.roll(ext, 1, 0)[halo.shape[0]:], pltpu.roll(ext, 2, 0)[halo.shape[0]:]


def _shifts_up(x, halo):
    n = x.shape[0]
    ext = jnp.concatenate([x, halo], axis=0)
    total = ext.shape[0]
    return pltpu.roll(ext, total - 1, 0)[:n], pltpu.roll(ext, total - 2, 0)[:n]


def _conv3(x, halo, w_ref):
    x1, x2 = _shifts_down(x, halo)
    return w_ref[0:1, :] * x2 + w_ref[1:2, :] * x1 + w_ref[2:3, :] * x, x1, x2


def _conv3_t(g, halo, w_ref):
    g1, g2 = _shifts_up(g, halo)
    return w_ref[2:3, :] * g + w_ref[1:2, :] * g1 + w_ref[0:1, :] * g2, g1, g2


def _silu_parts(x):
    s = _sigmoid(x)
    return x * s, s * (1.0 + x * (1.0 - s))


def _norm_bwd(dn, x, r, g):
    gd = g * dn
    return r * gd - x * (r * r * r) * jnp.mean(gd * x, axis=-1, keepdims=True)


def _head_norm_bwd(dn, y, rs, g, avg):
    gd = g * dn
    return rs * gd - y * (rs * rs * rs) * _dot_split(gd * y, avg, 2)


def _me():
    x, y, c = lax.axis_index('x'), lax.axis_index('y'), lax.axis_index('c')
    return x, y, c, 4 * x + 2 * y + c


def _peer(k):
    x, y, c, _ = _me()
    px = 1 - x if k & 4 else x
    py = 1 - y if k & 2 else y
    pc = 1 - c if k & 1 else c
    return (px, py, pc), 4 * px + 2 * py + pc


SIBLING = 1
OTHER_CHIPS = (2, 4, 6)


def _remote(src, dst, sems, a, k, dev):
    return pltpu.make_async_remote_copy(src_ref=src, dst_ref=dst, send_sem=sems[0].at[a, k - 1],
                                        recv_sem=sems[1].at[a, k - 1], device_id=dev,
                                        device_id_type=pl.DeviceIdType.MESH)


def _exchange_copies(ins, outs, sems, scatter):
    me = _me()[3]
    local, first, relay, arrivals = [], [], [], []
    for a in range(len(ins)):
        src = ins[a].at[me] if scatter else ins[a]
        local.append(pltpu.make_async_copy(src, outs[a].at[me], sems[2].at[a]))
        for k in range(1, N_DEV):
            dev, idx = _peer(k)
            landed = _remote(src, outs[a].at[idx], sems, a, k, dev)
            if scatter:
                first.append(_remote(ins[a].at[idx], outs[a].at[me], sems, a, k, dev))
                arrivals.append(landed)
            elif k == SIBLING:
                first.append(_remote(src, outs[a].at[me], sems, a, k, dev))
                arrivals.append(landed)
            elif k in OTHER_CHIPS:
                first.append(_remote(src, outs[a].at[me], sems, a, k, dev))
                sib, _ = _peer(SIBLING)
                relay.append((landed, _remote(outs[a].at[idx], outs[a].at[idx], sems, a, k | SIBLING, sib)))
            else:
                arrivals.append(landed)
    return local, first, relay, arrivals


def _exchange_start(ins, outs, sems, scatter):
    local, first, _, _ = _exchange_copies(ins, outs, sems, scatter)
    for cp in local + first:
        cp.start()


def _exchange_wait(ins, outs, sems, scatter):
    local, first, relay, arrivals = _exchange_copies(ins, outs, sems, scatter)
    for landed, forward in relay:
        landed.wait_recv()
        forward.start()
    for cp in arrivals:
        cp.wait_recv()
    for cp in first + [forward for _, forward in relay]:
        cp.wait_send()
    for cp in local:
        cp.wait()


def _exchange_shapes(arrs, scatter):
    return [_sds(a.shape if scatter else (N_DEV,) + a.shape, a.dtype) for a in arrs]


def _exchange_sems(n):
    return [pltpu.SemaphoreType.DMA((n, N_DEV - 1)), pltpu.SemaphoreType.DMA((n, N_DEV - 1)),
            pltpu.SemaphoreType.DMA((n,))]


def _exchange(arrs, *, name, scatter):
    n = len(arrs)

    def body(*refs):
        _exchange_start(refs[:n], refs[n:2 * n], refs[2 * n:], scatter)
        _exchange_wait(refs[:n], refs[n:2 * n], refs[2 * n:], scatter)

    any_spec = pl.BlockSpec(memory_space=pl.ANY)
    outs = pl.pallas_call(body, name=name, out_shape=_exchange_shapes(arrs, scatter), in_specs=[any_spec] * n,
                          out_specs=[any_spec] * n, scratch_shapes=_exchange_sems(n))(*arrs)
    return list(outs)


def _mod_cols(c_all, w_ada, b_cols):
    def body(c_ref, w_ref, b_ref, mod_ref, act_ref):
        c = c_ref[...]
        act = c * _sigmoid(c)
        act_ref[...] = act
        mod_ref[...] = _dot(act.astype(BF16), w_ref[...].astype(BF16)) + b_ref[...]

    return _call(body, name='mod_cols', grid=(1,),
                 in_specs=[_const(c_all.shape), _const(w_ada.shape), _const(b_cols.shape)],
                 out_specs=[_const((N_DEV, ADA_SHARD)), _const(c_all.shape)],
                 out_shape=[_sds((N_DEV, ADA_SHARD)), _sds(c_all.shape)], vmem=VMEM_BIG)(c_all, w_ada, b_cols)


def _grad_w_ada(act_t, dmod_cols):
    def body(a_ref, d_ref, o_ref):
        o_ref[...] = _dot(a_ref[...], d_ref[...])

    return _call(body, name='grad_w_ada', grid=(1,), in_specs=[_const(act_t.shape), _const(dmod_cols.shape)],
                 out_specs=_const((D_MODEL, ADA_SHARD)), out_shape=_sds((D_MODEL, ADA_SHARD)),
                 vmem=VMEM_BIG)(act_t, dmod_cols)


def _pre_mix(x, sc, sh, g, w_s, tm, ride):
    T = x.shape[0]

    def body(x_ref, sc_ref, sh_ref, g_ref, w_ref, proj_ref, h_ref):
        @pl.when(pl.program_id(1) == 0)
        def _():
            xv = x_ref[...]
            h_ref[...] = ((xv * _rsqrt_mean(xv) * g_ref[...]) * (1.0 + sc_ref[...]) + sh_ref[...]).astype(BF16)

        for s in range(2):
            proj_ref[:, s * IN_SHARD:(s + 1) * IN_SHARD] = _dot(h_ref[...], w_ref[s])

    row = pl.BlockSpec((tm, D_MODEL), lambda i, j: (i, 0))
    vec = _const((1, D_MODEL))
    return _call(body, name='pre_mix', grid=(T // tm, N_DEV // 2),
                 in_specs=[row, vec, vec, vec, pl.BlockSpec((2, D_MODEL, IN_SHARD), lambda i, j: (j, 0, 0))],
                 out_specs=[pl.BlockSpec((tm, 2 * IN_SHARD), lambda i, j: (i, j)), row],
                 out_shape=[_sds((T, D_IN_PROJ)), _sds((T, D_MODEL), BF16)],
                 sem=('parallel', 'arbitrary'), ride=ride)(x, sc, sh, g, w_s)


def _halo_before(tm, rows=HALO):
    return lambda i: jnp.maximum(i * (tm // rows) - 1, 0)


def _halo_after(tm, T, rows=HALO):
    return lambda i: jnp.minimum((i + 1) * (tm // rows), T // rows - 1)


def _mix_fwd(yssm, proj, d, glu_w, glu_b, g_ssm, cw, g_conv, avg16, avg64, tm):
    T = yssm.shape[0]
    hb = _halo_before(tm)

    def body(y_ref, p_ref, ph_ref, d_ref, gw_ref, gb_ref, gs_ref, cw_ref, gc_ref, a16_ref, a64_ref, o_ref):
        i = pl.program_id(0)
        u = p_ref[:, 0:D_SSM]
        y = y_ref[...] + d_ref[...] * u
        z, _ = _gelu(y)
        gate = _sigmoid(_dot(z.astype(BF16), gw_ref[...]) + gb_ref[...])
        ya = z * gate
        rs = lax.rsqrt(_dot_split(ya * ya, a16_ref[...], 2) + EPS)
        o_ref[:, 0:D_SSM] = (ya * rs * gs_ref[...]).astype(BF16)
        bg = p_ref[:, D_SSM:D_SSM + D_CONV]
        cv = p_ref[:, D_SSM + D_CONV:D_SSM + 2 * D_CONV] * p_ref[:, D_SSM + 2 * D_CONV:D_IN_PROJ]
        hv = ph_ref[:, D_SSM + D_CONV:D_SSM + 2 * D_CONV] * ph_ref[:, D_SSM + 2 * D_CONV:D_IN_PROJ]
        hv = jnp.where(i > 0, hv, 0.0)
        conv, _, _ = _conv3(cv, hv, cw_ref)
        yb = bg * conv
        rsb = lax.rsqrt(_dot_split(yb * yb, a64_ref[...], 2) + EPS)
        o_ref[:, D_SSM:D_MODEL] = (yb * rsb * gc_ref[...]).astype(BF16)

    vec = _const((1, D_SSM))
    sq = _const((D_SSM, D_SSM))
    return _call(body, name='mix_fwd', grid=(T // tm,),
                 in_specs=[pl.BlockSpec((tm, D_SSM), lambda i: (i, 0)), pl.BlockSpec((tm, D_IN_PROJ), lambda i: (i, 0)),
                           pl.BlockSpec((HALO, D_IN_PROJ), lambda i: (hb(i), 0)), vec, sq, vec, vec,
                           _const((3, D_CONV)), vec, sq, sq],
                 out_specs=pl.BlockSpec((tm, D_MODEL), lambda i: (i, 0)), out_shape=_sds((T, D_MODEL), BF16),
                 sem=('parallel',), vmem=VMEM_BIG)(yssm, proj, proj, d, glu_w, glu_b, g_ssm, cw, g_conv, avg16, avg64)


def _out_proj(ycat, w_out, x, gt, g_post, g_pre, sc, sh, tm):
    T = x.shape[0]

    def body(y_ref, w_ref, x_ref, gt_ref, gp_ref, g2_ref, sc_ref, sh_ref, o_ref, x1_ref, h_ref):
        o = _dot(y_ref[...], w_ref[...])
        o_ref[...] = o
        x1 = x_ref[...] + gt_ref[...] * (o * _rsqrt_mean(o) * gp_ref[...])
        x1_ref[...] = x1
        h_ref[...] = ((x1 * _rsqrt_mean(x1) * g2_ref[...]) * (1.0 + sc_ref[...]) + sh_ref[...]).astype(BF16)

    row = pl.BlockSpec((tm, D_MODEL), lambda i: (i, 0))
    vec = _const((1, D_MODEL))
    return _call(body, name='out_proj', grid=(T // tm,),
                 in_specs=[row, _const((D_MODEL, D_MODEL)), row, vec, vec, vec, vec, vec],
                 out_specs=[row, row, row],
                 out_shape=[_sds((T, D_MODEL)), _sds((T, D_MODEL)), _sds((T, D_MODEL), BF16)],
                 sem=('parallel',), vmem=VMEM_BIG)(ycat, w_out, x, gt, g_post, g_pre, sc, sh)


def _ffn_up(h2, w_s, cw8, tm):
    T = h2.shape[0]
    hb = _halo_before(tm, HALO16)

    def body(h_ref, hh_ref, w_ref, cw_ref, up_ref, hid_ref):
        up = _dot_nt(h_ref[...], w_ref[...])
        up_ref[...] = up.astype(BF16)
        before = jnp.where(pl.program_id(0) > 0, _dot_nt(hh_ref[...], w_ref[...]), 0.0)
        hid_ref[...] = _conv3(up, before, cw_ref)[0].astype(BF16)

    out = pl.BlockSpec((None, tm, FF_SHARD), lambda i, j: (j, i, 0))
    return _call(body, name='ffn_up', grid=(T // tm, N_DEV),
                 in_specs=[pl.BlockSpec((tm, D_MODEL), lambda i, j: (i, 0)),
                           pl.BlockSpec((HALO16, D_MODEL), lambda i, j: (hb(i), 0)),
                           pl.BlockSpec((None, FF_SHARD, D_MODEL), lambda i, j: (j, 0, 0)),
                           pl.BlockSpec((None, 3, FF_SHARD), lambda i, j: (j, 0, 0))],
                 out_specs=[out, out], out_shape=[_sds((N_DEV, T, FF_SHARD), BF16)] * 2,
                 sem=('parallel', 'parallel'))(h2, h2, w_s, cw8)


def _ffn_down(hid4, wd4, x1, tgt, gt, g_post, tm):
    T = x1.shape[0]
    nb = T // tm

    def body(a_ref, w_ref, x1_ref, t_ref, gt_ref, g_ref, ddn_ref, dx_ref, loss_ref, dgt_ref, dg_ref, dn_ref):
        i, j = pl.program_id(0), pl.program_id(1)
        part = None
        for s in range(2):
            act = (_silu_parts(a_ref[0, s].astype(F32))[0] * a_ref[1, s].astype(F32)).astype(BF16)
            term = _dot(act, w_ref[s])
            part = term if part is None else part + term

        @pl.when(jnp.logical_and(i == 0, j == 0))
        def _():
            dgt_ref[...] = jnp.zeros_like(dgt_ref)
            dg_ref[...] = jnp.zeros_like(dg_ref)

        @pl.when(j == 0)
        def _():
            dn_ref[...] = part

        @pl.when(j > 0)
        def _():
            dn_ref[...] += part

        @pl.when(j == 1)
        def _():
            dn, gv, gate = dn_ref[...], g_ref[...], gt_ref[...]
            r = _rsqrt_mean(dn)
            normed = dn * r * gv
            err = x1_ref[...] + gate * normed - t_ref[...]
            dx = err * (1.0 / D_MODEL)
            dx_ref[...] = dx
            tot = jnp.sum(jnp.sum(err * err, axis=1, keepdims=True), axis=0, keepdims=True) * (0.5 / D_MODEL)
            loss_ref[...] = jnp.broadcast_to(tot, (8, 128))
            dgt_ref[...] += _colsum(dx * normed)
            dnn = dx * gate
            dg_ref[...] += _colsum(dnn * dn * r)
            ddn_ref[...] = _norm_bwd(dnn, dn, r, gv).astype(BF16)

    row = pl.BlockSpec((tm, D_MODEL), lambda i, j: (i, 0))
    vec = _const((1, D_MODEL))
    return _call(body, name='ffn_down', grid=(nb, 2),
                 in_specs=[pl.BlockSpec((2, 2, tm, FF_SHARD), lambda i, j: (0, j, i, 0)),
                           pl.BlockSpec((2, FF_SHARD, D_MODEL), lambda i, j: (j, 0, 0)), row, row, vec, vec],
                 out_specs=[row, row, pl.BlockSpec((None, 8, 128), lambda i, j: (i, 0, 0)), vec, vec],
                 out_shape=[_sds((T, D_MODEL), BF16), _sds((T, D_MODEL)), _sds((nb, 8, 128)), _sds((1, D_MODEL)),
                            _sds((1, D_MODEL))],
                 scratch=[pltpu.VMEM((tm, D_MODEL), F32)], sem=('arbitrary', 'arbitrary'),
                 vmem=VMEM_BIG)(hid4, wd4, x1, tgt, gt, g_post)


def _ssm_prep(lre, lim, lst, b_re, b_im):
    def body(lre_ref, lim_ref, lst_ref, br_ref, bi_ref, ar_ref, ai_ref, bbr_ref, bbi_ref):
        ar, ai, qr, qi = _zoh(lre_ref[...], lim_ref[...], lst_ref[...])[:4]
        ar_ref[...] = ar
        ai_ref[...] = ai
        bbr_ref[...] = qr * br_ref[...] - qi * bi_ref[...]
        bbi_ref[...] = qr * bi_ref[...] + qi * br_ref[...]

    shp = lre.shape
    return _call(body, name='ssm_prep', grid=(1,), in_specs=[_const(shp)] * 5, out_specs=[_const(shp)] * 4,
                 out_shape=[_sds(shp)] * 4)(lre, lim, lst, b_re, b_im)


def _zoh(lre, lim, lst):
    lr = jnp.minimum(lre, LAMBDA_RE_MAX)
    st = jnp.exp(lst)
    mag = jnp.exp(lr * st)
    ar = mag * jnp.cos(lim * st)
    ai = mag * jnp.sin(lim * st)
    den = lr * lr + lim * lim
    qr = ((ar - 1.0) * lr + ai * lim) / den
    qi = (ai * lr - (ar - 1.0) * lim) / den
    return ar, ai, qr, qi, lr, st, den


def _ssm_prep_bwd(lre, lim, lst, b_re, b_im, dbbr, dbbi, dar, dai, seg):
    def body(lre_ref, lim_ref, lst_ref, br_ref, bi_ref, dbbr_ref, dbbi_ref, dar_ref, dai_ref, seg_ref,
             dbr_ref, dbi_ref, dlre_ref, dlim_ref, dlst_ref):
        lre_v = lre_ref[...]
        li = lim_ref[...]
        ar, ai, qr, qi, lr, st, den = _zoh(lre_v, li, lst_ref[...])
        br, bi, gbr, gbi = br_ref[...], bi_ref[...], dbbr_ref[...], dbbi_ref[...]
        dbr_ref[...] = qr * gbr + qi * gbi
        dbi_ref[...] = qr * gbi - qi * gbr
        gqr = _dot_split(br * gbr + bi * gbi, seg_ref[...], 3)
        gqi = _dot_split(br * gbi - bi * gbr, seg_ref[...], 3)
        ir, ii = lr / den, -li / den
        gar = dar_ref[...] + ir * gqr + ii * gqi
        gai = dai_ref[...] + ir * gqi - ii * gqr
        tr, ti = qr * ir - qi * ii, qr * ii + qi * ir
        glr = -(tr * gqr + ti * gqi)
        gli = -(tr * gqi - ti * gqr)
        gzr = ar * gar + ai * gai
        gzi = ar * gai - ai * gar
        glr = glr + st * gzr
        gli = gli + st * gzi
        gst = (lr * gzr + li * gzi) * st
        dlre_ref[...] = jnp.where(lre_v < LAMBDA_RE_MAX, glr, 0.0)
        dlim_ref[...] = gli
        dlst_ref[...] = jnp.sum(gst, axis=1, keepdims=True) * (1.0 / SSM_GROUP)

    shp = lre.shape
    return _call(body, name='ssm_prep_bwd', grid=(1,), in_specs=[_const(shp)] * 9 + [_const(seg.shape)],
                 out_specs=[_const(shp)] * 4 + [_const((N_GROUPS, 1))],
                 out_shape=[_sds(shp)] * 4 + [_sds((N_GROUPS, 1))], vmem=VMEM_BIG)(
                     lre, lim, lst, b_re, b_im, dbbr, dbbi, dar, dai, seg)


def _scan_specs(T):
    return dict(
        chan=pl.BlockSpec((T, CHAN_BLOCK), lambda cb: (0, cb)),
        state=pl.BlockSpec((T, STATE_BLOCK), lambda cb: (0, cb)),
        b=pl.BlockSpec((CHAN_BLOCK, STATE_BLOCK), lambda cb: (cb, cb)),
        c=pl.BlockSpec((STATE_BLOCK, CHAN_BLOCK), lambda cb: (cb, cb)),
        lam=pl.BlockSpec((1, STATE_BLOCK), lambda cb: (0, cb)),
    )


def _complex_power(re, im, n):
    out = None
    while True:
        if n & 1:
            out = (re, im) if out is None else (out[0] * re - out[1] * im, out[0] * im + out[1] * re)
        n >>= 1
        if n == 0:
            return out
        re, im = re * re - im * im, 2.0 * re * im


def _rows8(i):
    if isinstance(i, int):
        return pl.ds(i * SUBLANES, SUBLANES)
    return pl.ds(pl.multiple_of(i * SUBLANES, SUBLANES), SUBLANES)


def _scan_loop(n_steps, body, init):
    trips = n_steps // SCAN_UNROLL

    def trip(t, carry):
        for u in range(SCAN_UNROLL):
            carry = body(t * SCAN_UNROLL + u, carry)
        return carry

    carry = lax.fori_loop(0, trips, trip, init)
    for step in range(trips * SCAN_UNROLL, n_steps):
        carry = body(step, carry)
    return carry


def _ssm_fwd(u_perm, b_re, b_im, c_re, c_im, lam_r, lam_i, ride):
    T = u_perm.shape[0]
    ls = T // SUBLANES
    rc = min(512, T)
    sp = _scan_specs(T)

    def body(u_ref, bre_ref, bim_ref, cre_ref, cim_ref, lr_ref, li_ref, so_re_ref, so_im_ref, y_ref, sre_ref, sim_ref):
        for c in range(T // rc):
            rows = pl.ds(c * rc, rc)
            ub = u_ref[rows, :].astype(BF16)
            sre_ref[rows, :] = _dot(ub, bre_ref[...])
            sim_ref[rows, :] = _dot(ub, bim_ref[...])
        shp = (SUBLANES, STATE_BLOCK)
        lr = jnp.broadcast_to(lr_ref[...], shp)
        li = jnp.broadcast_to(li_ref[...], shp)
        zero = jnp.zeros(shp, F32)

        def step(i, carry):
            sr, si = carry
            rows = _rows8(i)
            nr = lr * sr - li * si + sre_ref[rows, :]
            ni = lr * si + li * sr + sim_ref[rows, :]
            sre_ref[rows, :] = nr
            sim_ref[rows, :] = ni
            return nr, ni

        fr, fi = _scan_loop(ls, step, (zero, zero))
        pr, pi_ = _complex_power(lr, li, ls)
        row = lax.broadcasted_iota(jnp.int32, shp, 0)
        ir, ii = zero, zero
        for _ in range(SUBLANES - 1):
            er = fr + pr * ir - pi_ * ii
            ei = fi + pr * ii + pi_ * ir
            ir = jnp.where(row == 0, 0.0, pltpu.roll(er, 1, 0))
            ii = jnp.where(row == 0, 0.0, pltpu.roll(ei, 1, 0))

        def fix(i, carry):
            cr, ci = carry
            rows = _rows8(i)
            nr = lr * cr - li * ci
            ni = lr * ci + li * cr
            sre_ref[rows, :] += nr
            sim_ref[rows, :] += ni
            return nr, ni

        _scan_loop(ls, fix, (ir, ii))
        for c in range(T // rc):
            rows = pl.ds(c * rc, rc)
            s_r, s_i = sre_ref[rows, :].astype(BF16), sim_ref[rows, :].astype(BF16)
            so_re_ref[rows, :] = s_r
            so_im_ref[rows, :] = s_i
            y_ref[rows, :] = _dot(s_r, cre_ref[...]) - _dot(s_i, cim_ref[...])

    return _call(body, name='ssm_fwd', grid=(N_STATE // STATE_BLOCK,),
                 in_specs=[sp['chan'], sp['b'], sp['b'], sp['c'], sp['c'], sp['lam'], sp['lam']],
                 out_specs=[sp['state'], sp['state'], sp['chan']],
                 out_shape=[_sds((T, N_STATE), BF16), _sds((T, N_STATE), BF16), _sds((T, D_SSM))],
                 scratch=[pltpu.VMEM((T, STATE_BLOCK), F32), pltpu.VMEM((T, STATE_BLOCK), F32)],
                 sem=('arbitrary',), vmem=VMEM_MOST, ride=ride)(u_perm, b_re, b_im, c_re, c_im, lam_r, lam_i)


def _ssm_bwd(dy_perm, u_perm, s_re, s_im, b_re, b_im, c_re, c_im, lam_r, lam_i, ride):
    T = u_perm.shape[0]
    ls = T // SUBLANES
    rc = min(512, T)
    sp = _scan_specs(T)
    ncb = N_STATE // STATE_BLOCK

    def body(dy_ref, u_ref, sre_ref, sim_ref, bre_ref, bim_ref, cre_ref, cim_ref, lr_ref, li_ref,
             du_ref, dbr_ref, dbi_ref, dcr_ref, dci_ref, dar_ref, dai_ref, gre_ref, gim_ref):
        shp = (SUBLANES, STATE_BLOCK)
        zero = jnp.zeros(shp, F32)
        tail = pl.ds(T, SUBLANES)
        gre_ref[tail, :] = zero
        gim_ref[tail, :] = zero
        for c in range(T // rc):
            rows = pl.ds(c * rc, rc)
            dyb = dy_ref[rows, :].astype(BF16)
            gre_ref[rows, :] = _dot_nt(dyb, cre_ref[...])
            gim_ref[rows, :] = -_dot_nt(dyb, cim_ref[...])
        lr = jnp.broadcast_to(lr_ref[...], shp)
        li = jnp.broadcast_to(li_ref[...], shp)

        def step(k, carry):
            gr, gi = carry
            rows = _rows8(ls - 1 - k)
            nr = lr * gr + li * gi + gre_ref[rows, :]
            ni = lr * gi - li * gr + gim_ref[rows, :]
            gre_ref[rows, :] = nr
            gim_ref[rows, :] = ni
            return nr, ni

        fr, fi = _scan_loop(ls, step, (zero, zero))
        pr, pi_ = _complex_power(lr, -li, ls)
        row = lax.broadcasted_iota(jnp.int32, shp, 0)
        cr, ci = zero, zero
        for _ in range(SUBLANES - 1):
            er = fr + pr * cr - pi_ * ci
            ei = fi + pr * ci + pi_ * cr
            cr = jnp.where(row == SUBLANES - 1, 0.0, pltpu.roll(er, SUBLANES - 1, 0))
            ci = jnp.where(row == SUBLANES - 1, 0.0, pltpu.roll(ei, SUBLANES - 1, 0))

        def fix(k, carry):
            dr, di = carry
            rows = _rows8(ls - 1 - k)
            dr, di = lr * dr + li * di, lr * di - li * dr
            gre_ref[rows, :] += dr
            gim_ref[rows, :] += di
            return dr, di

        _scan_loop(ls, fix, (cr, ci))

        acc_r = jnp.zeros((1, STATE_BLOCK), F32)
        acc_i = jnp.zeros((1, STATE_BLOCK), F32)
        for c in range(T // rc):
            rows, nxt = pl.ds(c * rc, rc), pl.ds(c * rc + SUBLANES, rc)
            s_r, s_i = sre_ref[rows, :].astype(F32), sim_ref[rows, :].astype(F32)
            g_r, g_i = gre_ref[nxt, :], gim_ref[nxt, :]
            acc_r = acc_r + _colsum(g_r * s_r + g_i * s_i)
            acc_i = acc_i + _colsum(g_i * s_r - g_r * s_i)
        last = pl.ds(T - 2 * SUBLANES, 2 * SUBLANES)
        first = pl.ds(0, SUBLANES)
        spr = jnp.where(row == 0, 0.0, pltpu.roll(sre_ref[last, :].astype(F32)[SUBLANES:], 1, 0))
        spi = jnp.where(row == 0, 0.0, pltpu.roll(sim_ref[last, :].astype(F32)[SUBLANES:], 1, 0))
        gr, gi = gre_ref[first, :], gim_ref[first, :]
        dar_ref[...] = acc_r + _colsum(gr * spr + gi * spi)
        dai_ref[...] = acc_i + _colsum(gi * spr - gr * spi)

        for c in range(T // rc):
            rows = pl.ds(c * rc, rc)
            g_r, g_i = gre_ref[rows, :].astype(BF16), gim_ref[rows, :].astype(BF16)
            s_r, s_i = sre_ref[rows, :], sim_ref[rows, :]
            ub, dyb = u_ref[rows, :].astype(BF16), dy_ref[rows, :].astype(BF16)
            du_ref[rows, :] = _dot_nt(g_r, bre_ref[...]) + _dot_nt(g_i, bim_ref[...])
            parts = (_dot_tn(ub, g_r), _dot_tn(ub, g_i), _dot_tn(s_r, dyb), -_dot_tn(s_i, dyb))
            outs = (dbr_ref, dbi_ref, dcr_ref, dci_ref)
            for o_ref, part in zip(outs, parts):
                if c == 0:
                    o_ref[...] = part
                else:
                    o_ref[...] += part

    blk = lambda r, c: pl.BlockSpec((None, r, c), lambda cb: (cb, 0, 0))
    return _call(body, name='ssm_bwd', grid=(ncb,),
                 in_specs=[sp['chan'], sp['chan'], sp['state'], sp['state'], sp['b'], sp['b'], sp['c'], sp['c'],
                           sp['lam'], sp['lam']],
                 out_specs=[sp['chan'], blk(CHAN_BLOCK, STATE_BLOCK), blk(CHAN_BLOCK, STATE_BLOCK),
                            blk(STATE_BLOCK, CHAN_BLOCK), blk(STATE_BLOCK, CHAN_BLOCK), blk(1, STATE_BLOCK),
                            blk(1, STATE_BLOCK)],
                 out_shape=[_sds((T, D_SSM)), _sds((ncb, CHAN_BLOCK, STATE_BLOCK)), _sds((ncb, CHAN_BLOCK, STATE_BLOCK)),
                            _sds((ncb, STATE_BLOCK, CHAN_BLOCK)), _sds((ncb, STATE_BLOCK, CHAN_BLOCK)),
                            _sds((ncb, 1, STATE_BLOCK)), _sds((ncb, 1, STATE_BLOCK))],
                 scratch=[pltpu.VMEM((T + SUBLANES, STATE_BLOCK), F32), pltpu.VMEM((T + SUBLANES, STATE_BLOCK), F32)],
                 sem=('arbitrary',), vmem=VMEM_MOST, ride=ride)(dy_perm, u_perm, s_re, s_im, b_re, b_im, c_re, c_im,
                                                                lam_r, lam_i)


def _ffn_dact(ddn, wd4, hid4, tm):
    T = ddn.shape[0]

    def body(d_ref, w_ref, hid_ref, o_ref, act_ref):
        dact = _dot_nt(d_ref[...], w_ref[...])
        silu, dsilu = _silu_parts(hid_ref[0].astype(F32))
        hid_v = hid_ref[1].astype(F32)
        o_ref[0] = (dact * hid_v * dsilu).astype(BF16)
        o_ref[1] = (dact * silu).astype(BF16)
        act_ref[...] = (silu * hid_v).astype(BF16)

    blk = pl.BlockSpec((2, None, tm, FF_SHARD), lambda i, j: (0, j, i, 0))
    return _call(body, name='ffn_dact', grid=(T // tm, 4),
                 in_specs=[pl.BlockSpec((tm, D_MODEL), lambda i, j: (i, 0)),
                           pl.BlockSpec((None, FF_SHARD, D_MODEL), lambda i, j: (j, 0, 0)), blk],
                 out_specs=[blk, pl.BlockSpec((None, tm, FF_SHARD), lambda i, j: (j, i, 0))],
                 out_shape=[_sds((2, 4, T, FF_SHARD), BF16), _sds((4, T, FF_SHARD), BF16)],
                 sem=('parallel', 'parallel'), vmem=VMEM_BIG)(ddn, wd4, hid4)


def _ffn_dup(dhid8, up8, cw8, tm, ride):
    T = up8.shape[1]
    nb = T // tm
    ha = _halo_after(tm, T, HALO16)

    def body(dh_ref, dha_ref, up_ref, cw_ref, dup_ref, dcw_ref):
        i = pl.program_id(1)

        @pl.when(i == 0)
        def _():
            dcw_ref[...] = jnp.zeros_like(dcw_ref)

        dh = dh_ref[...].astype(F32)
        dup, dh1, dh2 = _conv3_t(dh, jnp.where(i < nb - 1, dha_ref[...].astype(F32), 0.0), cw_ref)
        dup_ref[...] = dup.astype(BF16)
        up = up_ref[...].astype(F32)
        dcw_ref[0:1, :] += _colsum(dh2 * up)
        dcw_ref[1:2, :] += _colsum(dh1 * up)
        dcw_ref[2:3, :] += _colsum(dh * up)

    main = pl.BlockSpec((None, tm, FF_SHARD), lambda j, i: (j, i, 0))
    return _call(body, name='ffn_dup', grid=(N_DEV, nb),
                 in_specs=[main, pl.BlockSpec((None, HALO16, FF_SHARD), lambda j, i: (j, ha(i), 0)), main,
                           pl.BlockSpec((None, 3, FF_SHARD), lambda j, i: (j, 0, 0))],
                 out_specs=[main, pl.BlockSpec((None, 8, FF_SHARD), lambda j, i: (j, 0, 0))],
                 out_shape=[_sds((N_DEV, T, FF_SHARD), BF16), _sds((N_DEV, 8, FF_SHARD))],
                 sem=('parallel', 'arbitrary'), vmem=VMEM_BIG, ride=ride)(dhid8, dhid8, up8, cw8)


def _grad_tn(a, b, a_spec, b_spec, groups, m, n, tk, name, ride=None, parts=1):
    T = a.shape[-2]
    nk = T // tk
    mp = m // parts

    def body(a_ref, b_ref, *refs):
        o_refs, acc_ref = refs[:parts], refs[parts]
        k = pl.program_id(1)
        part = _dot_tn(a_ref[...], b_ref[...])

        @pl.when(k == 0)
        def _():
            acc_ref[...] = part

        @pl.when(k > 0)
        def _():
            acc_ref[...] += part

        @pl.when(k == nk - 1)
        def _():
            for p, o_ref in enumerate(o_refs):
                o_ref[...] = acc_ref[p * mp:(p + 1) * mp, :].astype(BF16)

    out_spec = pl.BlockSpec((None, mp, n), lambda g, k: (g, 0, 0))
    res = _call(body, name=name, grid=(groups, nk), in_specs=[a_spec, b_spec], out_specs=[out_spec] * parts,
                out_shape=[_sds((groups, mp, n), BF16)] * parts, scratch=[pltpu.VMEM((m, n), F32)],
                sem=('parallel', 'arbitrary'), vmem=VMEM_BIG, ride=ride)(a, b)
    if parts > 1:
        return res
    return res[0] if ride is None else (res[0][0], res[1])


def _grad_w_in(h1, dproj, tk, ride):
    T = h1.shape[0]
    nk = T // tk
    half = D_IN_PROJ // 2

    def body(a_ref, b_ref, o_ref, acc_ref):
        k = pl.program_id(0)
        for h in range(2):
            cols = slice(h * half, (h + 1) * half)
            part = _dot_tn(a_ref[...], b_ref[:, cols])

            @pl.when(k == 0)
            def _():
                acc_ref[:, cols] = part

            @pl.when(k > 0)
            def _():
                acc_ref[:, cols] += part

        @pl.when(k == nk - 1)
        def _():
            for g in range(N_DEV):
                o_ref[g] = acc_ref[:, g * IN_SHARD:(g + 1) * IN_SHARD].astype(BF16)

    return _call(body, name='grad_w_in', grid=(nk,),
                 in_specs=[pl.BlockSpec((tk, D_MODEL), lambda k: (k, 0)), pl.BlockSpec((tk, D_IN_PROJ), lambda k: (k, 0))],
                 out_specs=_const((N_DEV, D_MODEL, IN_SHARD)), out_shape=_sds((N_DEV, D_MODEL, IN_SHARD), BF16),
                 scratch=[pltpu.VMEM((D_MODEL, D_IN_PROJ), F32)], sem=('arbitrary',), vmem=VMEM_BIG, ride=ride)(h1, dproj)


def _pre_norm_bwd(dz, dz_spec, w_s, xin, dres, sc, g, tm, name, ride, below=None, group=1, w_t=False):
    T = xin.shape[0]
    n = w_s.shape[1] if w_t else w_s.shape[2]
    mul = _dot if w_t else _dot_nt
    steps = N_DEV // group

    def body(dz_ref, w_ref, x_ref, dr_ref, sc_ref, g_ref, *refs):
        if below is None:
            dx_ref, dsh_ref, dsc_ref, dg_ref = refs
            sums = (dsh_ref, dsc_ref, dg_ref)
        else:
            v_ref, gate_ref, g2_ref, dx_ref, dsh_ref, dsc_ref, dg_ref, dv_ref, dgate_ref, dg2_ref = refs
            sums = (dsh_ref, dsc_ref, dg_ref, dgate_ref, dg2_ref)
        i, j = pl.program_id(0), pl.program_id(1)
        piece = (lambda s: dz_ref[s]) if dz.ndim == 3 else (lambda s: dz_ref[:, s * n:(s + 1) * n])
        part = mul(piece(0), w_ref[0])
        for s in range(1, group):
            part = part + mul(piece(s), w_ref[s])

        @pl.when(jnp.logical_and(i == 0, j == 0))
        def _():
            for s_ref in sums:
                s_ref[...] = jnp.zeros_like(s_ref)

        @pl.when(j == 0)
        def _():
            dx_ref[...] = part

        @pl.when(j > 0)
        def _():
            dx_ref[...] += part

        @pl.when(j == steps - 1)
        def _():
            dh, xv, gv = dx_ref[...], x_ref[...], g_ref[...]
            r = _rsqrt_mean(xv)
            dsh_ref[...] += _colsum(dh)
            dsc_ref[...] += _colsum(dh * (xv * r * gv))
            dxn = dh * (1.0 + sc_ref[...])
            dg_ref[...] += _colsum(dxn * xv * r)
            dx = dr_ref[...] + _norm_bwd(dxn, xv, r, gv)
            dx_ref[...] = dx
            if below is not None:
                v, g2 = v_ref[...], g2_ref[...]
                rv = _rsqrt_mean(v)
                dgate_ref[...] += _colsum(dx * (v * rv * g2))
                dn = dx * gate_ref[...]
                dg2_ref[...] += _colsum(dn * v * rv)
                dv_ref[...] = _norm_bwd(dn, v, rv, g2).astype(BF16)

    row = pl.BlockSpec((tm, D_MODEL), lambda i, j: (i, 0))
    vec = _const((1, D_MODEL))
    in_specs = [dz_spec, pl.BlockSpec((group,) + w_s.shape[1:], lambda i, j: (j, 0, 0)), row, row, vec, vec]
    out_specs = [row, vec, vec, vec]
    out_shape = [_sds((T, D_MODEL)), _sds((1, D_MODEL)), _sds((1, D_MODEL)), _sds((1, D_MODEL))]
    args = [dz, w_s, xin, dres, sc, g]
    if below is not None:
        in_specs += [row, vec, vec]
        out_specs += [row, vec, vec]
        out_shape += [_sds((T, D_MODEL), BF16), _sds((1, D_MODEL)), _sds((1, D_MODEL))]
        args += list(below)
    return _call(body, name=name, grid=(T // tm, steps), in_specs=in_specs, out_specs=out_specs,
                 out_shape=out_shape, sem=('arbitrary', 'arbitrary'), vmem=VMEM_MOST, ride=ride)(*args)


def _mix_bwd(d_o, w_out, yssm, proj, d, glu_w, glu_b, g_ssm, cw, g_conv, avg16, avg64, tm, ride):
    T = yssm.shape[0]
    hb = _halo_before(tm)

    def body(do_ref, wo_ref, y_ref, p_ref, ph_ref, d_ref, gw_ref, gb_ref, gs_ref, cw_ref, gc_ref, a16_ref, a64_ref,
             dy_ref, dconv_ref, dbg_ref, z_ref, dlin_ref, acc_ref):
        i = pl.program_id(0)
        dyc = _dot_nt(do_ref[...], wo_ref[...])

        @pl.when(i == 0)
        def _():
            acc_ref[...] = jnp.zeros_like(acc_ref)

        u = p_ref[:, 0:D_SSM]
        y = y_ref[...] + d_ref[...] * u
        z, t = _gelu(y)
        gate = _sigmoid(_dot(z.astype(BF16), gw_ref[...]) + gb_ref[...])
        ya = z * gate
        rs = lax.rsqrt(_dot_split(ya * ya, a16_ref[...], 2) + EPS)
        dna = dyc[:, 0:D_SSM]
        acc_ref[1:2, :] += _colsum(dna * ya * rs)
        dya = _head_norm_bwd(dna, ya, rs, gs_ref[...], a16_ref[...])
        dlin = dya * z * gate * (1.0 - gate)
        acc_ref[0:1, :] += _colsum(dlin)
        dlin_b = dlin.astype(BF16)
        dz = dya * gate + _dot_nt(dlin_b, gw_ref[...])
        dy = dz * _gelu_grad(y, t)
        acc_ref[3:4, :] += _colsum(dy * u)
        dy_ref[...] = dy
        z_ref[...] = z.astype(BF16)
        dlin_ref[...] = dlin_b

        bg = p_ref[:, D_SSM:D_SSM + D_CONV]
        cv = p_ref[:, D_SSM + D_CONV:D_SSM + 2 * D_CONV] * p_ref[:, D_SSM + 2 * D_CONV:D_IN_PROJ]
        hv = ph_ref[:, D_SSM + D_CONV:D_SSM + 2 * D_CONV] * ph_ref[:, D_SSM + 2 * D_CONV:D_IN_PROJ]
        hv = jnp.where(i > 0, hv, 0.0)
        conv, cv1, cv2 = _conv3(cv, hv, cw_ref)
        yb = bg * conv
        rsb = lax.rsqrt(_dot_split(yb * yb, a64_ref[...], 2) + EPS)
        dnb = dyc[:, D_SSM:D_MODEL]
        acc_ref[2:3, :] += _colsum(dnb * yb * rsb)
        dyb = _head_norm_bwd(dnb, yb, rsb, gc_ref[...], a64_ref[...])
        dbg_ref[...] = dyb * conv
        dconv = dyb * bg
        dconv_ref[...] = dconv
        acc_ref[4:5, :] += _colsum(dconv * cv2)
        acc_ref[5:6, :] += _colsum(dconv * cv1)
        acc_ref[6:7, :] += _colsum(dconv * cv)

    vec = _const((1, D_SSM))
    sq = _const((D_SSM, D_SSM))
    half = pl.BlockSpec((tm, D_SSM), lambda i: (i, 0))
    return _call(body, name='mix_bwd', grid=(T // tm,),
                 in_specs=[pl.BlockSpec((tm, D_MODEL), lambda i: (i, 0)), _const((D_MODEL, D_MODEL)), half,
                           pl.BlockSpec((tm, D_IN_PROJ), lambda i: (i, 0)),
                           pl.BlockSpec((HALO, D_IN_PROJ), lambda i: (hb(i), 0)), vec, sq, vec, vec,
                           _const((3, D_CONV)), vec, sq, sq],
                 out_specs=[half, half, half, half, half, _const((8, D_SSM))],
                 out_shape=[_sds((T, D_SSM)), _sds((T, D_SSM)), _sds((T, D_SSM)), _sds((T, D_SSM), BF16),
                            _sds((T, D_SSM), BF16), _sds((8, D_SSM))],
                 sem=('arbitrary',), vmem=VMEM_BIG, ride=ride)(d_o, w_out, yssm, proj, proj, d, glu_w, glu_b, g_ssm, cw,
                                                              g_conv, avg16, avg64)


def _mix_bwd_proj(dconv, proj, du_ssm, dy, d, dbg, cw, tm):
    T = dy.shape[0]
    nb = T // tm
    ha = _halo_after(tm, T)

    def body(dc_ref, dch_ref, cg_ref, v_ref, du_ref, dy_ref, d_ref, dbg_ref, cw_ref, o_ref):
        i = pl.program_id(0)
        dcv = _conv3_t(dc_ref[...], jnp.where(i < nb - 1, dch_ref[...], 0.0), cw_ref)[0]
        o_ref[:, 0:D_SSM] = (du_ref[...] + dy_ref[...] * d_ref[...]).astype(BF16)
        o_ref[:, D_SSM:D_SSM + D_CONV] = dbg_ref[...].astype(BF16)
        o_ref[:, D_SSM + D_CONV:D_SSM + 2 * D_CONV] = (dcv * v_ref[...]).astype(BF16)
        o_ref[:, D_SSM + 2 * D_CONV:D_IN_PROJ] = (dcv * cg_ref[...]).astype(BF16)

    half = pl.BlockSpec((tm, D_SSM), lambda i: (i, 0))
    return _call(body, name='mix_bwd_proj', grid=(nb,),
                 in_specs=[half, pl.BlockSpec((HALO, D_CONV), lambda i: (ha(i), 0)),
                           pl.BlockSpec((tm, D_CONV), lambda i: (i, 2)), pl.BlockSpec((tm, D_CONV), lambda i: (i, 3)),
                           half, half, _const((1, D_SSM)), half, _const((3, D_CONV))],
                 out_specs=pl.BlockSpec((tm, D_IN_PROJ), lambda i: (i, 0)), out_shape=_sds((T, D_IN_PROJ), BF16),
                 sem=('parallel',))(dconv, dconv, proj, proj, du_ssm, dy, d, dbg, cw)


def _row_tile(rows, cols, slots):
    for cand in (512, 256, 128, 64, 32, 16, 8):
        if rows % cand == 0 and slots * cand * cols * 4 <= (2 << 20):
            return cand
    return rows


def _adamw_math(g, w, m, v):
    m2 = ADAM_B1 * m + (1.0 - ADAM_B1) * g
    v2 = ADAM_B2 * v + (1.0 - ADAM_B2) * (g * g)
    m_hat = m2 / (1.0 - ADAM_B1 ** ADAM_STEP)
    v_hat = v2 / (1.0 - ADAM_B2 ** ADAM_STEP)
    return -ADAM_LR * (m_hat / (jnp.sqrt(v_hat) + ADAM_EPS) + ADAM_WD * w), m2, v2


def _adamw(pieces, w, m, v, name):
    slots, _, cols = pieces[0].shape
    rows = sum(p.shape[1] for p in pieces)
    tr = _row_tile(pieces[0].shape[1], cols, slots)
    starts, pos = [], 0
    for p in pieces:
        assert p.shape[1] % tr == 0
        starts.append(pos)
        pos += p.shape[1] // tr

    def body(*refs):
        g_refs = refs[:len(pieces)]
        w_ref, m_ref, v_ref, go_ref, d_ref, mo_ref, vo_ref = refs[len(pieces):]
        i = pl.program_id(0)
        g = None
        for g_ref, start in zip(g_refs, starts):
            part = g_ref[0].astype(F32)
            for s in range(1, slots):
                part = part + g_ref[s].astype(F32)
            g = part if g is None else jnp.where(i >= start, part, g)
        go_ref[...] = g
        d_ref[...], mo_ref[...], vo_ref[...] = _adamw_math(g, w_ref[...], m_ref[...], v_ref[...])

    def piece_spec(start, count):
        return pl.BlockSpec((slots, tr, cols), lambda i: (0, jnp.clip(i - start, 0, count - 1), 0))

    blk = pl.BlockSpec((tr, cols), lambda i: (i, 0))
    return _call(body, name=name, grid=(rows // tr,),
                 in_specs=[piece_spec(s, p.shape[1] // tr) for s, p in zip(starts, pieces)] + [blk, blk, blk],
                 out_specs=[blk] * 4, out_shape=[_sds((rows, cols))] * 4, sem=('parallel',))(*pieces, w, m, v)


def _to_scan_rows(a):
    T, n = a.shape
    return a.reshape(SUBLANES, T // SUBLANES, n).transpose(1, 0, 2).reshape(T, n)


def _from_scan_rows(a):
    T, n = a.shape
    return a.reshape(T // SUBLANES, SUBLANES, n).transpose(1, 0, 2).reshape(T, n)


def _expand(a):
    return jnp.repeat(a, SSM_GROUP, axis=1)


def _block_diag(rows, row_group, col_group):
    r, n = rows.shape
    tiled = jnp.tile(rows, (1, N_GROUPS))
    keep = (jnp.arange(r)[:, None] // row_group) == (jnp.arange(n * N_GROUPS)[None, :] // col_group)
    return jnp.where(keep, tiled, 0.0)


def _block_diag_b(bb):
    return _block_diag(bb.transpose(0, 2, 1).reshape(D_SSM, SSM_STATE), SSM_GROUP, SSM_STATE)


def _block_diag_c(cc):
    return _block_diag(cc.transpose(0, 2, 1).reshape(N_STATE, SSM_GROUP), SSM_STATE, SSM_GROUP)


def _diag_blocks(x, chan_major):
    per = CHAN_BLOCK // SSM_GROUP
    eye = jnp.eye(per, dtype=x.dtype)
    if chan_major:
        x = x.reshape(-1, per, SSM_GROUP, per, SSM_STATE) * eye[None, :, None, :, None]
        return x.sum(axis=1).transpose(0, 2, 3, 1).reshape(N_GROUPS, SSM_STATE, SSM_GROUP)
    x = x.reshape(-1, per, SSM_STATE, per, SSM_GROUP) * eye[None, :, None, :, None]
    return x.sum(axis=3).reshape(N_GROUPS, SSM_STATE, SSM_GROUP)


SMALL_LAYOUT = {
    'ssm_b_re': (0, 0, 32, 1024), 'ssm_b_im': (32, 0, 32, 1024), 'ssm_c_re': (64, 0, 32, 1024),
    'ssm_c_im': (96, 0, 32, 1024), 'b_ada': (128, 0, 6, 1024), 'g_pre_mix': (134, 0, 1, 1024),
    'g_post_mix': (135, 0, 1, 1024), 'ssm_lam_re': (136, 0, 2, 1024), 'ssm_lam_im': (138, 0, 2, 1024),
    'ssm_log_step': (140, 0, 1, 32), 'glu_b': (141, 0, 1, 512), 'g_out_ssm': (141, 512, 1, 512),
    'g_out_conv': (142, 0, 1, 512), 'ssm_d': (142, 512, 1, 512), 'g_pre_ffn': (143, 0, 1, 1024),
    'g_post_ffn': (144, 0, 1, 1024)}
SMALL_ROWS = 152
B_ADA_ROW = SMALL_LAYOUT['b_ada'][0]
LATE_ROWS = {('b_ada', 0): 0, ('b_ada', 1): 1, ('g_pre_mix', 0): 2}


def _adamw_small(gathered, late, wts, mom_m, mom_v):
    names = list(SMALL_LAYOUT)
    n = len(names)

    def body(*refs):
        g_ref, late_ref, ins, outs = refs[0], refs[1], refs[2:2 + 3 * n], refs[2 + 3 * n:]
        for p, name in enumerate(names):
            r0, c0, rows, cols = SMALL_LAYOUT[name]
            pieces = [(0, rows)] if rows % 8 == 0 else [(r, 1) for r in range(rows)]
            for r, cnt in pieces:
                src_ref, first = (late_ref, LATE_ROWS[name, r]) if (name, r) in LATE_ROWS else (g_ref, r0 + r)
                g = src_ref[0, first:first + cnt, c0:c0 + cols]
                for s in range(1, N_DEV):
                    g = g + src_ref[s, first:first + cnt, c0:c0 + cols]
                w, m, v = (ins[3 * p + q][r:r + cnt, :] for q in range(3))
                res = (g,) + _adamw_math(g, w, m, v)
                for q in range(4):
                    outs[4 * p + q][r:r + cnt, :] = res[q]

    shapes = [SMALL_LAYOUT[name][2:] for name in names]
    args = [gathered, late]
    for name, shp in zip(names, shapes):
        args += [wts[name].reshape(shp), mom_m[name].reshape(shp), mom_v[name].reshape(shp)]
    outs = _call(body, name='adamw_small', grid=(1,),
                 in_specs=[_const(gathered.shape), _const(late.shape)]
                 + [_const(shp) for shp in shapes for _ in range(3)],
                 out_specs=[_const(shp) for shp in shapes for _ in range(4)],
                 out_shape=[_sds(shp) for shp in shapes for _ in range(4)], vmem=VMEM_BIG)(*args)
    res = {}
    for p, name in enumerate(names):
        for q, kind in enumerate(('g', 'd', 'm', 'v')):
            res[kind, name] = outs[4 * p + q].reshape(wts[name].shape)
    return res


def kernel(x, c, w_ada, b_ada, g_pre_mix, g_post_mix, w_in, ssm_lam_re, ssm_lam_im, ssm_log_step, ssm_b_re, ssm_b_im, ssm_c_re, ssm_c_im, ssm_d, glu_w, glu_b, g_out_ssm, conv_w, g_out_conv, w_out, g_pre_ffn, g_post_ffn, w_up, ffn_conv_w, w_down, loss_target, m_w_ada, m_b_ada, m_g_pre_mix, m_g_post_mix, m_w_in, m_ssm_lam_re, m_ssm_lam_im, m_ssm_log_step, m_ssm_b_re, m_ssm_b_im, m_ssm_c_re, m_ssm_c_im, m_ssm_d, m_glu_w, m_glu_b, m_g_out_ssm, m_conv_w, m_g_out_conv, m_w_out, m_g_pre_ffn, m_g_post_ffn, m_w_up, m_ffn_conv_w, m_w_down, v_w_ada, v_b_ada, v_g_pre_mix, v_g_post_mix, v_w_in, v_ssm_lam_re, v_ssm_lam_im, v_ssm_log_step, v_ssm_b_re, v_ssm_b_im, v_ssm_c_re, v_ssm_c_im, v_ssm_d, v_glu_w, v_glu_b, v_g_out_ssm, v_conv_w, v_g_out_conv, v_w_out, v_g_pre_ffn, v_g_post_ffn, v_w_up, v_ffn_conv_w, v_w_down):
    args = dict(locals())
    wts = {n: args[n] for n in WEIGHTS}
    mom_m = {n: args['m_' + n] for n in WEIGHTS}
    mom_v = {n: args['v_' + n] for n in WEIGHTS}
    T = x.shape[1]
    tm = min(512, T)
    tw = min(1024, T)
    me = _me()[3]
    xt, tgt = x[0], loss_target[0]

    c_all, w_in_s = _exchange([c, w_in[0].astype(BF16)], name='gather_first', scatter=False)
    c_all = c_all.reshape(N_DEV, D_MODEL)
    b_cols = lax.dynamic_slice(b_ada, (0, me * ADA_SHARD), (1, ADA_SHARD))
    mod_cols, c_act = _mod_cols(c_all, w_ada[0], b_cols)
    (mod_all,) = _exchange([mod_cols], name='gather_mod', scatter=False)
    mod = lax.dynamic_slice(mod_all, (0, me, 0), (N_DEV, 1, ADA_SHARD)).reshape(N_MOD, 1, D_MODEL)
    sh1, sc1, gt1, sh2, sc2, gt2 = [mod[k] for k in range(N_MOD)]


    lre_x, lim_x = _expand(ssm_lam_re[0]), _expand(ssm_lam_im[0])
    lst_x = jnp.broadcast_to(ssm_log_step[0][:, None], (N_GROUPS, SSM_STATE * SSM_GROUP))
    b_re_x = ssm_b_re[0].reshape(N_GROUPS, -1)
    b_im_x = ssm_b_im[0].reshape(N_GROUPS, -1)
    ar_x, ai_x, bbr_x, bbi_x = _ssm_prep(lre_x, lim_x, lst_x, b_re_x, b_im_x)
    lam_r = ar_x[:, ::SSM_GROUP].reshape(1, N_STATE)
    lam_i = ai_x[:, ::SSM_GROUP].reshape(1, N_STATE)
    big_b_re = _block_diag_b(bbr_x.reshape(N_GROUPS, SSM_STATE, SSM_GROUP)).astype(BF16)
    big_b_im = _block_diag_b(bbi_x.reshape(N_GROUPS, SSM_STATE, SSM_GROUP)).astype(BF16)
    big_c_re = _block_diag_c(ssm_c_re[0]).astype(BF16)
    big_c_im = _block_diag_c(ssm_c_im[0]).astype(BF16)
    head = jnp.arange(D_SSM)
    avg16 = jnp.where(head[:, None] // SSM_GROUP == head[None, :] // SSM_GROUP, 1.0 / SSM_GROUP, 0.0).astype(BF16)
    hd = D_CONV // CONV_HEADS
    avg64 = jnp.where(head[:, None] // hd == head[None, :] // hd, 1.0 / hd, 0.0).astype(BF16)

    (proj, h1), (w_down_s, ffn_conv_s, glu_s, w_out_s, conv_s) = _pre_mix(
        xt, sc1, sh1, g_pre_mix, w_in_s, tw,
        ([w_down[0].astype(BF16), ffn_conv_w[0], glu_w[0].astype(BF16), w_out[0].astype(BF16), conv_w[0]], False))
    glu_full = glu_s.reshape(D_SSM, D_SSM)
    w_out_full = w_out_s.reshape(D_MODEL, D_MODEL)
    cw_full = conv_s.transpose(1, 0, 2).reshape(3, D_CONV)
    wd4 = w_down_s.reshape(4, FF_SHARD, D_MODEL)
    u_perm = _to_scan_rows(proj[:, :D_SSM])
    (s_re, s_im, y_perm), (w_up_s,) = _ssm_fwd(u_perm, big_b_re, big_b_im, big_c_re, big_c_im, lam_r, lam_i,
                                               ([w_up[0].T.astype(BF16)], False))
    yssm = _from_scan_rows(y_perm)
    mix_args = (ssm_d, glu_full, glu_b, g_out_ssm, cw_full, g_out_conv, avg16, avg64)
    ycat = _mix_fwd(yssm, proj, *mix_args, tm)
    o, x1, h2 = _out_proj(ycat, w_out_full, xt, gt1, g_post_mix, g_pre_ffn, sc2, sh2, tm)
    up8, hid8 = _ffn_up(h2, w_up_s, ffn_conv_s, tw)
    hid4 = hid8.reshape(2, 4, T, FF_SHARD)
    ddn, dx2, loss_parts, d_gt2, d_g_post_ffn = _ffn_down(hid4, wd4, x1, tgt, gt2, g_post_ffn, tm)
    loss_local = jnp.sum(loss_parts[:, 0, 0])

    got = {}
    dhid, act = _ffn_dact(ddn, wd4, hid4, tw)
    g_w_down = _grad_tn(act, ddn, pl.BlockSpec((None, tw, FF_SHARD), lambda g, k: (g, k, 0)),
                        pl.BlockSpec((tw, D_MODEL), lambda g, k: (k, 0)), 4, FF_SHARD, D_MODEL, tw, 'grad_w_down')
    (dup8, dcw_ffn), (got['w_down'],) = _ffn_dup(dhid.reshape(N_DEV, T, FF_SHARD), up8, ffn_conv_s, tw,
                                                 ([g_w_down.reshape(N_DEV, D_FF // N_DEV, D_MODEL)], True))
    g_w_up_halves = _grad_tn(dup8, h2, pl.BlockSpec((None, tw, FF_SHARD), lambda g, k: (g, k, 0)),
                             pl.BlockSpec((tw, D_MODEL), lambda g, k: (k, 0)), N_DEV, FF_SHARD, D_MODEL, tw,
                             'grad_w_up', parts=2)
    (dx1, d_sh2, d_sc2, d_g_pre_ffn, d_o, d_gt1, d_g_post_mix), (got_up_0, got['ffn_conv_w']) = _pre_norm_bwd(
        dup8, pl.BlockSpec((2, tw, FF_SHARD), lambda i, j: (j, i, 0)), w_up_s, x1, dx2, sc2, g_pre_ffn, tw,
        'ffn_in_bwd', ([g_w_up_halves[0], dcw_ffn], True), below=(o, gt1, g_post_mix), group=2, w_t=True)

    g_w_out = _grad_tn(ycat, d_o, pl.BlockSpec((tw, D_MODEL), lambda g, k: (k, 0)),
                       pl.BlockSpec((tw, D_MODEL), lambda g, k: (k, 0)), 1, D_MODEL, D_MODEL, tw, 'grad_w_out')
    (dy, dconv, dbg, z_b, dlin_b, sums), (got['w_out'],) = _mix_bwd(
        d_o, w_out_full, yssm, proj, *mix_args, tm, ([g_w_out.reshape(N_DEV, D_MODEL // N_DEV, D_MODEL)], True))
    g_glu_w = _grad_tn(z_b, dlin_b, pl.BlockSpec((tw, D_SSM), lambda g, k: (k, 0)),
                       pl.BlockSpec((tw, D_SSM), lambda g, k: (k, 0)), 1, D_SSM, D_SSM, tw, 'grad_glu_w')
    dy_perm = _to_scan_rows(dy)
    (du_perm, dbr_blk, dbi_blk, dcr_blk, dci_blk, dar_blk, dai_blk), (got_up_1, got['glu_w']) = _ssm_bwd(
        dy_perm, u_perm, s_re, s_im, big_b_re, big_b_im, big_c_re, big_c_im, lam_r, lam_i,
        ([g_w_up_halves[1], g_glu_w.reshape(N_DEV, D_SSM // N_DEV, D_SSM)], True))
    du_ssm = _from_scan_rows(du_perm)
    dproj = _mix_bwd_proj(dconv, proj, du_ssm, dy, ssm_d, dbg, cw_full, tm)
    dbb_re = _diag_blocks(dbr_blk, True).reshape(N_GROUPS, -1)
    dbb_im = _diag_blocks(dbi_blk, True).reshape(N_GROUPS, -1)
    d_c_re = _diag_blocks(dcr_blk, False).transpose(0, 2, 1)
    d_c_im = _diag_blocks(dci_blk, False).transpose(0, 2, 1)
    lane = jnp.arange(SSM_STATE * SSM_GROUP)
    seg = jnp.where(lane[:, None] // SSM_GROUP == lane[None, :] // SSM_GROUP, 1.0, 0.0).astype(BF16)
    d_b_re_x, d_b_im_x, d_lre_x, d_lim_x, d_lst = _ssm_prep_bwd(
        lre_x, lim_x, lst_x, b_re_x, b_im_x, dbb_re, dbb_im, _expand(dar_blk.reshape(N_GROUPS, SSM_STATE)),
        _expand(dai_blk.reshape(N_GROUPS, SSM_STATE)), seg)

    row = lambda a: a.reshape(-1, PACK_COLS)
    blank = jnp.zeros((1, PACK_COLS), F32)
    small_pack = jnp.concatenate([
        d_b_re_x, d_b_im_x, row(d_c_re), row(d_c_im), blank, blank, d_gt1, d_sh2, d_sc2, d_gt2, blank,
        d_g_post_mix, row(d_lre_x[:, ::SSM_GROUP]), row(d_lim_x[:, ::SSM_GROUP]),
        jnp.pad(d_lst.reshape(1, N_GROUPS), ((0, 0), (0, PACK_COLS - N_GROUPS))), row(sums[0:4]), d_g_pre_ffn,
        d_g_post_ffn, jnp.zeros((SMALL_ROWS - 145, PACK_COLS), F32)])
    g_w_in, (small_all,) = _grad_w_in(h1, dproj, tw, ([small_pack], False))
    g_conv_slots = jnp.concatenate([sums[4:7], jnp.zeros((5, D_CONV), F32)]).reshape(
        8, N_DEV, D_CONV // N_DEV).transpose(1, 0, 2)
    (grad_x, d_sh1, d_sc1, d_g_pre_mix), (got['w_in'], got['conv_w']) = _pre_norm_bwd(
        dproj, pl.BlockSpec((tw, D_IN_PROJ), lambda i, j: (i, j)), w_in_s, xt, dx1, sc1, g_pre_mix, tw,
        'mix_in_bwd', ([g_w_in, g_conv_slots], True), group=N_DEV)
    late_pack = jnp.concatenate([d_sh1, d_sc1, d_g_pre_mix, jnp.full((1, PACK_COLS), loss_local, F32),
                                 jnp.zeros((4, PACK_COLS), F32)])
    (late_all,) = _exchange([late_pack], name='gather_late_grads', scatter=False)
    loss = jnp.sum(late_all[:, 3, 0])
    res = _adamw_small(small_all, late_all, wts, mom_m, mom_v)

    dmod_all = jnp.concatenate([late_all[:, 0:2, :], small_all[:, B_ADA_ROW + 2:B_ADA_ROW + N_MOD, :]],
                               axis=1).reshape(N_DEV, N_MOD * D_MODEL)
    dmod_cols = lax.dynamic_slice(dmod_all, (0, me * ADA_SHARD), (N_DEV, ADA_SHARD))
    g_w_ada = _grad_w_ada(c_act.T, dmod_cols)

    pieces = {n: [slots[:, :3, :] if n in ('conv_w', 'ffn_conv_w') else slots] for n, slots in got.items()}
    for n, parts in pieces.items():
        outs = _adamw(parts, wts[n][0], mom_m[n][0], mom_v[n][0], 'adamw_' + n)
        for kind, val in zip(('g', 'd', 'm', 'v'), outs):
            res[kind, n] = val[None]
    outs = _adamw([got_up_0, got_up_1], w_up[0].T, m_w_up[0].T, v_w_up[0].T, 'adamw_w_up')
    for kind, val in zip(('g', 'd', 'm', 'v'), outs):
        res[kind, 'w_up'] = val.T[None]
    outs = _adamw([g_w_ada[None]], w_ada[0], m_w_ada[0], v_w_ada[0], 'adamw_w_ada')
    for kind, val in zip(('g', 'd', 'm', 'v'), outs):
        res[kind, 'w_ada'] = val[None]

    return (loss, grad_x[None], *[res['g', n] for n in WEIGHTS], *[res['d', n] for n in WEIGHTS],
            *[res['m', n] for n in WEIGHTS], *[res['v', n] for n in WEIGHTS])
```

```python
import math

import jax
import jax.numpy as jnp
from jax import lax
from jax.experimental import pallas as pl
from jax.experimental.pallas import tpu as pltpu

F32, BF16 = jnp.float32, jnp.bfloat16

D_MODEL = 1024
D_SSM = 512
D_CONV = 512
SSM_GROUP = 16
N_GROUPS = 32
SSM_STATE = 64
N_STATE = N_GROUPS * SSM_STATE
CONV_HEADS = 8
D_FF = 2816
N_MOD = 6
D_IN_PROJ = D_SSM + 3 * D_CONV
N_DEV = 8
FF_SHARD = 2 * D_FF // N_DEV
IN_SHARD = D_IN_PROJ // N_DEV
ADA_SHARD = N_MOD * D_MODEL // N_DEV
EPS = 1e-6
LAMBDA_RE_MAX = -1e-4
ADAM_LR, ADAM_B1, ADAM_B2, ADAM_EPS, ADAM_WD, ADAM_STEP = 0.001, 0.9, 0.999, 1e-08, 0.01, 10
GELU_C = math.sqrt(2.0 / math.pi)
GELU_A = 0.044715

SUBLANES = 8
HALO = 8
HALO16 = 16
SCAN_UNROLL = 8
STATE_BLOCK = 512
CHAN_BLOCK = 128
VMEM_BIG = 48 << 20
VMEM_MOST = 58 << 20

WEIGHTS = ['w_ada', 'b_ada', 'g_pre_mix', 'g_post_mix', 'w_in', 'ssm_lam_re', 'ssm_lam_im', 'ssm_log_step',
           'ssm_b_re', 'ssm_b_im', 'ssm_c_re', 'ssm_c_im', 'ssm_d', 'glu_w', 'glu_b', 'g_out_ssm', 'conv_w',
           'g_out_conv', 'w_out', 'g_pre_ffn', 'g_post_ffn', 'w_up', 'ffn_conv_w', 'w_down']
SHARDED = ('w_ada', 'w_in', 'glu_w', 'conv_w', 'w_out', 'w_up', 'ffn_conv_w', 'w_down')
PACK_COLS = 1024


def _call(body, *, name, grid, in_specs, out_specs, out_shape, scratch=(), sem=None, vmem=None, ride=None):
    params = {}
    if vmem is not None:
        params['vmem_limit_bytes'] = vmem
    if ride is None:
        if sem is not None:
            params['dimension_semantics'] = sem
        return pl.pallas_call(body, name=name, grid=grid, in_specs=in_specs, out_specs=out_specs,
                              out_shape=out_shape, scratch_shapes=list(scratch),
                              compiler_params=pltpu.CompilerParams(**params))
    arrs, scatter = ride
    single = not isinstance(out_shape, (list, tuple))
    out_shape_l = [out_shape] if single else list(out_shape)
    out_specs_l = [out_specs] if single else list(out_specs)
    n, n_in, n_out, n_scr = len(arrs), len(in_specs), len(out_shape_l), len(scratch)
    any_spec = pl.BlockSpec(memory_space=pl.ANY)
    params['dimension_semantics'] = ('arbitrary',) * len(grid)

    def carried(*refs):
        ins, rin = refs[:n_in], refs[n_in:n_in + n]
        outs, rout = refs[n_in + n:n_in + n + n_out], refs[n_in + n + n_out:n_in + 2 * n + n_out]
        scr, sems = refs[n_in + 2 * n + n_out:n_in + 2 * n + n_out + n_scr], refs[n_in + 2 * n + n_out + n_scr:]
        first = pl.program_id(0) == 0
        last = pl.program_id(0) == grid[0] - 1
        for ax in range(1, len(grid)):
            first = jnp.logical_and(first, pl.program_id(ax) == 0)
            last = jnp.logical_and(last, pl.program_id(ax) == grid[ax] - 1)

        @pl.when(first)
        def _():
            _exchange_start(rin, rout, sems, scatter)

        body(*ins, *outs, *scr)

        @pl.when(last)
        def _():
            _exchange_wait(rin, rout, sems, scatter)

    call = pl.pallas_call(carried, name=name, grid=grid, in_specs=list(in_specs) + [any_spec] * n,
                          out_specs=out_specs_l + [any_spec] * n,
                          out_shape=out_shape_l + _exchange_shapes(arrs, scatter),
                          scratch_shapes=list(scratch) + _exchange_sems(n),
                          compiler_params=pltpu.CompilerParams(**params))

    def run(*args):
        res = call(*args, *arrs)
        own = res[0] if single else list(res[:n_out])
        return own, list(res[n_out:])

    return run


def _const(shape):
    nd = len(shape)
    return pl.BlockSpec(shape, lambda *_: (0,) * nd)


def _sds(shape, dtype=F32):
    return jax.ShapeDtypeStruct(shape, dtype)


def _dot(a, b):
    return jnp.dot(a, b, preferred_element_type=F32)


def _dot_nt(a, b):
    return lax.dot_general(a, b, (((1,), (1,)), ((), ())), preferred_element_type=F32)


def _dot_tn(a, b):
    return lax.dot_general(a, b, (((0,), (0,)), ((), ())), preferred_element_type=F32)


def _dot_split(x, mat, parts):
    acc = None
    rem = x
    for _ in range(parts):
        piece = rem.astype(BF16)
        rem = rem - piece.astype(F32)
        term = _dot(piece, mat)
        acc = term if acc is None else acc + term
    return acc


def _sigmoid(x):
    return 1.0 / (1.0 + jnp.exp(-x))


def _gelu(x):
    t = jnp.tanh(GELU_C * (x + GELU_A * x * x * x))
    return 0.5 * x * (1.0 + t), t


def _gelu_grad(x, t):
    return 0.5 * (1.0 + t) + 0.5 * x * (1.0 - t * t) * GELU_C * (1.0 + 3.0 * GELU_A * x * x)


def _rsqrt_mean(x):
    return lax.rsqrt(jnp.mean(x * x, axis=-1, keepdims=True) + EPS)


def _colsum(x):
    return jnp.sum(x, axis=0, keepdims=True)


def _shifts_down(x, halo):
    ext = jnp.concatenate([halo, x], axis=0)
    return pltpu.roll(ext, 1, 0)[halo.shape[0]:], pltpu.roll(ext, 2, 0)[halo.shape[0]:]


def _shifts_up(x, halo):
    n = x.shape[0]
    ext = jnp.concatenate([x, halo], axis=0)
    total = ext.shape[0]
    return pltpu.roll(ext, total - 1, 0)[:n], pltpu.roll(ext, total - 2, 0)[:n]


def _conv3(x, halo, w_ref):
    x1, x2 = _shifts_down(x, halo)
    return w_ref[0:1, :] * x2 + w_ref[1:2, :] * x1 + w_ref[2:3, :] * x, x1, x2


def _conv3_t(g, halo, w_ref):
    g1, g2 = _shifts_up(g, halo)
    return w_ref[2:3, :] * g + w_ref[1:2, :] * g1 + w_ref[0:1, :] * g2, g1, g2


def _silu_parts(x):
    s = _sigmoid(x)
    return x * s, s * (1.0 + x * (1.0 - s))


def _norm_bwd(dn, x, r, g):
    gd = g * dn
    return r * gd - x * (r * r * r) * jnp.mean(gd * x, axis=-1, keepdims=True)


def _head_norm_bwd(dn, y, rs, g, avg):
    gd = g * dn
    return rs * gd - y * (rs * rs * rs) * _dot_split(gd * y, avg, 2)


def _me():
    x, y, c = lax.axis_index('x'), lax.axis_index('y'), lax.axis_index('c')
    return x, y, c, 4 * x + 2 * y + c


def _peer(k):
    x, y, c, _ = _me()
    px = 1 - x if k & 4 else x
    py = 1 - y if k & 2 else y
    pc = 1 - c if k & 1 else c
    return (px, py, pc), 4 * px + 2 * py + pc


SIBLING = 1
OTHER_CHIPS = (2, 4, 6)


def _remote(src, dst, sems, a, k, dev):
    return pltpu.make_async_remote_copy(src_ref=src, dst_ref=dst, send_sem=sems[0].at[a, k - 1],
                                        recv_sem=sems[1].at[a, k - 1], device_id=dev,
                                        device_id_type=pl.DeviceIdType.MESH)


def _exchange_copies(ins, outs, sems, scatter):
    me = _me()[3]
    local, first, relay, arrivals = [], [], [], []
    for a in range(len(ins)):
        src = ins[a].at[me] if scatter else ins[a]
        local.append(pltpu.make_async_copy(src, outs[a].at[me], sems[2].at[a]))
        for k in range(1, N_DEV):
            dev, idx = _peer(k)
            landed = _remote(src, outs[a].at[idx], sems, a, k, dev)
            if scatter:
                first.append(_remote(ins[a].at[idx], outs[a].at[me], sems, a, k, dev))
                arrivals.append(landed)
            elif k == SIBLING:
                first.append(_remote(src, outs[a].at[me], sems, a, k, dev))
                arrivals.append(landed)
            elif k in OTHER_CHIPS:
                first.append(_remote(src, outs[a].at[me], sems, a, k, dev))
                sib, _ = _peer(SIBLING)
                relay.append((landed, _remote(outs[a].at[idx], outs[a].at[idx], sems, a, k | SIBLING, sib)))
            else:
                arrivals.append(landed)
    return local, first, relay, arrivals


def _exchange_start(ins, outs, sems, scatter):
    local, first, _, _ = _exchange_copies(ins, outs, sems, scatter)
    for cp in local + first:
        cp.start()


def _exchange_wait(ins, outs, sems, scatter):
    local, first, relay, arrivals = _exchange_copies(ins, outs, sems, scatter)
    for landed, forward in relay:
        landed.wait_recv()
        forward.start()
    for cp in arrivals:
        cp.wait_recv()
    for cp in first + [forward for _, forward in relay]:
        cp.wait_send()
    for cp in local:
        cp.wait()


def _exchange_shapes(arrs, scatter):
    return [_sds(a.shape if scatter else (N_DEV,) + a.shape, a.dtype) for a in arrs]


def _exchange_sems(n):
    return [pltpu.SemaphoreType.DMA((n, N_DEV - 1)), pltpu.SemaphoreType.DMA((n, N_DEV - 1)),
            pltpu.SemaphoreType.DMA((n,))]


def _exchange(arrs, *, name, scatter):
    n = len(arrs)

    def body(*refs):
        _exchange_start(refs[:n], refs[n:2 * n], refs[2 * n:], scatter)
        _exchange_wait(refs[:n], refs[n:2 * n], refs[2 * n:], scatter)

    any_spec = pl.BlockSpec(memory_space=pl.ANY)
    outs = pl.pallas_call(body, name=name, out_shape=_exchange_shapes(arrs, scatter), in_specs=[any_spec] * n,
                          out_specs=[any_spec] * n, scratch_shapes=_exchange_sems(n))(*arrs)
    return list(outs)


def _mod_cols(c_all, w_ada, b_cols):
    def body(c_ref, w_ref, b_ref, mod_ref, act_ref):
        c = c_ref[...]
        act = c * _sigmoid(c)
        act_ref[...] = act
        mod_ref[...] = _dot(act.astype(BF16), w_ref[...].astype(BF16)) + b_ref[...]

    return _call(body, name='mod_cols', grid=(1,),
                 in_specs=[_const(c_all.shape), _const(w_ada.shape), _const(b_cols.shape)],
                 out_specs=[_const((N_DEV, ADA_SHARD)), _const(c_all.shape)],
                 out_shape=[_sds((N_DEV, ADA_SHARD)), _sds(c_all.shape)], vmem=VMEM_BIG)(c_all, w_ada, b_cols)


def _grad_w_ada(act_t, dmod_cols):
    def body(a_ref, d_ref, o_ref):
        o_ref[...] = _dot(a_ref[...], d_ref[...])

    return _call(body, name='grad_w_ada', grid=(1,), in_specs=[_const(act_t.shape), _const(dmod_cols.shape)],
                 out_specs=_const((D_MODEL, ADA_SHARD)), out_shape=_sds((D_MODEL, ADA_SHARD)),
                 vmem=VMEM_BIG)(act_t, dmod_cols)


def _pre_mix(x, sc, sh, g, w_s, tm, ride):
    T = x.shape[0]

    def body(x_ref, sc_ref, sh_ref, g_ref, w_ref, proj_ref, h_ref):
        @pl.when(pl.program_id(1) == 0)
        def _():
            xv = x_ref[...]
            h_ref[...] = ((xv * _rsqrt_mean(xv) * g_ref[...]) * (1.0 + sc_ref[...]) + sh_ref[...]).astype(BF16)

        for s in range(2):
            proj_ref[:, s * IN_SHARD:(s + 1) * IN_SHARD] = _dot(h_ref[...], w_ref[s])

    row = pl.BlockSpec((tm, D_MODEL), lambda i, j: (i, 0))
    vec = _const((1, D_MODEL))
    return _call(body, name='pre_mix', grid=(T // tm, N_DEV // 2),
                 in_specs=[row, vec, vec, vec, pl.BlockSpec((2, D_MODEL, IN_SHARD), lambda i, j: (j, 0, 0))],
                 out_specs=[pl.BlockSpec((tm, 2 * IN_SHARD), lambda i, j: (i, j)), row],
                 out_shape=[_sds((T, D_IN_PROJ)), _sds((T, D_MODEL), BF16)],
                 sem=('parallel', 'arbitrary'), ride=ride)(x, sc, sh, g, w_s)


def _halo_before(tm, rows=HALO):
    return lambda i: jnp.maximum(i * (tm // rows) - 1, 0)


def _halo_after(tm, T, rows=HALO):
    return lambda i: jnp.minimum((i + 1) * (tm // rows), T // rows - 1)


def _mix_fwd(yssm, proj, d, glu_w, glu_b, g_ssm, cw, g_conv, avg16, avg64, tm):
    T = yssm.shape[0]
    hb = _halo_before(tm)

    def body(y_ref, p_ref, ph_ref, d_ref, gw_ref, gb_ref, gs_ref, cw_ref, gc_ref, a16_ref, a64_ref, o_ref):
        i = pl.program_id(0)
        u = p_ref[:, 0:D_SSM]
        y = y_ref[...] + d_ref[...] * u
        z, _ = _gelu(y)
        gate = _sigmoid(_dot(z.astype(BF16), gw_ref[...]) + gb_ref[...])
        ya = z * gate
        rs = lax.rsqrt(_dot_split(ya * ya, a16_ref[...], 2) + EPS)
        o_ref[:, 0:D_SSM] = (ya * rs * gs_ref[...]).astype(BF16)
        bg = p_ref[:, D_SSM:D_SSM + D_CONV]
        cv = p_ref[:, D_SSM + D_CONV:D_SSM + 2 * D_CONV] * p_ref[:, D_SSM + 2 * D_CONV:D_IN_PROJ]
        hv = ph_ref[:, D_SSM + D_CONV:D_SSM + 2 * D_CONV] * ph_ref[:, D_SSM + 2 * D_CONV:D_IN_PROJ]
        hv = jnp.where(i > 0, hv, 0.0)
        conv, _, _ = _conv3(cv, hv, cw_ref)
        yb = bg * conv
        rsb = lax.rsqrt(_dot_split(yb * yb, a64_ref[...], 2) + EPS)
        o_ref[:, D_SSM:D_MODEL] = (yb * rsb * gc_ref[...]).astype(BF16)

    vec = _const((1, D_SSM))
    sq = _const((D_SSM, D_SSM))
    return _call(body, name='mix_fwd', grid=(T // tm,),
                 in_specs=[pl.BlockSpec((tm, D_SSM), lambda i: (i, 0)), pl.BlockSpec((tm, D_IN_PROJ), lambda i: (i, 0)),
                           pl.BlockSpec((HALO, D_IN_PROJ), lambda i: (hb(i), 0)), vec, sq, vec, vec,
                           _const((3, D_CONV)), vec, sq, sq],
                 out_specs=pl.BlockSpec((tm, D_MODEL), lambda i: (i, 0)), out_shape=_sds((T, D_MODEL), BF16),
                 sem=('parallel',), vmem=VMEM_BIG)(yssm, proj, proj, d, glu_w, glu_b, g_ssm, cw, g_conv, avg16, avg64)


def _out_proj(ycat, w_out, x, gt, g_post, g_pre, sc, sh, tm):
    T = x.shape[0]

    def body(y_ref, w_ref, x_ref, gt_ref, gp_ref, g2_ref, sc_ref, sh_ref, o_ref, x1_ref, h_ref):
        o = _dot(y_ref[...], w_ref[...])
        o_ref[...] = o
        x1 = x_ref[...] + gt_ref[...] * (o * _rsqrt_mean(o) * gp_ref[...])
        x1_ref[...] = x1
        h_ref[...] = ((x1 * _rsqrt_mean(x1) * g2_ref[...]) * (1.0 + sc_ref[...]) + sh_ref[...]).astype(BF16)

    row = pl.BlockSpec((tm, D_MODEL), lambda i: (i, 0))
    vec = _const((1, D_MODEL))
    return _call(body, name='out_proj', grid=(T // tm,),
                 in_specs=[row, _const((D_MODEL, D_MODEL)), row, vec, vec, vec, vec, vec],
                 out_specs=[row, row, row],
                 out_shape=[_sds((T, D_MODEL)), _sds((T, D_MODEL)), _sds((T, D_MODEL), BF16)],
                 sem=('parallel',), vmem=VMEM_BIG)(ycat, w_out, x, gt, g_post, g_pre, sc, sh)


def _ffn_up(h2, w_s, cw8, tm):
    T = h2.shape[0]
    hb = _halo_before(tm, HALO16)

    def body(h_ref, hh_ref, w_ref, cw_ref, up_ref, hid_ref):
        up = _dot_nt(h_ref[...], w_ref[...])
        up_ref[...] = up.astype(BF16)
        before = jnp.where(pl.program_id(0) > 0, _dot_nt(hh_ref[...], w_ref[...]), 0.0)
        hid_ref[...] = _conv3(up, before, cw_ref)[0].astype(BF16)

    out = pl.BlockSpec((None, tm, FF_SHARD), lambda i, j: (j, i, 0))
    return _call(body, name='ffn_up', grid=(T // tm, N_DEV),
                 in_specs=[pl.BlockSpec((tm, D_MODEL), lambda i, j: (i, 0)),
                           pl.BlockSpec((HALO16, D_MODEL), lambda i, j: (hb(i), 0)),
                           pl.BlockSpec((None, FF_SHARD, D_MODEL), lambda i, j: (j, 0, 0)),
                           pl.BlockSpec((None, 3, FF_SHARD), lambda i, j: (j, 0, 0))],
                 out_specs=[out, out], out_shape=[_sds((N_DEV, T, FF_SHARD), BF16)] * 2,
                 sem=('parallel', 'parallel'))(h2, h2, w_s, cw8)


def _ffn_down(hid4, wd4, x1, tgt, gt, g_post, tm):
    T = x1.shape[0]
    nb = T // tm

    def body(a_ref, w_ref, x1_ref, t_ref, gt_ref, g_ref, ddn_ref, dx_ref, loss_ref, dgt_ref, dg_ref, dn_ref):
        i, j = pl.program_id(0), pl.program_id(1)
        part = None
        for s in range(2):
            act = (_silu_parts(a_ref[0, s].astype(F32))[0] * a_ref[1, s].astype(F32)).astype(BF16)
            term = _dot(act, w_ref[s])
            part = term if part is None else part + term

        @pl.when(jnp.logical_and(i == 0, j == 0))
        def _():
            dgt_ref[...] = jnp.zeros_like(dgt_ref)
            dg_ref[...] = jnp.zeros_like(dg_ref)

        @pl.when(j == 0)
        def _():
            dn_ref[...] = part

        @pl.when(j > 0)
        def _():
            dn_ref[...] += part

        @pl.when(j == 1)
        def _():
            dn, gv, gate = dn_ref[...], g_ref[...], gt_ref[...]
            r = _rsqrt_mean(dn)
            normed = dn * r * gv
            err = x1_ref[...] + gate * normed - t_ref[...]
            dx = err * (1.0 / D_MODEL)
            dx_ref[...] = dx
            tot = jnp.sum(jnp.sum(err * err, axis=1, keepdims=True), axis=0, keepdims=True) * (0.5 / D_MODEL)
            loss_ref[...] = jnp.broadcast_to(tot, (8, 128))
            dgt_ref[...] += _colsum(dx * normed)
            dnn = dx * gate
            dg_ref[...] += _colsum(dnn * dn * r)
            ddn_ref[...] = _norm_bwd(dnn, dn, r, gv).astype(BF16)

    row = pl.BlockSpec((tm, D_MODEL), lambda i, j: (i, 0))
    vec = _const((1, D_MODEL))
    return _call(body, name='ffn_down', grid=(nb, 2),
                 in_specs=[pl.BlockSpec((2, 2, tm, FF_SHARD), lambda i, j: (0, j, i, 0)),
                           pl.BlockSpec((2, FF_SHARD, D_MODEL), lambda i, j: (j, 0, 0)), row, row, vec, vec],
                 out_specs=[row, row, pl.BlockSpec((None, 8, 128), lambda i, j: (i, 0, 0)), vec, vec],
                 out_shape=[_sds((T, D_MODEL), BF16), _sds((T, D_MODEL)), _sds((nb, 8, 128)), _sds((1, D_MODEL)),
                            _sds((1, D_MODEL))],
                 scratch=[pltpu.VMEM((tm, D_MODEL), F32)], sem=('arbitrary', 'arbitrary'),
                 vmem=VMEM_BIG)(hid4, wd4, x1, tgt, gt, g_post)


def _ssm_prep(lre, lim, lst, b_re, b_im):
    def body(lre_ref, lim_ref, lst_ref, br_ref, bi_ref, ar_ref, ai_ref, bbr_ref, bbi_ref):
        ar, ai, qr, qi = _zoh(lre_ref[...], lim_ref[...], lst_ref[...])[:4]
        ar_ref[...] = ar
        ai_ref[...] = ai
        bbr_ref[...] = qr * br_ref[...] - qi * bi_ref[...]
        bbi_ref[...] = qr * bi_ref[...] + qi * br_ref[...]

    shp = lre.shape
    return _call(body, name='ssm_prep', grid=(1,), in_specs=[_const(shp)] * 5, out_specs=[_const(shp)] * 4,
                 out_shape=[_sds(shp)] * 4)(lre, lim, lst, b_re, b_im)


def _zoh(lre, lim, lst):
    lr = jnp.minimum(lre, LAMBDA_RE_MAX)
    st = jnp.exp(lst)
    mag = jnp.exp(lr * st)
    ar = mag * jnp.cos(lim * st)
    ai = mag * jnp.sin(lim * st)
    den = lr * lr + lim * lim
    qr = ((ar - 1.0) * lr + ai * lim) / den
    qi = (ai * lr - (ar - 1.0) * lim) / den
    return ar, ai, qr, qi, lr, st, den


def _ssm_prep_bwd(lre, lim, lst, b_re, b_im, dbbr, dbbi, dar, dai, seg):
    def body(lre_ref, lim_ref, lst_ref, br_ref, bi_ref, dbbr_ref, dbbi_ref, dar_ref, dai_ref, seg_ref,
             dbr_ref, dbi_ref, dlre_ref, dlim_ref, dlst_ref):
        lre_v = lre_ref[...]
        li = lim_ref[...]
        ar, ai, qr, qi, lr, st, den = _zoh(lre_v, li, lst_ref[...])
        br, bi, gbr, gbi = br_ref[...], bi_ref[...], dbbr_ref[...], dbbi_ref[...]
        dbr_ref[...] = qr * gbr + qi * gbi
        dbi_ref[...] = qr * gbi - qi * gbr
        gqr = _dot_split(br * gbr + bi * gbi, seg_ref[...], 3)
        gqi = _dot_split(br * gbi - bi * gbr, seg_ref[...], 3)
        ir, ii = lr / den, -li / den
        gar = dar_ref[...] + ir * gqr + ii * gqi
        gai = dai_ref[...] + ir * gqi - ii * gqr
        tr, ti = qr * ir - qi * ii, qr * ii + qi * ir
        glr = -(tr * gqr + ti * gqi)
        gli = -(tr * gqi - ti * gqr)
        gzr = ar * gar + ai * gai
        gzi = ar * gai - ai * gar
        glr = glr + st * gzr
        gli = gli + st * gzi
        gst = (lr * gzr + li * gzi) * st
        dlre_ref[...] = jnp.where(lre_v < LAMBDA_RE_MAX, glr, 0.0)
        dlim_ref[...] = gli
        dlst_ref[...] = jnp.sum(gst, axis=1, keepdims=True) * (1.0 / SSM_GROUP)

    shp = lre.shape
    return _call(body, name='ssm_prep_bwd', grid=(1,), in_specs=[_const(shp)] * 9 + [_const(seg.shape)],
                 out_specs=[_const(shp)] * 4 + [_const((N_GROUPS, 1))],
                 out_shape=[_sds(shp)] * 4 + [_sds((N_GROUPS, 1))], vmem=VMEM_BIG)(
                     lre, lim, lst, b_re, b_im, dbbr, dbbi, dar, dai, seg)


def _scan_specs(T):
    return dict(
        chan=pl.BlockSpec((T, CHAN_BLOCK), lambda cb: (0, cb)),
        state=pl.BlockSpec((T, STATE_BLOCK), lambda cb: (0, cb)),
        b=pl.BlockSpec((CHAN_BLOCK, STATE_BLOCK), lambda cb: (cb, cb)),
        c=pl.BlockSpec((STATE_BLOCK, CHAN_BLOCK), lambda cb: (cb, cb)),
        lam=pl.BlockSpec((1, STATE_BLOCK), lambda cb: (0, cb)),
    )


def _complex_power(re, im, n):
    out = None
    while True:
        if n & 1:
            out = (re, im) if out is None else (out[0] * re - out[1] * im, out[0] * im + out[1] * re)
        n >>= 1
        if n == 0:
            return out
        re, im = re * re - im * im, 2.0 * re * im


def _rows8(i):
    if isinstance(i, int):
        return pl.ds(i * SUBLANES, SUBLANES)
    return pl.ds(pl.multiple_of(i * SUBLANES, SUBLANES), SUBLANES)


def _scan_loop(n_steps, body, init):
    trips = n_steps // SCAN_UNROLL

    def trip(t, carry):
        for u in range(SCAN_UNROLL):
            carry = body(t * SCAN_UNROLL + u, carry)
        return carry

    carry = lax.fori_loop(0, trips, trip, init)
    for step in range(trips * SCAN_UNROLL, n_steps):
        carry = body(step, carry)
    return carry


def _ssm_fwd(u_perm, b_re, b_im, c_re, c_im, lam_r, lam_i, ride):
    T = u_perm.shape[0]
    ls = T // SUBLANES
    rc = min(512, T)
    sp = _scan_specs(T)

    def body(u_ref, bre_ref, bim_ref, cre_ref, cim_ref, lr_ref, li_ref, so_re_ref, so_im_ref, y_ref, sre_ref, sim_ref):
        for c in range(T // rc):
            rows = pl.ds(c * rc, rc)
            ub = u_ref[rows, :].astype(BF16)
            sre_ref[rows, :] = _dot(ub, bre_ref[...])
            sim_ref[rows, :] = _dot(ub, bim_ref[...])
        shp = (SUBLANES, STATE_BLOCK)
        lr = jnp.broadcast_to(lr_ref[...], shp)
        li = jnp.broadcast_to(li_ref[...], shp)
        zero = jnp.zeros(shp, F32)

        def step(i, carry):
            sr, si = carry
            rows = _rows8(i)
            nr = lr * sr - li * si + sre_ref[rows, :]
            ni = lr * si + li * sr + sim_ref[rows, :]
            sre_ref[rows, :] = nr
            sim_ref[rows, :] = ni
            return nr, ni

        fr, fi = _scan_loop(ls, step, (zero, zero))
        pr, pi_ = _complex_power(lr, li, ls)
        row = lax.broadcasted_iota(jnp.int32, shp, 0)
        ir, ii = zero, zero
        for _ in range(SUBLANES - 1):
            er = fr + pr * ir - pi_ * ii
            ei = fi + pr * ii + pi_ * ir
            ir = jnp.where(row == 0, 0.0, pltpu.roll(er, 1, 0))
            ii = jnp.where(row == 0, 0.0, pltpu.roll(ei, 1, 0))

        def fix(i, carry):
            cr, ci = carry
            rows = _rows8(i)
            nr = lr * cr - li * ci
            ni = lr * ci + li * cr
            sre_ref[rows, :] += nr
            sim_ref[rows, :] += ni
            return nr, ni

        _scan_loop(ls, fix, (ir, ii))
        for c in range(T // rc):
            rows = pl.ds(c * rc, rc)
            s_r, s_i = sre_ref[rows, :].astype(BF16), sim_ref[rows, :].astype(BF16)
            so_re_ref[rows, :] = s_r
            so_im_ref[rows, :] = s_i
            y_ref[rows, :] = _dot(s_r, cre_ref[...]) - _dot(s_i, cim_ref[...])

    return _call(body, name='ssm_fwd', grid=(N_STATE // STATE_BLOCK,),
                 in_specs=[sp['chan'], sp['b'], sp['b'], sp['c'], sp['c'], sp['lam'], sp['lam']],
                 out_specs=[sp['state'], sp['state'], sp['chan']],
                 out_shape=[_sds((T, N_STATE), BF16), _sds((T, N_STATE), BF16), _sds((T, D_SSM))],
                 scratch=[pltpu.VMEM((T, STATE_BLOCK), F32), pltpu.VMEM((T, STATE_BLOCK), F32)],
                 sem=('arbitrary',), vmem=VMEM_MOST, ride=ride)(u_perm, b_re, b_im, c_re, c_im, lam_r, lam_i)


def _ssm_bwd(dy_perm, u_perm, s_re, s_im, b_re, b_im, c_re, c_im, lam_r, lam_i, ride):
    T = u_perm.shape[0]
    ls = T // SUBLANES
    rc = min(512, T)
    sp = _scan_specs(T)
    ncb = N_STATE // STATE_BLOCK

    def body(dy_ref, u_ref, sre_ref, sim_ref, bre_ref, bim_ref, cre_ref, cim_ref, lr_ref, li_ref,
             du_ref, dbr_ref, dbi_ref, dcr_ref, dci_ref, dar_ref, dai_ref, gre_ref, gim_ref):
        shp = (SUBLANES, STATE_BLOCK)
        zero = jnp.zeros(shp, F32)
        tail = pl.ds(T, SUBLANES)
        gre_ref[tail, :] = zero
        gim_ref[tail, :] = zero
        for c in range(T // rc):
            rows = pl.ds(c * rc, rc)
            dyb = dy_ref[rows, :].astype(BF16)
            gre_ref[rows, :] = _dot_nt(dyb, cre_ref[...])
            gim_ref[rows, :] = -_dot_nt(dyb, cim_ref[...])
        lr = jnp.broadcast_to(lr_ref[...], shp)
        li = jnp.broadcast_to(li_ref[...], shp)

        def step(k, carry):
            gr, gi = carry
            rows = _rows8(ls - 1 - k)
            nr = lr * gr + li * gi + gre_ref[rows, :]
            ni = lr * gi - li * gr + gim_ref[rows, :]
            gre_ref[rows, :] = nr
            gim_ref[rows, :] = ni
            return nr, ni

        fr, fi = _scan_loop(ls, step, (zero, zero))
        pr, pi_ = _complex_power(lr, -li, ls)
        row = lax.broadcasted_iota(jnp.int32, shp, 0)
        cr, ci = zero, zero
        for _ in range(SUBLANES - 1):
            er = fr + pr * cr - pi_ * ci
            ei = fi + pr * ci + pi_ * cr
            cr = jnp.where(row == SUBLANES - 1, 0.0, pltpu.roll(er, SUBLANES - 1, 0))
            ci = jnp.where(row == SUBLANES - 1, 0.0, pltpu.roll(ei, SUBLANES - 1, 0))

        def fix(k, carry):
            dr, di = carry
            rows = _rows8(ls - 1 - k)
            dr, di = lr * dr + li * di, lr * di - li * dr
            gre_ref[rows, :] += dr
            gim_ref[rows, :] += di
            return dr, di

        _scan_loop(ls, fix, (cr, ci))

        acc_r = jnp.zeros((1, STATE_BLOCK), F32)
        acc_i = jnp.zeros((1, STATE_BLOCK), F32)
        for c in range(T // rc):
            rows, nxt = pl.ds(c * rc, rc), pl.ds(c * rc + SUBLANES, rc)
            s_r, s_i = sre_ref[rows, :].astype(F32), sim_ref[rows, :].astype(F32)
            g_r, g_i = gre_ref[nxt, :], gim_ref[nxt, :]
            acc_r = acc_r + _colsum(g_r * s_r + g_i * s_i)
            acc_i = acc_i + _colsum(g_i * s_r - g_r * s_i)
        last = pl.ds(T - 2 * SUBLANES, 2 * SUBLANES)
        first = pl.ds(0, SUBLANES)
        spr = jnp.where(row == 0, 0.0, pltpu.roll(sre_ref[last, :].astype(F32)[SUBLANES:], 1, 0))
        spi = jnp.where(row == 0, 0.0, pltpu.roll(sim_ref[last, :].astype(F32)[SUBLANES:], 1, 0))
        gr, gi = gre_ref[first, :], gim_ref[first, :]
        dar_ref[...] = acc_r + _colsum(gr * spr + gi * spi)
        dai_ref[...] = acc_i + _colsum(gi * spr - gr * spi)

        for c in range(T // rc):
            rows = pl.ds(c * rc, rc)
            g_r, g_i = gre_ref[rows, :].astype(BF16), gim_ref[rows, :].astype(BF16)
            s_r, s_i = sre_ref[rows, :], sim_ref[rows, :]
            ub, dyb = u_ref[rows, :].astype(BF16), dy_ref[rows, :].astype(BF16)
            du_ref[rows, :] = _dot_nt(g_r, bre_ref[...]) + _dot_nt(g_i, bim_ref[...])
            parts = (_dot_tn(ub, g_r), _dot_tn(ub, g_i), _dot_tn(s_r, dyb), -_dot_tn(s_i, dyb))
            outs = (dbr_ref, dbi_ref, dcr_ref, dci_ref)
            for o_ref, part in zip(outs, parts):
                if c == 0:
                    o_ref[...] = part
                else:
                    o_ref[...] += part

    blk = lambda r, c: pl.BlockSpec((None, r, c), lambda cb: (cb, 0, 0))
    return _call(body, name='ssm_bwd', grid=(ncb,),
                 in_specs=[sp['chan'], sp['chan'], sp['state'], sp['state'], sp['b'], sp['b'], sp['c'], sp['c'],
                           sp['lam'], sp['lam']],
                 out_specs=[sp['chan'], blk(CHAN_BLOCK, STATE_BLOCK), blk(CHAN_BLOCK, STATE_BLOCK),
                            blk(STATE_BLOCK, CHAN_BLOCK), blk(STATE_BLOCK, CHAN_BLOCK), blk(1, STATE_BLOCK),
                            blk(1, STATE_BLOCK)],
                 out_shape=[_sds((T, D_SSM)), _sds((ncb, CHAN_BLOCK, STATE_BLOCK)), _sds((ncb, CHAN_BLOCK, STATE_BLOCK)),
                            _sds((ncb, STATE_BLOCK, CHAN_BLOCK)), _sds((ncb, STATE_BLOCK, CHAN_BLOCK)),
                            _sds((ncb, 1, STATE_BLOCK)), _sds((ncb, 1, STATE_BLOCK))],
                 scratch=[pltpu.VMEM((T + SUBLANES, STATE_BLOCK), F32), pltpu.VMEM((T + SUBLANES, STATE_BLOCK), F32)],
                 sem=('arbitrary',), vmem=VMEM_MOST, ride=ride)(dy_perm, u_perm, s_re, s_im, b_re, b_im, c_re, c_im,
                                                                lam_r, lam_i)


def _ffn_dact(ddn, wd4, hid4, tm):
    T = ddn.shape[0]

    def body(d_ref, w_ref, hid_ref, o_ref, act_ref):
        dact = _dot_nt(d_ref[...], w_ref[...])
        silu, dsilu = _silu_parts(hid_ref[0].astype(F32))
        hid_v = hid_ref[1].astype(F32)
        o_ref[0] = (dact * hid_v * dsilu).astype(BF16)
        o_ref[1] = (dact * silu).astype(BF16)
        act_ref[...] = (silu * hid_v).astype(BF16)

    blk = pl.BlockSpec((2, None, tm, FF_SHARD), lambda i, j: (0, j, i, 0))
    return _call(body, name='ffn_dact', grid=(T // tm, 4),
                 in_specs=[pl.BlockSpec((tm, D_MODEL), lambda i, j: (i, 0)),
                           pl.BlockSpec((None, FF_SHARD, D_MODEL), lambda i, j: (j, 0, 0)), blk],
                 out_specs=[blk, pl.BlockSpec((None, tm, FF_SHARD), lambda i, j: (j, i, 0))],
                 out_shape=[_sds((2, 4, T, FF_SHARD), BF16), _sds((4, T, FF_SHARD), BF16)],
                 sem=('parallel', 'parallel'), vmem=VMEM_BIG)(ddn, wd4, hid4)


def _ffn_dup(dhid8, up8, cw8, tm, ride):
    T = up8.shape[1]
    nb = T // tm
    ha = _halo_after(tm, T, HALO16)

    def body(dh_ref, dha_ref, up_ref, cw_ref, dup_ref, dcw_ref):
        i = pl.program_id(1)

        @pl.when(i == 0)
        def _():
            dcw_ref[...] = jnp.zeros_like(dcw_ref)

        dh = dh_ref[...].astype(F32)
        dup, dh1, dh2 = _conv3_t(dh, jnp.where(i < nb - 1, dha_ref[...].astype(F32), 0.0), cw_ref)
        dup_ref[...] = dup.astype(BF16)
        up = up_ref[...].astype(F32)
        dcw_ref[0:1, :] += _colsum(dh2 * up)
        dcw_ref[1:2, :] += _colsum(dh1 * up)
        dcw_ref[2:3, :] += _colsum(dh * up)

    main = pl.BlockSpec((None, tm, FF_SHARD), lambda j, i: (j, i, 0))
    return _call(body, name='ffn_dup', grid=(N_DEV, nb),
                 in_specs=[main, pl.BlockSpec((None, HALO16, FF_SHARD), lambda j, i: (j, ha(i), 0)), main,
                           pl.BlockSpec((None, 3, FF_SHARD), lambda j, i: (j, 0, 0))],
                 out_specs=[main, pl.BlockSpec((None, 8, FF_SHARD), lambda j, i: (j, 0, 0))],
                 out_shape=[_sds((N_DEV, T, FF_SHARD), BF16), _sds((N_DEV, 8, FF_SHARD))],
                 sem=('parallel', 'arbitrary'), vmem=VMEM_BIG, ride=ride)(dhid8, dhid8, up8, cw8)


def _grad_tn(a, b, a_spec, b_spec, groups, m, n, tk, name, ride=None, parts=1):
    T = a.shape[-2]
    nk = T // tk
    mp = m // parts

    def body(a_ref, b_ref, *refs):
        o_refs, acc_ref = refs[:parts], refs[parts]
        k = pl.program_id(1)
        part = _dot_tn(a_ref[...], b_ref[...])

        @pl.when(k == 0)
        def _():
            acc_ref[...] = part

        @pl.when(k > 0)
        def _():
            acc_ref[...] += part

        @pl.when(k == nk - 1)
        def _():
            for p, o_ref in enumerate(o_refs):
                o_ref[...] = acc_ref[p * mp:(p + 1) * mp, :].astype(BF16)

    out_spec = pl.BlockSpec((None, mp, n), lambda g, k: (g, 0, 0))
    res = _call(body, name=name, grid=(groups, nk), in_specs=[a_spec, b_spec], out_specs=[out_spec] * parts,
                out_shape=[_sds((groups, mp, n), BF16)] * parts, scratch=[pltpu.VMEM((m, n), F32)],
                sem=('parallel', 'arbitrary'), vmem=VMEM_BIG, ride=ride)(a, b)
    if parts > 1:
        return res
    return res[0] if ride is None else (res[0][0], res[1])


def _grad_w_in(h1, dproj, tk, ride):
    T = h1.shape[0]
    nk = T // tk
    half = D_IN_PROJ // 2

    def body(a_ref, b_ref, o_ref, acc_ref):
        k = pl.program_id(0)
        for h in range(2):
            cols = slice(h * half, (h + 1) * half)
            part = _dot_tn(a_ref[...], b_ref[:, cols])

            @pl.when(k == 0)
            def _():
                acc_ref[:, cols] = part

            @pl.when(k > 0)
            def _():
                acc_ref[:, cols] += part

        @pl.when(k == nk - 1)
        def _():
            for g in range(N_DEV):
                o_ref[g] = acc_ref[:, g * IN_SHARD:(g + 1) * IN_SHARD].astype(BF16)

    return _call(body, name='grad_w_in', grid=(nk,),
                 in_specs=[pl.BlockSpec((tk, D_MODEL), lambda k: (k, 0)), pl.BlockSpec((tk, D_IN_PROJ), lambda k: (k, 0))],
                 out_specs=_const((N_DEV, D_MODEL, IN_SHARD)), out_shape=_sds((N_DEV, D_MODEL, IN_SHARD), BF16),
                 scratch=[pltpu.VMEM((D_MODEL, D_IN_PROJ), F32)], sem=('arbitrary',), vmem=VMEM_BIG, ride=ride)(h1, dproj)


def _pre_norm_bwd(dz, dz_spec, w_s, xin, dres, sc, g, tm, name, ride, below=None, group=1, w_t=False):
    T = xin.shape[0]
    n = w_s.shape[1] if w_t else w_s.shape[2]
    mul = _dot if w_t else _dot_nt
    steps = N_DEV // group

    def body(dz_ref, w_ref, x_ref, dr_ref, sc_ref, g_ref, *refs):
        if below is None:
            dx_ref, dsh_ref, dsc_ref, dg_ref = refs
            sums = (dsh_ref, dsc_ref, dg_ref)
        else:
            v_ref, gate_ref, g2_ref, dx_ref, dsh_ref, dsc_ref, dg_ref, dv_ref, dgate_ref, dg2_ref = refs
            sums = (dsh_ref, dsc_ref, dg_ref, dgate_ref, dg2_ref)
        i, j = pl.program_id(0), pl.program_id(1)
        piece = (lambda s: dz_ref[s]) if dz.ndim == 3 else (lambda s: dz_ref[:, s * n:(s + 1) * n])
        part = mul(piece(0), w_ref[0])
        for s in range(1, group):
            part = part + mul(piece(s), w_ref[s])

        @pl.when(jnp.logical_and(i == 0, j == 0))
        def _():
            for s_ref in sums:
                s_ref[...] = jnp.zeros_like(s_ref)

        @pl.when(j == 0)
        def _():
            dx_ref[...] = part

        @pl.when(j > 0)
        def _():
            dx_ref[...] += part

        @pl.when(j == steps - 1)
        def _():
            dh, xv, gv = dx_ref[...], x_ref[...], g_ref[...]
            r = _rsqrt_mean(xv)
            dsh_ref[...] += _colsum(dh)
            dsc_ref[...] += _colsum(dh * (xv * r * gv))
            dxn = dh * (1.0 + sc_ref[...])
            dg_ref[...] += _colsum(dxn * xv * r)
            dx = dr_ref[...] + _norm_bwd(dxn, xv, r, gv)
            dx_ref[...] = dx
            if below is not None:
                v, g2 = v_ref[...], g2_ref[...]
                rv = _rsqrt_mean(v)
                dgate_ref[...] += _colsum(dx * (v * rv * g2))
                dn = dx * gate_ref[...]
                dg2_ref[...] += _colsum(dn * v * rv)
                dv_ref[...] = _norm_bwd(dn, v, rv, g2).astype(BF16)

    row = pl.BlockSpec((tm, D_MODEL), lambda i, j: (i, 0))
    vec = _const((1, D_MODEL))
    in_specs = [dz_spec, pl.BlockSpec((group,) + w_s.shape[1:], lambda i, j: (j, 0, 0)), row, row, vec, vec]
    out_specs = [row, vec, vec, vec]
    out_shape = [_sds((T, D_MODEL)), _sds((1, D_MODEL)), _sds((1, D_MODEL)), _sds((1, D_MODEL))]
    args = [dz, w_s, xin, dres, sc, g]
    if below is not None:
        in_specs += [row, vec, vec]
        out_specs += [row, vec, vec]
        out_shape += [_sds((T, D_MODEL), BF16), _sds((1, D_MODEL)), _sds((1, D_MODEL))]
        args += list(below)
    return _call(body, name=name, grid=(T // tm, steps), in_specs=in_specs, out_specs=out_specs,
                 out_shape=out_shape, sem=('arbitrary', 'arbitrary'), vmem=VMEM_MOST, ride=ride)(*args)


def _mix_bwd(d_o, w_out, yssm, proj, d, glu_w, glu_b, g_ssm, cw, g_conv, avg16, avg64, tm, ride):
    T = yssm.shape[0]
    hb = _halo_before(tm)

    def body(do_ref, wo_ref, y_ref, p_ref, ph_ref, d_ref, gw_ref, gb_ref, gs_ref, cw_ref, gc_ref, a16_ref, a64_ref,
             dy_ref, dconv_ref, dbg_ref, z_ref, dlin_ref, acc_ref):
        i = pl.program_id(0)
        dyc = _dot_nt(do_ref[...], wo_ref[...])

        @pl.when(i == 0)
        def _():
            acc_ref[...] = jnp.zeros_like(acc_ref)

        u = p_ref[:, 0:D_SSM]
        y = y_ref[...] + d_ref[...] * u
        z, t = _gelu(y)
        gate = _sigmoid(_dot(z.astype(BF16), gw_ref[...]) + gb_ref[...])
        ya = z * gate
        rs = lax.rsqrt(_dot_split(ya * ya, a16_ref[...], 2) + EPS)
        dna = dyc[:, 0:D_SSM]
        acc_ref[1:2, :] += _colsum(dna * ya * rs)
        dya = _head_norm_bwd(dna, ya, rs, gs_ref[...], a16_ref[...])
        dlin = dya * z * gate * (1.0 - gate)
        acc_ref[0:1, :] += _colsum(dlin)
        dlin_b = dlin.astype(BF16)
        dz = dya * gate + _dot_nt(dlin_b, gw_ref[...])
        dy = dz * _gelu_grad(y, t)
        acc_ref[3:4, :] += _colsum(dy * u)
        dy_ref[...] = dy
        z_ref[...] = z.astype(BF16)
        dlin_ref[...] = dlin_b

        bg = p_ref[:, D_SSM:D_SSM + D_CONV]
        cv = p_ref[:, D_SSM + D_CONV:D_SSM + 2 * D_CONV] * p_ref[:, D_SSM + 2 * D_CONV:D_IN_PROJ]
        hv = ph_ref[:, D_SSM + D_CONV:D_SSM + 2 * D_CONV] * ph_ref[:, D_SSM + 2 * D_CONV:D_IN_PROJ]
        hv = jnp.where(i > 0, hv, 0.0)
        conv, cv1, cv2 = _conv3(cv, hv, cw_ref)
        yb = bg * conv
        rsb = lax.rsqrt(_dot_split(yb * yb, a64_ref[...], 2) + EPS)
        dnb = dyc[:, D_SSM:D_MODEL]
        acc_ref[2:3, :] += _colsum(dnb * yb * rsb)
        dyb = _head_norm_bwd(dnb, yb, rsb, gc_ref[...], a64_ref[...])
        dbg_ref[...] = dyb * conv
        dconv = dyb * bg
        dconv_ref[...] = dconv
        acc_ref[4:5, :] += _colsum(dconv * cv2)
        acc_ref[5:6, :] += _colsum(dconv * cv1)
        acc_ref[6:7, :] += _colsum(dconv * cv)

    vec = _const((1, D_SSM))
    sq = _const((D_SSM, D_SSM))
    half = pl.BlockSpec((tm, D_SSM), lambda i: (i, 0))
    return _call(body, name='mix_bwd', grid=(T // tm,),
                 in_specs=[pl.BlockSpec((tm, D_MODEL), lambda i: (i, 0)), _const((D_MODEL, D_MODEL)), half,
                           pl.BlockSpec((tm, D_IN_PROJ), lambda i: (i, 0)),
                           pl.BlockSpec((HALO, D_IN_PROJ), lambda i: (hb(i), 0)), vec, sq, vec, vec,
                           _const((3, D_CONV)), vec, sq, sq],
                 out_specs=[half, half, half, half, half, _const((8, D_SSM))],
                 out_shape=[_sds((T, D_SSM)), _sds((T, D_SSM)), _sds((T, D_SSM)), _sds((T, D_SSM), BF16),
                            _sds((T, D_SSM), BF16), _sds((8, D_SSM))],
                 sem=('arbitrary',), vmem=VMEM_BIG, ride=ride)(d_o, w_out, yssm, proj, proj, d, glu_w, glu_b, g_ssm, cw,
                                                              g_conv, avg16, avg64)


def _mix_bwd_proj(dconv, proj, du_ssm, dy, d, dbg, cw, tm):
    T = dy.shape[0]
    nb = T // tm
    ha = _halo_after(tm, T)

    def body(dc_ref, dch_ref, cg_ref, v_ref, du_ref, dy_ref, d_ref, dbg_ref, cw_ref, o_ref):
        i = pl.program_id(0)
        dcv = _conv3_t(dc_ref[...], jnp.where(i < nb - 1, dch_ref[...], 0.0), cw_ref)[0]
        o_ref[:, 0:D_SSM] = (du_ref[...] + dy_ref[...] * d_ref[...]).astype(BF16)
        o_ref[:, D_SSM:D_SSM + D_CONV] = dbg_ref[...].astype(BF16)
        o_ref[:, D_SSM + D_CONV:D_SSM + 2 * D_CONV] = (dcv * v_ref[...]).astype(BF16)
        o_ref[:, D_SSM + 2 * D_CONV:D_IN_PROJ] = (dcv * cg_ref[...]).astype(BF16)

    half = pl.BlockSpec((tm, D_SSM), lambda i: (i, 0))
    return _call(body, name='mix_bwd_proj', grid=(nb,),
                 in_specs=[half, pl.BlockSpec((HALO, D_CONV), lambda i: (ha(i), 0)),
                           pl.BlockSpec((tm, D_CONV), lambda i: (i, 2)), pl.BlockSpec((tm, D_CONV), lambda i: (i, 3)),
                           half, half, _const((1, D_SSM)), half, _const((3, D_CONV))],
                 out_specs=pl.BlockSpec((tm, D_IN_PROJ), lambda i: (i, 0)), out_shape=_sds((T, D_IN_PROJ), BF16),
                 sem=('parallel',), vmem=VMEM_BIG)(dconv, dconv, proj, proj, du_ssm, dy, d, dbg, cw)


def _row_tile(rows, cols, slots):
    for cand in (512, 256, 128, 64, 32, 16, 8):
        if rows % cand == 0 and slots * cand * cols * 4 <= (2 << 20):
            return cand
    return rows


def _adamw_math(g, w, m, v):
    m2 = ADAM_B1 * m + (1.0 - ADAM_B1) * g
    v2 = ADAM_B2 * v + (1.0 - ADAM_B2) * (g * g)
    m_hat = m2 / (1.0 - ADAM_B1 ** ADAM_STEP)
    v_hat = v2 / (1.0 - ADAM_B2 ** ADAM_STEP)
    return -ADAM_LR * (m_hat / (jnp.sqrt(v_hat) + ADAM_EPS) + ADAM_WD * w), m2, v2


def _adamw(pieces, w, m, v, name):
    slots, _, cols = pieces[0].shape
    rows = sum(p.shape[1] for p in pieces)
    tr = _row_tile(pieces[0].shape[1], cols, slots)
    starts, pos = [], 0
    for p in pieces:
        assert p.shape[1] % tr == 0
        starts.append(pos)
        pos += p.shape[1] // tr

    def body(*refs):
        g_refs = refs[:len(pieces)]
        w_ref, m_ref, v_ref, go_ref, d_ref, mo_ref, vo_ref = refs[len(pieces):]
        i = pl.program_id(0)
        g = None
        for g_ref, start in zip(g_refs, starts):
            part = g_ref[0].astype(F32)
            for s in range(1, slots):
                part = part + g_ref[s].astype(F32)
            g = part if g is None else jnp.where(i >= start, part, g)
        go_ref[...] = g
        d_ref[...], mo_ref[...], vo_ref[...] = _adamw_math(g, w_ref[...], m_ref[...], v_ref[...])

    def piece_spec(start, count):
        return pl.BlockSpec((slots, tr, cols), lambda i: (0, jnp.clip(i - start, 0, count - 1), 0))

    blk = pl.BlockSpec((tr, cols), lambda i: (i, 0))
    return _call(body, name=name, grid=(rows // tr,),
                 in_specs=[piece_spec(s, p.shape[1] // tr) for s, p in zip(starts, pieces)] + [blk, blk, blk],
                 out_specs=[blk] * 4, out_shape=[_sds((rows, cols))] * 4, sem=('parallel',))(*pieces, w, m, v)


def _to_scan_rows(a):
    T, n = a.shape
    return a.reshape(SUBLANES, T // SUBLANES, n).transpose(1, 0, 2).reshape(T, n)


def _from_scan_rows(a):
    T, n = a.shape
    return a.reshape(T // SUBLANES, SUBLANES, n).transpose(1, 0, 2).reshape(T, n)


def _expand(a):
    return jnp.repeat(a, SSM_GROUP, axis=1)


def _block_diag(rows, row_group, col_group):
    r, n = rows.shape
    tiled = jnp.tile(rows, (1, N_GROUPS))
    keep = (jnp.arange(r)[:, None] // row_group) == (jnp.arange(n * N_GROUPS)[None, :] // col_group)
    return jnp.where(keep, tiled, 0.0)


def _block_diag_b(bb):
    return _block_diag(bb.transpose(0, 2, 1).reshape(D_SSM, SSM_STATE), SSM_GROUP, SSM_STATE)


def _block_diag_c(cc):
    return _block_diag(cc.transpose(0, 2, 1).reshape(N_STATE, SSM_GROUP), SSM_STATE, SSM_GROUP)


def _diag_blocks(x, chan_major):
    per = CHAN_BLOCK // SSM_GROUP
    eye = jnp.eye(per, dtype=x.dtype)
    if chan_major:
        x = x.reshape(-1, per, SSM_GROUP, per, SSM_STATE) * eye[None, :, None, :, None]
        return x.sum(axis=1).transpose(0, 2, 3, 1).reshape(N_GROUPS, SSM_STATE, SSM_GROUP)
    x = x.reshape(-1, per, SSM_STATE, per, SSM_GROUP) * eye[None, :, None, :, None]
    return x.sum(axis=3).reshape(N_GROUPS, SSM_STATE, SSM_GROUP)


SMALL_LAYOUT = {
    'ssm_b_re': (0, 0, 32, 1024), 'ssm_b_im': (32, 0, 32, 1024), 'ssm_c_re': (64, 0, 32, 1024),
    'ssm_c_im': (96, 0, 32, 1024), 'b_ada': (128, 0, 6, 1024), 'g_pre_mix': (134, 0, 1, 1024),
    'g_post_mix': (135, 0, 1, 1024), 'ssm_lam_re': (136, 0, 2, 1024), 'ssm_lam_im': (138, 0, 2, 1024),
    'ssm_log_step': (140, 0, 1, 32), 'glu_b': (141, 0, 1, 512), 'g_out_ssm': (141, 512, 1, 512),
    'g_out_conv': (142, 0, 1, 512), 'ssm_d': (142, 512, 1, 512), 'g_pre_ffn': (143, 0, 1, 1024),
    'g_post_ffn': (144, 0, 1, 1024)}
SMALL_ROWS = 152
B_ADA_ROW = SMALL_LAYOUT['b_ada'][0]
LATE_ROWS = {('b_ada', 0): 0, ('b_ada', 1): 1, ('g_pre_mix', 0): 2}


def _adamw_small(gathered, late, wts, mom_m, mom_v):
    names = list(SMALL_LAYOUT)
    n = len(names)

    def body(*refs):
        g_ref, late_ref, ins, outs = refs[0], refs[1], refs[2:2 + 3 * n], refs[2 + 3 * n:]
        for p, name in enumerate(names):
            r0, c0, rows, cols = SMALL_LAYOUT[name]
            pieces = [(0, rows)] if rows % 8 == 0 else [(r, 1) for r in range(rows)]
            for r, cnt in pieces:
                src_ref, first = (late_ref, LATE_ROWS[name, r]) if (name, r) in LATE_ROWS else (g_ref, r0 + r)
                g = src_ref[0, first:first + cnt, c0:c0 + cols]
                for s in range(1, N_DEV):
                    g = g + src_ref[s, first:first + cnt, c0:c0 + cols]
                w, m, v = (ins[3 * p + q][r:r + cnt, :] for q in range(3))
                res = (g,) + _adamw_math(g, w, m, v)
                for q in range(4):
                    outs[4 * p + q][r:r + cnt, :] = res[q]

    shapes = [SMALL_LAYOUT[name][2:] for name in names]
    args = [gathered, late]
    for name, shp in zip(names, shapes):
        args += [wts[name].reshape(shp), mom_m[name].reshape(shp), mom_v[name].reshape(shp)]
    outs = _call(body, name='adamw_small', grid=(1,),
                 in_specs=[_const(gathered.shape), _const(late.shape)]
                 + [_const(shp) for shp in shapes for _ in range(3)],
                 out_specs=[_const(shp) for shp in shapes for _ in range(4)],
                 out_shape=[_sds(shp) for shp in shapes for _ in range(4)], vmem=VMEM_BIG)(*args)
    res = {}
    for p, name in enumerate(names):
        for q, kind in enumerate(('g', 'd', 'm', 'v')):
            res[kind, name] = outs[4 * p + q].reshape(wts[name].shape)
    return res


def kernel(x, c, w_ada, b_ada, g_pre_mix, g_post_mix, w_in, ssm_lam_re, ssm_lam_im, ssm_log_step, ssm_b_re, ssm_b_im, ssm_c_re, ssm_c_im, ssm_d, glu_w, glu_b, g_out_ssm, conv_w, g_out_conv, w_out, g_pre_ffn, g_post_ffn, w_up, ffn_conv_w, w_down, loss_target, m_w_ada, m_b_ada, m_g_pre_mix, m_g_post_mix, m_w_in, m_ssm_lam_re, m_ssm_lam_im, m_ssm_log_step, m_ssm_b_re, m_ssm_b_im, m_ssm_c_re, m_ssm_c_im, m_ssm_d, m_glu_w, m_glu_b, m_g_out_ssm, m_conv_w, m_g_out_conv, m_w_out, m_g_pre_ffn, m_g_post_ffn, m_w_up, m_ffn_conv_w, m_w_down, v_w_ada, v_b_ada, v_g_pre_mix, v_g_post_mix, v_w_in, v_ssm_lam_re, v_ssm_lam_im, v_ssm_log_step, v_ssm_b_re, v_ssm_b_im, v_ssm_c_re, v_ssm_c_im, v_ssm_d, v_glu_w, v_glu_b, v_g_out_ssm, v_conv_w, v_g_out_conv, v_w_out, v_g_pre_ffn, v_g_post_ffn, v_w_up, v_ffn_conv_w, v_w_down):
    args = dict(locals())
    wts = {n: args[n] for n in WEIGHTS}
    mom_m = {n: args['m_' + n] for n in WEIGHTS}
    mom_v = {n: args['v_' + n] for n in WEIGHTS}
    T = x.shape[1]
    tm = min(512, T)
    tw = min(1024, T)
    tk = min(2048, T)
    me = _me()[3]
    xt, tgt = x[0], loss_target[0]

    c_all, w_in_s = _exchange([c, w_in[0].astype(BF16)], name='gather_first', scatter=False)
    c_all = c_all.reshape(N_DEV, D_MODEL)
    b_cols = lax.dynamic_slice(b_ada, (0, me * ADA_SHARD), (1, ADA_SHARD))
    mod_cols, c_act = _mod_cols(c_all, w_ada[0], b_cols)
    (mod_all,) = _exchange([mod_cols], name='gather_mod', scatter=False)
    mod = lax.dynamic_slice(mod_all, (0, me, 0), (N_DEV, 1, ADA_SHARD)).reshape(N_MOD, 1, D_MODEL)
    sh1, sc1, gt1, sh2, sc2, gt2 = [mod[k] for k in range(N_MOD)]


    lre_x, lim_x = _expand(ssm_lam_re[0]), _expand(ssm_lam_im[0])
    lst_x = jnp.broadcast_to(ssm_log_step[0][:, None], (N_GROUPS, SSM_STATE * SSM_GROUP))
    b_re_x = ssm_b_re[0].reshape(N_GROUPS, -1)
    b_im_x = ssm_b_im[0].reshape(N_GROUPS, -1)
    ar_x, ai_x, bbr_x, bbi_x = _ssm_prep(lre_x, lim_x, lst_x, b_re_x, b_im_x)
    lam_r = ar_x[:, ::SSM_GROUP].reshape(1, N_STATE)
    lam_i = ai_x[:, ::SSM_GROUP].reshape(1, N_STATE)
    big_b_re = _block_diag_b(bbr_x.reshape(N_GROUPS, SSM_STATE, SSM_GROUP)).astype(BF16)
    big_b_im = _block_diag_b(bbi_x.reshape(N_GROUPS, SSM_STATE, SSM_GROUP)).astype(BF16)
    big_c_re = _block_diag_c(ssm_c_re[0]).astype(BF16)
    big_c_im = _block_diag_c(ssm_c_im[0]).astype(BF16)
    head = jnp.arange(D_SSM)
    avg16 = jnp.where(head[:, None] // SSM_GROUP == head[None, :] // SSM_GROUP, 1.0 / SSM_GROUP, 0.0).astype(BF16)
    hd = D_CONV // CONV_HEADS
    avg64 = jnp.where(head[:, None] // hd == head[None, :] // hd, 1.0 / hd, 0.0).astype(BF16)

    (proj, h1), (w_down_s, ffn_conv_s, glu_s, w_out_s, conv_s) = _pre_mix(
        xt, sc1, sh1, g_pre_mix, w_in_s, tw,
        ([w_down[0].astype(BF16), ffn_conv_w[0], glu_w[0].astype(BF16), w_out[0].astype(BF16), conv_w[0]], False))
    glu_full = glu_s.reshape(D_SSM, D_SSM)
    w_out_full = w_out_s.reshape(D_MODEL, D_MODEL)
    cw_full = conv_s.transpose(1, 0, 2).reshape(3, D_CONV)
    wd4 = w_down_s.reshape(4, FF_SHARD, D_MODEL)
    u_perm = _to_scan_rows(proj[:, :D_SSM])
    (s_re, s_im, y_perm), (w_up_s,) = _ssm_fwd(u_perm, big_b_re, big_b_im, big_c_re, big_c_im, lam_r, lam_i,
                                               ([w_up[0].T.astype(BF16)], False))
    yssm = _from_scan_rows(y_perm)
    mix_args = (ssm_d, glu_full, glu_b, g_out_ssm, cw_full, g_out_conv, avg16, avg64)
    ycat = _mix_fwd(yssm, proj, *mix_args, tw)
    o, x1, h2 = _out_proj(ycat, w_out_full, xt, gt1, g_post_mix, g_pre_ffn, sc2, sh2, tw)
    up8, hid8 = _ffn_up(h2, w_up_s, ffn_conv_s, tw)
    hid4 = hid8.reshape(2, 4, T, FF_SHARD)
    ddn, dx2, loss_parts, d_gt2, d_g_post_ffn = _ffn_down(hid4, wd4, x1, tgt, gt2, g_post_ffn, tm)
    loss_local = jnp.sum(loss_parts[:, 0, 0])

    got = {}
    dhid, act = _ffn_dact(ddn, wd4, hid4, tw)
    g_w_down = _grad_tn(act, ddn, pl.BlockSpec((None, tk, FF_SHARD), lambda g, k: (g, k, 0)),
                        pl.BlockSpec((tk, D_MODEL), lambda g, k: (k, 0)), 4, FF_SHARD, D_MODEL, tk, 'grad_w_down')
    (dup8, dcw_ffn), (got['w_down'],) = _ffn_dup(dhid.reshape(N_DEV, T, FF_SHARD), up8, ffn_conv_s, tw,
                                                 ([g_w_down.reshape(N_DEV, D_FF // N_DEV, D_MODEL)], True))
    g_w_up_halves = _grad_tn(dup8, h2, pl.BlockSpec((None, tk, FF_SHARD), lambda g, k: (g, k, 0)),
                             pl.BlockSpec((tk, D_MODEL), lambda g, k: (k, 0)), N_DEV, FF_SHARD, D_MODEL, tk,
                             'grad_w_up', parts=2)
    (dx1, d_sh2, d_sc2, d_g_pre_ffn, d_o, d_gt1, d_g_post_mix), (got_up_0, got['ffn_conv_w']) = _pre_norm_bwd(
        dup8, pl.BlockSpec((2, tw, FF_SHARD), lambda i, j: (j, i, 0)), w_up_s, x1, dx2, sc2, g_pre_ffn, tw,
        'ffn_in_bwd', ([g_w_up_halves[0], dcw_ffn], True), below=(o, gt1, g_post_mix), group=2, w_t=True)

    g_w_out = _grad_tn(ycat, d_o, pl.BlockSpec((tk, D_MODEL), lambda g, k: (k, 0)),
                       pl.BlockSpec((tk, D_MODEL), lambda g, k: (k, 0)), 1, D_MODEL, D_MODEL, tk, 'grad_w_out')
    (dy, dconv, dbg, z_b, dlin_b, sums), (got['w_out'],) = _mix_bwd(
        d_o, w_out_full, yssm, proj, *mix_args, tm, ([g_w_out.reshape(N_DEV, D_MODEL // N_DEV, D_MODEL)], True))
    g_glu_w = _grad_tn(z_b, dlin_b, pl.BlockSpec((tk, D_SSM), lambda g, k: (k, 0)),
                       pl.BlockSpec((tk, D_SSM), lambda g, k: (k, 0)), 1, D_SSM, D_SSM, tk, 'grad_glu_w')
    dy_perm = _to_scan_rows(dy)
    (du_perm, dbr_blk, dbi_blk, dcr_blk, dci_blk, dar_blk, dai_blk), (got_up_1, got['glu_w']) = _ssm_bwd(
        dy_perm, u_perm, s_re, s_im, big_b_re, big_b_im, big_c_re, big_c_im, lam_r, lam_i,
        ([g_w_up_halves[1], g_glu_w.reshape(N_DEV, D_SSM // N_DEV, D_SSM)], True))
    du_ssm = _from_scan_rows(du_perm)
    dproj = _mix_bwd_proj(dconv, proj, du_ssm, dy, ssm_d, dbg, cw_full, tw)
    dbb_re = _diag_blocks(dbr_blk, True).reshape(N_GROUPS, -1)
    dbb_im = _diag_blocks(dbi_blk, True).reshape(N_GROUPS, -1)
    d_c_re = _diag_blocks(dcr_blk, False).transpose(0, 2, 1)
    d_c_im = _diag_blocks(dci_blk, False).transpose(0, 2, 1)
    lane = jnp.arange(SSM_STATE * SSM_GROUP)
    seg = jnp.where(lane[:, None] // SSM_GROUP == lane[None, :] // SSM_GROUP, 1.0, 0.0).astype(BF16)
    d_b_re_x, d_b_im_x, d_lre_x, d_lim_x, d_lst = _ssm_prep_bwd(
        lre_x, lim_x, lst_x, b_re_x, b_im_x, dbb_re, dbb_im, _expand(dar_blk.reshape(N_GROUPS, SSM_STATE)),
        _expand(dai_blk.reshape(N_GROUPS, SSM_STATE)), seg)

    row = lambda a: a.reshape(-1, PACK_COLS)
    blank = jnp.zeros((1, PACK_COLS), F32)
    small_pack = jnp.concatenate([
        d_b_re_x, d_b_im_x, row(d_c_re), row(d_c_im), blank, blank, d_gt1, d_sh2, d_sc2, d_gt2, blank,
        d_g_post_mix, row(d_lre_x[:, ::SSM_GROUP]), row(d_lim_x[:, ::SSM_GROUP]),
        jnp.pad(d_lst.reshape(1, N_GROUPS), ((0, 0), (0, PACK_COLS - N_GROUPS))), row(sums[0:4]), d_g_pre_ffn,
        d_g_post_ffn, jnp.zeros((SMALL_ROWS - 145, PACK_COLS), F32)])
    g_w_in, (small_all,) = _grad_w_in(h1, dproj, tk, ([small_pack], False))
    g_conv_slots = jnp.concatenate([sums[4:7], jnp.zeros((5, D_CONV), F32)]).reshape(
        8, N_DEV, D_CONV // N_DEV).transpose(1, 0, 2)
    (grad_x, d_sh1, d_sc1, d_g_pre_mix), (got['w_in'], got['conv_w']) = _pre_norm_bwd(
        dproj, pl.BlockSpec((tw, D_IN_PROJ), lambda i, j: (i, j)), w_in_s, xt, dx1, sc1, g_pre_mix, tw,
        'mix_in_bwd', ([g_w_in, g_conv_slots], True), group=N_DEV)
    late_pack = jnp.concatenate([d_sh1, d_sc1, d_g_pre_mix, jnp.full((1, PACK_COLS), loss_local, F32),
                                 jnp.zeros((4, PACK_COLS), F32)])
    (late_all,) = _exchange([late_pack], name='gather_late_grads', scatter=False)
    loss = jnp.sum(late_all[:, 3, 0])
    res = _adamw_small(small_all, late_all, wts, mom_m, mom_v)

    dmod_all = jnp.concatenate([late_all[:, 0:2, :], small_all[:, B_ADA_ROW + 2:B_ADA_ROW + N_MOD, :]],
                               axis=1).reshape(N_DEV, N_MOD * D_MODEL)
    dmod_cols = lax.dynamic_slice(dmod_all, (0, me * ADA_SHARD), (N_DEV, ADA_SHARD))
    g_w_ada = _grad_w_ada(c_act.T, dmod_cols)

    pieces = {n: [slots[:, :3, :] if n in ('conv_w', 'ffn_conv_w') else slots] for n, slots in got.items()}
    for n, parts in pieces.items():
        outs = _adamw(parts, wts[n][0], mom_m[n][0], mom_v[n][0], 'adamw_' + n)
        for kind, val in zip(('g', 'd', 'm', 'v'), outs):
            res[kind, n] = val[None]
    outs = _adamw([got_up_0, got_up_1], w_up[0].T, m_w_up[0].T, v_w_up[0].T, 'adamw_w_up')
    for kind, val in zip(('g', 'd', 'm', 'v'), outs):
        res[kind, 'w_up'] = val.T[None]
    outs = _adamw([g_w_ada[None]], w_ada[0], m_w_ada[0], v_w_ada[0], 'adamw_w_ada')
    for kind, val in zip(('g', 'd', 'm', 'v'), outs):
        res[kind, 'w_ada'] = val[None]

    return (loss, grad_x[None], *[res['g', n] for n in WEIGHTS], *[res['d', n] for n in WEIGHTS],
            *[res['m', n] for n in WEIGHTS], *[res['v', n] for n in WEIGHTS])
```

```python
import math

import jax
import jax.numpy as jnp
from jax import lax
from jax.experimental import pallas as pl
from jax.experimental.pallas import tpu as pltpu

F32, BF16 = jnp.float32, jnp.bfloat16

D_MODEL = 1024
D_SSM = 512
D_CONV = 512
SSM_GROUP = 16
N_GROUPS = 32
SSM_STATE = 64
N_STATE = N_GROUPS * SSM_STATE
CONV_HEADS = 8
D_FF = 2816
N_MOD = 6
D_IN_PROJ = D_SSM + 3 * D_CONV
N_DEV = 8
FF_SHARD = 2 * D_FF // N_DEV
IN_SHARD = D_IN_PROJ // N_DEV
ADA_SHARD = N_MOD * D_MODEL // N_DEV
EPS = 1e-6
LAMBDA_RE_MAX = -1e-4
ADAM_LR, ADAM_B1, ADAM_B2, ADAM_EPS, ADAM_WD, ADAM_STEP = 0.001, 0.9, 0.999, 1e-08, 0.01, 10
GELU_C = math.sqrt(2.0 / math.pi)
GELU_A = 0.044715

SUBLANES = 8
HALO = 8
HALO16 = 16
SCAN_UNROLL = 8
STATE_BLOCK = 512
CHAN_BLOCK = 128
VMEM_BIG = 48 << 20
VMEM_MOST = 58 << 20

WEIGHTS = ['w_ada', 'b_ada', 'g_pre_mix', 'g_post_mix', 'w_in', 'ssm_lam_re', 'ssm_lam_im', 'ssm_log_step',
           'ssm_b_re', 'ssm_b_im', 'ssm_c_re', 'ssm_c_im', 'ssm_d', 'glu_w', 'glu_b', 'g_out_ssm', 'conv_w',
           'g_out_conv', 'w_out', 'g_pre_ffn', 'g_post_ffn', 'w_up', 'ffn_conv_w', 'w_down']
SHARDED = ('w_ada', 'w_in', 'glu_w', 'conv_w', 'w_out', 'w_up', 'ffn_conv_w', 'w_down')
PACK_COLS = 1024


def _call(body, *, name, grid, in_specs, out_specs, out_shape, scratch=(), sem=None, vmem=None, ride=None):
    params = {}
    if vmem is not None:
        params['vmem_limit_bytes'] = vmem
    if ride is None:
        if sem is not None:
            params['dimension_semantics'] = sem
        return pl.pallas_call(body, name=name, grid=grid, in_specs=in_specs, out_specs=out_specs,
                              out_shape=out_shape, scratch_shapes=list(scratch),
                              compiler_params=pltpu.CompilerParams(**params))
    arrs, scatter = ride
    single = not isinstance(out_shape, (list, tuple))
    out_shape_l = [out_shape] if single else list(out_shape)
    out_specs_l = [out_specs] if single else list(out_specs)
    n, n_in, n_out, n_scr = len(arrs), len(in_specs), len(out_shape_l), len(scratch)
    any_spec = pl.BlockSpec(memory_space=pl.ANY)
    params['dimension_semantics'] = ('arbitrary',) * len(grid)

    def carried(*refs):
        ins, rin = refs[:n_in], refs[n_in:n_in + n]
        outs, rout = refs[n_in + n:n_in + n + n_out], refs[n_in + n + n_out:n_in + 2 * n + n_out]
        scr, sems = refs[n_in + 2 * n + n_out:n_in + 2 * n + n_out + n_scr], refs[n_in + 2 * n + n_out + n_scr:]
        first = pl.program_id(0) == 0
        last = pl.program_id(0) == grid[0] - 1
        for ax in range(1, len(grid)):
            first = jnp.logical_and(first, pl.program_id(ax) == 0)
            last = jnp.logical_and(last, pl.program_id(ax) == grid[ax] - 1)

        @pl.when(first)
        def _():
            _exchange_start(rin, rout, sems, scatter)

        body(*ins, *outs, *scr)

        @pl.when(last)
        def _():
            _exchange_wait(rin, rout, sems, scatter)

    call = pl.pallas_call(carried, name=name, grid=grid, in_specs=list(in_specs) + [any_spec] * n,
                          out_specs=out_specs_l + [any_spec] * n,
                          out_shape=out_shape_l + _exchange_shapes(arrs, scatter),
                          scratch_shapes=list(scratch) + _exchange_sems(n),
                          compiler_params=pltpu.CompilerParams(**params))

    def run(*args):
        res = call(*args, *arrs)
        own = res[0] if single else list(res[:n_out])
        return own, list(res[n_out:])

    return run


def _const(shape):
    nd = len(shape)
    return pl.BlockSpec(shape, lambda *_: (0,) * nd)


def _sds(shape, dtype=F32):
    return jax.ShapeDtypeStruct(shape, dtype)


def _dot(a, b):
    return jnp.dot(a, b, preferred_element_type=F32)


def _dot_nt(a, b):
    return lax.dot_general(a, b, (((1,), (1,)), ((), ())), preferred_element_type=F32)


def _dot_tn(a, b):
    return lax.dot_general(a, b, (((0,), (0,)), ((), ())), preferred_element_type=F32)


def _dot_split(x, mat, parts):
    acc = None
    rem = x
    for _ in range(parts):
        piece = rem.astype(BF16)
        rem = rem - piece.astype(F32)
        term = _dot(piece, mat)
        acc = term if acc is None else acc + term
    return acc


def _sigmoid(x):
    return 1.0 / (1.0 + jnp.exp(-x))


def _gelu(x):
    t = jnp.tanh(GELU_C * (x + GELU_A * x * x * x))
    return 0.5 * x * (1.0 + t), t


def _gelu_grad(x, t):
    return 0.5 * (1.0 + t) + 0.5 * x * (1.0 - t * t) * GELU_C * (1.0 + 3.0 * GELU_A * x * x)


def _rsqrt_mean(x):
    return lax.rsqrt(jnp.mean(x * x, axis=-1, keepdims=True) + EPS)


def _colsum(x):
    return jnp.sum(x, axis=0, keepdims=True)


def _shifts_down(x, halo):
    ext = jnp.concatenate([halo, x], axis=0)
    return pltpu.roll(ext, 1, 0)[halo.shape[0]:], pltpu.roll(ext, 2, 0)[halo.shape[0]:]


def _shifts_up(x, halo):
    n = x.shape[0]
    ext = jnp.concatenate([x, halo], axis=0)
    total = ext.shape[0]
    return pltpu.roll(ext, total - 1, 0)[:n], pltpu.roll(ext, total - 2, 0)[:n]


def _conv3(x, halo, w_ref):
    x1, x2 = _shifts_down(x, halo)
    return w_ref[0:1, :] * x2 + w_ref[1:2, :] * x1 + w_ref[2:3, :] * x, x1, x2


def _conv3_t(g, halo, w_ref):
    g1, g2 = _shifts_up(g, halo)
    return w_ref[2:3, :] * g + w_ref[1:2, :] * g1 + w_ref[0:1, :] * g2, g1, g2


def _silu_parts(x):
    s = _sigmoid(x)
    return x * s, s * (1.0 + x * (1.0 - s))


def _norm_bwd(dn, x, r, g):
    gd = g * dn
    return r * gd - x * (r * r * r) * jnp.mean(gd * x, axis=-1, keepdims=True)


def _head_norm_bwd(dn, y, rs, g, avg):
    gd = g * dn
    return rs * gd - y * (rs * rs * rs) * _dot_split(gd * y, avg, 2)


def _me():
    x, y, c = lax.axis_index('x'), lax.axis_index('y'), lax.axis_index('c')
    return x, y, c, 4 * x + 2 * y + c


def _peer(k):
    x, y, c, _ = _me()
    px = 1 - x if k & 4 else x
    py = 1 - y if k & 2 else y
    pc = 1 - c if k & 1 else c
    return (px, py, pc), 4 * px + 2 * py + pc


SIBLING = 1
OTHER_CHIPS = (2, 4, 6)


def _remote(src, dst, sems, a, k, dev):
    return pltpu.make_async_remote_copy(src_ref=src, dst_ref=dst, send_sem=sems[0].at[a, k - 1],
                                        recv_sem=sems[1].at[a, k - 1], device_id=dev,
                                        device_id_type=pl.DeviceIdType.MESH)


def _exchange_copies(ins, outs, sems, scatter):
    me = _me()[3]
    local, first, relay, arrivals = [], [], [], []
    for a in range(len(ins)):
        src = ins[a].at[me] if scatter else ins[a]
        local.append(pltpu.make_async_copy(src, outs[a].at[me], sems[2].at[a]))
        for k in range(1, N_DEV):
            dev, idx = _peer(k)
            landed = _remote(src, outs[a].at[idx], sems, a, k, dev)
            if scatter:
                first.append(_remote(ins[a].at[idx], outs[a].at[me], sems, a, k, dev))
                arrivals.append(landed)
            elif k == SIBLING:
                first.append(_remote(src, outs[a].at[me], sems, a, k, dev))
                arrivals.append(landed)
            elif k in OTHER_CHIPS:
                first.append(_remote(src, outs[a].at[me], sems, a, k, dev))
                sib, _ = _peer(SIBLING)
                relay.append((landed, _remote(outs[a].at[idx], outs[a].at[idx], sems, a, k | SIBLING, sib)))
            else:
                arrivals.append(landed)
    return local, first, relay, arrivals


def _exchange_start(ins, outs, sems, scatter):
    local, first, _, _ = _exchange_copies(ins, outs, sems, scatter)
    for cp in local + first:
        cp.start()


def _exchange_wait(ins, outs, sems, scatter):
    local, first, relay, arrivals = _exchange_copies(ins, outs, sems, scatter)
    for landed, forward in relay:
        landed.wait_recv()
        forward.start()
    for cp in arrivals:
        cp.wait_recv()
    for cp in first + [forward for _, forward in relay]:
        cp.wait_send()
    for cp in local:
        cp.wait()


def _exchange_shapes(arrs, scatter):
    return [_sds(a.shape if scatter else (N_DEV,) + a.shape, a.dtype) for a in arrs]


def _exchange_sems(n):
    return [pltpu.SemaphoreType.DMA((n, N_DEV - 1)), pltpu.SemaphoreType.DMA((n, N_DEV - 1)),
            pltpu.SemaphoreType.DMA((n,))]


def _exchange(arrs, *, name, scatter):
    n = len(arrs)

    def body(*refs):
        _exchange_start(refs[:n], refs[n:2 * n], refs[2 * n:], scatter)
        _exchange_wait(refs[:n], refs[n:2 * n], refs[2 * n:], scatter)

    any_spec = pl.BlockSpec(memory_space=pl.ANY)
    outs = pl.pallas_call(body, name=name, out_shape=_exchange_shapes(arrs, scatter), in_specs=[any_spec] * n,
                          out_specs=[any_spec] * n, scratch_shapes=_exchange_sems(n))(*arrs)
    return list(outs)


def _mod_cols(c_all, w_ada, b_cols):
    def body(c_ref, w_ref, b_ref, mod_ref, act_ref):
        c = c_ref[...]
        act = c * _sigmoid(c)
        act_ref[...] = act
        mod_ref[...] = _dot(act.astype(BF16), w_ref[...].astype(BF16)) + b_ref[...]

    return _call(body, name='mod_cols', grid=(1,),
                 in_specs=[_const(c_all.shape), _const(w_ada.shape), _const(b_cols.shape)],
                 out_specs=[_const((N_DEV, ADA_SHARD)), _const(c_all.shape)],
                 out_shape=[_sds((N_DEV, ADA_SHARD)), _sds(c_all.shape)], vmem=VMEM_BIG)(c_all, w_ada, b_cols)


def _grad_w_ada(act_t, dmod_cols):
    def body(a_ref, d_ref, o_ref):
        o_ref[...] = _dot(a_ref[...], d_ref[...])

    return _call(body, name='grad_w_ada', grid=(1,), in_specs=[_const(act_t.shape), _const(dmod_cols.shape)],
                 out_specs=_const((D_MODEL, ADA_SHARD)), out_shape=_sds((D_MODEL, ADA_SHARD)),
                 vmem=VMEM_BIG)(act_t, dmod_cols)


def _pre_mix(x, sc, sh, g, w_s, tm, ride):
    T = x.shape[0]

    def body(x_ref, sc_ref, sh_ref, g_ref, w_ref, proj_ref, h_ref):
        @pl.when(pl.program_id(1) == 0)
        def _():
            xv = x_ref[...]
            h_ref[...] = ((xv * _rsqrt_mean(xv) * g_ref[...]) * (1.0 + sc_ref[...]) + sh_ref[...]).astype(BF16)

        for s in range(2):
            proj_ref[:, s * IN_SHARD:(s + 1) * IN_SHARD] = _dot(h_ref[...], w_ref[s])

    row = pl.BlockSpec((tm, D_MODEL), lambda i, j: (i, 0))
    vec = _const((1, D_MODEL))
    return _call(body, name='pre_mix', grid=(T // tm, N_DEV // 2),
                 in_specs=[row, vec, vec, vec, pl.BlockSpec((2, D_MODEL, IN_SHARD), lambda i, j: (j, 0, 0))],
                 out_specs=[pl.BlockSpec((tm, 2 * IN_SHARD), lambda i, j: (i, j)), row],
                 out_shape=[_sds((T, D_IN_PROJ)), _sds((T, D_MODEL), BF16)],
                 sem=('parallel', 'arbitrary'), ride=ride)(x, sc, sh, g, w_s)


def _halo_before(tm, rows=HALO):
    return lambda i: jnp.maximum(i * (tm // rows) - 1, 0)


def _halo_after(tm, T, rows=HALO):
    return lambda i: jnp.minimum((i + 1) * (tm // rows), T // rows - 1)


def _mix_fwd(yssm, proj, d, glu_w, glu_b, g_ssm, cw, g_conv, avg16, avg64, tm):
    T = yssm.shape[0]
    hb = _halo_before(tm)

    def body(y_ref, p_ref, ph_ref, d_ref, gw_ref, gb_ref, gs_ref, cw_ref, gc_ref, a16_ref, a64_ref, o_ref):
        i = pl.program_id(0)
        u = p_ref[:, 0:D_SSM]
        y = y_ref[...] + d_ref[...] * u
        z, _ = _gelu(y)
        gate = _sigmoid(_dot(z.astype(BF16), gw_ref[...]) + gb_ref[...])
        ya = z * gate
        rs = lax.rsqrt(_dot_split(ya * ya, a16_ref[...], 2) + EPS)
        o_ref[:, 0:D_SSM] = (ya * rs * gs_ref[...]).astype(BF16)
        bg = p_ref[:, D_SSM:D_SSM + D_CONV]
        cv = p_ref[:, D_SSM + D_CONV:D_SSM + 2 * D_CONV] * p_ref[:, D_SSM + 2 * D_CONV:D_IN_PROJ]
        hv = ph_ref[:, D_SSM + D_CONV:D_SSM + 2 * D_CONV] * ph_ref[:, D_SSM + 2 * D_CONV:D_IN_PROJ]
        hv = jnp.where(i > 0, hv, 0.0)
        conv, _, _ = _conv3(cv, hv, cw_ref)
        yb = bg * conv
        rsb = lax.rsqrt(_dot_split(yb * yb, a64_ref[...], 2) + EPS)
        o_ref[:, D_SSM:D_MODEL] = (yb * rsb * gc_ref[...]).astype(BF16)

    vec = _const((1, D_SSM))
    sq = _const((D_SSM, D_SSM))
    return _call(body, name='mix_fwd', grid=(T // tm,),
                 in_specs=[pl.BlockSpec((tm, D_SSM), lambda i: (i, 0)), pl.BlockSpec((tm, D_IN_PROJ), lambda i: (i, 0)),
                           pl.BlockSpec((HALO, D_IN_PROJ), lambda i: (hb(i), 0)), vec, sq, vec, vec,
                           _const((3, D_CONV)), vec, sq, sq],
                 out_specs=pl.BlockSpec((tm, D_MODEL), lambda i: (i, 0)), out_shape=_sds((T, D_MODEL), BF16),
                 sem=('parallel',), vmem=VMEM_BIG)(yssm, proj, proj, d, glu_w, glu_b, g_ssm, cw, g_conv, avg16, avg64)


def _out_proj(ycat, w_out, x, gt, g_post, g_pre, sc, sh, tm):
    T = x.shape[0]

    def body(y_ref, w_ref, x_ref, gt_ref, gp_ref, g2_ref, sc_ref, sh_ref, o_ref, x1_ref, h_ref):
        o = _dot(y_ref[...], w_ref[...])
        o_ref[...] = o
        x1 = x_ref[...] + gt_ref[...] * (o * _rsqrt_mean(o) * gp_ref[...])
        x1_ref[...] = x1
        h_ref[...] = ((x1 * _rsqrt_mean(x1) * g2_ref[...]) * (1.0 + sc_ref[...]) + sh_ref[...]).astype(BF16)

    row = pl.BlockSpec((tm, D_MODEL), lambda i: (i, 0))
    vec = _const((1, D_MODEL))
    return _call(body, name='out_proj', grid=(T // tm,),
                 in_specs=[row, _const((D_MODEL, D_MODEL)), row, vec, vec, vec, vec, vec],
                 out_specs=[row, row, row],
                 out_shape=[_sds((T, D_MODEL)), _sds((T, D_MODEL)), _sds((T, D_MODEL), BF16)],
                 sem=('parallel',), vmem=VMEM_BIG)(ycat, w_out, x, gt, g_post, g_pre, sc, sh)


def _ffn_up(h2, w_s, cw8, tm):
    T = h2.shape[0]
    hb = _halo_before(tm, HALO16)

    def body(h_ref, hh_ref, w_ref, cw_ref, up_ref, hid_ref):
        up = _dot_nt(h_ref[...], w_ref[...])
        up_ref[...] = up.astype(BF16)
        before = jnp.where(pl.program_id(0) > 0, _dot_nt(hh_ref[...], w_ref[...]), 0.0)
        hid_ref[...] = _conv3(up, before, cw_ref)[0].astype(BF16)

    out = pl.BlockSpec((None, tm, FF_SHARD), lambda i, j: (j, i, 0))
    return _call(body, name='ffn_up', grid=(T // tm, N_DEV),
                 in_specs=[pl.BlockSpec((tm, D_MODEL), lambda i, j: (i, 0)),
                           pl.BlockSpec((HALO16, D_MODEL), lambda i, j: (hb(i), 0)),
                           pl.BlockSpec((None, FF_SHARD, D_MODEL), lambda i, j: (j, 0, 0)),
                           pl.BlockSpec((None, 3, FF_SHARD), lambda i, j: (j, 0, 0))],
                 out_specs=[out, out], out_shape=[_sds((N_DEV, T, FF_SHARD), BF16)] * 2,
                 sem=('parallel', 'parallel'))(h2, h2, w_s, cw8)


def _ffn_down(hid4, wd4, x1, tgt, gt, g_post, tm):
    T = x1.shape[0]
    nb = T // tm

    def body(a_ref, w_ref, x1_ref, t_ref, gt_ref, g_ref, ddn_ref, dx_ref, loss_ref, dgt_ref, dg_ref, dn_ref):
        i, j = pl.program_id(0), pl.program_id(1)
        part = None
        for s in range(2):
            act = (_silu_parts(a_ref[0, s].astype(F32))[0] * a_ref[1, s].astype(F32)).astype(BF16)
            term = _dot(act, w_ref[s])
            part = term if part is None else part + term

        @pl.when(jnp.logical_and(i == 0, j == 0))
        def _():
            dgt_ref[...] = jnp.zeros_like(dgt_ref)
            dg_ref[...] = jnp.zeros_like(dg_ref)

        @pl.when(j == 0)
        def _():
            dn_ref[...] = part

        @pl.when(j > 0)
        def _():
            dn_ref[...] += part

        @pl.when(j == 1)
        def _():
            dn, gv, gate = dn_ref[...], g_ref[...], gt_ref[...]
            r = _rsqrt_mean(dn)
            normed = dn * r * gv
            err = x1_ref[...] + gate * normed - t_ref[...]
            dx = err * (1.0 / D_MODEL)
            dx_ref[...] = dx
            tot = jnp.sum(jnp.sum(err * err, axis=1, keepdims=True), axis=0, keepdims=True) * (0.5 / D_MODEL)
            loss_ref[...] = jnp.broadcast_to(tot, (8, 128))
            dgt_ref[...] += _colsum(dx * normed)
            dnn = dx * gate
            dg_ref[...] += _colsum(dnn * dn * r)
            ddn_ref[...] = _norm_bwd(dnn, dn, r, gv).astype(BF16)

    row = pl.BlockSpec((tm, D_MODEL), lambda i, j: (i, 0))
    vec = _const((1, D_MODEL))
    return _call(body, name='ffn_down', grid=(nb, 2),
                 in_specs=[pl.BlockSpec((2, 2, tm, FF_SHARD), lambda i, j: (0, j, i, 0)),
                           pl.BlockSpec((2, FF_SHARD, D_MODEL), lambda i, j: (j, 0, 0)), row, row, vec, vec],
                 out_specs=[row, row, pl.BlockSpec((None, 8, 128), lambda i, j: (i, 0, 0)), vec, vec],
                 out_shape=[_sds((T, D_MODEL), BF16), _sds((T, D_MODEL)), _sds((nb, 8, 128)), _sds((1, D_MODEL)),
                            _sds((1, D_MODEL))],
                 scratch=[pltpu.VMEM((tm, D_MODEL), F32)], sem=('arbitrary', 'arbitrary'),
                 vmem=VMEM_BIG)(hid4, wd4, x1, tgt, gt, g_post)


def _ssm_prep(lre, lim, lst, b_re, b_im):
    def body(lre_ref, lim_ref, lst_ref, br_ref, bi_ref, ar_ref, ai_ref, bbr_ref, bbi_ref):
        ar, ai, qr, qi = _zoh(lre_ref[...], lim_ref[...], lst_ref[...])[:4]
        ar_ref[...] = ar
        ai_ref[...] = ai
        bbr_ref[...] = qr * br_ref[...] - qi * bi_ref[...]
        bbi_ref[...] = qr * bi_ref[...] + qi * br_ref[...]

    shp = lre.shape
    return _call(body, name='ssm_prep', grid=(1,), in_specs=[_const(shp)] * 5, out_specs=[_const(shp)] * 4,
                 out_shape=[_sds(shp)] * 4)(lre, lim, lst, b_re, b_im)


def _zoh(lre, lim, lst):
    lr = jnp.minimum(lre, LAMBDA_RE_MAX)
    st = jnp.exp(lst)
    mag = jnp.exp(lr * st)
    ar = mag * jnp.cos(lim * st)
    ai = mag * jnp.sin(lim * st)
    den = lr * lr + lim * lim
    qr = ((ar - 1.0) * lr + ai * lim) / den
    qi = (ai * lr - (ar - 1.0) * lim) / den
    return ar, ai, qr, qi, lr, st, den


def _ssm_prep_bwd(lre, lim, lst, b_re, b_im, dbbr, dbbi, dar, dai, seg):
    def body(lre_ref, lim_ref, lst_ref, br_ref, bi_ref, dbbr_ref, dbbi_ref, dar_ref, dai_ref, seg_ref,
             dbr_ref, dbi_ref, dlre_ref, dlim_ref, dlst_ref):
        lre_v = lre_ref[...]
        li = lim_ref[...]
        ar, ai, qr, qi, lr, st, den = _zoh(lre_v, li, lst_ref[...])
        br, bi, gbr, gbi = br_ref[...], bi_ref[...], dbbr_ref[...], dbbi_ref[...]
        dbr_ref[...] = qr * gbr + qi * gbi
        dbi_ref[...] = qr * gbi - qi * gbr
        gqr = _dot_split(br * gbr + bi * gbi, seg_ref[...], 3)
        gqi = _dot_split(br * gbi - bi * gbr, seg_ref[...], 3)
        ir, ii = lr / den, -li / den
        gar = dar_ref[...] + ir * gqr + ii * gqi
        gai = dai_ref[...] + ir * gqi - ii * gqr
        tr, ti = qr * ir - qi * ii, qr * ii + qi * ir
        glr = -(tr * gqr + ti * gqi)
        gli = -(tr * gqi - ti * gqr)
        gzr = ar * gar + ai * gai
        gzi = ar * gai - ai * gar
        glr = glr + st * gzr
        gli = gli + st * gzi
        gst = (lr * gzr + li * gzi) * st
        dlre_ref[...] = jnp.where(lre_v < LAMBDA_RE_MAX, glr, 0.0)
        dlim_ref[...] = gli
        dlst_ref[...] = jnp.sum(gst, axis=1, keepdims=True) * (1.0 / SSM_GROUP)

    shp = lre.shape
    return _call(body, name='ssm_prep_bwd', grid=(1,), in_specs=[_const(shp)] * 9 + [_const(seg.shape)],
                 out_specs=[_const(shp)] * 4 + [_const((N_GROUPS, 1))],
                 out_shape=[_sds(shp)] * 4 + [_sds((N_GROUPS, 1))], vmem=VMEM_BIG)(
                     lre, lim, lst, b_re, b_im, dbbr, dbbi, dar, dai, seg)


def _scan_specs(T):
    return dict(
        chan=pl.BlockSpec((T, CHAN_BLOCK), lambda cb: (0, cb)),
        state=pl.BlockSpec((T, STATE_BLOCK), lambda cb: (0, cb)),
        b=pl.BlockSpec((CHAN_BLOCK, STATE_BLOCK), lambda cb: (cb, cb)),
        c=pl.BlockSpec((STATE_BLOCK, CHAN_BLOCK), lambda cb: (cb, cb)),
        lam=pl.BlockSpec((1, STATE_BLOCK), lambda cb: (0, cb)),
    )


def _complex_power(re, im, n):
    out = None
    while True:
        if n & 1:
            out = (re, im) if out is None else (out[0] * re - out[1] * im, out[0] * im + out[1] * re)
        n >>= 1
        if n == 0:
            return out
        re, im = re * re - im * im, 2.0 * re * im


def _rows8(i):
    if isinstance(i, int):
        return pl.ds(i * SUBLANES, SUBLANES)
    return pl.ds(pl.multiple_of(i * SUBLANES, SUBLANES), SUBLANES)


def _scan_loop(n_steps, body, init):
    trips = n_steps // SCAN_UNROLL

    def trip(t, carry):
        for u in range(SCAN_UNROLL):
            carry = body(t * SCAN_UNROLL + u, carry)
        return carry

    carry = lax.fori_loop(0, trips, trip, init)
    for step in range(trips * SCAN_UNROLL, n_steps):
        carry = body(step, carry)
    return carry


def _ssm_fwd(u_perm, b_re, b_im, c_re, c_im, lam_r, lam_i, ride):
    T = u_perm.shape[0]
    ls = T // SUBLANES
    rc = min(512, T)
    sp = _scan_specs(T)

    def body(u_ref, bre_ref, bim_ref, cre_ref, cim_ref, lr_ref, li_ref, so_re_ref, so_im_ref, y_ref, sre_ref, sim_ref):
        for c in range(T // rc):
            rows = pl.ds(c * rc, rc)
            ub = u_ref[rows, :].astype(BF16)
            sre_ref[rows, :] = _dot(ub, bre_ref[...])
            sim_ref[rows, :] = _dot(ub, bim_ref[...])
        shp = (SUBLANES, STATE_BLOCK)
        lr = jnp.broadcast_to(lr_ref[...], shp)
        li = jnp.broadcast_to(li_ref[...], shp)
        zero = jnp.zeros(shp, F32)

        def step(i, carry):
            sr, si = carry
            rows = _rows8(i)
            nr = lr * sr - li * si + sre_ref[rows, :]
            ni = lr * si + li * sr + sim_ref[rows, :]
            sre_ref[rows, :] = nr
            sim_ref[rows, :] = ni
            return nr, ni

        fr, fi = _scan_loop(ls, step, (zero, zero))
        pr, pi_ = _complex_power(lr, li, ls)
        row = lax.broadcasted_iota(jnp.int32, shp, 0)
        ir, ii = zero, zero
        for _ in range(SUBLANES - 1):
            er = fr + pr * ir - pi_ * ii
            ei = fi + pr * ii + pi_ * ir
            ir = jnp.where(row == 0, 0.0, pltpu.roll(er, 1, 0))
            ii = jnp.where(row == 0, 0.0, pltpu.roll(ei, 1, 0))

        def fix(i, carry):
            cr, ci = carry
            rows = _rows8(i)
            nr = lr * cr - li * ci
            ni = lr * ci + li * cr
            sre_ref[rows, :] += nr
            sim_ref[rows, :] += ni
            return nr, ni

        _scan_loop(ls, fix, (ir, ii))
        for c in range(T // rc):
            rows = pl.ds(c * rc, rc)
            s_r, s_i = sre_ref[rows, :].astype(BF16), sim_ref[rows, :].astype(BF16)
            so_re_ref[rows, :] = s_r
            so_im_ref[rows, :] = s_i
            y_ref[rows, :] = _dot(s_r, cre_ref[...]) - _dot(s_i, cim_ref[...])

    return _call(body, name='ssm_fwd', grid=(N_STATE // STATE_BLOCK,),
                 in_specs=[sp['chan'], sp['b'], sp['b'], sp['c'], sp['c'], sp['lam'], sp['lam']],
                 out_specs=[sp['state'], sp['state'], sp['chan']],
                 out_shape=[_sds((T, N_STATE), BF16), _sds((T, N_STATE), BF16), _sds((T, D_SSM))],
                 scratch=[pltpu.VMEM((T, STATE_BLOCK), F32), pltpu.VMEM((T, STATE_BLOCK), F32)],
                 sem=('arbitrary',), vmem=VMEM_MOST, ride=ride)(u_perm, b_re, b_im, c_re, c_im, lam_r, lam_i)


def _ssm_bwd(dy_perm, u_perm, s_re, s_im, b_re, b_im, c_re, c_im, lam_r, lam_i, ride):
    T = u_perm.shape[0]
    ls = T // SUBLANES
    rc = min(512, T)
    sp = _scan_specs(T)
    ncb = N_STATE // STATE_BLOCK

    def body(dy_ref, u_ref, sre_ref, sim_ref, bre_ref, bim_ref, cre_ref, cim_ref, lr_ref, li_ref,
             du_ref, dbr_ref, dbi_ref, dcr_ref, dci_ref, dar_ref, dai_ref, gre_ref, gim_ref):
        shp = (SUBLANES, STATE_BLOCK)
        zero = jnp.zeros(shp, F32)
        tail = pl.ds(T, SUBLANES)
        gre_ref[tail, :] = zero
        gim_ref[tail, :] = zero
        for c in range(T // rc):
            rows = pl.ds(c * rc, rc)
            dyb = dy_ref[rows, :].astype(BF16)
            gre_ref[rows, :] = _dot_nt(dyb, cre_ref[...])
            gim_ref[rows, :] = -_dot_nt(dyb, cim_ref[...])
        lr = jnp.broadcast_to(lr_ref[...], shp)
        li = jnp.broadcast_to(li_ref[...], shp)

        def step(k, carry):
            gr, gi = carry
            rows = _rows8(ls - 1 - k)
            nr = lr * gr + li * gi + gre_ref[rows, :]
            ni = lr * gi - li * gr + gim_ref[rows, :]
            gre_ref[rows, :] = nr
            gim_ref[rows, :] = ni
            return nr, ni

        fr, fi = _scan_loop(ls, step, (zero, zero))
        pr, pi_ = _complex_power(lr, -li, ls)
        row = lax.broadcasted_iota(jnp.int32, shp, 0)
        cr, ci = zero, zero
        for _ in range(SUBLANES - 1):
            er = fr + pr * cr - pi_ * ci
            ei = fi + pr * ci + pi_ * cr
            cr = jnp.where(row == SUBLANES - 1, 0.0, pltpu.roll(er, SUBLANES - 1, 0))
            ci = jnp.where(row == SUBLANES - 1, 0.0, pltpu.roll(ei, SUBLANES - 1, 0))

        def fix(k, carry):
            dr, di = carry
            rows = _rows8(ls - 1 - k)
            dr, di = lr * dr + li * di, lr * di - li * dr
            gre_ref[rows, :] += dr
            gim_ref[rows, :] += di
            return dr, di

        _scan_loop(ls, fix, (cr, ci))

        acc_r = jnp.zeros((1, STATE_BLOCK), F32)
        acc_i = jnp.zeros((1, STATE_BLOCK), F32)
        for c in range(T // rc):
            rows, nxt = pl.ds(c * rc, rc), pl.ds(c * rc + SUBLANES, rc)
            s_r, s_i = sre_ref[rows, :].astype(F32), sim_ref[rows, :].astype(F32)
            g_r, g_i = gre_ref[nxt, :], gim_ref[nxt, :]
            acc_r = acc_r + _colsum(g_r * s_r + g_i * s_i)
            acc_i = acc_i + _colsum(g_i * s_r - g_r * s_i)
        last = pl.ds(T - 2 * SUBLANES, 2 * SUBLANES)
        first = pl.ds(0, SUBLANES)
        spr = jnp.where(row == 0, 0.0, pltpu.roll(sre_ref[last, :].astype(F32)[SUBLANES:], 1, 0))
        spi = jnp.where(row == 0, 0.0, pltpu.roll(sim_ref[last, :].astype(F32)[SUBLANES:], 1, 0))
        gr, gi = gre_ref[first, :], gim_ref[first, :]
        dar_ref[...] = acc_r + _colsum(gr * spr + gi * spi)
        dai_ref[...] = acc_i + _colsum(gi * spr - gr * spi)

        for c in range(T // rc):
            rows = pl.ds(c * rc, rc)
            g_r, g_i = gre_ref[rows, :].astype(BF16), gim_ref[rows, :].astype(BF16)
            s_r, s_i = sre_ref[rows, :], sim_ref[rows, :]
            ub, dyb = u_ref[rows, :].astype(BF16), dy_ref[rows, :].astype(BF16)
            du_ref[rows, :] = _dot_nt(g_r, bre_ref[...]) + _dot_nt(g_i, bim_ref[...])
            parts = (_dot_tn(ub, g_r), _dot_tn(ub, g_i), _dot_tn(s_r, dyb), -_dot_tn(s_i, dyb))
            outs = (dbr_ref, dbi_ref, dcr_ref, dci_ref)
            for o_ref, part in zip(outs, parts):
                if c == 0:
                    o_ref[...] = part
                else:
                    o_ref[...] += part

    blk = lambda r, c: pl.BlockSpec((None, r, c), lambda cb: (cb, 0, 0))
    return _call(body, name='ssm_bwd', grid=(ncb,),
                 in_specs=[sp['chan'], sp['chan'], sp['state'], sp['state'], sp['b'], sp['b'], sp['c'], sp['c'],
                           sp['lam'], sp['lam']],
                 out_specs=[sp['chan'], blk(CHAN_BLOCK, STATE_BLOCK), blk(CHAN_BLOCK, STATE_BLOCK),
                            blk(STATE_BLOCK, CHAN_BLOCK), blk(STATE_BLOCK, CHAN_BLOCK), blk(1, STATE_BLOCK),
                            blk(1, STATE_BLOCK)],
                 out_shape=[_sds((T, D_SSM)), _sds((ncb, CHAN_BLOCK, STATE_BLOCK)), _sds((ncb, CHAN_BLOCK, STATE_BLOCK)),
                            _sds((ncb, STATE_BLOCK, CHAN_BLOCK)), _sds((ncb, STATE_BLOCK, CHAN_BLOCK)),
                            _sds((ncb, 1, STATE_BLOCK)), _sds((ncb, 1, STATE_BLOCK))],
                 scratch=[pltpu.VMEM((T + SUBLANES, STATE_BLOCK), F32), pltpu.VMEM((T + SUBLANES, STATE_BLOCK), F32)],
                 sem=('arbitrary',), vmem=VMEM_MOST, ride=ride)(dy_perm, u_perm, s_re, s_im, b_re, b_im, c_re, c_im,
                                                                lam_r, lam_i)


def _ffn_dact(ddn, wd4, hid4, tm):
    T = ddn.shape[0]

    def body(d_ref, w_ref, hid_ref, o_ref, act_ref):
        dact = _dot_nt(d_ref[...], w_ref[...])
        silu, dsilu = _silu_parts(hid_ref[0].astype(F32))
        hid_v = hid_ref[1].astype(F32)
        o_ref[0] = (dact * hid_v * dsilu).astype(BF16)
        o_ref[1] = (dact * silu).astype(BF16)
        act_ref[...] = (silu * hid_v).astype(BF16)

    blk = pl.BlockSpec((2, None, tm, FF_SHARD), lambda i, j: (0, j, i, 0))
    return _call(body, name='ffn_dact', grid=(T // tm, 4),
                 in_specs=[pl.BlockSpec((tm, D_MODEL), lambda i, j: (i, 0)),
                           pl.BlockSpec((None, FF_SHARD, D_MODEL), lambda i, j: (j, 0, 0)), blk],
                 out_specs=[blk, pl.BlockSpec((None, tm, FF_SHARD), lambda i, j: (j, i, 0))],
                 out_shape=[_sds((2, 4, T, FF_SHARD), BF16), _sds((4, T, FF_SHARD), BF16)],
                 sem=('parallel', 'parallel'), vmem=VMEM_BIG)(ddn, wd4, hid4)


def _ffn_dup(dhid8, up8, cw8, tm, ride):
    T = up8.shape[1]
    nb = T // tm
    ha = _halo_after(tm, T, HALO16)

    def body(dh_ref, dha_ref, up_ref, cw_ref, dup_ref, dcw_ref):
        i = pl.program_id(1)

        @pl.when(i == 0)
        def _():
            dcw_ref[...] = jnp.zeros_like(dcw_ref)

        dh = dh_ref[...].astype(F32)
        dup, dh1, dh2 = _conv3_t(dh, jnp.where(i < nb - 1, dha_ref[...].astype(F32), 0.0), cw_ref)
        dup_ref[...] = dup.astype(BF16)
        up = up_ref[...].astype(F32)
        dcw_ref[0:1, :] += _colsum(dh2 * up)
        dcw_ref[1:2, :] += _colsum(dh1 * up)
        dcw_ref[2:3, :] += _colsum(dh * up)

    main = pl.BlockSpec((None, tm, FF_SHARD), lambda j, i: (j, i, 0))
    return _call(body, name='ffn_dup', grid=(N_DEV, nb),
                 in_specs=[main, pl.BlockSpec((None, HALO16, FF_SHARD), lambda j, i: (j, ha(i), 0)), main,
                           pl.BlockSpec((None, 3, FF_SHARD), lambda j, i: (j, 0, 0))],
                 out_specs=[main, pl.BlockSpec((None, 8, FF_SHARD), lambda j, i: (j, 0, 0))],
                 out_shape=[_sds((N_DEV, T, FF_SHARD), BF16), _sds((N_DEV, 8, FF_SHARD))],
                 sem=('parallel', 'arbitrary'), vmem=VMEM_BIG, ride=ride)(dhid8, dhid8, up8, cw8)


def _grad_tn(a, b, a_spec, b_spec, groups, m, n, tk, name, ride=None, parts=1):
    T = a.shape[-2]
    nk = T // tk
    mp = m // parts

    def body(a_ref, b_ref, *refs):
        o_refs, acc_ref = refs[:parts], refs[parts]
        k = pl.program_id(1)
        part = _dot_tn(a_ref[...], b_ref[...])

        @pl.when(k == 0)
        def _():
            acc_ref[...] = part

        @pl.when(k > 0)
        def _():
            acc_ref[...] += part

        @pl.when(k == nk - 1)
        def _():
            for p, o_ref in enumerate(o_refs):
                o_ref[...] = acc_ref[p * mp:(p + 1) * mp, :].astype(BF16)

    out_spec = pl.BlockSpec((None, mp, n), lambda g, k: (g, 0, 0))
    res = _call(body, name=name, grid=(groups, nk), in_specs=[a_spec, b_spec], out_specs=[out_spec] * parts,
                out_shape=[_sds((groups, mp, n), BF16)] * parts, scratch=[pltpu.VMEM((m, n), F32)],
                sem=('parallel', 'arbitrary'), vmem=VMEM_BIG, ride=ride)(a, b)
    if parts > 1:
        return res
    return res[0] if ride is None else (res[0][0], res[1])


def _grad_w_in(h1, dproj, tk, ride):
    T = h1.shape[0]
    nk = T // tk
    half = D_IN_PROJ // 2

    def body(a_ref, b_ref, o_ref, acc_ref):
        k = pl.program_id(0)
        for h in range(2):
            cols = slice(h * half, (h + 1) * half)
            part = _dot_tn(a_ref[...], b_ref[:, cols])

            @pl.when(k == 0)
            def _():
                acc_ref[:, cols] = part

            @pl.when(k > 0)
            def _():
                acc_ref[:, cols] += part

        @pl.when(k == nk - 1)
        def _():
            for g in range(N_DEV):
                o_ref[g] = acc_ref[:, g * IN_SHARD:(g + 1) * IN_SHARD].astype(BF16)

    return _call(body, name='grad_w_in', grid=(nk,),
                 in_specs=[pl.BlockSpec((tk, D_MODEL), lambda k: (k, 0)), pl.BlockSpec((tk, D_IN_PROJ), lambda k: (k, 0))],
                 out_specs=_const((N_DEV, D_MODEL, IN_SHARD)), out_shape=_sds((N_DEV, D_MODEL, IN_SHARD), BF16),
                 scratch=[pltpu.VMEM((D_MODEL, D_IN_PROJ), F32)], sem=('arbitrary',), vmem=VMEM_BIG, ride=ride)(h1, dproj)


def _pre_norm_bwd(dz, dz_spec, w_s, xin, dres, sc, g, tm, name, ride, below=None, group=1, w_t=False):
    T = xin.shape[0]
    n = w_s.shape[1] if w_t else w_s.shape[2]
    mul = _dot if w_t else _dot_nt
    steps = N_DEV // group

    def body(dz_ref, w_ref, x_ref, dr_ref, sc_ref, g_ref, *refs):
        if below is None:
            dx_ref, dsh_ref, dsc_ref, dg_ref = refs
            sums = (dsh_ref, dsc_ref, dg_ref)
        else:
            v_ref, gate_ref, g2_ref, dx_ref, dsh_ref, dsc_ref, dg_ref, dv_ref, dgate_ref, dg2_ref = refs
            sums = (dsh_ref, dsc_ref, dg_ref, dgate_ref, dg2_ref)
        i, j = pl.program_id(0), pl.program_id(1)
        piece = (lambda s: dz_ref[s]) if dz.ndim == 3 else (lambda s: dz_ref[:, s * n:(s + 1) * n])
        part = mul(piece(0), w_ref[0])
        for s in range(1, group):
            part = part + mul(piece(s), w_ref[s])

        @pl.when(jnp.logical_and(i == 0, j == 0))
        def _():
            for s_ref in sums:
                s_ref[...] = jnp.zeros_like(s_ref)

        @pl.when(j == 0)
        def _():
            dx_ref[...] = part

        @pl.when(j > 0)
        def _():
            dx_ref[...] += part

        @pl.when(j == steps - 1)
        def _():
            dh, xv, gv = dx_ref[...], x_ref[...], g_ref[...]
            r = _rsqrt_mean(xv)
            dsh_ref[...] += _colsum(dh)
            dsc_ref[...] += _colsum(dh * (xv * r * gv))
            dxn = dh * (1.0 + sc_ref[...])
            dg_ref[...] += _colsum(dxn * xv * r)
            dx = dr_ref[...] + _norm_bwd(dxn, xv, r, gv)
            dx_ref[...] = dx
            if below is not None:
                v, g2 = v_ref[...], g2_ref[...]
                rv = _rsqrt_mean(v)
                dgate_ref[...] += _colsum(dx * (v * rv * g2))
                dn = dx * gate_ref[...]
                dg2_ref[...] += _colsum(dn * v * rv)
                dv_ref[...] = _norm_bwd(dn, v, rv, g2).astype(BF16)

    row = pl.BlockSpec((tm, D_MODEL), lambda i, j: (i, 0))
    vec = _const((1, D_MODEL))
    in_specs = [dz_spec, pl.BlockSpec((group,) + w_s.shape[1:], lambda i, j: (j, 0, 0)), row, row, vec, vec]
    out_specs = [row, vec, vec, vec]
    out_shape = [_sds((T, D_MODEL)), _sds((1, D_MODEL)), _sds((1, D_MODEL)), _sds((1, D_MODEL))]
    args = [dz, w_s, xin, dres, sc, g]
    if below is not None:
        in_specs += [row, vec, vec]
        out_specs += [row, vec, vec]
        out_shape += [_sds((T, D_MODEL), BF16), _sds((1, D_MODEL)), _sds((1, D_MODEL))]
        args += list(below)
    return _call(body, name=name, grid=(T // tm, steps), in_specs=in_specs, out_specs=out_specs,
                 out_shape=out_shape, sem=('arbitrary', 'arbitrary'), vmem=VMEM_MOST, ride=ride)(*args)


def _mix_bwd(d_o, w_out, yssm, proj, d, glu_w, glu_b, g_ssm, cw, g_conv, avg16, avg64, tm, ride):
    T = yssm.shape[0]
    hb = _halo_before(tm)

    def body(do_ref, wo_ref, y_ref, p_ref, ph_ref, d_ref, gw_ref, gb_ref, gs_ref, cw_ref, gc_ref, a16_ref, a64_ref,
             dy_ref, dconv_ref, dbg_ref, z_ref, dlin_ref, acc_ref):
        i = pl.program_id(0)
        dyc = _dot_nt(do_ref[...], wo_ref[...])

        @pl.when(i == 0)
        def _():
            acc_ref[...] = jnp.zeros_like(acc_ref)

        u = p_ref[:, 0:D_SSM]
        y = y_ref[...] + d_ref[...] * u
        z, t = _gelu(y)
        gate = _sigmoid(_dot(z.astype(BF16), gw_ref[...]) + gb_ref[...])
        ya = z * gate
        rs = lax.rsqrt(_dot_split(ya * ya, a16_ref[...], 2) + EPS)
        dna = dyc[:, 0:D_SSM]
        acc_ref[1:2, :] += _colsum(dna * ya * rs)
        dya = _head_norm_bwd(dna, ya, rs, gs_ref[...], a16_ref[...])
        dlin = dya * z * gate * (1.0 - gate)
        acc_ref[0:1, :] += _colsum(dlin)
        dlin_b = dlin.astype(BF16)
        dz = dya * gate + _dot_nt(dlin_b, gw_ref[...])
        dy = dz * _gelu_grad(y, t)
        acc_ref[3:4, :] += _colsum(dy * u)
        dy_ref[...] = dy
        z_ref[...] = z.astype(BF16)
        dlin_ref[...] = dlin_b

        bg = p_ref[:, D_SSM:D_SSM + D_CONV]
        cv = p_ref[:, D_SSM + D_CONV:D_SSM + 2 * D_CONV] * p_ref[:, D_SSM + 2 * D_CONV:D_IN_PROJ]
        hv = ph_ref[:, D_SSM + D_CONV:D_SSM + 2 * D_CONV] * ph_ref[:, D_SSM + 2 * D_CONV:D_IN_PROJ]
        hv = jnp.where(i > 0, hv, 0.0)
        conv, cv1, cv2 = _conv3(cv, hv, cw_ref)
        yb = bg * conv
        rsb = lax.rsqrt(_dot_split(yb * yb, a64_ref[...], 2) + EPS)
        dnb = dyc[:, D_SSM:D_MODEL]
        acc_ref[2:3, :] += _colsum(dnb * yb * rsb)
        dyb = _head_norm_bwd(dnb, yb, rsb, gc_ref[...], a64_ref[...])
        dbg_ref[...] = dyb * conv
        dconv = dyb * bg
        dconv_ref[...] = dconv
        acc_ref[4:5, :] += _colsum(dconv * cv2)
        acc_ref[5:6, :] += _colsum(dconv * cv1)
        acc_ref[6:7, :] += _colsum(dconv * cv)

    vec = _const((1, D_SSM))
    sq = _const((D_SSM, D_SSM))
    half = pl.BlockSpec((tm, D_SSM), lambda i: (i, 0))
    return _call(body, name='mix_bwd', grid=(T // tm,),
                 in_specs=[pl.BlockSpec((tm, D_MODEL), lambda i: (i, 0)), _const((D_MODEL, D_MODEL)), half,
                           pl.BlockSpec((tm, D_IN_PROJ), lambda i: (i, 0)),
                           pl.BlockSpec((HALO, D_IN_PROJ), lambda i: (hb(i), 0)), vec, sq, vec, vec,
                           _const((3, D_CONV)), vec, sq, sq],
                 out_specs=[half, half, half, half, half, _const((8, D_SSM))],
                 out_shape=[_sds((T, D_SSM)), _sds((T, D_SSM)), _sds((T, D_SSM)), _sds((T, D_SSM), BF16),
                            _sds((T, D_SSM), BF16), _sds((8, D_SSM))],
                 sem=('arbitrary',), vmem=VMEM_BIG, ride=ride)(d_o, w_out, yssm, proj, proj, d, glu_w, glu_b, g_ssm, cw,
                                                              g_conv, avg16, avg64)


def _mix_bwd_proj(dconv, proj, du_ssm, dy, d, dbg, cw, tm):
    T = dy.shape[0]
    nb = T // tm
    ha = _halo_after(tm, T)

    def body(dc_ref, dch_ref, cg_ref, v_ref, du_ref, dy_ref, d_ref, dbg_ref, cw_ref, o_ref):
        i = pl.program_id(0)
        dcv = _conv3_t(dc_ref[...], jnp.where(i < nb - 1, dch_ref[...], 0.0), cw_ref)[0]
        o_ref[:, 0:D_SSM] = (du_ref[...] + dy_ref[...] * d_ref[...]).astype(BF16)
        o_ref[:, D_SSM:D_SSM + D_CONV] = dbg_ref[...].astype(BF16)
        o_ref[:, D_SSM + D_CONV:D_SSM + 2 * D_CONV] = (dcv * v_ref[...]).astype(BF16)
        o_ref[:, D_SSM + 2 * D_CONV:D_IN_PROJ] = (dcv * cg_ref[...]).astype(BF16)

    half = pl.BlockSpec((tm, D_SSM), lambda i: (i, 0))
    return _call(body, name='mix_bwd_proj', grid=(nb,),
                 in_specs=[half, pl.BlockSpec((HALO, D_CONV), lambda i: (ha(i), 0)),
                           pl.BlockSpec((tm, D_CONV), lambda i: (i, 2)), pl.BlockSpec((tm, D_CONV), lambda i: (i, 3)),
                           half, half, _const((1, D_SSM)), half, _const((3, D_CONV))],
                 out_specs=pl.BlockSpec((tm, D_IN_PROJ), lambda i: (i, 0)), out_shape=_sds((T, D_IN_PROJ), BF16),
                 sem=('parallel',), vmem=VMEM_BIG)(dconv, dconv, proj, proj, du_ssm, dy, d, dbg, cw)


ADAMW_SLOT_BYTES = 8 << 20
ADAMW_ROW_BYTES = 3 << 19


def _row_tile(rows, cols, slots):
    for cand in range(rows, 15, -1):
        if (rows % cand == 0 and cand % 16 == 0 and slots * cand * cols * 4 <= ADAMW_SLOT_BYTES
                and cand * cols * 4 <= ADAMW_ROW_BYTES):
            return cand
    return rows


def _adamw_math(g, w, m, v):
    m2 = ADAM_B1 * m + (1.0 - ADAM_B1) * g
    v2 = ADAM_B2 * v + (1.0 - ADAM_B2) * (g * g)
    m_hat = m2 / (1.0 - ADAM_B1 ** ADAM_STEP)
    v_hat = v2 / (1.0 - ADAM_B2 ** ADAM_STEP)
    return -ADAM_LR * (m_hat / (jnp.sqrt(v_hat) + ADAM_EPS) + ADAM_WD * w), m2, v2


def _adamw(pieces, w, m, v, name):
    slots, _, cols = pieces[0].shape
    rows = sum(p.shape[1] for p in pieces)
    tr = _row_tile(pieces[0].shape[1], cols, slots)
    starts, pos = [], 0
    for p in pieces:
        assert p.shape[1] % tr == 0
        starts.append(pos)
        pos += p.shape[1] // tr

    def body(*refs):
        g_refs = refs[:len(pieces)]
        w_ref, m_ref, v_ref, go_ref, d_ref, mo_ref, vo_ref = refs[len(pieces):]
        i = pl.program_id(0)
        g = None
        for g_ref, start in zip(g_refs, starts):
            part = g_ref[0].astype(F32)
            for s in range(1, slots):
                part = part + g_ref[s].astype(F32)
            g = part if g is None else jnp.where(i >= start, part, g)
        go_ref[...] = g
        d_ref[...], mo_ref[...], vo_ref[...] = _adamw_math(g, w_ref[...], m_ref[...], v_ref[...])

    def piece_spec(start, count):
        return pl.BlockSpec((slots, tr, cols), lambda i: (0, jnp.clip(i - start, 0, count - 1), 0))

    blk = pl.BlockSpec((tr, cols), lambda i: (i, 0))
    return _call(body, name=name, grid=(rows // tr,),
                 in_specs=[piece_spec(s, p.shape[1] // tr) for s, p in zip(starts, pieces)] + [blk, blk, blk],
                 out_specs=[blk] * 4, out_shape=[_sds((rows, cols))] * 4, sem=('parallel',),
                 vmem=VMEM_BIG)(*pieces, w, m, v)


def _to_scan_rows(a):
    T, n = a.shape
    return a.reshape(SUBLANES, T // SUBLANES, n).transpose(1, 0, 2).reshape(T, n)


def _from_scan_rows(a):
    T, n = a.shape
    return a.reshape(T // SUBLANES, SUBLANES, n).transpose(1, 0, 2).reshape(T, n)


def _expand(a):
    return jnp.repeat(a, SSM_GROUP, axis=1)


def _block_diag(rows, row_group, col_group):
    r, n = rows.shape
    tiled = jnp.tile(rows, (1, N_GROUPS))
    keep = (jnp.arange(r)[:, None] // row_group) == (jnp.arange(n * N_GROUPS)[None, :] // col_group)
    return jnp.where(keep, tiled, 0.0)


def _block_diag_b(bb):
    return _block_diag(bb.transpose(0, 2, 1).reshape(D_SSM, SSM_STATE), SSM_GROUP, SSM_STATE)


def _block_diag_c(cc):
    return _block_diag(cc.transpose(0, 2, 1).reshape(N_STATE, SSM_GROUP), SSM_STATE, SSM_GROUP)


def _diag_blocks(x, chan_major):
    per = CHAN_BLOCK // SSM_GROUP
    eye = jnp.eye(per, dtype=x.dtype)
    if chan_major:
        x = x.reshape(-1, per, SSM_GROUP, per, SSM_STATE) * eye[None, :, None, :, None]
        return x.sum(axis=1).transpose(0, 2, 3, 1).reshape(N_GROUPS, SSM_STATE, SSM_GROUP)
    x = x.reshape(-1, per, SSM_STATE, per, SSM_GROUP) * eye[None, :, None, :, None]
    return x.sum(axis=3).reshape(N_GROUPS, SSM_STATE, SSM_GROUP)


SMALL_LAYOUT = {
    'ssm_b_re': (0, 0, 32, 1024), 'ssm_b_im': (32, 0, 32, 1024), 'ssm_c_re': (64, 0, 32, 1024),
    'ssm_c_im': (96, 0, 32, 1024), 'b_ada': (128, 0, 6, 1024), 'g_pre_mix': (134, 0, 1, 1024),
    'g_post_mix': (135, 0, 1, 1024), 'ssm_lam_re': (136, 0, 2, 1024), 'ssm_lam_im': (138, 0, 2, 1024),
    'ssm_log_step': (140, 0, 1, 32), 'glu_b': (141, 0, 1, 512), 'g_out_ssm': (141, 512, 1, 512),
    'g_out_conv': (142, 0, 1, 512), 'ssm_d': (142, 512, 1, 512), 'g_pre_ffn': (143, 0, 1, 1024),
    'g_post_ffn': (144, 0, 1, 1024)}
SMALL_ROWS = 152
B_ADA_ROW = SMALL_LAYOUT['b_ada'][0]
LATE_ROWS = {('b_ada', 0): 0, ('b_ada', 1): 1, ('g_pre_mix', 0): 2}


def _adamw_small(gathered, late, wts, mom_m, mom_v):
    names = list(SMALL_LAYOUT)
    n = len(names)

    def body(*refs):
        g_ref, late_ref, ins, outs = refs[0], refs[1], refs[2:2 + 3 * n], refs[2 + 3 * n:]
        for p, name in enumerate(names):
            r0, c0, rows, cols = SMALL_LAYOUT[name]
            pieces = [(0, rows)] if rows % 8 == 0 else [(r, 1) for r in range(rows)]
            for r, cnt in pieces:
                src_ref, first = (late_ref, LATE_ROWS[name, r]) if (name, r) in LATE_ROWS else (g_ref, r0 + r)
                g = src_ref[0, first:first + cnt, c0:c0 + cols]
                for s in range(1, N_DEV):
                    g = g + src_ref[s, first:first + cnt, c0:c0 + cols]
                w, m, v = (ins[3 * p + q][r:r + cnt, :] for q in range(3))
                res = (g,) + _adamw_math(g, w, m, v)
                for q in range(4):
                    outs[4 * p + q][r:r + cnt, :] = res[q]

    shapes = [SMALL_LAYOUT[name][2:] for name in names]
    args = [gathered, late]
    for name, shp in zip(names, shapes):
        args += [wts[name].reshape(shp), mom_m[name].reshape(shp), mom_v[name].reshape(shp)]
    outs = _call(body, name='adamw_small', grid=(1,),
                 in_specs=[_const(gathered.shape), _const(late.shape)]
                 + [_const(shp) for shp in shapes for _ in range(3)],
                 out_specs=[_const(shp) for shp in shapes for _ in range(4)],
                 out_shape=[_sds(shp) for shp in shapes for _ in range(4)], vmem=VMEM_BIG)(*args)
    res = {}
    for p, name in enumerate(names):
        for q, kind in enumerate(('g', 'd', 'm', 'v')):
            res[kind, name] = outs[4 * p + q].reshape(wts[name].shape)
    return res


def kernel(x, c, w_ada, b_ada, g_pre_mix, g_post_mix, w_in, ssm_lam_re, ssm_lam_im, ssm_log_step, ssm_b_re, ssm_b_im, ssm_c_re, ssm_c_im, ssm_d, glu_w, glu_b, g_out_ssm, conv_w, g_out_conv, w_out, g_pre_ffn, g_post_ffn, w_up, ffn_conv_w, w_down, loss_target, m_w_ada, m_b_ada, m_g_pre_mix, m_g_post_mix, m_w_in, m_ssm_lam_re, m_ssm_lam_im, m_ssm_log_step, m_ssm_b_re, m_ssm_b_im, m_ssm_c_re, m_ssm_c_im, m_ssm_d, m_glu_w, m_glu_b, m_g_out_ssm, m_conv_w, m_g_out_conv, m_w_out, m_g_pre_ffn, m_g_post_ffn, m_w_up, m_ffn_conv_w, m_w_down, v_w_ada, v_b_ada, v_g_pre_mix, v_g_post_mix, v_w_in, v_ssm_lam_re, v_ssm_lam_im, v_ssm_log_step, v_ssm_b_re, v_ssm_b_im, v_ssm_c_re, v_ssm_c_im, v_ssm_d, v_glu_w, v_glu_b, v_g_out_ssm, v_conv_w, v_g_out_conv, v_w_out, v_g_pre_ffn, v_g_post_ffn, v_w_up, v_ffn_conv_w, v_w_down):
    args = dict(locals())
    wts = {n: args[n] for n in WEIGHTS}
    mom_m = {n: args['m_' + n] for n in WEIGHTS}
    mom_v = {n: args['v_' + n] for n in WEIGHTS}
    T = x.shape[1]
    tm = min(512, T)
    tw = min(1024, T)
    tk = min(2048, T)
    me = _me()[3]
    xt, tgt = x[0], loss_target[0]

    c_all, w_in_s = _exchange([c, w_in[0].astype(BF16)], name='gather_first', scatter=False)
    c_all = c_all.reshape(N_DEV, D_MODEL)
    b_cols = lax.dynamic_slice(b_ada, (0, me * ADA_SHARD), (1, ADA_SHARD))
    mod_cols, c_act = _mod_cols(c_all, w_ada[0], b_cols)
    (mod_all,) = _exchange([mod_cols], name='gather_mod', scatter=False)
    mod = lax.dynamic_slice(mod_all, (0, me, 0), (N_DEV, 1, ADA_SHARD)).reshape(N_MOD, 1, D_MODEL)
    sh1, sc1, gt1, sh2, sc2, gt2 = [mod[k] for k in range(N_MOD)]


    lre_x, lim_x = _expand(ssm_lam_re[0]), _expand(ssm_lam_im[0])
    lst_x = jnp.broadcast_to(ssm_log_step[0][:, None], (N_GROUPS, SSM_STATE * SSM_GROUP))
    b_re_x = ssm_b_re[0].reshape(N_GROUPS, -1)
    b_im_x = ssm_b_im[0].reshape(N_GROUPS, -1)
    ar_x, ai_x, bbr_x, bbi_x = _ssm_prep(lre_x, lim_x, lst_x, b_re_x, b_im_x)
    lam_r = ar_x[:, ::SSM_GROUP].reshape(1, N_STATE)
    lam_i = ai_x[:, ::SSM_GROUP].reshape(1, N_STATE)
    big_b_re = _block_diag_b(bbr_x.reshape(N_GROUPS, SSM_STATE, SSM_GROUP)).astype(BF16)
    big_b_im = _block_diag_b(bbi_x.reshape(N_GROUPS, SSM_STATE, SSM_GROUP)).astype(BF16)
    big_c_re = _block_diag_c(ssm_c_re[0]).astype(BF16)
    big_c_im = _block_diag_c(ssm_c_im[0]).astype(BF16)
    head = jnp.arange(D_SSM)
    avg16 = jnp.where(head[:, None] // SSM_GROUP == head[None, :] // SSM_GROUP, 1.0 / SSM_GROUP, 0.0).astype(BF16)
    hd = D_CONV // CONV_HEADS
    avg64 = jnp.where(head[:, None] // hd == head[None, :] // hd, 1.0 / hd, 0.0).astype(BF16)

    (proj, h1), (w_down_s, ffn_conv_s, glu_s, w_out_s, conv_s) = _pre_mix(
        xt, sc1, sh1, g_pre_mix, w_in_s, tw,
        ([w_down[0].astype(BF16), ffn_conv_w[0], glu_w[0].astype(BF16), w_out[0].astype(BF16), conv_w[0]], False))
    glu_full = glu_s.reshape(D_SSM, D_SSM)
    w_out_full = w_out_s.reshape(D_MODEL, D_MODEL)
    cw_full = conv_s.transpose(1, 0, 2).reshape(3, D_CONV)
    wd4 = w_down_s.reshape(4, FF_SHARD, D_MODEL)
    u_perm = _to_scan_rows(proj[:, :D_SSM])
    (s_re, s_im, y_perm), (w_up_s,) = _ssm_fwd(u_perm, big_b_re, big_b_im, big_c_re, big_c_im, lam_r, lam_i,
                                               ([w_up[0].T.astype(BF16)], False))
    yssm = _from_scan_rows(y_perm)
    mix_args = (ssm_d, glu_full, glu_b, g_out_ssm, cw_full, g_out_conv, avg16, avg64)
    ycat = _mix_fwd(yssm, proj, *mix_args, tw)
    o, x1, h2 = _out_proj(ycat, w_out_full, xt, gt1, g_post_mix, g_pre_ffn, sc2, sh2, tw)
    up8, hid8 = _ffn_up(h2, w_up_s, ffn_conv_s, tw)
    hid4 = hid8.reshape(2, 4, T, FF_SHARD)
    ddn, dx2, loss_parts, d_gt2, d_g_post_ffn = _ffn_down(hid4, wd4, x1, tgt, gt2, g_post_ffn, tm)
    loss_local = jnp.sum(loss_parts[:, 0, 0])

    got = {}
    dhid, act = _ffn_dact(ddn, wd4, hid4, tw)
    g_w_down = _grad_tn(act, ddn, pl.BlockSpec((None, tk, FF_SHARD), lambda g, k: (g, k, 0)),
                        pl.BlockSpec((tk, D_MODEL), lambda g, k: (k, 0)), 4, FF_SHARD, D_MODEL, tk, 'grad_w_down')
    (dup8, dcw_ffn), (got['w_down'],) = _ffn_dup(dhid.reshape(N_DEV, T, FF_SHARD), up8, ffn_conv_s, tw,
                                                 ([g_w_down.reshape(N_DEV, D_FF // N_DEV, D_MODEL)], True))
    g_w_up_halves = _grad_tn(dup8, h2, pl.BlockSpec((None, tk, FF_SHARD), lambda g, k: (g, k, 0)),
                             pl.BlockSpec((tk, D_MODEL), lambda g, k: (k, 0)), N_DEV, FF_SHARD, D_MODEL, tk,
                             'grad_w_up', parts=2)
    (dx1, d_sh2, d_sc2, d_g_pre_ffn, d_o, d_gt1, d_g_post_mix), (got_up_0, got['ffn_conv_w']) = _pre_norm_bwd(
        dup8, pl.BlockSpec((2, tw, FF_SHARD), lambda i, j: (j, i, 0)), w_up_s, x1, dx2, sc2, g_pre_ffn, tw,
        'ffn_in_bwd', ([g_w_up_halves[0], dcw_ffn], True), below=(o, gt1, g_post_mix), group=2, w_t=True)

    g_w_out = _grad_tn(ycat, d_o, pl.BlockSpec((tk, D_MODEL), lambda g, k: (k, 0)),
                       pl.BlockSpec((tk, D_MODEL), lambda g, k: (k, 0)), 1, D_MODEL, D_MODEL, tk, 'grad_w_out')
    (dy, dconv, dbg, z_b, dlin_b, sums), (got['w_out'],) = _mix_bwd(
        d_o, w_out_full, yssm, proj, *mix_args, tm, ([g_w_out.reshape(N_DEV, D_MODEL // N_DEV, D_MODEL)], True))
    g_glu_w = _grad_tn(z_b, dlin_b, pl.BlockSpec((tk, D_SSM), lambda g, k: (k, 0)),
                       pl.BlockSpec((tk, D_SSM), lambda g, k: (k, 0)), 1, D_SSM, D_SSM, tk, 'grad_glu_w')
    dy_perm = _to_scan_rows(dy)
    (du_perm, dbr_blk, dbi_blk, dcr_blk, dci_blk, dar_blk, dai_blk), (got_up_1, got['glu_w']) = _ssm_bwd(
        dy_perm, u_perm, s_re, s_im, big_b_re, big_b_im, big_c_re, big_c_im, lam_r, lam_i,
        ([g_w_up_halves[1], g_glu_w.reshape(N_DEV, D_SSM // N_DEV, D_SSM)], True))
    du_ssm = _from_scan_rows(du_perm)
    dproj = _mix_bwd_proj(dconv, proj, du_ssm, dy, ssm_d, dbg, cw_full, tw)
    dbb_re = _diag_blocks(dbr_blk, True).reshape(N_GROUPS, -1)
    dbb_im = _diag_blocks(dbi_blk, True).reshape(N_GROUPS, -1)
    d_c_re = _diag_blocks(dcr_blk, False).transpose(0, 2, 1)
    d_c_im = _diag_blocks(dci_blk, False).transpose(0, 2, 1)
    lane = jnp.arange(SSM_STATE * SSM_GROUP)
    seg = jnp.where(lane[:, None] // SSM_GROUP == lane[None, :] // SSM_GROUP, 1.0, 0.0).astype(BF16)
    d_b_re_x, d_b_im_x, d_lre_x, d_lim_x, d_lst = _ssm_prep_bwd(
        lre_x, lim_x, lst_x, b_re_x, b_im_x, dbb_re, dbb_im, _expand(dar_blk.reshape(N_GROUPS, SSM_STATE)),
        _expand(dai_blk.reshape(N_GROUPS, SSM_STATE)), seg)

    row = lambda a: a.reshape(-1, PACK_COLS)
    blank = jnp.zeros((1, PACK_COLS), F32)
    small_pack = jnp.concatenate([
        d_b_re_x, d_b_im_x, row(d_c_re), row(d_c_im), blank, blank, d_gt1, d_sh2, d_sc2, d_gt2, blank,
        d_g_post_mix, row(d_lre_x[:, ::SSM_GROUP]), row(d_lim_x[:, ::SSM_GROUP]),
        jnp.pad(d_lst.reshape(1, N_GROUPS), ((0, 0), (0, PACK_COLS - N_GROUPS))), row(sums[0:4]), d_g_pre_ffn,
        d_g_post_ffn, jnp.zeros((SMALL_ROWS - 145, PACK_COLS), F32)])
    g_w_in, (small_all,) = _grad_w_in(h1, dproj, tk, ([small_pack], False))
    g_conv_slots = jnp.concatenate([sums[4:7], jnp.zeros((5, D_CONV), F32)]).reshape(
        8, N_DEV, D_CONV // N_DEV).transpose(1, 0, 2)
    (grad_x, d_sh1, d_sc1, d_g_pre_mix), (got['w_in'], got['conv_w']) = _pre_norm_bwd(
        dproj, pl.BlockSpec((tw, D_IN_PROJ), lambda i, j: (i, j)), w_in_s, xt, dx1, sc1, g_pre_mix, tw,
        'mix_in_bwd', ([g_w_in, g_conv_slots], True), group=N_DEV)
    late_pack = jnp.concatenate([d_sh1, d_sc1, d_g_pre_mix, jnp.full((1, PACK_COLS), loss_local, F32),
                                 jnp.zeros((4, PACK_COLS), F32)])
    (late_all,) = _exchange([late_pack], name='gather_late_grads', scatter=False)
    loss = jnp.sum(late_all[:, 3, 0])
    res = _adamw_small(small_all, late_all, wts, mom_m, mom_v)

    dmod_all = jnp.concatenate([late_all[:, 0:2, :], small_all[:, B_ADA_ROW + 2:B_ADA_ROW + N_MOD, :]],
                               axis=1).reshape(N_DEV, N_MOD * D_MODEL)
    dmod_cols = lax.dynamic_slice(dmod_all, (0, me * ADA_SHARD), (N_DEV, ADA_SHARD))
    g_w_ada = _grad_w_ada(c_act.T, dmod_cols)

    pieces = {n: [slots[:, :3, :] if n in ('conv_w', 'ffn_conv_w') else slots] for n, slots in got.items()}
    for n, parts in pieces.items():
        outs = _adamw(parts, wts[n][0], mom_m[n][0], mom_v[n][0], 'adamw_' + n)
        for kind, val in zip(('g', 'd', 'm', 'v'), outs):
            res[kind, n] = val[None]
    outs = _adamw([got_up_0, got_up_1], w_up[0].T, m_w_up[0].T, v_w_up[0].T, 'adamw_w_up')
    for kind, val in zip(('g', 'd', 'm', 'v'), outs):
        res[kind, 'w_up'] = val.T[None]
    outs = _adamw([g_w_ada[None]], w_ada[0], m_w_ada[0], v_w_ada[0], 'adamw_w_ada')
    for kind, val in zip(('g', 'd', 'm', 'v'), outs):
        res[kind, 'w_ada'] = val[None]

    return (loss, grad_x[None], *[res['g', n] for n in WEIGHTS], *[res['d', n] for n in WEIGHTS],
            *[res['m', n] for n in WEIGHTS], *[res['v', n] for n in WEIGHTS])
```

```python
import math

import jax
import jax.numpy as jnp
from jax import lax
from jax.experimental import pallas as pl
from jax.experimental.pallas import tpu as pltpu

F32, BF16 = jnp.float32, jnp.bfloat16

D_MODEL = 1024
D_SSM = 512
D_CONV = 512
SSM_GROUP = 16
N_GROUPS = 32
SSM_STATE = 64
N_STATE = N_GROUPS * SSM_STATE
CONV_HEADS = 8
D_FF = 2816
N_MOD = 6
D_IN_PROJ = D_SSM + 3 * D_CONV
N_DEV = 8
FF_SHARD = 2 * D_FF // N_DEV
IN_SHARD = D_IN_PROJ // N_DEV
ADA_SHARD = N_MOD * D_MODEL // N_DEV
EPS = 1e-6
LAMBDA_RE_MAX = -1e-4
ADAM_LR, ADAM_B1, ADAM_B2, ADAM_EPS, ADAM_WD, ADAM_STEP = 0.001, 0.9, 0.999, 1e-08, 0.01, 10
GELU_C = math.sqrt(2.0 / math.pi)
GELU_A = 0.044715

SUBLANES = 8
HALO = 8
HALO16 = 16
SCAN_UNROLL = 8
STATE_BLOCK = 512
CHAN_BLOCK = 128
VMEM_BIG = 48 << 20
VMEM_MOST = 58 << 20

WEIGHTS = ['w_ada', 'b_ada', 'g_pre_mix', 'g_post_mix', 'w_in', 'ssm_lam_re', 'ssm_lam_im', 'ssm_log_step',
           'ssm_b_re', 'ssm_b_im', 'ssm_c_re', 'ssm_c_im', 'ssm_d', 'glu_w', 'glu_b', 'g_out_ssm', 'conv_w',
           'g_out_conv', 'w_out', 'g_pre_ffn', 'g_post_ffn', 'w_up', 'ffn_conv_w', 'w_down']
SHARDED = ('w_ada', 'w_in', 'glu_w', 'conv_w', 'w_out', 'w_up', 'ffn_conv_w', 'w_down')
PACK_COLS = 1024


def _call(body, *, name, grid, in_specs, out_specs, out_shape, scratch=(), sem=None, vmem=None, ride=None):
    params = {}
    if vmem is not None:
        params['vmem_limit_bytes'] = vmem
    if ride is None:
        if sem is not None:
            params['dimension_semantics'] = sem
        return pl.pallas_call(body, name=name, grid=grid, in_specs=in_specs, out_specs=out_specs,
                              out_shape=out_shape, scratch_shapes=list(scratch),
                              compiler_params=pltpu.CompilerParams(**params))
    arrs, scatter = ride
    single = not isinstance(out_shape, (list, tuple))
    out_shape_l = [out_shape] if single else list(out_shape)
    out_specs_l = [out_specs] if single else list(out_specs)
    n, n_in, n_out, n_scr = len(arrs), len(in_specs), len(out_shape_l), len(scratch)
    any_spec = pl.BlockSpec(memory_space=pl.ANY)
    params['dimension_semantics'] = ('arbitrary',) * len(grid)

    def carried(*refs):
        ins, rin = refs[:n_in], refs[n_in:n_in + n]
        outs, rout = refs[n_in + n:n_in + n + n_out], refs[n_in + n + n_out:n_in + 2 * n + n_out]
        scr, sems = refs[n_in + 2 * n + n_out:n_in + 2 * n + n_out + n_scr], refs[n_in + 2 * n + n_out + n_scr:]
        first = pl.program_id(0) == 0
        last = pl.program_id(0) == grid[0] - 1
        for ax in range(1, len(grid)):
            first = jnp.logical_and(first, pl.program_id(ax) == 0)
            last = jnp.logical_and(last, pl.program_id(ax) == grid[ax] - 1)

        @pl.when(first)
        def _():
            _exchange_start(rin, rout, sems, scatter)

        body(*ins, *outs, *scr)

        @pl.when(last)
        def _():
            _exchange_wait(rin, rout, sems, scatter)

    call = pl.pallas_call(carried, name=name, grid=grid, in_specs=list(in_specs) + [any_spec] * n,
                          out_specs=out_specs_l + [any_spec] * n,
                          out_shape=out_shape_l + _exchange_shapes(arrs, scatter),
                          scratch_shapes=list(scratch) + _exchange_sems(n),
                          compiler_params=pltpu.CompilerParams(**params))

    def run(*args):
        res = call(*args, *arrs)
        own = res[0] if single else list(res[:n_out])
        return own, list(res[n_out:])

    return run


def _const(shape):
    nd = len(shape)
    return pl.BlockSpec(shape, lambda *_: (0,) * nd)


def _sds(shape, dtype=F32):
    return jax.ShapeDtypeStruct(shape, dtype)


def _dot(a, b):
    return jnp.dot(a, b, preferred_element_type=F32)


def _dot_nt(a, b):
    return lax.dot_general(a, b, (((1,), (1,)), ((), ())), preferred_element_type=F32)


def _dot_tn(a, b):
    return lax.dot_general(a, b, (((0,), (0,)), ((), ())), preferred_element_type=F32)


def _dot_split(x, mat, parts):
    acc = None
    rem = x
    for _ in range(parts):
        piece = rem.astype(BF16)
        rem = rem - piece.astype(F32)
        term = _dot(piece, mat)
        acc = term if acc is None else acc + term
    return acc


def _sigmoid(x):
    return 1.0 / (1.0 + jnp.exp(-x))


def _gelu(x):
    t = jnp.tanh(GELU_C * (x + GELU_A * x * x * x))
    return 0.5 * x * (1.0 + t), t


def _gelu_grad(x, t):
    return 0.5 * (1.0 + t) + 0.5 * x * (1.0 - t * t) * GELU_C * (1.0 + 3.0 * GELU_A * x * x)


def _rsqrt_mean(x):
    return lax.rsqrt(jnp.mean(x * x, axis=-1, keepdims=True) + EPS)


def _colsum(x):
    return jnp.sum(x, axis=0, keepdims=True)


def _shifts_down(x, halo):
    ext = jnp.concatenate([halo, x], axis=0)
    return pltpu.roll(ext, 1, 0)[halo.shape[0]:], pltpu.roll(ext, 2, 0)[halo.shape[0]:]


def _shifts_up(x, halo):
    n = x.shape[0]
    ext = jnp.concatenate([x, halo], axis=0)
    total = ext.shape[0]
    return pltpu.roll(ext, total - 1, 0)[:n], pltpu.roll(ext, total - 2, 0)[:n]


def _conv3(x, halo, w_ref):
    x1, x2 = _shifts_down(x, halo)
    return w_ref[0:1, :] * x2 + w_ref[1:2, :] * x1 + w_ref[2:3, :] * x, x1, x2


def _conv3_t(g, halo, w_ref):
    g1, g2 = _shifts_up(g, halo)
    return w_ref[2:3, :] * g + w_ref[1:2, :] * g1 + w_ref[0:1, :] * g2, g1, g2


def _silu_parts(x):
    s = _sigmoid(x)
    return x * s, s * (1.0 + x * (1.0 - s))


def _norm_bwd(dn, x, r, g):
    gd = g * dn
    return r * gd - x * (r * r * r) * jnp.mean(gd * x, axis=-1, keepdims=True)


def _head_norm_bwd(dn, y, rs, g, avg):
    gd = g * dn
    return rs * gd - y * (rs * rs * rs) * _dot_split(gd * y, avg, 2)


def _me():
    x, y, c = lax.axis_index('x'), lax.axis_index('y'), lax.axis_index('c')
    return x, y, c, 4 * x + 2 * y + c


def _peer(k):
    x, y, c, _ = _me()
    px = 1 - x if k & 4 else x
    py = 1 - y if k & 2 else y
    pc = 1 - c if k & 1 else c
    return (px, py, pc), 4 * px + 2 * py + pc


SIBLING = 1
OTHER_CHIPS = (2, 4, 6)


def _remote(src, dst, sems, a, k, dev):
    return pltpu.make_async_remote_copy(src_ref=src, dst_ref=dst, send_sem=sems[0].at[a, k - 1],
                                        recv_sem=sems[1].at[a, k - 1], device_id=dev,
                                        device_id_type=pl.DeviceIdType.MESH)


def _exchange_copies(ins, outs, sems, scatter):
    me = _me()[3]
    local, first, relay, arrivals = [], [], [], []
    for a in range(len(ins)):
        src = ins[a].at[me] if scatter else ins[a]
        local.append(pltpu.make_async_copy(src, outs[a].at[me], sems[2].at[a]))
        for k in range(1, N_DEV):
            dev, idx = _peer(k)
            landed = _remote(src, outs[a].at[idx], sems, a, k, dev)
            if scatter:
                first.append(_remote(ins[a].at[idx], outs[a].at[me], sems, a, k, dev))
                arrivals.append(landed)
            elif k == SIBLING:
                first.append(_remote(src, outs[a].at[me], sems, a, k, dev))
                arrivals.append(landed)
            elif k in OTHER_CHIPS:
                first.append(_remote(src, outs[a].at[me], sems, a, k, dev))
                sib, _ = _peer(SIBLING)
                relay.append((landed, _remote(outs[a].at[idx], outs[a].at[idx], sems, a, k | SIBLING, sib)))
            else:
                arrivals.append(landed)
    return local, first, relay, arrivals


def _exchange_start(ins, outs, sems, scatter):
    local, first, _, _ = _exchange_copies(ins, outs, sems, scatter)
    for cp in local + first:
        cp.start()


def _exchange_wait(ins, outs, sems, scatter):
    local, first, relay, arrivals = _exchange_copies(ins, outs, sems, scatter)
    for landed, forward in relay:
        landed.wait_recv()
        forward.start()
    for cp in arrivals:
        cp.wait_recv()
    for cp in first + [forward for _, forward in relay]:
        cp.wait_send()
    for cp in local:
        cp.wait()


def _exchange_shapes(arrs, scatter):
    return [_sds(a.shape if scatter else (N_DEV,) + a.shape, a.dtype) for a in arrs]


def _exchange_sems(n):
    return [pltpu.SemaphoreType.DMA((n, N_DEV - 1)), pltpu.SemaphoreType.DMA((n, N_DEV - 1)),
            pltpu.SemaphoreType.DMA((n,))]


def _exchange(arrs, *, name, scatter):
    n = len(arrs)

    def body(*refs):
        _exchange_start(refs[:n], refs[n:2 * n], refs[2 * n:], scatter)
        _exchange_wait(refs[:n], refs[n:2 * n], refs[2 * n:], scatter)

    any_spec = pl.BlockSpec(memory_space=pl.ANY)
    outs = pl.pallas_call(body, name=name, out_shape=_exchange_shapes(arrs, scatter), in_specs=[any_spec] * n,
                          out_specs=[any_spec] * n, scratch_shapes=_exchange_sems(n))(*arrs)
    return list(outs)


def _mod_cols(c_all, w_ada, b_cols):
    def body(c_ref, w_ref, b_ref, mod_ref, act_ref):
        c = c_ref[...]
        act = c * _sigmoid(c)
        act_ref[...] = act
        mod_ref[...] = _dot(act.astype(BF16), w_ref[...].astype(BF16)) + b_ref[...]

    return _call(body, name='mod_cols', grid=(1,),
                 in_specs=[_const(c_all.shape), _const(w_ada.shape), _const(b_cols.shape)],
                 out_specs=[_const((N_DEV, ADA_SHARD)), _const(c_all.shape)],
                 out_shape=[_sds((N_DEV, ADA_SHARD)), _sds(c_all.shape)], vmem=VMEM_BIG)(c_all, w_ada, b_cols)


def _grad_w_ada(act_t, dmod_cols):
    def body(a_ref, d_ref, o_ref):
        o_ref[...] = _dot(a_ref[...], d_ref[...])

    return _call(body, name='grad_w_ada', grid=(1,), in_specs=[_const(act_t.shape), _const(dmod_cols.shape)],
                 out_specs=_const((D_MODEL, ADA_SHARD)), out_shape=_sds((D_MODEL, ADA_SHARD)),
                 vmem=VMEM_BIG)(act_t, dmod_cols)


def _pre_mix(x, sc, sh, g, w_s, tm, ride):
    T = x.shape[0]

    def body(x_ref, sc_ref, sh_ref, g_ref, w_ref, proj_ref, h_ref):
        @pl.when(pl.program_id(1) == 0)
        def _():
            xv = x_ref[...]
            h_ref[...] = ((xv * _rsqrt_mean(xv) * g_ref[...]) * (1.0 + sc_ref[...]) + sh_ref[...]).astype(BF16)

        for s in range(2):
            proj_ref[:, s * IN_SHARD:(s + 1) * IN_SHARD] = _dot(h_ref[...], w_ref[s])

    row = pl.BlockSpec((tm, D_MODEL), lambda i, j: (i, 0))
    vec = _const((1, D_MODEL))
    return _call(body, name='pre_mix', grid=(T // tm, N_DEV // 2),
                 in_specs=[row, vec, vec, vec, pl.BlockSpec((2, D_MODEL, IN_SHARD), lambda i, j: (j, 0, 0))],
                 out_specs=[pl.BlockSpec((tm, 2 * IN_SHARD), lambda i, j: (i, j)), row],
                 out_shape=[_sds((T, D_IN_PROJ)), _sds((T, D_MODEL), BF16)],
                 sem=('parallel', 'arbitrary'), ride=ride)(x, sc, sh, g, w_s)


def _halo_before(tm, rows=HALO):
    return lambda i: jnp.maximum(i * (tm // rows) - 1, 0)


def _halo_after(tm, T, rows=HALO):
    return lambda i: jnp.minimum((i + 1) * (tm // rows), T // rows - 1)


def _mix_fwd(yssm, proj, d, glu_w, glu_b, g_ssm, cw, g_conv, avg16, avg64, tm):
    T = yssm.shape[0]
    hb = _halo_before(tm)

    def body(y_ref, p_ref, ph_ref, d_ref, gw_ref, gb_ref, gs_ref, cw_ref, gc_ref, a16_ref, a64_ref, o_ref):
        i = pl.program_id(0)
        u = p_ref[:, 0:D_SSM]
        y = y_ref[...] + d_ref[...] * u
        z, _ = _gelu(y)
        gate = _sigmoid(_dot(z.astype(BF16), gw_ref[...]) + gb_ref[...])
        ya = z * gate
        rs = lax.rsqrt(_dot_split(ya * ya, a16_ref[...], 2) + EPS)
        o_ref[:, 0:D_SSM] = (ya * rs * gs_ref[...]).astype(BF16)
        bg = p_ref[:, D_SSM:D_SSM + D_CONV]
        cv = p_ref[:, D_SSM + D_CONV:D_SSM + 2 * D_CONV] * p_ref[:, D_SSM + 2 * D_CONV:D_IN_PROJ]
        hv = ph_ref[:, D_SSM + D_CONV:D_SSM + 2 * D_CONV] * ph_ref[:, D_SSM + 2 * D_CONV:D_IN_PROJ]
        hv = jnp.where(i > 0, hv, 0.0)
        conv, _, _ = _conv3(cv, hv, cw_ref)
        yb = bg * conv
        rsb = lax.rsqrt(_dot_split(yb * yb, a64_ref[...], 2) + EPS)
        o_ref[:, D_SSM:D_MODEL] = (yb * rsb * gc_ref[...]).astype(BF16)

    vec = _const((1, D_SSM))
    sq = _const((D_SSM, D_SSM))
    return _call(body, name='mix_fwd', grid=(T // tm,),
                 in_specs=[pl.BlockSpec((tm, D_SSM), lambda i: (i, 0)), pl.BlockSpec((tm, D_IN_PROJ), lambda i: (i, 0)),
                           pl.BlockSpec((HALO, D_IN_PROJ), lambda i: (hb(i), 0)), vec, sq, vec, vec,
                           _const((3, D_CONV)), vec, sq, sq],
                 out_specs=pl.BlockSpec((tm, D_MODEL), lambda i: (i, 0)), out_shape=_sds((T, D_MODEL), BF16),
                 sem=('parallel',), vmem=VMEM_BIG)(yssm, proj, proj, d, glu_w, glu_b, g_ssm, cw, g_conv, avg16, avg64)


def _out_proj(ycat, w_out, x, gt, g_post, g_pre, sc, sh, tm):
    T = x.shape[0]

    def body(y_ref, w_ref, x_ref, gt_ref, gp_ref, g2_ref, sc_ref, sh_ref, o_ref, x1_ref, h_ref):
        o = _dot(y_ref[...], w_ref[...])
        o_ref[...] = o
        x1 = x_ref[...] + gt_ref[...] * (o * _rsqrt_mean(o) * gp_ref[...])
        x1_ref[...] = x1
        h_ref[...] = ((x1 * _rsqrt_mean(x1) * g2_ref[...]) * (1.0 + sc_ref[...]) + sh_ref[...]).astype(BF16)

    row = pl.BlockSpec((tm, D_MODEL), lambda i: (i, 0))
    vec = _const((1, D_MODEL))
    return _call(body, name='out_proj', grid=(T // tm,),
                 in_specs=[row, _const((D_MODEL, D_MODEL)), row, vec, vec, vec, vec, vec],
                 out_specs=[row, row, row],
                 out_shape=[_sds((T, D_MODEL)), _sds((T, D_MODEL)), _sds((T, D_MODEL), BF16)],
                 sem=('parallel',), vmem=VMEM_BIG)(ycat, w_out, x, gt, g_post, g_pre, sc, sh)


def _ffn_up(h2, w_s, cw8, tm):
    T = h2.shape[0]
    hb = _halo_before(tm, HALO16)

    def body(h_ref, hh_ref, w_ref, cw_ref, up_ref, hid_ref):
        up = _dot_nt(h_ref[...], w_ref[...])
        up_ref[...] = up.astype(BF16)
        before = jnp.where(pl.program_id(0) > 0, _dot_nt(hh_ref[...], w_ref[...]), 0.0)
        hid_ref[...] = _conv3(up, before, cw_ref)[0].astype(BF16)

    out = pl.BlockSpec((None, tm, FF_SHARD), lambda i, j: (j, i, 0))
    return _call(body, name='ffn_up', grid=(T // tm, N_DEV),
                 in_specs=[pl.BlockSpec((tm, D_MODEL), lambda i, j: (i, 0)),
                           pl.BlockSpec((HALO16, D_MODEL), lambda i, j: (hb(i), 0)),
                           pl.BlockSpec((None, FF_SHARD, D_MODEL), lambda i, j: (j, 0, 0)),
                           pl.BlockSpec((None, 3, FF_SHARD), lambda i, j: (j, 0, 0))],
                 out_specs=[out, out], out_shape=[_sds((N_DEV, T, FF_SHARD), BF16)] * 2,
                 sem=('parallel', 'parallel'))(h2, h2, w_s, cw8)


def _ffn_down(hid4, wd4, x1, tgt, gt, g_post, tm):
    T = x1.shape[0]
    nb = T // tm

    def body(a_ref, w_ref, x1_ref, t_ref, gt_ref, g_ref, ddn_ref, dx_ref, loss_ref, dgt_ref, dg_ref, dn_ref):
        i, j = pl.program_id(0), pl.program_id(1)
        part = None
        for s in range(2):
            act = (_silu_parts(a_ref[0, s].astype(F32))[0] * a_ref[1, s].astype(F32)).astype(BF16)
            term = _dot(act, w_ref[s])
            part = term if part is None else part + term

        @pl.when(jnp.logical_and(i == 0, j == 0))
        def _():
            dgt_ref[...] = jnp.zeros_like(dgt_ref)
            dg_ref[...] = jnp.zeros_like(dg_ref)

        @pl.when(j == 0)
        def _():
            dn_ref[...] = part

        @pl.when(j > 0)
        def _():
            dn_ref[...] += part

        @pl.when(j == 1)
        def _():
            dn, gv, gate = dn_ref[...], g_ref[...], gt_ref[...]
            r = _rsqrt_mean(dn)
            normed = dn * r * gv
            err = x1_ref[...] + gate * normed - t_ref[...]
            dx = err * (1.0 / D_MODEL)
            dx_ref[...] = dx
            tot = jnp.sum(jnp.sum(err * err, axis=1, keepdims=True), axis=0, keepdims=True) * (0.5 / D_MODEL)
            loss_ref[...] = jnp.broadcast_to(tot, (8, 128))
            dgt_ref[...] += _colsum(dx * normed)
            dnn = dx * gate
            dg_ref[...] += _colsum(dnn * dn * r)
            ddn_ref[...] = _norm_bwd(dnn, dn, r, gv).astype(BF16)

    row = pl.BlockSpec((tm, D_MODEL), lambda i, j: (i, 0))
    vec = _const((1, D_MODEL))
    return _call(body, name='ffn_down', grid=(nb, 2),
                 in_specs=[pl.BlockSpec((2, 2, tm, FF_SHARD), lambda i, j: (0, j, i, 0)),
                           pl.BlockSpec((2, FF_SHARD, D_MODEL), lambda i, j: (j, 0, 0)), row, row, vec, vec],
                 out_specs=[row, row, pl.BlockSpec((None, 8, 128), lambda i, j: (i, 0, 0)), vec, vec],
                 out_shape=[_sds((T, D_MODEL), BF16), _sds((T, D_MODEL)), _sds((nb, 8, 128)), _sds((1, D_MODEL)),
                            _sds((1, D_MODEL))],
                 scratch=[pltpu.VMEM((tm, D_MODEL), F32)], sem=('arbitrary', 'arbitrary'),
                 vmem=VMEM_BIG)(hid4, wd4, x1, tgt, gt, g_post)


def _ssm_prep(lre, lim, lst, b_re, b_im):
    def body(lre_ref, lim_ref, lst_ref, br_ref, bi_ref, ar_ref, ai_ref, bbr_ref, bbi_ref):
        ar, ai, qr, qi = _zoh(lre_ref[...], lim_ref[...], lst_ref[...])[:4]
        ar_ref[...] = ar
        ai_ref[...] = ai
        bbr_ref[...] = qr * br_ref[...] - qi * bi_ref[...]
        bbi_ref[...] = qr * bi_ref[...] + qi * br_ref[...]

    shp = lre.shape
    return _call(body, name='ssm_prep', grid=(1,), in_specs=[_const(shp)] * 5, out_specs=[_const(shp)] * 4,
                 out_shape=[_sds(shp)] * 4)(lre, lim, lst, b_re, b_im)


def _zoh(lre, lim, lst):
    lr = jnp.minimum(lre, LAMBDA_RE_MAX)
    st = jnp.exp(lst)
    mag = jnp.exp(lr * st)
    ar = mag * jnp.cos(lim * st)
    ai = mag * jnp.sin(lim * st)
    den = lr * lr + lim * lim
    qr = ((ar - 1.0) * lr + ai * lim) / den
    qi = (ai * lr - (ar - 1.0) * lim) / den
    return ar, ai, qr, qi, lr, st, den


def _ssm_prep_bwd(lre, lim, lst, b_re, b_im, dbbr, dbbi, dar, dai, seg):
    def body(lre_ref, lim_ref, lst_ref, br_ref, bi_ref, dbbr_ref, dbbi_ref, dar_ref, dai_ref, seg_ref,
             dbr_ref, dbi_ref, dlre_ref, dlim_ref, dlst_ref):
        lre_v = lre_ref[...]
        li = lim_ref[...]
        ar, ai, qr, qi, lr, st, den = _zoh(lre_v, li, lst_ref[...])
        br, bi, gbr, gbi = br_ref[...], bi_ref[...], dbbr_ref[...], dbbi_ref[...]
        dbr_ref[...] = qr * gbr + qi * gbi
        dbi_ref[...] = qr * gbi - qi * gbr
        gqr = _dot_split(br * gbr + bi * gbi, seg_ref[...], 3)
        gqi = _dot_split(br * gbi - bi * gbr, seg_ref[...], 3)
        ir, ii = lr / den, -li / den
        gar = dar_ref[...] + ir * gqr + ii * gqi
        gai = dai_ref[...] + ir * gqi - ii * gqr
        tr, ti = qr * ir - qi * ii, qr * ii + qi * ir
        glr = -(tr * gqr + ti * gqi)
        gli = -(tr * gqi - ti * gqr)
        gzr = ar * gar + ai * gai
        gzi = ar * gai - ai * gar
        glr = glr + st * gzr
        gli = gli + st * gzi
        gst = (lr * gzr + li * gzi) * st
        dlre_ref[...] = jnp.where(lre_v < LAMBDA_RE_MAX, glr, 0.0)
        dlim_ref[...] = gli
        dlst_ref[...] = jnp.sum(gst, axis=1, keepdims=True) * (1.0 / SSM_GROUP)

    shp = lre.shape
    return _call(body, name='ssm_prep_bwd', grid=(1,), in_specs=[_const(shp)] * 9 + [_const(seg.shape)],
                 out_specs=[_const(shp)] * 4 + [_const((N_GROUPS, 1))],
                 out_shape=[_sds(shp)] * 4 + [_sds((N_GROUPS, 1))], vmem=VMEM_BIG)(
                     lre, lim, lst, b_re, b_im, dbbr, dbbi, dar, dai, seg)


def _scan_specs(T):
    return dict(
        chan=pl.BlockSpec((T, CHAN_BLOCK), lambda cb: (0, cb)),
        state=pl.BlockSpec((T, STATE_BLOCK), lambda cb: (0, cb)),
        b=pl.BlockSpec((CHAN_BLOCK, STATE_BLOCK), lambda cb: (cb, cb)),
        c=pl.BlockSpec((STATE_BLOCK, CHAN_BLOCK), lambda cb: (cb, cb)),
        lam=pl.BlockSpec((1, STATE_BLOCK), lambda cb: (0, cb)),
    )


def _complex_power(re, im, n):
    out = None
    while True:
        if n & 1:
            out = (re, im) if out is None else (out[0] * re - out[1] * im, out[0] * im + out[1] * re)
        n >>= 1
        if n == 0:
            return out
        re, im = re * re - im * im, 2.0 * re * im


def _rows8(i):
    if isinstance(i, int):
        return pl.ds(i * SUBLANES, SUBLANES)
    return pl.ds(pl.multiple_of(i * SUBLANES, SUBLANES), SUBLANES)


def _scan_loop(n_steps, body, init):
    trips = n_steps // SCAN_UNROLL

    def trip(t, carry):
        for u in range(SCAN_UNROLL):
            carry = body(t * SCAN_UNROLL + u, carry)
        return carry

    carry = lax.fori_loop(0, trips, trip, init)
    for step in range(trips * SCAN_UNROLL, n_steps):
        carry = body(step, carry)
    return carry


def _ssm_fwd(u_perm, b_re, b_im, c_re, c_im, lam_r, lam_i, ride):
    T = u_perm.shape[0]
    ls = T // SUBLANES
    rc = min(512, T)
    sp = _scan_specs(T)

    def body(u_ref, bre_ref, bim_ref, cre_ref, cim_ref, lr_ref, li_ref, so_re_ref, so_im_ref, y_ref, sre_ref, sim_ref):
        for c in range(T // rc):
            rows = pl.ds(c * rc, rc)
            ub = u_ref[rows, :].astype(BF16)
            sre_ref[rows, :] = _dot(ub, bre_ref[...])
            sim_ref[rows, :] = _dot(ub, bim_ref[...])
        shp = (SUBLANES, STATE_BLOCK)
        lr = jnp.broadcast_to(lr_ref[...], shp)
        li = jnp.broadcast_to(li_ref[...], shp)
        zero = jnp.zeros(shp, F32)

        def step(i, carry):
            sr, si = carry
            rows = _rows8(i)
            nr = lr * sr - li * si + sre_ref[rows, :]
            ni = lr * si + li * sr + sim_ref[rows, :]
            sre_ref[rows, :] = nr
            sim_ref[rows, :] = ni
            return nr, ni

        fr, fi = _scan_loop(ls, step, (zero, zero))
        pr, pi_ = _complex_power(lr, li, ls)
        row = lax.broadcasted_iota(jnp.int32, shp, 0)
        ir, ii = zero, zero
        for _ in range(SUBLANES - 1):
            er = fr + pr * ir - pi_ * ii
            ei = fi + pr * ii + pi_ * ir
            ir = jnp.where(row == 0, 0.0, pltpu.roll(er, 1, 0))
            ii = jnp.where(row == 0, 0.0, pltpu.roll(ei, 1, 0))

        def fix(i, carry):
            cr, ci = carry
            rows = _rows8(i)
            nr = lr * cr - li * ci
            ni = lr * ci + li * cr
            sre_ref[rows, :] += nr
            sim_ref[rows, :] += ni
            return nr, ni

        _scan_loop(ls, fix, (ir, ii))
        for c in range(T // rc):
            rows = pl.ds(c * rc, rc)
            s_r, s_i = sre_ref[rows, :].astype(BF16), sim_ref[rows, :].astype(BF16)
            so_re_ref[rows, :] = s_r
            so_im_ref[rows, :] = s_i
            y_ref[rows, :] = _dot(s_r, cre_ref[...]) - _dot(s_i, cim_ref[...])

    return _call(body, name='ssm_fwd', grid=(N_STATE // STATE_BLOCK,),
                 in_specs=[sp['chan'], sp['b'], sp['b'], sp['c'], sp['c'], sp['lam'], sp['lam']],
                 out_specs=[sp['state'], sp['state'], sp['chan']],
                 out_shape=[_sds((T, N_STATE), BF16), _sds((T, N_STATE), BF16), _sds((T, D_SSM))],
                 scratch=[pltpu.VMEM((T, STATE_BLOCK), F32), pltpu.VMEM((T, STATE_BLOCK), F32)],
                 sem=('arbitrary',), vmem=VMEM_MOST, ride=ride)(u_perm, b_re, b_im, c_re, c_im, lam_r, lam_i)


def _ssm_bwd(dy_perm, u_perm, s_re, s_im, b_re, b_im, c_re, c_im, lam_r, lam_i, ride):
    T = u_perm.shape[0]
    ls = T // SUBLANES
    rc = min(512, T)
    sp = _scan_specs(T)
    ncb = N_STATE // STATE_BLOCK

    def body(dy_ref, u_ref, sre_ref, sim_ref, bre_ref, bim_ref, cre_ref, cim_ref, lr_ref, li_ref,
             du_ref, dbr_ref, dbi_ref, dcr_ref, dci_ref, dar_ref, dai_ref, gre_ref, gim_ref):
        shp = (SUBLANES, STATE_BLOCK)
        zero = jnp.zeros(shp, F32)
        tail = pl.ds(T, SUBLANES)
        gre_ref[tail, :] = zero
        gim_ref[tail, :] = zero
        for c in range(T // rc):
            rows = pl.ds(c * rc, rc)
            dyb = dy_ref[rows, :].astype(BF16)
            gre_ref[rows, :] = _dot_nt(dyb, cre_ref[...])
            gim_ref[rows, :] = -_dot_nt(dyb, cim_ref[...])
        lr = jnp.broadcast_to(lr_ref[...], shp)
        li = jnp.broadcast_to(li_ref[...], shp)

        def step(k, carry):
            gr, gi = carry
            rows = _rows8(ls - 1 - k)
            nr = lr * gr + li * gi + gre_ref[rows, :]
            ni = lr * gi - li * gr + gim_ref[rows, :]
            gre_ref[rows, :] = nr
            gim_ref[rows, :] = ni
            return nr, ni

        fr, fi = _scan_loop(ls, step, (zero, zero))
        pr, pi_ = _complex_power(lr, -li, ls)
        row = lax.broadcasted_iota(jnp.int32, shp, 0)
        cr, ci = zero, zero
        for _ in range(SUBLANES - 1):
            er = fr + pr * cr - pi_ * ci
            ei = fi + pr * ci + pi_ * cr
            cr = jnp.where(row == SUBLANES - 1, 0.0, pltpu.roll(er, SUBLANES - 1, 0))
            ci = jnp.where(row == SUBLANES - 1, 0.0, pltpu.roll(ei, SUBLANES - 1, 0))

        def fix(k, carry):
            dr, di = carry
            rows = _rows8(ls - 1 - k)
            dr, di = lr * dr + li * di, lr * di - li * dr
            gre_ref[rows, :] += dr
            gim_ref[rows, :] += di
            return dr, di

        _scan_loop(ls, fix, (cr, ci))

        acc_r = jnp.zeros((1, STATE_BLOCK), F32)
        acc_i = jnp.zeros((1, STATE_BLOCK), F32)
        for c in range(T // rc):
            rows, nxt = pl.ds(c * rc, rc), pl.ds(c * rc + SUBLANES, rc)
            s_r, s_i = sre_ref[rows, :].astype(F32), sim_ref[rows, :].astype(F32)
            g_r, g_i = gre_ref[nxt, :], gim_ref[nxt, :]
            acc_r = acc_r + _colsum(g_r * s_r + g_i * s_i)
            acc_i = acc_i + _colsum(g_i * s_r - g_r * s_i)
        last = pl.ds(T - 2 * SUBLANES, 2 * SUBLANES)
        first = pl.ds(0, SUBLANES)
        spr = jnp.where(row == 0, 0.0, pltpu.roll(sre_ref[last, :].astype(F32)[SUBLANES:], 1, 0))
        spi = jnp.where(row == 0, 0.0, pltpu.roll(sim_ref[last, :].astype(F32)[SUBLANES:], 1, 0))
        gr, gi = gre_ref[first, :], gim_ref[first, :]
        dar_ref[...] = acc_r + _colsum(gr * spr + gi * spi)
        dai_ref[...] = acc_i + _colsum(gi * spr - gr * spi)

        for c in range(T // rc):
            rows = pl.ds(c * rc, rc)
            g_r, g_i = gre_ref[rows, :].astype(BF16), gim_ref[rows, :].astype(BF16)
            s_r, s_i = sre_ref[rows, :], sim_ref[rows, :]
            ub, dyb = u_ref[rows, :].astype(BF16), dy_ref[rows, :].astype(BF16)
            du_ref[rows, :] = _dot_nt(g_r, bre_ref[...]) + _dot_nt(g_i, bim_ref[...])
            parts = (_dot_tn(ub, g_r), _dot_tn(ub, g_i), _dot_tn(s_r, dyb), -_dot_tn(s_i, dyb))
            outs = (dbr_ref, dbi_ref, dcr_ref, dci_ref)
            for o_ref, part in zip(outs, parts):
                if c == 0:
                    o_ref[...] = part
                else:
                    o_ref[...] += part

    blk = lambda r, c: pl.BlockSpec((None, r, c), lambda cb: (cb, 0, 0))
    return _call(body, name='ssm_bwd', grid=(ncb,),
                 in_specs=[sp['chan'], sp['chan'], sp['state'], sp['state'], sp['b'], sp['b'], sp['c'], sp['c'],
                           sp['lam'], sp['lam']],
                 out_specs=[sp['chan'], blk(CHAN_BLOCK, STATE_BLOCK), blk(CHAN_BLOCK, STATE_BLOCK),
                            blk(STATE_BLOCK, CHAN_BLOCK), blk(STATE_BLOCK, CHAN_BLOCK), blk(1, STATE_BLOCK),
                            blk(1, STATE_BLOCK)],
                 out_shape=[_sds((T, D_SSM)), _sds((ncb, CHAN_BLOCK, STATE_BLOCK)), _sds((ncb, CHAN_BLOCK, STATE_BLOCK)),
                            _sds((ncb, STATE_BLOCK, CHAN_BLOCK)), _sds((ncb, STATE_BLOCK, CHAN_BLOCK)),
                            _sds((ncb, 1, STATE_BLOCK)), _sds((ncb, 1, STATE_BLOCK))],
                 scratch=[pltpu.VMEM((T + SUBLANES, STATE_BLOCK), F32), pltpu.VMEM((T + SUBLANES, STATE_BLOCK), F32)],
                 sem=('arbitrary',), vmem=VMEM_MOST, ride=ride)(dy_perm, u_perm, s_re, s_im, b_re, b_im, c_re, c_im,
                                                                lam_r, lam_i)


def _ffn_dact(ddn, wd4, hid4, tm):
    T = ddn.shape[0]
    nb = T // tm

    def body(d_ref, w_ref, hid_ref, o_ref, gw_ref, acc_ref):
        i = pl.program_id(1)
        d = d_ref[...]
        dact = _dot_nt(d, w_ref[...])
        silu, dsilu = _silu_parts(hid_ref[0].astype(F32))
        hid_v = hid_ref[1].astype(F32)
        o_ref[0] = (dact * hid_v * dsilu).astype(BF16)
        o_ref[1] = (dact * silu).astype(BF16)
        part = _dot_tn((silu * hid_v).astype(BF16), d)

        @pl.when(i == 0)
        def _():
            acc_ref[...] = part

        @pl.when(i > 0)
        def _():
            acc_ref[...] += part

        @pl.when(i == nb - 1)
        def _():
            gw_ref[...] = acc_ref[...].astype(BF16)

    blk = pl.BlockSpec((2, None, tm, FF_SHARD), lambda j, i: (0, j, i, 0))
    w_blk = pl.BlockSpec((None, FF_SHARD, D_MODEL), lambda j, i: (j, 0, 0))
    return _call(body, name='ffn_dact', grid=(4, nb),
                 in_specs=[pl.BlockSpec((tm, D_MODEL), lambda j, i: (i, 0)), w_blk, blk],
                 out_specs=[blk, w_blk],
                 out_shape=[_sds((2, 4, T, FF_SHARD), BF16), _sds((4, FF_SHARD, D_MODEL), BF16)],
                 scratch=[pltpu.VMEM((FF_SHARD, D_MODEL), F32)], sem=('parallel', 'arbitrary'),
                 vmem=VMEM_BIG)(ddn, wd4, hid4)


def _ffn_dup(dhid8, up8, cw8, tm, ride):
    T = up8.shape[1]
    nb = T // tm
    ha = _halo_after(tm, T, HALO16)

    def body(dh_ref, dha_ref, up_ref, cw_ref, dup_ref, dcw_ref):
        i = pl.program_id(1)

        @pl.when(i == 0)
        def _():
            dcw_ref[...] = jnp.zeros_like(dcw_ref)

        dh = dh_ref[...].astype(F32)
        dup, dh1, dh2 = _conv3_t(dh, jnp.where(i < nb - 1, dha_ref[...].astype(F32), 0.0), cw_ref)
        dup_ref[...] = dup.astype(BF16)
        up = up_ref[...].astype(F32)
        dcw_ref[0:1, :] += _colsum(dh2 * up)
        dcw_ref[1:2, :] += _colsum(dh1 * up)
        dcw_ref[2:3, :] += _colsum(dh * up)

    main = pl.BlockSpec((None, tm, FF_SHARD), lambda j, i: (j, i, 0))
    return _call(body, name='ffn_dup', grid=(N_DEV, nb),
                 in_specs=[main, pl.BlockSpec((None, HALO16, FF_SHARD), lambda j, i: (j, ha(i), 0)), main,
                           pl.BlockSpec((None, 3, FF_SHARD), lambda j, i: (j, 0, 0))],
                 out_specs=[main, pl.BlockSpec((None, 8, FF_SHARD), lambda j, i: (j, 0, 0))],
                 out_shape=[_sds((N_DEV, T, FF_SHARD), BF16), _sds((N_DEV, 8, FF_SHARD))],
                 sem=('parallel', 'arbitrary'), vmem=VMEM_BIG, ride=ride)(dhid8, dhid8, up8, cw8)


def _grad_tn(a, b, a_spec, b_spec, groups, m, n, tk, name, ride=None, parts=1):
    T = a.shape[-2]
    nk = T // tk
    mp = m // parts

    def body(a_ref, b_ref, *refs):
        o_refs, acc_ref = refs[:parts], refs[parts]
        k = pl.program_id(1)
        part = _dot_tn(a_ref[...], b_ref[...])

        @pl.when(k == 0)
        def _():
            acc_ref[...] = part

        @pl.when(k > 0)
        def _():
            acc_ref[...] += part

        @pl.when(k == nk - 1)
        def _():
            for p, o_ref in enumerate(o_refs):
                o_ref[...] = acc_ref[p * mp:(p + 1) * mp, :].astype(BF16)

    out_spec = pl.BlockSpec((None, mp, n), lambda g, k: (g, 0, 0))
    res = _call(body, name=name, grid=(groups, nk), in_specs=[a_spec, b_spec], out_specs=[out_spec] * parts,
                out_shape=[_sds((groups, mp, n), BF16)] * parts, scratch=[pltpu.VMEM((m, n), F32)],
                sem=('parallel', 'arbitrary'), vmem=VMEM_BIG, ride=ride)(a, b)
    if parts > 1:
        return res
    return res[0] if ride is None else (res[0][0], res[1])


def _grad_w_in(h1, dproj, tk, ride):
    T = h1.shape[0]
    nk = T // tk
    half = D_IN_PROJ // 2

    def body(a_ref, b_ref, o_ref, acc_ref):
        k = pl.program_id(0)
        for h in range(2):
            cols = slice(h * half, (h + 1) * half)
            part = _dot_tn(a_ref[...], b_ref[:, cols])

            @pl.when(k == 0)
            def _():
                acc_ref[:, cols] = part

            @pl.when(k > 0)
            def _():
                acc_ref[:, cols] += part

        @pl.when(k == nk - 1)
        def _():
            for g in range(N_DEV):
                o_ref[g] = acc_ref[:, g * IN_SHARD:(g + 1) * IN_SHARD].astype(BF16)

    return _call(body, name='grad_w_in', grid=(nk,),
                 in_specs=[pl.BlockSpec((tk, D_MODEL), lambda k: (k, 0)), pl.BlockSpec((tk, D_IN_PROJ), lambda k: (k, 0))],
                 out_specs=_const((N_DEV, D_MODEL, IN_SHARD)), out_shape=_sds((N_DEV, D_MODEL, IN_SHARD), BF16),
                 scratch=[pltpu.VMEM((D_MODEL, D_IN_PROJ), F32)], sem=('arbitrary',), vmem=VMEM_BIG, ride=ride)(h1, dproj)


def _pre_norm_bwd(dz, dz_spec, w_s, xin, dres, sc, g, tm, name, ride, below=None, group=1, w_t=False):
    T = xin.shape[0]
    n = w_s.shape[1] if w_t else w_s.shape[2]
    mul = _dot if w_t else _dot_nt
    steps = N_DEV // group

    def body(dz_ref, w_ref, x_ref, dr_ref, sc_ref, g_ref, *refs):
        if below is None:
            dx_ref, dsh_ref, dsc_ref, dg_ref = refs
            sums = (dsh_ref, dsc_ref, dg_ref)
        else:
            v_ref, gate_ref, g2_ref, dx_ref, dsh_ref, dsc_ref, dg_ref, dv_ref, dgate_ref, dg2_ref = refs
            sums = (dsh_ref, dsc_ref, dg_ref, dgate_ref, dg2_ref)
        i, j = pl.program_id(0), pl.program_id(1)
        piece = (lambda s: dz_ref[s]) if dz.ndim == 3 else (lambda s: dz_ref[:, s * n:(s + 1) * n])
        part = mul(piece(0), w_ref[0])
        for s in range(1, group):
            part = part + mul(piece(s), w_ref[s])

        @pl.when(jnp.logical_and(i == 0, j == 0))
        def _():
            for s_ref in sums:
                s_ref[...] = jnp.zeros_like(s_ref)

        @pl.when(j == 0)
        def _():
            dx_ref[...] = part

        @pl.when(j > 0)
        def _():
            dx_ref[...] += part

        @pl.when(j == steps - 1)
        def _():
            dh, xv, gv = dx_ref[...], x_ref[...], g_ref[...]
            r = _rsqrt_mean(xv)
            dsh_ref[...] += _colsum(dh)
            dsc_ref[...] += _colsum(dh * (xv * r * gv))
            dxn = dh * (1.0 + sc_ref[...])
            dg_ref[...] += _colsum(dxn * xv * r)
            dx = dr_ref[...] + _norm_bwd(dxn, xv, r, gv)
            dx_ref[...] = dx
            if below is not None:
                v, g2 = v_ref[...], g2_ref[...]
                rv = _rsqrt_mean(v)
                dgate_ref[...] += _colsum(dx * (v * rv * g2))
                dn = dx * gate_ref[...]
                dg2_ref[...] += _colsum(dn * v * rv)
                dv_ref[...] = _norm_bwd(dn, v, rv, g2).astype(BF16)

    row = pl.BlockSpec((tm, D_MODEL), lambda i, j: (i, 0))
    vec = _const((1, D_MODEL))
    in_specs = [dz_spec, pl.BlockSpec((group,) + w_s.shape[1:], lambda i, j: (j, 0, 0)), row, row, vec, vec]
    out_specs = [row, vec, vec, vec]
    out_shape = [_sds((T, D_MODEL)), _sds((1, D_MODEL)), _sds((1, D_MODEL)), _sds((1, D_MODEL))]
    args = [dz, w_s, xin, dres, sc, g]
    if below is not None:
        in_specs += [row, vec, vec]
        out_specs += [row, vec, vec]
        out_shape += [_sds((T, D_MODEL), BF16), _sds((1, D_MODEL)), _sds((1, D_MODEL))]
        args += list(below)
    return _call(body, name=name, grid=(T // tm, steps), in_specs=in_specs, out_specs=out_specs,
                 out_shape=out_shape, sem=('arbitrary', 'arbitrary'), vmem=VMEM_MOST, ride=ride)(*args)


def _mix_bwd(d_o, w_out, yssm, proj, d, glu_w, glu_b, g_ssm, cw, g_conv, avg16, avg64, tm, ride):
    T = yssm.shape[0]
    hb = _halo_before(tm)

    def body(do_ref, wo_ref, y_ref, p_ref, ph_ref, d_ref, gw_ref, gb_ref, gs_ref, cw_ref, gc_ref, a16_ref, a64_ref,
             dy_ref, dconv_ref, dbg_ref, z_ref, dlin_ref, acc_ref):
        i = pl.program_id(0)
        dyc = _dot_nt(do_ref[...], wo_ref[...])

        @pl.when(i == 0)
        def _():
            acc_ref[...] = jnp.zeros_like(acc_ref)

        u = p_ref[:, 0:D_SSM]
        y = y_ref[...] + d_ref[...] * u
        z, t = _gelu(y)
        gate = _sigmoid(_dot(z.astype(BF16), gw_ref[...]) + gb_ref[...])
        ya = z * gate
        rs = lax.rsqrt(_dot_split(ya * ya, a16_ref[...], 2) + EPS)
        dna = dyc[:, 0:D_SSM]
        acc_ref[1:2, :] += _colsum(dna * ya * rs)
        dya = _head_norm_bwd(dna, ya, rs, gs_ref[...], a16_ref[...])
        dlin = dya * z * gate * (1.0 - gate)
        acc_ref[0:1, :] += _colsum(dlin)
        dlin_b = dlin.astype(BF16)
        dz = dya * gate + _dot_nt(dlin_b, gw_ref[...])
        dy = dz * _gelu_grad(y, t)
        acc_ref[3:4, :] += _colsum(dy * u)
        dy_ref[...] = dy
        z_ref[...] = z.astype(BF16)
        dlin_ref[...] = dlin_b

        bg = p_ref[:, D_SSM:D_SSM + D_CONV]
        cv = p_ref[:, D_SSM + D_CONV:D_SSM + 2 * D_CONV] * p_ref[:, D_SSM + 2 * D_CONV:D_IN_PROJ]
        hv = ph_ref[:, D_SSM + D_CONV:D_SSM + 2 * D_CONV] * ph_ref[:, D_SSM + 2 * D_CONV:D_IN_PROJ]
        hv = jnp.where(i > 0, hv, 0.0)
        conv, cv1, cv2 = _conv3(cv, hv, cw_ref)
        yb = bg * conv
        rsb = lax.rsqrt(_dot_split(yb * yb, a64_ref[...], 2) + EPS)
        dnb = dyc[:, D_SSM:D_MODEL]
        acc_ref[2:3, :] += _colsum(dnb * yb * rsb)
        dyb = _head_norm_bwd(dnb, yb, rsb, gc_ref[...], a64_ref[...])
        dbg_ref[...] = dyb * conv
        dconv = dyb * bg
        dconv_ref[...] = dconv
        acc_ref[4:5, :] += _colsum(dconv * cv2)
        acc_ref[5:6, :] += _colsum(dconv * cv1)
        acc_ref[6:7, :] += _colsum(dconv * cv)

    vec = _const((1, D_SSM))
    sq = _const((D_SSM, D_SSM))
    half = pl.BlockSpec((tm, D_SSM), lambda i: (i, 0))
    return _call(body, name='mix_bwd', grid=(T // tm,),
                 in_specs=[pl.BlockSpec((tm, D_MODEL), lambda i: (i, 0)), _const((D_MODEL, D_MODEL)), half,
                           pl.BlockSpec((tm, D_IN_PROJ), lambda i: (i, 0)),
                           pl.BlockSpec((HALO, D_IN_PROJ), lambda i: (hb(i), 0)), vec, sq, vec, vec,
                           _const((3, D_CONV)), vec, sq, sq],
                 out_specs=[half, half, half, half, half, _const((8, D_SSM))],
                 out_shape=[_sds((T, D_SSM)), _sds((T, D_SSM)), _sds((T, D_SSM)), _sds((T, D_SSM), BF16),
                            _sds((T, D_SSM), BF16), _sds((8, D_SSM))],
                 sem=('arbitrary',), vmem=VMEM_BIG, ride=ride)(d_o, w_out, yssm, proj, proj, d, glu_w, glu_b, g_ssm, cw,
                                                              g_conv, avg16, avg64)


def _mix_bwd_proj(dconv, proj, du_ssm, dy, d, dbg, cw, tm):
    T = dy.shape[0]
    nb = T // tm
    ha = _halo_after(tm, T)

    def body(dc_ref, dch_ref, cg_ref, v_ref, du_ref, dy_ref, d_ref, dbg_ref, cw_ref, o_ref):
        i = pl.program_id(0)
        dcv = _conv3_t(dc_ref[...], jnp.where(i < nb - 1, dch_ref[...], 0.0), cw_ref)[0]
        o_ref[:, 0:D_SSM] = (du_ref[...] + dy_ref[...] * d_ref[...]).astype(BF16)
        o_ref[:, D_SSM:D_SSM + D_CONV] = dbg_ref[...].astype(BF16)
        o_ref[:, D_SSM + D_CONV:D_SSM + 2 * D_CONV] = (dcv * v_ref[...]).astype(BF16)
        o_ref[:, D_SSM + 2 * D_CONV:D_IN_PROJ] = (dcv * cg_ref[...]).astype(BF16)

    half = pl.BlockSpec((tm, D_SSM), lambda i: (i, 0))
    return _call(body, name='mix_bwd_proj', grid=(nb,),
                 in_specs=[half, pl.BlockSpec((HALO, D_CONV), lambda i: (ha(i), 0)),
                           pl.BlockSpec((tm, D_CONV), lambda i: (i, 2)), pl.BlockSpec((tm, D_CONV), lambda i: (i, 3)),
                           half, half, _const((1, D_SSM)), half, _const((3, D_CONV))],
                 out_specs=pl.BlockSpec((tm, D_IN_PROJ), lambda i: (i, 0)), out_shape=_sds((T, D_IN_PROJ), BF16),
                 sem=('parallel',), vmem=VMEM_BIG)(dconv, dconv, proj, proj, du_ssm, dy, d, dbg, cw)


ADAMW_SLOT_BYTES = 8 << 20
ADAMW_ROW_BYTES = 3 << 19


def _row_tile(rows, cols, slots):
    for cand in range(rows, 15, -1):
        if (rows % cand == 0 and cand % 16 == 0 and slots * cand * cols * 4 <= ADAMW_SLOT_BYTES
                and cand * cols * 4 <= ADAMW_ROW_BYTES):
            return cand
    return rows


def _adamw_math(g, w, m, v):
    m2 = ADAM_B1 * m + (1.0 - ADAM_B1) * g
    v2 = ADAM_B2 * v + (1.0 - ADAM_B2) * (g * g)
    m_hat = m2 / (1.0 - ADAM_B1 ** ADAM_STEP)
    v_hat = v2 / (1.0 - ADAM_B2 ** ADAM_STEP)
    return -ADAM_LR * (m_hat / (jnp.sqrt(v_hat) + ADAM_EPS) + ADAM_WD * w), m2, v2


def _adamw(pieces, w, m, v, name):
    slots, _, cols = pieces[0].shape
    rows = sum(p.shape[1] for p in pieces)
    tr = _row_tile(pieces[0].shape[1], cols, slots)
    starts, pos = [], 0
    for p in pieces:
        assert p.shape[1] % tr == 0
        starts.append(pos)
        pos += p.shape[1] // tr

    def body(*refs):
        g_refs = refs[:len(pieces)]
        w_ref, m_ref, v_ref, go_ref, d_ref, mo_ref, vo_ref = refs[len(pieces):]
        i = pl.program_id(0)
        g = None
        for g_ref, start in zip(g_refs, starts):
            part = g_ref[0].astype(F32)
            for s in range(1, slots):
                part = part + g_ref[s].astype(F32)
            g = part if g is None else jnp.where(i >= start, part, g)
        go_ref[...] = g
        d_ref[...], mo_ref[...], vo_ref[...] = _adamw_math(g, w_ref[...], m_ref[...], v_ref[...])

    def piece_spec(start, count):
        return pl.BlockSpec((slots, tr, cols), lambda i: (0, jnp.clip(i - start, 0, count - 1), 0))

    blk = pl.BlockSpec((tr, cols), lambda i: (i, 0))
    return _call(body, name=name, grid=(rows // tr,),
                 in_specs=[piece_spec(s, p.shape[1] // tr) for s, p in zip(starts, pieces)] + [blk, blk, blk],
                 out_specs=[blk] * 4, out_shape=[_sds((rows, cols))] * 4, sem=('parallel',),
                 vmem=VMEM_BIG)(*pieces, w, m, v)


def _to_scan_rows(a):
    T, n = a.shape
    return a.reshape(SUBLANES, T // SUBLANES, n).transpose(1, 0, 2).reshape(T, n)


def _from_scan_rows(a):
    T, n = a.shape
    return a.reshape(T // SUBLANES, SUBLANES, n).transpose(1, 0, 2).reshape(T, n)


def _expand(a):
    return jnp.repeat(a, SSM_GROUP, axis=1)


def _block_diag(rows, row_group, col_group):
    r, n = rows.shape
    tiled = jnp.tile(rows, (1, N_GROUPS))
    keep = (jnp.arange(r)[:, None] // row_group) == (jnp.arange(n * N_GROUPS)[None, :] // col_group)
    return jnp.where(keep, tiled, 0.0)


def _block_diag_b(bb):
    return _block_diag(bb.transpose(0, 2, 1).reshape(D_SSM, SSM_STATE), SSM_GROUP, SSM_STATE)


def _block_diag_c(cc):
    return _block_diag(cc.transpose(0, 2, 1).reshape(N_STATE, SSM_GROUP), SSM_STATE, SSM_GROUP)


def _diag_blocks(x, chan_major):
    per = CHAN_BLOCK // SSM_GROUP
    eye = jnp.eye(per, dtype=x.dtype)
    if chan_major:
        x = x.reshape(-1, per, SSM_GROUP, per, SSM_STATE) * eye[None, :, None, :, None]
        return x.sum(axis=1).transpose(0, 2, 3, 1).reshape(N_GROUPS, SSM_STATE, SSM_GROUP)
    x = x.reshape(-1, per, SSM_STATE, per, SSM_GROUP) * eye[None, :, None, :, None]
    return x.sum(axis=3).reshape(N_GROUPS, SSM_STATE, SSM_GROUP)


SMALL_LAYOUT = {
    'ssm_b_re': (0, 0, 32, 1024), 'ssm_b_im': (32, 0, 32, 1024), 'ssm_c_re': (64, 0, 32, 1024),
    'ssm_c_im': (96, 0, 32, 1024), 'b_ada': (128, 0, 6, 1024), 'g_pre_mix': (134, 0, 1, 1024),
    'g_post_mix': (135, 0, 1, 1024), 'ssm_lam_re': (136, 0, 2, 1024), 'ssm_lam_im': (138, 0, 2, 1024),
    'ssm_log_step': (140, 0, 1, 32), 'glu_b': (141, 0, 1, 512), 'g_out_ssm': (141, 512, 1, 512),
    'g_out_conv': (142, 0, 1, 512), 'ssm_d': (142, 512, 1, 512), 'g_pre_ffn': (143, 0, 1, 1024),
    'g_post_ffn': (144, 0, 1, 1024)}
SMALL_ROWS = 152
B_ADA_ROW = SMALL_LAYOUT['b_ada'][0]
LATE_ROWS = {('b_ada', 0): 0, ('b_ada', 1): 1, ('g_pre_mix', 0): 2}


def _adamw_small(gathered, late, wts, mom_m, mom_v):
    names = list(SMALL_LAYOUT)
    n = len(names)

    def body(*refs):
        g_ref, late_ref, ins, outs = refs[0], refs[1], refs[2:2 + 3 * n], refs[2 + 3 * n:]
        for p, name in enumerate(names):
            r0, c0, rows, cols = SMALL_LAYOUT[name]
            pieces = [(0, rows)] if rows % 8 == 0 else [(r, 1) for r in range(rows)]
            for r, cnt in pieces:
                src_ref, first = (late_ref, LATE_ROWS[name, r]) if (name, r) in LATE_ROWS else (g_ref, r0 + r)
                g = src_ref[0, first:first + cnt, c0:c0 + cols]
                for s in range(1, N_DEV):
                    g = g + src_ref[s, first:first + cnt, c0:c0 + cols]
                w, m, v = (ins[3 * p + q][r:r + cnt, :] for q in range(3))
                res = (g,) + _adamw_math(g, w, m, v)
                for q in range(4):
                    outs[4 * p + q][r:r + cnt, :] = res[q]

    shapes = [SMALL_LAYOUT[name][2:] for name in names]
    args = [gathered, late]
    for name, shp in zip(names, shapes):
        args += [wts[name].reshape(shp), mom_m[name].reshape(shp), mom_v[name].reshape(shp)]
    outs = _call(body, name='adamw_small', grid=(1,),
                 in_specs=[_const(gathered.shape), _const(late.shape)]
                 + [_const(shp) for shp in shapes for _ in range(3)],
                 out_specs=[_const(shp) for shp in shapes for _ in range(4)],
                 out_shape=[_sds(shp) for shp in shapes for _ in range(4)], vmem=VMEM_BIG)(*args)
    res = {}
    for p, name in enumerate(names):
        for q, kind in enumerate(('g', 'd', 'm', 'v')):
            res[kind, name] = outs[4 * p + q].reshape(wts[name].shape)
    return res


def kernel(x, c, w_ada, b_ada, g_pre_mix, g_post_mix, w_in, ssm_lam_re, ssm_lam_im, ssm_log_step, ssm_b_re, ssm_b_im, ssm_c_re, ssm_c_im, ssm_d, glu_w, glu_b, g_out_ssm, conv_w, g_out_conv, w_out, g_pre_ffn, g_post_ffn, w_up, ffn_conv_w, w_down, loss_target, m_w_ada, m_b_ada, m_g_pre_mix, m_g_post_mix, m_w_in, m_ssm_lam_re, m_ssm_lam_im, m_ssm_log_step, m_ssm_b_re, m_ssm_b_im, m_ssm_c_re, m_ssm_c_im, m_ssm_d, m_glu_w, m_glu_b, m_g_out_ssm, m_conv_w, m_g_out_conv, m_w_out, m_g_pre_ffn, m_g_post_ffn, m_w_up, m_ffn_conv_w, m_w_down, v_w_ada, v_b_ada, v_g_pre_mix, v_g_post_mix, v_w_in, v_ssm_lam_re, v_ssm_lam_im, v_ssm_log_step, v_ssm_b_re, v_ssm_b_im, v_ssm_c_re, v_ssm_c_im, v_ssm_d, v_glu_w, v_glu_b, v_g_out_ssm, v_conv_w, v_g_out_conv, v_w_out, v_g_pre_ffn, v_g_post_ffn, v_w_up, v_ffn_conv_w, v_w_down):
    args = dict(locals())
    wts = {n: args[n] for n in WEIGHTS}
    mom_m = {n: args['m_' + n] for n in WEIGHTS}
    mom_v = {n: args['v_' + n] for n in WEIGHTS}
    T = x.shape[1]
    tm = min(512, T)
    tw = min(1024, T)
    tk = min(2048, T)
    me = _me()[3]
    xt, tgt = x[0], loss_target[0]

    c_all, w_in_s = _exchange([c, w_in[0].astype(BF16)], name='gather_first', scatter=False)
    c_all = c_all.reshape(N_DEV, D_MODEL)
    b_cols = lax.dynamic_slice(b_ada, (0, me * ADA_SHARD), (1, ADA_SHARD))
    mod_cols, c_act = _mod_cols(c_all, w_ada[0], b_cols)
    (mod_all,) = _exchange([mod_cols], name='gather_mod', scatter=False)
    mod = lax.dynamic_slice(mod_all, (0, me, 0), (N_DEV, 1, ADA_SHARD)).reshape(N_MOD, 1, D_MODEL)
    sh1, sc1, gt1, sh2, sc2, gt2 = [mod[k] for k in range(N_MOD)]


    lre_x, lim_x = _expand(ssm_lam_re[0]), _expand(ssm_lam_im[0])
    lst_x = jnp.broadcast_to(ssm_log_step[0][:, None], (N_GROUPS, SSM_STATE * SSM_GROUP))
    b_re_x = ssm_b_re[0].reshape(N_GROUPS, -1)
    b_im_x = ssm_b_im[0].reshape(N_GROUPS, -1)
    ar_x, ai_x, bbr_x, bbi_x = _ssm_prep(lre_x, lim_x, lst_x, b_re_x, b_im_x)
    lam_r = ar_x[:, ::SSM_GROUP].reshape(1, N_STATE)
    lam_i = ai_x[:, ::SSM_GROUP].reshape(1, N_STATE)
    big_b_re = _block_diag_b(bbr_x.reshape(N_GROUPS, SSM_STATE, SSM_GROUP)).astype(BF16)
    big_b_im = _block_diag_b(bbi_x.reshape(N_GROUPS, SSM_STATE, SSM_GROUP)).astype(BF16)
    big_c_re = _block_diag_c(ssm_c_re[0]).astype(BF16)
    big_c_im = _block_diag_c(ssm_c_im[0]).astype(BF16)
    head = jnp.arange(D_SSM)
    avg16 = jnp.where(head[:, None] // SSM_GROUP == head[None, :] // SSM_GROUP, 1.0 / SSM_GROUP, 0.0).astype(BF16)
    hd = D_CONV // CONV_HEADS
    avg64 = jnp.where(head[:, None] // hd == head[None, :] // hd, 1.0 / hd, 0.0).astype(BF16)

    (proj, h1), (w_down_s, ffn_conv_s, glu_s, w_out_s, conv_s) = _pre_mix(
        xt, sc1, sh1, g_pre_mix, w_in_s, tw,
        ([w_down[0].astype(BF16), ffn_conv_w[0], glu_w[0].astype(BF16), w_out[0].astype(BF16), conv_w[0]], False))
    glu_full = glu_s.reshape(D_SSM, D_SSM)
    w_out_full = w_out_s.reshape(D_MODEL, D_MODEL)
    cw_full = conv_s.transpose(1, 0, 2).reshape(3, D_CONV)
    wd4 = w_down_s.reshape(4, FF_SHARD, D_MODEL)
    u_perm = _to_scan_rows(proj[:, :D_SSM])
    (s_re, s_im, y_perm), (w_up_s,) = _ssm_fwd(u_perm, big_b_re, big_b_im, big_c_re, big_c_im, lam_r, lam_i,
                                               ([w_up[0].T.astype(BF16)], False))
    yssm = _from_scan_rows(y_perm)
    mix_args = (ssm_d, glu_full, glu_b, g_out_ssm, cw_full, g_out_conv, avg16, avg64)
    ycat = _mix_fwd(yssm, proj, *mix_args, tw)
    o, x1, h2 = _out_proj(ycat, w_out_full, xt, gt1, g_post_mix, g_pre_ffn, sc2, sh2, tw)
    up8, hid8 = _ffn_up(h2, w_up_s, ffn_conv_s, tw)
    hid4 = hid8.reshape(2, 4, T, FF_SHARD)
    ddn, dx2, loss_parts, d_gt2, d_g_post_ffn = _ffn_down(hid4, wd4, x1, tgt, gt2, g_post_ffn, tm)
    loss_local = jnp.sum(loss_parts[:, 0, 0])

    got = {}
    dhid, g_w_down = _ffn_dact(ddn, wd4, hid4, tw)
    (dup8, dcw_ffn), (got['w_down'],) = _ffn_dup(dhid.reshape(N_DEV, T, FF_SHARD), up8, ffn_conv_s, tw,
                                                 ([g_w_down.reshape(N_DEV, D_FF // N_DEV, D_MODEL)], True))
    g_w_up_halves = _grad_tn(dup8, h2, pl.BlockSpec((None, tk, FF_SHARD), lambda g, k: (g, k, 0)),
                             pl.BlockSpec((tk, D_MODEL), lambda g, k: (k, 0)), N_DEV, FF_SHARD, D_MODEL, tk,
                             'grad_w_up', parts=2)
    (dx1, d_sh2, d_sc2, d_g_pre_ffn, d_o, d_gt1, d_g_post_mix), (got_up_0, got['ffn_conv_w']) = _pre_norm_bwd(
        dup8, pl.BlockSpec((2, tw, FF_SHARD), lambda i, j: (j, i, 0)), w_up_s, x1, dx2, sc2, g_pre_ffn, tw,
        'ffn_in_bwd', ([g_w_up_halves[0], dcw_ffn], True), below=(o, gt1, g_post_mix), group=2, w_t=True)

    g_w_out = _grad_tn(ycat, d_o, pl.BlockSpec((tk, D_MODEL), lambda g, k: (k, 0)),
                       pl.BlockSpec((tk, D_MODEL), lambda g, k: (k, 0)), 1, D_MODEL, D_MODEL, tk, 'grad_w_out')
    (dy, dconv, dbg, z_b, dlin_b, sums), (got['w_out'],) = _mix_bwd(
        d_o, w_out_full, yssm, proj, *mix_args, tm, ([g_w_out.reshape(N_DEV, D_MODEL // N_DEV, D_MODEL)], True))
    g_glu_w = _grad_tn(z_b, dlin_b, pl.BlockSpec((tk, D_SSM), lambda g, k: (k, 0)),
                       pl.BlockSpec((tk, D_SSM), lambda g, k: (k, 0)), 1, D_SSM, D_SSM, tk, 'grad_glu_w')
    dy_perm = _to_scan_rows(dy)
    (du_perm, dbr_blk, dbi_blk, dcr_blk, dci_blk, dar_blk, dai_blk), (got_up_1, got['glu_w']) = _ssm_bwd(
        dy_perm, u_perm, s_re, s_im, big_b_re, big_b_im, big_c_re, big_c_im, lam_r, lam_i,
        ([g_w_up_halves[1], g_glu_w.reshape(N_DEV, D_SSM // N_DEV, D_SSM)], True))
    du_ssm = _from_scan_rows(du_perm)
    dproj = _mix_bwd_proj(dconv, proj, du_ssm, dy, ssm_d, dbg, cw_full, tw)
    dbb_re = _diag_blocks(dbr_blk, True).reshape(N_GROUPS, -1)
    dbb_im = _diag_blocks(dbi_blk, True).reshape(N_GROUPS, -1)
    d_c_re = _diag_blocks(dcr_blk, False).transpose(0, 2, 1)
    d_c_im = _diag_blocks(dci_blk, False).transpose(0, 2, 1)
    lane = jnp.arange(SSM_STATE * SSM_GROUP)
    seg = jnp.where(lane[:, None] // SSM_GROUP == lane[None, :] // SSM_GROUP, 1.0, 0.0).astype(BF16)
    d_b_re_x, d_b_im_x, d_lre_x, d_lim_x, d_lst = _ssm_prep_bwd(
        lre_x, lim_x, lst_x, b_re_x, b_im_x, dbb_re, dbb_im, _expand(dar_blk.reshape(N_GROUPS, SSM_STATE)),
        _expand(dai_blk.reshape(N_GROUPS, SSM_STATE)), seg)

    row = lambda a: a.reshape(-1, PACK_COLS)
    blank = jnp.zeros((1, PACK_COLS), F32)
    small_pack = jnp.concatenate([
        d_b_re_x, d_b_im_x, row(d_c_re), row(d_c_im), blank, blank, d_gt1, d_sh2, d_sc2, d_gt2, blank,
        d_g_post_mix, row(d_lre_x[:, ::SSM_GROUP]), row(d_lim_x[:, ::SSM_GROUP]),
        jnp.pad(d_lst.reshape(1, N_GROUPS), ((0, 0), (0, PACK_COLS - N_GROUPS))), row(sums[0:4]), d_g_pre_ffn,
        d_g_post_ffn, jnp.zeros((SMALL_ROWS - 145, PACK_COLS), F32)])
    g_w_in, (small_all,) = _grad_w_in(h1, dproj, tk, ([small_pack], False))
    g_conv_slots = jnp.concatenate([sums[4:7], jnp.zeros((5, D_CONV), F32)]).reshape(
        8, N_DEV, D_CONV // N_DEV).transpose(1, 0, 2)
    (grad_x, d_sh1, d_sc1, d_g_pre_mix), (got['w_in'], got['conv_w']) = _pre_norm_bwd(
        dproj, pl.BlockSpec((tw, D_IN_PROJ), lambda i, j: (i, j)), w_in_s, xt, dx1, sc1, g_pre_mix, tw,
        'mix_in_bwd', ([g_w_in, g_conv_slots], True), group=N_DEV)
    late_pack = jnp.concatenate([d_sh1, d_sc1, d_g_pre_mix, jnp.full((1, PACK_COLS), loss_local, F32),
                                 jnp.zeros((4, PACK_COLS), F32)])
    (late_all,) = _exchange([late_pack], name='gather_late_grads', scatter=False)
    loss = jnp.sum(late_all[:, 3, 0])
    res = _adamw_small(small_all, late_all, wts, mom_m, mom_v)

    dmod_all = jnp.concatenate([late_all[:, 0:2, :], small_all[:, B_ADA_ROW + 2:B_ADA_ROW + N_MOD, :]],
                               axis=1).reshape(N_DEV, N_MOD * D_MODEL)
    dmod_cols = lax.dynamic_slice(dmod_all, (0, me * ADA_SHARD), (N_DEV, ADA_SHARD))
    g_w_ada = _grad_w_ada(c_act.T, dmod_cols)

    pieces = {n: [slots[:, :3, :] if n in ('conv_w', 'ffn_conv_w') else slots] for n, slots in got.items()}
    for n, parts in pieces.items():
        outs = _adamw(parts, wts[n][0], mom_m[n][0], mom_v[n][0], 'adamw_' + n)
        for kind, val in zip(('g', 'd', 'm', 'v'), outs):
            res[kind, n] = val[None]
    outs = _adamw([got_up_0, got_up_1], w_up[0].T, m_w_up[0].T, v_w_up[0].T, 'adamw_w_up')
    for kind, val in zip(('g', 'd', 'm', 'v'), outs):
        res[kind, 'w_up'] = val.T[None]
    outs = _adamw([g_w_ada[None]], w_ada[0], m_w_ada[0], v_w_ada[0], 'adamw_w_ada')
    for kind, val in zip(('g', 'd', 'm', 'v'), outs):
        res[kind, 'w_ada'] = val[None]

    return (loss, grad_x[None], *[res['g', n] for n in WEIGHTS], *[res['d', n] for n in WEIGHTS],
            *[res['m', n] for n in WEIGHTS], *[res['v', n] for n in WEIGHTS])
```

```python
import math

import jax
import jax.numpy as jnp
from jax import lax
from jax.experimental import pallas as pl
from jax.experimental.pallas import tpu as pltpu

F32, BF16 = jnp.float32, jnp.bfloat16

D_MODEL = 1024
D_SSM = 512
D_CONV = 512
SSM_GROUP = 16
N_GROUPS = 32
SSM_STATE = 64
N_STATE = N_GROUPS * SSM_STATE
CONV_HEADS = 8
D_FF = 2816
N_MOD = 6
D_IN_PROJ = D_SSM + 3 * D_CONV
N_DEV = 8
FF_SHARD = 2 * D_FF // N_DEV
IN_SHARD = D_IN_PROJ // N_DEV
ADA_SHARD = N_MOD * D_MODEL // N_DEV
EPS = 1e-6
LAMBDA_RE_MAX = -1e-4
ADAM_LR, ADAM_B1, ADAM_B2, ADAM_EPS, ADAM_WD, ADAM_STEP = 0.001, 0.9, 0.999, 1e-08, 0.01, 10
GELU_C = math.sqrt(2.0 / math.pi)
GELU_A = 0.044715

SUBLANES = 8
HALO = 8
HALO16 = 16
SCAN_UNROLL = 8
STATE_BLOCK = 512
CHAN_BLOCK = 128
VMEM_BIG = 48 << 20
VMEM_MOST = 58 << 20

WEIGHTS = ['w_ada', 'b_ada', 'g_pre_mix', 'g_post_mix', 'w_in', 'ssm_lam_re', 'ssm_lam_im', 'ssm_log_step',
           'ssm_b_re', 'ssm_b_im', 'ssm_c_re', 'ssm_c_im', 'ssm_d', 'glu_w', 'glu_b', 'g_out_ssm', 'conv_w',
           'g_out_conv', 'w_out', 'g_pre_ffn', 'g_post_ffn', 'w_up', 'ffn_conv_w', 'w_down']
SHARDED = ('w_ada', 'w_in', 'glu_w', 'conv_w', 'w_out', 'w_up', 'ffn_conv_w', 'w_down')
PACK_COLS = 1024


def _call(body, *, name, grid, in_specs, out_specs, out_shape, scratch=(), sem=None, vmem=None, ride=None):
    params = {}
    if vmem is not None:
        params['vmem_limit_bytes'] = vmem
    if ride is None:
        if sem is not None:
            params['dimension_semantics'] = sem
        return pl.pallas_call(body, name=name, grid=grid, in_specs=in_specs, out_specs=out_specs,
                              out_shape=out_shape, scratch_shapes=list(scratch),
                              compiler_params=pltpu.CompilerParams(**params))
    arrs, scatter = ride
    single = not isinstance(out_shape, (list, tuple))
    out_shape_l = [out_shape] if single else list(out_shape)
    out_specs_l = [out_specs] if single else list(out_specs)
    n, n_in, n_out, n_scr = len(arrs), len(in_specs), len(out_shape_l), len(scratch)
    any_spec = pl.BlockSpec(memory_space=pl.ANY)
    params['dimension_semantics'] = ('arbitrary',) * len(grid)

    def carried(*refs):
        ins, rin = refs[:n_in], refs[n_in:n_in + n]
        outs, rout = refs[n_in + n:n_in + n + n_out], refs[n_in + n + n_out:n_in + 2 * n + n_out]
        scr, sems = refs[n_in + 2 * n + n_out:n_in + 2 * n + n_out + n_scr], refs[n_in + 2 * n + n_out + n_scr:]
        first = pl.program_id(0) == 0
        last = pl.program_id(0) == grid[0] - 1
        for ax in range(1, len(grid)):
            first = jnp.logical_and(first, pl.program_id(ax) == 0)
            last = jnp.logical_and(last, pl.program_id(ax) == grid[ax] - 1)

        @pl.when(first)
        def _():
            _exchange_start(rin, rout, sems, scatter)

        body(*ins, *outs, *scr)

        @pl.when(last)
        def _():
            _exchange_wait(rin, rout, sems, scatter)

    call = pl.pallas_call(carried, name=name, grid=grid, in_specs=list(in_specs) + [any_spec] * n,
                          out_specs=out_specs_l + [any_spec] * n,
                          out_shape=out_shape_l + _exchange_shapes(arrs, scatter),
                          scratch_shapes=list(scratch) + _exchange_sems(n),
                          compiler_params=pltpu.CompilerParams(**params))

    def run(*args):
        res = call(*args, *arrs)
        own = res[0] if single else list(res[:n_out])
        return own, list(res[n_out:])

    return run


def _const(shape):
    nd = len(shape)
    return pl.BlockSpec(shape, lambda *_: (0,) * nd)


def _sds(shape, dtype=F32):
    return jax.ShapeDtypeStruct(shape, dtype)


def _dot(a, b):
    return jnp.dot(a, b, preferred_element_type=F32)


def _dot_nt(a, b):
    return lax.dot_general(a, b, (((1,), (1,)), ((), ())), preferred_element_type=F32)


def _dot_tn(a, b):
    return lax.dot_general(a, b, (((0,), (0,)), ((), ())), preferred_element_type=F32)


def _dot_split(x, mat, parts):
    acc = None
    rem = x
    for _ in range(parts):
        piece = rem.astype(BF16)
        rem = rem - piece.astype(F32)
        term = _dot(piece, mat)
        acc = term if acc is None else acc + term
    return acc


def _sigmoid(x):
    return 1.0 / (1.0 + jnp.exp(-x))


def _gelu(x):
    t = jnp.tanh(GELU_C * (x + GELU_A * x * x * x))
    return 0.5 * x * (1.0 + t), t


def _gelu_grad(x, t):
    return 0.5 * (1.0 + t) + 0.5 * x * (1.0 - t * t) * GELU_C * (1.0 + 3.0 * GELU_A * x * x)


def _rsqrt_mean(x):
    return lax.rsqrt(jnp.mean(x * x, axis=-1, keepdims=True) + EPS)


def _colsum(x):
    return jnp.sum(x, axis=0, keepdims=True)


def _shifts_down(x, halo):
    ext = jnp.concatenate([halo, x], axis=0)
    return pltpu.roll(ext, 1, 0)[halo.shape[0]:], pltpu.roll(ext, 2, 0)[halo.shape[0]:]


def _shifts_up(x, halo):
    n = x.shape[0]
    ext = jnp.concatenate([x, halo], axis=0)
    total = ext.shape[0]
    return pltpu.roll(ext, total - 1, 0)[:n], pltpu.roll(ext, total - 2, 0)[:n]


def _conv3(x, halo, w_ref):
    x1, x2 = _shifts_down(x, halo)
    return w_ref[0:1, :] * x2 + w_ref[1:2, :] * x1 + w_ref[2:3, :] * x, x1, x2


def _conv3_t(g, halo, w_ref):
    g1, g2 = _shifts_up(g, halo)
    return w_ref[2:3, :] * g + w_ref[1:2, :] * g1 + w_ref[0:1, :] * g2, g1, g2


def _silu_parts(x):
    s = _sigmoid(x)
    return x * s, s * (1.0 + x * (1.0 - s))


def _norm_bwd(dn, x, r, g):
    gd = g * dn
    return r * gd - x * (r * r * r) * jnp.mean(gd * x, axis=-1, keepdims=True)


def _head_norm_bwd(dn, y, rs, g, avg):
    gd = g * dn
    return rs * gd - y * (rs * rs * rs) * _dot_split(gd * y, avg, 2)


def _me():
    x, y, c = lax.axis_index('x'), lax.axis_index('y'), lax.axis_index('c')
    return x, y, c, 4 * x + 2 * y + c


def _peer(k):
    x, y, c, _ = _me()
    px = 1 - x if k & 4 else x
    py = 1 - y if k & 2 else y
    pc = 1 - c if k & 1 else c
    return (px, py, pc), 4 * px + 2 * py + pc


SIBLING = 1
OTHER_CHIPS = (2, 4, 6)


def _remote(src, dst, sems, a, k, dev):
    return pltpu.make_async_remote_copy(src_ref=src, dst_ref=dst, send_sem=sems[0].at[a, k - 1],
                                        recv_sem=sems[1].at[a, k - 1], device_id=dev,
                                        device_id_type=pl.DeviceIdType.MESH)


def _exchange_copies(ins, outs, sems, scatter):
    me = _me()[3]
    local, first, relay, arrivals = [], [], [], []
    for a in range(len(ins)):
        src = ins[a].at[me] if scatter else ins[a]
        local.append(pltpu.make_async_copy(src, outs[a].at[me], sems[2].at[a]))
        for k in range(1, N_DEV):
            dev, idx = _peer(k)
            landed = _remote(src, outs[a].at[idx], sems, a, k, dev)
            if scatter:
                first.append(_remote(ins[a].at[idx], outs[a].at[me], sems, a, k, dev))
                arrivals.append(landed)
            elif k == SIBLING:
                first.append(_remote(src, outs[a].at[me], sems, a, k, dev))
                arrivals.append(landed)
            elif k in OTHER_CHIPS:
                first.append(_remote(src, outs[a].at[me], sems, a, k, dev))
                sib, _ = _peer(SIBLING)
                relay.append((landed, _remote(outs[a].at[idx], outs[a].at[idx], sems, a, k | SIBLING, sib)))
            else:
                arrivals.append(landed)
    return local, first, relay, arrivals


def _exchange_start(ins, outs, sems, scatter):
    local, first, _, _ = _exchange_copies(ins, outs, sems, scatter)
    for cp in local + first:
        cp.start()


def _exchange_wait(ins, outs, sems, scatter):
    local, first, relay, arrivals = _exchange_copies(ins, outs, sems, scatter)
    for landed, forward in relay:
        landed.wait_recv()
        forward.start()
    for cp in arrivals:
        cp.wait_recv()
    for cp in first + [forward for _, forward in relay]:
        cp.wait_send()
    for cp in local:
        cp.wait()


def _exchange_shapes(arrs, scatter):
    return [_sds(a.shape if scatter else (N_DEV,) + a.shape, a.dtype) for a in arrs]


def _exchange_sems(n):
    return [pltpu.SemaphoreType.DMA((n, N_DEV - 1)), pltpu.SemaphoreType.DMA((n, N_DEV - 1)),
            pltpu.SemaphoreType.DMA((n,))]


def _exchange(arrs, *, name, scatter):
    n = len(arrs)

    def body(*refs):
        _exchange_start(refs[:n], refs[n:2 * n], refs[2 * n:], scatter)
        _exchange_wait(refs[:n], refs[n:2 * n], refs[2 * n:], scatter)

    any_spec = pl.BlockSpec(memory_space=pl.ANY)
    outs = pl.pallas_call(body, name=name, out_shape=_exchange_shapes(arrs, scatter), in_specs=[any_spec] * n,
                          out_specs=[any_spec] * n, scratch_shapes=_exchange_sems(n))(*arrs)
    return list(outs)


def _mod_cols(c_all, w_ada, b_cols):
    def body(c_ref, w_ref, b_ref, mod_ref, act_ref):
        c = c_ref[...]
        act = c * _sigmoid(c)
        act_ref[...] = act
        mod_ref[...] = _dot(act.astype(BF16), w_ref[...].astype(BF16)) + b_ref[...]

    return _call(body, name='mod_cols', grid=(1,),
                 in_specs=[_const(c_all.shape), _const(w_ada.shape), _const(b_cols.shape)],
                 out_specs=[_const((N_DEV, ADA_SHARD)), _const(c_all.shape)],
                 out_shape=[_sds((N_DEV, ADA_SHARD)), _sds(c_all.shape)], vmem=VMEM_BIG)(c_all, w_ada, b_cols)


def _grad_w_ada(act_t, dmod_cols):
    def body(a_ref, d_ref, o_ref):
        o_ref[...] = _dot(a_ref[...], d_ref[...])

    return _call(body, name='grad_w_ada', grid=(1,), in_specs=[_const(act_t.shape), _const(dmod_cols.shape)],
                 out_specs=_const((D_MODEL, ADA_SHARD)), out_shape=_sds((D_MODEL, ADA_SHARD)),
                 vmem=VMEM_BIG)(act_t, dmod_cols)


def _pre_mix(x, sc, sh, g, w_s, tm, ride):
    T = x.shape[0]
    group = 4

    def body(x_ref, sc_ref, sh_ref, g_ref, w_ref, proj_ref, h_ref):
        @pl.when(pl.program_id(1) == 0)
        def _():
            xv = x_ref[...]
            h_ref[...] = ((xv * _rsqrt_mean(xv) * g_ref[...]) * (1.0 + sc_ref[...]) + sh_ref[...]).astype(BF16)

        for s in range(group):
            proj_ref[:, s * IN_SHARD:(s + 1) * IN_SHARD] = _dot(h_ref[...], w_ref[s])

    row = pl.BlockSpec((tm, D_MODEL), lambda i, j: (i, 0))
    vec = _const((1, D_MODEL))
    return _call(body, name='pre_mix', grid=(T // tm, N_DEV // group),
                 in_specs=[row, vec, vec, vec, pl.BlockSpec((group, D_MODEL, IN_SHARD), lambda i, j: (j, 0, 0))],
                 out_specs=[pl.BlockSpec((tm, group * IN_SHARD), lambda i, j: (i, j)), row],
                 out_shape=[_sds((T, D_IN_PROJ)), _sds((T, D_MODEL), BF16)],
                 sem=('parallel', 'arbitrary'), ride=ride)(x, sc, sh, g, w_s)


def _halo_before(tm, rows=HALO):
    return lambda i: jnp.maximum(i * (tm // rows) - 1, 0)


def _halo_after(tm, T, rows=HALO):
    return lambda i: jnp.minimum((i + 1) * (tm // rows), T // rows - 1)


def _mix_fwd(yssm, proj, d, glu_w, glu_b, g_ssm, cw, g_conv, avg16, avg64, tm):
    T = yssm.shape[0]
    hb = _halo_before(tm)

    def body(y_ref, p_ref, ph_ref, d_ref, gw_ref, gb_ref, gs_ref, cw_ref, gc_ref, a16_ref, a64_ref, o_ref):
        i = pl.program_id(0)
        u = p_ref[:, 0:D_SSM]
        y = y_ref[...] + d_ref[...] * u
        z, _ = _gelu(y)
        gate = _sigmoid(_dot(z.astype(BF16), gw_ref[...]) + gb_ref[...])
        ya = z * gate
        rs = lax.rsqrt(_dot_split(ya * ya, a16_ref[...], 2) + EPS)
        o_ref[:, 0:D_SSM] = (ya * rs * gs_ref[...]).astype(BF16)
        bg = p_ref[:, D_SSM:D_SSM + D_CONV]
        cv = p_ref[:, D_SSM + D_CONV:D_SSM + 2 * D_CONV] * p_ref[:, D_SSM + 2 * D_CONV:D_IN_PROJ]
        hv = ph_ref[:, D_SSM + D_CONV:D_SSM + 2 * D_CONV] * ph_ref[:, D_SSM + 2 * D_CONV:D_IN_PROJ]
        hv = jnp.where(i > 0, hv, 0.0)
        conv, _, _ = _conv3(cv, hv, cw_ref)
        yb = bg * conv
        rsb = lax.rsqrt(_dot_split(yb * yb, a64_ref[...], 2) + EPS)
        o_ref[:, D_SSM:D_MODEL] = (yb * rsb * gc_ref[...]).astype(BF16)

    vec = _const((1, D_SSM))
    sq = _const((D_SSM, D_SSM))
    return _call(body, name='mix_fwd', grid=(T // tm,),
                 in_specs=[pl.BlockSpec((tm, D_SSM), lambda i: (i, 0)), pl.BlockSpec((tm, D_IN_PROJ), lambda i: (i, 0)),
                           pl.BlockSpec((HALO, D_IN_PROJ), lambda i: (hb(i), 0)), vec, sq, vec, vec,
                           _const((3, D_CONV)), vec, sq, sq],
                 out_specs=pl.BlockSpec((tm, D_MODEL), lambda i: (i, 0)), out_shape=_sds((T, D_MODEL), BF16),
                 sem=('parallel',), vmem=VMEM_BIG)(yssm, proj, proj, d, glu_w, glu_b, g_ssm, cw, g_conv, avg16, avg64)


def _out_proj(ycat, w_out, x, gt, g_post, g_pre, sc, sh, tm):
    T = x.shape[0]

    def body(y_ref, w_ref, x_ref, gt_ref, gp_ref, g2_ref, sc_ref, sh_ref, o_ref, x1_ref, h_ref):
        o = _dot(y_ref[...], w_ref[...])
        o_ref[...] = o
        x1 = x_ref[...] + gt_ref[...] * (o * _rsqrt_mean(o) * gp_ref[...])
        x1_ref[...] = x1
        h_ref[...] = ((x1 * _rsqrt_mean(x1) * g2_ref[...]) * (1.0 + sc_ref[...]) + sh_ref[...]).astype(BF16)

    row = pl.BlockSpec((tm, D_MODEL), lambda i: (i, 0))
    vec = _const((1, D_MODEL))
    return _call(body, name='out_proj', grid=(T // tm,),
                 in_specs=[row, _const((D_MODEL, D_MODEL)), row, vec, vec, vec, vec, vec],
                 out_specs=[row, row, row],
                 out_shape=[_sds((T, D_MODEL)), _sds((T, D_MODEL)), _sds((T, D_MODEL), BF16)],
                 sem=('parallel',), vmem=VMEM_BIG)(ycat, w_out, x, gt, g_post, g_pre, sc, sh)


def _ffn_up(h2, w_s, cw8, tm):
    T = h2.shape[0]
    hb = _halo_before(tm, HALO16)

    def body(h_ref, hh_ref, w_ref, cw_ref, up_ref, hid_ref):
        for s in range(2):
            up = _dot_nt(h_ref[...], w_ref[s])
            up_ref[s] = up.astype(BF16)
            before = jnp.where(pl.program_id(0) > 0, _dot_nt(hh_ref[...], w_ref[s]), 0.0)
            hid_ref[s] = _conv3(up, before, cw_ref.at[s])[0].astype(BF16)

    out = pl.BlockSpec((2, tm, FF_SHARD), lambda i, j: (j, i, 0))
    return _call(body, name='ffn_up', grid=(T // tm, N_DEV // 2),
                 in_specs=[pl.BlockSpec((tm, D_MODEL), lambda i, j: (i, 0)),
                           pl.BlockSpec((HALO16, D_MODEL), lambda i, j: (hb(i), 0)),
                           pl.BlockSpec((2, FF_SHARD, D_MODEL), lambda i, j: (j, 0, 0)),
                           pl.BlockSpec((2, 3, FF_SHARD), lambda i, j: (j, 0, 0))],
                 out_specs=[out, out], out_shape=[_sds((N_DEV, T, FF_SHARD), BF16)] * 2,
                 sem=('parallel', 'parallel'), vmem=VMEM_BIG)(h2, h2, w_s, cw8)


def _ffn_down(hid4, wd4, x1, tgt, gt, g_post, tm):
    T = x1.shape[0]
    nb = T // tm

    def body(a_ref, w_ref, x1_ref, t_ref, gt_ref, g_ref, ddn_ref, dx_ref, loss_ref, dgt_ref, dg_ref, dn_ref):
        i, j = pl.program_id(0), pl.program_id(1)
        part = None
        for s in range(2):
            act = (_silu_parts(a_ref[0, s].astype(F32))[0] * a_ref[1, s].astype(F32)).astype(BF16)
            term = _dot(act, w_ref[s])
            part = term if part is None else part + term

        @pl.when(jnp.logical_and(i == 0, j == 0))
        def _():
            dgt_ref[...] = jnp.zeros_like(dgt_ref)
            dg_ref[...] = jnp.zeros_like(dg_ref)

        @pl.when(j == 0)
        def _():
            dn_ref[...] = part

        @pl.when(j > 0)
        def _():
            dn_ref[...] += part

        @pl.when(j == 1)
        def _():
            dn, gv, gate = dn_ref[...], g_ref[...], gt_ref[...]
            r = _rsqrt_mean(dn)
            normed = dn * r * gv
            err = x1_ref[...] + gate * normed - t_ref[...]
            dx = err * (1.0 / D_MODEL)
            dx_ref[...] = dx
            tot = jnp.sum(jnp.sum(err * err, axis=1, keepdims=True), axis=0, keepdims=True) * (0.5 / D_MODEL)
            loss_ref[...] = jnp.broadcast_to(tot, (8, 128))
            dgt_ref[...] += _colsum(dx * normed)
            dnn = dx * gate
            dg_ref[...] += _colsum(dnn * dn * r)
            ddn_ref[...] = _norm_bwd(dnn, dn, r, gv).astype(BF16)

    row = pl.BlockSpec((tm, D_MODEL), lambda i, j: (i, 0))
    vec = _const((1, D_MODEL))
    return _call(body, name='ffn_down', grid=(nb, 2),
                 in_specs=[pl.BlockSpec((2, 2, tm, FF_SHARD), lambda i, j: (0, j, i, 0)),
                           pl.BlockSpec((2, FF_SHARD, D_MODEL), lambda i, j: (j, 0, 0)), row, row, vec, vec],
                 out_specs=[row, row, pl.BlockSpec((None, 8, 128), lambda i, j: (i, 0, 0)), vec, vec],
                 out_shape=[_sds((T, D_MODEL), BF16), _sds((T, D_MODEL)), _sds((nb, 8, 128)), _sds((1, D_MODEL)),
                            _sds((1, D_MODEL))],
                 scratch=[pltpu.VMEM((tm, D_MODEL), F32)], sem=('arbitrary', 'arbitrary'),
                 vmem=VMEM_BIG)(hid4, wd4, x1, tgt, gt, g_post)


def _ssm_prep(lre, lim, lst, b_re, b_im):
    def body(lre_ref, lim_ref, lst_ref, br_ref, bi_ref, ar_ref, ai_ref, bbr_ref, bbi_ref):
        ar, ai, qr, qi = _zoh(lre_ref[...], lim_ref[...], lst_ref[...])[:4]
        ar_ref[...] = ar
        ai_ref[...] = ai
        bbr_ref[...] = qr * br_ref[...] - qi * bi_ref[...]
        bbi_ref[...] = qr * bi_ref[...] + qi * br_ref[...]

    shp = lre.shape
    return _call(body, name='ssm_prep', grid=(1,), in_specs=[_const(shp)] * 5, out_specs=[_const(shp)] * 4,
                 out_shape=[_sds(shp)] * 4)(lre, lim, lst, b_re, b_im)


def _zoh(lre, lim, lst):
    lr = jnp.minimum(lre, LAMBDA_RE_MAX)
    st = jnp.exp(lst)
    mag = jnp.exp(lr * st)
    ar = mag * jnp.cos(lim * st)
    ai = mag * jnp.sin(lim * st)
    den = lr * lr + lim * lim
    qr = ((ar - 1.0) * lr + ai * lim) / den
    qi = (ai * lr - (ar - 1.0) * lim) / den
    return ar, ai, qr, qi, lr, st, den


def _ssm_prep_bwd(lre, lim, lst, b_re, b_im, dbbr, dbbi, dar, dai, seg):
    def body(lre_ref, lim_ref, lst_ref, br_ref, bi_ref, dbbr_ref, dbbi_ref, dar_ref, dai_ref, seg_ref,
             dbr_ref, dbi_ref, dlre_ref, dlim_ref, dlst_ref):
        lre_v = lre_ref[...]
        li = lim_ref[...]
        ar, ai, qr, qi, lr, st, den = _zoh(lre_v, li, lst_ref[...])
        br, bi, gbr, gbi = br_ref[...], bi_ref[...], dbbr_ref[...], dbbi_ref[...]
        dbr_ref[...] = qr * gbr + qi * gbi
        dbi_ref[...] = qr * gbi - qi * gbr
        gqr = _dot_split(br * gbr + bi * gbi, seg_ref[...], 3)
        gqi = _dot_split(br * gbi - bi * gbr, seg_ref[...], 3)
        ir, ii = lr / den, -li / den
        gar = dar_ref[...] + ir * gqr + ii * gqi
        gai = dai_ref[...] + ir * gqi - ii * gqr
        tr, ti = qr * ir - qi * ii, qr * ii + qi * ir
        glr = -(tr * gqr + ti * gqi)
        gli = -(tr * gqi - ti * gqr)
        gzr = ar * gar + ai * gai
        gzi = ar * gai - ai * gar
        glr = glr + st * gzr
        gli = gli + st * gzi
        gst = (lr * gzr + li * gzi) * st
        dlre_ref[...] = jnp.where(lre_v < LAMBDA_RE_MAX, glr, 0.0)
        dlim_ref[...] = gli
        dlst_ref[...] = jnp.sum(gst, axis=1, keepdims=True) * (1.0 / SSM_GROUP)

    shp = lre.shape
    return _call(body, name='ssm_prep_bwd', grid=(1,), in_specs=[_const(shp)] * 9 + [_const(seg.shape)],
                 out_specs=[_const(shp)] * 4 + [_const((N_GROUPS, 1))],
                 out_shape=[_sds(shp)] * 4 + [_sds((N_GROUPS, 1))], vmem=VMEM_BIG)(
                     lre, lim, lst, b_re, b_im, dbbr, dbbi, dar, dai, seg)


def _scan_specs(T):
    return dict(
        chan=pl.BlockSpec((T, CHAN_BLOCK), lambda cb: (0, cb)),
        state=pl.BlockSpec((T, STATE_BLOCK), lambda cb: (0, cb)),
        b=pl.BlockSpec((CHAN_BLOCK, STATE_BLOCK), lambda cb: (cb, cb)),
        c=pl.BlockSpec((STATE_BLOCK, CHAN_BLOCK), lambda cb: (cb, cb)),
        lam=pl.BlockSpec((1, STATE_BLOCK), lambda cb: (0, cb)),
    )


def _complex_power(re, im, n):
    out = None
    while True:
        if n & 1:
            out = (re, im) if out is None else (out[0] * re - out[1] * im, out[0] * im + out[1] * re)
        n >>= 1
        if n == 0:
            return out
        re, im = re * re - im * im, 2.0 * re * im


def _rows8(i):
    if isinstance(i, int):
        return pl.ds(i * SUBLANES, SUBLANES)
    return pl.ds(pl.multiple_of(i * SUBLANES, SUBLANES), SUBLANES)


def _scan_loop(n_steps, body, init):
    trips = n_steps // SCAN_UNROLL

    def trip(t, carry):
        for u in range(SCAN_UNROLL):
            carry = body(t * SCAN_UNROLL + u, carry)
        return carry

    carry = lax.fori_loop(0, trips, trip, init)
    for step in range(trips * SCAN_UNROLL, n_steps):
        carry = body(step, carry)
    return carry


def _ssm_fwd(u_perm, b_re, b_im, c_re, c_im, lam_r, lam_i, ride):
    T = u_perm.shape[0]
    ls = T // SUBLANES
    rc = min(512, T)
    sp = _scan_specs(T)

    def body(u_ref, bre_ref, bim_ref, cre_ref, cim_ref, lr_ref, li_ref, so_re_ref, so_im_ref, y_ref, sre_ref, sim_ref):
        for c in range(T // rc):
            rows = pl.ds(c * rc, rc)
            ub = u_ref[rows, :].astype(BF16)
            sre_ref[rows, :] = _dot(ub, bre_ref[...])
            sim_ref[rows, :] = _dot(ub, bim_ref[...])
        shp = (SUBLANES, STATE_BLOCK)
        lr = jnp.broadcast_to(lr_ref[...], shp)
        li = jnp.broadcast_to(li_ref[...], shp)
        zero = jnp.zeros(shp, F32)

        def step(i, carry):
            sr, si = carry
            rows = _rows8(i)
            nr = lr * sr - li * si + sre_ref[rows, :]
            ni = lr * si + li * sr + sim_ref[rows, :]
            sre_ref[rows, :] = nr
            sim_ref[rows, :] = ni
            return nr, ni

        fr, fi = _scan_loop(ls, step, (zero, zero))
        pr, pi_ = _complex_power(lr, li, ls)
        row = lax.broadcasted_iota(jnp.int32, shp, 0)
        ir, ii = zero, zero
        for _ in range(SUBLANES - 1):
            er = fr + pr * ir - pi_ * ii
            ei = fi + pr * ii + pi_ * ir
            ir = jnp.where(row == 0, 0.0, pltpu.roll(er, 1, 0))
            ii = jnp.where(row == 0, 0.0, pltpu.roll(ei, 1, 0))

        def fix(i, carry):
            cr, ci = carry
            rows = _rows8(i)
            nr = lr * cr - li * ci
            ni = lr * ci + li * cr
            sre_ref[rows, :] += nr
            sim_ref[rows, :] += ni
            return nr, ni

        _scan_loop(ls, fix, (ir, ii))
        for c in range(T // rc):
            rows = pl.ds(c * rc, rc)
            s_r, s_i = sre_ref[rows, :].astype(BF16), sim_ref[rows, :].astype(BF16)
            so_re_ref[rows, :] = s_r
            so_im_ref[rows, :] = s_i
            y_ref[rows, :] = _dot(s_r, cre_ref[...]) - _dot(s_i, cim_ref[...])

    return _call(body, name='ssm_fwd', grid=(N_STATE // STATE_BLOCK,),
                 in_specs=[sp['chan'], sp['b'], sp['b'], sp['c'], sp['c'], sp['lam'], sp['lam']],
                 out_specs=[sp['state'], sp['state'], sp['chan']],
                 out_shape=[_sds((T, N_STATE), BF16), _sds((T, N_STATE), BF16), _sds((T, D_SSM))],
                 scratch=[pltpu.VMEM((T, STATE_BLOCK), F32), pltpu.VMEM((T, STATE_BLOCK), F32)],
                 sem=('arbitrary',), vmem=VMEM_MOST, ride=ride)(u_perm, b_re, b_im, c_re, c_im, lam_r, lam_i)


def _ssm_bwd(dy_perm, u_perm, s_re, s_im, b_re, b_im, c_re, c_im, lam_r, lam_i, ride):
    T = u_perm.shape[0]
    ls = T // SUBLANES
    rc = min(512, T)
    sp = _scan_specs(T)
    ncb = N_STATE // STATE_BLOCK

    def body(dy_ref, u_ref, sre_ref, sim_ref, bre_ref, bim_ref, cre_ref, cim_ref, lr_ref, li_ref,
             du_ref, dbr_ref, dbi_ref, dcr_ref, dci_ref, dar_ref, dai_ref, gre_ref, gim_ref):
        shp = (SUBLANES, STATE_BLOCK)
        zero = jnp.zeros(shp, F32)
        tail = pl.ds(T, SUBLANES)
        gre_ref[tail, :] = zero
        gim_ref[tail, :] = zero
        for c in range(T // rc):
            rows = pl.ds(c * rc, rc)
            dyb = dy_ref[rows, :].astype(BF16)
            gre_ref[rows, :] = _dot_nt(dyb, cre_ref[...])
            gim_ref[rows, :] = -_dot_nt(dyb, cim_ref[...])
        lr = jnp.broadcast_to(lr_ref[...], shp)
        li = jnp.broadcast_to(li_ref[...], shp)

        def step(k, carry):
            gr, gi = carry
            rows = _rows8(ls - 1 - k)
            nr = lr * gr + li * gi + gre_ref[rows, :]
            ni = lr * gi - li * gr + gim_ref[rows, :]
            gre_ref[rows, :] = nr
            gim_ref[rows, :] = ni
            return nr, ni

        fr, fi = _scan_loop(ls, step, (zero, zero))
        pr, pi_ = _complex_power(lr, -li, ls)
        row = lax.broadcasted_iota(jnp.int32, shp, 0)
        cr, ci = zero, zero
        for _ in range(SUBLANES - 1):
            er = fr + pr * cr - pi_ * ci
            ei = fi + pr * ci + pi_ * cr
            cr = jnp.where(row == SUBLANES - 1, 0.0, pltpu.roll(er, SUBLANES - 1, 0))
            ci = jnp.where(row == SUBLANES - 1, 0.0, pltpu.roll(ei, SUBLANES - 1, 0))

        def fix(k, carry):
            dr, di = carry
            rows = _rows8(ls - 1 - k)
            dr, di = lr * dr + li * di, lr * di - li * dr
            gre_ref[rows, :] += dr
            gim_ref[rows, :] += di
            return dr, di

        _scan_loop(ls, fix, (cr, ci))

        acc_r = jnp.zeros((1, STATE_BLOCK), F32)
        acc_i = jnp.zeros((1, STATE_BLOCK), F32)
        for c in range(T // rc):
            rows, nxt = pl.ds(c * rc, rc), pl.ds(c * rc + SUBLANES, rc)
            s_r, s_i = sre_ref[rows, :].astype(F32), sim_ref[rows, :].astype(F32)
            g_r, g_i = gre_ref[nxt, :], gim_ref[nxt, :]
            acc_r = acc_r + _colsum(g_r * s_r + g_i * s_i)
            acc_i = acc_i + _colsum(g_i * s_r - g_r * s_i)
        last = pl.ds(T - 2 * SUBLANES, 2 * SUBLANES)
        first = pl.ds(0, SUBLANES)
        spr = jnp.where(row == 0, 0.0, pltpu.roll(sre_ref[last, :].astype(F32)[SUBLANES:], 1, 0))
        spi = jnp.where(row == 0, 0.0, pltpu.roll(sim_ref[last, :].astype(F32)[SUBLANES:], 1, 0))
        gr, gi = gre_ref[first, :], gim_ref[first, :]
        dar_ref[...] = acc_r + _colsum(gr * spr + gi * spi)
        dai_ref[...] = acc_i + _colsum(gi * spr - gr * spi)

        for c in range(T // rc):
            rows = pl.ds(c * rc, rc)
            g_r, g_i = gre_ref[rows, :].astype(BF16), gim_ref[rows, :].astype(BF16)
            s_r, s_i = sre_ref[rows, :], sim_ref[rows, :]
            ub, dyb = u_ref[rows, :].astype(BF16), dy_ref[rows, :].astype(BF16)
            du_ref[rows, :] = _dot_nt(g_r, bre_ref[...]) + _dot_nt(g_i, bim_ref[...])
            parts = (_dot_tn(ub, g_r), _dot_tn(ub, g_i), _dot_tn(s_r, dyb), -_dot_tn(s_i, dyb))
            outs = (dbr_ref, dbi_ref, dcr_ref, dci_ref)
            for o_ref, part in zip(outs, parts):
                if c == 0:
                    o_ref[...] = part
                else:
                    o_ref[...] += part

    blk = lambda r, c: pl.BlockSpec((None, r, c), lambda cb: (cb, 0, 0))
    return _call(body, name='ssm_bwd', grid=(ncb,),
                 in_specs=[sp['chan'], sp['chan'], sp['state'], sp['state'], sp['b'], sp['b'], sp['c'], sp['c'],
                           sp['lam'], sp['lam']],
                 out_specs=[sp['chan'], blk(CHAN_BLOCK, STATE_BLOCK), blk(CHAN_BLOCK, STATE_BLOCK),
                            blk(STATE_BLOCK, CHAN_BLOCK), blk(STATE_BLOCK, CHAN_BLOCK), blk(1, STATE_BLOCK),
                            blk(1, STATE_BLOCK)],
                 out_shape=[_sds((T, D_SSM)), _sds((ncb, CHAN_BLOCK, STATE_BLOCK)), _sds((ncb, CHAN_BLOCK, STATE_BLOCK)),
                            _sds((ncb, STATE_BLOCK, CHAN_BLOCK)), _sds((ncb, STATE_BLOCK, CHAN_BLOCK)),
                            _sds((ncb, 1, STATE_BLOCK)), _sds((ncb, 1, STATE_BLOCK))],
                 scratch=[pltpu.VMEM((T + SUBLANES, STATE_BLOCK), F32), pltpu.VMEM((T + SUBLANES, STATE_BLOCK), F32)],
                 sem=('arbitrary',), vmem=VMEM_MOST, ride=ride)(dy_perm, u_perm, s_re, s_im, b_re, b_im, c_re, c_im,
                                                                lam_r, lam_i)


def _ffn_dact(ddn, wd4, hid4, tm):
    T = ddn.shape[0]
    nb = T // tm

    def body(d_ref, w_ref, hid_ref, o_ref, gw_ref, acc_ref):
        i = pl.program_id(1)
        d = d_ref[...]
        dact = _dot_nt(d, w_ref[...])
        silu, dsilu = _silu_parts(hid_ref[0].astype(F32))
        hid_v = hid_ref[1].astype(F32)
        o_ref[0] = (dact * hid_v * dsilu).astype(BF16)
        o_ref[1] = (dact * silu).astype(BF16)
        part = _dot_tn((silu * hid_v).astype(BF16), d)

        @pl.when(i == 0)
        def _():
            acc_ref[...] = part

        @pl.when(i > 0)
        def _():
            acc_ref[...] += part

        @pl.when(i == nb - 1)
        def _():
            gw_ref[...] = acc_ref[...].astype(BF16)

    blk = pl.BlockSpec((2, None, tm, FF_SHARD), lambda j, i: (0, j, i, 0))
    w_blk = pl.BlockSpec((None, FF_SHARD, D_MODEL), lambda j, i: (j, 0, 0))
    return _call(body, name='ffn_dact', grid=(4, nb),
                 in_specs=[pl.BlockSpec((tm, D_MODEL), lambda j, i: (i, 0)), w_blk, blk],
                 out_specs=[blk, w_blk],
                 out_shape=[_sds((2, 4, T, FF_SHARD), BF16), _sds((4, FF_SHARD, D_MODEL), BF16)],
                 scratch=[pltpu.VMEM((FF_SHARD, D_MODEL), F32)], sem=('parallel', 'arbitrary'),
                 vmem=VMEM_BIG)(ddn, wd4, hid4)


def _ffn_dup(dhid8, up8, cw8, tm, ride):
    T = up8.shape[1]
    nb = T // tm
    ha = _halo_after(tm, T, HALO16)

    def body(dh_ref, dha_ref, up_ref, cw_ref, dup_ref, dcw_ref):
        i = pl.program_id(1)

        @pl.when(i == 0)
        def _():
            dcw_ref[...] = jnp.zeros_like(dcw_ref)

        dh = dh_ref[...].astype(F32)
        dup, dh1, dh2 = _conv3_t(dh, jnp.where(i < nb - 1, dha_ref[...].astype(F32), 0.0), cw_ref)
        dup_ref[...] = dup.astype(BF16)
        up = up_ref[...].astype(F32)
        dcw_ref[0:1, :] += _colsum(dh2 * up)
        dcw_ref[1:2, :] += _colsum(dh1 * up)
        dcw_ref[2:3, :] += _colsum(dh * up)

    main = pl.BlockSpec((None, tm, FF_SHARD), lambda j, i: (j, i, 0))
    return _call(body, name='ffn_dup', grid=(N_DEV, nb),
                 in_specs=[main, pl.BlockSpec((None, HALO16, FF_SHARD), lambda j, i: (j, ha(i), 0)), main,
                           pl.BlockSpec((None, 3, FF_SHARD), lambda j, i: (j, 0, 0))],
                 out_specs=[main, pl.BlockSpec((None, 8, FF_SHARD), lambda j, i: (j, 0, 0))],
                 out_shape=[_sds((N_DEV, T, FF_SHARD), BF16), _sds((N_DEV, 8, FF_SHARD))],
                 sem=('parallel', 'arbitrary'), vmem=VMEM_BIG, ride=ride)(dhid8, dhid8, up8, cw8)


def _grad_tn(a, b, a_spec, b_spec, groups, m, n, tk, name, ride=None, parts=1):
    T = a.shape[-2]
    nk = T // tk
    mp = m // parts

    def body(a_ref, b_ref, *refs):
        o_refs, acc_ref = refs[:parts], refs[parts]
        k = pl.program_id(1)
        part = _dot_tn(a_ref[...], b_ref[...])

        @pl.when(k == 0)
        def _():
            acc_ref[...] = part

        @pl.when(k > 0)
        def _():
            acc_ref[...] += part

        @pl.when(k == nk - 1)
        def _():
            for p, o_ref in enumerate(o_refs):
                o_ref[...] = acc_ref[p * mp:(p + 1) * mp, :].astype(BF16)

    out_spec = pl.BlockSpec((None, mp, n), lambda g, k: (g, 0, 0))
    res = _call(body, name=name, grid=(groups, nk), in_specs=[a_spec, b_spec], out_specs=[out_spec] * parts,
                out_shape=[_sds((groups, mp, n), BF16)] * parts, scratch=[pltpu.VMEM((m, n), F32)],
                sem=('parallel', 'arbitrary'), vmem=VMEM_BIG, ride=ride)(a, b)
    if parts > 1:
        return res
    return res[0] if ride is None else (res[0][0], res[1])


def _grad_w_in(h1, dproj, tk, ride):
    T = h1.shape[0]
    nk = T // tk
    half = D_IN_PROJ // 2

    def body(a_ref, b_ref, o_ref, acc_ref):
        k = pl.program_id(0)
        for h in range(2):
            cols = slice(h * half, (h + 1) * half)
            part = _dot_tn(a_ref[...], b_ref[:, cols])

            @pl.when(k == 0)
            def _():
                acc_ref[:, cols] = part

            @pl.when(k > 0)
            def _():
                acc_ref[:, cols] += part

        @pl.when(k == nk - 1)
        def _():
            for g in range(N_DEV):
                o_ref[g] = acc_ref[:, g * IN_SHARD:(g + 1) * IN_SHARD].astype(BF16)

    return _call(body, name='grad_w_in', grid=(nk,),
                 in_specs=[pl.BlockSpec((tk, D_MODEL), lambda k: (k, 0)), pl.BlockSpec((tk, D_IN_PROJ), lambda k: (k, 0))],
                 out_specs=_const((N_DEV, D_MODEL, IN_SHARD)), out_shape=_sds((N_DEV, D_MODEL, IN_SHARD), BF16),
                 scratch=[pltpu.VMEM((D_MODEL, D_IN_PROJ), F32)], sem=('arbitrary',), vmem=VMEM_BIG, ride=ride)(h1, dproj)


def _pre_norm_bwd(dz, dz_spec, w_s, xin, dres, sc, g, tm, name, ride, below=None, group=1, w_t=False):
    T = xin.shape[0]
    n = w_s.shape[1] if w_t else w_s.shape[2]
    mul = _dot if w_t else _dot_nt
    steps = N_DEV // group

    def body(dz_ref, w_ref, x_ref, dr_ref, sc_ref, g_ref, *refs):
        if below is None:
            dx_ref, dsh_ref, dsc_ref, dg_ref = refs
            sums = (dsh_ref, dsc_ref, dg_ref)
        else:
            v_ref, gate_ref, g2_ref, dx_ref, dsh_ref, dsc_ref, dg_ref, dv_ref, dgate_ref, dg2_ref = refs
            sums = (dsh_ref, dsc_ref, dg_ref, dgate_ref, dg2_ref)
        i, j = pl.program_id(0), pl.program_id(1)
        piece = (lambda s: dz_ref[s]) if dz.ndim == 3 else (lambda s: dz_ref[:, s * n:(s + 1) * n])
        part = mul(piece(0), w_ref[0])
        for s in range(1, group):
            part = part + mul(piece(s), w_ref[s])

        @pl.when(jnp.logical_and(i == 0, j == 0))
        def _():
            for s_ref in sums:
                s_ref[...] = jnp.zeros_like(s_ref)

        @pl.when(j == 0)
        def _():
            dx_ref[...] = part

        @pl.when(j > 0)
        def _():
            dx_ref[...] += part

        @pl.when(j == steps - 1)
        def _():
            dh, xv, gv = dx_ref[...], x_ref[...], g_ref[...]
            r = _rsqrt_mean(xv)
            dsh_ref[...] += _colsum(dh)
            dsc_ref[...] += _colsum(dh * (xv * r * gv))
            dxn = dh * (1.0 + sc_ref[...])
            dg_ref[...] += _colsum(dxn * xv * r)
            dx = dr_ref[...] + _norm_bwd(dxn, xv, r, gv)
            dx_ref[...] = dx
            if below is not None:
                v, g2 = v_ref[...], g2_ref[...]
                rv = _rsqrt_mean(v)
                dgate_ref[...] += _colsum(dx * (v * rv * g2))
                dn = dx * gate_ref[...]
                dg2_ref[...] += _colsum(dn * v * rv)
                dv_ref[...] = _norm_bwd(dn, v, rv, g2).astype(BF16)

    row = pl.BlockSpec((tm, D_MODEL), lambda i, j: (i, 0))
    vec = _const((1, D_MODEL))
    in_specs = [dz_spec, pl.BlockSpec((group,) + w_s.shape[1:], lambda i, j: (j, 0, 0)), row, row, vec, vec]
    out_specs = [row, vec, vec, vec]
    out_shape = [_sds((T, D_MODEL)), _sds((1, D_MODEL)), _sds((1, D_MODEL)), _sds((1, D_MODEL))]
    args = [dz, w_s, xin, dres, sc, g]
    if below is not None:
        in_specs += [row, vec, vec]
        out_specs += [row, vec, vec]
        out_shape += [_sds((T, D_MODEL), BF16), _sds((1, D_MODEL)), _sds((1, D_MODEL))]
        args += list(below)
    return _call(body, name=name, grid=(T // tm, steps), in_specs=in_specs, out_specs=out_specs,
                 out_shape=out_shape, sem=('arbitrary', 'arbitrary'), vmem=VMEM_MOST, ride=ride)(*args)


def _mix_bwd(d_o, w_out, yssm, proj, d, glu_w, glu_b, g_ssm, cw, g_conv, avg16, avg64, tm, ride):
    T = yssm.shape[0]
    hb = _halo_before(tm)

    def body(do_ref, wo_ref, y_ref, p_ref, ph_ref, d_ref, gw_ref, gb_ref, gs_ref, cw_ref, gc_ref, a16_ref, a64_ref,
             dy_ref, dconv_ref, dbg_ref, z_ref, dlin_ref, acc_ref):
        i = pl.program_id(0)
        dyc = _dot_nt(do_ref[...], wo_ref[...])

        @pl.when(i == 0)
        def _():
            acc_ref[...] = jnp.zeros_like(acc_ref)

        u = p_ref[:, 0:D_SSM]
        y = y_ref[...] + d_ref[...] * u
        z, t = _gelu(y)
        gate = _sigmoid(_dot(z.astype(BF16), gw_ref[...]) + gb_ref[...])
        ya = z * gate
        rs = lax.rsqrt(_dot_split(ya * ya, a16_ref[...], 2) + EPS)
        dna = dyc[:, 0:D_SSM]
        acc_ref[1:2, :] += _colsum(dna * ya * rs)
        dya = _head_norm_bwd(dna, ya, rs, gs_ref[...], a16_ref[...])
        dlin = dya * z * gate * (1.0 - gate)
        acc_ref[0:1, :] += _colsum(dlin)
        dlin_b = dlin.astype(BF16)
        dz = dya * gate + _dot_nt(dlin_b, gw_ref[...])
        dy = dz * _gelu_grad(y, t)
        acc_ref[3:4, :] += _colsum(dy * u)
        dy_ref[...] = dy
        z_ref[...] = z.astype(BF16)
        dlin_ref[...] = dlin_b

        bg = p_ref[:, D_SSM:D_SSM + D_CONV]
        cv = p_ref[:, D_SSM + D_CONV:D_SSM + 2 * D_CONV] * p_ref[:, D_SSM + 2 * D_CONV:D_IN_PROJ]
        hv = ph_ref[:, D_SSM + D_CONV:D_SSM + 2 * D_CONV] * ph_ref[:, D_SSM + 2 * D_CONV:D_IN_PROJ]
        hv = jnp.where(i > 0, hv, 0.0)
        conv, cv1, cv2 = _conv3(cv, hv, cw_ref)
        yb = bg * conv
        rsb = lax.rsqrt(_dot_split(yb * yb, a64_ref[...], 2) + EPS)
        dnb = dyc[:, D_SSM:D_MODEL]
        acc_ref[2:3, :] += _colsum(dnb * yb * rsb)
        dyb = _head_norm_bwd(dnb, yb, rsb, gc_ref[...], a64_ref[...])
        dbg_ref[...] = dyb * conv
        dconv = dyb * bg
        dconv_ref[...] = dconv
        acc_ref[4:5, :] += _colsum(dconv * cv2)
        acc_ref[5:6, :] += _colsum(dconv * cv1)
        acc_ref[6:7, :] += _colsum(dconv * cv)

    vec = _const((1, D_SSM))
    sq = _const((D_SSM, D_SSM))
    half = pl.BlockSpec((tm, D_SSM), lambda i: (i, 0))
    return _call(body, name='mix_bwd', grid=(T // tm,),
                 in_specs=[pl.BlockSpec((tm, D_MODEL), lambda i: (i, 0)), _const((D_MODEL, D_MODEL)), half,
                           pl.BlockSpec((tm, D_IN_PROJ), lambda i: (i, 0)),
                           pl.BlockSpec((HALO, D_IN_PROJ), lambda i: (hb(i), 0)), vec, sq, vec, vec,
                           _const((3, D_CONV)), vec, sq, sq],
                 out_specs=[half, half, half, half, half, _const((8, D_SSM))],
                 out_shape=[_sds((T, D_SSM)), _sds((T, D_SSM)), _sds((T, D_SSM)), _sds((T, D_SSM), BF16),
                            _sds((T, D_SSM), BF16), _sds((8, D_SSM))],
                 sem=('arbitrary',), vmem=VMEM_BIG, ride=ride)(d_o, w_out, yssm, proj, proj, d, glu_w, glu_b, g_ssm, cw,
                                                              g_conv, avg16, avg64)


def _mix_bwd_proj(dconv, proj, du_ssm, dy, d, dbg, cw, tm):
    T = dy.shape[0]
    nb = T // tm
    ha = _halo_after(tm, T)

    def body(dc_ref, dch_ref, cg_ref, v_ref, du_ref, dy_ref, d_ref, dbg_ref, cw_ref, o_ref):
        i = pl.program_id(0)
        dcv = _conv3_t(dc_ref[...], jnp.where(i < nb - 1, dch_ref[...], 0.0), cw_ref)[0]
        o_ref[:, 0:D_SSM] = (du_ref[...] + dy_ref[...] * d_ref[...]).astype(BF16)
        o_ref[:, D_SSM:D_SSM + D_CONV] = dbg_ref[...].astype(BF16)
        o_ref[:, D_SSM + D_CONV:D_SSM + 2 * D_CONV] = (dcv * v_ref[...]).astype(BF16)
        o_ref[:, D_SSM + 2 * D_CONV:D_IN_PROJ] = (dcv * cg_ref[...]).astype(BF16)

    half = pl.BlockSpec((tm, D_SSM), lambda i: (i, 0))
    return _call(body, name='mix_bwd_proj', grid=(nb,),
                 in_specs=[half, pl.BlockSpec((HALO, D_CONV), lambda i: (ha(i), 0)),
                           pl.BlockSpec((tm, D_CONV), lambda i: (i, 2)), pl.BlockSpec((tm, D_CONV), lambda i: (i, 3)),
                           half, half, _const((1, D_SSM)), half, _const((3, D_CONV))],
                 out_specs=pl.BlockSpec((tm, D_IN_PROJ), lambda i: (i, 0)), out_shape=_sds((T, D_IN_PROJ), BF16),
                 sem=('parallel',), vmem=VMEM_BIG)(dconv, dconv, proj, proj, du_ssm, dy, d, dbg, cw)


ADAMW_SLOT_BYTES = 8 << 20
ADAMW_ROW_BYTES = 3 << 19


def _row_tile(rows, cols, slots):
    for cand in range(rows, 15, -1):
        if (rows % cand == 0 and cand % 16 == 0 and slots * cand * cols * 4 <= ADAMW_SLOT_BYTES
                and cand * cols * 4 <= ADAMW_ROW_BYTES):
            return cand
    return rows


def _adamw_math(g, w, m, v):
    m2 = ADAM_B1 * m + (1.0 - ADAM_B1) * g
    v2 = ADAM_B2 * v + (1.0 - ADAM_B2) * (g * g)
    m_hat = m2 / (1.0 - ADAM_B1 ** ADAM_STEP)
    v_hat = v2 / (1.0 - ADAM_B2 ** ADAM_STEP)
    return -ADAM_LR * (m_hat / (jnp.sqrt(v_hat) + ADAM_EPS) + ADAM_WD * w), m2, v2


def _adamw(pieces, w, m, v, name):
    slots, _, cols = pieces[0].shape
    rows = sum(p.shape[1] for p in pieces)
    tr = _row_tile(pieces[0].shape[1], cols, slots)
    starts, pos = [], 0
    for p in pieces:
        assert p.shape[1] % tr == 0
        starts.append(pos)
        pos += p.shape[1] // tr

    def body(*refs):
        g_refs = refs[:len(pieces)]
        w_ref, m_ref, v_ref, go_ref, d_ref, mo_ref, vo_ref = refs[len(pieces):]
        i = pl.program_id(0)
        g = None
        for g_ref, start in zip(g_refs, starts):
            part = g_ref[0].astype(F32)
            for s in range(1, slots):
                part = part + g_ref[s].astype(F32)
            g = part if g is None else jnp.where(i >= start, part, g)
        go_ref[...] = g
        d_ref[...], mo_ref[...], vo_ref[...] = _adamw_math(g, w_ref[...], m_ref[...], v_ref[...])

    def piece_spec(start, count):
        return pl.BlockSpec((slots, tr, cols), lambda i: (0, jnp.clip(i - start, 0, count - 1), 0))

    blk = pl.BlockSpec((tr, cols), lambda i: (i, 0))
    return _call(body, name=name, grid=(rows // tr,),
                 in_specs=[piece_spec(s, p.shape[1] // tr) for s, p in zip(starts, pieces)] + [blk, blk, blk],
                 out_specs=[blk] * 4, out_shape=[_sds((rows, cols))] * 4, sem=('parallel',),
                 vmem=VMEM_BIG)(*pieces, w, m, v)


def _to_scan_rows(a):
    T, n = a.shape
    return a.reshape(SUBLANES, T // SUBLANES, n).transpose(1, 0, 2).reshape(T, n)


def _from_scan_rows(a):
    T, n = a.shape
    return a.reshape(T // SUBLANES, SUBLANES, n).transpose(1, 0, 2).reshape(T, n)


def _expand(a):
    return jnp.repeat(a, SSM_GROUP, axis=1)


def _block_diag(rows, row_group, col_group):
    r, n = rows.shape
    tiled = jnp.tile(rows, (1, N_GROUPS))
    keep = (jnp.arange(r)[:, None] // row_group) == (jnp.arange(n * N_GROUPS)[None, :] // col_group)
    return jnp.where(keep, tiled, 0.0)


def _block_diag_b(bb):
    return _block_diag(bb.transpose(0, 2, 1).reshape(D_SSM, SSM_STATE), SSM_GROUP, SSM_STATE)


def _block_diag_c(cc):
    return _block_diag(cc.transpose(0, 2, 1).reshape(N_STATE, SSM_GROUP), SSM_STATE, SSM_GROUP)


def _diag_blocks(x, chan_major):
    per = CHAN_BLOCK // SSM_GROUP
    eye = jnp.eye(per, dtype=x.dtype)
    if chan_major:
        x = x.reshape(-1, per, SSM_GROUP, per, SSM_STATE) * eye[None, :, None, :, None]
        return x.sum(axis=1).transpose(0, 2, 3, 1).reshape(N_GROUPS, SSM_STATE, SSM_GROUP)
    x = x.reshape(-1, per, SSM_STATE, per, SSM_GROUP) * eye[None, :, None, :, None]
    return x.sum(axis=3).reshape(N_GROUPS, SSM_STATE, SSM_GROUP)


SMALL_LAYOUT = {
    'ssm_b_re': (0, 0, 32, 1024), 'ssm_b_im': (32, 0, 32, 1024), 'ssm_c_re': (64, 0, 32, 1024),
    'ssm_c_im': (96, 0, 32, 1024), 'b_ada': (128, 0, 6, 1024), 'g_pre_mix': (134, 0, 1, 1024),
    'g_post_mix': (135, 0, 1, 1024), 'ssm_lam_re': (136, 0, 2, 1024), 'ssm_lam_im': (138, 0, 2, 1024),
    'ssm_log_step': (140, 0, 1, 32), 'glu_b': (141, 0, 1, 512), 'g_out_ssm': (141, 512, 1, 512),
    'g_out_conv': (142, 0, 1, 512), 'ssm_d': (142, 512, 1, 512), 'g_pre_ffn': (143, 0, 1, 1024),
    'g_post_ffn': (144, 0, 1, 1024)}
SMALL_ROWS = 152
B_ADA_ROW = SMALL_LAYOUT['b_ada'][0]
LATE_ROWS = {('b_ada', 0): 0, ('b_ada', 1): 1, ('g_pre_mix', 0): 2}


def _adamw_small(gathered, late, wts, mom_m, mom_v):
    names = list(SMALL_LAYOUT)
    n = len(names)

    def body(*refs):
        g_ref, late_ref, ins, outs = refs[0], refs[1], refs[2:2 + 3 * n], refs[2 + 3 * n:]
        for p, name in enumerate(names):
            r0, c0, rows, cols = SMALL_LAYOUT[name]
            pieces = [(0, rows)] if rows % 8 == 0 else [(r, 1) for r in range(rows)]
            for r, cnt in pieces:
                src_ref, first = (late_ref, LATE_ROWS[name, r]) if (name, r) in LATE_ROWS else (g_ref, r0 + r)
                g = src_ref[0, first:first + cnt, c0:c0 + cols]
                for s in range(1, N_DEV):
                    g = g + src_ref[s, first:first + cnt, c0:c0 + cols]
                w, m, v = (ins[3 * p + q][r:r + cnt, :] for q in range(3))
                res = (g,) + _adamw_math(g, w, m, v)
                for q in range(4):
                    outs[4 * p + q][r:r + cnt, :] = res[q]

    shapes = [SMALL_LAYOUT[name][2:] for name in names]
    args = [gathered, late]
    for name, shp in zip(names, shapes):
        args += [wts[name].reshape(shp), mom_m[name].reshape(shp), mom_v[name].reshape(shp)]
    outs = _call(body, name='adamw_small', grid=(1,),
                 in_specs=[_const(gathered.shape), _const(late.shape)]
                 + [_const(shp) for shp in shapes for _ in range(3)],
                 out_specs=[_const(shp) for shp in shapes for _ in range(4)],
                 out_shape=[_sds(shp) for shp in shapes for _ in range(4)], vmem=VMEM_BIG)(*args)
    res = {}
    for p, name in enumerate(names):
        for q, kind in enumerate(('g', 'd', 'm', 'v')):
            res[kind, name] = outs[4 * p + q].reshape(wts[name].shape)
    return res


def kernel(x, c, w_ada, b_ada, g_pre_mix, g_post_mix, w_in, ssm_lam_re, ssm_lam_im, ssm_log_step, ssm_b_re, ssm_b_im, ssm_c_re, ssm_c_im, ssm_d, glu_w, glu_b, g_out_ssm, conv_w, g_out_conv, w_out, g_pre_ffn, g_post_ffn, w_up, ffn_conv_w, w_down, loss_target, m_w_ada, m_b_ada, m_g_pre_mix, m_g_post_mix, m_w_in, m_ssm_lam_re, m_ssm_lam_im, m_ssm_log_step, m_ssm_b_re, m_ssm_b_im, m_ssm_c_re, m_ssm_c_im, m_ssm_d, m_glu_w, m_glu_b, m_g_out_ssm, m_conv_w, m_g_out_conv, m_w_out, m_g_pre_ffn, m_g_post_ffn, m_w_up, m_ffn_conv_w, m_w_down, v_w_ada, v_b_ada, v_g_pre_mix, v_g_post_mix, v_w_in, v_ssm_lam_re, v_ssm_lam_im, v_ssm_log_step, v_ssm_b_re, v_ssm_b_im, v_ssm_c_re, v_ssm_c_im, v_ssm_d, v_glu_w, v_glu_b, v_g_out_ssm, v_conv_w, v_g_out_conv, v_w_out, v_g_pre_ffn, v_g_post_ffn, v_w_up, v_ffn_conv_w, v_w_down):
    args = dict(locals())
    wts = {n: args[n] for n in WEIGHTS}
    mom_m = {n: args['m_' + n] for n in WEIGHTS}
    mom_v = {n: args['v_' + n] for n in WEIGHTS}
    T = x.shape[1]
    tm = min(512, T)
    tw = min(1024, T)
    tk = min(2048, T)
    me = _me()[3]
    xt, tgt = x[0], loss_target[0]

    c_all, w_in_s = _exchange([c, w_in[0].astype(BF16)], name='gather_first', scatter=False)
    c_all = c_all.reshape(N_DEV, D_MODEL)
    b_cols = lax.dynamic_slice(b_ada, (0, me * ADA_SHARD), (1, ADA_SHARD))
    mod_cols, c_act = _mod_cols(c_all, w_ada[0], b_cols)
    (mod_all,) = _exchange([mod_cols], name='gather_mod', scatter=False)
    mod = lax.dynamic_slice(mod_all, (0, me, 0), (N_DEV, 1, ADA_SHARD)).reshape(N_MOD, 1, D_MODEL)
    sh1, sc1, gt1, sh2, sc2, gt2 = [mod[k] for k in range(N_MOD)]


    lre_x, lim_x = _expand(ssm_lam_re[0]), _expand(ssm_lam_im[0])
    lst_x = jnp.broadcast_to(ssm_log_step[0][:, None], (N_GROUPS, SSM_STATE * SSM_GROUP))
    b_re_x = ssm_b_re[0].reshape(N_GROUPS, -1)
    b_im_x = ssm_b_im[0].reshape(N_GROUPS, -1)
    ar_x, ai_x, bbr_x, bbi_x = _ssm_prep(lre_x, lim_x, lst_x, b_re_x, b_im_x)
    lam_r = ar_x[:, ::SSM_GROUP].reshape(1, N_STATE)
    lam_i = ai_x[:, ::SSM_GROUP].reshape(1, N_STATE)
    big_b_re = _block_diag_b(bbr_x.reshape(N_GROUPS, SSM_STATE, SSM_GROUP)).astype(BF16)
    big_b_im = _block_diag_b(bbi_x.reshape(N_GROUPS, SSM_STATE, SSM_GROUP)).astype(BF16)
    big_c_re = _block_diag_c(ssm_c_re[0]).astype(BF16)
    big_c_im = _block_diag_c(ssm_c_im[0]).astype(BF16)
    head = jnp.arange(D_SSM)
    avg16 = jnp.where(head[:, None] // SSM_GROUP == head[None, :] // SSM_GROUP, 1.0 / SSM_GROUP, 0.0).astype(BF16)
    hd = D_CONV // CONV_HEADS
    avg64 = jnp.where(head[:, None] // hd == head[None, :] // hd, 1.0 / hd, 0.0).astype(BF16)

    (proj, h1), (w_down_s, ffn_conv_s, glu_s, w_out_s, conv_s) = _pre_mix(
        xt, sc1, sh1, g_pre_mix, w_in_s, tw,
        ([w_down[0].astype(BF16), ffn_conv_w[0], glu_w[0].astype(BF16), w_out[0].astype(BF16), conv_w[0]], False))
    glu_full = glu_s.reshape(D_SSM, D_SSM)
    w_out_full = w_out_s.reshape(D_MODEL, D_MODEL)
    cw_full = conv_s.transpose(1, 0, 2).reshape(3, D_CONV)
    wd4 = w_down_s.reshape(4, FF_SHARD, D_MODEL)
    u_perm = _to_scan_rows(proj[:, :D_SSM])
    (s_re, s_im, y_perm), (w_up_s,) = _ssm_fwd(u_perm, big_b_re, big_b_im, big_c_re, big_c_im, lam_r, lam_i,
                                               ([w_up[0].T.astype(BF16)], False))
    yssm = _from_scan_rows(y_perm)
    mix_args = (ssm_d, glu_full, glu_b, g_out_ssm, cw_full, g_out_conv, avg16, avg64)
    ycat = _mix_fwd(yssm, proj, *mix_args, tw)
    o, x1, h2 = _out_proj(ycat, w_out_full, xt, gt1, g_post_mix, g_pre_ffn, sc2, sh2, tw)
    up8, hid8 = _ffn_up(h2, w_up_s, ffn_conv_s, tw)
    hid4 = hid8.reshape(2, 4, T, FF_SHARD)
    ddn, dx2, loss_parts, d_gt2, d_g_post_ffn = _ffn_down(hid4, wd4, x1, tgt, gt2, g_post_ffn, tm)
    loss_local = jnp.sum(loss_parts[:, 0, 0])

    got = {}
    dhid, g_w_down = _ffn_dact(ddn, wd4, hid4, tw)
    (dup8, dcw_ffn), (got['w_down'],) = _ffn_dup(dhid.reshape(N_DEV, T, FF_SHARD), up8, ffn_conv_s, tw,
                                                 ([g_w_down.reshape(N_DEV, D_FF // N_DEV, D_MODEL)], True))
    g_w_up_halves = _grad_tn(dup8, h2, pl.BlockSpec((None, tk, FF_SHARD), lambda g, k: (g, k, 0)),
                             pl.BlockSpec((tk, D_MODEL), lambda g, k: (k, 0)), N_DEV, FF_SHARD, D_MODEL, tk,
                             'grad_w_up', parts=2)
    (dx1, d_sh2, d_sc2, d_g_pre_ffn, d_o, d_gt1, d_g_post_mix), (got_up_0, got['ffn_conv_w']) = _pre_norm_bwd(
        dup8, pl.BlockSpec((2, tw, FF_SHARD), lambda i, j: (j, i, 0)), w_up_s, x1, dx2, sc2, g_pre_ffn, tw,
        'ffn_in_bwd', ([g_w_up_halves[0], dcw_ffn], True), below=(o, gt1, g_post_mix), group=2, w_t=True)

    g_w_out = _grad_tn(ycat, d_o, pl.BlockSpec((tk, D_MODEL), lambda g, k: (k, 0)),
                       pl.BlockSpec((tk, D_MODEL), lambda g, k: (k, 0)), 1, D_MODEL, D_MODEL, tk, 'grad_w_out')
    (dy, dconv, dbg, z_b, dlin_b, sums), (got['w_out'],) = _mix_bwd(
        d_o, w_out_full, yssm, proj, *mix_args, tm, ([g_w_out.reshape(N_DEV, D_MODEL // N_DEV, D_MODEL)], True))
    g_glu_w = _grad_tn(z_b, dlin_b, pl.BlockSpec((tk, D_SSM), lambda g, k: (k, 0)),
                       pl.BlockSpec((tk, D_SSM), lambda g, k: (k, 0)), 1, D_SSM, D_SSM, tk, 'grad_glu_w')
    dy_perm = _to_scan_rows(dy)
    (du_perm, dbr_blk, dbi_blk, dcr_blk, dci_blk, dar_blk, dai_blk), (got_up_1, got['glu_w']) = _ssm_bwd(
        dy_perm, u_perm, s_re, s_im, big_b_re, big_b_im, big_c_re, big_c_im, lam_r, lam_i,
        ([g_w_up_halves[1], g_glu_w.reshape(N_DEV, D_SSM // N_DEV, D_SSM)], True))
    du_ssm = _from_scan_rows(du_perm)
    dproj = _mix_bwd_proj(dconv, proj, du_ssm, dy, ssm_d, dbg, cw_full, tw)
    dbb_re = _diag_blocks(dbr_blk, True).reshape(N_GROUPS, -1)
    dbb_im = _diag_blocks(dbi_blk, True).reshape(N_GROUPS, -1)
    d_c_re = _diag_blocks(dcr_blk, False).transpose(0, 2, 1)
    d_c_im = _diag_blocks(dci_blk, False).transpose(0, 2, 1)
    lane = jnp.arange(SSM_STATE * SSM_GROUP)
    seg = jnp.where(lane[:, None] // SSM_GROUP == lane[None, :] // SSM_GROUP, 1.0, 0.0).astype(BF16)
    d_b_re_x, d_b_im_x, d_lre_x, d_lim_x, d_lst = _ssm_prep_bwd(
        lre_x, lim_x, lst_x, b_re_x, b_im_x, dbb_re, dbb_im, _expand(dar_blk.reshape(N_GROUPS, SSM_STATE)),
        _expand(dai_blk.reshape(N_GROUPS, SSM_STATE)), seg)

    row = lambda a: a.reshape(-1, PACK_COLS)
    blank = jnp.zeros((1, PACK_COLS), F32)
    small_pack = jnp.concatenate([
        d_b_re_x, d_b_im_x, row(d_c_re), row(d_c_im), blank, blank, d_gt1, d_sh2, d_sc2, d_gt2, blank,
        d_g_post_mix, row(d_lre_x[:, ::SSM_GROUP]), row(d_lim_x[:, ::SSM_GROUP]),
        jnp.pad(d_lst.reshape(1, N_GROUPS), ((0, 0), (0, PACK_COLS - N_GROUPS))), row(sums[0:4]), d_g_pre_ffn,
        d_g_post_ffn, jnp.zeros((SMALL_ROWS - 145, PACK_COLS), F32)])
    g_w_in, (small_all,) = _grad_w_in(h1, dproj, tk, ([small_pack], False))
    g_conv_slots = jnp.concatenate([sums[4:7], jnp.zeros((5, D_CONV), F32)]).reshape(
        8, N_DEV, D_CONV // N_DEV).transpose(1, 0, 2)
    (grad_x, d_sh1, d_sc1, d_g_pre_mix), (got['w_in'], got['conv_w']) = _pre_norm_bwd(
        dproj, pl.BlockSpec((tw, D_IN_PROJ), lambda i, j: (i, j)), w_in_s, xt, dx1, sc1, g_pre_mix, tw,
        'mix_in_bwd', ([g_w_in, g_conv_slots], True), group=N_DEV)
    late_pack = jnp.concatenate([d_sh1, d_sc1, d_g_pre_mix, jnp.full((1, PACK_COLS), loss_local, F32),
                                 jnp.zeros((4, PACK_COLS), F32)])
    (late_all,) = _exchange([late_pack], name='gather_late_grads', scatter=False)
    loss = jnp.sum(late_all[:, 3, 0])
    res = _adamw_small(small_all, late_all, wts, mom_m, mom_v)

    dmod_all = jnp.concatenate([late_all[:, 0:2, :], small_all[:, B_ADA_ROW + 2:B_ADA_ROW + N_MOD, :]],
                               axis=1).reshape(N_DEV, N_MOD * D_MODEL)
    dmod_cols = lax.dynamic_slice(dmod_all, (0, me * ADA_SHARD), (N_DEV, ADA_SHARD))
    g_w_ada = _grad_w_ada(c_act.T, dmod_cols)

    pieces = {n: [slots[:, :3, :] if n in ('conv_w', 'ffn_conv_w') else slots] for n, slots in got.items()}
    for n, parts in pieces.items():
        outs = _adamw(parts, wts[n][0], mom_m[n][0], mom_v[n][0], 'adamw_' + n)
        for kind, val in zip(('g', 'd', 'm', 'v'), outs):
            res[kind, n] = val[None]
    outs = _adamw([got_up_0, got_up_1], w_up[0].T, m_w_up[0].T, v_w_up[0].T, 'adamw_w_up')
    for kind, val in zip(('g', 'd', 'm', 'v'), outs):
        res[kind, 'w_up'] = val.T[None]
    outs = _adamw([g_w_ada[None]], w_ada[0], m_w_ada[0], v_w_ada[0], 'adamw_w_ada')
    for kind, val in zip(('g', 'd', 'm', 'v'), outs):
        res[kind, 'w_ada'] = val[None]

    return (loss, grad_x[None], *[res['g', n] for n in WEIGHTS], *[res['d', n] for n in WEIGHTS],
            *[res['m', n] for n in WEIGHTS], *[res['v', n] for n in WEIGHTS])
```

```python
import math

import jax
import jax.numpy as jnp
from jax import lax
from jax.experimental import pallas as pl
from jax.experimental.pallas import tpu as pltpu

F32, BF16 = jnp.float32, jnp.bfloat16

D_MODEL = 1024
D_SSM = 512
D_CONV = 512
SSM_GROUP = 16
N_GROUPS = 32
SSM_STATE = 64
N_STATE = N_GROUPS * SSM_STATE
CONV_HEADS = 8
D_FF = 2816
N_MOD = 6
D_IN_PROJ = D_SSM + 3 * D_CONV
N_DEV = 8
FF_SHARD = 2 * D_FF // N_DEV
IN_SHARD = D_IN_PROJ // N_DEV
ADA_SHARD = N_MOD * D_MODEL // N_DEV
EPS = 1e-6
LAMBDA_RE_MAX = -1e-4
ADAM_LR, ADAM_B1, ADAM_B2, ADAM_EPS, ADAM_WD, ADAM_STEP = 0.001, 0.9, 0.999, 1e-08, 0.01, 10
GELU_C = math.sqrt(2.0 / math.pi)
GELU_A = 0.044715

SUBLANES = 8
HALO = 8
HALO16 = 16
SCAN_UNROLL = 8
STATE_BLOCK = 512
CHAN_BLOCK = 128
VMEM_BIG = 48 << 20
VMEM_MOST = 58 << 20

WEIGHTS = ['w_ada', 'b_ada', 'g_pre_mix', 'g_post_mix', 'w_in', 'ssm_lam_re', 'ssm_lam_im', 'ssm_log_step',
           'ssm_b_re', 'ssm_b_im', 'ssm_c_re', 'ssm_c_im', 'ssm_d', 'glu_w', 'glu_b', 'g_out_ssm', 'conv_w',
           'g_out_conv', 'w_out', 'g_pre_ffn', 'g_post_ffn', 'w_up', 'ffn_conv_w', 'w_down']
PACK_COLS = 1024


def _call(body, *, name, grid, in_specs, out_specs, out_shape, scratch=(), sem=None, vmem=None, ride=None):
    params = {}
    if vmem is not None:
        params['vmem_limit_bytes'] = vmem
    if ride is None:
        if sem is not None:
            params['dimension_semantics'] = sem
        return pl.pallas_call(body, name=name, grid=grid, in_specs=in_specs, out_specs=out_specs,
                              out_shape=out_shape, scratch_shapes=list(scratch),
                              compiler_params=pltpu.CompilerParams(**params))
    arrs, scatter = ride
    single = not isinstance(out_shape, (list, tuple))
    out_shape_l = [out_shape] if single else list(out_shape)
    out_specs_l = [out_specs] if single else list(out_specs)
    n, n_in, n_out, n_scr = len(arrs), len(in_specs), len(out_shape_l), len(scratch)
    any_spec = pl.BlockSpec(memory_space=pl.ANY)
    params['dimension_semantics'] = ('arbitrary',) * len(grid)

    def carried(*refs):
        ins, rin = refs[:n_in], refs[n_in:n_in + n]
        outs, rout = refs[n_in + n:n_in + n + n_out], refs[n_in + n + n_out:n_in + 2 * n + n_out]
        scr, sems = refs[n_in + 2 * n + n_out:n_in + 2 * n + n_out + n_scr], refs[n_in + 2 * n + n_out + n_scr:]
        first = pl.program_id(0) == 0
        last = pl.program_id(0) == grid[0] - 1
        for ax in range(1, len(grid)):
            first = jnp.logical_and(first, pl.program_id(ax) == 0)
            last = jnp.logical_and(last, pl.program_id(ax) == grid[ax] - 1)

        @pl.when(first)
        def _():
            _exchange_start(rin, rout, sems, scatter)

        body(*ins, *outs, *scr)

        @pl.when(last)
        def _():
            _exchange_wait(rin, rout, sems, scatter)

    call = pl.pallas_call(carried, name=name, grid=grid, in_specs=list(in_specs) + [any_spec] * n,
                          out_specs=out_specs_l + [any_spec] * n,
                          out_shape=out_shape_l + _exchange_shapes(arrs, scatter),
                          scratch_shapes=list(scratch) + _exchange_sems(n),
                          compiler_params=pltpu.CompilerParams(**params))

    def run(*args):
        res = call(*args, *arrs)
        own = res[0] if single else list(res[:n_out])
        return own, list(res[n_out:])

    return run


def _const(shape):
    nd = len(shape)
    return pl.BlockSpec(shape, lambda *_: (0,) * nd)


def _sds(shape, dtype=F32):
    return jax.ShapeDtypeStruct(shape, dtype)


def _dot(a, b):
    return jnp.dot(a, b, preferred_element_type=F32)


def _dot_nt(a, b):
    return lax.dot_general(a, b, (((1,), (1,)), ((), ())), preferred_element_type=F32)


def _dot_tn(a, b):
    return lax.dot_general(a, b, (((0,), (0,)), ((), ())), preferred_element_type=F32)


def _dot_split(x, mat, parts):
    acc = None
    rem = x
    for _ in range(parts):
        piece = rem.astype(BF16)
        rem = rem - piece.astype(F32)
        term = _dot(piece, mat)
        acc = term if acc is None else acc + term
    return acc


def _sigmoid(x):
    return 1.0 / (1.0 + jnp.exp(-x))


def _gelu(x):
    t = jnp.tanh(GELU_C * (x + GELU_A * x * x * x))
    return 0.5 * x * (1.0 + t), t


def _gelu_grad(x, t):
    return 0.5 * (1.0 + t) + 0.5 * x * (1.0 - t * t) * GELU_C * (1.0 + 3.0 * GELU_A * x * x)


def _rsqrt_mean(x):
    return lax.rsqrt(jnp.mean(x * x, axis=-1, keepdims=True) + EPS)


def _colsum(x):
    return jnp.sum(x, axis=0, keepdims=True)


def _shifts_down(x, halo):
    ext = jnp.concatenate([halo, x], axis=0)
    return pltpu.roll(ext, 1, 0)[halo.shape[0]:], pltpu.roll(ext, 2, 0)[halo.shape[0]:]


def _shifts_up(x, halo):
    n = x.shape[0]
    ext = jnp.concatenate([x, halo], axis=0)
    total = ext.shape[0]
    return pltpu.roll(ext, total - 1, 0)[:n], pltpu.roll(ext, total - 2, 0)[:n]


def _conv3(x, halo, w_ref):
    x1, x2 = _shifts_down(x, halo)
    return w_ref[0:1, :] * x2 + w_ref[1:2, :] * x1 + w_ref[2:3, :] * x, x1, x2


def _conv3_t(g, halo, w_ref):
    g1, g2 = _shifts_up(g, halo)
    return w_ref[2:3, :] * g + w_ref[1:2, :] * g1 + w_ref[0:1, :] * g2, g1, g2


def _silu_parts(x):
    s = _sigmoid(x)
    return x * s, s * (1.0 + x * (1.0 - s))


def _norm_bwd(dn, x, r, g):
    gd = g * dn
    return r * gd - x * (r * r * r) * jnp.mean(gd * x, axis=-1, keepdims=True)


def _head_norm_bwd(dn, y, rs, g, avg):
    gd = g * dn
    return rs * gd - y * (rs * rs * rs) * _dot_split(gd * y, avg, 2)


def _me():
    x, y, c = lax.axis_index('x'), lax.axis_index('y'), lax.axis_index('c')
    return x, y, c, 4 * x + 2 * y + c


def _peer(k):
    x, y, c, _ = _me()
    px = 1 - x if k & 4 else x
    py = 1 - y if k & 2 else y
    pc = 1 - c if k & 1 else c
    return (px, py, pc), 4 * px + 2 * py + pc


SIBLING = 1
OTHER_CHIPS = (2, 4, 6)


def _remote(src, dst, sems, a, k, dev):
    return pltpu.make_async_remote_copy(src_ref=src, dst_ref=dst, send_sem=sems[0].at[a, k - 1],
                                        recv_sem=sems[1].at[a, k - 1], device_id=dev,
                                        device_id_type=pl.DeviceIdType.MESH)


def _exchange_copies(ins, outs, sems, scatter):
    me = _me()[3]
    local, first, relay, arrivals = [], [], [], []
    for a in range(len(ins)):
        src = ins[a].at[me] if scatter else ins[a]
        local.append(pltpu.make_async_copy(src, outs[a].at[me], sems[2].at[a]))
        for k in range(1, N_DEV):
            dev, idx = _peer(k)
            landed = _remote(src, outs[a].at[idx], sems, a, k, dev)
            if scatter:
                first.append(_remote(ins[a].at[idx], outs[a].at[me], sems, a, k, dev))
                arrivals.append(landed)
            elif k == SIBLING:
                first.append(_remote(src, outs[a].at[me], sems, a, k, dev))
                arrivals.append(landed)
            elif k in OTHER_CHIPS:
                first.append(_remote(src, outs[a].at[me], sems, a, k, dev))
                sib, _ = _peer(SIBLING)
                relay.append((landed, _remote(outs[a].at[idx], outs[a].at[idx], sems, a, k | SIBLING, sib)))
            else:
                arrivals.append(landed)
    return local, first, relay, arrivals


def _exchange_start(ins, outs, sems, scatter):
    local, first, _, _ = _exchange_copies(ins, outs, sems, scatter)
    for cp in local + first:
        cp.start()


def _exchange_wait(ins, outs, sems, scatter):
    local, first, relay, arrivals = _exchange_copies(ins, outs, sems, scatter)
    for landed, forward in relay:
        landed.wait_recv()
        forward.start()
    for cp in arrivals:
        cp.wait_recv()
    for cp in first + [forward for _, forward in relay]:
        cp.wait_send()
    for cp in local:
        cp.wait()


def _exchange_shapes(arrs, scatter):
    return [_sds(a.shape if scatter else (N_DEV,) + a.shape, a.dtype) for a in arrs]


def _exchange_sems(n):
    return [pltpu.SemaphoreType.DMA((n, N_DEV - 1)), pltpu.SemaphoreType.DMA((n, N_DEV - 1)),
            pltpu.SemaphoreType.DMA((n,))]


def _exchange(arrs, *, name, scatter):
    n = len(arrs)

    def body(*refs):
        _exchange_start(refs[:n], refs[n:2 * n], refs[2 * n:], scatter)
        _exchange_wait(refs[:n], refs[n:2 * n], refs[2 * n:], scatter)

    any_spec = pl.BlockSpec(memory_space=pl.ANY)
    outs = pl.pallas_call(body, name=name, out_shape=_exchange_shapes(arrs, scatter), in_specs=[any_spec] * n,
                          out_specs=[any_spec] * n, scratch_shapes=_exchange_sems(n))(*arrs)
    return list(outs)


def _mod_cols(c_all, w_ada, b_cols):
    def body(c_ref, w_ref, b_ref, mod_ref, act_ref):
        c = c_ref[...]
        act = c * _sigmoid(c)
        act_ref[...] = act
        mod_ref[...] = _dot(act.astype(BF16), w_ref[...].astype(BF16)) + b_ref[...]

    return _call(body, name='mod_cols', grid=(1,),
                 in_specs=[_const(c_all.shape), _const(w_ada.shape), _const(b_cols.shape)],
                 out_specs=[_const((N_DEV, ADA_SHARD)), _const(c_all.shape)],
                 out_shape=[_sds((N_DEV, ADA_SHARD)), _sds(c_all.shape)], vmem=VMEM_BIG)(c_all, w_ada, b_cols)


def _grad_w_ada(act_t, dmod_cols):
    def body(a_ref, d_ref, o_ref):
        o_ref[...] = _dot(a_ref[...], d_ref[...])

    return _call(body, name='grad_w_ada', grid=(1,), in_specs=[_const(act_t.shape), _const(dmod_cols.shape)],
                 out_specs=_const((D_MODEL, ADA_SHARD)), out_shape=_sds((D_MODEL, ADA_SHARD)),
                 vmem=VMEM_BIG)(act_t, dmod_cols)


def _pre_mix(x, sc, sh, g, w_s, tm, ride):
    T = x.shape[0]
    group = 4

    def body(x_ref, sc_ref, sh_ref, g_ref, w_ref, proj_ref, h_ref):
        @pl.when(pl.program_id(1) == 0)
        def _():
            xv = x_ref[...]
            h_ref[...] = ((xv * _rsqrt_mean(xv) * g_ref[...]) * (1.0 + sc_ref[...]) + sh_ref[...]).astype(BF16)

        for s in range(group):
            proj_ref[:, s * IN_SHARD:(s + 1) * IN_SHARD] = _dot(h_ref[...], w_ref[s])

    row = pl.BlockSpec((tm, D_MODEL), lambda i, j: (i, 0))
    vec = _const((1, D_MODEL))
    return _call(body, name='pre_mix', grid=(T // tm, N_DEV // group),
                 in_specs=[row, vec, vec, vec, pl.BlockSpec((group, D_MODEL, IN_SHARD), lambda i, j: (j, 0, 0))],
                 out_specs=[pl.BlockSpec((tm, group * IN_SHARD), lambda i, j: (i, j)), row],
                 out_shape=[_sds((T, D_IN_PROJ)), _sds((T, D_MODEL), BF16)],
                 sem=('parallel', 'arbitrary'), ride=ride)(x, sc, sh, g, w_s)


def _halo_before(tm, rows=HALO):
    return lambda i: jnp.maximum(i * (tm // rows) - 1, 0)


def _halo_after(tm, T, rows=HALO):
    return lambda i: jnp.minimum((i + 1) * (tm // rows), T // rows - 1)


def _mix_fwd(yssm, proj, d, glu_w, glu_b, g_ssm, cw, g_conv, avg16, avg64, tm):
    T = yssm.shape[0]
    hb = _halo_before(tm)

    def body(y_ref, p_ref, ph_ref, d_ref, gw_ref, gb_ref, gs_ref, cw_ref, gc_ref, a16_ref, a64_ref, o_ref):
        i = pl.program_id(0)
        u = p_ref[:, 0:D_SSM]
        y = y_ref[...] + d_ref[...] * u
        z, _ = _gelu(y)
        gate = _sigmoid(_dot(z.astype(BF16), gw_ref[...]) + gb_ref[...])
        ya = z * gate
        rs = lax.rsqrt(_dot_split(ya * ya, a16_ref[...], 2) + EPS)
        o_ref[:, 0:D_SSM] = (ya * rs * gs_ref[...]).astype(BF16)
        bg = p_ref[:, D_SSM:D_SSM + D_CONV]
        cv = p_ref[:, D_SSM + D_CONV:D_SSM + 2 * D_CONV] * p_ref[:, D_SSM + 2 * D_CONV:D_IN_PROJ]
        hv = ph_ref[:, D_SSM + D_CONV:D_SSM + 2 * D_CONV] * ph_ref[:, D_SSM + 2 * D_CONV:D_IN_PROJ]
        hv = jnp.where(i > 0, hv, 0.0)
        conv, _, _ = _conv3(cv, hv, cw_ref)
        yb = bg * conv
        rsb = lax.rsqrt(_dot_split(yb * yb, a64_ref[...], 2) + EPS)
        o_ref[:, D_SSM:D_MODEL] = (yb * rsb * gc_ref[...]).astype(BF16)

    vec = _const((1, D_SSM))
    sq = _const((D_SSM, D_SSM))
    return _call(body, name='mix_fwd', grid=(T // tm,),
                 in_specs=[pl.BlockSpec((tm, D_SSM), lambda i: (i, 0)), pl.BlockSpec((tm, D_IN_PROJ), lambda i: (i, 0)),
                           pl.BlockSpec((HALO, D_IN_PROJ), lambda i: (hb(i), 0)), vec, sq, vec, vec,
                           _const((3, D_CONV)), vec, sq, sq],
                 out_specs=pl.BlockSpec((tm, D_MODEL), lambda i: (i, 0)), out_shape=_sds((T, D_MODEL), BF16),
                 sem=('parallel',), vmem=VMEM_BIG)(yssm, proj, proj, d, glu_w, glu_b, g_ssm, cw, g_conv, avg16, avg64)


def _out_proj(ycat, w_out, x, gt, g_post, g_pre, sc, sh, tm):
    T = x.shape[0]

    def body(y_ref, w_ref, x_ref, gt_ref, gp_ref, g2_ref, sc_ref, sh_ref, o_ref, x1_ref, h_ref):
        o = _dot(y_ref[...], w_ref[...])
        o_ref[...] = o
        x1 = x_ref[...] + gt_ref[...] * (o * _rsqrt_mean(o) * gp_ref[...])
        x1_ref[...] = x1
        h_ref[...] = ((x1 * _rsqrt_mean(x1) * g2_ref[...]) * (1.0 + sc_ref[...]) + sh_ref[...]).astype(BF16)

    row = pl.BlockSpec((tm, D_MODEL), lambda i: (i, 0))
    vec = _const((1, D_MODEL))
    return _call(body, name='out_proj', grid=(T // tm,),
                 in_specs=[row, _const((D_MODEL, D_MODEL)), row, vec, vec, vec, vec, vec],
                 out_specs=[row, row, row],
                 out_shape=[_sds((T, D_MODEL)), _sds((T, D_MODEL)), _sds((T, D_MODEL), BF16)],
                 sem=('parallel',), vmem=VMEM_BIG)(ycat, w_out, x, gt, g_post, g_pre, sc, sh)


def _ffn_up(h2, w_s, cw8, tm, ride):
    T = h2.shape[0]
    hb = _halo_before(tm, HALO16)

    def body(h_ref, hh_ref, w_ref, cw_ref, up_ref, hid_ref):
        for s in range(2):
            up = _dot_nt(h_ref[...], w_ref[s])
            up_ref[s] = up.astype(BF16)
            before = jnp.where(pl.program_id(0) > 0, _dot_nt(hh_ref[...], w_ref[s]), 0.0)
            hid_ref[s] = _conv3(up, before, cw_ref.at[s])[0].astype(BF16)

    out = pl.BlockSpec((2, tm, FF_SHARD), lambda i, j: (j, i, 0))
    return _call(body, name='ffn_up', grid=(T // tm, N_DEV // 2),
                 in_specs=[pl.BlockSpec((tm, D_MODEL), lambda i, j: (i, 0)),
                           pl.BlockSpec((HALO16, D_MODEL), lambda i, j: (hb(i), 0)),
                           pl.BlockSpec((2, FF_SHARD, D_MODEL), lambda i, j: (j, 0, 0)),
                           pl.BlockSpec((2, 3, FF_SHARD), lambda i, j: (j, 0, 0))],
                 out_specs=[out, out], out_shape=[_sds((N_DEV, T, FF_SHARD), BF16)] * 2,
                 sem=('parallel', 'parallel'), vmem=VMEM_BIG, ride=ride)(h2, h2, w_s, cw8)


def _ffn_down(hid4, wd4, x1, tgt, gt, g_post, tm):
    T = x1.shape[0]
    nb = T // tm

    def body(a_ref, w_ref, x1_ref, t_ref, gt_ref, g_ref, ddn_ref, dx_ref, loss_ref, dgt_ref, dg_ref, dn_ref):
        i, j = pl.program_id(0), pl.program_id(1)
        part = None
        for s in range(2):
            act = (_silu_parts(a_ref[0, s].astype(F32))[0] * a_ref[1, s].astype(F32)).astype(BF16)
            term = _dot(act, w_ref[s])
            part = term if part is None else part + term

        @pl.when(jnp.logical_and(i == 0, j == 0))
        def _():
            dgt_ref[...] = jnp.zeros_like(dgt_ref)
            dg_ref[...] = jnp.zeros_like(dg_ref)

        @pl.when(j == 0)
        def _():
            dn_ref[...] = part

        @pl.when(j > 0)
        def _():
            dn_ref[...] += part

        @pl.when(j == 1)
        def _():
            dn, gv, gate = dn_ref[...], g_ref[...], gt_ref[...]
            r = _rsqrt_mean(dn)
            normed = dn * r * gv
            err = x1_ref[...] + gate * normed - t_ref[...]
            dx = err * (1.0 / D_MODEL)
            dx_ref[...] = dx
            tot = jnp.sum(jnp.sum(err * err, axis=1, keepdims=True), axis=0, keepdims=True) * (0.5 / D_MODEL)
            loss_ref[...] = jnp.broadcast_to(tot, (8, 128))
            dgt_ref[...] += _colsum(dx * normed)
            dnn = dx * gate
            dg_ref[...] += _colsum(dnn * dn * r)
            ddn_ref[...] = _norm_bwd(dnn, dn, r, gv).astype(BF16)

    row = pl.BlockSpec((tm, D_MODEL), lambda i, j: (i, 0))
    vec = _const((1, D_MODEL))
    return _call(body, name='ffn_down', grid=(nb, 2),
                 in_specs=[pl.BlockSpec((2, 2, tm, FF_SHARD), lambda i, j: (0, j, i, 0)),
                           pl.BlockSpec((2, FF_SHARD, D_MODEL), lambda i, j: (j, 0, 0)), row, row, vec, vec],
                 out_specs=[row, row, pl.BlockSpec((None, 8, 128), lambda i, j: (i, 0, 0)), vec, vec],
                 out_shape=[_sds((T, D_MODEL), BF16), _sds((T, D_MODEL)), _sds((nb, 8, 128)), _sds((1, D_MODEL)),
                            _sds((1, D_MODEL))],
                 scratch=[pltpu.VMEM((tm, D_MODEL), F32)], sem=('arbitrary', 'arbitrary'),
                 vmem=VMEM_BIG)(hid4, wd4, x1, tgt, gt, g_post)


def _ssm_prep(lre, lim, lst, b_re, b_im):
    def body(lre_ref, lim_ref, lst_ref, br_ref, bi_ref, ar_ref, ai_ref, bbr_ref, bbi_ref):
        ar, ai, qr, qi = _zoh(lre_ref[...], lim_ref[...], lst_ref[...])[:4]
        ar_ref[...] = ar
        ai_ref[...] = ai
        bbr_ref[...] = qr * br_ref[...] - qi * bi_ref[...]
        bbi_ref[...] = qr * bi_ref[...] + qi * br_ref[...]

    shp = lre.shape
    return _call(body, name='ssm_prep', grid=(1,), in_specs=[_const(shp)] * 5, out_specs=[_const(shp)] * 4,
                 out_shape=[_sds(shp)] * 4)(lre, lim, lst, b_re, b_im)


def _zoh(lre, lim, lst):
    lr = jnp.minimum(lre, LAMBDA_RE_MAX)
    st = jnp.exp(lst)
    mag = jnp.exp(lr * st)
    ar = mag * jnp.cos(lim * st)
    ai = mag * jnp.sin(lim * st)
    den = lr * lr + lim * lim
    qr = ((ar - 1.0) * lr + ai * lim) / den
    qi = (ai * lr - (ar - 1.0) * lim) / den
    return ar, ai, qr, qi, lr, st, den


def _ssm_prep_bwd(lre, lim, lst, b_re, b_im, dbbr, dbbi, dar, dai, seg):
    def body(lre_ref, lim_ref, lst_ref, br_ref, bi_ref, dbbr_ref, dbbi_ref, dar_ref, dai_ref, seg_ref,
             dbr_ref, dbi_ref, dlre_ref, dlim_ref, dlst_ref):
        lre_v = lre_ref[...]
        li = lim_ref[...]
        ar, ai, qr, qi, lr, st, den = _zoh(lre_v, li, lst_ref[...])
        br, bi, gbr, gbi = br_ref[...], bi_ref[...], dbbr_ref[...], dbbi_ref[...]
        dbr_ref[...] = qr * gbr + qi * gbi
        dbi_ref[...] = qr * gbi - qi * gbr
        gqr = _dot_split(br * gbr + bi * gbi, seg_ref[...], 3)
        gqi = _dot_split(br * gbi - bi * gbr, seg_ref[...], 3)
        ir, ii = lr / den, -li / den
        gar = dar_ref[...] + ir * gqr + ii * gqi
        gai = dai_ref[...] + ir * gqi - ii * gqr
        tr, ti = qr * ir - qi * ii, qr * ii + qi * ir
        glr = -(tr * gqr + ti * gqi)
        gli = -(tr * gqi - ti * gqr)
        gzr = ar * gar + ai * gai
        gzi = ar * gai - ai * gar
        glr = glr + st * gzr
        gli = gli + st * gzi
        gst = (lr * gzr + li * gzi) * st
        dlre_ref[...] = jnp.where(lre_v < LAMBDA_RE_MAX, glr, 0.0)
        dlim_ref[...] = gli
        dlst_ref[...] = jnp.sum(gst, axis=1, keepdims=True) * (1.0 / SSM_GROUP)

    shp = lre.shape
    return _call(body, name='ssm_prep_bwd', grid=(1,), in_specs=[_const(shp)] * 9 + [_const(seg.shape)],
                 out_specs=[_const(shp)] * 4 + [_const((N_GROUPS, 1))],
                 out_shape=[_sds(shp)] * 4 + [_sds((N_GROUPS, 1))], vmem=VMEM_BIG)(
                     lre, lim, lst, b_re, b_im, dbbr, dbbi, dar, dai, seg)


def _scan_specs(T):
    return dict(
        chan=pl.BlockSpec((T, CHAN_BLOCK), lambda cb: (0, cb)),
        state=pl.BlockSpec((T, STATE_BLOCK), lambda cb: (0, cb)),
        b=pl.BlockSpec((CHAN_BLOCK, STATE_BLOCK), lambda cb: (cb, cb)),
        c=pl.BlockSpec((STATE_BLOCK, CHAN_BLOCK), lambda cb: (cb, cb)),
        lam=pl.BlockSpec((1, STATE_BLOCK), lambda cb: (0, cb)),
    )


def _complex_power(re, im, n):
    out = None
    while True:
        if n & 1:
            out = (re, im) if out is None else (out[0] * re - out[1] * im, out[0] * im + out[1] * re)
        n >>= 1
        if n == 0:
            return out
        re, im = re * re - im * im, 2.0 * re * im


def _rows8(i):
    if isinstance(i, int):
        return pl.ds(i * SUBLANES, SUBLANES)
    return pl.ds(pl.multiple_of(i * SUBLANES, SUBLANES), SUBLANES)


def _scan_loop(n_steps, body, init):
    trips = n_steps // SCAN_UNROLL

    def trip(t, carry):
        for u in range(SCAN_UNROLL):
            carry = body(t * SCAN_UNROLL + u, carry)
        return carry

    carry = lax.fori_loop(0, trips, trip, init)
    for step in range(trips * SCAN_UNROLL, n_steps):
        carry = body(step, carry)
    return carry


def _ssm_fwd(u_perm, b_re, b_im, c_re, c_im, lam_r, lam_i, ride):
    T = u_perm.shape[0]
    ls = T // SUBLANES
    rc = min(512, T)
    sp = _scan_specs(T)

    def body(u_ref, bre_ref, bim_ref, cre_ref, cim_ref, lr_ref, li_ref, so_re_ref, so_im_ref, y_ref, sre_ref, sim_ref):
        for c in range(T // rc):
            rows = pl.ds(c * rc, rc)
            ub = u_ref[rows, :].astype(BF16)
            sre_ref[rows, :] = _dot(ub, bre_ref[...])
            sim_ref[rows, :] = _dot(ub, bim_ref[...])
        shp = (SUBLANES, STATE_BLOCK)
        lr = jnp.broadcast_to(lr_ref[...], shp)
        li = jnp.broadcast_to(li_ref[...], shp)
        zero = jnp.zeros(shp, F32)

        def step(i, carry):
            sr, si = carry
            rows = _rows8(i)
            nr = lr * sr - li * si + sre_ref[rows, :]
            ni = lr * si + li * sr + sim_ref[rows, :]
            sre_ref[rows, :] = nr
            sim_ref[rows, :] = ni
            return nr, ni

        fr, fi = _scan_loop(ls, step, (zero, zero))
        pr, pi_ = _complex_power(lr, li, ls)
        row = lax.broadcasted_iota(jnp.int32, shp, 0)
        ir, ii = zero, zero
        for _ in range(SUBLANES - 1):
            er = fr + pr * ir - pi_ * ii
            ei = fi + pr * ii + pi_ * ir
            ir = jnp.where(row == 0, 0.0, pltpu.roll(er, 1, 0))
            ii = jnp.where(row == 0, 0.0, pltpu.roll(ei, 1, 0))

        def fix(i, carry):
            cr, ci = carry
            rows = _rows8(i)
            nr = lr * cr - li * ci
            ni = lr * ci + li * cr
            sre_ref[rows, :] += nr
            sim_ref[rows, :] += ni
            return nr, ni

        _scan_loop(ls, fix, (ir, ii))
        for c in range(T // rc):
            rows = pl.ds(c * rc, rc)
            s_r, s_i = sre_ref[rows, :].astype(BF16), sim_ref[rows, :].astype(BF16)
            so_re_ref[rows, :] = s_r
            so_im_ref[rows, :] = s_i
            y_ref[rows, :] = _dot(s_r, cre_ref[...]) - _dot(s_i, cim_ref[...])

    return _call(body, name='ssm_fwd', grid=(N_STATE // STATE_BLOCK,),
                 in_specs=[sp['chan'], sp['b'], sp['b'], sp['c'], sp['c'], sp['lam'], sp['lam']],
                 out_specs=[sp['state'], sp['state'], sp['chan']],
                 out_shape=[_sds((T, N_STATE), BF16), _sds((T, N_STATE), BF16), _sds((T, D_SSM))],
                 scratch=[pltpu.VMEM((T, STATE_BLOCK), F32), pltpu.VMEM((T, STATE_BLOCK), F32)],
                 sem=('arbitrary',), vmem=VMEM_MOST, ride=ride)(u_perm, b_re, b_im, c_re, c_im, lam_r, lam_i)


def _ssm_bwd(dy_perm, u_perm, s_re, s_im, b_re, b_im, c_re, c_im, lam_r, lam_i, ride):
    T = u_perm.shape[0]
    ls = T // SUBLANES
    rc = min(512, T)
    sp = _scan_specs(T)
    ncb = N_STATE // STATE_BLOCK

    def body(dy_ref, u_ref, sre_ref, sim_ref, bre_ref, bim_ref, cre_ref, cim_ref, lr_ref, li_ref,
             du_ref, dbr_ref, dbi_ref, dcr_ref, dci_ref, dar_ref, dai_ref, gre_ref, gim_ref):
        shp = (SUBLANES, STATE_BLOCK)
        zero = jnp.zeros(shp, F32)
        tail = pl.ds(T, SUBLANES)
        gre_ref[tail, :] = zero
        gim_ref[tail, :] = zero
        for c in range(T // rc):
            rows = pl.ds(c * rc, rc)
            dyb = dy_ref[rows, :].astype(BF16)
            gre_ref[rows, :] = _dot_nt(dyb, cre_ref[...])
            gim_ref[rows, :] = -_dot_nt(dyb, cim_ref[...])
        lr = jnp.broadcast_to(lr_ref[...], shp)
        li = jnp.broadcast_to(li_ref[...], shp)

        def step(k, carry):
            gr, gi = carry
            rows = _rows8(ls - 1 - k)
            nr = lr * gr + li * gi + gre_ref[rows, :]
            ni = lr * gi - li * gr + gim_ref[rows, :]
            gre_ref[rows, :] = nr
            gim_ref[rows, :] = ni
            return nr, ni

        fr, fi = _scan_loop(ls, step, (zero, zero))
        pr, pi_ = _complex_power(lr, -li, ls)
        row = lax.broadcasted_iota(jnp.int32, shp, 0)
        cr, ci = zero, zero
        for _ in range(SUBLANES - 1):
            er = fr + pr * cr - pi_ * ci
            ei = fi + pr * ci + pi_ * cr
            cr = jnp.where(row == SUBLANES - 1, 0.0, pltpu.roll(er, SUBLANES - 1, 0))
            ci = jnp.where(row == SUBLANES - 1, 0.0, pltpu.roll(ei, SUBLANES - 1, 0))

        def fix(k, carry):
            dr, di = carry
            rows = _rows8(ls - 1 - k)
            dr, di = lr * dr + li * di, lr * di - li * dr
            gre_ref[rows, :] += dr
            gim_ref[rows, :] += di
            return dr, di

        _scan_loop(ls, fix, (cr, ci))

        acc_r = jnp.zeros((1, STATE_BLOCK), F32)
        acc_i = jnp.zeros((1, STATE_BLOCK), F32)
        for c in range(T // rc):
            rows, nxt = pl.ds(c * rc, rc), pl.ds(c * rc + SUBLANES, rc)
            s_r, s_i = sre_ref[rows, :].astype(F32), sim_ref[rows, :].astype(F32)
            g_r, g_i = gre_ref[nxt, :], gim_ref[nxt, :]
            acc_r = acc_r + _colsum(g_r * s_r + g_i * s_i)
            acc_i = acc_i + _colsum(g_i * s_r - g_r * s_i)
        last = pl.ds(T - 2 * SUBLANES, 2 * SUBLANES)
        first = pl.ds(0, SUBLANES)
        spr = jnp.where(row == 0, 0.0, pltpu.roll(sre_ref[last, :].astype(F32)[SUBLANES:], 1, 0))
        spi = jnp.where(row == 0, 0.0, pltpu.roll(sim_ref[last, :].astype(F32)[SUBLANES:], 1, 0))
        gr, gi = gre_ref[first, :], gim_ref[first, :]
        dar_ref[...] = acc_r + _colsum(gr * spr + gi * spi)
        dai_ref[...] = acc_i + _colsum(gi * spr - gr * spi)

        for c in range(T // rc):
            rows = pl.ds(c * rc, rc)
            g_r, g_i = gre_ref[rows, :].astype(BF16), gim_ref[rows, :].astype(BF16)
            s_r, s_i = sre_ref[rows, :], sim_ref[rows, :]
            ub, dyb = u_ref[rows, :].astype(BF16), dy_ref[rows, :].astype(BF16)
            du_ref[rows, :] = _dot_nt(g_r, bre_ref[...]) + _dot_nt(g_i, bim_ref[...])
            parts = (_dot_tn(ub, g_r), _dot_tn(ub, g_i), _dot_tn(s_r, dyb), -_dot_tn(s_i, dyb))
            outs = (dbr_ref, dbi_ref, dcr_ref, dci_ref)
            for o_ref, part in zip(outs, parts):
                if c == 0:
                    o_ref[...] = part
                else:
                    o_ref[...] += part

    blk = lambda r, c: pl.BlockSpec((None, r, c), lambda cb: (cb, 0, 0))
    return _call(body, name='ssm_bwd', grid=(ncb,),
                 in_specs=[sp['chan'], sp['chan'], sp['state'], sp['state'], sp['b'], sp['b'], sp['c'], sp['c'],
                           sp['lam'], sp['lam']],
                 out_specs=[sp['chan'], blk(CHAN_BLOCK, STATE_BLOCK), blk(CHAN_BLOCK, STATE_BLOCK),
                            blk(STATE_BLOCK, CHAN_BLOCK), blk(STATE_BLOCK, CHAN_BLOCK), blk(1, STATE_BLOCK),
                            blk(1, STATE_BLOCK)],
                 out_shape=[_sds((T, D_SSM)), _sds((ncb, CHAN_BLOCK, STATE_BLOCK)), _sds((ncb, CHAN_BLOCK, STATE_BLOCK)),
                            _sds((ncb, STATE_BLOCK, CHAN_BLOCK)), _sds((ncb, STATE_BLOCK, CHAN_BLOCK)),
                            _sds((ncb, 1, STATE_BLOCK)), _sds((ncb, 1, STATE_BLOCK))],
                 scratch=[pltpu.VMEM((T + SUBLANES, STATE_BLOCK), F32), pltpu.VMEM((T + SUBLANES, STATE_BLOCK), F32)],
                 sem=('arbitrary',), vmem=VMEM_MOST, ride=ride)(dy_perm, u_perm, s_re, s_im, b_re, b_im, c_re, c_im,
                                                                lam_r, lam_i)


def _ffn_dact(ddn, wd4, hid4, tm):
    T = ddn.shape[0]
    nb = T // tm

    def body(d_ref, w_ref, hid_ref, o_ref, gw_ref, acc_ref):
        i = pl.program_id(1)
        d = d_ref[...]
        dact = _dot_nt(d, w_ref[...])
        silu, dsilu = _silu_parts(hid_ref[0].astype(F32))
        hid_v = hid_ref[1].astype(F32)
        o_ref[0] = (dact * hid_v * dsilu).astype(BF16)
        o_ref[1] = (dact * silu).astype(BF16)
        part = _dot_tn((silu * hid_v).astype(BF16), d)

        @pl.when(i == 0)
        def _():
            acc_ref[...] = part

        @pl.when(i > 0)
        def _():
            acc_ref[...] += part

        @pl.when(i == nb - 1)
        def _():
            gw_ref[...] = acc_ref[...].astype(BF16)

    blk = pl.BlockSpec((2, None, tm, FF_SHARD), lambda j, i: (0, j, i, 0))
    w_blk = pl.BlockSpec((None, FF_SHARD, D_MODEL), lambda j, i: (j, 0, 0))
    return _call(body, name='ffn_dact', grid=(4, nb),
                 in_specs=[pl.BlockSpec((tm, D_MODEL), lambda j, i: (i, 0)), w_blk, blk],
                 out_specs=[blk, w_blk],
                 out_shape=[_sds((2, 4, T, FF_SHARD), BF16), _sds((4, FF_SHARD, D_MODEL), BF16)],
                 scratch=[pltpu.VMEM((FF_SHARD, D_MODEL), F32)], sem=('parallel', 'arbitrary'),
                 vmem=VMEM_BIG)(ddn, wd4, hid4)


def _ffn_dup(dhid8, up8, cw8, tm, ride):
    T = up8.shape[1]
    nb = T // tm
    ha = _halo_after(tm, T, HALO16)

    def body(dh_ref, dha_ref, up_ref, cw_ref, dup_ref, dcw_ref):
        i = pl.program_id(1)

        @pl.when(i == 0)
        def _():
            dcw_ref[...] = jnp.zeros_like(dcw_ref)

        dh = dh_ref[...].astype(F32)
        dup, dh1, dh2 = _conv3_t(dh, jnp.where(i < nb - 1, dha_ref[...].astype(F32), 0.0), cw_ref)
        dup_ref[...] = dup.astype(BF16)
        up = up_ref[...].astype(F32)
        dcw_ref[0:1, :] += _colsum(dh2 * up)
        dcw_ref[1:2, :] += _colsum(dh1 * up)
        dcw_ref[2:3, :] += _colsum(dh * up)

    main = pl.BlockSpec((None, tm, FF_SHARD), lambda j, i: (j, i, 0))
    return _call(body, name='ffn_dup', grid=(N_DEV, nb),
                 in_specs=[main, pl.BlockSpec((None, HALO16, FF_SHARD), lambda j, i: (j, ha(i), 0)), main,
                           pl.BlockSpec((None, 3, FF_SHARD), lambda j, i: (j, 0, 0))],
                 out_specs=[main, pl.BlockSpec((None, 8, FF_SHARD), lambda j, i: (j, 0, 0))],
                 out_shape=[_sds((N_DEV, T, FF_SHARD), BF16), _sds((N_DEV, 8, FF_SHARD))],
                 sem=('parallel', 'arbitrary'), vmem=VMEM_BIG, ride=ride)(dhid8, dhid8, up8, cw8)


def _grad_tn(a, b, a_spec, b_spec, groups, m, n, tk, name, ride=None, parts=1):
    T = a.shape[-2]
    nk = T // tk
    mp = m // parts

    def body(a_ref, b_ref, *refs):
        o_refs, acc_ref = refs[:parts], refs[parts]
        k = pl.program_id(1)
        part = _dot_tn(a_ref[...], b_ref[...])

        @pl.when(k == 0)
        def _():
            acc_ref[...] = part

        @pl.when(k > 0)
        def _():
            acc_ref[...] += part

        @pl.when(k == nk - 1)
        def _():
            for p, o_ref in enumerate(o_refs):
                o_ref[...] = acc_ref[p * mp:(p + 1) * mp, :].astype(BF16)

    out_spec = pl.BlockSpec((None, mp, n), lambda g, k: (g, 0, 0))
    res = _call(body, name=name, grid=(groups, nk), in_specs=[a_spec, b_spec], out_specs=[out_spec] * parts,
                out_shape=[_sds((groups, mp, n), BF16)] * parts, scratch=[pltpu.VMEM((m, n), F32)],
                sem=('parallel', 'arbitrary'), vmem=VMEM_BIG, ride=ride)(a, b)
    if parts > 1:
        return res
    return res[0] if ride is None else (res[0][0], res[1])


def _grad_w_in(h1, dproj, tk, ride):
    T = h1.shape[0]
    nk = T // tk
    half = D_IN_PROJ // 2

    def body(a_ref, b_ref, o_ref, acc_ref):
        k = pl.program_id(0)
        for h in range(2):
            cols = slice(h * half, (h + 1) * half)
            part = _dot_tn(a_ref[...], b_ref[:, cols])

            @pl.when(k == 0)
            def _():
                acc_ref[:, cols] = part

            @pl.when(k > 0)
            def _():
                acc_ref[:, cols] += part

        @pl.when(k == nk - 1)
        def _():
            for g in range(N_DEV):
                o_ref[g] = acc_ref[:, g * IN_SHARD:(g + 1) * IN_SHARD].astype(BF16)

    return _call(body, name='grad_w_in', grid=(nk,),
                 in_specs=[pl.BlockSpec((tk, D_MODEL), lambda k: (k, 0)), pl.BlockSpec((tk, D_IN_PROJ), lambda k: (k, 0))],
                 out_specs=_const((N_DEV, D_MODEL, IN_SHARD)), out_shape=_sds((N_DEV, D_MODEL, IN_SHARD), BF16),
                 scratch=[pltpu.VMEM((D_MODEL, D_IN_PROJ), F32)], sem=('arbitrary',), vmem=VMEM_BIG, ride=ride)(h1, dproj)


def _pre_norm_bwd(dz, dz_spec, w_s, xin, dres, sc, g, tm, name, ride, below=None, group=1, w_t=False):
    T = xin.shape[0]
    n = w_s.shape[1] if w_t else w_s.shape[2]
    mul = _dot if w_t else _dot_nt
    steps = N_DEV // group

    def body(dz_ref, w_ref, x_ref, dr_ref, sc_ref, g_ref, *refs):
        if below is None:
            dx_ref, dsh_ref, dsc_ref, dg_ref = refs
            sums = (dsh_ref, dsc_ref, dg_ref)
        else:
            v_ref, gate_ref, g2_ref, dx_ref, dsh_ref, dsc_ref, dg_ref, dv_ref, dgate_ref, dg2_ref = refs
            sums = (dsh_ref, dsc_ref, dg_ref, dgate_ref, dg2_ref)
        i, j = pl.program_id(0), pl.program_id(1)
        piece = (lambda s: dz_ref[s]) if dz.ndim == 3 else (lambda s: dz_ref[:, s * n:(s + 1) * n])
        part = mul(piece(0), w_ref[0])
        for s in range(1, group):
            part = part + mul(piece(s), w_ref[s])

        @pl.when(jnp.logical_and(i == 0, j == 0))
        def _():
            for s_ref in sums:
                s_ref[...] = jnp.zeros_like(s_ref)

        @pl.when(j == 0)
        def _():
            dx_ref[...] = part

        @pl.when(j > 0)
        def _():
            dx_ref[...] += part

        @pl.when(j == steps - 1)
        def _():
            dh, xv, gv = dx_ref[...], x_ref[...], g_ref[...]
            r = _rsqrt_mean(xv)
            dsh_ref[...] += _colsum(dh)
            dsc_ref[...] += _colsum(dh * (xv * r * gv))
            dxn = dh * (1.0 + sc_ref[...])
            dg_ref[...] += _colsum(dxn * xv * r)
            dx = dr_ref[...] + _norm_bwd(dxn, xv, r, gv)
            dx_ref[...] = dx
            if below is not None:
                v, g2 = v_ref[...], g2_ref[...]
                rv = _rsqrt_mean(v)
                dgate_ref[...] += _colsum(dx * (v * rv * g2))
                dn = dx * gate_ref[...]
                dg2_ref[...] += _colsum(dn * v * rv)
                dv_ref[...] = _norm_bwd(dn, v, rv, g2).astype(BF16)

    row = pl.BlockSpec((tm, D_MODEL), lambda i, j: (i, 0))
    vec = _const((1, D_MODEL))
    in_specs = [dz_spec, pl.BlockSpec((group,) + w_s.shape[1:], lambda i, j: (j, 0, 0)), row, row, vec, vec]
    out_specs = [row, vec, vec, vec]
    out_shape = [_sds((T, D_MODEL)), _sds((1, D_MODEL)), _sds((1, D_MODEL)), _sds((1, D_MODEL))]
    args = [dz, w_s, xin, dres, sc, g]
    if below is not None:
        in_specs += [row, vec, vec]
        out_specs += [row, vec, vec]
        out_shape += [_sds((T, D_MODEL), BF16), _sds((1, D_MODEL)), _sds((1, D_MODEL))]
        args += list(below)
    return _call(body, name=name, grid=(T // tm, steps), in_specs=in_specs, out_specs=out_specs,
                 out_shape=out_shape, sem=('arbitrary', 'arbitrary'), vmem=VMEM_MOST, ride=ride)(*args)


def _mix_bwd(d_o, w_out, yssm, proj, d, glu_w, glu_b, g_ssm, cw, g_conv, avg16, avg64, tm, ride):
    T = yssm.shape[0]
    hb = _halo_before(tm)

    def body(do_ref, wo_ref, y_ref, p_ref, ph_ref, d_ref, gw_ref, gb_ref, gs_ref, cw_ref, gc_ref, a16_ref, a64_ref,
             dy_ref, dconv_ref, dbg_ref, z_ref, dlin_ref, acc_ref):
        i = pl.program_id(0)
        dyc = _dot_nt(do_ref[...], wo_ref[...])

        @pl.when(i == 0)
        def _():
            acc_ref[...] = jnp.zeros_like(acc_ref)

        u = p_ref[:, 0:D_SSM]
        y = y_ref[...] + d_ref[...] * u
        z, t = _gelu(y)
        gate = _sigmoid(_dot(z.astype(BF16), gw_ref[...]) + gb_ref[...])
        ya = z * gate
        rs = lax.rsqrt(_dot_split(ya * ya, a16_ref[...], 2) + EPS)
        dna = dyc[:, 0:D_SSM]
        acc_ref[1:2, :] += _colsum(dna * ya * rs)
        dya = _head_norm_bwd(dna, ya, rs, gs_ref[...], a16_ref[...])
        dlin = dya * z * gate * (1.0 - gate)
        acc_ref[0:1, :] += _colsum(dlin)
        dlin_b = dlin.astype(BF16)
        dz = dya * gate + _dot_nt(dlin_b, gw_ref[...])
        dy = dz * _gelu_grad(y, t)
        acc_ref[3:4, :] += _colsum(dy * u)
        dy_ref[...] = dy
        z_ref[...] = z.astype(BF16)
        dlin_ref[...] = dlin_b

        bg = p_ref[:, D_SSM:D_SSM + D_CONV]
        cv = p_ref[:, D_SSM + D_CONV:D_SSM + 2 * D_CONV] * p_ref[:, D_SSM + 2 * D_CONV:D_IN_PROJ]
        hv = ph_ref[:, D_SSM + D_CONV:D_SSM + 2 * D_CONV] * ph_ref[:, D_SSM + 2 * D_CONV:D_IN_PROJ]
        hv = jnp.where(i > 0, hv, 0.0)
        conv, cv1, cv2 = _conv3(cv, hv, cw_ref)
        yb = bg * conv
        rsb = lax.rsqrt(_dot_split(yb * yb, a64_ref[...], 2) + EPS)
        dnb = dyc[:, D_SSM:D_MODEL]
        acc_ref[2:3, :] += _colsum(dnb * yb * rsb)
        dyb = _head_norm_bwd(dnb, yb, rsb, gc_ref[...], a64_ref[...])
        dbg_ref[...] = dyb * conv
        dconv = dyb * bg
        dconv_ref[...] = dconv
        acc_ref[4:5, :] += _colsum(dconv * cv2)
        acc_ref[5:6, :] += _colsum(dconv * cv1)
        acc_ref[6:7, :] += _colsum(dconv * cv)

    vec = _const((1, D_SSM))
    sq = _const((D_SSM, D_SSM))
    half = pl.BlockSpec((tm, D_SSM), lambda i: (i, 0))
    return _call(body, name='mix_bwd', grid=(T // tm,),
                 in_specs=[pl.BlockSpec((tm, D_MODEL), lambda i: (i, 0)), _const((D_MODEL, D_MODEL)), half,
                           pl.BlockSpec((tm, D_IN_PROJ), lambda i: (i, 0)),
                           pl.BlockSpec((HALO, D_IN_PROJ), lambda i: (hb(i), 0)), vec, sq, vec, vec,
                           _const((3, D_CONV)), vec, sq, sq],
                 out_specs=[half, half, half, half, half, _const((8, D_SSM))],
                 out_shape=[_sds((T, D_SSM)), _sds((T, D_SSM)), _sds((T, D_SSM)), _sds((T, D_SSM), BF16),
                            _sds((T, D_SSM), BF16), _sds((8, D_SSM))],
                 sem=('arbitrary',), vmem=VMEM_BIG, ride=ride)(d_o, w_out, yssm, proj, proj, d, glu_w, glu_b, g_ssm, cw,
                                                              g_conv, avg16, avg64)


def _mix_bwd_proj(dconv, proj, du_ssm, dy, d, dbg, cw, tm):
    T = dy.shape[0]
    nb = T // tm
    ha = _halo_after(tm, T)

    def body(dc_ref, dch_ref, cg_ref, v_ref, du_ref, dy_ref, d_ref, dbg_ref, cw_ref, o_ref):
        i = pl.program_id(0)
        dcv = _conv3_t(dc_ref[...], jnp.where(i < nb - 1, dch_ref[...], 0.0), cw_ref)[0]
        o_ref[:, 0:D_SSM] = (du_ref[...] + dy_ref[...] * d_ref[...]).astype(BF16)
        o_ref[:, D_SSM:D_SSM + D_CONV] = dbg_ref[...].astype(BF16)
        o_ref[:, D_SSM + D_CONV:D_SSM + 2 * D_CONV] = (dcv * v_ref[...]).astype(BF16)
        o_ref[:, D_SSM + 2 * D_CONV:D_IN_PROJ] = (dcv * cg_ref[...]).astype(BF16)

    half = pl.BlockSpec((tm, D_SSM), lambda i: (i, 0))
    return _call(body, name='mix_bwd_proj', grid=(nb,),
                 in_specs=[half, pl.BlockSpec((HALO, D_CONV), lambda i: (ha(i), 0)),
                           pl.BlockSpec((tm, D_CONV), lambda i: (i, 2)), pl.BlockSpec((tm, D_CONV), lambda i: (i, 3)),
                           half, half, _const((1, D_SSM)), half, _const((3, D_CONV))],
                 out_specs=pl.BlockSpec((tm, D_IN_PROJ), lambda i: (i, 0)), out_shape=_sds((T, D_IN_PROJ), BF16),
                 sem=('parallel',), vmem=VMEM_BIG)(dconv, dconv, proj, proj, du_ssm, dy, d, dbg, cw)


ADAMW_SLOT_BYTES = 8 << 20
ADAMW_ROW_BYTES = 3 << 19


def _row_tile(rows, cols, slots):
    for cand in range(rows, 15, -1):
        if (rows % cand == 0 and cand % 16 == 0 and slots * cand * cols * 4 <= ADAMW_SLOT_BYTES
                and cand * cols * 4 <= ADAMW_ROW_BYTES):
            return cand
    return rows


def _adamw_math(g, w, m, v):
    m2 = ADAM_B1 * m + (1.0 - ADAM_B1) * g
    v2 = ADAM_B2 * v + (1.0 - ADAM_B2) * (g * g)
    m_hat = m2 / (1.0 - ADAM_B1 ** ADAM_STEP)
    v_hat = v2 / (1.0 - ADAM_B2 ** ADAM_STEP)
    return -ADAM_LR * (m_hat / (jnp.sqrt(v_hat) + ADAM_EPS) + ADAM_WD * w), m2, v2


def _adamw(pieces, w, m, v, name):
    slots, _, cols = pieces[0].shape
    rows = sum(p.shape[1] for p in pieces)
    tr = _row_tile(pieces[0].shape[1], cols, slots)
    starts, pos = [], 0
    for p in pieces:
        assert p.shape[1] % tr == 0
        starts.append(pos)
        pos += p.shape[1] // tr

    def body(*refs):
        g_refs = refs[:len(pieces)]
        w_ref, m_ref, v_ref, go_ref, d_ref, mo_ref, vo_ref = refs[len(pieces):]
        i = pl.program_id(0)
        g = None
        for g_ref, start in zip(g_refs, starts):
            part = g_ref[0].astype(F32)
            for s in range(1, slots):
                part = part + g_ref[s].astype(F32)
            g = part if g is None else jnp.where(i >= start, part, g)
        go_ref[...] = g
        d_ref[...], mo_ref[...], vo_ref[...] = _adamw_math(g, w_ref[...], m_ref[...], v_ref[...])

    def piece_spec(start, count):
        return pl.BlockSpec((slots, tr, cols), lambda i: (0, jnp.clip(i - start, 0, count - 1), 0))

    blk = pl.BlockSpec((tr, cols), lambda i: (i, 0))
    return _call(body, name=name, grid=(rows // tr,),
                 in_specs=[piece_spec(s, p.shape[1] // tr) for s, p in zip(starts, pieces)] + [blk, blk, blk],
                 out_specs=[blk] * 4, out_shape=[_sds((rows, cols))] * 4, sem=('parallel',),
                 vmem=VMEM_BIG)(*pieces, w, m, v)


def _to_scan_rows(a):
    T, n = a.shape
    return a.reshape(SUBLANES, T // SUBLANES, n).transpose(1, 0, 2).reshape(T, n)


def _from_scan_rows(a):
    T, n = a.shape
    return a.reshape(T // SUBLANES, SUBLANES, n).transpose(1, 0, 2).reshape(T, n)


def _expand(a):
    return jnp.repeat(a, SSM_GROUP, axis=1)


def _block_diag(rows, row_group, col_group):
    r, n = rows.shape
    tiled = jnp.tile(rows, (1, N_GROUPS))
    keep = (jnp.arange(r)[:, None] // row_group) == (jnp.arange(n * N_GROUPS)[None, :] // col_group)
    return jnp.where(keep, tiled, 0.0)


def _block_diag_b(bb):
    return _block_diag(bb.transpose(0, 2, 1).reshape(D_SSM, SSM_STATE), SSM_GROUP, SSM_STATE)


def _block_diag_c(cc):
    return _block_diag(cc.transpose(0, 2, 1).reshape(N_STATE, SSM_GROUP), SSM_STATE, SSM_GROUP)


def _diag_blocks(x, chan_major):
    per = CHAN_BLOCK // SSM_GROUP
    eye = jnp.eye(per, dtype=x.dtype)
    if chan_major:
        x = x.reshape(-1, per, SSM_GROUP, per, SSM_STATE) * eye[None, :, None, :, None]
        return x.sum(axis=1).transpose(0, 2, 3, 1).reshape(N_GROUPS, SSM_STATE, SSM_GROUP)
    x = x.reshape(-1, per, SSM_STATE, per, SSM_GROUP) * eye[None, :, None, :, None]
    return x.sum(axis=3).reshape(N_GROUPS, SSM_STATE, SSM_GROUP)


SMALL_LAYOUT = {
    'ssm_b_re': (0, 0, 32, 1024), 'ssm_b_im': (32, 0, 32, 1024), 'ssm_c_re': (64, 0, 32, 1024),
    'ssm_c_im': (96, 0, 32, 1024), 'b_ada': (128, 0, 6, 1024), 'g_pre_mix': (134, 0, 1, 1024),
    'g_post_mix': (135, 0, 1, 1024), 'ssm_lam_re': (136, 0, 2, 1024), 'ssm_lam_im': (138, 0, 2, 1024),
    'ssm_log_step': (140, 0, 1, 32), 'glu_b': (141, 0, 1, 512), 'g_out_ssm': (141, 512, 1, 512),
    'g_out_conv': (142, 0, 1, 512), 'ssm_d': (142, 512, 1, 512), 'g_pre_ffn': (143, 0, 1, 1024),
    'g_post_ffn': (144, 0, 1, 1024)}
SMALL_ROWS = 152
B_ADA_ROW = SMALL_LAYOUT['b_ada'][0]
LATE_ROWS = {('b_ada', 0): 0, ('b_ada', 1): 1, ('g_pre_mix', 0): 2}


def _adamw_small(gathered, late, wts, mom_m, mom_v):
    names = list(SMALL_LAYOUT)
    n = len(names)

    def body(*refs):
        g_ref, late_ref, ins, outs = refs[0], refs[1], refs[2:2 + 3 * n], refs[2 + 3 * n:]
        for p, name in enumerate(names):
            r0, c0, rows, cols = SMALL_LAYOUT[name]
            pieces = [(0, rows)] if rows % 8 == 0 else [(r, 1) for r in range(rows)]
            for r, cnt in pieces:
                src_ref, first = (late_ref, LATE_ROWS[name, r]) if (name, r) in LATE_ROWS else (g_ref, r0 + r)
                g = src_ref[0, first:first + cnt, c0:c0 + cols]
                for s in range(1, N_DEV):
                    g = g + src_ref[s, first:first + cnt, c0:c0 + cols]
                w, m, v = (ins[3 * p + q][r:r + cnt, :] for q in range(3))
                res = (g,) + _adamw_math(g, w, m, v)
                for q in range(4):
                    outs[4 * p + q][r:r + cnt, :] = res[q]

    shapes = [SMALL_LAYOUT[name][2:] for name in names]
    args = [gathered, late]
    for name, shp in zip(names, shapes):
        args += [wts[name].reshape(shp), mom_m[name].reshape(shp), mom_v[name].reshape(shp)]
    outs = _call(body, name='adamw_small', grid=(1,),
                 in_specs=[_const(gathered.shape), _const(late.shape)]
                 + [_const(shp) for shp in shapes for _ in range(3)],
                 out_specs=[_const(shp) for shp in shapes for _ in range(4)],
                 out_shape=[_sds(shp) for shp in shapes for _ in range(4)], vmem=VMEM_BIG)(*args)
    res = {}
    for p, name in enumerate(names):
        for q, kind in enumerate(('g', 'd', 'm', 'v')):
            res[kind, name] = outs[4 * p + q].reshape(wts[name].shape)
    return res


def kernel(x, c, w_ada, b_ada, g_pre_mix, g_post_mix, w_in, ssm_lam_re, ssm_lam_im, ssm_log_step, ssm_b_re, ssm_b_im, ssm_c_re, ssm_c_im, ssm_d, glu_w, glu_b, g_out_ssm, conv_w, g_out_conv, w_out, g_pre_ffn, g_post_ffn, w_up, ffn_conv_w, w_down, loss_target, m_w_ada, m_b_ada, m_g_pre_mix, m_g_post_mix, m_w_in, m_ssm_lam_re, m_ssm_lam_im, m_ssm_log_step, m_ssm_b_re, m_ssm_b_im, m_ssm_c_re, m_ssm_c_im, m_ssm_d, m_glu_w, m_glu_b, m_g_out_ssm, m_conv_w, m_g_out_conv, m_w_out, m_g_pre_ffn, m_g_post_ffn, m_w_up, m_ffn_conv_w, m_w_down, v_w_ada, v_b_ada, v_g_pre_mix, v_g_post_mix, v_w_in, v_ssm_lam_re, v_ssm_lam_im, v_ssm_log_step, v_ssm_b_re, v_ssm_b_im, v_ssm_c_re, v_ssm_c_im, v_ssm_d, v_glu_w, v_glu_b, v_g_out_ssm, v_conv_w, v_g_out_conv, v_w_out, v_g_pre_ffn, v_g_post_ffn, v_w_up, v_ffn_conv_w, v_w_down):
    args = dict(locals())
    wts = {n: args[n] for n in WEIGHTS}
    mom_m = {n: args['m_' + n] for n in WEIGHTS}
    mom_v = {n: args['v_' + n] for n in WEIGHTS}
    T = x.shape[1]
    tm = min(512, T)
    tw = min(1024, T)
    tk = min(2048, T)
    me = _me()[3]
    xt, tgt = x[0], loss_target[0]

    c_all, w_in_s = _exchange([c, w_in[0].astype(BF16)], name='gather_first', scatter=False)
    c_all = c_all.reshape(N_DEV, D_MODEL)
    b_cols = lax.dynamic_slice(b_ada, (0, me * ADA_SHARD), (1, ADA_SHARD))
    mod_cols, c_act = _mod_cols(c_all, w_ada[0], b_cols)
    (mod_all,) = _exchange([mod_cols], name='gather_mod', scatter=False)
    mod = lax.dynamic_slice(mod_all, (0, me, 0), (N_DEV, 1, ADA_SHARD)).reshape(N_MOD, 1, D_MODEL)
    sh1, sc1, gt1, sh2, sc2, gt2 = [mod[k] for k in range(N_MOD)]


    lre_x, lim_x = _expand(ssm_lam_re[0]), _expand(ssm_lam_im[0])
    lst_x = jnp.broadcast_to(ssm_log_step[0][:, None], (N_GROUPS, SSM_STATE * SSM_GROUP))
    b_re_x = ssm_b_re[0].reshape(N_GROUPS, -1)
    b_im_x = ssm_b_im[0].reshape(N_GROUPS, -1)
    ar_x, ai_x, bbr_x, bbi_x = _ssm_prep(lre_x, lim_x, lst_x, b_re_x, b_im_x)
    lam_r = ar_x[:, ::SSM_GROUP].reshape(1, N_STATE)
    lam_i = ai_x[:, ::SSM_GROUP].reshape(1, N_STATE)
    big_b_re = _block_diag_b(bbr_x.reshape(N_GROUPS, SSM_STATE, SSM_GROUP)).astype(BF16)
    big_b_im = _block_diag_b(bbi_x.reshape(N_GROUPS, SSM_STATE, SSM_GROUP)).astype(BF16)
    big_c_re = _block_diag_c(ssm_c_re[0]).astype(BF16)
    big_c_im = _block_diag_c(ssm_c_im[0]).astype(BF16)
    head = jnp.arange(D_SSM)
    avg16 = jnp.where(head[:, None] // SSM_GROUP == head[None, :] // SSM_GROUP, 1.0 / SSM_GROUP, 0.0).astype(BF16)
    hd = D_CONV // CONV_HEADS
    avg64 = jnp.where(head[:, None] // hd == head[None, :] // hd, 1.0 / hd, 0.0).astype(BF16)

    (proj, h1), (ffn_conv_s, glu_s, w_out_s, conv_s) = _pre_mix(
        xt, sc1, sh1, g_pre_mix, w_in_s, tw,
        ([ffn_conv_w[0], glu_w[0].astype(BF16), w_out[0].astype(BF16), conv_w[0]], False))
    glu_full = glu_s.reshape(D_SSM, D_SSM)
    w_out_full = w_out_s.reshape(D_MODEL, D_MODEL)
    cw_full = conv_s.transpose(1, 0, 2).reshape(3, D_CONV)
    u_perm = _to_scan_rows(proj[:, :D_SSM])
    (s_re, s_im, y_perm), (w_up_s,) = _ssm_fwd(u_perm, big_b_re, big_b_im, big_c_re, big_c_im, lam_r, lam_i,
                                               ([w_up[0].T.astype(BF16)], False))
    yssm = _from_scan_rows(y_perm)
    mix_args = (ssm_d, glu_full, glu_b, g_out_ssm, cw_full, g_out_conv, avg16, avg64)
    ycat = _mix_fwd(yssm, proj, *mix_args, tw)
    o, x1, h2 = _out_proj(ycat, w_out_full, xt, gt1, g_post_mix, g_pre_ffn, sc2, sh2, tw)
    (up8, hid8), (w_down_s,) = _ffn_up(h2, w_up_s, ffn_conv_s, tw, ([w_down[0].astype(BF16)], False))
    wd4 = w_down_s.reshape(4, FF_SHARD, D_MODEL)
    hid4 = hid8.reshape(2, 4, T, FF_SHARD)
    ddn, dx2, loss_parts, d_gt2, d_g_post_ffn = _ffn_down(hid4, wd4, x1, tgt, gt2, g_post_ffn, tm)
    loss_local = jnp.sum(loss_parts[:, 0, 0])

    got = {}
    dhid, g_w_down = _ffn_dact(ddn, wd4, hid4, tw)
    (dup8, dcw_ffn), (got['w_down'],) = _ffn_dup(dhid.reshape(N_DEV, T, FF_SHARD), up8, ffn_conv_s, tw,
                                                 ([g_w_down.reshape(N_DEV, D_FF // N_DEV, D_MODEL)], True))
    g_w_up_halves = _grad_tn(dup8, h2, pl.BlockSpec((None, tk, FF_SHARD), lambda g, k: (g, k, 0)),
                             pl.BlockSpec((tk, D_MODEL), lambda g, k: (k, 0)), N_DEV, FF_SHARD, D_MODEL, tk,
                             'grad_w_up', parts=2)
    (dx1, d_sh2, d_sc2, d_g_pre_ffn, d_o, d_gt1, d_g_post_mix), (got_up_0, got['ffn_conv_w']) = _pre_norm_bwd(
        dup8, pl.BlockSpec((2, tw, FF_SHARD), lambda i, j: (j, i, 0)), w_up_s, x1, dx2, sc2, g_pre_ffn, tw,
        'ffn_in_bwd', ([g_w_up_halves[0], dcw_ffn], True), below=(o, gt1, g_post_mix), group=2, w_t=True)

    g_w_out = _grad_tn(ycat, d_o, pl.BlockSpec((tk, D_MODEL), lambda g, k: (k, 0)),
                       pl.BlockSpec((tk, D_MODEL), lambda g, k: (k, 0)), 1, D_MODEL, D_MODEL, tk, 'grad_w_out')
    (dy, dconv, dbg, z_b, dlin_b, sums), (got['w_out'],) = _mix_bwd(
        d_o, w_out_full, yssm, proj, *mix_args, tm, ([g_w_out.reshape(N_DEV, D_MODEL // N_DEV, D_MODEL)], True))
    g_glu_w = _grad_tn(z_b, dlin_b, pl.BlockSpec((tk, D_SSM), lambda g, k: (k, 0)),
                       pl.BlockSpec((tk, D_SSM), lambda g, k: (k, 0)), 1, D_SSM, D_SSM, tk, 'grad_glu_w')
    dy_perm = _to_scan_rows(dy)
    (du_perm, dbr_blk, dbi_blk, dcr_blk, dci_blk, dar_blk, dai_blk), (got_up_1, got['glu_w']) = _ssm_bwd(
        dy_perm, u_perm, s_re, s_im, big_b_re, big_b_im, big_c_re, big_c_im, lam_r, lam_i,
        ([g_w_up_halves[1], g_glu_w.reshape(N_DEV, D_SSM // N_DEV, D_SSM)], True))
    du_ssm = _from_scan_rows(du_perm)
    dproj = _mix_bwd_proj(dconv, proj, du_ssm, dy, ssm_d, dbg, cw_full, tw)
    dbb_re = _diag_blocks(dbr_blk, True).reshape(N_GROUPS, -1)
    dbb_im = _diag_blocks(dbi_blk, True).reshape(N_GROUPS, -1)
    d_c_re = _diag_blocks(dcr_blk, False).transpose(0, 2, 1)
    d_c_im = _diag_blocks(dci_blk, False).transpose(0, 2, 1)
    lane = jnp.arange(SSM_STATE * SSM_GROUP)
    seg = jnp.where(lane[:, None] // SSM_GROUP == lane[None, :] // SSM_GROUP, 1.0, 0.0).astype(BF16)
    d_b_re_x, d_b_im_x, d_lre_x, d_lim_x, d_lst = _ssm_prep_bwd(
        lre_x, lim_x, lst_x, b_re_x, b_im_x, dbb_re, dbb_im, _expand(dar_blk.reshape(N_GROUPS, SSM_STATE)),
        _expand(dai_blk.reshape(N_GROUPS, SSM_STATE)), seg)

    row = lambda a: a.reshape(-1, PACK_COLS)
    blank = jnp.zeros((1, PACK_COLS), F32)
    small_pack = jnp.concatenate([
        d_b_re_x, d_b_im_x, row(d_c_re), row(d_c_im), blank, blank, d_gt1, d_sh2, d_sc2, d_gt2, blank,
        d_g_post_mix, row(d_lre_x[:, ::SSM_GROUP]), row(d_lim_x[:, ::SSM_GROUP]),
        jnp.pad(d_lst.reshape(1, N_GROUPS), ((0, 0), (0, PACK_COLS - N_GROUPS))), row(sums[0:4]), d_g_pre_ffn,
        d_g_post_ffn, jnp.zeros((SMALL_ROWS - 145, PACK_COLS), F32)])
    g_w_in, (small_all,) = _grad_w_in(h1, dproj, tk, ([small_pack], False))
    g_conv_slots = jnp.concatenate([sums[4:7], jnp.zeros((5, D_CONV), F32)]).reshape(
        8, N_DEV, D_CONV // N_DEV).transpose(1, 0, 2)
    (grad_x, d_sh1, d_sc1, d_g_pre_mix), (got['w_in'], got['conv_w']) = _pre_norm_bwd(
        dproj, pl.BlockSpec((tw, D_IN_PROJ), lambda i, j: (i, j)), w_in_s, xt, dx1, sc1, g_pre_mix, tw,
        'mix_in_bwd', ([g_w_in, g_conv_slots], True), group=N_DEV)
    late_pack = jnp.concatenate([d_sh1, d_sc1, d_g_pre_mix, jnp.full((1, PACK_COLS), loss_local, F32),
                                 jnp.zeros((4, PACK_COLS), F32)])
    (late_all,) = _exchange([late_pack], name='gather_late_grads', scatter=False)
    loss = jnp.sum(late_all[:, 3, 0])
    res = _adamw_small(small_all, late_all, wts, mom_m, mom_v)

    dmod_all = jnp.concatenate([late_all[:, 0:2, :], small_all[:, B_ADA_ROW + 2:B_ADA_ROW + N_MOD, :]],
                               axis=1).reshape(N_DEV, N_MOD * D_MODEL)
    dmod_cols = lax.dynamic_slice(dmod_all, (0, me * ADA_SHARD), (N_DEV, ADA_SHARD))
    g_w_ada = _grad_w_ada(c_act.T, dmod_cols)

    pieces = {n: [slots[:, :3, :] if n in ('conv_w', 'ffn_conv_w') else slots] for n, slots in got.items()}
    for n, parts in pieces.items():
        outs = _adamw(parts, wts[n][0], mom_m[n][0], mom_v[n][0], 'adamw_' + n)
        for kind, val in zip(('g', 'd', 'm', 'v'), outs):
            res[kind, n] = val[None]
    outs = _adamw([got_up_0, got_up_1], w_up[0].T, m_w_up[0].T, v_w_up[0].T, 'adamw_w_up')
    for kind, val in zip(('g', 'd', 'm', 'v'), outs):
        res[kind, 'w_up'] = val.T[None]
    outs = _adamw([g_w_ada[None]], w_ada[0], m_w_ada[0], v_w_ada[0], 'adamw_w_ada')
    for kind, val in zip(('g', 'd', 'm', 'v'), outs):
        res[kind, 'w_ada'] = val[None]

    return (loss, grad_x[None], *[res['g', n] for n in WEIGHTS], *[res['d', n] for n in WEIGHTS],
            *[res['m', n] for n in WEIGHTS], *[res['v', n] for n in WEIGHTS])
```

```python
import math

import jax
import jax.numpy as jnp
from jax import lax
from jax.experimental import pallas as pl
from jax.experimental.pallas import tpu as pltpu

F32, BF16 = jnp.float32, jnp.bfloat16

D_MODEL = 1024
D_SSM = 512
D_CONV = 512
SSM_GROUP = 16
N_GROUPS = 32
SSM_STATE = 64
N_STATE = N_GROUPS * SSM_STATE
CONV_HEADS = 8
D_FF = 2816
N_MOD = 6
D_IN_PROJ = D_SSM + 3 * D_CONV
N_DEV = 8
FF_SHARD = 2 * D_FF // N_DEV
IN_SHARD = D_IN_PROJ // N_DEV
ADA_SHARD = N_MOD * D_MODEL // N_DEV
EPS = 1e-6
LAMBDA_RE_MAX = -1e-4
ADAM_LR, ADAM_B1, ADAM_B2, ADAM_EPS, ADAM_WD, ADAM_STEP = 0.001, 0.9, 0.999, 1e-08, 0.01, 10
GELU_C = math.sqrt(2.0 / math.pi)
GELU_A = 0.044715

SUBLANES = 8
HALO = 8
HALO16 = 16
SCAN_UNROLL = 8
STATE_BLOCK = 512
CHAN_BLOCK = 128
VMEM_BIG = 48 << 20
VMEM_MOST = 58 << 20

WEIGHTS = ['w_ada', 'b_ada', 'g_pre_mix', 'g_post_mix', 'w_in', 'ssm_lam_re', 'ssm_lam_im', 'ssm_log_step',
           'ssm_b_re', 'ssm_b_im', 'ssm_c_re', 'ssm_c_im', 'ssm_d', 'glu_w', 'glu_b', 'g_out_ssm', 'conv_w',
           'g_out_conv', 'w_out', 'g_pre_ffn', 'g_post_ffn', 'w_up', 'ffn_conv_w', 'w_down']
PACK_COLS = 1024


def _call(body, *, name, grid, in_specs, out_specs, out_shape, scratch=(), sem=None, vmem=None, ride=None):
    params = {}
    if vmem is not None:
        params['vmem_limit_bytes'] = vmem
    if ride is None:
        if sem is not None:
            params['dimension_semantics'] = sem
        return pl.pallas_call(body, name=name, grid=grid, in_specs=in_specs, out_specs=out_specs,
                              out_shape=out_shape, scratch_shapes=list(scratch),
                              compiler_params=pltpu.CompilerParams(**params))
    arrs, scatter = ride
    single = not isinstance(out_shape, (list, tuple))
    out_shape_l = [out_shape] if single else list(out_shape)
    out_specs_l = [out_specs] if single else list(out_specs)
    n, n_in, n_out, n_scr = len(arrs), len(in_specs), len(out_shape_l), len(scratch)
    any_spec = pl.BlockSpec(memory_space=pl.ANY)
    params['dimension_semantics'] = ('arbitrary',) * len(grid)

    def carried(*refs):
        ins, rin = refs[:n_in], refs[n_in:n_in + n]
        outs, rout = refs[n_in + n:n_in + n + n_out], refs[n_in + n + n_out:n_in + 2 * n + n_out]
        scr, sems = refs[n_in + 2 * n + n_out:n_in + 2 * n + n_out + n_scr], refs[n_in + 2 * n + n_out + n_scr:]
        first = pl.program_id(0) == 0
        last = pl.program_id(0) == grid[0] - 1
        for ax in range(1, len(grid)):
            first = jnp.logical_and(first, pl.program_id(ax) == 0)
            last = jnp.logical_and(last, pl.program_id(ax) == grid[ax] - 1)

        @pl.when(first)
        def _():
            _exchange_start(rin, rout, sems, scatter)

        body(*ins, *outs, *scr)

        @pl.when(last)
        def _():
            _exchange_wait(rin, rout, sems, scatter)

    call = pl.pallas_call(carried, name=name, grid=grid, in_specs=list(in_specs) + [any_spec] * n,
                          out_specs=out_specs_l + [any_spec] * n,
                          out_shape=out_shape_l + _exchange_shapes(arrs, scatter),
                          scratch_shapes=list(scratch) + _exchange_sems(n),
                          compiler_params=pltpu.CompilerParams(**params))

    def run(*args):
        res = call(*args, *arrs)
        own = res[0] if single else list(res[:n_out])
        return own, list(res[n_out:])

    return run


def _const(shape):
    nd = len(shape)
    return pl.BlockSpec(shape, lambda *_: (0,) * nd)


def _sds(shape, dtype=F32):
    return jax.ShapeDtypeStruct(shape, dtype)


def _dot(a, b):
    return jnp.dot(a, b, preferred_element_type=F32)


def _dot_nt(a, b):
    return lax.dot_general(a, b, (((1,), (1,)), ((), ())), preferred_element_type=F32)


def _dot_tn(a, b):
    return lax.dot_general(a, b, (((0,), (0,)), ((), ())), preferred_element_type=F32)


def _dot_split(x, mat, parts):
    acc = None
    rem = x
    for _ in range(parts):
        piece = rem.astype(BF16)
        rem = rem - piece.astype(F32)
        term = _dot(piece, mat)
        acc = term if acc is None else acc + term
    return acc


def _sigmoid(x):
    return 1.0 / (1.0 + jnp.exp(-x))


def _gelu(x):
    t = jnp.tanh(GELU_C * (x + GELU_A * x * x * x))
    return 0.5 * x * (1.0 + t), t


def _gelu_grad(x, t):
    return 0.5 * (1.0 + t) + 0.5 * x * (1.0 - t * t) * GELU_C * (1.0 + 3.0 * GELU_A * x * x)


def _rsqrt_mean(x):
    return lax.rsqrt(jnp.mean(x * x, axis=-1, keepdims=True) + EPS)


def _colsum(x):
    return jnp.sum(x, axis=0, keepdims=True)


def _shifts_down(x, halo):
    ext = jnp.concatenate([halo, x], axis=0)
    return pltpu.roll(ext, 1, 0)[halo.shape[0]:], pltpu.roll(ext, 2, 0)[halo.shape[0]:]


def _shifts_up(x, halo):
    n = x.shape[0]
    ext = jnp.concatenate([x, halo], axis=0)
    total = ext.shape[0]
    return pltpu.roll(ext, total - 1, 0)[:n], pltpu.roll(ext, total - 2, 0)[:n]


def _conv3(x, halo, w_ref):
    x1, x2 = _shifts_down(x, halo)
    return w_ref[0:1, :] * x2 + w_ref[1:2, :] * x1 + w_ref[2:3, :] * x, x1, x2


def _conv3_t(g, halo, w_ref):
    g1, g2 = _shifts_up(g, halo)
    return w_ref[2:3, :] * g + w_ref[1:2, :] * g1 + w_ref[0:1, :] * g2, g1, g2


def _silu_parts(x):
    s = _sigmoid(x)
    return x * s, s * (1.0 + x * (1.0 - s))


def _norm_bwd(dn, x, r, g):
    gd = g * dn
    return r * gd - x * (r * r * r) * jnp.mean(gd * x, axis=-1, keepdims=True)


def _head_norm_bwd(dn, y, rs, g, avg):
    gd = g * dn
    return rs * gd - y * (rs * rs * rs) * _dot_split(gd * y, avg, 2)


def _me():
    x, y, c = lax.axis_index('x'), lax.axis_index('y'), lax.axis_index('c')
    return x, y, c, 4 * x + 2 * y + c


def _peer(k):
    x, y, c, _ = _me()
    px = 1 - x if k & 4 else x
    py = 1 - y if k & 2 else y
    pc = 1 - c if k & 1 else c
    return (px, py, pc), 4 * px + 2 * py + pc


SIBLING = 1
OTHER_CHIPS = (2, 4, 6)


def _remote(src, dst, sems, a, k, dev):
    return pltpu.make_async_remote_copy(src_ref=src, dst_ref=dst, send_sem=sems[0].at[a, k - 1],
                                        recv_sem=sems[1].at[a, k - 1], device_id=dev,
                                        device_id_type=pl.DeviceIdType.MESH)


def _exchange_copies(ins, outs, sems, scatter):
    me = _me()[3]
    local, first, relay, arrivals = [], [], [], []
    for a in range(len(ins)):
        src = ins[a].at[me] if scatter else ins[a]
        local.append(pltpu.make_async_copy(src, outs[a].at[me], sems[2].at[a]))
        for k in range(1, N_DEV):
            dev, idx = _peer(k)
            landed = _remote(src, outs[a].at[idx], sems, a, k, dev)
            if scatter:
                first.append(_remote(ins[a].at[idx], outs[a].at[me], sems, a, k, dev))
                arrivals.append(landed)
            elif k == SIBLING:
                first.append(_remote(src, outs[a].at[me], sems, a, k, dev))
                arrivals.append(landed)
            elif k in OTHER_CHIPS:
                first.append(_remote(src, outs[a].at[me], sems, a, k, dev))
                sib, _ = _peer(SIBLING)
                relay.append((landed, _remote(outs[a].at[idx], outs[a].at[idx], sems, a, k | SIBLING, sib)))
            else:
                arrivals.append(landed)
    return local, first, relay, arrivals


def _exchange_start(ins, outs, sems, scatter):
    local, first, _, _ = _exchange_copies(ins, outs, sems, scatter)
    for cp in local + first:
        cp.start()


def _exchange_wait(ins, outs, sems, scatter):
    local, first, relay, arrivals = _exchange_copies(ins, outs, sems, scatter)
    for landed, forward in relay:
        landed.wait_recv()
        forward.start()
    for cp in arrivals:
        cp.wait_recv()
    for cp in first + [forward for _, forward in relay]:
        cp.wait_send()
    for cp in local:
        cp.wait()


def _exchange_shapes(arrs, scatter):
    return [_sds(a.shape if scatter else (N_DEV,) + a.shape, a.dtype) for a in arrs]


def _exchange_sems(n):
    return [pltpu.SemaphoreType.DMA((n, N_DEV - 1)), pltpu.SemaphoreType.DMA((n, N_DEV - 1)),
            pltpu.SemaphoreType.DMA((n,))]


def _exchange(arrs, *, name, scatter):
    n = len(arrs)

    def body(*refs):
        _exchange_start(refs[:n], refs[n:2 * n], refs[2 * n:], scatter)
        _exchange_wait(refs[:n], refs[n:2 * n], refs[2 * n:], scatter)

    any_spec = pl.BlockSpec(memory_space=pl.ANY)
    outs = pl.pallas_call(body, name=name, out_shape=_exchange_shapes(arrs, scatter), in_specs=[any_spec] * n,
                          out_specs=[any_spec] * n, scratch_shapes=_exchange_sems(n))(*arrs)
    return list(outs)


def _mod_cols(c_all, w_ada, b_cols):
    def body(c_ref, w_ref, b_ref, mod_ref, act_ref):
        c = c_ref[...]
        act = c * _sigmoid(c)
        act_ref[...] = act
        mod_ref[...] = _dot(act.astype(BF16), w_ref[...].astype(BF16)) + b_ref[...]

    return _call(body, name='mod_cols', grid=(1,),
                 in_specs=[_const(c_all.shape), _const(w_ada.shape), _const(b_cols.shape)],
                 out_specs=[_const((N_DEV, ADA_SHARD)), _const(c_all.shape)],
                 out_shape=[_sds((N_DEV, ADA_SHARD)), _sds(c_all.shape)], vmem=VMEM_BIG)(c_all, w_ada, b_cols)


def _grad_w_ada(act_t, dmod_cols):
    def body(a_ref, d_ref, o_ref):
        o_ref[...] = _dot(a_ref[...], d_ref[...])

    return _call(body, name='grad_w_ada', grid=(1,), in_specs=[_const(act_t.shape), _const(dmod_cols.shape)],
                 out_specs=_const((D_MODEL, ADA_SHARD)), out_shape=_sds((D_MODEL, ADA_SHARD)),
                 vmem=VMEM_BIG)(act_t, dmod_cols)


def _pre_mix(x, sc, sh, g, w_s, tm, ride):
    T = x.shape[0]
    group = 4

    def body(x_ref, sc_ref, sh_ref, g_ref, w_ref, proj_ref, h_ref):
        @pl.when(pl.program_id(1) == 0)
        def _():
            xv = x_ref[...]
            h_ref[...] = ((xv * _rsqrt_mean(xv) * g_ref[...]) * (1.0 + sc_ref[...]) + sh_ref[...]).astype(BF16)

        for s in range(group):
            proj_ref[:, s * IN_SHARD:(s + 1) * IN_SHARD] = _dot(h_ref[...], w_ref[s])

    row = pl.BlockSpec((tm, D_MODEL), lambda i, j: (i, 0))
    vec = _const((1, D_MODEL))
    return _call(body, name='pre_mix', grid=(T // tm, N_DEV // group),
                 in_specs=[row, vec, vec, vec, pl.BlockSpec((group, D_MODEL, IN_SHARD), lambda i, j: (j, 0, 0))],
                 out_specs=[pl.BlockSpec((tm, group * IN_SHARD), lambda i, j: (i, j)), row],
                 out_shape=[_sds((T, D_IN_PROJ)), _sds((T, D_MODEL), BF16)],
                 sem=('parallel', 'arbitrary'), ride=ride)(x, sc, sh, g, w_s)


def _halo_before(tm, rows=HALO):
    return lambda i: jnp.maximum(i * (tm // rows) - 1, 0)


def _halo_after(tm, T, rows=HALO):
    return lambda i: jnp.minimum((i + 1) * (tm // rows), T // rows - 1)


def _mix_fwd(yssm, proj, d, glu_w, glu_b, g_ssm, cw, g_conv, avg16, avg64, tm):
    T = yssm.shape[0]
    hb = _halo_before(tm)

    def body(y_ref, p_ref, ph_ref, d_ref, gw_ref, gb_ref, gs_ref, cw_ref, gc_ref, a16_ref, a64_ref, o_ref):
        i = pl.program_id(0)
        u = p_ref[:, 0:D_SSM]
        y = y_ref[...] + d_ref[...] * u
        z, _ = _gelu(y)
        gate = _sigmoid(_dot(z.astype(BF16), gw_ref[...]) + gb_ref[...])
        ya = z * gate
        rs = lax.rsqrt(_dot_split(ya * ya, a16_ref[...], 2) + EPS)
        o_ref[:, 0:D_SSM] = (ya * rs * gs_ref[...]).astype(BF16)
        bg = p_ref[:, D_SSM:D_SSM + D_CONV]
        cv = p_ref[:, D_SSM + D_CONV:D_SSM + 2 * D_CONV] * p_ref[:, D_SSM + 2 * D_CONV:D_IN_PROJ]
        hv = ph_ref[:, D_SSM + D_CONV:D_SSM + 2 * D_CONV] * ph_ref[:, D_SSM + 2 * D_CONV:D_IN_PROJ]
        hv = jnp.where(i > 0, hv, 0.0)
        conv, _, _ = _conv3(cv, hv, cw_ref)
        yb = bg * conv
        rsb = lax.rsqrt(_dot_split(yb * yb, a64_ref[...], 2) + EPS)
        o_ref[:, D_SSM:D_MODEL] = (yb * rsb * gc_ref[...]).astype(BF16)

    vec = _const((1, D_SSM))
    sq = _const((D_SSM, D_SSM))
    return _call(body, name='mix_fwd', grid=(T // tm,),
                 in_specs=[pl.BlockSpec((tm, D_SSM), lambda i: (i, 0)), pl.BlockSpec((tm, D_IN_PROJ), lambda i: (i, 0)),
                           pl.BlockSpec((HALO, D_IN_PROJ), lambda i: (hb(i), 0)), vec, sq, vec, vec,
                           _const((3, D_CONV)), vec, sq, sq],
                 out_specs=pl.BlockSpec((tm, D_MODEL), lambda i: (i, 0)), out_shape=_sds((T, D_MODEL), BF16),
                 sem=('parallel',), vmem=VMEM_BIG)(yssm, proj, proj, d, glu_w, glu_b, g_ssm, cw, g_conv, avg16, avg64)


def _out_proj(ycat, w_out, x, gt, g_post, g_pre, sc, sh, tm):
    T = x.shape[0]

    def body(y_ref, w_ref, x_ref, gt_ref, gp_ref, g2_ref, sc_ref, sh_ref, o_ref, x1_ref, h_ref):
        o = _dot(y_ref[...], w_ref[...])
        o_ref[...] = o
        x1 = x_ref[...] + gt_ref[...] * (o * _rsqrt_mean(o) * gp_ref[...])
        x1_ref[...] = x1
        h_ref[...] = ((x1 * _rsqrt_mean(x1) * g2_ref[...]) * (1.0 + sc_ref[...]) + sh_ref[...]).astype(BF16)

    row = pl.BlockSpec((tm, D_MODEL), lambda i: (i, 0))
    vec = _const((1, D_MODEL))
    return _call(body, name='out_proj', grid=(T // tm,),
                 in_specs=[row, _const((D_MODEL, D_MODEL)), row, vec, vec, vec, vec, vec],
                 out_specs=[row, row, row],
                 out_shape=[_sds((T, D_MODEL)), _sds((T, D_MODEL)), _sds((T, D_MODEL), BF16)],
                 sem=('parallel',), vmem=VMEM_BIG)(ycat, w_out, x, gt, g_post, g_pre, sc, sh)


def _ffn_up(h2, w_a, w_b, cw8, tm, ride):
    T = h2.shape[0]
    hb = _halo_before(tm, HALO16)
    half = D_MODEL // 2

    def body(h_ref, hh_ref, wa_ref, wb_ref, cw_ref, up_ref, hid_ref):
        def times_w(ref, s):
            return _dot_nt(ref[:, :half], wa_ref[s]) + _dot_nt(ref[:, half:], wb_ref[s])

        for s in range(2):
            up = times_w(h_ref, s)
            up_ref[s] = up.astype(BF16)
            before = jnp.where(pl.program_id(0) > 0, times_w(hh_ref, s), 0.0)
            hid_ref[s] = _conv3(up, before, cw_ref.at[s])[0].astype(BF16)

    out = pl.BlockSpec((2, tm, FF_SHARD), lambda i, j: (j, i, 0))
    return _call(body, name='ffn_up', grid=(T // tm, N_DEV // 2),
                 in_specs=[pl.BlockSpec((tm, D_MODEL), lambda i, j: (i, 0)),
                           pl.BlockSpec((HALO16, D_MODEL), lambda i, j: (hb(i), 0)),
                           pl.BlockSpec((2, FF_SHARD, half), lambda i, j: (j, 0, 0)),
                           pl.BlockSpec((2, FF_SHARD, half), lambda i, j: (j, 0, 0)),
                           pl.BlockSpec((2, 3, FF_SHARD), lambda i, j: (j, 0, 0))],
                 out_specs=[out, out], out_shape=[_sds((N_DEV, T, FF_SHARD), BF16)] * 2,
                 sem=('parallel', 'parallel'), vmem=VMEM_BIG, ride=ride)(h2, h2, w_a, w_b, cw8)


def _ffn_down(hid4, wd4, x1, tgt, gt, g_post, tm):
    T = x1.shape[0]
    nb = T // tm

    def body(a_ref, w_ref, x1_ref, t_ref, gt_ref, g_ref, ddn_ref, dx_ref, loss_ref, dgt_ref, dg_ref, dn_ref):
        i, j = pl.program_id(0), pl.program_id(1)
        part = None
        for s in range(2):
            act = (_silu_parts(a_ref[0, s].astype(F32))[0] * a_ref[1, s].astype(F32)).astype(BF16)
            term = _dot(act, w_ref[s])
            part = term if part is None else part + term

        @pl.when(jnp.logical_and(i == 0, j == 0))
        def _():
            dgt_ref[...] = jnp.zeros_like(dgt_ref)
            dg_ref[...] = jnp.zeros_like(dg_ref)

        @pl.when(j == 0)
        def _():
            dn_ref[...] = part

        @pl.when(j > 0)
        def _():
            dn_ref[...] += part

        @pl.when(j == 1)
        def _():
            dn, gv, gate = dn_ref[...], g_ref[...], gt_ref[...]
            r = _rsqrt_mean(dn)
            normed = dn * r * gv
            err = x1_ref[...] + gate * normed - t_ref[...]
            dx = err * (1.0 / D_MODEL)
            dx_ref[...] = dx
            tot = jnp.sum(jnp.sum(err * err, axis=1, keepdims=True), axis=0, keepdims=True) * (0.5 / D_MODEL)
            loss_ref[...] = jnp.broadcast_to(tot, (8, 128))
            dgt_ref[...] += _colsum(dx * normed)
            dnn = dx * gate
            dg_ref[...] += _colsum(dnn * dn * r)
            ddn_ref[...] = _norm_bwd(dnn, dn, r, gv).astype(BF16)

    row = pl.BlockSpec((tm, D_MODEL), lambda i, j: (i, 0))
    vec = _const((1, D_MODEL))
    return _call(body, name='ffn_down', grid=(nb, 2),
                 in_specs=[pl.BlockSpec((2, 2, tm, FF_SHARD), lambda i, j: (0, j, i, 0)),
                           pl.BlockSpec((2, FF_SHARD, D_MODEL), lambda i, j: (j, 0, 0)), row, row, vec, vec],
                 out_specs=[row, row, pl.BlockSpec((None, 8, 128), lambda i, j: (i, 0, 0)), vec, vec],
                 out_shape=[_sds((T, D_MODEL), BF16), _sds((T, D_MODEL)), _sds((nb, 8, 128)), _sds((1, D_MODEL)),
                            _sds((1, D_MODEL))],
                 scratch=[pltpu.VMEM((tm, D_MODEL), F32)], sem=('arbitrary', 'arbitrary'),
                 vmem=VMEM_BIG)(hid4, wd4, x1, tgt, gt, g_post)


def _ssm_prep(lre, lim, lst, b_re, b_im):
    def body(lre_ref, lim_ref, lst_ref, br_ref, bi_ref, ar_ref, ai_ref, bbr_ref, bbi_ref):
        ar, ai, qr, qi = _zoh(lre_ref[...], lim_ref[...], lst_ref[...])[:4]
        ar_ref[...] = ar
        ai_ref[...] = ai
        bbr_ref[...] = qr * br_ref[...] - qi * bi_ref[...]
        bbi_ref[...] = qr * bi_ref[...] + qi * br_ref[...]

    shp = lre.shape
    return _call(body, name='ssm_prep', grid=(1,), in_specs=[_const(shp)] * 5, out_specs=[_const(shp)] * 4,
                 out_shape=[_sds(shp)] * 4)(lre, lim, lst, b_re, b_im)


def _zoh(lre, lim, lst):
    lr = jnp.minimum(lre, LAMBDA_RE_MAX)
    st = jnp.exp(lst)
    mag = jnp.exp(lr * st)
    ar = mag * jnp.cos(lim * st)
    ai = mag * jnp.sin(lim * st)
    den = lr * lr + lim * lim
    qr = ((ar - 1.0) * lr + ai * lim) / den
    qi = (ai * lr - (ar - 1.0) * lim) / den
    return ar, ai, qr, qi, lr, st, den


def _ssm_prep_bwd(lre, lim, lst, b_re, b_im, dbbr, dbbi, dar, dai, seg):
    def body(lre_ref, lim_ref, lst_ref, br_ref, bi_ref, dbbr_ref, dbbi_ref, dar_ref, dai_ref, seg_ref,
             dbr_ref, dbi_ref, dlre_ref, dlim_ref, dlst_ref):
        lre_v = lre_ref[...]
        li = lim_ref[...]
        ar, ai, qr, qi, lr, st, den = _zoh(lre_v, li, lst_ref[...])
        br, bi, gbr, gbi = br_ref[...], bi_ref[...], dbbr_ref[...], dbbi_ref[...]
        dbr_ref[...] = qr * gbr + qi * gbi
        dbi_ref[...] = qr * gbi - qi * gbr
        gqr = _dot_split(br * gbr + bi * gbi, seg_ref[...], 3)
        gqi = _dot_split(br * gbi - bi * gbr, seg_ref[...], 3)
        ir, ii = lr / den, -li / den
        gar = dar_ref[...] + ir * gqr + ii * gqi
        gai = dai_ref[...] + ir * gqi - ii * gqr
        tr, ti = qr * ir - qi * ii, qr * ii + qi * ir
        glr = -(tr * gqr + ti * gqi)
        gli = -(tr * gqi - ti * gqr)
        gzr = ar * gar + ai * gai
        gzi = ar * gai - ai * gar
        glr = glr + st * gzr
        gli = gli + st * gzi
        gst = (lr * gzr + li * gzi) * st
        dlre_ref[...] = jnp.where(lre_v < LAMBDA_RE_MAX, glr, 0.0)
        dlim_ref[...] = gli
        dlst_ref[...] = jnp.sum(gst, axis=1, keepdims=True) * (1.0 / SSM_GROUP)

    shp = lre.shape
    return _call(body, name='ssm_prep_bwd', grid=(1,), in_specs=[_const(shp)] * 9 + [_const(seg.shape)],
                 out_specs=[_const(shp)] * 4 + [_const((N_GROUPS, 1))],
                 out_shape=[_sds(shp)] * 4 + [_sds((N_GROUPS, 1))], vmem=VMEM_BIG)(
                     lre, lim, lst, b_re, b_im, dbbr, dbbi, dar, dai, seg)


def _scan_specs(T):
    return dict(
        chan=pl.BlockSpec((T, CHAN_BLOCK), lambda cb: (0, cb)),
        state=pl.BlockSpec((T, STATE_BLOCK), lambda cb: (0, cb)),
        b=pl.BlockSpec((CHAN_BLOCK, STATE_BLOCK), lambda cb: (cb, cb)),
        c=pl.BlockSpec((STATE_BLOCK, CHAN_BLOCK), lambda cb: (cb, cb)),
        lam=pl.BlockSpec((1, STATE_BLOCK), lambda cb: (0, cb)),
    )


def _complex_power(re, im, n):
    out = None
    while True:
        if n & 1:
            out = (re, im) if out is None else (out[0] * re - out[1] * im, out[0] * im + out[1] * re)
        n >>= 1
        if n == 0:
            return out
        re, im = re * re - im * im, 2.0 * re * im


def _rows8(i):
    if isinstance(i, int):
        return pl.ds(i * SUBLANES, SUBLANES)
    return pl.ds(pl.multiple_of(i * SUBLANES, SUBLANES), SUBLANES)


def _scan_loop(n_steps, body, init):
    trips = n_steps // SCAN_UNROLL

    def trip(t, carry):
        for u in range(SCAN_UNROLL):
            carry = body(t * SCAN_UNROLL + u, carry)
        return carry

    carry = lax.fori_loop(0, trips, trip, init)
    for step in range(trips * SCAN_UNROLL, n_steps):
        carry = body(step, carry)
    return carry


def _ssm_fwd(u_perm, b_re, b_im, c_re, c_im, lam_r, lam_i, ride):
    T = u_perm.shape[0]
    ls = T // SUBLANES
    rc = min(512, T)
    sp = _scan_specs(T)

    def body(u_ref, bre_ref, bim_ref, cre_ref, cim_ref, lr_ref, li_ref, so_re_ref, so_im_ref, y_ref, sre_ref, sim_ref):
        for c in range(T // rc):
            rows = pl.ds(c * rc, rc)
            ub = u_ref[rows, :].astype(BF16)
            sre_ref[rows, :] = _dot(ub, bre_ref[...])
            sim_ref[rows, :] = _dot(ub, bim_ref[...])
        shp = (SUBLANES, STATE_BLOCK)
        lr = jnp.broadcast_to(lr_ref[...], shp)
        li = jnp.broadcast_to(li_ref[...], shp)
        zero = jnp.zeros(shp, F32)

        def step(i, carry):
            sr, si = carry
            rows = _rows8(i)
            nr = lr * sr - li * si + sre_ref[rows, :]
            ni = lr * si + li * sr + sim_ref[rows, :]
            sre_ref[rows, :] = nr
            sim_ref[rows, :] = ni
            return nr, ni

        fr, fi = _scan_loop(ls, step, (zero, zero))
        pr, pi_ = _complex_power(lr, li, ls)
        row = lax.broadcasted_iota(jnp.int32, shp, 0)
        ir, ii = zero, zero
        for _ in range(SUBLANES - 1):
            er = fr + pr * ir - pi_ * ii
            ei = fi + pr * ii + pi_ * ir
            ir = jnp.where(row == 0, 0.0, pltpu.roll(er, 1, 0))
            ii = jnp.where(row == 0, 0.0, pltpu.roll(ei, 1, 0))

        def fix(i, carry):
            cr, ci = carry
            rows = _rows8(i)
            nr = lr * cr - li * ci
            ni = lr * ci + li * cr
            sre_ref[rows, :] += nr
            sim_ref[rows, :] += ni
            return nr, ni

        _scan_loop(ls, fix, (ir, ii))
        for c in range(T // rc):
            rows = pl.ds(c * rc, rc)
            s_r, s_i = sre_ref[rows, :].astype(BF16), sim_ref[rows, :].astype(BF16)
            so_re_ref[rows, :] = s_r
            so_im_ref[rows, :] = s_i
            y_ref[rows, :] = _dot(s_r, cre_ref[...]) - _dot(s_i, cim_ref[...])

    return _call(body, name='ssm_fwd', grid=(N_STATE // STATE_BLOCK,),
                 in_specs=[sp['chan'], sp['b'], sp['b'], sp['c'], sp['c'], sp['lam'], sp['lam']],
                 out_specs=[sp['state'], sp['state'], sp['chan']],
                 out_shape=[_sds((T, N_STATE), BF16), _sds((T, N_STATE), BF16), _sds((T, D_SSM))],
                 scratch=[pltpu.VMEM((T, STATE_BLOCK), F32), pltpu.VMEM((T, STATE_BLOCK), F32)],
                 sem=('arbitrary',), vmem=VMEM_MOST, ride=ride)(u_perm, b_re, b_im, c_re, c_im, lam_r, lam_i)


def _ssm_bwd(dy_perm, u_perm, s_re, s_im, b_re, b_im, c_re, c_im, lam_r, lam_i, ride):
    T = u_perm.shape[0]
    ls = T // SUBLANES
    rc = min(512, T)
    sp = _scan_specs(T)
    ncb = N_STATE // STATE_BLOCK

    def body(dy_ref, u_ref, sre_ref, sim_ref, bre_ref, bim_ref, cre_ref, cim_ref, lr_ref, li_ref,
             du_ref, dbr_ref, dbi_ref, dcr_ref, dci_ref, dar_ref, dai_ref, gre_ref, gim_ref):
        shp = (SUBLANES, STATE_BLOCK)
        zero = jnp.zeros(shp, F32)
        tail = pl.ds(T, SUBLANES)
        gre_ref[tail, :] = zero
        gim_ref[tail, :] = zero
        for c in range(T // rc):
            rows = pl.ds(c * rc, rc)
            dyb = dy_ref[rows, :].astype(BF16)
            gre_ref[rows, :] = _dot_nt(dyb, cre_ref[...])
            gim_ref[rows, :] = -_dot_nt(dyb, cim_ref[...])
        lr = jnp.broadcast_to(lr_ref[...], shp)
        li = jnp.broadcast_to(li_ref[...], shp)

        def step(k, carry):
            gr, gi = carry
            rows = _rows8(ls - 1 - k)
            nr = lr * gr + li * gi + gre_ref[rows, :]
            ni = lr * gi - li * gr + gim_ref[rows, :]
            gre_ref[rows, :] = nr
            gim_ref[rows, :] = ni
            return nr, ni

        fr, fi = _scan_loop(ls, step, (zero, zero))
        pr, pi_ = _complex_power(lr, -li, ls)
        row = lax.broadcasted_iota(jnp.int32, shp, 0)
        cr, ci = zero, zero
        for _ in range(SUBLANES - 1):
            er = fr + pr * cr - pi_ * ci
            ei = fi + pr * ci + pi_ * cr
            cr = jnp.where(row == SUBLANES - 1, 0.0, pltpu.roll(er, SUBLANES - 1, 0))
            ci = jnp.where(row == SUBLANES - 1, 0.0, pltpu.roll(ei, SUBLANES - 1, 0))

        def fix(k, carry):
            dr, di = carry
            rows = _rows8(ls - 1 - k)
            dr, di = lr * dr + li * di, lr * di - li * dr
            gre_ref[rows, :] += dr
            gim_ref[rows, :] += di
            return dr, di

        _scan_loop(ls, fix, (cr, ci))

        acc_r = jnp.zeros((1, STATE_BLOCK), F32)
        acc_i = jnp.zeros((1, STATE_BLOCK), F32)
        for c in range(T // rc):
            rows, nxt = pl.ds(c * rc, rc), pl.ds(c * rc + SUBLANES, rc)
            s_r, s_i = sre_ref[rows, :].astype(F32), sim_ref[rows, :].astype(F32)
            g_r, g_i = gre_ref[nxt, :], gim_ref[nxt, :]
            acc_r = acc_r + _colsum(g_r * s_r + g_i * s_i)
            acc_i = acc_i + _colsum(g_i * s_r - g_r * s_i)
        last = pl.ds(T - 2 * SUBLANES, 2 * SUBLANES)
        first = pl.ds(0, SUBLANES)
        spr = jnp.where(row == 0, 0.0, pltpu.roll(sre_ref[last, :].astype(F32)[SUBLANES:], 1, 0))
        spi = jnp.where(row == 0, 0.0, pltpu.roll(sim_ref[last, :].astype(F32)[SUBLANES:], 1, 0))
        gr, gi = gre_ref[first, :], gim_ref[first, :]
        dar_ref[...] = acc_r + _colsum(gr * spr + gi * spi)
        dai_ref[...] = acc_i + _colsum(gi * spr - gr * spi)

        for c in range(T // rc):
            rows = pl.ds(c * rc, rc)
            g_r, g_i = gre_ref[rows, :].astype(BF16), gim_ref[rows, :].astype(BF16)
            s_r, s_i = sre_ref[rows, :], sim_ref[rows, :]
            ub, dyb = u_ref[rows, :].astype(BF16), dy_ref[rows, :].astype(BF16)
            du_ref[rows, :] = _dot_nt(g_r, bre_ref[...]) + _dot_nt(g_i, bim_ref[...])
            parts = (_dot_tn(ub, g_r), _dot_tn(ub, g_i), _dot_tn(s_r, dyb), -_dot_tn(s_i, dyb))
            outs = (dbr_ref, dbi_ref, dcr_ref, dci_ref)
            for o_ref, part in zip(outs, parts):
                if c == 0:
                    o_ref[...] = part
                else:
                    o_ref[...] += part

    blk = lambda r, c: pl.BlockSpec((None, r, c), lambda cb: (cb, 0, 0))
    return _call(body, name='ssm_bwd', grid=(ncb,),
                 in_specs=[sp['chan'], sp['chan'], sp['state'], sp['state'], sp['b'], sp['b'], sp['c'], sp['c'],
                           sp['lam'], sp['lam']],
                 out_specs=[sp['chan'], blk(CHAN_BLOCK, STATE_BLOCK), blk(CHAN_BLOCK, STATE_BLOCK),
                            blk(STATE_BLOCK, CHAN_BLOCK), blk(STATE_BLOCK, CHAN_BLOCK), blk(1, STATE_BLOCK),
                            blk(1, STATE_BLOCK)],
                 out_shape=[_sds((T, D_SSM)), _sds((ncb, CHAN_BLOCK, STATE_BLOCK)), _sds((ncb, CHAN_BLOCK, STATE_BLOCK)),
                            _sds((ncb, STATE_BLOCK, CHAN_BLOCK)), _sds((ncb, STATE_BLOCK, CHAN_BLOCK)),
                            _sds((ncb, 1, STATE_BLOCK)), _sds((ncb, 1, STATE_BLOCK))],
                 scratch=[pltpu.VMEM((T + SUBLANES, STATE_BLOCK), F32), pltpu.VMEM((T + SUBLANES, STATE_BLOCK), F32)],
                 sem=('arbitrary',), vmem=VMEM_MOST, ride=ride)(dy_perm, u_perm, s_re, s_im, b_re, b_im, c_re, c_im,
                                                                lam_r, lam_i)


def _ffn_dact(ddn, wd4, hid4, tm):
    T = ddn.shape[0]
    nb = T // tm

    def body(d_ref, w_ref, hid_ref, o_ref, gw_ref, acc_ref):
        i = pl.program_id(1)
        d = d_ref[...]
        dact = _dot_nt(d, w_ref[...])
        silu, dsilu = _silu_parts(hid_ref[0].astype(F32))
        hid_v = hid_ref[1].astype(F32)
        o_ref[0] = (dact * hid_v * dsilu).astype(BF16)
        o_ref[1] = (dact * silu).astype(BF16)
        part = _dot_tn((silu * hid_v).astype(BF16), d)

        @pl.when(i == 0)
        def _():
            acc_ref[...] = part

        @pl.when(i > 0)
        def _():
            acc_ref[...] += part

        @pl.when(i == nb - 1)
        def _():
            gw_ref[...] = acc_ref[...].astype(BF16)

    blk = pl.BlockSpec((2, None, tm, FF_SHARD), lambda j, i: (0, j, i, 0))
    w_blk = pl.BlockSpec((None, FF_SHARD, D_MODEL), lambda j, i: (j, 0, 0))
    return _call(body, name='ffn_dact', grid=(4, nb),
                 in_specs=[pl.BlockSpec((tm, D_MODEL), lambda j, i: (i, 0)), w_blk, blk],
                 out_specs=[blk, w_blk],
                 out_shape=[_sds((2, 4, T, FF_SHARD), BF16), _sds((4, FF_SHARD, D_MODEL), BF16)],
                 scratch=[pltpu.VMEM((FF_SHARD, D_MODEL), F32)], sem=('parallel', 'arbitrary'),
                 vmem=VMEM_BIG)(ddn, wd4, hid4)


def _ffn_dup(dhid8, up8, cw8, tm, ride):
    T = up8.shape[1]
    nb = T // tm
    ha = _halo_after(tm, T, HALO16)

    def body(dh_ref, dha_ref, up_ref, cw_ref, dup_ref, dcw_ref):
        i = pl.program_id(1)

        @pl.when(i == 0)
        def _():
            dcw_ref[...] = jnp.zeros_like(dcw_ref)

        dh = dh_ref[...].astype(F32)
        dup, dh1, dh2 = _conv3_t(dh, jnp.where(i < nb - 1, dha_ref[...].astype(F32), 0.0), cw_ref)
        dup_ref[...] = dup.astype(BF16)
        up = up_ref[...].astype(F32)
        dcw_ref[0:1, :] += _colsum(dh2 * up)
        dcw_ref[1:2, :] += _colsum(dh1 * up)
        dcw_ref[2:3, :] += _colsum(dh * up)

    main = pl.BlockSpec((None, tm, FF_SHARD), lambda j, i: (j, i, 0))
    return _call(body, name='ffn_dup', grid=(N_DEV, nb),
                 in_specs=[main, pl.BlockSpec((None, HALO16, FF_SHARD), lambda j, i: (j, ha(i), 0)), main,
                           pl.BlockSpec((None, 3, FF_SHARD), lambda j, i: (j, 0, 0))],
                 out_specs=[main, pl.BlockSpec((None, 8, FF_SHARD), lambda j, i: (j, 0, 0))],
                 out_shape=[_sds((N_DEV, T, FF_SHARD), BF16), _sds((N_DEV, 8, FF_SHARD))],
                 sem=('parallel', 'arbitrary'), vmem=VMEM_BIG, ride=ride)(dhid8, dhid8, up8, cw8)


def _grad_tn(a, b, a_spec, b_spec, groups, m, n, tk, name, ride=None, parts=1):
    T = a.shape[-2]
    nk = T // tk
    mp = m // parts

    def body(a_ref, b_ref, *refs):
        o_refs, acc_ref = refs[:parts], refs[parts]
        k = pl.program_id(1)
        part = _dot_tn(a_ref[...], b_ref[...])

        @pl.when(k == 0)
        def _():
            acc_ref[...] = part

        @pl.when(k > 0)
        def _():
            acc_ref[...] += part

        @pl.when(k == nk - 1)
        def _():
            for p, o_ref in enumerate(o_refs):
                o_ref[...] = acc_ref[p * mp:(p + 1) * mp, :].astype(BF16)

    out_spec = pl.BlockSpec((None, mp, n), lambda g, k: (g, 0, 0))
    res = _call(body, name=name, grid=(groups, nk), in_specs=[a_spec, b_spec], out_specs=[out_spec] * parts,
                out_shape=[_sds((groups, mp, n), BF16)] * parts, scratch=[pltpu.VMEM((m, n), F32)],
                sem=('parallel', 'arbitrary'), vmem=VMEM_BIG, ride=ride)(a, b)
    if parts > 1:
        return res
    return res[0] if ride is None else (res[0][0], res[1])


def _grad_w_in(h1, dproj, tk, ride):
    T = h1.shape[0]
    nk = T // tk
    half = D_IN_PROJ // 2

    def body(a_ref, b_ref, o_ref, acc_ref):
        k = pl.program_id(0)
        for h in range(2):
            cols = slice(h * half, (h + 1) * half)
            part = _dot_tn(a_ref[...], b_ref[:, cols])

            @pl.when(k == 0)
            def _():
                acc_ref[:, cols] = part

            @pl.when(k > 0)
            def _():
                acc_ref[:, cols] += part

        @pl.when(k == nk - 1)
        def _():
            for g in range(N_DEV):
                o_ref[g] = acc_ref[:, g * IN_SHARD:(g + 1) * IN_SHARD].astype(BF16)

    return _call(body, name='grad_w_in', grid=(nk,),
                 in_specs=[pl.BlockSpec((tk, D_MODEL), lambda k: (k, 0)), pl.BlockSpec((tk, D_IN_PROJ), lambda k: (k, 0))],
                 out_specs=_const((N_DEV, D_MODEL, IN_SHARD)), out_shape=_sds((N_DEV, D_MODEL, IN_SHARD), BF16),
                 scratch=[pltpu.VMEM((D_MODEL, D_IN_PROJ), F32)], sem=('arbitrary',), vmem=VMEM_BIG, ride=ride)(h1, dproj)


def _pre_norm_bwd(dz, dz_spec, w_parts, xin, dres, sc, g, tm, name, ride, below=None, group=1, w_t=False):
    T = xin.shape[0]
    n = w_parts[0].shape[1] if w_t else w_parts[0].shape[2]
    mul = _dot if w_t else _dot_nt
    steps = N_DEV // group
    width = D_MODEL // len(w_parts)

    def body(dz_ref, *refs):
        w_refs, (x_ref, dr_ref, sc_ref, g_ref), refs = refs[:len(w_parts)], refs[len(w_parts):len(w_parts) + 4], \
            refs[len(w_parts) + 4:]
        if below is None:
            dx_ref, dsh_ref, dsc_ref, dg_ref = refs
            sums = (dsh_ref, dsc_ref, dg_ref)
        else:
            v_ref, gate_ref, g2_ref, dx_ref, dsh_ref, dsc_ref, dg_ref, dv_ref, dgate_ref, dg2_ref = refs
            sums = (dsh_ref, dsc_ref, dg_ref, dgate_ref, dg2_ref)
        i, j = pl.program_id(0), pl.program_id(1)
        piece = (lambda s: dz_ref[s]) if dz.ndim == 3 else (lambda s: dz_ref[:, s * n:(s + 1) * n])
        parts = []
        for w_ref in w_refs:
            part = mul(piece(0), w_ref[0])
            for s in range(1, group):
                part = part + mul(piece(s), w_ref[s])
            parts.append(part)

        @pl.when(jnp.logical_and(i == 0, j == 0))
        def _():
            for s_ref in sums:
                s_ref[...] = jnp.zeros_like(s_ref)

        @pl.when(j == 0)
        def _():
            for k, part in enumerate(parts):
                dx_ref[:, k * width:(k + 1) * width] = part

        @pl.when(j > 0)
        def _():
            for k, part in enumerate(parts):
                dx_ref[:, k * width:(k + 1) * width] += part

        @pl.when(j == steps - 1)
        def _():
            dh, xv, gv = dx_ref[...], x_ref[...], g_ref[...]
            r = _rsqrt_mean(xv)
            dsh_ref[...] += _colsum(dh)
            dsc_ref[...] += _colsum(dh * (xv * r * gv))
            dxn = dh * (1.0 + sc_ref[...])
            dg_ref[...] += _colsum(dxn * xv * r)
            dx = dr_ref[...] + _norm_bwd(dxn, xv, r, gv)
            dx_ref[...] = dx
            if below is not None:
                v, g2 = v_ref[...], g2_ref[...]
                rv = _rsqrt_mean(v)
                dgate_ref[...] += _colsum(dx * (v * rv * g2))
                dn = dx * gate_ref[...]
                dg2_ref[...] += _colsum(dn * v * rv)
                dv_ref[...] = _norm_bwd(dn, v, rv, g2).astype(BF16)

    row = pl.BlockSpec((tm, D_MODEL), lambda i, j: (i, 0))
    vec = _const((1, D_MODEL))
    in_specs = [dz_spec] + [pl.BlockSpec((group,) + w.shape[1:], lambda i, j: (j, 0, 0)) for w in w_parts]
    in_specs += [row, row, vec, vec]
    out_specs = [row, vec, vec, vec]
    out_shape = [_sds((T, D_MODEL)), _sds((1, D_MODEL)), _sds((1, D_MODEL)), _sds((1, D_MODEL))]
    args = [dz, *w_parts, xin, dres, sc, g]
    if below is not None:
        in_specs += [row, vec, vec]
        out_specs += [row, vec, vec]
        out_shape += [_sds((T, D_MODEL), BF16), _sds((1, D_MODEL)), _sds((1, D_MODEL))]
        args += list(below)
    return _call(body, name=name, grid=(T // tm, steps), in_specs=in_specs, out_specs=out_specs,
                 out_shape=out_shape, sem=('arbitrary', 'arbitrary'), vmem=VMEM_MOST, ride=ride)(*args)


def _mix_bwd(d_o, w_out, yssm, proj, d, glu_w, glu_b, g_ssm, cw, g_conv, avg16, avg64, tm, ride):
    T = yssm.shape[0]
    hb = _halo_before(tm)

    def body(do_ref, wo_ref, y_ref, p_ref, ph_ref, d_ref, gw_ref, gb_ref, gs_ref, cw_ref, gc_ref, a16_ref, a64_ref,
             dy_ref, dconv_ref, dbg_ref, z_ref, dlin_ref, acc_ref):
        i = pl.program_id(0)
        dyc = _dot_nt(do_ref[...], wo_ref[...])

        @pl.when(i == 0)
        def _():
            acc_ref[...] = jnp.zeros_like(acc_ref)

        u = p_ref[:, 0:D_SSM]
        y = y_ref[...] + d_ref[...] * u
        z, t = _gelu(y)
        gate = _sigmoid(_dot(z.astype(BF16), gw_ref[...]) + gb_ref[...])
        ya = z * gate
        rs = lax.rsqrt(_dot_split(ya * ya, a16_ref[...], 2) + EPS)
        dna = dyc[:, 0:D_SSM]
        acc_ref[1:2, :] += _colsum(dna * ya * rs)
        dya = _head_norm_bwd(dna, ya, rs, gs_ref[...], a16_ref[...])
        dlin = dya * z * gate * (1.0 - gate)
        acc_ref[0:1, :] += _colsum(dlin)
        dlin_b = dlin.astype(BF16)
        dz = dya * gate + _dot_nt(dlin_b, gw_ref[...])
        dy = dz * _gelu_grad(y, t)
        acc_ref[3:4, :] += _colsum(dy * u)
        dy_ref[...] = dy
        z_ref[...] = z.astype(BF16)
        dlin_ref[...] = dlin_b

        bg = p_ref[:, D_SSM:D_SSM + D_CONV]
        cv = p_ref[:, D_SSM + D_CONV:D_SSM + 2 * D_CONV] * p_ref[:, D_SSM + 2 * D_CONV:D_IN_PROJ]
        hv = ph_ref[:, D_SSM + D_CONV:D_SSM + 2 * D_CONV] * ph_ref[:, D_SSM + 2 * D_CONV:D_IN_PROJ]
        hv = jnp.where(i > 0, hv, 0.0)
        conv, cv1, cv2 = _conv3(cv, hv, cw_ref)
        yb = bg * conv
        rsb = lax.rsqrt(_dot_split(yb * yb, a64_ref[...], 2) + EPS)
        dnb = dyc[:, D_SSM:D_MODEL]
        acc_ref[2:3, :] += _colsum(dnb * yb * rsb)
        dyb = _head_norm_bwd(dnb, yb, rsb, gc_ref[...], a64_ref[...])
        dbg_ref[...] = dyb * conv
        dconv = dyb * bg
        dconv_ref[...] = dconv
        acc_ref[4:5, :] += _colsum(dconv * cv2)
        acc_ref[5:6, :] += _colsum(dconv * cv1)
        acc_ref[6:7, :] += _colsum(dconv * cv)

    vec = _const((1, D_SSM))
    sq = _const((D_SSM, D_SSM))
    half = pl.BlockSpec((tm, D_SSM), lambda i: (i, 0))
    return _call(body, name='mix_bwd', grid=(T // tm,),
                 in_specs=[pl.BlockSpec((tm, D_MODEL), lambda i: (i, 0)), _const((D_MODEL, D_MODEL)), half,
                           pl.BlockSpec((tm, D_IN_PROJ), lambda i: (i, 0)),
                           pl.BlockSpec((HALO, D_IN_PROJ), lambda i: (hb(i), 0)), vec, sq, vec, vec,
                           _const((3, D_CONV)), vec, sq, sq],
                 out_specs=[half, half, half, half, half, _const((8, D_SSM))],
                 out_shape=[_sds((T, D_SSM)), _sds((T, D_SSM)), _sds((T, D_SSM)), _sds((T, D_SSM), BF16),
                            _sds((T, D_SSM), BF16), _sds((8, D_SSM))],
                 sem=('arbitrary',), vmem=VMEM_BIG, ride=ride)(d_o, w_out, yssm, proj, proj, d, glu_w, glu_b, g_ssm, cw,
                                                              g_conv, avg16, avg64)


def _mix_bwd_proj(dconv, proj, du_ssm, dy, d, dbg, cw, tm):
    T = dy.shape[0]
    nb = T // tm
    ha = _halo_after(tm, T)

    def body(dc_ref, dch_ref, cg_ref, v_ref, du_ref, dy_ref, d_ref, dbg_ref, cw_ref, o_ref):
        i = pl.program_id(0)
        dcv = _conv3_t(dc_ref[...], jnp.where(i < nb - 1, dch_ref[...], 0.0), cw_ref)[0]
        o_ref[:, 0:D_SSM] = (du_ref[...] + dy_ref[...] * d_ref[...]).astype(BF16)
        o_ref[:, D_SSM:D_SSM + D_CONV] = dbg_ref[...].astype(BF16)
        o_ref[:, D_SSM + D_CONV:D_SSM + 2 * D_CONV] = (dcv * v_ref[...]).astype(BF16)
        o_ref[:, D_SSM + 2 * D_CONV:D_IN_PROJ] = (dcv * cg_ref[...]).astype(BF16)

    half = pl.BlockSpec((tm, D_SSM), lambda i: (i, 0))
    return _call(body, name='mix_bwd_proj', grid=(nb,),
                 in_specs=[half, pl.BlockSpec((HALO, D_CONV), lambda i: (ha(i), 0)),
                           pl.BlockSpec((tm, D_CONV), lambda i: (i, 2)), pl.BlockSpec((tm, D_CONV), lambda i: (i, 3)),
                           half, half, _const((1, D_SSM)), half, _const((3, D_CONV))],
                 out_specs=pl.BlockSpec((tm, D_IN_PROJ), lambda i: (i, 0)), out_shape=_sds((T, D_IN_PROJ), BF16),
                 sem=('parallel',), vmem=VMEM_BIG)(dconv, dconv, proj, proj, du_ssm, dy, d, dbg, cw)


ADAMW_SLOT_BYTES = 8 << 20
ADAMW_ROW_BYTES = 3 << 19


def _row_tile(rows, cols, slots):
    for cand in range(rows, 15, -1):
        if (rows % cand == 0 and cand % 16 == 0 and slots * cand * cols * 4 <= ADAMW_SLOT_BYTES
                and cand * cols * 4 <= ADAMW_ROW_BYTES):
            return cand
    return rows


def _adamw_math(g, w, m, v):
    m2 = ADAM_B1 * m + (1.0 - ADAM_B1) * g
    v2 = ADAM_B2 * v + (1.0 - ADAM_B2) * (g * g)
    m_hat = m2 / (1.0 - ADAM_B1 ** ADAM_STEP)
    v_hat = v2 / (1.0 - ADAM_B2 ** ADAM_STEP)
    return -ADAM_LR * (m_hat / (jnp.sqrt(v_hat) + ADAM_EPS) + ADAM_WD * w), m2, v2


def _adamw(pieces, w, m, v, name):
    slots, _, cols = pieces[0].shape
    rows = sum(p.shape[1] for p in pieces)
    tr = _row_tile(pieces[0].shape[1], cols, slots)
    starts, pos = [], 0
    for p in pieces:
        assert p.shape[1] % tr == 0
        starts.append(pos)
        pos += p.shape[1] // tr

    def body(*refs):
        g_refs = refs[:len(pieces)]
        w_ref, m_ref, v_ref, go_ref, d_ref, mo_ref, vo_ref = refs[len(pieces):]
        i = pl.program_id(0)
        g = None
        for g_ref, start in zip(g_refs, starts):
            part = g_ref[0].astype(F32)
            for s in range(1, slots):
                part = part + g_ref[s].astype(F32)
            g = part if g is None else jnp.where(i >= start, part, g)
        go_ref[...] = g
        d_ref[...], mo_ref[...], vo_ref[...] = _adamw_math(g, w_ref[...], m_ref[...], v_ref[...])

    def piece_spec(start, count):
        return pl.BlockSpec((slots, tr, cols), lambda i: (0, jnp.clip(i - start, 0, count - 1), 0))

    blk = pl.BlockSpec((tr, cols), lambda i: (i, 0))
    return _call(body, name=name, grid=(rows // tr,),
                 in_specs=[piece_spec(s, p.shape[1] // tr) for s, p in zip(starts, pieces)] + [blk, blk, blk],
                 out_specs=[blk] * 4, out_shape=[_sds((rows, cols))] * 4, sem=('parallel',),
                 vmem=VMEM_BIG)(*pieces, w, m, v)


def _to_scan_rows(a):
    T, n = a.shape
    return a.reshape(SUBLANES, T // SUBLANES, n).transpose(1, 0, 2).reshape(T, n)


def _from_scan_rows(a):
    T, n = a.shape
    return a.reshape(T // SUBLANES, SUBLANES, n).transpose(1, 0, 2).reshape(T, n)


def _expand(a):
    return jnp.repeat(a, SSM_GROUP, axis=1)


def _block_diag(rows, row_group, col_group):
    r, n = rows.shape
    tiled = jnp.tile(rows, (1, N_GROUPS))
    keep = (jnp.arange(r)[:, None] // row_group) == (jnp.arange(n * N_GROUPS)[None, :] // col_group)
    return jnp.where(keep, tiled, 0.0)


def _block_diag_b(bb):
    return _block_diag(bb.transpose(0, 2, 1).reshape(D_SSM, SSM_STATE), SSM_GROUP, SSM_STATE)


def _block_diag_c(cc):
    return _block_diag(cc.transpose(0, 2, 1).reshape(N_STATE, SSM_GROUP), SSM_STATE, SSM_GROUP)


def _diag_blocks(x, chan_major):
    per = CHAN_BLOCK // SSM_GROUP
    eye = jnp.eye(per, dtype=x.dtype)
    if chan_major:
        x = x.reshape(-1, per, SSM_GROUP, per, SSM_STATE) * eye[None, :, None, :, None]
        return x.sum(axis=1).transpose(0, 2, 3, 1).reshape(N_GROUPS, SSM_STATE, SSM_GROUP)
    x = x.reshape(-1, per, SSM_STATE, per, SSM_GROUP) * eye[None, :, None, :, None]
    return x.sum(axis=3).reshape(N_GROUPS, SSM_STATE, SSM_GROUP)


SMALL_LAYOUT = {
    'ssm_b_re': (0, 0, 32, 1024), 'ssm_b_im': (32, 0, 32, 1024), 'ssm_c_re': (64, 0, 32, 1024),
    'ssm_c_im': (96, 0, 32, 1024), 'b_ada': (128, 0, 6, 1024), 'g_pre_mix': (134, 0, 1, 1024),
    'g_post_mix': (135, 0, 1, 1024), 'ssm_lam_re': (136, 0, 2, 1024), 'ssm_lam_im': (138, 0, 2, 1024),
    'ssm_log_step': (140, 0, 1, 32), 'glu_b': (141, 0, 1, 512), 'g_out_ssm': (141, 512, 1, 512),
    'g_out_conv': (142, 0, 1, 512), 'ssm_d': (142, 512, 1, 512), 'g_pre_ffn': (143, 0, 1, 1024),
    'g_post_ffn': (144, 0, 1, 1024)}
SMALL_ROWS = 152
B_ADA_ROW = SMALL_LAYOUT['b_ada'][0]
LATE_ROWS = {('b_ada', 0): 0, ('b_ada', 1): 1, ('g_pre_mix', 0): 2}


def _adamw_small(gathered, late, wts, mom_m, mom_v):
    names = list(SMALL_LAYOUT)
    n = len(names)

    def body(*refs):
        g_ref, late_ref, ins, outs = refs[0], refs[1], refs[2:2 + 3 * n], refs[2 + 3 * n:]
        for p, name in enumerate(names):
            r0, c0, rows, cols = SMALL_LAYOUT[name]
            pieces = [(0, rows)] if rows % 8 == 0 else [(r, 1) for r in range(rows)]
            for r, cnt in pieces:
                src_ref, first = (late_ref, LATE_ROWS[name, r]) if (name, r) in LATE_ROWS else (g_ref, r0 + r)
                g = src_ref[0, first:first + cnt, c0:c0 + cols]
                for s in range(1, N_DEV):
                    g = g + src_ref[s, first:first + cnt, c0:c0 + cols]
                w, m, v = (ins[3 * p + q][r:r + cnt, :] for q in range(3))
                res = (g,) + _adamw_math(g, w, m, v)
                for q in range(4):
                    outs[4 * p + q][r:r + cnt, :] = res[q]

    shapes = [SMALL_LAYOUT[name][2:] for name in names]
    args = [gathered, late]
    for name, shp in zip(names, shapes):
        args += [wts[name].reshape(shp), mom_m[name].reshape(shp), mom_v[name].reshape(shp)]
    outs = _call(body, name='adamw_small', grid=(1,),
                 in_specs=[_const(gathered.shape), _const(late.shape)]
                 + [_const(shp) for shp in shapes for _ in range(3)],
                 out_specs=[_const(shp) for shp in shapes for _ in range(4)],
                 out_shape=[_sds(shp) for shp in shapes for _ in range(4)], vmem=VMEM_BIG)(*args)
    res = {}
    for p, name in enumerate(names):
        for q, kind in enumerate(('g', 'd', 'm', 'v')):
            res[kind, name] = outs[4 * p + q].reshape(wts[name].shape)
    return res


def kernel(x, c, w_ada, b_ada, g_pre_mix, g_post_mix, w_in, ssm_lam_re, ssm_lam_im, ssm_log_step, ssm_b_re, ssm_b_im, ssm_c_re, ssm_c_im, ssm_d, glu_w, glu_b, g_out_ssm, conv_w, g_out_conv, w_out, g_pre_ffn, g_post_ffn, w_up, ffn_conv_w, w_down, loss_target, m_w_ada, m_b_ada, m_g_pre_mix, m_g_post_mix, m_w_in, m_ssm_lam_re, m_ssm_lam_im, m_ssm_log_step, m_ssm_b_re, m_ssm_b_im, m_ssm_c_re, m_ssm_c_im, m_ssm_d, m_glu_w, m_glu_b, m_g_out_ssm, m_conv_w, m_g_out_conv, m_w_out, m_g_pre_ffn, m_g_post_ffn, m_w_up, m_ffn_conv_w, m_w_down, v_w_ada, v_b_ada, v_g_pre_mix, v_g_post_mix, v_w_in, v_ssm_lam_re, v_ssm_lam_im, v_ssm_log_step, v_ssm_b_re, v_ssm_b_im, v_ssm_c_re, v_ssm_c_im, v_ssm_d, v_glu_w, v_glu_b, v_g_out_ssm, v_conv_w, v_g_out_conv, v_w_out, v_g_pre_ffn, v_g_post_ffn, v_w_up, v_ffn_conv_w, v_w_down):
    args = dict(locals())
    wts = {n: args[n] for n in WEIGHTS}
    mom_m = {n: args['m_' + n] for n in WEIGHTS}
    mom_v = {n: args['v_' + n] for n in WEIGHTS}
    T = x.shape[1]
    tm = min(512, T)
    tw = min(1024, T)
    tk = min(2048, T)
    me = _me()[3]
    xt, tgt = x[0], loss_target[0]

    c_all, w_in_s = _exchange([c, w_in[0].astype(BF16)], name='gather_first', scatter=False)
    c_all = c_all.reshape(N_DEV, D_MODEL)
    b_cols = lax.dynamic_slice(b_ada, (0, me * ADA_SHARD), (1, ADA_SHARD))
    mod_cols, c_act = _mod_cols(c_all, w_ada[0], b_cols)
    (mod_all,) = _exchange([mod_cols], name='gather_mod', scatter=False)
    mod = lax.dynamic_slice(mod_all, (0, me, 0), (N_DEV, 1, ADA_SHARD)).reshape(N_MOD, 1, D_MODEL)
    sh1, sc1, gt1, sh2, sc2, gt2 = [mod[k] for k in range(N_MOD)]


    lre_x, lim_x = _expand(ssm_lam_re[0]), _expand(ssm_lam_im[0])
    lst_x = jnp.broadcast_to(ssm_log_step[0][:, None], (N_GROUPS, SSM_STATE * SSM_GROUP))
    b_re_x = ssm_b_re[0].reshape(N_GROUPS, -1)
    b_im_x = ssm_b_im[0].reshape(N_GROUPS, -1)
    ar_x, ai_x, bbr_x, bbi_x = _ssm_prep(lre_x, lim_x, lst_x, b_re_x, b_im_x)
    lam_r = ar_x[:, ::SSM_GROUP].reshape(1, N_STATE)
    lam_i = ai_x[:, ::SSM_GROUP].reshape(1, N_STATE)
    big_b_re = _block_diag_b(bbr_x.reshape(N_GROUPS, SSM_STATE, SSM_GROUP)).astype(BF16)
    big_b_im = _block_diag_b(bbi_x.reshape(N_GROUPS, SSM_STATE, SSM_GROUP)).astype(BF16)
    big_c_re = _block_diag_c(ssm_c_re[0]).astype(BF16)
    big_c_im = _block_diag_c(ssm_c_im[0]).astype(BF16)
    head = jnp.arange(D_SSM)
    avg16 = jnp.where(head[:, None] // SSM_GROUP == head[None, :] // SSM_GROUP, 1.0 / SSM_GROUP, 0.0).astype(BF16)
    hd = D_CONV // CONV_HEADS
    avg64 = jnp.where(head[:, None] // hd == head[None, :] // hd, 1.0 / hd, 0.0).astype(BF16)

    w_up_t, half = w_up[0].T, D_MODEL // 2
    (proj, h1), (ffn_conv_s, glu_s, w_out_s, conv_s, w_up_a) = _pre_mix(
        xt, sc1, sh1, g_pre_mix, w_in_s, tw,
        ([ffn_conv_w[0], glu_w[0].astype(BF16), w_out[0].astype(BF16), conv_w[0], w_up_t[:, :half].astype(BF16)],
         False))
    glu_full = glu_s.reshape(D_SSM, D_SSM)
    w_out_full = w_out_s.reshape(D_MODEL, D_MODEL)
    cw_full = conv_s.transpose(1, 0, 2).reshape(3, D_CONV)
    u_perm = _to_scan_rows(proj[:, :D_SSM])
    (s_re, s_im, y_perm), (w_up_b,) = _ssm_fwd(u_perm, big_b_re, big_b_im, big_c_re, big_c_im, lam_r, lam_i,
                                               ([w_up_t[:, half:].astype(BF16)], False))
    yssm = _from_scan_rows(y_perm)
    mix_args = (ssm_d, glu_full, glu_b, g_out_ssm, cw_full, g_out_conv, avg16, avg64)
    ycat = _mix_fwd(yssm, proj, *mix_args, tw)
    o, x1, h2 = _out_proj(ycat, w_out_full, xt, gt1, g_post_mix, g_pre_ffn, sc2, sh2, tw)
    (up8, hid8), (w_down_s,) = _ffn_up(h2, w_up_a, w_up_b, ffn_conv_s, tw, ([w_down[0].astype(BF16)], False))
    wd4 = w_down_s.reshape(4, FF_SHARD, D_MODEL)
    hid4 = hid8.reshape(2, 4, T, FF_SHARD)
    ddn, dx2, loss_parts, d_gt2, d_g_post_ffn = _ffn_down(hid4, wd4, x1, tgt, gt2, g_post_ffn, tm)
    loss_local = jnp.sum(loss_parts[:, 0, 0])

    got = {}
    dhid, g_w_down = _ffn_dact(ddn, wd4, hid4, tw)
    (dup8, dcw_ffn), (got['w_down'],) = _ffn_dup(dhid.reshape(N_DEV, T, FF_SHARD), up8, ffn_conv_s, tw,
                                                 ([g_w_down.reshape(N_DEV, D_FF // N_DEV, D_MODEL)], True))
    g_w_up_halves = _grad_tn(dup8, h2, pl.BlockSpec((None, tk, FF_SHARD), lambda g, k: (g, k, 0)),
                             pl.BlockSpec((tk, D_MODEL), lambda g, k: (k, 0)), N_DEV, FF_SHARD, D_MODEL, tk,
                             'grad_w_up', parts=2)
    (dx1, d_sh2, d_sc2, d_g_pre_ffn, d_o, d_gt1, d_g_post_mix), (got_up_0, got['ffn_conv_w']) = _pre_norm_bwd(
        dup8, pl.BlockSpec((2, tw, FF_SHARD), lambda i, j: (j, i, 0)), [w_up_a, w_up_b], x1, dx2, sc2, g_pre_ffn, tw,
        'ffn_in_bwd', ([g_w_up_halves[0], dcw_ffn], True), below=(o, gt1, g_post_mix), group=2, w_t=True)

    g_w_out = _grad_tn(ycat, d_o, pl.BlockSpec((tk, D_MODEL), lambda g, k: (k, 0)),
                       pl.BlockSpec((tk, D_MODEL), lambda g, k: (k, 0)), 1, D_MODEL, D_MODEL, tk, 'grad_w_out')
    (dy, dconv, dbg, z_b, dlin_b, sums), (got['w_out'],) = _mix_bwd(
        d_o, w_out_full, yssm, proj, *mix_args, tm, ([g_w_out.reshape(N_DEV, D_MODEL // N_DEV, D_MODEL)], True))
    g_glu_w = _grad_tn(z_b, dlin_b, pl.BlockSpec((tk, D_SSM), lambda g, k: (k, 0)),
                       pl.BlockSpec((tk, D_SSM), lambda g, k: (k, 0)), 1, D_SSM, D_SSM, tk, 'grad_glu_w')
    dy_perm = _to_scan_rows(dy)
    (du_perm, dbr_blk, dbi_blk, dcr_blk, dci_blk, dar_blk, dai_blk), (got_up_1, got['glu_w']) = _ssm_bwd(
        dy_perm, u_perm, s_re, s_im, big_b_re, big_b_im, big_c_re, big_c_im, lam_r, lam_i,
        ([g_w_up_halves[1], g_glu_w.reshape(N_DEV, D_SSM // N_DEV, D_SSM)], True))
    du_ssm = _from_scan_rows(du_perm)
    dproj = _mix_bwd_proj(dconv, proj, du_ssm, dy, ssm_d, dbg, cw_full, tw)
    dbb_re = _diag_blocks(dbr_blk, True).reshape(N_GROUPS, -1)
    dbb_im = _diag_blocks(dbi_blk, True).reshape(N_GROUPS, -1)
    d_c_re = _diag_blocks(dcr_blk, False).transpose(0, 2, 1)
    d_c_im = _diag_blocks(dci_blk, False).transpose(0, 2, 1)
    lane = jnp.arange(SSM_STATE * SSM_GROUP)
    seg = jnp.where(lane[:, None] // SSM_GROUP == lane[None, :] // SSM_GROUP, 1.0, 0.0).astype(BF16)
    d_b_re_x, d_b_im_x, d_lre_x, d_lim_x, d_lst = _ssm_prep_bwd(
        lre_x, lim_x, lst_x, b_re_x, b_im_x, dbb_re, dbb_im, _expand(dar_blk.reshape(N_GROUPS, SSM_STATE)),
        _expand(dai_blk.reshape(N_GROUPS, SSM_STATE)), seg)

    row = lambda a: a.reshape(-1, PACK_COLS)
    blank = jnp.zeros((1, PACK_COLS), F32)
    small_pack = jnp.concatenate([
        d_b_re_x, d_b_im_x, row(d_c_re), row(d_c_im), blank, blank, d_gt1, d_sh2, d_sc2, d_gt2, blank,
        d_g_post_mix, row(d_lre_x[:, ::SSM_GROUP]), row(d_lim_x[:, ::SSM_GROUP]),
        jnp.pad(d_lst.reshape(1, N_GROUPS), ((0, 0), (0, PACK_COLS - N_GROUPS))), row(sums[0:4]), d_g_pre_ffn,
        d_g_post_ffn, jnp.zeros((SMALL_ROWS - 145, PACK_COLS), F32)])
    g_w_in, (small_all,) = _grad_w_in(h1, dproj, tk, ([small_pack], False))
    g_conv_slots = jnp.concatenate([sums[4:7], jnp.zeros((5, D_CONV), F32)]).reshape(
        8, N_DEV, D_CONV // N_DEV).transpose(1, 0, 2)
    (grad_x, d_sh1, d_sc1, d_g_pre_mix), (got['w_in'], got['conv_w']) = _pre_norm_bwd(
        dproj, pl.BlockSpec((tw, D_IN_PROJ), lambda i, j: (i, j)), [w_in_s], xt, dx1, sc1, g_pre_mix, tw,
        'mix_in_bwd', ([g_w_in, g_conv_slots], True), group=N_DEV)
    late_pack = jnp.concatenate([d_sh1, d_sc1, d_g_pre_mix, jnp.full((1, PACK_COLS), loss_local, F32),
                                 jnp.zeros((4, PACK_COLS), F32)])
    (late_all,) = _exchange([late_pack], name='gather_late_grads', scatter=False)
    loss = jnp.sum(late_all[:, 3, 0])
    res = _adamw_small(small_all, late_all, wts, mom_m, mom_v)

    dmod_all = jnp.concatenate([late_all[:, 0:2, :], small_all[:, B_ADA_ROW + 2:B_ADA_ROW + N_MOD, :]],
                               axis=1).reshape(N_DEV, N_MOD * D_MODEL)
    dmod_cols = lax.dynamic_slice(dmod_all, (0, me * ADA_SHARD), (N_DEV, ADA_SHARD))
    g_w_ada = _grad_w_ada(c_act.T, dmod_cols)

    pieces = {n: [slots[:, :3, :] if n in ('conv_w', 'ffn_conv_w') else slots] for n, slots in got.items()}
    for n, parts in pieces.items():
        outs = _adamw(parts, wts[n][0], mom_m[n][0], mom_v[n][0], 'adamw_' + n)
        for kind, val in zip(('g', 'd', 'm', 'v'), outs):
            res[kind, n] = val[None]
    outs = _adamw([got_up_0, got_up_1], w_up[0].T, m_w_up[0].T, v_w_up[0].T, 'adamw_w_up')
    for kind, val in zip(('g', 'd', 'm', 'v'), outs):
        res[kind, 'w_up'] = val.T[None]
    outs = _adamw([g_w_ada[None]], w_ada[0], m_w_ada[0], v_w_ada[0], 'adamw_w_ada')
    for kind, val in zip(('g', 'd', 'm', 'v'), outs):
        res[kind, 'w_ada'] = val[None]

    return (loss, grad_x[None], *[res['g', n] for n in WEIGHTS], *[res['d', n] for n in WEIGHTS],
            *[res['m', n] for n in WEIGHTS], *[res['v', n] for n in WEIGHTS])
```

```python
import math

import jax
import jax.numpy as jnp
from jax import lax
from jax.experimental import pallas as pl
from jax.experimental.pallas import tpu as pltpu

F32, BF16 = jnp.float32, jnp.bfloat16

D_MODEL = 1024
D_SSM = 512
D_CONV = 512
SSM_GROUP = 16
N_GROUPS = 32
SSM_STATE = 64
N_STATE = N_GROUPS * SSM_STATE
CONV_HEADS = 8
D_FF = 2816
N_MOD = 6
D_IN_PROJ = D_SSM + 3 * D_CONV
N_DEV = 8
FF_SHARD = 2 * D_FF // N_DEV
IN_SHARD = D_IN_PROJ // N_DEV
ADA_SHARD = N_MOD * D_MODEL // N_DEV
EPS = 1e-6
LAMBDA_RE_MAX = -1e-4
ADAM_LR, ADAM_B1, ADAM_B2, ADAM_EPS, ADAM_WD, ADAM_STEP = 0.001, 0.9, 0.999, 1e-08, 0.01, 10
GELU_C = math.sqrt(2.0 / math.pi)
GELU_A = 0.044715

SUBLANES = 8
HALO = 8
HALO16 = 16
SCAN_UNROLL = 8
STATE_BLOCK = 512
CHAN_BLOCK = 128
VMEM_BIG = 48 << 20
VMEM_MOST = 58 << 20

WEIGHTS = ['w_ada', 'b_ada', 'g_pre_mix', 'g_post_mix', 'w_in', 'ssm_lam_re', 'ssm_lam_im', 'ssm_log_step',
           'ssm_b_re', 'ssm_b_im', 'ssm_c_re', 'ssm_c_im', 'ssm_d', 'glu_w', 'glu_b', 'g_out_ssm', 'conv_w',
           'g_out_conv', 'w_out', 'g_pre_ffn', 'g_post_ffn', 'w_up', 'ffn_conv_w', 'w_down']
PACK_COLS = 1024


def _call(body, *, name, grid, in_specs, out_specs, out_shape, scratch=(), sem=None, vmem=None, ride=None):
    params = {}
    if vmem is not None:
        params['vmem_limit_bytes'] = vmem
    if ride is None:
        if sem is not None:
            params['dimension_semantics'] = sem
        return pl.pallas_call(body, name=name, grid=grid, in_specs=in_specs, out_specs=out_specs,
                              out_shape=out_shape, scratch_shapes=list(scratch),
                              compiler_params=pltpu.CompilerParams(**params))
    arrs, scatter = ride
    single = not isinstance(out_shape, (list, tuple))
    out_shape_l = [out_shape] if single else list(out_shape)
    out_specs_l = [out_specs] if single else list(out_specs)
    n, n_in, n_out, n_scr = len(arrs), len(in_specs), len(out_shape_l), len(scratch)
    any_spec = pl.BlockSpec(memory_space=pl.ANY)
    params['dimension_semantics'] = ('arbitrary',) * len(grid)

    def carried(*refs):
        ins, rin = refs[:n_in], refs[n_in:n_in + n]
        outs, rout = refs[n_in + n:n_in + n + n_out], refs[n_in + n + n_out:n_in + 2 * n + n_out]
        scr, sems = refs[n_in + 2 * n + n_out:n_in + 2 * n + n_out + n_scr], refs[n_in + 2 * n + n_out + n_scr:]
        first = pl.program_id(0) == 0
        last = pl.program_id(0) == grid[0] - 1
        for ax in range(1, len(grid)):
            first = jnp.logical_and(first, pl.program_id(ax) == 0)
            last = jnp.logical_and(last, pl.program_id(ax) == grid[ax] - 1)

        @pl.when(first)
        def _():
            _exchange_start(rin, rout, sems, scatter)

        body(*ins, *outs, *scr)

        @pl.when(last)
        def _():
            _exchange_wait(rin, rout, sems, scatter)

    call = pl.pallas_call(carried, name=name, grid=grid, in_specs=list(in_specs) + [any_spec] * n,
                          out_specs=out_specs_l + [any_spec] * n,
                          out_shape=out_shape_l + _exchange_shapes(arrs, scatter),
                          scratch_shapes=list(scratch) + _exchange_sems(n),
                          compiler_params=pltpu.CompilerParams(**params))

    def run(*args):
        res = call(*args, *arrs)
        own = res[0] if single else list(res[:n_out])
        return own, list(res[n_out:])

    return run


def _const(shape):
    nd = len(shape)
    return pl.BlockSpec(shape, lambda *_: (0,) * nd)


def _sds(shape, dtype=F32):
    return jax.ShapeDtypeStruct(shape, dtype)


def _dot(a, b):
    return jnp.dot(a, b, preferred_element_type=F32)


def _dot_nt(a, b):
    return lax.dot_general(a, b, (((1,), (1,)), ((), ())), preferred_element_type=F32)


def _dot_tn(a, b):
    return lax.dot_general(a, b, (((0,), (0,)), ((), ())), preferred_element_type=F32)


def _dot_split(x, mat, parts):
    acc = None
    rem = x
    for _ in range(parts):
        piece = rem.astype(BF16)
        rem = rem - piece.astype(F32)
        term = _dot(piece, mat)
        acc = term if acc is None else acc + term
    return acc


def _sigmoid(x):
    return 1.0 / (1.0 + jnp.exp(-x))


def _gelu(x):
    t = jnp.tanh(GELU_C * (x + GELU_A * x * x * x))
    return 0.5 * x * (1.0 + t), t


def _gelu_grad(x, t):
    return 0.5 * (1.0 + t) + 0.5 * x * (1.0 - t * t) * GELU_C * (1.0 + 3.0 * GELU_A * x * x)


def _rsqrt_mean(x):
    return lax.rsqrt(jnp.mean(x * x, axis=-1, keepdims=True) + EPS)


def _colsum(x):
    return jnp.sum(x, axis=0, keepdims=True)


def _shifts_down(x, halo):
    ext = jnp.concatenate([halo, x], axis=0)
    return pltpu.roll(ext, 1, 0)[halo.shape[0]:], pltpu.roll(ext, 2, 0)[halo.shape[0]:]


def _shifts_up(x, halo):
    n = x.shape[0]
    ext = jnp.concatenate([x, halo], axis=0)
    total = ext.shape[0]
    return pltpu.roll(ext, total - 1, 0)[:n], pltpu.roll(ext, total - 2, 0)[:n]


def _conv3(x, halo, w_ref):
    x1, x2 = _shifts_down(x, halo)
    return w_ref[0:1, :] * x2 + w_ref[1:2, :] * x1 + w_ref[2:3, :] * x, x1, x2


def _conv3_t(g, halo, w_ref):
    g1, g2 = _shifts_up(g, halo)
    return w_ref[2:3, :] * g + w_ref[1:2, :] * g1 + w_ref[0:1, :] * g2, g1, g2


def _silu_parts(x):
    s = _sigmoid(x)
    return x * s, s * (1.0 + x * (1.0 - s))


def _norm_bwd(dn, x, r, g):
    gd = g * dn
    return r * gd - x * (r * r * r) * jnp.mean(gd * x, axis=-1, keepdims=True)


def _head_norm_bwd(dn, y, rs, g, avg):
    gd = g * dn
    return rs * gd - y * (rs * rs * rs) * _dot_split(gd * y, avg, 2)


def _me():
    x, y, c = lax.axis_index('x'), lax.axis_index('y'), lax.axis_index('c')
    return x, y, c, 4 * x + 2 * y + c


def _peer(k):
    x, y, c, _ = _me()
    px = 1 - x if k & 4 else x
    py = 1 - y if k & 2 else y
    pc = 1 - c if k & 1 else c
    return (px, py, pc), 4 * px + 2 * py + pc


SIBLING = 1
OTHER_CHIPS = (2, 4, 6)


def _remote(src, dst, sems, a, k, dev):
    return pltpu.make_async_remote_copy(src_ref=src, dst_ref=dst, send_sem=sems[0].at[a, k - 1],
                                        recv_sem=sems[1].at[a, k - 1], device_id=dev,
                                        device_id_type=pl.DeviceIdType.MESH)


def _exchange_copies(ins, outs, sems, scatter):
    me = _me()[3]
    local, first, relay, arrivals = [], [], [], []
    for a in range(len(ins)):
        src = ins[a].at[me] if scatter else ins[a]
        local.append(pltpu.make_async_copy(src, outs[a].at[me], sems[2].at[a]))
        for k in range(1, N_DEV):
            dev, idx = _peer(k)
            landed = _remote(src, outs[a].at[idx], sems, a, k, dev)
            if scatter:
                first.append(_remote(ins[a].at[idx], outs[a].at[me], sems, a, k, dev))
                arrivals.append(landed)
            elif k == SIBLING:
                first.append(_remote(src, outs[a].at[me], sems, a, k, dev))
                arrivals.append(landed)
            elif k in OTHER_CHIPS:
                first.append(_remote(src, outs[a].at[me], sems, a, k, dev))
                sib, _ = _peer(SIBLING)
                relay.append((landed, _remote(outs[a].at[idx], outs[a].at[idx], sems, a, k | SIBLING, sib)))
            else:
                arrivals.append(landed)
    return local, first, relay, arrivals


def _exchange_start(ins, outs, sems, scatter):
    local, first, _, _ = _exchange_copies(ins, outs, sems, scatter)
    for cp in local + first:
        cp.start()


def _exchange_wait(ins, outs, sems, scatter):
    local, first, relay, arrivals = _exchange_copies(ins, outs, sems, scatter)
    for landed, forward in relay:
        landed.wait_recv()
        forward.start()
    for cp in arrivals:
        cp.wait_recv()
    for cp in first + [forward for _, forward in relay]:
        cp.wait_send()
    for cp in local:
        cp.wait()


def _exchange_shapes(arrs, scatter):
    return [_sds(a.shape if scatter else (N_DEV,) + a.shape, a.dtype) for a in arrs]


def _exchange_sems(n):
    return [pltpu.SemaphoreType.DMA((n, N_DEV - 1)), pltpu.SemaphoreType.DMA((n, N_DEV - 1)),
            pltpu.SemaphoreType.DMA((n,))]


def _exchange(arrs, *, name, scatter):
    n = len(arrs)

    def body(*refs):
        _exchange_start(refs[:n], refs[n:2 * n], refs[2 * n:], scatter)
        _exchange_wait(refs[:n], refs[n:2 * n], refs[2 * n:], scatter)

    any_spec = pl.BlockSpec(memory_space=pl.ANY)
    outs = pl.pallas_call(body, name=name, out_shape=_exchange_shapes(arrs, scatter), in_specs=[any_spec] * n,
                          out_specs=[any_spec] * n, scratch_shapes=_exchange_sems(n))(*arrs)
    return list(outs)


def _mod_cols(c_all, w_ada, b_cols):
    def body(c_ref, w_ref, b_ref, mod_ref, act_ref):
        c = c_ref[...]
        act = c * _sigmoid(c)
        act_ref[...] = act
        mod_ref[...] = _dot(act.astype(BF16), w_ref[...].astype(BF16)) + b_ref[...]

    return _call(body, name='mod_cols', grid=(1,),
                 in_specs=[_const(c_all.shape), _const(w_ada.shape), _const(b_cols.shape)],
                 out_specs=[_const((N_DEV, ADA_SHARD)), _const(c_all.shape)],
                 out_shape=[_sds((N_DEV, ADA_SHARD)), _sds(c_all.shape)], vmem=VMEM_BIG)(c_all, w_ada, b_cols)


def _grad_w_ada(act_t, dmod_cols):
    def body(a_ref, d_ref, o_ref):
        o_ref[...] = _dot(a_ref[...], d_ref[...])

    return _call(body, name='grad_w_ada', grid=(1,), in_specs=[_const(act_t.shape), _const(dmod_cols.shape)],
                 out_specs=_const((D_MODEL, ADA_SHARD)), out_shape=_sds((D_MODEL, ADA_SHARD)),
                 vmem=VMEM_BIG)(act_t, dmod_cols)


def _pre_mix(x, sc, sh, g, w_s, tm, ride):
    T = x.shape[0]
    group = 4

    def body(x_ref, sc_ref, sh_ref, g_ref, w_ref, proj_ref, h_ref):
        @pl.when(pl.program_id(1) == 0)
        def _():
            xv = x_ref[...]
            h_ref[...] = ((xv * _rsqrt_mean(xv) * g_ref[...]) * (1.0 + sc_ref[...]) + sh_ref[...]).astype(BF16)

        for s in range(group):
            proj_ref[:, s * IN_SHARD:(s + 1) * IN_SHARD] = _dot(h_ref[...], w_ref[s])

    row = pl.BlockSpec((tm, D_MODEL), lambda i, j: (i, 0))
    vec = _const((1, D_MODEL))
    return _call(body, name='pre_mix', grid=(T // tm, N_DEV // group),
                 in_specs=[row, vec, vec, vec, pl.BlockSpec((group, D_MODEL, IN_SHARD), lambda i, j: (j, 0, 0))],
                 out_specs=[pl.BlockSpec((tm, group * IN_SHARD), lambda i, j: (i, j)), row],
                 out_shape=[_sds((T, D_IN_PROJ)), _sds((T, D_MODEL), BF16)],
                 sem=('parallel', 'arbitrary'), ride=ride)(x, sc, sh, g, w_s)


def _halo_before(tm, rows=HALO):
    return lambda i: jnp.maximum(i * (tm // rows) - 1, 0)


def _halo_after(tm, T, rows=HALO):
    return lambda i: jnp.minimum((i + 1) * (tm // rows), T // rows - 1)


def _mix_fwd(yssm, proj, d, glu_w, glu_b, g_ssm, cw, g_conv, avg16, avg64, tm):
    T = yssm.shape[0]
    hb = _halo_before(tm)

    def body(y_ref, p_ref, ph_ref, d_ref, gw_ref, gb_ref, gs_ref, cw_ref, gc_ref, a16_ref, a64_ref, o_ref):
        i = pl.program_id(0)
        u = p_ref[:, 0:D_SSM]
        y = y_ref[...] + d_ref[...] * u
        z, _ = _gelu(y)
        gate = _sigmoid(_dot(z.astype(BF16), gw_ref[...]) + gb_ref[...])
        ya = z * gate
        rs = lax.rsqrt(_dot_split(ya * ya, a16_ref[...], 2) + EPS)
        o_ref[:, 0:D_SSM] = (ya * rs * gs_ref[...]).astype(BF16)
        bg = p_ref[:, D_SSM:D_SSM + D_CONV]
        cv = p_ref[:, D_SSM + D_CONV:D_SSM + 2 * D_CONV] * p_ref[:, D_SSM + 2 * D_CONV:D_IN_PROJ]
        hv = ph_ref[:, D_SSM + D_CONV:D_SSM + 2 * D_CONV] * ph_ref[:, D_SSM + 2 * D_CONV:D_IN_PROJ]
        hv = jnp.where(i > 0, hv, 0.0)
        conv, _, _ = _conv3(cv, hv, cw_ref)
        yb = bg * conv
        rsb = lax.rsqrt(_dot_split(yb * yb, a64_ref[...], 2) + EPS)
        o_ref[:, D_SSM:D_MODEL] = (yb * rsb * gc_ref[...]).astype(BF16)

    vec = _const((1, D_SSM))
    sq = _const((D_SSM, D_SSM))
    return _call(body, name='mix_fwd', grid=(T // tm,),
                 in_specs=[pl.BlockSpec((tm, D_SSM), lambda i: (i, 0)), pl.BlockSpec((tm, D_IN_PROJ), lambda i: (i, 0)),
                           pl.BlockSpec((HALO, D_IN_PROJ), lambda i: (hb(i), 0)), vec, sq, vec, vec,
                           _const((3, D_CONV)), vec, sq, sq],
                 out_specs=pl.BlockSpec((tm, D_MODEL), lambda i: (i, 0)), out_shape=_sds((T, D_MODEL), BF16),
                 sem=('parallel',), vmem=VMEM_BIG)(yssm, proj, proj, d, glu_w, glu_b, g_ssm, cw, g_conv, avg16, avg64)


def _out_proj(ycat, w_out, x, gt, g_post, g_pre, sc, sh, tm):
    T = x.shape[0]

    def body(y_ref, w_ref, x_ref, gt_ref, gp_ref, g2_ref, sc_ref, sh_ref, o_ref, x1_ref, h_ref):
        o = _dot(y_ref[...], w_ref[...])
        o_ref[...] = o
        x1 = x_ref[...] + gt_ref[...] * (o * _rsqrt_mean(o) * gp_ref[...])
        x1_ref[...] = x1
        h_ref[...] = ((x1 * _rsqrt_mean(x1) * g2_ref[...]) * (1.0 + sc_ref[...]) + sh_ref[...]).astype(BF16)

    row = pl.BlockSpec((tm, D_MODEL), lambda i: (i, 0))
    vec = _const((1, D_MODEL))
    return _call(body, name='out_proj', grid=(T // tm,),
                 in_specs=[row, _const((D_MODEL, D_MODEL)), row, vec, vec, vec, vec, vec],
                 out_specs=[row, row, row],
                 out_shape=[_sds((T, D_MODEL)), _sds((T, D_MODEL)), _sds((T, D_MODEL), BF16)],
                 sem=('parallel',), vmem=VMEM_BIG)(ycat, w_out, x, gt, g_post, g_pre, sc, sh)


def _ffn_up(h2, w_a, w_b, cw8, tm, ride):
    T = h2.shape[0]
    hb = _halo_before(tm, HALO16)
    half = D_MODEL // 2

    def body(h_ref, hh_ref, wa_ref, wb_ref, cw_ref, up_ref, hid_ref):
        def times_w(ref, s):
            return _dot_nt(ref[:, :half], wa_ref[s]) + _dot_nt(ref[:, half:], wb_ref[s])

        for s in range(2):
            up = times_w(h_ref, s)
            up_ref[s] = up.astype(BF16)
            before = jnp.where(pl.program_id(0) > 0, times_w(hh_ref, s), 0.0)
            hid_ref[s] = _conv3(up, before, cw_ref.at[s])[0].astype(BF16)

    out = pl.BlockSpec((2, tm, FF_SHARD), lambda i, j: (j, i, 0))
    return _call(body, name='ffn_up', grid=(T // tm, N_DEV // 2),
                 in_specs=[pl.BlockSpec((tm, D_MODEL), lambda i, j: (i, 0)),
                           pl.BlockSpec((HALO16, D_MODEL), lambda i, j: (hb(i), 0)),
                           pl.BlockSpec((2, FF_SHARD, half), lambda i, j: (j, 0, 0)),
                           pl.BlockSpec((2, FF_SHARD, half), lambda i, j: (j, 0, 0)),
                           pl.BlockSpec((2, 3, FF_SHARD), lambda i, j: (j, 0, 0))],
                 out_specs=[out, out], out_shape=[_sds((N_DEV, T, FF_SHARD), BF16)] * 2,
                 sem=('parallel', 'parallel'), vmem=VMEM_BIG, ride=ride)(h2, h2, w_a, w_b, cw8)


def _ffn_down(hid4, wd4, x1, tgt, gt, g_post, tm):
    T = x1.shape[0]
    nb = T // tm

    def body(a_ref, w_ref, x1_ref, t_ref, gt_ref, g_ref, ddn_ref, dx_ref, loss_ref, dgt_ref, dg_ref, dn_ref):
        i, j = pl.program_id(0), pl.program_id(1)
        part = None
        for s in range(2):
            act = (_silu_parts(a_ref[0, s].astype(F32))[0] * a_ref[1, s].astype(F32)).astype(BF16)
            term = _dot(act, w_ref[s])
            part = term if part is None else part + term

        @pl.when(jnp.logical_and(i == 0, j == 0))
        def _():
            dgt_ref[...] = jnp.zeros_like(dgt_ref)
            dg_ref[...] = jnp.zeros_like(dg_ref)

        @pl.when(j == 0)
        def _():
            dn_ref[...] = part

        @pl.when(j > 0)
        def _():
            dn_ref[...] += part

        @pl.when(j == 1)
        def _():
            dn, gv, gate = dn_ref[...], g_ref[...], gt_ref[...]
            r = _rsqrt_mean(dn)
            normed = dn * r * gv
            err = x1_ref[...] + gate * normed - t_ref[...]
            dx = err * (1.0 / D_MODEL)
            dx_ref[...] = dx
            tot = jnp.sum(jnp.sum(err * err, axis=1, keepdims=True), axis=0, keepdims=True) * (0.5 / D_MODEL)
            loss_ref[...] = jnp.broadcast_to(tot, (8, 128))
            dgt_ref[...] += _colsum(dx * normed)
            dnn = dx * gate
            dg_ref[...] += _colsum(dnn * dn * r)
            ddn_ref[...] = _norm_bwd(dnn, dn, r, gv).astype(BF16)

    row = pl.BlockSpec((tm, D_MODEL), lambda i, j: (i, 0))
    vec = _const((1, D_MODEL))
    return _call(body, name='ffn_down', grid=(nb, 2),
                 in_specs=[pl.BlockSpec((2, 2, tm, FF_SHARD), lambda i, j: (0, j, i, 0)),
                           pl.BlockSpec((2, FF_SHARD, D_MODEL), lambda i, j: (j, 0, 0)), row, row, vec, vec],
                 out_specs=[row, row, pl.BlockSpec((None, 8, 128), lambda i, j: (i, 0, 0)), vec, vec],
                 out_shape=[_sds((T, D_MODEL), BF16), _sds((T, D_MODEL)), _sds((nb, 8, 128)), _sds((1, D_MODEL)),
                            _sds((1, D_MODEL))],
                 scratch=[pltpu.VMEM((tm, D_MODEL), F32)], sem=('arbitrary', 'arbitrary'),
                 vmem=VMEM_BIG)(hid4, wd4, x1, tgt, gt, g_post)


def _ssm_prep(lre, lim, lst, b_re, b_im):
    def body(lre_ref, lim_ref, lst_ref, br_ref, bi_ref, ar_ref, ai_ref, bbr_ref, bbi_ref):
        ar, ai, qr, qi = _zoh(lre_ref[...], lim_ref[...], lst_ref[...])[:4]
        ar_ref[...] = ar
        ai_ref[...] = ai
        bbr_ref[...] = qr * br_ref[...] - qi * bi_ref[...]
        bbi_ref[...] = qr * bi_ref[...] + qi * br_ref[...]

    shp = lre.shape
    return _call(body, name='ssm_prep', grid=(1,), in_specs=[_const(shp)] * 5, out_specs=[_const(shp)] * 4,
                 out_shape=[_sds(shp)] * 4)(lre, lim, lst, b_re, b_im)


def _zoh(lre, lim, lst):
    lr = jnp.minimum(lre, LAMBDA_RE_MAX)
    st = jnp.exp(lst)
    mag = jnp.exp(lr * st)
    ar = mag * jnp.cos(lim * st)
    ai = mag * jnp.sin(lim * st)
    den = lr * lr + lim * lim
    qr = ((ar - 1.0) * lr + ai * lim) / den
    qi = (ai * lr - (ar - 1.0) * lim) / den
    return ar, ai, qr, qi, lr, st, den


def _ssm_prep_bwd(lre, lim, lst, b_re, b_im, dbbr, dbbi, dar, dai, seg):
    def body(lre_ref, lim_ref, lst_ref, br_ref, bi_ref, dbbr_ref, dbbi_ref, dar_ref, dai_ref, seg_ref,
             dbr_ref, dbi_ref, dlre_ref, dlim_ref, dlst_ref):
        lre_v = lre_ref[...]
        li = lim_ref[...]
        ar, ai, qr, qi, lr, st, den = _zoh(lre_v, li, lst_ref[...])
        br, bi, gbr, gbi = br_ref[...], bi_ref[...], dbbr_ref[...], dbbi_ref[...]
        dbr_ref[...] = qr * gbr + qi * gbi
        dbi_ref[...] = qr * gbi - qi * gbr
        gqr = _dot_split(br * gbr + bi * gbi, seg_ref[...], 3)
        gqi = _dot_split(br * gbi - bi * gbr, seg_ref[...], 3)
        ir, ii = lr / den, -li / den
        gar = dar_ref[...] + ir * gqr + ii * gqi
        gai = dai_ref[...] + ir * gqi - ii * gqr
        tr, ti = qr * ir - qi * ii, qr * ii + qi * ir
        glr = -(tr * gqr + ti * gqi)
        gli = -(tr * gqi - ti * gqr)
        gzr = ar * gar + ai * gai
        gzi = ar * gai - ai * gar
        glr = glr + st * gzr
        gli = gli + st * gzi
        gst = (lr * gzr + li * gzi) * st
        dlre_ref[...] = jnp.where(lre_v < LAMBDA_RE_MAX, glr, 0.0)
        dlim_ref[...] = gli
        dlst_ref[...] = jnp.sum(gst, axis=1, keepdims=True) * (1.0 / SSM_GROUP)

    shp = lre.shape
    return _call(body, name='ssm_prep_bwd', grid=(1,), in_specs=[_const(shp)] * 9 + [_const(seg.shape)],
                 out_specs=[_const(shp)] * 4 + [_const((N_GROUPS, 1))],
                 out_shape=[_sds(shp)] * 4 + [_sds((N_GROUPS, 1))], vmem=VMEM_BIG)(
                     lre, lim, lst, b_re, b_im, dbbr, dbbi, dar, dai, seg)


def _scan_specs(T):
    return dict(
        chan=pl.BlockSpec((T, CHAN_BLOCK), lambda cb: (0, cb)),
        state=pl.BlockSpec((T, STATE_BLOCK), lambda cb: (0, cb)),
        b=pl.BlockSpec((CHAN_BLOCK, STATE_BLOCK), lambda cb: (cb, cb)),
        c=pl.BlockSpec((STATE_BLOCK, CHAN_BLOCK), lambda cb: (cb, cb)),
        lam=pl.BlockSpec((1, STATE_BLOCK), lambda cb: (0, cb)),
    )


def _complex_power(re, im, n):
    out = None
    while True:
        if n & 1:
            out = (re, im) if out is None else (out[0] * re - out[1] * im, out[0] * im + out[1] * re)
        n >>= 1
        if n == 0:
            return out
        re, im = re * re - im * im, 2.0 * re * im


def _rows8(i):
    if isinstance(i, int):
        return pl.ds(i * SUBLANES, SUBLANES)
    return pl.ds(pl.multiple_of(i * SUBLANES, SUBLANES), SUBLANES)


def _scan_loop(n_steps, body, init):
    trips = n_steps // SCAN_UNROLL

    def trip(t, carry):
        for u in range(SCAN_UNROLL):
            carry = body(t * SCAN_UNROLL + u, carry)
        return carry

    carry = lax.fori_loop(0, trips, trip, init)
    for step in range(trips * SCAN_UNROLL, n_steps):
        carry = body(step, carry)
    return carry


def _ssm_fwd(u_perm, b_re, b_im, c_re, c_im, lam_r, lam_i, ride):
    T = u_perm.shape[0]
    ls = T // SUBLANES
    rc = min(512, T)
    sp = _scan_specs(T)

    def body(u_ref, bre_ref, bim_ref, cre_ref, cim_ref, lr_ref, li_ref, so_re_ref, so_im_ref, y_ref, sre_ref, sim_ref):
        for c in range(T // rc):
            rows = pl.ds(c * rc, rc)
            ub = u_ref[rows, :].astype(BF16)
            sre_ref[rows, :] = _dot(ub, bre_ref[...])
            sim_ref[rows, :] = _dot(ub, bim_ref[...])
        shp = (SUBLANES, STATE_BLOCK)
        lr = jnp.broadcast_to(lr_ref[...], shp)
        li = jnp.broadcast_to(li_ref[...], shp)
        zero = jnp.zeros(shp, F32)

        def step(i, carry):
            sr, si = carry
            rows = _rows8(i)
            nr = lr * sr - li * si + sre_ref[rows, :]
            ni = lr * si + li * sr + sim_ref[rows, :]
            sre_ref[rows, :] = nr
            sim_ref[rows, :] = ni
            return nr, ni

        fr, fi = _scan_loop(ls, step, (zero, zero))
        pr, pi_ = _complex_power(lr, li, ls)
        row = lax.broadcasted_iota(jnp.int32, shp, 0)
        ir, ii = zero, zero
        for _ in range(SUBLANES - 1):
            er = fr + pr * ir - pi_ * ii
            ei = fi + pr * ii + pi_ * ir
            ir = jnp.where(row == 0, 0.0, pltpu.roll(er, 1, 0))
            ii = jnp.where(row == 0, 0.0, pltpu.roll(ei, 1, 0))

        def fix(i, carry):
            cr, ci = carry
            rows = _rows8(i)
            nr = lr * cr - li * ci
            ni = lr * ci + li * cr
            sre_ref[rows, :] += nr
            sim_ref[rows, :] += ni
            return nr, ni

        _scan_loop(ls, fix, (ir, ii))
        for c in range(T // rc):
            rows = pl.ds(c * rc, rc)
            s_r, s_i = sre_ref[rows, :].astype(BF16), sim_ref[rows, :].astype(BF16)
            so_re_ref[rows, :] = s_r
            so_im_ref[rows, :] = s_i
            y_ref[rows, :] = _dot(s_r, cre_ref[...]) - _dot(s_i, cim_ref[...])

    return _call(body, name='ssm_fwd', grid=(N_STATE // STATE_BLOCK,),
                 in_specs=[sp['chan'], sp['b'], sp['b'], sp['c'], sp['c'], sp['lam'], sp['lam']],
                 out_specs=[sp['state'], sp['state'], sp['chan']],
                 out_shape=[_sds((T, N_STATE), BF16), _sds((T, N_STATE), BF16), _sds((T, D_SSM))],
                 scratch=[pltpu.VMEM((T, STATE_BLOCK), F32), pltpu.VMEM((T, STATE_BLOCK), F32)],
                 sem=('arbitrary',), vmem=VMEM_MOST, ride=ride)(u_perm, b_re, b_im, c_re, c_im, lam_r, lam_i)


def _ssm_bwd(dy_perm, u_perm, s_re, s_im, b_re, b_im, c_re, c_im, lam_r, lam_i, ride):
    T = u_perm.shape[0]
    ls = T // SUBLANES
    rc = min(512, T)
    sp = _scan_specs(T)
    ncb = N_STATE // STATE_BLOCK

    def body(dy_ref, u_ref, sre_ref, sim_ref, bre_ref, bim_ref, cre_ref, cim_ref, lr_ref, li_ref,
             du_ref, dbr_ref, dbi_ref, dcr_ref, dci_ref, dar_ref, dai_ref, gre_ref, gim_ref):
        shp = (SUBLANES, STATE_BLOCK)
        zero = jnp.zeros(shp, F32)
        tail = pl.ds(T, SUBLANES)
        gre_ref[tail, :] = zero
        gim_ref[tail, :] = zero
        for c in range(T // rc):
            rows = pl.ds(c * rc, rc)
            dyb = dy_ref[rows, :].astype(BF16)
            gre_ref[rows, :] = _dot_nt(dyb, cre_ref[...])
            gim_ref[rows, :] = -_dot_nt(dyb, cim_ref[...])
        lr = jnp.broadcast_to(lr_ref[...], shp)
        li = jnp.broadcast_to(li_ref[...], shp)

        def step(k, carry):
            gr, gi = carry
            rows = _rows8(ls - 1 - k)
            nr = lr * gr + li * gi + gre_ref[rows, :]
            ni = lr * gi - li * gr + gim_ref[rows, :]
            gre_ref[rows, :] = nr
            gim_ref[rows, :] = ni
            return nr, ni

        fr, fi = _scan_loop(ls, step, (zero, zero))
        pr, pi_ = _complex_power(lr, -li, ls)
        row = lax.broadcasted_iota(jnp.int32, shp, 0)
        cr, ci = zero, zero
        for _ in range(SUBLANES - 1):
            er = fr + pr * cr - pi_ * ci
            ei = fi + pr * ci + pi_ * cr
            cr = jnp.where(row == SUBLANES - 1, 0.0, pltpu.roll(er, SUBLANES - 1, 0))
            ci = jnp.where(row == SUBLANES - 1, 0.0, pltpu.roll(ei, SUBLANES - 1, 0))

        def fix(k, carry):
            dr, di = carry
            rows = _rows8(ls - 1 - k)
            dr, di = lr * dr + li * di, lr * di - li * dr
            gre_ref[rows, :] += dr
            gim_ref[rows, :] += di
            return dr, di

        _scan_loop(ls, fix, (cr, ci))

        acc_r = jnp.zeros((1, STATE_BLOCK), F32)
        acc_i = jnp.zeros((1, STATE_BLOCK), F32)
        for c in range(T // rc):
            rows, nxt = pl.ds(c * rc, rc), pl.ds(c * rc + SUBLANES, rc)
            s_r, s_i = sre_ref[rows, :].astype(F32), sim_ref[rows, :].astype(F32)
            g_r, g_i = gre_ref[nxt, :], gim_ref[nxt, :]
            acc_r = acc_r + _colsum(g_r * s_r + g_i * s_i)
            acc_i = acc_i + _colsum(g_i * s_r - g_r * s_i)
        last = pl.ds(T - 2 * SUBLANES, 2 * SUBLANES)
        first = pl.ds(0, SUBLANES)
        spr = jnp.where(row == 0, 0.0, pltpu.roll(sre_ref[last, :].astype(F32)[SUBLANES:], 1, 0))
        spi = jnp.where(row == 0, 0.0, pltpu.roll(sim_ref[last, :].astype(F32)[SUBLANES:], 1, 0))
        gr, gi = gre_ref[first, :], gim_ref[first, :]
        dar_ref[...] = acc_r + _colsum(gr * spr + gi * spi)
        dai_ref[...] = acc_i + _colsum(gi * spr - gr * spi)

        for c in range(T // rc):
            rows = pl.ds(c * rc, rc)
            g_r, g_i = gre_ref[rows, :].astype(BF16), gim_ref[rows, :].astype(BF16)
            s_r, s_i = sre_ref[rows, :], sim_ref[rows, :]
            ub, dyb = u_ref[rows, :].astype(BF16), dy_ref[rows, :].astype(BF16)
            du_ref[rows, :] = _dot_nt(g_r, bre_ref[...]) + _dot_nt(g_i, bim_ref[...])
            parts = (_dot_tn(ub, g_r), _dot_tn(ub, g_i), _dot_tn(s_r, dyb), -_dot_tn(s_i, dyb))
            outs = (dbr_ref, dbi_ref, dcr_ref, dci_ref)
            for o_ref, part in zip(outs, parts):
                if c == 0:
                    o_ref[...] = part
                else:
                    o_ref[...] += part

    blk = lambda r, c: pl.BlockSpec((None, r, c), lambda cb: (cb, 0, 0))
    return _call(body, name='ssm_bwd', grid=(ncb,),
                 in_specs=[sp['chan'], sp['chan'], sp['state'], sp['state'], sp['b'], sp['b'], sp['c'], sp['c'],
                           sp['lam'], sp['lam']],
                 out_specs=[sp['chan'], blk(CHAN_BLOCK, STATE_BLOCK), blk(CHAN_BLOCK, STATE_BLOCK),
                            blk(STATE_BLOCK, CHAN_BLOCK), blk(STATE_BLOCK, CHAN_BLOCK), blk(1, STATE_BLOCK),
                            blk(1, STATE_BLOCK)],
                 out_shape=[_sds((T, D_SSM)), _sds((ncb, CHAN_BLOCK, STATE_BLOCK)), _sds((ncb, CHAN_BLOCK, STATE_BLOCK)),
                            _sds((ncb, STATE_BLOCK, CHAN_BLOCK)), _sds((ncb, STATE_BLOCK, CHAN_BLOCK)),
                            _sds((ncb, 1, STATE_BLOCK)), _sds((ncb, 1, STATE_BLOCK))],
                 scratch=[pltpu.VMEM((T + SUBLANES, STATE_BLOCK), F32), pltpu.VMEM((T + SUBLANES, STATE_BLOCK), F32)],
                 sem=('arbitrary',), vmem=VMEM_MOST, ride=ride)(dy_perm, u_perm, s_re, s_im, b_re, b_im, c_re, c_im,
                                                                lam_r, lam_i)


def _ffn_dact(ddn, wd4, hid4, tm):
    T = ddn.shape[0]
    nb = T // tm

    def body(d_ref, w_ref, hid_ref, o_ref, gw_ref, acc_ref):
        i = pl.program_id(1)
        d = d_ref[...]
        dact = _dot_nt(d, w_ref[...])
        silu, dsilu = _silu_parts(hid_ref[0].astype(F32))
        hid_v = hid_ref[1].astype(F32)
        o_ref[0] = (dact * hid_v * dsilu).astype(BF16)
        o_ref[1] = (dact * silu).astype(BF16)
        part = _dot_tn((silu * hid_v).astype(BF16), d)

        @pl.when(i == 0)
        def _():
            acc_ref[...] = part

        @pl.when(i > 0)
        def _():
            acc_ref[...] += part

        @pl.when(i == nb - 1)
        def _():
            gw_ref[...] = acc_ref[...].astype(BF16)

    blk = pl.BlockSpec((2, None, tm, FF_SHARD), lambda j, i: (0, j, i, 0))
    w_blk = pl.BlockSpec((None, FF_SHARD, D_MODEL), lambda j, i: (j, 0, 0))
    return _call(body, name='ffn_dact', grid=(4, nb),
                 in_specs=[pl.BlockSpec((tm, D_MODEL), lambda j, i: (i, 0)), w_blk, blk],
                 out_specs=[blk, w_blk],
                 out_shape=[_sds((2, 4, T, FF_SHARD), BF16), _sds((4, FF_SHARD, D_MODEL), BF16)],
                 scratch=[pltpu.VMEM((FF_SHARD, D_MODEL), F32)], sem=('parallel', 'arbitrary'),
                 vmem=VMEM_BIG)(ddn, wd4, hid4)


def _ffn_dup(dhid8, up8, cw8, tm, ride):
    T = up8.shape[1]
    nb = T // tm
    ha = _halo_after(tm, T, HALO16)

    def body(dh_ref, dha_ref, up_ref, cw_ref, dup_ref, dcw_ref):
        i = pl.program_id(1)

        @pl.when(i == 0)
        def _():
            dcw_ref[...] = jnp.zeros_like(dcw_ref)

        dh = dh_ref[...].astype(F32)
        dup, dh1, dh2 = _conv3_t(dh, jnp.where(i < nb - 1, dha_ref[...].astype(F32), 0.0), cw_ref)
        dup_ref[...] = dup.astype(BF16)
        up = up_ref[...].astype(F32)
        dcw_ref[0:1, :] += _colsum(dh2 * up)
        dcw_ref[1:2, :] += _colsum(dh1 * up)
        dcw_ref[2:3, :] += _colsum(dh * up)

    main = pl.BlockSpec((None, tm, FF_SHARD), lambda j, i: (j, i, 0))
    return _call(body, name='ffn_dup', grid=(N_DEV, nb),
                 in_specs=[main, pl.BlockSpec((None, HALO16, FF_SHARD), lambda j, i: (j, ha(i), 0)), main,
                           pl.BlockSpec((None, 3, FF_SHARD), lambda j, i: (j, 0, 0))],
                 out_specs=[main, pl.BlockSpec((None, 8, FF_SHARD), lambda j, i: (j, 0, 0))],
                 out_shape=[_sds((N_DEV, T, FF_SHARD), BF16), _sds((N_DEV, 8, FF_SHARD))],
                 sem=('parallel', 'arbitrary'), vmem=VMEM_BIG, ride=ride)(dhid8, dhid8, up8, cw8)


def _grad_tn(a, b, a_spec, b_spec, groups, m, n, tk, name, ride=None, parts=1):
    T = a.shape[-2]
    nk = T // tk
    mp = m // parts

    def body(a_ref, b_ref, *refs):
        o_refs, acc_ref = refs[:parts], refs[parts]
        k = pl.program_id(1)
        part = _dot_tn(a_ref[...], b_ref[...])

        @pl.when(k == 0)
        def _():
            acc_ref[...] = part

        @pl.when(k > 0)
        def _():
            acc_ref[...] += part

        @pl.when(k == nk - 1)
        def _():
            for p, o_ref in enumerate(o_refs):
                o_ref[...] = acc_ref[p * mp:(p + 1) * mp, :].astype(BF16)

    out_spec = pl.BlockSpec((None, mp, n), lambda g, k: (g, 0, 0))
    res = _call(body, name=name, grid=(groups, nk), in_specs=[a_spec, b_spec], out_specs=[out_spec] * parts,
                out_shape=[_sds((groups, mp, n), BF16)] * parts, scratch=[pltpu.VMEM((m, n), F32)],
                sem=('parallel', 'arbitrary'), vmem=VMEM_BIG, ride=ride)(a, b)
    if parts > 1:
        return res
    return res[0] if ride is None else (res[0][0], res[1])


def _grad_w_in(h1, dproj, tk, ride):
    T = h1.shape[0]
    nk = T // tk
    half = D_IN_PROJ // 2

    def body(a_ref, b_ref, o_ref, acc_ref):
        k = pl.program_id(0)
        for h in range(2):
            cols = slice(h * half, (h + 1) * half)
            part = _dot_tn(a_ref[...], b_ref[:, cols])

            @pl.when(k == 0)
            def _():
                acc_ref[:, cols] = part

            @pl.when(k > 0)
            def _():
                acc_ref[:, cols] += part

        @pl.when(k == nk - 1)
        def _():
            for g in range(N_DEV):
                o_ref[g] = acc_ref[:, g * IN_SHARD:(g + 1) * IN_SHARD].astype(BF16)

    return _call(body, name='grad_w_in', grid=(nk,),
                 in_specs=[pl.BlockSpec((tk, D_MODEL), lambda k: (k, 0)), pl.BlockSpec((tk, D_IN_PROJ), lambda k: (k, 0))],
                 out_specs=_const((N_DEV, D_MODEL, IN_SHARD)), out_shape=_sds((N_DEV, D_MODEL, IN_SHARD), BF16),
                 scratch=[pltpu.VMEM((D_MODEL, D_IN_PROJ), F32)], sem=('arbitrary',), vmem=VMEM_BIG, ride=ride)(h1, dproj)


def _pre_norm_bwd(dz, dz_spec, w_parts, xin, dres, sc, g, tm, name, ride, below=None, group=1, w_t=False):
    T = xin.shape[0]
    n = w_parts[0].shape[1] if w_t else w_parts[0].shape[2]
    mul = _dot if w_t else _dot_nt
    steps = N_DEV // group
    width = D_MODEL // len(w_parts)

    def body(dz_ref, *refs):
        w_refs, (x_ref, dr_ref, sc_ref, g_ref), refs = refs[:len(w_parts)], refs[len(w_parts):len(w_parts) + 4], \
            refs[len(w_parts) + 4:]
        if below is None:
            dx_ref, dsh_ref, dsc_ref, dg_ref = refs
            sums = (dsh_ref, dsc_ref, dg_ref)
        else:
            v_ref, gate_ref, g2_ref, dx_ref, dsh_ref, dsc_ref, dg_ref, dv_ref, dgate_ref, dg2_ref = refs
            sums = (dsh_ref, dsc_ref, dg_ref, dgate_ref, dg2_ref)
        i, j = pl.program_id(0), pl.program_id(1)
        piece = (lambda s: dz_ref[s]) if dz.ndim == 3 else (lambda s: dz_ref[:, s * n:(s + 1) * n])
        parts = []
        for w_ref in w_refs:
            part = mul(piece(0), w_ref[0])
            for s in range(1, group):
                part = part + mul(piece(s), w_ref[s])
            parts.append(part)

        @pl.when(jnp.logical_and(i == 0, j == 0))
        def _():
            for s_ref in sums:
                s_ref[...] = jnp.zeros_like(s_ref)

        @pl.when(j == 0)
        def _():
            for k, part in enumerate(parts):
                dx_ref[:, k * width:(k + 1) * width] = part

        @pl.when(j > 0)
        def _():
            for k, part in enumerate(parts):
                dx_ref[:, k * width:(k + 1) * width] += part

        @pl.when(j == steps - 1)
        def _():
            dh, xv, gv = dx_ref[...], x_ref[...], g_ref[...]
            r = _rsqrt_mean(xv)
            dsh_ref[...] += _colsum(dh)
            dsc_ref[...] += _colsum(dh * (xv * r * gv))
            dxn = dh * (1.0 + sc_ref[...])
            dg_ref[...] += _colsum(dxn * xv * r)
            dx = dr_ref[...] + _norm_bwd(dxn, xv, r, gv)
            dx_ref[...] = dx
            if below is not None:
                v, g2 = v_ref[...], g2_ref[...]
                rv = _rsqrt_mean(v)
                dgate_ref[...] += _colsum(dx * (v * rv * g2))
                dn = dx * gate_ref[...]
                dg2_ref[...] += _colsum(dn * v * rv)
                dv_ref[...] = _norm_bwd(dn, v, rv, g2).astype(BF16)

    row = pl.BlockSpec((tm, D_MODEL), lambda i, j: (i, 0))
    vec = _const((1, D_MODEL))
    in_specs = [dz_spec] + [pl.BlockSpec((group,) + w.shape[1:], lambda i, j: (j, 0, 0)) for w in w_parts]
    in_specs += [row, row, vec, vec]
    out_specs = [row, vec, vec, vec]
    out_shape = [_sds((T, D_MODEL)), _sds((1, D_MODEL)), _sds((1, D_MODEL)), _sds((1, D_MODEL))]
    args = [dz, *w_parts, xin, dres, sc, g]
    if below is not None:
        in_specs += [row, vec, vec]
        out_specs += [row, vec, vec]
        out_shape += [_sds((T, D_MODEL), BF16), _sds((1, D_MODEL)), _sds((1, D_MODEL))]
        args += list(below)
    return _call(body, name=name, grid=(T // tm, steps), in_specs=in_specs, out_specs=out_specs,
                 out_shape=out_shape, sem=('arbitrary', 'arbitrary'), vmem=VMEM_MOST, ride=ride)(*args)


def _mix_bwd(d_o, w_out, yssm, proj, d, glu_w, glu_b, g_ssm, cw, g_conv, avg16, avg64, tm, ride):
    T = yssm.shape[0]
    hb = _halo_before(tm)

    def body(do_ref, wo_ref, y_ref, p_ref, ph_ref, d_ref, gw_ref, gb_ref, gs_ref, cw_ref, gc_ref, a16_ref, a64_ref,
             dy_ref, dconv_ref, dbg_ref, z_ref, dlin_ref, acc_ref):
        i = pl.program_id(0)
        dyc = _dot_nt(do_ref[...], wo_ref[...])

        @pl.when(i == 0)
        def _():
            acc_ref[...] = jnp.zeros_like(acc_ref)

        u = p_ref[:, 0:D_SSM]
        y = y_ref[...] + d_ref[...] * u
        z, t = _gelu(y)
        gate = _sigmoid(_dot(z.astype(BF16), gw_ref[...]) + gb_ref[...])
        ya = z * gate
        rs = lax.rsqrt(_dot_split(ya * ya, a16_ref[...], 2) + EPS)
        dna = dyc[:, 0:D_SSM]
        acc_ref[1:2, :] += _colsum(dna * ya * rs)
        dya = _head_norm_bwd(dna, ya, rs, gs_ref[...], a16_ref[...])
        dlin = dya * z * gate * (1.0 - gate)
        acc_ref[0:1, :] += _colsum(dlin)
        dlin_b = dlin.astype(BF16)
        dz = dya * gate + _dot_nt(dlin_b, gw_ref[...])
        dy = dz * _gelu_grad(y, t)
        acc_ref[3:4, :] += _colsum(dy * u)
        dy_ref[...] = dy
        z_ref[...] = z.astype(BF16)
        dlin_ref[...] = dlin_b

        bg = p_ref[:, D_SSM:D_SSM + D_CONV]
        cv = p_ref[:, D_SSM + D_CONV:D_SSM + 2 * D_CONV] * p_ref[:, D_SSM + 2 * D_CONV:D_IN_PROJ]
        hv = ph_ref[:, D_SSM + D_CONV:D_SSM + 2 * D_CONV] * ph_ref[:, D_SSM + 2 * D_CONV:D_IN_PROJ]
        hv = jnp.where(i > 0, hv, 0.0)
        conv, cv1, cv2 = _conv3(cv, hv, cw_ref)
        yb = bg * conv
        rsb = lax.rsqrt(_dot_split(yb * yb, a64_ref[...], 2) + EPS)
        dnb = dyc[:, D_SSM:D_MODEL]
        acc_ref[2:3, :] += _colsum(dnb * yb * rsb)
        dyb = _head_norm_bwd(dnb, yb, rsb, gc_ref[...], a64_ref[...])
        dbg_ref[...] = dyb * conv
        dconv = dyb * bg
        dconv_ref[...] = dconv
        acc_ref[4:5, :] += _colsum(dconv * cv2)
        acc_ref[5:6, :] += _colsum(dconv * cv1)
        acc_ref[6:7, :] += _colsum(dconv * cv)

    vec = _const((1, D_SSM))
    sq = _const((D_SSM, D_SSM))
    half = pl.BlockSpec((tm, D_SSM), lambda i: (i, 0))
    return _call(body, name='mix_bwd', grid=(T // tm,),
                 in_specs=[pl.BlockSpec((tm, D_MODEL), lambda i: (i, 0)), _const((D_MODEL, D_MODEL)), half,
                           pl.BlockSpec((tm, D_IN_PROJ), lambda i: (i, 0)),
                           pl.BlockSpec((HALO, D_IN_PROJ), lambda i: (hb(i), 0)), vec, sq, vec, vec,
                           _const((3, D_CONV)), vec, sq, sq],
                 out_specs=[half, half, half, half, half, _const((8, D_SSM))],
                 out_shape=[_sds((T, D_SSM)), _sds((T, D_SSM)), _sds((T, D_SSM)), _sds((T, D_SSM), BF16),
                            _sds((T, D_SSM), BF16), _sds((8, D_SSM))],
                 sem=('arbitrary',), vmem=VMEM_BIG, ride=ride)(d_o, w_out, yssm, proj, proj, d, glu_w, glu_b, g_ssm, cw,
                                                              g_conv, avg16, avg64)


def _mix_bwd_proj(dconv, proj, du_ssm, dy, d, dbg, cw, tm):
    T = dy.shape[0]
    nb = T // tm
    ha = _halo_after(tm, T)

    def body(dc_ref, dch_ref, cg_ref, v_ref, du_ref, dy_ref, d_ref, dbg_ref, cw_ref, o_ref):
        i = pl.program_id(0)
        dcv = _conv3_t(dc_ref[...], jnp.where(i < nb - 1, dch_ref[...], 0.0), cw_ref)[0]
        o_ref[:, 0:D_SSM] = (du_ref[...] + dy_ref[...] * d_ref[...]).astype(BF16)
        o_ref[:, D_SSM:D_SSM + D_CONV] = dbg_ref[...].astype(BF16)
        o_ref[:, D_SSM + D_CONV:D_SSM + 2 * D_CONV] = (dcv * v_ref[...]).astype(BF16)
        o_ref[:, D_SSM + 2 * D_CONV:D_IN_PROJ] = (dcv * cg_ref[...]).astype(BF16)

    half = pl.BlockSpec((tm, D_SSM), lambda i: (i, 0))
    return _call(body, name='mix_bwd_proj', grid=(nb,),
                 in_specs=[half, pl.BlockSpec((HALO, D_CONV), lambda i: (ha(i), 0)),
                           pl.BlockSpec((tm, D_CONV), lambda i: (i, 2)), pl.BlockSpec((tm, D_CONV), lambda i: (i, 3)),
                           half, half, _const((1, D_SSM)), half, _const((3, D_CONV))],
                 out_specs=pl.BlockSpec((tm, D_IN_PROJ), lambda i: (i, 0)), out_shape=_sds((T, D_IN_PROJ), BF16),
                 sem=('parallel',), vmem=VMEM_BIG)(dconv, dconv, proj, proj, du_ssm, dy, d, dbg, cw)


ADAMW_SLOT_BYTES = 8 << 20
ADAMW_ROW_BYTES = 3 << 19


def _row_tile(rows, cols, slots):
    for cand in range(rows, 15, -1):
        if (rows % cand == 0 and cand % 16 == 0 and slots * cand * cols * 4 <= ADAMW_SLOT_BYTES
                and cand * cols * 4 <= ADAMW_ROW_BYTES):
            return cand
    return rows


def _adamw_math(g, w, m, v):
    m2 = ADAM_B1 * m + (1.0 - ADAM_B1) * g
    v2 = ADAM_B2 * v + (1.0 - ADAM_B2) * (g * g)
    m_hat = m2 / (1.0 - ADAM_B1 ** ADAM_STEP)
    v_hat = v2 / (1.0 - ADAM_B2 ** ADAM_STEP)
    return -ADAM_LR * (m_hat / (jnp.sqrt(v_hat) + ADAM_EPS) + ADAM_WD * w), m2, v2


def _adamw(pieces, w, m, v, name):
    slots, _, cols = pieces[0].shape
    rows = sum(p.shape[1] for p in pieces)
    tr = _row_tile(pieces[0].shape[1], cols, slots)
    starts, pos = [], 0
    for p in pieces:
        assert p.shape[1] % tr == 0
        starts.append(pos)
        pos += p.shape[1] // tr

    def body(*refs):
        g_refs = refs[:len(pieces)]
        w_ref, m_ref, v_ref, go_ref, d_ref, mo_ref, vo_ref = refs[len(pieces):]
        i = pl.program_id(0)
        g = None
        for g_ref, start in zip(g_refs, starts):
            part = g_ref[0].astype(F32)
            for s in range(1, slots):
                part = part + g_ref[s].astype(F32)
            g = part if g is None else jnp.where(i >= start, part, g)
        go_ref[...] = g
        d_ref[...], mo_ref[...], vo_ref[...] = _adamw_math(g, w_ref[...], m_ref[...], v_ref[...])

    def piece_spec(start, count):
        return pl.BlockSpec((slots, tr, cols), lambda i: (0, jnp.clip(i - start, 0, count - 1), 0))

    blk = pl.BlockSpec((tr, cols), lambda i: (i, 0))
    return _call(body, name=name, grid=(rows // tr,),
                 in_specs=[piece_spec(s, p.shape[1] // tr) for s, p in zip(starts, pieces)] + [blk, blk, blk],
                 out_specs=[blk] * 4, out_shape=[_sds((rows, cols))] * 4, sem=('parallel',),
                 vmem=VMEM_BIG)(*pieces, w, m, v)


def _to_scan_rows(a):
    T, n = a.shape
    return a.reshape(SUBLANES, T // SUBLANES, n).transpose(1, 0, 2).reshape(T, n)


def _from_scan_rows(a):
    T, n = a.shape
    return a.reshape(T // SUBLANES, SUBLANES, n).transpose(1, 0, 2).reshape(T, n)


def _expand(a):
    return jnp.repeat(a, SSM_GROUP, axis=1)


def _block_diag(rows, row_group, col_group):
    r, n = rows.shape
    tiled = jnp.tile(rows, (1, N_GROUPS))
    keep = (jnp.arange(r)[:, None] // row_group) == (jnp.arange(n * N_GROUPS)[None, :] // col_group)
    return jnp.where(keep, tiled, 0.0)


def _block_diag_b(bb):
    return _block_diag(bb.transpose(0, 2, 1).reshape(D_SSM, SSM_STATE), SSM_GROUP, SSM_STATE)


def _block_diag_c(cc):
    return _block_diag(cc.transpose(0, 2, 1).reshape(N_STATE, SSM_GROUP), SSM_STATE, SSM_GROUP)


def _diag_blocks(x, chan_major):
    per = CHAN_BLOCK // SSM_GROUP
    eye = jnp.eye(per, dtype=x.dtype)
    if chan_major:
        x = x.reshape(-1, per, SSM_GROUP, per, SSM_STATE) * eye[None, :, None, :, None]
        return x.sum(axis=1).transpose(0, 2, 3, 1).reshape(N_GROUPS, SSM_STATE, SSM_GROUP)
    x = x.reshape(-1, per, SSM_STATE, per, SSM_GROUP) * eye[None, :, None, :, None]
    return x.sum(axis=3).reshape(N_GROUPS, SSM_STATE, SSM_GROUP)


SMALL_LAYOUT = {
    'ssm_b_re': (0, 0, 32, 1024), 'ssm_b_im': (32, 0, 32, 1024), 'ssm_c_re': (64, 0, 32, 1024),
    'ssm_c_im': (96, 0, 32, 1024), 'b_ada': (128, 0, 6, 1024), 'g_pre_mix': (134, 0, 1, 1024),
    'g_post_mix': (135, 0, 1, 1024), 'ssm_lam_re': (136, 0, 2, 1024), 'ssm_lam_im': (138, 0, 2, 1024),
    'ssm_log_step': (140, 0, 1, 32), 'glu_b': (141, 0, 1, 512), 'g_out_ssm': (141, 512, 1, 512),
    'g_out_conv': (142, 0, 1, 512), 'ssm_d': (142, 512, 1, 512), 'g_pre_ffn': (143, 0, 1, 1024),
    'g_post_ffn': (144, 0, 1, 1024)}
SMALL_ROWS = 152
B_ADA_ROW = SMALL_LAYOUT['b_ada'][0]
LATE_ROWS = {('b_ada', 0): 0, ('b_ada', 1): 1, ('g_pre_mix', 0): 2}


def _adamw_small(gathered, late, wts, mom_m, mom_v):
    names = list(SMALL_LAYOUT)
    n = len(names)

    def body(*refs):
        g_ref, late_ref, ins, outs = refs[0], refs[1], refs[2:2 + 3 * n], refs[2 + 3 * n:]
        for p, name in enumerate(names):
            r0, c0, rows, cols = SMALL_LAYOUT[name]
            pieces = [(0, rows)] if rows % 8 == 0 else [(r, 1) for r in range(rows)]
            for r, cnt in pieces:
                src_ref, first = (late_ref, LATE_ROWS[name, r]) if (name, r) in LATE_ROWS else (g_ref, r0 + r)
                g = src_ref[0, first:first + cnt, c0:c0 + cols]
                for s in range(1, N_DEV):
                    g = g + src_ref[s, first:first + cnt, c0:c0 + cols]
                w, m, v = (ins[3 * p + q][r:r + cnt, :] for q in range(3))
                res = (g,) + _adamw_math(g, w, m, v)
                for q in range(4):
                    outs[4 * p + q][r:r + cnt, :] = res[q]

    shapes = [SMALL_LAYOUT[name][2:] for name in names]
    args = [gathered, late]
    for name, shp in zip(names, shapes):
        args += [wts[name].reshape(shp), mom_m[name].reshape(shp), mom_v[name].reshape(shp)]
    outs = _call(body, name='adamw_small', grid=(1,),
                 in_specs=[_const(gathered.shape), _const(late.shape)]
                 + [_const(shp) for shp in shapes for _ in range(3)],
                 out_specs=[_const(shp) for shp in shapes for _ in range(4)],
                 out_shape=[_sds(shp) for shp in shapes for _ in range(4)], vmem=VMEM_BIG)(*args)
    res = {}
    for p, name in enumerate(names):
        for q, kind in enumerate(('g', 'd', 'm', 'v')):
            res[kind, name] = outs[4 * p + q].reshape(wts[name].shape)
    return res


def kernel(x, c, w_ada, b_ada, g_pre_mix, g_post_mix, w_in, ssm_lam_re, ssm_lam_im, ssm_log_step, ssm_b_re, ssm_b_im, ssm_c_re, ssm_c_im, ssm_d, glu_w, glu_b, g_out_ssm, conv_w, g_out_conv, w_out, g_pre_ffn, g_post_ffn, w_up, ffn_conv_w, w_down, loss_target, m_w_ada, m_b_ada, m_g_pre_mix, m_g_post_mix, m_w_in, m_ssm_lam_re, m_ssm_lam_im, m_ssm_log_step, m_ssm_b_re, m_ssm_b_im, m_ssm_c_re, m_ssm_c_im, m_ssm_d, m_glu_w, m_glu_b, m_g_out_ssm, m_conv_w, m_g_out_conv, m_w_out, m_g_pre_ffn, m_g_post_ffn, m_w_up, m_ffn_conv_w, m_w_down, v_w_ada, v_b_ada, v_g_pre_mix, v_g_post_mix, v_w_in, v_ssm_lam_re, v_ssm_lam_im, v_ssm_log_step, v_ssm_b_re, v_ssm_b_im, v_ssm_c_re, v_ssm_c_im, v_ssm_d, v_glu_w, v_glu_b, v_g_out_ssm, v_conv_w, v_g_out_conv, v_w_out, v_g_pre_ffn, v_g_post_ffn, v_w_up, v_ffn_conv_w, v_w_down):
    args = dict(locals())
    wts = {n: args[n] for n in WEIGHTS}
    mom_m = {n: args['m_' + n] for n in WEIGHTS}
    mom_v = {n: args['v_' + n] for n in WEIGHTS}
    T = x.shape[1]
    tm = min(512, T)
    tw = min(1024, T)
    tk = min(2048, T)
    me = _me()[3]
    xt, tgt = x[0], loss_target[0]

    c_all, w_in_s = _exchange([c, w_in[0].astype(BF16)], name='gather_first', scatter=False)
    c_all = c_all.reshape(N_DEV, D_MODEL)
    b_cols = lax.dynamic_slice(b_ada, (0, me * ADA_SHARD), (1, ADA_SHARD))
    mod_cols, c_act = _mod_cols(c_all, w_ada[0], b_cols)
    (mod_all,) = _exchange([mod_cols], name='gather_mod', scatter=False)
    mod = lax.dynamic_slice(mod_all, (0, me, 0), (N_DEV, 1, ADA_SHARD)).reshape(N_MOD, 1, D_MODEL)
    sh1, sc1, gt1, sh2, sc2, gt2 = [mod[k] for k in range(N_MOD)]


    lre_x, lim_x = _expand(ssm_lam_re[0]), _expand(ssm_lam_im[0])
    lst_x = jnp.broadcast_to(ssm_log_step[0][:, None], (N_GROUPS, SSM_STATE * SSM_GROUP))
    b_re_x = ssm_b_re[0].reshape(N_GROUPS, -1)
    b_im_x = ssm_b_im[0].reshape(N_GROUPS, -1)
    ar_x, ai_x, bbr_x, bbi_x = _ssm_prep(lre_x, lim_x, lst_x, b_re_x, b_im_x)
    lam_r = ar_x[:, ::SSM_GROUP].reshape(1, N_STATE)
    lam_i = ai_x[:, ::SSM_GROUP].reshape(1, N_STATE)
    big_b_re = _block_diag_b(bbr_x.reshape(N_GROUPS, SSM_STATE, SSM_GROUP)).astype(BF16)
    big_b_im = _block_diag_b(bbi_x.reshape(N_GROUPS, SSM_STATE, SSM_GROUP)).astype(BF16)
    big_c_re = _block_diag_c(ssm_c_re[0]).astype(BF16)
    big_c_im = _block_diag_c(ssm_c_im[0]).astype(BF16)
    head = jnp.arange(D_SSM)
    avg16 = jnp.where(head[:, None] // SSM_GROUP == head[None, :] // SSM_GROUP, 1.0 / SSM_GROUP, 0.0).astype(BF16)
    hd = D_CONV // CONV_HEADS
    avg64 = jnp.where(head[:, None] // hd == head[None, :] // hd, 1.0 / hd, 0.0).astype(BF16)

    w_up_t, half = w_up[0].T, D_MODEL // 2
    (proj, h1), (ffn_conv_s, conv_s, w_up_a) = _pre_mix(
        xt, sc1, sh1, g_pre_mix, w_in_s, tw, ([ffn_conv_w[0], conv_w[0], w_up_t[:, :half].astype(BF16)], False))
    cw_full = conv_s.transpose(1, 0, 2).reshape(3, D_CONV)
    u_perm = _to_scan_rows(proj[:, :D_SSM])
    (s_re, s_im, y_perm), (w_up_b, glu_s, w_out_s) = _ssm_fwd(
        u_perm, big_b_re, big_b_im, big_c_re, big_c_im, lam_r, lam_i,
        ([w_up_t[:, half:].astype(BF16), glu_w[0].astype(BF16), w_out[0].astype(BF16)], False))
    glu_full = glu_s.reshape(D_SSM, D_SSM)
    w_out_full = w_out_s.reshape(D_MODEL, D_MODEL)
    yssm = _from_scan_rows(y_perm)
    mix_args = (ssm_d, glu_full, glu_b, g_out_ssm, cw_full, g_out_conv, avg16, avg64)
    ycat = _mix_fwd(yssm, proj, *mix_args, tw)
    o, x1, h2 = _out_proj(ycat, w_out_full, xt, gt1, g_post_mix, g_pre_ffn, sc2, sh2, tw)
    (up8, hid8), (w_down_s,) = _ffn_up(h2, w_up_a, w_up_b, ffn_conv_s, tw, ([w_down[0].astype(BF16)], False))
    wd4 = w_down_s.reshape(4, FF_SHARD, D_MODEL)
    hid4 = hid8.reshape(2, 4, T, FF_SHARD)
    ddn, dx2, loss_parts, d_gt2, d_g_post_ffn = _ffn_down(hid4, wd4, x1, tgt, gt2, g_post_ffn, tm)
    loss_local = jnp.sum(loss_parts[:, 0, 0])

    got = {}
    dhid, g_w_down = _ffn_dact(ddn, wd4, hid4, tw)
    (dup8, dcw_ffn), (got['w_down'],) = _ffn_dup(dhid.reshape(N_DEV, T, FF_SHARD), up8, ffn_conv_s, tw,
                                                 ([g_w_down.reshape(N_DEV, D_FF // N_DEV, D_MODEL)], True))
    g_w_up_halves = _grad_tn(dup8, h2, pl.BlockSpec((None, tk, FF_SHARD), lambda g, k: (g, k, 0)),
                             pl.BlockSpec((tk, D_MODEL), lambda g, k: (k, 0)), N_DEV, FF_SHARD, D_MODEL, tk,
                             'grad_w_up', parts=2)
    (dx1, d_sh2, d_sc2, d_g_pre_ffn, d_o, d_gt1, d_g_post_mix), (got_up_0, got['ffn_conv_w']) = _pre_norm_bwd(
        dup8, pl.BlockSpec((2, tw, FF_SHARD), lambda i, j: (j, i, 0)), [w_up_a, w_up_b], x1, dx2, sc2, g_pre_ffn, tw,
        'ffn_in_bwd', ([g_w_up_halves[0], dcw_ffn], True), below=(o, gt1, g_post_mix), group=2, w_t=True)

    g_w_out = _grad_tn(ycat, d_o, pl.BlockSpec((tk, D_MODEL), lambda g, k: (k, 0)),
                       pl.BlockSpec((tk, D_MODEL), lambda g, k: (k, 0)), 1, D_MODEL, D_MODEL, tk, 'grad_w_out')
    (dy, dconv, dbg, z_b, dlin_b, sums), (got['w_out'],) = _mix_bwd(
        d_o, w_out_full, yssm, proj, *mix_args, tm, ([g_w_out.reshape(N_DEV, D_MODEL // N_DEV, D_MODEL)], True))
    g_glu_w = _grad_tn(z_b, dlin_b, pl.BlockSpec((tk, D_SSM), lambda g, k: (k, 0)),
                       pl.BlockSpec((tk, D_SSM), lambda g, k: (k, 0)), 1, D_SSM, D_SSM, tk, 'grad_glu_w')
    dy_perm = _to_scan_rows(dy)
    (du_perm, dbr_blk, dbi_blk, dcr_blk, dci_blk, dar_blk, dai_blk), (got_up_1, got['glu_w']) = _ssm_bwd(
        dy_perm, u_perm, s_re, s_im, big_b_re, big_b_im, big_c_re, big_c_im, lam_r, lam_i,
        ([g_w_up_halves[1], g_glu_w.reshape(N_DEV, D_SSM // N_DEV, D_SSM)], True))
    du_ssm = _from_scan_rows(du_perm)
    dproj = _mix_bwd_proj(dconv, proj, du_ssm, dy, ssm_d, dbg, cw_full, tw)
    dbb_re = _diag_blocks(dbr_blk, True).reshape(N_GROUPS, -1)
    dbb_im = _diag_blocks(dbi_blk, True).reshape(N_GROUPS, -1)
    d_c_re = _diag_blocks(dcr_blk, False).transpose(0, 2, 1)
    d_c_im = _diag_blocks(dci_blk, False).transpose(0, 2, 1)
    lane = jnp.arange(SSM_STATE * SSM_GROUP)
    seg = jnp.where(lane[:, None] // SSM_GROUP == lane[None, :] // SSM_GROUP, 1.0, 0.0).astype(BF16)
    d_b_re_x, d_b_im_x, d_lre_x, d_lim_x, d_lst = _ssm_prep_bwd(
        lre_x, lim_x, lst_x, b_re_x, b_im_x, dbb_re, dbb_im, _expand(dar_blk.reshape(N_GROUPS, SSM_STATE)),
        _expand(dai_blk.reshape(N_GROUPS, SSM_STATE)), seg)

    row = lambda a: a.reshape(-1, PACK_COLS)
    blank = jnp.zeros((1, PACK_COLS), F32)
    small_pack = jnp.concatenate([
        d_b_re_x, d_b_im_x, row(d_c_re), row(d_c_im), blank, blank, d_gt1, d_sh2, d_sc2, d_gt2, blank,
        d_g_post_mix, row(d_lre_x[:, ::SSM_GROUP]), row(d_lim_x[:, ::SSM_GROUP]),
        jnp.pad(d_lst.reshape(1, N_GROUPS), ((0, 0), (0, PACK_COLS - N_GROUPS))), row(sums[0:4]), d_g_pre_ffn,
        d_g_post_ffn, jnp.zeros((SMALL_ROWS - 145, PACK_COLS), F32)])
    g_w_in, (small_all,) = _grad_w_in(h1, dproj, tk, ([small_pack], False))
    g_conv_slots = jnp.concatenate([sums[4:7], jnp.zeros((5, D_CONV), F32)]).reshape(
        8, N_DEV, D_CONV // N_DEV).transpose(1, 0, 2)
    (grad_x, d_sh1, d_sc1, d_g_pre_mix), (got['w_in'], got['conv_w']) = _pre_norm_bwd(
        dproj, pl.BlockSpec((tw, D_IN_PROJ), lambda i, j: (i, j)), [w_in_s], xt, dx1, sc1, g_pre_mix, tw,
        'mix_in_bwd', ([g_w_in, g_conv_slots], True), group=N_DEV)
    late_pack = jnp.concatenate([d_sh1, d_sc1, d_g_pre_mix, jnp.full((1, PACK_COLS), loss_local, F32),
                                 jnp.zeros((4, PACK_COLS), F32)])
    (late_all,) = _exchange([late_pack], name='gather_late_grads', scatter=False)
    loss = jnp.sum(late_all[:, 3, 0])
    res = _adamw_small(small_all, late_all, wts, mom_m, mom_v)

    dmod_all = jnp.concatenate([late_all[:, 0:2, :], small_all[:, B_ADA_ROW + 2:B_ADA_ROW + N_MOD, :]],
                               axis=1).reshape(N_DEV, N_MOD * D_MODEL)
    dmod_cols = lax.dynamic_slice(dmod_all, (0, me * ADA_SHARD), (N_DEV, ADA_SHARD))
    g_w_ada = _grad_w_ada(c_act.T, dmod_cols)

    pieces = {n: [slots[:, :3, :] if n in ('conv_w', 'ffn_conv_w') else slots] for n, slots in got.items()}
    for n, parts in pieces.items():
        outs = _adamw(parts, wts[n][0], mom_m[n][0], mom_v[n][0], 'adamw_' + n)
        for kind, val in zip(('g', 'd', 'm', 'v'), outs):
            res[kind, n] = val[None]
    outs = _adamw([got_up_0, got_up_1], w_up[0].T, m_w_up[0].T, v_w_up[0].T, 'adamw_w_up')
    for kind, val in zip(('g', 'd', 'm', 'v'), outs):
        res[kind, 'w_up'] = val.T[None]
    outs = _adamw([g_w_ada[None]], w_ada[0], m_w_ada[0], v_w_ada[0], 'adamw_w_ada')
    for kind, val in zip(('g', 'd', 'm', 'v'), outs):
        res[kind, 'w_ada'] = val[None]

    return (loss, grad_x[None], *[res['g', n] for n in WEIGHTS], *[res['d', n] for n in WEIGHTS],
            *[res['m', n] for n in WEIGHTS], *[res['v', n] for n in WEIGHTS])
```

```python
import math

import jax
import jax.numpy as jnp
from jax import lax
from jax.experimental import pallas as pl
from jax.experimental.pallas import tpu as pltpu

F32, BF16 = jnp.float32, jnp.bfloat16

D_MODEL = 1024
D_SSM = 512
D_CONV = 512
SSM_GROUP = 16
N_GROUPS = 32
SSM_STATE = 64
N_STATE = N_GROUPS * SSM_STATE
CONV_HEADS = 8
D_FF = 2816
N_MOD = 6
D_IN_PROJ = D_SSM + 3 * D_CONV
N_DEV = 8
FF_SHARD = 2 * D_FF // N_DEV
IN_SHARD = D_IN_PROJ // N_DEV
ADA_SHARD = N_MOD * D_MODEL // N_DEV
EPS = 1e-6
LAMBDA_RE_MAX = -1e-4
ADAM_LR, ADAM_B1, ADAM_B2, ADAM_EPS, ADAM_WD, ADAM_STEP = 0.001, 0.9, 0.999, 1e-08, 0.01, 10
GELU_C = math.sqrt(2.0 / math.pi)
GELU_A = 0.044715

SUBLANES = 8
HALO = 8
HALO16 = 16
SCAN_UNROLL = 16
STATE_BLOCK = 512
CHAN_BLOCK = 128
VMEM_BIG = 48 << 20
VMEM_MOST = 58 << 20

WEIGHTS = ['w_ada', 'b_ada', 'g_pre_mix', 'g_post_mix', 'w_in', 'ssm_lam_re', 'ssm_lam_im', 'ssm_log_step',
           'ssm_b_re', 'ssm_b_im', 'ssm_c_re', 'ssm_c_im', 'ssm_d', 'glu_w', 'glu_b', 'g_out_ssm', 'conv_w',
           'g_out_conv', 'w_out', 'g_pre_ffn', 'g_post_ffn', 'w_up', 'ffn_conv_w', 'w_down']
PACK_COLS = 1024


def _call(body, *, name, grid, in_specs, out_specs, out_shape, scratch=(), sem=None, vmem=None, ride=None):
    params = {}
    if vmem is not None:
        params['vmem_limit_bytes'] = vmem
    if ride is None:
        if sem is not None:
            params['dimension_semantics'] = sem
        return pl.pallas_call(body, name=name, grid=grid, in_specs=in_specs, out_specs=out_specs,
                              out_shape=out_shape, scratch_shapes=list(scratch),
                              compiler_params=pltpu.CompilerParams(**params))
    arrs, scatter = ride
    single = not isinstance(out_shape, (list, tuple))
    out_shape_l = [out_shape] if single else list(out_shape)
    out_specs_l = [out_specs] if single else list(out_specs)
    n, n_in, n_out, n_scr = len(arrs), len(in_specs), len(out_shape_l), len(scratch)
    any_spec = pl.BlockSpec(memory_space=pl.ANY)
    params['dimension_semantics'] = ('arbitrary',) * len(grid)

    def carried(*refs):
        ins, rin = refs[:n_in], refs[n_in:n_in + n]
        outs, rout = refs[n_in + n:n_in + n + n_out], refs[n_in + n + n_out:n_in + 2 * n + n_out]
        scr, sems = refs[n_in + 2 * n + n_out:n_in + 2 * n + n_out + n_scr], refs[n_in + 2 * n + n_out + n_scr:]
        first = pl.program_id(0) == 0
        last = pl.program_id(0) == grid[0] - 1
        for ax in range(1, len(grid)):
            first = jnp.logical_and(first, pl.program_id(ax) == 0)
            last = jnp.logical_and(last, pl.program_id(ax) == grid[ax] - 1)

        @pl.when(first)
        def _():
            _exchange_start(rin, rout, sems, scatter)

        body(*ins, *outs, *scr)

        @pl.when(last)
        def _():
            _exchange_wait(rin, rout, sems, scatter)

    call = pl.pallas_call(carried, name=name, grid=grid, in_specs=list(in_specs) + [any_spec] * n,
                          out_specs=out_specs_l + [any_spec] * n,
                          out_shape=out_shape_l + _exchange_shapes(arrs, scatter),
                          scratch_shapes=list(scratch) + _exchange_sems(n),
                          compiler_params=pltpu.CompilerParams(**params))

    def run(*args):
        res = call(*args, *arrs)
        own = res[0] if single else list(res[:n_out])
        return own, list(res[n_out:])

    return run


def _const(shape):
    nd = len(shape)
    return pl.BlockSpec(shape, lambda *_: (0,) * nd)


def _sds(shape, dtype=F32):
    return jax.ShapeDtypeStruct(shape, dtype)


def _dot(a, b):
    return jnp.dot(a, b, preferred_element_type=F32)


def _dot_nt(a, b):
    return lax.dot_general(a, b, (((1,), (1,)), ((), ())), preferred_element_type=F32)


def _dot_tn(a, b):
    return lax.dot_general(a, b, (((0,), (0,)), ((), ())), preferred_element_type=F32)


def _dot_split(x, mat, parts):
    acc = None
    rem = x
    for _ in range(parts):
        piece = rem.astype(BF16)
        rem = rem - piece.astype(F32)
        term = _dot(piece, mat)
        acc = term if acc is None else acc + term
    return acc


def _sigmoid(x):
    return 1.0 / (1.0 + jnp.exp(-x))


def _gelu(x):
    t = jnp.tanh(GELU_C * (x + GELU_A * x * x * x))
    return 0.5 * x * (1.0 + t), t


def _gelu_grad(x, t):
    return 0.5 * (1.0 + t) + 0.5 * x * (1.0 - t * t) * GELU_C * (1.0 + 3.0 * GELU_A * x * x)


def _rsqrt_mean(x):
    return lax.rsqrt(jnp.mean(x * x, axis=-1, keepdims=True) + EPS)


def _colsum(x):
    return jnp.sum(x, axis=0, keepdims=True)


def _shifts_down(x, halo):
    ext = jnp.concatenate([halo, x], axis=0)
    return pltpu.roll(ext, 1, 0)[halo.shape[0]:], pltpu.roll(ext, 2, 0)[halo.shape[0]:]


def _shifts_up(x, halo):
    n = x.shape[0]
    ext = jnp.concatenate([x, halo], axis=0)
    total = ext.shape[0]
    return pltpu.roll(ext, total - 1, 0)[:n], pltpu.roll(ext, total - 2, 0)[:n]


def _conv3(x, halo, w_ref):
    x1, x2 = _shifts_down(x, halo)
    return w_ref[0:1, :] * x2 + w_ref[1:2, :] * x1 + w_ref[2:3, :] * x, x1, x2


def _conv3_t(g, halo, w_ref):
    g1, g2 = _shifts_up(g, halo)
    return w_ref[2:3, :] * g + w_ref[1:2, :] * g1 + w_ref[0:1, :] * g2, g1, g2


def _silu_parts(x):
    s = _sigmoid(x)
    return x * s, s * (1.0 + x * (1.0 - s))


def _norm_bwd(dn, x, r, g):
    gd = g * dn
    return r * gd - x * (r * r * r) * jnp.mean(gd * x, axis=-1, keepdims=True)


def _head_norm_bwd(dn, y, rs, g, avg):
    gd = g * dn
    return rs * gd - y * (rs * rs * rs) * _dot_split(gd * y, avg, 2)


def _me():
    x, y, c = lax.axis_index('x'), lax.axis_index('y'), lax.axis_index('c')
    return x, y, c, 4 * x + 2 * y + c


def _peer(k):
    x, y, c, _ = _me()
    px = 1 - x if k & 4 else x
    py = 1 - y if k & 2 else y
    pc = 1 - c if k & 1 else c
    return (px, py, pc), 4 * px + 2 * py + pc


SIBLING = 1
OTHER_CHIPS = (2, 4, 6)


def _remote(src, dst, sems, a, k, dev):
    return pltpu.make_async_remote_copy(src_ref=src, dst_ref=dst, send_sem=sems[0].at[a, k - 1],
                                        recv_sem=sems[1].at[a, k - 1], device_id=dev,
                                        device_id_type=pl.DeviceIdType.MESH)


def _exchange_copies(ins, outs, sems, scatter):
    me = _me()[3]
    local, first, relay, arrivals = [], [], [], []
    for a in range(len(ins)):
        src = ins[a].at[me] if scatter else ins[a]
        local.append(pltpu.make_async_copy(src, outs[a].at[me], sems[2].at[a]))
        for k in range(1, N_DEV):
            dev, idx = _peer(k)
            landed = _remote(src, outs[a].at[idx], sems, a, k, dev)
            if scatter:
                first.append(_remote(ins[a].at[idx], outs[a].at[me], sems, a, k, dev))
                arrivals.append(landed)
            elif k == SIBLING:
                first.append(_remote(src, outs[a].at[me], sems, a, k, dev))
                arrivals.append(landed)
            elif k in OTHER_CHIPS:
                first.append(_remote(src, outs[a].at[me], sems, a, k, dev))
                sib, _ = _peer(SIBLING)
                relay.append((landed, _remote(outs[a].at[idx], outs[a].at[idx], sems, a, k | SIBLING, sib)))
            else:
                arrivals.append(landed)
    return local, first, relay, arrivals


def _exchange_start(ins, outs, sems, scatter):
    local, first, _, _ = _exchange_copies(ins, outs, sems, scatter)
    for cp in local + first:
        cp.start()


def _exchange_wait(ins, outs, sems, scatter):
    local, first, relay, arrivals = _exchange_copies(ins, outs, sems, scatter)
    for landed, forward in relay:
        landed.wait_recv()
        forward.start()
    for cp in arrivals:
        cp.wait_recv()
    for cp in first + [forward for _, forward in relay]:
        cp.wait_send()
    for cp in local:
        cp.wait()


def _exchange_shapes(arrs, scatter):
    return [_sds(a.shape if scatter else (N_DEV,) + a.shape, a.dtype) for a in arrs]


def _exchange_sems(n):
    return [pltpu.SemaphoreType.DMA((n, N_DEV - 1)), pltpu.SemaphoreType.DMA((n, N_DEV - 1)),
            pltpu.SemaphoreType.DMA((n,))]


def _exchange(arrs, *, name, scatter):
    n = len(arrs)

    def body(*refs):
        _exchange_start(refs[:n], refs[n:2 * n], refs[2 * n:], scatter)
        _exchange_wait(refs[:n], refs[n:2 * n], refs[2 * n:], scatter)

    any_spec = pl.BlockSpec(memory_space=pl.ANY)
    outs = pl.pallas_call(body, name=name, out_shape=_exchange_shapes(arrs, scatter), in_specs=[any_spec] * n,
                          out_specs=[any_spec] * n, scratch_shapes=_exchange_sems(n))(*arrs)
    return list(outs)


def _mod_cols(c_all, w_ada, b_cols):
    def body(c_ref, w_ref, b_ref, mod_ref, act_ref):
        c = c_ref[...]
        act = c * _sigmoid(c)
        act_ref[...] = act
        mod_ref[...] = _dot(act.astype(BF16), w_ref[...].astype(BF16)) + b_ref[...]

    return _call(body, name='mod_cols', grid=(1,),
                 in_specs=[_const(c_all.shape), _const(w_ada.shape), _const(b_cols.shape)],
                 out_specs=[_const((N_DEV, ADA_SHARD)), _const(c_all.shape)],
                 out_shape=[_sds((N_DEV, ADA_SHARD)), _sds(c_all.shape)], vmem=VMEM_BIG)(c_all, w_ada, b_cols)


def _grad_w_ada(act_t, dmod_cols):
    def body(a_ref, d_ref, o_ref):
        o_ref[...] = _dot(a_ref[...], d_ref[...])

    return _call(body, name='grad_w_ada', grid=(1,), in_specs=[_const(act_t.shape), _const(dmod_cols.shape)],
                 out_specs=_const((D_MODEL, ADA_SHARD)), out_shape=_sds((D_MODEL, ADA_SHARD)),
                 vmem=VMEM_BIG)(act_t, dmod_cols)


def _pre_mix(x, sc, sh, g, w_s, tm, ride):
    T = x.shape[0]
    group = 4

    def body(x_ref, sc_ref, sh_ref, g_ref, w_ref, proj_ref, h_ref):
        @pl.when(pl.program_id(1) == 0)
        def _():
            xv = x_ref[...]
            h_ref[...] = ((xv * _rsqrt_mean(xv) * g_ref[...]) * (1.0 + sc_ref[...]) + sh_ref[...]).astype(BF16)

        for s in range(group):
            proj_ref[:, s * IN_SHARD:(s + 1) * IN_SHARD] = _dot(h_ref[...], w_ref[s])

    row = pl.BlockSpec((tm, D_MODEL), lambda i, j: (i, 0))
    vec = _const((1, D_MODEL))
    return _call(body, name='pre_mix', grid=(T // tm, N_DEV // group),
                 in_specs=[row, vec, vec, vec, pl.BlockSpec((group, D_MODEL, IN_SHARD), lambda i, j: (j, 0, 0))],
                 out_specs=[pl.BlockSpec((tm, group * IN_SHARD), lambda i, j: (i, j)), row],
                 out_shape=[_sds((T, D_IN_PROJ)), _sds((T, D_MODEL), BF16)],
                 sem=('parallel', 'arbitrary'), ride=ride)(x, sc, sh, g, w_s)


def _halo_before(tm, rows=HALO):
    return lambda i: jnp.maximum(i * (tm // rows) - 1, 0)


def _halo_after(tm, T, rows=HALO):
    return lambda i: jnp.minimum((i + 1) * (tm // rows), T // rows - 1)


def _mix_fwd(yssm, proj, d, glu_w, glu_b, g_ssm, cw, g_conv, avg16, avg64, tm):
    T = yssm.shape[0]
    hb = _halo_before(tm)

    def body(y_ref, p_ref, ph_ref, d_ref, gw_ref, gb_ref, gs_ref, cw_ref, gc_ref, a16_ref, a64_ref, o_ref):
        i = pl.program_id(0)
        u = p_ref[:, 0:D_SSM]
        y = y_ref[...] + d_ref[...] * u
        z, _ = _gelu(y)
        gate = _sigmoid(_dot(z.astype(BF16), gw_ref[...]) + gb_ref[...])
        ya = z * gate
        rs = lax.rsqrt(_dot_split(ya * ya, a16_ref[...], 2) + EPS)
        o_ref[:, 0:D_SSM] = (ya * rs * gs_ref[...]).astype(BF16)
        bg = p_ref[:, D_SSM:D_SSM + D_CONV]
        cv = p_ref[:, D_SSM + D_CONV:D_SSM + 2 * D_CONV] * p_ref[:, D_SSM + 2 * D_CONV:D_IN_PROJ]
        hv = ph_ref[:, D_SSM + D_CONV:D_SSM + 2 * D_CONV] * ph_ref[:, D_SSM + 2 * D_CONV:D_IN_PROJ]
        hv = jnp.where(i > 0, hv, 0.0)
        conv, _, _ = _conv3(cv, hv, cw_ref)
        yb = bg * conv
        rsb = lax.rsqrt(_dot_split(yb * yb, a64_ref[...], 2) + EPS)
        o_ref[:, D_SSM:D_MODEL] = (yb * rsb * gc_ref[...]).astype(BF16)

    vec = _const((1, D_SSM))
    sq = _const((D_SSM, D_SSM))
    return _call(body, name='mix_fwd', grid=(T // tm,),
                 in_specs=[pl.BlockSpec((tm, D_SSM), lambda i: (i, 0)), pl.BlockSpec((tm, D_IN_PROJ), lambda i: (i, 0)),
                           pl.BlockSpec((HALO, D_IN_PROJ), lambda i: (hb(i), 0)), vec, sq, vec, vec,
                           _const((3, D_CONV)), vec, sq, sq],
                 out_specs=pl.BlockSpec((tm, D_MODEL), lambda i: (i, 0)), out_shape=_sds((T, D_MODEL), BF16),
                 sem=('parallel',), vmem=VMEM_BIG)(yssm, proj, proj, d, glu_w, glu_b, g_ssm, cw, g_conv, avg16, avg64)


def _out_proj(ycat, w_out, x, gt, g_post, g_pre, sc, sh, tm):
    T = x.shape[0]

    def body(y_ref, w_ref, x_ref, gt_ref, gp_ref, g2_ref, sc_ref, sh_ref, o_ref, x1_ref, h_ref):
        o = _dot(y_ref[...], w_ref[...])
        o_ref[...] = o
        x1 = x_ref[...] + gt_ref[...] * (o * _rsqrt_mean(o) * gp_ref[...])
        x1_ref[...] = x1
        h_ref[...] = ((x1 * _rsqrt_mean(x1) * g2_ref[...]) * (1.0 + sc_ref[...]) + sh_ref[...]).astype(BF16)

    row = pl.BlockSpec((tm, D_MODEL), lambda i: (i, 0))
    vec = _const((1, D_MODEL))
    return _call(body, name='out_proj', grid=(T // tm,),
                 in_specs=[row, _const((D_MODEL, D_MODEL)), row, vec, vec, vec, vec, vec],
                 out_specs=[row, row, row],
                 out_shape=[_sds((T, D_MODEL)), _sds((T, D_MODEL)), _sds((T, D_MODEL), BF16)],
                 sem=('parallel',), vmem=VMEM_BIG)(ycat, w_out, x, gt, g_post, g_pre, sc, sh)


def _ffn_up(h2, w_a, w_b, cw8, tm, ride):
    T = h2.shape[0]
    hb = _halo_before(tm, HALO16)
    half = D_MODEL // 2

    def body(h_ref, hh_ref, wa_ref, wb_ref, cw_ref, up_ref, hid_ref):
        def times_w(ref, s):
            return _dot_nt(ref[:, :half], wa_ref[s]) + _dot_nt(ref[:, half:], wb_ref[s])

        for s in range(2):
            up = times_w(h_ref, s)
            up_ref[s] = up.astype(BF16)
            before = jnp.where(pl.program_id(0) > 0, times_w(hh_ref, s), 0.0)
            hid_ref[s] = _conv3(up, before, cw_ref.at[s])[0].astype(BF16)

    out = pl.BlockSpec((2, tm, FF_SHARD), lambda i, j: (j, i, 0))
    return _call(body, name='ffn_up', grid=(T // tm, N_DEV // 2),
                 in_specs=[pl.BlockSpec((tm, D_MODEL), lambda i, j: (i, 0)),
                           pl.BlockSpec((HALO16, D_MODEL), lambda i, j: (hb(i), 0)),
                           pl.BlockSpec((2, FF_SHARD, half), lambda i, j: (j, 0, 0)),
                           pl.BlockSpec((2, FF_SHARD, half), lambda i, j: (j, 0, 0)),
                           pl.BlockSpec((2, 3, FF_SHARD), lambda i, j: (j, 0, 0))],
                 out_specs=[out, out], out_shape=[_sds((N_DEV, T, FF_SHARD), BF16)] * 2,
                 sem=('parallel', 'parallel'), vmem=VMEM_BIG, ride=ride)(h2, h2, w_a, w_b, cw8)


def _ffn_down(hid4, wd4, x1, tgt, gt, g_post, tm):
    T = x1.shape[0]
    nb = T // tm

    def body(a_ref, w_ref, x1_ref, t_ref, gt_ref, g_ref, ddn_ref, dx_ref, loss_ref, dgt_ref, dg_ref, dn_ref):
        i, j = pl.program_id(0), pl.program_id(1)
        part = None
        for s in range(2):
            act = (_silu_parts(a_ref[0, s].astype(F32))[0] * a_ref[1, s].astype(F32)).astype(BF16)
            term = _dot(act, w_ref[s])
            part = term if part is None else part + term

        @pl.when(jnp.logical_and(i == 0, j == 0))
        def _():
            dgt_ref[...] = jnp.zeros_like(dgt_ref)
            dg_ref[...] = jnp.zeros_like(dg_ref)

        @pl.when(j == 0)
        def _():
            dn_ref[...] = part

        @pl.when(j > 0)
        def _():
            dn_ref[...] += part

        @pl.when(j == 1)
        def _():
            dn, gv, gate = dn_ref[...], g_ref[...], gt_ref[...]
            r = _rsqrt_mean(dn)
            normed = dn * r * gv
            err = x1_ref[...] + gate * normed - t_ref[...]
            dx = err * (1.0 / D_MODEL)
            dx_ref[...] = dx
            tot = jnp.sum(jnp.sum(err * err, axis=1, keepdims=True), axis=0, keepdims=True) * (0.5 / D_MODEL)
            loss_ref[...] = jnp.broadcast_to(tot, (8, 128))
            dgt_ref[...] += _colsum(dx * normed)
            dnn = dx * gate
            dg_ref[...] += _colsum(dnn * dn * r)
            ddn_ref[...] = _norm_bwd(dnn, dn, r, gv).astype(BF16)

    row = pl.BlockSpec((tm, D_MODEL), lambda i, j: (i, 0))
    vec = _const((1, D_MODEL))
    return _call(body, name='ffn_down', grid=(nb, 2),
                 in_specs=[pl.BlockSpec((2, 2, tm, FF_SHARD), lambda i, j: (0, j, i, 0)),
                           pl.BlockSpec((2, FF_SHARD, D_MODEL), lambda i, j: (j, 0, 0)), row, row, vec, vec],
                 out_specs=[row, row, pl.BlockSpec((None, 8, 128), lambda i, j: (i, 0, 0)), vec, vec],
                 out_shape=[_sds((T, D_MODEL), BF16), _sds((T, D_MODEL)), _sds((nb, 8, 128)), _sds((1, D_MODEL)),
                            _sds((1, D_MODEL))],
                 scratch=[pltpu.VMEM((tm, D_MODEL), F32)], sem=('arbitrary', 'arbitrary'),
                 vmem=VMEM_BIG)(hid4, wd4, x1, tgt, gt, g_post)


def _ssm_prep(lre, lim, lst, b_re, b_im):
    def body(lre_ref, lim_ref, lst_ref, br_ref, bi_ref, ar_ref, ai_ref, bbr_ref, bbi_ref):
        ar, ai, qr, qi = _zoh(lre_ref[...], lim_ref[...], lst_ref[...])[:4]
        ar_ref[...] = ar
        ai_ref[...] = ai
        bbr_ref[...] = qr * br_ref[...] - qi * bi_ref[...]
        bbi_ref[...] = qr * bi_ref[...] + qi * br_ref[...]

    shp = lre.shape
    return _call(body, name='ssm_prep', grid=(1,), in_specs=[_const(shp)] * 5, out_specs=[_const(shp)] * 4,
                 out_shape=[_sds(shp)] * 4)(lre, lim, lst, b_re, b_im)


def _zoh(lre, lim, lst):
    lr = jnp.minimum(lre, LAMBDA_RE_MAX)
    st = jnp.exp(lst)
    mag = jnp.exp(lr * st)
    ar = mag * jnp.cos(lim * st)
    ai = mag * jnp.sin(lim * st)
    den = lr * lr + lim * lim
    qr = ((ar - 1.0) * lr + ai * lim) / den
    qi = (ai * lr - (ar - 1.0) * lim) / den
    return ar, ai, qr, qi, lr, st, den


def _ssm_prep_bwd(lre, lim, lst, b_re, b_im, dbbr, dbbi, dar, dai, seg):
    def body(lre_ref, lim_ref, lst_ref, br_ref, bi_ref, dbbr_ref, dbbi_ref, dar_ref, dai_ref, seg_ref,
             dbr_ref, dbi_ref, dlre_ref, dlim_ref, dlst_ref):
        lre_v = lre_ref[...]
        li = lim_ref[...]
        ar, ai, qr, qi, lr, st, den = _zoh(lre_v, li, lst_ref[...])
        br, bi, gbr, gbi = br_ref[...], bi_ref[...], dbbr_ref[...], dbbi_ref[...]
        dbr_ref[...] = qr * gbr + qi * gbi
        dbi_ref[...] = qr * gbi - qi * gbr
        gqr = _dot_split(br * gbr + bi * gbi, seg_ref[...], 3)
        gqi = _dot_split(br * gbi - bi * gbr, seg_ref[...], 3)
        ir, ii = lr / den, -li / den
        gar = dar_ref[...] + ir * gqr + ii * gqi
        gai = dai_ref[...] + ir * gqi - ii * gqr
        tr, ti = qr * ir - qi * ii, qr * ii + qi * ir
        glr = -(tr * gqr + ti * gqi)
        gli = -(tr * gqi - ti * gqr)
        gzr = ar * gar + ai * gai
        gzi = ar * gai - ai * gar
        glr = glr + st * gzr
        gli = gli + st * gzi
        gst = (lr * gzr + li * gzi) * st
        dlre_ref[...] = jnp.where(lre_v < LAMBDA_RE_MAX, glr, 0.0)
        dlim_ref[...] = gli
        dlst_ref[...] = jnp.sum(gst, axis=1, keepdims=True) * (1.0 / SSM_GROUP)

    shp = lre.shape
    return _call(body, name='ssm_prep_bwd', grid=(1,), in_specs=[_const(shp)] * 9 + [_const(seg.shape)],
                 out_specs=[_const(shp)] * 4 + [_const((N_GROUPS, 1))],
                 out_shape=[_sds(shp)] * 4 + [_sds((N_GROUPS, 1))], vmem=VMEM_BIG)(
                     lre, lim, lst, b_re, b_im, dbbr, dbbi, dar, dai, seg)


def _scan_specs(T):
    return dict(
        chan=pl.BlockSpec((T, CHAN_BLOCK), lambda cb: (0, cb)),
        state=pl.BlockSpec((T, STATE_BLOCK), lambda cb: (0, cb)),
        b=pl.BlockSpec((CHAN_BLOCK, STATE_BLOCK), lambda cb: (cb, cb)),
        c=pl.BlockSpec((STATE_BLOCK, CHAN_BLOCK), lambda cb: (cb, cb)),
        lam=pl.BlockSpec((1, STATE_BLOCK), lambda cb: (0, cb)),
    )


def _complex_power(re, im, n):
    out = None
    while True:
        if n & 1:
            out = (re, im) if out is None else (out[0] * re - out[1] * im, out[0] * im + out[1] * re)
        n >>= 1
        if n == 0:
            return out
        re, im = re * re - im * im, 2.0 * re * im


def _rows8(i):
    if isinstance(i, int):
        return pl.ds(i * SUBLANES, SUBLANES)
    return pl.ds(pl.multiple_of(i * SUBLANES, SUBLANES), SUBLANES)


def _scan_loop(n_steps, body, init):
    trips = n_steps // SCAN_UNROLL

    def trip(t, carry):
        for u in range(SCAN_UNROLL):
            carry = body(t * SCAN_UNROLL + u, carry)
        return carry

    carry = lax.fori_loop(0, trips, trip, init)
    for step in range(trips * SCAN_UNROLL, n_steps):
        carry = body(step, carry)
    return carry


def _ssm_fwd(u_perm, b_re, b_im, c_re, c_im, lam_r, lam_i, ride):
    T = u_perm.shape[0]
    ls = T // SUBLANES
    rc = min(512, T)
    sp = _scan_specs(T)

    def body(u_ref, bre_ref, bim_ref, cre_ref, cim_ref, lr_ref, li_ref, so_re_ref, so_im_ref, y_ref, sre_ref, sim_ref):
        for c in range(T // rc):
            rows = pl.ds(c * rc, rc)
            ub = u_ref[rows, :].astype(BF16)
            sre_ref[rows, :] = _dot(ub, bre_ref[...])
            sim_ref[rows, :] = _dot(ub, bim_ref[...])
        shp = (SUBLANES, STATE_BLOCK)
        lr = jnp.broadcast_to(lr_ref[...], shp)
        li = jnp.broadcast_to(li_ref[...], shp)
        zero = jnp.zeros(shp, F32)

        def step(i, carry):
            sr, si = carry
            rows = _rows8(i)
            nr = lr * sr - li * si + sre_ref[rows, :]
            ni = lr * si + li * sr + sim_ref[rows, :]
            sre_ref[rows, :] = nr
            sim_ref[rows, :] = ni
            return nr, ni

        fr, fi = _scan_loop(ls, step, (zero, zero))
        pr, pi_ = _complex_power(lr, li, ls)
        row = lax.broadcasted_iota(jnp.int32, shp, 0)
        ir, ii = zero, zero
        for _ in range(SUBLANES - 1):
            er = fr + pr * ir - pi_ * ii
            ei = fi + pr * ii + pi_ * ir
            ir = jnp.where(row == 0, 0.0, pltpu.roll(er, 1, 0))
            ii = jnp.where(row == 0, 0.0, pltpu.roll(ei, 1, 0))

        def fix(i, carry):
            cr, ci = carry
            rows = _rows8(i)
            nr = lr * cr - li * ci
            ni = lr * ci + li * cr
            sre_ref[rows, :] += nr
            sim_ref[rows, :] += ni
            return nr, ni

        _scan_loop(ls, fix, (ir, ii))
        for c in range(T // rc):
            rows = pl.ds(c * rc, rc)
            s_r, s_i = sre_ref[rows, :].astype(BF16), sim_ref[rows, :].astype(BF16)
            so_re_ref[rows, :] = s_r
            so_im_ref[rows, :] = s_i
            y_ref[rows, :] = _dot(s_r, cre_ref[...]) - _dot(s_i, cim_ref[...])

    return _call(body, name='ssm_fwd', grid=(N_STATE // STATE_BLOCK,),
                 in_specs=[sp['chan'], sp['b'], sp['b'], sp['c'], sp['c'], sp['lam'], sp['lam']],
                 out_specs=[sp['state'], sp['state'], sp['chan']],
                 out_shape=[_sds((T, N_STATE), BF16), _sds((T, N_STATE), BF16), _sds((T, D_SSM))],
                 scratch=[pltpu.VMEM((T, STATE_BLOCK), F32), pltpu.VMEM((T, STATE_BLOCK), F32)],
                 sem=('arbitrary',), vmem=VMEM_MOST, ride=ride)(u_perm, b_re, b_im, c_re, c_im, lam_r, lam_i)


def _ssm_bwd(dy_perm, u_perm, s_re, s_im, b_re, b_im, c_re, c_im, lam_r, lam_i, ride):
    T = u_perm.shape[0]
    ls = T // SUBLANES
    rc = min(512, T)
    sp = _scan_specs(T)
    ncb = N_STATE // STATE_BLOCK

    def body(dy_ref, u_ref, sre_ref, sim_ref, bre_ref, bim_ref, cre_ref, cim_ref, lr_ref, li_ref,
             du_ref, dbr_ref, dbi_ref, dcr_ref, dci_ref, dar_ref, dai_ref, gre_ref, gim_ref):
        shp = (SUBLANES, STATE_BLOCK)
        zero = jnp.zeros(shp, F32)
        tail = pl.ds(T, SUBLANES)
        gre_ref[tail, :] = zero
        gim_ref[tail, :] = zero
        for c in range(T // rc):
            rows = pl.ds(c * rc, rc)
            dyb = dy_ref[rows, :].astype(BF16)
            gre_ref[rows, :] = _dot_nt(dyb, cre_ref[...])
            gim_ref[rows, :] = -_dot_nt(dyb, cim_ref[...])
        lr = jnp.broadcast_to(lr_ref[...], shp)
        li = jnp.broadcast_to(li_ref[...], shp)

        def step(k, carry):
            gr, gi = carry
            rows = _rows8(ls - 1 - k)
            nr = lr * gr + li * gi + gre_ref[rows, :]
            ni = lr * gi - li * gr + gim_ref[rows, :]
            gre_ref[rows, :] = nr
            gim_ref[rows, :] = ni
            return nr, ni

        fr, fi = _scan_loop(ls, step, (zero, zero))
        pr, pi_ = _complex_power(lr, -li, ls)
        row = lax.broadcasted_iota(jnp.int32, shp, 0)
        cr, ci = zero, zero
        for _ in range(SUBLANES - 1):
            er = fr + pr * cr - pi_ * ci
            ei = fi + pr * ci + pi_ * cr
            cr = jnp.where(row == SUBLANES - 1, 0.0, pltpu.roll(er, SUBLANES - 1, 0))
            ci = jnp.where(row == SUBLANES - 1, 0.0, pltpu.roll(ei, SUBLANES - 1, 0))

        def fix(k, carry):
            dr, di = carry
            rows = _rows8(ls - 1 - k)
            dr, di = lr * dr + li * di, lr * di - li * dr
            gre_ref[rows, :] += dr
            gim_ref[rows, :] += di
            return dr, di

        _scan_loop(ls, fix, (cr, ci))

        acc_r = jnp.zeros((1, STATE_BLOCK), F32)
        acc_i = jnp.zeros((1, STATE_BLOCK), F32)
        for c in range(T // rc):
            rows, nxt = pl.ds(c * rc, rc), pl.ds(c * rc + SUBLANES, rc)
            s_r, s_i = sre_ref[rows, :].astype(F32), sim_ref[rows, :].astype(F32)
            g_r, g_i = gre_ref[nxt, :], gim_ref[nxt, :]
            acc_r = acc_r + _colsum(g_r * s_r + g_i * s_i)
            acc_i = acc_i + _colsum(g_i * s_r - g_r * s_i)
        last = pl.ds(T - 2 * SUBLANES, 2 * SUBLANES)
        first = pl.ds(0, SUBLANES)
        spr = jnp.where(row == 0, 0.0, pltpu.roll(sre_ref[last, :].astype(F32)[SUBLANES:], 1, 0))
        spi = jnp.where(row == 0, 0.0, pltpu.roll(sim_ref[last, :].astype(F32)[SUBLANES:], 1, 0))
        gr, gi = gre_ref[first, :], gim_ref[first, :]
        dar_ref[...] = acc_r + _colsum(gr * spr + gi * spi)
        dai_ref[...] = acc_i + _colsum(gi * spr - gr * spi)

        for c in range(T // rc):
            rows = pl.ds(c * rc, rc)
            g_r, g_i = gre_ref[rows, :].astype(BF16), gim_ref[rows, :].astype(BF16)
            s_r, s_i = sre_ref[rows, :], sim_ref[rows, :]
            ub, dyb = u_ref[rows, :].astype(BF16), dy_ref[rows, :].astype(BF16)
            du_ref[rows, :] = _dot_nt(g_r, bre_ref[...]) + _dot_nt(g_i, bim_ref[...])
            parts = (_dot_tn(ub, g_r), _dot_tn(ub, g_i), _dot_tn(s_r, dyb), -_dot_tn(s_i, dyb))
            outs = (dbr_ref, dbi_ref, dcr_ref, dci_ref)
            for o_ref, part in zip(outs, parts):
                if c == 0:
                    o_ref[...] = part
                else:
                    o_ref[...] += part

    blk = lambda r, c: pl.BlockSpec((None, r, c), lambda cb: (cb, 0, 0))
    return _call(body, name='ssm_bwd', grid=(ncb,),
                 in_specs=[sp['chan'], sp['chan'], sp['state'], sp['state'], sp['b'], sp['b'], sp['c'], sp['c'],
                           sp['lam'], sp['lam']],
                 out_specs=[sp['chan'], blk(CHAN_BLOCK, STATE_BLOCK), blk(CHAN_BLOCK, STATE_BLOCK),
                            blk(STATE_BLOCK, CHAN_BLOCK), blk(STATE_BLOCK, CHAN_BLOCK), blk(1, STATE_BLOCK),
                            blk(1, STATE_BLOCK)],
                 out_shape=[_sds((T, D_SSM)), _sds((ncb, CHAN_BLOCK, STATE_BLOCK)), _sds((ncb, CHAN_BLOCK, STATE_BLOCK)),
                            _sds((ncb, STATE_BLOCK, CHAN_BLOCK)), _sds((ncb, STATE_BLOCK, CHAN_BLOCK)),
                            _sds((ncb, 1, STATE_BLOCK)), _sds((ncb, 1, STATE_BLOCK))],
                 scratch=[pltpu.VMEM((T + SUBLANES, STATE_BLOCK), F32), pltpu.VMEM((T + SUBLANES, STATE_BLOCK), F32)],
                 sem=('arbitrary',), vmem=VMEM_MOST, ride=ride)(dy_perm, u_perm, s_re, s_im, b_re, b_im, c_re, c_im,
                                                                lam_r, lam_i)


def _ffn_dact(ddn, wd4, hid4, tm):
    T = ddn.shape[0]
    nb = T // tm

    def body(d_ref, w_ref, hid_ref, o_ref, gw_ref, acc_ref):
        i = pl.program_id(1)
        d = d_ref[...]
        dact = _dot_nt(d, w_ref[...])
        silu, dsilu = _silu_parts(hid_ref[0].astype(F32))
        hid_v = hid_ref[1].astype(F32)
        o_ref[0] = (dact * hid_v * dsilu).astype(BF16)
        o_ref[1] = (dact * silu).astype(BF16)
        part = _dot_tn((silu * hid_v).astype(BF16), d)

        @pl.when(i == 0)
        def _():
            acc_ref[...] = part

        @pl.when(i > 0)
        def _():
            acc_ref[...] += part

        @pl.when(i == nb - 1)
        def _():
            gw_ref[...] = acc_ref[...].astype(BF16)

    blk = pl.BlockSpec((2, None, tm, FF_SHARD), lambda j, i: (0, j, i, 0))
    w_blk = pl.BlockSpec((None, FF_SHARD, D_MODEL), lambda j, i: (j, 0, 0))
    return _call(body, name='ffn_dact', grid=(4, nb),
                 in_specs=[pl.BlockSpec((tm, D_MODEL), lambda j, i: (i, 0)), w_blk, blk],
                 out_specs=[blk, w_blk],
                 out_shape=[_sds((2, 4, T, FF_SHARD), BF16), _sds((4, FF_SHARD, D_MODEL), BF16)],
                 scratch=[pltpu.VMEM((FF_SHARD, D_MODEL), F32)], sem=('parallel', 'arbitrary'),
                 vmem=VMEM_BIG)(ddn, wd4, hid4)


def _ffn_dup(dhid8, up8, cw8, tm, ride):
    T = up8.shape[1]
    nb = T // tm
    ha = _halo_after(tm, T, HALO16)

    def body(dh_ref, dha_ref, up_ref, cw_ref, dup_ref, dcw_ref):
        i = pl.program_id(1)

        @pl.when(i == 0)
        def _():
            dcw_ref[...] = jnp.zeros_like(dcw_ref)

        dh = dh_ref[...].astype(F32)
        dup, dh1, dh2 = _conv3_t(dh, jnp.where(i < nb - 1, dha_ref[...].astype(F32), 0.0), cw_ref)
        dup_ref[...] = dup.astype(BF16)
        up = up_ref[...].astype(F32)
        dcw_ref[0:1, :] += _colsum(dh2 * up)
        dcw_ref[1:2, :] += _colsum(dh1 * up)
        dcw_ref[2:3, :] += _colsum(dh * up)

    main = pl.BlockSpec((None, tm, FF_SHARD), lambda j, i: (j, i, 0))
    return _call(body, name='ffn_dup', grid=(N_DEV, nb),
                 in_specs=[main, pl.BlockSpec((None, HALO16, FF_SHARD), lambda j, i: (j, ha(i), 0)), main,
                           pl.BlockSpec((None, 3, FF_SHARD), lambda j, i: (j, 0, 0))],
                 out_specs=[main, pl.BlockSpec((None, 8, FF_SHARD), lambda j, i: (j, 0, 0))],
                 out_shape=[_sds((N_DEV, T, FF_SHARD), BF16), _sds((N_DEV, 8, FF_SHARD))],
                 sem=('parallel', 'arbitrary'), vmem=VMEM_BIG, ride=ride)(dhid8, dhid8, up8, cw8)


def _grad_tn(a, b, a_spec, b_spec, groups, m, n, tk, name, ride=None, parts=1):
    T = a.shape[-2]
    nk = T // tk
    mp = m // parts

    def body(a_ref, b_ref, *refs):
        o_refs, acc_ref = refs[:parts], refs[parts]
        k = pl.program_id(1)
        part = _dot_tn(a_ref[...], b_ref[...])

        @pl.when(k == 0)
        def _():
            acc_ref[...] = part

        @pl.when(k > 0)
        def _():
            acc_ref[...] += part

        @pl.when(k == nk - 1)
        def _():
            for p, o_ref in enumerate(o_refs):
                o_ref[...] = acc_ref[p * mp:(p + 1) * mp, :].astype(BF16)

    out_spec = pl.BlockSpec((None, mp, n), lambda g, k: (g, 0, 0))
    res = _call(body, name=name, grid=(groups, nk), in_specs=[a_spec, b_spec], out_specs=[out_spec] * parts,
                out_shape=[_sds((groups, mp, n), BF16)] * parts, scratch=[pltpu.VMEM((m, n), F32)],
                sem=('parallel', 'arbitrary'), vmem=VMEM_BIG, ride=ride)(a, b)
    if parts > 1:
        return res
    return res[0] if ride is None else (res[0][0], res[1])


def _grad_w_in(h1, dproj, tk, ride):
    T = h1.shape[0]
    nk = T // tk
    half = D_IN_PROJ // 2

    def body(a_ref, b_ref, o_ref, acc_ref):
        k = pl.program_id(0)
        for h in range(2):
            cols = slice(h * half, (h + 1) * half)
            part = _dot_tn(a_ref[...], b_ref[:, cols])

            @pl.when(k == 0)
            def _():
                acc_ref[:, cols] = part

            @pl.when(k > 0)
            def _():
                acc_ref[:, cols] += part

        @pl.when(k == nk - 1)
        def _():
            for g in range(N_DEV):
                o_ref[g] = acc_ref[:, g * IN_SHARD:(g + 1) * IN_SHARD].astype(BF16)

    return _call(body, name='grad_w_in', grid=(nk,),
                 in_specs=[pl.BlockSpec((tk, D_MODEL), lambda k: (k, 0)), pl.BlockSpec((tk, D_IN_PROJ), lambda k: (k, 0))],
                 out_specs=_const((N_DEV, D_MODEL, IN_SHARD)), out_shape=_sds((N_DEV, D_MODEL, IN_SHARD), BF16),
                 scratch=[pltpu.VMEM((D_MODEL, D_IN_PROJ), F32)], sem=('arbitrary',), vmem=VMEM_BIG, ride=ride)(h1, dproj)


def _pre_norm_bwd(dz, dz_spec, w_parts, xin, dres, sc, g, tm, name, ride, below=None, group=1, w_t=False):
    T = xin.shape[0]
    n = w_parts[0].shape[1] if w_t else w_parts[0].shape[2]
    mul = _dot if w_t else _dot_nt
    steps = N_DEV // group
    width = D_MODEL // len(w_parts)

    def body(dz_ref, *refs):
        w_refs, (x_ref, dr_ref, sc_ref, g_ref), refs = refs[:len(w_parts)], refs[len(w_parts):len(w_parts) + 4], \
            refs[len(w_parts) + 4:]
        if below is None:
            dx_ref, dsh_ref, dsc_ref, dg_ref = refs
            sums = (dsh_ref, dsc_ref, dg_ref)
        else:
            v_ref, gate_ref, g2_ref, dx_ref, dsh_ref, dsc_ref, dg_ref, dv_ref, dgate_ref, dg2_ref = refs
            sums = (dsh_ref, dsc_ref, dg_ref, dgate_ref, dg2_ref)
        i, j = pl.program_id(0), pl.program_id(1)
        piece = (lambda s: dz_ref[s]) if dz.ndim == 3 else (lambda s: dz_ref[:, s * n:(s + 1) * n])
        parts = []
        for w_ref in w_refs:
            part = mul(piece(0), w_ref[0])
            for s in range(1, group):
                part = part + mul(piece(s), w_ref[s])
            parts.append(part)

        @pl.when(jnp.logical_and(i == 0, j == 0))
        def _():
            for s_ref in sums:
                s_ref[...] = jnp.zeros_like(s_ref)

        @pl.when(j == 0)
        def _():
            for k, part in enumerate(parts):
                dx_ref[:, k * width:(k + 1) * width] = part

        @pl.when(j > 0)
        def _():
            for k, part in enumerate(parts):
                dx_ref[:, k * width:(k + 1) * width] += part

        @pl.when(j == steps - 1)
        def _():
            dh, xv, gv = dx_ref[...], x_ref[...], g_ref[...]
            r = _rsqrt_mean(xv)
            dsh_ref[...] += _colsum(dh)
            dsc_ref[...] += _colsum(dh * (xv * r * gv))
            dxn = dh * (1.0 + sc_ref[...])
            dg_ref[...] += _colsum(dxn * xv * r)
            dx = dr_ref[...] + _norm_bwd(dxn, xv, r, gv)
            dx_ref[...] = dx
            if below is not None:
                v, g2 = v_ref[...], g2_ref[...]
                rv = _rsqrt_mean(v)
                dgate_ref[...] += _colsum(dx * (v * rv * g2))
                dn = dx * gate_ref[...]
                dg2_ref[...] += _colsum(dn * v * rv)
                dv_ref[...] = _norm_bwd(dn, v, rv, g2).astype(BF16)

    row = pl.BlockSpec((tm, D_MODEL), lambda i, j: (i, 0))
    vec = _const((1, D_MODEL))
    in_specs = [dz_spec] + [pl.BlockSpec((group,) + w.shape[1:], lambda i, j: (j, 0, 0)) for w in w_parts]
    in_specs += [row, row, vec, vec]
    out_specs = [row, vec, vec, vec]
    out_shape = [_sds((T, D_MODEL)), _sds((1, D_MODEL)), _sds((1, D_MODEL)), _sds((1, D_MODEL))]
    args = [dz, *w_parts, xin, dres, sc, g]
    if below is not None:
        in_specs += [row, vec, vec]
        out_specs += [row, vec, vec]
        out_shape += [_sds((T, D_MODEL), BF16), _sds((1, D_MODEL)), _sds((1, D_MODEL))]
        args += list(below)
    return _call(body, name=name, grid=(T // tm, steps), in_specs=in_specs, out_specs=out_specs,
                 out_shape=out_shape, sem=('arbitrary', 'arbitrary'), vmem=VMEM_MOST, ride=ride)(*args)


def _mix_bwd(d_o, w_out, yssm, proj, d, glu_w, glu_b, g_ssm, cw, g_conv, avg16, avg64, tm, ride):
    T = yssm.shape[0]
    hb = _halo_before(tm)

    def body(do_ref, wo_ref, y_ref, p_ref, ph_ref, d_ref, gw_ref, gb_ref, gs_ref, cw_ref, gc_ref, a16_ref, a64_ref,
             dy_ref, dconv_ref, dbg_ref, z_ref, dlin_ref, acc_ref):
        i = pl.program_id(0)
        dyc = _dot_nt(do_ref[...], wo_ref[...])

        @pl.when(i == 0)
        def _():
            acc_ref[...] = jnp.zeros_like(acc_ref)

        u = p_ref[:, 0:D_SSM]
        y = y_ref[...] + d_ref[...] * u
        z, t = _gelu(y)
        gate = _sigmoid(_dot(z.astype(BF16), gw_ref[...]) + gb_ref[...])
        ya = z * gate
        rs = lax.rsqrt(_dot_split(ya * ya, a16_ref[...], 2) + EPS)
        dna = dyc[:, 0:D_SSM]
        acc_ref[1:2, :] += _colsum(dna * ya * rs)
        dya = _head_norm_bwd(dna, ya, rs, gs_ref[...], a16_ref[...])
        dlin = dya * z * gate * (1.0 - gate)
        acc_ref[0:1, :] += _colsum(dlin)
        dlin_b = dlin.astype(BF16)
        dz = dya * gate + _dot_nt(dlin_b, gw_ref[...])
        dy = dz * _gelu_grad(y, t)
        acc_ref[3:4, :] += _colsum(dy * u)
        dy_ref[...] = dy
        z_ref[...] = z.astype(BF16)
        dlin_ref[...] = dlin_b

        bg = p_ref[:, D_SSM:D_SSM + D_CONV]
        cv = p_ref[:, D_SSM + D_CONV:D_SSM + 2 * D_CONV] * p_ref[:, D_SSM + 2 * D_CONV:D_IN_PROJ]
        hv = ph_ref[:, D_SSM + D_CONV:D_SSM + 2 * D_CONV] * ph_ref[:, D_SSM + 2 * D_CONV:D_IN_PROJ]
        hv = jnp.where(i > 0, hv, 0.0)
        conv, cv1, cv2 = _conv3(cv, hv, cw_ref)
        yb = bg * conv
        rsb = lax.rsqrt(_dot_split(yb * yb, a64_ref[...], 2) + EPS)
        dnb = dyc[:, D_SSM:D_MODEL]
        acc_ref[2:3, :] += _colsum(dnb * yb * rsb)
        dyb = _head_norm_bwd(dnb, yb, rsb, gc_ref[...], a64_ref[...])
        dbg_ref[...] = dyb * conv
        dconv = dyb * bg
        dconv_ref[...] = dconv
        acc_ref[4:5, :] += _colsum(dconv * cv2)
        acc_ref[5:6, :] += _colsum(dconv * cv1)
        acc_ref[6:7, :] += _colsum(dconv * cv)

    vec = _const((1, D_SSM))
    sq = _const((D_SSM, D_SSM))
    half = pl.BlockSpec((tm, D_SSM), lambda i: (i, 0))
    return _call(body, name='mix_bwd', grid=(T // tm,),
                 in_specs=[pl.BlockSpec((tm, D_MODEL), lambda i: (i, 0)), _const((D_MODEL, D_MODEL)), half,
                           pl.BlockSpec((tm, D_IN_PROJ), lambda i: (i, 0)),
                           pl.BlockSpec((HALO, D_IN_PROJ), lambda i: (hb(i), 0)), vec, sq, vec, vec,
                           _const((3, D_CONV)), vec, sq, sq],
                 out_specs=[half, half, half, half, half, _const((8, D_SSM))],
                 out_shape=[_sds((T, D_SSM)), _sds((T, D_SSM)), _sds((T, D_SSM)), _sds((T, D_SSM), BF16),
                            _sds((T, D_SSM), BF16), _sds((8, D_SSM))],
                 sem=('arbitrary',), vmem=VMEM_BIG, ride=ride)(d_o, w_out, yssm, proj, proj, d, glu_w, glu_b, g_ssm, cw,
                                                              g_conv, avg16, avg64)


def _mix_bwd_proj(dconv, proj, du_ssm, dy, d, dbg, cw, tm):
    T = dy.shape[0]
    nb = T // tm
    ha = _halo_after(tm, T)

    def body(dc_ref, dch_ref, cg_ref, v_ref, du_ref, dy_ref, d_ref, dbg_ref, cw_ref, o_ref):
        i = pl.program_id(0)
        dcv = _conv3_t(dc_ref[...], jnp.where(i < nb - 1, dch_ref[...], 0.0), cw_ref)[0]
        o_ref[:, 0:D_SSM] = (du_ref[...] + dy_ref[...] * d_ref[...]).astype(BF16)
        o_ref[:, D_SSM:D_SSM + D_CONV] = dbg_ref[...].astype(BF16)
        o_ref[:, D_SSM + D_CONV:D_SSM + 2 * D_CONV] = (dcv * v_ref[...]).astype(BF16)
        o_ref[:, D_SSM + 2 * D_CONV:D_IN_PROJ] = (dcv * cg_ref[...]).astype(BF16)

    half = pl.BlockSpec((tm, D_SSM), lambda i: (i, 0))
    return _call(body, name='mix_bwd_proj', grid=(nb,),
                 in_specs=[half, pl.BlockSpec((HALO, D_CONV), lambda i: (ha(i), 0)),
                           pl.BlockSpec((tm, D_CONV), lambda i: (i, 2)), pl.BlockSpec((tm, D_CONV), lambda i: (i, 3)),
                           half, half, _const((1, D_SSM)), half, _const((3, D_CONV))],
                 out_specs=pl.BlockSpec((tm, D_IN_PROJ), lambda i: (i, 0)), out_shape=_sds((T, D_IN_PROJ), BF16),
                 sem=('parallel',), vmem=VMEM_BIG)(dconv, dconv, proj, proj, du_ssm, dy, d, dbg, cw)


ADAMW_SLOT_BYTES = 8 << 20
ADAMW_ROW_BYTES = 3 << 19


def _row_tile(rows, cols, slots):
    for cand in range(rows, 15, -1):
        if (rows % cand == 0 and cand % 16 == 0 and slots * cand * cols * 4 <= ADAMW_SLOT_BYTES
                and cand * cols * 4 <= ADAMW_ROW_BYTES):
            return cand
    return rows


def _adamw_math(g, w, m, v):
    m2 = ADAM_B1 * m + (1.0 - ADAM_B1) * g
    v2 = ADAM_B2 * v + (1.0 - ADAM_B2) * (g * g)
    m_hat = m2 / (1.0 - ADAM_B1 ** ADAM_STEP)
    v_hat = v2 / (1.0 - ADAM_B2 ** ADAM_STEP)
    return -ADAM_LR * (m_hat / (jnp.sqrt(v_hat) + ADAM_EPS) + ADAM_WD * w), m2, v2


def _adamw(pieces, w, m, v, name):
    slots, _, cols = pieces[0].shape
    rows = sum(p.shape[1] for p in pieces)
    tr = _row_tile(pieces[0].shape[1], cols, slots)
    starts, pos = [], 0
    for p in pieces:
        assert p.shape[1] % tr == 0
        starts.append(pos)
        pos += p.shape[1] // tr

    def body(*refs):
        g_refs = refs[:len(pieces)]
        w_ref, m_ref, v_ref, go_ref, d_ref, mo_ref, vo_ref = refs[len(pieces):]
        i = pl.program_id(0)
        g = None
        for g_ref, start in zip(g_refs, starts):
            part = g_ref[0].astype(F32)
            for s in range(1, slots):
                part = part + g_ref[s].astype(F32)
            g = part if g is None else jnp.where(i >= start, part, g)
        go_ref[...] = g
        d_ref[...], mo_ref[...], vo_ref[...] = _adamw_math(g, w_ref[...], m_ref[...], v_ref[...])

    def piece_spec(start, count):
        return pl.BlockSpec((slots, tr, cols), lambda i: (0, jnp.clip(i - start, 0, count - 1), 0))

    blk = pl.BlockSpec((tr, cols), lambda i: (i, 0))
    return _call(body, name=name, grid=(rows // tr,),
                 in_specs=[piece_spec(s, p.shape[1] // tr) for s, p in zip(starts, pieces)] + [blk, blk, blk],
                 out_specs=[blk] * 4, out_shape=[_sds((rows, cols))] * 4, sem=('parallel',),
                 vmem=VMEM_BIG)(*pieces, w, m, v)


def _to_scan_rows(a):
    T, n = a.shape
    return a.reshape(SUBLANES, T // SUBLANES, n).transpose(1, 0, 2).reshape(T, n)


def _from_scan_rows(a):
    T, n = a.shape
    return a.reshape(T // SUBLANES, SUBLANES, n).transpose(1, 0, 2).reshape(T, n)


def _expand(a):
    return jnp.repeat(a, SSM_GROUP, axis=1)


def _block_diag(rows, row_group, col_group):
    r, n = rows.shape
    tiled = jnp.tile(rows, (1, N_GROUPS))
    keep = (jnp.arange(r)[:, None] // row_group) == (jnp.arange(n * N_GROUPS)[None, :] // col_group)
    return jnp.where(keep, tiled, 0.0)


def _block_diag_b(bb):
    return _block_diag(bb.transpose(0, 2, 1).reshape(D_SSM, SSM_STATE), SSM_GROUP, SSM_STATE)


def _block_diag_c(cc):
    return _block_diag(cc.transpose(0, 2, 1).reshape(N_STATE, SSM_GROUP), SSM_STATE, SSM_GROUP)


def _diag_blocks(x, chan_major):
    per = CHAN_BLOCK // SSM_GROUP
    eye = jnp.eye(per, dtype=x.dtype)
    if chan_major:
        x = x.reshape(-1, per, SSM_GROUP, per, SSM_STATE) * eye[None, :, None, :, None]
        return x.sum(axis=1).transpose(0, 2, 3, 1).reshape(N_GROUPS, SSM_STATE, SSM_GROUP)
    x = x.reshape(-1, per, SSM_STATE, per, SSM_GROUP) * eye[None, :, None, :, None]
    return x.sum(axis=3).reshape(N_GROUPS, SSM_STATE, SSM_GROUP)


SMALL_LAYOUT = {
    'ssm_b_re': (0, 0, 32, 1024), 'ssm_b_im': (32, 0, 32, 1024), 'ssm_c_re': (64, 0, 32, 1024),
    'ssm_c_im': (96, 0, 32, 1024), 'b_ada': (128, 0, 6, 1024), 'g_pre_mix': (134, 0, 1, 1024),
    'g_post_mix': (135, 0, 1, 1024), 'ssm_lam_re': (136, 0, 2, 1024), 'ssm_lam_im': (138, 0, 2, 1024),
    'ssm_log_step': (140, 0, 1, 32), 'glu_b': (141, 0, 1, 512), 'g_out_ssm': (141, 512, 1, 512),
    'g_out_conv': (142, 0, 1, 512), 'ssm_d': (142, 512, 1, 512), 'g_pre_ffn': (143, 0, 1, 1024),
    'g_post_ffn': (144, 0, 1, 1024)}
SMALL_ROWS = 152
B_ADA_ROW = SMALL_LAYOUT['b_ada'][0]
LATE_ROWS = {('b_ada', 0): 0, ('b_ada', 1): 1, ('g_pre_mix', 0): 2}


def _adamw_small(gathered, late, wts, mom_m, mom_v):
    names = list(SMALL_LAYOUT)
    n = len(names)

    def body(*refs):
        g_ref, late_ref, ins, outs = refs[0], refs[1], refs[2:2 + 3 * n], refs[2 + 3 * n:]
        for p, name in enumerate(names):
            r0, c0, rows, cols = SMALL_LAYOUT[name]
            pieces = [(0, rows)] if rows % 8 == 0 else [(r, 1) for r in range(rows)]
            for r, cnt in pieces:
                src_ref, first = (late_ref, LATE_ROWS[name, r]) if (name, r) in LATE_ROWS else (g_ref, r0 + r)
                g = src_ref[0, first:first + cnt, c0:c0 + cols]
                for s in range(1, N_DEV):
                    g = g + src_ref[s, first:first + cnt, c0:c0 + cols]
                w, m, v = (ins[3 * p + q][r:r + cnt, :] for q in range(3))
                res = (g,) + _adamw_math(g, w, m, v)
                for q in range(4):
                    outs[4 * p + q][r:r + cnt, :] = res[q]

    shapes = [SMALL_LAYOUT[name][2:] for name in names]
    args = [gathered, late]
    for name, shp in zip(names, shapes):
        args += [wts[name].reshape(shp), mom_m[name].reshape(shp), mom_v[name].reshape(shp)]
    outs = _call(body, name='adamw_small', grid=(1,),
                 in_specs=[_const(gathered.shape), _const(late.shape)]
                 + [_const(shp) for shp in shapes for _ in range(3)],
                 out_specs=[_const(shp) for shp in shapes for _ in range(4)],
                 out_shape=[_sds(shp) for shp in shapes for _ in range(4)], vmem=VMEM_BIG)(*args)
    res = {}
    for p, name in enumerate(names):
        for q, kind in enumerate(('g', 'd', 'm', 'v')):
            res[kind, name] = outs[4 * p + q].reshape(wts[name].shape)
    return res


def kernel(x, c, w_ada, b_ada, g_pre_mix, g_post_mix, w_in, ssm_lam_re, ssm_lam_im, ssm_log_step, ssm_b_re, ssm_b_im, ssm_c_re, ssm_c_im, ssm_d, glu_w, glu_b, g_out_ssm, conv_w, g_out_conv, w_out, g_pre_ffn, g_post_ffn, w_up, ffn_conv_w, w_down, loss_target, m_w_ada, m_b_ada, m_g_pre_mix, m_g_post_mix, m_w_in, m_ssm_lam_re, m_ssm_lam_im, m_ssm_log_step, m_ssm_b_re, m_ssm_b_im, m_ssm_c_re, m_ssm_c_im, m_ssm_d, m_glu_w, m_glu_b, m_g_out_ssm, m_conv_w, m_g_out_conv, m_w_out, m_g_pre_ffn, m_g_post_ffn, m_w_up, m_ffn_conv_w, m_w_down, v_w_ada, v_b_ada, v_g_pre_mix, v_g_post_mix, v_w_in, v_ssm_lam_re, v_ssm_lam_im, v_ssm_log_step, v_ssm_b_re, v_ssm_b_im, v_ssm_c_re, v_ssm_c_im, v_ssm_d, v_glu_w, v_glu_b, v_g_out_ssm, v_conv_w, v_g_out_conv, v_w_out, v_g_pre_ffn, v_g_post_ffn, v_w_up, v_ffn_conv_w, v_w_down):
    args = dict(locals())
    wts = {n: args[n] for n in WEIGHTS}
    mom_m = {n: args['m_' + n] for n in WEIGHTS}
    mom_v = {n: args['v_' + n] for n in WEIGHTS}
    T = x.shape[1]
    tm = min(512, T)
    tw = min(1024, T)
    tk = min(2048, T)
    me = _me()[3]
    xt, tgt = x[0], loss_target[0]

    c_all, w_in_s = _exchange([c, w_in[0].astype(BF16)], name='gather_first', scatter=False)
    c_all = c_all.reshape(N_DEV, D_MODEL)
    b_cols = lax.dynamic_slice(b_ada, (0, me * ADA_SHARD), (1, ADA_SHARD))
    mod_cols, c_act = _mod_cols(c_all, w_ada[0], b_cols)
    (mod_all,) = _exchange([mod_cols], name='gather_mod', scatter=False)
    mod = lax.dynamic_slice(mod_all, (0, me, 0), (N_DEV, 1, ADA_SHARD)).reshape(N_MOD, 1, D_MODEL)
    sh1, sc1, gt1, sh2, sc2, gt2 = [mod[k] for k in range(N_MOD)]


    lre_x, lim_x = _expand(ssm_lam_re[0]), _expand(ssm_lam_im[0])
    lst_x = jnp.broadcast_to(ssm_log_step[0][:, None], (N_GROUPS, SSM_STATE * SSM_GROUP))
    b_re_x = ssm_b_re[0].reshape(N_GROUPS, -1)
    b_im_x = ssm_b_im[0].reshape(N_GROUPS, -1)
    ar_x, ai_x, bbr_x, bbi_x = _ssm_prep(lre_x, lim_x, lst_x, b_re_x, b_im_x)
    lam_r = ar_x[:, ::SSM_GROUP].reshape(1, N_STATE)
    lam_i = ai_x[:, ::SSM_GROUP].reshape(1, N_STATE)
    big_b_re = _block_diag_b(bbr_x.reshape(N_GROUPS, SSM_STATE, SSM_GROUP)).astype(BF16)
    big_b_im = _block_diag_b(bbi_x.reshape(N_GROUPS, SSM_STATE, SSM_GROUP)).astype(BF16)
    big_c_re = _block_diag_c(ssm_c_re[0]).astype(BF16)
    big_c_im = _block_diag_c(ssm_c_im[0]).astype(BF16)
    head = jnp.arange(D_SSM)
    avg16 = jnp.where(head[:, None] // SSM_GROUP == head[None, :] // SSM_GROUP, 1.0 / SSM_GROUP, 0.0).astype(BF16)
    hd = D_CONV // CONV_HEADS
    avg64 = jnp.where(head[:, None] // hd == head[None, :] // hd, 1.0 / hd, 0.0).astype(BF16)

    w_up_t, half = w_up[0].T, D_MODEL // 2
    (proj, h1), (ffn_conv_s, conv_s, w_up_a) = _pre_mix(
        xt, sc1, sh1, g_pre_mix, w_in_s, tw, ([ffn_conv_w[0], conv_w[0], w_up_t[:, :half].astype(BF16)], False))
    cw_full = conv_s.transpose(1, 0, 2).reshape(3, D_CONV)
    u_perm = _to_scan_rows(proj[:, :D_SSM])
    (s_re, s_im, y_perm), (w_up_b, glu_s, w_out_s) = _ssm_fwd(
        u_perm, big_b_re, big_b_im, big_c_re, big_c_im, lam_r, lam_i,
        ([w_up_t[:, half:].astype(BF16), glu_w[0].astype(BF16), w_out[0].astype(BF16)], False))
    glu_full = glu_s.reshape(D_SSM, D_SSM)
    w_out_full = w_out_s.reshape(D_MODEL, D_MODEL)
    yssm = _from_scan_rows(y_perm)
    mix_args = (ssm_d, glu_full, glu_b, g_out_ssm, cw_full, g_out_conv, avg16, avg64)
    ycat = _mix_fwd(yssm, proj, *mix_args, tw)
    o, x1, h2 = _out_proj(ycat, w_out_full, xt, gt1, g_post_mix, g_pre_ffn, sc2, sh2, tw)
    (up8, hid8), (w_down_s,) = _ffn_up(h2, w_up_a, w_up_b, ffn_conv_s, tw, ([w_down[0].astype(BF16)], False))
    wd4 = w_down_s.reshape(4, FF_SHARD, D_MODEL)
    hid4 = hid8.reshape(2, 4, T, FF_SHARD)
    ddn, dx2, loss_parts, d_gt2, d_g_post_ffn = _ffn_down(hid4, wd4, x1, tgt, gt2, g_post_ffn, tm)
    loss_local = jnp.sum(loss_parts[:, 0, 0])

    got = {}
    dhid, g_w_down = _ffn_dact(ddn, wd4, hid4, tw)
    (dup8, dcw_ffn), (got['w_down'],) = _ffn_dup(dhid.reshape(N_DEV, T, FF_SHARD), up8, ffn_conv_s, tw,
                                                 ([g_w_down.reshape(N_DEV, D_FF // N_DEV, D_MODEL)], True))
    g_w_up_halves = _grad_tn(dup8, h2, pl.BlockSpec((None, tk, FF_SHARD), lambda g, k: (g, k, 0)),
                             pl.BlockSpec((tk, D_MODEL), lambda g, k: (k, 0)), N_DEV, FF_SHARD, D_MODEL, tk,
                             'grad_w_up', parts=2)
    (dx1, d_sh2, d_sc2, d_g_pre_ffn, d_o, d_gt1, d_g_post_mix), (got_up_0, got['ffn_conv_w']) = _pre_norm_bwd(
        dup8, pl.BlockSpec((2, tw, FF_SHARD), lambda i, j: (j, i, 0)), [w_up_a, w_up_b], x1, dx2, sc2, g_pre_ffn, tw,
        'ffn_in_bwd', ([g_w_up_halves[0], dcw_ffn], True), below=(o, gt1, g_post_mix), group=2, w_t=True)

    g_w_out = _grad_tn(ycat, d_o, pl.BlockSpec((tk, D_MODEL), lambda g, k: (k, 0)),
                       pl.BlockSpec((tk, D_MODEL), lambda g, k: (k, 0)), 1, D_MODEL, D_MODEL, tk, 'grad_w_out')
    (dy, dconv, dbg, z_b, dlin_b, sums), (got['w_out'],) = _mix_bwd(
        d_o, w_out_full, yssm, proj, *mix_args, tm, ([g_w_out.reshape(N_DEV, D_MODEL // N_DEV, D_MODEL)], True))
    g_glu_w = _grad_tn(z_b, dlin_b, pl.BlockSpec((tk, D_SSM), lambda g, k: (k, 0)),
                       pl.BlockSpec((tk, D_SSM), lambda g, k: (k, 0)), 1, D_SSM, D_SSM, tk, 'grad_glu_w')
    dy_perm = _to_scan_rows(dy)
    (du_perm, dbr_blk, dbi_blk, dcr_blk, dci_blk, dar_blk, dai_blk), (got_up_1, got['glu_w']) = _ssm_bwd(
        dy_perm, u_perm, s_re, s_im, big_b_re, big_b_im, big_c_re, big_c_im, lam_r, lam_i,
        ([g_w_up_halves[1], g_glu_w.reshape(N_DEV, D_SSM // N_DEV, D_SSM)], True))
    du_ssm = _from_scan_rows(du_perm)
    dproj = _mix_bwd_proj(dconv, proj, du_ssm, dy, ssm_d, dbg, cw_full, tw)
    dbb_re = _diag_blocks(dbr_blk, True).reshape(N_GROUPS, -1)
    dbb_im = _diag_blocks(dbi_blk, True).reshape(N_GROUPS, -1)
    d_c_re = _diag_blocks(dcr_blk, False).transpose(0, 2, 1)
    d_c_im = _diag_blocks(dci_blk, False).transpose(0, 2, 1)
    lane = jnp.arange(SSM_STATE * SSM_GROUP)
    seg = jnp.where(lane[:, None] // SSM_GROUP == lane[None, :] // SSM_GROUP, 1.0, 0.0).astype(BF16)
    d_b_re_x, d_b_im_x, d_lre_x, d_lim_x, d_lst = _ssm_prep_bwd(
        lre_x, lim_x, lst_x, b_re_x, b_im_x, dbb_re, dbb_im, _expand(dar_blk.reshape(N_GROUPS, SSM_STATE)),
        _expand(dai_blk.reshape(N_GROUPS, SSM_STATE)), seg)

    row = lambda a: a.reshape(-1, PACK_COLS)
    blank = jnp.zeros((1, PACK_COLS), F32)
    small_pack = jnp.concatenate([
        d_b_re_x, d_b_im_x, row(d_c_re), row(d_c_im), blank, blank, d_gt1, d_sh2, d_sc2, d_gt2, blank,
        d_g_post_mix, row(d_lre_x[:, ::SSM_GROUP]), row(d_lim_x[:, ::SSM_GROUP]),
        jnp.pad(d_lst.reshape(1, N_GROUPS), ((0, 0), (0, PACK_COLS - N_GROUPS))), row(sums[0:4]), d_g_pre_ffn,
        d_g_post_ffn, jnp.zeros((SMALL_ROWS - 145, PACK_COLS), F32)])
    g_w_in, (small_all,) = _grad_w_in(h1, dproj, tk, ([small_pack], False))
    g_conv_slots = jnp.concatenate([sums[4:7], jnp.zeros((5, D_CONV), F32)]).reshape(
        8, N_DEV, D_CONV // N_DEV).transpose(1, 0, 2)
    (grad_x, d_sh1, d_sc1, d_g_pre_mix), (got['w_in'], got['conv_w']) = _pre_norm_bwd(
        dproj, pl.BlockSpec((tw, 4 * IN_SHARD), lambda i, j: (i, j)), [w_in_s], xt, dx1, sc1, g_pre_mix, tw,
        'mix_in_bwd', ([g_w_in, g_conv_slots], True), group=4)
    late_pack = jnp.concatenate([d_sh1, d_sc1, d_g_pre_mix, jnp.full((1, PACK_COLS), loss_local, F32),
                                 jnp.zeros((4, PACK_COLS), F32)])
    (late_all,) = _exchange([late_pack], name='gather_late_grads', scatter=False)
    loss = jnp.sum(late_all[:, 3, 0])
    res = _adamw_small(small_all, late_all, wts, mom_m, mom_v)

    dmod_all = jnp.concatenate([late_all[:, 0:2, :], small_all[:, B_ADA_ROW + 2:B_ADA_ROW + N_MOD, :]],
                               axis=1).reshape(N_DEV, N_MOD * D_MODEL)
    dmod_cols = lax.dynamic_slice(dmod_all, (0, me * ADA_SHARD), (N_DEV, ADA_SHARD))
    g_w_ada = _grad_w_ada(c_act.T, dmod_cols)

    pieces = {n: [slots[:, :3, :] if n in ('conv_w', 'ffn_conv_w') else slots] for n, slots in got.items()}
    for n, parts in pieces.items():
        outs = _adamw(parts, wts[n][0], mom_m[n][0], mom_v[n][0], 'adamw_' + n)
        for kind, val in zip(('g', 'd', 'm', 'v'), outs):
            res[kind, n] = val[None]
    outs = _adamw([got_up_0, got_up_1], w_up[0].T, m_w_up[0].T, v_w_up[0].T, 'adamw_w_up')
    for kind, val in zip(('g', 'd', 'm', 'v'), outs):
        res[kind, 'w_up'] = val.T[None]
    outs = _adamw([g_w_ada[None]], w_ada[0], m_w_ada[0], v_w_ada[0], 'adamw_w_ada')
    for kind, val in zip(('g', 'd', 'm', 'v'), outs):
        res[kind, 'w_ada'] = val[None]

    return (loss, grad_x[None], *[res['g', n] for n in WEIGHTS], *[res['d', n] for n in WEIGHTS],
            *[res['m', n] for n in WEIGHTS], *[res['v', n] for n in WEIGHTS])
```

```python
import math

import jax
import jax.numpy as jnp
from jax import lax
from jax.experimental import pallas as pl
from jax.experimental.pallas import tpu as pltpu

F32, BF16 = jnp.float32, jnp.bfloat16

D_MODEL = 1024
D_SSM = 512
D_CONV = 512
SSM_GROUP = 16
N_GROUPS = 32
SSM_STATE = 64
N_STATE = N_GROUPS * SSM_STATE
CONV_HEADS = 8
D_FF = 2816
N_MOD = 6
D_IN_PROJ = D_SSM + 3 * D_CONV
N_DEV = 8
FF_SHARD = 2 * D_FF // N_DEV
IN_SHARD = D_IN_PROJ // N_DEV
ADA_SHARD = N_MOD * D_MODEL // N_DEV
EPS = 1e-6
LAMBDA_RE_MAX = -1e-4
ADAM_LR, ADAM_B1, ADAM_B2, ADAM_EPS, ADAM_WD, ADAM_STEP = 0.001, 0.9, 0.999, 1e-08, 0.01, 10
GELU_C = math.sqrt(2.0 / math.pi)
GELU_A = 0.044715

SUBLANES = 8
HALO = 8
HALO16 = 16
SCAN_UNROLL = 16
STATE_BLOCK = 512
CHAN_BLOCK = 128
VMEM_BIG = 48 << 20
VMEM_MOST = 58 << 20

WEIGHTS = ['w_ada', 'b_ada', 'g_pre_mix', 'g_post_mix', 'w_in', 'ssm_lam_re', 'ssm_lam_im', 'ssm_log_step',
           'ssm_b_re', 'ssm_b_im', 'ssm_c_re', 'ssm_c_im', 'ssm_d', 'glu_w', 'glu_b', 'g_out_ssm', 'conv_w',
           'g_out_conv', 'w_out', 'g_pre_ffn', 'g_post_ffn', 'w_up', 'ffn_conv_w', 'w_down']
PACK_COLS = 1024


def _call(body, *, name, grid, in_specs, out_specs, out_shape, scratch=(), sem=None, vmem=None, ride=None):
    params = {}
    if vmem is not None:
        params['vmem_limit_bytes'] = vmem
    if ride is None:
        if sem is not None:
            params['dimension_semantics'] = sem
        return pl.pallas_call(body, name=name, grid=grid, in_specs=in_specs, out_specs=out_specs,
                              out_shape=out_shape, scratch_shapes=list(scratch),
                              compiler_params=pltpu.CompilerParams(**params))
    arrs, scatter = ride
    single = not isinstance(out_shape, (list, tuple))
    out_shape_l = [out_shape] if single else list(out_shape)
    out_specs_l = [out_specs] if single else list(out_specs)
    n, n_in, n_out, n_scr = len(arrs), len(in_specs), len(out_shape_l), len(scratch)
    any_spec = pl.BlockSpec(memory_space=pl.ANY)
    params['dimension_semantics'] = ('arbitrary',) * len(grid)

    def carried(*refs):
        ins, rin = refs[:n_in], refs[n_in:n_in + n]
        outs, rout = refs[n_in + n:n_in + n + n_out], refs[n_in + n + n_out:n_in + 2 * n + n_out]
        scr, sems = refs[n_in + 2 * n + n_out:n_in + 2 * n + n_out + n_scr], refs[n_in + 2 * n + n_out + n_scr:]
        first = pl.program_id(0) == 0
        last = pl.program_id(0) == grid[0] - 1
        for ax in range(1, len(grid)):
            first = jnp.logical_and(first, pl.program_id(ax) == 0)
            last = jnp.logical_and(last, pl.program_id(ax) == grid[ax] - 1)

        @pl.when(first)
        def _():
            _exchange_start(rin, rout, sems, scatter)

        body(*ins, *outs, *scr)

        @pl.when(last)
        def _():
            _exchange_wait(rin, rout, sems, scatter)

    call = pl.pallas_call(carried, name=name, grid=grid, in_specs=list(in_specs) + [any_spec] * n,
                          out_specs=out_specs_l + [any_spec] * n,
                          out_shape=out_shape_l + _exchange_shapes(arrs, scatter),
                          scratch_shapes=list(scratch) + _exchange_sems(n),
                          compiler_params=pltpu.CompilerParams(**params))

    def run(*args):
        res = call(*args, *arrs)
        own = res[0] if single else list(res[:n_out])
        return own, list(res[n_out:])

    return run


def _const(shape):
    nd = len(shape)
    return pl.BlockSpec(shape, lambda *_: (0,) * nd)


def _sds(shape, dtype=F32):
    return jax.ShapeDtypeStruct(shape, dtype)


def _dot(a, b):
    return jnp.dot(a, b, preferred_element_type=F32)


def _dot_nt(a, b):
    return lax.dot_general(a, b, (((1,), (1,)), ((), ())), preferred_element_type=F32)


def _dot_tn(a, b):
    return lax.dot_general(a, b, (((0,), (0,)), ((), ())), preferred_element_type=F32)


def _dot_split(x, mat, parts):
    acc = None
    rem = x
    for _ in range(parts):
        piece = rem.astype(BF16)
        rem = rem - piece.astype(F32)
        term = _dot(piece, mat)
        acc = term if acc is None else acc + term
    return acc


def _sigmoid(x):
    return 1.0 / (1.0 + jnp.exp(-x))


def _gelu(x):
    t = jnp.tanh(GELU_C * (x + GELU_A * x * x * x))
    return 0.5 * x * (1.0 + t), t


def _gelu_grad(x, t):
    return 0.5 * (1.0 + t) + 0.5 * x * (1.0 - t * t) * GELU_C * (1.0 + 3.0 * GELU_A * x * x)


def _rsqrt_mean(x):
    return lax.rsqrt(jnp.mean(x * x, axis=-1, keepdims=True) + EPS)


def _colsum(x):
    return jnp.sum(x, axis=0, keepdims=True)


def _shifts_down(x, halo):
    ext = jnp.concatenate([halo, x], axis=0)
    return pltpu.roll(ext, 1, 0)[halo.shape[0]:], pltpu.roll(ext, 2, 0)[halo.shape[0]:]


def _shifts_up(x, halo):
    n = x.shape[0]
    ext = jnp.concatenate([x, halo], axis=0)
    total = ext.shape[0]
    return pltpu.roll(ext, total - 1, 0)[:n], pltpu.roll(ext, total - 2, 0)[:n]


def _conv3(x, halo, w_ref):
    x1, x2 = _shifts_down(x, halo)
    return w_ref[0:1, :] * x2 + w_ref[1:2, :] * x1 + w_ref[2:3, :] * x, x1, x2


def _conv3_t(g, halo, w_ref):
    g1, g2 = _shifts_up(g, halo)
    return w_ref[2:3, :] * g + w_ref[1:2, :] * g1 + w_ref[0:1, :] * g2, g1, g2


def _silu_parts(x):
    s = _sigmoid(x)
    return x * s, s * (1.0 + x * (1.0 - s))


def _norm_bwd(dn, x, r, g):
    gd = g * dn
    return r * gd - x * (r * r * r) * jnp.mean(gd * x, axis=-1, keepdims=True)


def _head_norm_bwd(dn, y, rs, g, avg):
    gd = g * dn
    return rs * gd - y * (rs * rs * rs) * _dot_split(gd * y, avg, 2)


def _me():
    x, y, c = lax.axis_index('x'), lax.axis_index('y'), lax.axis_index('c')
    return x, y, c, 4 * x + 2 * y + c


def _peer(k):
    x, y, c, _ = _me()
    px = 1 - x if k & 4 else x
    py = 1 - y if k & 2 else y
    pc = 1 - c if k & 1 else c
    return (px, py, pc), 4 * px + 2 * py + pc


SIBLING = 1
OTHER_CHIPS = (2, 4, 6)


def _remote(src, dst, sems, a, k, dev):
    return pltpu.make_async_remote_copy(src_ref=src, dst_ref=dst, send_sem=sems[0].at[a, k - 1],
                                        recv_sem=sems[1].at[a, k - 1], device_id=dev,
                                        device_id_type=pl.DeviceIdType.MESH)


def _exchange_copies(ins, outs, sems, scatter):
    me = _me()[3]
    local, first, relay, arrivals = [], [], [], []
    for a in range(len(ins)):
        src = ins[a].at[me] if scatter else ins[a]
        local.append(pltpu.make_async_copy(src, outs[a].at[me], sems[2].at[a]))
        for k in range(1, N_DEV):
            dev, idx = _peer(k)
            landed = _remote(src, outs[a].at[idx], sems, a, k, dev)
            if scatter:
                first.append(_remote(ins[a].at[idx], outs[a].at[me], sems, a, k, dev))
                arrivals.append(landed)
            elif k == SIBLING:
                first.append(_remote(src, outs[a].at[me], sems, a, k, dev))
                arrivals.append(landed)
            elif k in OTHER_CHIPS:
                first.append(_remote(src, outs[a].at[me], sems, a, k, dev))
                sib, _ = _peer(SIBLING)
                relay.append((landed, _remote(outs[a].at[idx], outs[a].at[idx], sems, a, k | SIBLING, sib)))
            else:
                arrivals.append(landed)
    return local, first, relay, arrivals


def _exchange_start(ins, outs, sems, scatter):
    local, first, _, _ = _exchange_copies(ins, outs, sems, scatter)
    for cp in local + first:
        cp.start()


def _exchange_wait(ins, outs, sems, scatter):
    local, first, relay, arrivals = _exchange_copies(ins, outs, sems, scatter)
    for landed, forward in relay:
        landed.wait_recv()
        forward.start()
    for cp in arrivals:
        cp.wait_recv()
    for cp in first + [forward for _, forward in relay]:
        cp.wait_send()
    for cp in local:
        cp.wait()


def _exchange_shapes(arrs, scatter):
    return [_sds(a.shape if scatter else (N_DEV,) + a.shape, a.dtype) for a in arrs]


def _exchange_sems(n):
    return [pltpu.SemaphoreType.DMA((n, N_DEV - 1)), pltpu.SemaphoreType.DMA((n, N_DEV - 1)),
            pltpu.SemaphoreType.DMA((n,))]


def _exchange(arrs, *, name, scatter):
    n = len(arrs)

    def body(*refs):
        _exchange_start(refs[:n], refs[n:2 * n], refs[2 * n:], scatter)
        _exchange_wait(refs[:n], refs[n:2 * n], refs[2 * n:], scatter)

    any_spec = pl.BlockSpec(memory_space=pl.ANY)
    outs = pl.pallas_call(body, name=name, out_shape=_exchange_shapes(arrs, scatter), in_specs=[any_spec] * n,
                          out_specs=[any_spec] * n, scratch_shapes=_exchange_sems(n))(*arrs)
    return list(outs)


def _mod_cols(c_all, w_ada, b_cols):
    def body(c_ref, w_ref, b_ref, mod_ref, act_ref):
        c = c_ref[...]
        act = c * _sigmoid(c)
        act_ref[...] = act
        mod_ref[...] = _dot(act.astype(BF16), w_ref[...].astype(BF16)) + b_ref[...]

    return _call(body, name='mod_cols', grid=(1,),
                 in_specs=[_const(c_all.shape), _const(w_ada.shape), _const(b_cols.shape)],
                 out_specs=[_const((N_DEV, ADA_SHARD)), _const(c_all.shape)],
                 out_shape=[_sds((N_DEV, ADA_SHARD)), _sds(c_all.shape)], vmem=VMEM_BIG)(c_all, w_ada, b_cols)


def _grad_w_ada(act_t, dmod_cols):
    def body(a_ref, d_ref, o_ref):
        o_ref[...] = _dot(a_ref[...], d_ref[...])

    return _call(body, name='grad_w_ada', grid=(1,), in_specs=[_const(act_t.shape), _const(dmod_cols.shape)],
                 out_specs=_const((D_MODEL, ADA_SHARD)), out_shape=_sds((D_MODEL, ADA_SHARD)),
                 vmem=VMEM_BIG)(act_t, dmod_cols)


def _pre_mix(x, sc, sh, g, w_s, tm, ride):
    T = x.shape[0]
    group = 4

    def body(x_ref, sc_ref, sh_ref, g_ref, w_ref, proj_ref, h_ref):
        @pl.when(pl.program_id(1) == 0)
        def _():
            xv = x_ref[...]
            h_ref[...] = ((xv * _rsqrt_mean(xv) * g_ref[...]) * (1.0 + sc_ref[...]) + sh_ref[...]).astype(BF16)

        for s in range(group):
            proj_ref[:, s * IN_SHARD:(s + 1) * IN_SHARD] = _dot(h_ref[...], w_ref[s])

    row = pl.BlockSpec((tm, D_MODEL), lambda i, j: (i, 0))
    vec = _const((1, D_MODEL))
    return _call(body, name='pre_mix', grid=(T // tm, N_DEV // group),
                 in_specs=[row, vec, vec, vec, pl.BlockSpec((group, D_MODEL, IN_SHARD), lambda i, j: (j, 0, 0))],
                 out_specs=[pl.BlockSpec((tm, group * IN_SHARD), lambda i, j: (i, j)), row],
                 out_shape=[_sds((T, D_IN_PROJ)), _sds((T, D_MODEL), BF16)],
                 sem=('parallel', 'arbitrary'), ride=ride)(x, sc, sh, g, w_s)


def _halo_before(tm, rows=HALO):
    return lambda i: jnp.maximum(i * (tm // rows) - 1, 0)


def _halo_after(tm, T, rows=HALO):
    return lambda i: jnp.minimum((i + 1) * (tm // rows), T // rows - 1)


def _mix_fwd(yssm, proj, d, glu_w, glu_b, g_ssm, cw, g_conv, avg16, avg64, tm):
    T = yssm.shape[0]
    hb = _halo_before(tm)

    def body(y_ref, p_ref, ph_ref, d_ref, gw_ref, gb_ref, gs_ref, cw_ref, gc_ref, a16_ref, a64_ref, o_ref):
        i = pl.program_id(0)
        u = p_ref[:, 0:D_SSM]
        y = y_ref[...] + d_ref[...] * u
        z, _ = _gelu(y)
        gate = _sigmoid(_dot(z.astype(BF16), gw_ref[...]) + gb_ref[...])
        ya = z * gate
        rs = lax.rsqrt(_dot_split(ya * ya, a16_ref[...], 2) + EPS)
        o_ref[:, 0:D_SSM] = (ya * rs * gs_ref[...]).astype(BF16)
        bg = p_ref[:, D_SSM:D_SSM + D_CONV]
        cv = p_ref[:, D_SSM + D_CONV:D_SSM + 2 * D_CONV] * p_ref[:, D_SSM + 2 * D_CONV:D_IN_PROJ]
        hv = ph_ref[:, D_SSM + D_CONV:D_SSM + 2 * D_CONV] * ph_ref[:, D_SSM + 2 * D_CONV:D_IN_PROJ]
        hv = jnp.where(i > 0, hv, 0.0)
        conv, _, _ = _conv3(cv, hv, cw_ref)
        yb = bg * conv
        rsb = lax.rsqrt(_dot_split(yb * yb, a64_ref[...], 2) + EPS)
        o_ref[:, D_SSM:D_MODEL] = (yb * rsb * gc_ref[...]).astype(BF16)

    vec = _const((1, D_SSM))
    sq = _const((D_SSM, D_SSM))
    return _call(body, name='mix_fwd', grid=(T // tm,),
                 in_specs=[pl.BlockSpec((tm, D_SSM), lambda i: (i, 0)), pl.BlockSpec((tm, D_IN_PROJ), lambda i: (i, 0)),
                           pl.BlockSpec((HALO, D_IN_PROJ), lambda i: (hb(i), 0)), vec, sq, vec, vec,
                           _const((3, D_CONV)), vec, sq, sq],
                 out_specs=pl.BlockSpec((tm, D_MODEL), lambda i: (i, 0)), out_shape=_sds((T, D_MODEL), BF16),
                 sem=('parallel',), vmem=VMEM_BIG)(yssm, proj, proj, d, glu_w, glu_b, g_ssm, cw, g_conv, avg16, avg64)


def _out_proj(ycat, w_out, x, gt, g_post, g_pre, sc, sh, tm):
    T = x.shape[0]

    def body(y_ref, w_ref, x_ref, gt_ref, gp_ref, g2_ref, sc_ref, sh_ref, o_ref, x1_ref, h_ref):
        o = _dot(y_ref[...], w_ref[...])
        o_ref[...] = o
        x1 = x_ref[...] + gt_ref[...] * (o * _rsqrt_mean(o) * gp_ref[...])
        x1_ref[...] = x1
        h_ref[...] = ((x1 * _rsqrt_mean(x1) * g2_ref[...]) * (1.0 + sc_ref[...]) + sh_ref[...]).astype(BF16)

    row = pl.BlockSpec((tm, D_MODEL), lambda i: (i, 0))
    vec = _const((1, D_MODEL))
    return _call(body, name='out_proj', grid=(T // tm,),
                 in_specs=[row, _const((D_MODEL, D_MODEL)), row, vec, vec, vec, vec, vec],
                 out_specs=[row, row, row],
                 out_shape=[_sds((T, D_MODEL)), _sds((T, D_MODEL)), _sds((T, D_MODEL), BF16)],
                 sem=('parallel',), vmem=VMEM_BIG)(ycat, w_out, x, gt, g_post, g_pre, sc, sh)


def _ffn_up(h2, w_a, w_b, cw8, tm, ride):
    T = h2.shape[0]
    hb = _halo_before(tm, HALO16)
    half = D_MODEL // 2

    def body(h_ref, hh_ref, wa_ref, wb_ref, cw_ref, up_ref, hid_ref):
        def times_w(ref, s):
            return _dot_nt(ref[:, :half], wa_ref[s]) + _dot_nt(ref[:, half:], wb_ref[s])

        for s in range(2):
            up = times_w(h_ref, s)
            up_ref[s] = up.astype(BF16)
            before = jnp.where(pl.program_id(0) > 0, times_w(hh_ref, s), 0.0)
            hid_ref[s] = _conv3(up, before, cw_ref.at[s])[0].astype(BF16)

    out = pl.BlockSpec((2, tm, FF_SHARD), lambda i, j: (j, i, 0))
    return _call(body, name='ffn_up', grid=(T // tm, N_DEV // 2),
                 in_specs=[pl.BlockSpec((tm, D_MODEL), lambda i, j: (i, 0)),
                           pl.BlockSpec((HALO16, D_MODEL), lambda i, j: (hb(i), 0)),
                           pl.BlockSpec((2, FF_SHARD, half), lambda i, j: (j, 0, 0)),
                           pl.BlockSpec((2, FF_SHARD, half), lambda i, j: (j, 0, 0)),
                           pl.BlockSpec((2, 3, FF_SHARD), lambda i, j: (j, 0, 0))],
                 out_specs=[out, out], out_shape=[_sds((N_DEV, T, FF_SHARD), BF16)] * 2,
                 sem=('parallel', 'parallel'), vmem=VMEM_BIG, ride=ride)(h2, h2, w_a, w_b, cw8)


def _ffn_down(hid4, wd4, x1, tgt, gt, g_post, tm):
    T = x1.shape[0]
    nb = T // tm

    def body(a_ref, w_ref, x1_ref, t_ref, gt_ref, g_ref, ddn_ref, dx_ref, loss_ref, dgt_ref, dg_ref, dn_ref):
        i, j = pl.program_id(0), pl.program_id(1)
        part = None
        for s in range(2):
            act = (_silu_parts(a_ref[0, s].astype(F32))[0] * a_ref[1, s].astype(F32)).astype(BF16)
            term = _dot(act, w_ref[s])
            part = term if part is None else part + term

        @pl.when(jnp.logical_and(i == 0, j == 0))
        def _():
            dgt_ref[...] = jnp.zeros_like(dgt_ref)
            dg_ref[...] = jnp.zeros_like(dg_ref)

        @pl.when(j == 0)
        def _():
            dn_ref[...] = part

        @pl.when(j > 0)
        def _():
            dn_ref[...] += part

        @pl.when(j == 1)
        def _():
            dn, gv, gate = dn_ref[...], g_ref[...], gt_ref[...]
            r = _rsqrt_mean(dn)
            normed = dn * r * gv
            err = x1_ref[...] + gate * normed - t_ref[...]
            dx = err * (1.0 / D_MODEL)
            dx_ref[...] = dx
            tot = jnp.sum(jnp.sum(err * err, axis=1, keepdims=True), axis=0, keepdims=True) * (0.5 / D_MODEL)
            loss_ref[...] = jnp.broadcast_to(tot, (8, 128))
            dgt_ref[...] += _colsum(dx * normed)
            dnn = dx * gate
            dg_ref[...] += _colsum(dnn * dn * r)
            ddn_ref[...] = _norm_bwd(dnn, dn, r, gv).astype(BF16)

    row = pl.BlockSpec((tm, D_MODEL), lambda i, j: (i, 0))
    vec = _const((1, D_MODEL))
    return _call(body, name='ffn_down', grid=(nb, 2),
                 in_specs=[pl.BlockSpec((2, 2, tm, FF_SHARD), lambda i, j: (0, j, i, 0)),
                           pl.BlockSpec((2, FF_SHARD, D_MODEL), lambda i, j: (j, 0, 0)), row, row, vec, vec],
                 out_specs=[row, row, pl.BlockSpec((None, 8, 128), lambda i, j: (i, 0, 0)), vec, vec],
                 out_shape=[_sds((T, D_MODEL), BF16), _sds((T, D_MODEL)), _sds((nb, 8, 128)), _sds((1, D_MODEL)),
                            _sds((1, D_MODEL))],
                 scratch=[pltpu.VMEM((tm, D_MODEL), F32)], sem=('arbitrary', 'arbitrary'),
                 vmem=VMEM_BIG)(hid4, wd4, x1, tgt, gt, g_post)


def _ssm_prep(lre, lim, lst, b_re, b_im):
    def body(lre_ref, lim_ref, lst_ref, br_ref, bi_ref, ar_ref, ai_ref, bbr_ref, bbi_ref):
        ar, ai, qr, qi = _zoh(lre_ref[...], lim_ref[...], lst_ref[...])[:4]
        ar_ref[...] = ar
        ai_ref[...] = ai
        bbr_ref[...] = qr * br_ref[...] - qi * bi_ref[...]
        bbi_ref[...] = qr * bi_ref[...] + qi * br_ref[...]

    shp = lre.shape
    return _call(body, name='ssm_prep', grid=(1,), in_specs=[_const(shp)] * 5, out_specs=[_const(shp)] * 4,
                 out_shape=[_sds(shp)] * 4)(lre, lim, lst, b_re, b_im)


def _zoh(lre, lim, lst):
    lr = jnp.minimum(lre, LAMBDA_RE_MAX)
    st = jnp.exp(lst)
    mag = jnp.exp(lr * st)
    ar = mag * jnp.cos(lim * st)
    ai = mag * jnp.sin(lim * st)
    den = lr * lr + lim * lim
    qr = ((ar - 1.0) * lr + ai * lim) / den
    qi = (ai * lr - (ar - 1.0) * lim) / den
    return ar, ai, qr, qi, lr, st, den


def _ssm_prep_bwd(lre, lim, lst, b_re, b_im, dbbr, dbbi, dar, dai, seg):
    def body(lre_ref, lim_ref, lst_ref, br_ref, bi_ref, dbbr_ref, dbbi_ref, dar_ref, dai_ref, seg_ref,
             dbr_ref, dbi_ref, dlre_ref, dlim_ref, dlst_ref):
        lre_v = lre_ref[...]
        li = lim_ref[...]
        ar, ai, qr, qi, lr, st, den = _zoh(lre_v, li, lst_ref[...])
        br, bi, gbr, gbi = br_ref[...], bi_ref[...], dbbr_ref[...], dbbi_ref[...]
        dbr_ref[...] = qr * gbr + qi * gbi
        dbi_ref[...] = qr * gbi - qi * gbr
        gqr = _dot_split(br * gbr + bi * gbi, seg_ref[...], 3)
        gqi = _dot_split(br * gbi - bi * gbr, seg_ref[...], 3)
        ir, ii = lr / den, -li / den
        gar = dar_ref[...] + ir * gqr + ii * gqi
        gai = dai_ref[...] + ir * gqi - ii * gqr
        tr, ti = qr * ir - qi * ii, qr * ii + qi * ir
        glr = -(tr * gqr + ti * gqi)
        gli = -(tr * gqi - ti * gqr)
        gzr = ar * gar + ai * gai
        gzi = ar * gai - ai * gar
        glr = glr + st * gzr
        gli = gli + st * gzi
        gst = (lr * gzr + li * gzi) * st
        dlre_ref[...] = jnp.where(lre_v < LAMBDA_RE_MAX, glr, 0.0)
        dlim_ref[...] = gli
        dlst_ref[...] = jnp.sum(gst, axis=1, keepdims=True) * (1.0 / SSM_GROUP)

    shp = lre.shape
    return _call(body, name='ssm_prep_bwd', grid=(1,), in_specs=[_const(shp)] * 9 + [_const(seg.shape)],
                 out_specs=[_const(shp)] * 4 + [_const((N_GROUPS, 1))],
                 out_shape=[_sds(shp)] * 4 + [_sds((N_GROUPS, 1))], vmem=VMEM_BIG)(
                     lre, lim, lst, b_re, b_im, dbbr, dbbi, dar, dai, seg)


def _scan_specs(T):
    return dict(
        chan=pl.BlockSpec((T, CHAN_BLOCK), lambda cb: (0, cb)),
        state=pl.BlockSpec((T, STATE_BLOCK), lambda cb: (0, cb)),
        b=pl.BlockSpec((CHAN_BLOCK, STATE_BLOCK), lambda cb: (cb, cb)),
        c=pl.BlockSpec((STATE_BLOCK, CHAN_BLOCK), lambda cb: (cb, cb)),
        lam=pl.BlockSpec((1, STATE_BLOCK), lambda cb: (0, cb)),
    )


def _complex_power(re, im, n):
    out = None
    while True:
        if n & 1:
            out = (re, im) if out is None else (out[0] * re - out[1] * im, out[0] * im + out[1] * re)
        n >>= 1
        if n == 0:
            return out
        re, im = re * re - im * im, 2.0 * re * im


def _rows8(i):
    if isinstance(i, int):
        return pl.ds(i * SUBLANES, SUBLANES)
    return pl.ds(pl.multiple_of(i * SUBLANES, SUBLANES), SUBLANES)


def _scan_loop(n_steps, body, init):
    trips = n_steps // SCAN_UNROLL

    def trip(t, carry):
        for u in range(SCAN_UNROLL):
            carry = body(t * SCAN_UNROLL + u, carry)
        return carry

    carry = lax.fori_loop(0, trips, trip, init)
    for step in range(trips * SCAN_UNROLL, n_steps):
        carry = body(step, carry)
    return carry


def _ssm_fwd(u_perm, b_re, b_im, c_re, c_im, lam_r, lam_i, ride):
    T = u_perm.shape[0]
    ls = T // SUBLANES
    rc = min(1024, T)
    sp = _scan_specs(T)

    def body(u_ref, bre_ref, bim_ref, cre_ref, cim_ref, lr_ref, li_ref, so_re_ref, so_im_ref, y_ref, sre_ref, sim_ref):
        for c in range(T // rc):
            rows = pl.ds(c * rc, rc)
            ub = u_ref[rows, :].astype(BF16)
            sre_ref[rows, :] = _dot(ub, bre_ref[...])
            sim_ref[rows, :] = _dot(ub, bim_ref[...])
        shp = (SUBLANES, STATE_BLOCK)
        lr = jnp.broadcast_to(lr_ref[...], shp)
        li = jnp.broadcast_to(li_ref[...], shp)
        zero = jnp.zeros(shp, F32)

        def step(i, carry):
            sr, si = carry
            rows = _rows8(i)
            nr = lr * sr - li * si + sre_ref[rows, :]
            ni = lr * si + li * sr + sim_ref[rows, :]
            sre_ref[rows, :] = nr
            sim_ref[rows, :] = ni
            return nr, ni

        fr, fi = _scan_loop(ls, step, (zero, zero))
        pr, pi_ = _complex_power(lr, li, ls)
        row = lax.broadcasted_iota(jnp.int32, shp, 0)
        ir, ii = zero, zero
        for _ in range(SUBLANES - 1):
            er = fr + pr * ir - pi_ * ii
            ei = fi + pr * ii + pi_ * ir
            ir = jnp.where(row == 0, 0.0, pltpu.roll(er, 1, 0))
            ii = jnp.where(row == 0, 0.0, pltpu.roll(ei, 1, 0))

        def fix(i, carry):
            cr, ci = carry
            rows = _rows8(i)
            nr = lr * cr - li * ci
            ni = lr * ci + li * cr
            sre_ref[rows, :] += nr
            sim_ref[rows, :] += ni
            return nr, ni

        _scan_loop(ls, fix, (ir, ii))
        for c in range(T // rc):
            rows = pl.ds(c * rc, rc)
            s_r, s_i = sre_ref[rows, :].astype(BF16), sim_ref[rows, :].astype(BF16)
            so_re_ref[rows, :] = s_r
            so_im_ref[rows, :] = s_i
            y_ref[rows, :] = _dot(s_r, cre_ref[...]) - _dot(s_i, cim_ref[...])

    return _call(body, name='ssm_fwd', grid=(N_STATE // STATE_BLOCK,),
                 in_specs=[sp['chan'], sp['b'], sp['b'], sp['c'], sp['c'], sp['lam'], sp['lam']],
                 out_specs=[sp['state'], sp['state'], sp['chan']],
                 out_shape=[_sds((T, N_STATE), BF16), _sds((T, N_STATE), BF16), _sds((T, D_SSM))],
                 scratch=[pltpu.VMEM((T, STATE_BLOCK), F32), pltpu.VMEM((T, STATE_BLOCK), F32)],
                 sem=('arbitrary',), vmem=VMEM_MOST, ride=ride)(u_perm, b_re, b_im, c_re, c_im, lam_r, lam_i)


def _ssm_bwd(dy_perm, u_perm, s_re, s_im, b_re, b_im, c_re, c_im, lam_r, lam_i, ride):
    T = u_perm.shape[0]
    ls = T // SUBLANES
    rc = min(1024, T)
    sp = _scan_specs(T)
    ncb = N_STATE // STATE_BLOCK

    def body(dy_ref, u_ref, sre_ref, sim_ref, bre_ref, bim_ref, cre_ref, cim_ref, lr_ref, li_ref,
             du_ref, dbr_ref, dbi_ref, dcr_ref, dci_ref, dar_ref, dai_ref, gre_ref, gim_ref):
        shp = (SUBLANES, STATE_BLOCK)
        zero = jnp.zeros(shp, F32)
        tail = pl.ds(T, SUBLANES)
        gre_ref[tail, :] = zero
        gim_ref[tail, :] = zero
        for c in range(T // rc):
            rows = pl.ds(c * rc, rc)
            dyb = dy_ref[rows, :].astype(BF16)
            gre_ref[rows, :] = _dot_nt(dyb, cre_ref[...])
            gim_ref[rows, :] = -_dot_nt(dyb, cim_ref[...])
        lr = jnp.broadcast_to(lr_ref[...], shp)
        li = jnp.broadcast_to(li_ref[...], shp)

        def step(k, carry):
            gr, gi = carry
            rows = _rows8(ls - 1 - k)
            nr = lr * gr + li * gi + gre_ref[rows, :]
            ni = lr * gi - li * gr + gim_ref[rows, :]
            gre_ref[rows, :] = nr
            gim_ref[rows, :] = ni
            return nr, ni

        fr, fi = _scan_loop(ls, step, (zero, zero))
        pr, pi_ = _complex_power(lr, -li, ls)
        row = lax.broadcasted_iota(jnp.int32, shp, 0)
        cr, ci = zero, zero
        for _ in range(SUBLANES - 1):
            er = fr + pr * cr - pi_ * ci
            ei = fi + pr * ci + pi_ * cr
            cr = jnp.where(row == SUBLANES - 1, 0.0, pltpu.roll(er, SUBLANES - 1, 0))
            ci = jnp.where(row == SUBLANES - 1, 0.0, pltpu.roll(ei, SUBLANES - 1, 0))

        def fix(k, carry):
            dr, di = carry
            rows = _rows8(ls - 1 - k)
            dr, di = lr * dr + li * di, lr * di - li * dr
            gre_ref[rows, :] += dr
            gim_ref[rows, :] += di
            return dr, di

        _scan_loop(ls, fix, (cr, ci))

        acc_r = jnp.zeros((1, STATE_BLOCK), F32)
        acc_i = jnp.zeros((1, STATE_BLOCK), F32)
        for c in range(T // rc):
            rows, nxt = pl.ds(c * rc, rc), pl.ds(c * rc + SUBLANES, rc)
            s_r, s_i = sre_ref[rows, :].astype(F32), sim_ref[rows, :].astype(F32)
            g_r, g_i = gre_ref[nxt, :], gim_ref[nxt, :]
            acc_r = acc_r + _colsum(g_r * s_r + g_i * s_i)
            acc_i = acc_i + _colsum(g_i * s_r - g_r * s_i)
        last = pl.ds(T - 2 * SUBLANES, 2 * SUBLANES)
        first = pl.ds(0, SUBLANES)
        spr = jnp.where(row == 0, 0.0, pltpu.roll(sre_ref[last, :].astype(F32)[SUBLANES:], 1, 0))
        spi = jnp.where(row == 0, 0.0, pltpu.roll(sim_ref[last, :].astype(F32)[SUBLANES:], 1, 0))
        gr, gi = gre_ref[first, :], gim_ref[first, :]
        dar_ref[...] = acc_r + _colsum(gr * spr + gi * spi)
        dai_ref[...] = acc_i + _colsum(gi * spr - gr * spi)

        for c in range(T // rc):
            rows = pl.ds(c * rc, rc)
            g_r, g_i = gre_ref[rows, :].astype(BF16), gim_ref[rows, :].astype(BF16)
            s_r, s_i = sre_ref[rows, :], sim_ref[rows, :]
            ub, dyb = u_ref[rows, :].astype(BF16), dy_ref[rows, :].astype(BF16)
            du_ref[rows, :] = _dot_nt(g_r, bre_ref[...]) + _dot_nt(g_i, bim_ref[...])
            parts = (_dot_tn(ub, g_r), _dot_tn(ub, g_i), _dot_tn(s_r, dyb), -_dot_tn(s_i, dyb))
            outs = (dbr_ref, dbi_ref, dcr_ref, dci_ref)
            for o_ref, part in zip(outs, parts):
                if c == 0:
                    o_ref[...] = part
                else:
                    o_ref[...] += part

    blk = lambda r, c: pl.BlockSpec((None, r, c), lambda cb: (cb, 0, 0))
    return _call(body, name='ssm_bwd', grid=(ncb,),
                 in_specs=[sp['chan'], sp['chan'], sp['state'], sp['state'], sp['b'], sp['b'], sp['c'], sp['c'],
                           sp['lam'], sp['lam']],
                 out_specs=[sp['chan'], blk(CHAN_BLOCK, STATE_BLOCK), blk(CHAN_BLOCK, STATE_BLOCK),
                            blk(STATE_BLOCK, CHAN_BLOCK), blk(STATE_BLOCK, CHAN_BLOCK), blk(1, STATE_BLOCK),
                            blk(1, STATE_BLOCK)],
                 out_shape=[_sds((T, D_SSM)), _sds((ncb, CHAN_BLOCK, STATE_BLOCK)), _sds((ncb, CHAN_BLOCK, STATE_BLOCK)),
                            _sds((ncb, STATE_BLOCK, CHAN_BLOCK)), _sds((ncb, STATE_BLOCK, CHAN_BLOCK)),
                            _sds((ncb, 1, STATE_BLOCK)), _sds((ncb, 1, STATE_BLOCK))],
                 scratch=[pltpu.VMEM((T + SUBLANES, STATE_BLOCK), F32), pltpu.VMEM((T + SUBLANES, STATE_BLOCK), F32)],
                 sem=('arbitrary',), vmem=VMEM_MOST, ride=ride)(dy_perm, u_perm, s_re, s_im, b_re, b_im, c_re, c_im,
                                                                lam_r, lam_i)


def _ffn_dact(ddn, wd4, hid4, tm):
    T = ddn.shape[0]
    nb = T // tm

    def body(d_ref, w_ref, hid_ref, o_ref, gw_ref, acc_ref):
        i = pl.program_id(1)
        d = d_ref[...]
        dact = _dot_nt(d, w_ref[...])
        silu, dsilu = _silu_parts(hid_ref[0].astype(F32))
        hid_v = hid_ref[1].astype(F32)
        o_ref[0] = (dact * hid_v * dsilu).astype(BF16)
        o_ref[1] = (dact * silu).astype(BF16)
        part = _dot_tn((silu * hid_v).astype(BF16), d)

        @pl.when(i == 0)
        def _():
            acc_ref[...] = part

        @pl.when(i > 0)
        def _():
            acc_ref[...] += part

        @pl.when(i == nb - 1)
        def _():
            gw_ref[...] = acc_ref[...].astype(BF16)

    blk = pl.BlockSpec((2, None, tm, FF_SHARD), lambda j, i: (0, j, i, 0))
    w_blk = pl.BlockSpec((None, FF_SHARD, D_MODEL), lambda j, i: (j, 0, 0))
    return _call(body, name='ffn_dact', grid=(4, nb),
                 in_specs=[pl.BlockSpec((tm, D_MODEL), lambda j, i: (i, 0)), w_blk, blk],
                 out_specs=[blk, w_blk],
                 out_shape=[_sds((2, 4, T, FF_SHARD), BF16), _sds((4, FF_SHARD, D_MODEL), BF16)],
                 scratch=[pltpu.VMEM((FF_SHARD, D_MODEL), F32)], sem=('parallel', 'arbitrary'),
                 vmem=VMEM_BIG)(ddn, wd4, hid4)


def _ffn_dup(dhid8, up8, cw8, tm, ride):
    T = up8.shape[1]
    nb = T // tm
    ha = _halo_after(tm, T, HALO16)

    def body(dh_ref, dha_ref, up_ref, cw_ref, dup_ref, dcw_ref):
        i = pl.program_id(1)

        @pl.when(i == 0)
        def _():
            dcw_ref[...] = jnp.zeros_like(dcw_ref)

        dh = dh_ref[...].astype(F32)
        dup, dh1, dh2 = _conv3_t(dh, jnp.where(i < nb - 1, dha_ref[...].astype(F32), 0.0), cw_ref)
        dup_ref[...] = dup.astype(BF16)
        up = up_ref[...].astype(F32)
        dcw_ref[0:1, :] += _colsum(dh2 * up)
        dcw_ref[1:2, :] += _colsum(dh1 * up)
        dcw_ref[2:3, :] += _colsum(dh * up)

    main = pl.BlockSpec((None, tm, FF_SHARD), lambda j, i: (j, i, 0))
    return _call(body, name='ffn_dup', grid=(N_DEV, nb),
                 in_specs=[main, pl.BlockSpec((None, HALO16, FF_SHARD), lambda j, i: (j, ha(i), 0)), main,
                           pl.BlockSpec((None, 3, FF_SHARD), lambda j, i: (j, 0, 0))],
                 out_specs=[main, pl.BlockSpec((None, 8, FF_SHARD), lambda j, i: (j, 0, 0))],
                 out_shape=[_sds((N_DEV, T, FF_SHARD), BF16), _sds((N_DEV, 8, FF_SHARD))],
                 sem=('parallel', 'arbitrary'), vmem=VMEM_BIG, ride=ride)(dhid8, dhid8, up8, cw8)


def _grad_tn(a, b, a_spec, b_spec, groups, m, n, tk, name, ride=None, parts=1):
    T = a.shape[-2]
    nk = T // tk
    mp = m // parts

    def body(a_ref, b_ref, *refs):
        o_refs, acc_ref = refs[:parts], refs[parts]
        k = pl.program_id(1)
        part = _dot_tn(a_ref[...], b_ref[...])

        @pl.when(k == 0)
        def _():
            acc_ref[...] = part

        @pl.when(k > 0)
        def _():
            acc_ref[...] += part

        @pl.when(k == nk - 1)
        def _():
            for p, o_ref in enumerate(o_refs):
                o_ref[...] = acc_ref[p * mp:(p + 1) * mp, :].astype(BF16)

    out_spec = pl.BlockSpec((None, mp, n), lambda g, k: (g, 0, 0))
    res = _call(body, name=name, grid=(groups, nk), in_specs=[a_spec, b_spec], out_specs=[out_spec] * parts,
                out_shape=[_sds((groups, mp, n), BF16)] * parts, scratch=[pltpu.VMEM((m, n), F32)],
                sem=('parallel', 'arbitrary'), vmem=VMEM_BIG, ride=ride)(a, b)
    if parts > 1:
        return res
    return res[0] if ride is None else (res[0][0], res[1])


def _grad_w_in(h1, dproj, tk, ride):
    T = h1.shape[0]
    nk = T // tk
    half = D_IN_PROJ // 2

    def body(a_ref, b_ref, o_ref, acc_ref):
        k = pl.program_id(0)
        for h in range(2):
            cols = slice(h * half, (h + 1) * half)
            part = _dot_tn(a_ref[...], b_ref[:, cols])

            @pl.when(k == 0)
            def _():
                acc_ref[:, cols] = part

            @pl.when(k > 0)
            def _():
                acc_ref[:, cols] += part

        @pl.when(k == nk - 1)
        def _():
            for g in range(N_DEV):
                o_ref[g] = acc_ref[:, g * IN_SHARD:(g + 1) * IN_SHARD].astype(BF16)

    return _call(body, name='grad_w_in', grid=(nk,),
                 in_specs=[pl.BlockSpec((tk, D_MODEL), lambda k: (k, 0)), pl.BlockSpec((tk, D_IN_PROJ), lambda k: (k, 0))],
                 out_specs=_const((N_DEV, D_MODEL, IN_SHARD)), out_shape=_sds((N_DEV, D_MODEL, IN_SHARD), BF16),
                 scratch=[pltpu.VMEM((D_MODEL, D_IN_PROJ), F32)], sem=('arbitrary',), vmem=VMEM_BIG, ride=ride)(h1, dproj)


def _pre_norm_bwd(dz, dz_spec, w_parts, xin, dres, sc, g, tm, name, ride, below=None, group=1, w_t=False):
    T = xin.shape[0]
    n = w_parts[0].shape[1] if w_t else w_parts[0].shape[2]
    mul = _dot if w_t else _dot_nt
    steps = N_DEV // group
    width = D_MODEL // len(w_parts)

    def body(dz_ref, *refs):
        w_refs, (x_ref, dr_ref, sc_ref, g_ref), refs = refs[:len(w_parts)], refs[len(w_parts):len(w_parts) + 4], \
            refs[len(w_parts) + 4:]
        if below is None:
            dx_ref, dsh_ref, dsc_ref, dg_ref = refs
            sums = (dsh_ref, dsc_ref, dg_ref)
        else:
            v_ref, gate_ref, g2_ref, dx_ref, dsh_ref, dsc_ref, dg_ref, dv_ref, dgate_ref, dg2_ref = refs
            sums = (dsh_ref, dsc_ref, dg_ref, dgate_ref, dg2_ref)
        i, j = pl.program_id(0), pl.program_id(1)
        piece = (lambda s: dz_ref[s]) if dz.ndim == 3 else (lambda s: dz_ref[:, s * n:(s + 1) * n])
        parts = []
        for w_ref in w_refs:
            part = mul(piece(0), w_ref[0])
            for s in range(1, group):
                part = part + mul(piece(s), w_ref[s])
            parts.append(part)

        @pl.when(jnp.logical_and(i == 0, j == 0))
        def _():
            for s_ref in sums:
                s_ref[...] = jnp.zeros_like(s_ref)

        @pl.when(j == 0)
        def _():
            for k, part in enumerate(parts):
                dx_ref[:, k * width:(k + 1) * width] = part

        @pl.when(j > 0)
        def _():
            for k, part in enumerate(parts):
                dx_ref[:, k * width:(k + 1) * width] += part

        @pl.when(j == steps - 1)
        def _():
            dh, xv, gv = dx_ref[...], x_ref[...], g_ref[...]
            r = _rsqrt_mean(xv)
            dsh_ref[...] += _colsum(dh)
            dsc_ref[...] += _colsum(dh * (xv * r * gv))
            dxn = dh * (1.0 + sc_ref[...])
            dg_ref[...] += _colsum(dxn * xv * r)
            dx = dr_ref[...] + _norm_bwd(dxn, xv, r, gv)
            dx_ref[...] = dx
            if below is not None:
                v, g2 = v_ref[...], g2_ref[...]
                rv = _rsqrt_mean(v)
                dgate_ref[...] += _colsum(dx * (v * rv * g2))
                dn = dx * gate_ref[...]
                dg2_ref[...] += _colsum(dn * v * rv)
                dv_ref[...] = _norm_bwd(dn, v, rv, g2).astype(BF16)

    row = pl.BlockSpec((tm, D_MODEL), lambda i, j: (i, 0))
    vec = _const((1, D_MODEL))
    in_specs = [dz_spec] + [pl.BlockSpec((group,) + w.shape[1:], lambda i, j: (j, 0, 0)) for w in w_parts]
    in_specs += [row, row, vec, vec]
    out_specs = [row, vec, vec, vec]
    out_shape = [_sds((T, D_MODEL)), _sds((1, D_MODEL)), _sds((1, D_MODEL)), _sds((1, D_MODEL))]
    args = [dz, *w_parts, xin, dres, sc, g]
    if below is not None:
        in_specs += [row, vec, vec]
        out_specs += [row, vec, vec]
        out_shape += [_sds((T, D_MODEL), BF16), _sds((1, D_MODEL)), _sds((1, D_MODEL))]
        args += list(below)
    return _call(body, name=name, grid=(T // tm, steps), in_specs=in_specs, out_specs=out_specs,
                 out_shape=out_shape, sem=('arbitrary', 'arbitrary'), vmem=VMEM_MOST, ride=ride)(*args)


def _mix_bwd(d_o, w_out, yssm, proj, d, glu_w, glu_b, g_ssm, cw, g_conv, avg16, avg64, tm, ride):
    T = yssm.shape[0]
    hb = _halo_before(tm)

    def body(do_ref, wo_ref, y_ref, p_ref, ph_ref, d_ref, gw_ref, gb_ref, gs_ref, cw_ref, gc_ref, a16_ref, a64_ref,
             dy_ref, dconv_ref, dbg_ref, z_ref, dlin_ref, acc_ref):
        i = pl.program_id(0)
        dyc = _dot_nt(do_ref[...], wo_ref[...])

        @pl.when(i == 0)
        def _():
            acc_ref[...] = jnp.zeros_like(acc_ref)

        u = p_ref[:, 0:D_SSM]
        y = y_ref[...] + d_ref[...] * u
        z, t = _gelu(y)
        gate = _sigmoid(_dot(z.astype(BF16), gw_ref[...]) + gb_ref[...])
        ya = z * gate
        rs = lax.rsqrt(_dot_split(ya * ya, a16_ref[...], 2) + EPS)
        dna = dyc[:, 0:D_SSM]
        acc_ref[1:2, :] += _colsum(dna * ya * rs)
        dya = _head_norm_bwd(dna, ya, rs, gs_ref[...], a16_ref[...])
        dlin = dya * z * gate * (1.0 - gate)
        acc_ref[0:1, :] += _colsum(dlin)
        dlin_b = dlin.astype(BF16)
        dz = dya * gate + _dot_nt(dlin_b, gw_ref[...])
        dy = dz * _gelu_grad(y, t)
        acc_ref[3:4, :] += _colsum(dy * u)
        dy_ref[...] = dy
        z_ref[...] = z.astype(BF16)
        dlin_ref[...] = dlin_b

        bg = p_ref[:, D_SSM:D_SSM + D_CONV]
        cv = p_ref[:, D_SSM + D_CONV:D_SSM + 2 * D_CONV] * p_ref[:, D_SSM + 2 * D_CONV:D_IN_PROJ]
        hv = ph_ref[:, D_SSM + D_CONV:D_SSM + 2 * D_CONV] * ph_ref[:, D_SSM + 2 * D_CONV:D_IN_PROJ]
        hv = jnp.where(i > 0, hv, 0.0)
        conv, cv1, cv2 = _conv3(cv, hv, cw_ref)
        yb = bg * conv
        rsb = lax.rsqrt(_dot_split(yb * yb, a64_ref[...], 2) + EPS)
        dnb = dyc[:, D_SSM:D_MODEL]
        acc_ref[2:3, :] += _colsum(dnb * yb * rsb)
        dyb = _head_norm_bwd(dnb, yb, rsb, gc_ref[...], a64_ref[...])
        dbg_ref[...] = dyb * conv
        dconv = dyb * bg
        dconv_ref[...] = dconv
        acc_ref[4:5, :] += _colsum(dconv * cv2)
        acc_ref[5:6, :] += _colsum(dconv * cv1)
        acc_ref[6:7, :] += _colsum(dconv * cv)

    vec = _const((1, D_SSM))
    sq = _const((D_SSM, D_SSM))
    half = pl.BlockSpec((tm, D_SSM), lambda i: (i, 0))
    return _call(body, name='mix_bwd', grid=(T // tm,),
                 in_specs=[pl.BlockSpec((tm, D_MODEL), lambda i: (i, 0)), _const((D_MODEL, D_MODEL)), half,
                           pl.BlockSpec((tm, D_IN_PROJ), lambda i: (i, 0)),
                           pl.BlockSpec((HALO, D_IN_PROJ), lambda i: (hb(i), 0)), vec, sq, vec, vec,
                           _const((3, D_CONV)), vec, sq, sq],
                 out_specs=[half, half, half, half, half, _const((8, D_SSM))],
                 out_shape=[_sds((T, D_SSM)), _sds((T, D_SSM)), _sds((T, D_SSM)), _sds((T, D_SSM), BF16),
                            _sds((T, D_SSM), BF16), _sds((8, D_SSM))],
                 sem=('arbitrary',), vmem=VMEM_BIG, ride=ride)(d_o, w_out, yssm, proj, proj, d, glu_w, glu_b, g_ssm, cw,
                                                              g_conv, avg16, avg64)


def _mix_bwd_proj(dconv, proj, du_ssm, dy, d, dbg, cw, tm):
    T = dy.shape[0]
    nb = T // tm
    ha = _halo_after(tm, T)

    def body(dc_ref, dch_ref, cg_ref, v_ref, du_ref, dy_ref, d_ref, dbg_ref, cw_ref, o_ref):
        i = pl.program_id(0)
        dcv = _conv3_t(dc_ref[...], jnp.where(i < nb - 1, dch_ref[...], 0.0), cw_ref)[0]
        o_ref[:, 0:D_SSM] = (du_ref[...] + dy_ref[...] * d_ref[...]).astype(BF16)
        o_ref[:, D_SSM:D_SSM + D_CONV] = dbg_ref[...].astype(BF16)
        o_ref[:, D_SSM + D_CONV:D_SSM + 2 * D_CONV] = (dcv * v_ref[...]).astype(BF16)
        o_ref[:, D_SSM + 2 * D_CONV:D_IN_PROJ] = (dcv * cg_ref[...]).astype(BF16)

    half = pl.BlockSpec((tm, D_SSM), lambda i: (i, 0))
    return _call(body, name='mix_bwd_proj', grid=(nb,),
                 in_specs=[half, pl.BlockSpec((HALO, D_CONV), lambda i: (ha(i), 0)),
                           pl.BlockSpec((tm, D_CONV), lambda i: (i, 2)), pl.BlockSpec((tm, D_CONV), lambda i: (i, 3)),
                           half, half, _const((1, D_SSM)), half, _const((3, D_CONV))],
                 out_specs=pl.BlockSpec((tm, D_IN_PROJ), lambda i: (i, 0)), out_shape=_sds((T, D_IN_PROJ), BF16),
                 sem=('parallel',), vmem=VMEM_BIG)(dconv, dconv, proj, proj, du_ssm, dy, d, dbg, cw)


ADAMW_SLOT_BYTES = 8 << 20
ADAMW_ROW_BYTES = 3 << 19


def _row_tile(rows, cols, slots):
    for cand in range(rows, 15, -1):
        if (rows % cand == 0 and cand % 16 == 0 and slots * cand * cols * 4 <= ADAMW_SLOT_BYTES
                and cand * cols * 4 <= ADAMW_ROW_BYTES):
            return cand
    return rows


def _adamw_math(g, w, m, v):
    m2 = ADAM_B1 * m + (1.0 - ADAM_B1) * g
    v2 = ADAM_B2 * v + (1.0 - ADAM_B2) * (g * g)
    m_hat = m2 / (1.0 - ADAM_B1 ** ADAM_STEP)
    v_hat = v2 / (1.0 - ADAM_B2 ** ADAM_STEP)
    return -ADAM_LR * (m_hat / (jnp.sqrt(v_hat) + ADAM_EPS) + ADAM_WD * w), m2, v2


def _adamw(pieces, w, m, v, name):
    slots, _, cols = pieces[0].shape
    rows = sum(p.shape[1] for p in pieces)
    tr = _row_tile(pieces[0].shape[1], cols, slots)
    starts, pos = [], 0
    for p in pieces:
        assert p.shape[1] % tr == 0
        starts.append(pos)
        pos += p.shape[1] // tr

    def body(*refs):
        g_refs = refs[:len(pieces)]
        w_ref, m_ref, v_ref, go_ref, d_ref, mo_ref, vo_ref = refs[len(pieces):]
        i = pl.program_id(0)
        g = None
        for g_ref, start in zip(g_refs, starts):
            part = g_ref[0].astype(F32)
            for s in range(1, slots):
                part = part + g_ref[s].astype(F32)
            g = part if g is None else jnp.where(i >= start, part, g)
        go_ref[...] = g
        d_ref[...], mo_ref[...], vo_ref[...] = _adamw_math(g, w_ref[...], m_ref[...], v_ref[...])

    def piece_spec(start, count):
        return pl.BlockSpec((slots, tr, cols), lambda i: (0, jnp.clip(i - start, 0, count - 1), 0))

    blk = pl.BlockSpec((tr, cols), lambda i: (i, 0))
    return _call(body, name=name, grid=(rows // tr,),
                 in_specs=[piece_spec(s, p.shape[1] // tr) for s, p in zip(starts, pieces)] + [blk, blk, blk],
                 out_specs=[blk] * 4, out_shape=[_sds((rows, cols))] * 4, sem=('parallel',),
                 vmem=VMEM_BIG)(*pieces, w, m, v)


def _to_scan_rows(a):
    T, n = a.shape
    return a.reshape(SUBLANES, T // SUBLANES, n).transpose(1, 0, 2).reshape(T, n)


def _from_scan_rows(a):
    T, n = a.shape
    return a.reshape(T // SUBLANES, SUBLANES, n).transpose(1, 0, 2).reshape(T, n)


def _expand(a):
    return jnp.repeat(a, SSM_GROUP, axis=1)


def _block_diag(rows, row_group, col_group):
    r, n = rows.shape
    tiled = jnp.tile(rows, (1, N_GROUPS))
    keep = (jnp.arange(r)[:, None] // row_group) == (jnp.arange(n * N_GROUPS)[None, :] // col_group)
    return jnp.where(keep, tiled, 0.0)


def _block_diag_b(bb):
    return _block_diag(bb.transpose(0, 2, 1).reshape(D_SSM, SSM_STATE), SSM_GROUP, SSM_STATE)


def _block_diag_c(cc):
    return _block_diag(cc.transpose(0, 2, 1).reshape(N_STATE, SSM_GROUP), SSM_STATE, SSM_GROUP)


def _diag_blocks(x, chan_major):
    per = CHAN_BLOCK // SSM_GROUP
    eye = jnp.eye(per, dtype=x.dtype)
    if chan_major:
        x = x.reshape(-1, per, SSM_GROUP, per, SSM_STATE) * eye[None, :, None, :, None]
        return x.sum(axis=1).transpose(0, 2, 3, 1).reshape(N_GROUPS, SSM_STATE, SSM_GROUP)
    x = x.reshape(-1, per, SSM_STATE, per, SSM_GROUP) * eye[None, :, None, :, None]
    return x.sum(axis=3).reshape(N_GROUPS, SSM_STATE, SSM_GROUP)


SMALL_LAYOUT = {
    'ssm_b_re': (0, 0, 32, 1024), 'ssm_b_im': (32, 0, 32, 1024), 'ssm_c_re': (64, 0, 32, 1024),
    'ssm_c_im': (96, 0, 32, 1024), 'b_ada': (128, 0, 6, 1024), 'g_pre_mix': (134, 0, 1, 1024),
    'g_post_mix': (135, 0, 1, 1024), 'ssm_lam_re': (136, 0, 2, 1024), 'ssm_lam_im': (138, 0, 2, 1024),
    'ssm_log_step': (140, 0, 1, 32), 'glu_b': (141, 0, 1, 512), 'g_out_ssm': (141, 512, 1, 512),
    'g_out_conv': (142, 0, 1, 512), 'ssm_d': (142, 512, 1, 512), 'g_pre_ffn': (143, 0, 1, 1024),
    'g_post_ffn': (144, 0, 1, 1024)}
SMALL_ROWS = 152
B_ADA_ROW = SMALL_LAYOUT['b_ada'][0]
LATE_ROWS = {('b_ada', 0): 0, ('b_ada', 1): 1, ('g_pre_mix', 0): 2}


def _adamw_small(gathered, late, wts, mom_m, mom_v):
    names = list(SMALL_LAYOUT)
    n = len(names)

    def body(*refs):
        g_ref, late_ref, ins, outs = refs[0], refs[1], refs[2:2 + 3 * n], refs[2 + 3 * n:]
        for p, name in enumerate(names):
            r0, c0, rows, cols = SMALL_LAYOUT[name]
            pieces = [(0, rows)] if rows % 8 == 0 else [(r, 1) for r in range(rows)]
            for r, cnt in pieces:
                src_ref, first = (late_ref, LATE_ROWS[name, r]) if (name, r) in LATE_ROWS else (g_ref, r0 + r)
                g = src_ref[0, first:first + cnt, c0:c0 + cols]
                for s in range(1, N_DEV):
                    g = g + src_ref[s, first:first + cnt, c0:c0 + cols]
                w, m, v = (ins[3 * p + q][r:r + cnt, :] for q in range(3))
                res = (g,) + _adamw_math(g, w, m, v)
                for q in range(4):
                    outs[4 * p + q][r:r + cnt, :] = res[q]

    shapes = [SMALL_LAYOUT[name][2:] for name in names]
    args = [gathered, late]
    for name, shp in zip(names, shapes):
        args += [wts[name].reshape(shp), mom_m[name].reshape(shp), mom_v[name].reshape(shp)]
    outs = _call(body, name='adamw_small', grid=(1,),
                 in_specs=[_const(gathered.shape), _const(late.shape)]
                 + [_const(shp) for shp in shapes for _ in range(3)],
                 out_specs=[_const(shp) for shp in shapes for _ in range(4)],
                 out_shape=[_sds(shp) for shp in shapes for _ in range(4)], vmem=VMEM_BIG)(*args)
    res = {}
    for p, name in enumerate(names):
        for q, kind in enumerate(('g', 'd', 'm', 'v')):
            res[kind, name] = outs[4 * p + q].reshape(wts[name].shape)
    return res


def kernel(x, c, w_ada, b_ada, g_pre_mix, g_post_mix, w_in, ssm_lam_re, ssm_lam_im, ssm_log_step, ssm_b_re, ssm_b_im, ssm_c_re, ssm_c_im, ssm_d, glu_w, glu_b, g_out_ssm, conv_w, g_out_conv, w_out, g_pre_ffn, g_post_ffn, w_up, ffn_conv_w, w_down, loss_target, m_w_ada, m_b_ada, m_g_pre_mix, m_g_post_mix, m_w_in, m_ssm_lam_re, m_ssm_lam_im, m_ssm_log_step, m_ssm_b_re, m_ssm_b_im, m_ssm_c_re, m_ssm_c_im, m_ssm_d, m_glu_w, m_glu_b, m_g_out_ssm, m_conv_w, m_g_out_conv, m_w_out, m_g_pre_ffn, m_g_post_ffn, m_w_up, m_ffn_conv_w, m_w_down, v_w_ada, v_b_ada, v_g_pre_mix, v_g_post_mix, v_w_in, v_ssm_lam_re, v_ssm_lam_im, v_ssm_log_step, v_ssm_b_re, v_ssm_b_im, v_ssm_c_re, v_ssm_c_im, v_ssm_d, v_glu_w, v_glu_b, v_g_out_ssm, v_conv_w, v_g_out_conv, v_w_out, v_g_pre_ffn, v_g_post_ffn, v_w_up, v_ffn_conv_w, v_w_down):
    args = dict(locals())
    wts = {n: args[n] for n in WEIGHTS}
    mom_m = {n: args['m_' + n] for n in WEIGHTS}
    mom_v = {n: args['v_' + n] for n in WEIGHTS}
    T = x.shape[1]
    tm = min(512, T)
    tw = min(1024, T)
    tk = min(2048, T)
    me = _me()[3]
    xt, tgt = x[0], loss_target[0]

    c_all, w_in_s = _exchange([c, w_in[0].astype(BF16)], name='gather_first', scatter=False)
    c_all = c_all.reshape(N_DEV, D_MODEL)
    b_cols = lax.dynamic_slice(b_ada, (0, me * ADA_SHARD), (1, ADA_SHARD))
    mod_cols, c_act = _mod_cols(c_all, w_ada[0], b_cols)
    (mod_all,) = _exchange([mod_cols], name='gather_mod', scatter=False)
    mod = lax.dynamic_slice(mod_all, (0, me, 0), (N_DEV, 1, ADA_SHARD)).reshape(N_MOD, 1, D_MODEL)
    sh1, sc1, gt1, sh2, sc2, gt2 = [mod[k] for k in range(N_MOD)]


    lre_x, lim_x = _expand(ssm_lam_re[0]), _expand(ssm_lam_im[0])
    lst_x = jnp.broadcast_to(ssm_log_step[0][:, None], (N_GROUPS, SSM_STATE * SSM_GROUP))
    b_re_x = ssm_b_re[0].reshape(N_GROUPS, -1)
    b_im_x = ssm_b_im[0].reshape(N_GROUPS, -1)
    ar_x, ai_x, bbr_x, bbi_x = _ssm_prep(lre_x, lim_x, lst_x, b_re_x, b_im_x)
    lam_r = ar_x[:, ::SSM_GROUP].reshape(1, N_STATE)
    lam_i = ai_x[:, ::SSM_GROUP].reshape(1, N_STATE)
    big_b_re = _block_diag_b(bbr_x.reshape(N_GROUPS, SSM_STATE, SSM_GROUP)).astype(BF16)
    big_b_im = _block_diag_b(bbi_x.reshape(N_GROUPS, SSM_STATE, SSM_GROUP)).astype(BF16)
    big_c_re = _block_diag_c(ssm_c_re[0]).astype(BF16)
    big_c_im = _block_diag_c(ssm_c_im[0]).astype(BF16)
    head = jnp.arange(D_SSM)
    avg16 = jnp.where(head[:, None] // SSM_GROUP == head[None, :] // SSM_GROUP, 1.0 / SSM_GROUP, 0.0).astype(BF16)
    hd = D_CONV // CONV_HEADS
    avg64 = jnp.where(head[:, None] // hd == head[None, :] // hd, 1.0 / hd, 0.0).astype(BF16)

    w_up_t, half = w_up[0].T, D_MODEL // 2
    (proj, h1), (ffn_conv_s, conv_s, w_up_a) = _pre_mix(
        xt, sc1, sh1, g_pre_mix, w_in_s, tw, ([ffn_conv_w[0], conv_w[0], w_up_t[:, :half].astype(BF16)], False))
    cw_full = conv_s.transpose(1, 0, 2).reshape(3, D_CONV)
    u_perm = _to_scan_rows(proj[:, :D_SSM])
    (s_re, s_im, y_perm), (w_up_b, glu_s, w_out_s) = _ssm_fwd(
        u_perm, big_b_re, big_b_im, big_c_re, big_c_im, lam_r, lam_i,
        ([w_up_t[:, half:].astype(BF16), glu_w[0].astype(BF16), w_out[0].astype(BF16)], False))
    glu_full = glu_s.reshape(D_SSM, D_SSM)
    w_out_full = w_out_s.reshape(D_MODEL, D_MODEL)
    yssm = _from_scan_rows(y_perm)
    mix_args = (ssm_d, glu_full, glu_b, g_out_ssm, cw_full, g_out_conv, avg16, avg64)
    ycat = _mix_fwd(yssm, proj, *mix_args, tw)
    o, x1, h2 = _out_proj(ycat, w_out_full, xt, gt1, g_post_mix, g_pre_ffn, sc2, sh2, tw)
    (up8, hid8), (w_down_s,) = _ffn_up(h2, w_up_a, w_up_b, ffn_conv_s, tw, ([w_down[0].astype(BF16)], False))
    wd4 = w_down_s.reshape(4, FF_SHARD, D_MODEL)
    hid4 = hid8.reshape(2, 4, T, FF_SHARD)
    ddn, dx2, loss_parts, d_gt2, d_g_post_ffn = _ffn_down(hid4, wd4, x1, tgt, gt2, g_post_ffn, tm)
    loss_local = jnp.sum(loss_parts[:, 0, 0])

    got = {}
    dhid, g_w_down = _ffn_dact(ddn, wd4, hid4, tw)
    (dup8, dcw_ffn), (got['w_down'],) = _ffn_dup(dhid.reshape(N_DEV, T, FF_SHARD), up8, ffn_conv_s, tw,
                                                 ([g_w_down.reshape(N_DEV, D_FF // N_DEV, D_MODEL)], True))
    g_w_up_halves = _grad_tn(dup8, h2, pl.BlockSpec((None, tk, FF_SHARD), lambda g, k: (g, k, 0)),
                             pl.BlockSpec((tk, D_MODEL), lambda g, k: (k, 0)), N_DEV, FF_SHARD, D_MODEL, tk,
                             'grad_w_up', parts=2)
    (dx1, d_sh2, d_sc2, d_g_pre_ffn, d_o, d_gt1, d_g_post_mix), (got_up_0, got['ffn_conv_w']) = _pre_norm_bwd(
        dup8, pl.BlockSpec((2, tw, FF_SHARD), lambda i, j: (j, i, 0)), [w_up_a, w_up_b], x1, dx2, sc2, g_pre_ffn, tw,
        'ffn_in_bwd', ([g_w_up_halves[0], dcw_ffn], True), below=(o, gt1, g_post_mix), group=2, w_t=True)

    g_w_out = _grad_tn(ycat, d_o, pl.BlockSpec((tk, D_MODEL), lambda g, k: (k, 0)),
                       pl.BlockSpec((tk, D_MODEL), lambda g, k: (k, 0)), 1, D_MODEL, D_MODEL, tk, 'grad_w_out')
    (dy, dconv, dbg, z_b, dlin_b, sums), (got['w_out'],) = _mix_bwd(
        d_o, w_out_full, yssm, proj, *mix_args, tm, ([g_w_out.reshape(N_DEV, D_MODEL // N_DEV, D_MODEL)], True))
    g_glu_w = _grad_tn(z_b, dlin_b, pl.BlockSpec((tk, D_SSM), lambda g, k: (k, 0)),
                       pl.BlockSpec((tk, D_SSM), lambda g, k: (k, 0)), 1, D_SSM, D_SSM, tk, 'grad_glu_w')
    dy_perm = _to_scan_rows(dy)
    (du_perm, dbr_blk, dbi_blk, dcr_blk, dci_blk, dar_blk, dai_blk), (got_up_1, got['glu_w']) = _ssm_bwd(
        dy_perm, u_perm, s_re, s_im, big_b_re, big_b_im, big_c_re, big_c_im, lam_r, lam_i,
        ([g_w_up_halves[1], g_glu_w.reshape(N_DEV, D_SSM // N_DEV, D_SSM)], True))
    du_ssm = _from_scan_rows(du_perm)
    dproj = _mix_bwd_proj(dconv, proj, du_ssm, dy, ssm_d, dbg, cw_full, tw)
    dbb_re = _diag_blocks(dbr_blk, True).reshape(N_GROUPS, -1)
    dbb_im = _diag_blocks(dbi_blk, True).reshape(N_GROUPS, -1)
    d_c_re = _diag_blocks(dcr_blk, False).transpose(0, 2, 1)
    d_c_im = _diag_blocks(dci_blk, False).transpose(0, 2, 1)
    lane = jnp.arange(SSM_STATE * SSM_GROUP)
    seg = jnp.where(lane[:, None] // SSM_GROUP == lane[None, :] // SSM_GROUP, 1.0, 0.0).astype(BF16)
    d_b_re_x, d_b_im_x, d_lre_x, d_lim_x, d_lst = _ssm_prep_bwd(
        lre_x, lim_x, lst_x, b_re_x, b_im_x, dbb_re, dbb_im, _expand(dar_blk.reshape(N_GROUPS, SSM_STATE)),
        _expand(dai_blk.reshape(N_GROUPS, SSM_STATE)), seg)

    row = lambda a: a.reshape(-1, PACK_COLS)
    blank = jnp.zeros((1, PACK_COLS), F32)
    small_pack = jnp.concatenate([
        d_b_re_x, d_b_im_x, row(d_c_re), row(d_c_im), blank, blank, d_gt1, d_sh2, d_sc2, d_gt2, blank,
        d_g_post_mix, row(d_lre_x[:, ::SSM_GROUP]), row(d_lim_x[:, ::SSM_GROUP]),
        jnp.pad(d_lst.reshape(1, N_GROUPS), ((0, 0), (0, PACK_COLS - N_GROUPS))), row(sums[0:4]), d_g_pre_ffn,
        d_g_post_ffn, jnp.zeros((SMALL_ROWS - 145, PACK_COLS), F32)])
    g_w_in, (small_all,) = _grad_w_in(h1, dproj, tk, ([small_pack], False))
    g_conv_slots = jnp.concatenate([sums[4:7], jnp.zeros((5, D_CONV), F32)]).reshape(
        8, N_DEV, D_CONV // N_DEV).transpose(1, 0, 2)
    (grad_x, d_sh1, d_sc1, d_g_pre_mix), (got['w_in'], got['conv_w']) = _pre_norm_bwd(
        dproj, pl.BlockSpec((tw, 4 * IN_SHARD), lambda i, j: (i, j)), [w_in_s], xt, dx1, sc1, g_pre_mix, tw,
        'mix_in_bwd', ([g_w_in, g_conv_slots], True), group=4)
    late_pack = jnp.concatenate([d_sh1, d_sc1, d_g_pre_mix, jnp.full((1, PACK_COLS), loss_local, F32),
                                 jnp.zeros((4, PACK_COLS), F32)])
    (late_all,) = _exchange([late_pack], name='gather_late_grads', scatter=False)
    loss = jnp.sum(late_all[:, 3, 0])
    res = _adamw_small(small_all, late_all, wts, mom_m, mom_v)

    dmod_all = jnp.concatenate([late_all[:, 0:2, :], small_all[:, B_ADA_ROW + 2:B_ADA_ROW + N_MOD, :]],
                               axis=1).reshape(N_DEV, N_MOD * D_MODEL)
    dmod_cols = lax.dynamic_slice(dmod_all, (0, me * ADA_SHARD), (N_DEV, ADA_SHARD))
    g_w_ada = _grad_w_ada(c_act.T, dmod_cols)

    pieces = {n: [slots[:, :3, :] if n in ('conv_w', 'ffn_conv_w') else slots] for n, slots in got.items()}
    for n, parts in pieces.items():
        outs = _adamw(parts, wts[n][0], mom_m[n][0], mom_v[n][0], 'adamw_' + n)
        for kind, val in zip(('g', 'd', 'm', 'v'), outs):
            res[kind, n] = val[None]
    outs = _adamw([got_up_0, got_up_1], w_up[0].T, m_w_up[0].T, v_w_up[0].T, 'adamw_w_up')
    for kind, val in zip(('g', 'd', 'm', 'v'), outs):
        res[kind, 'w_up'] = val.T[None]
    outs = _adamw([g_w_ada[None]], w_ada[0], m_w_ada[0], v_w_ada[0], 'adamw_w_ada')
    for kind, val in zip(('g', 'd', 'm', 'v'), outs):
        res[kind, 'w_ada'] = val[None]

    return (loss, grad_x[None], *[res['g', n] for n in WEIGHTS], *[res['d', n] for n in WEIGHTS],
            *[res['m', n] for n in WEIGHTS], *[res['v', n] for n in WEIGHTS])
```

```python
import math

import jax
import jax.numpy as jnp
from jax import lax
from jax.experimental import pallas as pl
from jax.experimental.pallas import tpu as pltpu

F32, BF16 = jnp.float32, jnp.bfloat16

D_MODEL = 1024
D_SSM = 512
D_CONV = 512
SSM_GROUP = 16
N_GROUPS = 32
SSM_STATE = 64
N_STATE = N_GROUPS * SSM_STATE
CONV_HEADS = 8
D_FF = 2816
N_MOD = 6
D_IN_PROJ = D_SSM + 3 * D_CONV
N_DEV = 8
FF_SHARD = 2 * D_FF // N_DEV
IN_SHARD = D_IN_PROJ // N_DEV
ADA_SHARD = N_MOD * D_MODEL // N_DEV
EPS = 1e-6
LAMBDA_RE_MAX = -1e-4
ADAM_LR, ADAM_B1, ADAM_B2, ADAM_EPS, ADAM_WD, ADAM_STEP = 0.001, 0.9, 0.999, 1e-08, 0.01, 10
GELU_C = math.sqrt(2.0 / math.pi)
GELU_A = 0.044715

SUBLANES = 8
HALO = 8
HALO16 = 16
SCAN_UNROLL = 16
STATE_BLOCK = 512
CHAN_BLOCK = 128
VMEM_BIG = 48 << 20
VMEM_MOST = 58 << 20

WEIGHTS = ['w_ada', 'b_ada', 'g_pre_mix', 'g_post_mix', 'w_in', 'ssm_lam_re', 'ssm_lam_im', 'ssm_log_step',
           'ssm_b_re', 'ssm_b_im', 'ssm_c_re', 'ssm_c_im', 'ssm_d', 'glu_w', 'glu_b', 'g_out_ssm', 'conv_w',
           'g_out_conv', 'w_out', 'g_pre_ffn', 'g_post_ffn', 'w_up', 'ffn_conv_w', 'w_down']
PACK_COLS = 1024


def _call(body, *, name, grid, in_specs, out_specs, out_shape, scratch=(), sem=None, vmem=None, ride=None):
    params = {}
    if vmem is not None:
        params['vmem_limit_bytes'] = vmem
    if ride is None:
        if sem is not None:
            params['dimension_semantics'] = sem
        return pl.pallas_call(body, name=name, grid=grid, in_specs=in_specs, out_specs=out_specs,
                              out_shape=out_shape, scratch_shapes=list(scratch),
                              compiler_params=pltpu.CompilerParams(**params))
    arrs, scatter = ride
    single = not isinstance(out_shape, (list, tuple))
    out_shape_l = [out_shape] if single else list(out_shape)
    out_specs_l = [out_specs] if single else list(out_specs)
    n, n_in, n_out, n_scr = len(arrs), len(in_specs), len(out_shape_l), len(scratch)
    any_spec = pl.BlockSpec(memory_space=pl.ANY)
    params['dimension_semantics'] = ('arbitrary',) * len(grid)

    def carried(*refs):
        ins, rin = refs[:n_in], refs[n_in:n_in + n]
        outs, rout = refs[n_in + n:n_in + n + n_out], refs[n_in + n + n_out:n_in + 2 * n + n_out]
        scr, sems = refs[n_in + 2 * n + n_out:n_in + 2 * n + n_out + n_scr], refs[n_in + 2 * n + n_out + n_scr:]
        first = pl.program_id(0) == 0
        last = pl.program_id(0) == grid[0] - 1
        for ax in range(1, len(grid)):
            first = jnp.logical_and(first, pl.program_id(ax) == 0)
            last = jnp.logical_and(last, pl.program_id(ax) == grid[ax] - 1)

        @pl.when(first)
        def _():
            _exchange_start(rin, rout, sems, scatter)

        body(*ins, *outs, *scr)

        @pl.when(last)
        def _():
            _exchange_wait(rin, rout, sems, scatter)

    call = pl.pallas_call(carried, name=name, grid=grid, in_specs=list(in_specs) + [any_spec] * n,
                          out_specs=out_specs_l + [any_spec] * n,
                          out_shape=out_shape_l + _exchange_shapes(arrs, scatter),
                          scratch_shapes=list(scratch) + _exchange_sems(n),
                          compiler_params=pltpu.CompilerParams(**params))

    def run(*args):
        res = call(*args, *arrs)
        own = res[0] if single else list(res[:n_out])
        return own, list(res[n_out:])

    return run


def _const(shape):
    nd = len(shape)
    return pl.BlockSpec(shape, lambda *_: (0,) * nd)


def _sds(shape, dtype=F32):
    return jax.ShapeDtypeStruct(shape, dtype)


def _dot(a, b):
    return jnp.dot(a, b, preferred_element_type=F32)


def _dot_nt(a, b):
    return lax.dot_general(a, b, (((1,), (1,)), ((), ())), preferred_element_type=F32)


def _dot_tn(a, b):
    return lax.dot_general(a, b, (((0,), (0,)), ((), ())), preferred_element_type=F32)


def _dot_split(x, mat, parts):
    acc = None
    rem = x
    for _ in range(parts):
        piece = rem.astype(BF16)
        rem = rem - piece.astype(F32)
        term = _dot(piece, mat)
        acc = term if acc is None else acc + term
    return acc


def _sigmoid(x):
    return 1.0 / (1.0 + jnp.exp(-x))


def _gelu(x):
    t = jnp.tanh(GELU_C * (x + GELU_A * x * x * x))
    return 0.5 * x * (1.0 + t), t


def _gelu_grad(x, t):
    return 0.5 * (1.0 + t) + 0.5 * x * (1.0 - t * t) * GELU_C * (1.0 + 3.0 * GELU_A * x * x)


def _rsqrt_mean(x):
    return lax.rsqrt(jnp.mean(x * x, axis=-1, keepdims=True) + EPS)


def _colsum(x):
    return jnp.sum(x, axis=0, keepdims=True)


def _shifts_down(x, halo):
    ext = jnp.concatenate([halo, x], axis=0)
    return pltpu.roll(ext, 1, 0)[halo.shape[0]:], pltpu.roll(ext, 2, 0)[halo.shape[0]:]


def _shifts_up(x, halo):
    n = x.shape[0]
    ext = jnp.concatenate([x, halo], axis=0)
    total = ext.shape[0]
    return pltpu.roll(ext, total - 1, 0)[:n], pltpu.roll(ext, total - 2, 0)[:n]


def _conv3(x, halo, w_ref):
    x1, x2 = _shifts_down(x, halo)
    return w_ref[0:1, :] * x2 + w_ref[1:2, :] * x1 + w_ref[2:3, :] * x, x1, x2


def _conv3_t(g, halo, w_ref):
    g1, g2 = _shifts_up(g, halo)
    return w_ref[2:3, :] * g + w_ref[1:2, :] * g1 + w_ref[0:1, :] * g2, g1, g2


def _silu_parts(x):
    s = _sigmoid(x)
    return x * s, s * (1.0 + x * (1.0 - s))


def _norm_bwd(dn, x, r, g):
    gd = g * dn
    return r * gd - x * (r * r * r) * jnp.mean(gd * x, axis=-1, keepdims=True)


def _head_norm_bwd(dn, y, rs, g, avg):
    gd = g * dn
    return rs * gd - y * (rs * rs * rs) * _dot_split(gd * y, avg, 2)


def _me():
    x, y, c = lax.axis_index('x'), lax.axis_index('y'), lax.axis_index('c')
    return x, y, c, 4 * x + 2 * y + c


def _peer(k):
    x, y, c, _ = _me()
    px = 1 - x if k & 4 else x
    py = 1 - y if k & 2 else y
    pc = 1 - c if k & 1 else c
    return (px, py, pc), 4 * px + 2 * py + pc


SIBLING = 1
OTHER_CHIPS = (2, 4, 6)


def _remote(src, dst, sems, a, k, dev):
    return pltpu.make_async_remote_copy(src_ref=src, dst_ref=dst, send_sem=sems[0].at[a, k - 1],
                                        recv_sem=sems[1].at[a, k - 1], device_id=dev,
                                        device_id_type=pl.DeviceIdType.MESH)


def _exchange_copies(ins, outs, sems, scatter):
    me = _me()[3]
    local, first, relay, arrivals = [], [], [], []
    for a in range(len(ins)):
        src = ins[a].at[me] if scatter else ins[a]
        local.append(pltpu.make_async_copy(src, outs[a].at[me], sems[2].at[a]))
        for k in range(1, N_DEV):
            dev, idx = _peer(k)
            landed = _remote(src, outs[a].at[idx], sems, a, k, dev)
            if scatter:
                first.append(_remote(ins[a].at[idx], outs[a].at[me], sems, a, k, dev))
                arrivals.append(landed)
            elif k == SIBLING:
                first.append(_remote(src, outs[a].at[me], sems, a, k, dev))
                arrivals.append(landed)
            elif k in OTHER_CHIPS:
                first.append(_remote(src, outs[a].at[me], sems, a, k, dev))
                sib, _ = _peer(SIBLING)
                relay.append((landed, _remote(outs[a].at[idx], outs[a].at[idx], sems, a, k | SIBLING, sib)))
            else:
                arrivals.append(landed)
    return local, first, relay, arrivals


def _exchange_start(ins, outs, sems, scatter):
    local, first, _, _ = _exchange_copies(ins, outs, sems, scatter)
    for cp in local + first:
        cp.start()


def _exchange_wait(ins, outs, sems, scatter):
    local, first, relay, arrivals = _exchange_copies(ins, outs, sems, scatter)
    for landed, forward in relay:
        landed.wait_recv()
        forward.start()
    for cp in arrivals:
        cp.wait_recv()
    for cp in first + [forward for _, forward in relay]:
        cp.wait_send()
    for cp in local:
        cp.wait()


def _exchange_shapes(arrs, scatter):
    return [_sds(a.shape if scatter else (N_DEV,) + a.shape, a.dtype) for a in arrs]


def _exchange_sems(n):
    return [pltpu.SemaphoreType.DMA((n, N_DEV - 1)), pltpu.SemaphoreType.DMA((n, N_DEV - 1)),
            pltpu.SemaphoreType.DMA((n,))]


def _exchange(arrs, *, name, scatter):
    n = len(arrs)

    def body(*refs):
        _exchange_start(refs[:n], refs[n:2 * n], refs[2 * n:], scatter)
        _exchange_wait(refs[:n], refs[n:2 * n], refs[2 * n:], scatter)

    any_spec = pl.BlockSpec(memory_space=pl.ANY)
    outs = pl.pallas_call(body, name=name, out_shape=_exchange_shapes(arrs, scatter), in_specs=[any_spec] * n,
                          out_specs=[any_spec] * n, scratch_shapes=_exchange_sems(n))(*arrs)
    return list(outs)


def _mod_cols(c_all, w_ada, b_cols):
    def body(c_ref, w_ref, b_ref, mod_ref, act_ref):
        c = c_ref[...]
        act = c * _sigmoid(c)
        act_ref[...] = act
        mod_ref[...] = _dot(act.astype(BF16), w_ref[...].astype(BF16)) + b_ref[...]

    return _call(body, name='mod_cols', grid=(1,),
                 in_specs=[_const(c_all.shape), _const(w_ada.shape), _const(b_cols.shape)],
                 out_specs=[_const((N_DEV, ADA_SHARD)), _const(c_all.shape)],
                 out_shape=[_sds((N_DEV, ADA_SHARD)), _sds(c_all.shape)], vmem=VMEM_BIG)(c_all, w_ada, b_cols)


def _grad_w_ada(act_t, dmod_cols):
    def body(a_ref, d_ref, o_ref):
        o_ref[...] = _dot(a_ref[...], d_ref[...])

    return _call(body, name='grad_w_ada', grid=(1,), in_specs=[_const(act_t.shape), _const(dmod_cols.shape)],
                 out_specs=_const((D_MODEL, ADA_SHARD)), out_shape=_sds((D_MODEL, ADA_SHARD)),
                 vmem=VMEM_BIG)(act_t, dmod_cols)


def _pre_mix(x, sc, sh, g, w_s, tm, ride):
    T = x.shape[0]
    group = 4

    def body(x_ref, sc_ref, sh_ref, g_ref, w_ref, proj_ref, h_ref):
        @pl.when(pl.program_id(1) == 0)
        def _():
            xv = x_ref[...]
            h_ref[...] = ((xv * _rsqrt_mean(xv) * g_ref[...]) * (1.0 + sc_ref[...]) + sh_ref[...]).astype(BF16)

        for s in range(group):
            proj_ref[:, s * IN_SHARD:(s + 1) * IN_SHARD] = _dot(h_ref[...], w_ref[s])

    row = pl.BlockSpec((tm, D_MODEL), lambda i, j: (i, 0))
    vec = _const((1, D_MODEL))
    return _call(body, name='pre_mix', grid=(T // tm, N_DEV // group),
                 in_specs=[row, vec, vec, vec, pl.BlockSpec((group, D_MODEL, IN_SHARD), lambda i, j: (j, 0, 0))],
                 out_specs=[pl.BlockSpec((tm, group * IN_SHARD), lambda i, j: (i, j)), row],
                 out_shape=[_sds((T, D_IN_PROJ)), _sds((T, D_MODEL), BF16)],
                 sem=('parallel', 'arbitrary'), ride=ride)(x, sc, sh, g, w_s)


def _halo_before(tm, rows=HALO):
    return lambda i: jnp.maximum(i * (tm // rows) - 1, 0)


def _halo_after(tm, T, rows=HALO):
    return lambda i: jnp.minimum((i + 1) * (tm // rows), T // rows - 1)


def _mix_fwd(yssm, proj, d, glu_w, glu_b, g_ssm, cw, g_conv, avg16, avg64, tm):
    T = yssm.shape[0]
    hb = _halo_before(tm)

    def body(y_ref, p_ref, ph_ref, d_ref, gw_ref, gb_ref, gs_ref, cw_ref, gc_ref, a16_ref, a64_ref, o_ref):
        i = pl.program_id(0)
        u = p_ref[:, 0:D_SSM]
        y = y_ref[...] + d_ref[...] * u
        z, _ = _gelu(y)
        gate = _sigmoid(_dot(z.astype(BF16), gw_ref[...]) + gb_ref[...])
        ya = z * gate
        rs = lax.rsqrt(_dot_split(ya * ya, a16_ref[...], 2) + EPS)
        o_ref[:, 0:D_SSM] = (ya * rs * gs_ref[...]).astype(BF16)
        bg = p_ref[:, D_SSM:D_SSM + D_CONV]
        cv = p_ref[:, D_SSM + D_CONV:D_SSM + 2 * D_CONV] * p_ref[:, D_SSM + 2 * D_CONV:D_IN_PROJ]
        hv = ph_ref[:, D_SSM + D_CONV:D_SSM + 2 * D_CONV] * ph_ref[:, D_SSM + 2 * D_CONV:D_IN_PROJ]
        hv = jnp.where(i > 0, hv, 0.0)
        conv, _, _ = _conv3(cv, hv, cw_ref)
        yb = bg * conv
        rsb = lax.rsqrt(_dot_split(yb * yb, a64_ref[...], 2) + EPS)
        o_ref[:, D_SSM:D_MODEL] = (yb * rsb * gc_ref[...]).astype(BF16)

    vec = _const((1, D_SSM))
    sq = _const((D_SSM, D_SSM))
    return _call(body, name='mix_fwd', grid=(T // tm,),
                 in_specs=[pl.BlockSpec((tm, D_SSM), lambda i: (i, 0)), pl.BlockSpec((tm, D_IN_PROJ), lambda i: (i, 0)),
                           pl.BlockSpec((HALO, D_IN_PROJ), lambda i: (hb(i), 0)), vec, sq, vec, vec,
                           _const((3, D_CONV)), vec, sq, sq],
                 out_specs=pl.BlockSpec((tm, D_MODEL), lambda i: (i, 0)), out_shape=_sds((T, D_MODEL), BF16),
                 sem=('parallel',), vmem=VMEM_BIG)(yssm, proj, proj, d, glu_w, glu_b, g_ssm, cw, g_conv, avg16, avg64)


def _out_proj(ycat, w_out, x, gt, g_post, g_pre, sc, sh, tm):
    T = x.shape[0]

    def body(y_ref, w_ref, x_ref, gt_ref, gp_ref, g2_ref, sc_ref, sh_ref, o_ref, x1_ref, h_ref):
        o = _dot(y_ref[...], w_ref[...])
        o_ref[...] = o
        x1 = x_ref[...] + gt_ref[...] * (o * _rsqrt_mean(o) * gp_ref[...])
        x1_ref[...] = x1
        h_ref[...] = ((x1 * _rsqrt_mean(x1) * g2_ref[...]) * (1.0 + sc_ref[...]) + sh_ref[...]).astype(BF16)

    row = pl.BlockSpec((tm, D_MODEL), lambda i: (i, 0))
    vec = _const((1, D_MODEL))
    return _call(body, name='out_proj', grid=(T // tm,),
                 in_specs=[row, _const((D_MODEL, D_MODEL)), row, vec, vec, vec, vec, vec],
                 out_specs=[row, row, row],
                 out_shape=[_sds((T, D_MODEL)), _sds((T, D_MODEL)), _sds((T, D_MODEL), BF16)],
                 sem=('parallel',), vmem=VMEM_BIG)(ycat, w_out, x, gt, g_post, g_pre, sc, sh)


def _ffn_up(h2, w_a, w_b, cw8, tm, ride):
    T = h2.shape[0]
    hb = _halo_before(tm, HALO16)
    half = D_MODEL // 2

    def body(h_ref, hh_ref, wa_ref, wb_ref, cw_ref, up_ref, hid_ref):
        def times_w(ref, s):
            return _dot_nt(ref[:, :half], wa_ref[s]) + _dot_nt(ref[:, half:], wb_ref[s])

        for s in range(2):
            up = times_w(h_ref, s)
            up_ref[s] = up.astype(BF16)
            before = jnp.where(pl.program_id(0) > 0, times_w(hh_ref, s), 0.0)
            hid_ref[s] = _conv3(up, before, cw_ref.at[s])[0].astype(BF16)

    out = pl.BlockSpec((2, tm, FF_SHARD), lambda i, j: (j, i, 0))
    return _call(body, name='ffn_up', grid=(T // tm, N_DEV // 2),
                 in_specs=[pl.BlockSpec((tm, D_MODEL), lambda i, j: (i, 0)),
                           pl.BlockSpec((HALO16, D_MODEL), lambda i, j: (hb(i), 0)),
                           pl.BlockSpec((2, FF_SHARD, half), lambda i, j: (j, 0, 0)),
                           pl.BlockSpec((2, FF_SHARD, half), lambda i, j: (j, 0, 0)),
                           pl.BlockSpec((2, 3, FF_SHARD), lambda i, j: (j, 0, 0))],
                 out_specs=[out, out], out_shape=[_sds((N_DEV, T, FF_SHARD), BF16)] * 2,
                 sem=('parallel', 'parallel'), vmem=VMEM_BIG, ride=ride)(h2, h2, w_a, w_b, cw8)


def _ffn_down(hid4, wd4, x1, tgt, gt, g_post, tm):
    T = x1.shape[0]
    nb = T // tm

    def body(a_ref, w_ref, x1_ref, t_ref, gt_ref, g_ref, ddn_ref, dx_ref, loss_ref, dgt_ref, dg_ref, dn_ref):
        i, j = pl.program_id(0), pl.program_id(1)
        part = None
        for s in range(2):
            act = (_silu_parts(a_ref[0, s].astype(F32))[0] * a_ref[1, s].astype(F32)).astype(BF16)
            term = _dot(act, w_ref[s])
            part = term if part is None else part + term

        @pl.when(jnp.logical_and(i == 0, j == 0))
        def _():
            dgt_ref[...] = jnp.zeros_like(dgt_ref)
            dg_ref[...] = jnp.zeros_like(dg_ref)

        @pl.when(j == 0)
        def _():
            dn_ref[...] = part

        @pl.when(j > 0)
        def _():
            dn_ref[...] += part

        @pl.when(j == 1)
        def _():
            dn, gv, gate = dn_ref[...], g_ref[...], gt_ref[...]
            r = _rsqrt_mean(dn)
            normed = dn * r * gv
            err = x1_ref[...] + gate * normed - t_ref[...]
            dx = err * (1.0 / D_MODEL)
            dx_ref[...] = dx
            tot = jnp.sum(jnp.sum(err * err, axis=1, keepdims=True), axis=0, keepdims=True) * (0.5 / D_MODEL)
            loss_ref[...] = jnp.broadcast_to(tot, (8, 128))
            dgt_ref[...] += _colsum(dx * normed)
            dnn = dx * gate
            dg_ref[...] += _colsum(dnn * dn * r)
            ddn_ref[...] = _norm_bwd(dnn, dn, r, gv).astype(BF16)

    row = pl.BlockSpec((tm, D_MODEL), lambda i, j: (i, 0))
    vec = _const((1, D_MODEL))
    return _call(body, name='ffn_down', grid=(nb, 2),
                 in_specs=[pl.BlockSpec((2, 2, tm, FF_SHARD), lambda i, j: (0, j, i, 0)),
                           pl.BlockSpec((2, FF_SHARD, D_MODEL), lambda i, j: (j, 0, 0)), row, row, vec, vec],
                 out_specs=[row, row, pl.BlockSpec((None, 8, 128), lambda i, j: (i, 0, 0)), vec, vec],
                 out_shape=[_sds((T, D_MODEL), BF16), _sds((T, D_MODEL)), _sds((nb, 8, 128)), _sds((1, D_MODEL)),
                            _sds((1, D_MODEL))],
                 scratch=[pltpu.VMEM((tm, D_MODEL), F32)], sem=('arbitrary', 'arbitrary'),
                 vmem=VMEM_BIG)(hid4, wd4, x1, tgt, gt, g_post)


def _ssm_prep(lre, lim, lst, b_re, b_im):
    def body(lre_ref, lim_ref, lst_ref, br_ref, bi_ref, ar_ref, ai_ref, bbr_ref, bbi_ref):
        ar, ai, qr, qi = _zoh(lre_ref[...], lim_ref[...], lst_ref[...])[:4]
        ar_ref[...] = ar
        ai_ref[...] = ai
        bbr_ref[...] = qr * br_ref[...] - qi * bi_ref[...]
        bbi_ref[...] = qr * bi_ref[...] + qi * br_ref[...]

    shp = lre.shape
    return _call(body, name='ssm_prep', grid=(1,), in_specs=[_const(shp)] * 5, out_specs=[_const(shp)] * 4,
                 out_shape=[_sds(shp)] * 4)(lre, lim, lst, b_re, b_im)


def _zoh(lre, lim, lst):
    lr = jnp.minimum(lre, LAMBDA_RE_MAX)
    st = jnp.exp(lst)
    mag = jnp.exp(lr * st)
    ar = mag * jnp.cos(lim * st)
    ai = mag * jnp.sin(lim * st)
    den = lr * lr + lim * lim
    qr = ((ar - 1.0) * lr + ai * lim) / den
    qi = (ai * lr - (ar - 1.0) * lim) / den
    return ar, ai, qr, qi, lr, st, den


def _ssm_prep_bwd(lre, lim, lst, b_re, b_im, dbbr, dbbi, dar, dai, seg):
    def body(lre_ref, lim_ref, lst_ref, br_ref, bi_ref, dbbr_ref, dbbi_ref, dar_ref, dai_ref, seg_ref,
             dbr_ref, dbi_ref, dlre_ref, dlim_ref, dlst_ref):
        lre_v = lre_ref[...]
        li = lim_ref[...]
        ar, ai, qr, qi, lr, st, den = _zoh(lre_v, li, lst_ref[...])
        br, bi, gbr, gbi = br_ref[...], bi_ref[...], dbbr_ref[...], dbbi_ref[...]
        dbr_ref[...] = qr * gbr + qi * gbi
        dbi_ref[...] = qr * gbi - qi * gbr
        gqr = _dot_split(br * gbr + bi * gbi, seg_ref[...], 3)
        gqi = _dot_split(br * gbi - bi * gbr, seg_ref[...], 3)
        ir, ii = lr / den, -li / den
        gar = dar_ref[...] + ir * gqr + ii * gqi
        gai = dai_ref[...] + ir * gqi - ii * gqr
        tr, ti = qr * ir - qi * ii, qr * ii + qi * ir
        glr = -(tr * gqr + ti * gqi)
        gli = -(tr * gqi - ti * gqr)
        gzr = ar * gar + ai * gai
        gzi = ar * gai - ai * gar
        glr = glr + st * gzr
        gli = gli + st * gzi
        gst = (lr * gzr + li * gzi) * st
        dlre_ref[...] = jnp.where(lre_v < LAMBDA_RE_MAX, glr, 0.0)
        dlim_ref[...] = gli
        dlst_ref[...] = jnp.sum(gst, axis=1, keepdims=True) * (1.0 / SSM_GROUP)

    shp = lre.shape
    return _call(body, name='ssm_prep_bwd', grid=(1,), in_specs=[_const(shp)] * 9 + [_const(seg.shape)],
                 out_specs=[_const(shp)] * 4 + [_const((N_GROUPS, 1))],
                 out_shape=[_sds(shp)] * 4 + [_sds((N_GROUPS, 1))], vmem=VMEM_BIG)(
                     lre, lim, lst, b_re, b_im, dbbr, dbbi, dar, dai, seg)


def _scan_specs(T):
    return dict(
        chan=pl.BlockSpec((T, CHAN_BLOCK), lambda cb: (0, cb)),
        state=pl.BlockSpec((T, STATE_BLOCK), lambda cb: (0, cb)),
        b=pl.BlockSpec((CHAN_BLOCK, STATE_BLOCK), lambda cb: (cb, cb)),
        c=pl.BlockSpec((STATE_BLOCK, CHAN_BLOCK), lambda cb: (cb, cb)),
        lam=pl.BlockSpec((1, STATE_BLOCK), lambda cb: (0, cb)),
    )


def _complex_power(re, im, n):
    out = None
    while True:
        if n & 1:
            out = (re, im) if out is None else (out[0] * re - out[1] * im, out[0] * im + out[1] * re)
        n >>= 1
        if n == 0:
            return out
        re, im = re * re - im * im, 2.0 * re * im


def _rows8(i):
    if isinstance(i, int):
        return pl.ds(i * SUBLANES, SUBLANES)
    return pl.ds(pl.multiple_of(i * SUBLANES, SUBLANES), SUBLANES)


def _scan_loop(n_steps, body, init):
    trips = n_steps // SCAN_UNROLL

    def trip(t, carry):
        for u in range(SCAN_UNROLL):
            carry = body(t * SCAN_UNROLL + u, carry)
        return carry

    carry = lax.fori_loop(0, trips, trip, init)
    for step in range(trips * SCAN_UNROLL, n_steps):
        carry = body(step, carry)
    return carry


def _ssm_fwd(u_perm, b_re, b_im, c_re, c_im, lam_r, lam_i, ride):
    T = u_perm.shape[0]
    ls = T // SUBLANES
    rc = min(1024, T)
    sp = _scan_specs(T)

    def body(u_ref, bre_ref, bim_ref, cre_ref, cim_ref, lr_ref, li_ref, so_re_ref, so_im_ref, y_ref, sre_ref, sim_ref):
        for c in range(T // rc):
            rows = pl.ds(c * rc, rc)
            ub = u_ref[rows, :].astype(BF16)
            sre_ref[rows, :] = _dot(ub, bre_ref[...])
            sim_ref[rows, :] = _dot(ub, bim_ref[...])
        shp = (SUBLANES, STATE_BLOCK)
        lr = jnp.broadcast_to(lr_ref[...], shp)
        li = jnp.broadcast_to(li_ref[...], shp)
        zero = jnp.zeros(shp, F32)

        def step(i, carry):
            sr, si = carry
            rows = _rows8(i)
            nr = lr * sr - li * si + sre_ref[rows, :]
            ni = lr * si + li * sr + sim_ref[rows, :]
            sre_ref[rows, :] = nr
            sim_ref[rows, :] = ni
            return nr, ni

        fr, fi = _scan_loop(ls, step, (zero, zero))
        pr, pi_ = _complex_power(lr, li, ls)
        row = lax.broadcasted_iota(jnp.int32, shp, 0)
        ir, ii = zero, zero
        for _ in range(SUBLANES - 1):
            er = fr + pr * ir - pi_ * ii
            ei = fi + pr * ii + pi_ * ir
            ir = jnp.where(row == 0, 0.0, pltpu.roll(er, 1, 0))
            ii = jnp.where(row == 0, 0.0, pltpu.roll(ei, 1, 0))

        def fix(i, carry):
            cr, ci = carry
            rows = _rows8(i)
            nr = lr * cr - li * ci
            ni = lr * ci + li * cr
            sre_ref[rows, :] += nr
            sim_ref[rows, :] += ni
            return nr, ni

        _scan_loop(ls, fix, (ir, ii))
        for c in range(T // rc):
            rows = pl.ds(c * rc, rc)
            s_r, s_i = sre_ref[rows, :].astype(BF16), sim_ref[rows, :].astype(BF16)
            so_re_ref[rows, :] = s_r
            so_im_ref[rows, :] = s_i
            y_ref[rows, :] = _dot(s_r, cre_ref[...]) - _dot(s_i, cim_ref[...])

    return _call(body, name='ssm_fwd', grid=(N_STATE // STATE_BLOCK,),
                 in_specs=[sp['chan'], sp['b'], sp['b'], sp['c'], sp['c'], sp['lam'], sp['lam']],
                 out_specs=[sp['state'], sp['state'], sp['chan']],
                 out_shape=[_sds((T, N_STATE), BF16), _sds((T, N_STATE), BF16), _sds((T, D_SSM))],
                 scratch=[pltpu.VMEM((T, STATE_BLOCK), F32), pltpu.VMEM((T, STATE_BLOCK), F32)],
                 sem=('arbitrary',), vmem=VMEM_MOST, ride=ride)(u_perm, b_re, b_im, c_re, c_im, lam_r, lam_i)


def _ssm_bwd(dy_perm, u_perm, s_re, s_im, b_re, b_im, c_re, c_im, lam_r, lam_i, ride):
    T = u_perm.shape[0]
    ls = T // SUBLANES
    rc = min(1024, T)
    sp = _scan_specs(T)
    ncb = N_STATE // STATE_BLOCK

    def body(dy_ref, u_ref, sre_ref, sim_ref, bre_ref, bim_ref, cre_ref, cim_ref, lr_ref, li_ref,
             du_ref, dbr_ref, dbi_ref, dcr_ref, dci_ref, dar_ref, dai_ref, gre_ref, gim_ref):
        shp = (SUBLANES, STATE_BLOCK)
        zero = jnp.zeros(shp, F32)
        tail = pl.ds(T, SUBLANES)
        gre_ref[tail, :] = zero
        gim_ref[tail, :] = zero
        for c in range(T // rc):
            rows = pl.ds(c * rc, rc)
            dyb = dy_ref[rows, :].astype(BF16)
            gre_ref[rows, :] = _dot_nt(dyb, cre_ref[...])
            gim_ref[rows, :] = -_dot_nt(dyb, cim_ref[...])
        lr = jnp.broadcast_to(lr_ref[...], shp)
        li = jnp.broadcast_to(li_ref[...], shp)

        def step(k, carry):
            gr, gi = carry
            rows = _rows8(ls - 1 - k)
            nr = lr * gr + li * gi + gre_ref[rows, :]
            ni = lr * gi - li * gr + gim_ref[rows, :]
            gre_ref[rows, :] = nr
            gim_ref[rows, :] = ni
            return nr, ni

        fr, fi = _scan_loop(ls, step, (zero, zero))
        pr, pi_ = _complex_power(lr, -li, ls)
        row = lax.broadcasted_iota(jnp.int32, shp, 0)
        cr, ci = zero, zero
        for _ in range(SUBLANES - 1):
            er = fr + pr * cr - pi_ * ci
            ei = fi + pr * ci + pi_ * cr
            cr = jnp.where(row == SUBLANES - 1, 0.0, pltpu.roll(er, SUBLANES - 1, 0))
            ci = jnp.where(row == SUBLANES - 1, 0.0, pltpu.roll(ei, SUBLANES - 1, 0))

        def fix(k, carry):
            dr, di = carry
            rows = _rows8(ls - 1 - k)
            dr, di = lr * dr + li * di, lr * di - li * dr
            gre_ref[rows, :] += dr
            gim_ref[rows, :] += di
            return dr, di

        _scan_loop(ls, fix, (cr, ci))

        acc_r = jnp.zeros((1, STATE_BLOCK), F32)
        acc_i = jnp.zeros((1, STATE_BLOCK), F32)
        for c in range(T // rc):
            rows, nxt = pl.ds(c * rc, rc), pl.ds(c * rc + SUBLANES, rc)
            s_r, s_i = sre_ref[rows, :].astype(F32), sim_ref[rows, :].astype(F32)
            g_r, g_i = gre_ref[nxt, :], gim_ref[nxt, :]
            acc_r = acc_r + _colsum(g_r * s_r + g_i * s_i)
            acc_i = acc_i + _colsum(g_i * s_r - g_r * s_i)
        last = pl.ds(T - 2 * SUBLANES, 2 * SUBLANES)
        first = pl.ds(0, SUBLANES)
        spr = jnp.where(row == 0, 0.0, pltpu.roll(sre_ref[last, :].astype(F32)[SUBLANES:], 1, 0))
        spi = jnp.where(row == 0, 0.0, pltpu.roll(sim_ref[last, :].astype(F32)[SUBLANES:], 1, 0))
        gr, gi = gre_ref[first, :], gim_ref[first, :]
        dar_ref[...] = acc_r + _colsum(gr * spr + gi * spi)
        dai_ref[...] = acc_i + _colsum(gi * spr - gr * spi)

        for c in range(T // rc):
            rows = pl.ds(c * rc, rc)
            g_r, g_i = gre_ref[rows, :].astype(BF16), gim_ref[rows, :].astype(BF16)
            s_r, s_i = sre_ref[rows, :], sim_ref[rows, :]
            ub, dyb = u_ref[rows, :].astype(BF16), dy_ref[rows, :].astype(BF16)
            du_ref[rows, :] = _dot_nt(g_r, bre_ref[...]) + _dot_nt(g_i, bim_ref[...])
            parts = (_dot_tn(ub, g_r), _dot_tn(ub, g_i), _dot_tn(s_r, dyb), -_dot_tn(s_i, dyb))
            outs = (dbr_ref, dbi_ref, dcr_ref, dci_ref)
            for o_ref, part in zip(outs, parts):
                if c == 0:
                    o_ref[...] = part
                else:
                    o_ref[...] += part

    blk = lambda r, c: pl.BlockSpec((None, r, c), lambda cb: (cb, 0, 0))
    return _call(body, name='ssm_bwd', grid=(ncb,),
                 in_specs=[sp['chan'], sp['chan'], sp['state'], sp['state'], sp['b'], sp['b'], sp['c'], sp['c'],
                           sp['lam'], sp['lam']],
                 out_specs=[sp['chan'], blk(CHAN_BLOCK, STATE_BLOCK), blk(CHAN_BLOCK, STATE_BLOCK),
                            blk(STATE_BLOCK, CHAN_BLOCK), blk(STATE_BLOCK, CHAN_BLOCK), blk(1, STATE_BLOCK),
                            blk(1, STATE_BLOCK)],
                 out_shape=[_sds((T, D_SSM)), _sds((ncb, CHAN_BLOCK, STATE_BLOCK)), _sds((ncb, CHAN_BLOCK, STATE_BLOCK)),
                            _sds((ncb, STATE_BLOCK, CHAN_BLOCK)), _sds((ncb, STATE_BLOCK, CHAN_BLOCK)),
                            _sds((ncb, 1, STATE_BLOCK)), _sds((ncb, 1, STATE_BLOCK))],
                 scratch=[pltpu.VMEM((T + SUBLANES, STATE_BLOCK), F32), pltpu.VMEM((T + SUBLANES, STATE_BLOCK), F32)],
                 sem=('arbitrary',), vmem=VMEM_MOST, ride=ride)(dy_perm, u_perm, s_re, s_im, b_re, b_im, c_re, c_im,
                                                                lam_r, lam_i)


def _ffn_dact(ddn, wd4, hid4, tm):
    T = ddn.shape[0]
    nb = T // tm

    def body(d_ref, w_ref, hid_ref, o_ref, gw_ref, acc_ref):
        i = pl.program_id(1)
        d = d_ref[...]
        dact = _dot_nt(d, w_ref[...])
        silu, dsilu = _silu_parts(hid_ref[0].astype(F32))
        hid_v = hid_ref[1].astype(F32)
        o_ref[0] = (dact * hid_v * dsilu).astype(BF16)
        o_ref[1] = (dact * silu).astype(BF16)
        part = _dot_tn((silu * hid_v).astype(BF16), d)

        @pl.when(i == 0)
        def _():
            acc_ref[...] = part

        @pl.when(i > 0)
        def _():
            acc_ref[...] += part

        @pl.when(i == nb - 1)
        def _():
            gw_ref[...] = acc_ref[...].astype(BF16)

    blk = pl.BlockSpec((2, None, tm, FF_SHARD), lambda j, i: (0, j, i, 0))
    w_blk = pl.BlockSpec((None, FF_SHARD, D_MODEL), lambda j, i: (j, 0, 0))
    return _call(body, name='ffn_dact', grid=(4, nb),
                 in_specs=[pl.BlockSpec((tm, D_MODEL), lambda j, i: (i, 0)), w_blk, blk],
                 out_specs=[blk, w_blk],
                 out_shape=[_sds((2, 4, T, FF_SHARD), BF16), _sds((4, FF_SHARD, D_MODEL), BF16)],
                 scratch=[pltpu.VMEM((FF_SHARD, D_MODEL), F32)], sem=('parallel', 'arbitrary'),
                 vmem=VMEM_BIG)(ddn, wd4, hid4)


def _ffn_dup(dhid8, up8, cw8, tm, ride):
    T = up8.shape[1]
    nb = T // tm
    ha = _halo_after(tm, T, HALO16)

    def body(dh_ref, dha_ref, up_ref, cw_ref, dup_ref, dcw_ref):
        i = pl.program_id(1)

        @pl.when(i == 0)
        def _():
            dcw_ref[...] = jnp.zeros_like(dcw_ref)

        dh = dh_ref[...].astype(F32)
        dup, dh1, dh2 = _conv3_t(dh, jnp.where(i < nb - 1, dha_ref[...].astype(F32), 0.0), cw_ref)
        dup_ref[...] = dup.astype(BF16)
        up = up_ref[...].astype(F32)
        dcw_ref[0:1, :] += _colsum(dh2 * up)
        dcw_ref[1:2, :] += _colsum(dh1 * up)
        dcw_ref[2:3, :] += _colsum(dh * up)

    main = pl.BlockSpec((None, tm, FF_SHARD), lambda j, i: (j, i, 0))
    return _call(body, name='ffn_dup', grid=(N_DEV, nb),
                 in_specs=[main, pl.BlockSpec((None, HALO16, FF_SHARD), lambda j, i: (j, ha(i), 0)), main,
                           pl.BlockSpec((None, 3, FF_SHARD), lambda j, i: (j, 0, 0))],
                 out_specs=[main, pl.BlockSpec((None, 8, FF_SHARD), lambda j, i: (j, 0, 0))],
                 out_shape=[_sds((N_DEV, T, FF_SHARD), BF16), _sds((N_DEV, 8, FF_SHARD))],
                 sem=('parallel', 'arbitrary'), vmem=VMEM_BIG, ride=ride)(dhid8, dhid8, up8, cw8)


def _grad_tn(a, b, a_spec, b_spec, groups, m, n, tk, name, ride=None, parts=1):
    T = a.shape[-2]
    nk = T // tk
    mp = m // parts

    def body(a_ref, b_ref, *refs):
        o_refs, acc_ref = refs[:parts], refs[parts]
        k = pl.program_id(1)
        part = _dot_tn(a_ref[...], b_ref[...])

        @pl.when(k == 0)
        def _():
            acc_ref[...] = part

        @pl.when(k > 0)
        def _():
            acc_ref[...] += part

        @pl.when(k == nk - 1)
        def _():
            for p, o_ref in enumerate(o_refs):
                o_ref[...] = acc_ref[p * mp:(p + 1) * mp, :].astype(BF16)

    out_spec = pl.BlockSpec((None, mp, n), lambda g, k: (g, 0, 0))
    res = _call(body, name=name, grid=(groups, nk), in_specs=[a_spec, b_spec], out_specs=[out_spec] * parts,
                out_shape=[_sds((groups, mp, n), BF16)] * parts, scratch=[pltpu.VMEM((m, n), F32)],
                sem=('parallel', 'arbitrary'), vmem=VMEM_BIG, ride=ride)(a, b)
    if parts > 1:
        return res
    return res[0] if ride is None else (res[0][0], res[1])


def _grad_w_in(h1, dproj, tk, ride):
    T = h1.shape[0]
    nk = T // tk
    half = D_IN_PROJ // 2

    def body(a_ref, b_ref, o_ref, acc_ref):
        k = pl.program_id(0)
        for h in range(2):
            cols = slice(h * half, (h + 1) * half)
            part = _dot_tn(a_ref[...], b_ref[:, cols])

            @pl.when(k == 0)
            def _():
                acc_ref[:, cols] = part

            @pl.when(k > 0)
            def _():
                acc_ref[:, cols] += part

        @pl.when(k == nk - 1)
        def _():
            for g in range(N_DEV):
                o_ref[g] = acc_ref[:, g * IN_SHARD:(g + 1) * IN_SHARD].astype(BF16)

    return _call(body, name='grad_w_in', grid=(nk,),
                 in_specs=[pl.BlockSpec((tk, D_MODEL), lambda k: (k, 0)), pl.BlockSpec((tk, D_IN_PROJ), lambda k: (k, 0))],
                 out_specs=_const((N_DEV, D_MODEL, IN_SHARD)), out_shape=_sds((N_DEV, D_MODEL, IN_SHARD), BF16),
                 scratch=[pltpu.VMEM((D_MODEL, D_IN_PROJ), F32)], sem=('arbitrary',), vmem=VMEM_BIG, ride=ride)(h1, dproj)


def _pre_norm_bwd(dz, dz_spec, w_parts, xin, dres, sc, g, tm, name, ride, below=None, group=1, w_t=False):
    T = xin.shape[0]
    n = w_parts[0].shape[1] if w_t else w_parts[0].shape[2]
    mul = _dot if w_t else _dot_nt
    steps = N_DEV // group
    width = D_MODEL // len(w_parts)

    def body(dz_ref, *refs):
        w_refs, (x_ref, dr_ref, sc_ref, g_ref), refs = refs[:len(w_parts)], refs[len(w_parts):len(w_parts) + 4], \
            refs[len(w_parts) + 4:]
        if below is None:
            dx_ref, dsh_ref, dsc_ref, dg_ref = refs
            sums = (dsh_ref, dsc_ref, dg_ref)
        else:
            v_ref, gate_ref, g2_ref, dx_ref, dsh_ref, dsc_ref, dg_ref, dv_ref, dgate_ref, dg2_ref = refs
            sums = (dsh_ref, dsc_ref, dg_ref, dgate_ref, dg2_ref)
        i, j = pl.program_id(0), pl.program_id(1)
        piece = (lambda s: dz_ref[s]) if dz.ndim == 3 else (lambda s: dz_ref[:, s * n:(s + 1) * n])
        parts = []
        for w_ref in w_refs:
            part = mul(piece(0), w_ref[0])
            for s in range(1, group):
                part = part + mul(piece(s), w_ref[s])
            parts.append(part)

        @pl.when(jnp.logical_and(i == 0, j == 0))
        def _():
            for s_ref in sums:
                s_ref[...] = jnp.zeros_like(s_ref)

        @pl.when(j == 0)
        def _():
            for k, part in enumerate(parts):
                dx_ref[:, k * width:(k + 1) * width] = part

        @pl.when(j > 0)
        def _():
            for k, part in enumerate(parts):
                dx_ref[:, k * width:(k + 1) * width] += part

        @pl.when(j == steps - 1)
        def _():
            dh, xv, gv = dx_ref[...], x_ref[...], g_ref[...]
            r = _rsqrt_mean(xv)
            dsh_ref[...] += _colsum(dh)
            dsc_ref[...] += _colsum(dh * (xv * r * gv))
            dxn = dh * (1.0 + sc_ref[...])
            dg_ref[...] += _colsum(dxn * xv * r)
            dx = dr_ref[...] + _norm_bwd(dxn, xv, r, gv)
            dx_ref[...] = dx
            if below is not None:
                v, g2 = v_ref[...], g2_ref[...]
                rv = _rsqrt_mean(v)
                dgate_ref[...] += _colsum(dx * (v * rv * g2))
                dn = dx * gate_ref[...]
                dg2_ref[...] += _colsum(dn * v * rv)
                dv_ref[...] = _norm_bwd(dn, v, rv, g2).astype(BF16)

    row = pl.BlockSpec((tm, D_MODEL), lambda i, j: (i, 0))
    vec = _const((1, D_MODEL))
    in_specs = [dz_spec] + [pl.BlockSpec((group,) + w.shape[1:], lambda i, j: (j, 0, 0)) for w in w_parts]
    in_specs += [row, row, vec, vec]
    out_specs = [row, vec, vec, vec]
    out_shape = [_sds((T, D_MODEL)), _sds((1, D_MODEL)), _sds((1, D_MODEL)), _sds((1, D_MODEL))]
    args = [dz, *w_parts, xin, dres, sc, g]
    if below is not None:
        in_specs += [row, vec, vec]
        out_specs += [row, vec, vec]
        out_shape += [_sds((T, D_MODEL), BF16), _sds((1, D_MODEL)), _sds((1, D_MODEL))]
        args += list(below)
    return _call(body, name=name, grid=(T // tm, steps), in_specs=in_specs, out_specs=out_specs,
                 out_shape=out_shape, sem=('arbitrary', 'arbitrary'), vmem=VMEM_MOST, ride=ride)(*args)


def _mix_bwd(d_o, w_out, yssm, proj, d, glu_w, glu_b, g_ssm, cw, g_conv, avg16, avg64, tm, ride):
    T = yssm.shape[0]
    hb = _halo_before(tm)

    def body(do_ref, wo_ref, y_ref, p_ref, ph_ref, d_ref, gw_ref, gb_ref, gs_ref, cw_ref, gc_ref, a16_ref, a64_ref,
             dy_ref, dconv_ref, dbg_ref, z_ref, dlin_ref, acc_ref):
        i = pl.program_id(0)
        dyc = _dot_nt(do_ref[...], wo_ref[...])

        @pl.when(i == 0)
        def _():
            acc_ref[...] = jnp.zeros_like(acc_ref)

        u = p_ref[:, 0:D_SSM]
        y = y_ref[...] + d_ref[...] * u
        z, t = _gelu(y)
        gate = _sigmoid(_dot(z.astype(BF16), gw_ref[...]) + gb_ref[...])
        ya = z * gate
        rs = lax.rsqrt(_dot_split(ya * ya, a16_ref[...], 2) + EPS)
        dna = dyc[:, 0:D_SSM]
        acc_ref[1:2, :] += _colsum(dna * ya * rs)
        dya = _head_norm_bwd(dna, ya, rs, gs_ref[...], a16_ref[...])
        dlin = dya * z * gate * (1.0 - gate)
        acc_ref[0:1, :] += _colsum(dlin)
        dlin_b = dlin.astype(BF16)
        dz = dya * gate + _dot_nt(dlin_b, gw_ref[...])
        dy = dz * _gelu_grad(y, t)
        acc_ref[3:4, :] += _colsum(dy * u)
        dy_ref[...] = dy
        z_ref[...] = z.astype(BF16)
        dlin_ref[...] = dlin_b

        bg = p_ref[:, D_SSM:D_SSM + D_CONV]
        cv = p_ref[:, D_SSM + D_CONV:D_SSM + 2 * D_CONV] * p_ref[:, D_SSM + 2 * D_CONV:D_IN_PROJ]
        hv = ph_ref[:, D_SSM + D_CONV:D_SSM + 2 * D_CONV] * ph_ref[:, D_SSM + 2 * D_CONV:D_IN_PROJ]
        hv = jnp.where(i > 0, hv, 0.0)
        conv, cv1, cv2 = _conv3(cv, hv, cw_ref)
        yb = bg * conv
        rsb = lax.rsqrt(_dot_split(yb * yb, a64_ref[...], 2) + EPS)
        dnb = dyc[:, D_SSM:D_MODEL]
        acc_ref[2:3, :] += _colsum(dnb * yb * rsb)
        dyb = _head_norm_bwd(dnb, yb, rsb, gc_ref[...], a64_ref[...])
        dbg_ref[...] = dyb * conv
        dconv = dyb * bg
        dconv_ref[...] = dconv
        acc_ref[4:5, :] += _colsum(dconv * cv2)
        acc_ref[5:6, :] += _colsum(dconv * cv1)
        acc_ref[6:7, :] += _colsum(dconv * cv)

    vec = _const((1, D_SSM))
    sq = _const((D_SSM, D_SSM))
    half = pl.BlockSpec((tm, D_SSM), lambda i: (i, 0))
    return _call(body, name='mix_bwd', grid=(T // tm,),
                 in_specs=[pl.BlockSpec((tm, D_MODEL), lambda i: (i, 0)), _const((D_MODEL, D_MODEL)), half,
                           pl.BlockSpec((tm, D_IN_PROJ), lambda i: (i, 0)),
                           pl.BlockSpec((HALO, D_IN_PROJ), lambda i: (hb(i), 0)), vec, sq, vec, vec,
                           _const((3, D_CONV)), vec, sq, sq],
                 out_specs=[half, half, half, half, half, _const((8, D_SSM))],
                 out_shape=[_sds((T, D_SSM)), _sds((T, D_SSM)), _sds((T, D_SSM)), _sds((T, D_SSM), BF16),
                            _sds((T, D_SSM), BF16), _sds((8, D_SSM))],
                 sem=('arbitrary',), vmem=VMEM_BIG, ride=ride)(d_o, w_out, yssm, proj, proj, d, glu_w, glu_b, g_ssm, cw,
                                                              g_conv, avg16, avg64)


def _mix_bwd_proj(dconv, proj, du_ssm, dy, d, dbg, cw, tm):
    T = dy.shape[0]
    nb = T // tm
    ha = _halo_after(tm, T)

    def body(dc_ref, dch_ref, cg_ref, v_ref, du_ref, dy_ref, d_ref, dbg_ref, cw_ref, o_ref):
        i = pl.program_id(0)
        dcv = _conv3_t(dc_ref[...], jnp.where(i < nb - 1, dch_ref[...], 0.0), cw_ref)[0]
        o_ref[:, 0:D_SSM] = (du_ref[...] + dy_ref[...] * d_ref[...]).astype(BF16)
        o_ref[:, D_SSM:D_SSM + D_CONV] = dbg_ref[...].astype(BF16)
        o_ref[:, D_SSM + D_CONV:D_SSM + 2 * D_CONV] = (dcv * v_ref[...]).astype(BF16)
        o_ref[:, D_SSM + 2 * D_CONV:D_IN_PROJ] = (dcv * cg_ref[...]).astype(BF16)

    half = pl.BlockSpec((tm, D_SSM), lambda i: (i, 0))
    return _call(body, name='mix_bwd_proj', grid=(nb,),
                 in_specs=[half, pl.BlockSpec((HALO, D_CONV), lambda i: (ha(i), 0)),
                           pl.BlockSpec((tm, D_CONV), lambda i: (i, 2)), pl.BlockSpec((tm, D_CONV), lambda i: (i, 3)),
                           half, half, _const((1, D_SSM)), half, _const((3, D_CONV))],
                 out_specs=pl.BlockSpec((tm, D_IN_PROJ), lambda i: (i, 0)), out_shape=_sds((T, D_IN_PROJ), BF16),
                 sem=('parallel',), vmem=VMEM_BIG)(dconv, dconv, proj, proj, du_ssm, dy, d, dbg, cw)


ADAMW_SLOT_BYTES = 8 << 20
ADAMW_ROW_BYTES = 3 << 19


def _row_tile(rows, cols, slots):
    for cand in range(rows, 15, -1):
        if (rows % cand == 0 and cand % 16 == 0 and slots * cand * cols * 4 <= ADAMW_SLOT_BYTES
                and cand * cols * 4 <= ADAMW_ROW_BYTES):
            return cand
    return rows


def _adamw_math(g, w, m, v):
    m2 = ADAM_B1 * m + (1.0 - ADAM_B1) * g
    v2 = ADAM_B2 * v + (1.0 - ADAM_B2) * (g * g)
    m_hat = m2 / (1.0 - ADAM_B1 ** ADAM_STEP)
    v_hat = v2 / (1.0 - ADAM_B2 ** ADAM_STEP)
    return -ADAM_LR * (m_hat / (jnp.sqrt(v_hat) + ADAM_EPS) + ADAM_WD * w), m2, v2


def _adamw(pieces, w, m, v, name):
    slots, _, cols = pieces[0].shape
    rows = sum(p.shape[1] for p in pieces)
    tr = _row_tile(pieces[0].shape[1], cols, slots)
    starts, pos = [], 0
    for p in pieces:
        assert p.shape[1] % tr == 0
        starts.append(pos)
        pos += p.shape[1] // tr

    def body(*refs):
        g_refs = refs[:len(pieces)]
        w_ref, m_ref, v_ref, go_ref, d_ref, mo_ref, vo_ref = refs[len(pieces):]
        i = pl.program_id(0)
        g = None
        for g_ref, start in zip(g_refs, starts):
            part = g_ref[0].astype(F32)
            for s in range(1, slots):
                part = part + g_ref[s].astype(F32)
            g = part if g is None else jnp.where(i >= start, part, g)
        go_ref[...] = g
        d_ref[...], mo_ref[...], vo_ref[...] = _adamw_math(g, w_ref[...], m_ref[...], v_ref[...])

    def piece_spec(start, count):
        return pl.BlockSpec((slots, tr, cols), lambda i: (0, jnp.clip(i - start, 0, count - 1), 0))

    blk = pl.BlockSpec((tr, cols), lambda i: (i, 0))
    return _call(body, name=name, grid=(rows // tr,),
                 in_specs=[piece_spec(s, p.shape[1] // tr) for s, p in zip(starts, pieces)] + [blk, blk, blk],
                 out_specs=[blk] * 4, out_shape=[_sds((rows, cols))] * 4, sem=('parallel',),
                 vmem=VMEM_BIG)(*pieces, w, m, v)


def _to_scan_rows(a):
    T, n = a.shape
    return a.reshape(SUBLANES, T // SUBLANES, n).transpose(1, 0, 2).reshape(T, n)


def _from_scan_rows(a):
    T, n = a.shape
    return a.reshape(T // SUBLANES, SUBLANES, n).transpose(1, 0, 2).reshape(T, n)


def _expand(a):
    return jnp.repeat(a, SSM_GROUP, axis=1)


def _block_diag(rows, row_group, col_group):
    r, n = rows.shape
    tiled = jnp.tile(rows, (1, N_GROUPS))
    keep = (jnp.arange(r)[:, None] // row_group) == (jnp.arange(n * N_GROUPS)[None, :] // col_group)
    return jnp.where(keep, tiled, 0.0)


def _block_diag_b(bb):
    return _block_diag(bb.transpose(0, 2, 1).reshape(D_SSM, SSM_STATE), SSM_GROUP, SSM_STATE)


def _block_diag_c(cc):
    return _block_diag(cc.transpose(0, 2, 1).reshape(N_STATE, SSM_GROUP), SSM_STATE, SSM_GROUP)


def _diag_blocks(x, chan_major):
    per = CHAN_BLOCK // SSM_GROUP
    eye = jnp.eye(per, dtype=x.dtype)
    if chan_major:
        x = x.reshape(-1, per, SSM_GROUP, per, SSM_STATE) * eye[None, :, None, :, None]
        return x.sum(axis=1).transpose(0, 2, 3, 1).reshape(N_GROUPS, SSM_STATE, SSM_GROUP)
    x = x.reshape(-1, per, SSM_STATE, per, SSM_GROUP) * eye[None, :, None, :, None]
    return x.sum(axis=3).reshape(N_GROUPS, SSM_STATE, SSM_GROUP)


SMALL_LAYOUT = {
    'ssm_b_re': (0, 0, 32, 1024), 'ssm_b_im': (32, 0, 32, 1024), 'ssm_c_re': (64, 0, 32, 1024),
    'ssm_c_im': (96, 0, 32, 1024), 'b_ada': (128, 0, 6, 1024), 'g_pre_mix': (134, 0, 1, 1024),
    'g_post_mix': (135, 0, 1, 1024), 'ssm_lam_re': (136, 0, 2, 1024), 'ssm_lam_im': (138, 0, 2, 1024),
    'ssm_log_step': (140, 0, 1, 32), 'glu_b': (141, 0, 1, 512), 'g_out_ssm': (141, 512, 1, 512),
    'g_out_conv': (142, 0, 1, 512), 'ssm_d': (142, 512, 1, 512), 'g_pre_ffn': (143, 0, 1, 1024),
    'g_post_ffn': (144, 0, 1, 1024)}
SMALL_ROWS = 152
B_ADA_ROW = SMALL_LAYOUT['b_ada'][0]
LATE_ROWS = {('b_ada', 0): 0, ('b_ada', 1): 1, ('g_pre_mix', 0): 2}


def _adamw_small(gathered, late, wts, mom_m, mom_v):
    names = list(SMALL_LAYOUT)
    n = len(names)

    def body(*refs):
        g_ref, late_ref, ins, outs = refs[0], refs[1], refs[2:2 + 3 * n], refs[2 + 3 * n:]
        for p, name in enumerate(names):
            r0, c0, rows, cols = SMALL_LAYOUT[name]
            pieces = [(0, rows)] if rows % 8 == 0 else [(r, 1) for r in range(rows)]
            for r, cnt in pieces:
                src_ref, first = (late_ref, LATE_ROWS[name, r]) if (name, r) in LATE_ROWS else (g_ref, r0 + r)
                g = src_ref[0, first:first + cnt, c0:c0 + cols]
                for s in range(1, N_DEV):
                    g = g + src_ref[s, first:first + cnt, c0:c0 + cols]
                w, m, v = (ins[3 * p + q][r:r + cnt, :] for q in range(3))
                res = (g,) + _adamw_math(g, w, m, v)
                for q in range(4):
                    outs[4 * p + q][r:r + cnt, :] = res[q]

    shapes = [SMALL_LAYOUT[name][2:] for name in names]
    args = [gathered, late]
    for name, shp in zip(names, shapes):
        args += [wts[name].reshape(shp), mom_m[name].reshape(shp), mom_v[name].reshape(shp)]
    outs = _call(body, name='adamw_small', grid=(1,),
                 in_specs=[_const(gathered.shape), _const(late.shape)]
                 + [_const(shp) for shp in shapes for _ in range(3)],
                 out_specs=[_const(shp) for shp in shapes for _ in range(4)],
                 out_shape=[_sds(shp) for shp in shapes for _ in range(4)], vmem=VMEM_BIG)(*args)
    res = {}
    for p, name in enumerate(names):
        for q, kind in enumerate(('g', 'd', 'm', 'v')):
            res[kind, name] = outs[4 * p + q].reshape(wts[name].shape)
    return res


def kernel(x, c, w_ada, b_ada, g_pre_mix, g_post_mix, w_in, ssm_lam_re, ssm_lam_im, ssm_log_step, ssm_b_re, ssm_b_im, ssm_c_re, ssm_c_im, ssm_d, glu_w, glu_b, g_out_ssm, conv_w, g_out_conv, w_out, g_pre_ffn, g_post_ffn, w_up, ffn_conv_w, w_down, loss_target, m_w_ada, m_b_ada, m_g_pre_mix, m_g_post_mix, m_w_in, m_ssm_lam_re, m_ssm_lam_im, m_ssm_log_step, m_ssm_b_re, m_ssm_b_im, m_ssm_c_re, m_ssm_c_im, m_ssm_d, m_glu_w, m_glu_b, m_g_out_ssm, m_conv_w, m_g_out_conv, m_w_out, m_g_pre_ffn, m_g_post_ffn, m_w_up, m_ffn_conv_w, m_w_down, v_w_ada, v_b_ada, v_g_pre_mix, v_g_post_mix, v_w_in, v_ssm_lam_re, v_ssm_lam_im, v_ssm_log_step, v_ssm_b_re, v_ssm_b_im, v_ssm_c_re, v_ssm_c_im, v_ssm_d, v_glu_w, v_glu_b, v_g_out_ssm, v_conv_w, v_g_out_conv, v_w_out, v_g_pre_ffn, v_g_post_ffn, v_w_up, v_ffn_conv_w, v_w_down):
    args = dict(locals())
    wts = {n: args[n] for n in WEIGHTS}
    mom_m = {n: args['m_' + n] for n in WEIGHTS}
    mom_v = {n: args['v_' + n] for n in WEIGHTS}
    T = x.shape[1]
    tm = min(512, T)
    tw = min(1024, T)
    tk = min(2048, T)
    me = _me()[3]
    xt, tgt = x[0], loss_target[0]

    c_all, w_in_s = _exchange([c, w_in[0].astype(BF16)], name='gather_first', scatter=False)
    c_all = c_all.reshape(N_DEV, D_MODEL)
    b_cols = lax.dynamic_slice(b_ada, (0, me * ADA_SHARD), (1, ADA_SHARD))
    mod_cols, c_act = _mod_cols(c_all, w_ada[0], b_cols)
    (mod_all,) = _exchange([mod_cols], name='gather_mod', scatter=False)
    mod = lax.dynamic_slice(mod_all, (0, me, 0), (N_DEV, 1, ADA_SHARD)).reshape(N_MOD, 1, D_MODEL)
    sh1, sc1, gt1, sh2, sc2, gt2 = [mod[k] for k in range(N_MOD)]


    lre_x, lim_x = _expand(ssm_lam_re[0]), _expand(ssm_lam_im[0])
    lst_x = jnp.broadcast_to(ssm_log_step[0][:, None], (N_GROUPS, SSM_STATE * SSM_GROUP))
    b_re_x = ssm_b_re[0].reshape(N_GROUPS, -1)
    b_im_x = ssm_b_im[0].reshape(N_GROUPS, -1)
    ar_x, ai_x, bbr_x, bbi_x = _ssm_prep(lre_x, lim_x, lst_x, b_re_x, b_im_x)
    lam_r = ar_x[:, ::SSM_GROUP].reshape(1, N_STATE)
    lam_i = ai_x[:, ::SSM_GROUP].reshape(1, N_STATE)
    big_b_re = _block_diag_b(bbr_x.reshape(N_GROUPS, SSM_STATE, SSM_GROUP)).astype(BF16)
    big_b_im = _block_diag_b(bbi_x.reshape(N_GROUPS, SSM_STATE, SSM_GROUP)).astype(BF16)
    big_c_re = _block_diag_c(ssm_c_re[0]).astype(BF16)
    big_c_im = _block_diag_c(ssm_c_im[0]).astype(BF16)
    head = jnp.arange(D_SSM)
    avg16 = jnp.where(head[:, None] // SSM_GROUP == head[None, :] // SSM_GROUP, 1.0 / SSM_GROUP, 0.0).astype(BF16)
    hd = D_CONV // CONV_HEADS
    avg64 = jnp.where(head[:, None] // hd == head[None, :] // hd, 1.0 / hd, 0.0).astype(BF16)

    w_up_t, half = w_up[0].T, D_MODEL // 2
    (proj, h1), (ffn_conv_s, conv_s, w_up_a) = _pre_mix(
        xt, sc1, sh1, g_pre_mix, w_in_s, tw, ([ffn_conv_w[0], conv_w[0], w_up_t[:, :half].astype(BF16)], False))
    cw_full = conv_s.transpose(1, 0, 2).reshape(3, D_CONV)
    u_perm = _to_scan_rows(proj[:, :D_SSM])
    (s_re, s_im, y_perm), (w_up_b, glu_s, w_out_s) = _ssm_fwd(
        u_perm, big_b_re, big_b_im, big_c_re, big_c_im, lam_r, lam_i,
        ([w_up_t[:, half:].astype(BF16), glu_w[0].astype(BF16), w_out[0].astype(BF16)], False))
    glu_full = glu_s.reshape(D_SSM, D_SSM)
    w_out_full = w_out_s.reshape(D_MODEL, D_MODEL)
    yssm = _from_scan_rows(y_perm)
    mix_args = (ssm_d, glu_full, glu_b, g_out_ssm, cw_full, g_out_conv, avg16, avg64)
    ycat = _mix_fwd(yssm, proj, *mix_args, tw)
    o, x1, h2 = _out_proj(ycat, w_out_full, xt, gt1, g_post_mix, g_pre_ffn, sc2, sh2, tw)
    (up8, hid8), (w_down_s,) = _ffn_up(h2, w_up_a, w_up_b, ffn_conv_s, tw, ([w_down[0].astype(BF16)], False))
    wd4 = w_down_s.reshape(4, FF_SHARD, D_MODEL)
    hid4 = hid8.reshape(2, 4, T, FF_SHARD)
    ddn, dx2, loss_parts, d_gt2, d_g_post_ffn = _ffn_down(hid4, wd4, x1, tgt, gt2, g_post_ffn, tm)
    loss_local = jnp.sum(loss_parts[:, 0, 0])

    got = {}
    dhid, g_w_down = _ffn_dact(ddn, wd4, hid4, tw)
    (dup8, dcw_ffn), (got['w_down'],) = _ffn_dup(dhid.reshape(N_DEV, T, FF_SHARD), up8, ffn_conv_s, tw,
                                                 ([g_w_down.reshape(N_DEV, D_FF // N_DEV, D_MODEL)], True))
    g_w_up_halves = _grad_tn(dup8, h2, pl.BlockSpec((None, tk, FF_SHARD), lambda g, k: (g, k, 0)),
                             pl.BlockSpec((tk, D_MODEL), lambda g, k: (k, 0)), N_DEV, FF_SHARD, D_MODEL, tk,
                             'grad_w_up', parts=2)
    (dx1, d_sh2, d_sc2, d_g_pre_ffn, d_o, d_gt1, d_g_post_mix), (got_up_0, got['ffn_conv_w']) = _pre_norm_bwd(
        dup8, pl.BlockSpec((2, tw, FF_SHARD), lambda i, j: (j, i, 0)), [w_up_a, w_up_b], x1, dx2, sc2, g_pre_ffn, tw,
        'ffn_in_bwd', ([g_w_up_halves[0], dcw_ffn], True), below=(o, gt1, g_post_mix), group=2, w_t=True)

    g_w_out = _grad_tn(ycat, d_o, pl.BlockSpec((tk, D_MODEL), lambda g, k: (k, 0)),
                       pl.BlockSpec((tk, D_MODEL), lambda g, k: (k, 0)), 1, D_MODEL, D_MODEL, tk, 'grad_w_out')
    (dy, dconv, dbg, z_b, dlin_b, sums), (got_up_1,) = _mix_bwd(
        d_o, w_out_full, yssm, proj, *mix_args, tm, ([g_w_up_halves[1]], True))
    g_glu_w = _grad_tn(z_b, dlin_b, pl.BlockSpec((tk, D_SSM), lambda g, k: (k, 0)),
                       pl.BlockSpec((tk, D_SSM), lambda g, k: (k, 0)), 1, D_SSM, D_SSM, tk, 'grad_glu_w')
    dy_perm = _to_scan_rows(dy)
    (du_perm, dbr_blk, dbi_blk, dcr_blk, dci_blk, dar_blk, dai_blk), (got['w_out'], got['glu_w']) = _ssm_bwd(
        dy_perm, u_perm, s_re, s_im, big_b_re, big_b_im, big_c_re, big_c_im, lam_r, lam_i,
        ([g_w_out.reshape(N_DEV, D_MODEL // N_DEV, D_MODEL), g_glu_w.reshape(N_DEV, D_SSM // N_DEV, D_SSM)], True))
    du_ssm = _from_scan_rows(du_perm)
    dproj = _mix_bwd_proj(dconv, proj, du_ssm, dy, ssm_d, dbg, cw_full, tw)
    dbb_re = _diag_blocks(dbr_blk, True).reshape(N_GROUPS, -1)
    dbb_im = _diag_blocks(dbi_blk, True).reshape(N_GROUPS, -1)
    d_c_re = _diag_blocks(dcr_blk, False).transpose(0, 2, 1)
    d_c_im = _diag_blocks(dci_blk, False).transpose(0, 2, 1)
    lane = jnp.arange(SSM_STATE * SSM_GROUP)
    seg = jnp.where(lane[:, None] // SSM_GROUP == lane[None, :] // SSM_GROUP, 1.0, 0.0).astype(BF16)
    d_b_re_x, d_b_im_x, d_lre_x, d_lim_x, d_lst = _ssm_prep_bwd(
        lre_x, lim_x, lst_x, b_re_x, b_im_x, dbb_re, dbb_im, _expand(dar_blk.reshape(N_GROUPS, SSM_STATE)),
        _expand(dai_blk.reshape(N_GROUPS, SSM_STATE)), seg)

    row = lambda a: a.reshape(-1, PACK_COLS)
    blank = jnp.zeros((1, PACK_COLS), F32)
    small_pack = jnp.concatenate([
        d_b_re_x, d_b_im_x, row(d_c_re), row(d_c_im), blank, blank, d_gt1, d_sh2, d_sc2, d_gt2, blank,
        d_g_post_mix, row(d_lre_x[:, ::SSM_GROUP]), row(d_lim_x[:, ::SSM_GROUP]),
        jnp.pad(d_lst.reshape(1, N_GROUPS), ((0, 0), (0, PACK_COLS - N_GROUPS))), row(sums[0:4]), d_g_pre_ffn,
        d_g_post_ffn, jnp.zeros((SMALL_ROWS - 145, PACK_COLS), F32)])
    g_w_in, (small_all,) = _grad_w_in(h1, dproj, tk, ([small_pack], False))
    g_conv_slots = jnp.concatenate([sums[4:7], jnp.zeros((5, D_CONV), F32)]).reshape(
        8, N_DEV, D_CONV // N_DEV).transpose(1, 0, 2)
    (grad_x, d_sh1, d_sc1, d_g_pre_mix), (got['w_in'], got['conv_w']) = _pre_norm_bwd(
        dproj, pl.BlockSpec((tw, 4 * IN_SHARD), lambda i, j: (i, j)), [w_in_s], xt, dx1, sc1, g_pre_mix, tw,
        'mix_in_bwd', ([g_w_in, g_conv_slots], True), group=4)
    late_pack = jnp.concatenate([d_sh1, d_sc1, d_g_pre_mix, jnp.full((1, PACK_COLS), loss_local, F32),
                                 jnp.zeros((4, PACK_COLS), F32)])
    (late_all,) = _exchange([late_pack], name='gather_late_grads', scatter=False)
    loss = jnp.sum(late_all[:, 3, 0])
    res = _adamw_small(small_all, late_all, wts, mom_m, mom_v)

    dmod_all = jnp.concatenate([late_all[:, 0:2, :], small_all[:, B_ADA_ROW + 2:B_ADA_ROW + N_MOD, :]],
                               axis=1).reshape(N_DEV, N_MOD * D_MODEL)
    dmod_cols = lax.dynamic_slice(dmod_all, (0, me * ADA_SHARD), (N_DEV, ADA_SHARD))
    g_w_ada = _grad_w_ada(c_act.T, dmod_cols)

    pieces = {n: [slots[:, :3, :] if n in ('conv_w', 'ffn_conv_w') else slots] for n, slots in got.items()}
    for n, parts in pieces.items():
        outs = _adamw(parts, wts[n][0], mom_m[n][0], mom_v[n][0], 'adamw_' + n)
        for kind, val in zip(('g', 'd', 'm', 'v'), outs):
            res[kind, n] = val[None]
    outs = _adamw([got_up_0, got_up_1], w_up[0].T, m_w_up[0].T, v_w_up[0].T, 'adamw_w_up')
    for kind, val in zip(('g', 'd', 'm', 'v'), outs):
        res[kind, 'w_up'] = val.T[None]
    outs = _adamw([g_w_ada[None]], w_ada[0], m_w_ada[0], v_w_ada[0], 'adamw_w_ada')
    for kind, val in zip(('g', 'd', 'm', 'v'), outs):
        res[kind, 'w_ada'] = val[None]

    return (loss, grad_x[None], *[res['g', n] for n in WEIGHTS], *[res['d', n] for n in WEIGHTS],
            *[res['m', n] for n in WEIGHTS], *[res['v', n] for n in WEIGHTS])
```

```python
import math

import jax
import jax.numpy as jnp
from jax import lax
from jax.experimental import pallas as pl
from jax.experimental.pallas import tpu as pltpu

F32, BF16 = jnp.float32, jnp.bfloat16

D_MODEL = 1024
D_SSM = 512
D_CONV = 512
SSM_GROUP = 16
N_GROUPS = 32
SSM_STATE = 64
N_STATE = N_GROUPS * SSM_STATE
CONV_HEADS = 8
D_FF = 2816
N_MOD = 6
D_IN_PROJ = D_SSM + 3 * D_CONV
N_DEV = 8
FF_SHARD = 2 * D_FF // N_DEV
IN_SHARD = D_IN_PROJ // N_DEV
ADA_SHARD = N_MOD * D_MODEL // N_DEV
EPS = 1e-6
LAMBDA_RE_MAX = -1e-4
ADAM_LR, ADAM_B1, ADAM_B2, ADAM_EPS, ADAM_WD, ADAM_STEP = 0.001, 0.9, 0.999, 1e-08, 0.01, 10
GELU_C = math.sqrt(2.0 / math.pi)
GELU_A = 0.044715

SUBLANES = 8
HALO = 8
HALO16 = 16
SCAN_UNROLL = 16
STATE_BLOCK = 512
CHAN_BLOCK = 128
VMEM_BIG = 48 << 20
VMEM_MOST = 58 << 20

WEIGHTS = ['w_ada', 'b_ada', 'g_pre_mix', 'g_post_mix', 'w_in', 'ssm_lam_re', 'ssm_lam_im', 'ssm_log_step',
           'ssm_b_re', 'ssm_b_im', 'ssm_c_re', 'ssm_c_im', 'ssm_d', 'glu_w', 'glu_b', 'g_out_ssm', 'conv_w',
           'g_out_conv', 'w_out', 'g_pre_ffn', 'g_post_ffn', 'w_up', 'ffn_conv_w', 'w_down']
PACK_COLS = 1024


def _call(body, *, name, grid, in_specs, out_specs, out_shape, scratch=(), sem=None, vmem=None, ride=None):
    params = {}
    if vmem is not None:
        params['vmem_limit_bytes'] = vmem
    if ride is None:
        if sem is not None:
            params['dimension_semantics'] = sem
        return pl.pallas_call(body, name=name, grid=grid, in_specs=in_specs, out_specs=out_specs,
                              out_shape=out_shape, scratch_shapes=list(scratch),
                              compiler_params=pltpu.CompilerParams(**params))
    arrs, scatter = ride
    single = not isinstance(out_shape, (list, tuple))
    out_shape_l = [out_shape] if single else list(out_shape)
    out_specs_l = [out_specs] if single else list(out_specs)
    n, n_in, n_out, n_scr = len(arrs), len(in_specs), len(out_shape_l), len(scratch)
    any_spec = pl.BlockSpec(memory_space=pl.ANY)
    params['dimension_semantics'] = ('arbitrary',) * len(grid)

    def carried(*refs):
        ins, rin = refs[:n_in], refs[n_in:n_in + n]
        outs, rout = refs[n_in + n:n_in + n + n_out], refs[n_in + n + n_out:n_in + 2 * n + n_out]
        scr, sems = refs[n_in + 2 * n + n_out:n_in + 2 * n + n_out + n_scr], refs[n_in + 2 * n + n_out + n_scr:]
        first = pl.program_id(0) == 0
        last = pl.program_id(0) == grid[0] - 1
        for ax in range(1, len(grid)):
            first = jnp.logical_and(first, pl.program_id(ax) == 0)
            last = jnp.logical_and(last, pl.program_id(ax) == grid[ax] - 1)

        @pl.when(first)
        def _():
            _exchange_start(rin, rout, sems, scatter)

        body(*ins, *outs, *scr)

        @pl.when(last)
        def _():
            _exchange_wait(rin, rout, sems, scatter)

    call = pl.pallas_call(carried, name=name, grid=grid, in_specs=list(in_specs) + [any_spec] * n,
                          out_specs=out_specs_l + [any_spec] * n,
                          out_shape=out_shape_l + _exchange_shapes(arrs, scatter),
                          scratch_shapes=list(scratch) + _exchange_sems(n),
                          compiler_params=pltpu.CompilerParams(**params))

    def run(*args):
        res = call(*args, *arrs)
        own = res[0] if single else list(res[:n_out])
        return own, list(res[n_out:])

    return run


def _const(shape):
    nd = len(shape)
    return pl.BlockSpec(shape, lambda *_: (0,) * nd)


def _sds(shape, dtype=F32):
    return jax.ShapeDtypeStruct(shape, dtype)


def _dot(a, b):
    return jnp.dot(a, b, preferred_element_type=F32)


def _dot_nt(a, b):
    return lax.dot_general(a, b, (((1,), (1,)), ((), ())), preferred_element_type=F32)


def _dot_tn(a, b):
    return lax.dot_general(a, b, (((0,), (0,)), ((), ())), preferred_element_type=F32)


def _dot_split(x, mat, parts):
    acc = None
    rem = x
    for _ in range(parts):
        piece = rem.astype(BF16)
        rem = rem - piece.astype(F32)
        term = _dot(piece, mat)
        acc = term if acc is None else acc + term
    return acc


def _sigmoid(x):
    return 1.0 / (1.0 + jnp.exp(-x))


def _gelu(x):
    t = jnp.tanh(GELU_C * (x + GELU_A * x * x * x))
    return 0.5 * x * (1.0 + t), t


def _gelu_grad(x, t):
    return 0.5 * (1.0 + t) + 0.5 * x * (1.0 - t * t) * GELU_C * (1.0 + 3.0 * GELU_A * x * x)


def _rsqrt_mean(x):
    return lax.rsqrt(jnp.mean(x * x, axis=-1, keepdims=True) + EPS)


def _colsum(x):
    return jnp.sum(x, axis=0, keepdims=True)


def _shifts_down(x, halo):
    ext = jnp.concatenate([halo, x], axis=0)
    return pltpu.roll(ext, 1, 0)[halo.shape[0]:], pltpu.roll(ext, 2, 0)[halo.shape[0]:]


def _shifts_up(x, halo):
    n = x.shape[0]
    ext = jnp.concatenate([x, halo], axis=0)
    total = ext.shape[0]
    return pltpu.roll(ext, total - 1, 0)[:n], pltpu.roll(ext, total - 2, 0)[:n]


def _conv3(x, halo, w_ref):
    x1, x2 = _shifts_down(x, halo)
    return w_ref[0:1, :] * x2 + w_ref[1:2, :] * x1 + w_ref[2:3, :] * x, x1, x2


def _conv3_t(g, halo, w_ref):
    g1, g2 = _shifts_up(g, halo)
    return w_ref[2:3, :] * g + w_ref[1:2, :] * g1 + w_ref[0:1, :] * g2, g1, g2


def _silu_parts(x):
    s = _sigmoid(x)
    return x * s, s * (1.0 + x * (1.0 - s))


def _norm_bwd(dn, x, r, g):
    gd = g * dn
    return r * gd - x * (r * r * r) * jnp.mean(gd * x, axis=-1, keepdims=True)


def _head_norm_bwd(dn, y, rs, g, avg):
    gd = g * dn
    return rs * gd - y * (rs * rs * rs) * _dot_split(gd * y, avg, 2)


def _me():
    x, y, c = lax.axis_index('x'), lax.axis_index('y'), lax.axis_index('c')
    return x, y, c, 4 * x + 2 * y + c


def _peer(k):
    x, y, c, _ = _me()
    px = 1 - x if k & 4 else x
    py = 1 - y if k & 2 else y
    pc = 1 - c if k & 1 else c
    return (px, py, pc), 4 * px + 2 * py + pc


SIBLING = 1
OTHER_CHIPS = (2, 4, 6)


def _remote(src, dst, sems, a, k, dev):
    return pltpu.make_async_remote_copy(src_ref=src, dst_ref=dst, send_sem=sems[0].at[a, k - 1],
                                        recv_sem=sems[1].at[a, k - 1], device_id=dev,
                                        device_id_type=pl.DeviceIdType.MESH)


def _exchange_copies(ins, outs, sems, scatter):
    me = _me()[3]
    local, first, relay, arrivals = [], [], [], []
    for a in range(len(ins)):
        src = ins[a].at[me] if scatter else ins[a]
        local.append(pltpu.make_async_copy(src, outs[a].at[me], sems[2].at[a]))
        for k in range(1, N_DEV):
            dev, idx = _peer(k)
            landed = _remote(src, outs[a].at[idx], sems, a, k, dev)
            if scatter:
                first.append(_remote(ins[a].at[idx], outs[a].at[me], sems, a, k, dev))
                arrivals.append(landed)
            elif k == SIBLING:
                first.append(_remote(src, outs[a].at[me], sems, a, k, dev))
                arrivals.append(landed)
            elif k in OTHER_CHIPS:
                first.append(_remote(src, outs[a].at[me], sems, a, k, dev))
                sib, _ = _peer(SIBLING)
                relay.append((landed, _remote(outs[a].at[idx], outs[a].at[idx], sems, a, k | SIBLING, sib)))
            else:
                arrivals.append(landed)
    return local, first, relay, arrivals


def _exchange_start(ins, outs, sems, scatter):
    local, first, _, _ = _exchange_copies(ins, outs, sems, scatter)
    for cp in local + first:
        cp.start()


def _exchange_wait(ins, outs, sems, scatter):
    local, first, relay, arrivals = _exchange_copies(ins, outs, sems, scatter)
    for landed, forward in relay:
        landed.wait_recv()
        forward.start()
    for cp in arrivals:
        cp.wait_recv()
    for cp in first + [forward for _, forward in relay]:
        cp.wait_send()
    for cp in local:
        cp.wait()


def _exchange_shapes(arrs, scatter):
    return [_sds(a.shape if scatter else (N_DEV,) + a.shape, a.dtype) for a in arrs]


def _exchange_sems(n):
    return [pltpu.SemaphoreType.DMA((n, N_DEV - 1)), pltpu.SemaphoreType.DMA((n, N_DEV - 1)),
            pltpu.SemaphoreType.DMA((n,))]


def _exchange(arrs, *, name, scatter):
    n = len(arrs)

    def body(*refs):
        _exchange_start(refs[:n], refs[n:2 * n], refs[2 * n:], scatter)
        _exchange_wait(refs[:n], refs[n:2 * n], refs[2 * n:], scatter)

    any_spec = pl.BlockSpec(memory_space=pl.ANY)
    outs = pl.pallas_call(body, name=name, out_shape=_exchange_shapes(arrs, scatter), in_specs=[any_spec] * n,
                          out_specs=[any_spec] * n, scratch_shapes=_exchange_sems(n))(*arrs)
    return list(outs)


def _mod_cols(c_all, w_ada, b_cols):
    def body(c_ref, w_ref, b_ref, mod_ref, act_ref):
        c = c_ref[...]
        act = c * _sigmoid(c)
        act_ref[...] = act
        mod_ref[...] = _dot(act.astype(BF16), w_ref[...].astype(BF16)) + b_ref[...]

    return _call(body, name='mod_cols', grid=(1,),
                 in_specs=[_const(c_all.shape), _const(w_ada.shape), _const(b_cols.shape)],
                 out_specs=[_const((N_DEV, ADA_SHARD)), _const(c_all.shape)],
                 out_shape=[_sds((N_DEV, ADA_SHARD)), _sds(c_all.shape)], vmem=VMEM_BIG)(c_all, w_ada, b_cols)


def _grad_w_ada(act_t, dmod_cols):
    def body(a_ref, d_ref, o_ref):
        o_ref[...] = _dot(a_ref[...], d_ref[...])

    return _call(body, name='grad_w_ada', grid=(1,), in_specs=[_const(act_t.shape), _const(dmod_cols.shape)],
                 out_specs=_const((D_MODEL, ADA_SHARD)), out_shape=_sds((D_MODEL, ADA_SHARD)),
                 vmem=VMEM_BIG)(act_t, dmod_cols)


def _pre_mix(x, sc, sh, g, w_s, tm, ride):
    T = x.shape[0]
    group = 4

    def body(x_ref, sc_ref, sh_ref, g_ref, w_ref, proj_ref, h_ref):
        @pl.when(pl.program_id(1) == 0)
        def _():
            xv = x_ref[...]
            h_ref[...] = ((xv * _rsqrt_mean(xv) * g_ref[...]) * (1.0 + sc_ref[...]) + sh_ref[...]).astype(BF16)

        for s in range(group):
            proj_ref[:, s * IN_SHARD:(s + 1) * IN_SHARD] = _dot(h_ref[...], w_ref[s])

    row = pl.BlockSpec((tm, D_MODEL), lambda i, j: (i, 0))
    vec = _const((1, D_MODEL))
    return _call(body, name='pre_mix', grid=(T // tm, N_DEV // group),
                 in_specs=[row, vec, vec, vec, pl.BlockSpec((group, D_MODEL, IN_SHARD), lambda i, j: (j, 0, 0))],
                 out_specs=[pl.BlockSpec((tm, group * IN_SHARD), lambda i, j: (i, j)), row],
                 out_shape=[_sds((T, D_IN_PROJ)), _sds((T, D_MODEL), BF16)],
                 sem=('parallel', 'arbitrary'), ride=ride)(x, sc, sh, g, w_s)


def _halo_before(tm, rows=HALO):
    return lambda i: jnp.maximum(i * (tm // rows) - 1, 0)


def _halo_after(tm, T, rows=HALO):
    return lambda i: jnp.minimum((i + 1) * (tm // rows), T // rows - 1)


def _mix_fwd(yssm, proj, d, glu_w, glu_b, g_ssm, cw, g_conv, avg16, avg64, tm):
    T = yssm.shape[0]
    hb = _halo_before(tm)

    def body(y_ref, p_ref, ph_ref, d_ref, gw_ref, gb_ref, gs_ref, cw_ref, gc_ref, a16_ref, a64_ref, o_ref):
        i = pl.program_id(0)
        u = p_ref[:, 0:D_SSM]
        y = y_ref[...] + d_ref[...] * u
        z, _ = _gelu(y)
        gate = _sigmoid(_dot(z.astype(BF16), gw_ref[...]) + gb_ref[...])
        ya = z * gate
        rs = lax.rsqrt(_dot_split(ya * ya, a16_ref[...], 2) + EPS)
        o_ref[:, 0:D_SSM] = (ya * rs * gs_ref[...]).astype(BF16)
        bg = p_ref[:, D_SSM:D_SSM + D_CONV]
        cv = p_ref[:, D_SSM + D_CONV:D_SSM + 2 * D_CONV] * p_ref[:, D_SSM + 2 * D_CONV:D_IN_PROJ]
        hv = ph_ref[:, D_SSM + D_CONV:D_SSM + 2 * D_CONV] * ph_ref[:, D_SSM + 2 * D_CONV:D_IN_PROJ]
        hv = jnp.where(i > 0, hv, 0.0)
        conv, _, _ = _conv3(cv, hv, cw_ref)
        yb = bg * conv
        rsb = lax.rsqrt(_dot_split(yb * yb, a64_ref[...], 2) + EPS)
        o_ref[:, D_SSM:D_MODEL] = (yb * rsb * gc_ref[...]).astype(BF16)

    vec = _const((1, D_SSM))
    sq = _const((D_SSM, D_SSM))
    return _call(body, name='mix_fwd', grid=(T // tm,),
                 in_specs=[pl.BlockSpec((tm, D_SSM), lambda i: (i, 0)), pl.BlockSpec((tm, D_IN_PROJ), lambda i: (i, 0)),
                           pl.BlockSpec((HALO, D_IN_PROJ), lambda i: (hb(i), 0)), vec, sq, vec, vec,
                           _const((3, D_CONV)), vec, sq, sq],
                 out_specs=pl.BlockSpec((tm, D_MODEL), lambda i: (i, 0)), out_shape=_sds((T, D_MODEL), BF16),
                 sem=('parallel',), vmem=VMEM_BIG)(yssm, proj, proj, d, glu_w, glu_b, g_ssm, cw, g_conv, avg16, avg64)


def _out_proj(ycat, w_out, x, gt, g_post, g_pre, sc, sh, tm):
    T = x.shape[0]

    def body(y_ref, w_ref, x_ref, gt_ref, gp_ref, g2_ref, sc_ref, sh_ref, o_ref, x1_ref, h_ref):
        o = _dot(y_ref[...], w_ref[...])
        o_ref[...] = o
        x1 = x_ref[...] + gt_ref[...] * (o * _rsqrt_mean(o) * gp_ref[...])
        x1_ref[...] = x1
        h_ref[...] = ((x1 * _rsqrt_mean(x1) * g2_ref[...]) * (1.0 + sc_ref[...]) + sh_ref[...]).astype(BF16)

    row = pl.BlockSpec((tm, D_MODEL), lambda i: (i, 0))
    vec = _const((1, D_MODEL))
    return _call(body, name='out_proj', grid=(T // tm,),
                 in_specs=[row, _const((D_MODEL, D_MODEL)), row, vec, vec, vec, vec, vec],
                 out_specs=[row, row, row],
                 out_shape=[_sds((T, D_MODEL)), _sds((T, D_MODEL)), _sds((T, D_MODEL), BF16)],
                 sem=('parallel',), vmem=VMEM_BIG)(ycat, w_out, x, gt, g_post, g_pre, sc, sh)


def _ffn_up(h2, w_a, w_b, cw8, tm, ride):
    T = h2.shape[0]
    hb = _halo_before(tm, HALO16)
    half = D_MODEL // 2

    def body(h_ref, hh_ref, wa_ref, wb_ref, cw_ref, up_ref, hid_ref):
        def times_w(ref, s):
            return _dot_nt(ref[:, :half], wa_ref[s]) + _dot_nt(ref[:, half:], wb_ref[s])

        for s in range(2):
            up = times_w(h_ref, s)
            up_ref[s] = up.astype(BF16)
            before = jnp.where(pl.program_id(0) > 0, times_w(hh_ref, s), 0.0)
            hid_ref[s] = _conv3(up, before, cw_ref.at[s])[0].astype(BF16)

    out = pl.BlockSpec((2, tm, FF_SHARD), lambda i, j: (j, i, 0))
    return _call(body, name='ffn_up', grid=(T // tm, N_DEV // 2),
                 in_specs=[pl.BlockSpec((tm, D_MODEL), lambda i, j: (i, 0)),
                           pl.BlockSpec((HALO16, D_MODEL), lambda i, j: (hb(i), 0)),
                           pl.BlockSpec((2, FF_SHARD, half), lambda i, j: (j, 0, 0)),
                           pl.BlockSpec((2, FF_SHARD, half), lambda i, j: (j, 0, 0)),
                           pl.BlockSpec((2, 3, FF_SHARD), lambda i, j: (j, 0, 0))],
                 out_specs=[out, out], out_shape=[_sds((N_DEV, T, FF_SHARD), BF16)] * 2,
                 sem=('parallel', 'parallel'), vmem=VMEM_BIG, ride=ride)(h2, h2, w_a, w_b, cw8)


def _ffn_down(hid4, wd4, x1, tgt, gt, g_post, tm):
    T = x1.shape[0]
    nb = T // tm

    def body(a_ref, w_ref, x1_ref, t_ref, gt_ref, g_ref, ddn_ref, dx_ref, loss_ref, dgt_ref, dg_ref, dn_ref):
        i, j = pl.program_id(0), pl.program_id(1)
        part = None
        for s in range(2):
            act = (_silu_parts(a_ref[0, s].astype(F32))[0] * a_ref[1, s].astype(F32)).astype(BF16)
            term = _dot(act, w_ref[s])
            part = term if part is None else part + term

        @pl.when(jnp.logical_and(i == 0, j == 0))
        def _():
            dgt_ref[...] = jnp.zeros_like(dgt_ref)
            dg_ref[...] = jnp.zeros_like(dg_ref)

        @pl.when(j == 0)
        def _():
            dn_ref[...] = part

        @pl.when(j > 0)
        def _():
            dn_ref[...] += part

        @pl.when(j == 1)
        def _():
            dn, gv, gate = dn_ref[...], g_ref[...], gt_ref[...]
            r = _rsqrt_mean(dn)
            normed = dn * r * gv
            err = x1_ref[...] + gate * normed - t_ref[...]
            dx = err * (1.0 / D_MODEL)
            dx_ref[...] = dx
            tot = jnp.sum(jnp.sum(err * err, axis=1, keepdims=True), axis=0, keepdims=True) * (0.5 / D_MODEL)
            loss_ref[...] = jnp.broadcast_to(tot, (8, 128))
            dgt_ref[...] += _colsum(dx * normed)
            dnn = dx * gate
            dg_ref[...] += _colsum(dnn * dn * r)
            ddn_ref[...] = _norm_bwd(dnn, dn, r, gv).astype(BF16)

    row = pl.BlockSpec((tm, D_MODEL), lambda i, j: (i, 0))
    vec = _const((1, D_MODEL))
    return _call(body, name='ffn_down', grid=(nb, 2),
                 in_specs=[pl.BlockSpec((2, 2, tm, FF_SHARD), lambda i, j: (0, j, i, 0)),
                           pl.BlockSpec((2, FF_SHARD, D_MODEL), lambda i, j: (j, 0, 0)), row, row, vec, vec],
                 out_specs=[row, row, pl.BlockSpec((None, 8, 128), lambda i, j: (i, 0, 0)), vec, vec],
                 out_shape=[_sds((T, D_MODEL), BF16), _sds((T, D_MODEL)), _sds((nb, 8, 128)), _sds((1, D_MODEL)),
                            _sds((1, D_MODEL))],
                 scratch=[pltpu.VMEM((tm, D_MODEL), F32)], sem=('arbitrary', 'arbitrary'),
                 vmem=VMEM_BIG)(hid4, wd4, x1, tgt, gt, g_post)


def _ssm_prep(lre, lim, lst, b_re, b_im):
    def body(lre_ref, lim_ref, lst_ref, br_ref, bi_ref, ar_ref, ai_ref, bbr_ref, bbi_ref):
        ar, ai, qr, qi = _zoh(lre_ref[...], lim_ref[...], lst_ref[...])[:4]
        ar_ref[...] = ar
        ai_ref[...] = ai
        bbr_ref[...] = qr * br_ref[...] - qi * bi_ref[...]
        bbi_ref[...] = qr * bi_ref[...] + qi * br_ref[...]

    shp = lre.shape
    return _call(body, name='ssm_prep', grid=(1,), in_specs=[_const(shp)] * 5, out_specs=[_const(shp)] * 4,
                 out_shape=[_sds(shp)] * 4)(lre, lim, lst, b_re, b_im)


def _zoh(lre, lim, lst):
    lr = jnp.minimum(lre, LAMBDA_RE_MAX)
    st = jnp.exp(lst)
    mag = jnp.exp(lr * st)
    ar = mag * jnp.cos(lim * st)
    ai = mag * jnp.sin(lim * st)
    den = lr * lr + lim * lim
    qr = ((ar - 1.0) * lr + ai * lim) / den
    qi = (ai * lr - (ar - 1.0) * lim) / den
    return ar, ai, qr, qi, lr, st, den


def _ssm_prep_bwd(lre, lim, lst, b_re, b_im, dbbr, dbbi, dar, dai, seg):
    def body(lre_ref, lim_ref, lst_ref, br_ref, bi_ref, dbbr_ref, dbbi_ref, dar_ref, dai_ref, seg_ref,
             dbr_ref, dbi_ref, dlre_ref, dlim_ref, dlst_ref):
        lre_v = lre_ref[...]
        li = lim_ref[...]
        ar, ai, qr, qi, lr, st, den = _zoh(lre_v, li, lst_ref[...])
        br, bi, gbr, gbi = br_ref[...], bi_ref[...], dbbr_ref[...], dbbi_ref[...]
        dbr_ref[...] = qr * gbr + qi * gbi
        dbi_ref[...] = qr * gbi - qi * gbr
        gqr = _dot_split(br * gbr + bi * gbi, seg_ref[...], 3)
        gqi = _dot_split(br * gbi - bi * gbr, seg_ref[...], 3)
        ir, ii = lr / den, -li / den
        gar = dar_ref[...] + ir * gqr + ii * gqi
        gai = dai_ref[...] + ir * gqi - ii * gqr
        tr, ti = qr * ir - qi * ii, qr * ii + qi * ir
        glr = -(tr * gqr + ti * gqi)
        gli = -(tr * gqi - ti * gqr)
        gzr = ar * gar + ai * gai
        gzi = ar * gai - ai * gar
        glr = glr + st * gzr
        gli = gli + st * gzi
        gst = (lr * gzr + li * gzi) * st
        dlre_ref[...] = jnp.where(lre_v < LAMBDA_RE_MAX, glr, 0.0)
        dlim_ref[...] = gli
        dlst_ref[...] = jnp.sum(gst, axis=1, keepdims=True) * (1.0 / SSM_GROUP)

    shp = lre.shape
    return _call(body, name='ssm_prep_bwd', grid=(1,), in_specs=[_const(shp)] * 9 + [_const(seg.shape)],
                 out_specs=[_const(shp)] * 4 + [_const((N_GROUPS, 1))],
                 out_shape=[_sds(shp)] * 4 + [_sds((N_GROUPS, 1))], vmem=VMEM_BIG)(
                     lre, lim, lst, b_re, b_im, dbbr, dbbi, dar, dai, seg)


def _scan_specs(T):
    return dict(
        chan=pl.BlockSpec((T, CHAN_BLOCK), lambda cb: (0, cb)),
        state=pl.BlockSpec((T, STATE_BLOCK), lambda cb: (0, cb)),
        b=pl.BlockSpec((CHAN_BLOCK, STATE_BLOCK), lambda cb: (cb, cb)),
        c=pl.BlockSpec((STATE_BLOCK, CHAN_BLOCK), lambda cb: (cb, cb)),
        lam=pl.BlockSpec((1, STATE_BLOCK), lambda cb: (0, cb)),
    )


def _complex_power(re, im, n):
    out = None
    while True:
        if n & 1:
            out = (re, im) if out is None else (out[0] * re - out[1] * im, out[0] * im + out[1] * re)
        n >>= 1
        if n == 0:
            return out
        re, im = re * re - im * im, 2.0 * re * im


def _rows8(i):
    if isinstance(i, int):
        return pl.ds(i * SUBLANES, SUBLANES)
    return pl.ds(pl.multiple_of(i * SUBLANES, SUBLANES), SUBLANES)


def _scan_loop(n_steps, body, init):
    trips = n_steps // SCAN_UNROLL

    def trip(t, carry):
        for u in range(SCAN_UNROLL):
            carry = body(t * SCAN_UNROLL + u, carry)
        return carry

    carry = lax.fori_loop(0, trips, trip, init)
    for step in range(trips * SCAN_UNROLL, n_steps):
        carry = body(step, carry)
    return carry


def _ssm_fwd(u_perm, b_re, b_im, c_re, c_im, lam_r, lam_i, ride):
    T = u_perm.shape[0]
    ls = T // SUBLANES
    rc = min(1024, T)
    sp = _scan_specs(T)

    def body(u_ref, bre_ref, bim_ref, cre_ref, cim_ref, lr_ref, li_ref, so_re_ref, so_im_ref, y_ref, sre_ref, sim_ref):
        for c in range(T // rc):
            rows = pl.ds(c * rc, rc)
            ub = u_ref[rows, :].astype(BF16)
            sre_ref[rows, :] = _dot(ub, bre_ref[...])
            sim_ref[rows, :] = _dot(ub, bim_ref[...])
        shp = (SUBLANES, STATE_BLOCK)
        lr = jnp.broadcast_to(lr_ref[...], shp)
        li = jnp.broadcast_to(li_ref[...], shp)
        zero = jnp.zeros(shp, F32)

        def step(i, carry):
            sr, si = carry
            rows = _rows8(i)
            nr = lr * sr - li * si + sre_ref[rows, :]
            ni = lr * si + li * sr + sim_ref[rows, :]
            sre_ref[rows, :] = nr
            sim_ref[rows, :] = ni
            return nr, ni

        fr, fi = _scan_loop(ls, step, (zero, zero))
        pr, pi_ = _complex_power(lr, li, ls)
        row = lax.broadcasted_iota(jnp.int32, shp, 0)
        ir, ii = zero, zero
        for _ in range(SUBLANES - 1):
            er = fr + pr * ir - pi_ * ii
            ei = fi + pr * ii + pi_ * ir
            ir = jnp.where(row == 0, 0.0, pltpu.roll(er, 1, 0))
            ii = jnp.where(row == 0, 0.0, pltpu.roll(ei, 1, 0))

        def fix(i, carry):
            cr, ci = carry
            rows = _rows8(i)
            nr = lr * cr - li * ci
            ni = lr * ci + li * cr
            sre_ref[rows, :] += nr
            sim_ref[rows, :] += ni
            return nr, ni

        _scan_loop(ls, fix, (ir, ii))
        for c in range(T // rc):
            rows = pl.ds(c * rc, rc)
            s_r, s_i = sre_ref[rows, :].astype(BF16), sim_ref[rows, :].astype(BF16)
            so_re_ref[rows, :] = s_r
            so_im_ref[rows, :] = s_i
            y_ref[rows, :] = _dot(s_r, cre_ref[...]) - _dot(s_i, cim_ref[...])

    return _call(body, name='ssm_fwd', grid=(N_STATE // STATE_BLOCK,),
                 in_specs=[sp['chan'], sp['b'], sp['b'], sp['c'], sp['c'], sp['lam'], sp['lam']],
                 out_specs=[sp['state'], sp['state'], sp['chan']],
                 out_shape=[_sds((T, N_STATE), BF16), _sds((T, N_STATE), BF16), _sds((T, D_SSM))],
                 scratch=[pltpu.VMEM((T, STATE_BLOCK), F32), pltpu.VMEM((T, STATE_BLOCK), F32)],
                 sem=('arbitrary',), vmem=VMEM_MOST, ride=ride)(u_perm, b_re, b_im, c_re, c_im, lam_r, lam_i)


def _ssm_bwd(dy_perm, u_perm, s_re, s_im, b_re, b_im, c_re, c_im, lam_r, lam_i, ride):
    T = u_perm.shape[0]
    ls = T // SUBLANES
    rc = min(1024, T)
    sp = _scan_specs(T)
    ncb = N_STATE // STATE_BLOCK

    def body(dy_ref, u_ref, sre_ref, sim_ref, bre_ref, bim_ref, cre_ref, cim_ref, lr_ref, li_ref,
             du_ref, dbr_ref, dbi_ref, dcr_ref, dci_ref, dar_ref, dai_ref, gre_ref, gim_ref):
        shp = (SUBLANES, STATE_BLOCK)
        zero = jnp.zeros(shp, F32)
        tail = pl.ds(T, SUBLANES)
        gre_ref[tail, :] = zero
        gim_ref[tail, :] = zero
        for c in range(T // rc):
            rows = pl.ds(c * rc, rc)
            dyb = dy_ref[rows, :].astype(BF16)
            gre_ref[rows, :] = _dot_nt(dyb, cre_ref[...])
            gim_ref[rows, :] = -_dot_nt(dyb, cim_ref[...])
        lr = jnp.broadcast_to(lr_ref[...], shp)
        li = jnp.broadcast_to(li_ref[...], shp)

        def step(k, carry):
            gr, gi = carry
            rows = _rows8(ls - 1 - k)
            nr = lr * gr + li * gi + gre_ref[rows, :]
            ni = lr * gi - li * gr + gim_ref[rows, :]
            gre_ref[rows, :] = nr
            gim_ref[rows, :] = ni
            return nr, ni

        fr, fi = _scan_loop(ls, step, (zero, zero))
        pr, pi_ = _complex_power(lr, -li, ls)
        row = lax.broadcasted_iota(jnp.int32, shp, 0)
        cr, ci = zero, zero
        for _ in range(SUBLANES - 1):
            er = fr + pr * cr - pi_ * ci
            ei = fi + pr * ci + pi_ * cr
            cr = jnp.where(row == SUBLANES - 1, 0.0, pltpu.roll(er, SUBLANES - 1, 0))
            ci = jnp.where(row == SUBLANES - 1, 0.0, pltpu.roll(ei, SUBLANES - 1, 0))

        def fix(k, carry):
            dr, di = carry
            rows = _rows8(ls - 1 - k)
            dr, di = lr * dr + li * di, lr * di - li * dr
            gre_ref[rows, :] += dr
            gim_ref[rows, :] += di
            return dr, di

        _scan_loop(ls, fix, (cr, ci))

        acc_r = jnp.zeros((1, STATE_BLOCK), F32)
        acc_i = jnp.zeros((1, STATE_BLOCK), F32)
        for c in range(T // rc):
            rows, nxt = pl.ds(c * rc, rc), pl.ds(c * rc + SUBLANES, rc)
            s_r, s_i = sre_ref[rows, :].astype(F32), sim_ref[rows, :].astype(F32)
            g_r, g_i = gre_ref[nxt, :], gim_ref[nxt, :]
            acc_r = acc_r + _colsum(g_r * s_r + g_i * s_i)
            acc_i = acc_i + _colsum(g_i * s_r - g_r * s_i)
        last = pl.ds(T - 2 * SUBLANES, 2 * SUBLANES)
        first = pl.ds(0, SUBLANES)
        spr = jnp.where(row == 0, 0.0, pltpu.roll(sre_ref[last, :].astype(F32)[SUBLANES:], 1, 0))
        spi = jnp.where(row == 0, 0.0, pltpu.roll(sim_ref[last, :].astype(F32)[SUBLANES:], 1, 0))
        gr, gi = gre_ref[first, :], gim_ref[first, :]
        dar_ref[...] = acc_r + _colsum(gr * spr + gi * spi)
        dai_ref[...] = acc_i + _colsum(gi * spr - gr * spi)

        for c in range(T // rc):
            rows = pl.ds(c * rc, rc)
            g_r, g_i = gre_ref[rows, :].astype(BF16), gim_ref[rows, :].astype(BF16)
            s_r, s_i = sre_ref[rows, :], sim_ref[rows, :]
            ub, dyb = u_ref[rows, :].astype(BF16), dy_ref[rows, :].astype(BF16)
            du_ref[rows, :] = _dot_nt(g_r, bre_ref[...]) + _dot_nt(g_i, bim_ref[...])
            parts = (_dot_tn(ub, g_r), _dot_tn(ub, g_i), _dot_tn(s_r, dyb), -_dot_tn(s_i, dyb))
            outs = (dbr_ref, dbi_ref, dcr_ref, dci_ref)
            for o_ref, part in zip(outs, parts):
                if c == 0:
                    o_ref[...] = part
                else:
                    o_ref[...] += part

    blk = lambda r, c: pl.BlockSpec((None, r, c), lambda cb: (cb, 0, 0))
    return _call(body, name='ssm_bwd', grid=(ncb,),
                 in_specs=[sp['chan'], sp['chan'], sp['state'], sp['state'], sp['b'], sp['b'], sp['c'], sp['c'],
                           sp['lam'], sp['lam']],
                 out_specs=[sp['chan'], blk(CHAN_BLOCK, STATE_BLOCK), blk(CHAN_BLOCK, STATE_BLOCK),
                            blk(STATE_BLOCK, CHAN_BLOCK), blk(STATE_BLOCK, CHAN_BLOCK), blk(1, STATE_BLOCK),
                            blk(1, STATE_BLOCK)],
                 out_shape=[_sds((T, D_SSM)), _sds((ncb, CHAN_BLOCK, STATE_BLOCK)), _sds((ncb, CHAN_BLOCK, STATE_BLOCK)),
                            _sds((ncb, STATE_BLOCK, CHAN_BLOCK)), _sds((ncb, STATE_BLOCK, CHAN_BLOCK)),
                            _sds((ncb, 1, STATE_BLOCK)), _sds((ncb, 1, STATE_BLOCK))],
                 scratch=[pltpu.VMEM((T + SUBLANES, STATE_BLOCK), F32), pltpu.VMEM((T + SUBLANES, STATE_BLOCK), F32)],
                 sem=('arbitrary',), vmem=VMEM_MOST, ride=ride)(dy_perm, u_perm, s_re, s_im, b_re, b_im, c_re, c_im,
                                                                lam_r, lam_i)


def _ffn_dact(ddn, wd4, hid4, tm):
    T = ddn.shape[0]
    nb = T // tm

    def body(d_ref, w_ref, hid_ref, o_ref, gw_ref, acc_ref):
        i = pl.program_id(1)
        d = d_ref[...]
        dact = _dot_nt(d, w_ref[...])
        silu, dsilu = _silu_parts(hid_ref[0].astype(F32))
        hid_v = hid_ref[1].astype(F32)
        o_ref[0] = (dact * hid_v * dsilu).astype(BF16)
        o_ref[1] = (dact * silu).astype(BF16)
        part = _dot_tn((silu * hid_v).astype(BF16), d)

        @pl.when(i == 0)
        def _():
            acc_ref[...] = part

        @pl.when(i > 0)
        def _():
            acc_ref[...] += part

        @pl.when(i == nb - 1)
        def _():
            gw_ref[...] = acc_ref[...].astype(BF16)

    blk = pl.BlockSpec((2, None, tm, FF_SHARD), lambda j, i: (0, j, i, 0))
    w_blk = pl.BlockSpec((None, FF_SHARD, D_MODEL), lambda j, i: (j, 0, 0))
    return _call(body, name='ffn_dact', grid=(4, nb),
                 in_specs=[pl.BlockSpec((tm, D_MODEL), lambda j, i: (i, 0)), w_blk, blk],
                 out_specs=[blk, w_blk],
                 out_shape=[_sds((2, 4, T, FF_SHARD), BF16), _sds((4, FF_SHARD, D_MODEL), BF16)],
                 scratch=[pltpu.VMEM((FF_SHARD, D_MODEL), F32)], sem=('parallel', 'arbitrary'),
                 vmem=VMEM_BIG)(ddn, wd4, hid4)


def _ffn_dup(dhid8, up8, cw8, tm, ride):
    T = up8.shape[1]
    nb = T // tm
    ha = _halo_after(tm, T, HALO16)

    def body(dh_ref, dha_ref, up_ref, cw_ref, dup_ref, dcw_ref):
        i = pl.program_id(1)

        @pl.when(i == 0)
        def _():
            dcw_ref[...] = jnp.zeros_like(dcw_ref)

        dh = dh_ref[...].astype(F32)
        dup, dh1, dh2 = _conv3_t(dh, jnp.where(i < nb - 1, dha_ref[...].astype(F32), 0.0), cw_ref)
        dup_ref[...] = dup.astype(BF16)
        up = up_ref[...].astype(F32)
        dcw_ref[0:1, :] += _colsum(dh2 * up)
        dcw_ref[1:2, :] += _colsum(dh1 * up)
        dcw_ref[2:3, :] += _colsum(dh * up)

    main = pl.BlockSpec((None, tm, FF_SHARD), lambda j, i: (j, i, 0))
    return _call(body, name='ffn_dup', grid=(N_DEV, nb),
                 in_specs=[main, pl.BlockSpec((None, HALO16, FF_SHARD), lambda j, i: (j, ha(i), 0)), main,
                           pl.BlockSpec((None, 3, FF_SHARD), lambda j, i: (j, 0, 0))],
                 out_specs=[main, pl.BlockSpec((None, 8, FF_SHARD), lambda j, i: (j, 0, 0))],
                 out_shape=[_sds((N_DEV, T, FF_SHARD), BF16), _sds((N_DEV, 8, FF_SHARD))],
                 sem=('parallel', 'arbitrary'), vmem=VMEM_BIG, ride=ride)(dhid8, dhid8, up8, cw8)


def _grad_tn(a, b, a_spec, b_spec, groups, m, n, tk, name, ride=None, parts=1):
    T = a.shape[-2]
    nk = T // tk
    mp = m // parts

    def body(a_ref, b_ref, *refs):
        o_refs, acc_ref = refs[:parts], refs[parts]
        k = pl.program_id(1)
        part = _dot_tn(a_ref[...], b_ref[...])

        @pl.when(k == 0)
        def _():
            acc_ref[...] = part

        @pl.when(k > 0)
        def _():
            acc_ref[...] += part

        @pl.when(k == nk - 1)
        def _():
            for p, o_ref in enumerate(o_refs):
                o_ref[...] = acc_ref[p * mp:(p + 1) * mp, :].astype(BF16)

    out_spec = pl.BlockSpec((None, mp, n), lambda g, k: (g, 0, 0))
    res = _call(body, name=name, grid=(groups, nk), in_specs=[a_spec, b_spec], out_specs=[out_spec] * parts,
                out_shape=[_sds((groups, mp, n), BF16)] * parts, scratch=[pltpu.VMEM((m, n), F32)],
                sem=('parallel', 'arbitrary'), vmem=VMEM_BIG, ride=ride)(a, b)
    if parts > 1:
        return res
    return res[0] if ride is None else (res[0][0], res[1])


def _grad_w_in(h1, dproj, tk, ride):
    T = h1.shape[0]
    nk = T // tk
    half = D_IN_PROJ // 2

    def body(a_ref, b_ref, o_ref, acc_ref):
        k = pl.program_id(0)
        for h in range(2):
            cols = slice(h * half, (h + 1) * half)
            part = _dot_tn(a_ref[...], b_ref[:, cols])

            @pl.when(k == 0)
            def _():
                acc_ref[:, cols] = part

            @pl.when(k > 0)
            def _():
                acc_ref[:, cols] += part

        @pl.when(k == nk - 1)
        def _():
            for g in range(N_DEV):
                o_ref[g] = acc_ref[:, g * IN_SHARD:(g + 1) * IN_SHARD].astype(BF16)

    return _call(body, name='grad_w_in', grid=(nk,),
                 in_specs=[pl.BlockSpec((tk, D_MODEL), lambda k: (k, 0)), pl.BlockSpec((tk, D_IN_PROJ), lambda k: (k, 0))],
                 out_specs=_const((N_DEV, D_MODEL, IN_SHARD)), out_shape=_sds((N_DEV, D_MODEL, IN_SHARD), BF16),
                 scratch=[pltpu.VMEM((D_MODEL, D_IN_PROJ), F32)], sem=('arbitrary',), vmem=VMEM_BIG, ride=ride)(h1, dproj)


def _pre_norm_bwd(dz, dz_spec, w_parts, xin, dres, sc, g, tm, name, ride, below=None, group=1, w_t=False):
    T = xin.shape[0]
    n = w_parts[0].shape[1] if w_t else w_parts[0].shape[2]
    mul = _dot if w_t else _dot_nt
    steps = N_DEV // group
    width = D_MODEL // len(w_parts)

    def body(dz_ref, *refs):
        w_refs, (x_ref, dr_ref, sc_ref, g_ref), refs = refs[:len(w_parts)], refs[len(w_parts):len(w_parts) + 4], \
            refs[len(w_parts) + 4:]
        if below is None:
            dx_ref, dsh_ref, dsc_ref, dg_ref = refs
            sums = (dsh_ref, dsc_ref, dg_ref)
        else:
            v_ref, gate_ref, g2_ref, dx_ref, dsh_ref, dsc_ref, dg_ref, dv_ref, dgate_ref, dg2_ref = refs
            sums = (dsh_ref, dsc_ref, dg_ref, dgate_ref, dg2_ref)
        i, j = pl.program_id(0), pl.program_id(1)
        piece = (lambda s: dz_ref[s]) if dz.ndim == 3 else (lambda s: dz_ref[:, s * n:(s + 1) * n])
        parts = []
        for w_ref in w_refs:
            part = mul(piece(0), w_ref[0])
            for s in range(1, group):
                part = part + mul(piece(s), w_ref[s])
            parts.append(part)

        @pl.when(jnp.logical_and(i == 0, j == 0))
        def _():
            for s_ref in sums:
                s_ref[...] = jnp.zeros_like(s_ref)

        @pl.when(j == 0)
        def _():
            for k, part in enumerate(parts):
                dx_ref[:, k * width:(k + 1) * width] = part

        @pl.when(j > 0)
        def _():
            for k, part in enumerate(parts):
                dx_ref[:, k * width:(k + 1) * width] += part

        @pl.when(j == steps - 1)
        def _():
            dh, xv, gv = dx_ref[...], x_ref[...], g_ref[...]
            r = _rsqrt_mean(xv)
            dsh_ref[...] += _colsum(dh)
            dsc_ref[...] += _colsum(dh * (xv * r * gv))
            dxn = dh * (1.0 + sc_ref[...])
            dg_ref[...] += _colsum(dxn * xv * r)
            dx = dr_ref[...] + _norm_bwd(dxn, xv, r, gv)
            dx_ref[...] = dx
            if below is not None:
                v, g2 = v_ref[...], g2_ref[...]
                rv = _rsqrt_mean(v)
                dgate_ref[...] += _colsum(dx * (v * rv * g2))
                dn = dx * gate_ref[...]
                dg2_ref[...] += _colsum(dn * v * rv)
                dv_ref[...] = _norm_bwd(dn, v, rv, g2).astype(BF16)

    row = pl.BlockSpec((tm, D_MODEL), lambda i, j: (i, 0))
    vec = _const((1, D_MODEL))
    in_specs = [dz_spec] + [pl.BlockSpec((group,) + w.shape[1:], lambda i, j: (j, 0, 0)) for w in w_parts]
    in_specs += [row, row, vec, vec]
    out_specs = [row, vec, vec, vec]
    out_shape = [_sds((T, D_MODEL)), _sds((1, D_MODEL)), _sds((1, D_MODEL)), _sds((1, D_MODEL))]
    args = [dz, *w_parts, xin, dres, sc, g]
    if below is not None:
        in_specs += [row, vec, vec]
        out_specs += [row, vec, vec]
        out_shape += [_sds((T, D_MODEL), BF16), _sds((1, D_MODEL)), _sds((1, D_MODEL))]
        args += list(below)
    return _call(body, name=name, grid=(T // tm, steps), in_specs=in_specs, out_specs=out_specs,
                 out_shape=out_shape, sem=('arbitrary', 'arbitrary'), vmem=VMEM_MOST, ride=ride)(*args)


def _mix_bwd(d_o, w_out, yssm, proj, d, glu_w, glu_b, g_ssm, cw, g_conv, avg16, avg64, tm, ride):
    T = yssm.shape[0]
    hb = _halo_before(tm)

    def body(do_ref, wo_ref, y_ref, p_ref, ph_ref, d_ref, gw_ref, gb_ref, gs_ref, cw_ref, gc_ref, a16_ref, a64_ref,
             dy_ref, dconv_ref, dbg_ref, z_ref, dlin_ref, acc_ref):
        i = pl.program_id(0)
        dyc = _dot_nt(do_ref[...], wo_ref[...])

        @pl.when(i == 0)
        def _():
            acc_ref[...] = jnp.zeros_like(acc_ref)

        u = p_ref[:, 0:D_SSM]
        y = y_ref[...] + d_ref[...] * u
        z, t = _gelu(y)
        gate = _sigmoid(_dot(z.astype(BF16), gw_ref[...]) + gb_ref[...])
        ya = z * gate
        rs = lax.rsqrt(_dot_split(ya * ya, a16_ref[...], 2) + EPS)
        dna = dyc[:, 0:D_SSM]
        acc_ref[1:2, :] += _colsum(dna * ya * rs)
        dya = _head_norm_bwd(dna, ya, rs, gs_ref[...], a16_ref[...])
        dlin = dya * z * gate * (1.0 - gate)
        acc_ref[0:1, :] += _colsum(dlin)
        dlin_b = dlin.astype(BF16)
        dz = dya * gate + _dot_nt(dlin_b, gw_ref[...])
        dy = dz * _gelu_grad(y, t)
        acc_ref[3:4, :] += _colsum(dy * u)
        dy_ref[...] = dy
        z_ref[...] = z.astype(BF16)
        dlin_ref[...] = dlin_b

        bg = p_ref[:, D_SSM:D_SSM + D_CONV]
        cv = p_ref[:, D_SSM + D_CONV:D_SSM + 2 * D_CONV] * p_ref[:, D_SSM + 2 * D_CONV:D_IN_PROJ]
        hv = ph_ref[:, D_SSM + D_CONV:D_SSM + 2 * D_CONV] * ph_ref[:, D_SSM + 2 * D_CONV:D_IN_PROJ]
        hv = jnp.where(i > 0, hv, 0.0)
        conv, cv1, cv2 = _conv3(cv, hv, cw_ref)
        yb = bg * conv
        rsb = lax.rsqrt(_dot_split(yb * yb, a64_ref[...], 2) + EPS)
        dnb = dyc[:, D_SSM:D_MODEL]
        acc_ref[2:3, :] += _colsum(dnb * yb * rsb)
        dyb = _head_norm_bwd(dnb, yb, rsb, gc_ref[...], a64_ref[...])
        dbg_ref[...] = dyb * conv
        dconv = dyb * bg
        dconv_ref[...] = dconv
        acc_ref[4:5, :] += _colsum(dconv * cv2)
        acc_ref[5:6, :] += _colsum(dconv * cv1)
        acc_ref[6:7, :] += _colsum(dconv * cv)

    vec = _const((1, D_SSM))
    sq = _const((D_SSM, D_SSM))
    half = pl.BlockSpec((tm, D_SSM), lambda i: (i, 0))
    return _call(body, name='mix_bwd', grid=(T // tm,),
                 in_specs=[pl.BlockSpec((tm, D_MODEL), lambda i: (i, 0)), _const((D_MODEL, D_MODEL)), half,
                           pl.BlockSpec((tm, D_IN_PROJ), lambda i: (i, 0)),
                           pl.BlockSpec((HALO, D_IN_PROJ), lambda i: (hb(i), 0)), vec, sq, vec, vec,
                           _const((3, D_CONV)), vec, sq, sq],
                 out_specs=[half, half, half, half, half, _const((8, D_SSM))],
                 out_shape=[_sds((T, D_SSM)), _sds((T, D_SSM)), _sds((T, D_SSM)), _sds((T, D_SSM), BF16),
                            _sds((T, D_SSM), BF16), _sds((8, D_SSM))],
                 sem=('arbitrary',), vmem=VMEM_BIG, ride=ride)(d_o, w_out, yssm, proj, proj, d, glu_w, glu_b, g_ssm, cw,
                                                              g_conv, avg16, avg64)


def _mix_bwd_proj(dconv, proj, du_ssm, dy, d, dbg, cw, tm):
    T = dy.shape[0]
    nb = T // tm
    ha = _halo_after(tm, T)

    def body(dc_ref, dch_ref, cg_ref, v_ref, du_ref, dy_ref, d_ref, dbg_ref, cw_ref, o_ref):
        i = pl.program_id(0)
        dcv = _conv3_t(dc_ref[...], jnp.where(i < nb - 1, dch_ref[...], 0.0), cw_ref)[0]
        o_ref[:, 0:D_SSM] = (du_ref[...] + dy_ref[...] * d_ref[...]).astype(BF16)
        o_ref[:, D_SSM:D_SSM + D_CONV] = dbg_ref[...].astype(BF16)
        o_ref[:, D_SSM + D_CONV:D_SSM + 2 * D_CONV] = (dcv * v_ref[...]).astype(BF16)
        o_ref[:, D_SSM + 2 * D_CONV:D_IN_PROJ] = (dcv * cg_ref[...]).astype(BF16)

    half = pl.BlockSpec((tm, D_SSM), lambda i: (i, 0))
    return _call(body, name='mix_bwd_proj', grid=(nb,),
                 in_specs=[half, pl.BlockSpec((HALO, D_CONV), lambda i: (ha(i), 0)),
                           pl.BlockSpec((tm, D_CONV), lambda i: (i, 2)), pl.BlockSpec((tm, D_CONV), lambda i: (i, 3)),
                           half, half, _const((1, D_SSM)), half, _const((3, D_CONV))],
                 out_specs=pl.BlockSpec((tm, D_IN_PROJ), lambda i: (i, 0)), out_shape=_sds((T, D_IN_PROJ), BF16),
                 sem=('parallel',), vmem=VMEM_BIG)(dconv, dconv, proj, proj, du_ssm, dy, d, dbg, cw)


ADAMW_SLOT_BYTES = 8 << 20
ADAMW_ROW_BYTES = 3 << 19


def _row_tile(rows, cols, slots):
    for cand in range(rows, 15, -1):
        if (rows % cand == 0 and cand % 16 == 0 and slots * cand * cols * 4 <= ADAMW_SLOT_BYTES
                and cand * cols * 4 <= ADAMW_ROW_BYTES):
            return cand
    return rows


def _adamw_math(g, w, m, v):
    m2 = ADAM_B1 * m + (1.0 - ADAM_B1) * g
    v2 = ADAM_B2 * v + (1.0 - ADAM_B2) * (g * g)
    m_hat = m2 / (1.0 - ADAM_B1 ** ADAM_STEP)
    v_hat = v2 / (1.0 - ADAM_B2 ** ADAM_STEP)
    return -ADAM_LR * (m_hat / (jnp.sqrt(v_hat) + ADAM_EPS) + ADAM_WD * w), m2, v2


def _adamw(pieces, w, m, v, name):
    slots, _, cols = pieces[0].shape
    rows = sum(p.shape[1] for p in pieces)
    tr = _row_tile(pieces[0].shape[1], cols, slots)
    starts, pos = [], 0
    for p in pieces:
        assert p.shape[1] % tr == 0
        starts.append(pos)
        pos += p.shape[1] // tr

    def body(*refs):
        g_refs = refs[:len(pieces)]
        w_ref, m_ref, v_ref, go_ref, d_ref, mo_ref, vo_ref = refs[len(pieces):]
        i = pl.program_id(0)
        g = None
        for g_ref, start in zip(g_refs, starts):
            part = g_ref[0].astype(F32)
            for s in range(1, slots):
                part = part + g_ref[s].astype(F32)
            g = part if g is None else jnp.where(i >= start, part, g)
        go_ref[...] = g
        d_ref[...], mo_ref[...], vo_ref[...] = _adamw_math(g, w_ref[...], m_ref[...], v_ref[...])

    def piece_spec(start, count):
        return pl.BlockSpec((slots, tr, cols), lambda i: (0, jnp.clip(i - start, 0, count - 1), 0))

    blk = pl.BlockSpec((tr, cols), lambda i: (i, 0))
    return _call(body, name=name, grid=(rows // tr,),
                 in_specs=[piece_spec(s, p.shape[1] // tr) for s, p in zip(starts, pieces)] + [blk, blk, blk],
                 out_specs=[blk] * 4, out_shape=[_sds((rows, cols))] * 4, sem=('parallel',),
                 vmem=VMEM_BIG)(*pieces, w, m, v)


def _to_scan_rows(a):
    T, n = a.shape
    return a.reshape(SUBLANES, T // SUBLANES, n).transpose(1, 0, 2).reshape(T, n)


def _from_scan_rows(a):
    T, n = a.shape
    return a.reshape(T // SUBLANES, SUBLANES, n).transpose(1, 0, 2).reshape(T, n)


def _expand(a):
    return jnp.repeat(a, SSM_GROUP, axis=1)


def _block_diag(rows, row_group, col_group):
    r, n = rows.shape
    tiled = jnp.tile(rows, (1, N_GROUPS))
    keep = (jnp.arange(r)[:, None] // row_group) == (jnp.arange(n * N_GROUPS)[None, :] // col_group)
    return jnp.where(keep, tiled, 0.0)


def _block_diag_b(bb):
    return _block_diag(bb.transpose(0, 2, 1).reshape(D_SSM, SSM_STATE), SSM_GROUP, SSM_STATE)


def _block_diag_c(cc):
    return _block_diag(cc.transpose(0, 2, 1).reshape(N_STATE, SSM_GROUP), SSM_STATE, SSM_GROUP)


def _diag_blocks(x, chan_major):
    per = CHAN_BLOCK // SSM_GROUP
    eye = jnp.eye(per, dtype=x.dtype)
    if chan_major:
        x = x.reshape(-1, per, SSM_GROUP, per, SSM_STATE) * eye[None, :, None, :, None]
        return x.sum(axis=1).transpose(0, 2, 3, 1).reshape(N_GROUPS, SSM_STATE, SSM_GROUP)
    x = x.reshape(-1, per, SSM_STATE, per, SSM_GROUP) * eye[None, :, None, :, None]
    return x.sum(axis=3).reshape(N_GROUPS, SSM_STATE, SSM_GROUP)


SMALL_LAYOUT = {
    'ssm_b_re': (0, 0, 32, 1024), 'ssm_b_im': (32, 0, 32, 1024), 'ssm_c_re': (64, 0, 32, 1024),
    'ssm_c_im': (96, 0, 32, 1024), 'b_ada': (128, 0, 6, 1024), 'g_pre_mix': (134, 0, 1, 1024),
    'g_post_mix': (135, 0, 1, 1024), 'ssm_lam_re': (136, 0, 2, 1024), 'ssm_lam_im': (138, 0, 2, 1024),
    'ssm_log_step': (140, 0, 1, 32), 'glu_b': (141, 0, 1, 512), 'g_out_ssm': (141, 512, 1, 512),
    'g_out_conv': (142, 0, 1, 512), 'ssm_d': (142, 512, 1, 512), 'g_pre_ffn': (143, 0, 1, 1024),
    'g_post_ffn': (144, 0, 1, 1024)}
SMALL_ROWS = 152
B_ADA_ROW = SMALL_LAYOUT['b_ada'][0]
LATE_ROWS = {('b_ada', 0): 0, ('b_ada', 1): 1, ('g_pre_mix', 0): 2}


def _adamw_small(gathered, late, wts, mom_m, mom_v):
    names = list(SMALL_LAYOUT)
    n = len(names)

    def body(*refs):
        g_ref, late_ref, ins, outs = refs[0], refs[1], refs[2:2 + 3 * n], refs[2 + 3 * n:]
        for p, name in enumerate(names):
            r0, c0, rows, cols = SMALL_LAYOUT[name]
            pieces = [(0, rows)] if rows % 8 == 0 else [(r, 1) for r in range(rows)]
            for r, cnt in pieces:
                src_ref, first = (late_ref, LATE_ROWS[name, r]) if (name, r) in LATE_ROWS else (g_ref, r0 + r)
                g = src_ref[0, first:first + cnt, c0:c0 + cols]
                for s in range(1, N_DEV):
                    g = g + src_ref[s, first:first + cnt, c0:c0 + cols]
                w, m, v = (ins[3 * p + q][r:r + cnt, :] for q in range(3))
                res = (g,) + _adamw_math(g, w, m, v)
                for q in range(4):
                    outs[4 * p + q][r:r + cnt, :] = res[q]

    shapes = [SMALL_LAYOUT[name][2:] for name in names]
    args = [gathered, late]
    for name, shp in zip(names, shapes):
        args += [wts[name].reshape(shp), mom_m[name].reshape(shp), mom_v[name].reshape(shp)]
    outs = _call(body, name='adamw_small', grid=(1,),
                 in_specs=[_const(gathered.shape), _const(late.shape)]
                 + [_const(shp) for shp in shapes for _ in range(3)],
                 out_specs=[_const(shp) for shp in shapes for _ in range(4)],
                 out_shape=[_sds(shp) for shp in shapes for _ in range(4)], vmem=VMEM_BIG)(*args)
    res = {}
    for p, name in enumerate(names):
        for q, kind in enumerate(('g', 'd', 'm', 'v')):
            res[kind, name] = outs[4 * p + q].reshape(wts[name].shape)
    return res


def kernel(x, c, w_ada, b_ada, g_pre_mix, g_post_mix, w_in, ssm_lam_re, ssm_lam_im, ssm_log_step, ssm_b_re, ssm_b_im, ssm_c_re, ssm_c_im, ssm_d, glu_w, glu_b, g_out_ssm, conv_w, g_out_conv, w_out, g_pre_ffn, g_post_ffn, w_up, ffn_conv_w, w_down, loss_target, m_w_ada, m_b_ada, m_g_pre_mix, m_g_post_mix, m_w_in, m_ssm_lam_re, m_ssm_lam_im, m_ssm_log_step, m_ssm_b_re, m_ssm_b_im, m_ssm_c_re, m_ssm_c_im, m_ssm_d, m_glu_w, m_glu_b, m_g_out_ssm, m_conv_w, m_g_out_conv, m_w_out, m_g_pre_ffn, m_g_post_ffn, m_w_up, m_ffn_conv_w, m_w_down, v_w_ada, v_b_ada, v_g_pre_mix, v_g_post_mix, v_w_in, v_ssm_lam_re, v_ssm_lam_im, v_ssm_log_step, v_ssm_b_re, v_ssm_b_im, v_ssm_c_re, v_ssm_c_im, v_ssm_d, v_glu_w, v_glu_b, v_g_out_ssm, v_conv_w, v_g_out_conv, v_w_out, v_g_pre_ffn, v_g_post_ffn, v_w_up, v_ffn_conv_w, v_w_down):
    args = dict(locals())
    wts = {n: args[n] for n in WEIGHTS}
    mom_m = {n: args['m_' + n] for n in WEIGHTS}
    mom_v = {n: args['v_' + n] for n in WEIGHTS}
    T = x.shape[1]
    tm = min(512, T)
    tw = min(1024, T)
    tk = min(2048, T)
    me = _me()[3]
    xt, tgt = x[0], loss_target[0]

    c_all, w_in_s = _exchange([c, w_in[0].astype(BF16)], name='gather_first', scatter=False)
    c_all = c_all.reshape(N_DEV, D_MODEL)
    b_cols = lax.dynamic_slice(b_ada, (0, me * ADA_SHARD), (1, ADA_SHARD))
    mod_cols, c_act = _mod_cols(c_all, w_ada[0], b_cols)
    (mod_all,) = _exchange([mod_cols], name='gather_mod', scatter=False)
    mod = lax.dynamic_slice(mod_all, (0, me, 0), (N_DEV, 1, ADA_SHARD)).reshape(N_MOD, 1, D_MODEL)
    sh1, sc1, gt1, sh2, sc2, gt2 = [mod[k] for k in range(N_MOD)]


    lre_x, lim_x = _expand(ssm_lam_re[0]), _expand(ssm_lam_im[0])
    lst_x = jnp.broadcast_to(ssm_log_step[0][:, None], (N_GROUPS, SSM_STATE * SSM_GROUP))
    b_re_x = ssm_b_re[0].reshape(N_GROUPS, -1)
    b_im_x = ssm_b_im[0].reshape(N_GROUPS, -1)
    ar_x, ai_x, bbr_x, bbi_x = _ssm_prep(lre_x, lim_x, lst_x, b_re_x, b_im_x)
    lam_r = ar_x[:, ::SSM_GROUP].reshape(1, N_STATE)
    lam_i = ai_x[:, ::SSM_GROUP].reshape(1, N_STATE)
    big_b_re = _block_diag_b(bbr_x.reshape(N_GROUPS, SSM_STATE, SSM_GROUP)).astype(BF16)
    big_b_im = _block_diag_b(bbi_x.reshape(N_GROUPS, SSM_STATE, SSM_GROUP)).astype(BF16)
    big_c_re = _block_diag_c(ssm_c_re[0]).astype(BF16)
    big_c_im = _block_diag_c(ssm_c_im[0]).astype(BF16)
    head = jnp.arange(D_SSM)
    avg16 = jnp.where(head[:, None] // SSM_GROUP == head[None, :] // SSM_GROUP, 1.0 / SSM_GROUP, 0.0).astype(BF16)
    hd = D_CONV // CONV_HEADS
    avg64 = jnp.where(head[:, None] // hd == head[None, :] // hd, 1.0 / hd, 0.0).astype(BF16)

    w_up_t, half = w_up[0].T, D_MODEL // 2
    (proj, h1), (ffn_conv_s, conv_s, w_up_a) = _pre_mix(
        xt, sc1, sh1, g_pre_mix, w_in_s, tw, ([ffn_conv_w[0], conv_w[0], w_up_t[:, :half].astype(BF16)], False))
    cw_full = conv_s.transpose(1, 0, 2).reshape(3, D_CONV)
    u_perm = _to_scan_rows(proj[:, :D_SSM])
    (s_re, s_im, y_perm), (w_up_b, glu_s, w_out_s) = _ssm_fwd(
        u_perm, big_b_re, big_b_im, big_c_re, big_c_im, lam_r, lam_i,
        ([w_up_t[:, half:].astype(BF16), glu_w[0].astype(BF16), w_out[0].astype(BF16)], False))
    glu_full = glu_s.reshape(D_SSM, D_SSM)
    w_out_full = w_out_s.reshape(D_MODEL, D_MODEL)
    yssm = _from_scan_rows(y_perm)
    mix_args = (ssm_d, glu_full, glu_b, g_out_ssm, cw_full, g_out_conv, avg16, avg64)
    ycat = _mix_fwd(yssm, proj, *mix_args, tw)
    o, x1, h2 = _out_proj(ycat, w_out_full, xt, gt1, g_post_mix, g_pre_ffn, sc2, sh2, tw)
    (up8, hid8), (w_down_s,) = _ffn_up(h2, w_up_a, w_up_b, ffn_conv_s, tw, ([w_down[0].astype(BF16)], False))
    wd4 = w_down_s.reshape(4, FF_SHARD, D_MODEL)
    hid4 = hid8.reshape(2, 4, T, FF_SHARD)
    ddn, dx2, loss_parts, d_gt2, d_g_post_ffn = _ffn_down(hid4, wd4, x1, tgt, gt2, g_post_ffn, tm)
    loss_local = jnp.sum(loss_parts[:, 0, 0])

    got = {}
    dhid, g_w_down = _ffn_dact(ddn, wd4, hid4, tw)
    (dup8, dcw_ffn), (got['w_down'],) = _ffn_dup(dhid.reshape(N_DEV, T, FF_SHARD), up8, ffn_conv_s, tw,
                                                 ([g_w_down.reshape(N_DEV, D_FF // N_DEV, D_MODEL)], True))
    g_w_up_halves = _grad_tn(dup8, h2, pl.BlockSpec((None, T, FF_SHARD), lambda g, k: (g, k, 0)),
                             pl.BlockSpec((T, D_MODEL), lambda g, k: (k, 0)), N_DEV, FF_SHARD, D_MODEL, T,
                             'grad_w_up', parts=2)
    (dx1, d_sh2, d_sc2, d_g_pre_ffn, d_o, d_gt1, d_g_post_mix), (got_up_0, got['ffn_conv_w']) = _pre_norm_bwd(
        dup8, pl.BlockSpec((2, tw, FF_SHARD), lambda i, j: (j, i, 0)), [w_up_a, w_up_b], x1, dx2, sc2, g_pre_ffn, tw,
        'ffn_in_bwd', ([g_w_up_halves[0], dcw_ffn], True), below=(o, gt1, g_post_mix), group=2, w_t=True)

    g_w_out = _grad_tn(ycat, d_o, pl.BlockSpec((tk, D_MODEL), lambda g, k: (k, 0)),
                       pl.BlockSpec((tk, D_MODEL), lambda g, k: (k, 0)), 1, D_MODEL, D_MODEL, tk, 'grad_w_out')
    (dy, dconv, dbg, z_b, dlin_b, sums), (got['w_out'],) = _mix_bwd(
        d_o, w_out_full, yssm, proj, *mix_args, tm, ([g_w_out.reshape(N_DEV, D_MODEL // N_DEV, D_MODEL)], True))
    g_glu_w = _grad_tn(z_b, dlin_b, pl.BlockSpec((tk, D_SSM), lambda g, k: (k, 0)),
                       pl.BlockSpec((tk, D_SSM), lambda g, k: (k, 0)), 1, D_SSM, D_SSM, tk, 'grad_glu_w')
    dy_perm = _to_scan_rows(dy)
    (du_perm, dbr_blk, dbi_blk, dcr_blk, dci_blk, dar_blk, dai_blk), (got_up_1, got['glu_w']) = _ssm_bwd(
        dy_perm, u_perm, s_re, s_im, big_b_re, big_b_im, big_c_re, big_c_im, lam_r, lam_i,
        ([g_w_up_halves[1], g_glu_w.reshape(N_DEV, D_SSM // N_DEV, D_SSM)], True))
    du_ssm = _from_scan_rows(du_perm)
    dproj = _mix_bwd_proj(dconv, proj, du_ssm, dy, ssm_d, dbg, cw_full, tw)
    dbb_re = _diag_blocks(dbr_blk, True).reshape(N_GROUPS, -1)
    dbb_im = _diag_blocks(dbi_blk, True).reshape(N_GROUPS, -1)
    d_c_re = _diag_blocks(dcr_blk, False).transpose(0, 2, 1)
    d_c_im = _diag_blocks(dci_blk, False).transpose(0, 2, 1)
    lane = jnp.arange(SSM_STATE * SSM_GROUP)
    seg = jnp.where(lane[:, None] // SSM_GROUP == lane[None, :] // SSM_GROUP, 1.0, 0.0).astype(BF16)
    d_b_re_x, d_b_im_x, d_lre_x, d_lim_x, d_lst = _ssm_prep_bwd(
        lre_x, lim_x, lst_x, b_re_x, b_im_x, dbb_re, dbb_im, _expand(dar_blk.reshape(N_GROUPS, SSM_STATE)),
        _expand(dai_blk.reshape(N_GROUPS, SSM_STATE)), seg)

    row = lambda a: a.reshape(-1, PACK_COLS)
    blank = jnp.zeros((1, PACK_COLS), F32)
    small_pack = jnp.concatenate([
        d_b_re_x, d_b_im_x, row(d_c_re), row(d_c_im), blank, blank, d_gt1, d_sh2, d_sc2, d_gt2, blank,
        d_g_post_mix, row(d_lre_x[:, ::SSM_GROUP]), row(d_lim_x[:, ::SSM_GROUP]),
        jnp.pad(d_lst.reshape(1, N_GROUPS), ((0, 0), (0, PACK_COLS - N_GROUPS))), row(sums[0:4]), d_g_pre_ffn,
        d_g_post_ffn, jnp.zeros((SMALL_ROWS - 145, PACK_COLS), F32)])
    g_w_in, (small_all,) = _grad_w_in(h1, dproj, tk, ([small_pack], False))
    g_conv_slots = jnp.concatenate([sums[4:7], jnp.zeros((5, D_CONV), F32)]).reshape(
        8, N_DEV, D_CONV // N_DEV).transpose(1, 0, 2)
    (grad_x, d_sh1, d_sc1, d_g_pre_mix), (got['w_in'], got['conv_w']) = _pre_norm_bwd(
        dproj, pl.BlockSpec((tw, 4 * IN_SHARD), lambda i, j: (i, j)), [w_in_s], xt, dx1, sc1, g_pre_mix, tw,
        'mix_in_bwd', ([g_w_in, g_conv_slots], True), group=4)
    late_pack = jnp.concatenate([d_sh1, d_sc1, d_g_pre_mix, jnp.full((1, PACK_COLS), loss_local, F32),
                                 jnp.zeros((4, PACK_COLS), F32)])
    (late_all,) = _exchange([late_pack], name='gather_late_grads', scatter=False)
    loss = jnp.sum(late_all[:, 3, 0])
    res = _adamw_small(small_all, late_all, wts, mom_m, mom_v)

    dmod_all = jnp.concatenate([late_all[:, 0:2, :], small_all[:, B_ADA_ROW + 2:B_ADA_ROW + N_MOD, :]],
                               axis=1).reshape(N_DEV, N_MOD * D_MODEL)
    dmod_cols = lax.dynamic_slice(dmod_all, (0, me * ADA_SHARD), (N_DEV, ADA_SHARD))
    g_w_ada = _grad_w_ada(c_act.T, dmod_cols)

    pieces = {n: [slots[:, :3, :] if n in ('conv_w', 'ffn_conv_w') else slots] for n, slots in got.items()}
    for n, parts in pieces.items():
        outs = _adamw(parts, wts[n][0], mom_m[n][0], mom_v[n][0], 'adamw_' + n)
        for kind, val in zip(('g', 'd', 'm', 'v'), outs):
            res[kind, n] = val[None]
    outs = _adamw([got_up_0, got_up_1], w_up[0].T, m_w_up[0].T, v_w_up[0].T, 'adamw_w_up')
    for kind, val in zip(('g', 'd', 'm', 'v'), outs):
        res[kind, 'w_up'] = val.T[None]
    outs = _adamw([g_w_ada[None]], w_ada[0], m_w_ada[0], v_w_ada[0], 'adamw_w_ada')
    for kind, val in zip(('g', 'd', 'm', 'v'), outs):
        res[kind, 'w_ada'] = val[None]

    return (loss, grad_x[None], *[res['g', n] for n in WEIGHTS], *[res['d', n] for n in WEIGHTS],
            *[res['m', n] for n in WEIGHTS], *[res['v', n] for n in WEIGHTS])
```

```python
import math

import jax
import jax.numpy as jnp
from jax import lax
from jax.experimental import pallas as pl
from jax.experimental.pallas import tpu as pltpu

F32, BF16 = jnp.float32, jnp.bfloat16

D_MODEL = 1024
D_SSM = 512
D_CONV = 512
SSM_GROUP = 16
N_GROUPS = 32
SSM_STATE = 64
N_STATE = N_GROUPS * SSM_STATE
CONV_HEADS = 8
D_FF = 2816
N_MOD = 6
D_IN_PROJ = D_SSM + 3 * D_CONV
N_DEV = 8
FF_SHARD = 2 * D_FF // N_DEV
IN_SHARD = D_IN_PROJ // N_DEV
ADA_SHARD = N_MOD * D_MODEL // N_DEV
EPS = 1e-6
LAMBDA_RE_MAX = -1e-4
ADAM_LR, ADAM_B1, ADAM_B2, ADAM_EPS, ADAM_WD, ADAM_STEP = 0.001, 0.9, 0.999, 1e-08, 0.01, 10
GELU_C = math.sqrt(2.0 / math.pi)
GELU_A = 0.044715

SUBLANES = 8
HALO = 8
HALO16 = 16
SCAN_UNROLL = 16
STATE_BLOCK = 512
CHAN_BLOCK = 128
VMEM_BIG = 48 << 20
VMEM_MOST = 58 << 20

WEIGHTS = ['w_ada', 'b_ada', 'g_pre_mix', 'g_post_mix', 'w_in', 'ssm_lam_re', 'ssm_lam_im', 'ssm_log_step',
           'ssm_b_re', 'ssm_b_im', 'ssm_c_re', 'ssm_c_im', 'ssm_d', 'glu_w', 'glu_b', 'g_out_ssm', 'conv_w',
           'g_out_conv', 'w_out', 'g_pre_ffn', 'g_post_ffn', 'w_up', 'ffn_conv_w', 'w_down']
PACK_COLS = 1024


def _call(body, *, name, grid, in_specs, out_specs, out_shape, scratch=(), sem=None, vmem=None, ride=None):
    params = {}
    if vmem is not None:
        params['vmem_limit_bytes'] = vmem
    if ride is None:
        if sem is not None:
            params['dimension_semantics'] = sem
        return pl.pallas_call(body, name=name, grid=grid, in_specs=in_specs, out_specs=out_specs,
                              out_shape=out_shape, scratch_shapes=list(scratch),
                              compiler_params=pltpu.CompilerParams(**params))
    arrs, scatter = ride
    single = not isinstance(out_shape, (list, tuple))
    out_shape_l = [out_shape] if single else list(out_shape)
    out_specs_l = [out_specs] if single else list(out_specs)
    n, n_in, n_out, n_scr = len(arrs), len(in_specs), len(out_shape_l), len(scratch)
    any_spec = pl.BlockSpec(memory_space=pl.ANY)
    params['dimension_semantics'] = ('arbitrary',) * len(grid)

    def carried(*refs):
        ins, rin = refs[:n_in], refs[n_in:n_in + n]
        outs, rout = refs[n_in + n:n_in + n + n_out], refs[n_in + n + n_out:n_in + 2 * n + n_out]
        scr, sems = refs[n_in + 2 * n + n_out:n_in + 2 * n + n_out + n_scr], refs[n_in + 2 * n + n_out + n_scr:]
        first = pl.program_id(0) == 0
        last = pl.program_id(0) == grid[0] - 1
        for ax in range(1, len(grid)):
            first = jnp.logical_and(first, pl.program_id(ax) == 0)
            last = jnp.logical_and(last, pl.program_id(ax) == grid[ax] - 1)

        @pl.when(first)
        def _():
            _exchange_start(rin, rout, sems, scatter)

        body(*ins, *outs, *scr)

        @pl.when(last)
        def _():
            _exchange_wait(rin, rout, sems, scatter)

    call = pl.pallas_call(carried, name=name, grid=grid, in_specs=list(in_specs) + [any_spec] * n,
                          out_specs=out_specs_l + [any_spec] * n,
                          out_shape=out_shape_l + _exchange_shapes(arrs, scatter),
                          scratch_shapes=list(scratch) + _exchange_sems(n),
                          compiler_params=pltpu.CompilerParams(**params))

    def run(*args):
        res = call(*args, *arrs)
        own = res[0] if single else list(res[:n_out])
        return own, list(res[n_out:])

    return run


def _const(shape):
    nd = len(shape)
    return pl.BlockSpec(shape, lambda *_: (0,) * nd)


def _sds(shape, dtype=F32):
    return jax.ShapeDtypeStruct(shape, dtype)


def _dot(a, b):
    return jnp.dot(a, b, preferred_element_type=F32)


def _dot_nt(a, b):
    return lax.dot_general(a, b, (((1,), (1,)), ((), ())), preferred_element_type=F32)


def _dot_tn(a, b):
    return lax.dot_general(a, b, (((0,), (0,)), ((), ())), preferred_element_type=F32)


def _dot_split(x, mat, parts):
    acc = None
    rem = x
    for _ in range(parts):
        piece = rem.astype(BF16)
        rem = rem - piece.astype(F32)
        term = _dot(piece, mat)
        acc = term if acc is None else acc + term
    return acc


def _sigmoid(x):
    return 1.0 / (1.0 + jnp.exp(-x))


def _gelu(x):
    t = jnp.tanh(GELU_C * (x + GELU_A * x * x * x))
    return 0.5 * x * (1.0 + t), t


def _gelu_grad(x, t):
    return 0.5 * (1.0 + t) + 0.5 * x * (1.0 - t * t) * GELU_C * (1.0 + 3.0 * GELU_A * x * x)


def _rsqrt_mean(x):
    return lax.rsqrt(jnp.mean(x * x, axis=-1, keepdims=True) + EPS)


def _colsum(x):
    return jnp.sum(x, axis=0, keepdims=True)


def _shifts_down(x, halo):
    ext = jnp.concatenate([halo, x], axis=0)
    return pltpu.roll(ext, 1, 0)[halo.shape[0]:], pltpu.roll(ext, 2, 0)[halo.shape[0]:]


def _shifts_up(x, halo):
    n = x.shape[0]
    ext = jnp.concatenate([x, halo], axis=0)
    total = ext.shape[0]
    return pltpu.roll(ext, total - 1, 0)[:n], pltpu.roll(ext, total - 2, 0)[:n]


def _conv3(x, halo, w_ref):
    x1, x2 = _shifts_down(x, halo)
    return w_ref[0:1, :] * x2 + w_ref[1:2, :] * x1 + w_ref[2:3, :] * x, x1, x2


def _conv3_t(g, halo, w_ref):
    g1, g2 = _shifts_up(g, halo)
    return w_ref[2:3, :] * g + w_ref[1:2, :] * g1 + w_ref[0:1, :] * g2, g1, g2


def _silu_parts(x):
    s = _sigmoid(x)
    return x * s, s * (1.0 + x * (1.0 - s))


def _norm_bwd(dn, x, r, g):
    gd = g * dn
    return r * gd - x * (r * r * r) * jnp.mean(gd * x, axis=-1, keepdims=True)


def _head_norm_bwd(dn, y, rs, g, avg):
    gd = g * dn
    return rs * gd - y * (rs * rs * rs) * _dot_split(gd * y, avg, 2)


def _me():
    x, y, c = lax.axis_index('x'), lax.axis_index('y'), lax.axis_index('c')
    return x, y, c, 4 * x + 2 * y + c


def _peer(k):
    x, y, c, _ = _me()
    px = 1 - x if k & 4 else x
    py = 1 - y if k & 2 else y
    pc = 1 - c if k & 1 else c
    return (px, py, pc), 4 * px + 2 * py + pc


SIBLING = 1
OTHER_CHIPS = (2, 4, 6)


def _remote(src, dst, sems, a, k, dev):
    return pltpu.make_async_remote_copy(src_ref=src, dst_ref=dst, send_sem=sems[0].at[a, k - 1],
                                        recv_sem=sems[1].at[a, k - 1], device_id=dev,
                                        device_id_type=pl.DeviceIdType.MESH)


def _exchange_copies(ins, outs, sems, scatter):
    me = _me()[3]
    local, first, relay, arrivals = [], [], [], []
    for a in range(len(ins)):
        src = ins[a].at[me] if scatter else ins[a]
        local.append(pltpu.make_async_copy(src, outs[a].at[me], sems[2].at[a]))
        for k in range(1, N_DEV):
            dev, idx = _peer(k)
            landed = _remote(src, outs[a].at[idx], sems, a, k, dev)
            if scatter:
                first.append(_remote(ins[a].at[idx], outs[a].at[me], sems, a, k, dev))
                arrivals.append(landed)
            elif k == SIBLING:
                first.append(_remote(src, outs[a].at[me], sems, a, k, dev))
                arrivals.append(landed)
            elif k in OTHER_CHIPS:
                first.append(_remote(src, outs[a].at[me], sems, a, k, dev))
                sib, _ = _peer(SIBLING)
                relay.append((landed, _remote(outs[a].at[idx], outs[a].at[idx], sems, a, k | SIBLING, sib)))
            else:
                arrivals.append(landed)
    return local, first, relay, arrivals


def _exchange_start(ins, outs, sems, scatter):
    local, first, _, _ = _exchange_copies(ins, outs, sems, scatter)
    for cp in local + first:
        cp.start()


def _exchange_wait(ins, outs, sems, scatter):
    local, first, relay, arrivals = _exchange_copies(ins, outs, sems, scatter)
    for landed, forward in relay:
        landed.wait_recv()
        forward.start()
    for cp in arrivals:
        cp.wait_recv()
    for cp in first + [forward for _, forward in relay]:
        cp.wait_send()
    for cp in local:
        cp.wait()


def _exchange_shapes(arrs, scatter):
    return [_sds(a.shape if scatter else (N_DEV,) + a.shape, a.dtype) for a in arrs]


def _exchange_sems(n):
    return [pltpu.SemaphoreType.DMA((n, N_DEV - 1)), pltpu.SemaphoreType.DMA((n, N_DEV - 1)),
            pltpu.SemaphoreType.DMA((n,))]


def _exchange(arrs, *, name, scatter):
    n = len(arrs)

    def body(*refs):
        _exchange_start(refs[:n], refs[n:2 * n], refs[2 * n:], scatter)
        _exchange_wait(refs[:n], refs[n:2 * n], refs[2 * n:], scatter)

    any_spec = pl.BlockSpec(memory_space=pl.ANY)
    outs = pl.pallas_call(body, name=name, out_shape=_exchange_shapes(arrs, scatter), in_specs=[any_spec] * n,
                          out_specs=[any_spec] * n, scratch_shapes=_exchange_sems(n))(*arrs)
    return list(outs)


def _mod_cols(c_all, w_ada, b_cols):
    def body(c_ref, w_ref, b_ref, mod_ref, act_ref):
        c = c_ref[...]
        act = c * _sigmoid(c)
        act_ref[...] = act
        mod_ref[...] = _dot(act.astype(BF16), w_ref[...].astype(BF16)) + b_ref[...]

    return _call(body, name='mod_cols', grid=(1,),
                 in_specs=[_const(c_all.shape), _const(w_ada.shape), _const(b_cols.shape)],
                 out_specs=[_const((N_DEV, ADA_SHARD)), _const(c_all.shape)],
                 out_shape=[_sds((N_DEV, ADA_SHARD)), _sds(c_all.shape)], vmem=VMEM_BIG)(c_all, w_ada, b_cols)


def _grad_w_ada(act_t, dmod_cols):
    def body(a_ref, d_ref, o_ref):
        o_ref[...] = _dot(a_ref[...], d_ref[...])

    return _call(body, name='grad_w_ada', grid=(1,), in_specs=[_const(act_t.shape), _const(dmod_cols.shape)],
                 out_specs=_const((D_MODEL, ADA_SHARD)), out_shape=_sds((D_MODEL, ADA_SHARD)),
                 vmem=VMEM_BIG)(act_t, dmod_cols)


def _pre_mix(x, sc, sh, g, w_s, tm, ride):
    T = x.shape[0]
    group = 4

    def body(x_ref, sc_ref, sh_ref, g_ref, w_ref, proj_ref, h_ref):
        @pl.when(pl.program_id(1) == 0)
        def _():
            xv = x_ref[...]
            h_ref[...] = ((xv * _rsqrt_mean(xv) * g_ref[...]) * (1.0 + sc_ref[...]) + sh_ref[...]).astype(BF16)

        for s in range(group):
            proj_ref[:, s * IN_SHARD:(s + 1) * IN_SHARD] = _dot(h_ref[...], w_ref[s])

    row = pl.BlockSpec((tm, D_MODEL), lambda i, j: (i, 0))
    vec = _const((1, D_MODEL))
    return _call(body, name='pre_mix', grid=(T // tm, N_DEV // group),
                 in_specs=[row, vec, vec, vec, pl.BlockSpec((group, D_MODEL, IN_SHARD), lambda i, j: (j, 0, 0))],
                 out_specs=[pl.BlockSpec((tm, group * IN_SHARD), lambda i, j: (i, j)), row],
                 out_shape=[_sds((T, D_IN_PROJ)), _sds((T, D_MODEL), BF16)],
                 sem=('parallel', 'arbitrary'), ride=ride)(x, sc, sh, g, w_s)


def _halo_before(tm, rows=HALO):
    return lambda i: jnp.maximum(i * (tm // rows) - 1, 0)


def _halo_after(tm, T, rows=HALO):
    return lambda i: jnp.minimum((i + 1) * (tm // rows), T // rows - 1)


def _mix_fwd(yssm, proj, d, glu_w, glu_b, g_ssm, cw, g_conv, avg16, avg64, tm):
    T = yssm.shape[0]
    hb = _halo_before(tm)

    def body(y_ref, p_ref, ph_ref, d_ref, gw_ref, gb_ref, gs_ref, cw_ref, gc_ref, a16_ref, a64_ref, o_ref):
        i = pl.program_id(0)
        u = p_ref[:, 0:D_SSM]
        y = y_ref[...] + d_ref[...] * u
        z, _ = _gelu(y)
        gate = _sigmoid(_dot(z.astype(BF16), gw_ref[...]) + gb_ref[...])
        ya = z * gate
        rs = lax.rsqrt(_dot_split(ya * ya, a16_ref[...], 2) + EPS)
        o_ref[:, 0:D_SSM] = (ya * rs * gs_ref[...]).astype(BF16)
        bg = p_ref[:, D_SSM:D_SSM + D_CONV]
        cv = p_ref[:, D_SSM + D_CONV:D_SSM + 2 * D_CONV] * p_ref[:, D_SSM + 2 * D_CONV:D_IN_PROJ]
        hv = ph_ref[:, D_SSM + D_CONV:D_SSM + 2 * D_CONV] * ph_ref[:, D_SSM + 2 * D_CONV:D_IN_PROJ]
        hv = jnp.where(i > 0, hv, 0.0)
        conv, _, _ = _conv3(cv, hv, cw_ref)
        yb = bg * conv
        rsb = lax.rsqrt(_dot_split(yb * yb, a64_ref[...], 2) + EPS)
        o_ref[:, D_SSM:D_MODEL] = (yb * rsb * gc_ref[...]).astype(BF16)

    vec = _const((1, D_SSM))
    sq = _const((D_SSM, D_SSM))
    return _call(body, name='mix_fwd', grid=(T // tm,),
                 in_specs=[pl.BlockSpec((tm, D_SSM), lambda i: (i, 0)), pl.BlockSpec((tm, D_IN_PROJ), lambda i: (i, 0)),
                           pl.BlockSpec((HALO, D_IN_PROJ), lambda i: (hb(i), 0)), vec, sq, vec, vec,
                           _const((3, D_CONV)), vec, sq, sq],
                 out_specs=pl.BlockSpec((tm, D_MODEL), lambda i: (i, 0)), out_shape=_sds((T, D_MODEL), BF16),
                 sem=('parallel',), vmem=VMEM_BIG)(yssm, proj, proj, d, glu_w, glu_b, g_ssm, cw, g_conv, avg16, avg64)


def _out_proj(ycat, w_out, x, gt, g_post, g_pre, sc, sh, tm):
    T = x.shape[0]

    def body(y_ref, w_ref, x_ref, gt_ref, gp_ref, g2_ref, sc_ref, sh_ref, o_ref, x1_ref, h_ref):
        o = _dot(y_ref[...], w_ref[...])
        o_ref[...] = o
        x1 = x_ref[...] + gt_ref[...] * (o * _rsqrt_mean(o) * gp_ref[...])
        x1_ref[...] = x1
        h_ref[...] = ((x1 * _rsqrt_mean(x1) * g2_ref[...]) * (1.0 + sc_ref[...]) + sh_ref[...]).astype(BF16)

    row = pl.BlockSpec((tm, D_MODEL), lambda i: (i, 0))
    vec = _const((1, D_MODEL))
    return _call(body, name='out_proj', grid=(T // tm,),
                 in_specs=[row, _const((D_MODEL, D_MODEL)), row, vec, vec, vec, vec, vec],
                 out_specs=[row, row, row],
                 out_shape=[_sds((T, D_MODEL)), _sds((T, D_MODEL)), _sds((T, D_MODEL), BF16)],
                 sem=('parallel',), vmem=VMEM_BIG)(ycat, w_out, x, gt, g_post, g_pre, sc, sh)


def _ffn_up(h2, w_a, w_b, cw8, tm, ride):
    T = h2.shape[0]
    hb = _halo_before(tm, HALO16)
    half = D_MODEL // 2

    def body(h_ref, hh_ref, wa_ref, wb_ref, cw_ref, up_ref, hid_ref):
        def times_w(ref, s):
            return _dot_nt(ref[:, :half], wa_ref[s]) + _dot_nt(ref[:, half:], wb_ref[s])

        for s in range(2):
            up = times_w(h_ref, s)
            up_ref[s] = up.astype(BF16)
            before = jnp.where(pl.program_id(0) > 0, times_w(hh_ref, s), 0.0)
            hid_ref[s] = _conv3(up, before, cw_ref.at[s])[0].astype(BF16)

    out = pl.BlockSpec((2, tm, FF_SHARD), lambda i, j: (j, i, 0))
    return _call(body, name='ffn_up', grid=(T // tm, N_DEV // 2),
                 in_specs=[pl.BlockSpec((tm, D_MODEL), lambda i, j: (i, 0)),
                           pl.BlockSpec((HALO16, D_MODEL), lambda i, j: (hb(i), 0)),
                           pl.BlockSpec((2, FF_SHARD, half), lambda i, j: (j, 0, 0)),
                           pl.BlockSpec((2, FF_SHARD, half), lambda i, j: (j, 0, 0)),
                           pl.BlockSpec((2, 3, FF_SHARD), lambda i, j: (j, 0, 0))],
                 out_specs=[out, out], out_shape=[_sds((N_DEV, T, FF_SHARD), BF16)] * 2,
                 sem=('parallel', 'parallel'), vmem=VMEM_BIG, ride=ride)(h2, h2, w_a, w_b, cw8)


def _ffn_down(hid4, wd4, x1, tgt, gt, g_post, tm):
    T = x1.shape[0]
    nb = T // tm

    def body(a_ref, w_ref, x1_ref, t_ref, gt_ref, g_ref, ddn_ref, dx_ref, loss_ref, dgt_ref, dg_ref, dn_ref):
        i, j = pl.program_id(0), pl.program_id(1)
        part = None
        for s in range(2):
            act = (_silu_parts(a_ref[0, s].astype(F32))[0] * a_ref[1, s].astype(F32)).astype(BF16)
            term = _dot(act, w_ref[s])
            part = term if part is None else part + term

        @pl.when(jnp.logical_and(i == 0, j == 0))
        def _():
            dgt_ref[...] = jnp.zeros_like(dgt_ref)
            dg_ref[...] = jnp.zeros_like(dg_ref)

        @pl.when(j == 0)
        def _():
            dn_ref[...] = part

        @pl.when(j > 0)
        def _():
            dn_ref[...] += part

        @pl.when(j == 1)
        def _():
            dn, gv, gate = dn_ref[...], g_ref[...], gt_ref[...]
            r = _rsqrt_mean(dn)
            normed = dn * r * gv
            err = x1_ref[...] + gate * normed - t_ref[...]
            dx = err * (1.0 / D_MODEL)
            dx_ref[...] = dx
            tot = jnp.sum(jnp.sum(err * err, axis=1, keepdims=True), axis=0, keepdims=True) * (0.5 / D_MODEL)
            loss_ref[...] = jnp.broadcast_to(tot, (8, 128))
            dgt_ref[...] += _colsum(dx * normed)
            dnn = dx * gate
            dg_ref[...] += _colsum(dnn * dn * r)
            ddn_ref[...] = _norm_bwd(dnn, dn, r, gv).astype(BF16)

    row = pl.BlockSpec((tm, D_MODEL), lambda i, j: (i, 0))
    vec = _const((1, D_MODEL))
    return _call(body, name='ffn_down', grid=(nb, 2),
                 in_specs=[pl.BlockSpec((2, 2, tm, FF_SHARD), lambda i, j: (0, j, i, 0)),
                           pl.BlockSpec((2, FF_SHARD, D_MODEL), lambda i, j: (j, 0, 0)), row, row, vec, vec],
                 out_specs=[row, row, pl.BlockSpec((None, 8, 128), lambda i, j: (i, 0, 0)), vec, vec],
                 out_shape=[_sds((T, D_MODEL), BF16), _sds((T, D_MODEL)), _sds((nb, 8, 128)), _sds((1, D_MODEL)),
                            _sds((1, D_MODEL))],
                 scratch=[pltpu.VMEM((tm, D_MODEL), F32)], sem=('arbitrary', 'arbitrary'),
                 vmem=VMEM_BIG)(hid4, wd4, x1, tgt, gt, g_post)


def _ssm_prep(lre, lim, lst, b_re, b_im):
    def body(lre_ref, lim_ref, lst_ref, br_ref, bi_ref, ar_ref, ai_ref, bbr_ref, bbi_ref):
        ar, ai, qr, qi = _zoh(lre_ref[...], lim_ref[...], lst_ref[...])[:4]
        ar_ref[...] = ar
        ai_ref[...] = ai
        bbr_ref[...] = qr * br_ref[...] - qi * bi_ref[...]
        bbi_ref[...] = qr * bi_ref[...] + qi * br_ref[...]

    shp = lre.shape
    return _call(body, name='ssm_prep', grid=(1,), in_specs=[_const(shp)] * 5, out_specs=[_const(shp)] * 4,
                 out_shape=[_sds(shp)] * 4)(lre, lim, lst, b_re, b_im)


def _zoh(lre, lim, lst):
    lr = jnp.minimum(lre, LAMBDA_RE_MAX)
    st = jnp.exp(lst)
    mag = jnp.exp(lr * st)
    ar = mag * jnp.cos(lim * st)
    ai = mag * jnp.sin(lim * st)
    den = lr * lr + lim * lim
    qr = ((ar - 1.0) * lr + ai * lim) / den
    qi = (ai * lr - (ar - 1.0) * lim) / den
    return ar, ai, qr, qi, lr, st, den


def _ssm_prep_bwd(lre, lim, lst, b_re, b_im, dbbr, dbbi, dar, dai, seg):
    def body(lre_ref, lim_ref, lst_ref, br_ref, bi_ref, dbbr_ref, dbbi_ref, dar_ref, dai_ref, seg_ref,
             dbr_ref, dbi_ref, dlre_ref, dlim_ref, dlst_ref):
        lre_v = lre_ref[...]
        li = lim_ref[...]
        ar, ai, qr, qi, lr, st, den = _zoh(lre_v, li, lst_ref[...])
        br, bi, gbr, gbi = br_ref[...], bi_ref[...], dbbr_ref[...], dbbi_ref[...]
        dbr_ref[...] = qr * gbr + qi * gbi
        dbi_ref[...] = qr * gbi - qi * gbr
        gqr = _dot_split(br * gbr + bi * gbi, seg_ref[...], 3)
        gqi = _dot_split(br * gbi - bi * gbr, seg_ref[...], 3)
        ir, ii = lr / den, -li / den
        gar = dar_ref[...] + ir * gqr + ii * gqi
        gai = dai_ref[...] + ir * gqi - ii * gqr
        tr, ti = qr * ir - qi * ii, qr * ii + qi * ir
        glr = -(tr * gqr + ti * gqi)
        gli = -(tr * gqi - ti * gqr)
        gzr = ar * gar + ai * gai
        gzi = ar * gai - ai * gar
        glr = glr + st * gzr
        gli = gli + st * gzi
        gst = (lr * gzr + li * gzi) * st
        dlre_ref[...] = jnp.where(lre_v < LAMBDA_RE_MAX, glr, 0.0)
        dlim_ref[...] = gli
        dlst_ref[...] = jnp.sum(gst, axis=1, keepdims=True) * (1.0 / SSM_GROUP)

    shp = lre.shape
    return _call(body, name='ssm_prep_bwd', grid=(1,), in_specs=[_const(shp)] * 9 + [_const(seg.shape)],
                 out_specs=[_const(shp)] * 4 + [_const((N_GROUPS, 1))],
                 out_shape=[_sds(shp)] * 4 + [_sds((N_GROUPS, 1))], vmem=VMEM_BIG)(
                     lre, lim, lst, b_re, b_im, dbbr, dbbi, dar, dai, seg)


def _scan_specs(T):
    return dict(
        chan=pl.BlockSpec((T, CHAN_BLOCK), lambda cb: (0, cb)),
        state=pl.BlockSpec((T, STATE_BLOCK), lambda cb: (0, cb)),
        b=pl.BlockSpec((CHAN_BLOCK, STATE_BLOCK), lambda cb: (cb, cb)),
        c=pl.BlockSpec((STATE_BLOCK, CHAN_BLOCK), lambda cb: (cb, cb)),
        lam=pl.BlockSpec((1, STATE_BLOCK), lambda cb: (0, cb)),
    )


def _complex_power(re, im, n):
    out = None
    while True:
        if n & 1:
            out = (re, im) if out is None else (out[0] * re - out[1] * im, out[0] * im + out[1] * re)
        n >>= 1
        if n == 0:
            return out
        re, im = re * re - im * im, 2.0 * re * im


def _rows8(i):
    if isinstance(i, int):
        return pl.ds(i * SUBLANES, SUBLANES)
    return pl.ds(pl.multiple_of(i * SUBLANES, SUBLANES), SUBLANES)


def _scan_loop(n_steps, body, init):
    trips = n_steps // SCAN_UNROLL

    def trip(t, carry):
        for u in range(SCAN_UNROLL):
            carry = body(t * SCAN_UNROLL + u, carry)
        return carry

    carry = lax.fori_loop(0, trips, trip, init)
    for step in range(trips * SCAN_UNROLL, n_steps):
        carry = body(step, carry)
    return carry


def _ssm_fwd(u_perm, b_re, b_im, c_re, c_im, lam_r, lam_i, ride):
    T = u_perm.shape[0]
    ls = T // SUBLANES
    rc = min(1024, T)
    sp = _scan_specs(T)

    def body(u_ref, bre_ref, bim_ref, cre_ref, cim_ref, lr_ref, li_ref, so_re_ref, so_im_ref, y_ref, sre_ref, sim_ref):
        for c in range(T // rc):
            rows = pl.ds(c * rc, rc)
            ub = u_ref[rows, :].astype(BF16)
            sre_ref[rows, :] = _dot(ub, bre_ref[...])
            sim_ref[rows, :] = _dot(ub, bim_ref[...])
        shp = (SUBLANES, STATE_BLOCK)
        lr = jnp.broadcast_to(lr_ref[...], shp)
        li = jnp.broadcast_to(li_ref[...], shp)
        zero = jnp.zeros(shp, F32)

        def step(i, carry):
            sr, si = carry
            rows = _rows8(i)
            nr = lr * sr - li * si + sre_ref[rows, :]
            ni = lr * si + li * sr + sim_ref[rows, :]
            sre_ref[rows, :] = nr
            sim_ref[rows, :] = ni
            return nr, ni

        fr, fi = _scan_loop(ls, step, (zero, zero))
        pr, pi_ = _complex_power(lr, li, ls)
        row = lax.broadcasted_iota(jnp.int32, shp, 0)
        ir, ii = zero, zero
        for _ in range(SUBLANES - 1):
            er = fr + pr * ir - pi_ * ii
            ei = fi + pr * ii + pi_ * ir
            ir = jnp.where(row == 0, 0.0, pltpu.roll(er, 1, 0))
            ii = jnp.where(row == 0, 0.0, pltpu.roll(ei, 1, 0))

        def fix(i, carry):
            cr, ci = carry
            rows = _rows8(i)
            nr = lr * cr - li * ci
            ni = lr * ci + li * cr
            sre_ref[rows, :] += nr
            sim_ref[rows, :] += ni
            return nr, ni

        _scan_loop(ls, fix, (ir, ii))
        for c in range(T // rc):
            rows = pl.ds(c * rc, rc)
            s_r, s_i = sre_ref[rows, :].astype(BF16), sim_ref[rows, :].astype(BF16)
            so_re_ref[rows, :] = s_r
            so_im_ref[rows, :] = s_i
            y_ref[rows, :] = _dot(s_r, cre_ref[...]) - _dot(s_i, cim_ref[...])

    return _call(body, name='ssm_fwd', grid=(N_STATE // STATE_BLOCK,),
                 in_specs=[sp['chan'], sp['b'], sp['b'], sp['c'], sp['c'], sp['lam'], sp['lam']],
                 out_specs=[sp['state'], sp['state'], sp['chan']],
                 out_shape=[_sds((T, N_STATE), BF16), _sds((T, N_STATE), BF16), _sds((T, D_SSM))],
                 scratch=[pltpu.VMEM((T, STATE_BLOCK), F32), pltpu.VMEM((T, STATE_BLOCK), F32)],
                 sem=('arbitrary',), vmem=VMEM_MOST, ride=ride)(u_perm, b_re, b_im, c_re, c_im, lam_r, lam_i)


def _ssm_bwd(dy_perm, u_perm, s_re, s_im, b_re, b_im, c_re, c_im, lam_r, lam_i, ride):
    T = u_perm.shape[0]
    ls = T // SUBLANES
    rc = min(1024, T)
    sp = _scan_specs(T)
    ncb = N_STATE // STATE_BLOCK

    def body(dy_ref, u_ref, sre_ref, sim_ref, bre_ref, bim_ref, cre_ref, cim_ref, lr_ref, li_ref,
             du_ref, dbr_ref, dbi_ref, dcr_ref, dci_ref, dar_ref, dai_ref, gre_ref, gim_ref):
        shp = (SUBLANES, STATE_BLOCK)
        zero = jnp.zeros(shp, F32)
        tail = pl.ds(T, SUBLANES)
        gre_ref[tail, :] = zero
        gim_ref[tail, :] = zero
        for c in range(T // rc):
            rows = pl.ds(c * rc, rc)
            dyb = dy_ref[rows, :].astype(BF16)
            gre_ref[rows, :] = _dot_nt(dyb, cre_ref[...])
            gim_ref[rows, :] = -_dot_nt(dyb, cim_ref[...])
        lr = jnp.broadcast_to(lr_ref[...], shp)
        li = jnp.broadcast_to(li_ref[...], shp)

        def step(k, carry):
            gr, gi = carry
            rows = _rows8(ls - 1 - k)
            nr = lr * gr + li * gi + gre_ref[rows, :]
            ni = lr * gi - li * gr + gim_ref[rows, :]
            gre_ref[rows, :] = nr
            gim_ref[rows, :] = ni
            return nr, ni

        fr, fi = _scan_loop(ls, step, (zero, zero))
        pr, pi_ = _complex_power(lr, -li, ls)
        row = lax.broadcasted_iota(jnp.int32, shp, 0)
        cr, ci = zero, zero
        for _ in range(SUBLANES - 1):
            er = fr + pr * cr - pi_ * ci
            ei = fi + pr * ci + pi_ * cr
            cr = jnp.where(row == SUBLANES - 1, 0.0, pltpu.roll(er, SUBLANES - 1, 0))
            ci = jnp.where(row == SUBLANES - 1, 0.0, pltpu.roll(ei, SUBLANES - 1, 0))

        def fix(k, carry):
            dr, di = carry
            rows = _rows8(ls - 1 - k)
            dr, di = lr * dr + li * di, lr * di - li * dr
            gre_ref[rows, :] += dr
            gim_ref[rows, :] += di
            return dr, di

        _scan_loop(ls, fix, (cr, ci))

        acc_r = jnp.zeros((1, STATE_BLOCK), F32)
        acc_i = jnp.zeros((1, STATE_BLOCK), F32)
        for c in range(T // rc):
            rows, nxt = pl.ds(c * rc, rc), pl.ds(c * rc + SUBLANES, rc)
            s_r, s_i = sre_ref[rows, :].astype(F32), sim_ref[rows, :].astype(F32)
            g_r, g_i = gre_ref[nxt, :], gim_ref[nxt, :]
            acc_r = acc_r + _colsum(g_r * s_r + g_i * s_i)
            acc_i = acc_i + _colsum(g_i * s_r - g_r * s_i)
        last = pl.ds(T - 2 * SUBLANES, 2 * SUBLANES)
        first = pl.ds(0, SUBLANES)
        spr = jnp.where(row == 0, 0.0, pltpu.roll(sre_ref[last, :].astype(F32)[SUBLANES:], 1, 0))
        spi = jnp.where(row == 0, 0.0, pltpu.roll(sim_ref[last, :].astype(F32)[SUBLANES:], 1, 0))
        gr, gi = gre_ref[first, :], gim_ref[first, :]
        dar_ref[...] = acc_r + _colsum(gr * spr + gi * spi)
        dai_ref[...] = acc_i + _colsum(gi * spr - gr * spi)

        for c in range(T // rc):
            rows = pl.ds(c * rc, rc)
            g_r, g_i = gre_ref[rows, :].astype(BF16), gim_ref[rows, :].astype(BF16)
            s_r, s_i = sre_ref[rows, :], sim_ref[rows, :]
            ub, dyb = u_ref[rows, :].astype(BF16), dy_ref[rows, :].astype(BF16)
            du_ref[rows, :] = _dot_nt(g_r, bre_ref[...]) + _dot_nt(g_i, bim_ref[...])
            parts = (_dot_tn(ub, g_r), _dot_tn(ub, g_i), _dot_tn(s_r, dyb), -_dot_tn(s_i, dyb))
            outs = (dbr_ref, dbi_ref, dcr_ref, dci_ref)
            for o_ref, part in zip(outs, parts):
                if c == 0:
                    o_ref[...] = part
                else:
                    o_ref[...] += part

    blk = lambda r, c: pl.BlockSpec((None, r, c), lambda cb: (cb, 0, 0))
    return _call(body, name='ssm_bwd', grid=(ncb,),
                 in_specs=[sp['chan'], sp['chan'], sp['state'], sp['state'], sp['b'], sp['b'], sp['c'], sp['c'],
                           sp['lam'], sp['lam']],
                 out_specs=[sp['chan'], blk(CHAN_BLOCK, STATE_BLOCK), blk(CHAN_BLOCK, STATE_BLOCK),
                            blk(STATE_BLOCK, CHAN_BLOCK), blk(STATE_BLOCK, CHAN_BLOCK), blk(1, STATE_BLOCK),
                            blk(1, STATE_BLOCK)],
                 out_shape=[_sds((T, D_SSM)), _sds((ncb, CHAN_BLOCK, STATE_BLOCK)), _sds((ncb, CHAN_BLOCK, STATE_BLOCK)),
                            _sds((ncb, STATE_BLOCK, CHAN_BLOCK)), _sds((ncb, STATE_BLOCK, CHAN_BLOCK)),
                            _sds((ncb, 1, STATE_BLOCK)), _sds((ncb, 1, STATE_BLOCK))],
                 scratch=[pltpu.VMEM((T + SUBLANES, STATE_BLOCK), F32), pltpu.VMEM((T + SUBLANES, STATE_BLOCK), F32)],
                 sem=('arbitrary',), vmem=VMEM_MOST, ride=ride)(dy_perm, u_perm, s_re, s_im, b_re, b_im, c_re, c_im,
                                                                lam_r, lam_i)


def _ffn_dact(ddn, wd4, hid4, tm):
    T = ddn.shape[0]
    nb = T // tm

    def body(d_ref, w_ref, hid_ref, o_ref, gw_ref, acc_ref):
        i = pl.program_id(1)
        d = d_ref[...]
        dact = _dot_nt(d, w_ref[...])
        silu, dsilu = _silu_parts(hid_ref[0].astype(F32))
        hid_v = hid_ref[1].astype(F32)
        o_ref[0] = (dact * hid_v * dsilu).astype(BF16)
        o_ref[1] = (dact * silu).astype(BF16)
        part = _dot_tn((silu * hid_v).astype(BF16), d)

        @pl.when(i == 0)
        def _():
            acc_ref[...] = part

        @pl.when(i > 0)
        def _():
            acc_ref[...] += part

        @pl.when(i == nb - 1)
        def _():
            gw_ref[...] = acc_ref[...].astype(BF16)

    blk = pl.BlockSpec((2, None, tm, FF_SHARD), lambda j, i: (0, j, i, 0))
    w_blk = pl.BlockSpec((None, FF_SHARD, D_MODEL), lambda j, i: (j, 0, 0))
    return _call(body, name='ffn_dact', grid=(4, nb),
                 in_specs=[pl.BlockSpec((tm, D_MODEL), lambda j, i: (i, 0)), w_blk, blk],
                 out_specs=[blk, w_blk],
                 out_shape=[_sds((2, 4, T, FF_SHARD), BF16), _sds((4, FF_SHARD, D_MODEL), BF16)],
                 scratch=[pltpu.VMEM((FF_SHARD, D_MODEL), F32)], sem=('parallel', 'arbitrary'),
                 vmem=VMEM_BIG)(ddn, wd4, hid4)


def _ffn_dup(dhid8, up8, cw8, tm, ride):
    T = up8.shape[1]
    nb = T // tm
    ha = _halo_after(tm, T, HALO16)

    def body(dh_ref, dha_ref, up_ref, cw_ref, dup_ref, dcw_ref):
        i = pl.program_id(1)

        @pl.when(i == 0)
        def _():
            dcw_ref[...] = jnp.zeros_like(dcw_ref)

        dh = dh_ref[...].astype(F32)
        dup, dh1, dh2 = _conv3_t(dh, jnp.where(i < nb - 1, dha_ref[...].astype(F32), 0.0), cw_ref)
        dup_ref[...] = dup.astype(BF16)
        up = up_ref[...].astype(F32)
        dcw_ref[0:1, :] += _colsum(dh2 * up)
        dcw_ref[1:2, :] += _colsum(dh1 * up)
        dcw_ref[2:3, :] += _colsum(dh * up)

    main = pl.BlockSpec((None, tm, FF_SHARD), lambda j, i: (j, i, 0))
    return _call(body, name='ffn_dup', grid=(N_DEV, nb),
                 in_specs=[main, pl.BlockSpec((None, HALO16, FF_SHARD), lambda j, i: (j, ha(i), 0)), main,
                           pl.BlockSpec((None, 3, FF_SHARD), lambda j, i: (j, 0, 0))],
                 out_specs=[main, pl.BlockSpec((None, 8, FF_SHARD), lambda j, i: (j, 0, 0))],
                 out_shape=[_sds((N_DEV, T, FF_SHARD), BF16), _sds((N_DEV, 8, FF_SHARD))],
                 sem=('parallel', 'arbitrary'), vmem=VMEM_BIG, ride=ride)(dhid8, dhid8, up8, cw8)


def _grad_tn(a, b, a_spec, b_spec, groups, m, n, tk, name, ride=None, parts=1):
    T = a.shape[-2]
    nk = T // tk
    mp = m // parts

    def body(a_ref, b_ref, *refs):
        o_refs, acc_ref = refs[:parts], refs[parts]
        k = pl.program_id(1)
        part = _dot_tn(a_ref[...], b_ref[...])

        @pl.when(k == 0)
        def _():
            acc_ref[...] = part

        @pl.when(k > 0)
        def _():
            acc_ref[...] += part

        @pl.when(k == nk - 1)
        def _():
            for p, o_ref in enumerate(o_refs):
                o_ref[...] = acc_ref[p * mp:(p + 1) * mp, :].astype(BF16)

    out_spec = pl.BlockSpec((None, mp, n), lambda g, k: (g, 0, 0))
    res = _call(body, name=name, grid=(groups, nk), in_specs=[a_spec, b_spec], out_specs=[out_spec] * parts,
                out_shape=[_sds((groups, mp, n), BF16)] * parts, scratch=[pltpu.VMEM((m, n), F32)],
                sem=('parallel', 'arbitrary'), vmem=VMEM_BIG, ride=ride)(a, b)
    if parts > 1:
        return res
    return res[0] if ride is None else (res[0][0], res[1])


def _grad_w_in(h1, dproj, tk, ride):
    T = h1.shape[0]
    nk = T // tk
    half = D_IN_PROJ // 2

    def body(a_ref, b_ref, o_ref, acc_ref):
        k = pl.program_id(0)
        for h in range(2):
            cols = slice(h * half, (h + 1) * half)
            part = _dot_tn(a_ref[...], b_ref[:, cols])

            @pl.when(k == 0)
            def _():
                acc_ref[:, cols] = part

            @pl.when(k > 0)
            def _():
                acc_ref[:, cols] += part

        @pl.when(k == nk - 1)
        def _():
            for g in range(N_DEV):
                o_ref[g] = acc_ref[:, g * IN_SHARD:(g + 1) * IN_SHARD].astype(BF16)

    return _call(body, name='grad_w_in', grid=(nk,),
                 in_specs=[pl.BlockSpec((tk, D_MODEL), lambda k: (k, 0)), pl.BlockSpec((tk, D_IN_PROJ), lambda k: (k, 0))],
                 out_specs=_const((N_DEV, D_MODEL, IN_SHARD)), out_shape=_sds((N_DEV, D_MODEL, IN_SHARD), BF16),
                 scratch=[pltpu.VMEM((D_MODEL, D_IN_PROJ), F32)], sem=('arbitrary',), vmem=VMEM_BIG, ride=ride)(h1, dproj)


def _pre_norm_bwd(dz, dz_spec, w_parts, xin, dres, sc, g, tm, name, ride, below=None, group=1, w_t=False):
    T = xin.shape[0]
    n = w_parts[0].shape[1] if w_t else w_parts[0].shape[2]
    mul = _dot if w_t else _dot_nt
    steps = N_DEV // group
    width = D_MODEL // len(w_parts)

    def body(dz_ref, *refs):
        w_refs, (x_ref, dr_ref, sc_ref, g_ref), refs = refs[:len(w_parts)], refs[len(w_parts):len(w_parts) + 4], \
            refs[len(w_parts) + 4:]
        if below is None:
            dx_ref, dsh_ref, dsc_ref, dg_ref = refs
            sums = (dsh_ref, dsc_ref, dg_ref)
        else:
            v_ref, gate_ref, g2_ref, dx_ref, dsh_ref, dsc_ref, dg_ref, dv_ref, dgate_ref, dg2_ref = refs
            sums = (dsh_ref, dsc_ref, dg_ref, dgate_ref, dg2_ref)
        i, j = pl.program_id(0), pl.program_id(1)
        piece = (lambda s: dz_ref[s]) if dz.ndim == 3 else (lambda s: dz_ref[:, s * n:(s + 1) * n])
        parts = []
        for w_ref in w_refs:
            part = mul(piece(0), w_ref[0])
            for s in range(1, group):
                part = part + mul(piece(s), w_ref[s])
            parts.append(part)

        @pl.when(jnp.logical_and(i == 0, j == 0))
        def _():
            for s_ref in sums:
                s_ref[...] = jnp.zeros_like(s_ref)

        @pl.when(j == 0)
        def _():
            for k, part in enumerate(parts):
                dx_ref[:, k * width:(k + 1) * width] = part

        @pl.when(j > 0)
        def _():
            for k, part in enumerate(parts):
                dx_ref[:, k * width:(k + 1) * width] += part

        @pl.when(j == steps - 1)
        def _():
            dh, xv, gv = dx_ref[...], x_ref[...], g_ref[...]
            r = _rsqrt_mean(xv)
            dsh_ref[...] += _colsum(dh)
            dsc_ref[...] += _colsum(dh * (xv * r * gv))
            dxn = dh * (1.0 + sc_ref[...])
            dg_ref[...] += _colsum(dxn * xv * r)
            dx = dr_ref[...] + _norm_bwd(dxn, xv, r, gv)
            dx_ref[...] = dx
            if below is not None:
                v, g2 = v_ref[...], g2_ref[...]
                rv = _rsqrt_mean(v)
                dgate_ref[...] += _colsum(dx * (v * rv * g2))
                dn = dx * gate_ref[...]
                dg2_ref[...] += _colsum(dn * v * rv)
                dv_ref[...] = _norm_bwd(dn, v, rv, g2).astype(BF16)

    row = pl.BlockSpec((tm, D_MODEL), lambda i, j: (i, 0))
    vec = _const((1, D_MODEL))
    in_specs = [dz_spec] + [pl.BlockSpec((group,) + w.shape[1:], lambda i, j: (j, 0, 0)) for w in w_parts]
    in_specs += [row, row, vec, vec]
    out_specs = [row, vec, vec, vec]
    out_shape = [_sds((T, D_MODEL)), _sds((1, D_MODEL)), _sds((1, D_MODEL)), _sds((1, D_MODEL))]
    args = [dz, *w_parts, xin, dres, sc, g]
    if below is not None:
        in_specs += [row, vec, vec]
        out_specs += [row, vec, vec]
        out_shape += [_sds((T, D_MODEL), BF16), _sds((1, D_MODEL)), _sds((1, D_MODEL))]
        args += list(below)
    return _call(body, name=name, grid=(T // tm, steps), in_specs=in_specs, out_specs=out_specs,
                 out_shape=out_shape, sem=('arbitrary', 'arbitrary'), vmem=VMEM_MOST, ride=ride)(*args)


def _mix_bwd(d_o, w_out, yssm, proj, d, glu_w, glu_b, g_ssm, cw, g_conv, avg16, avg64, tm, ride):
    T = yssm.shape[0]
    hb = _halo_before(tm)

    def body(do_ref, wo_ref, y_ref, p_ref, ph_ref, d_ref, gw_ref, gb_ref, gs_ref, cw_ref, gc_ref, a16_ref, a64_ref,
             dy_ref, dconv_ref, dbg_ref, z_ref, dlin_ref, acc_ref):
        i = pl.program_id(0)
        dyc = _dot_nt(do_ref[...], wo_ref[...])

        @pl.when(i == 0)
        def _():
            acc_ref[...] = jnp.zeros_like(acc_ref)

        u = p_ref[:, 0:D_SSM]
        y = y_ref[...] + d_ref[...] * u
        z, t = _gelu(y)
        gate = _sigmoid(_dot(z.astype(BF16), gw_ref[...]) + gb_ref[...])
        ya = z * gate
        rs = lax.rsqrt(_dot_split(ya * ya, a16_ref[...], 2) + EPS)
        dna = dyc[:, 0:D_SSM]
        acc_ref[1:2, :] += _colsum(dna * ya * rs)
        dya = _head_norm_bwd(dna, ya, rs, gs_ref[...], a16_ref[...])
        dlin = dya * z * gate * (1.0 - gate)
        acc_ref[0:1, :] += _colsum(dlin)
        dlin_b = dlin.astype(BF16)
        dz = dya * gate + _dot_nt(dlin_b, gw_ref[...])
        dy = dz * _gelu_grad(y, t)
        acc_ref[3:4, :] += _colsum(dy * u)
        dy_ref[...] = dy
        z_ref[...] = z.astype(BF16)
        dlin_ref[...] = dlin_b

        bg = p_ref[:, D_SSM:D_SSM + D_CONV]
        cv = p_ref[:, D_SSM + D_CONV:D_SSM + 2 * D_CONV] * p_ref[:, D_SSM + 2 * D_CONV:D_IN_PROJ]
        hv = ph_ref[:, D_SSM + D_CONV:D_SSM + 2 * D_CONV] * ph_ref[:, D_SSM + 2 * D_CONV:D_IN_PROJ]
        hv = jnp.where(i > 0, hv, 0.0)
        conv, cv1, cv2 = _conv3(cv, hv, cw_ref)
        yb = bg * conv
        rsb = lax.rsqrt(_dot_split(yb * yb, a64_ref[...], 2) + EPS)
        dnb = dyc[:, D_SSM:D_MODEL]
        acc_ref[2:3, :] += _colsum(dnb * yb * rsb)
        dyb = _head_norm_bwd(dnb, yb, rsb, gc_ref[...], a64_ref[...])
        dbg_ref[...] = dyb * conv
        dconv = dyb * bg
        dconv_ref[...] = dconv
        acc_ref[4:5, :] += _colsum(dconv * cv2)
        acc_ref[5:6, :] += _colsum(dconv * cv1)
        acc_ref[6:7, :] += _colsum(dconv * cv)

    vec = _const((1, D_SSM))
    sq = _const((D_SSM, D_SSM))
    half = pl.BlockSpec((tm, D_SSM), lambda i: (i, 0))
    return _call(body, name='mix_bwd', grid=(T // tm,),
                 in_specs=[pl.BlockSpec((tm, D_MODEL), lambda i: (i, 0)), _const((D_MODEL, D_MODEL)), half,
                           pl.BlockSpec((tm, D_IN_PROJ), lambda i: (i, 0)),
                           pl.BlockSpec((HALO, D_IN_PROJ), lambda i: (hb(i), 0)), vec, sq, vec, vec,
                           _const((3, D_CONV)), vec, sq, sq],
                 out_specs=[half, half, half, half, half, _const((8, D_SSM))],
                 out_shape=[_sds((T, D_SSM)), _sds((T, D_SSM)), _sds((T, D_SSM)), _sds((T, D_SSM), BF16),
                            _sds((T, D_SSM), BF16), _sds((8, D_SSM))],
                 sem=('arbitrary',), vmem=VMEM_BIG, ride=ride)(d_o, w_out, yssm, proj, proj, d, glu_w, glu_b, g_ssm, cw,
                                                              g_conv, avg16, avg64)


def _mix_bwd_proj(dconv, proj, du_ssm, dy, d, dbg, cw, tm):
    T = dy.shape[0]
    nb = T // tm
    ha = _halo_after(tm, T)

    def body(dc_ref, dch_ref, cg_ref, v_ref, du_ref, dy_ref, d_ref, dbg_ref, cw_ref, o_ref):
        i = pl.program_id(0)
        dcv = _conv3_t(dc_ref[...], jnp.where(i < nb - 1, dch_ref[...], 0.0), cw_ref)[0]
        o_ref[:, 0:D_SSM] = (du_ref[...] + dy_ref[...] * d_ref[...]).astype(BF16)
        o_ref[:, D_SSM:D_SSM + D_CONV] = dbg_ref[...].astype(BF16)
        o_ref[:, D_SSM + D_CONV:D_SSM + 2 * D_CONV] = (dcv * v_ref[...]).astype(BF16)
        o_ref[:, D_SSM + 2 * D_CONV:D_IN_PROJ] = (dcv * cg_ref[...]).astype(BF16)

    half = pl.BlockSpec((tm, D_SSM), lambda i: (i, 0))
    return _call(body, name='mix_bwd_proj', grid=(nb,),
                 in_specs=[half, pl.BlockSpec((HALO, D_CONV), lambda i: (ha(i), 0)),
                           pl.BlockSpec((tm, D_CONV), lambda i: (i, 2)), pl.BlockSpec((tm, D_CONV), lambda i: (i, 3)),
                           half, half, _const((1, D_SSM)), half, _const((3, D_CONV))],
                 out_specs=pl.BlockSpec((tm, D_IN_PROJ), lambda i: (i, 0)), out_shape=_sds((T, D_IN_PROJ), BF16),
                 sem=('parallel',), vmem=VMEM_BIG)(dconv, dconv, proj, proj, du_ssm, dy, d, dbg, cw)


ADAMW_SLOT_BYTES = 8 << 20
ADAMW_ROW_BYTES = 3 << 19


def _row_tile(rows, cols, slots):
    for cand in range(rows, 15, -1):
        if (rows % cand == 0 and cand % 16 == 0 and slots * cand * cols * 4 <= ADAMW_SLOT_BYTES
                and cand * cols * 4 <= ADAMW_ROW_BYTES):
            return cand
    return rows


def _adamw_math(g, w, m, v):
    m2 = ADAM_B1 * m + (1.0 - ADAM_B1) * g
    v2 = ADAM_B2 * v + (1.0 - ADAM_B2) * (g * g)
    m_hat = m2 / (1.0 - ADAM_B1 ** ADAM_STEP)
    v_hat = v2 / (1.0 - ADAM_B2 ** ADAM_STEP)
    return -ADAM_LR * (m_hat / (jnp.sqrt(v_hat) + ADAM_EPS) + ADAM_WD * w), m2, v2


def _adamw(pieces, w, m, v, name):
    slots, _, cols = pieces[0].shape
    rows = sum(p.shape[1] for p in pieces)
    tr = _row_tile(pieces[0].shape[1], cols, slots)
    starts, pos = [], 0
    for p in pieces:
        assert p.shape[1] % tr == 0
        starts.append(pos)
        pos += p.shape[1] // tr

    def body(*refs):
        g_refs = refs[:len(pieces)]
        w_ref, m_ref, v_ref, go_ref, d_ref, mo_ref, vo_ref = refs[len(pieces):]
        i = pl.program_id(0)
        g = None
        for g_ref, start in zip(g_refs, starts):
            part = g_ref[0].astype(F32)
            for s in range(1, slots):
                part = part + g_ref[s].astype(F32)
            g = part if g is None else jnp.where(i >= start, part, g)
        go_ref[...] = g
        d_ref[...], mo_ref[...], vo_ref[...] = _adamw_math(g, w_ref[...], m_ref[...], v_ref[...])

    def piece_spec(start, count):
        return pl.BlockSpec((slots, tr, cols), lambda i: (0, jnp.clip(i - start, 0, count - 1), 0))

    blk = pl.BlockSpec((tr, cols), lambda i: (i, 0))
    return _call(body, name=name, grid=(rows // tr,),
                 in_specs=[piece_spec(s, p.shape[1] // tr) for s, p in zip(starts, pieces)] + [blk, blk, blk],
                 out_specs=[blk] * 4, out_shape=[_sds((rows, cols))] * 4, sem=('parallel',),
                 vmem=VMEM_BIG)(*pieces, w, m, v)


def _to_scan_rows(a):
    T, n = a.shape
    return a.reshape(SUBLANES, T // SUBLANES, n).transpose(1, 0, 2).reshape(T, n)


def _from_scan_rows(a):
    T, n = a.shape
    return a.reshape(T // SUBLANES, SUBLANES, n).transpose(1, 0, 2).reshape(T, n)


def _expand(a):
    return jnp.repeat(a, SSM_GROUP, axis=1)


def _block_diag(rows, row_group, col_group):
    r, n = rows.shape
    tiled = jnp.tile(rows, (1, N_GROUPS))
    keep = (jnp.arange(r)[:, None] // row_group) == (jnp.arange(n * N_GROUPS)[None, :] // col_group)
    return jnp.where(keep, tiled, 0.0)


def _block_diag_b(bb):
    return _block_diag(bb.transpose(0, 2, 1).reshape(D_SSM, SSM_STATE), SSM_GROUP, SSM_STATE)


def _block_diag_c(cc):
    return _block_diag(cc.transpose(0, 2, 1).reshape(N_STATE, SSM_GROUP), SSM_STATE, SSM_GROUP)


def _diag_blocks(x, chan_major):
    per = CHAN_BLOCK // SSM_GROUP
    eye = jnp.eye(per, dtype=x.dtype)
    if chan_major:
        x = x.reshape(-1, per, SSM_GROUP, per, SSM_STATE) * eye[None, :, None, :, None]
        return x.sum(axis=1).transpose(0, 2, 3, 1).reshape(N_GROUPS, SSM_STATE, SSM_GROUP)
    x = x.reshape(-1, per, SSM_STATE, per, SSM_GROUP) * eye[None, :, None, :, None]
    return x.sum(axis=3).reshape(N_GROUPS, SSM_STATE, SSM_GROUP)


SMALL_LAYOUT = {
    'ssm_b_re': (0, 0, 32, 1024), 'ssm_b_im': (32, 0, 32, 1024), 'ssm_c_re': (64, 0, 32, 1024),
    'ssm_c_im': (96, 0, 32, 1024), 'b_ada': (128, 0, 6, 1024), 'g_pre_mix': (134, 0, 1, 1024),
    'g_post_mix': (135, 0, 1, 1024), 'ssm_lam_re': (136, 0, 2, 1024), 'ssm_lam_im': (138, 0, 2, 1024),
    'ssm_log_step': (140, 0, 1, 32), 'glu_b': (141, 0, 1, 512), 'g_out_ssm': (141, 512, 1, 512),
    'g_out_conv': (142, 0, 1, 512), 'ssm_d': (142, 512, 1, 512), 'g_pre_ffn': (143, 0, 1, 1024),
    'g_post_ffn': (144, 0, 1, 1024)}
SMALL_ROWS = 152
B_ADA_ROW = SMALL_LAYOUT['b_ada'][0]
LATE_ROWS = {('b_ada', 0): 0, ('b_ada', 1): 1, ('g_pre_mix', 0): 2}


def _adamw_small(gathered, late, wts, mom_m, mom_v):
    names = list(SMALL_LAYOUT)
    n = len(names)

    def body(*refs):
        g_ref, late_ref, ins, outs = refs[0], refs[1], refs[2:2 + 3 * n], refs[2 + 3 * n:]
        for p, name in enumerate(names):
            r0, c0, rows, cols = SMALL_LAYOUT[name]
            pieces = [(0, rows)] if rows % 8 == 0 else [(r, 1) for r in range(rows)]
            for r, cnt in pieces:
                src_ref, first = (late_ref, LATE_ROWS[name, r]) if (name, r) in LATE_ROWS else (g_ref, r0 + r)
                g = src_ref[0, first:first + cnt, c0:c0 + cols]
                for s in range(1, N_DEV):
                    g = g + src_ref[s, first:first + cnt, c0:c0 + cols]
                w, m, v = (ins[3 * p + q][r:r + cnt, :] for q in range(3))
                res = (g,) + _adamw_math(g, w, m, v)
                for q in range(4):
                    outs[4 * p + q][r:r + cnt, :] = res[q]

    shapes = [SMALL_LAYOUT[name][2:] for name in names]
    args = [gathered, late]
    for name, shp in zip(names, shapes):
        args += [wts[name].reshape(shp), mom_m[name].reshape(shp), mom_v[name].reshape(shp)]
    outs = _call(body, name='adamw_small', grid=(1,),
                 in_specs=[_const(gathered.shape), _const(late.shape)]
                 + [_const(shp) for shp in shapes for _ in range(3)],
                 out_specs=[_const(shp) for shp in shapes for _ in range(4)],
                 out_shape=[_sds(shp) for shp in shapes for _ in range(4)], vmem=VMEM_BIG)(*args)
    res = {}
    for p, name in enumerate(names):
        for q, kind in enumerate(('g', 'd', 'm', 'v')):
            res[kind, name] = outs[4 * p + q].reshape(wts[name].shape)
    return res


def kernel(x, c, w_ada, b_ada, g_pre_mix, g_post_mix, w_in, ssm_lam_re, ssm_lam_im, ssm_log_step, ssm_b_re, ssm_b_im, ssm_c_re, ssm_c_im, ssm_d, glu_w, glu_b, g_out_ssm, conv_w, g_out_conv, w_out, g_pre_ffn, g_post_ffn, w_up, ffn_conv_w, w_down, loss_target, m_w_ada, m_b_ada, m_g_pre_mix, m_g_post_mix, m_w_in, m_ssm_lam_re, m_ssm_lam_im, m_ssm_log_step, m_ssm_b_re, m_ssm_b_im, m_ssm_c_re, m_ssm_c_im, m_ssm_d, m_glu_w, m_glu_b, m_g_out_ssm, m_conv_w, m_g_out_conv, m_w_out, m_g_pre_ffn, m_g_post_ffn, m_w_up, m_ffn_conv_w, m_w_down, v_w_ada, v_b_ada, v_g_pre_mix, v_g_post_mix, v_w_in, v_ssm_lam_re, v_ssm_lam_im, v_ssm_log_step, v_ssm_b_re, v_ssm_b_im, v_ssm_c_re, v_ssm_c_im, v_ssm_d, v_glu_w, v_glu_b, v_g_out_ssm, v_conv_w, v_g_out_conv, v_w_out, v_g_pre_ffn, v_g_post_ffn, v_w_up, v_ffn_conv_w, v_w_down):
    args = dict(locals())
    wts = {n: args[n] for n in WEIGHTS}
    mom_m = {n: args['m_' + n] for n in WEIGHTS}
    mom_v = {n: args['v_' + n] for n in WEIGHTS}
    T = x.shape[1]
    tm = min(512, T)
    tw = min(1024, T)
    tk = min(2048, T)
    me = _me()[3]
    xt, tgt = x[0], loss_target[0]

    c_all, w_in_s = _exchange([c, w_in[0].astype(BF16)], name='gather_first', scatter=False)
    c_all = c_all.reshape(N_DEV, D_MODEL)
    b_cols = lax.dynamic_slice(b_ada, (0, me * ADA_SHARD), (1, ADA_SHARD))
    mod_cols, c_act = _mod_cols(c_all, w_ada[0], b_cols)
    (mod_all,) = _exchange([mod_cols], name='gather_mod', scatter=False)
    mod = lax.dynamic_slice(mod_all, (0, me, 0), (N_DEV, 1, ADA_SHARD)).reshape(N_MOD, 1, D_MODEL)
    sh1, sc1, gt1, sh2, sc2, gt2 = [mod[k] for k in range(N_MOD)]


    lre_x, lim_x = _expand(ssm_lam_re[0]), _expand(ssm_lam_im[0])
    lst_x = jnp.broadcast_to(ssm_log_step[0][:, None], (N_GROUPS, SSM_STATE * SSM_GROUP))
    b_re_x = ssm_b_re[0].reshape(N_GROUPS, -1)
    b_im_x = ssm_b_im[0].reshape(N_GROUPS, -1)
    ar_x, ai_x, bbr_x, bbi_x = _ssm_prep(lre_x, lim_x, lst_x, b_re_x, b_im_x)
    lam_r = ar_x[:, ::SSM_GROUP].reshape(1, N_STATE)
    lam_i = ai_x[:, ::SSM_GROUP].reshape(1, N_STATE)
    big_b_re = _block_diag_b(bbr_x.reshape(N_GROUPS, SSM_STATE, SSM_GROUP)).astype(BF16)
    big_b_im = _block_diag_b(bbi_x.reshape(N_GROUPS, SSM_STATE, SSM_GROUP)).astype(BF16)
    big_c_re = _block_diag_c(ssm_c_re[0]).astype(BF16)
    big_c_im = _block_diag_c(ssm_c_im[0]).astype(BF16)
    head = jnp.arange(D_SSM)
    avg16 = jnp.where(head[:, None] // SSM_GROUP == head[None, :] // SSM_GROUP, 1.0 / SSM_GROUP, 0.0).astype(BF16)
    hd = D_CONV // CONV_HEADS
    avg64 = jnp.where(head[:, None] // hd == head[None, :] // hd, 1.0 / hd, 0.0).astype(BF16)

    w_up_t, half = w_up[0].T, D_MODEL // 2
    (proj, h1), (ffn_conv_s, conv_s, w_up_a) = _pre_mix(
        xt, sc1, sh1, g_pre_mix, w_in_s, tw, ([ffn_conv_w[0], conv_w[0], w_up_t[:, :half].astype(BF16)], False))
    cw_full = conv_s.transpose(1, 0, 2).reshape(3, D_CONV)
    u_perm = _to_scan_rows(proj[:, :D_SSM])
    (s_re, s_im, y_perm), (w_up_b, glu_s, w_out_s) = _ssm_fwd(
        u_perm, big_b_re, big_b_im, big_c_re, big_c_im, lam_r, lam_i,
        ([w_up_t[:, half:].astype(BF16), glu_w[0].astype(BF16), w_out[0].astype(BF16)], False))
    glu_full = glu_s.reshape(D_SSM, D_SSM)
    w_out_full = w_out_s.reshape(D_MODEL, D_MODEL)
    yssm = _from_scan_rows(y_perm)
    mix_args = (ssm_d, glu_full, glu_b, g_out_ssm, cw_full, g_out_conv, avg16, avg64)
    ycat = _mix_fwd(yssm, proj, *mix_args, tw)
    o, x1, h2 = _out_proj(ycat, w_out_full, xt, gt1, g_post_mix, g_pre_ffn, sc2, sh2, tw)
    (up8, hid8), (w_down_s,) = _ffn_up(h2, w_up_a, w_up_b, ffn_conv_s, tw, ([w_down[0].astype(BF16)], False))
    wd4 = w_down_s.reshape(4, FF_SHARD, D_MODEL)
    hid4 = hid8.reshape(2, 4, T, FF_SHARD)
    ddn, dx2, loss_parts, d_gt2, d_g_post_ffn = _ffn_down(hid4, wd4, x1, tgt, gt2, g_post_ffn, tm)
    loss_local = jnp.sum(loss_parts[:, 0, 0])

    got = {}
    dhid, g_w_down = _ffn_dact(ddn, wd4, hid4, tw)
    (dup8, dcw_ffn), (got['w_down'],) = _ffn_dup(dhid.reshape(N_DEV, T, FF_SHARD), up8, ffn_conv_s, tw,
                                                 ([g_w_down.reshape(N_DEV, D_FF // N_DEV, D_MODEL)], True))
    g_w_up_halves = _grad_tn(dup8, h2, pl.BlockSpec((None, T, FF_SHARD), lambda g, k: (g, k, 0)),
                             pl.BlockSpec((T, D_MODEL), lambda g, k: (k, 0)), N_DEV, FF_SHARD, D_MODEL, T,
                             'grad_w_up', parts=2)
    (dx1, d_sh2, d_sc2, d_g_pre_ffn, d_o, d_gt1, d_g_post_mix), (got_up_0, got['ffn_conv_w']) = _pre_norm_bwd(
        dup8, pl.BlockSpec((2, tw, FF_SHARD), lambda i, j: (j, i, 0)), [w_up_a, w_up_b], x1, dx2, sc2, g_pre_ffn, tw,
        'ffn_in_bwd', ([g_w_up_halves[0], dcw_ffn], True), below=(o, gt1, g_post_mix), group=2, w_t=True)

    g_w_out = _grad_tn(ycat, d_o, pl.BlockSpec((T, D_MODEL), lambda g, k: (k, 0)),
                       pl.BlockSpec((T, D_MODEL), lambda g, k: (k, 0)), 1, D_MODEL, D_MODEL, T, 'grad_w_out')
    (dy, dconv, dbg, z_b, dlin_b, sums), (got['w_out'],) = _mix_bwd(
        d_o, w_out_full, yssm, proj, *mix_args, tm, ([g_w_out.reshape(N_DEV, D_MODEL // N_DEV, D_MODEL)], True))
    g_glu_w = _grad_tn(z_b, dlin_b, pl.BlockSpec((T, D_SSM), lambda g, k: (k, 0)),
                       pl.BlockSpec((T, D_SSM), lambda g, k: (k, 0)), 1, D_SSM, D_SSM, T, 'grad_glu_w')
    dy_perm = _to_scan_rows(dy)
    (du_perm, dbr_blk, dbi_blk, dcr_blk, dci_blk, dar_blk, dai_blk), (got_up_1, got['glu_w']) = _ssm_bwd(
        dy_perm, u_perm, s_re, s_im, big_b_re, big_b_im, big_c_re, big_c_im, lam_r, lam_i,
        ([g_w_up_halves[1], g_glu_w.reshape(N_DEV, D_SSM // N_DEV, D_SSM)], True))
    du_ssm = _from_scan_rows(du_perm)
    dproj = _mix_bwd_proj(dconv, proj, du_ssm, dy, ssm_d, dbg, cw_full, tw)
    dbb_re = _diag_blocks(dbr_blk, True).reshape(N_GROUPS, -1)
    dbb_im = _diag_blocks(dbi_blk, True).reshape(N_GROUPS, -1)
    d_c_re = _diag_blocks(dcr_blk, False).transpose(0, 2, 1)
    d_c_im = _diag_blocks(dci_blk, False).transpose(0, 2, 1)
    lane = jnp.arange(SSM_STATE * SSM_GROUP)
    seg = jnp.where(lane[:, None] // SSM_GROUP == lane[None, :] // SSM_GROUP, 1.0, 0.0).astype(BF16)
    d_b_re_x, d_b_im_x, d_lre_x, d_lim_x, d_lst = _ssm_prep_bwd(
        lre_x, lim_x, lst_x, b_re_x, b_im_x, dbb_re, dbb_im, _expand(dar_blk.reshape(N_GROUPS, SSM_STATE)),
        _expand(dai_blk.reshape(N_GROUPS, SSM_STATE)), seg)

    row = lambda a: a.reshape(-1, PACK_COLS)
    blank = jnp.zeros((1, PACK_COLS), F32)
    small_pack = jnp.concatenate([
        d_b_re_x, d_b_im_x, row(d_c_re), row(d_c_im), blank, blank, d_gt1, d_sh2, d_sc2, d_gt2, blank,
        d_g_post_mix, row(d_lre_x[:, ::SSM_GROUP]), row(d_lim_x[:, ::SSM_GROUP]),
        jnp.pad(d_lst.reshape(1, N_GROUPS), ((0, 0), (0, PACK_COLS - N_GROUPS))), row(sums[0:4]), d_g_pre_ffn,
        d_g_post_ffn, jnp.zeros((SMALL_ROWS - 145, PACK_COLS), F32)])
    g_w_in, (small_all,) = _grad_w_in(h1, dproj, tk, ([small_pack], False))
    g_conv_slots = jnp.concatenate([sums[4:7], jnp.zeros((5, D_CONV), F32)]).reshape(
        8, N_DEV, D_CONV // N_DEV).transpose(1, 0, 2)
    (grad_x, d_sh1, d_sc1, d_g_pre_mix), (got['w_in'], got['conv_w']) = _pre_norm_bwd(
        dproj, pl.BlockSpec((tw, 4 * IN_SHARD), lambda i, j: (i, j)), [w_in_s], xt, dx1, sc1, g_pre_mix, tw,
        'mix_in_bwd', ([g_w_in, g_conv_slots], True), group=4)
    late_pack = jnp.concatenate([d_sh1, d_sc1, d_g_pre_mix, jnp.full((1, PACK_COLS), loss_local, F32),
                                 jnp.zeros((4, PACK_COLS), F32)])
    (late_all,) = _exchange([late_pack], name='gather_late_grads', scatter=False)
    loss = jnp.sum(late_all[:, 3, 0])
    res = _adamw_small(small_all, late_all, wts, mom_m, mom_v)

    dmod_all = jnp.concatenate([late_all[:, 0:2, :], small_all[:, B_ADA_ROW + 2:B_ADA_ROW + N_MOD, :]],
                               axis=1).reshape(N_DEV, N_MOD * D_MODEL)
    dmod_cols = lax.dynamic_slice(dmod_all, (0, me * ADA_SHARD), (N_DEV, ADA_SHARD))
    g_w_ada = _grad_w_ada(c_act.T, dmod_cols)

    pieces = {n: [slots[:, :3, :] if n in ('conv_w', 'ffn_conv_w') else slots] for n, slots in got.items()}
    for n, parts in pieces.items():
        outs = _adamw(parts, wts[n][0], mom_m[n][0], mom_v[n][0], 'adamw_' + n)
        for kind, val in zip(('g', 'd', 'm', 'v'), outs):
            res[kind, n] = val[None]
    outs = _adamw([got_up_0, got_up_1], w_up[0].T, m_w_up[0].T, v_w_up[0].T, 'adamw_w_up')
    for kind, val in zip(('g', 'd', 'm', 'v'), outs):
        res[kind, 'w_up'] = val.T[None]
    outs = _adamw([g_w_ada[None]], w_ada[0], m_w_ada[0], v_w_ada[0], 'adamw_w_ada')
    for kind, val in zip(('g', 'd', 'm', 'v'), outs):
        res[kind, 'w_ada'] = val[None]

    return (loss, grad_x[None], *[res['g', n] for n in WEIGHTS], *[res['d', n] for n in WEIGHTS],
            *[res['m', n] for n in WEIGHTS], *[res['v', n] for n in WEIGHTS])
```

```python
import math

import jax
import jax.numpy as jnp
from jax import lax
from jax.experimental import pallas as pl
from jax.experimental.pallas import tpu as pltpu

F32, BF16 = jnp.float32, jnp.bfloat16

D_MODEL = 1024
D_SSM = 512
D_CONV = 512
SSM_GROUP = 16
N_GROUPS = 32
SSM_STATE = 64
N_STATE = N_GROUPS * SSM_STATE
CONV_HEADS = 8
D_FF = 2816
N_MOD = 6
D_IN_PROJ = D_SSM + 3 * D_CONV
N_DEV = 8
FF_SHARD = 2 * D_FF // N_DEV
IN_SHARD = D_IN_PROJ // N_DEV
ADA_SHARD = N_MOD * D_MODEL // N_DEV
EPS = 1e-6
LAMBDA_RE_MAX = -1e-4
ADAM_LR, ADAM_B1, ADAM_B2, ADAM_EPS, ADAM_WD, ADAM_STEP = 0.001, 0.9, 0.999, 1e-08, 0.01, 10
GELU_C = math.sqrt(2.0 / math.pi)
GELU_A = 0.044715

SUBLANES = 8
HALO = 8
HALO16 = 16
SCAN_UNROLL = 16
STATE_BLOCK = 512
CHAN_BLOCK = 128
VMEM_BIG = 48 << 20
VMEM_MOST = 58 << 20

WEIGHTS = ['w_ada', 'b_ada', 'g_pre_mix', 'g_post_mix', 'w_in', 'ssm_lam_re', 'ssm_lam_im', 'ssm_log_step',
           'ssm_b_re', 'ssm_b_im', 'ssm_c_re', 'ssm_c_im', 'ssm_d', 'glu_w', 'glu_b', 'g_out_ssm', 'conv_w',
           'g_out_conv', 'w_out', 'g_pre_ffn', 'g_post_ffn', 'w_up', 'ffn_conv_w', 'w_down']
PACK_COLS = 1024


def _call(body, *, name, grid, in_specs, out_specs, out_shape, scratch=(), sem=None, vmem=None, ride=None):
    params = {}
    if vmem is not None:
        params['vmem_limit_bytes'] = vmem
    if ride is None:
        if sem is not None:
            params['dimension_semantics'] = sem
        return pl.pallas_call(body, name=name, grid=grid, in_specs=in_specs, out_specs=out_specs,
                              out_shape=out_shape, scratch_shapes=list(scratch),
                              compiler_params=pltpu.CompilerParams(**params))
    arrs, scatter = ride
    single = not isinstance(out_shape, (list, tuple))
    out_shape_l = [out_shape] if single else list(out_shape)
    out_specs_l = [out_specs] if single else list(out_specs)
    n, n_in, n_out, n_scr = len(arrs), len(in_specs), len(out_shape_l), len(scratch)
    any_spec = pl.BlockSpec(memory_space=pl.ANY)
    params['dimension_semantics'] = ('arbitrary',) * len(grid)

    def carried(*refs):
        ins, rin = refs[:n_in], refs[n_in:n_in + n]
        outs, rout = refs[n_in + n:n_in + n + n_out], refs[n_in + n + n_out:n_in + 2 * n + n_out]
        scr, sems = refs[n_in + 2 * n + n_out:n_in + 2 * n + n_out + n_scr], refs[n_in + 2 * n + n_out + n_scr:]
        first = pl.program_id(0) == 0
        last = pl.program_id(0) == grid[0] - 1
        for ax in range(1, len(grid)):
            first = jnp.logical_and(first, pl.program_id(ax) == 0)
            last = jnp.logical_and(last, pl.program_id(ax) == grid[ax] - 1)

        @pl.when(first)
        def _():
            _exchange_start(rin, rout, sems, scatter)

        body(*ins, *outs, *scr)

        @pl.when(last)
        def _():
            _exchange_wait(rin, rout, sems, scatter)

    call = pl.pallas_call(carried, name=name, grid=grid, in_specs=list(in_specs) + [any_spec] * n,
                          out_specs=out_specs_l + [any_spec] * n,
                          out_shape=out_shape_l + _exchange_shapes(arrs, scatter),
                          scratch_shapes=list(scratch) + _exchange_sems(n),
                          compiler_params=pltpu.CompilerParams(**params))

    def run(*args):
        res = call(*args, *arrs)
        own = res[0] if single else list(res[:n_out])
        return own, list(res[n_out:])

    return run


def _const(shape):
    nd = len(shape)
    return pl.BlockSpec(shape, lambda *_: (0,) * nd)


def _sds(shape, dtype=F32):
    return jax.ShapeDtypeStruct(shape, dtype)


def _dot(a, b):
    return jnp.dot(a, b, preferred_element_type=F32)


def _dot_nt(a, b):
    return lax.dot_general(a, b, (((1,), (1,)), ((), ())), preferred_element_type=F32)


def _dot_tn(a, b):
    return lax.dot_general(a, b, (((0,), (0,)), ((), ())), preferred_element_type=F32)


def _dot_split(x, mat, parts):
    acc = None
    rem = x
    for _ in range(parts):
        piece = rem.astype(BF16)
        rem = rem - piece.astype(F32)
        term = _dot(piece, mat)
        acc = term if acc is None else acc + term
    return acc


def _sigmoid(x):
    return 1.0 / (1.0 + jnp.exp(-x))


def _gelu(x):
    t = jnp.tanh(GELU_C * (x + GELU_A * x * x * x))
    return 0.5 * x * (1.0 + t), t


def _gelu_grad(x, t):
    return 0.5 * (1.0 + t) + 0.5 * x * (1.0 - t * t) * GELU_C * (1.0 + 3.0 * GELU_A * x * x)


def _rsqrt_mean(x):
    return lax.rsqrt(jnp.mean(x * x, axis=-1, keepdims=True) + EPS)


def _colsum(x):
    return jnp.sum(x, axis=0, keepdims=True)


def _shifts_down(x, halo):
    ext = jnp.concatenate([halo, x], axis=0)
    return pltpu.roll(ext, 1, 0)[halo.shape[0]:], pltpu.roll(ext, 2, 0)[halo.shape[0]:]


def _shifts_up(x, halo):
    n = x.shape[0]
    ext = jnp.concatenate([x, halo], axis=0)
    total = ext.shape[0]
    return pltpu.roll(ext, total - 1, 0)[:n], pltpu.roll(ext, total - 2, 0)[:n]


def _conv3(x, halo, w_ref):
    x1, x2 = _shifts_down(x, halo)
    return w_ref[0:1, :] * x2 + w_ref[1:2, :] * x1 + w_ref[2:3, :] * x, x1, x2


def _conv3_t(g, halo, w_ref):
    g1, g2 = _shifts_up(g, halo)
    return w_ref[2:3, :] * g + w_ref[1:2, :] * g1 + w_ref[0:1, :] * g2, g1, g2


def _silu_parts(x):
    s = _sigmoid(x)
    return x * s, s * (1.0 + x * (1.0 - s))


def _norm_bwd(dn, x, r, g):
    gd = g * dn
    return r * gd - x * (r * r * r) * jnp.mean(gd * x, axis=-1, keepdims=True)


def _head_norm_bwd(dn, y, rs, g, avg):
    gd = g * dn
    return rs * gd - y * (rs * rs * rs) * _dot_split(gd * y, avg, 2)


def _me():
    x, y, c = lax.axis_index('x'), lax.axis_index('y'), lax.axis_index('c')
    return x, y, c, 4 * x + 2 * y + c


def _peer(k):
    x, y, c, _ = _me()
    px = 1 - x if k & 4 else x
    py = 1 - y if k & 2 else y
    pc = 1 - c if k & 1 else c
    return (px, py, pc), 4 * px + 2 * py + pc


SIBLING = 1
OTHER_CHIPS = (2, 4, 6)


def _remote(src, dst, sems, a, k, dev):
    return pltpu.make_async_remote_copy(src_ref=src, dst_ref=dst, send_sem=sems[0].at[a, k - 1],
                                        recv_sem=sems[1].at[a, k - 1], device_id=dev,
                                        device_id_type=pl.DeviceIdType.MESH)


def _exchange_copies(ins, outs, sems, scatter):
    me = _me()[3]
    local, first, relay, arrivals = [], [], [], []
    for a in range(len(ins)):
        src = ins[a].at[me] if scatter else ins[a]
        local.append(pltpu.make_async_copy(src, outs[a].at[me], sems[2].at[a]))
        for k in range(1, N_DEV):
            dev, idx = _peer(k)
            landed = _remote(src, outs[a].at[idx], sems, a, k, dev)
            if scatter:
                first.append(_remote(ins[a].at[idx], outs[a].at[me], sems, a, k, dev))
                arrivals.append(landed)
            elif k == SIBLING:
                first.append(_remote(src, outs[a].at[me], sems, a, k, dev))
                arrivals.append(landed)
            elif k in OTHER_CHIPS:
                first.append(_remote(src, outs[a].at[me], sems, a, k, dev))
                sib, _ = _peer(SIBLING)
                relay.append((landed, _remote(outs[a].at[idx], outs[a].at[idx], sems, a, k | SIBLING, sib)))
            else:
                arrivals.append(landed)
    return local, first, relay, arrivals


def _exchange_start(ins, outs, sems, scatter):
    local, first, _, _ = _exchange_copies(ins, outs, sems, scatter)
    for cp in local + first:
        cp.start()


def _exchange_wait(ins, outs, sems, scatter):
    local, first, relay, arrivals = _exchange_copies(ins, outs, sems, scatter)
    for landed, forward in relay:
        landed.wait_recv()
        forward.start()
    for cp in arrivals:
        cp.wait_recv()
    for cp in first + [forward for _, forward in relay]:
        cp.wait_send()
    for cp in local:
        cp.wait()


def _exchange_shapes(arrs, scatter):
    return [_sds(a.shape if scatter else (N_DEV,) + a.shape, a.dtype) for a in arrs]


def _exchange_sems(n):
    return [pltpu.SemaphoreType.DMA((n, N_DEV - 1)), pltpu.SemaphoreType.DMA((n, N_DEV - 1)),
            pltpu.SemaphoreType.DMA((n,))]


def _exchange(arrs, *, name, scatter):
    n = len(arrs)

    def body(*refs):
        _exchange_start(refs[:n], refs[n:2 * n], refs[2 * n:], scatter)
        _exchange_wait(refs[:n], refs[n:2 * n], refs[2 * n:], scatter)

    any_spec = pl.BlockSpec(memory_space=pl.ANY)
    outs = pl.pallas_call(body, name=name, out_shape=_exchange_shapes(arrs, scatter), in_specs=[any_spec] * n,
                          out_specs=[any_spec] * n, scratch_shapes=_exchange_sems(n))(*arrs)
    return list(outs)


def _mod_cols(c_all, w_ada, b_cols):
    def body(c_ref, w_ref, b_ref, mod_ref, act_ref):
        c = c_ref[...]
        act = c * _sigmoid(c)
        act_ref[...] = act
        mod_ref[...] = _dot(act.astype(BF16), w_ref[...].astype(BF16)) + b_ref[...]

    return _call(body, name='mod_cols', grid=(1,),
                 in_specs=[_const(c_all.shape), _const(w_ada.shape), _const(b_cols.shape)],
                 out_specs=[_const((N_DEV, ADA_SHARD)), _const(c_all.shape)],
                 out_shape=[_sds((N_DEV, ADA_SHARD)), _sds(c_all.shape)], vmem=VMEM_BIG)(c_all, w_ada, b_cols)


def _grad_w_ada(act_t, dmod_cols):
    def body(a_ref, d_ref, o_ref):
        o_ref[...] = _dot(a_ref[...], d_ref[...])

    return _call(body, name='grad_w_ada', grid=(1,), in_specs=[_const(act_t.shape), _const(dmod_cols.shape)],
                 out_specs=_const((D_MODEL, ADA_SHARD)), out_shape=_sds((D_MODEL, ADA_SHARD)),
                 vmem=VMEM_BIG)(act_t, dmod_cols)


def _pre_mix(x, sc, sh, g, w_s, tm, ride):
    T = x.shape[0]
    group = 4

    def body(x_ref, sc_ref, sh_ref, g_ref, w_ref, proj_ref, h_ref):
        @pl.when(pl.program_id(1) == 0)
        def _():
            xv = x_ref[...]
            h_ref[...] = ((xv * _rsqrt_mean(xv) * g_ref[...]) * (1.0 + sc_ref[...]) + sh_ref[...]).astype(BF16)

        for s in range(group):
            proj_ref[:, s * IN_SHARD:(s + 1) * IN_SHARD] = _dot(h_ref[...], w_ref[s])

    row = pl.BlockSpec((tm, D_MODEL), lambda i, j: (i, 0))
    vec = _const((1, D_MODEL))
    return _call(body, name='pre_mix', grid=(T // tm, N_DEV // group),
                 in_specs=[row, vec, vec, vec, pl.BlockSpec((group, D_MODEL, IN_SHARD), lambda i, j: (j, 0, 0))],
                 out_specs=[pl.BlockSpec((tm, group * IN_SHARD), lambda i, j: (i, j)), row],
                 out_shape=[_sds((T, D_IN_PROJ)), _sds((T, D_MODEL), BF16)],
                 sem=('parallel', 'arbitrary'), ride=ride)(x, sc, sh, g, w_s)


def _halo_before(tm, rows=HALO):
    return lambda i: jnp.maximum(i * (tm // rows) - 1, 0)


def _halo_after(tm, T, rows=HALO):
    return lambda i: jnp.minimum((i + 1) * (tm // rows), T // rows - 1)


def _mix_fwd(yssm, proj, d, glu_w, glu_b, g_ssm, cw, g_conv, avg16, avg64, tm):
    T = yssm.shape[0]
    hb = _halo_before(tm)

    def body(y_ref, p_ref, ph_ref, d_ref, gw_ref, gb_ref, gs_ref, cw_ref, gc_ref, a16_ref, a64_ref, o_ref):
        i = pl.program_id(0)
        u = p_ref[:, 0:D_SSM]
        y = y_ref[...] + d_ref[...] * u
        z, _ = _gelu(y)
        gate = _sigmoid(_dot(z.astype(BF16), gw_ref[...]) + gb_ref[...])
        ya = z * gate
        rs = lax.rsqrt(_dot_split(ya * ya, a16_ref[...], 2) + EPS)
        o_ref[:, 0:D_SSM] = (ya * rs * gs_ref[...]).astype(BF16)
        bg = p_ref[:, D_SSM:D_SSM + D_CONV]
        cv = p_ref[:, D_SSM + D_CONV:D_SSM + 2 * D_CONV] * p_ref[:, D_SSM + 2 * D_CONV:D_IN_PROJ]
        hv = ph_ref[:, D_SSM + D_CONV:D_SSM + 2 * D_CONV] * ph_ref[:, D_SSM + 2 * D_CONV:D_IN_PROJ]
        hv = jnp.where(i > 0, hv, 0.0)
        conv, _, _ = _conv3(cv, hv, cw_ref)
        yb = bg * conv
        rsb = lax.rsqrt(_dot_split(yb * yb, a64_ref[...], 2) + EPS)
        o_ref[:, D_SSM:D_MODEL] = (yb * rsb * gc_ref[...]).astype(BF16)

    vec = _const((1, D_SSM))
    sq = _const((D_SSM, D_SSM))
    return _call(body, name='mix_fwd', grid=(T // tm,),
                 in_specs=[pl.BlockSpec((tm, D_SSM), lambda i: (i, 0)), pl.BlockSpec((tm, D_IN_PROJ), lambda i: (i, 0)),
                           pl.BlockSpec((HALO, D_IN_PROJ), lambda i: (hb(i), 0)), vec, sq, vec, vec,
                           _const((3, D_CONV)), vec, sq, sq],
                 out_specs=pl.BlockSpec((tm, D_MODEL), lambda i: (i, 0)), out_shape=_sds((T, D_MODEL), BF16),
                 sem=('parallel',), vmem=VMEM_BIG)(yssm, proj, proj, d, glu_w, glu_b, g_ssm, cw, g_conv, avg16, avg64)


def _out_proj(ycat, w_out, x, gt, g_post, g_pre, sc, sh, tm):
    T = x.shape[0]

    def body(y_ref, w_ref, x_ref, gt_ref, gp_ref, g2_ref, sc_ref, sh_ref, o_ref, x1_ref, h_ref):
        o = _dot(y_ref[...], w_ref[...])
        o_ref[...] = o.astype(BF16)
        x1 = x_ref[...] + gt_ref[...] * (o * _rsqrt_mean(o) * gp_ref[...])
        x1_ref[...] = x1
        h_ref[...] = ((x1 * _rsqrt_mean(x1) * g2_ref[...]) * (1.0 + sc_ref[...]) + sh_ref[...]).astype(BF16)

    row = pl.BlockSpec((tm, D_MODEL), lambda i: (i, 0))
    vec = _const((1, D_MODEL))
    return _call(body, name='out_proj', grid=(T // tm,),
                 in_specs=[row, _const((D_MODEL, D_MODEL)), row, vec, vec, vec, vec, vec],
                 out_specs=[row, row, row],
                 out_shape=[_sds((T, D_MODEL), BF16), _sds((T, D_MODEL)), _sds((T, D_MODEL), BF16)],
                 sem=('parallel',), vmem=VMEM_BIG)(ycat, w_out, x, gt, g_post, g_pre, sc, sh)


def _ffn_up(h2, w_a, w_b, cw8, tm, ride):
    T = h2.shape[0]
    hb = _halo_before(tm, HALO16)
    half = D_MODEL // 2

    def body(h_ref, hh_ref, wa_ref, wb_ref, cw_ref, up_ref, hid_ref):
        def times_w(ref, s):
            return _dot_nt(ref[:, :half], wa_ref[s]) + _dot_nt(ref[:, half:], wb_ref[s])

        for s in range(2):
            up = times_w(h_ref, s)
            up_ref[s] = up.astype(BF16)
            before = jnp.where(pl.program_id(0) > 0, times_w(hh_ref, s), 0.0)
            hid_ref[s] = _conv3(up, before, cw_ref.at[s])[0].astype(BF16)

    out = pl.BlockSpec((2, tm, FF_SHARD), lambda i, j: (j, i, 0))
    return _call(body, name='ffn_up', grid=(T // tm, N_DEV // 2),
                 in_specs=[pl.BlockSpec((tm, D_MODEL), lambda i, j: (i, 0)),
                           pl.BlockSpec((HALO16, D_MODEL), lambda i, j: (hb(i), 0)),
                           pl.BlockSpec((2, FF_SHARD, half), lambda i, j: (j, 0, 0)),
                           pl.BlockSpec((2, FF_SHARD, half), lambda i, j: (j, 0, 0)),
                           pl.BlockSpec((2, 3, FF_SHARD), lambda i, j: (j, 0, 0))],
                 out_specs=[out, out], out_shape=[_sds((N_DEV, T, FF_SHARD), BF16)] * 2,
                 sem=('parallel', 'parallel'), vmem=VMEM_BIG, ride=ride)(h2, h2, w_a, w_b, cw8)


def _ffn_down(hid4, wd4, x1, tgt, gt, g_post, tm):
    T = x1.shape[0]
    nb = T // tm

    def body(a_ref, w_ref, x1_ref, t_ref, gt_ref, g_ref, ddn_ref, dx_ref, loss_ref, dgt_ref, dg_ref, dn_ref):
        i, j = pl.program_id(0), pl.program_id(1)
        part = None
        for s in range(2):
            act = (_silu_parts(a_ref[0, s].astype(F32))[0] * a_ref[1, s].astype(F32)).astype(BF16)
            term = _dot(act, w_ref[s])
            part = term if part is None else part + term

        @pl.when(jnp.logical_and(i == 0, j == 0))
        def _():
            dgt_ref[...] = jnp.zeros_like(dgt_ref)
            dg_ref[...] = jnp.zeros_like(dg_ref)

        @pl.when(j == 0)
        def _():
            dn_ref[...] = part

        @pl.when(j > 0)
        def _():
            dn_ref[...] += part

        @pl.when(j == 1)
        def _():
            dn, gv, gate = dn_ref[...], g_ref[...], gt_ref[...]
            r = _rsqrt_mean(dn)
            normed = dn * r * gv
            err = x1_ref[...] + gate * normed - t_ref[...]
            dx = err * (1.0 / D_MODEL)
            dx_ref[...] = dx
            tot = jnp.sum(jnp.sum(err * err, axis=1, keepdims=True), axis=0, keepdims=True) * (0.5 / D_MODEL)
            loss_ref[...] = jnp.broadcast_to(tot, (8, 128))
            dgt_ref[...] += _colsum(dx * normed)
            dnn = dx * gate
            dg_ref[...] += _colsum(dnn * dn * r)
            ddn_ref[...] = _norm_bwd(dnn, dn, r, gv).astype(BF16)

    row = pl.BlockSpec((tm, D_MODEL), lambda i, j: (i, 0))
    vec = _const((1, D_MODEL))
    return _call(body, name='ffn_down', grid=(nb, 2),
                 in_specs=[pl.BlockSpec((2, 2, tm, FF_SHARD), lambda i, j: (0, j, i, 0)),
                           pl.BlockSpec((2, FF_SHARD, D_MODEL), lambda i, j: (j, 0, 0)), row, row, vec, vec],
                 out_specs=[row, row, pl.BlockSpec((None, 8, 128), lambda i, j: (i, 0, 0)), vec, vec],
                 out_shape=[_sds((T, D_MODEL), BF16), _sds((T, D_MODEL)), _sds((nb, 8, 128)), _sds((1, D_MODEL)),
                            _sds((1, D_MODEL))],
                 scratch=[pltpu.VMEM((tm, D_MODEL), F32)], sem=('arbitrary', 'arbitrary'),
                 vmem=VMEM_BIG)(hid4, wd4, x1, tgt, gt, g_post)


def _ssm_prep(lre, lim, lst, b_re, b_im):
    def body(lre_ref, lim_ref, lst_ref, br_ref, bi_ref, ar_ref, ai_ref, bbr_ref, bbi_ref):
        ar, ai, qr, qi = _zoh(lre_ref[...], lim_ref[...], lst_ref[...])[:4]
        ar_ref[...] = ar
        ai_ref[...] = ai
        bbr_ref[...] = qr * br_ref[...] - qi * bi_ref[...]
        bbi_ref[...] = qr * bi_ref[...] + qi * br_ref[...]

    shp = lre.shape
    return _call(body, name='ssm_prep', grid=(1,), in_specs=[_const(shp)] * 5, out_specs=[_const(shp)] * 4,
                 out_shape=[_sds(shp)] * 4)(lre, lim, lst, b_re, b_im)


def _zoh(lre, lim, lst):
    lr = jnp.minimum(lre, LAMBDA_RE_MAX)
    st = jnp.exp(lst)
    mag = jnp.exp(lr * st)
    ar = mag * jnp.cos(lim * st)
    ai = mag * jnp.sin(lim * st)
    den = lr * lr + lim * lim
    qr = ((ar - 1.0) * lr + ai * lim) / den
    qi = (ai * lr - (ar - 1.0) * lim) / den
    return ar, ai, qr, qi, lr, st, den


def _ssm_prep_bwd(lre, lim, lst, b_re, b_im, dbbr, dbbi, dar, dai, seg):
    def body(lre_ref, lim_ref, lst_ref, br_ref, bi_ref, dbbr_ref, dbbi_ref, dar_ref, dai_ref, seg_ref,
             dbr_ref, dbi_ref, dlre_ref, dlim_ref, dlst_ref):
        lre_v = lre_ref[...]
        li = lim_ref[...]
        ar, ai, qr, qi, lr, st, den = _zoh(lre_v, li, lst_ref[...])
        br, bi, gbr, gbi = br_ref[...], bi_ref[...], dbbr_ref[...], dbbi_ref[...]
        dbr_ref[...] = qr * gbr + qi * gbi
        dbi_ref[...] = qr * gbi - qi * gbr
        gqr = _dot_split(br * gbr + bi * gbi, seg_ref[...], 3)
        gqi = _dot_split(br * gbi - bi * gbr, seg_ref[...], 3)
        ir, ii = lr / den, -li / den
        gar = dar_ref[...] + ir * gqr + ii * gqi
        gai = dai_ref[...] + ir * gqi - ii * gqr
        tr, ti = qr * ir - qi * ii, qr * ii + qi * ir
        glr = -(tr * gqr + ti * gqi)
        gli = -(tr * gqi - ti * gqr)
        gzr = ar * gar + ai * gai
        gzi = ar * gai - ai * gar
        glr = glr + st * gzr
        gli = gli + st * gzi
        gst = (lr * gzr + li * gzi) * st
        dlre_ref[...] = jnp.where(lre_v < LAMBDA_RE_MAX, glr, 0.0)
        dlim_ref[...] = gli
        dlst_ref[...] = jnp.sum(gst, axis=1, keepdims=True) * (1.0 / SSM_GROUP)

    shp = lre.shape
    return _call(body, name='ssm_prep_bwd', grid=(1,), in_specs=[_const(shp)] * 9 + [_const(seg.shape)],
                 out_specs=[_const(shp)] * 4 + [_const((N_GROUPS, 1))],
                 out_shape=[_sds(shp)] * 4 + [_sds((N_GROUPS, 1))], vmem=VMEM_BIG)(
                     lre, lim, lst, b_re, b_im, dbbr, dbbi, dar, dai, seg)


def _scan_specs(T):
    return dict(
        chan=pl.BlockSpec((T, CHAN_BLOCK), lambda cb: (0, cb)),
        state=pl.BlockSpec((T, STATE_BLOCK), lambda cb: (0, cb)),
        b=pl.BlockSpec((CHAN_BLOCK, STATE_BLOCK), lambda cb: (cb, cb)),
        c=pl.BlockSpec((STATE_BLOCK, CHAN_BLOCK), lambda cb: (cb, cb)),
        lam=pl.BlockSpec((1, STATE_BLOCK), lambda cb: (0, cb)),
    )


def _complex_power(re, im, n):
    out = None
    while True:
        if n & 1:
            out = (re, im) if out is None else (out[0] * re - out[1] * im, out[0] * im + out[1] * re)
        n >>= 1
        if n == 0:
            return out
        re, im = re * re - im * im, 2.0 * re * im


def _rows8(i):
    if isinstance(i, int):
        return pl.ds(i * SUBLANES, SUBLANES)
    return pl.ds(pl.multiple_of(i * SUBLANES, SUBLANES), SUBLANES)


def _scan_loop(n_steps, body, init):
    trips = n_steps // SCAN_UNROLL

    def trip(t, carry):
        for u in range(SCAN_UNROLL):
            carry = body(t * SCAN_UNROLL + u, carry)
        return carry

    carry = lax.fori_loop(0, trips, trip, init)
    for step in range(trips * SCAN_UNROLL, n_steps):
        carry = body(step, carry)
    return carry


def _ssm_fwd(u_perm, b_re, b_im, c_re, c_im, lam_r, lam_i, ride):
    T = u_perm.shape[0]
    ls = T // SUBLANES
    rc = min(1024, T)
    sp = _scan_specs(T)

    def body(u_ref, bre_ref, bim_ref, cre_ref, cim_ref, lr_ref, li_ref, so_re_ref, so_im_ref, y_ref, sre_ref, sim_ref):
        for c in range(T // rc):
            rows = pl.ds(c * rc, rc)
            ub = u_ref[rows, :].astype(BF16)
            sre_ref[rows, :] = _dot(ub, bre_ref[...])
            sim_ref[rows, :] = _dot(ub, bim_ref[...])
        shp = (SUBLANES, STATE_BLOCK)
        lr = jnp.broadcast_to(lr_ref[...], shp)
        li = jnp.broadcast_to(li_ref[...], shp)
        zero = jnp.zeros(shp, F32)

        def step(i, carry):
            sr, si = carry
            rows = _rows8(i)
            nr = lr * sr - li * si + sre_ref[rows, :]
            ni = lr * si + li * sr + sim_ref[rows, :]
            sre_ref[rows, :] = nr
            sim_ref[rows, :] = ni
            return nr, ni

        fr, fi = _scan_loop(ls, step, (zero, zero))
        pr, pi_ = _complex_power(lr, li, ls)
        row = lax.broadcasted_iota(jnp.int32, shp, 0)
        ir, ii = zero, zero
        for _ in range(SUBLANES - 1):
            er = fr + pr * ir - pi_ * ii
            ei = fi + pr * ii + pi_ * ir
            ir = jnp.where(row == 0, 0.0, pltpu.roll(er, 1, 0))
            ii = jnp.where(row == 0, 0.0, pltpu.roll(ei, 1, 0))

        def fix(i, carry):
            cr, ci = carry
            rows = _rows8(i)
            nr = lr * cr - li * ci
            ni = lr * ci + li * cr
            sre_ref[rows, :] += nr
            sim_ref[rows, :] += ni
            return nr, ni

        _scan_loop(ls, fix, (ir, ii))
        for c in range(T // rc):
            rows = pl.ds(c * rc, rc)
            s_r, s_i = sre_ref[rows, :].astype(BF16), sim_ref[rows, :].astype(BF16)
            so_re_ref[rows, :] = s_r
            so_im_ref[rows, :] = s_i
            y_ref[rows, :] = _dot(s_r, cre_ref[...]) - _dot(s_i, cim_ref[...])

    return _call(body, name='ssm_fwd', grid=(N_STATE // STATE_BLOCK,),
                 in_specs=[sp['chan'], sp['b'], sp['b'], sp['c'], sp['c'], sp['lam'], sp['lam']],
                 out_specs=[sp['state'], sp['state'], sp['chan']],
                 out_shape=[_sds((T, N_STATE), BF16), _sds((T, N_STATE), BF16), _sds((T, D_SSM))],
                 scratch=[pltpu.VMEM((T, STATE_BLOCK), F32), pltpu.VMEM((T, STATE_BLOCK), F32)],
                 sem=('arbitrary',), vmem=VMEM_MOST, ride=ride)(u_perm, b_re, b_im, c_re, c_im, lam_r, lam_i)


def _ssm_bwd(dy_perm, u_perm, s_re, s_im, b_re, b_im, c_re, c_im, lam_r, lam_i, ride):
    T = u_perm.shape[0]
    ls = T // SUBLANES
    rc = min(1024, T)
    sp = _scan_specs(T)
    ncb = N_STATE // STATE_BLOCK

    def body(dy_ref, u_ref, sre_ref, sim_ref, bre_ref, bim_ref, cre_ref, cim_ref, lr_ref, li_ref,
             du_ref, dbr_ref, dbi_ref, dcr_ref, dci_ref, dar_ref, dai_ref, gre_ref, gim_ref):
        shp = (SUBLANES, STATE_BLOCK)
        zero = jnp.zeros(shp, F32)
        tail = pl.ds(T, SUBLANES)
        gre_ref[tail, :] = zero
        gim_ref[tail, :] = zero
        for c in range(T // rc):
            rows = pl.ds(c * rc, rc)
            dyb = dy_ref[rows, :].astype(BF16)
            gre_ref[rows, :] = _dot_nt(dyb, cre_ref[...])
            gim_ref[rows, :] = -_dot_nt(dyb, cim_ref[...])
        lr = jnp.broadcast_to(lr_ref[...], shp)
        li = jnp.broadcast_to(li_ref[...], shp)

        def step(k, carry):
            gr, gi = carry
            rows = _rows8(ls - 1 - k)
            nr = lr * gr + li * gi + gre_ref[rows, :]
            ni = lr * gi - li * gr + gim_ref[rows, :]
            gre_ref[rows, :] = nr
            gim_ref[rows, :] = ni
            return nr, ni

        fr, fi = _scan_loop(ls, step, (zero, zero))
        pr, pi_ = _complex_power(lr, -li, ls)
        row = lax.broadcasted_iota(jnp.int32, shp, 0)
        cr, ci = zero, zero
        for _ in range(SUBLANES - 1):
            er = fr + pr * cr - pi_ * ci
            ei = fi + pr * ci + pi_ * cr
            cr = jnp.where(row == SUBLANES - 1, 0.0, pltpu.roll(er, SUBLANES - 1, 0))
            ci = jnp.where(row == SUBLANES - 1, 0.0, pltpu.roll(ei, SUBLANES - 1, 0))

        def fix(k, carry):
            dr, di = carry
            rows = _rows8(ls - 1 - k)
            dr, di = lr * dr + li * di, lr * di - li * dr
            gre_ref[rows, :] += dr
            gim_ref[rows, :] += di
            return dr, di

        _scan_loop(ls, fix, (cr, ci))

        acc_r = jnp.zeros((1, STATE_BLOCK), F32)
        acc_i = jnp.zeros((1, STATE_BLOCK), F32)
        for c in range(T // rc):
            rows, nxt = pl.ds(c * rc, rc), pl.ds(c * rc + SUBLANES, rc)
            s_r, s_i = sre_ref[rows, :].astype(F32), sim_ref[rows, :].astype(F32)
            g_r, g_i = gre_ref[nxt, :], gim_ref[nxt, :]
            acc_r = acc_r + _colsum(g_r * s_r + g_i * s_i)
            acc_i = acc_i + _colsum(g_i * s_r - g_r * s_i)
        last = pl.ds(T - 2 * SUBLANES, 2 * SUBLANES)
        first = pl.ds(0, SUBLANES)
        spr = jnp.where(row == 0, 0.0, pltpu.roll(sre_ref[last, :].astype(F32)[SUBLANES:], 1, 0))
        spi = jnp.where(row == 0, 0.0, pltpu.roll(sim_ref[last, :].astype(F32)[SUBLANES:], 1, 0))
        gr, gi = gre_ref[first, :], gim_ref[first, :]
        dar_ref[...] = acc_r + _colsum(gr * spr + gi * spi)
        dai_ref[...] = acc_i + _colsum(gi * spr - gr * spi)

        for c in range(T // rc):
            rows = pl.ds(c * rc, rc)
            g_r, g_i = gre_ref[rows, :].astype(BF16), gim_ref[rows, :].astype(BF16)
            s_r, s_i = sre_ref[rows, :], sim_ref[rows, :]
            ub, dyb = u_ref[rows, :].astype(BF16), dy_ref[rows, :].astype(BF16)
            du_ref[rows, :] = _dot_nt(g_r, bre_ref[...]) + _dot_nt(g_i, bim_ref[...])
            parts = (_dot_tn(ub, g_r), _dot_tn(ub, g_i), _dot_tn(s_r, dyb), -_dot_tn(s_i, dyb))
            outs = (dbr_ref, dbi_ref, dcr_ref, dci_ref)
            for o_ref, part in zip(outs, parts):
                if c == 0:
                    o_ref[...] = part
                else:
                    o_ref[...] += part

    blk = lambda r, c: pl.BlockSpec((None, r, c), lambda cb: (cb, 0, 0))
    return _call(body, name='ssm_bwd', grid=(ncb,),
                 in_specs=[sp['chan'], sp['chan'], sp['state'], sp['state'], sp['b'], sp['b'], sp['c'], sp['c'],
                           sp['lam'], sp['lam']],
                 out_specs=[sp['chan'], blk(CHAN_BLOCK, STATE_BLOCK), blk(CHAN_BLOCK, STATE_BLOCK),
                            blk(STATE_BLOCK, CHAN_BLOCK), blk(STATE_BLOCK, CHAN_BLOCK), blk(1, STATE_BLOCK),
                            blk(1, STATE_BLOCK)],
                 out_shape=[_sds((T, D_SSM)), _sds((ncb, CHAN_BLOCK, STATE_BLOCK)), _sds((ncb, CHAN_BLOCK, STATE_BLOCK)),
                            _sds((ncb, STATE_BLOCK, CHAN_BLOCK)), _sds((ncb, STATE_BLOCK, CHAN_BLOCK)),
                            _sds((ncb, 1, STATE_BLOCK)), _sds((ncb, 1, STATE_BLOCK))],
                 scratch=[pltpu.VMEM((T + SUBLANES, STATE_BLOCK), F32), pltpu.VMEM((T + SUBLANES, STATE_BLOCK), F32)],
                 sem=('arbitrary',), vmem=VMEM_MOST, ride=ride)(dy_perm, u_perm, s_re, s_im, b_re, b_im, c_re, c_im,
                                                                lam_r, lam_i)


def _ffn_dact(ddn, wd4, hid4, tm):
    T = ddn.shape[0]
    nb = T // tm

    def body(d_ref, w_ref, hid_ref, o_ref, gw_ref, acc_ref):
        i = pl.program_id(1)
        d = d_ref[...]
        dact = _dot_nt(d, w_ref[...])
        silu, dsilu = _silu_parts(hid_ref[0].astype(F32))
        hid_v = hid_ref[1].astype(F32)
        o_ref[0] = (dact * hid_v * dsilu).astype(BF16)
        o_ref[1] = (dact * silu).astype(BF16)
        part = _dot_tn((silu * hid_v).astype(BF16), d)

        @pl.when(i == 0)
        def _():
            acc_ref[...] = part

        @pl.when(i > 0)
        def _():
            acc_ref[...] += part

        @pl.when(i == nb - 1)
        def _():
            gw_ref[...] = acc_ref[...].astype(BF16)

    blk = pl.BlockSpec((2, None, tm, FF_SHARD), lambda j, i: (0, j, i, 0))
    w_blk = pl.BlockSpec((None, FF_SHARD, D_MODEL), lambda j, i: (j, 0, 0))
    return _call(body, name='ffn_dact', grid=(4, nb),
                 in_specs=[pl.BlockSpec((tm, D_MODEL), lambda j, i: (i, 0)), w_blk, blk],
                 out_specs=[blk, w_blk],
                 out_shape=[_sds((2, 4, T, FF_SHARD), BF16), _sds((4, FF_SHARD, D_MODEL), BF16)],
                 scratch=[pltpu.VMEM((FF_SHARD, D_MODEL), F32)], sem=('parallel', 'arbitrary'),
                 vmem=VMEM_BIG)(ddn, wd4, hid4)


def _ffn_dup(dhid8, up8, cw8, tm, ride):
    T = up8.shape[1]
    nb = T // tm
    ha = _halo_after(tm, T, HALO16)

    def body(dh_ref, dha_ref, up_ref, cw_ref, dup_ref, dcw_ref):
        i = pl.program_id(1)

        @pl.when(i == 0)
        def _():
            dcw_ref[...] = jnp.zeros_like(dcw_ref)

        dh = dh_ref[...].astype(F32)
        dup, dh1, dh2 = _conv3_t(dh, jnp.where(i < nb - 1, dha_ref[...].astype(F32), 0.0), cw_ref)
        dup_ref[...] = dup.astype(BF16)
        up = up_ref[...].astype(F32)
        dcw_ref[0:1, :] += _colsum(dh2 * up)
        dcw_ref[1:2, :] += _colsum(dh1 * up)
        dcw_ref[2:3, :] += _colsum(dh * up)

    main = pl.BlockSpec((None, tm, FF_SHARD), lambda j, i: (j, i, 0))
    return _call(body, name='ffn_dup', grid=(N_DEV, nb),
                 in_specs=[main, pl.BlockSpec((None, HALO16, FF_SHARD), lambda j, i: (j, ha(i), 0)), main,
                           pl.BlockSpec((None, 3, FF_SHARD), lambda j, i: (j, 0, 0))],
                 out_specs=[main, pl.BlockSpec((None, 8, FF_SHARD), lambda j, i: (j, 0, 0))],
                 out_shape=[_sds((N_DEV, T, FF_SHARD), BF16), _sds((N_DEV, 8, FF_SHARD))],
                 sem=('parallel', 'arbitrary'), vmem=VMEM_BIG, ride=ride)(dhid8, dhid8, up8, cw8)


def _grad_tn(a, b, a_spec, b_spec, groups, m, n, tk, name, ride=None, parts=1):
    T = a.shape[-2]
    nk = T // tk
    mp = m // parts

    def body(a_ref, b_ref, *refs):
        o_refs, acc_ref = refs[:parts], refs[parts]
        k = pl.program_id(1)
        part = _dot_tn(a_ref[...], b_ref[...])

        @pl.when(k == 0)
        def _():
            acc_ref[...] = part

        @pl.when(k > 0)
        def _():
            acc_ref[...] += part

        @pl.when(k == nk - 1)
        def _():
            for p, o_ref in enumerate(o_refs):
                o_ref[...] = acc_ref[p * mp:(p + 1) * mp, :].astype(BF16)

    out_spec = pl.BlockSpec((None, mp, n), lambda g, k: (g, 0, 0))
    res = _call(body, name=name, grid=(groups, nk), in_specs=[a_spec, b_spec], out_specs=[out_spec] * parts,
                out_shape=[_sds((groups, mp, n), BF16)] * parts, scratch=[pltpu.VMEM((m, n), F32)],
                sem=('parallel', 'arbitrary'), vmem=VMEM_BIG, ride=ride)(a, b)
    if parts > 1:
        return res
    return res[0] if ride is None else (res[0][0], res[1])


def _grad_w_in(h1, dproj, tk, ride):
    T = h1.shape[0]
    nk = T // tk
    half = D_IN_PROJ // 2

    def body(a_ref, b_ref, o_ref, acc_ref):
        k = pl.program_id(0)
        for h in range(2):
            cols = slice(h * half, (h + 1) * half)
            part = _dot_tn(a_ref[...], b_ref[:, cols])

            @pl.when(k == 0)
            def _():
                acc_ref[:, cols] = part

            @pl.when(k > 0)
            def _():
                acc_ref[:, cols] += part

        @pl.when(k == nk - 1)
        def _():
            for g in range(N_DEV):
                o_ref[g] = acc_ref[:, g * IN_SHARD:(g + 1) * IN_SHARD].astype(BF16)

    return _call(body, name='grad_w_in', grid=(nk,),
                 in_specs=[pl.BlockSpec((tk, D_MODEL), lambda k: (k, 0)), pl.BlockSpec((tk, D_IN_PROJ), lambda k: (k, 0))],
                 out_specs=_const((N_DEV, D_MODEL, IN_SHARD)), out_shape=_sds((N_DEV, D_MODEL, IN_SHARD), BF16),
                 scratch=[pltpu.VMEM((D_MODEL, D_IN_PROJ), F32)], sem=('arbitrary',), vmem=VMEM_BIG, ride=ride)(h1, dproj)


def _pre_norm_bwd(dz, dz_spec, w_parts, xin, dres, sc, g, tm, name, ride, below=None, group=1, w_t=False):
    T = xin.shape[0]
    n = w_parts[0].shape[1] if w_t else w_parts[0].shape[2]
    mul = _dot if w_t else _dot_nt
    steps = N_DEV // group
    width = D_MODEL // len(w_parts)

    def body(dz_ref, *refs):
        w_refs, (x_ref, dr_ref, sc_ref, g_ref), refs = refs[:len(w_parts)], refs[len(w_parts):len(w_parts) + 4], \
            refs[len(w_parts) + 4:]
        if below is None:
            dx_ref, dsh_ref, dsc_ref, dg_ref = refs
            sums = (dsh_ref, dsc_ref, dg_ref)
        else:
            v_ref, gate_ref, g2_ref, dx_ref, dsh_ref, dsc_ref, dg_ref, dv_ref, dgate_ref, dg2_ref = refs
            sums = (dsh_ref, dsc_ref, dg_ref, dgate_ref, dg2_ref)
        i, j = pl.program_id(0), pl.program_id(1)
        piece = (lambda s: dz_ref[s]) if dz.ndim == 3 else (lambda s: dz_ref[:, s * n:(s + 1) * n])
        parts = []
        for w_ref in w_refs:
            part = mul(piece(0), w_ref[0])
            for s in range(1, group):
                part = part + mul(piece(s), w_ref[s])
            parts.append(part)

        @pl.when(jnp.logical_and(i == 0, j == 0))
        def _():
            for s_ref in sums:
                s_ref[...] = jnp.zeros_like(s_ref)

        @pl.when(j == 0)
        def _():
            for k, part in enumerate(parts):
                dx_ref[:, k * width:(k + 1) * width] = part

        @pl.when(j > 0)
        def _():
            for k, part in enumerate(parts):
                dx_ref[:, k * width:(k + 1) * width] += part

        @pl.when(j == steps - 1)
        def _():
            dh, xv, gv = dx_ref[...], x_ref[...], g_ref[...]
            r = _rsqrt_mean(xv)
            dsh_ref[...] += _colsum(dh)
            dsc_ref[...] += _colsum(dh * (xv * r * gv))
            dxn = dh * (1.0 + sc_ref[...])
            dg_ref[...] += _colsum(dxn * xv * r)
            dx = dr_ref[...] + _norm_bwd(dxn, xv, r, gv)
            dx_ref[...] = dx
            if below is not None:
                v, g2 = v_ref[...].astype(F32), g2_ref[...]
                rv = _rsqrt_mean(v)
                dgate_ref[...] += _colsum(dx * (v * rv * g2))
                dn = dx * gate_ref[...]
                dg2_ref[...] += _colsum(dn * v * rv)
                dv_ref[...] = _norm_bwd(dn, v, rv, g2).astype(BF16)

    row = pl.BlockSpec((tm, D_MODEL), lambda i, j: (i, 0))
    vec = _const((1, D_MODEL))
    in_specs = [dz_spec] + [pl.BlockSpec((group,) + w.shape[1:], lambda i, j: (j, 0, 0)) for w in w_parts]
    in_specs += [row, row, vec, vec]
    out_specs = [row, vec, vec, vec]
    out_shape = [_sds((T, D_MODEL)), _sds((1, D_MODEL)), _sds((1, D_MODEL)), _sds((1, D_MODEL))]
    args = [dz, *w_parts, xin, dres, sc, g]
    if below is not None:
        in_specs += [row, vec, vec]
        out_specs += [row, vec, vec]
        out_shape += [_sds((T, D_MODEL), BF16), _sds((1, D_MODEL)), _sds((1, D_MODEL))]
        args += list(below)
    return _call(body, name=name, grid=(T // tm, steps), in_specs=in_specs, out_specs=out_specs,
                 out_shape=out_shape, sem=('arbitrary', 'arbitrary'), vmem=VMEM_MOST, ride=ride)(*args)


def _mix_bwd(d_o, w_out, yssm, proj, d, glu_w, glu_b, g_ssm, cw, g_conv, avg16, avg64, tm, ride):
    T = yssm.shape[0]
    hb = _halo_before(tm)

    def body(do_ref, wo_ref, y_ref, p_ref, ph_ref, d_ref, gw_ref, gb_ref, gs_ref, cw_ref, gc_ref, a16_ref, a64_ref,
             dy_ref, dconv_ref, dbg_ref, z_ref, dlin_ref, acc_ref):
        i = pl.program_id(0)
        dyc = _dot_nt(do_ref[...], wo_ref[...])

        @pl.when(i == 0)
        def _():
            acc_ref[...] = jnp.zeros_like(acc_ref)

        u = p_ref[:, 0:D_SSM]
        y = y_ref[...] + d_ref[...] * u
        z, t = _gelu(y)
        gate = _sigmoid(_dot(z.astype(BF16), gw_ref[...]) + gb_ref[...])
        ya = z * gate
        rs = lax.rsqrt(_dot_split(ya * ya, a16_ref[...], 2) + EPS)
        dna = dyc[:, 0:D_SSM]
        acc_ref[1:2, :] += _colsum(dna * ya * rs)
        dya = _head_norm_bwd(dna, ya, rs, gs_ref[...], a16_ref[...])
        dlin = dya * z * gate * (1.0 - gate)
        acc_ref[0:1, :] += _colsum(dlin)
        dlin_b = dlin.astype(BF16)
        dz = dya * gate + _dot_nt(dlin_b, gw_ref[...])
        dy = dz * _gelu_grad(y, t)
        acc_ref[3:4, :] += _colsum(dy * u)
        dy_ref[...] = dy
        z_ref[...] = z.astype(BF16)
        dlin_ref[...] = dlin_b

        bg = p_ref[:, D_SSM:D_SSM + D_CONV]
        cv = p_ref[:, D_SSM + D_CONV:D_SSM + 2 * D_CONV] * p_ref[:, D_SSM + 2 * D_CONV:D_IN_PROJ]
        hv = ph_ref[:, D_SSM + D_CONV:D_SSM + 2 * D_CONV] * ph_ref[:, D_SSM + 2 * D_CONV:D_IN_PROJ]
        hv = jnp.where(i > 0, hv, 0.0)
        conv, cv1, cv2 = _conv3(cv, hv, cw_ref)
        yb = bg * conv
        rsb = lax.rsqrt(_dot_split(yb * yb, a64_ref[...], 2) + EPS)
        dnb = dyc[:, D_SSM:D_MODEL]
        acc_ref[2:3, :] += _colsum(dnb * yb * rsb)
        dyb = _head_norm_bwd(dnb, yb, rsb, gc_ref[...], a64_ref[...])
        dbg_ref[...] = dyb * conv
        dconv = dyb * bg
        dconv_ref[...] = dconv
        acc_ref[4:5, :] += _colsum(dconv * cv2)
        acc_ref[5:6, :] += _colsum(dconv * cv1)
        acc_ref[6:7, :] += _colsum(dconv * cv)

    vec = _const((1, D_SSM))
    sq = _const((D_SSM, D_SSM))
    half = pl.BlockSpec((tm, D_SSM), lambda i: (i, 0))
    return _call(body, name='mix_bwd', grid=(T // tm,),
                 in_specs=[pl.BlockSpec((tm, D_MODEL), lambda i: (i, 0)), _const((D_MODEL, D_MODEL)), half,
                           pl.BlockSpec((tm, D_IN_PROJ), lambda i: (i, 0)),
                           pl.BlockSpec((HALO, D_IN_PROJ), lambda i: (hb(i), 0)), vec, sq, vec, vec,
                           _const((3, D_CONV)), vec, sq, sq],
                 out_specs=[half, half, half, half, half, _const((8, D_SSM))],
                 out_shape=[_sds((T, D_SSM)), _sds((T, D_SSM)), _sds((T, D_SSM)), _sds((T, D_SSM), BF16),
                            _sds((T, D_SSM), BF16), _sds((8, D_SSM))],
                 sem=('arbitrary',), vmem=VMEM_BIG, ride=ride)(d_o, w_out, yssm, proj, proj, d, glu_w, glu_b, g_ssm, cw,
                                                              g_conv, avg16, avg64)


def _mix_bwd_proj(dconv, proj, du_ssm, dy, d, dbg, cw, tm):
    T = dy.shape[0]
    nb = T // tm
    ha = _halo_after(tm, T)

    def body(dc_ref, dch_ref, cg_ref, v_ref, du_ref, dy_ref, d_ref, dbg_ref, cw_ref, o_ref):
        i = pl.program_id(0)
        dcv = _conv3_t(dc_ref[...], jnp.where(i < nb - 1, dch_ref[...], 0.0), cw_ref)[0]
        o_ref[:, 0:D_SSM] = (du_ref[...] + dy_ref[...] * d_ref[...]).astype(BF16)
        o_ref[:, D_SSM:D_SSM + D_CONV] = dbg_ref[...].astype(BF16)
        o_ref[:, D_SSM + D_CONV:D_SSM + 2 * D_CONV] = (dcv * v_ref[...]).astype(BF16)
        o_ref[:, D_SSM + 2 * D_CONV:D_IN_PROJ] = (dcv * cg_ref[...]).astype(BF16)

    half = pl.BlockSpec((tm, D_SSM), lambda i: (i, 0))
    return _call(body, name='mix_bwd_proj', grid=(nb,),
                 in_specs=[half, pl.BlockSpec((HALO, D_CONV), lambda i: (ha(i), 0)),
                           pl.BlockSpec((tm, D_CONV), lambda i: (i, 2)), pl.BlockSpec((tm, D_CONV), lambda i: (i, 3)),
                           half, half, _const((1, D_SSM)), half, _const((3, D_CONV))],
                 out_specs=pl.BlockSpec((tm, D_IN_PROJ), lambda i: (i, 0)), out_shape=_sds((T, D_IN_PROJ), BF16),
                 sem=('parallel',), vmem=VMEM_BIG)(dconv, dconv, proj, proj, du_ssm, dy, d, dbg, cw)


ADAMW_SLOT_BYTES = 8 << 20
ADAMW_ROW_BYTES = 3 << 19


def _row_tile(rows, cols, slots):
    for cand in range(rows, 15, -1):
        if (rows % cand == 0 and cand % 16 == 0 and slots * cand * cols * 4 <= ADAMW_SLOT_BYTES
                and cand * cols * 4 <= ADAMW_ROW_BYTES):
            return cand
    return rows


def _adamw_math(g, w, m, v):
    m2 = ADAM_B1 * m + (1.0 - ADAM_B1) * g
    v2 = ADAM_B2 * v + (1.0 - ADAM_B2) * (g * g)
    m_hat = m2 / (1.0 - ADAM_B1 ** ADAM_STEP)
    v_hat = v2 / (1.0 - ADAM_B2 ** ADAM_STEP)
    return -ADAM_LR * (m_hat / (jnp.sqrt(v_hat) + ADAM_EPS) + ADAM_WD * w), m2, v2


def _adamw(pieces, w, m, v, name):
    slots, _, cols = pieces[0].shape
    rows = sum(p.shape[1] for p in pieces)
    tr = _row_tile(pieces[0].shape[1], cols, slots)
    starts, pos = [], 0
    for p in pieces:
        assert p.shape[1] % tr == 0
        starts.append(pos)
        pos += p.shape[1] // tr

    def body(*refs):
        g_refs = refs[:len(pieces)]
        w_ref, m_ref, v_ref, go_ref, d_ref, mo_ref, vo_ref = refs[len(pieces):]
        i = pl.program_id(0)
        g = None
        for g_ref, start in zip(g_refs, starts):
            part = g_ref[0].astype(F32)
            for s in range(1, slots):
                part = part + g_ref[s].astype(F32)
            g = part if g is None else jnp.where(i >= start, part, g)
        go_ref[...] = g
        d_ref[...], mo_ref[...], vo_ref[...] = _adamw_math(g, w_ref[...], m_ref[...], v_ref[...])

    def piece_spec(start, count):
        return pl.BlockSpec((slots, tr, cols), lambda i: (0, jnp.clip(i - start, 0, count - 1), 0))

    blk = pl.BlockSpec((tr, cols), lambda i: (i, 0))
    return _call(body, name=name, grid=(rows // tr,),
                 in_specs=[piece_spec(s, p.shape[1] // tr) for s, p in zip(starts, pieces)] + [blk, blk, blk],
                 out_specs=[blk] * 4, out_shape=[_sds((rows, cols))] * 4, sem=('parallel',),
                 vmem=VMEM_BIG)(*pieces, w, m, v)


def _to_scan_rows(a):
    T, n = a.shape
    return a.reshape(SUBLANES, T // SUBLANES, n).transpose(1, 0, 2).reshape(T, n)


def _from_scan_rows(a):
    T, n = a.shape
    return a.reshape(T // SUBLANES, SUBLANES, n).transpose(1, 0, 2).reshape(T, n)


def _expand(a):
    return jnp.repeat(a, SSM_GROUP, axis=1)


def _block_diag(rows, row_group, col_group):
    r, n = rows.shape
    tiled = jnp.tile(rows, (1, N_GROUPS))
    keep = (jnp.arange(r)[:, None] // row_group) == (jnp.arange(n * N_GROUPS)[None, :] // col_group)
    return jnp.where(keep, tiled, 0.0)


def _block_diag_b(bb):
    return _block_diag(bb.transpose(0, 2, 1).reshape(D_SSM, SSM_STATE), SSM_GROUP, SSM_STATE)


def _block_diag_c(cc):
    return _block_diag(cc.transpose(0, 2, 1).reshape(N_STATE, SSM_GROUP), SSM_STATE, SSM_GROUP)


def _diag_blocks(x, chan_major):
    per = CHAN_BLOCK // SSM_GROUP
    eye = jnp.eye(per, dtype=x.dtype)
    if chan_major:
        x = x.reshape(-1, per, SSM_GROUP, per, SSM_STATE) * eye[None, :, None, :, None]
        return x.sum(axis=1).transpose(0, 2, 3, 1).reshape(N_GROUPS, SSM_STATE, SSM_GROUP)
    x = x.reshape(-1, per, SSM_STATE, per, SSM_GROUP) * eye[None, :, None, :, None]
    return x.sum(axis=3).reshape(N_GROUPS, SSM_STATE, SSM_GROUP)


SMALL_LAYOUT = {
    'ssm_b_re': (0, 0, 32, 1024), 'ssm_b_im': (32, 0, 32, 1024), 'ssm_c_re': (64, 0, 32, 1024),
    'ssm_c_im': (96, 0, 32, 1024), 'b_ada': (128, 0, 6, 1024), 'g_pre_mix': (134, 0, 1, 1024),
    'g_post_mix': (135, 0, 1, 1024), 'ssm_lam_re': (136, 0, 2, 1024), 'ssm_lam_im': (138, 0, 2, 1024),
    'ssm_log_step': (140, 0, 1, 32), 'glu_b': (141, 0, 1, 512), 'g_out_ssm': (141, 512, 1, 512),
    'g_out_conv': (142, 0, 1, 512), 'ssm_d': (142, 512, 1, 512), 'g_pre_ffn': (143, 0, 1, 1024),
    'g_post_ffn': (144, 0, 1, 1024)}
SMALL_ROWS = 152
B_ADA_ROW = SMALL_LAYOUT['b_ada'][0]
LATE_ROWS = {('b_ada', 0): 0, ('b_ada', 1): 1, ('g_pre_mix', 0): 2}


def _adamw_small(gathered, late, wts, mom_m, mom_v):
    names = list(SMALL_LAYOUT)
    n = len(names)

    def body(*refs):
        g_ref, late_ref, ins, outs = refs[0], refs[1], refs[2:2 + 3 * n], refs[2 + 3 * n:]
        for p, name in enumerate(names):
            r0, c0, rows, cols = SMALL_LAYOUT[name]
            pieces = [(0, rows)] if rows % 8 == 0 else [(r, 1) for r in range(rows)]
            for r, cnt in pieces:
                src_ref, first = (late_ref, LATE_ROWS[name, r]) if (name, r) in LATE_ROWS else (g_ref, r0 + r)
                g = src_ref[0, first:first + cnt, c0:c0 + cols]
                for s in range(1, N_DEV):
                    g = g + src_ref[s, first:first + cnt, c0:c0 + cols]
                w, m, v = (ins[3 * p + q][r:r + cnt, :] for q in range(3))
                res = (g,) + _adamw_math(g, w, m, v)
                for q in range(4):
                    outs[4 * p + q][r:r + cnt, :] = res[q]

    shapes = [SMALL_LAYOUT[name][2:] for name in names]
    args = [gathered, late]
    for name, shp in zip(names, shapes):
        args += [wts[name].reshape(shp), mom_m[name].reshape(shp), mom_v[name].reshape(shp)]
    outs = _call(body, name='adamw_small', grid=(1,),
                 in_specs=[_const(gathered.shape), _const(late.shape)]
                 + [_const(shp) for shp in shapes for _ in range(3)],
                 out_specs=[_const(shp) for shp in shapes for _ in range(4)],
                 out_shape=[_sds(shp) for shp in shapes for _ in range(4)], vmem=VMEM_BIG)(*args)
    res = {}
    for p, name in enumerate(names):
        for q, kind in enumerate(('g', 'd', 'm', 'v')):
            res[kind, name] = outs[4 * p + q].reshape(wts[name].shape)
    return res


def kernel(x, c, w_ada, b_ada, g_pre_mix, g_post_mix, w_in, ssm_lam_re, ssm_lam_im, ssm_log_step, ssm_b_re, ssm_b_im, ssm_c_re, ssm_c_im, ssm_d, glu_w, glu_b, g_out_ssm, conv_w, g_out_conv, w_out, g_pre_ffn, g_post_ffn, w_up, ffn_conv_w, w_down, loss_target, m_w_ada, m_b_ada, m_g_pre_mix, m_g_post_mix, m_w_in, m_ssm_lam_re, m_ssm_lam_im, m_ssm_log_step, m_ssm_b_re, m_ssm_b_im, m_ssm_c_re, m_ssm_c_im, m_ssm_d, m_glu_w, m_glu_b, m_g_out_ssm, m_conv_w, m_g_out_conv, m_w_out, m_g_pre_ffn, m_g_post_ffn, m_w_up, m_ffn_conv_w, m_w_down, v_w_ada, v_b_ada, v_g_pre_mix, v_g_post_mix, v_w_in, v_ssm_lam_re, v_ssm_lam_im, v_ssm_log_step, v_ssm_b_re, v_ssm_b_im, v_ssm_c_re, v_ssm_c_im, v_ssm_d, v_glu_w, v_glu_b, v_g_out_ssm, v_conv_w, v_g_out_conv, v_w_out, v_g_pre_ffn, v_g_post_ffn, v_w_up, v_ffn_conv_w, v_w_down):
    args = dict(locals())
    wts = {n: args[n] for n in WEIGHTS}
    mom_m = {n: args['m_' + n] for n in WEIGHTS}
    mom_v = {n: args['v_' + n] for n in WEIGHTS}
    T = x.shape[1]
    tm = min(512, T)
    tw = min(1024, T)
    tk = min(2048, T)
    me = _me()[3]
    xt, tgt = x[0], loss_target[0]

    c_all, w_in_s = _exchange([c, w_in[0].astype(BF16)], name='gather_first', scatter=False)
    c_all = c_all.reshape(N_DEV, D_MODEL)
    b_cols = lax.dynamic_slice(b_ada, (0, me * ADA_SHARD), (1, ADA_SHARD))
    mod_cols, c_act = _mod_cols(c_all, w_ada[0], b_cols)
    (mod_all,) = _exchange([mod_cols], name='gather_mod', scatter=False)
    mod = lax.dynamic_slice(mod_all, (0, me, 0), (N_DEV, 1, ADA_SHARD)).reshape(N_MOD, 1, D_MODEL)
    sh1, sc1, gt1, sh2, sc2, gt2 = [mod[k] for k in range(N_MOD)]


    lre_x, lim_x = _expand(ssm_lam_re[0]), _expand(ssm_lam_im[0])
    lst_x = jnp.broadcast_to(ssm_log_step[0][:, None], (N_GROUPS, SSM_STATE * SSM_GROUP))
    b_re_x = ssm_b_re[0].reshape(N_GROUPS, -1)
    b_im_x = ssm_b_im[0].reshape(N_GROUPS, -1)
    ar_x, ai_x, bbr_x, bbi_x = _ssm_prep(lre_x, lim_x, lst_x, b_re_x, b_im_x)
    lam_r = ar_x[:, ::SSM_GROUP].reshape(1, N_STATE)
    lam_i = ai_x[:, ::SSM_GROUP].reshape(1, N_STATE)
    big_b_re = _block_diag_b(bbr_x.reshape(N_GROUPS, SSM_STATE, SSM_GROUP)).astype(BF16)
    big_b_im = _block_diag_b(bbi_x.reshape(N_GROUPS, SSM_STATE, SSM_GROUP)).astype(BF16)
    big_c_re = _block_diag_c(ssm_c_re[0]).astype(BF16)
    big_c_im = _block_diag_c(ssm_c_im[0]).astype(BF16)
    head = jnp.arange(D_SSM)
    avg16 = jnp.where(head[:, None] // SSM_GROUP == head[None, :] // SSM_GROUP, 1.0 / SSM_GROUP, 0.0).astype(BF16)
    hd = D_CONV // CONV_HEADS
    avg64 = jnp.where(head[:, None] // hd == head[None, :] // hd, 1.0 / hd, 0.0).astype(BF16)

    w_up_t, half = w_up[0].T, D_MODEL // 2
    (proj, h1), (ffn_conv_s, conv_s, w_up_a) = _pre_mix(
        xt, sc1, sh1, g_pre_mix, w_in_s, tw, ([ffn_conv_w[0], conv_w[0], w_up_t[:, :half].astype(BF16)], False))
    cw_full = conv_s.transpose(1, 0, 2).reshape(3, D_CONV)
    u_perm = _to_scan_rows(proj[:, :D_SSM])
    (s_re, s_im, y_perm), (w_up_b, glu_s, w_out_s) = _ssm_fwd(
        u_perm, big_b_re, big_b_im, big_c_re, big_c_im, lam_r, lam_i,
        ([w_up_t[:, half:].astype(BF16), glu_w[0].astype(BF16), w_out[0].astype(BF16)], False))
    glu_full = glu_s.reshape(D_SSM, D_SSM)
    w_out_full = w_out_s.reshape(D_MODEL, D_MODEL)
    yssm = _from_scan_rows(y_perm)
    mix_args = (ssm_d, glu_full, glu_b, g_out_ssm, cw_full, g_out_conv, avg16, avg64)
    ycat = _mix_fwd(yssm, proj, *mix_args, tw)
    o, x1, h2 = _out_proj(ycat, w_out_full, xt, gt1, g_post_mix, g_pre_ffn, sc2, sh2, tw)
    (up8, hid8), (w_down_s,) = _ffn_up(h2, w_up_a, w_up_b, ffn_conv_s, tw, ([w_down[0].astype(BF16)], False))
    wd4 = w_down_s.reshape(4, FF_SHARD, D_MODEL)
    hid4 = hid8.reshape(2, 4, T, FF_SHARD)
    ddn, dx2, loss_parts, d_gt2, d_g_post_ffn = _ffn_down(hid4, wd4, x1, tgt, gt2, g_post_ffn, tm)
    loss_local = jnp.sum(loss_parts[:, 0, 0])

    got = {}
    dhid, g_w_down = _ffn_dact(ddn, wd4, hid4, tw)
    (dup8, dcw_ffn), (got['w_down'],) = _ffn_dup(dhid.reshape(N_DEV, T, FF_SHARD), up8, ffn_conv_s, tw,
                                                 ([g_w_down.reshape(N_DEV, D_FF // N_DEV, D_MODEL)], True))
    g_w_up_halves = _grad_tn(dup8, h2, pl.BlockSpec((None, T, FF_SHARD), lambda g, k: (g, k, 0)),
                             pl.BlockSpec((T, D_MODEL), lambda g, k: (k, 0)), N_DEV, FF_SHARD, D_MODEL, T,
                             'grad_w_up', parts=2)
    (dx1, d_sh2, d_sc2, d_g_pre_ffn, d_o, d_gt1, d_g_post_mix), (got_up_0, got['ffn_conv_w']) = _pre_norm_bwd(
        dup8, pl.BlockSpec((2, tw, FF_SHARD), lambda i, j: (j, i, 0)), [w_up_a, w_up_b], x1, dx2, sc2, g_pre_ffn, tw,
        'ffn_in_bwd', ([g_w_up_halves[0], dcw_ffn], True), below=(o, gt1, g_post_mix), group=2, w_t=True)

    g_w_out = _grad_tn(ycat, d_o, pl.BlockSpec((tk, D_MODEL), lambda g, k: (k, 0)),
                       pl.BlockSpec((tk, D_MODEL), lambda g, k: (k, 0)), 1, D_MODEL, D_MODEL, tk, 'grad_w_out')
    (dy, dconv, dbg, z_b, dlin_b, sums), (got['w_out'],) = _mix_bwd(
        d_o, w_out_full, yssm, proj, *mix_args, tm, ([g_w_out.reshape(N_DEV, D_MODEL // N_DEV, D_MODEL)], True))
    g_glu_w = _grad_tn(z_b, dlin_b, pl.BlockSpec((tk, D_SSM), lambda g, k: (k, 0)),
                       pl.BlockSpec((tk, D_SSM), lambda g, k: (k, 0)), 1, D_SSM, D_SSM, tk, 'grad_glu_w')
    dy_perm = _to_scan_rows(dy)
    (du_perm, dbr_blk, dbi_blk, dcr_blk, dci_blk, dar_blk, dai_blk), (got_up_1, got['glu_w']) = _ssm_bwd(
        dy_perm, u_perm, s_re, s_im, big_b_re, big_b_im, big_c_re, big_c_im, lam_r, lam_i,
        ([g_w_up_halves[1], g_glu_w.reshape(N_DEV, D_SSM // N_DEV, D_SSM)], True))
    du_ssm = _from_scan_rows(du_perm)
    dproj = _mix_bwd_proj(dconv, proj, du_ssm, dy, ssm_d, dbg, cw_full, tw)
    dbb_re = _diag_blocks(dbr_blk, True).reshape(N_GROUPS, -1)
    dbb_im = _diag_blocks(dbi_blk, True).reshape(N_GROUPS, -1)
    d_c_re = _diag_blocks(dcr_blk, False).transpose(0, 2, 1)
    d_c_im = _diag_blocks(dci_blk, False).transpose(0, 2, 1)
    lane = jnp.arange(SSM_STATE * SSM_GROUP)
    seg = jnp.where(lane[:, None] // SSM_GROUP == lane[None, :] // SSM_GROUP, 1.0, 0.0).astype(BF16)
    d_b_re_x, d_b_im_x, d_lre_x, d_lim_x, d_lst = _ssm_prep_bwd(
        lre_x, lim_x, lst_x, b_re_x, b_im_x, dbb_re, dbb_im, _expand(dar_blk.reshape(N_GROUPS, SSM_STATE)),
        _expand(dai_blk.reshape(N_GROUPS, SSM_STATE)), seg)

    row = lambda a: a.reshape(-1, PACK_COLS)
    blank = jnp.zeros((1, PACK_COLS), F32)
    small_pack = jnp.concatenate([
        d_b_re_x, d_b_im_x, row(d_c_re), row(d_c_im), blank, blank, d_gt1, d_sh2, d_sc2, d_gt2, blank,
        d_g_post_mix, row(d_lre_x[:, ::SSM_GROUP]), row(d_lim_x[:, ::SSM_GROUP]),
        jnp.pad(d_lst.reshape(1, N_GROUPS), ((0, 0), (0, PACK_COLS - N_GROUPS))), row(sums[0:4]), d_g_pre_ffn,
        d_g_post_ffn, jnp.zeros((SMALL_ROWS - 145, PACK_COLS), F32)])
    g_w_in, (small_all,) = _grad_w_in(h1, dproj, tk, ([small_pack], False))
    g_conv_slots = jnp.concatenate([sums[4:7], jnp.zeros((5, D_CONV), F32)]).reshape(
        8, N_DEV, D_CONV // N_DEV).transpose(1, 0, 2)
    (grad_x, d_sh1, d_sc1, d_g_pre_mix), (got['w_in'], got['conv_w']) = _pre_norm_bwd(
        dproj, pl.BlockSpec((tw, 4 * IN_SHARD), lambda i, j: (i, j)), [w_in_s], xt, dx1, sc1, g_pre_mix, tw,
        'mix_in_bwd', ([g_w_in, g_conv_slots], True), group=4)
    late_pack = jnp.concatenate([d_sh1, d_sc1, d_g_pre_mix, jnp.full((1, PACK_COLS), loss_local, F32),
                                 jnp.zeros((4, PACK_COLS), F32)])
    (late_all,) = _exchange([late_pack], name='gather_late_grads', scatter=False)
    loss = jnp.sum(late_all[:, 3, 0])
    res = _adamw_small(small_all, late_all, wts, mom_m, mom_v)

    dmod_all = jnp.concatenate([late_all[:, 0:2, :], small_all[:, B_ADA_ROW + 2:B_ADA_ROW + N_MOD, :]],
                               axis=1).reshape(N_DEV, N_MOD * D_MODEL)
    dmod_cols = lax.dynamic_slice(dmod_all, (0, me * ADA_SHARD), (N_DEV, ADA_SHARD))
    g_w_ada = _grad_w_ada(c_act.T, dmod_cols)

    pieces = {n: [slots[:, :3, :] if n in ('conv_w', 'ffn_conv_w') else slots] for n, slots in got.items()}
    for n, parts in pieces.items():
        outs = _adamw(parts, wts[n][0], mom_m[n][0], mom_v[n][0], 'adamw_' + n)
        for kind, val in zip(('g', 'd', 'm', 'v'), outs):
            res[kind, n] = val[None]
    outs = _adamw([got_up_0, got_up_1], w_up[0].T, m_w_up[0].T, v_w_up[0].T, 'adamw_w_up')
    for kind, val in zip(('g', 'd', 'm', 'v'), outs):
        res[kind, 'w_up'] = val.T[None]
    outs = _adamw([g_w_ada[None]], w_ada[0], m_w_ada[0], v_w_ada[0], 'adamw_w_ada')
    for kind, val in zip(('g', 'd', 'm', 'v'), outs):
        res[kind, 'w_ada'] = val[None]

    return (loss, grad_x[None], *[res['g', n] for n in WEIGHTS], *[res['d', n] for n in WEIGHTS],
            *[res['m', n] for n in WEIGHTS], *[res['v', n] for n in WEIGHTS])
```

```python
import math

import jax
import jax.numpy as jnp
from jax import lax
from jax.experimental import pallas as pl
from jax.experimental.pallas import tpu as pltpu

F32, BF16 = jnp.float32, jnp.bfloat16

D_MODEL = 1024
D_SSM = 512
D_CONV = 512
SSM_GROUP = 16
N_GROUPS = 32
SSM_STATE = 64
N_STATE = N_GROUPS * SSM_STATE
CONV_HEADS = 8
D_FF = 2816
N_MOD = 6
D_IN_PROJ = D_SSM + 3 * D_CONV
N_DEV = 8
FF_SHARD = 2 * D_FF // N_DEV
IN_SHARD = D_IN_PROJ // N_DEV
ADA_SHARD = N_MOD * D_MODEL // N_DEV
EPS = 1e-6
LAMBDA_RE_MAX = -1e-4
ADAM_LR, ADAM_B1, ADAM_B2, ADAM_EPS, ADAM_WD, ADAM_STEP = 0.001, 0.9, 0.999, 1e-08, 0.01, 10
GELU_C = math.sqrt(2.0 / math.pi)
GELU_A = 0.044715

SUBLANES = 8
HALO = 8
HALO16 = 16
SCAN_UNROLL = 16
STATE_BLOCK = 512
CHAN_BLOCK = 128
VMEM_BIG = 48 << 20
VMEM_MOST = 58 << 20

WEIGHTS = ['w_ada', 'b_ada', 'g_pre_mix', 'g_post_mix', 'w_in', 'ssm_lam_re', 'ssm_lam_im', 'ssm_log_step',
           'ssm_b_re', 'ssm_b_im', 'ssm_c_re', 'ssm_c_im', 'ssm_d', 'glu_w', 'glu_b', 'g_out_ssm', 'conv_w',
           'g_out_conv', 'w_out', 'g_pre_ffn', 'g_post_ffn', 'w_up', 'ffn_conv_w', 'w_down']
PACK_COLS = 1024


def _call(body, *, name, grid, in_specs, out_specs, out_shape, scratch=(), sem=None, vmem=None, ride=None):
    params = {}
    if vmem is not None:
        params['vmem_limit_bytes'] = vmem
    if ride is None:
        if sem is not None:
            params['dimension_semantics'] = sem
        return pl.pallas_call(body, name=name, grid=grid, in_specs=in_specs, out_specs=out_specs,
                              out_shape=out_shape, scratch_shapes=list(scratch),
                              compiler_params=pltpu.CompilerParams(**params))
    arrs, scatter = ride
    single = not isinstance(out_shape, (list, tuple))
    out_shape_l = [out_shape] if single else list(out_shape)
    out_specs_l = [out_specs] if single else list(out_specs)
    n, n_in, n_out, n_scr = len(arrs), len(in_specs), len(out_shape_l), len(scratch)
    any_spec = pl.BlockSpec(memory_space=pl.ANY)
    params['dimension_semantics'] = ('arbitrary',) * len(grid)

    def carried(*refs):
        ins, rin = refs[:n_in], refs[n_in:n_in + n]
        outs, rout = refs[n_in + n:n_in + n + n_out], refs[n_in + n + n_out:n_in + 2 * n + n_out]
        scr, sems = refs[n_in + 2 * n + n_out:n_in + 2 * n + n_out + n_scr], refs[n_in + 2 * n + n_out + n_scr:]
        first = pl.program_id(0) == 0
        last = pl.program_id(0) == grid[0] - 1
        for ax in range(1, len(grid)):
            first = jnp.logical_and(first, pl.program_id(ax) == 0)
            last = jnp.logical_and(last, pl.program_id(ax) == grid[ax] - 1)

        @pl.when(first)
        def _():
            _exchange_start(rin, rout, sems, scatter)

        body(*ins, *outs, *scr)

        @pl.when(last)
        def _():
            _exchange_wait(rin, rout, sems, scatter)

    call = pl.pallas_call(carried, name=name, grid=grid, in_specs=list(in_specs) + [any_spec] * n,
                          out_specs=out_specs_l + [any_spec] * n,
                          out_shape=out_shape_l + _exchange_shapes(arrs, scatter),
                          scratch_shapes=list(scratch) + _exchange_sems(n),
                          compiler_params=pltpu.CompilerParams(**params))

    def run(*args):
        res = call(*args, *arrs)
        own = res[0] if single else list(res[:n_out])
        return own, list(res[n_out:])

    return run


def _const(shape):
    nd = len(shape)
    return pl.BlockSpec(shape, lambda *_: (0,) * nd)


def _sds(shape, dtype=F32):
    return jax.ShapeDtypeStruct(shape, dtype)


def _dot(a, b):
    return jnp.dot(a, b, preferred_element_type=F32)


def _dot_nt(a, b):
    return lax.dot_general(a, b, (((1,), (1,)), ((), ())), preferred_element_type=F32)


def _dot_tn(a, b):
    return lax.dot_general(a, b, (((0,), (0,)), ((), ())), preferred_element_type=F32)


def _dot_split(x, mat, parts):
    acc = None
    rem = x
    for _ in range(parts):
        piece = rem.astype(BF16)
        rem = rem - piece.astype(F32)
        term = _dot(piece, mat)
        acc = term if acc is None else acc + term
    return acc


def _sigmoid(x):
    return 1.0 / (1.0 + jnp.exp(-x))


def _gelu(x):
    t = jnp.tanh(GELU_C * (x + GELU_A * x * x * x))
    return 0.5 * x * (1.0 + t), t


def _gelu_grad(x, t):
    return 0.5 * (1.0 + t) + 0.5 * x * (1.0 - t * t) * GELU_C * (1.0 + 3.0 * GELU_A * x * x)


def _rsqrt_mean(x):
    return lax.rsqrt(jnp.mean(x * x, axis=-1, keepdims=True) + EPS)


def _colsum(x):
    return jnp.sum(x, axis=0, keepdims=True)


def _shifts_down(x, halo):
    ext = jnp.concatenate([halo, x], axis=0)
    return pltpu.roll(ext, 1, 0)[halo.shape[0]:], pltpu.roll(ext, 2, 0)[halo.shape[0]:]


def _shifts_up(x, halo):
    n = x.shape[0]
    ext = jnp.concatenate([x, halo], axis=0)
    total = ext.shape[0]
    return pltpu.roll(ext, total - 1, 0)[:n], pltpu.roll(ext, total - 2, 0)[:n]


def _conv3(x, halo, w_ref):
    x1, x2 = _shifts_down(x, halo)
    return w_ref[0:1, :] * x2 + w_ref[1:2, :] * x1 + w_ref[2:3, :] * x, x1, x2


def _conv3_t(g, halo, w_ref):
    g1, g2 = _shifts_up(g, halo)
    return w_ref[2:3, :] * g + w_ref[1:2, :] * g1 + w_ref[0:1, :] * g2, g1, g2


def _silu_parts(x):
    s = _sigmoid(x)
    return x * s, s * (1.0 + x * (1.0 - s))


def _norm_bwd(dn, x, r, g):
    gd = g * dn
    return r * gd - x * (r * r * r) * jnp.mean(gd * x, axis=-1, keepdims=True)


def _head_norm_bwd(dn, y, rs, g, avg):
    gd = g * dn
    return rs * gd - y * (rs * rs * rs) * _dot_split(gd * y, avg, 2)


def _me():
    x, y, c = lax.axis_index('x'), lax.axis_index('y'), lax.axis_index('c')
    return x, y, c, 4 * x + 2 * y + c


def _peer(k):
    x, y, c, _ = _me()
    px = 1 - x if k & 4 else x
    py = 1 - y if k & 2 else y
    pc = 1 - c if k & 1 else c
    return (px, py, pc), 4 * px + 2 * py + pc


SIBLING = 1
OTHER_CHIPS = (2, 4, 6)


def _remote(src, dst, sems, a, k, dev):
    return pltpu.make_async_remote_copy(src_ref=src, dst_ref=dst, send_sem=sems[0].at[a, k - 1],
                                        recv_sem=sems[1].at[a, k - 1], device_id=dev,
                                        device_id_type=pl.DeviceIdType.MESH)


def _exchange_copies(ins, outs, sems, scatter):
    me = _me()[3]
    local, first, relay, arrivals = [], [], [], []
    for a in range(len(ins)):
        src = ins[a].at[me] if scatter else ins[a]
        local.append(pltpu.make_async_copy(src, outs[a].at[me], sems[2].at[a]))
        for k in range(1, N_DEV):
            dev, idx = _peer(k)
            landed = _remote(src, outs[a].at[idx], sems, a, k, dev)
            if scatter:
                first.append(_remote(ins[a].at[idx], outs[a].at[me], sems, a, k, dev))
                arrivals.append(landed)
            elif k == SIBLING:
                first.append(_remote(src, outs[a].at[me], sems, a, k, dev))
                arrivals.append(landed)
            elif k in OTHER_CHIPS:
                first.append(_remote(src, outs[a].at[me], sems, a, k, dev))
                sib, _ = _peer(SIBLING)
                relay.append((landed, _remote(outs[a].at[idx], outs[a].at[idx], sems, a, k | SIBLING, sib)))
            else:
                arrivals.append(landed)
    return local, first, relay, arrivals


def _exchange_start(ins, outs, sems, scatter):
    local, first, _, _ = _exchange_copies(ins, outs, sems, scatter)
    for cp in local + first:
        cp.start()


def _exchange_wait(ins, outs, sems, scatter):
    local, first, relay, arrivals = _exchange_copies(ins, outs, sems, scatter)
    for landed, forward in relay:
        landed.wait_recv()
        forward.start()
    for cp in arrivals:
        cp.wait_recv()
    for cp in first + [forward for _, forward in relay]:
        cp.wait_send()
    for cp in local:
        cp.wait()


def _exchange_shapes(arrs, scatter):
    return [_sds(a.shape if scatter else (N_DEV,) + a.shape, a.dtype) for a in arrs]


def _exchange_sems(n):
    return [pltpu.SemaphoreType.DMA((n, N_DEV - 1)), pltpu.SemaphoreType.DMA((n, N_DEV - 1)),
            pltpu.SemaphoreType.DMA((n,))]


def _exchange(arrs, *, name, scatter):
    n = len(arrs)

    def body(*refs):
        _exchange_start(refs[:n], refs[n:2 * n], refs[2 * n:], scatter)
        _exchange_wait(refs[:n], refs[n:2 * n], refs[2 * n:], scatter)

    any_spec = pl.BlockSpec(memory_space=pl.ANY)
    outs = pl.pallas_call(body, name=name, out_shape=_exchange_shapes(arrs, scatter), in_specs=[any_spec] * n,
                          out_specs=[any_spec] * n, scratch_shapes=_exchange_sems(n))(*arrs)
    return list(outs)


def _mod_cols(c_all, w_ada, b_cols):
    def body(c_ref, w_ref, b_ref, mod_ref, act_ref):
        c = c_ref[...]
        act = c * _sigmoid(c)
        act_ref[...] = act
        mod_ref[...] = _dot(act.astype(BF16), w_ref[...].astype(BF16)) + b_ref[...]

    return _call(body, name='mod_cols', grid=(1,),
                 in_specs=[_const(c_all.shape), _const(w_ada.shape), _const(b_cols.shape)],
                 out_specs=[_const((N_DEV, ADA_SHARD)), _const(c_all.shape)],
                 out_shape=[_sds((N_DEV, ADA_SHARD)), _sds(c_all.shape)], vmem=VMEM_BIG)(c_all, w_ada, b_cols)


def _grad_w_ada(act_t, dmod_cols):
    def body(a_ref, d_ref, o_ref):
        o_ref[...] = _dot(a_ref[...], d_ref[...])

    return _call(body, name='grad_w_ada', grid=(1,), in_specs=[_const(act_t.shape), _const(dmod_cols.shape)],
                 out_specs=_const((D_MODEL, ADA_SHARD)), out_shape=_sds((D_MODEL, ADA_SHARD)),
                 vmem=VMEM_BIG)(act_t, dmod_cols)


def _pre_mix(x, sc, sh, g, w_s, tm, ride):
    T = x.shape[0]
    group = 4

    def body(x_ref, sc_ref, sh_ref, g_ref, w_ref, proj_ref, h_ref):
        @pl.when(pl.program_id(1) == 0)
        def _():
            xv = x_ref[...]
            h_ref[...] = ((xv * _rsqrt_mean(xv) * g_ref[...]) * (1.0 + sc_ref[...]) + sh_ref[...]).astype(BF16)

        for s in range(group):
            proj_ref[:, s * IN_SHARD:(s + 1) * IN_SHARD] = _dot(h_ref[...], w_ref[s])

    row = pl.BlockSpec((tm, D_MODEL), lambda i, j: (i, 0))
    vec = _const((1, D_MODEL))
    return _call(body, name='pre_mix', grid=(T // tm, N_DEV // group),
                 in_specs=[row, vec, vec, vec, pl.BlockSpec((group, D_MODEL, IN_SHARD), lambda i, j: (j, 0, 0))],
                 out_specs=[pl.BlockSpec((tm, group * IN_SHARD), lambda i, j: (i, j)), row],
                 out_shape=[_sds((T, D_IN_PROJ)), _sds((T, D_MODEL), BF16)],
                 sem=('parallel', 'arbitrary'), ride=ride)(x, sc, sh, g, w_s)


def _halo_before(tm, rows=HALO):
    return lambda i: jnp.maximum(i * (tm // rows) - 1, 0)


def _halo_after(tm, T, rows=HALO):
    return lambda i: jnp.minimum((i + 1) * (tm // rows), T // rows - 1)


def _mix_fwd(yssm, proj, d, glu_w, glu_b, g_ssm, cw, g_conv, avg16, avg64, tm):
    T = yssm.shape[0]
    hb = _halo_before(tm)

    def body(y_ref, p_ref, ph_ref, d_ref, gw_ref, gb_ref, gs_ref, cw_ref, gc_ref, a16_ref, a64_ref, o_ref):
        i = pl.program_id(0)
        u = p_ref[:, 0:D_SSM]
        y = y_ref[...] + d_ref[...] * u
        z, _ = _gelu(y)
        gate = _sigmoid(_dot(z.astype(BF16), gw_ref[...]) + gb_ref[...])
        ya = z * gate
        rs = lax.rsqrt(_dot_split(ya * ya, a16_ref[...], 2) + EPS)
        o_ref[:, 0:D_SSM] = (ya * rs * gs_ref[...]).astype(BF16)
        bg = p_ref[:, D_SSM:D_SSM + D_CONV]
        cv = p_ref[:, D_SSM + D_CONV:D_SSM + 2 * D_CONV] * p_ref[:, D_SSM + 2 * D_CONV:D_IN_PROJ]
        hv = ph_ref[:, D_SSM + D_CONV:D_SSM + 2 * D_CONV] * ph_ref[:, D_SSM + 2 * D_CONV:D_IN_PROJ]
        hv = jnp.where(i > 0, hv, 0.0)
        conv, _, _ = _conv3(cv, hv, cw_ref)
        yb = bg * conv
        rsb = lax.rsqrt(_dot_split(yb * yb, a64_ref[...], 2) + EPS)
        o_ref[:, D_SSM:D_MODEL] = (yb * rsb * gc_ref[...]).astype(BF16)

    vec = _const((1, D_SSM))
    sq = _const((D_SSM, D_SSM))
    return _call(body, name='mix_fwd', grid=(T // tm,),
                 in_specs=[pl.BlockSpec((tm, D_SSM), lambda i: (i, 0)), pl.BlockSpec((tm, D_IN_PROJ), lambda i: (i, 0)),
                           pl.BlockSpec((HALO, D_IN_PROJ), lambda i: (hb(i), 0)), vec, sq, vec, vec,
                           _const((3, D_CONV)), vec, sq, sq],
                 out_specs=pl.BlockSpec((tm, D_MODEL), lambda i: (i, 0)), out_shape=_sds((T, D_MODEL), BF16),
                 sem=('parallel',), vmem=VMEM_BIG)(yssm, proj, proj, d, glu_w, glu_b, g_ssm, cw, g_conv, avg16, avg64)


def _out_proj(ycat, w_out, x, gt, g_post, g_pre, sc, sh, tm):
    T = x.shape[0]

    def body(y_ref, w_ref, x_ref, gt_ref, gp_ref, g2_ref, sc_ref, sh_ref, o_ref, x1_ref, h_ref):
        o = _dot(y_ref[...], w_ref[...])
        o_ref[...] = o.astype(BF16)
        x1 = x_ref[...] + gt_ref[...] * (o * _rsqrt_mean(o) * gp_ref[...])
        x1_ref[...] = x1
        h_ref[...] = ((x1 * _rsqrt_mean(x1) * g2_ref[...]) * (1.0 + sc_ref[...]) + sh_ref[...]).astype(BF16)

    row = pl.BlockSpec((tm, D_MODEL), lambda i: (i, 0))
    vec = _const((1, D_MODEL))
    return _call(body, name='out_proj', grid=(T // tm,),
                 in_specs=[row, _const((D_MODEL, D_MODEL)), row, vec, vec, vec, vec, vec],
                 out_specs=[row, row, row],
                 out_shape=[_sds((T, D_MODEL), BF16), _sds((T, D_MODEL)), _sds((T, D_MODEL), BF16)],
                 sem=('parallel',), vmem=VMEM_BIG)(ycat, w_out, x, gt, g_post, g_pre, sc, sh)


def _ffn_up(h2, w_a, w_b, cw8, tm, ride):
    T = h2.shape[0]
    hb = _halo_before(tm, HALO16)
    half = D_MODEL // 2

    def body(h_ref, hh_ref, wa_ref, wb_ref, cw_ref, up_ref, hid_ref):
        def times_w(ref, s):
            return _dot_nt(ref[:, :half], wa_ref[s]) + _dot_nt(ref[:, half:], wb_ref[s])

        for s in range(2):
            up = times_w(h_ref, s)
            up_ref[s] = up.astype(BF16)
            before = jnp.where(pl.program_id(0) > 0, times_w(hh_ref, s), 0.0)
            hid_ref[s] = _conv3(up, before, cw_ref.at[s])[0].astype(BF16)

    out = pl.BlockSpec((2, tm, FF_SHARD), lambda i, j: (j, i, 0))
    return _call(body, name='ffn_up', grid=(T // tm, N_DEV // 2),
                 in_specs=[pl.BlockSpec((tm, D_MODEL), lambda i, j: (i, 0)),
                           pl.BlockSpec((HALO16, D_MODEL), lambda i, j: (hb(i), 0)),
                           pl.BlockSpec((2, FF_SHARD, half), lambda i, j: (j, 0, 0)),
                           pl.BlockSpec((2, FF_SHARD, half), lambda i, j: (j, 0, 0)),
                           pl.BlockSpec((2, 3, FF_SHARD), lambda i, j: (j, 0, 0))],
                 out_specs=[out, out], out_shape=[_sds((N_DEV, T, FF_SHARD), BF16)] * 2,
                 sem=('parallel', 'parallel'), vmem=VMEM_BIG, ride=ride)(h2, h2, w_a, w_b, cw8)


def _ffn_down(hid4, wd4, x1, tgt, gt, g_post, tm):
    T = x1.shape[0]
    nb = T // tm

    def body(a_ref, w_ref, x1_ref, t_ref, gt_ref, g_ref, ddn_ref, dx_ref, loss_ref, dgt_ref, dg_ref, dn_ref):
        i, j = pl.program_id(0), pl.program_id(1)
        part = None
        for s in range(4):
            act = (_silu_parts(a_ref[0, s].astype(F32))[0] * a_ref[1, s].astype(F32)).astype(BF16)
            term = _dot(act, w_ref[s])
            part = term if part is None else part + term

        @pl.when(jnp.logical_and(i == 0, j == 0))
        def _():
            dgt_ref[...] = jnp.zeros_like(dgt_ref)
            dg_ref[...] = jnp.zeros_like(dg_ref)

        @pl.when(j == 0)
        def _():
            dn_ref[...] = part

        @pl.when(j > 0)
        def _():
            dn_ref[...] += part

        @pl.when(j == 0)
        def _():
            dn, gv, gate = dn_ref[...], g_ref[...], gt_ref[...]
            r = _rsqrt_mean(dn)
            normed = dn * r * gv
            err = x1_ref[...] + gate * normed - t_ref[...]
            dx = err * (1.0 / D_MODEL)
            dx_ref[...] = dx
            tot = jnp.sum(jnp.sum(err * err, axis=1, keepdims=True), axis=0, keepdims=True) * (0.5 / D_MODEL)
            loss_ref[...] = jnp.broadcast_to(tot, (8, 128))
            dgt_ref[...] += _colsum(dx * normed)
            dnn = dx * gate
            dg_ref[...] += _colsum(dnn * dn * r)
            ddn_ref[...] = _norm_bwd(dnn, dn, r, gv).astype(BF16)

    row = pl.BlockSpec((tm, D_MODEL), lambda i, j: (i, 0))
    vec = _const((1, D_MODEL))
    return _call(body, name='ffn_down', grid=(nb, 1),
                 in_specs=[pl.BlockSpec((2, 4, tm, FF_SHARD), lambda i, j: (0, j, i, 0)),
                           pl.BlockSpec((4, FF_SHARD, D_MODEL), lambda i, j: (j, 0, 0)), row, row, vec, vec],
                 out_specs=[row, row, pl.BlockSpec((None, 8, 128), lambda i, j: (i, 0, 0)), vec, vec],
                 out_shape=[_sds((T, D_MODEL), BF16), _sds((T, D_MODEL)), _sds((nb, 8, 128)), _sds((1, D_MODEL)),
                            _sds((1, D_MODEL))],
                 scratch=[pltpu.VMEM((tm, D_MODEL), F32)], sem=('arbitrary', 'arbitrary'),
                 vmem=VMEM_MOST)(hid4, wd4, x1, tgt, gt, g_post)


def _ssm_prep(lre, lim, lst, b_re, b_im):
    def body(lre_ref, lim_ref, lst_ref, br_ref, bi_ref, ar_ref, ai_ref, bbr_ref, bbi_ref):
        ar, ai, qr, qi = _zoh(lre_ref[...], lim_ref[...], lst_ref[...])[:4]
        ar_ref[...] = ar
        ai_ref[...] = ai
        bbr_ref[...] = qr * br_ref[...] - qi * bi_ref[...]
        bbi_ref[...] = qr * bi_ref[...] + qi * br_ref[...]

    shp = lre.shape
    return _call(body, name='ssm_prep', grid=(1,), in_specs=[_const(shp)] * 5, out_specs=[_const(shp)] * 4,
                 out_shape=[_sds(shp)] * 4)(lre, lim, lst, b_re, b_im)


def _zoh(lre, lim, lst):
    lr = jnp.minimum(lre, LAMBDA_RE_MAX)
    st = jnp.exp(lst)
    mag = jnp.exp(lr * st)
    ar = mag * jnp.cos(lim * st)
    ai = mag * jnp.sin(lim * st)
    den = lr * lr + lim * lim
    qr = ((ar - 1.0) * lr + ai * lim) / den
    qi = (ai * lr - (ar - 1.0) * lim) / den
    return ar, ai, qr, qi, lr, st, den


def _ssm_prep_bwd(lre, lim, lst, b_re, b_im, dbbr, dbbi, dar, dai, seg):
    def body(lre_ref, lim_ref, lst_ref, br_ref, bi_ref, dbbr_ref, dbbi_ref, dar_ref, dai_ref, seg_ref,
             dbr_ref, dbi_ref, dlre_ref, dlim_ref, dlst_ref):
        lre_v = lre_ref[...]
        li = lim_ref[...]
        ar, ai, qr, qi, lr, st, den = _zoh(lre_v, li, lst_ref[...])
        br, bi, gbr, gbi = br_ref[...], bi_ref[...], dbbr_ref[...], dbbi_ref[...]
        dbr_ref[...] = qr * gbr + qi * gbi
        dbi_ref[...] = qr * gbi - qi * gbr
        gqr = _dot_split(br * gbr + bi * gbi, seg_ref[...], 3)
        gqi = _dot_split(br * gbi - bi * gbr, seg_ref[...], 3)
        ir, ii = lr / den, -li / den
        gar = dar_ref[...] + ir * gqr + ii * gqi
        gai = dai_ref[...] + ir * gqi - ii * gqr
        tr, ti = qr * ir - qi * ii, qr * ii + qi * ir
        glr = -(tr * gqr + ti * gqi)
        gli = -(tr * gqi - ti * gqr)
        gzr = ar * gar + ai * gai
        gzi = ar * gai - ai * gar
        glr = glr + st * gzr
        gli = gli + st * gzi
        gst = (lr * gzr + li * gzi) * st
        dlre_ref[...] = jnp.where(lre_v < LAMBDA_RE_MAX, glr, 0.0)
        dlim_ref[...] = gli
        dlst_ref[...] = jnp.sum(gst, axis=1, keepdims=True) * (1.0 / SSM_GROUP)

    shp = lre.shape
    return _call(body, name='ssm_prep_bwd', grid=(1,), in_specs=[_const(shp)] * 9 + [_const(seg.shape)],
                 out_specs=[_const(shp)] * 4 + [_const((N_GROUPS, 1))],
                 out_shape=[_sds(shp)] * 4 + [_sds((N_GROUPS, 1))], vmem=VMEM_BIG)(
                     lre, lim, lst, b_re, b_im, dbbr, dbbi, dar, dai, seg)


def _scan_specs(T):
    return dict(
        chan=pl.BlockSpec((T, CHAN_BLOCK), lambda cb: (0, cb)),
        state=pl.BlockSpec((T, STATE_BLOCK), lambda cb: (0, cb)),
        b=pl.BlockSpec((CHAN_BLOCK, STATE_BLOCK), lambda cb: (cb, cb)),
        c=pl.BlockSpec((STATE_BLOCK, CHAN_BLOCK), lambda cb: (cb, cb)),
        lam=pl.BlockSpec((1, STATE_BLOCK), lambda cb: (0, cb)),
    )


def _complex_power(re, im, n):
    out = None
    while True:
        if n & 1:
            out = (re, im) if out is None else (out[0] * re - out[1] * im, out[0] * im + out[1] * re)
        n >>= 1
        if n == 0:
            return out
        re, im = re * re - im * im, 2.0 * re * im


def _rows8(i):
    if isinstance(i, int):
        return pl.ds(i * SUBLANES, SUBLANES)
    return pl.ds(pl.multiple_of(i * SUBLANES, SUBLANES), SUBLANES)


def _scan_loop(n_steps, body, init):
    trips = n_steps // SCAN_UNROLL

    def trip(t, carry):
        for u in range(SCAN_UNROLL):
            carry = body(t * SCAN_UNROLL + u, carry)
        return carry

    carry = lax.fori_loop(0, trips, trip, init)
    for step in range(trips * SCAN_UNROLL, n_steps):
        carry = body(step, carry)
    return carry


def _ssm_fwd(u_perm, b_re, b_im, c_re, c_im, lam_r, lam_i, ride):
    T = u_perm.shape[0]
    ls = T // SUBLANES
    rc = min(1024, T)
    sp = _scan_specs(T)

    def body(u_ref, bre_ref, bim_ref, cre_ref, cim_ref, lr_ref, li_ref, so_re_ref, so_im_ref, y_ref, sre_ref, sim_ref):
        for c in range(T // rc):
            rows = pl.ds(c * rc, rc)
            ub = u_ref[rows, :].astype(BF16)
            sre_ref[rows, :] = _dot(ub, bre_ref[...])
            sim_ref[rows, :] = _dot(ub, bim_ref[...])
        shp = (SUBLANES, STATE_BLOCK)
        lr = jnp.broadcast_to(lr_ref[...], shp)
        li = jnp.broadcast_to(li_ref[...], shp)
        zero = jnp.zeros(shp, F32)

        def step(i, carry):
            sr, si = carry
            rows = _rows8(i)
            nr = lr * sr - li * si + sre_ref[rows, :]
            ni = lr * si + li * sr + sim_ref[rows, :]
            sre_ref[rows, :] = nr
            sim_ref[rows, :] = ni
            return nr, ni

        fr, fi = _scan_loop(ls, step, (zero, zero))
        pr, pi_ = _complex_power(lr, li, ls)
        row = lax.broadcasted_iota(jnp.int32, shp, 0)
        ir, ii = zero, zero
        for _ in range(SUBLANES - 1):
            er = fr + pr * ir - pi_ * ii
            ei = fi + pr * ii + pi_ * ir
            ir = jnp.where(row == 0, 0.0, pltpu.roll(er, 1, 0))
            ii = jnp.where(row == 0, 0.0, pltpu.roll(ei, 1, 0))

        def fix(i, carry):
            cr, ci = carry
            rows = _rows8(i)
            nr = lr * cr - li * ci
            ni = lr * ci + li * cr
            sre_ref[rows, :] += nr
            sim_ref[rows, :] += ni
            return nr, ni

        _scan_loop(ls, fix, (ir, ii))
        for c in range(T // rc):
            rows = pl.ds(c * rc, rc)
            s_r, s_i = sre_ref[rows, :].astype(BF16), sim_ref[rows, :].astype(BF16)
            so_re_ref[rows, :] = s_r
            so_im_ref[rows, :] = s_i
            y_ref[rows, :] = _dot(s_r, cre_ref[...]) - _dot(s_i, cim_ref[...])

    return _call(body, name='ssm_fwd', grid=(N_STATE // STATE_BLOCK,),
                 in_specs=[sp['chan'], sp['b'], sp['b'], sp['c'], sp['c'], sp['lam'], sp['lam']],
                 out_specs=[sp['state'], sp['state'], sp['chan']],
                 out_shape=[_sds((T, N_STATE), BF16), _sds((T, N_STATE), BF16), _sds((T, D_SSM))],
                 scratch=[pltpu.VMEM((T, STATE_BLOCK), F32), pltpu.VMEM((T, STATE_BLOCK), F32)],
                 sem=('arbitrary',), vmem=VMEM_MOST, ride=ride)(u_perm, b_re, b_im, c_re, c_im, lam_r, lam_i)


def _ssm_bwd(dy_perm, u_perm, s_re, s_im, b_re, b_im, c_re, c_im, lam_r, lam_i, ride):
    T = u_perm.shape[0]
    ls = T // SUBLANES
    rc = min(1024, T)
    sp = _scan_specs(T)
    ncb = N_STATE // STATE_BLOCK

    def body(dy_ref, u_ref, sre_ref, sim_ref, bre_ref, bim_ref, cre_ref, cim_ref, lr_ref, li_ref,
             du_ref, dbr_ref, dbi_ref, dcr_ref, dci_ref, dar_ref, dai_ref, gre_ref, gim_ref):
        shp = (SUBLANES, STATE_BLOCK)
        zero = jnp.zeros(shp, F32)
        tail = pl.ds(T, SUBLANES)
        gre_ref[tail, :] = zero
        gim_ref[tail, :] = zero
        for c in range(T // rc):
            rows = pl.ds(c * rc, rc)
            dyb = dy_ref[rows, :].astype(BF16)
            gre_ref[rows, :] = _dot_nt(dyb, cre_ref[...])
            gim_ref[rows, :] = -_dot_nt(dyb, cim_ref[...])
        lr = jnp.broadcast_to(lr_ref[...], shp)
        li = jnp.broadcast_to(li_ref[...], shp)

        def step(k, carry):
            gr, gi = carry
            rows = _rows8(ls - 1 - k)
            nr = lr * gr + li * gi + gre_ref[rows, :]
            ni = lr * gi - li * gr + gim_ref[rows, :]
            gre_ref[rows, :] = nr
            gim_ref[rows, :] = ni
            return nr, ni

        fr, fi = _scan_loop(ls, step, (zero, zero))
        pr, pi_ = _complex_power(lr, -li, ls)
        row = lax.broadcasted_iota(jnp.int32, shp, 0)
        cr, ci = zero, zero
        for _ in range(SUBLANES - 1):
            er = fr + pr * cr - pi_ * ci
            ei = fi + pr * ci + pi_ * cr
            cr = jnp.where(row == SUBLANES - 1, 0.0, pltpu.roll(er, SUBLANES - 1, 0))
            ci = jnp.where(row == SUBLANES - 1, 0.0, pltpu.roll(ei, SUBLANES - 1, 0))

        def fix(k, carry):
            dr, di = carry
            rows = _rows8(ls - 1 - k)
            dr, di = lr * dr + li * di, lr * di - li * dr
            gre_ref[rows, :] += dr
            gim_ref[rows, :] += di
            return dr, di

        _scan_loop(ls, fix, (cr, ci))

        acc_r = jnp.zeros((1, STATE_BLOCK), F32)
        acc_i = jnp.zeros((1, STATE_BLOCK), F32)
        for c in range(T // rc):
            rows, nxt = pl.ds(c * rc, rc), pl.ds(c * rc + SUBLANES, rc)
            s_r, s_i = sre_ref[rows, :].astype(F32), sim_ref[rows, :].astype(F32)
            g_r, g_i = gre_ref[nxt, :], gim_ref[nxt, :]
            acc_r = acc_r + _colsum(g_r * s_r + g_i * s_i)
            acc_i = acc_i + _colsum(g_i * s_r - g_r * s_i)
        last = pl.ds(T - 2 * SUBLANES, 2 * SUBLANES)
        first = pl.ds(0, SUBLANES)
        spr = jnp.where(row == 0, 0.0, pltpu.roll(sre_ref[last, :].astype(F32)[SUBLANES:], 1, 0))
        spi = jnp.where(row == 0, 0.0, pltpu.roll(sim_ref[last, :].astype(F32)[SUBLANES:], 1, 0))
        gr, gi = gre_ref[first, :], gim_ref[first, :]
        dar_ref[...] = acc_r + _colsum(gr * spr + gi * spi)
        dai_ref[...] = acc_i + _colsum(gi * spr - gr * spi)

        for c in range(T // rc):
            rows = pl.ds(c * rc, rc)
            g_r, g_i = gre_ref[rows, :].astype(BF16), gim_ref[rows, :].astype(BF16)
            s_r, s_i = sre_ref[rows, :], sim_ref[rows, :]
            ub, dyb = u_ref[rows, :].astype(BF16), dy_ref[rows, :].astype(BF16)
            du_ref[rows, :] = _dot_nt(g_r, bre_ref[...]) + _dot_nt(g_i, bim_ref[...])
            parts = (_dot_tn(ub, g_r), _dot_tn(ub, g_i), _dot_tn(s_r, dyb), -_dot_tn(s_i, dyb))
            outs = (dbr_ref, dbi_ref, dcr_ref, dci_ref)
            for o_ref, part in zip(outs, parts):
                if c == 0:
                    o_ref[...] = part
                else:
                    o_ref[...] += part

    blk = lambda r, c: pl.BlockSpec((None, r, c), lambda cb: (cb, 0, 0))
    return _call(body, name='ssm_bwd', grid=(ncb,),
                 in_specs=[sp['chan'], sp['chan'], sp['state'], sp['state'], sp['b'], sp['b'], sp['c'], sp['c'],
                           sp['lam'], sp['lam']],
                 out_specs=[sp['chan'], blk(CHAN_BLOCK, STATE_BLOCK), blk(CHAN_BLOCK, STATE_BLOCK),
                            blk(STATE_BLOCK, CHAN_BLOCK), blk(STATE_BLOCK, CHAN_BLOCK), blk(1, STATE_BLOCK),
                            blk(1, STATE_BLOCK)],
                 out_shape=[_sds((T, D_SSM)), _sds((ncb, CHAN_BLOCK, STATE_BLOCK)), _sds((ncb, CHAN_BLOCK, STATE_BLOCK)),
                            _sds((ncb, STATE_BLOCK, CHAN_BLOCK)), _sds((ncb, STATE_BLOCK, CHAN_BLOCK)),
                            _sds((ncb, 1, STATE_BLOCK)), _sds((ncb, 1, STATE_BLOCK))],
                 scratch=[pltpu.VMEM((T + SUBLANES, STATE_BLOCK), F32), pltpu.VMEM((T + SUBLANES, STATE_BLOCK), F32)],
                 sem=('arbitrary',), vmem=VMEM_MOST, ride=ride)(dy_perm, u_perm, s_re, s_im, b_re, b_im, c_re, c_im,
                                                                lam_r, lam_i)


def _ffn_dact(ddn, wd4, hid4, tm):
    T = ddn.shape[0]
    nb = T // tm

    def body(d_ref, w_ref, hid_ref, o_ref, gw_ref, acc_ref):
        i = pl.program_id(1)
        d = d_ref[...]
        dact = _dot_nt(d, w_ref[...])
        silu, dsilu = _silu_parts(hid_ref[0].astype(F32))
        hid_v = hid_ref[1].astype(F32)
        o_ref[0] = (dact * hid_v * dsilu).astype(BF16)
        o_ref[1] = (dact * silu).astype(BF16)
        part = _dot_tn((silu * hid_v).astype(BF16), d)

        @pl.when(i == 0)
        def _():
            acc_ref[...] = part

        @pl.when(i > 0)
        def _():
            acc_ref[...] += part

        @pl.when(i == nb - 1)
        def _():
            gw_ref[...] = acc_ref[...].astype(BF16)

    blk = pl.BlockSpec((2, None, tm, FF_SHARD), lambda j, i: (0, j, i, 0))
    w_blk = pl.BlockSpec((None, FF_SHARD, D_MODEL), lambda j, i: (j, 0, 0))
    return _call(body, name='ffn_dact', grid=(4, nb),
                 in_specs=[pl.BlockSpec((tm, D_MODEL), lambda j, i: (i, 0)), w_blk, blk],
                 out_specs=[blk, w_blk],
                 out_shape=[_sds((2, 4, T, FF_SHARD), BF16), _sds((4, FF_SHARD, D_MODEL), BF16)],
                 scratch=[pltpu.VMEM((FF_SHARD, D_MODEL), F32)], sem=('parallel', 'arbitrary'),
                 vmem=VMEM_BIG)(ddn, wd4, hid4)


def _ffn_dup(dhid8, up8, cw8, tm, ride):
    T = up8.shape[1]
    nb = T // tm
    ha = _halo_after(tm, T, HALO16)

    def body(dh_ref, dha_ref, up_ref, cw_ref, dup_ref, dcw_ref):
        i = pl.program_id(1)

        @pl.when(i == 0)
        def _():
            dcw_ref[...] = jnp.zeros_like(dcw_ref)

        dh = dh_ref[...].astype(F32)
        dup, dh1, dh2 = _conv3_t(dh, jnp.where(i < nb - 1, dha_ref[...].astype(F32), 0.0), cw_ref)
        dup_ref[...] = dup.astype(BF16)
        up = up_ref[...].astype(F32)
        dcw_ref[0:1, :] += _colsum(dh2 * up)
        dcw_ref[1:2, :] += _colsum(dh1 * up)
        dcw_ref[2:3, :] += _colsum(dh * up)

    main = pl.BlockSpec((None, tm, FF_SHARD), lambda j, i: (j, i, 0))
    return _call(body, name='ffn_dup', grid=(N_DEV, nb),
                 in_specs=[main, pl.BlockSpec((None, HALO16, FF_SHARD), lambda j, i: (j, ha(i), 0)), main,
                           pl.BlockSpec((None, 3, FF_SHARD), lambda j, i: (j, 0, 0))],
                 out_specs=[main, pl.BlockSpec((None, 8, FF_SHARD), lambda j, i: (j, 0, 0))],
                 out_shape=[_sds((N_DEV, T, FF_SHARD), BF16), _sds((N_DEV, 8, FF_SHARD))],
                 sem=('parallel', 'arbitrary'), vmem=VMEM_BIG, ride=ride)(dhid8, dhid8, up8, cw8)


def _grad_tn(a, b, a_spec, b_spec, groups, m, n, tk, name, ride=None, parts=1):
    T = a.shape[-2]
    nk = T // tk
    mp = m // parts

    def body(a_ref, b_ref, *refs):
        o_refs, acc_ref = refs[:parts], refs[parts]
        k = pl.program_id(1)
        part = _dot_tn(a_ref[...], b_ref[...])

        @pl.when(k == 0)
        def _():
            acc_ref[...] = part

        @pl.when(k > 0)
        def _():
            acc_ref[...] += part

        @pl.when(k == nk - 1)
        def _():
            for p, o_ref in enumerate(o_refs):
                o_ref[...] = acc_ref[p * mp:(p + 1) * mp, :].astype(BF16)

    out_spec = pl.BlockSpec((None, mp, n), lambda g, k: (g, 0, 0))
    res = _call(body, name=name, grid=(groups, nk), in_specs=[a_spec, b_spec], out_specs=[out_spec] * parts,
                out_shape=[_sds((groups, mp, n), BF16)] * parts, scratch=[pltpu.VMEM((m, n), F32)],
                sem=('parallel', 'arbitrary'), vmem=VMEM_BIG, ride=ride)(a, b)
    if parts > 1:
        return res
    return res[0] if ride is None else (res[0][0], res[1])


def _grad_w_in(h1, dproj, tk, ride):
    T = h1.shape[0]
    nk = T // tk
    half = D_IN_PROJ // 2

    def body(a_ref, b_ref, o_ref, acc_ref):
        k = pl.program_id(0)
        for h in range(2):
            cols = slice(h * half, (h + 1) * half)
            part = _dot_tn(a_ref[...], b_ref[:, cols])

            @pl.when(k == 0)
            def _():
                acc_ref[:, cols] = part

            @pl.when(k > 0)
            def _():
                acc_ref[:, cols] += part

        @pl.when(k == nk - 1)
        def _():
            for g in range(N_DEV):
                o_ref[g] = acc_ref[:, g * IN_SHARD:(g + 1) * IN_SHARD].astype(BF16)

    return _call(body, name='grad_w_in', grid=(nk,),
                 in_specs=[pl.BlockSpec((tk, D_MODEL), lambda k: (k, 0)), pl.BlockSpec((tk, D_IN_PROJ), lambda k: (k, 0))],
                 out_specs=_const((N_DEV, D_MODEL, IN_SHARD)), out_shape=_sds((N_DEV, D_MODEL, IN_SHARD), BF16),
                 scratch=[pltpu.VMEM((D_MODEL, D_IN_PROJ), F32)], sem=('arbitrary',), vmem=VMEM_BIG, ride=ride)(h1, dproj)


def _pre_norm_bwd(dz, dz_spec, w_parts, xin, dres, sc, g, tm, name, ride, below=None, group=1, w_t=False):
    T = xin.shape[0]
    n = w_parts[0].shape[1] if w_t else w_parts[0].shape[2]
    mul = _dot if w_t else _dot_nt
    steps = N_DEV // group
    width = D_MODEL // len(w_parts)

    def body(dz_ref, *refs):
        w_refs, (x_ref, dr_ref, sc_ref, g_ref), refs = refs[:len(w_parts)], refs[len(w_parts):len(w_parts) + 4], \
            refs[len(w_parts) + 4:]
        if below is None:
            dx_ref, dsh_ref, dsc_ref, dg_ref = refs
            sums = (dsh_ref, dsc_ref, dg_ref)
        else:
            v_ref, gate_ref, g2_ref, dx_ref, dsh_ref, dsc_ref, dg_ref, dv_ref, dgate_ref, dg2_ref = refs
            sums = (dsh_ref, dsc_ref, dg_ref, dgate_ref, dg2_ref)
        i, j = pl.program_id(0), pl.program_id(1)
        piece = (lambda s: dz_ref[s]) if dz.ndim == 3 else (lambda s: dz_ref[:, s * n:(s + 1) * n])
        parts = []
        for w_ref in w_refs:
            part = mul(piece(0), w_ref[0])
            for s in range(1, group):
                part = part + mul(piece(s), w_ref[s])
            parts.append(part)

        @pl.when(jnp.logical_and(i == 0, j == 0))
        def _():
            for s_ref in sums:
                s_ref[...] = jnp.zeros_like(s_ref)

        @pl.when(j == 0)
        def _():
            for k, part in enumerate(parts):
                dx_ref[:, k * width:(k + 1) * width] = part

        @pl.when(j > 0)
        def _():
            for k, part in enumerate(parts):
                dx_ref[:, k * width:(k + 1) * width] += part

        @pl.when(j == steps - 1)
        def _():
            dh, xv, gv = dx_ref[...], x_ref[...], g_ref[...]
            r = _rsqrt_mean(xv)
            dsh_ref[...] += _colsum(dh)
            dsc_ref[...] += _colsum(dh * (xv * r * gv))
            dxn = dh * (1.0 + sc_ref[...])
            dg_ref[...] += _colsum(dxn * xv * r)
            dx = dr_ref[...] + _norm_bwd(dxn, xv, r, gv)
            dx_ref[...] = dx
            if below is not None:
                v, g2 = v_ref[...].astype(F32), g2_ref[...]
                rv = _rsqrt_mean(v)
                dgate_ref[...] += _colsum(dx * (v * rv * g2))
                dn = dx * gate_ref[...]
                dg2_ref[...] += _colsum(dn * v * rv)
                dv_ref[...] = _norm_bwd(dn, v, rv, g2).astype(BF16)

    row = pl.BlockSpec((tm, D_MODEL), lambda i, j: (i, 0))
    vec = _const((1, D_MODEL))
    in_specs = [dz_spec] + [pl.BlockSpec((group,) + w.shape[1:], lambda i, j: (j, 0, 0)) for w in w_parts]
    in_specs += [row, row, vec, vec]
    out_specs = [row, vec, vec, vec]
    out_shape = [_sds((T, D_MODEL)), _sds((1, D_MODEL)), _sds((1, D_MODEL)), _sds((1, D_MODEL))]
    args = [dz, *w_parts, xin, dres, sc, g]
    if below is not None:
        in_specs += [row, vec, vec]
        out_specs += [row, vec, vec]
        out_shape += [_sds((T, D_MODEL), BF16), _sds((1, D_MODEL)), _sds((1, D_MODEL))]
        args += list(below)
    return _call(body, name=name, grid=(T // tm, steps), in_specs=in_specs, out_specs=out_specs,
                 out_shape=out_shape, sem=('arbitrary', 'arbitrary'), vmem=VMEM_MOST, ride=ride)(*args)


def _mix_bwd(d_o, w_out, yssm, proj, d, glu_w, glu_b, g_ssm, cw, g_conv, avg16, avg64, tm, ride):
    T = yssm.shape[0]
    hb = _halo_before(tm)

    def body(do_ref, wo_ref, y_ref, p_ref, ph_ref, d_ref, gw_ref, gb_ref, gs_ref, cw_ref, gc_ref, a16_ref, a64_ref,
             dy_ref, dconv_ref, dbg_ref, z_ref, dlin_ref, acc_ref):
        i = pl.program_id(0)
        dyc = _dot_nt(do_ref[...], wo_ref[...])

        @pl.when(i == 0)
        def _():
            acc_ref[...] = jnp.zeros_like(acc_ref)

        u = p_ref[:, 0:D_SSM]
        y = y_ref[...] + d_ref[...] * u
        z, t = _gelu(y)
        gate = _sigmoid(_dot(z.astype(BF16), gw_ref[...]) + gb_ref[...])
        ya = z * gate
        rs = lax.rsqrt(_dot_split(ya * ya, a16_ref[...], 2) + EPS)
        dna = dyc[:, 0:D_SSM]
        acc_ref[1:2, :] += _colsum(dna * ya * rs)
        dya = _head_norm_bwd(dna, ya, rs, gs_ref[...], a16_ref[...])
        dlin = dya * z * gate * (1.0 - gate)
        acc_ref[0:1, :] += _colsum(dlin)
        dlin_b = dlin.astype(BF16)
        dz = dya * gate + _dot_nt(dlin_b, gw_ref[...])
        dy = dz * _gelu_grad(y, t)
        acc_ref[3:4, :] += _colsum(dy * u)
        dy_ref[...] = dy
        z_ref[...] = z.astype(BF16)
        dlin_ref[...] = dlin_b

        bg = p_ref[:, D_SSM:D_SSM + D_CONV]
        cv = p_ref[:, D_SSM + D_CONV:D_SSM + 2 * D_CONV] * p_ref[:, D_SSM + 2 * D_CONV:D_IN_PROJ]
        hv = ph_ref[:, D_SSM + D_CONV:D_SSM + 2 * D_CONV] * ph_ref[:, D_SSM + 2 * D_CONV:D_IN_PROJ]
        hv = jnp.where(i > 0, hv, 0.0)
        conv, cv1, cv2 = _conv3(cv, hv, cw_ref)
        yb = bg * conv
        rsb = lax.rsqrt(_dot_split(yb * yb, a64_ref[...], 2) + EPS)
        dnb = dyc[:, D_SSM:D_MODEL]
        acc_ref[2:3, :] += _colsum(dnb * yb * rsb)
        dyb = _head_norm_bwd(dnb, yb, rsb, gc_ref[...], a64_ref[...])
        dbg_ref[...] = dyb * conv
        dconv = dyb * bg
        dconv_ref[...] = dconv
        acc_ref[4:5, :] += _colsum(dconv * cv2)
        acc_ref[5:6, :] += _colsum(dconv * cv1)
        acc_ref[6:7, :] += _colsum(dconv * cv)

    vec = _const((1, D_SSM))
    sq = _const((D_SSM, D_SSM))
    half = pl.BlockSpec((tm, D_SSM), lambda i: (i, 0))
    return _call(body, name='mix_bwd', grid=(T // tm,),
                 in_specs=[pl.BlockSpec((tm, D_MODEL), lambda i: (i, 0)), _const((D_MODEL, D_MODEL)), half,
                           pl.BlockSpec((tm, D_IN_PROJ), lambda i: (i, 0)),
                           pl.BlockSpec((HALO, D_IN_PROJ), lambda i: (hb(i), 0)), vec, sq, vec, vec,
                           _const((3, D_CONV)), vec, sq, sq],
                 out_specs=[half, half, half, half, half, _const((8, D_SSM))],
                 out_shape=[_sds((T, D_SSM)), _sds((T, D_SSM)), _sds((T, D_SSM)), _sds((T, D_SSM), BF16),
                            _sds((T, D_SSM), BF16), _sds((8, D_SSM))],
                 sem=('arbitrary',), vmem=VMEM_BIG, ride=ride)(d_o, w_out, yssm, proj, proj, d, glu_w, glu_b, g_ssm, cw,
                                                              g_conv, avg16, avg64)


def _mix_bwd_proj(dconv, proj, du_ssm, dy, d, dbg, cw, tm):
    T = dy.shape[0]
    nb = T // tm
    ha = _halo_after(tm, T)

    def body(dc_ref, dch_ref, cg_ref, v_ref, du_ref, dy_ref, d_ref, dbg_ref, cw_ref, o_ref):
        i = pl.program_id(0)
        dcv = _conv3_t(dc_ref[...], jnp.where(i < nb - 1, dch_ref[...], 0.0), cw_ref)[0]
        o_ref[:, 0:D_SSM] = (du_ref[...] + dy_ref[...] * d_ref[...]).astype(BF16)
        o_ref[:, D_SSM:D_SSM + D_CONV] = dbg_ref[...].astype(BF16)
        o_ref[:, D_SSM + D_CONV:D_SSM + 2 * D_CONV] = (dcv * v_ref[...]).astype(BF16)
        o_ref[:, D_SSM + 2 * D_CONV:D_IN_PROJ] = (dcv * cg_ref[...]).astype(BF16)

    half = pl.BlockSpec((tm, D_SSM), lambda i: (i, 0))
    return _call(body, name='mix_bwd_proj', grid=(nb,),
                 in_specs=[half, pl.BlockSpec((HALO, D_CONV), lambda i: (ha(i), 0)),
                           pl.BlockSpec((tm, D_CONV), lambda i: (i, 2)), pl.BlockSpec((tm, D_CONV), lambda i: (i, 3)),
                           half, half, _const((1, D_SSM)), half, _const((3, D_CONV))],
                 out_specs=pl.BlockSpec((tm, D_IN_PROJ), lambda i: (i, 0)), out_shape=_sds((T, D_IN_PROJ), BF16),
                 sem=('parallel',), vmem=VMEM_BIG)(dconv, dconv, proj, proj, du_ssm, dy, d, dbg, cw)


ADAMW_SLOT_BYTES = 8 << 20
ADAMW_ROW_BYTES = 3 << 19


def _row_tile(rows, cols, slots):
    for cand in range(rows, 15, -1):
        if (rows % cand == 0 and cand % 16 == 0 and slots * cand * cols * 4 <= ADAMW_SLOT_BYTES
                and cand * cols * 4 <= ADAMW_ROW_BYTES):
            return cand
    return rows


def _adamw_math(g, w, m, v):
    m2 = ADAM_B1 * m + (1.0 - ADAM_B1) * g
    v2 = ADAM_B2 * v + (1.0 - ADAM_B2) * (g * g)
    m_hat = m2 / (1.0 - ADAM_B1 ** ADAM_STEP)
    v_hat = v2 / (1.0 - ADAM_B2 ** ADAM_STEP)
    return -ADAM_LR * (m_hat / (jnp.sqrt(v_hat) + ADAM_EPS) + ADAM_WD * w), m2, v2


def _adamw(pieces, w, m, v, name):
    slots, _, cols = pieces[0].shape
    rows = sum(p.shape[1] for p in pieces)
    tr = _row_tile(pieces[0].shape[1], cols, slots)
    starts, pos = [], 0
    for p in pieces:
        assert p.shape[1] % tr == 0
        starts.append(pos)
        pos += p.shape[1] // tr

    def body(*refs):
        g_refs = refs[:len(pieces)]
        w_ref, m_ref, v_ref, go_ref, d_ref, mo_ref, vo_ref = refs[len(pieces):]
        i = pl.program_id(0)
        g = None
        for g_ref, start in zip(g_refs, starts):
            part = g_ref[0].astype(F32)
            for s in range(1, slots):
                part = part + g_ref[s].astype(F32)
            g = part if g is None else jnp.where(i >= start, part, g)
        go_ref[...] = g
        d_ref[...], mo_ref[...], vo_ref[...] = _adamw_math(g, w_ref[...], m_ref[...], v_ref[...])

    def piece_spec(start, count):
        return pl.BlockSpec((slots, tr, cols), lambda i: (0, jnp.clip(i - start, 0, count - 1), 0))

    blk = pl.BlockSpec((tr, cols), lambda i: (i, 0))
    return _call(body, name=name, grid=(rows // tr,),
                 in_specs=[piece_spec(s, p.shape[1] // tr) for s, p in zip(starts, pieces)] + [blk, blk, blk],
                 out_specs=[blk] * 4, out_shape=[_sds((rows, cols))] * 4, sem=('parallel',),
                 vmem=VMEM_BIG)(*pieces, w, m, v)


def _to_scan_rows(a):
    T, n = a.shape
    return a.reshape(SUBLANES, T // SUBLANES, n).transpose(1, 0, 2).reshape(T, n)


def _from_scan_rows(a):
    T, n = a.shape
    return a.reshape(T // SUBLANES, SUBLANES, n).transpose(1, 0, 2).reshape(T, n)


def _expand(a):
    return jnp.repeat(a, SSM_GROUP, axis=1)


def _block_diag(rows, row_group, col_group):
    r, n = rows.shape
    tiled = jnp.tile(rows, (1, N_GROUPS))
    keep = (jnp.arange(r)[:, None] // row_group) == (jnp.arange(n * N_GROUPS)[None, :] // col_group)
    return jnp.where(keep, tiled, 0.0)


def _block_diag_b(bb):
    return _block_diag(bb.transpose(0, 2, 1).reshape(D_SSM, SSM_STATE), SSM_GROUP, SSM_STATE)


def _block_diag_c(cc):
    return _block_diag(cc.transpose(0, 2, 1).reshape(N_STATE, SSM_GROUP), SSM_STATE, SSM_GROUP)


def _diag_blocks(x, chan_major):
    per = CHAN_BLOCK // SSM_GROUP
    eye = jnp.eye(per, dtype=x.dtype)
    if chan_major:
        x = x.reshape(-1, per, SSM_GROUP, per, SSM_STATE) * eye[None, :, None, :, None]
        return x.sum(axis=1).transpose(0, 2, 3, 1).reshape(N_GROUPS, SSM_STATE, SSM_GROUP)
    x = x.reshape(-1, per, SSM_STATE, per, SSM_GROUP) * eye[None, :, None, :, None]
    return x.sum(axis=3).reshape(N_GROUPS, SSM_STATE, SSM_GROUP)


SMALL_LAYOUT = {
    'ssm_b_re': (0, 0, 32, 1024), 'ssm_b_im': (32, 0, 32, 1024), 'ssm_c_re': (64, 0, 32, 1024),
    'ssm_c_im': (96, 0, 32, 1024), 'b_ada': (128, 0, 6, 1024), 'g_pre_mix': (134, 0, 1, 1024),
    'g_post_mix': (135, 0, 1, 1024), 'ssm_lam_re': (136, 0, 2, 1024), 'ssm_lam_im': (138, 0, 2, 1024),
    'ssm_log_step': (140, 0, 1, 32), 'glu_b': (141, 0, 1, 512), 'g_out_ssm': (141, 512, 1, 512),
    'g_out_conv': (142, 0, 1, 512), 'ssm_d': (142, 512, 1, 512), 'g_pre_ffn': (143, 0, 1, 1024),
    'g_post_ffn': (144, 0, 1, 1024)}
SMALL_ROWS = 152
B_ADA_ROW = SMALL_LAYOUT['b_ada'][0]
LATE_ROWS = {('b_ada', 0): 0, ('b_ada', 1): 1, ('g_pre_mix', 0): 2}


def _adamw_small(gathered, late, wts, mom_m, mom_v):
    names = list(SMALL_LAYOUT)
    n = len(names)

    def body(*refs):
        g_ref, late_ref, ins, outs = refs[0], refs[1], refs[2:2 + 3 * n], refs[2 + 3 * n:]
        for p, name in enumerate(names):
            r0, c0, rows, cols = SMALL_LAYOUT[name]
            pieces = [(0, rows)] if rows % 8 == 0 else [(r, 1) for r in range(rows)]
            for r, cnt in pieces:
                src_ref, first = (late_ref, LATE_ROWS[name, r]) if (name, r) in LATE_ROWS else (g_ref, r0 + r)
                g = src_ref[0, first:first + cnt, c0:c0 + cols]
                for s in range(1, N_DEV):
                    g = g + src_ref[s, first:first + cnt, c0:c0 + cols]
                w, m, v = (ins[3 * p + q][r:r + cnt, :] for q in range(3))
                res = (g,) + _adamw_math(g, w, m, v)
                for q in range(4):
                    outs[4 * p + q][r:r + cnt, :] = res[q]

    shapes = [SMALL_LAYOUT[name][2:] for name in names]
    args = [gathered, late]
    for name, shp in zip(names, shapes):
        args += [wts[name].reshape(shp), mom_m[name].reshape(shp), mom_v[name].reshape(shp)]
    outs = _call(body, name='adamw_small', grid=(1,),
                 in_specs=[_const(gathered.shape), _const(late.shape)]
                 + [_const(shp) for shp in shapes for _ in range(3)],
                 out_specs=[_const(shp) for shp in shapes for _ in range(4)],
                 out_shape=[_sds(shp) for shp in shapes for _ in range(4)], vmem=VMEM_BIG)(*args)
    res = {}
    for p, name in enumerate(names):
        for q, kind in enumerate(('g', 'd', 'm', 'v')):
            res[kind, name] = outs[4 * p + q].reshape(wts[name].shape)
    return res


def kernel(x, c, w_ada, b_ada, g_pre_mix, g_post_mix, w_in, ssm_lam_re, ssm_lam_im, ssm_log_step, ssm_b_re, ssm_b_im, ssm_c_re, ssm_c_im, ssm_d, glu_w, glu_b, g_out_ssm, conv_w, g_out_conv, w_out, g_pre_ffn, g_post_ffn, w_up, ffn_conv_w, w_down, loss_target, m_w_ada, m_b_ada, m_g_pre_mix, m_g_post_mix, m_w_in, m_ssm_lam_re, m_ssm_lam_im, m_ssm_log_step, m_ssm_b_re, m_ssm_b_im, m_ssm_c_re, m_ssm_c_im, m_ssm_d, m_glu_w, m_glu_b, m_g_out_ssm, m_conv_w, m_g_out_conv, m_w_out, m_g_pre_ffn, m_g_post_ffn, m_w_up, m_ffn_conv_w, m_w_down, v_w_ada, v_b_ada, v_g_pre_mix, v_g_post_mix, v_w_in, v_ssm_lam_re, v_ssm_lam_im, v_ssm_log_step, v_ssm_b_re, v_ssm_b_im, v_ssm_c_re, v_ssm_c_im, v_ssm_d, v_glu_w, v_glu_b, v_g_out_ssm, v_conv_w, v_g_out_conv, v_w_out, v_g_pre_ffn, v_g_post_ffn, v_w_up, v_ffn_conv_w, v_w_down):
    args = dict(locals())
    wts = {n: args[n] for n in WEIGHTS}
    mom_m = {n: args['m_' + n] for n in WEIGHTS}
    mom_v = {n: args['v_' + n] for n in WEIGHTS}
    T = x.shape[1]
    tm = min(512, T)
    tw = min(1024, T)
    tk = min(2048, T)
    me = _me()[3]
    xt, tgt = x[0], loss_target[0]

    c_all, w_in_s = _exchange([c, w_in[0].astype(BF16)], name='gather_first', scatter=False)
    c_all = c_all.reshape(N_DEV, D_MODEL)
    b_cols = lax.dynamic_slice(b_ada, (0, me * ADA_SHARD), (1, ADA_SHARD))
    mod_cols, c_act = _mod_cols(c_all, w_ada[0], b_cols)
    (mod_all,) = _exchange([mod_cols], name='gather_mod', scatter=False)
    mod = lax.dynamic_slice(mod_all, (0, me, 0), (N_DEV, 1, ADA_SHARD)).reshape(N_MOD, 1, D_MODEL)
    sh1, sc1, gt1, sh2, sc2, gt2 = [mod[k] for k in range(N_MOD)]


    lre_x, lim_x = _expand(ssm_lam_re[0]), _expand(ssm_lam_im[0])
    lst_x = jnp.broadcast_to(ssm_log_step[0][:, None], (N_GROUPS, SSM_STATE * SSM_GROUP))
    b_re_x = ssm_b_re[0].reshape(N_GROUPS, -1)
    b_im_x = ssm_b_im[0].reshape(N_GROUPS, -1)
    ar_x, ai_x, bbr_x, bbi_x = _ssm_prep(lre_x, lim_x, lst_x, b_re_x, b_im_x)
    lam_r = ar_x[:, ::SSM_GROUP].reshape(1, N_STATE)
    lam_i = ai_x[:, ::SSM_GROUP].reshape(1, N_STATE)
    big_b_re = _block_diag_b(bbr_x.reshape(N_GROUPS, SSM_STATE, SSM_GROUP)).astype(BF16)
    big_b_im = _block_diag_b(bbi_x.reshape(N_GROUPS, SSM_STATE, SSM_GROUP)).astype(BF16)
    big_c_re = _block_diag_c(ssm_c_re[0]).astype(BF16)
    big_c_im = _block_diag_c(ssm_c_im[0]).astype(BF16)
    head = jnp.arange(D_SSM)
    avg16 = jnp.where(head[:, None] // SSM_GROUP == head[None, :] // SSM_GROUP, 1.0 / SSM_GROUP, 0.0).astype(BF16)
    hd = D_CONV // CONV_HEADS
    avg64 = jnp.where(head[:, None] // hd == head[None, :] // hd, 1.0 / hd, 0.0).astype(BF16)

    w_up_t, half = w_up[0].T, D_MODEL // 2
    (proj, h1), (ffn_conv_s, conv_s, w_up_a) = _pre_mix(
        xt, sc1, sh1, g_pre_mix, w_in_s, tw, ([ffn_conv_w[0], conv_w[0], w_up_t[:, :half].astype(BF16)], False))
    cw_full = conv_s.transpose(1, 0, 2).reshape(3, D_CONV)
    u_perm = _to_scan_rows(proj[:, :D_SSM])
    (s_re, s_im, y_perm), (w_up_b, glu_s, w_out_s) = _ssm_fwd(
        u_perm, big_b_re, big_b_im, big_c_re, big_c_im, lam_r, lam_i,
        ([w_up_t[:, half:].astype(BF16), glu_w[0].astype(BF16), w_out[0].astype(BF16)], False))
    glu_full = glu_s.reshape(D_SSM, D_SSM)
    w_out_full = w_out_s.reshape(D_MODEL, D_MODEL)
    yssm = _from_scan_rows(y_perm)
    mix_args = (ssm_d, glu_full, glu_b, g_out_ssm, cw_full, g_out_conv, avg16, avg64)
    ycat = _mix_fwd(yssm, proj, *mix_args, tw)
    o, x1, h2 = _out_proj(ycat, w_out_full, xt, gt1, g_post_mix, g_pre_ffn, sc2, sh2, tw)
    (up8, hid8), (w_down_s,) = _ffn_up(h2, w_up_a, w_up_b, ffn_conv_s, tw, ([w_down[0].astype(BF16)], False))
    wd4 = w_down_s.reshape(4, FF_SHARD, D_MODEL)
    hid4 = hid8.reshape(2, 4, T, FF_SHARD)
    ddn, dx2, loss_parts, d_gt2, d_g_post_ffn = _ffn_down(hid4, wd4, x1, tgt, gt2, g_post_ffn, tm)
    loss_local = jnp.sum(loss_parts[:, 0, 0])

    got = {}
    dhid, g_w_down = _ffn_dact(ddn, wd4, hid4, tw)
    (dup8, dcw_ffn), (got['w_down'],) = _ffn_dup(dhid.reshape(N_DEV, T, FF_SHARD), up8, ffn_conv_s, tw,
                                                 ([g_w_down.reshape(N_DEV, D_FF // N_DEV, D_MODEL)], True))
    g_w_up_halves = _grad_tn(dup8, h2, pl.BlockSpec((None, T, FF_SHARD), lambda g, k: (g, k, 0)),
                             pl.BlockSpec((T, D_MODEL), lambda g, k: (k, 0)), N_DEV, FF_SHARD, D_MODEL, T,
                             'grad_w_up', parts=2)
    (dx1, d_sh2, d_sc2, d_g_pre_ffn, d_o, d_gt1, d_g_post_mix), (got_up_0, got['ffn_conv_w']) = _pre_norm_bwd(
        dup8, pl.BlockSpec((2, tw, FF_SHARD), lambda i, j: (j, i, 0)), [w_up_a, w_up_b], x1, dx2, sc2, g_pre_ffn, tw,
        'ffn_in_bwd', ([g_w_up_halves[0], dcw_ffn], True), below=(o, gt1, g_post_mix), group=2, w_t=True)

    g_w_out = _grad_tn(ycat, d_o, pl.BlockSpec((tk, D_MODEL), lambda g, k: (k, 0)),
                       pl.BlockSpec((tk, D_MODEL), lambda g, k: (k, 0)), 1, D_MODEL, D_MODEL, tk, 'grad_w_out')
    (dy, dconv, dbg, z_b, dlin_b, sums), (got['w_out'],) = _mix_bwd(
        d_o, w_out_full, yssm, proj, *mix_args, tm, ([g_w_out.reshape(N_DEV, D_MODEL // N_DEV, D_MODEL)], True))
    g_glu_w = _grad_tn(z_b, dlin_b, pl.BlockSpec((tk, D_SSM), lambda g, k: (k, 0)),
                       pl.BlockSpec((tk, D_SSM), lambda g, k: (k, 0)), 1, D_SSM, D_SSM, tk, 'grad_glu_w')
    dy_perm = _to_scan_rows(dy)
    (du_perm, dbr_blk, dbi_blk, dcr_blk, dci_blk, dar_blk, dai_blk), (got_up_1, got['glu_w']) = _ssm_bwd(
        dy_perm, u_perm, s_re, s_im, big_b_re, big_b_im, big_c_re, big_c_im, lam_r, lam_i,
        ([g_w_up_halves[1], g_glu_w.reshape(N_DEV, D_SSM // N_DEV, D_SSM)], True))
    du_ssm = _from_scan_rows(du_perm)
    dproj = _mix_bwd_proj(dconv, proj, du_ssm, dy, ssm_d, dbg, cw_full, tw)
    dbb_re = _diag_blocks(dbr_blk, True).reshape(N_GROUPS, -1)
    dbb_im = _diag_blocks(dbi_blk, True).reshape(N_GROUPS, -1)
    d_c_re = _diag_blocks(dcr_blk, False).transpose(0, 2, 1)
    d_c_im = _diag_blocks(dci_blk, False).transpose(0, 2, 1)
    lane = jnp.arange(SSM_STATE * SSM_GROUP)
    seg = jnp.where(lane[:, None] // SSM_GROUP == lane[None, :] // SSM_GROUP, 1.0, 0.0).astype(BF16)
    d_b_re_x, d_b_im_x, d_lre_x, d_lim_x, d_lst = _ssm_prep_bwd(
        lre_x, lim_x, lst_x, b_re_x, b_im_x, dbb_re, dbb_im, _expand(dar_blk.reshape(N_GROUPS, SSM_STATE)),
        _expand(dai_blk.reshape(N_GROUPS, SSM_STATE)), seg)

    row = lambda a: a.reshape(-1, PACK_COLS)
    blank = jnp.zeros((1, PACK_COLS), F32)
    small_pack = jnp.concatenate([
        d_b_re_x, d_b_im_x, row(d_c_re), row(d_c_im), blank, blank, d_gt1, d_sh2, d_sc2, d_gt2, blank,
        d_g_post_mix, row(d_lre_x[:, ::SSM_GROUP]), row(d_lim_x[:, ::SSM_GROUP]),
        jnp.pad(d_lst.reshape(1, N_GROUPS), ((0, 0), (0, PACK_COLS - N_GROUPS))), row(sums[0:4]), d_g_pre_ffn,
        d_g_post_ffn, jnp.zeros((SMALL_ROWS - 145, PACK_COLS), F32)])
    g_w_in, (small_all,) = _grad_w_in(h1, dproj, tk, ([small_pack], False))
    g_conv_slots = jnp.concatenate([sums[4:7], jnp.zeros((5, D_CONV), F32)]).reshape(
        8, N_DEV, D_CONV // N_DEV).transpose(1, 0, 2)
    (grad_x, d_sh1, d_sc1, d_g_pre_mix), (got['w_in'], got['conv_w']) = _pre_norm_bwd(
        dproj, pl.BlockSpec((tw, 4 * IN_SHARD), lambda i, j: (i, j)), [w_in_s], xt, dx1, sc1, g_pre_mix, tw,
        'mix_in_bwd', ([g_w_in, g_conv_slots], True), group=4)
    late_pack = jnp.concatenate([d_sh1, d_sc1, d_g_pre_mix, jnp.full((1, PACK_COLS), loss_local, F32),
                                 jnp.zeros((4, PACK_COLS), F32)])
    (late_all,) = _exchange([late_pack], name='gather_late_grads', scatter=False)
    loss = jnp.sum(late_all[:, 3, 0])
    res = _adamw_small(small_all, late_all, wts, mom_m, mom_v)

    dmod_all = jnp.concatenate([late_all[:, 0:2, :], small_all[:, B_ADA_ROW + 2:B_ADA_ROW + N_MOD, :]],
                               axis=1).reshape(N_DEV, N_MOD * D_MODEL)
    dmod_cols = lax.dynamic_slice(dmod_all, (0, me * ADA_SHARD), (N_DEV, ADA_SHARD))
    g_w_ada = _grad_w_ada(c_act.T, dmod_cols)

    pieces = {n: [slots[:, :3, :] if n in ('conv_w', 'ffn_conv_w') else slots] for n, slots in got.items()}
    for n, parts in pieces.items():
        outs = _adamw(parts, wts[n][0], mom_m[n][0], mom_v[n][0], 'adamw_' + n)
        for kind, val in zip(('g', 'd', 'm', 'v'), outs):
            res[kind, n] = val[None]
    outs = _adamw([got_up_0, got_up_1], w_up[0].T, m_w_up[0].T, v_w_up[0].T, 'adamw_w_up')
    for kind, val in zip(('g', 'd', 'm', 'v'), outs):
        res[kind, 'w_up'] = val.T[None]
    outs = _adamw([g_w_ada[None]], w_ada[0], m_w_ada[0], v_w_ada[0], 'adamw_w_ada')
    for kind, val in zip(('g', 'd', 'm', 'v'), outs):
        res[kind, 'w_ada'] = val[None]

    return (loss, grad_x[None], *[res['g', n] for n in WEIGHTS], *[res['d', n] for n in WEIGHTS],
            *[res['m', n] for n in WEIGHTS], *[res['v', n] for n in WEIGHTS])
```

```python
import math

import jax
import jax.numpy as jnp
from jax import lax
from jax.experimental import pallas as pl
from jax.experimental.pallas import tpu as pltpu

F32, BF16 = jnp.float32, jnp.bfloat16

D_MODEL = 1024
D_SSM = 512
D_CONV = 512
SSM_GROUP = 16
N_GROUPS = 32
SSM_STATE = 64
N_STATE = N_GROUPS * SSM_STATE
CONV_HEADS = 8
D_FF = 2816
N_MOD = 6
D_IN_PROJ = D_SSM + 3 * D_CONV
N_DEV = 8
FF_SHARD = 2 * D_FF // N_DEV
IN_SHARD = D_IN_PROJ // N_DEV
ADA_SHARD = N_MOD * D_MODEL // N_DEV
EPS = 1e-6
LAMBDA_RE_MAX = -1e-4
ADAM_LR, ADAM_B1, ADAM_B2, ADAM_EPS, ADAM_WD, ADAM_STEP = 0.001, 0.9, 0.999, 1e-08, 0.01, 10
GELU_C = math.sqrt(2.0 / math.pi)
GELU_A = 0.044715

SUBLANES = 8
HALO = 8
HALO16 = 16
SCAN_UNROLL = 16
STATE_BLOCK = 512
CHAN_BLOCK = 128
VMEM_BIG = 48 << 20
VMEM_MOST = 58 << 20

WEIGHTS = ['w_ada', 'b_ada', 'g_pre_mix', 'g_post_mix', 'w_in', 'ssm_lam_re', 'ssm_lam_im', 'ssm_log_step',
           'ssm_b_re', 'ssm_b_im', 'ssm_c_re', 'ssm_c_im', 'ssm_d', 'glu_w', 'glu_b', 'g_out_ssm', 'conv_w',
           'g_out_conv', 'w_out', 'g_pre_ffn', 'g_post_ffn', 'w_up', 'ffn_conv_w', 'w_down']
PACK_COLS = 1024


def _call(body, *, name, grid, in_specs, out_specs, out_shape, scratch=(), sem=None, vmem=None, ride=None):
    params = {}
    if vmem is not None:
        params['vmem_limit_bytes'] = vmem
    if ride is None:
        if sem is not None:
            params['dimension_semantics'] = sem
        return pl.pallas_call(body, name=name, grid=grid, in_specs=in_specs, out_specs=out_specs,
                              out_shape=out_shape, scratch_shapes=list(scratch),
                              compiler_params=pltpu.CompilerParams(**params))
    arrs, scatter = ride
    single = not isinstance(out_shape, (list, tuple))
    out_shape_l = [out_shape] if single else list(out_shape)
    out_specs_l = [out_specs] if single else list(out_specs)
    n, n_in, n_out, n_scr = len(arrs), len(in_specs), len(out_shape_l), len(scratch)
    any_spec = pl.BlockSpec(memory_space=pl.ANY)
    params['dimension_semantics'] = ('arbitrary',) * len(grid)

    def carried(*refs):
        ins, rin = refs[:n_in], refs[n_in:n_in + n]
        outs, rout = refs[n_in + n:n_in + n + n_out], refs[n_in + n + n_out:n_in + 2 * n + n_out]
        scr, sems = refs[n_in + 2 * n + n_out:n_in + 2 * n + n_out + n_scr], refs[n_in + 2 * n + n_out + n_scr:]
        first = pl.program_id(0) == 0
        last = pl.program_id(0) == grid[0] - 1
        for ax in range(1, len(grid)):
            first = jnp.logical_and(first, pl.program_id(ax) == 0)
            last = jnp.logical_and(last, pl.program_id(ax) == grid[ax] - 1)

        @pl.when(first)
        def _():
            _exchange_start(rin, rout, sems, scatter)

        body(*ins, *outs, *scr)

        @pl.when(last)
        def _():
            _exchange_wait(rin, rout, sems, scatter)

    call = pl.pallas_call(carried, name=name, grid=grid, in_specs=list(in_specs) + [any_spec] * n,
                          out_specs=out_specs_l + [any_spec] * n,
                          out_shape=out_shape_l + _exchange_shapes(arrs, scatter),
                          scratch_shapes=list(scratch) + _exchange_sems(n),
                          compiler_params=pltpu.CompilerParams(**params))

    def run(*args):
        res = call(*args, *arrs)
        own = res[0] if single else list(res[:n_out])
        return own, list(res[n_out:])

    return run


def _const(shape):
    nd = len(shape)
    return pl.BlockSpec(shape, lambda *_: (0,) * nd)


def _sds(shape, dtype=F32):
    return jax.ShapeDtypeStruct(shape, dtype)


def _dot(a, b):
    return jnp.dot(a, b, preferred_element_type=F32)


def _dot_nt(a, b):
    return lax.dot_general(a, b, (((1,), (1,)), ((), ())), preferred_element_type=F32)


def _dot_tn(a, b):
    return lax.dot_general(a, b, (((0,), (0,)), ((), ())), preferred_element_type=F32)


def _dot_split(x, mat, parts):
    acc = None
    rem = x
    for _ in range(parts):
        piece = rem.astype(BF16)
        rem = rem - piece.astype(F32)
        term = _dot(piece, mat)
        acc = term if acc is None else acc + term
    return acc


def _sigmoid(x):
    return 1.0 / (1.0 + jnp.exp(-x))


def _gelu(x):
    t = jnp.tanh(GELU_C * (x + GELU_A * x * x * x))
    return 0.5 * x * (1.0 + t), t


def _gelu_grad(x, t):
    return 0.5 * (1.0 + t) + 0.5 * x * (1.0 - t * t) * GELU_C * (1.0 + 3.0 * GELU_A * x * x)


def _rsqrt_mean(x):
    return lax.rsqrt(jnp.mean(x * x, axis=-1, keepdims=True) + EPS)


def _colsum(x):
    return jnp.sum(x, axis=0, keepdims=True)


def _shifts_down(x, halo):
    ext = jnp.concatenate([halo, x], axis=0)
    return pltpu.roll(ext, 1, 0)[halo.shape[0]:], pltpu.roll(ext, 2, 0)[halo.shape[0]:]


def _shifts_up(x, halo):
    n = x.shape[0]
    ext = jnp.concatenate([x, halo], axis=0)
    total = ext.shape[0]
    return pltpu.roll(ext, total - 1, 0)[:n], pltpu.roll(ext, total - 2, 0)[:n]


def _conv3(x, halo, w_ref):
    x1, x2 = _shifts_down(x, halo)
    return w_ref[0:1, :] * x2 + w_ref[1:2, :] * x1 + w_ref[2:3, :] * x, x1, x2


def _conv3_t(g, halo, w_ref):
    g1, g2 = _shifts_up(g, halo)
    return w_ref[2:3, :] * g + w_ref[1:2, :] * g1 + w_ref[0:1, :] * g2, g1, g2


def _silu_parts(x):
    s = _sigmoid(x)
    return x * s, s * (1.0 + x * (1.0 - s))


def _norm_bwd(dn, x, r, g):
    gd = g * dn
    return r * gd - x * (r * r * r) * jnp.mean(gd * x, axis=-1, keepdims=True)


def _head_norm_bwd(dn, y, rs, g, avg):
    gd = g * dn
    return rs * gd - y * (rs * rs * rs) * _dot_split(gd * y, avg, 2)


def _me():
    x, y, c = lax.axis_index('x'), lax.axis_index('y'), lax.axis_index('c')
    return x, y, c, 4 * x + 2 * y + c


def _peer(k):
    x, y, c, _ = _me()
    px = 1 - x if k & 4 else x
    py = 1 - y if k & 2 else y
    pc = 1 - c if k & 1 else c
    return (px, py, pc), 4 * px + 2 * py + pc


SIBLING = 1
OTHER_CHIPS = (2, 4, 6)


def _remote(src, dst, sems, a, k, dev):
    return pltpu.make_async_remote_copy(src_ref=src, dst_ref=dst, send_sem=sems[0].at[a, k - 1],
                                        recv_sem=sems[1].at[a, k - 1], device_id=dev,
                                        device_id_type=pl.DeviceIdType.MESH)


def _exchange_copies(ins, outs, sems, scatter):
    me = _me()[3]
    local, first, relay, arrivals = [], [], [], []
    for a in range(len(ins)):
        src = ins[a].at[me] if scatter else ins[a]
        local.append(pltpu.make_async_copy(src, outs[a].at[me], sems[2].at[a]))
        for k in range(1, N_DEV):
            dev, idx = _peer(k)
            landed = _remote(src, outs[a].at[idx], sems, a, k, dev)
            if scatter:
                first.append(_remote(ins[a].at[idx], outs[a].at[me], sems, a, k, dev))
                arrivals.append(landed)
            elif k == SIBLING:
                first.append(_remote(src, outs[a].at[me], sems, a, k, dev))
                arrivals.append(landed)
            elif k in OTHER_CHIPS:
                first.append(_remote(src, outs[a].at[me], sems, a, k, dev))
                sib, _ = _peer(SIBLING)
                relay.append((landed, _remote(outs[a].at[idx], outs[a].at[idx], sems, a, k | SIBLING, sib)))
            else:
                arrivals.append(landed)
    return local, first, relay, arrivals


def _exchange_start(ins, outs, sems, scatter):
    local, first, _, _ = _exchange_copies(ins, outs, sems, scatter)
    for cp in local + first:
        cp.start()


def _exchange_wait(ins, outs, sems, scatter):
    local, first, relay, arrivals = _exchange_copies(ins, outs, sems, scatter)
    for landed, forward in relay:
        landed.wait_recv()
        forward.start()
    for cp in arrivals:
        cp.wait_recv()
    for cp in first + [forward for _, forward in relay]:
        cp.wait_send()
    for cp in local:
        cp.wait()


def _exchange_shapes(arrs, scatter):
    return [_sds(a.shape if scatter else (N_DEV,) + a.shape, a.dtype) for a in arrs]


def _exchange_sems(n):
    return [pltpu.SemaphoreType.DMA((n, N_DEV - 1)), pltpu.SemaphoreType.DMA((n, N_DEV - 1)),
            pltpu.SemaphoreType.DMA((n,))]


def _exchange(arrs, *, name, scatter):
    n = len(arrs)

    def body(*refs):
        _exchange_start(refs[:n], refs[n:2 * n], refs[2 * n:], scatter)
        _exchange_wait(refs[:n], refs[n:2 * n], refs[2 * n:], scatter)

    any_spec = pl.BlockSpec(memory_space=pl.ANY)
    outs = pl.pallas_call(body, name=name, out_shape=_exchange_shapes(arrs, scatter), in_specs=[any_spec] * n,
                          out_specs=[any_spec] * n, scratch_shapes=_exchange_sems(n))(*arrs)
    return list(outs)


def _mod_cols(c_all, w_ada, b_cols):
    def body(c_ref, w_ref, b_ref, mod_ref, act_ref):
        c = c_ref[...]
        act = c * _sigmoid(c)
        act_ref[...] = act
        mod_ref[...] = _dot(act.astype(BF16), w_ref[...].astype(BF16)) + b_ref[...]

    return _call(body, name='mod_cols', grid=(1,),
                 in_specs=[_const(c_all.shape), _const(w_ada.shape), _const(b_cols.shape)],
                 out_specs=[_const((N_DEV, ADA_SHARD)), _const(c_all.shape)],
                 out_shape=[_sds((N_DEV, ADA_SHARD)), _sds(c_all.shape)], vmem=VMEM_BIG)(c_all, w_ada, b_cols)


def _grad_w_ada(act_t, dmod_cols):
    def body(a_ref, d_ref, o_ref):
        o_ref[...] = _dot(a_ref[...], d_ref[...])

    return _call(body, name='grad_w_ada', grid=(1,), in_specs=[_const(act_t.shape), _const(dmod_cols.shape)],
                 out_specs=_const((D_MODEL, ADA_SHARD)), out_shape=_sds((D_MODEL, ADA_SHARD)),
                 vmem=VMEM_BIG)(act_t, dmod_cols)


def _pre_mix(x, sc, sh, g, w_s, tm, ride):
    T = x.shape[0]
    group = 4

    def body(x_ref, sc_ref, sh_ref, g_ref, w_ref, proj_ref, h_ref):
        @pl.when(pl.program_id(1) == 0)
        def _():
            xv = x_ref[...]
            h_ref[...] = ((xv * _rsqrt_mean(xv) * g_ref[...]) * (1.0 + sc_ref[...]) + sh_ref[...]).astype(BF16)

        for s in range(group):
            proj_ref[:, s * IN_SHARD:(s + 1) * IN_SHARD] = _dot(h_ref[...], w_ref[s])

    row = pl.BlockSpec((tm, D_MODEL), lambda i, j: (i, 0))
    vec = _const((1, D_MODEL))
    return _call(body, name='pre_mix', grid=(T // tm, N_DEV // group),
                 in_specs=[row, vec, vec, vec, pl.BlockSpec((group, D_MODEL, IN_SHARD), lambda i, j: (j, 0, 0))],
                 out_specs=[pl.BlockSpec((tm, group * IN_SHARD), lambda i, j: (i, j)), row],
                 out_shape=[_sds((T, D_IN_PROJ)), _sds((T, D_MODEL), BF16)],
                 sem=('parallel', 'arbitrary'), ride=ride)(x, sc, sh, g, w_s)


def _halo_before(tm, rows=HALO):
    return lambda i: jnp.maximum(i * (tm // rows) - 1, 0)


def _halo_after(tm, T, rows=HALO):
    return lambda i: jnp.minimum((i + 1) * (tm // rows), T // rows - 1)


def _mix_fwd(yssm, proj, d, glu_w, glu_b, g_ssm, cw, g_conv, avg16, avg64, tm):
    T = yssm.shape[0]
    hb = _halo_before(tm)

    def body(y_ref, p_ref, ph_ref, d_ref, gw_ref, gb_ref, gs_ref, cw_ref, gc_ref, a16_ref, a64_ref, o_ref):
        i = pl.program_id(0)
        u = p_ref[:, 0:D_SSM]
        y = y_ref[...] + d_ref[...] * u
        z, _ = _gelu(y)
        gate = _sigmoid(_dot(z.astype(BF16), gw_ref[...]) + gb_ref[...])
        ya = z * gate
        rs = lax.rsqrt(_dot_split(ya * ya, a16_ref[...], 2) + EPS)
        o_ref[:, 0:D_SSM] = (ya * rs * gs_ref[...]).astype(BF16)
        bg = p_ref[:, D_SSM:D_SSM + D_CONV]
        cv = p_ref[:, D_SSM + D_CONV:D_SSM + 2 * D_CONV] * p_ref[:, D_SSM + 2 * D_CONV:D_IN_PROJ]
        hv = ph_ref[:, D_SSM + D_CONV:D_SSM + 2 * D_CONV] * ph_ref[:, D_SSM + 2 * D_CONV:D_IN_PROJ]
        hv = jnp.where(i > 0, hv, 0.0)
        conv, _, _ = _conv3(cv, hv, cw_ref)
        yb = bg * conv
        rsb = lax.rsqrt(_dot_split(yb * yb, a64_ref[...], 2) + EPS)
        o_ref[:, D_SSM:D_MODEL] = (yb * rsb * gc_ref[...]).astype(BF16)

    vec = _const((1, D_SSM))
    sq = _const((D_SSM, D_SSM))
    return _call(body, name='mix_fwd', grid=(T // tm,),
                 in_specs=[pl.BlockSpec((tm, D_SSM), lambda i: (i, 0)), pl.BlockSpec((tm, D_IN_PROJ), lambda i: (i, 0)),
                           pl.BlockSpec((HALO, D_IN_PROJ), lambda i: (hb(i), 0)), vec, sq, vec, vec,
                           _const((3, D_CONV)), vec, sq, sq],
                 out_specs=pl.BlockSpec((tm, D_MODEL), lambda i: (i, 0)), out_shape=_sds((T, D_MODEL), BF16),
                 sem=('parallel',), vmem=VMEM_BIG)(yssm, proj, proj, d, glu_w, glu_b, g_ssm, cw, g_conv, avg16, avg64)


def _out_proj(ycat, w_out, x, gt, g_post, g_pre, sc, sh, tm):
    T = x.shape[0]

    def body(y_ref, w_ref, x_ref, gt_ref, gp_ref, g2_ref, sc_ref, sh_ref, o_ref, x1_ref, h_ref):
        o = _dot(y_ref[...], w_ref[...])
        o_ref[...] = o.astype(BF16)
        x1 = x_ref[...] + gt_ref[...] * (o * _rsqrt_mean(o) * gp_ref[...])
        x1_ref[...] = x1
        h_ref[...] = ((x1 * _rsqrt_mean(x1) * g2_ref[...]) * (1.0 + sc_ref[...]) + sh_ref[...]).astype(BF16)

    row = pl.BlockSpec((tm, D_MODEL), lambda i: (i, 0))
    vec = _const((1, D_MODEL))
    return _call(body, name='out_proj', grid=(T // tm,),
                 in_specs=[row, _const((D_MODEL, D_MODEL)), row, vec, vec, vec, vec, vec],
                 out_specs=[row, row, row],
                 out_shape=[_sds((T, D_MODEL), BF16), _sds((T, D_MODEL)), _sds((T, D_MODEL), BF16)],
                 sem=('parallel',), vmem=VMEM_BIG)(ycat, w_out, x, gt, g_post, g_pre, sc, sh)


def _ffn_up(h2, w_a, w_b, cw8, tm, ride):
    T = h2.shape[0]
    hb = _halo_before(tm, HALO16)
    half = D_MODEL // 2

    def body(h_ref, hh_ref, wa_ref, wb_ref, cw_ref, up_ref, hid_ref):
        def times_w(ref, s):
            return _dot_nt(ref[:, :half], wa_ref[s]) + _dot_nt(ref[:, half:], wb_ref[s])

        for s in range(2):
            up = times_w(h_ref, s)
            up_ref[s] = up.astype(BF16)
            before = jnp.where(pl.program_id(0) > 0, times_w(hh_ref, s), 0.0)
            hid_ref[s] = _conv3(up, before, cw_ref.at[s])[0].astype(BF16)

    out = pl.BlockSpec((2, tm, FF_SHARD), lambda i, j: (j, i, 0))
    return _call(body, name='ffn_up', grid=(T // tm, N_DEV // 2),
                 in_specs=[pl.BlockSpec((tm, D_MODEL), lambda i, j: (i, 0)),
                           pl.BlockSpec((HALO16, D_MODEL), lambda i, j: (hb(i), 0)),
                           pl.BlockSpec((2, FF_SHARD, half), lambda i, j: (j, 0, 0)),
                           pl.BlockSpec((2, FF_SHARD, half), lambda i, j: (j, 0, 0)),
                           pl.BlockSpec((2, 3, FF_SHARD), lambda i, j: (j, 0, 0))],
                 out_specs=[out, out], out_shape=[_sds((N_DEV, T, FF_SHARD), BF16)] * 2,
                 sem=('parallel', 'parallel'), vmem=VMEM_BIG, ride=ride)(h2, h2, w_a, w_b, cw8)


def _ffn_down(hid4, wd4, x1, tgt, gt, g_post, tm):
    T = x1.shape[0]
    nb = T // tm

    def body(a_ref, w_ref, x1_ref, t_ref, gt_ref, g_ref, ddn_ref, dx_ref, loss_ref, dgt_ref, dg_ref, dn_ref):
        i, j = pl.program_id(0), pl.program_id(1)
        part = None
        for s in range(4):
            act = (_silu_parts(a_ref[0, s].astype(F32))[0] * a_ref[1, s].astype(F32)).astype(BF16)
            term = _dot(act, w_ref[s])
            part = term if part is None else part + term

        @pl.when(jnp.logical_and(i == 0, j == 0))
        def _():
            dgt_ref[...] = jnp.zeros_like(dgt_ref)
            dg_ref[...] = jnp.zeros_like(dg_ref)

        @pl.when(j == 0)
        def _():
            dn_ref[...] = part

        @pl.when(j > 0)
        def _():
            dn_ref[...] += part

        @pl.when(j == 0)
        def _():
            dn, gv, gate = dn_ref[...], g_ref[...], gt_ref[...]
            r = _rsqrt_mean(dn)
            normed = dn * r * gv
            err = x1_ref[...] + gate * normed - t_ref[...]
            dx = err * (1.0 / D_MODEL)
            dx_ref[...] = dx
            tot = jnp.sum(jnp.sum(err * err, axis=1, keepdims=True), axis=0, keepdims=True) * (0.5 / D_MODEL)
            loss_ref[...] = jnp.broadcast_to(tot, (8, 128))
            dgt_ref[...] += _colsum(dx * normed)
            dnn = dx * gate
            dg_ref[...] += _colsum(dnn * dn * r)
            ddn_ref[...] = _norm_bwd(dnn, dn, r, gv).astype(BF16)

    row = pl.BlockSpec((tm, D_MODEL), lambda i, j: (i, 0))
    vec = _const((1, D_MODEL))
    return _call(body, name='ffn_down', grid=(nb, 1),
                 in_specs=[pl.BlockSpec((2, 4, tm, FF_SHARD), lambda i, j: (0, j, i, 0)),
                           pl.BlockSpec((4, FF_SHARD, D_MODEL), lambda i, j: (j, 0, 0)), row, row, vec, vec],
                 out_specs=[row, row, pl.BlockSpec((None, 8, 128), lambda i, j: (i, 0, 0)), vec, vec],
                 out_shape=[_sds((T, D_MODEL), BF16), _sds((T, D_MODEL)), _sds((nb, 8, 128)), _sds((1, D_MODEL)),
                            _sds((1, D_MODEL))],
                 scratch=[pltpu.VMEM((tm, D_MODEL), F32)], sem=('arbitrary', 'arbitrary'),
                 vmem=VMEM_MOST)(hid4, wd4, x1, tgt, gt, g_post)


def _ssm_prep(lre, lim, lst, b_re, b_im):
    def body(lre_ref, lim_ref, lst_ref, br_ref, bi_ref, ar_ref, ai_ref, bbr_ref, bbi_ref):
        ar, ai, qr, qi = _zoh(lre_ref[...], lim_ref[...], lst_ref[...])[:4]
        ar_ref[...] = ar
        ai_ref[...] = ai
        bbr_ref[...] = qr * br_ref[...] - qi * bi_ref[...]
        bbi_ref[...] = qr * bi_ref[...] + qi * br_ref[...]

    shp = lre.shape
    return _call(body, name='ssm_prep', grid=(1,), in_specs=[_const(shp)] * 5, out_specs=[_const(shp)] * 4,
                 out_shape=[_sds(shp)] * 4)(lre, lim, lst, b_re, b_im)


def _zoh(lre, lim, lst):
    lr = jnp.minimum(lre, LAMBDA_RE_MAX)
    st = jnp.exp(lst)
    mag = jnp.exp(lr * st)
    ar = mag * jnp.cos(lim * st)
    ai = mag * jnp.sin(lim * st)
    den = lr * lr + lim * lim
    qr = ((ar - 1.0) * lr + ai * lim) / den
    qi = (ai * lr - (ar - 1.0) * lim) / den
    return ar, ai, qr, qi, lr, st, den


def _ssm_prep_bwd(lre, lim, lst, b_re, b_im, dbbr, dbbi, dar, dai, seg):
    def body(lre_ref, lim_ref, lst_ref, br_ref, bi_ref, dbbr_ref, dbbi_ref, dar_ref, dai_ref, seg_ref,
             dbr_ref, dbi_ref, dlre_ref, dlim_ref, dlst_ref):
        lre_v = lre_ref[...]
        li = lim_ref[...]
        ar, ai, qr, qi, lr, st, den = _zoh(lre_v, li, lst_ref[...])
        br, bi, gbr, gbi = br_ref[...], bi_ref[...], dbbr_ref[...], dbbi_ref[...]
        dbr_ref[...] = qr * gbr + qi * gbi
        dbi_ref[...] = qr * gbi - qi * gbr
        gqr = _dot_split(br * gbr + bi * gbi, seg_ref[...], 3)
        gqi = _dot_split(br * gbi - bi * gbr, seg_ref[...], 3)
        ir, ii = lr / den, -li / den
        gar = dar_ref[...] + ir * gqr + ii * gqi
        gai = dai_ref[...] + ir * gqi - ii * gqr
        tr, ti = qr * ir - qi * ii, qr * ii + qi * ir
        glr = -(tr * gqr + ti * gqi)
        gli = -(tr * gqi - ti * gqr)
        gzr = ar * gar + ai * gai
        gzi = ar * gai - ai * gar
        glr = glr + st * gzr
        gli = gli + st * gzi
        gst = (lr * gzr + li * gzi) * st
        dlre_ref[...] = jnp.where(lre_v < LAMBDA_RE_MAX, glr, 0.0)
        dlim_ref[...] = gli
        dlst_ref[...] = jnp.sum(gst, axis=1, keepdims=True) * (1.0 / SSM_GROUP)

    shp = lre.shape
    return _call(body, name='ssm_prep_bwd', grid=(1,), in_specs=[_const(shp)] * 9 + [_const(seg.shape)],
                 out_specs=[_const(shp)] * 4 + [_const((N_GROUPS, 1))],
                 out_shape=[_sds(shp)] * 4 + [_sds((N_GROUPS, 1))], vmem=VMEM_BIG)(
                     lre, lim, lst, b_re, b_im, dbbr, dbbi, dar, dai, seg)


def _scan_specs(T):
    return dict(
        chan=pl.BlockSpec((T, CHAN_BLOCK), lambda cb: (0, cb)),
        state=pl.BlockSpec((T, STATE_BLOCK), lambda cb: (0, cb)),
        b=pl.BlockSpec((CHAN_BLOCK, STATE_BLOCK), lambda cb: (cb, cb)),
        c=pl.BlockSpec((STATE_BLOCK, CHAN_BLOCK), lambda cb: (cb, cb)),
        lam=pl.BlockSpec((1, STATE_BLOCK), lambda cb: (0, cb)),
    )


def _complex_power(re, im, n):
    out = None
    while True:
        if n & 1:
            out = (re, im) if out is None else (out[0] * re - out[1] * im, out[0] * im + out[1] * re)
        n >>= 1
        if n == 0:
            return out
        re, im = re * re - im * im, 2.0 * re * im


def _rows8(i):
    if isinstance(i, int):
        return pl.ds(i * SUBLANES, SUBLANES)
    return pl.ds(pl.multiple_of(i * SUBLANES, SUBLANES), SUBLANES)


def _scan_loop(n_steps, body, init):
    trips = n_steps // SCAN_UNROLL

    def trip(t, carry):
        for u in range(SCAN_UNROLL):
            carry = body(t * SCAN_UNROLL + u, carry)
        return carry

    carry = lax.fori_loop(0, trips, trip, init)
    for step in range(trips * SCAN_UNROLL, n_steps):
        carry = body(step, carry)
    return carry


def _ssm_fwd(u_perm, b_re, b_im, c_re, c_im, lam_r, lam_i, ride):
    T = u_perm.shape[0]
    ls = T // SUBLANES
    rc = min(1024, T)
    sp = _scan_specs(T)

    def body(u_ref, bre_ref, bim_ref, cre_ref, cim_ref, lr_ref, li_ref, so_re_ref, so_im_ref, y_ref, sre_ref, sim_ref):
        for c in range(T // rc):
            rows = pl.ds(c * rc, rc)
            ub = u_ref[rows, :].astype(BF16)
            sre_ref[rows, :] = _dot(ub, bre_ref[...])
            sim_ref[rows, :] = _dot(ub, bim_ref[...])
        shp = (SUBLANES, STATE_BLOCK)
        lr = jnp.broadcast_to(lr_ref[...], shp)
        li = jnp.broadcast_to(li_ref[...], shp)
        zero = jnp.zeros(shp, F32)

        def step(i, carry):
            sr, si = carry
            rows = _rows8(i)
            nr = lr * sr - li * si + sre_ref[rows, :]
            ni = lr * si + li * sr + sim_ref[rows, :]
            sre_ref[rows, :] = nr
            sim_ref[rows, :] = ni
            return nr, ni

        fr, fi = _scan_loop(ls, step, (zero, zero))
        pr, pi_ = _complex_power(lr, li, ls)
        row = lax.broadcasted_iota(jnp.int32, shp, 0)
        ir, ii = zero, zero
        for _ in range(SUBLANES - 1):
            er = fr + pr * ir - pi_ * ii
            ei = fi + pr * ii + pi_ * ir
            ir = jnp.where(row == 0, 0.0, pltpu.roll(er, 1, 0))
            ii = jnp.where(row == 0, 0.0, pltpu.roll(ei, 1, 0))

        def fix(i, carry):
            cr, ci = carry
            rows = _rows8(i)
            nr = lr * cr - li * ci
            ni = lr * ci + li * cr
            sre_ref[rows, :] += nr
            sim_ref[rows, :] += ni
            return nr, ni

        _scan_loop(ls, fix, (ir, ii))
        for c in range(T // rc):
            rows = pl.ds(c * rc, rc)
            s_r, s_i = sre_ref[rows, :].astype(BF16), sim_ref[rows, :].astype(BF16)
            so_re_ref[rows, :] = s_r
            so_im_ref[rows, :] = s_i
            y_ref[rows, :] = _dot(s_r, cre_ref[...]) - _dot(s_i, cim_ref[...])

    return _call(body, name='ssm_fwd', grid=(N_STATE // STATE_BLOCK,),
                 in_specs=[sp['chan'], sp['b'], sp['b'], sp['c'], sp['c'], sp['lam'], sp['lam']],
                 out_specs=[sp['state'], sp['state'], sp['chan']],
                 out_shape=[_sds((T, N_STATE), BF16), _sds((T, N_STATE), BF16), _sds((T, D_SSM))],
                 scratch=[pltpu.VMEM((T, STATE_BLOCK), F32), pltpu.VMEM((T, STATE_BLOCK), F32)],
                 sem=('arbitrary',), vmem=VMEM_MOST, ride=ride)(u_perm, b_re, b_im, c_re, c_im, lam_r, lam_i)


def _ssm_bwd(dy_perm, u_perm, s_re, s_im, b_re, b_im, c_re, c_im, lam_r, lam_i, ride):
    T = u_perm.shape[0]
    ls = T // SUBLANES
    rc = min(1024, T)
    sp = _scan_specs(T)
    ncb = N_STATE // STATE_BLOCK

    def body(dy_ref, u_ref, sre_ref, sim_ref, bre_ref, bim_ref, cre_ref, cim_ref, lr_ref, li_ref,
             du_ref, dbr_ref, dbi_ref, dcr_ref, dci_ref, dar_ref, dai_ref, gre_ref, gim_ref):
        shp = (SUBLANES, STATE_BLOCK)
        zero = jnp.zeros(shp, F32)
        tail = pl.ds(T, SUBLANES)
        gre_ref[tail, :] = zero
        gim_ref[tail, :] = zero
        for c in range(T // rc):
            rows = pl.ds(c * rc, rc)
            dyb = dy_ref[rows, :].astype(BF16)
            gre_ref[rows, :] = _dot_nt(dyb, cre_ref[...])
            gim_ref[rows, :] = -_dot_nt(dyb, cim_ref[...])
        lr = jnp.broadcast_to(lr_ref[...], shp)
        li = jnp.broadcast_to(li_ref[...], shp)

        def step(k, carry):
            gr, gi = carry
            rows = _rows8(ls - 1 - k)
            nr = lr * gr + li * gi + gre_ref[rows, :]
            ni = lr * gi - li * gr + gim_ref[rows, :]
            gre_ref[rows, :] = nr
            gim_ref[rows, :] = ni
            return nr, ni

        fr, fi = _scan_loop(ls, step, (zero, zero))
        pr, pi_ = _complex_power(lr, -li, ls)
        row = lax.broadcasted_iota(jnp.int32, shp, 0)
        cr, ci = zero, zero
        for _ in range(SUBLANES - 1):
            er = fr + pr * cr - pi_ * ci
            ei = fi + pr * ci + pi_ * cr
            cr = jnp.where(row == SUBLANES - 1, 0.0, pltpu.roll(er, SUBLANES - 1, 0))
            ci = jnp.where(row == SUBLANES - 1, 0.0, pltpu.roll(ei, SUBLANES - 1, 0))

        def fix(k, carry):
            dr, di = carry
            rows = _rows8(ls - 1 - k)
            dr, di = lr * dr + li * di, lr * di - li * dr
            gre_ref[rows, :] += dr
            gim_ref[rows, :] += di
            return dr, di

        _scan_loop(ls, fix, (cr, ci))

        acc_r = jnp.zeros((1, STATE_BLOCK), F32)
        acc_i = jnp.zeros((1, STATE_BLOCK), F32)
        for c in range(T // rc):
            rows, nxt = pl.ds(c * rc, rc), pl.ds(c * rc + SUBLANES, rc)
            s_r, s_i = sre_ref[rows, :].astype(F32), sim_ref[rows, :].astype(F32)
            g_r, g_i = gre_ref[nxt, :], gim_ref[nxt, :]
            acc_r = acc_r + _colsum(g_r * s_r + g_i * s_i)
            acc_i = acc_i + _colsum(g_i * s_r - g_r * s_i)
        last = pl.ds(T - 2 * SUBLANES, 2 * SUBLANES)
        first = pl.ds(0, SUBLANES)
        spr = jnp.where(row == 0, 0.0, pltpu.roll(sre_ref[last, :].astype(F32)[SUBLANES:], 1, 0))
        spi = jnp.where(row == 0, 0.0, pltpu.roll(sim_ref[last, :].astype(F32)[SUBLANES:], 1, 0))
        gr, gi = gre_ref[first, :], gim_ref[first, :]
        dar_ref[...] = acc_r + _colsum(gr * spr + gi * spi)
        dai_ref[...] = acc_i + _colsum(gi * spr - gr * spi)

        for c in range(T // rc):
            rows = pl.ds(c * rc, rc)
            g_r, g_i = gre_ref[rows, :].astype(BF16), gim_ref[rows, :].astype(BF16)
            s_r, s_i = sre_ref[rows, :], sim_ref[rows, :]
            ub, dyb = u_ref[rows, :].astype(BF16), dy_ref[rows, :].astype(BF16)
            du_ref[rows, :] = _dot_nt(g_r, bre_ref[...]) + _dot_nt(g_i, bim_ref[...])
            parts = (_dot_tn(ub, g_r), _dot_tn(ub, g_i), _dot_tn(s_r, dyb), -_dot_tn(s_i, dyb))
            outs = (dbr_ref, dbi_ref, dcr_ref, dci_ref)
            for o_ref, part in zip(outs, parts):
                if c == 0:
                    o_ref[...] = part
                else:
                    o_ref[...] += part

    blk = lambda r, c: pl.BlockSpec((None, r, c), lambda cb: (cb, 0, 0))
    return _call(body, name='ssm_bwd', grid=(ncb,),
                 in_specs=[sp['chan'], sp['chan'], sp['state'], sp['state'], sp['b'], sp['b'], sp['c'], sp['c'],
                           sp['lam'], sp['lam']],
                 out_specs=[sp['chan'], blk(CHAN_BLOCK, STATE_BLOCK), blk(CHAN_BLOCK, STATE_BLOCK),
                            blk(STATE_BLOCK, CHAN_BLOCK), blk(STATE_BLOCK, CHAN_BLOCK), blk(1, STATE_BLOCK),
                            blk(1, STATE_BLOCK)],
                 out_shape=[_sds((T, D_SSM)), _sds((ncb, CHAN_BLOCK, STATE_BLOCK)), _sds((ncb, CHAN_BLOCK, STATE_BLOCK)),
                            _sds((ncb, STATE_BLOCK, CHAN_BLOCK)), _sds((ncb, STATE_BLOCK, CHAN_BLOCK)),
                            _sds((ncb, 1, STATE_BLOCK)), _sds((ncb, 1, STATE_BLOCK))],
                 scratch=[pltpu.VMEM((T + SUBLANES, STATE_BLOCK), F32), pltpu.VMEM((T + SUBLANES, STATE_BLOCK), F32)],
                 sem=('arbitrary',), vmem=VMEM_MOST, ride=ride)(dy_perm, u_perm, s_re, s_im, b_re, b_im, c_re, c_im,
                                                                lam_r, lam_i)


def _ffn_dact(ddn, wd4, hid4, tm):
    T = ddn.shape[0]
    nb = T // tm

    def body(d_ref, w_ref, hid_ref, o_ref, gw_ref, acc_ref):
        i = pl.program_id(1)
        d = d_ref[...]
        dact = _dot_nt(d, w_ref[...])
        silu, dsilu = _silu_parts(hid_ref[0].astype(F32))
        hid_v = hid_ref[1].astype(F32)
        o_ref[0] = (dact * hid_v * dsilu).astype(BF16)
        o_ref[1] = (dact * silu).astype(BF16)
        part = _dot_tn((silu * hid_v).astype(BF16), d)

        @pl.when(i == 0)
        def _():
            acc_ref[...] = part

        @pl.when(i > 0)
        def _():
            acc_ref[...] += part

        @pl.when(i == nb - 1)
        def _():
            gw_ref[...] = acc_ref[...].astype(BF16)

    blk = pl.BlockSpec((2, None, tm, FF_SHARD), lambda j, i: (0, j, i, 0))
    w_blk = pl.BlockSpec((None, FF_SHARD, D_MODEL), lambda j, i: (j, 0, 0))
    return _call(body, name='ffn_dact', grid=(4, nb),
                 in_specs=[pl.BlockSpec((tm, D_MODEL), lambda j, i: (i, 0)), w_blk, blk],
                 out_specs=[blk, w_blk],
                 out_shape=[_sds((2, 4, T, FF_SHARD), BF16), _sds((4, FF_SHARD, D_MODEL), BF16)],
                 scratch=[pltpu.VMEM((FF_SHARD, D_MODEL), F32)], sem=('parallel', 'arbitrary'),
                 vmem=VMEM_BIG)(ddn, wd4, hid4)


def _ffn_dup(dhid8, up8, cw8, tm, ride):
    T = up8.shape[1]
    nb = T // tm
    ha = _halo_after(tm, T, HALO16)

    def body(dh_ref, dha_ref, up_ref, cw_ref, dup_ref, dcw_ref):
        i = pl.program_id(1)

        @pl.when(i == 0)
        def _():
            dcw_ref[...] = jnp.zeros_like(dcw_ref)

        dh = dh_ref[...].astype(F32)
        dup, dh1, dh2 = _conv3_t(dh, jnp.where(i < nb - 1, dha_ref[...].astype(F32), 0.0), cw_ref)
        dup_ref[...] = dup.astype(BF16)
        up = up_ref[...].astype(F32)
        dcw_ref[0:1, :] += _colsum(dh2 * up)
        dcw_ref[1:2, :] += _colsum(dh1 * up)
        dcw_ref[2:3, :] += _colsum(dh * up)

    main = pl.BlockSpec((None, tm, FF_SHARD), lambda j, i: (j, i, 0))
    return _call(body, name='ffn_dup', grid=(N_DEV, nb),
                 in_specs=[main, pl.BlockSpec((None, HALO16, FF_SHARD), lambda j, i: (j, ha(i), 0)), main,
                           pl.BlockSpec((None, 3, FF_SHARD), lambda j, i: (j, 0, 0))],
                 out_specs=[main, pl.BlockSpec((None, 8, FF_SHARD), lambda j, i: (j, 0, 0))],
                 out_shape=[_sds((N_DEV, T, FF_SHARD), BF16), _sds((N_DEV, 8, FF_SHARD))],
                 sem=('parallel', 'arbitrary'), vmem=VMEM_BIG, ride=ride)(dhid8, dhid8, up8, cw8)


def _grad_tn(a, b, a_spec, b_spec, groups, m, n, tk, name, ride=None, parts=1):
    T = a.shape[-2]
    nk = T // tk
    mp = m // parts

    def body(a_ref, b_ref, *refs):
        o_refs, acc_ref = refs[:parts], refs[parts]
        k = pl.program_id(1)
        part = _dot_tn(a_ref[...], b_ref[...])

        @pl.when(k == 0)
        def _():
            acc_ref[...] = part

        @pl.when(k > 0)
        def _():
            acc_ref[...] += part

        @pl.when(k == nk - 1)
        def _():
            for p, o_ref in enumerate(o_refs):
                o_ref[...] = acc_ref[p * mp:(p + 1) * mp, :].astype(BF16)

    out_spec = pl.BlockSpec((None, mp, n), lambda g, k: (g, 0, 0))
    res = _call(body, name=name, grid=(groups, nk), in_specs=[a_spec, b_spec], out_specs=[out_spec] * parts,
                out_shape=[_sds((groups, mp, n), BF16)] * parts, scratch=[pltpu.VMEM((m, n), F32)],
                sem=('parallel', 'arbitrary'), vmem=VMEM_BIG, ride=ride)(a, b)
    if parts > 1:
        return res
    return res[0] if ride is None else (res[0][0], res[1])


def _grad_w_in(h1, dproj, tk, ride):
    T = h1.shape[0]
    nk = T // tk
    half = D_IN_PROJ // 2

    def body(a_ref, b_ref, o_ref, acc_ref):
        k = pl.program_id(0)
        for h in range(2):
            cols = slice(h * half, (h + 1) * half)
            part = _dot_tn(a_ref[...], b_ref[:, cols])

            @pl.when(k == 0)
            def _():
                acc_ref[:, cols] = part

            @pl.when(k > 0)
            def _():
                acc_ref[:, cols] += part

        @pl.when(k == nk - 1)
        def _():
            for g in range(N_DEV):
                o_ref[g] = acc_ref[:, g * IN_SHARD:(g + 1) * IN_SHARD].astype(BF16)

    return _call(body, name='grad_w_in', grid=(nk,),
                 in_specs=[pl.BlockSpec((tk, D_MODEL), lambda k: (k, 0)), pl.BlockSpec((tk, D_IN_PROJ), lambda k: (k, 0))],
                 out_specs=_const((N_DEV, D_MODEL, IN_SHARD)), out_shape=_sds((N_DEV, D_MODEL, IN_SHARD), BF16),
                 scratch=[pltpu.VMEM((D_MODEL, D_IN_PROJ), F32)], sem=('arbitrary',), vmem=VMEM_BIG, ride=ride)(h1, dproj)


def _pre_norm_bwd(dz, dz_spec, w_parts, xin, dres, sc, g, tm, name, ride, below=None, group=1, w_t=False):
    T = xin.shape[0]
    n = w_parts[0].shape[1] if w_t else w_parts[0].shape[2]
    mul = _dot if w_t else _dot_nt
    steps = N_DEV // group
    width = D_MODEL // len(w_parts)

    def body(dz_ref, *refs):
        w_refs, (x_ref, dr_ref, sc_ref, g_ref), refs = refs[:len(w_parts)], refs[len(w_parts):len(w_parts) + 4], \
            refs[len(w_parts) + 4:]
        if below is None:
            dx_ref, dsh_ref, dsc_ref, dg_ref = refs
            sums = (dsh_ref, dsc_ref, dg_ref)
        else:
            v_ref, gate_ref, g2_ref, dx_ref, dsh_ref, dsc_ref, dg_ref, dv_ref, dgate_ref, dg2_ref = refs
            sums = (dsh_ref, dsc_ref, dg_ref, dgate_ref, dg2_ref)
        i, j = pl.program_id(0), pl.program_id(1)
        piece = (lambda s: dz_ref[s]) if dz.ndim == 3 else (lambda s: dz_ref[:, s * n:(s + 1) * n])
        parts = []
        for w_ref in w_refs:
            part = mul(piece(0), w_ref[0])
            for s in range(1, group):
                part = part + mul(piece(s), w_ref[s])
            parts.append(part)

        @pl.when(jnp.logical_and(i == 0, j == 0))
        def _():
            for s_ref in sums:
                s_ref[...] = jnp.zeros_like(s_ref)

        @pl.when(j == 0)
        def _():
            for k, part in enumerate(parts):
                dx_ref[:, k * width:(k + 1) * width] = part

        @pl.when(j > 0)
        def _():
            for k, part in enumerate(parts):
                dx_ref[:, k * width:(k + 1) * width] += part

        @pl.when(j == steps - 1)
        def _():
            dh, xv, gv = dx_ref[...], x_ref[...], g_ref[...]
            r = _rsqrt_mean(xv)
            dsh_ref[...] += _colsum(dh)
            dsc_ref[...] += _colsum(dh * (xv * r * gv))
            dxn = dh * (1.0 + sc_ref[...])
            dg_ref[...] += _colsum(dxn * xv * r)
            dx = dr_ref[...] + _norm_bwd(dxn, xv, r, gv)
            dx_ref[...] = dx
            if below is not None:
                v, g2 = v_ref[...].astype(F32), g2_ref[...]
                rv = _rsqrt_mean(v)
                dgate_ref[...] += _colsum(dx * (v * rv * g2))
                dn = dx * gate_ref[...]
                dg2_ref[...] += _colsum(dn * v * rv)
                dv_ref[...] = _norm_bwd(dn, v, rv, g2).astype(BF16)

    row = pl.BlockSpec((tm, D_MODEL), lambda i, j: (i, 0))
    vec = _const((1, D_MODEL))
    in_specs = [dz_spec] + [pl.BlockSpec((group,) + w.shape[1:], lambda i, j: (j, 0, 0)) for w in w_parts]
    in_specs += [row, row, vec, vec]
    out_specs = [row, vec, vec, vec]
    out_shape = [_sds((T, D_MODEL)), _sds((1, D_MODEL)), _sds((1, D_MODEL)), _sds((1, D_MODEL))]
    args = [dz, *w_parts, xin, dres, sc, g]
    if below is not None:
        in_specs += [row, vec, vec]
        out_specs += [row, vec, vec]
        out_shape += [_sds((T, D_MODEL), BF16), _sds((1, D_MODEL)), _sds((1, D_MODEL))]
        args += list(below)
    return _call(body, name=name, grid=(T // tm, steps), in_specs=in_specs, out_specs=out_specs,
                 out_shape=out_shape, sem=('arbitrary', 'arbitrary'), vmem=VMEM_MOST, ride=ride)(*args)


def _mix_bwd(d_o, w_out, yssm, proj, d, glu_w, glu_b, g_ssm, cw, g_conv, avg16, avg64, tm, ride):
    T = yssm.shape[0]
    hb = _halo_before(tm)

    def body(do_ref, wo_ref, y_ref, p_ref, ph_ref, d_ref, gw_ref, gb_ref, gs_ref, cw_ref, gc_ref, a16_ref, a64_ref,
             dy_ref, dconv_ref, dbg_ref, z_ref, dlin_ref, acc_ref):
        i = pl.program_id(0)
        dyc = _dot_nt(do_ref[...], wo_ref[...])

        @pl.when(i == 0)
        def _():
            acc_ref[...] = jnp.zeros_like(acc_ref)

        u = p_ref[:, 0:D_SSM]
        y = y_ref[...] + d_ref[...] * u
        z, t = _gelu(y)
        gate = _sigmoid(_dot(z.astype(BF16), gw_ref[...]) + gb_ref[...])
        ya = z * gate
        rs = lax.rsqrt(_dot_split(ya * ya, a16_ref[...], 2) + EPS)
        dna = dyc[:, 0:D_SSM]
        acc_ref[1:2, :] += _colsum(dna * ya * rs)
        dya = _head_norm_bwd(dna, ya, rs, gs_ref[...], a16_ref[...])
        dlin = dya * z * gate * (1.0 - gate)
        acc_ref[0:1, :] += _colsum(dlin)
        dlin_b = dlin.astype(BF16)
        dz = dya * gate + _dot_nt(dlin_b, gw_ref[...])
        dy = dz * _gelu_grad(y, t)
        acc_ref[3:4, :] += _colsum(dy * u)
        dy_ref[...] = dy
        z_ref[...] = z.astype(BF16)
        dlin_ref[...] = dlin_b

        bg = p_ref[:, D_SSM:D_SSM + D_CONV]
        cv = p_ref[:, D_SSM + D_CONV:D_SSM + 2 * D_CONV] * p_ref[:, D_SSM + 2 * D_CONV:D_IN_PROJ]
        hv = ph_ref[:, D_SSM + D_CONV:D_SSM + 2 * D_CONV] * ph_ref[:, D_SSM + 2 * D_CONV:D_IN_PROJ]
        hv = jnp.where(i > 0, hv, 0.0)
        conv, cv1, cv2 = _conv3(cv, hv, cw_ref)
        yb = bg * conv
        rsb = lax.rsqrt(_dot_split(yb * yb, a64_ref[...], 2) + EPS)
        dnb = dyc[:, D_SSM:D_MODEL]
        acc_ref[2:3, :] += _colsum(dnb * yb * rsb)
        dyb = _head_norm_bwd(dnb, yb, rsb, gc_ref[...], a64_ref[...])
        dbg_ref[...] = dyb * conv
        dconv = dyb * bg
        dconv_ref[...] = dconv
        acc_ref[4:5, :] += _colsum(dconv * cv2)
        acc_ref[5:6, :] += _colsum(dconv * cv1)
        acc_ref[6:7, :] += _colsum(dconv * cv)

    vec = _const((1, D_SSM))
    sq = _const((D_SSM, D_SSM))
    half = pl.BlockSpec((tm, D_SSM), lambda i: (i, 0))
    return _call(body, name='mix_bwd', grid=(T // tm,),
                 in_specs=[pl.BlockSpec((tm, D_MODEL), lambda i: (i, 0)), _const((D_MODEL, D_MODEL)), half,
                           pl.BlockSpec((tm, D_IN_PROJ), lambda i: (i, 0)),
                           pl.BlockSpec((HALO, D_IN_PROJ), lambda i: (hb(i), 0)), vec, sq, vec, vec,
                           _const((3, D_CONV)), vec, sq, sq],
                 out_specs=[half, half, half, half, half, _const((8, D_SSM))],
                 out_shape=[_sds((T, D_SSM)), _sds((T, D_SSM)), _sds((T, D_SSM)), _sds((T, D_SSM), BF16),
                            _sds((T, D_SSM), BF16), _sds((8, D_SSM))],
                 sem=('arbitrary',), vmem=VMEM_BIG, ride=ride)(d_o, w_out, yssm, proj, proj, d, glu_w, glu_b, g_ssm, cw,
                                                              g_conv, avg16, avg64)


def _mix_bwd_proj(dconv, proj, du_ssm, dy, d, dbg, cw, tm):
    T = dy.shape[0]
    nb = T // tm
    ha = _halo_after(tm, T)

    def body(dc_ref, dch_ref, cg_ref, v_ref, du_ref, dy_ref, d_ref, dbg_ref, cw_ref, o_ref):
        i = pl.program_id(0)
        dcv = _conv3_t(dc_ref[...], jnp.where(i < nb - 1, dch_ref[...], 0.0), cw_ref)[0]
        o_ref[:, 0:D_SSM] = (du_ref[...] + dy_ref[...] * d_ref[...]).astype(BF16)
        o_ref[:, D_SSM:D_SSM + D_CONV] = dbg_ref[...].astype(BF16)
        o_ref[:, D_SSM + D_CONV:D_SSM + 2 * D_CONV] = (dcv * v_ref[...]).astype(BF16)
        o_ref[:, D_SSM + 2 * D_CONV:D_IN_PROJ] = (dcv * cg_ref[...]).astype(BF16)

    half = pl.BlockSpec((tm, D_SSM), lambda i: (i, 0))
    return _call(body, name='mix_bwd_proj', grid=(nb,),
                 in_specs=[half, pl.BlockSpec((HALO, D_CONV), lambda i: (ha(i), 0)),
                           pl.BlockSpec((tm, D_CONV), lambda i: (i, 2)), pl.BlockSpec((tm, D_CONV), lambda i: (i, 3)),
                           half, half, _const((1, D_SSM)), half, _const((3, D_CONV))],
                 out_specs=pl.BlockSpec((tm, D_IN_PROJ), lambda i: (i, 0)), out_shape=_sds((T, D_IN_PROJ), BF16),
                 sem=('parallel',), vmem=VMEM_BIG)(dconv, dconv, proj, proj, du_ssm, dy, d, dbg, cw)


ADAMW_SLOT_BYTES = 8 << 20
ADAMW_ROW_BYTES = 3 << 19


def _row_tile(rows, cols, slots):
    for cand in range(rows, 15, -1):
        if (rows % cand == 0 and cand % 16 == 0 and slots * cand * cols * 4 <= ADAMW_SLOT_BYTES
                and cand * cols * 4 <= ADAMW_ROW_BYTES):
            return cand
    return rows


def _adamw_math(g, w, m, v):
    m2 = ADAM_B1 * m + (1.0 - ADAM_B1) * g
    v2 = ADAM_B2 * v + (1.0 - ADAM_B2) * (g * g)
    m_hat = m2 / (1.0 - ADAM_B1 ** ADAM_STEP)
    v_hat = v2 / (1.0 - ADAM_B2 ** ADAM_STEP)
    return -ADAM_LR * (m_hat / (jnp.sqrt(v_hat) + ADAM_EPS) + ADAM_WD * w), m2, v2


def _adamw(pieces, w, m, v, name):
    slots, _, cols = pieces[0].shape
    rows = sum(p.shape[1] for p in pieces)
    tr = _row_tile(pieces[0].shape[1], cols, slots)
    starts, pos = [], 0
    for p in pieces:
        assert p.shape[1] % tr == 0
        starts.append(pos)
        pos += p.shape[1] // tr

    def body(*refs):
        g_refs = refs[:len(pieces)]
        w_ref, m_ref, v_ref, go_ref, d_ref, mo_ref, vo_ref = refs[len(pieces):]
        i = pl.program_id(0)
        g = None
        for g_ref, start in zip(g_refs, starts):
            part = g_ref[0].astype(F32)
            for s in range(1, slots):
                part = part + g_ref[s].astype(F32)
            g = part if g is None else jnp.where(i >= start, part, g)
        go_ref[...] = g
        d_ref[...], mo_ref[...], vo_ref[...] = _adamw_math(g, w_ref[...], m_ref[...], v_ref[...])

    def piece_spec(start, count):
        return pl.BlockSpec((slots, tr, cols), lambda i: (0, jnp.clip(i - start, 0, count - 1), 0))

    blk = pl.BlockSpec((tr, cols), lambda i: (i, 0))
    return _call(body, name=name, grid=(rows // tr,),
                 in_specs=[piece_spec(s, p.shape[1] // tr) for s, p in zip(starts, pieces)] + [blk, blk, blk],
                 out_specs=[blk] * 4, out_shape=[_sds((rows, cols))] * 4, sem=('parallel',),
                 vmem=VMEM_BIG)(*pieces, w, m, v)


def _to_scan_rows(a):
    T, n = a.shape
    return a.reshape(SUBLANES, T // SUBLANES, n).transpose(1, 0, 2).reshape(T, n)


def _from_scan_rows(a):
    T, n = a.shape
    return a.reshape(T // SUBLANES, SUBLANES, n).transpose(1, 0, 2).reshape(T, n)


def _expand(a):
    return jnp.repeat(a, SSM_GROUP, axis=1)


def _block_diag(rows, row_group, col_group):
    r, n = rows.shape
    tiled = jnp.tile(rows, (1, N_GROUPS))
    keep = (jnp.arange(r)[:, None] // row_group) == (jnp.arange(n * N_GROUPS)[None, :] // col_group)
    return jnp.where(keep, tiled, 0.0)


def _block_diag_b(bb):
    return _block_diag(bb.transpose(0, 2, 1).reshape(D_SSM, SSM_STATE), SSM_GROUP, SSM_STATE)


def _block_diag_c(cc):
    return _block_diag(cc.transpose(0, 2, 1).reshape(N_STATE, SSM_GROUP), SSM_STATE, SSM_GROUP)


def _diag_blocks(x, chan_major):
    per = CHAN_BLOCK // SSM_GROUP
    eye = jnp.eye(per, dtype=x.dtype)
    if chan_major:
        x = x.reshape(-1, per, SSM_GROUP, per, SSM_STATE) * eye[None, :, None, :, None]
        return x.sum(axis=1).transpose(0, 2, 3, 1).reshape(N_GROUPS, SSM_STATE, SSM_GROUP)
    x = x.reshape(-1, per, SSM_STATE, per, SSM_GROUP) * eye[None, :, None, :, None]
    return x.sum(axis=3).reshape(N_GROUPS, SSM_STATE, SSM_GROUP)


SMALL_LAYOUT = {
    'ssm_b_re': (0, 0, 32, 1024), 'ssm_b_im': (32, 0, 32, 1024), 'ssm_c_re': (64, 0, 32, 1024),
    'ssm_c_im': (96, 0, 32, 1024), 'b_ada': (128, 0, 6, 1024), 'g_pre_mix': (134, 0, 1, 1024),
    'g_post_mix': (135, 0, 1, 1024), 'ssm_lam_re': (136, 0, 2, 1024), 'ssm_lam_im': (138, 0, 2, 1024),
    'ssm_log_step': (140, 0, 1, 32), 'glu_b': (141, 0, 1, 512), 'g_out_ssm': (141, 512, 1, 512),
    'g_out_conv': (142, 0, 1, 512), 'ssm_d': (142, 512, 1, 512), 'g_pre_ffn': (143, 0, 1, 1024),
    'g_post_ffn': (144, 0, 1, 1024)}
SMALL_ROWS = 152
B_ADA_ROW = SMALL_LAYOUT['b_ada'][0]
LATE_ROWS = {('b_ada', 0): 0, ('b_ada', 1): 1, ('g_pre_mix', 0): 2}


def _adamw_small(gathered, late, wts, mom_m, mom_v):
    names = list(SMALL_LAYOUT)
    n = len(names)

    def body(*refs):
        g_ref, late_ref, ins, outs = refs[0], refs[1], refs[2:2 + 3 * n], refs[2 + 3 * n:]
        for p, name in enumerate(names):
            r0, c0, rows, cols = SMALL_LAYOUT[name]
            pieces = [(0, rows)] if rows % 8 == 0 else [(r, 1) for r in range(rows)]
            for r, cnt in pieces:
                src_ref, first = (late_ref, LATE_ROWS[name, r]) if (name, r) in LATE_ROWS else (g_ref, r0 + r)
                g = src_ref[0, first:first + cnt, c0:c0 + cols]
                for s in range(1, N_DEV):
                    g = g + src_ref[s, first:first + cnt, c0:c0 + cols]
                w, m, v = (ins[3 * p + q][r:r + cnt, :] for q in range(3))
                res = (g,) + _adamw_math(g, w, m, v)
                for q in range(4):
                    outs[4 * p + q][r:r + cnt, :] = res[q]

    shapes = [SMALL_LAYOUT[name][2:] for name in names]
    args = [gathered, late]
    for name, shp in zip(names, shapes):
        args += [wts[name].reshape(shp), mom_m[name].reshape(shp), mom_v[name].reshape(shp)]
    outs = _call(body, name='adamw_small', grid=(1,),
                 in_specs=[_const(gathered.shape), _const(late.shape)]
                 + [_const(shp) for shp in shapes for _ in range(3)],
                 out_specs=[_const(shp) for shp in shapes for _ in range(4)],
                 out_shape=[_sds(shp) for shp in shapes for _ in range(4)], vmem=VMEM_BIG)(*args)
    res = {}
    for p, name in enumerate(names):
        for q, kind in enumerate(('g', 'd', 'm', 'v')):
            res[kind, name] = outs[4 * p + q].reshape(wts[name].shape)
    return res


def kernel(x, c, w_ada, b_ada, g_pre_mix, g_post_mix, w_in, ssm_lam_re, ssm_lam_im, ssm_log_step, ssm_b_re, ssm_b_im, ssm_c_re, ssm_c_im, ssm_d, glu_w, glu_b, g_out_ssm, conv_w, g_out_conv, w_out, g_pre_ffn, g_post_ffn, w_up, ffn_conv_w, w_down, loss_target, m_w_ada, m_b_ada, m_g_pre_mix, m_g_post_mix, m_w_in, m_ssm_lam_re, m_ssm_lam_im, m_ssm_log_step, m_ssm_b_re, m_ssm_b_im, m_ssm_c_re, m_ssm_c_im, m_ssm_d, m_glu_w, m_glu_b, m_g_out_ssm, m_conv_w, m_g_out_conv, m_w_out, m_g_pre_ffn, m_g_post_ffn, m_w_up, m_ffn_conv_w, m_w_down, v_w_ada, v_b_ada, v_g_pre_mix, v_g_post_mix, v_w_in, v_ssm_lam_re, v_ssm_lam_im, v_ssm_log_step, v_ssm_b_re, v_ssm_b_im, v_ssm_c_re, v_ssm_c_im, v_ssm_d, v_glu_w, v_glu_b, v_g_out_ssm, v_conv_w, v_g_out_conv, v_w_out, v_g_pre_ffn, v_g_post_ffn, v_w_up, v_ffn_conv_w, v_w_down):
    args = dict(locals())
    wts = {n: args[n] for n in WEIGHTS}
    mom_m = {n: args['m_' + n] for n in WEIGHTS}
    mom_v = {n: args['v_' + n] for n in WEIGHTS}
    T = x.shape[1]
    tm = min(512, T)
    tw = min(1024, T)
    tk = min(2048, T)
    me = _me()[3]
    xt, tgt = x[0], loss_target[0]

    c_all, w_in_s = _exchange([c, w_in[0].astype(BF16)], name='gather_first', scatter=False)
    c_all = c_all.reshape(N_DEV, D_MODEL)
    b_cols = lax.dynamic_slice(b_ada, (0, me * ADA_SHARD), (1, ADA_SHARD))
    mod_cols, c_act = _mod_cols(c_all, w_ada[0], b_cols)
    (mod_all,) = _exchange([mod_cols], name='gather_mod', scatter=False)
    mod = lax.dynamic_slice(mod_all, (0, me, 0), (N_DEV, 1, ADA_SHARD)).reshape(N_MOD, 1, D_MODEL)
    sh1, sc1, gt1, sh2, sc2, gt2 = [mod[k] for k in range(N_MOD)]


    lre_x, lim_x = _expand(ssm_lam_re[0]), _expand(ssm_lam_im[0])
    lst_x = jnp.broadcast_to(ssm_log_step[0][:, None], (N_GROUPS, SSM_STATE * SSM_GROUP))
    b_re_x = ssm_b_re[0].reshape(N_GROUPS, -1)
    b_im_x = ssm_b_im[0].reshape(N_GROUPS, -1)
    ar_x, ai_x, bbr_x, bbi_x = _ssm_prep(lre_x, lim_x, lst_x, b_re_x, b_im_x)
    lam_r = ar_x[:, ::SSM_GROUP].reshape(1, N_STATE)
    lam_i = ai_x[:, ::SSM_GROUP].reshape(1, N_STATE)
    big_b_re = _block_diag_b(bbr_x.reshape(N_GROUPS, SSM_STATE, SSM_GROUP)).astype(BF16)
    big_b_im = _block_diag_b(bbi_x.reshape(N_GROUPS, SSM_STATE, SSM_GROUP)).astype(BF16)
    big_c_re = _block_diag_c(ssm_c_re[0]).astype(BF16)
    big_c_im = _block_diag_c(ssm_c_im[0]).astype(BF16)
    head = jnp.arange(D_SSM)
    avg16 = jnp.where(head[:, None] // SSM_GROUP == head[None, :] // SSM_GROUP, 1.0 / SSM_GROUP, 0.0).astype(BF16)
    hd = D_CONV // CONV_HEADS
    avg64 = jnp.where(head[:, None] // hd == head[None, :] // hd, 1.0 / hd, 0.0).astype(BF16)

    w_up_t, half = w_up[0].T, D_MODEL // 2
    (proj, h1), (ffn_conv_s, conv_s, w_up_a) = _pre_mix(
        xt, sc1, sh1, g_pre_mix, w_in_s, tw, ([ffn_conv_w[0], conv_w[0], w_up_t[:, :half].astype(BF16)], False))
    cw_full = conv_s.transpose(1, 0, 2).reshape(3, D_CONV)
    u_perm = _to_scan_rows(proj[:, :D_SSM])
    (s_re, s_im, y_perm), (w_up_b, glu_s, w_out_s) = _ssm_fwd(
        u_perm, big_b_re, big_b_im, big_c_re, big_c_im, lam_r, lam_i,
        ([w_up_t[:, half:].astype(BF16), glu_w[0].astype(BF16), w_out[0].astype(BF16)], False))
    glu_full = glu_s.reshape(D_SSM, D_SSM)
    w_out_full = w_out_s.reshape(D_MODEL, D_MODEL)
    yssm = _from_scan_rows(y_perm)
    mix_args = (ssm_d, glu_full, glu_b, g_out_ssm, cw_full, g_out_conv, avg16, avg64)
    ycat = _mix_fwd(yssm, proj, *mix_args, tw)
    o, x1, h2 = _out_proj(ycat, w_out_full, xt, gt1, g_post_mix, g_pre_ffn, sc2, sh2, tw)
    (up8, hid8), (w_down_s,) = _ffn_up(h2, w_up_a, w_up_b, ffn_conv_s, tw, ([w_down[0].astype(BF16)], False))
    wd4 = w_down_s.reshape(4, FF_SHARD, D_MODEL)
    hid4 = hid8.reshape(2, 4, T, FF_SHARD)
    ddn, dx2, loss_parts, d_gt2, d_g_post_ffn = _ffn_down(hid4, wd4, x1, tgt, gt2, g_post_ffn, tm)
    loss_local = jnp.sum(loss_parts[:, 0, 0])

    got = {}
    dhid, g_w_down = _ffn_dact(ddn, wd4, hid4, tw)
    (dup8, dcw_ffn), (got['w_down'],) = _ffn_dup(dhid.reshape(N_DEV, T, FF_SHARD), up8, ffn_conv_s, tw,
                                                 ([g_w_down.reshape(N_DEV, D_FF // N_DEV, D_MODEL)], True))
    g_w_up_halves = _grad_tn(dup8, h2, pl.BlockSpec((None, T, FF_SHARD), lambda g, k: (g, k, 0)),
                             pl.BlockSpec((T, D_MODEL), lambda g, k: (k, 0)), N_DEV, FF_SHARD, D_MODEL, T,
                             'grad_w_up', parts=2)
    (dx1, d_sh2, d_sc2, d_g_pre_ffn, d_o, d_gt1, d_g_post_mix), (got_up_0, got['ffn_conv_w']) = _pre_norm_bwd(
        dup8, pl.BlockSpec((N_DEV, tm, FF_SHARD), lambda i, j: (j, i, 0)), [w_up_a, w_up_b], x1, dx2, sc2, g_pre_ffn,
        tm, 'ffn_in_bwd', ([g_w_up_halves[0], dcw_ffn], True), below=(o, gt1, g_post_mix), group=N_DEV, w_t=True)

    g_w_out = _grad_tn(ycat, d_o, pl.BlockSpec((tk, D_MODEL), lambda g, k: (k, 0)),
                       pl.BlockSpec((tk, D_MODEL), lambda g, k: (k, 0)), 1, D_MODEL, D_MODEL, tk, 'grad_w_out')
    (dy, dconv, dbg, z_b, dlin_b, sums), (got['w_out'],) = _mix_bwd(
        d_o, w_out_full, yssm, proj, *mix_args, tm, ([g_w_out.reshape(N_DEV, D_MODEL // N_DEV, D_MODEL)], True))
    g_glu_w = _grad_tn(z_b, dlin_b, pl.BlockSpec((tk, D_SSM), lambda g, k: (k, 0)),
                       pl.BlockSpec((tk, D_SSM), lambda g, k: (k, 0)), 1, D_SSM, D_SSM, tk, 'grad_glu_w')
    dy_perm = _to_scan_rows(dy)
    (du_perm, dbr_blk, dbi_blk, dcr_blk, dci_blk, dar_blk, dai_blk), (got_up_1, got['glu_w']) = _ssm_bwd(
        dy_perm, u_perm, s_re, s_im, big_b_re, big_b_im, big_c_re, big_c_im, lam_r, lam_i,
        ([g_w_up_halves[1], g_glu_w.reshape(N_DEV, D_SSM // N_DEV, D_SSM)], True))
    du_ssm = _from_scan_rows(du_perm)
    dproj = _mix_bwd_proj(dconv, proj, du_ssm, dy, ssm_d, dbg, cw_full, tw)
    dbb_re = _diag_blocks(dbr_blk, True).reshape(N_GROUPS, -1)
    dbb_im = _diag_blocks(dbi_blk, True).reshape(N_GROUPS, -1)
    d_c_re = _diag_blocks(dcr_blk, False).transpose(0, 2, 1)
    d_c_im = _diag_blocks(dci_blk, False).transpose(0, 2, 1)
    lane = jnp.arange(SSM_STATE * SSM_GROUP)
    seg = jnp.where(lane[:, None] // SSM_GROUP == lane[None, :] // SSM_GROUP, 1.0, 0.0).astype(BF16)
    d_b_re_x, d_b_im_x, d_lre_x, d_lim_x, d_lst = _ssm_prep_bwd(
        lre_x, lim_x, lst_x, b_re_x, b_im_x, dbb_re, dbb_im, _expand(dar_blk.reshape(N_GROUPS, SSM_STATE)),
        _expand(dai_blk.reshape(N_GROUPS, SSM_STATE)), seg)

    row = lambda a: a.reshape(-1, PACK_COLS)
    blank = jnp.zeros((1, PACK_COLS), F32)
    small_pack = jnp.concatenate([
        d_b_re_x, d_b_im_x, row(d_c_re), row(d_c_im), blank, blank, d_gt1, d_sh2, d_sc2, d_gt2, blank,
        d_g_post_mix, row(d_lre_x[:, ::SSM_GROUP]), row(d_lim_x[:, ::SSM_GROUP]),
        jnp.pad(d_lst.reshape(1, N_GROUPS), ((0, 0), (0, PACK_COLS - N_GROUPS))), row(sums[0:4]), d_g_pre_ffn,
        d_g_post_ffn, jnp.zeros((SMALL_ROWS - 145, PACK_COLS), F32)])
    g_w_in, (small_all,) = _grad_w_in(h1, dproj, tk, ([small_pack], False))
    g_conv_slots = jnp.concatenate([sums[4:7], jnp.zeros((5, D_CONV), F32)]).reshape(
        8, N_DEV, D_CONV // N_DEV).transpose(1, 0, 2)
    (grad_x, d_sh1, d_sc1, d_g_pre_mix), (got['w_in'], got['conv_w']) = _pre_norm_bwd(
        dproj, pl.BlockSpec((tw, 4 * IN_SHARD), lambda i, j: (i, j)), [w_in_s], xt, dx1, sc1, g_pre_mix, tw,
        'mix_in_bwd', ([g_w_in, g_conv_slots], True), group=4)
    late_pack = jnp.concatenate([d_sh1, d_sc1, d_g_pre_mix, jnp.full((1, PACK_COLS), loss_local, F32),
                                 jnp.zeros((4, PACK_COLS), F32)])
    (late_all,) = _exchange([late_pack], name='gather_late_grads', scatter=False)
    loss = jnp.sum(late_all[:, 3, 0])
    res = _adamw_small(small_all, late_all, wts, mom_m, mom_v)

    dmod_all = jnp.concatenate([late_all[:, 0:2, :], small_all[:, B_ADA_ROW + 2:B_ADA_ROW + N_MOD, :]],
                               axis=1).reshape(N_DEV, N_MOD * D_MODEL)
    dmod_cols = lax.dynamic_slice(dmod_all, (0, me * ADA_SHARD), (N_DEV, ADA_SHARD))
    g_w_ada = _grad_w_ada(c_act.T, dmod_cols)

    pieces = {n: [slots[:, :3, :] if n in ('conv_w', 'ffn_conv_w') else slots] for n, slots in got.items()}
    for n, parts in pieces.items():
        outs = _adamw(parts, wts[n][0], mom_m[n][0], mom_v[n][0], 'adamw_' + n)
        for kind, val in zip(('g', 'd', 'm', 'v'), outs):
            res[kind, n] = val[None]
    outs = _adamw([got_up_0, got_up_1], w_up[0].T, m_w_up[0].T, v_w_up[0].T, 'adamw_w_up')
    for kind, val in zip(('g', 'd', 'm', 'v'), outs):
        res[kind, 'w_up'] = val.T[None]
    outs = _adamw([g_w_ada[None]], w_ada[0], m_w_ada[0], v_w_ada[0], 'adamw_w_ada')
    for kind, val in zip(('g', 'd', 'm', 'v'), outs):
        res[kind, 'w_ada'] = val[None]

    return (loss, grad_x[None], *[res['g', n] for n in WEIGHTS], *[res['d', n] for n in WEIGHTS],
            *[res['m', n] for n in WEIGHTS], *[res['v', n] for n in WEIGHTS])
```

```python
import math

import jax
import jax.numpy as jnp
from jax import lax
from jax.experimental import pallas as pl
from jax.experimental.pallas import tpu as pltpu

F32, BF16 = jnp.float32, jnp.bfloat16

D_MODEL = 1024
D_SSM = 512
D_CONV = 512
SSM_GROUP = 16
N_GROUPS = 32
SSM_STATE = 64
N_STATE = N_GROUPS * SSM_STATE
CONV_HEADS = 8
D_FF = 2816
N_MOD = 6
D_IN_PROJ = D_SSM + 3 * D_CONV
N_DEV = 8
FF_SHARD = 2 * D_FF // N_DEV
IN_SHARD = D_IN_PROJ // N_DEV
ADA_SHARD = N_MOD * D_MODEL // N_DEV
EPS = 1e-6
LAMBDA_RE_MAX = -1e-4
ADAM_LR, ADAM_B1, ADAM_B2, ADAM_EPS, ADAM_WD, ADAM_STEP = 0.001, 0.9, 0.999, 1e-08, 0.01, 10
GELU_C = math.sqrt(2.0 / math.pi)
GELU_A = 0.044715

SUBLANES = 8
HALO = 8
HALO16 = 16
SCAN_UNROLL = 16
STATE_BLOCK = 512
CHAN_BLOCK = 128
VMEM_BIG = 48 << 20
VMEM_MOST = 58 << 20

WEIGHTS = ['w_ada', 'b_ada', 'g_pre_mix', 'g_post_mix', 'w_in', 'ssm_lam_re', 'ssm_lam_im', 'ssm_log_step',
           'ssm_b_re', 'ssm_b_im', 'ssm_c_re', 'ssm_c_im', 'ssm_d', 'glu_w', 'glu_b', 'g_out_ssm', 'conv_w',
           'g_out_conv', 'w_out', 'g_pre_ffn', 'g_post_ffn', 'w_up', 'ffn_conv_w', 'w_down']
PACK_COLS = 1024


def _call(body, *, name, grid, in_specs, out_specs, out_shape, scratch=(), sem=None, vmem=None, ride=None):
    params = {}
    if vmem is not None:
        params['vmem_limit_bytes'] = vmem
    if ride is None:
        if sem is not None:
            params['dimension_semantics'] = sem
        return pl.pallas_call(body, name=name, grid=grid, in_specs=in_specs, out_specs=out_specs,
                              out_shape=out_shape, scratch_shapes=list(scratch),
                              compiler_params=pltpu.CompilerParams(**params))
    arrs, scatter = ride
    single = not isinstance(out_shape, (list, tuple))
    out_shape_l = [out_shape] if single else list(out_shape)
    out_specs_l = [out_specs] if single else list(out_specs)
    n, n_in, n_out, n_scr = len(arrs), len(in_specs), len(out_shape_l), len(scratch)
    any_spec = pl.BlockSpec(memory_space=pl.ANY)
    params['dimension_semantics'] = ('arbitrary',) * len(grid)

    def carried(*refs):
        ins, rin = refs[:n_in], refs[n_in:n_in + n]
        outs, rout = refs[n_in + n:n_in + n + n_out], refs[n_in + n + n_out:n_in + 2 * n + n_out]
        scr, sems = refs[n_in + 2 * n + n_out:n_in + 2 * n + n_out + n_scr], refs[n_in + 2 * n + n_out + n_scr:]
        first = pl.program_id(0) == 0
        last = pl.program_id(0) == grid[0] - 1
        for ax in range(1, len(grid)):
            first = jnp.logical_and(first, pl.program_id(ax) == 0)
            last = jnp.logical_and(last, pl.program_id(ax) == grid[ax] - 1)

        @pl.when(first)
        def _():
            _exchange_start(rin, rout, sems, scatter)

        body(*ins, *outs, *scr)

        @pl.when(last)
        def _():
            _exchange_wait(rin, rout, sems, scatter)

    call = pl.pallas_call(carried, name=name, grid=grid, in_specs=list(in_specs) + [any_spec] * n,
                          out_specs=out_specs_l + [any_spec] * n,
                          out_shape=out_shape_l + _exchange_shapes(arrs, scatter),
                          scratch_shapes=list(scratch) + _exchange_sems(n),
                          compiler_params=pltpu.CompilerParams(**params))

    def run(*args):
        res = call(*args, *arrs)
        own = res[0] if single else list(res[:n_out])
        return own, list(res[n_out:])

    return run


def _const(shape):
    nd = len(shape)
    return pl.BlockSpec(shape, lambda *_: (0,) * nd)


def _sds(shape, dtype=F32):
    return jax.ShapeDtypeStruct(shape, dtype)


def _dot(a, b):
    return jnp.dot(a, b, preferred_element_type=F32)


def _dot_nt(a, b):
    return lax.dot_general(a, b, (((1,), (1,)), ((), ())), preferred_element_type=F32)


def _dot_tn(a, b):
    return lax.dot_general(a, b, (((0,), (0,)), ((), ())), preferred_element_type=F32)


def _dot_split(x, mat, parts):
    acc = None
    rem = x
    for _ in range(parts):
        piece = rem.astype(BF16)
        rem = rem - piece.astype(F32)
        term = _dot(piece, mat)
        acc = term if acc is None else acc + term
    return acc


def _sigmoid(x):
    return 1.0 / (1.0 + jnp.exp(-x))


def _gelu(x):
    t = jnp.tanh(GELU_C * (x + GELU_A * x * x * x))
    return 0.5 * x * (1.0 + t), t


def _gelu_grad(x, t):
    return 0.5 * (1.0 + t) + 0.5 * x * (1.0 - t * t) * GELU_C * (1.0 + 3.0 * GELU_A * x * x)


def _rsqrt_mean(x):
    return lax.rsqrt(jnp.mean(x * x, axis=-1, keepdims=True) + EPS)


def _colsum(x):
    return jnp.sum(x, axis=0, keepdims=True)


def _shifts_down(x, halo):
    ext = jnp.concatenate([halo, x], axis=0)
    return pltpu.roll(ext, 1, 0)[halo.shape[0]:], pltpu.roll(ext, 2, 0)[halo.shape[0]:]


def _shifts_up(x, halo):
    n = x.shape[0]
    ext = jnp.concatenate([x, halo], axis=0)
    total = ext.shape[0]
    return pltpu.roll(ext, total - 1, 0)[:n], pltpu.roll(ext, total - 2, 0)[:n]


def _conv3(x, halo, w_ref):
    x1, x2 = _shifts_down(x, halo)
    return w_ref[0:1, :] * x2 + w_ref[1:2, :] * x1 + w_ref[2:3, :] * x, x1, x2


def _conv3_t(g, halo, w_ref):
    g1, g2 = _shifts_up(g, halo)
    return w_ref[2:3, :] * g + w_ref[1:2, :] * g1 + w_ref[0:1, :] * g2, g1, g2


def _silu_parts(x):
    s = _sigmoid(x)
    return x * s, s * (1.0 + x * (1.0 - s))


def _norm_bwd(dn, x, r, g):
    gd = g * dn
    return r * gd - x * (r * r * r) * jnp.mean(gd * x, axis=-1, keepdims=True)


def _head_norm_bwd(dn, y, rs, g, avg):
    gd = g * dn
    return rs * gd - y * (rs * rs * rs) * _dot_split(gd * y, avg, 2)


def _me():
    x, y, c = lax.axis_index('x'), lax.axis_index('y'), lax.axis_index('c')
    return x, y, c, 4 * x + 2 * y + c


def _peer(k):
    x, y, c, _ = _me()
    px = 1 - x if k & 4 else x
    py = 1 - y if k & 2 else y
    pc = 1 - c if k & 1 else c
    return (px, py, pc), 4 * px + 2 * py + pc


SIBLING = 1
OTHER_CHIPS = (2, 4, 6)


def _remote(src, dst, sems, a, k, dev):
    return pltpu.make_async_remote_copy(src_ref=src, dst_ref=dst, send_sem=sems[0].at[a, k - 1],
                                        recv_sem=sems[1].at[a, k - 1], device_id=dev,
                                        device_id_type=pl.DeviceIdType.MESH)


def _exchange_copies(ins, outs, sems, scatter):
    me = _me()[3]
    local, first, relay, arrivals = [], [], [], []
    for a in range(len(ins)):
        src = ins[a].at[me] if scatter else ins[a]
        local.append(pltpu.make_async_copy(src, outs[a].at[me], sems[2].at[a]))
        for k in range(1, N_DEV):
            dev, idx = _peer(k)
            landed = _remote(src, outs[a].at[idx], sems, a, k, dev)
            if scatter:
                first.append(_remote(ins[a].at[idx], outs[a].at[me], sems, a, k, dev))
                arrivals.append(landed)
            elif k == SIBLING:
                first.append(_remote(src, outs[a].at[me], sems, a, k, dev))
                arrivals.append(landed)
            elif k in OTHER_CHIPS:
                first.append(_remote(src, outs[a].at[me], sems, a, k, dev))
                sib, _ = _peer(SIBLING)
                relay.append((landed, _remote(outs[a].at[idx], outs[a].at[idx], sems, a, k | SIBLING, sib)))
            else:
                arrivals.append(landed)
    return local, first, relay, arrivals


def _exchange_start(ins, outs, sems, scatter):
    local, first, _, _ = _exchange_copies(ins, outs, sems, scatter)
    for cp in local + first:
        cp.start()


def _exchange_wait(ins, outs, sems, scatter):
    local, first, relay, arrivals = _exchange_copies(ins, outs, sems, scatter)
    for landed, forward in relay:
        landed.wait_recv()
        forward.start()
    for cp in arrivals:
        cp.wait_recv()
    for cp in first + [forward for _, forward in relay]:
        cp.wait_send()
    for cp in local:
        cp.wait()


def _exchange_shapes(arrs, scatter):
    return [_sds(a.shape if scatter else (N_DEV,) + a.shape, a.dtype) for a in arrs]


def _exchange_sems(n):
    return [pltpu.SemaphoreType.DMA((n, N_DEV - 1)), pltpu.SemaphoreType.DMA((n, N_DEV - 1)),
            pltpu.SemaphoreType.DMA((n,))]


def _exchange(arrs, *, name, scatter):
    n = len(arrs)

    def body(*refs):
        _exchange_start(refs[:n], refs[n:2 * n], refs[2 * n:], scatter)
        _exchange_wait(refs[:n], refs[n:2 * n], refs[2 * n:], scatter)

    any_spec = pl.BlockSpec(memory_space=pl.ANY)
    outs = pl.pallas_call(body, name=name, out_shape=_exchange_shapes(arrs, scatter), in_specs=[any_spec] * n,
                          out_specs=[any_spec] * n, scratch_shapes=_exchange_sems(n))(*arrs)
    return list(outs)


def _mod_cols(c_all, w_ada, b_cols):
    def body(c_ref, w_ref, b_ref, mod_ref, act_ref):
        c = c_ref[...]
        act = c * _sigmoid(c)
        act_ref[...] = act
        mod_ref[...] = _dot(act.astype(BF16), w_ref[...].astype(BF16)) + b_ref[...]

    return _call(body, name='mod_cols', grid=(1,),
                 in_specs=[_const(c_all.shape), _const(w_ada.shape), _const(b_cols.shape)],
                 out_specs=[_const((N_DEV, ADA_SHARD)), _const(c_all.shape)],
                 out_shape=[_sds((N_DEV, ADA_SHARD)), _sds(c_all.shape)], vmem=VMEM_BIG)(c_all, w_ada, b_cols)


def _grad_w_ada(act_t, dmod_cols):
    def body(a_ref, d_ref, o_ref):
        o_ref[...] = _dot(a_ref[...], d_ref[...])

    return _call(body, name='grad_w_ada', grid=(1,), in_specs=[_const(act_t.shape), _const(dmod_cols.shape)],
                 out_specs=_const((D_MODEL, ADA_SHARD)), out_shape=_sds((D_MODEL, ADA_SHARD)),
                 vmem=VMEM_BIG)(act_t, dmod_cols)


def _pre_mix(x, sc, sh, g, w_s, tm, ride):
    T = x.shape[0]
    group = 4

    def body(x_ref, sc_ref, sh_ref, g_ref, w_ref, proj_ref, h_ref):
        @pl.when(pl.program_id(1) == 0)
        def _():
            xv = x_ref[...]
            h_ref[...] = ((xv * _rsqrt_mean(xv) * g_ref[...]) * (1.0 + sc_ref[...]) + sh_ref[...]).astype(BF16)

        for s in range(group):
            proj_ref[:, s * IN_SHARD:(s + 1) * IN_SHARD] = _dot(h_ref[...], w_ref[s])

    row = pl.BlockSpec((tm, D_MODEL), lambda i, j: (i, 0))
    vec = _const((1, D_MODEL))
    return _call(body, name='pre_mix', grid=(T // tm, N_DEV // group),
                 in_specs=[row, vec, vec, vec, pl.BlockSpec((group, D_MODEL, IN_SHARD), lambda i, j: (j, 0, 0))],
                 out_specs=[pl.BlockSpec((tm, group * IN_SHARD), lambda i, j: (i, j)), row],
                 out_shape=[_sds((T, D_IN_PROJ)), _sds((T, D_MODEL), BF16)],
                 sem=('parallel', 'arbitrary'), ride=ride)(x, sc, sh, g, w_s)


def _halo_before(tm, rows=HALO):
    return lambda i: jnp.maximum(i * (tm // rows) - 1, 0)


def _halo_after(tm, T, rows=HALO):
    return lambda i: jnp.minimum((i + 1) * (tm // rows), T // rows - 1)


def _mix_fwd(yssm, proj, d, glu_w, glu_b, g_ssm, cw, g_conv, avg16, avg64, tm):
    T = yssm.shape[0]
    hb = _halo_before(tm)

    def body(y_ref, p_ref, ph_ref, d_ref, gw_ref, gb_ref, gs_ref, cw_ref, gc_ref, a16_ref, a64_ref, o_ref):
        i = pl.program_id(0)
        u = p_ref[:, 0:D_SSM]
        y = y_ref[...] + d_ref[...] * u
        z, _ = _gelu(y)
        gate = _sigmoid(_dot(z.astype(BF16), gw_ref[...]) + gb_ref[...])
        ya = z * gate
        rs = lax.rsqrt(_dot_split(ya * ya, a16_ref[...], 2) + EPS)
        o_ref[:, 0:D_SSM] = (ya * rs * gs_ref[...]).astype(BF16)
        bg = p_ref[:, D_SSM:D_SSM + D_CONV]
        cv = p_ref[:, D_SSM + D_CONV:D_SSM + 2 * D_CONV] * p_ref[:, D_SSM + 2 * D_CONV:D_IN_PROJ]
        hv = ph_ref[:, D_SSM + D_CONV:D_SSM + 2 * D_CONV] * ph_ref[:, D_SSM + 2 * D_CONV:D_IN_PROJ]
        hv = jnp.where(i > 0, hv, 0.0)
        conv, _, _ = _conv3(cv, hv, cw_ref)
        yb = bg * conv
        rsb = lax.rsqrt(_dot_split(yb * yb, a64_ref[...], 2) + EPS)
        o_ref[:, D_SSM:D_MODEL] = (yb * rsb * gc_ref[...]).astype(BF16)

    vec = _const((1, D_SSM))
    sq = _const((D_SSM, D_SSM))
    return _call(body, name='mix_fwd', grid=(T // tm,),
                 in_specs=[pl.BlockSpec((tm, D_SSM), lambda i: (i, 0)), pl.BlockSpec((tm, D_IN_PROJ), lambda i: (i, 0)),
                           pl.BlockSpec((HALO, D_IN_PROJ), lambda i: (hb(i), 0)), vec, sq, vec, vec,
                           _const((3, D_CONV)), vec, sq, sq],
                 out_specs=pl.BlockSpec((tm, D_MODEL), lambda i: (i, 0)), out_shape=_sds((T, D_MODEL), BF16),
                 sem=('parallel',), vmem=VMEM_BIG)(yssm, proj, proj, d, glu_w, glu_b, g_ssm, cw, g_conv, avg16, avg64)


def _out_proj(ycat, w_out, x, gt, g_post, g_pre, sc, sh, tm):
    T = x.shape[0]

    def body(y_ref, w_ref, x_ref, gt_ref, gp_ref, g2_ref, sc_ref, sh_ref, o_ref, x1_ref, h_ref):
        o = _dot(y_ref[...], w_ref[...])
        o_ref[...] = o.astype(BF16)
        x1 = x_ref[...] + gt_ref[...] * (o * _rsqrt_mean(o) * gp_ref[...])
        x1_ref[...] = x1
        h_ref[...] = ((x1 * _rsqrt_mean(x1) * g2_ref[...]) * (1.0 + sc_ref[...]) + sh_ref[...]).astype(BF16)

    row = pl.BlockSpec((tm, D_MODEL), lambda i: (i, 0))
    vec = _const((1, D_MODEL))
    return _call(body, name='out_proj', grid=(T // tm,),
                 in_specs=[row, _const((D_MODEL, D_MODEL)), row, vec, vec, vec, vec, vec],
                 out_specs=[row, row, row],
                 out_shape=[_sds((T, D_MODEL), BF16), _sds((T, D_MODEL)), _sds((T, D_MODEL), BF16)],
                 sem=('parallel',), vmem=VMEM_BIG)(ycat, w_out, x, gt, g_post, g_pre, sc, sh)


def _ffn_up(h2, w_a, w_b, cw8, tm, ride, group=N_DEV):
    T = h2.shape[0]
    hb = _halo_before(tm, HALO16)
    half = D_MODEL // 2

    def body(h_ref, hh_ref, wa_ref, wb_ref, cw_ref, up_ref, hid_ref):
        def times_w(ref, s):
            return _dot_nt(ref[:, :half], wa_ref[s]) + _dot_nt(ref[:, half:], wb_ref[s])

        for s in range(group):
            up = times_w(h_ref, s)
            up_ref[s] = up.astype(BF16)
            before = jnp.where(pl.program_id(0) > 0, times_w(hh_ref, s), 0.0)
            hid_ref[s] = _conv3(up, before, cw_ref.at[s])[0].astype(BF16)

    out = pl.BlockSpec((group, tm, FF_SHARD), lambda i, j: (j, i, 0))
    return _call(body, name='ffn_up', grid=(T // tm, N_DEV // group),
                 in_specs=[pl.BlockSpec((tm, D_MODEL), lambda i, j: (i, 0)),
                           pl.BlockSpec((HALO16, D_MODEL), lambda i, j: (hb(i), 0)),
                           pl.BlockSpec((group, FF_SHARD, half), lambda i, j: (j, 0, 0)),
                           pl.BlockSpec((group, FF_SHARD, half), lambda i, j: (j, 0, 0)),
                           pl.BlockSpec((group, 3, FF_SHARD), lambda i, j: (j, 0, 0))],
                 out_specs=[out, out], out_shape=[_sds((N_DEV, T, FF_SHARD), BF16)] * 2,
                 sem=('parallel', 'parallel'), vmem=VMEM_MOST, ride=ride)(h2, h2, w_a, w_b, cw8)


def _ffn_down(hid4, wd4, x1, tgt, gt, g_post, tm):
    T = x1.shape[0]
    nb = T // tm

    def body(a_ref, w_ref, x1_ref, t_ref, gt_ref, g_ref, ddn_ref, dx_ref, loss_ref, dgt_ref, dg_ref, dn_ref):
        i, j = pl.program_id(0), pl.program_id(1)
        part = None
        for s in range(4):
            act = (_silu_parts(a_ref[0, s].astype(F32))[0] * a_ref[1, s].astype(F32)).astype(BF16)
            term = _dot(act, w_ref[s])
            part = term if part is None else part + term

        @pl.when(jnp.logical_and(i == 0, j == 0))
        def _():
            dgt_ref[...] = jnp.zeros_like(dgt_ref)
            dg_ref[...] = jnp.zeros_like(dg_ref)

        @pl.when(j == 0)
        def _():
            dn_ref[...] = part

        @pl.when(j > 0)
        def _():
            dn_ref[...] += part

        @pl.when(j == 0)
        def _():
            dn, gv, gate = dn_ref[...], g_ref[...], gt_ref[...]
            r = _rsqrt_mean(dn)
            normed = dn * r * gv
            err = x1_ref[...] + gate * normed - t_ref[...]
            dx = err * (1.0 / D_MODEL)
            dx_ref[...] = dx
            tot = jnp.sum(jnp.sum(err * err, axis=1, keepdims=True), axis=0, keepdims=True) * (0.5 / D_MODEL)
            loss_ref[...] = jnp.broadcast_to(tot, (8, 128))
            dgt_ref[...] += _colsum(dx * normed)
            dnn = dx * gate
            dg_ref[...] += _colsum(dnn * dn * r)
            ddn_ref[...] = _norm_bwd(dnn, dn, r, gv).astype(BF16)

    row = pl.BlockSpec((tm, D_MODEL), lambda i, j: (i, 0))
    vec = _const((1, D_MODEL))
    return _call(body, name='ffn_down', grid=(nb, 1),
                 in_specs=[pl.BlockSpec((2, 4, tm, FF_SHARD), lambda i, j: (0, j, i, 0)),
                           pl.BlockSpec((4, FF_SHARD, D_MODEL), lambda i, j: (j, 0, 0)), row, row, vec, vec],
                 out_specs=[row, row, pl.BlockSpec((None, 8, 128), lambda i, j: (i, 0, 0)), vec, vec],
                 out_shape=[_sds((T, D_MODEL), BF16), _sds((T, D_MODEL)), _sds((nb, 8, 128)), _sds((1, D_MODEL)),
                            _sds((1, D_MODEL))],
                 scratch=[pltpu.VMEM((tm, D_MODEL), F32)], sem=('arbitrary', 'arbitrary'),
                 vmem=VMEM_MOST)(hid4, wd4, x1, tgt, gt, g_post)


def _ssm_prep(lre, lim, lst, b_re, b_im):
    def body(lre_ref, lim_ref, lst_ref, br_ref, bi_ref, ar_ref, ai_ref, bbr_ref, bbi_ref):
        ar, ai, qr, qi = _zoh(lre_ref[...], lim_ref[...], lst_ref[...])[:4]
        ar_ref[...] = ar
        ai_ref[...] = ai
        bbr_ref[...] = qr * br_ref[...] - qi * bi_ref[...]
        bbi_ref[...] = qr * bi_ref[...] + qi * br_ref[...]

    shp = lre.shape
    return _call(body, name='ssm_prep', grid=(1,), in_specs=[_const(shp)] * 5, out_specs=[_const(shp)] * 4,
                 out_shape=[_sds(shp)] * 4)(lre, lim, lst, b_re, b_im)


def _zoh(lre, lim, lst):
    lr = jnp.minimum(lre, LAMBDA_RE_MAX)
    st = jnp.exp(lst)
    mag = jnp.exp(lr * st)
    ar = mag * jnp.cos(lim * st)
    ai = mag * jnp.sin(lim * st)
    den = lr * lr + lim * lim
    qr = ((ar - 1.0) * lr + ai * lim) / den
    qi = (ai * lr - (ar - 1.0) * lim) / den
    return ar, ai, qr, qi, lr, st, den


def _ssm_prep_bwd(lre, lim, lst, b_re, b_im, dbbr, dbbi, dar, dai, seg):
    def body(lre_ref, lim_ref, lst_ref, br_ref, bi_ref, dbbr_ref, dbbi_ref, dar_ref, dai_ref, seg_ref,
             dbr_ref, dbi_ref, dlre_ref, dlim_ref, dlst_ref):
        lre_v = lre_ref[...]
        li = lim_ref[...]
        ar, ai, qr, qi, lr, st, den = _zoh(lre_v, li, lst_ref[...])
        br, bi, gbr, gbi = br_ref[...], bi_ref[...], dbbr_ref[...], dbbi_ref[...]
        dbr_ref[...] = qr * gbr + qi * gbi
        dbi_ref[...] = qr * gbi - qi * gbr
        gqr = _dot_split(br * gbr + bi * gbi, seg_ref[...], 3)
        gqi = _dot_split(br * gbi - bi * gbr, seg_ref[...], 3)
        ir, ii = lr / den, -li / den
        gar = dar_ref[...] + ir * gqr + ii * gqi
        gai = dai_ref[...] + ir * gqi - ii * gqr
        tr, ti = qr * ir - qi * ii, qr * ii + qi * ir
        glr = -(tr * gqr + ti * gqi)
        gli = -(tr * gqi - ti * gqr)
        gzr = ar * gar + ai * gai
        gzi = ar * gai - ai * gar
        glr = glr + st * gzr
        gli = gli + st * gzi
        gst = (lr * gzr + li * gzi) * st
        dlre_ref[...] = jnp.where(lre_v < LAMBDA_RE_MAX, glr, 0.0)
        dlim_ref[...] = gli
        dlst_ref[...] = jnp.sum(gst, axis=1, keepdims=True) * (1.0 / SSM_GROUP)

    shp = lre.shape
    return _call(body, name='ssm_prep_bwd', grid=(1,), in_specs=[_const(shp)] * 9 + [_const(seg.shape)],
                 out_specs=[_const(shp)] * 4 + [_const((N_GROUPS, 1))],
                 out_shape=[_sds(shp)] * 4 + [_sds((N_GROUPS, 1))], vmem=VMEM_BIG)(
                     lre, lim, lst, b_re, b_im, dbbr, dbbi, dar, dai, seg)


def _scan_specs(T):
    return dict(
        chan=pl.BlockSpec((T, CHAN_BLOCK), lambda cb: (0, cb)),
        state=pl.BlockSpec((T, STATE_BLOCK), lambda cb: (0, cb)),
        b=pl.BlockSpec((CHAN_BLOCK, STATE_BLOCK), lambda cb: (cb, cb)),
        c=pl.BlockSpec((STATE_BLOCK, CHAN_BLOCK), lambda cb: (cb, cb)),
        lam=pl.BlockSpec((1, STATE_BLOCK), lambda cb: (0, cb)),
    )


def _complex_power(re, im, n):
    out = None
    while True:
        if n & 1:
            out = (re, im) if out is None else (out[0] * re - out[1] * im, out[0] * im + out[1] * re)
        n >>= 1
        if n == 0:
            return out
        re, im = re * re - im * im, 2.0 * re * im


def _rows8(i):
    if isinstance(i, int):
        return pl.ds(i * SUBLANES, SUBLANES)
    return pl.ds(pl.multiple_of(i * SUBLANES, SUBLANES), SUBLANES)


def _scan_loop(n_steps, body, init):
    trips = n_steps // SCAN_UNROLL

    def trip(t, carry):
        for u in range(SCAN_UNROLL):
            carry = body(t * SCAN_UNROLL + u, carry)
        return carry

    carry = lax.fori_loop(0, trips, trip, init)
    for step in range(trips * SCAN_UNROLL, n_steps):
        carry = body(step, carry)
    return carry


def _ssm_fwd(u_perm, b_re, b_im, c_re, c_im, lam_r, lam_i, ride):
    T = u_perm.shape[0]
    ls = T // SUBLANES
    rc = min(1024, T)
    sp = _scan_specs(T)

    def body(u_ref, bre_ref, bim_ref, cre_ref, cim_ref, lr_ref, li_ref, so_re_ref, so_im_ref, y_ref, sre_ref, sim_ref):
        for c in range(T // rc):
            rows = pl.ds(c * rc, rc)
            ub = u_ref[rows, :].astype(BF16)
            sre_ref[rows, :] = _dot(ub, bre_ref[...])
            sim_ref[rows, :] = _dot(ub, bim_ref[...])
        shp = (SUBLANES, STATE_BLOCK)
        lr = jnp.broadcast_to(lr_ref[...], shp)
        li = jnp.broadcast_to(li_ref[...], shp)
        zero = jnp.zeros(shp, F32)

        def step(i, carry):
            sr, si = carry
            rows = _rows8(i)
            nr = lr * sr - li * si + sre_ref[rows, :]
            ni = lr * si + li * sr + sim_ref[rows, :]
            sre_ref[rows, :] = nr
            sim_ref[rows, :] = ni
            return nr, ni

        fr, fi = _scan_loop(ls, step, (zero, zero))
        pr, pi_ = _complex_power(lr, li, ls)
        row = lax.broadcasted_iota(jnp.int32, shp, 0)
        ir, ii = zero, zero
        for _ in range(SUBLANES - 1):
            er = fr + pr * ir - pi_ * ii
            ei = fi + pr * ii + pi_ * ir
            ir = jnp.where(row == 0, 0.0, pltpu.roll(er, 1, 0))
            ii = jnp.where(row == 0, 0.0, pltpu.roll(ei, 1, 0))

        def fix(i, carry):
            cr, ci = carry
            rows = _rows8(i)
            nr = lr * cr - li * ci
            ni = lr * ci + li * cr
            sre_ref[rows, :] += nr
            sim_ref[rows, :] += ni
            return nr, ni

        _scan_loop(ls, fix, (ir, ii))
        for c in range(T // rc):
            rows = pl.ds(c * rc, rc)
            s_r, s_i = sre_ref[rows, :].astype(BF16), sim_ref[rows, :].astype(BF16)
            so_re_ref[rows, :] = s_r
            so_im_ref[rows, :] = s_i
            y_ref[rows, :] = _dot(s_r, cre_ref[...]) - _dot(s_i, cim_ref[...])

    return _call(body, name='ssm_fwd', grid=(N_STATE // STATE_BLOCK,),
                 in_specs=[sp['chan'], sp['b'], sp['b'], sp['c'], sp['c'], sp['lam'], sp['lam']],
                 out_specs=[sp['state'], sp['state'], sp['chan']],
                 out_shape=[_sds((T, N_STATE), BF16), _sds((T, N_STATE), BF16), _sds((T, D_SSM))],
                 scratch=[pltpu.VMEM((T, STATE_BLOCK), F32), pltpu.VMEM((T, STATE_BLOCK), F32)],
                 sem=('arbitrary',), vmem=VMEM_MOST, ride=ride)(u_perm, b_re, b_im, c_re, c_im, lam_r, lam_i)


def _ssm_bwd(dy_perm, u_perm, s_re, s_im, b_re, b_im, c_re, c_im, lam_r, lam_i, ride):
    T = u_perm.shape[0]
    ls = T // SUBLANES
    rc = min(1024, T)
    sp = _scan_specs(T)
    ncb = N_STATE // STATE_BLOCK

    def body(dy_ref, u_ref, sre_ref, sim_ref, bre_ref, bim_ref, cre_ref, cim_ref, lr_ref, li_ref,
             du_ref, dbr_ref, dbi_ref, dcr_ref, dci_ref, dar_ref, dai_ref, gre_ref, gim_ref):
        shp = (SUBLANES, STATE_BLOCK)
        zero = jnp.zeros(shp, F32)
        tail = pl.ds(T, SUBLANES)
        gre_ref[tail, :] = zero
        gim_ref[tail, :] = zero
        for c in range(T // rc):
            rows = pl.ds(c * rc, rc)
            dyb = dy_ref[rows, :].astype(BF16)
            gre_ref[rows, :] = _dot_nt(dyb, cre_ref[...])
            gim_ref[rows, :] = -_dot_nt(dyb, cim_ref[...])
        lr = jnp.broadcast_to(lr_ref[...], shp)
        li = jnp.broadcast_to(li_ref[...], shp)

        def step(k, carry):
            gr, gi = carry
            rows = _rows8(ls - 1 - k)
            nr = lr * gr + li * gi + gre_ref[rows, :]
            ni = lr * gi - li * gr + gim_ref[rows, :]
            gre_ref[rows, :] = nr
            gim_ref[rows, :] = ni
            return nr, ni

        fr, fi = _scan_loop(ls, step, (zero, zero))
        pr, pi_ = _complex_power(lr, -li, ls)
        row = lax.broadcasted_iota(jnp.int32, shp, 0)
        cr, ci = zero, zero
        for _ in range(SUBLANES - 1):
            er = fr + pr * cr - pi_ * ci
            ei = fi + pr * ci + pi_ * cr
            cr = jnp.where(row == SUBLANES - 1, 0.0, pltpu.roll(er, SUBLANES - 1, 0))
            ci = jnp.where(row == SUBLANES - 1, 0.0, pltpu.roll(ei, SUBLANES - 1, 0))

        def fix(k, carry):
            dr, di = carry
            rows = _rows8(ls - 1 - k)
            dr, di = lr * dr + li * di, lr * di - li * dr
            gre_ref[rows, :] += dr
            gim_ref[rows, :] += di
            return dr, di

        _scan_loop(ls, fix, (cr, ci))

        acc_r = jnp.zeros((1, STATE_BLOCK), F32)
        acc_i = jnp.zeros((1, STATE_BLOCK), F32)
        for c in range(T // rc):
            rows, nxt = pl.ds(c * rc, rc), pl.ds(c * rc + SUBLANES, rc)
            s_r, s_i = sre_ref[rows, :].astype(F32), sim_ref[rows, :].astype(F32)
            g_r, g_i = gre_ref[nxt, :], gim_ref[nxt, :]
            acc_r = acc_r + _colsum(g_r * s_r + g_i * s_i)
            acc_i = acc_i + _colsum(g_i * s_r - g_r * s_i)
        last = pl.ds(T - 2 * SUBLANES, 2 * SUBLANES)
        first = pl.ds(0, SUBLANES)
        spr = jnp.where(row == 0, 0.0, pltpu.roll(sre_ref[last, :].astype(F32)[SUBLANES:], 1, 0))
        spi = jnp.where(row == 0, 0.0, pltpu.roll(sim_ref[last, :].astype(F32)[SUBLANES:], 1, 0))
        gr, gi = gre_ref[first, :], gim_ref[first, :]
        dar_ref[...] = acc_r + _colsum(gr * spr + gi * spi)
        dai_ref[...] = acc_i + _colsum(gi * spr - gr * spi)

        for c in range(T // rc):
            rows = pl.ds(c * rc, rc)
            g_r, g_i = gre_ref[rows, :].astype(BF16), gim_ref[rows, :].astype(BF16)
            s_r, s_i = sre_ref[rows, :], sim_ref[rows, :]
            ub, dyb = u_ref[rows, :].astype(BF16), dy_ref[rows, :].astype(BF16)
            du_ref[rows, :] = _dot_nt(g_r, bre_ref[...]) + _dot_nt(g_i, bim_ref[...])
            parts = (_dot_tn(ub, g_r), _dot_tn(ub, g_i), _dot_tn(s_r, dyb), -_dot_tn(s_i, dyb))
            outs = (dbr_ref, dbi_ref, dcr_ref, dci_ref)
            for o_ref, part in zip(outs, parts):
                if c == 0:
                    o_ref[...] = part
                else:
                    o_ref[...] += part

    blk = lambda r, c: pl.BlockSpec((None, r, c), lambda cb: (cb, 0, 0))
    return _call(body, name='ssm_bwd', grid=(ncb,),
                 in_specs=[sp['chan'], sp['chan'], sp['state'], sp['state'], sp['b'], sp['b'], sp['c'], sp['c'],
                           sp['lam'], sp['lam']],
                 out_specs=[sp['chan'], blk(CHAN_BLOCK, STATE_BLOCK), blk(CHAN_BLOCK, STATE_BLOCK),
                            blk(STATE_BLOCK, CHAN_BLOCK), blk(STATE_BLOCK, CHAN_BLOCK), blk(1, STATE_BLOCK),
                            blk(1, STATE_BLOCK)],
                 out_shape=[_sds((T, D_SSM)), _sds((ncb, CHAN_BLOCK, STATE_BLOCK)), _sds((ncb, CHAN_BLOCK, STATE_BLOCK)),
                            _sds((ncb, STATE_BLOCK, CHAN_BLOCK)), _sds((ncb, STATE_BLOCK, CHAN_BLOCK)),
                            _sds((ncb, 1, STATE_BLOCK)), _sds((ncb, 1, STATE_BLOCK))],
                 scratch=[pltpu.VMEM((T + SUBLANES, STATE_BLOCK), F32), pltpu.VMEM((T + SUBLANES, STATE_BLOCK), F32)],
                 sem=('arbitrary',), vmem=VMEM_MOST, ride=ride)(dy_perm, u_perm, s_re, s_im, b_re, b_im, c_re, c_im,
                                                                lam_r, lam_i)


def _ffn_dact(ddn, wd4, hid4, tm):
    T = ddn.shape[0]
    nb = T // tm

    def body(d_ref, w_ref, hid_ref, o_ref, gw_ref, acc_ref):
        i = pl.program_id(1)
        d = d_ref[...]
        dact = _dot_nt(d, w_ref[...])
        silu, dsilu = _silu_parts(hid_ref[0].astype(F32))
        hid_v = hid_ref[1].astype(F32)
        o_ref[0] = (dact * hid_v * dsilu).astype(BF16)
        o_ref[1] = (dact * silu).astype(BF16)
        part = _dot_tn((silu * hid_v).astype(BF16), d)

        @pl.when(i == 0)
        def _():
            acc_ref[...] = part

        @pl.when(i > 0)
        def _():
            acc_ref[...] += part

        @pl.when(i == nb - 1)
        def _():
            gw_ref[...] = acc_ref[...].astype(BF16)

    blk = pl.BlockSpec((2, None, tm, FF_SHARD), lambda j, i: (0, j, i, 0))
    w_blk = pl.BlockSpec((None, FF_SHARD, D_MODEL), lambda j, i: (j, 0, 0))
    return _call(body, name='ffn_dact', grid=(4, nb),
                 in_specs=[pl.BlockSpec((tm, D_MODEL), lambda j, i: (i, 0)), w_blk, blk],
                 out_specs=[blk, w_blk],
                 out_shape=[_sds((2, 4, T, FF_SHARD), BF16), _sds((4, FF_SHARD, D_MODEL), BF16)],
                 scratch=[pltpu.VMEM((FF_SHARD, D_MODEL), F32)], sem=('parallel', 'arbitrary'),
                 vmem=VMEM_BIG)(ddn, wd4, hid4)


def _ffn_dup(dhid8, up8, cw8, tm, ride):
    T = up8.shape[1]
    nb = T // tm
    ha = _halo_after(tm, T, HALO16)

    def body(dh_ref, dha_ref, up_ref, cw_ref, dup_ref, dcw_ref):
        i = pl.program_id(1)

        @pl.when(i == 0)
        def _():
            dcw_ref[...] = jnp.zeros_like(dcw_ref)

        dh = dh_ref[...].astype(F32)
        dup, dh1, dh2 = _conv3_t(dh, jnp.where(i < nb - 1, dha_ref[...].astype(F32), 0.0), cw_ref)
        dup_ref[...] = dup.astype(BF16)
        up = up_ref[...].astype(F32)
        dcw_ref[0:1, :] += _colsum(dh2 * up)
        dcw_ref[1:2, :] += _colsum(dh1 * up)
        dcw_ref[2:3, :] += _colsum(dh * up)

    main = pl.BlockSpec((None, tm, FF_SHARD), lambda j, i: (j, i, 0))
    return _call(body, name='ffn_dup', grid=(N_DEV, nb),
                 in_specs=[main, pl.BlockSpec((None, HALO16, FF_SHARD), lambda j, i: (j, ha(i), 0)), main,
                           pl.BlockSpec((None, 3, FF_SHARD), lambda j, i: (j, 0, 0))],
                 out_specs=[main, pl.BlockSpec((None, 8, FF_SHARD), lambda j, i: (j, 0, 0))],
                 out_shape=[_sds((N_DEV, T, FF_SHARD), BF16), _sds((N_DEV, 8, FF_SHARD))],
                 sem=('parallel', 'arbitrary'), vmem=VMEM_BIG, ride=ride)(dhid8, dhid8, up8, cw8)


def _grad_tn(a, b, a_spec, b_spec, groups, m, n, tk, name, ride=None, parts=1):
    T = a.shape[-2]
    nk = T // tk
    mp = m // parts

    def body(a_ref, b_ref, *refs):
        o_refs, acc_ref = refs[:parts], refs[parts]
        k = pl.program_id(1)
        part = _dot_tn(a_ref[...], b_ref[...])

        @pl.when(k == 0)
        def _():
            acc_ref[...] = part

        @pl.when(k > 0)
        def _():
            acc_ref[...] += part

        @pl.when(k == nk - 1)
        def _():
            for p, o_ref in enumerate(o_refs):
                o_ref[...] = acc_ref[p * mp:(p + 1) * mp, :].astype(BF16)

    out_spec = pl.BlockSpec((None, mp, n), lambda g, k: (g, 0, 0))
    res = _call(body, name=name, grid=(groups, nk), in_specs=[a_spec, b_spec], out_specs=[out_spec] * parts,
                out_shape=[_sds((groups, mp, n), BF16)] * parts, scratch=[pltpu.VMEM((m, n), F32)],
                sem=('parallel', 'arbitrary'), vmem=VMEM_BIG, ride=ride)(a, b)
    if parts > 1:
        return res
    return res[0] if ride is None else (res[0][0], res[1])


def _grad_w_in(h1, dproj, tk, ride):
    T = h1.shape[0]
    nk = T // tk
    half = D_IN_PROJ // 2

    def body(a_ref, b_ref, o_ref, acc_ref):
        k = pl.program_id(0)
        for h in range(2):
            cols = slice(h * half, (h + 1) * half)
            part = _dot_tn(a_ref[...], b_ref[:, cols])

            @pl.when(k == 0)
            def _():
                acc_ref[:, cols] = part

            @pl.when(k > 0)
            def _():
                acc_ref[:, cols] += part

        @pl.when(k == nk - 1)
        def _():
            for g in range(N_DEV):
                o_ref[g] = acc_ref[:, g * IN_SHARD:(g + 1) * IN_SHARD].astype(BF16)

    return _call(body, name='grad_w_in', grid=(nk,),
                 in_specs=[pl.BlockSpec((tk, D_MODEL), lambda k: (k, 0)), pl.BlockSpec((tk, D_IN_PROJ), lambda k: (k, 0))],
                 out_specs=_const((N_DEV, D_MODEL, IN_SHARD)), out_shape=_sds((N_DEV, D_MODEL, IN_SHARD), BF16),
                 scratch=[pltpu.VMEM((D_MODEL, D_IN_PROJ), F32)], sem=('arbitrary',), vmem=VMEM_BIG, ride=ride)(h1, dproj)


def _pre_norm_bwd(dz, dz_spec, w_parts, xin, dres, sc, g, tm, name, ride, below=None, group=1, w_t=False):
    T = xin.shape[0]
    n = w_parts[0].shape[1] if w_t else w_parts[0].shape[2]
    mul = _dot if w_t else _dot_nt
    steps = N_DEV // group
    width = D_MODEL // len(w_parts)

    def body(dz_ref, *refs):
        w_refs, (x_ref, dr_ref, sc_ref, g_ref), refs = refs[:len(w_parts)], refs[len(w_parts):len(w_parts) + 4], \
            refs[len(w_parts) + 4:]
        if below is None:
            dx_ref, dsh_ref, dsc_ref, dg_ref = refs
            sums = (dsh_ref, dsc_ref, dg_ref)
        else:
            v_ref, gate_ref, g2_ref, dx_ref, dsh_ref, dsc_ref, dg_ref, dv_ref, dgate_ref, dg2_ref = refs
            sums = (dsh_ref, dsc_ref, dg_ref, dgate_ref, dg2_ref)
        i, j = pl.program_id(0), pl.program_id(1)
        piece = (lambda s: dz_ref[s]) if dz.ndim == 3 else (lambda s: dz_ref[:, s * n:(s + 1) * n])
        parts = []
        for w_ref in w_refs:
            part = mul(piece(0), w_ref[0])
            for s in range(1, group):
                part = part + mul(piece(s), w_ref[s])
            parts.append(part)

        @pl.when(jnp.logical_and(i == 0, j == 0))
        def _():
            for s_ref in sums:
                s_ref[...] = jnp.zeros_like(s_ref)

        @pl.when(j == 0)
        def _():
            for k, part in enumerate(parts):
                dx_ref[:, k * width:(k + 1) * width] = part

        @pl.when(j > 0)
        def _():
            for k, part in enumerate(parts):
                dx_ref[:, k * width:(k + 1) * width] += part

        @pl.when(j == steps - 1)
        def _():
            dh, xv, gv = dx_ref[...], x_ref[...], g_ref[...]
            r = _rsqrt_mean(xv)
            dsh_ref[...] += _colsum(dh)
            dsc_ref[...] += _colsum(dh * (xv * r * gv))
            dxn = dh * (1.0 + sc_ref[...])
            dg_ref[...] += _colsum(dxn * xv * r)
            dx = dr_ref[...] + _norm_bwd(dxn, xv, r, gv)
            dx_ref[...] = dx
            if below is not None:
                v, g2 = v_ref[...].astype(F32), g2_ref[...]
                rv = _rsqrt_mean(v)
                dgate_ref[...] += _colsum(dx * (v * rv * g2))
                dn = dx * gate_ref[...]
                dg2_ref[...] += _colsum(dn * v * rv)
                dv_ref[...] = _norm_bwd(dn, v, rv, g2).astype(BF16)

    row = pl.BlockSpec((tm, D_MODEL), lambda i, j: (i, 0))
    vec = _const((1, D_MODEL))
    in_specs = [dz_spec] + [pl.BlockSpec((group,) + w.shape[1:], lambda i, j: (j, 0, 0)) for w in w_parts]
    in_specs += [row, row, vec, vec]
    out_specs = [row, vec, vec, vec]
    out_shape = [_sds((T, D_MODEL)), _sds((1, D_MODEL)), _sds((1, D_MODEL)), _sds((1, D_MODEL))]
    args = [dz, *w_parts, xin, dres, sc, g]
    if below is not None:
        in_specs += [row, vec, vec]
        out_specs += [row, vec, vec]
        out_shape += [_sds((T, D_MODEL), BF16), _sds((1, D_MODEL)), _sds((1, D_MODEL))]
        args += list(below)
    return _call(body, name=name, grid=(T // tm, steps), in_specs=in_specs, out_specs=out_specs,
                 out_shape=out_shape, sem=('arbitrary', 'arbitrary'), vmem=VMEM_MOST, ride=ride)(*args)


def _mix_bwd(d_o, w_out, yssm, proj, d, glu_w, glu_b, g_ssm, cw, g_conv, avg16, avg64, tm, ride):
    T = yssm.shape[0]
    hb = _halo_before(tm)

    def body(do_ref, wo_ref, y_ref, p_ref, ph_ref, d_ref, gw_ref, gb_ref, gs_ref, cw_ref, gc_ref, a16_ref, a64_ref,
             dy_ref, dconv_ref, dbg_ref, z_ref, dlin_ref, acc_ref):
        i = pl.program_id(0)
        dyc = _dot_nt(do_ref[...], wo_ref[...])

        @pl.when(i == 0)
        def _():
            acc_ref[...] = jnp.zeros_like(acc_ref)

        u = p_ref[:, 0:D_SSM]
        y = y_ref[...] + d_ref[...] * u
        z, t = _gelu(y)
        gate = _sigmoid(_dot(z.astype(BF16), gw_ref[...]) + gb_ref[...])
        ya = z * gate
        rs = lax.rsqrt(_dot_split(ya * ya, a16_ref[...], 2) + EPS)
        dna = dyc[:, 0:D_SSM]
        acc_ref[1:2, :] += _colsum(dna * ya * rs)
        dya = _head_norm_bwd(dna, ya, rs, gs_ref[...], a16_ref[...])
        dlin = dya * z * gate * (1.0 - gate)
        acc_ref[0:1, :] += _colsum(dlin)
        dlin_b = dlin.astype(BF16)
        dz = dya * gate + _dot_nt(dlin_b, gw_ref[...])
        dy = dz * _gelu_grad(y, t)
        acc_ref[3:4, :] += _colsum(dy * u)
        dy_ref[...] = dy
        z_ref[...] = z.astype(BF16)
        dlin_ref[...] = dlin_b

        bg = p_ref[:, D_SSM:D_SSM + D_CONV]
        cv = p_ref[:, D_SSM + D_CONV:D_SSM + 2 * D_CONV] * p_ref[:, D_SSM + 2 * D_CONV:D_IN_PROJ]
        hv = ph_ref[:, D_SSM + D_CONV:D_SSM + 2 * D_CONV] * ph_ref[:, D_SSM + 2 * D_CONV:D_IN_PROJ]
        hv = jnp.where(i > 0, hv, 0.0)
        conv, cv1, cv2 = _conv3(cv, hv, cw_ref)
        yb = bg * conv
        rsb = lax.rsqrt(_dot_split(yb * yb, a64_ref[...], 2) + EPS)
        dnb = dyc[:, D_SSM:D_MODEL]
        acc_ref[2:3, :] += _colsum(dnb * yb * rsb)
        dyb = _head_norm_bwd(dnb, yb, rsb, gc_ref[...], a64_ref[...])
        dbg_ref[...] = dyb * conv
        dconv = dyb * bg
        dconv_ref[...] = dconv
        acc_ref[4:5, :] += _colsum(dconv * cv2)
        acc_ref[5:6, :] += _colsum(dconv * cv1)
        acc_ref[6:7, :] += _colsum(dconv * cv)

    vec = _const((1, D_SSM))
    sq = _const((D_SSM, D_SSM))
    half = pl.BlockSpec((tm, D_SSM), lambda i: (i, 0))
    return _call(body, name='mix_bwd', grid=(T // tm,),
                 in_specs=[pl.BlockSpec((tm, D_MODEL), lambda i: (i, 0)), _const((D_MODEL, D_MODEL)), half,
                           pl.BlockSpec((tm, D_IN_PROJ), lambda i: (i, 0)),
                           pl.BlockSpec((HALO, D_IN_PROJ), lambda i: (hb(i), 0)), vec, sq, vec, vec,
                           _const((3, D_CONV)), vec, sq, sq],
                 out_specs=[half, half, half, half, half, _const((8, D_SSM))],
                 out_shape=[_sds((T, D_SSM)), _sds((T, D_SSM)), _sds((T, D_SSM)), _sds((T, D_SSM), BF16),
                            _sds((T, D_SSM), BF16), _sds((8, D_SSM))],
                 sem=('arbitrary',), vmem=VMEM_BIG, ride=ride)(d_o, w_out, yssm, proj, proj, d, glu_w, glu_b, g_ssm, cw,
                                                              g_conv, avg16, avg64)


def _mix_bwd_proj(dconv, proj, du_ssm, dy, d, dbg, cw, tm):
    T = dy.shape[0]
    nb = T // tm
    ha = _halo_after(tm, T)

    def body(dc_ref, dch_ref, cg_ref, v_ref, du_ref, dy_ref, d_ref, dbg_ref, cw_ref, o_ref):
        i = pl.program_id(0)
        dcv = _conv3_t(dc_ref[...], jnp.where(i < nb - 1, dch_ref[...], 0.0), cw_ref)[0]
        o_ref[:, 0:D_SSM] = (du_ref[...] + dy_ref[...] * d_ref[...]).astype(BF16)
        o_ref[:, D_SSM:D_SSM + D_CONV] = dbg_ref[...].astype(BF16)
        o_ref[:, D_SSM + D_CONV:D_SSM + 2 * D_CONV] = (dcv * v_ref[...]).astype(BF16)
        o_ref[:, D_SSM + 2 * D_CONV:D_IN_PROJ] = (dcv * cg_ref[...]).astype(BF16)

    half = pl.BlockSpec((tm, D_SSM), lambda i: (i, 0))
    return _call(body, name='mix_bwd_proj', grid=(nb,),
                 in_specs=[half, pl.BlockSpec((HALO, D_CONV), lambda i: (ha(i), 0)),
                           pl.BlockSpec((tm, D_CONV), lambda i: (i, 2)), pl.BlockSpec((tm, D_CONV), lambda i: (i, 3)),
                           half, half, _const((1, D_SSM)), half, _const((3, D_CONV))],
                 out_specs=pl.BlockSpec((tm, D_IN_PROJ), lambda i: (i, 0)), out_shape=_sds((T, D_IN_PROJ), BF16),
                 sem=('parallel',), vmem=VMEM_BIG)(dconv, dconv, proj, proj, du_ssm, dy, d, dbg, cw)


ADAMW_SLOT_BYTES = 8 << 20
ADAMW_ROW_BYTES = 3 << 19


def _row_tile(rows, cols, slots):
    for cand in range(rows, 15, -1):
        if (rows % cand == 0 and cand % 16 == 0 and slots * cand * cols * 4 <= ADAMW_SLOT_BYTES
                and cand * cols * 4 <= ADAMW_ROW_BYTES):
            return cand
    return rows


def _adamw_math(g, w, m, v):
    m2 = ADAM_B1 * m + (1.0 - ADAM_B1) * g
    v2 = ADAM_B2 * v + (1.0 - ADAM_B2) * (g * g)
    m_hat = m2 / (1.0 - ADAM_B1 ** ADAM_STEP)
    v_hat = v2 / (1.0 - ADAM_B2 ** ADAM_STEP)
    return -ADAM_LR * (m_hat / (jnp.sqrt(v_hat) + ADAM_EPS) + ADAM_WD * w), m2, v2


def _adamw(pieces, w, m, v, name):
    slots, _, cols = pieces[0].shape
    rows = sum(p.shape[1] for p in pieces)
    tr = _row_tile(pieces[0].shape[1], cols, slots)
    starts, pos = [], 0
    for p in pieces:
        assert p.shape[1] % tr == 0
        starts.append(pos)
        pos += p.shape[1] // tr

    def body(*refs):
        g_refs = refs[:len(pieces)]
        w_ref, m_ref, v_ref, go_ref, d_ref, mo_ref, vo_ref = refs[len(pieces):]
        i = pl.program_id(0)
        g = None
        for g_ref, start in zip(g_refs, starts):
            part = g_ref[0].astype(F32)
            for s in range(1, slots):
                part = part + g_ref[s].astype(F32)
            g = part if g is None else jnp.where(i >= start, part, g)
        go_ref[...] = g
        d_ref[...], mo_ref[...], vo_ref[...] = _adamw_math(g, w_ref[...], m_ref[...], v_ref[...])

    def piece_spec(start, count):
        return pl.BlockSpec((slots, tr, cols), lambda i: (0, jnp.clip(i - start, 0, count - 1), 0))

    blk = pl.BlockSpec((tr, cols), lambda i: (i, 0))
    return _call(body, name=name, grid=(rows // tr,),
                 in_specs=[piece_spec(s, p.shape[1] // tr) for s, p in zip(starts, pieces)] + [blk, blk, blk],
                 out_specs=[blk] * 4, out_shape=[_sds((rows, cols))] * 4, sem=('parallel',),
                 vmem=VMEM_BIG)(*pieces, w, m, v)


def _to_scan_rows(a):
    T, n = a.shape
    return a.reshape(SUBLANES, T // SUBLANES, n).transpose(1, 0, 2).reshape(T, n)


def _from_scan_rows(a):
    T, n = a.shape
    return a.reshape(T // SUBLANES, SUBLANES, n).transpose(1, 0, 2).reshape(T, n)


def _expand(a):
    return jnp.repeat(a, SSM_GROUP, axis=1)


def _block_diag(rows, row_group, col_group):
    r, n = rows.shape
    tiled = jnp.tile(rows, (1, N_GROUPS))
    keep = (jnp.arange(r)[:, None] // row_group) == (jnp.arange(n * N_GROUPS)[None, :] // col_group)
    return jnp.where(keep, tiled, 0.0)


def _block_diag_b(bb):
    return _block_diag(bb.transpose(0, 2, 1).reshape(D_SSM, SSM_STATE), SSM_GROUP, SSM_STATE)


def _block_diag_c(cc):
    return _block_diag(cc.transpose(0, 2, 1).reshape(N_STATE, SSM_GROUP), SSM_STATE, SSM_GROUP)


def _diag_blocks(x, chan_major):
    per = CHAN_BLOCK // SSM_GROUP
    eye = jnp.eye(per, dtype=x.dtype)
    if chan_major:
        x = x.reshape(-1, per, SSM_GROUP, per, SSM_STATE) * eye[None, :, None, :, None]
        return x.sum(axis=1).transpose(0, 2, 3, 1).reshape(N_GROUPS, SSM_STATE, SSM_GROUP)
    x = x.reshape(-1, per, SSM_STATE, per, SSM_GROUP) * eye[None, :, None, :, None]
    return x.sum(axis=3).reshape(N_GROUPS, SSM_STATE, SSM_GROUP)


SMALL_LAYOUT = {
    'ssm_b_re': (0, 0, 32, 1024), 'ssm_b_im': (32, 0, 32, 1024), 'ssm_c_re': (64, 0, 32, 1024),
    'ssm_c_im': (96, 0, 32, 1024), 'b_ada': (128, 0, 6, 1024), 'g_pre_mix': (134, 0, 1, 1024),
    'g_post_mix': (135, 0, 1, 1024), 'ssm_lam_re': (136, 0, 2, 1024), 'ssm_lam_im': (138, 0, 2, 1024),
    'ssm_log_step': (140, 0, 1, 32), 'glu_b': (141, 0, 1, 512), 'g_out_ssm': (141, 512, 1, 512),
    'g_out_conv': (142, 0, 1, 512), 'ssm_d': (142, 512, 1, 512), 'g_pre_ffn': (143, 0, 1, 1024),
    'g_post_ffn': (144, 0, 1, 1024)}
SMALL_ROWS = 152
B_ADA_ROW = SMALL_LAYOUT['b_ada'][0]
LATE_ROWS = {('b_ada', 0): 0, ('b_ada', 1): 1, ('g_pre_mix', 0): 2}


def _adamw_small(gathered, late, wts, mom_m, mom_v):
    names = list(SMALL_LAYOUT)
    n = len(names)

    def body(*refs):
        g_ref, late_ref, ins, outs = refs[0], refs[1], refs[2:2 + 3 * n], refs[2 + 3 * n:]
        for p, name in enumerate(names):
            r0, c0, rows, cols = SMALL_LAYOUT[name]
            pieces = [(0, rows)] if rows % 8 == 0 else [(r, 1) for r in range(rows)]
            for r, cnt in pieces:
                src_ref, first = (late_ref, LATE_ROWS[name, r]) if (name, r) in LATE_ROWS else (g_ref, r0 + r)
                g = src_ref[0, first:first + cnt, c0:c0 + cols]
                for s in range(1, N_DEV):
                    g = g + src_ref[s, first:first + cnt, c0:c0 + cols]
                w, m, v = (ins[3 * p + q][r:r + cnt, :] for q in range(3))
                res = (g,) + _adamw_math(g, w, m, v)
                for q in range(4):
                    outs[4 * p + q][r:r + cnt, :] = res[q]

    shapes = [SMALL_LAYOUT[name][2:] for name in names]
    args = [gathered, late]
    for name, shp in zip(names, shapes):
        args += [wts[name].reshape(shp), mom_m[name].reshape(shp), mom_v[name].reshape(shp)]
    outs = _call(body, name='adamw_small', grid=(1,),
                 in_specs=[_const(gathered.shape), _const(late.shape)]
                 + [_const(shp) for shp in shapes for _ in range(3)],
                 out_specs=[_const(shp) for shp in shapes for _ in range(4)],
                 out_shape=[_sds(shp) for shp in shapes for _ in range(4)], vmem=VMEM_BIG)(*args)
    res = {}
    for p, name in enumerate(names):
        for q, kind in enumerate(('g', 'd', 'm', 'v')):
            res[kind, name] = outs[4 * p + q].reshape(wts[name].shape)
    return res


def kernel(x, c, w_ada, b_ada, g_pre_mix, g_post_mix, w_in, ssm_lam_re, ssm_lam_im, ssm_log_step, ssm_b_re, ssm_b_im, ssm_c_re, ssm_c_im, ssm_d, glu_w, glu_b, g_out_ssm, conv_w, g_out_conv, w_out, g_pre_ffn, g_post_ffn, w_up, ffn_conv_w, w_down, loss_target, m_w_ada, m_b_ada, m_g_pre_mix, m_g_post_mix, m_w_in, m_ssm_lam_re, m_ssm_lam_im, m_ssm_log_step, m_ssm_b_re, m_ssm_b_im, m_ssm_c_re, m_ssm_c_im, m_ssm_d, m_glu_w, m_glu_b, m_g_out_ssm, m_conv_w, m_g_out_conv, m_w_out, m_g_pre_ffn, m_g_post_ffn, m_w_up, m_ffn_conv_w, m_w_down, v_w_ada, v_b_ada, v_g_pre_mix, v_g_post_mix, v_w_in, v_ssm_lam_re, v_ssm_lam_im, v_ssm_log_step, v_ssm_b_re, v_ssm_b_im, v_ssm_c_re, v_ssm_c_im, v_ssm_d, v_glu_w, v_glu_b, v_g_out_ssm, v_conv_w, v_g_out_conv, v_w_out, v_g_pre_ffn, v_g_post_ffn, v_w_up, v_ffn_conv_w, v_w_down):
    args = dict(locals())
    wts = {n: args[n] for n in WEIGHTS}
    mom_m = {n: args['m_' + n] for n in WEIGHTS}
    mom_v = {n: args['v_' + n] for n in WEIGHTS}
    T = x.shape[1]
    tm = min(512, T)
    tw = min(1024, T)
    tk = min(2048, T)
    me = _me()[3]
    xt, tgt = x[0], loss_target[0]

    c_all, w_in_s = _exchange([c, w_in[0].astype(BF16)], name='gather_first', scatter=False)
    c_all = c_all.reshape(N_DEV, D_MODEL)
    b_cols = lax.dynamic_slice(b_ada, (0, me * ADA_SHARD), (1, ADA_SHARD))
    mod_cols, c_act = _mod_cols(c_all, w_ada[0], b_cols)
    (mod_all,) = _exchange([mod_cols], name='gather_mod', scatter=False)
    mod = lax.dynamic_slice(mod_all, (0, me, 0), (N_DEV, 1, ADA_SHARD)).reshape(N_MOD, 1, D_MODEL)
    sh1, sc1, gt1, sh2, sc2, gt2 = [mod[k] for k in range(N_MOD)]


    lre_x, lim_x = _expand(ssm_lam_re[0]), _expand(ssm_lam_im[0])
    lst_x = jnp.broadcast_to(ssm_log_step[0][:, None], (N_GROUPS, SSM_STATE * SSM_GROUP))
    b_re_x = ssm_b_re[0].reshape(N_GROUPS, -1)
    b_im_x = ssm_b_im[0].reshape(N_GROUPS, -1)
    ar_x, ai_x, bbr_x, bbi_x = _ssm_prep(lre_x, lim_x, lst_x, b_re_x, b_im_x)
    lam_r = ar_x[:, ::SSM_GROUP].reshape(1, N_STATE)
    lam_i = ai_x[:, ::SSM_GROUP].reshape(1, N_STATE)
    big_b_re = _block_diag_b(bbr_x.reshape(N_GROUPS, SSM_STATE, SSM_GROUP)).astype(BF16)
    big_b_im = _block_diag_b(bbi_x.reshape(N_GROUPS, SSM_STATE, SSM_GROUP)).astype(BF16)
    big_c_re = _block_diag_c(ssm_c_re[0]).astype(BF16)
    big_c_im = _block_diag_c(ssm_c_im[0]).astype(BF16)
    head = jnp.arange(D_SSM)
    avg16 = jnp.where(head[:, None] // SSM_GROUP == head[None, :] // SSM_GROUP, 1.0 / SSM_GROUP, 0.0).astype(BF16)
    hd = D_CONV // CONV_HEADS
    avg64 = jnp.where(head[:, None] // hd == head[None, :] // hd, 1.0 / hd, 0.0).astype(BF16)

    w_up_t, half = w_up[0].T, D_MODEL // 2
    (proj, h1), (ffn_conv_s, conv_s, w_up_a) = _pre_mix(
        xt, sc1, sh1, g_pre_mix, w_in_s, tw, ([ffn_conv_w[0], conv_w[0], w_up_t[:, :half].astype(BF16)], False))
    cw_full = conv_s.transpose(1, 0, 2).reshape(3, D_CONV)
    u_perm = _to_scan_rows(proj[:, :D_SSM])
    (s_re, s_im, y_perm), (w_up_b, glu_s, w_out_s) = _ssm_fwd(
        u_perm, big_b_re, big_b_im, big_c_re, big_c_im, lam_r, lam_i,
        ([w_up_t[:, half:].astype(BF16), glu_w[0].astype(BF16), w_out[0].astype(BF16)], False))
    glu_full = glu_s.reshape(D_SSM, D_SSM)
    w_out_full = w_out_s.reshape(D_MODEL, D_MODEL)
    yssm = _from_scan_rows(y_perm)
    mix_args = (ssm_d, glu_full, glu_b, g_out_ssm, cw_full, g_out_conv, avg16, avg64)
    ycat = _mix_fwd(yssm, proj, *mix_args, tw)
    o, x1, h2 = _out_proj(ycat, w_out_full, xt, gt1, g_post_mix, g_pre_ffn, sc2, sh2, tw)
    (up8, hid8), (w_down_s,) = _ffn_up(h2, w_up_a, w_up_b, ffn_conv_s, tm, ([w_down[0].astype(BF16)], False))
    wd4 = w_down_s.reshape(4, FF_SHARD, D_MODEL)
    hid4 = hid8.reshape(2, 4, T, FF_SHARD)
    ddn, dx2, loss_parts, d_gt2, d_g_post_ffn = _ffn_down(hid4, wd4, x1, tgt, gt2, g_post_ffn, tm)
    loss_local = jnp.sum(loss_parts[:, 0, 0])

    got = {}
    dhid, g_w_down = _ffn_dact(ddn, wd4, hid4, tw)
    (dup8, dcw_ffn), (got['w_down'],) = _ffn_dup(dhid.reshape(N_DEV, T, FF_SHARD), up8, ffn_conv_s, tw,
                                                 ([g_w_down.reshape(N_DEV, D_FF // N_DEV, D_MODEL)], True))
    g_w_up_halves = _grad_tn(dup8, h2, pl.BlockSpec((None, T, FF_SHARD), lambda g, k: (g, k, 0)),
                             pl.BlockSpec((T, D_MODEL), lambda g, k: (k, 0)), N_DEV, FF_SHARD, D_MODEL, T,
                             'grad_w_up', parts=2)
    (dx1, d_sh2, d_sc2, d_g_pre_ffn, d_o, d_gt1, d_g_post_mix), (got_up_0, got['ffn_conv_w']) = _pre_norm_bwd(
        dup8, pl.BlockSpec((N_DEV, tm, FF_SHARD), lambda i, j: (j, i, 0)), [w_up_a, w_up_b], x1, dx2, sc2, g_pre_ffn,
        tm, 'ffn_in_bwd', ([g_w_up_halves[0], dcw_ffn], True), below=(o, gt1, g_post_mix), group=N_DEV, w_t=True)

    g_w_out = _grad_tn(ycat, d_o, pl.BlockSpec((tk, D_MODEL), lambda g, k: (k, 0)),
                       pl.BlockSpec((tk, D_MODEL), lambda g, k: (k, 0)), 1, D_MODEL, D_MODEL, tk, 'grad_w_out')
    (dy, dconv, dbg, z_b, dlin_b, sums), (got['w_out'],) = _mix_bwd(
        d_o, w_out_full, yssm, proj, *mix_args, tm, ([g_w_out.reshape(N_DEV, D_MODEL // N_DEV, D_MODEL)], True))
    g_glu_w = _grad_tn(z_b, dlin_b, pl.BlockSpec((tk, D_SSM), lambda g, k: (k, 0)),
                       pl.BlockSpec((tk, D_SSM), lambda g, k: (k, 0)), 1, D_SSM, D_SSM, tk, 'grad_glu_w')
    dy_perm = _to_scan_rows(dy)
    (du_perm, dbr_blk, dbi_blk, dcr_blk, dci_blk, dar_blk, dai_blk), (got_up_1, got['glu_w']) = _ssm_bwd(
        dy_perm, u_perm, s_re, s_im, big_b_re, big_b_im, big_c_re, big_c_im, lam_r, lam_i,
        ([g_w_up_halves[1], g_glu_w.reshape(N_DEV, D_SSM // N_DEV, D_SSM)], True))
    du_ssm = _from_scan_rows(du_perm)
    dproj = _mix_bwd_proj(dconv, proj, du_ssm, dy, ssm_d, dbg, cw_full, tw)
    dbb_re = _diag_blocks(dbr_blk, True).reshape(N_GROUPS, -1)
    dbb_im = _diag_blocks(dbi_blk, True).reshape(N_GROUPS, -1)
    d_c_re = _diag_blocks(dcr_blk, False).transpose(0, 2, 1)
    d_c_im = _diag_blocks(dci_blk, False).transpose(0, 2, 1)
    lane = jnp.arange(SSM_STATE * SSM_GROUP)
    seg = jnp.where(lane[:, None] // SSM_GROUP == lane[None, :] // SSM_GROUP, 1.0, 0.0).astype(BF16)
    d_b_re_x, d_b_im_x, d_lre_x, d_lim_x, d_lst = _ssm_prep_bwd(
        lre_x, lim_x, lst_x, b_re_x, b_im_x, dbb_re, dbb_im, _expand(dar_blk.reshape(N_GROUPS, SSM_STATE)),
        _expand(dai_blk.reshape(N_GROUPS, SSM_STATE)), seg)

    row = lambda a: a.reshape(-1, PACK_COLS)
    blank = jnp.zeros((1, PACK_COLS), F32)
    small_pack = jnp.concatenate([
        d_b_re_x, d_b_im_x, row(d_c_re), row(d_c_im), blank, blank, d_gt1, d_sh2, d_sc2, d_gt2, blank,
        d_g_post_mix, row(d_lre_x[:, ::SSM_GROUP]), row(d_lim_x[:, ::SSM_GROUP]),
        jnp.pad(d_lst.reshape(1, N_GROUPS), ((0, 0), (0, PACK_COLS - N_GROUPS))), row(sums[0:4]), d_g_pre_ffn,
        d_g_post_ffn, jnp.zeros((SMALL_ROWS - 145, PACK_COLS), F32)])
    g_w_in, (small_all,) = _grad_w_in(h1, dproj, tk, ([small_pack], False))
    g_conv_slots = jnp.concatenate([sums[4:7], jnp.zeros((5, D_CONV), F32)]).reshape(
        8, N_DEV, D_CONV // N_DEV).transpose(1, 0, 2)
    (grad_x, d_sh1, d_sc1, d_g_pre_mix), (got['w_in'], got['conv_w']) = _pre_norm_bwd(
        dproj, pl.BlockSpec((tw, 4 * IN_SHARD), lambda i, j: (i, j)), [w_in_s], xt, dx1, sc1, g_pre_mix, tw,
        'mix_in_bwd', ([g_w_in, g_conv_slots], True), group=4)
    late_pack = jnp.concatenate([d_sh1, d_sc1, d_g_pre_mix, jnp.full((1, PACK_COLS), loss_local, F32),
                                 jnp.zeros((4, PACK_COLS), F32)])
    (late_all,) = _exchange([late_pack], name='gather_late_grads', scatter=False)
    loss = jnp.sum(late_all[:, 3, 0])
    res = _adamw_small(small_all, late_all, wts, mom_m, mom_v)

    dmod_all = jnp.concatenate([late_all[:, 0:2, :], small_all[:, B_ADA_ROW + 2:B_ADA_ROW + N_MOD, :]],
                               axis=1).reshape(N_DEV, N_MOD * D_MODEL)
    dmod_cols = lax.dynamic_slice(dmod_all, (0, me * ADA_SHARD), (N_DEV, ADA_SHARD))
    g_w_ada = _grad_w_ada(c_act.T, dmod_cols)

    pieces = {n: [slots[:, :3, :] if n in ('conv_w', 'ffn_conv_w') else slots] for n, slots in got.items()}
    for n, parts in pieces.items():
        outs = _adamw(parts, wts[n][0], mom_m[n][0], mom_v[n][0], 'adamw_' + n)
        for kind, val in zip(('g', 'd', 'm', 'v'), outs):
            res[kind, n] = val[None]
    outs = _adamw([got_up_0, got_up_1], w_up[0].T, m_w_up[0].T, v_w_up[0].T, 'adamw_w_up')
    for kind, val in zip(('g', 'd', 'm', 'v'), outs):
        res[kind, 'w_up'] = val.T[None]
    outs = _adamw([g_w_ada[None]], w_ada[0], m_w_ada[0], v_w_ada[0], 'adamw_w_ada')
    for kind, val in zip(('g', 'd', 'm', 'v'), outs):
        res[kind, 'w_ada'] = val[None]

    return (loss, grad_x[None], *[res['g', n] for n in WEIGHTS], *[res['d', n] for n in WEIGHTS],
            *[res['m', n] for n in WEIGHTS], *[res['v', n] for n in WEIGHTS])
```

```python
import math

import jax
import jax.numpy as jnp
from jax import lax
from jax.experimental import pallas as pl
from jax.experimental.pallas import tpu as pltpu

F32, BF16 = jnp.float32, jnp.bfloat16

D_MODEL = 1024
D_SSM = 512
D_CONV = 512
SSM_GROUP = 16
N_GROUPS = 32
SSM_STATE = 64
N_STATE = N_GROUPS * SSM_STATE
CONV_HEADS = 8
D_FF = 2816
N_MOD = 6
D_IN_PROJ = D_SSM + 3 * D_CONV
N_DEV = 8
FF_SHARD = 2 * D_FF // N_DEV
IN_SHARD = D_IN_PROJ // N_DEV
ADA_SHARD = N_MOD * D_MODEL // N_DEV
EPS = 1e-6
LAMBDA_RE_MAX = -1e-4
ADAM_LR, ADAM_B1, ADAM_B2, ADAM_EPS, ADAM_WD, ADAM_STEP = 0.001, 0.9, 0.999, 1e-08, 0.01, 10
GELU_C = math.sqrt(2.0 / math.pi)
GELU_A = 0.044715

SUBLANES = 8
HALO = 8
HALO16 = 16
SCAN_UNROLL = 16
STATE_BLOCK = 512
CHAN_BLOCK = 128
VMEM_BIG = 48 << 20
VMEM_MOST = 58 << 20

WEIGHTS = ['w_ada', 'b_ada', 'g_pre_mix', 'g_post_mix', 'w_in', 'ssm_lam_re', 'ssm_lam_im', 'ssm_log_step',
           'ssm_b_re', 'ssm_b_im', 'ssm_c_re', 'ssm_c_im', 'ssm_d', 'glu_w', 'glu_b', 'g_out_ssm', 'conv_w',
           'g_out_conv', 'w_out', 'g_pre_ffn', 'g_post_ffn', 'w_up', 'ffn_conv_w', 'w_down']
PACK_COLS = 1024


def _call(body, *, name, grid, in_specs, out_specs, out_shape, scratch=(), sem=None, vmem=None, ride=None):
    params = {}
    if vmem is not None:
        params['vmem_limit_bytes'] = vmem
    if ride is None:
        if sem is not None:
            params['dimension_semantics'] = sem
        return pl.pallas_call(body, name=name, grid=grid, in_specs=in_specs, out_specs=out_specs,
                              out_shape=out_shape, scratch_shapes=list(scratch),
                              compiler_params=pltpu.CompilerParams(**params))
    arrs, scatter = ride
    single = not isinstance(out_shape, (list, tuple))
    out_shape_l = [out_shape] if single else list(out_shape)
    out_specs_l = [out_specs] if single else list(out_specs)
    n, n_in, n_out, n_scr = len(arrs), len(in_specs), len(out_shape_l), len(scratch)
    any_spec = pl.BlockSpec(memory_space=pl.ANY)
    params['dimension_semantics'] = ('arbitrary',) * len(grid)

    def carried(*refs):
        ins, rin = refs[:n_in], refs[n_in:n_in + n]
        outs, rout = refs[n_in + n:n_in + n + n_out], refs[n_in + n + n_out:n_in + 2 * n + n_out]
        scr, sems = refs[n_in + 2 * n + n_out:n_in + 2 * n + n_out + n_scr], refs[n_in + 2 * n + n_out + n_scr:]
        first = pl.program_id(0) == 0
        last = pl.program_id(0) == grid[0] - 1
        for ax in range(1, len(grid)):
            first = jnp.logical_and(first, pl.program_id(ax) == 0)
            last = jnp.logical_and(last, pl.program_id(ax) == grid[ax] - 1)

        @pl.when(first)
        def _():
            _exchange_start(rin, rout, sems, scatter)

        body(*ins, *outs, *scr)

        @pl.when(last)
        def _():
            _exchange_wait(rin, rout, sems, scatter)

    call = pl.pallas_call(carried, name=name, grid=grid, in_specs=list(in_specs) + [any_spec] * n,
                          out_specs=out_specs_l + [any_spec] * n,
                          out_shape=out_shape_l + _exchange_shapes(arrs, scatter),
                          scratch_shapes=list(scratch) + _exchange_sems(n),
                          compiler_params=pltpu.CompilerParams(**params))

    def run(*args):
        res = call(*args, *arrs)
        own = res[0] if single else list(res[:n_out])
        return own, list(res[n_out:])

    return run


def _const(shape):
    nd = len(shape)
    return pl.BlockSpec(shape, lambda *_: (0,) * nd)


def _sds(shape, dtype=F32):
    return jax.ShapeDtypeStruct(shape, dtype)


def _dot(a, b):
    return jnp.dot(a, b, preferred_element_type=F32)


def _dot_nt(a, b):
    return lax.dot_general(a, b, (((1,), (1,)), ((), ())), preferred_element_type=F32)


def _dot_tn(a, b):
    return lax.dot_general(a, b, (((0,), (0,)), ((), ())), preferred_element_type=F32)


def _dot_split(x, mat, parts):
    acc = None
    rem = x
    for _ in range(parts):
        piece = rem.astype(BF16)
        rem = rem - piece.astype(F32)
        term = _dot(piece, mat)
        acc = term if acc is None else acc + term
    return acc


def _sigmoid(x):
    return 1.0 / (1.0 + jnp.exp(-x))


def _gelu(x):
    t = jnp.tanh(GELU_C * (x + GELU_A * x * x * x))
    return 0.5 * x * (1.0 + t), t


def _gelu_grad(x, t):
    return 0.5 * (1.0 + t) + 0.5 * x * (1.0 - t * t) * GELU_C * (1.0 + 3.0 * GELU_A * x * x)


def _rsqrt_mean(x):
    return lax.rsqrt(jnp.mean(x * x, axis=-1, keepdims=True) + EPS)


def _colsum(x):
    return jnp.sum(x, axis=0, keepdims=True)


def _shifts_down(x, halo):
    ext = jnp.concatenate([halo, x], axis=0)
    return pltpu.roll(ext, 1, 0)[halo.shape[0]:], pltpu.roll(ext, 2, 0)[halo.shape[0]:]


def _shifts_up(x, halo):
    n = x.shape[0]
    ext = jnp.concatenate([x, halo], axis=0)
    total = ext.shape[0]
    return pltpu.roll(ext, total - 1, 0)[:n], pltpu.roll(ext, total - 2, 0)[:n]


def _conv3(x, halo, w_ref):
    x1, x2 = _shifts_down(x, halo)
    return w_ref[0:1, :] * x2 + w_ref[1:2, :] * x1 + w_ref[2:3, :] * x, x1, x2


def _conv3_t(g, halo, w_ref):
    g1, g2 = _shifts_up(g, halo)
    return w_ref[2:3, :] * g + w_ref[1:2, :] * g1 + w_ref[0:1, :] * g2, g1, g2


def _silu_parts(x):
    s = _sigmoid(x)
    return x * s, s * (1.0 + x * (1.0 - s))


def _norm_bwd(dn, x, r, g):
    gd = g * dn
    return r * gd - x * (r * r * r) * jnp.mean(gd * x, axis=-1, keepdims=True)


def _head_norm_bwd(dn, y, rs, g, avg):
    gd = g * dn
    return rs * gd - y * (rs * rs * rs) * _dot_split(gd * y, avg, 2)


def _me():
    x, y, c = lax.axis_index('x'), lax.axis_index('y'), lax.axis_index('c')
    return x, y, c, 4 * x + 2 * y + c


def _peer(k):
    x, y, c, _ = _me()
    px = 1 - x if k & 4 else x
    py = 1 - y if k & 2 else y
    pc = 1 - c if k & 1 else c
    return (px, py, pc), 4 * px + 2 * py + pc


SIBLING = 1
OTHER_CHIPS = (2, 4, 6)


def _remote(src, dst, sems, a, k, dev):
    return pltpu.make_async_remote_copy(src_ref=src, dst_ref=dst, send_sem=sems[0].at[a, k - 1],
                                        recv_sem=sems[1].at[a, k - 1], device_id=dev,
                                        device_id_type=pl.DeviceIdType.MESH)


def _exchange_copies(ins, outs, sems, scatter):
    me = _me()[3]
    local, first, relay, arrivals = [], [], [], []
    for a in range(len(ins)):
        src = ins[a].at[me] if scatter else ins[a]
        local.append(pltpu.make_async_copy(src, outs[a].at[me], sems[2].at[a]))
        for k in range(1, N_DEV):
            dev, idx = _peer(k)
            landed = _remote(src, outs[a].at[idx], sems, a, k, dev)
            if scatter:
                first.append(_remote(ins[a].at[idx], outs[a].at[me], sems, a, k, dev))
                arrivals.append(landed)
            elif k == SIBLING:
                first.append(_remote(src, outs[a].at[me], sems, a, k, dev))
                arrivals.append(landed)
            elif k in OTHER_CHIPS:
                first.append(_remote(src, outs[a].at[me], sems, a, k, dev))
                sib, _ = _peer(SIBLING)
                relay.append((landed, _remote(outs[a].at[idx], outs[a].at[idx], sems, a, k | SIBLING, sib)))
            else:
                arrivals.append(landed)
    return local, first, relay, arrivals


def _exchange_start(ins, outs, sems, scatter):
    local, first, _, _ = _exchange_copies(ins, outs, sems, scatter)
    for cp in local + first:
        cp.start()


def _exchange_wait(ins, outs, sems, scatter):
    local, first, relay, arrivals = _exchange_copies(ins, outs, sems, scatter)
    for landed, forward in relay:
        landed.wait_recv()
        forward.start()
    for cp in arrivals:
        cp.wait_recv()
    for cp in first + [forward for _, forward in relay]:
        cp.wait_send()
    for cp in local:
        cp.wait()


def _exchange_shapes(arrs, scatter):
    return [_sds(a.shape if scatter else (N_DEV,) + a.shape, a.dtype) for a in arrs]


def _exchange_sems(n):
    return [pltpu.SemaphoreType.DMA((n, N_DEV - 1)), pltpu.SemaphoreType.DMA((n, N_DEV - 1)),
            pltpu.SemaphoreType.DMA((n,))]


def _exchange(arrs, *, name, scatter):
    n = len(arrs)

    def body(*refs):
        _exchange_start(refs[:n], refs[n:2 * n], refs[2 * n:], scatter)
        _exchange_wait(refs[:n], refs[n:2 * n], refs[2 * n:], scatter)

    any_spec = pl.BlockSpec(memory_space=pl.ANY)
    outs = pl.pallas_call(body, name=name, out_shape=_exchange_shapes(arrs, scatter), in_specs=[any_spec] * n,
                          out_specs=[any_spec] * n, scratch_shapes=_exchange_sems(n))(*arrs)
    return list(outs)


def _mod_cols(c_all, w_ada, b_cols):
    def body(c_ref, w_ref, b_ref, mod_ref, act_ref):
        c = c_ref[...]
        act = c * _sigmoid(c)
        act_ref[...] = act
        mod_ref[...] = _dot(act.astype(BF16), w_ref[...].astype(BF16)) + b_ref[...]

    return _call(body, name='mod_cols', grid=(1,),
                 in_specs=[_const(c_all.shape), _const(w_ada.shape), _const(b_cols.shape)],
                 out_specs=[_const((N_DEV, ADA_SHARD)), _const(c_all.shape)],
                 out_shape=[_sds((N_DEV, ADA_SHARD)), _sds(c_all.shape)], vmem=VMEM_BIG)(c_all, w_ada, b_cols)


def _grad_w_ada(act_t, dmod_cols):
    def body(a_ref, d_ref, o_ref):
        o_ref[...] = _dot(a_ref[...], d_ref[...])

    return _call(body, name='grad_w_ada', grid=(1,), in_specs=[_const(act_t.shape), _const(dmod_cols.shape)],
                 out_specs=_const((D_MODEL, ADA_SHARD)), out_shape=_sds((D_MODEL, ADA_SHARD)),
                 vmem=VMEM_BIG)(act_t, dmod_cols)


def _pre_mix(x, sc, sh, g, w_s, tm, ride):
    T = x.shape[0]
    group = 4

    def body(x_ref, sc_ref, sh_ref, g_ref, w_ref, proj_ref, h_ref):
        @pl.when(pl.program_id(1) == 0)
        def _():
            xv = x_ref[...]
            h_ref[...] = ((xv * _rsqrt_mean(xv) * g_ref[...]) * (1.0 + sc_ref[...]) + sh_ref[...]).astype(BF16)

        for s in range(group):
            proj_ref[:, s * IN_SHARD:(s + 1) * IN_SHARD] = _dot(h_ref[...], w_ref[s])

    row = pl.BlockSpec((tm, D_MODEL), lambda i, j: (i, 0))
    vec = _const((1, D_MODEL))
    return _call(body, name='pre_mix', grid=(T // tm, N_DEV // group),
                 in_specs=[row, vec, vec, vec, pl.BlockSpec((group, D_MODEL, IN_SHARD), lambda i, j: (j, 0, 0))],
                 out_specs=[pl.BlockSpec((tm, group * IN_SHARD), lambda i, j: (i, j)), row],
                 out_shape=[_sds((T, D_IN_PROJ)), _sds((T, D_MODEL), BF16)],
                 sem=('parallel', 'arbitrary'), ride=ride)(x, sc, sh, g, w_s)


def _halo_before(tm, rows=HALO):
    return lambda i: jnp.maximum(i * (tm // rows) - 1, 0)


def _halo_after(tm, T, rows=HALO):
    return lambda i: jnp.minimum((i + 1) * (tm // rows), T // rows - 1)


def _mix_fwd(yssm, proj, d, glu_w, glu_b, g_ssm, cw, g_conv, avg16, avg64, tm):
    T = yssm.shape[0]
    hb = _halo_before(tm)

    def body(y_ref, p_ref, ph_ref, d_ref, gw_ref, gb_ref, gs_ref, cw_ref, gc_ref, a16_ref, a64_ref, o_ref):
        i = pl.program_id(0)
        u = p_ref[:, 0:D_SSM]
        y = y_ref[...] + d_ref[...] * u
        z, _ = _gelu(y)
        gate = _sigmoid(_dot(z.astype(BF16), gw_ref[...]) + gb_ref[...])
        ya = z * gate
        rs = lax.rsqrt(_dot_split(ya * ya, a16_ref[...], 2) + EPS)
        o_ref[:, 0:D_SSM] = (ya * rs * gs_ref[...]).astype(BF16)
        bg = p_ref[:, D_SSM:D_SSM + D_CONV]
        cv = p_ref[:, D_SSM + D_CONV:D_SSM + 2 * D_CONV] * p_ref[:, D_SSM + 2 * D_CONV:D_IN_PROJ]
        hv = ph_ref[:, D_SSM + D_CONV:D_SSM + 2 * D_CONV] * ph_ref[:, D_SSM + 2 * D_CONV:D_IN_PROJ]
        hv = jnp.where(i > 0, hv, 0.0)
        conv, _, _ = _conv3(cv, hv, cw_ref)
        yb = bg * conv
        rsb = lax.rsqrt(_dot_split(yb * yb, a64_ref[...], 2) + EPS)
        o_ref[:, D_SSM:D_MODEL] = (yb * rsb * gc_ref[...]).astype(BF16)

    vec = _const((1, D_SSM))
    sq = _const((D_SSM, D_SSM))
    return _call(body, name='mix_fwd', grid=(T // tm,),
                 in_specs=[pl.BlockSpec((tm, D_SSM), lambda i: (i, 0)), pl.BlockSpec((tm, D_IN_PROJ), lambda i: (i, 0)),
                           pl.BlockSpec((HALO, D_IN_PROJ), lambda i: (hb(i), 0)), vec, sq, vec, vec,
                           _const((3, D_CONV)), vec, sq, sq],
                 out_specs=pl.BlockSpec((tm, D_MODEL), lambda i: (i, 0)), out_shape=_sds((T, D_MODEL), BF16),
                 sem=('parallel',), vmem=VMEM_BIG)(yssm, proj, proj, d, glu_w, glu_b, g_ssm, cw, g_conv, avg16, avg64)


def _out_proj(ycat, w_out, x, gt, g_post, g_pre, sc, sh, tm):
    T = x.shape[0]

    def body(y_ref, w_ref, x_ref, gt_ref, gp_ref, g2_ref, sc_ref, sh_ref, o_ref, x1_ref, h_ref):
        o = _dot(y_ref[...], w_ref[...])
        o_ref[...] = o.astype(BF16)
        x1 = x_ref[...] + gt_ref[...] * (o * _rsqrt_mean(o) * gp_ref[...])
        x1_ref[...] = x1
        h_ref[...] = ((x1 * _rsqrt_mean(x1) * g2_ref[...]) * (1.0 + sc_ref[...]) + sh_ref[...]).astype(BF16)

    row = pl.BlockSpec((tm, D_MODEL), lambda i: (i, 0))
    vec = _const((1, D_MODEL))
    return _call(body, name='out_proj', grid=(T // tm,),
                 in_specs=[row, _const((D_MODEL, D_MODEL)), row, vec, vec, vec, vec, vec],
                 out_specs=[row, row, row],
                 out_shape=[_sds((T, D_MODEL), BF16), _sds((T, D_MODEL)), _sds((T, D_MODEL), BF16)],
                 sem=('parallel',), vmem=VMEM_BIG)(ycat, w_out, x, gt, g_post, g_pre, sc, sh)


def _ffn_up(h2, w_a, w_b, cw8, tm, ride):
    T = h2.shape[0]
    hb = _halo_before(tm, HALO16)
    half = D_MODEL // 2

    def body(h_ref, hh_ref, wa_ref, wb_ref, cw_ref, up_ref, hid_ref):
        def times_w(ref, s):
            return _dot_nt(ref[:, :half], wa_ref[s]) + _dot_nt(ref[:, half:], wb_ref[s])

        for s in range(2):
            up = times_w(h_ref, s)
            up_ref[s] = up.astype(BF16)
            before = jnp.where(pl.program_id(0) > 0, times_w(hh_ref, s), 0.0)
            hid_ref[s] = _conv3(up, before, cw_ref.at[s])[0].astype(BF16)

    out = pl.BlockSpec((2, tm, FF_SHARD), lambda i, j: (j, i, 0))
    return _call(body, name='ffn_up', grid=(T // tm, N_DEV // 2),
                 in_specs=[pl.BlockSpec((tm, D_MODEL), lambda i, j: (i, 0)),
                           pl.BlockSpec((HALO16, D_MODEL), lambda i, j: (hb(i), 0)),
                           pl.BlockSpec((2, FF_SHARD, half), lambda i, j: (j, 0, 0)),
                           pl.BlockSpec((2, FF_SHARD, half), lambda i, j: (j, 0, 0)),
                           pl.BlockSpec((2, 3, FF_SHARD), lambda i, j: (j, 0, 0))],
                 out_specs=[out, out], out_shape=[_sds((N_DEV, T, FF_SHARD), BF16)] * 2,
                 sem=('parallel', 'parallel'), vmem=VMEM_BIG, ride=ride)(h2, h2, w_a, w_b, cw8)


def _ffn_down(hid4, wd4, x1, tgt, gt, g_post, tm):
    T = x1.shape[0]
    nb = T // tm

    def body(a_ref, w_ref, x1_ref, t_ref, gt_ref, g_ref, ddn_ref, dx_ref, loss_ref, dgt_ref, dg_ref, dn_ref):
        i, j = pl.program_id(0), pl.program_id(1)
        part = None
        for s in range(4):
            act = (_silu_parts(a_ref[0, s].astype(F32))[0] * a_ref[1, s].astype(F32)).astype(BF16)
            term = _dot(act, w_ref[s])
            part = term if part is None else part + term

        @pl.when(jnp.logical_and(i == 0, j == 0))
        def _():
            dgt_ref[...] = jnp.zeros_like(dgt_ref)
            dg_ref[...] = jnp.zeros_like(dg_ref)

        @pl.when(j == 0)
        def _():
            dn_ref[...] = part

        @pl.when(j > 0)
        def _():
            dn_ref[...] += part

        @pl.when(j == 0)
        def _():
            dn, gv, gate = dn_ref[...], g_ref[...], gt_ref[...]
            r = _rsqrt_mean(dn)
            normed = dn * r * gv
            err = x1_ref[...] + gate * normed - t_ref[...]
            dx = err * (1.0 / D_MODEL)
            dx_ref[...] = dx
            tot = jnp.sum(jnp.sum(err * err, axis=1, keepdims=True), axis=0, keepdims=True) * (0.5 / D_MODEL)
            loss_ref[...] = jnp.broadcast_to(tot, (8, 128))
            dgt_ref[...] += _colsum(dx * normed)
            dnn = dx * gate
            dg_ref[...] += _colsum(dnn * dn * r)
            ddn_ref[...] = _norm_bwd(dnn, dn, r, gv).astype(BF16)

    row = pl.BlockSpec((tm, D_MODEL), lambda i, j: (i, 0))
    vec = _const((1, D_MODEL))
    return _call(body, name='ffn_down', grid=(nb, 1),
                 in_specs=[pl.BlockSpec((2, 4, tm, FF_SHARD), lambda i, j: (0, j, i, 0)),
                           pl.BlockSpec((4, FF_SHARD, D_MODEL), lambda i, j: (j, 0, 0)), row, row, vec, vec],
                 out_specs=[row, row, pl.BlockSpec((None, 8, 128), lambda i, j: (i, 0, 0)), vec, vec],
                 out_shape=[_sds((T, D_MODEL), BF16), _sds((T, D_MODEL)), _sds((nb, 8, 128)), _sds((1, D_MODEL)),
                            _sds((1, D_MODEL))],
                 scratch=[pltpu.VMEM((tm, D_MODEL), F32)], sem=('arbitrary', 'arbitrary'),
                 vmem=VMEM_MOST)(hid4, wd4, x1, tgt, gt, g_post)


def _ssm_prep(lre, lim, lst, b_re, b_im):
    def body(lre_ref, lim_ref, lst_ref, br_ref, bi_ref, ar_ref, ai_ref, bbr_ref, bbi_ref):
        ar, ai, qr, qi = _zoh(lre_ref[...], lim_ref[...], lst_ref[...])[:4]
        ar_ref[...] = ar
        ai_ref[...] = ai
        bbr_ref[...] = qr * br_ref[...] - qi * bi_ref[...]
        bbi_ref[...] = qr * bi_ref[...] + qi * br_ref[...]

    shp = lre.shape
    return _call(body, name='ssm_prep', grid=(1,), in_specs=[_const(shp)] * 5, out_specs=[_const(shp)] * 4,
                 out_shape=[_sds(shp)] * 4)(lre, lim, lst, b_re, b_im)


def _zoh(lre, lim, lst):
    lr = jnp.minimum(lre, LAMBDA_RE_MAX)
    st = jnp.exp(lst)
    mag = jnp.exp(lr * st)
    ar = mag * jnp.cos(lim * st)
    ai = mag * jnp.sin(lim * st)
    den = lr * lr + lim * lim
    qr = ((ar - 1.0) * lr + ai * lim) / den
    qi = (ai * lr - (ar - 1.0) * lim) / den
    return ar, ai, qr, qi, lr, st, den


def _ssm_prep_bwd(lre, lim, lst, b_re, b_im, dbbr, dbbi, dar, dai, seg):
    def body(lre_ref, lim_ref, lst_ref, br_ref, bi_ref, dbbr_ref, dbbi_ref, dar_ref, dai_ref, seg_ref,
             dbr_ref, dbi_ref, dlre_ref, dlim_ref, dlst_ref):
        lre_v = lre_ref[...]
        li = lim_ref[...]
        ar, ai, qr, qi, lr, st, den = _zoh(lre_v, li, lst_ref[...])
        br, bi, gbr, gbi = br_ref[...], bi_ref[...], dbbr_ref[...], dbbi_ref[...]
        dbr_ref[...] = qr * gbr + qi * gbi
        dbi_ref[...] = qr * gbi - qi * gbr
        gqr = _dot_split(br * gbr + bi * gbi, seg_ref[...], 3)
        gqi = _dot_split(br * gbi - bi * gbr, seg_ref[...], 3)
        ir, ii = lr / den, -li / den
        gar = dar_ref[...] + ir * gqr + ii * gqi
        gai = dai_ref[...] + ir * gqi - ii * gqr
        tr, ti = qr * ir - qi * ii, qr * ii + qi * ir
        glr = -(tr * gqr + ti * gqi)
        gli = -(tr * gqi - ti * gqr)
        gzr = ar * gar + ai * gai
        gzi = ar * gai - ai * gar
        glr = glr + st * gzr
        gli = gli + st * gzi
        gst = (lr * gzr + li * gzi) * st
        dlre_ref[...] = jnp.where(lre_v < LAMBDA_RE_MAX, glr, 0.0)
        dlim_ref[...] = gli
        dlst_ref[...] = jnp.sum(gst, axis=1, keepdims=True) * (1.0 / SSM_GROUP)

    shp = lre.shape
    return _call(body, name='ssm_prep_bwd', grid=(1,), in_specs=[_const(shp)] * 9 + [_const(seg.shape)],
                 out_specs=[_const(shp)] * 4 + [_const((N_GROUPS, 1))],
                 out_shape=[_sds(shp)] * 4 + [_sds((N_GROUPS, 1))], vmem=VMEM_BIG)(
                     lre, lim, lst, b_re, b_im, dbbr, dbbi, dar, dai, seg)


def _scan_specs(T):
    return dict(
        chan=pl.BlockSpec((T, CHAN_BLOCK), lambda cb: (0, cb)),
        state=pl.BlockSpec((T, STATE_BLOCK), lambda cb: (0, cb)),
        b=pl.BlockSpec((CHAN_BLOCK, STATE_BLOCK), lambda cb: (cb, cb)),
        c=pl.BlockSpec((STATE_BLOCK, CHAN_BLOCK), lambda cb: (cb, cb)),
        lam=pl.BlockSpec((1, STATE_BLOCK), lambda cb: (0, cb)),
    )


def _complex_power(re, im, n):
    out = None
    while True:
        if n & 1:
            out = (re, im) if out is None else (out[0] * re - out[1] * im, out[0] * im + out[1] * re)
        n >>= 1
        if n == 0:
            return out
        re, im = re * re - im * im, 2.0 * re * im


def _rows8(i):
    if isinstance(i, int):
        return pl.ds(i * SUBLANES, SUBLANES)
    return pl.ds(pl.multiple_of(i * SUBLANES, SUBLANES), SUBLANES)


def _scan_loop(n_steps, body, init):
    trips = n_steps // SCAN_UNROLL

    def trip(t, carry):
        for u in range(SCAN_UNROLL):
            carry = body(t * SCAN_UNROLL + u, carry)
        return carry

    carry = lax.fori_loop(0, trips, trip, init)
    for step in range(trips * SCAN_UNROLL, n_steps):
        carry = body(step, carry)
    return carry


def _ssm_fwd(u_perm, b_re, b_im, c_re, c_im, lam_r, lam_i, ride):
    T = u_perm.shape[0]
    ls = T // SUBLANES
    rc = min(1024, T)
    sp = _scan_specs(T)

    def body(u_ref, bre_ref, bim_ref, cre_ref, cim_ref, lr_ref, li_ref, so_re_ref, so_im_ref, y_ref, sre_ref, sim_ref):
        for c in range(T // rc):
            rows = pl.ds(c * rc, rc)
            ub = u_ref[rows, :].astype(BF16)
            sre_ref[rows, :] = _dot(ub, bre_ref[...])
            sim_ref[rows, :] = _dot(ub, bim_ref[...])
        shp = (SUBLANES, STATE_BLOCK)
        lr = jnp.broadcast_to(lr_ref[...], shp)
        li = jnp.broadcast_to(li_ref[...], shp)
        zero = jnp.zeros(shp, F32)

        def step(i, carry):
            sr, si = carry
            rows = _rows8(i)
            nr = lr * sr - li * si + sre_ref[rows, :]
            ni = lr * si + li * sr + sim_ref[rows, :]
            sre_ref[rows, :] = nr
            sim_ref[rows, :] = ni
            return nr, ni

        fr, fi = _scan_loop(ls, step, (zero, zero))
        pr, pi_ = _complex_power(lr, li, ls)
        row = lax.broadcasted_iota(jnp.int32, shp, 0)
        ir, ii = zero, zero
        for _ in range(SUBLANES - 1):
            er = fr + pr * ir - pi_ * ii
            ei = fi + pr * ii + pi_ * ir
            ir = jnp.where(row == 0, 0.0, pltpu.roll(er, 1, 0))
            ii = jnp.where(row == 0, 0.0, pltpu.roll(ei, 1, 0))

        def fix(i, carry):
            cr, ci = carry
            rows = _rows8(i)
            nr = lr * cr - li * ci
            ni = lr * ci + li * cr
            sre_ref[rows, :] += nr
            sim_ref[rows, :] += ni
            return nr, ni

        _scan_loop(ls, fix, (ir, ii))
        for c in range(T // rc):
            rows = pl.ds(c * rc, rc)
            s_r, s_i = sre_ref[rows, :].astype(BF16), sim_ref[rows, :].astype(BF16)
            so_re_ref[rows, :] = s_r
            so_im_ref[rows, :] = s_i
            y_ref[rows, :] = _dot(s_r, cre_ref[...]) - _dot(s_i, cim_ref[...])

    return _call(body, name='ssm_fwd', grid=(N_STATE // STATE_BLOCK,),
                 in_specs=[sp['chan'], sp['b'], sp['b'], sp['c'], sp['c'], sp['lam'], sp['lam']],
                 out_specs=[sp['state'], sp['state'], sp['chan']],
                 out_shape=[_sds((T, N_STATE), BF16), _sds((T, N_STATE), BF16), _sds((T, D_SSM))],
                 scratch=[pltpu.VMEM((T, STATE_BLOCK), F32), pltpu.VMEM((T, STATE_BLOCK), F32)],
                 sem=('arbitrary',), vmem=VMEM_MOST, ride=ride)(u_perm, b_re, b_im, c_re, c_im, lam_r, lam_i)


def _ssm_bwd(dy_perm, u_perm, s_re, s_im, b_re, b_im, c_re, c_im, lam_r, lam_i, ride):
    T = u_perm.shape[0]
    ls = T // SUBLANES
    rc = min(1024, T)
    sp = _scan_specs(T)
    ncb = N_STATE // STATE_BLOCK

    def body(dy_ref, u_ref, sre_ref, sim_ref, bre_ref, bim_ref, cre_ref, cim_ref, lr_ref, li_ref,
             du_ref, dbr_ref, dbi_ref, dcr_ref, dci_ref, dar_ref, dai_ref, gre_ref, gim_ref):
        shp = (SUBLANES, STATE_BLOCK)
        zero = jnp.zeros(shp, F32)
        tail = pl.ds(T, SUBLANES)
        gre_ref[tail, :] = zero
        gim_ref[tail, :] = zero
        for c in range(T // rc):
            rows = pl.ds(c * rc, rc)
            dyb = dy_ref[rows, :].astype(BF16)
            gre_ref[rows, :] = _dot_nt(dyb, cre_ref[...])
            gim_ref[rows, :] = -_dot_nt(dyb, cim_ref[...])
        lr = jnp.broadcast_to(lr_ref[...], shp)
        li = jnp.broadcast_to(li_ref[...], shp)

        def step(k, carry):
            gr, gi = carry
            rows = _rows8(ls - 1 - k)
            nr = lr * gr + li * gi + gre_ref[rows, :]
            ni = lr * gi - li * gr + gim_ref[rows, :]
            gre_ref[rows, :] = nr
            gim_ref[rows, :] = ni
            return nr, ni

        fr, fi = _scan_loop(ls, step, (zero, zero))
        pr, pi_ = _complex_power(lr, -li, ls)
        row = lax.broadcasted_iota(jnp.int32, shp, 0)
        cr, ci = zero, zero
        for _ in range(SUBLANES - 1):
            er = fr + pr * cr - pi_ * ci
            ei = fi + pr * ci + pi_ * cr
            cr = jnp.where(row == SUBLANES - 1, 0.0, pltpu.roll(er, SUBLANES - 1, 0))
            ci = jnp.where(row == SUBLANES - 1, 0.0, pltpu.roll(ei, SUBLANES - 1, 0))

        def fix(k, carry):
            dr, di = carry
            rows = _rows8(ls - 1 - k)
            dr, di = lr * dr + li * di, lr * di - li * dr
            gre_ref[rows, :] += dr
            gim_ref[rows, :] += di
            return dr, di

        _scan_loop(ls, fix, (cr, ci))

        acc_r = jnp.zeros((1, STATE_BLOCK), F32)
        acc_i = jnp.zeros((1, STATE_BLOCK), F32)
        for c in range(T // rc):
            rows, nxt = pl.ds(c * rc, rc), pl.ds(c * rc + SUBLANES, rc)
            s_r, s_i = sre_ref[rows, :].astype(F32), sim_ref[rows, :].astype(F32)
            g_r, g_i = gre_ref[nxt, :], gim_ref[nxt, :]
            acc_r = acc_r + _colsum(g_r * s_r + g_i * s_i)
            acc_i = acc_i + _colsum(g_i * s_r - g_r * s_i)
        last = pl.ds(T - 2 * SUBLANES, 2 * SUBLANES)
        first = pl.ds(0, SUBLANES)
        spr = jnp.where(row == 0, 0.0, pltpu.roll(sre_ref[last, :].astype(F32)[SUBLANES:], 1, 0))
        spi = jnp.where(row == 0, 0.0, pltpu.roll(sim_ref[last, :].astype(F32)[SUBLANES:], 1, 0))
        gr, gi = gre_ref[first, :], gim_ref[first, :]
        dar_ref[...] = acc_r + _colsum(gr * spr + gi * spi)
        dai_ref[...] = acc_i + _colsum(gi * spr - gr * spi)

        for c in range(T // rc):
            rows = pl.ds(c * rc, rc)
            g_r, g_i = gre_ref[rows, :].astype(BF16), gim_ref[rows, :].astype(BF16)
            s_r, s_i = sre_ref[rows, :], sim_ref[rows, :]
            ub, dyb = u_ref[rows, :].astype(BF16), dy_ref[rows, :].astype(BF16)
            du_ref[rows, :] = _dot_nt(g_r, bre_ref[...]) + _dot_nt(g_i, bim_ref[...])
            parts = (_dot_tn(ub, g_r), _dot_tn(ub, g_i), _dot_tn(s_r, dyb), -_dot_tn(s_i, dyb))
            outs = (dbr_ref, dbi_ref, dcr_ref, dci_ref)
            for o_ref, part in zip(outs, parts):
                if c == 0:
                    o_ref[...] = part
                else:
                    o_ref[...] += part

    blk = lambda r, c: pl.BlockSpec((None, r, c), lambda cb: (cb, 0, 0))
    return _call(body, name='ssm_bwd', grid=(ncb,),
                 in_specs=[sp['chan'], sp['chan'], sp['state'], sp['state'], sp['b'], sp['b'], sp['c'], sp['c'],
                           sp['lam'], sp['lam']],
                 out_specs=[sp['chan'], blk(CHAN_BLOCK, STATE_BLOCK), blk(CHAN_BLOCK, STATE_BLOCK),
                            blk(STATE_BLOCK, CHAN_BLOCK), blk(STATE_BLOCK, CHAN_BLOCK), blk(1, STATE_BLOCK),
                            blk(1, STATE_BLOCK)],
                 out_shape=[_sds((T, D_SSM)), _sds((ncb, CHAN_BLOCK, STATE_BLOCK)), _sds((ncb, CHAN_BLOCK, STATE_BLOCK)),
                            _sds((ncb, STATE_BLOCK, CHAN_BLOCK)), _sds((ncb, STATE_BLOCK, CHAN_BLOCK)),
                            _sds((ncb, 1, STATE_BLOCK)), _sds((ncb, 1, STATE_BLOCK))],
                 scratch=[pltpu.VMEM((T + SUBLANES, STATE_BLOCK), F32), pltpu.VMEM((T + SUBLANES, STATE_BLOCK), F32)],
                 sem=('arbitrary',), vmem=VMEM_MOST, ride=ride)(dy_perm, u_perm, s_re, s_im, b_re, b_im, c_re, c_im,
                                                                lam_r, lam_i)


def _ffn_dact(ddn, wd4, hid4, tm):
    T = ddn.shape[0]
    nb = T // tm

    def body(d_ref, w_ref, hid_ref, o_ref, gw_ref, acc_ref):
        i = pl.program_id(1)
        d = d_ref[...]
        dact = _dot_nt(d, w_ref[...])
        silu, dsilu = _silu_parts(hid_ref[0].astype(F32))
        hid_v = hid_ref[1].astype(F32)
        o_ref[0] = (dact * hid_v * dsilu).astype(BF16)
        o_ref[1] = (dact * silu).astype(BF16)
        part = _dot_tn((silu * hid_v).astype(BF16), d)

        @pl.when(i == 0)
        def _():
            acc_ref[...] = part

        @pl.when(i > 0)
        def _():
            acc_ref[...] += part

        @pl.when(i == nb - 1)
        def _():
            gw_ref[...] = acc_ref[...].astype(BF16)

    blk = pl.BlockSpec((2, None, tm, FF_SHARD), lambda j, i: (0, j, i, 0))
    w_blk = pl.BlockSpec((None, FF_SHARD, D_MODEL), lambda j, i: (j, 0, 0))
    return _call(body, name='ffn_dact', grid=(4, nb),
                 in_specs=[pl.BlockSpec((tm, D_MODEL), lambda j, i: (i, 0)), w_blk, blk],
                 out_specs=[blk, w_blk],
                 out_shape=[_sds((2, 4, T, FF_SHARD), BF16), _sds((4, FF_SHARD, D_MODEL), BF16)],
                 scratch=[pltpu.VMEM((FF_SHARD, D_MODEL), F32)], sem=('parallel', 'arbitrary'),
                 vmem=VMEM_BIG)(ddn, wd4, hid4)


def _ffn_dup(dhid8, up8, cw8, tm, ride):
    T = up8.shape[1]
    nb = T // tm
    ha = _halo_after(tm, T, HALO16)

    def body(dh_ref, dha_ref, up_ref, cw_ref, dup_ref, dcw_ref):
        i = pl.program_id(1)

        @pl.when(i == 0)
        def _():
            dcw_ref[...] = jnp.zeros_like(dcw_ref)

        dh = dh_ref[...].astype(F32)
        dup, dh1, dh2 = _conv3_t(dh, jnp.where(i < nb - 1, dha_ref[...].astype(F32), 0.0), cw_ref)
        dup_ref[...] = dup.astype(BF16)
        up = up_ref[...].astype(F32)
        dcw_ref[0:1, :] += _colsum(dh2 * up)
        dcw_ref[1:2, :] += _colsum(dh1 * up)
        dcw_ref[2:3, :] += _colsum(dh * up)

    main = pl.BlockSpec((None, tm, FF_SHARD), lambda j, i: (j, i, 0))
    return _call(body, name='ffn_dup', grid=(N_DEV, nb),
                 in_specs=[main, pl.BlockSpec((None, HALO16, FF_SHARD), lambda j, i: (j, ha(i), 0)), main,
                           pl.BlockSpec((None, 3, FF_SHARD), lambda j, i: (j, 0, 0))],
                 out_specs=[main, pl.BlockSpec((None, 8, FF_SHARD), lambda j, i: (j, 0, 0))],
                 out_shape=[_sds((N_DEV, T, FF_SHARD), BF16), _sds((N_DEV, 8, FF_SHARD))],
                 sem=('parallel', 'arbitrary'), vmem=VMEM_BIG, ride=ride)(dhid8, dhid8, up8, cw8)


def _grad_tn(a, b, a_spec, b_spec, groups, m, n, tk, name, ride=None, parts=1):
    T = a.shape[-2]
    nk = T // tk
    mp = m // parts

    def body(a_ref, b_ref, *refs):
        o_refs, acc_ref = refs[:parts], refs[parts]
        k = pl.program_id(1)
        part = _dot_tn(a_ref[...], b_ref[...])

        @pl.when(k == 0)
        def _():
            acc_ref[...] = part

        @pl.when(k > 0)
        def _():
            acc_ref[...] += part

        @pl.when(k == nk - 1)
        def _():
            for p, o_ref in enumerate(o_refs):
                o_ref[...] = acc_ref[p * mp:(p + 1) * mp, :].astype(BF16)

    out_spec = pl.BlockSpec((None, mp, n), lambda g, k: (g, 0, 0))
    res = _call(body, name=name, grid=(groups, nk), in_specs=[a_spec, b_spec], out_specs=[out_spec] * parts,
                out_shape=[_sds((groups, mp, n), BF16)] * parts, scratch=[pltpu.VMEM((m, n), F32)],
                sem=('parallel', 'arbitrary'), vmem=VMEM_BIG, ride=ride)(a, b)
    if parts > 1:
        return res
    return res[0] if ride is None else (res[0][0], res[1])


def _grad_w_in(h1, dproj, tk, ride):
    T = h1.shape[0]
    nk = T // tk
    half = D_IN_PROJ // 2

    def body(a_ref, b_ref, o_ref, acc_ref):
        k = pl.program_id(0)
        for h in range(2):
            cols = slice(h * half, (h + 1) * half)
            part = _dot_tn(a_ref[...], b_ref[:, cols])

            @pl.when(k == 0)
            def _():
                acc_ref[:, cols] = part

            @pl.when(k > 0)
            def _():
                acc_ref[:, cols] += part

        @pl.when(k == nk - 1)
        def _():
            for g in range(N_DEV):
                o_ref[g] = acc_ref[:, g * IN_SHARD:(g + 1) * IN_SHARD].astype(BF16)

    return _call(body, name='grad_w_in', grid=(nk,),
                 in_specs=[pl.BlockSpec((tk, D_MODEL), lambda k: (k, 0)), pl.BlockSpec((tk, D_IN_PROJ), lambda k: (k, 0))],
                 out_specs=_const((N_DEV, D_MODEL, IN_SHARD)), out_shape=_sds((N_DEV, D_MODEL, IN_SHARD), BF16),
                 scratch=[pltpu.VMEM((D_MODEL, D_IN_PROJ), F32)], sem=('arbitrary',), vmem=VMEM_BIG, ride=ride)(h1, dproj)


def _pre_norm_bwd(dz, dz_spec, w_parts, xin, dres, sc, g, tm, name, ride, below=None, group=1, w_t=False):
    T = xin.shape[0]
    n = w_parts[0].shape[1] if w_t else w_parts[0].shape[2]
    mul = _dot if w_t else _dot_nt
    steps = N_DEV // group
    width = D_MODEL // len(w_parts)

    def body(dz_ref, *refs):
        w_refs, (x_ref, dr_ref, sc_ref, g_ref), refs = refs[:len(w_parts)], refs[len(w_parts):len(w_parts) + 4], \
            refs[len(w_parts) + 4:]
        if below is None:
            dx_ref, dsh_ref, dsc_ref, dg_ref = refs
            sums = (dsh_ref, dsc_ref, dg_ref)
        else:
            v_ref, gate_ref, g2_ref, dx_ref, dsh_ref, dsc_ref, dg_ref, dv_ref, dgate_ref, dg2_ref = refs
            sums = (dsh_ref, dsc_ref, dg_ref, dgate_ref, dg2_ref)
        i, j = pl.program_id(0), pl.program_id(1)
        piece = (lambda s: dz_ref[s]) if dz.ndim == 3 else (lambda s: dz_ref[:, s * n:(s + 1) * n])
        parts = []
        for w_ref in w_refs:
            part = mul(piece(0), w_ref[0])
            for s in range(1, group):
                part = part + mul(piece(s), w_ref[s])
            parts.append(part)

        @pl.when(jnp.logical_and(i == 0, j == 0))
        def _():
            for s_ref in sums:
                s_ref[...] = jnp.zeros_like(s_ref)

        @pl.when(j == 0)
        def _():
            for k, part in enumerate(parts):
                dx_ref[:, k * width:(k + 1) * width] = part

        @pl.when(j > 0)
        def _():
            for k, part in enumerate(parts):
                dx_ref[:, k * width:(k + 1) * width] += part

        @pl.when(j == steps - 1)
        def _():
            dh, xv, gv = dx_ref[...], x_ref[...], g_ref[...]
            r = _rsqrt_mean(xv)
            dsh_ref[...] += _colsum(dh)
            dsc_ref[...] += _colsum(dh * (xv * r * gv))
            dxn = dh * (1.0 + sc_ref[...])
            dg_ref[...] += _colsum(dxn * xv * r)
            dx = dr_ref[...] + _norm_bwd(dxn, xv, r, gv)
            dx_ref[...] = dx
            if below is not None:
                v, g2 = v_ref[...].astype(F32), g2_ref[...]
                rv = _rsqrt_mean(v)
                dgate_ref[...] += _colsum(dx * (v * rv * g2))
                dn = dx * gate_ref[...]
                dg2_ref[...] += _colsum(dn * v * rv)
                dv_ref[...] = _norm_bwd(dn, v, rv, g2).astype(BF16)

    row = pl.BlockSpec((tm, D_MODEL), lambda i, j: (i, 0))
    vec = _const((1, D_MODEL))
    in_specs = [dz_spec] + [pl.BlockSpec((group,) + w.shape[1:], lambda i, j: (j, 0, 0)) for w in w_parts]
    in_specs += [row, row, vec, vec]
    out_specs = [row, vec, vec, vec]
    out_shape = [_sds((T, D_MODEL)), _sds((1, D_MODEL)), _sds((1, D_MODEL)), _sds((1, D_MODEL))]
    args = [dz, *w_parts, xin, dres, sc, g]
    if below is not None:
        in_specs += [row, vec, vec]
        out_specs += [row, vec, vec]
        out_shape += [_sds((T, D_MODEL), BF16), _sds((1, D_MODEL)), _sds((1, D_MODEL))]
        args += list(below)
    return _call(body, name=name, grid=(T // tm, steps), in_specs=in_specs, out_specs=out_specs,
                 out_shape=out_shape, sem=('arbitrary', 'arbitrary'), vmem=VMEM_MOST, ride=ride)(*args)


def _mix_bwd(d_o, w_out, yssm, proj, d, glu_w, glu_b, g_ssm, cw, g_conv, avg16, avg64, tm, ride):
    T = yssm.shape[0]
    hb = _halo_before(tm)

    def body(do_ref, wo_ref, y_ref, p_ref, ph_ref, d_ref, gw_ref, gb_ref, gs_ref, cw_ref, gc_ref, a16_ref, a64_ref,
             dy_ref, dconv_ref, dbg_ref, z_ref, dlin_ref, acc_ref):
        i = pl.program_id(0)
        dyc = _dot_nt(do_ref[...], wo_ref[...])

        @pl.when(i == 0)
        def _():
            acc_ref[...] = jnp.zeros_like(acc_ref)

        u = p_ref[:, 0:D_SSM]
        y = y_ref[...] + d_ref[...] * u
        z, t = _gelu(y)
        gate = _sigmoid(_dot(z.astype(BF16), gw_ref[...]) + gb_ref[...])
        ya = z * gate
        rs = lax.rsqrt(_dot_split(ya * ya, a16_ref[...], 2) + EPS)
        dna = dyc[:, 0:D_SSM]
        acc_ref[1:2, :] += _colsum(dna * ya * rs)
        dya = _head_norm_bwd(dna, ya, rs, gs_ref[...], a16_ref[...])
        dlin = dya * z * gate * (1.0 - gate)
        acc_ref[0:1, :] += _colsum(dlin)
        dlin_b = dlin.astype(BF16)
        dz = dya * gate + _dot_nt(dlin_b, gw_ref[...])
        dy = dz * _gelu_grad(y, t)
        acc_ref[3:4, :] += _colsum(dy * u)
        dy_ref[...] = dy
        z_ref[...] = z.astype(BF16)
        dlin_ref[...] = dlin_b

        bg = p_ref[:, D_SSM:D_SSM + D_CONV]
        cv = p_ref[:, D_SSM + D_CONV:D_SSM + 2 * D_CONV] * p_ref[:, D_SSM + 2 * D_CONV:D_IN_PROJ]
        hv = ph_ref[:, D_SSM + D_CONV:D_SSM + 2 * D_CONV] * ph_ref[:, D_SSM + 2 * D_CONV:D_IN_PROJ]
        hv = jnp.where(i > 0, hv, 0.0)
        conv, cv1, cv2 = _conv3(cv, hv, cw_ref)
        yb = bg * conv
        rsb = lax.rsqrt(_dot_split(yb * yb, a64_ref[...], 2) + EPS)
        dnb = dyc[:, D_SSM:D_MODEL]
        acc_ref[2:3, :] += _colsum(dnb * yb * rsb)
        dyb = _head_norm_bwd(dnb, yb, rsb, gc_ref[...], a64_ref[...])
        dbg_ref[...] = dyb * conv
        dconv = dyb * bg
        dconv_ref[...] = dconv
        acc_ref[4:5, :] += _colsum(dconv * cv2)
        acc_ref[5:6, :] += _colsum(dconv * cv1)
        acc_ref[6:7, :] += _colsum(dconv * cv)

    vec = _const((1, D_SSM))
    sq = _const((D_SSM, D_SSM))
    half = pl.BlockSpec((tm, D_SSM), lambda i: (i, 0))
    return _call(body, name='mix_bwd', grid=(T // tm,),
                 in_specs=[pl.BlockSpec((tm, D_MODEL), lambda i: (i, 0)), _const((D_MODEL, D_MODEL)), half,
                           pl.BlockSpec((tm, D_IN_PROJ), lambda i: (i, 0)),
                           pl.BlockSpec((HALO, D_IN_PROJ), lambda i: (hb(i), 0)), vec, sq, vec, vec,
                           _const((3, D_CONV)), vec, sq, sq],
                 out_specs=[half, half, half, half, half, _const((8, D_SSM))],
                 out_shape=[_sds((T, D_SSM)), _sds((T, D_SSM)), _sds((T, D_SSM)), _sds((T, D_SSM), BF16),
                            _sds((T, D_SSM), BF16), _sds((8, D_SSM))],
                 sem=('arbitrary',), vmem=VMEM_BIG, ride=ride)(d_o, w_out, yssm, proj, proj, d, glu_w, glu_b, g_ssm, cw,
                                                              g_conv, avg16, avg64)


def _mix_bwd_proj(dconv, proj, du_ssm, dy, d, dbg, cw, tm):
    T = dy.shape[0]
    nb = T // tm
    ha = _halo_after(tm, T)

    def body(dc_ref, dch_ref, cg_ref, v_ref, du_ref, dy_ref, d_ref, dbg_ref, cw_ref, o_ref):
        i = pl.program_id(0)
        dcv = _conv3_t(dc_ref[...], jnp.where(i < nb - 1, dch_ref[...], 0.0), cw_ref)[0]
        o_ref[:, 0:D_SSM] = (du_ref[...] + dy_ref[...] * d_ref[...]).astype(BF16)
        o_ref[:, D_SSM:D_SSM + D_CONV] = dbg_ref[...].astype(BF16)
        o_ref[:, D_SSM + D_CONV:D_SSM + 2 * D_CONV] = (dcv * v_ref[...]).astype(BF16)
        o_ref[:, D_SSM + 2 * D_CONV:D_IN_PROJ] = (dcv * cg_ref[...]).astype(BF16)

    half = pl.BlockSpec((tm, D_SSM), lambda i: (i, 0))
    return _call(body, name='mix_bwd_proj', grid=(nb,),
                 in_specs=[half, pl.BlockSpec((HALO, D_CONV), lambda i: (ha(i), 0)),
                           pl.BlockSpec((tm, D_CONV), lambda i: (i, 2)), pl.BlockSpec((tm, D_CONV), lambda i: (i, 3)),
                           half, half, _const((1, D_SSM)), half, _const((3, D_CONV))],
                 out_specs=pl.BlockSpec((tm, D_IN_PROJ), lambda i: (i, 0)), out_shape=_sds((T, D_IN_PROJ), BF16),
                 sem=('parallel',), vmem=VMEM_BIG)(dconv, dconv, proj, proj, du_ssm, dy, d, dbg, cw)


ADAMW_SLOT_BYTES = 8 << 20
ADAMW_ROW_BYTES = 3 << 19


def _row_tile(rows, cols, slots):
    for cand in range(rows, 15, -1):
        if (rows % cand == 0 and cand % 16 == 0 and slots * cand * cols * 4 <= ADAMW_SLOT_BYTES
                and cand * cols * 4 <= ADAMW_ROW_BYTES):
            return cand
    return rows


def _adamw_math(g, w, m, v):
    m2 = ADAM_B1 * m + (1.0 - ADAM_B1) * g
    v2 = ADAM_B2 * v + (1.0 - ADAM_B2) * (g * g)
    m_hat = m2 / (1.0 - ADAM_B1 ** ADAM_STEP)
    v_hat = v2 / (1.0 - ADAM_B2 ** ADAM_STEP)
    return -ADAM_LR * (m_hat / (jnp.sqrt(v_hat) + ADAM_EPS) + ADAM_WD * w), m2, v2


def _adamw(pieces, w, m, v, name):
    slots, _, cols = pieces[0].shape
    rows = sum(p.shape[1] for p in pieces)
    tr = _row_tile(pieces[0].shape[1], cols, slots)
    starts, pos = [], 0
    for p in pieces:
        assert p.shape[1] % tr == 0
        starts.append(pos)
        pos += p.shape[1] // tr

    def body(*refs):
        g_refs = refs[:len(pieces)]
        w_ref, m_ref, v_ref, go_ref, d_ref, mo_ref, vo_ref = refs[len(pieces):]
        i = pl.program_id(0)
        g = None
        for g_ref, start in zip(g_refs, starts):
            part = g_ref[0].astype(F32)
            for s in range(1, slots):
                part = part + g_ref[s].astype(F32)
            g = part if g is None else jnp.where(i >= start, part, g)
        go_ref[...] = g
        d_ref[...], mo_ref[...], vo_ref[...] = _adamw_math(g, w_ref[...], m_ref[...], v_ref[...])

    def piece_spec(start, count):
        return pl.BlockSpec((slots, tr, cols), lambda i: (0, jnp.clip(i - start, 0, count - 1), 0))

    blk = pl.BlockSpec((tr, cols), lambda i: (i, 0))
    return _call(body, name=name, grid=(rows // tr,),
                 in_specs=[piece_spec(s, p.shape[1] // tr) for s, p in zip(starts, pieces)] + [blk, blk, blk],
                 out_specs=[blk] * 4, out_shape=[_sds((rows, cols))] * 4, sem=('parallel',),
                 vmem=VMEM_BIG)(*pieces, w, m, v)


def _to_scan_rows(a):
    T, n = a.shape
    return a.reshape(SUBLANES, T // SUBLANES, n).transpose(1, 0, 2).reshape(T, n)


def _from_scan_rows(a):
    T, n = a.shape
    return a.reshape(T // SUBLANES, SUBLANES, n).transpose(1, 0, 2).reshape(T, n)


def _expand(a):
    return jnp.repeat(a, SSM_GROUP, axis=1)


def _block_diag(rows, row_group, col_group):
    r, n = rows.shape
    tiled = jnp.tile(rows, (1, N_GROUPS))
    keep = (jnp.arange(r)[:, None] // row_group) == (jnp.arange(n * N_GROUPS)[None, :] // col_group)
    return jnp.where(keep, tiled, 0.0)


def _block_diag_b(bb):
    return _block_diag(bb.transpose(0, 2, 1).reshape(D_SSM, SSM_STATE), SSM_GROUP, SSM_STATE)


def _block_diag_c(cc):
    return _block_diag(cc.transpose(0, 2, 1).reshape(N_STATE, SSM_GROUP), SSM_STATE, SSM_GROUP)


def _diag_blocks(x, chan_major):
    per = CHAN_BLOCK // SSM_GROUP
    eye = jnp.eye(per, dtype=x.dtype)
    if chan_major:
        x = x.reshape(-1, per, SSM_GROUP, per, SSM_STATE) * eye[None, :, None, :, None]
        return x.sum(axis=1).transpose(0, 2, 3, 1).reshape(N_GROUPS, SSM_STATE, SSM_GROUP)
    x = x.reshape(-1, per, SSM_STATE, per, SSM_GROUP) * eye[None, :, None, :, None]
    return x.sum(axis=3).reshape(N_GROUPS, SSM_STATE, SSM_GROUP)


SMALL_LAYOUT = {
    'ssm_b_re': (0, 0, 32, 1024), 'ssm_b_im': (32, 0, 32, 1024), 'ssm_c_re': (64, 0, 32, 1024),
    'ssm_c_im': (96, 0, 32, 1024), 'b_ada': (128, 0, 6, 1024), 'g_pre_mix': (134, 0, 1, 1024),
    'g_post_mix': (135, 0, 1, 1024), 'ssm_lam_re': (136, 0, 2, 1024), 'ssm_lam_im': (138, 0, 2, 1024),
    'ssm_log_step': (140, 0, 1, 32), 'glu_b': (141, 0, 1, 512), 'g_out_ssm': (141, 512, 1, 512),
    'g_out_conv': (142, 0, 1, 512), 'ssm_d': (142, 512, 1, 512), 'g_pre_ffn': (143, 0, 1, 1024),
    'g_post_ffn': (144, 0, 1, 1024)}
SMALL_ROWS = 152
B_ADA_ROW = SMALL_LAYOUT['b_ada'][0]
LATE_ROWS = {('b_ada', 0): 0, ('b_ada', 1): 1, ('g_pre_mix', 0): 2}


def _adamw_small(gathered, late, wts, mom_m, mom_v):
    names = list(SMALL_LAYOUT)
    n = len(names)

    def body(*refs):
        g_ref, late_ref, ins, outs = refs[0], refs[1], refs[2:2 + 3 * n], refs[2 + 3 * n:]
        for p, name in enumerate(names):
            r0, c0, rows, cols = SMALL_LAYOUT[name]
            pieces = [(0, rows)] if rows % 8 == 0 else [(r, 1) for r in range(rows)]
            for r, cnt in pieces:
                src_ref, first = (late_ref, LATE_ROWS[name, r]) if (name, r) in LATE_ROWS else (g_ref, r0 + r)
                g = src_ref[0, first:first + cnt, c0:c0 + cols]
                for s in range(1, N_DEV):
                    g = g + src_ref[s, first:first + cnt, c0:c0 + cols]
                w, m, v = (ins[3 * p + q][r:r + cnt, :] for q in range(3))
                res = (g,) + _adamw_math(g, w, m, v)
                for q in range(4):
                    outs[4 * p + q][r:r + cnt, :] = res[q]

    shapes = [SMALL_LAYOUT[name][2:] for name in names]
    args = [gathered, late]
    for name, shp in zip(names, shapes):
        args += [wts[name].reshape(shp), mom_m[name].reshape(shp), mom_v[name].reshape(shp)]
    outs = _call(body, name='adamw_small', grid=(1,),
                 in_specs=[_const(gathered.shape), _const(late.shape)]
                 + [_const(shp) for shp in shapes for _ in range(3)],
                 out_specs=[_const(shp) for shp in shapes for _ in range(4)],
                 out_shape=[_sds(shp) for shp in shapes for _ in range(4)], vmem=VMEM_BIG)(*args)
    res = {}
    for p, name in enumerate(names):
        for q, kind in enumerate(('g', 'd', 'm', 'v')):
            res[kind, name] = outs[4 * p + q].reshape(wts[name].shape)
    return res


def kernel(x, c, w_ada, b_ada, g_pre_mix, g_post_mix, w_in, ssm_lam_re, ssm_lam_im, ssm_log_step, ssm_b_re, ssm_b_im, ssm_c_re, ssm_c_im, ssm_d, glu_w, glu_b, g_out_ssm, conv_w, g_out_conv, w_out, g_pre_ffn, g_post_ffn, w_up, ffn_conv_w, w_down, loss_target, m_w_ada, m_b_ada, m_g_pre_mix, m_g_post_mix, m_w_in, m_ssm_lam_re, m_ssm_lam_im, m_ssm_log_step, m_ssm_b_re, m_ssm_b_im, m_ssm_c_re, m_ssm_c_im, m_ssm_d, m_glu_w, m_glu_b, m_g_out_ssm, m_conv_w, m_g_out_conv, m_w_out, m_g_pre_ffn, m_g_post_ffn, m_w_up, m_ffn_conv_w, m_w_down, v_w_ada, v_b_ada, v_g_pre_mix, v_g_post_mix, v_w_in, v_ssm_lam_re, v_ssm_lam_im, v_ssm_log_step, v_ssm_b_re, v_ssm_b_im, v_ssm_c_re, v_ssm_c_im, v_ssm_d, v_glu_w, v_glu_b, v_g_out_ssm, v_conv_w, v_g_out_conv, v_w_out, v_g_pre_ffn, v_g_post_ffn, v_w_up, v_ffn_conv_w, v_w_down):
    args = dict(locals())
    wts = {n: args[n] for n in WEIGHTS}
    mom_m = {n: args['m_' + n] for n in WEIGHTS}
    mom_v = {n: args['v_' + n] for n in WEIGHTS}
    T = x.shape[1]
    tm = min(512, T)
    tw = min(1024, T)
    tk = min(2048, T)
    me = _me()[3]
    xt, tgt = x[0], loss_target[0]

    c_all, w_in_s = _exchange([c, w_in[0].astype(BF16)], name='gather_first', scatter=False)
    c_all = c_all.reshape(N_DEV, D_MODEL)
    b_cols = lax.dynamic_slice(b_ada, (0, me * ADA_SHARD), (1, ADA_SHARD))
    mod_cols, c_act = _mod_cols(c_all, w_ada[0], b_cols)
    (mod_all,) = _exchange([mod_cols], name='gather_mod', scatter=False)
    mod = lax.dynamic_slice(mod_all, (0, me, 0), (N_DEV, 1, ADA_SHARD)).reshape(N_MOD, 1, D_MODEL)
    sh1, sc1, gt1, sh2, sc2, gt2 = [mod[k] for k in range(N_MOD)]


    lre_x, lim_x = _expand(ssm_lam_re[0]), _expand(ssm_lam_im[0])
    lst_x = jnp.broadcast_to(ssm_log_step[0][:, None], (N_GROUPS, SSM_STATE * SSM_GROUP))
    b_re_x = ssm_b_re[0].reshape(N_GROUPS, -1)
    b_im_x = ssm_b_im[0].reshape(N_GROUPS, -1)
    ar_x, ai_x, bbr_x, bbi_x = _ssm_prep(lre_x, lim_x, lst_x, b_re_x, b_im_x)
    lam_r = ar_x[:, ::SSM_GROUP].reshape(1, N_STATE)
    lam_i = ai_x[:, ::SSM_GROUP].reshape(1, N_STATE)
    big_b_re = _block_diag_b(bbr_x.reshape(N_GROUPS, SSM_STATE, SSM_GROUP)).astype(BF16)
    big_b_im = _block_diag_b(bbi_x.reshape(N_GROUPS, SSM_STATE, SSM_GROUP)).astype(BF16)
    big_c_re = _block_diag_c(ssm_c_re[0]).astype(BF16)
    big_c_im = _block_diag_c(ssm_c_im[0]).astype(BF16)
    head = jnp.arange(D_SSM)
    avg16 = jnp.where(head[:, None] // SSM_GROUP == head[None, :] // SSM_GROUP, 1.0 / SSM_GROUP, 0.0).astype(BF16)
    hd = D_CONV // CONV_HEADS
    avg64 = jnp.where(head[:, None] // hd == head[None, :] // hd, 1.0 / hd, 0.0).astype(BF16)

    w_up_t, half = w_up[0].T, D_MODEL // 2
    (proj, h1), (ffn_conv_s, conv_s, w_up_a) = _pre_mix(
        xt, sc1, sh1, g_pre_mix, w_in_s, tw, ([ffn_conv_w[0], conv_w[0], w_up_t[:, :half].astype(BF16)], False))
    cw_full = conv_s.transpose(1, 0, 2).reshape(3, D_CONV)
    u_perm = _to_scan_rows(proj[:, :D_SSM])
    (s_re, s_im, y_perm), (w_up_b, glu_s, w_out_s) = _ssm_fwd(
        u_perm, big_b_re, big_b_im, big_c_re, big_c_im, lam_r, lam_i,
        ([w_up_t[:, half:].astype(BF16), glu_w[0].astype(BF16), w_out[0].astype(BF16)], False))
    glu_full = glu_s.reshape(D_SSM, D_SSM)
    w_out_full = w_out_s.reshape(D_MODEL, D_MODEL)
    yssm = _from_scan_rows(y_perm)
    mix_args = (ssm_d, glu_full, glu_b, g_out_ssm, cw_full, g_out_conv, avg16, avg64)
    ycat = _mix_fwd(yssm, proj, *mix_args, tw)
    o, x1, h2 = _out_proj(ycat, w_out_full, xt, gt1, g_post_mix, g_pre_ffn, sc2, sh2, tw)
    (up8, hid8), (w_down_s,) = _ffn_up(h2, w_up_a, w_up_b, ffn_conv_s, tw, ([w_down[0].astype(BF16)], False))
    wd4 = w_down_s.reshape(4, FF_SHARD, D_MODEL)
    hid4 = hid8.reshape(2, 4, T, FF_SHARD)
    ddn, dx2, loss_parts, d_gt2, d_g_post_ffn = _ffn_down(hid4, wd4, x1, tgt, gt2, g_post_ffn, tm)
    loss_local = jnp.sum(loss_parts[:, 0, 0])

    got = {}
    dhid, g_w_down = _ffn_dact(ddn, wd4, hid4, tw)
    (dup8, dcw_ffn), (got['w_down'],) = _ffn_dup(dhid.reshape(N_DEV, T, FF_SHARD), up8, ffn_conv_s, tw,
                                                 ([g_w_down.reshape(N_DEV, D_FF // N_DEV, D_MODEL)], True))
    g_w_up_halves = _grad_tn(dup8, h2, pl.BlockSpec((None, T, FF_SHARD), lambda g, k: (g, k, 0)),
                             pl.BlockSpec((T, D_MODEL), lambda g, k: (k, 0)), N_DEV, FF_SHARD, D_MODEL, T,
                             'grad_w_up', parts=2)
    (dx1, d_sh2, d_sc2, d_g_pre_ffn, d_o, d_gt1, d_g_post_mix), (got_up_0, got['ffn_conv_w']) = _pre_norm_bwd(
        dup8, pl.BlockSpec((N_DEV, tm, FF_SHARD), lambda i, j: (j, i, 0)), [w_up_a, w_up_b], x1, dx2, sc2, g_pre_ffn,
        tm, 'ffn_in_bwd', ([g_w_up_halves[0], dcw_ffn], True), below=(o, gt1, g_post_mix), group=N_DEV, w_t=True)

    g_w_out = _grad_tn(ycat, d_o, pl.BlockSpec((tk, D_MODEL), lambda g, k: (k, 0)),
                       pl.BlockSpec((tk, D_MODEL), lambda g, k: (k, 0)), 1, D_MODEL, D_MODEL, tk, 'grad_w_out')
    (dy, dconv, dbg, z_b, dlin_b, sums), (got['w_out'],) = _mix_bwd(
        d_o, w_out_full, yssm, proj, *mix_args, tm, ([g_w_out.reshape(N_DEV, D_MODEL // N_DEV, D_MODEL)], True))
    g_glu_w = _grad_tn(z_b, dlin_b, pl.BlockSpec((tk, D_SSM), lambda g, k: (k, 0)),
                       pl.BlockSpec((tk, D_SSM), lambda g, k: (k, 0)), 1, D_SSM, D_SSM, tk, 'grad_glu_w')
    dy_perm = _to_scan_rows(dy)
    (du_perm, dbr_blk, dbi_blk, dcr_blk, dci_blk, dar_blk, dai_blk), (got_up_1, got['glu_w']) = _ssm_bwd(
        dy_perm, u_perm, s_re, s_im, big_b_re, big_b_im, big_c_re, big_c_im, lam_r, lam_i,
        ([g_w_up_halves[1], g_glu_w.reshape(N_DEV, D_SSM // N_DEV, D_SSM)], True))
    du_ssm = _from_scan_rows(du_perm)
    dproj = _mix_bwd_proj(dconv, proj, du_ssm, dy, ssm_d, dbg, cw_full, tw)
    dbb_re = _diag_blocks(dbr_blk, True).reshape(N_GROUPS, -1)
    dbb_im = _diag_blocks(dbi_blk, True).reshape(N_GROUPS, -1)
    d_c_re = _diag_blocks(dcr_blk, False).transpose(0, 2, 1)
    d_c_im = _diag_blocks(dci_blk, False).transpose(0, 2, 1)
    lane = jnp.arange(SSM_STATE * SSM_GROUP)
    seg = jnp.where(lane[:, None] // SSM_GROUP == lane[None, :] // SSM_GROUP, 1.0, 0.0).astype(BF16)
    d_b_re_x, d_b_im_x, d_lre_x, d_lim_x, d_lst = _ssm_prep_bwd(
        lre_x, lim_x, lst_x, b_re_x, b_im_x, dbb_re, dbb_im, _expand(dar_blk.reshape(N_GROUPS, SSM_STATE)),
        _expand(dai_blk.reshape(N_GROUPS, SSM_STATE)), seg)

    row = lambda a: a.reshape(-1, PACK_COLS)
    blank = jnp.zeros((1, PACK_COLS), F32)
    small_pack = jnp.concatenate([
        d_b_re_x, d_b_im_x, row(d_c_re), row(d_c_im), blank, blank, d_gt1, d_sh2, d_sc2, d_gt2, blank,
        d_g_post_mix, row(d_lre_x[:, ::SSM_GROUP]), row(d_lim_x[:, ::SSM_GROUP]),
        jnp.pad(d_lst.reshape(1, N_GROUPS), ((0, 0), (0, PACK_COLS - N_GROUPS))), row(sums[0:4]), d_g_pre_ffn,
        d_g_post_ffn, jnp.zeros((SMALL_ROWS - 145, PACK_COLS), F32)])
    g_w_in, (small_all,) = _grad_w_in(h1, dproj, tk, ([small_pack], False))
    g_conv_slots = jnp.concatenate([sums[4:7], jnp.zeros((5, D_CONV), F32)]).reshape(
        8, N_DEV, D_CONV // N_DEV).transpose(1, 0, 2)
    (grad_x, d_sh1, d_sc1, d_g_pre_mix), (got['w_in'], got['conv_w']) = _pre_norm_bwd(
        dproj, pl.BlockSpec((tm, D_IN_PROJ), lambda i, j: (i, j)), [w_in_s], xt, dx1, sc1, g_pre_mix, tm,
        'mix_in_bwd', ([g_w_in, g_conv_slots], True), group=N_DEV)
    late_pack = jnp.concatenate([d_sh1, d_sc1, d_g_pre_mix, jnp.full((1, PACK_COLS), loss_local, F32),
                                 jnp.zeros((4, PACK_COLS), F32)])
    (late_all,) = _exchange([late_pack], name='gather_late_grads', scatter=False)
    loss = jnp.sum(late_all[:, 3, 0])
    res = _adamw_small(small_all, late_all, wts, mom_m, mom_v)

    dmod_all = jnp.concatenate([late_all[:, 0:2, :], small_all[:, B_ADA_ROW + 2:B_ADA_ROW + N_MOD, :]],
                               axis=1).reshape(N_DEV, N_MOD * D_MODEL)
    dmod_cols = lax.dynamic_slice(dmod_all, (0, me * ADA_SHARD), (N_DEV, ADA_SHARD))
    g_w_ada = _grad_w_ada(c_act.T, dmod_cols)

    pieces = {n: [slots[:, :3, :] if n in ('conv_w', 'ffn_conv_w') else slots] for n, slots in got.items()}
    for n, parts in pieces.items():
        outs = _adamw(parts, wts[n][0], mom_m[n][0], mom_v[n][0], 'adamw_' + n)
        for kind, val in zip(('g', 'd', 'm', 'v'), outs):
            res[kind, n] = val[None]
    outs = _adamw([got_up_0, got_up_1], w_up[0].T, m_w_up[0].T, v_w_up[0].T, 'adamw_w_up')
    for kind, val in zip(('g', 'd', 'm', 'v'), outs):
        res[kind, 'w_up'] = val.T[None]
    outs = _adamw([g_w_ada[None]], w_ada[0], m_w_ada[0], v_w_ada[0], 'adamw_w_ada')
    for kind, val in zip(('g', 'd', 'm', 'v'), outs):
        res[kind, 'w_ada'] = val[None]

    return (loss, grad_x[None], *[res['g', n] for n in WEIGHTS], *[res['d', n] for n in WEIGHTS],
            *[res['m', n] for n in WEIGHTS], *[res['v', n] for n in WEIGHTS])
```
